```python
import jax, jax.numpy as jnp
from jax import lax
import numpy as np

D_MODEL = 1024
BATCH = 8
SEQ = 4096
DEPTH = 1

PLE_DIM = 256
NORM_EPS = 1e-6
RW_HEADS = 8
RW_HEAD_DIM = 64
RW_WIDTH = RW_HEADS * RW_HEAD_DIM
DECAY_LORA = 64
AAA_LORA = 64
GATE_LORA = 160
RW_COLS = 3 * RW_WIDTH + DECAY_LORA + AAA_LORA + GATE_LORA
RW_LN_EPS = 64e-5
ATT_GROUPS = ((128, 1), (512, 4), (2048, 16))
ATT_HEADS_PER_GROUP = 4
ATT_HEADS = ATT_HEADS_PER_GROUP * len(ATT_GROUPS)
ATT_HEAD_DIM = 64
ATT_WIDTH = ATT_HEADS * ATT_HEAD_DIM
ATT_OUT = ATT_HEADS_PER_GROUP * ATT_HEAD_DIM
ATT_COLS = 3 * ATT_WIDTH
IN_COLS = RW_COLS + ATT_COLS
D_FF = 3 * D_MODEL
CONV_WIDTH = 3

kernel_name = "hybrid_rwkv7_dilated_alibi_convglu"


def rms_norm(x, g):
    xf = x.astype(jnp.float32)
    y = xf * lax.rsqrt(jnp.mean(xf * xf, axis=-1, keepdims=True) + NORM_EPS)
    return (y * g.astype(jnp.float32)).astype(x.dtype)


def shift_right(u, n):
    if n == 0:
        return u
    return jnp.pad(u, ((0, 0), (n, 0), (0, 0)))[:, :-n]


def alibi_slopes(n):
    return jnp.asarray(np.array([2.0 ** (-8.0 * (h + 1) / n) for h in range(n)], dtype=np.float32))


def rwkv7_time_mix(P, mu, w0, w_up, a0, a_up, g_up, k_k, k_a, r_k, ln_g, ln_b):
    B, T, _ = P.shape
    H, N = RW_HEADS, RW_HEAD_DIM
    Pm = P + (shift_right(P, 1) - P) * mu
    cuts = list(np.cumsum([RW_WIDTH, RW_WIDTH, RW_WIDTH, DECAY_LORA, AAA_LORA]))
    r, k, v, xw, xa, xg = jnp.split(Pm, cuts, axis=-1)
    w = -jax.nn.softplus(-(w0 + jnp.tanh(xw) @ w_up)) - 0.5
    decay = jnp.exp(-jnp.exp(w.astype(jnp.float32)))
    a = jax.nn.sigmoid(a0 + xa @ a_up)
    g = jax.nn.sigmoid(xg) @ g_up
    hs = lambda z: z.astype(jnp.float32).reshape(B, T, H, N)
    r, k, v, a, decay = hs(r), hs(k), hs(v), hs(a), hs(decay)
    kk = k * k_k.reshape(H, N)
    kk = kk / jnp.maximum(jnp.linalg.norm(kk, axis=-1, keepdims=True), 1e-12)
    k = k * (1.0 + (a - 1.0) * k_a.reshape(H, N))
    a_vec = -kk
    b_vec = kk * a

    def step(S, inp):
        r_t, w_t, k_t, v_t, a_t, b_t = inp
        Sa = jnp.einsum("bhvk,bhk->bhv", S, a_t)
        S = S * w_t[:, :, None, :] + Sa[..., None] * b_t[:, :, None, :] + v_t[..., None] * k_t[:, :, None, :]
        y = jnp.einsum("bhvk,bhk->bhv", S, r_t)
        return S, y

    tm = lambda z: jnp.swapaxes(z, 0, 1)
    S0 = jnp.zeros((B, H, N, N), jnp.float32)
    _, y = lax.scan(step, S0, (tm(r), tm(decay), tm(k), tm(v), tm(a_vec), tm(b_vec)))
    y = jnp.swapaxes(y, 0, 1)
    mean = jnp.mean(y, axis=-1, keepdims=True)
    var = jnp.mean(jnp.square(y - mean), axis=-1, keepdims=True)
    y = ((y - mean) * lax.rsqrt(var + RW_LN_EPS)).reshape(B, T, RW_WIDTH) * ln_g + ln_b
    bonus = (jnp.sum(r * k * r_k, axis=-1, keepdims=True) * v).reshape(B, T, RW_WIDTH)
    return ((y + bonus) * g).astype(P.dtype)


def dilated_group_attention(q, k, v, window, dilation, slopes):
    B, T, H, E = q.shape
    L = window // dilation
    span = L * dilation
    Tp = -(-T // span) * span
    pad = Tp - T
    nb = Tp // span

    def to_blocks(z):
        z = jnp.pad(z, ((0, 0), (0, pad), (0, 0), (0, 0)))
        return z.reshape(B, nb, L, dilation, H, E)

    def with_prev(z):
        zp = jnp.pad(z, ((0, 0), (1, 0), (0, 0), (0, 0), (0, 0), (0, 0)))[:, :-1]
        return jnp.concatenate([zp, z], axis=2)

    qb = to_blocks(q)
    kc = with_prev(to_blocks(k))
    vc = with_prev(to_blocks(v))
    s = jnp.einsum("bnqrhe,bnkrhe->bnrhqk", qb, kc).astype(jnp.float32) * (E ** -0.5)
    qi = jnp.arange(L)[:, None]
    kj = jnp.arange(2 * L)[None, :]
    steps = qi + L - kj
    blk = jnp.arange(nb)[:, None, None]
    valid = (steps >= 0) & (steps <= L) & (blk * L - L + kj >= 0)
    bias = -slopes[:, None, None] * (dilation * steps).astype(jnp.float32)[None]
    logits = jnp.where(valid[None, :, None, None], s + bias, -jnp.inf)
    lse = jax.nn.logsumexp(logits, axis=-1)
    prob = jnp.exp(logits - lse[..., None])
    o = jnp.einsum("bnrhqk,bnkrhe->bnqrhe", prob.astype(v.dtype), vc)
    o = o.reshape(B, Tp, H, E)[:, :T]
    lse = jnp.moveaxis(lse, 4, 2).reshape(B, Tp, H)[:, :T]
    return o, lse


def dilated_mixture_attention(P, slopes):
    B, T, _ = P.shape
    q, k, v = [z.reshape(B, T, ATT_HEADS, ATT_HEAD_DIM) for z in jnp.split(P, 3, axis=-1)]
    outs, lses = [], []
    for gi, (window, dilation) in enumerate(ATT_GROUPS):
        hsl = slice(gi * ATT_HEADS_PER_GROUP, (gi + 1) * ATT_HEADS_PER_GROUP)
        o, l = dilated_group_attention(q[:, :, hsl], k[:, :, hsl], v[:, :, hsl], window, dilation, slopes[hsl])
        outs.append(o.astype(jnp.float32))
        lses.append(l)
    wts = jax.nn.softmax(jnp.stack(lses, axis=0), axis=0)
    o = jnp.sum(wts[..., None] * jnp.stack(outs, axis=0), axis=0)
    return o.reshape(B, T, ATT_OUT).astype(P.dtype)


def conv_glu_ffn(h, w_up, conv_w, conv_b, w_down):
    u = h @ w_up
    u = conv_b + sum(conv_w[j] * shift_right(u, j) for j in range(CONV_WIDTH))
    gate, val = jnp.split(u, 2, axis=-1)
    return (jax.nn.gelu(gate, approximate=True) * val) @ w_down


def _fwd_setup_inputs(seed: int = 0) -> dict:
    key = jax.random.key(seed)
    ks = jax.random.split(key, 32)
    Ld = DEPTH
    nrm = lambda kk, shape, fan: jax.random.normal(kk, shape, jnp.float32) * (fan ** -0.5)
    gain = lambda kk, n: 1.0 + 0.1 * jax.random.normal(kk, (Ld, n), jnp.float32)
    small = lambda kk, shape, s: s * jax.random.normal(kk, shape, jnp.float32)
    conv_w = jnp.array([1.0, 0.0, 0.0], jnp.float32)[None, :, None] + small(ks[22], (Ld, CONV_WIDTH, 2 * D_FF), 0.2)
    return {
        "x": jax.random.normal(ks[0], (BATCH, SEQ, D_MODEL), jnp.float32),
        "p": jax.random.normal(ks[1], (DEPTH, BATCH, SEQ, PLE_DIM), jnp.float32),
        "g_mix": gain(ks[2], D_MODEL),
        "w_in": nrm(ks[3], (Ld, D_MODEL, IN_COLS), D_MODEL),
        "rw_mu": jax.random.uniform(ks[4], (Ld, RW_COLS), jnp.float32),
        "rw_w0": jax.random.uniform(ks[5], (Ld, RW_WIDTH), jnp.float32, minval=-6.5, maxval=-1.5),
        "rw_w_up": 0.1 * nrm(ks[6], (Ld, DECAY_LORA, RW_WIDTH), DECAY_LORA),
        "rw_a0": small(ks[7], (Ld, RW_WIDTH), 0.1),
        "rw_a_up": nrm(ks[8], (Ld, AAA_LORA, RW_WIDTH), AAA_LORA),
        "rw_g_up": nrm(ks[9], (Ld, GATE_LORA, RW_WIDTH), GATE_LORA),
        "rw_k_k": 0.85 + small(ks[10], (Ld, RW_WIDTH), 0.05),
        "rw_k_a": 1.0 + small(ks[11], (Ld, RW_WIDTH), 0.05),
        "rw_r_k": small(ks[12], (Ld, RW_HEADS, RW_HEAD_DIM), 0.1),
        "rw_ln_g": gain(ks[13], RW_WIDTH),
        "rw_ln_b": small(ks[14], (Ld, RW_WIDTH), 0.01),
        "w_branch_a": nrm(ks[15], (Ld, RW_WIDTH, D_MODEL), RW_WIDTH),
        "w_branch_b": nrm(ks[16], (Ld, ATT_OUT, D_MODEL), ATT_OUT),
        "w_gate": nrm(ks[17], (Ld, D_MODEL, 2 * D_MODEL), D_MODEL),
        "b_gate": small(ks[18], (Ld, 2 * D_MODEL), 0.01),
        "w_out": nrm(ks[19], (Ld, D_MODEL, D_MODEL), D_MODEL),
        "g_ffn": gain(ks[20], D_MODEL),
        "w_up": nrm(ks[21], (Ld, D_MODEL, 2 * D_FF), D_MODEL),
        "conv_w": conv_w,
        "conv_b": small(ks[23], (Ld, 2 * D_FF), 0.01),
        "w_down": nrm(ks[24], (Ld, D_FF, D_MODEL), D_FF),
        "g_ple": gain(ks[25], D_MODEL),
        "w_ple_gate": nrm(ks[26], (Ld, D_MODEL, D_MODEL), D_MODEL),
        "w_ple": nrm(ks[27], (Ld, PLE_DIM, D_MODEL), PLE_DIM),
        "g_final": 1.0 + 0.1 * jax.random.normal(ks[28], (D_MODEL,), jnp.float32),
    }


def _fwd_reference(x, p, g_mix, w_in, rw_mu, rw_w0, rw_w_up, rw_a0, rw_a_up, rw_g_up, rw_k_k, rw_k_a, rw_r_k,
              rw_ln_g, rw_ln_b, w_branch_a, w_branch_b, w_gate, b_gate, w_out, g_ffn, w_up, conv_w, conv_b,
              w_down, g_ple, w_ple_gate, w_ple, g_final):
    slopes = alibi_slopes(ATT_HEADS)
    for i in range(DEPTH):
        h = rms_norm(x, g_mix[i])
        proj = h @ w_in[i]
        y_a = rwkv7_time_mix(proj[..., :RW_COLS], rw_mu[i], rw_w0[i], rw_w_up[i], rw_a0[i], rw_a_up[i],
                             rw_g_up[i], rw_k_k[i], rw_k_a[i], rw_r_k[i], rw_ln_g[i], rw_ln_b[i])
        y_b = dilated_mixture_attention(proj[..., RW_COLS:], slopes)
        gate_a, gate_b = jnp.split(jax.nn.sigmoid(h @ w_gate[i] + b_gate[i]), 2, axis=-1)
        merged = gate_a * (y_a @ w_branch_a[i]) + gate_b * (y_b @ w_branch_b[i])
        x = x + merged @ w_out[i]
        x = x + conv_glu_ffn(rms_norm(x, g_ffn[i]), w_up[i], conv_w[i], conv_b[i], w_down[i])
        ple_gate = jax.nn.sigmoid(rms_norm(x, g_ple[i]) @ w_ple_gate[i])
        x = x + ple_gate * (p[i] @ w_ple[i])
    return rms_norm(x, g_final)


import jax as _jax
import jax.numpy as _jnp

TWIN_FORMAT = 'train_step'
FWD_PARAMS = ['x', 'p', 'g_mix', 'w_in', 'rw_mu', 'rw_w0', 'rw_w_up', 'rw_a0', 'rw_a_up', 'rw_g_up', 'rw_k_k', 'rw_k_a', 'rw_r_k', 'rw_ln_g', 'rw_ln_b', 'w_branch_a', 'w_branch_b', 'w_gate', 'b_gate', 'w_out', 'g_ffn', 'w_up', 'conv_w', 'conv_b', 'w_down', 'g_ple', 'w_ple_gate', 'w_ple', 'g_final']
TWIN_WEIGHTS = ['g_mix', 'w_in', 'rw_mu', 'rw_w0', 'rw_w_up', 'rw_a0', 'rw_a_up', 'rw_g_up', 'rw_k_k', 'rw_k_a', 'rw_r_k', 'rw_ln_g', 'rw_ln_b', 'w_branch_a', 'w_branch_b', 'w_gate', 'b_gate', 'w_out', 'g_ffn', 'w_up', 'conv_w', 'conv_b', 'w_down', 'g_ple', 'w_ple_gate', 'w_ple', 'g_final']
TWIN_DIFF_INPUT = 'x'
TWIN_INPUTS = ['x', 'p', 'g_mix', 'w_in', 'rw_mu', 'rw_w0', 'rw_w_up', 'rw_a0', 'rw_a_up', 'rw_g_up', 'rw_k_k', 'rw_k_a', 'rw_r_k', 'rw_ln_g', 'rw_ln_b', 'w_branch_a', 'w_branch_b', 'w_gate', 'b_gate', 'w_out', 'g_ffn', 'w_up', 'conv_w', 'conv_b', 'w_down', 'g_ple', 'w_ple_gate', 'w_ple', 'g_final', 'loss_target', 'm_g_mix', 'm_w_in', 'm_rw_mu', 'm_rw_w0', 'm_rw_w_up', 'm_rw_a0', 'm_rw_a_up', 'm_rw_g_up', 'm_rw_k_k', 'm_rw_k_a', 'm_rw_r_k', 'm_rw_ln_g', 'm_rw_ln_b', 'm_w_branch_a', 'm_w_branch_b', 'm_w_gate', 'm_b_gate', 'm_w_out', 'm_g_ffn', 'm_w_up', 'm_conv_w', 'm_conv_b', 'm_w_down', 'm_g_ple', 'm_w_ple_gate', 'm_w_ple', 'm_g_final', 'v_g_mix', 'v_w_in', 'v_rw_mu', 'v_rw_w0', 'v_rw_w_up', 'v_rw_a0', 'v_rw_a_up', 'v_rw_g_up', 'v_rw_k_k', 'v_rw_k_a', 'v_rw_r_k', 'v_rw_ln_g', 'v_rw_ln_b', 'v_w_branch_a', 'v_w_branch_b', 'v_w_gate', 'v_b_gate', 'v_w_out', 'v_g_ffn', 'v_w_up', 'v_conv_w', 'v_conv_b', 'v_w_down', 'v_g_ple', 'v_w_ple_gate', 'v_w_ple', 'v_g_final']
TWIN_OUTPUTS = ['loss', 'grad_x', 'grad_g_mix', 'grad_w_in', 'grad_rw_mu', 'grad_rw_w0', 'grad_rw_w_up', 'grad_rw_a0', 'grad_rw_a_up', 'grad_rw_g_up', 'grad_rw_k_k', 'grad_rw_k_a', 'grad_rw_r_k', 'grad_rw_ln_g', 'grad_rw_ln_b', 'grad_w_branch_a', 'grad_w_branch_b', 'grad_w_gate', 'grad_b_gate', 'grad_w_out', 'grad_g_ffn', 'grad_w_up', 'grad_conv_w', 'grad_conv_b', 'grad_w_down', 'grad_g_ple', 'grad_w_ple_gate', 'grad_w_ple', 'grad_g_final', 'delta_g_mix', 'delta_w_in', 'delta_rw_mu', 'delta_rw_w0', 'delta_rw_w_up', 'delta_rw_a0', 'delta_rw_a_up', 'delta_rw_g_up', 'delta_rw_k_k', 'delta_rw_k_a', 'delta_rw_r_k', 'delta_rw_ln_g', 'delta_rw_ln_b', 'delta_w_branch_a', 'delta_w_branch_b', 'delta_w_gate', 'delta_b_gate', 'delta_w_out', 'delta_g_ffn', 'delta_w_up', 'delta_conv_w', 'delta_conv_b', 'delta_w_down', 'delta_g_ple', 'delta_w_ple_gate', 'delta_w_ple', 'delta_g_final', 'new_m_g_mix', 'new_m_w_in', 'new_m_rw_mu', 'new_m_rw_w0', 'new_m_rw_w_up', 'new_m_rw_a0', 'new_m_rw_a_up', 'new_m_rw_g_up', 'new_m_rw_k_k', 'new_m_rw_k_a', 'new_m_rw_r_k', 'new_m_rw_ln_g', 'new_m_rw_ln_b', 'new_m_w_branch_a', 'new_m_w_branch_b', 'new_m_w_gate', 'new_m_b_gate', 'new_m_w_out', 'new_m_g_ffn', 'new_m_w_up', 'new_m_conv_w', 'new_m_conv_b', 'new_m_w_down', 'new_m_g_ple', 'new_m_w_ple_gate', 'new_m_w_ple', 'new_m_g_final', 'new_v_g_mix', 'new_v_w_in', 'new_v_rw_mu', 'new_v_rw_w0', 'new_v_rw_w_up', 'new_v_rw_a0', 'new_v_rw_a_up', 'new_v_rw_g_up', 'new_v_rw_k_k', 'new_v_rw_k_a', 'new_v_rw_r_k', 'new_v_rw_ln_g', 'new_v_rw_ln_b', 'new_v_w_branch_a', 'new_v_w_branch_b', 'new_v_w_gate', 'new_v_b_gate', 'new_v_w_out', 'new_v_g_ffn', 'new_v_w_up', 'new_v_conv_w', 'new_v_conv_b', 'new_v_w_down', 'new_v_g_ple', 'new_v_w_ple_gate', 'new_v_w_ple', 'new_v_g_final']
TWIN_LEAF_KINDS = {'loss': 'loss', 'grad_x': 'grad_x', 'grad_g_mix': 'grad_w', 'grad_w_in': 'grad_w', 'grad_rw_mu': 'grad_w', 'grad_rw_w0': 'grad_w', 'grad_rw_w_up': 'grad_w', 'grad_rw_a0': 'grad_w', 'grad_rw_a_up': 'grad_w', 'grad_rw_g_up': 'grad_w', 'grad_rw_k_k': 'grad_w', 'grad_rw_k_a': 'grad_w', 'grad_rw_r_k': 'grad_w', 'grad_rw_ln_g': 'grad_w', 'grad_rw_ln_b': 'grad_w', 'grad_w_branch_a': 'grad_w', 'grad_w_branch_b': 'grad_w', 'grad_w_gate': 'grad_w', 'grad_b_gate': 'grad_w', 'grad_w_out': 'grad_w', 'grad_g_ffn': 'grad_w', 'grad_w_up': 'grad_w', 'grad_conv_w': 'grad_w', 'grad_conv_b': 'grad_w', 'grad_w_down': 'grad_w', 'grad_g_ple': 'grad_w', 'grad_w_ple_gate': 'grad_w', 'grad_w_ple': 'grad_w', 'grad_g_final': 'grad_w', 'delta_g_mix': 'delta_w', 'delta_w_in': 'delta_w', 'delta_rw_mu': 'delta_w', 'delta_rw_w0': 'delta_w', 'delta_rw_w_up': 'delta_w', 'delta_rw_a0': 'delta_w', 'delta_rw_a_up': 'delta_w', 'delta_rw_g_up': 'delta_w', 'delta_rw_k_k': 'delta_w', 'delta_rw_k_a': 'delta_w', 'delta_rw_r_k': 'delta_w', 'delta_rw_ln_g': 'delta_w', 'delta_rw_ln_b': 'delta_w', 'delta_w_branch_a': 'delta_w', 'delta_w_branch_b': 'delta_w', 'delta_w_gate': 'delta_w', 'delta_b_gate': 'delta_w', 'delta_w_out': 'delta_w', 'delta_g_ffn': 'delta_w', 'delta_w_up': 'delta_w', 'delta_conv_w': 'delta_w', 'delta_conv_b': 'delta_w', 'delta_w_down': 'delta_w', 'delta_g_ple': 'delta_w', 'delta_w_ple_gate': 'delta_w', 'delta_w_ple': 'delta_w', 'delta_g_final': 'delta_w', 'new_m_g_mix': 'new_m', 'new_m_w_in': 'new_m', 'new_m_rw_mu': 'new_m', 'new_m_rw_w0': 'new_m', 'new_m_rw_w_up': 'new_m', 'new_m_rw_a0': 'new_m', 'new_m_rw_a_up': 'new_m', 'new_m_rw_g_up': 'new_m', 'new_m_rw_k_k': 'new_m', 'new_m_rw_k_a': 'new_m', 'new_m_rw_r_k': 'new_m', 'new_m_rw_ln_g': 'new_m', 'new_m_rw_ln_b': 'new_m', 'new_m_w_branch_a': 'new_m', 'new_m_w_branch_b': 'new_m', 'new_m_w_gate': 'new_m', 'new_m_b_gate': 'new_m', 'new_m_w_out': 'new_m', 'new_m_g_ffn': 'new_m', 'new_m_w_up': 'new_m', 'new_m_conv_w': 'new_m', 'new_m_conv_b': 'new_m', 'new_m_w_down': 'new_m', 'new_m_g_ple': 'new_m', 'new_m_w_ple_gate': 'new_m', 'new_m_w_ple': 'new_m', 'new_m_g_final': 'new_m', 'new_v_g_mix': 'new_v', 'new_v_w_in': 'new_v', 'new_v_rw_mu': 'new_v', 'new_v_rw_w0': 'new_v', 'new_v_rw_w_up': 'new_v', 'new_v_rw_a0': 'new_v', 'new_v_rw_a_up': 'new_v', 'new_v_rw_g_up': 'new_v', 'new_v_rw_k_k': 'new_v', 'new_v_rw_k_a': 'new_v', 'new_v_rw_r_k': 'new_v', 'new_v_rw_ln_g': 'new_v', 'new_v_rw_ln_b': 'new_v', 'new_v_w_branch_a': 'new_v', 'new_v_w_branch_b': 'new_v', 'new_v_w_gate': 'new_v', 'new_v_b_gate': 'new_v', 'new_v_w_out': 'new_v', 'new_v_g_ffn': 'new_v', 'new_v_w_up': 'new_v', 'new_v_conv_w': 'new_v', 'new_v_conv_b': 'new_v', 'new_v_w_down': 'new_v', 'new_v_g_ple': 'new_v', 'new_v_w_ple_gate': 'new_v', 'new_v_w_ple': 'new_v', 'new_v_g_final': 'new_v'}


def _forward(args):
    return _fwd_reference(*[args[k] for k in FWD_PARAMS])


def _output_shape():
    out = _jax.eval_shape(lambda: _forward(_fwd_setup_inputs(0)))
    return out.shape, out.dtype

N_MICROBATCH = 1
ADAM_LR = 0.001
ADAM_B1 = 0.9
ADAM_B2 = 0.999
ADAM_EPS = 1e-08
ADAM_WD = 0.01
ADAM_STEP = 10
PER_EXAMPLE_BATCH_AXIS = {'x': 0, 'p': 1, 'loss_target': 0}
SHARED_INPUTS = []
_WEIGHT_DTYPES = {'g_mix': _jnp.float32, 'w_in': _jnp.float32, 'rw_mu': _jnp.float32, 'rw_w0': _jnp.float32, 'rw_w_up': _jnp.float32, 'rw_a0': _jnp.float32, 'rw_a_up': _jnp.float32, 'rw_g_up': _jnp.float32, 'rw_k_k': _jnp.float32, 'rw_k_a': _jnp.float32, 'rw_r_k': _jnp.float32, 'rw_ln_g': _jnp.float32, 'rw_ln_b': _jnp.float32, 'w_branch_a': _jnp.float32, 'w_branch_b': _jnp.float32, 'w_gate': _jnp.float32, 'b_gate': _jnp.float32, 'w_out': _jnp.float32, 'g_ffn': _jnp.float32, 'w_up': _jnp.float32, 'conv_w': _jnp.float32, 'conv_b': _jnp.float32, 'w_down': _jnp.float32, 'g_ple': _jnp.float32, 'w_ple_gate': _jnp.float32, 'w_ple': _jnp.float32, 'g_final': _jnp.float32}
MOMENT_SCALE = {'g_mix': 1.192860e-01, 'w_in': 5.933534e-02, 'rw_mu': 1.312985e-01, 'rw_w0': 3.519737e-02, 'rw_w_up': 3.158987e-03, 'rw_a0': 3.021009e-02, 'rw_a_up': 2.891498e-02, 'rw_g_up': 7.853262e-02, 'rw_k_k': 1.360414e-01, 'rw_k_a': 8.926919e-02, 'rw_r_k': 1.709251e-01, 'rw_ln_g': 7.876489e-02, 'rw_ln_b': 9.709578e-02, 'w_branch_a': 5.711235e-02, 'w_branch_b': 3.375667e-02, 'w_gate': 1.812323e-02, 'b_gate': 1.816494e-02, 'w_out': 6.565761e-02, 'g_ffn': 1.440089e-01, 'w_up': 5.840921e-02, 'conv_w': 5.823082e-02, 'conv_b': 6.026401e-02, 'w_down': 1.025817e-01, 'g_ple': 2.704539e-02, 'w_ple_gate': 2.770873e-02, 'w_ple': 7.668912e-02, 'g_final': 3.199660e+01}


def _to_microbatches(a, axis):
    t = _jnp.moveaxis(a, axis, 0)
    t = t.reshape((N_MICROBATCH, t.shape[0] // N_MICROBATCH) + t.shape[1:])
    return _jnp.moveaxis(t, 1, axis + 1)


def setup_inputs(seed: int = 0) -> dict:
    inp = _fwd_setup_inputs(seed)
    key = _jax.random.fold_in(_jax.random.key(seed), 7919)
    shape, _ = _output_shape()
    out = dict(inp)
    out["loss_target"] = _jax.random.normal(_jax.random.fold_in(key, 0), shape, _jnp.float32)
    for i, name in enumerate(TWIN_WEIGHTS):
        w = inp[name].astype(_jnp.float32)
        if MOMENT_SCALE is None:
            s = _jnp.sqrt(_jnp.mean(_jnp.square(w)) + 1e-30)
        else:
            s = MOMENT_SCALE[name]
        km, kv = _jax.random.split(_jax.random.fold_in(key, i + 1))
        out[name] = w
        out["m_" + name] = s * _jax.random.normal(km, w.shape, _jnp.float32)
        out["v_" + name] = (s * s) * _jax.random.uniform(kv, w.shape, _jnp.float32, 0.5, 1.5)
    if N_MICROBATCH > 1:
        for name, axis in PER_EXAMPLE_BATCH_AXIS.items():
            out[name] = _to_microbatches(out[name], axis)
    return {'x': out['x'], 'p': out['p'], 'g_mix': out['g_mix'], 'w_in': out['w_in'], 'rw_mu': out['rw_mu'], 'rw_w0': out['rw_w0'], 'rw_w_up': out['rw_w_up'], 'rw_a0': out['rw_a0'], 'rw_a_up': out['rw_a_up'], 'rw_g_up': out['rw_g_up'], 'rw_k_k': out['rw_k_k'], 'rw_k_a': out['rw_k_a'], 'rw_r_k': out['rw_r_k'], 'rw_ln_g': out['rw_ln_g'], 'rw_ln_b': out['rw_ln_b'], 'w_branch_a': out['w_branch_a'], 'w_branch_b': out['w_branch_b'], 'w_gate': out['w_gate'], 'b_gate': out['b_gate'], 'w_out': out['w_out'], 'g_ffn': out['g_ffn'], 'w_up': out['w_up'], 'conv_w': out['conv_w'], 'conv_b': out['conv_b'], 'w_down': out['w_down'], 'g_ple': out['g_ple'], 'w_ple_gate': out['w_ple_gate'], 'w_ple': out['w_ple'], 'g_final': out['g_final'], 'loss_target': out['loss_target'], 'm_g_mix': out['m_g_mix'], 'm_w_in': out['m_w_in'], 'm_rw_mu': out['m_rw_mu'], 'm_rw_w0': out['m_rw_w0'], 'm_rw_w_up': out['m_rw_w_up'], 'm_rw_a0': out['m_rw_a0'], 'm_rw_a_up': out['m_rw_a_up'], 'm_rw_g_up': out['m_rw_g_up'], 'm_rw_k_k': out['m_rw_k_k'], 'm_rw_k_a': out['m_rw_k_a'], 'm_rw_r_k': out['m_rw_r_k'], 'm_rw_ln_g': out['m_rw_ln_g'], 'm_rw_ln_b': out['m_rw_ln_b'], 'm_w_branch_a': out['m_w_branch_a'], 'm_w_branch_b': out['m_w_branch_b'], 'm_w_gate': out['m_w_gate'], 'm_b_gate': out['m_b_gate'], 'm_w_out': out['m_w_out'], 'm_g_ffn': out['m_g_ffn'], 'm_w_up': out['m_w_up'], 'm_conv_w': out['m_conv_w'], 'm_conv_b': out['m_conv_b'], 'm_w_down': out['m_w_down'], 'm_g_ple': out['m_g_ple'], 'm_w_ple_gate': out['m_w_ple_gate'], 'm_w_ple': out['m_w_ple'], 'm_g_final': out['m_g_final'], 'v_g_mix': out['v_g_mix'], 'v_w_in': out['v_w_in'], 'v_rw_mu': out['v_rw_mu'], 'v_rw_w0': out['v_rw_w0'], 'v_rw_w_up': out['v_rw_w_up'], 'v_rw_a0': out['v_rw_a0'], 'v_rw_a_up': out['v_rw_a_up'], 'v_rw_g_up': out['v_rw_g_up'], 'v_rw_k_k': out['v_rw_k_k'], 'v_rw_k_a': out['v_rw_k_a'], 'v_rw_r_k': out['v_rw_r_k'], 'v_rw_ln_g': out['v_rw_ln_g'], 'v_rw_ln_b': out['v_rw_ln_b'], 'v_w_branch_a': out['v_w_branch_a'], 'v_w_branch_b': out['v_w_branch_b'], 'v_w_gate': out['v_w_gate'], 'v_b_gate': out['v_b_gate'], 'v_w_out': out['v_w_out'], 'v_g_ffn': out['v_g_ffn'], 'v_w_up': out['v_w_up'], 'v_conv_w': out['v_conv_w'], 'v_conv_b': out['v_conv_b'], 'v_w_down': out['v_w_down'], 'v_g_ple': out['v_g_ple'], 'v_w_ple_gate': out['v_w_ple_gate'], 'v_w_ple': out['v_w_ple'], 'v_g_final': out['v_g_final']}


def _loss(weights, diff, rest, loss_target):
    with _jax.named_scope("forward"):
        args = {**rest, TWIN_DIFF_INPUT: diff, **{k: w.astype(_WEIGHT_DTYPES[k]) for k, w in weights.items()}}
        y = _forward(args)
    with _jax.named_scope("loss_head"):
        err = _jnp.square(y.astype(_jnp.float32) - loss_target)
        return 0.5 * _jnp.sum(_jnp.mean(err, axis=-1)) if err.ndim else 0.5 * err


def _adamw(w, g, m, v):
    m = ADAM_B1 * m + (1.0 - ADAM_B1) * g
    v = ADAM_B2 * v + (1.0 - ADAM_B2) * _jnp.square(g)
    m_hat = m / (1.0 - ADAM_B1 ** ADAM_STEP)
    v_hat = v / (1.0 - ADAM_B2 ** ADAM_STEP)
    delta = -ADAM_LR * (m_hat / (_jnp.sqrt(v_hat) + ADAM_EPS) + ADAM_WD * w)
    return delta, m, v


def reference(x, p, g_mix, w_in, rw_mu, rw_w0, rw_w_up, rw_a0, rw_a_up, rw_g_up, rw_k_k, rw_k_a, rw_r_k, rw_ln_g, rw_ln_b, w_branch_a, w_branch_b, w_gate, b_gate, w_out, g_ffn, w_up, conv_w, conv_b, w_down, g_ple, w_ple_gate, w_ple, g_final, loss_target, m_g_mix, m_w_in, m_rw_mu, m_rw_w0, m_rw_w_up, m_rw_a0, m_rw_a_up, m_rw_g_up, m_rw_k_k, m_rw_k_a, m_rw_r_k, m_rw_ln_g, m_rw_ln_b, m_w_branch_a, m_w_branch_b, m_w_gate, m_b_gate, m_w_out, m_g_ffn, m_w_up, m_conv_w, m_conv_b, m_w_down, m_g_ple, m_w_ple_gate, m_w_ple, m_g_final, v_g_mix, v_w_in, v_rw_mu, v_rw_w0, v_rw_w_up, v_rw_a0, v_rw_a_up, v_rw_g_up, v_rw_k_k, v_rw_k_a, v_rw_r_k, v_rw_ln_g, v_rw_ln_b, v_w_branch_a, v_w_branch_b, v_w_gate, v_b_gate, v_w_out, v_g_ffn, v_w_up, v_conv_w, v_conv_b, v_w_down, v_g_ple, v_w_ple_gate, v_w_ple, v_g_final):
    given = dict(x=x, p=p, g_mix=g_mix, w_in=w_in, rw_mu=rw_mu, rw_w0=rw_w0, rw_w_up=rw_w_up, rw_a0=rw_a0, rw_a_up=rw_a_up, rw_g_up=rw_g_up, rw_k_k=rw_k_k, rw_k_a=rw_k_a, rw_r_k=rw_r_k, rw_ln_g=rw_ln_g, rw_ln_b=rw_ln_b, w_branch_a=w_branch_a, w_branch_b=w_branch_b, w_gate=w_gate, b_gate=b_gate, w_out=w_out, g_ffn=g_ffn, w_up=w_up, conv_w=conv_w, conv_b=conv_b, w_down=w_down, g_ple=g_ple, w_ple_gate=w_ple_gate, w_ple=w_ple, g_final=g_final, loss_target=loss_target, m_g_mix=m_g_mix, m_w_in=m_w_in, m_rw_mu=m_rw_mu, m_rw_w0=m_rw_w0, m_rw_w_up=m_rw_w_up, m_rw_a0=m_rw_a0, m_rw_a_up=m_rw_a_up, m_rw_g_up=m_rw_g_up, m_rw_k_k=m_rw_k_k, m_rw_k_a=m_rw_k_a, m_rw_r_k=m_rw_r_k, m_rw_ln_g=m_rw_ln_g, m_rw_ln_b=m_rw_ln_b, m_w_branch_a=m_w_branch_a, m_w_branch_b=m_w_branch_b, m_w_gate=m_w_gate, m_b_gate=m_b_gate, m_w_out=m_w_out, m_g_ffn=m_g_ffn, m_w_up=m_w_up, m_conv_w=m_conv_w, m_conv_b=m_conv_b, m_w_down=m_w_down, m_g_ple=m_g_ple, m_w_ple_gate=m_w_ple_gate, m_w_ple=m_w_ple, m_g_final=m_g_final, v_g_mix=v_g_mix, v_w_in=v_w_in, v_rw_mu=v_rw_mu, v_rw_w0=v_rw_w0, v_rw_w_up=v_rw_w_up, v_rw_a0=v_rw_a0, v_rw_a_up=v_rw_a_up, v_rw_g_up=v_rw_g_up, v_rw_k_k=v_rw_k_k, v_rw_k_a=v_rw_k_a, v_rw_r_k=v_rw_r_k, v_rw_ln_g=v_rw_ln_g, v_rw_ln_b=v_rw_ln_b, v_w_branch_a=v_w_branch_a, v_w_branch_b=v_w_branch_b, v_w_gate=v_w_gate, v_b_gate=v_b_gate, v_w_out=v_w_out, v_g_ffn=v_g_ffn, v_w_up=v_w_up, v_conv_w=v_conv_w, v_conv_b=v_conv_b, v_w_down=v_w_down, v_g_ple=v_g_ple, v_w_ple_gate=v_w_ple_gate, v_w_ple=v_w_ple, v_g_final=v_g_final)
    weights = {n: given[n] for n in TWIN_WEIGHTS}
    shared = {n: given[n] for n in SHARED_INPUTS}
    per_example = {n: given[n] for n in ['x', 'p']}
    grad_fn = _jax.value_and_grad(_loss, argnums=(0, 1))

    def one_microbatch(ex, loss_target):
        ex = dict(ex)
        diff = ex.pop(TWIN_DIFF_INPUT)
        return grad_fn(weights, diff, {**shared, **ex}, loss_target)

    if N_MICROBATCH == 1:
        loss, (grad_w, grad_x) = one_microbatch(per_example, given["loss_target"])
    else:
        def body(carry, xs):
            loss_sum, grad_sum = carry
            l_k, (gw_k, gx_k) = one_microbatch(xs[0], xs[1])
            with _jax.named_scope("update"):
                return (loss_sum + l_k, _jax.tree.map(_jnp.add, grad_sum, gw_k)), gx_k

        init = (_jnp.zeros((), _jnp.float32), _jax.tree.map(_jnp.zeros_like, weights))
        (loss, grad_w), grad_x = _jax.lax.scan(body, init, (per_example, given["loss_target"]))
    with _jax.named_scope("update"):
        delta_w, new_m, new_v = {}, {}, {}
        for n in TWIN_WEIGHTS:
            delta_w[n], new_m[n], new_v[n] = _adamw(weights[n], grad_w[n], given["m_" + n], given["v_" + n])
    return (loss, grad_x, *[grad_w[n] for n in TWIN_WEIGHTS], *[delta_w[n] for n in TWIN_WEIGHTS],
            *[new_m[n] for n in TWIN_WEIGHTS], *[new_v[n] for n in TWIN_WEIGHTS])
```

```python
import math

import jax
import jax.numpy as jnp
import numpy as np
from jax import lax
from jax.experimental import pallas as pl
from jax.experimental.pallas import tpu as pltpu

F32 = jnp.float32
BF16 = jnp.bfloat16

D_MODEL = 1024
NORM_EPS = 1e-6
RW_HEADS = 8
RW_HEAD_DIM = 64
RW_WIDTH = 512
RW_LN_EPS = 64e-5
ATT_GROUP_DILATION = (1, 4, 16)
ATT_BLOCK = 128
ATT_HEADS = 12
ATT_HEAD_DIM = 64
ATT_GROUP_WIDTH = 256
ATT_WIDTH = 768
D_FF = 3072

ADAM_LR = 0.001
ADAM_B1 = 0.9
ADAM_B2 = 0.999
ADAM_EPS = 1e-08
ADAM_WD = 0.01
ADAM_STEP = 10

SUBLANES = 8
LANES = 128
VMEM_LIMIT = 56 * 1024 * 1024
SCAN_CHUNK = 128
SCAN_SUB = 64
N_CHIPS = 4
N_DEV = 8
MESH = pl.DeviceIdType.MESH


def _params(sem=None):
    return pltpu.CompilerParams(dimension_semantics=sem, vmem_limit_bytes=VMEM_LIMIT)


def _pick(dim, pref):
    if dim % LANES != 0 or dim <= pref:
        return dim
    best = LANES
    for t in range(LANES, pref + 1, LANES):
        if dim % t == 0:
            best = t
    return best


def _mm(name, a, b, mode, out_dtype=F32, add=None, tm=512, tn=512, tk=1024):
    if mode == "nn":
        (M, K), (K2, N) = a.shape, b.shape
    elif mode == "nt":
        (M, K), (N, K2) = a.shape, b.shape
    else:
        (K, M), (K2, N) = a.shape, b.shape
    assert K == K2, (name, a.shape, b.shape, mode)
    tm, tn, tk = _pick(M, tm), _pick(N, tn), _pick(K, tk)
    nk = K // tk
    if mode == "nn":
        a_spec = pl.BlockSpec((tm, tk), lambda i, j, k: (i, k))
        b_spec = pl.BlockSpec((tk, tn), lambda i, j, k: (k, j))
        dims = (((1,), (0,)), ((), ()))
    elif mode == "nt":
        a_spec = pl.BlockSpec((tm, tk), lambda i, j, k: (i, k))
        b_spec = pl.BlockSpec((tn, tk), lambda i, j, k: (j, k))
        dims = (((1,), (1,)), ((), ()))
    else:
        a_spec = pl.BlockSpec((tk, tm), lambda i, j, k: (k, i))
        b_spec = pl.BlockSpec((tk, tn), lambda i, j, k: (k, j))
        dims = (((0,), (0,)), ((), ()))
    o_spec = pl.BlockSpec((tm, tn), lambda i, j, k: (i, j))
    has_add = add is not None

    def body(*refs):
        if has_add:
            a_ref, b_ref, add_ref, o_ref, acc_ref = refs
        else:
            a_ref, b_ref, o_ref, acc_ref = refs
        k = pl.program_id(2)
        part = lax.dot_general(a_ref[...].astype(BF16), b_ref[...].astype(BF16), dims,
                               preferred_element_type=F32)

        @pl.when(k == 0)
        def _():
            acc_ref[...] = part

        @pl.when(k > 0)
        def _():
            acc_ref[...] += part

        @pl.when(k == nk - 1)
        def _():
            res = acc_ref[...]
            if has_add:
                res = res + add_ref[...].astype(F32)
            o_ref[...] = res.astype(o_ref.dtype)

    ins = [a, b] + ([add] if has_add else [])
    in_specs = [a_spec, b_spec] + ([o_spec] if has_add else [])
    return pl.pallas_call(
        body, name=name, grid=(M // tm, N // tn, nk),
        in_specs=in_specs, out_specs=o_spec,
        out_shape=jax.ShapeDtypeStruct((M, N), out_dtype),
        scratch_shapes=[pltpu.VMEM((tm, tn), F32)],
        compiler_params=_params(("parallel", "parallel", "arbitrary")),
    )(*ins)


def _rowwise(name, fn, T, tT, rows=(), prevs=(), nexts=(), consts=(), outs=()):
    n = T // tT
    per8 = tT // SUBLANES
    in_specs, ins = [], []
    for arr in rows:
        in_specs.append(pl.BlockSpec((tT, arr.shape[1]), lambda i: (i, 0)))
        ins.append(arr)
    for arr in prevs:
        in_specs.append(pl.BlockSpec((SUBLANES, arr.shape[1]), lambda i: (jnp.maximum(i * per8 - 1, 0), 0)))
        ins.append(arr)
    for arr in nexts:
        in_specs.append(pl.BlockSpec((SUBLANES, arr.shape[1]),
                                     lambda i: (jnp.minimum((i + 1) * per8, T // SUBLANES - 1), 0)))
        ins.append(arr)
    for arr in consts:
        in_specs.append(pl.BlockSpec(arr.shape, lambda i, nd=arr.ndim: (0,) * nd))
        ins.append(arr)
    out_specs, out_shapes = [], []
    for o in outs:
        if o[0] == "row":
            out_specs.append(pl.BlockSpec((tT, o[1]), lambda i: (i, 0)))
            out_shapes.append(jax.ShapeDtypeStruct((T, o[1]), o[2]))
        else:
            out_specs.append(pl.BlockSpec(o[1], lambda i: (0, 0)))
            out_shapes.append(jax.ShapeDtypeStruct(o[1], F32))
    nr, npv, nnx, nc = len(rows), len(prevs), len(nexts), len(consts)
    n_in = nr + npv + nnx + nc

    def body(*refs):
        i = pl.program_id(0)
        vals = [r[...] for r in refs[:n_in]]
        res = fn(i, n, vals[:nr], vals[nr:nr + npv], vals[nr + npv:nr + npv + nnx], vals[nr + npv + nnx:])
        for o, o_ref, val in zip(outs, refs[n_in:], res, strict=True):
            if o[0] == "row":
                o_ref[...] = val.astype(o_ref.dtype)
            else:
                @pl.when(i == 0)
                def _(o_ref=o_ref, val=val):
                    o_ref[...] = val.astype(F32)

                @pl.when(i > 0)
                def _(o_ref=o_ref, val=val):
                    o_ref[...] += val.astype(F32)

    res = pl.pallas_call(
        body, name=name, grid=(n,), in_specs=in_specs, out_specs=out_specs, out_shape=out_shapes,
        compiler_params=_params(("arbitrary",)),
    )(*ins)
    return list(res)


def _shift_down(x, prev8, i, s):
    rolled = pltpu.roll(x, s, 0)
    head = pltpu.roll(prev8, s, 0)
    head = jnp.where(i == 0, jnp.zeros_like(head), head)
    rid = lax.broadcasted_iota(jnp.int32, head.shape, 0)
    first = jnp.where(rid < s, head, rolled[:SUBLANES])
    return jnp.concatenate([first, rolled[SUBLANES:]], axis=0)


def _shift_up(x, next8, i, n, s):
    tT = x.shape[0]
    rolled = pltpu.roll(x, tT - s, 0)
    tail = pltpu.roll(next8, SUBLANES - s, 0)
    tail = jnp.where(i == n - 1, jnp.zeros_like(tail), tail)
    rid = lax.broadcasted_iota(jnp.int32, tail.shape, 0)
    last = jnp.where(rid >= SUBLANES - s, tail, rolled[tT - SUBLANES:])
    return jnp.concatenate([rolled[:tT - SUBLANES], last], axis=0)


def _colsum(x):
    return jnp.sum(x, axis=0, keepdims=True)


def _segsum(x, bd):
    return jnp.dot(x, bd, precision=lax.Precision.HIGHEST, preferred_element_type=F32)


def _block_diag_ones(width, seg):
    idx = np.arange(width) // seg
    return jnp.asarray((idx[:, None] == idx[None, :]).astype(np.float32))


def _sigmoid(z):
    return 1.0 / (1.0 + jnp.exp(-z))


def _softplus(z):
    return jnp.maximum(z, 0.0) + jnp.log(1.0 + jnp.exp(-jnp.abs(z)))


def _rms_fwd(x, g):
    r = lax.rsqrt(jnp.mean(x * x, axis=-1, keepdims=True) + NORM_EPS)
    return x * r * g


def _rms_bwd(x, g, dy):
    r = lax.rsqrt(jnp.mean(x * x, axis=-1, keepdims=True) + NORM_EPS)
    gdy = dy * g
    dx = r * (gdy - x * (r * r) * jnp.mean(x * gdy, axis=-1, keepdims=True))
    return dx, dy * x * r


GELU_C = math.sqrt(2.0 / math.pi)


def _gelu(x):
    return 0.5 * x * (1.0 + jnp.tanh(GELU_C * (x + 0.044715 * x * x * x)))


def _gelu_grad(x):
    th = jnp.tanh(GELU_C * (x + 0.044715 * x * x * x))
    return 0.5 * (1.0 + th) + 0.5 * x * (1.0 - th * th) * GELU_C * (1.0 + 3.0 * 0.044715 * x * x)


def _pick_col(blk, lane_id, t):
    return jnp.sum(jnp.where(lane_id == t, blk, 0.0), axis=1, keepdims=True)


def _rwkv_scan_fwd(r4, w4, k4, a4, b4, vT):
    H, T, N = r4.shape
    nC = T // SCAN_CHUNK
    subs = SCAN_CHUNK // SCAN_SUB

    def body(r_ref, w_ref, k_ref, a_ref, b_ref, vT_ref, yT_ref, ck_ref, S_ref):
        c = pl.program_id(0)

        @pl.when(c == 0)
        def _():
            S_ref[...] = jnp.zeros_like(S_ref)

        lane_id = lax.broadcasted_iota(jnp.int32, (N, SCAN_CHUNK), 1)
        yT_ref[...] = jnp.zeros_like(yT_ref)
        for sub in range(subs):
            ck_ref[sub] = S_ref[...]

            def step(t, carry):
                for h in range(H):
                    S = S_ref[h]
                    row = pl.ds(t, 1)
                    v_col = _pick_col(vT_ref[h], lane_id, t)
                    Sa = jnp.sum(S * a_ref[h, row, :], axis=1, keepdims=True)
                    S = S * w_ref[h, row, :] + Sa * b_ref[h, row, :] + v_col * k_ref[h, row, :]
                    y_col = jnp.sum(S * r_ref[h, row, :], axis=1, keepdims=True)
                    S_ref[h] = S
                    yT_ref[h] = jnp.where(lane_id == t, y_col, yT_ref[h])
                return carry

            lax.fori_loop(sub * SCAN_SUB, (sub + 1) * SCAN_SUB, step, 0)

    row_spec = pl.BlockSpec((H, SCAN_CHUNK, N), lambda c: (0, c, 0))
    col_spec = pl.BlockSpec((H, N, SCAN_CHUNK), lambda c: (0, 0, c))
    return pl.pallas_call(
        body, name="rwkv_scan_fwd", grid=(nC,),
        in_specs=[row_spec] * 5 + [col_spec],
        out_specs=[col_spec, pl.BlockSpec((subs, H, N, N), lambda c: (c, 0, 0, 0))],
        out_shape=[jax.ShapeDtypeStruct((H, N, T), F32),
                   jax.ShapeDtypeStruct((T // SCAN_SUB, H, N, N), F32)],
        scratch_shapes=[pltpu.VMEM((H, N, N), F32)],
        compiler_params=_params(("arbitrary",)),
    )(r4, w4, k4, a4, b4, vT)


def _rwkv_scan_bwd(r4, w4, k4, a4, b4, vT, dyT, ck):
    H, T, N = r4.shape
    nC = T // SCAN_CHUNK
    subs = SCAN_CHUNK // SCAN_SUB

    def body(r_ref, w_ref, k_ref, a_ref, b_ref, vT_ref, dyT_ref, ck_ref,
             dr_ref, dw_ref, dk_ref, da_ref, db_ref, dvT_ref, dS_ref, S_ref, hist_ref):
        c = pl.program_id(0)

        @pl.when(c == 0)
        def _():
            dS_ref[...] = jnp.zeros_like(dS_ref)

        lane_id = lax.broadcasted_iota(jnp.int32, (N, SCAN_CHUNK), 1)
        dvT_ref[...] = jnp.zeros_like(dvT_ref)
        for sub in reversed(range(subs)):
            base = sub * SCAN_SUB
            S_ref[...] = ck_ref[sub]

            def fwd_step(s, carry):
                t = base + s
                for h in range(H):
                    S = S_ref[h]
                    hist_ref[s, h] = S
                    row = pl.ds(t, 1)
                    v_col = _pick_col(vT_ref[h], lane_id, t)
                    Sa = jnp.sum(S * a_ref[h, row, :], axis=1, keepdims=True)
                    S_ref[h] = S * w_ref[h, row, :] + Sa * b_ref[h, row, :] + v_col * k_ref[h, row, :]
                return carry

            lax.fori_loop(0, SCAN_SUB, fwd_step, 0)

            def bwd_step(j, carry):
                s = SCAN_SUB - 1 - j
                t = base + s
                for h in range(H):
                    row = pl.ds(t, 1)
                    Sp = hist_ref[s, h]
                    a_row, w_row, b_row = a_ref[h, row, :], w_ref[h, row, :], b_ref[h, row, :]
                    k_row, r_row = k_ref[h, row, :], r_ref[h, row, :]
                    v_col = _pick_col(vT_ref[h], lane_id, t)
                    dy_col = _pick_col(dyT_ref[h], lane_id, t)
                    Sa = jnp.sum(Sp * a_row, axis=1, keepdims=True)
                    St = Sp * w_row + Sa * b_row + v_col * k_row
                    dS = dS_ref[h] + dy_col * r_row
                    dr_ref[h, row, :] = jnp.sum(St * dy_col, axis=0, keepdims=True)
                    dw_ref[h, row, :] = jnp.sum(dS * Sp, axis=0, keepdims=True)
                    dSa = jnp.sum(dS * b_row, axis=1, keepdims=True)
                    db_ref[h, row, :] = jnp.sum(dS * Sa, axis=0, keepdims=True)
                    dv_col = jnp.sum(dS * k_row, axis=1, keepdims=True)
                    dk_ref[h, row, :] = jnp.sum(dS * v_col, axis=0, keepdims=True)
                    da_ref[h, row, :] = jnp.sum(Sp * dSa, axis=0, keepdims=True)
                    dS_ref[h] = dS * w_row + dSa * a_row
                    dvT_ref[h] = jnp.where(lane_id == t, dv_col, dvT_ref[h])
                return carry

            lax.fori_loop(0, SCAN_SUB, bwd_step, 0)

    rev = lambda c: nC - 1 - c
    row_spec = pl.BlockSpec((H, SCAN_CHUNK, N), lambda c: (0, rev(c), 0))
    col_spec = pl.BlockSpec((H, N, SCAN_CHUNK), lambda c: (0, 0, rev(c)))
    ck_spec = pl.BlockSpec((subs, H, N, N), lambda c: (rev(c), 0, 0, 0))
    row_shape = jax.ShapeDtypeStruct((H, T, N), F32)
    return pl.pallas_call(
        body, name="rwkv_scan_bwd", grid=(nC,),
        in_specs=[row_spec] * 5 + [col_spec, col_spec, ck_spec],
        out_specs=[row_spec] * 5 + [col_spec],
        out_shape=[row_shape] * 5 + [jax.ShapeDtypeStruct((H, N, T), F32)],
        scratch_shapes=[pltpu.VMEM((H, N, N), F32), pltpu.VMEM((H, N, N), F32),
                        pltpu.VMEM((SCAN_SUB, H, N, N), F32)],
        compiler_params=_params(("arbitrary",)),
    )(r4, w4, k4, a4, b4, vT, dyT, ck)


def _to_heads(x):
    T = x.shape[0]
    return x.reshape(T, RW_HEADS, RW_HEAD_DIM).transpose(1, 0, 2)


def _to_heads_t(x):
    T = x.shape[0]
    return x.reshape(T, RW_HEADS, RW_HEAD_DIM).transpose(1, 2, 0)


def _from_heads(x):
    return x.transpose(1, 0, 2).reshape(x.shape[1], RW_WIDTH)


def _from_heads_t(x):
    return x.transpose(2, 0, 1).reshape(x.shape[2], RW_WIDTH)


def _alibi_slope(head):
    return float(np.float32(2.0 ** (-8.0 * (head + 1) / ATT_HEADS)))


def _att_masks():
    qi = lax.broadcasted_iota(jnp.int32, (ATT_BLOCK, ATT_BLOCK), 0)
    kj = lax.broadcasted_iota(jnp.int32, (ATT_BLOCK, ATT_BLOCK), 1)
    return qi, kj


NEG = -1e30


def _att_logits(q, k, slope_d, steps, valid):
    s = lax.dot_general(q.astype(BF16), k.astype(BF16), (((1,), (1,)), ((), ())),
                        preferred_element_type=F32) * (ATT_HEAD_DIM ** -0.5)
    return jnp.where(valid, s - slope_d * steps.astype(F32), NEG)


def _att_fwd(p_att, g):
    T = p_att.shape[0]
    d = ATT_GROUP_DILATION[g]
    nbs = T // (ATT_BLOCK * d)
    W = ATT_GROUP_WIDTH
    view = p_att.reshape(T // d, d * 3 * ATT_WIDTH)
    cols = 3 * ATT_WIDTH // W

    def body(q_ref, kc_ref, kp_ref, vc_ref, vp_ref, o_ref, l_ref):
        m = pl.program_id(1)
        qi, kj = _att_masks()
        has_prev = m > 0
        outs, lses = [], []
        for j in range(4):
            sl = slice(j * ATT_HEAD_DIM, (j + 1) * ATT_HEAD_DIM)
            slope_d = _alibi_slope(4 * g + j) * d
            q = q_ref[:, sl]
            lc = _att_logits(q, kc_ref[:, sl], slope_d, qi - kj, kj <= qi)
            lp = _att_logits(q, kp_ref[:, sl], slope_d, qi - kj + ATT_BLOCK, (kj >= qi) & has_prev)
            mx = jnp.maximum(jnp.max(lc, axis=1, keepdims=True), jnp.max(lp, axis=1, keepdims=True))
            ec, ep = jnp.exp(lc - mx), jnp.exp(lp - mx)
            den = jnp.sum(ec, axis=1, keepdims=True) + jnp.sum(ep, axis=1, keepdims=True)
            lse = mx + jnp.log(den)
            pc, pp = jnp.exp(lc - lse), jnp.exp(lp - lse)
            o = (jnp.dot(pc.astype(BF16), vc_ref[:, sl].astype(BF16), preferred_element_type=F32)
                 + jnp.dot(pp.astype(BF16), vp_ref[:, sl].astype(BF16), preferred_element_type=F32))
            outs.append(o)
            lses.append(jnp.broadcast_to(lse, (ATT_BLOCK, ATT_HEAD_DIM)))
        o_ref[...] = jnp.concatenate(outs, axis=1)
        l_ref[...] = jnp.concatenate(lses, axis=1)

    def spec(col0, prev):
        if prev:
            return pl.BlockSpec((ATT_BLOCK, W), lambda r, m: (jnp.maximum(m - 1, 0), r * cols + col0 + g))
        return pl.BlockSpec((ATT_BLOCK, W), lambda r, m: (m, r * cols + col0 + g))

    o_spec = pl.BlockSpec((ATT_BLOCK, W), lambda r, m: (m, r))
    o, l = pl.pallas_call(
        body, name=f"att_fwd_g{g}", grid=(d, nbs),
        in_specs=[spec(0, False), spec(3, False), spec(3, True), spec(6, False), spec(6, True)],
        out_specs=[o_spec, o_spec],
        out_shape=[jax.ShapeDtypeStruct((T // d, d * W), F32)] * 2,
        compiler_params=_params(("parallel", "arbitrary")),
    )(view, view, view, view, view)
    return o.reshape(T, W), l.reshape(T, W)


def _att_bwd(p_att, o, l, do, dl, g):
    T = p_att.shape[0]
    d = ATT_GROUP_DILATION[g]
    nbs = T // (ATT_BLOCK * d)
    W = ATT_GROUP_WIDTH
    view = p_att.reshape(T // d, d * 3 * ATT_WIDTH)
    cols = 3 * ATT_WIDTH // W
    ov, lv, dov, dlv = (z.reshape(T // d, d * W) for z in (o, l, do, dl))
    scale = ATT_HEAD_DIM ** -0.5

    def body(q_ref, k_ref, v_ref, o_ref, l_ref, do_ref, dl_ref,
             qn_ref, on_ref, ln_ref, don_ref, dln_ref, dq_ref, dk_ref, dv_ref, carry_ref):
        m = pl.program_id(1)
        qi, kj = _att_masks()
        has_next = m < nbs - 1

        @pl.when(m == 0)
        def _():
            carry_ref[...] = jnp.zeros_like(carry_ref)

        dqs, dks, dvs, carries = [], [], [], []
        for j in range(4):
            sl = slice(j * ATT_HEAD_DIM, (j + 1) * ATT_HEAD_DIM)
            slope_d = _alibi_slope(4 * g + j) * d
            k, v = k_ref[:, sl], v_ref[:, sl]
            kb, vb = k.astype(BF16), v.astype(BF16)

            def side(q, o_, lse, do_, dlse, steps, valid):
                lg = _att_logits(q, k, slope_d, steps, valid)
                p = jnp.exp(lg - lse)
                dp = lax.dot_general(do_.astype(BF16), vb, (((1,), (1,)), ((), ())), preferred_element_type=F32)
                dsum = jnp.sum(do_ * o_, axis=1, keepdims=True)
                ds = p * (dp - dsum + dlse)
                dv_ = jnp.dot(p.T.astype(BF16), do_.astype(BF16), preferred_element_type=F32)
                dk_ = jnp.dot(ds.T.astype(BF16), q.astype(BF16), preferred_element_type=F32) * scale
                dq_ = jnp.dot(ds.astype(BF16), kb, preferred_element_type=F32) * scale
                return dq_, dk_, dv_

            dq_c, dk_c, dv_c = side(q_ref[:, sl], o_ref[:, sl], l_ref[:, sl][:, :1], do_ref[:, sl],
                                    dl_ref[:, sl][:, :1], qi - kj, kj <= qi)
            dq_n, dk_n, dv_n = side(qn_ref[:, sl], on_ref[:, sl], ln_ref[:, sl][:, :1], don_ref[:, sl],
                                    dln_ref[:, sl][:, :1], qi - kj + ATT_BLOCK, (kj >= qi) & has_next)
            dqs.append(dq_c)
            carries.append(dq_n)
            dks.append(dk_c + dk_n)
            dvs.append(dv_c + dv_n)
        dq_ref[...] = jnp.concatenate(dqs, axis=1) + carry_ref[...]
        carry_ref[...] = jnp.concatenate(carries, axis=1)
        dk_ref[...] = jnp.concatenate(dks, axis=1)
        dv_ref[...] = jnp.concatenate(dvs, axis=1)

    nxt = lambda m: jnp.minimum(m + 1, nbs - 1)
    cur_p = lambda col0: pl.BlockSpec((ATT_BLOCK, W), lambda r, m: (m, r * cols + col0 + g))
    nxt_p = lambda col0: pl.BlockSpec((ATT_BLOCK, W), lambda r, m: (nxt(m), r * cols + col0 + g))
    cur_o = pl.BlockSpec((ATT_BLOCK, W), lambda r, m: (m, r))
    nxt_o = pl.BlockSpec((ATT_BLOCK, W), lambda r, m: (nxt(m), r))
    dq, dk, dv = pl.pallas_call(
        body, name=f"att_bwd_g{g}", grid=(d, nbs),
        in_specs=[cur_p(0), cur_p(3), cur_p(6), cur_o, cur_o, cur_o, cur_o,
                  nxt_p(0), nxt_o, nxt_o, nxt_o, nxt_o],
        out_specs=[cur_o, cur_o, cur_o],
        out_shape=[jax.ShapeDtypeStruct((T // d, d * W), F32)] * 3,
        scratch_shapes=[pltpu.VMEM((ATT_BLOCK, W), F32)],
        compiler_params=_params(("parallel", "arbitrary")),
    )(view, view, view, ov, lv, dov, dlv, view, ov, lv, dov, dlv)
    return dq.reshape(T, W), dk.reshape(T, W), dv.reshape(T, W)


RKV = 3 * RW_WIDTH
WA = 128
XG = 160
RW_COLS = RKV + WA + XG


def _local_step(x, p, W, target):
    T = x.shape[0]
    tT = 256
    bd512 = _block_diag_ones(RW_WIDTH, RW_HEAD_DIM)
    bd256 = _block_diag_ones(ATT_GROUP_WIDTH, ATT_HEAD_DIM)
    G = {}

    w_in = W["w_in"]
    w_rkv, w_wa, w_xg, w_att = (w_in[:, :RKV], w_in[:, RKV:RKV + WA], w_in[:, RKV + WA:RW_COLS],
                                w_in[:, RW_COLS:])
    mu = W["rw_mu"]
    mu_rkv, mu_wa, mu_xg = mu[:, :RKV], mu[:, RKV:RKV + WA], mu[:, RKV + WA:]
    zpad = jnp.zeros((64, RW_WIDTH), F32)
    w_up_pad = jnp.concatenate([W["rw_w_up"], zpad], axis=0)
    a_up_pad = jnp.concatenate([zpad, W["rw_a_up"]], axis=0)
    r_k = W["rw_r_k"].reshape(1, RW_WIDTH)

    (h,) = _rowwise("norm_mix", lambda i, n, r, pv, nx, c: [_rms_fwd(r[0], c[0])], T, tT,
                    rows=[x], consts=[W["g_mix"]], outs=[("row", D_MODEL, BF16)])
    p_rkv = _mm("proj_rkv", h, w_rkv, "nn")
    p_wa = _mm("proj_wa", h, w_wa, "nn")
    p_xg = _mm("proj_xg", h, w_xg, "nn")
    p_att = _mm("proj_att", h, w_att, "nn", tn=768)
    z_gate = _mm("proj_gate", h, W["w_gate"], "nn")

    def rw_pre_core(i, rows, prevs, consts):
        prkv, pwa, pxg = rows[:3]
        (mrkv, mwa, mxg, w0, a0, k_k, k_a, wup, aup, gup, bd) = consts[:11]
        m_rkv = prkv + (_shift_down(prkv, prevs[0], i, 1) - prkv) * mrkv
        m_wa = pwa + (_shift_down(pwa, prevs[1], i, 1) - pwa) * mwa
        m_xg = pxg + (_shift_down(pxg, prevs[2], i, 1) - pxg) * mxg
        r, k, v = m_rkv[:, :RW_WIDTH], m_rkv[:, RW_WIDTH:2 * RW_WIDTH], m_rkv[:, 2 * RW_WIDTH:]
        tw = jnp.tanh(m_wa)
        lw = w0 + jnp.dot(tw.astype(BF16), wup.astype(BF16), preferred_element_type=F32)
        wlog = -_softplus(-lw) - 0.5
        decay = jnp.exp(-jnp.exp(wlog))
        a = _sigmoid(a0 + jnp.dot(m_wa.astype(BF16), aup.astype(BF16), preferred_element_type=F32))
        sg = _sigmoid(m_xg)
        gate = jnp.dot(sg.astype(BF16), gup.astype(BF16), preferred_element_type=F32)
        kkp = k * k_k
        nrm = jnp.sqrt(_segsum(kkp * kkp, bd))
        nrm_c = jnp.maximum(nrm, 1e-12)
        kk = kkp / nrm_c
        k2 = k * (1.0 + (a - 1.0) * k_a)
        return dict(r=r, k=k, v=v, tw=tw, lw=lw, wlog=wlog, decay=decay, a=a, sg=sg, gate=gate, kkp=kkp,
                    nrm=nrm, nrm_c=nrm_c, kk=kk, k2=k2, m_rkv=m_rkv, m_wa=m_wa, m_xg=m_xg)

    pre_consts = [mu_rkv, mu_wa, mu_xg, W["rw_w0"], W["rw_a0"], W["rw_k_k"], W["rw_k_a"],
                  w_up_pad, a_up_pad, W["rw_g_up"], bd512]

    def rw_pre(i, n, rows, prevs, nexts, consts):
        q = rw_pre_core(i, rows, prevs, consts)
        return [q["r"], q["decay"], q["k2"], q["v"], -q["kk"], q["kk"] * q["a"], q["gate"]]

    r_s, w_s, k_s, v_s, a_s, b_s, gate_s = _rowwise(
        "rwkv_pre", rw_pre, T, tT, rows=[p_rkv, p_wa, p_xg], prevs=[p_rkv, p_wa, p_xg], consts=pre_consts,
        outs=[("row", RW_WIDTH, F32)] * 7)
    r4, w4, k4, a4, b4, vT = (_to_heads(r_s), _to_heads(w_s), _to_heads(k_s), _to_heads(a_s),
                              _to_heads(b_s), _to_heads_t(v_s))
    yT, ck = _rwkv_scan_fwd(r4, w4, k4, a4, b4, vT)
    y_scan = _from_heads_t(yT)

    def rw_post_core(rows, consts):
        y, r, k2, v, gate = rows[:5]
        ln_g, ln_b, rk, bd = consts[:4]
        mean = _segsum(y, bd) * (1.0 / RW_HEAD_DIM)
        yc = y - mean
        var = _segsum(yc * yc, bd) * (1.0 / RW_HEAD_DIM)
        rstd = lax.rsqrt(var + RW_LN_EPS)
        yn = yc * rstd
        s = _segsum(r * k2 * rk, bd)
        return dict(yn=yn, rstd=rstd, s=s, pre=yn * ln_g + ln_b + s * v)

    post_consts = [W["rw_ln_g"], W["rw_ln_b"], r_k, bd512]
    (y_a,) = _rowwise("rwkv_post", lambda i, n, r, pv, nx, c: [rw_post_core(r, c)["pre"] * r[4]], T, tT,
                      rows=[y_scan, r_s, k_s, v_s, gate_s], consts=post_consts, outs=[("row", RW_WIDTH, BF16)])

    att = [_att_fwd(p_att, g) for g in range(3)]

    def comb_weights(ls):
        mx = jnp.maximum(jnp.maximum(ls[0], ls[1]), ls[2])
        es = [jnp.exp(l - mx) for l in ls]
        den = es[0] + es[1] + es[2]
        return [e / den for e in es]

    def att_comb(i, n, rows, pv, nx, c):
        wts = comb_weights(rows[3:6])
        return [wts[0] * rows[0] + wts[1] * rows[1] + wts[2] * rows[2]]

    (y_b,) = _rowwise("att_combine", att_comb, T, tT, rows=[att[0][0], att[1][0], att[2][0], att[0][1], att[1][1],
                                                            att[2][1]], outs=[("row", ATT_GROUP_WIDTH, BF16)])

    br_a = _mm("branch_a", y_a, W["w_branch_a"], "nn")
    br_b = _mm("branch_b", y_b, W["w_branch_b"], "nn")

    def merge(i, n, rows, pv, nx, c):
        gates = _sigmoid(rows[0] + c[0])
        return [gates[:, :D_MODEL] * rows[1] + gates[:, D_MODEL:] * rows[2]]

    (merged,) = _rowwise("merge", merge, T, tT, rows=[z_gate, br_a, br_b], consts=[W["b_gate"]],
                         outs=[("row", D_MODEL, BF16)])
    x1 = _mm("mix_out", merged, W["w_out"], "nn", add=x)

    (h2,) = _rowwise("norm_ffn", lambda i, n, r, pv, nx, c: [_rms_fwd(r[0], c[0])], T, tT,
                     rows=[x1], consts=[W["g_ffn"]], outs=[("row", D_MODEL, BF16)])
    u = _mm("ffn_up", h2, W["w_up"], "nn")

    def conv_core(i, rows, prevs, consts):
        uu, cw, cb = rows[0], consts[0], consts[1]
        u1 = _shift_down(uu, prevs[0], i, 1)
        u2 = _shift_down(uu, prevs[0], i, 2)
        uc = cb + cw[0:1] * uu + cw[1:2] * u1 + cw[2:3] * u2
        return uc[:, :D_FF], uc[:, D_FF:], u1, u2

    def glu(i, n, rows, prevs, nx, consts):
        gate, val, _, _ = conv_core(i, rows, prevs, consts)
        return [_gelu(gate) * val]

    tF = 128
    (act,) = _rowwise("conv_glu", glu, T, tF, rows=[u], prevs=[u], consts=[W["conv_w"], W["conv_b"]],
                      outs=[("row", D_FF, BF16)])
    x2 = _mm("ffn_down", act, W["w_down"], "nn", add=x1)

    (h3,) = _rowwise("norm_ple", lambda i, n, r, pv, nx, c: [_rms_fwd(r[0], c[0])], T, tT,
                     rows=[x2], consts=[W["g_ple"]], outs=[("row", D_MODEL, BF16)])
    z_ple = _mm("ple_gate", h3, W["w_ple_gate"], "nn")
    e_ple = _mm("ple_emb", p, W["w_ple"], "nn")

    def head(i, n, rows, pv, nx, consts):
        x2_, z, e, tgt = rows
        pg = _sigmoid(z)
        x3 = x2_ + pg * e
        y = _rms_fwd(x3, consts[0])
        err = y - tgt
        loss = 0.5 * jnp.sum(jnp.sum(err * err, axis=1, keepdims=True) * (1.0 / D_MODEL), axis=0, keepdims=True)
        dy = err * (1.0 / D_MODEL)
        dx3, dgf = _rms_bwd(x3, consts[0], dy)
        return [dx3, dx3 * pg, dx3 * e * pg * (1.0 - pg), jnp.broadcast_to(loss, (1, LANES)), _colsum(dgf)]

    dx3, de, dz, loss_acc, G["g_final"] = _rowwise(
        "loss_head", head, T, tT, rows=[x2, z_ple, e_ple, target], consts=[W["g_final"].reshape(1, D_MODEL)],
        outs=[("row", D_MODEL, F32), ("row", D_MODEL, BF16), ("row", D_MODEL, BF16), ("acc", (1, LANES)),
              ("acc", (1, D_MODEL))])
    G["w_ple"] = _mm("d_w_ple", p, de, "tn")
    G["w_ple_gate"] = _mm("d_w_ple_gate", h3, dz, "tn")
    dh3 = _mm("d_h3", dz, W["w_ple_gate"], "nt")

    def norm_bwd(i, n, rows, pv, nx, consts):
        dx, dg = _rms_bwd(rows[0], consts[0], rows[1])
        return [rows[2] + dx, _colsum(dg)]

    dx2, G["g_ple"] = _rowwise("d_norm_ple", norm_bwd, T, tT, rows=[x2, dh3, dx3], consts=[W["g_ple"]],
                               outs=[("row", D_MODEL, F32), ("acc", (1, D_MODEL))])

    dact = _mm("d_act", dx2, W["w_down"], "nt")
    G["w_down"] = _mm("d_w_down", act, dx2, "tn")

    def glu_bwd(i, n, rows, prevs, nx, consts):
        gate, val, u1, u2 = conv_core(i, rows, prevs, consts)
        da = rows[1]
        duc = jnp.concatenate([da * val * _gelu_grad(gate), da * _gelu(gate)], axis=1)
        dcw = jnp.concatenate([_colsum(duc * rows[0]), _colsum(duc * u1), _colsum(duc * u2)], axis=0)
        return [duc, _colsum(duc), dcw]

    duc, G["conv_b"], G["conv_w"] = _rowwise(
        "d_conv_glu", glu_bwd, T, tF, rows=[u, dact], prevs=[u], consts=[W["conv_w"], W["conv_b"]],
        outs=[("row", 2 * D_FF, F32), ("acc", (1, 2 * D_FF)), ("acc", (3, 2 * D_FF))])

    def conv_bwd(i, n, rows, pv, nexts, consts):
        cw = consts[0]
        return [cw[0:1] * rows[0] + cw[1:2] * _shift_up(rows[0], nexts[0], i, n, 1)
                + cw[2:3] * _shift_up(rows[0], nexts[0], i, n, 2)]

    (du,) = _rowwise("d_conv", conv_bwd, T, tF, rows=[duc], nexts=[duc], consts=[W["conv_w"]],
                     outs=[("row", 2 * D_FF, BF16)])
    G["w_up"] = _mm("d_w_up", h2, du, "tn")
    dh2 = _mm("d_h2", du, W["w_up"], "nt")
    dx1, G["g_ffn"] = _rowwise("d_norm_ffn", norm_bwd, T, tT, rows=[x1, dh2, dx2], consts=[W["g_ffn"]],
                               outs=[("row", D_MODEL, F32), ("acc", (1, D_MODEL))])

    dmerged = _mm("d_merged", dx1, W["w_out"], "nt")
    G["w_out"] = _mm("d_w_out", merged, dx1, "tn")

    def merge_bwd(i, n, rows, pv, nx, consts):
        z, a_, b_, dm = rows
        gates = _sigmoid(z + consts[0])
        ga, gb = gates[:, :D_MODEL], gates[:, D_MODEL:]
        dz_ = jnp.concatenate([dm * a_ * ga * (1.0 - ga), dm * b_ * gb * (1.0 - gb)], axis=1)
        return [dm * ga, dm * gb, dz_, _colsum(dz_)]

    d_br_a, d_br_b, dz_gate, G["b_gate"] = _rowwise(
        "d_merge", merge_bwd, T, tT, rows=[z_gate, br_a, br_b, dmerged], consts=[W["b_gate"]],
        outs=[("row", D_MODEL, BF16), ("row", D_MODEL, BF16), ("row", 2 * D_MODEL, BF16), ("acc", (1, 2 * D_MODEL))])
    G["w_branch_a"] = _mm("d_w_branch_a", y_a, d_br_a, "tn")
    G["w_branch_b"] = _mm("d_w_branch_b", y_b, d_br_b, "tn")
    G["w_gate"] = _mm("d_w_gate", h, dz_gate, "tn")
    dy_a = _mm("d_y_a", d_br_a, W["w_branch_a"], "nt")
    dy_b = _mm("d_y_b", d_br_b, W["w_branch_b"], "nt")

    def att_comb_bwd(i, n, rows, pv, nx, consts):
        os_, ls, dy = rows[0:3], rows[3:6], rows[6]
        wts = comb_weights(ls)
        dws = [_segsum(dy * o_, consts[0]) for o_ in os_]
        mix = wts[0] * dws[0] + wts[1] * dws[1] + wts[2] * dws[2]
        return [wts[g_] * dy for g_ in range(3)] + [wts[g_] * (dws[g_] - mix) for g_ in range(3)]

    comb = _rowwise("d_att_combine", att_comb_bwd, T, tT,
                    rows=[att[0][0], att[1][0], att[2][0], att[0][1], att[1][1], att[2][1], dy_b], consts=[bd256],
                    outs=[("row", ATT_GROUP_WIDTH, F32)] * 6)
    dqkv = [_att_bwd(p_att, att[g][0], att[g][1], comb[g], comb[3 + g], g) for g in range(3)]
    dp_att = jnp.concatenate([dqkv[g][part] for part in range(3) for g in range(3)], axis=1).astype(BF16)

    def rw_post_bwd(i, n, rows, pv, nx, consts):
        y, r, k2, v, gate, dya = rows
        ln_g, ln_b, rk, bd = consts
        q = rw_post_core(rows, consts)
        dpre = dya * gate
        dgate = dya * q["pre"]
        dyn = dpre * ln_g
        inv = 1.0 / RW_HEAD_DIM
        dy_scan = q["rstd"] * (dyn - _segsum(dyn, bd) * inv - q["yn"] * (_segsum(dyn * q["yn"], bd) * inv))
        ds = _segsum(dpre * v, bd)
        return [dy_scan, dgate, ds * k2 * rk, ds * r * rk, dpre * q["s"],
                _colsum(dpre * q["yn"]), _colsum(dpre), _colsum(ds * r * k2)]

    dy_scan, dgate, dr_b, dk2_b, dv_b, G["rw_ln_g"], G["rw_ln_b"], d_rk = _rowwise(
        "d_rwkv_post", rw_post_bwd, T, tT, rows=[y_scan, r_s, k_s, v_s, gate_s, dy_a], consts=post_consts,
        outs=[("row", RW_WIDTH, F32)] * 5 + [("acc", (1, RW_WIDTH))] * 3)
    G["rw_r_k"] = d_rk.reshape(RW_HEADS, RW_HEAD_DIM)

    dr4, dw4, dk4, da4, db4, dvT = _rwkv_scan_bwd(r4, w4, k4, a4, b4, vT, _to_heads_t(dy_scan), ck)
    dr_s, dw_s, dk_s, da_s, db_s, dv_s = (_from_heads(dr4), _from_heads(dw4), _from_heads(dk4),
                                          _from_heads(da4), _from_heads(db4), _from_heads_t(dvT))

    def rw_pre_bwd(i, n, rows, prevs, nx, consts):
        q = rw_pre_core(i, rows, prevs, consts)
        (mrkv, mwa, mxg, w0, a0, k_k, k_a, wup, aup, gup, bd) = consts
        dr, ddecay, dk2, dv, dav, dbv, dgate_ = rows[3:10]
        dr = dr + rows[10]
        dk2 = dk2 + rows[11]
        dv = dv + rows[12]
        a, k, kk = q["a"], q["k"], q["kk"]
        dk = dk2 * (1.0 + (a - 1.0) * k_a)
        da = dk2 * k * k_a + dbv * kk
        dkk = dbv * a - dav
        live = q["nrm"] > 1e-12
        dkkp = jnp.where(live, dkk - kk * _segsum(dkk * kk, bd), dkk) / q["nrm_c"]
        dk = dk + dkkp * k_k
        dlw = ddecay * q["decay"] * (-jnp.exp(q["wlog"])) * _sigmoid(-q["lw"])
        dla = da * a * (1.0 - a)
        nt = (((1,), (1,)), ((), ()))
        dtw = lax.dot_general(dlw.astype(BF16), wup.astype(BF16), nt, preferred_element_type=F32)
        dxa = lax.dot_general(dla.astype(BF16), aup.astype(BF16), nt, preferred_element_type=F32)
        dm_wa = dtw * (1.0 - q["tw"] * q["tw"]) + dxa
        dsg = lax.dot_general(dgate_.astype(BF16), gup.astype(BF16), nt, preferred_element_type=F32)
        dm_xg = dsg * q["sg"] * (1.0 - q["sg"])
        dm_rkv = jnp.concatenate([dr, dk, dv], axis=1)
        prkv, pwa, pxg = rows[:3]
        dmu = jnp.concatenate([_colsum(dm_rkv * (_shift_down(prkv, prevs[0], i, 1) - prkv)),
                               _colsum(dm_wa * (_shift_down(pwa, prevs[1], i, 1) - pwa)),
                               _colsum(dm_xg * (_shift_down(pxg, prevs[2], i, 1) - pxg))], axis=1)
        return [dm_rkv, dm_wa, dm_xg, dlw, dla, q["tw"], q["m_wa"], q["sg"], dmu,
                _colsum(dlw), _colsum(dla), _colsum(dkkp * k), _colsum(dk2 * k * (a - 1.0))]

    (dm_rkv, dm_wa, dm_xg, dlw, dla, tw_s, mwa_s, sg_s, G["rw_mu"], G["rw_w0"], G["rw_a0"], G["rw_k_k"],
     G["rw_k_a"]) = _rowwise(
        "d_rwkv_pre", rw_pre_bwd, T, tT,
        rows=[p_rkv, p_wa, p_xg, dr_s, dw_s, dk_s, dv_s, da_s, db_s, dgate, dr_b, dk2_b, dv_b],
        prevs=[p_rkv, p_wa, p_xg], consts=pre_consts,
        outs=[("row", RKV, F32), ("row", WA, F32), ("row", XG, F32), ("row", RW_WIDTH, BF16),
              ("row", RW_WIDTH, BF16), ("row", WA, BF16), ("row", WA, BF16), ("row", XG, BF16),
              ("acc", (1, RW_COLS))] + [("acc", (1, RW_WIDTH))] * 4)
    G["rw_w_up"] = _mm("d_rw_w_up", tw_s, dlw, "tn")[:64]
    G["rw_a_up"] = _mm("d_rw_a_up", mwa_s, dla, "tn")[64:]
    G["rw_g_up"] = _mm("d_rw_g_up", sg_s, dgate, "tn")

    def shift_bwd(i, n, rows, pv, nexts, consts):
        return [rows[j] * (1.0 - consts[j]) + _shift_up(rows[j], nexts[j], i, n, 1) * consts[j] for j in range(3)]

    dp_rkv, dp_wa, dp_xg = _rowwise(
        "d_token_shift", shift_bwd, T, tT, rows=[dm_rkv, dm_wa, dm_xg], nexts=[dm_rkv, dm_wa, dm_xg],
        consts=[mu_rkv, mu_wa, mu_xg], outs=[("row", RKV, BF16), ("row", WA, BF16), ("row", XG, BF16)])

    G["w_in"] = jnp.concatenate([_mm("d_w_rkv", h, dp_rkv, "tn"), _mm("d_w_wa", h, dp_wa, "tn"),
                                 _mm("d_w_xg", h, dp_xg, "tn"), _mm("d_w_att", h, dp_att, "tn", tn=768)], axis=1)
    dh = _mm("d_h_gate", dz_gate, W["w_gate"], "nt")
    dh = _mm("d_h_rkv", dp_rkv, w_rkv, "nt", add=dh)
    dh = _mm("d_h_wa", dp_wa, w_wa, "nt", add=dh)
    dh = _mm("d_h_xg", dp_xg, w_xg, "nt", add=dh)
    dh = _mm("d_h_att", dp_att, w_att, "nt", add=dh)
    dx, G["g_mix"] = _rowwise("d_norm_mix", norm_bwd, T, tT, rows=[x, dh, dx1], consts=[W["g_mix"]],
                              outs=[("row", D_MODEL, F32), ("acc", (1, D_MODEL))])
    return loss_acc[:, :1], dx, G


HBM_SPEC = pl.BlockSpec(memory_space=pltpu.HBM)


def _place():
    x, y, c = lax.axis_index("x"), lax.axis_index("y"), lax.axis_index("c")
    return x, y, c, [(1 - x, y), (x, 1 - y), (1 - x, 1 - y)]


def _remote(src, dst, send_sems, recv_sems, k, to):
    return pltpu.make_async_remote_copy(src_ref=src, dst_ref=dst, send_sem=send_sems.at[k], recv_sem=recv_sems.at[k],
                                        device_id=to, device_id_type=MESH)


def _gather_chips(wp):
    R = wp.shape[0]
    half = R // 2

    def body(w_ref, out_ref, send_sems, recv_sems, local_sem):
        x, y, c, chips = _place()
        me = 2 * x + y
        mine = pl.ds(pl.multiple_of(c * half, SUBLANES), half)
        other = pl.ds(pl.multiple_of((1 - c) * half, SUBLANES), half)
        local = pltpu.make_async_copy(w_ref, out_ref.at[me], local_sem)
        local.start()
        sends = [_remote(w_ref.at[mine], out_ref.at[me, mine], send_sems, recv_sems, j, (px, py, c))
                 for j, (px, py) in enumerate(chips)]
        for cp in sends:
            cp.start()
        passed = []
        for j, (px, py) in enumerate(chips):
            landed = out_ref.at[2 * px + py, mine]
            _remote(landed, landed, send_sems, recv_sems, j, (px, py, c)).wait_recv()
            cp = _remote(landed, landed, send_sems, recv_sems, 3 + j, (x, y, 1 - c))
            cp.start()
            passed.append(cp)
        for j, (px, py) in enumerate(chips):
            landed = out_ref.at[2 * px + py, other]
            _remote(landed, landed, send_sems, recv_sems, 3 + j, (x, y, 1 - c)).wait_recv()
        for cp in sends + passed:
            cp.wait_send()
        local.wait()

    return pl.pallas_call(
        body, name="gather_weights", in_specs=[HBM_SPEC], out_specs=HBM_SPEC,
        out_shape=jax.ShapeDtypeStruct((N_CHIPS, R, LANES), wp.dtype),
        scratch_shapes=[pltpu.SemaphoreType.DMA((6,)), pltpu.SemaphoreType.DMA((6,)), pltpu.SemaphoreType.DMA],
    )(wp)


def _swap_halves(g):
    R = g.shape[1]
    half = R // 2

    def body(g_ref, out_ref, send_sems, recv_sems):
        x, y, c, _ = _place()
        theirs = pl.ds(pl.multiple_of((1 - c) * half, SUBLANES), half)
        cp = _remote(g_ref.at[:, theirs, :], out_ref, send_sems, recv_sems, 0, (x, y, 1 - c))
        cp.start()
        cp.wait()

    return pl.pallas_call(
        body, name="swap_halves", in_specs=[HBM_SPEC], out_specs=HBM_SPEC,
        out_shape=jax.ShapeDtypeStruct((N_CHIPS, half, LANES), g.dtype),
        scratch_shapes=[pltpu.SemaphoreType.DMA((1,)), pltpu.SemaphoreType.DMA((1,))],
    )(g)


def _scatter_chips(part):
    H = part.shape[1]

    def body(p_ref, out_ref, send_sems, recv_sems, local_sem):
        x, y, c, chips = _place()
        me = 2 * x + y
        local = pltpu.make_async_copy(p_ref.at[me], out_ref.at[me], local_sem)
        local.start()
        sends = [_remote(p_ref.at[2 * px + py], out_ref.at[me], send_sems, recv_sems, j, (px, py, c))
                 for j, (px, py) in enumerate(chips)]
        for cp in sends:
            cp.start()
        for j, (px, py) in enumerate(chips):
            landed = out_ref.at[2 * px + py]
            _remote(landed, landed, send_sems, recv_sems, j, (px, py, c)).wait_recv()
        for cp in sends:
            cp.wait_send()
        local.wait()

    return pl.pallas_call(
        body, name="scatter_grads", in_specs=[HBM_SPEC], out_specs=HBM_SPEC,
        out_shape=jax.ShapeDtypeStruct((N_CHIPS, H, LANES), part.dtype),
        scratch_shapes=[pltpu.SemaphoreType.DMA((3,)), pltpu.SemaphoreType.DMA((3,)), pltpu.SemaphoreType.DMA],
    )(part)


def _join_halves(red):
    H = red.shape[0]

    def body(r_ref, out_ref, send_sems, recv_sems, local_sem):
        x, y, c, _ = _place()
        mine = pl.ds(pl.multiple_of(c * H, SUBLANES), H)
        local = pltpu.make_async_copy(r_ref, out_ref.at[mine], local_sem)
        local.start()
        cp = _remote(r_ref, out_ref.at[mine], send_sems, recv_sems, 0, (x, y, 1 - c))
        cp.start()
        cp.wait()
        local.wait()

    return pl.pallas_call(
        body, name="join_halves", in_specs=[HBM_SPEC], out_specs=HBM_SPEC,
        out_shape=jax.ShapeDtypeStruct((2 * H, LANES), red.dtype),
        scratch_shapes=[pltpu.SemaphoreType.DMA((1,)), pltpu.SemaphoreType.DMA((1,)), pltpu.SemaphoreType.DMA],
    )(red)


def _gather_all(vec):
    R = vec.shape[0]

    def body(v_ref, out_ref, send_sems, recv_sems, local_sem):
        x, y, c, _ = _place()
        me = 4 * x + 2 * y + c
        local = pltpu.make_async_copy(v_ref, out_ref.at[me], local_sem)
        local.start()
        peers = [(x ^ (k >> 2), y ^ ((k >> 1) & 1), c ^ (k & 1)) for k in range(1, N_DEV)]
        sends = [_remote(v_ref, out_ref.at[me], send_sems, recv_sems, k, to) for k, to in enumerate(peers)]
        for cp in sends:
            cp.start()
        for k, (px, py, pc) in enumerate(peers):
            landed = out_ref.at[4 * px + 2 * py + pc]
            _remote(landed, landed, send_sems, recv_sems, k, (px, py, pc)).wait_recv()
        for cp in sends:
            cp.wait_send()
        local.wait()

    return pl.pallas_call(
        body, name="gather_small", in_specs=[HBM_SPEC], out_specs=HBM_SPEC,
        out_shape=jax.ShapeDtypeStruct((N_DEV, R, LANES), vec.dtype),
        scratch_shapes=[pltpu.SemaphoreType.DMA((7,)), pltpu.SemaphoreType.DMA((7,)), pltpu.SemaphoreType.DMA],
    )(vec)


def _sum_leading(name, parts, tr=1024):
    n, R, _ = parts.shape
    tr = math.gcd(R, tr)

    def body(p_ref, o_ref):
        acc = p_ref[0]
        for k in range(1, n):
            acc = acc + p_ref[k]
        o_ref[...] = acc

    return pl.pallas_call(
        body, name=name, grid=(R // tr,),
        in_specs=[pl.BlockSpec((n, tr, LANES), lambda i: (0, i, 0))],
        out_specs=pl.BlockSpec((tr, LANES), lambda i: (i, 0)),
        out_shape=jax.ShapeDtypeStruct((R, LANES), F32),
        compiler_params=_params(("parallel",)),
    )(parts)


def _add_pairs(name, a, b, tr=512):
    n, R, _ = a.shape
    tr = math.gcd(R, tr)

    def body(a_ref, b_ref, o_ref):
        o_ref[...] = a_ref[...] + b_ref[...]

    spec = pl.BlockSpec((n, tr, LANES), lambda i: (0, i, 0))
    return pl.pallas_call(
        body, name=name, grid=(R // tr,), in_specs=[spec, spec], out_specs=spec,
        out_shape=jax.ShapeDtypeStruct(a.shape, F32), compiler_params=_params(("parallel",)),
    )(a, b)


def _adamw_math(w, g, m, v):
    m = ADAM_B1 * m + (1.0 - ADAM_B1) * g
    v = ADAM_B2 * v + (1.0 - ADAM_B2) * (g * g)
    m_hat = m / (1.0 - ADAM_B1 ** ADAM_STEP)
    v_hat = v / (1.0 - ADAM_B2 ** ADAM_STEP)
    delta = -ADAM_LR * (m_hat / (jnp.sqrt(v_hat) + ADAM_EPS) + ADAM_WD * w)
    return delta, m, v


def _adamw(name, w, g, m, v):
    R, C = w.shape
    tr = R
    if R % SUBLANES == 0:
        for cand in range(SUBLANES, min(R, 256) + 1, SUBLANES):
            if R % cand == 0:
                tr = cand

    def body(w_ref, g_ref, m_ref, v_ref, d_ref, nm_ref, nv_ref):
        d, nm, nv = _adamw_math(w_ref[...], g_ref[...], m_ref[...], v_ref[...])
        d_ref[...] = d
        nm_ref[...] = nm
        nv_ref[...] = nv

    spec = pl.BlockSpec((tr, C), lambda i: (i, 0))
    shape = jax.ShapeDtypeStruct((R, C), F32)
    return pl.pallas_call(
        body, name=name, grid=(R // tr,), in_specs=[spec] * 4, out_specs=[spec] * 3, out_shape=[shape] * 3,
        compiler_params=_params(("parallel",)),
    )(w, g, m, v)


def _adamw_small(parts, w, m, v):
    n = parts.shape[0]

    def body(p_ref, w_ref, m_ref, v_ref, g_ref, d_ref, nm_ref, nv_ref):
        g = p_ref[0]
        for k in range(1, n):
            g = g + p_ref[k]
        d, nm, nv = _adamw_math(w_ref[...], g, m_ref[...], v_ref[...])
        g_ref[...] = g
        d_ref[...] = d
        nm_ref[...] = nm
        nv_ref[...] = nv

    shape = jax.ShapeDtypeStruct(w.shape, F32)
    return pl.pallas_call(body, name="adamw_small", out_shape=[shape] * 4, compiler_params=_params())(parts, w, m, v)


WEIGHTS = ['g_mix', 'w_in', 'rw_mu', 'rw_w0', 'rw_w_up', 'rw_a0', 'rw_a_up', 'rw_g_up', 'rw_k_k', 'rw_k_a',
           'rw_r_k', 'rw_ln_g', 'rw_ln_b', 'w_branch_a', 'w_branch_b', 'w_gate', 'b_gate', 'w_out', 'g_ffn', 'w_up',
           'conv_w', 'conv_b', 'w_down', 'g_ple', 'w_ple_gate', 'w_ple', 'g_final']
ARG_NAMES = (['x', 'p'] + WEIGHTS + ['loss_target'] + ['m_' + n for n in WEIGHTS] + ['v_' + n for n in WEIGHTS])
SHARDED = {'w_in': 1, 'rw_w_up': 1, 'rw_a_up': 1, 'rw_g_up': 1, 'w_branch_a': 1, 'w_branch_b': 1, 'w_gate': 1,
           'w_out': 0, 'w_up': 1, 'conv_w': 1, 'w_down': 0, 'w_ple_gate': 0, 'w_ple': 1}
SMALL = [n for n in WEIGHTS if n not in SHARDED]
PACK_ALIGN = 2 * SUBLANES * LANES * 8


def _pack_rows(flat_parts, lead):
    flat = jnp.concatenate(flat_parts, axis=1)
    n = flat.shape[1]
    padded = -(-n // PACK_ALIGN) * PACK_ALIGN
    flat = jnp.pad(flat, ((0, 0), (0, padded - n)))
    return flat.reshape(lead, padded // LANES, LANES)


def _full_from_shards(stack, shape, axis):
    s = stack.reshape((N_CHIPS,) + shape)
    if axis == 0:
        return s.reshape((N_CHIPS * shape[0], shape[1]))
    return s.transpose(1, 0, 2).reshape(shape[0], N_CHIPS * shape[1])


def _shards_from_full(full, axis):
    R, C = full.shape
    if axis == 0:
        return full.reshape(N_CHIPS, (R // N_CHIPS) * C)
    return full.reshape(R, N_CHIPS, C // N_CHIPS).transpose(1, 0, 2).reshape(N_CHIPS, R * (C // N_CHIPS))


def kernel(x, p, g_mix, w_in, rw_mu, rw_w0, rw_w_up, rw_a0, rw_a_up, rw_g_up, rw_k_k, rw_k_a, rw_r_k, rw_ln_g, rw_ln_b, w_branch_a, w_branch_b, w_gate, b_gate, w_out, g_ffn, w_up, conv_w, conv_b, w_down, g_ple, w_ple_gate, w_ple, g_final, loss_target, m_g_mix, m_w_in, m_rw_mu, m_rw_w0, m_rw_w_up, m_rw_a0, m_rw_a_up, m_rw_g_up, m_rw_k_k, m_rw_k_a, m_rw_r_k, m_rw_ln_g, m_rw_ln_b, m_w_branch_a, m_w_branch_b, m_w_gate, m_b_gate, m_w_out, m_g_ffn, m_w_up, m_conv_w, m_conv_b, m_w_down, m_g_ple, m_w_ple_gate, m_w_ple, m_g_final, v_g_mix, v_w_in, v_rw_mu, v_rw_w0, v_rw_w_up, v_rw_a0, v_rw_a_up, v_rw_g_up, v_rw_k_k, v_rw_k_a, v_rw_r_k, v_rw_ln_g, v_rw_ln_b, v_w_branch_a, v_w_branch_b, v_w_gate, v_b_gate, v_w_out, v_g_ffn, v_w_up, v_conv_w, v_conv_b, v_w_down, v_g_ple, v_w_ple_gate, v_w_ple, v_g_final):
    given = dict(zip(ARG_NAMES, (x, p, g_mix, w_in, rw_mu, rw_w0, rw_w_up, rw_a0, rw_a_up, rw_g_up, rw_k_k, rw_k_a, rw_r_k, rw_ln_g, rw_ln_b, w_branch_a, w_branch_b, w_gate, b_gate, w_out, g_ffn, w_up, conv_w, conv_b, w_down, g_ple, w_ple_gate, w_ple, g_final, loss_target, m_g_mix, m_w_in, m_rw_mu, m_rw_w0, m_rw_w_up, m_rw_a0, m_rw_a_up, m_rw_g_up, m_rw_k_k, m_rw_k_a, m_rw_r_k, m_rw_ln_g, m_rw_ln_b, m_w_branch_a, m_w_branch_b, m_w_gate, m_b_gate, m_w_out, m_g_ffn, m_w_up, m_conv_w, m_conv_b, m_w_down, m_g_ple, m_w_ple_gate, m_w_ple, m_g_final, v_g_mix, v_w_in, v_rw_mu, v_rw_w0, v_rw_w_up, v_rw_a0, v_rw_a_up, v_rw_g_up, v_rw_k_k, v_rw_k_a, v_rw_r_k, v_rw_ln_g, v_rw_ln_b, v_w_branch_a, v_w_branch_b, v_w_gate, v_b_gate, v_w_out, v_g_ffn, v_w_up, v_conv_w, v_conv_b, v_w_down, v_g_ple, v_w_ple_gate, v_w_ple, v_g_final), strict=True))

    def two_d(name, prefix=""):
        a = given[prefix + name]
        if name == "g_final":
            return a.reshape(1, D_MODEL)
        if name == "rw_r_k":
            return a.reshape(1, RW_WIDTH)
        return a[0] if a.ndim == 3 else a

    shard_shapes = {n: two_d(n).shape for n in SHARDED}
    sizes = {n: shard_shapes[n][0] * shard_shapes[n][1] for n in SHARDED}
    packed = _pack_rows([two_d(n).reshape(1, -1) for n in SHARDED], 1)[0]
    gathered = _gather_chips(packed).reshape(N_CHIPS, -1)
    W, off = {}, 0
    for n, axis in SHARDED.items():
        W[n] = _full_from_shards(gathered[:, off:off + sizes[n]], shard_shapes[n], axis)
        off += sizes[n]
    for n in SMALL:
        W[n] = two_d(n)
    W["rw_r_k"] = W["rw_r_k"].reshape(RW_HEADS, RW_HEAD_DIM)

    loss_part, grad_x, G = _local_step(x[0], p[0, 0], W, loss_target[0])

    gp = _pack_rows([_shards_from_full(G[n], axis) for n, axis in SHARDED.items()], N_CHIPS)
    half = gp.shape[1] // 2
    c = lax.axis_index("c")
    mine = lax.dynamic_slice_in_dim(gp, c * half, half, axis=1)
    pair = _add_pairs("sum_cores", mine, _swap_halves(gp))
    reduced = _sum_leading("sum_chips", _scatter_chips(pair))
    shard_grads = _join_halves(reduced).reshape(-1)

    small_sizes = {n: two_d(n).shape[1] for n in SMALL}
    n_small = sum(small_sizes.values())

    def pack_small(parts, last):
        return _pack_rows([a.reshape(1, -1) for a in parts] + [last.reshape(1, 1)], 1)[0]

    G["rw_r_k"] = G["rw_r_k"].reshape(1, RW_WIDTH)
    zero = jnp.zeros((1, 1), F32)
    all_small = _gather_all(pack_small([G[n] for n in SMALL], loss_part))
    gs, ds, nms, nvs = _adamw_small(all_small, pack_small([two_d(n) for n in SMALL], zero),
                                    pack_small([two_d(n, "m_") for n in SMALL], zero),
                                    pack_small([two_d(n, "v_") for n in SMALL], zero))
    gs, ds, nms, nvs = (a.reshape(-1) for a in (gs, ds, nms, nvs))
    loss = gs[n_small]

    grads, deltas, new_m, new_v = {}, {}, {}, {}
    off = 0
    for n in SHARDED:
        g = shard_grads[off:off + sizes[n]].reshape(shard_shapes[n])
        off += sizes[n]
        d, nm, nv = _adamw("adamw_" + n, two_d(n), g, two_d(n, "m_"), two_d(n, "v_"))
        grads[n], deltas[n], new_m[n], new_v[n] = g, d, nm, nv
    off = 0
    for n in SMALL:
        sl = slice(off, off + small_sizes[n])
        off += small_sizes[n]
        grads[n], deltas[n], new_m[n], new_v[n] = gs[sl], ds[sl], nms[sl], nvs[sl]
    outs = [loss, grad_x[None]]
    for table in (grads, deltas, new_m, new_v):
        outs += [table[n].reshape(given[n].shape) for n in WEIGHTS]
    return tuple(outs)
```

```python
import math

import jax
import jax.numpy as jnp
import numpy as np
from jax import lax
from jax.experimental import pallas as pl
from jax.experimental.pallas import tpu as pltpu

F32 = jnp.float32
BF16 = jnp.bfloat16

D_MODEL = 1024
NORM_EPS = 1e-6
RW_HEADS = 8
RW_HEAD_DIM = 64
RW_WIDTH = 512
RW_LN_EPS = 64e-5
ATT_GROUP_DILATION = (1, 4, 16)
ATT_BLOCK = 128
ATT_HEADS = 12
ATT_HEAD_DIM = 64
ATT_GROUP_WIDTH = 256
ATT_WIDTH = 768
D_FF = 3072

ADAM_LR = 0.001
ADAM_B1 = 0.9
ADAM_B2 = 0.999
ADAM_EPS = 1e-08
ADAM_WD = 0.01
ADAM_STEP = 10

SUBLANES = 8
LANES = 128
VMEM_LIMIT = 56 * 1024 * 1024
N_CHIPS = 4
N_DEV = 8
MESH = pl.DeviceIdType.MESH


def _params(sem=None):
    return pltpu.CompilerParams(dimension_semantics=sem, vmem_limit_bytes=VMEM_LIMIT)


def _pick(dim, pref):
    if dim % LANES != 0 or dim <= pref:
        return dim
    best = LANES
    for t in range(LANES, pref + 1, LANES):
        if dim % t == 0:
            best = t
    return best


def _mm(name, a, b, mode, out_dtype=F32, add=None, tm=512, tn=512, tk=1024):
    if mode == "nn":
        (M, K), (K2, N) = a.shape, b.shape
    elif mode == "nt":
        (M, K), (N, K2) = a.shape, b.shape
    else:
        (K, M), (K2, N) = a.shape, b.shape
    assert K == K2, (name, a.shape, b.shape, mode)
    tm, tn, tk = _pick(M, tm), _pick(N, tn), _pick(K, tk)
    nk = K // tk
    if mode == "nn":
        a_spec = pl.BlockSpec((tm, tk), lambda i, j, k: (i, k))
        b_spec = pl.BlockSpec((tk, tn), lambda i, j, k: (k, j))
        dims = (((1,), (0,)), ((), ()))
    elif mode == "nt":
        a_spec = pl.BlockSpec((tm, tk), lambda i, j, k: (i, k))
        b_spec = pl.BlockSpec((tn, tk), lambda i, j, k: (j, k))
        dims = (((1,), (1,)), ((), ()))
    else:
        a_spec = pl.BlockSpec((tk, tm), lambda i, j, k: (k, i))
        b_spec = pl.BlockSpec((tk, tn), lambda i, j, k: (k, j))
        dims = (((0,), (0,)), ((), ()))
    o_spec = pl.BlockSpec((tm, tn), lambda i, j, k: (i, j))
    has_add = add is not None

    def body(*refs):
        if has_add:
            a_ref, b_ref, add_ref, o_ref, acc_ref = refs
        else:
            a_ref, b_ref, o_ref, acc_ref = refs
        k = pl.program_id(2)
        part = lax.dot_general(a_ref[...].astype(BF16), b_ref[...].astype(BF16), dims,
                               preferred_element_type=F32)

        @pl.when(k == 0)
        def _():
            acc_ref[...] = part

        @pl.when(k > 0)
        def _():
            acc_ref[...] += part

        @pl.when(k == nk - 1)
        def _():
            res = acc_ref[...]
            if has_add:
                res = res + add_ref[...].astype(F32)
            o_ref[...] = res.astype(o_ref.dtype)

    ins = [a, b] + ([add] if has_add else [])
    in_specs = [a_spec, b_spec] + ([o_spec] if has_add else [])
    return pl.pallas_call(
        body, name=name, grid=(M // tm, N // tn, nk),
        in_specs=in_specs, out_specs=o_spec,
        out_shape=jax.ShapeDtypeStruct((M, N), out_dtype),
        scratch_shapes=[pltpu.VMEM((tm, tn), F32)],
        compiler_params=_params(("parallel", "parallel", "arbitrary")),
    )(*ins)


def _rowwise(name, fn, T, tT, rows=(), prevs=(), nexts=(), consts=(), outs=()):
    n = T // tT
    per8 = tT // SUBLANES
    in_specs, ins = [], []
    for arr in rows:
        in_specs.append(pl.BlockSpec((tT, arr.shape[1]), lambda i: (i, 0)))
        ins.append(arr)
    for arr in prevs:
        in_specs.append(pl.BlockSpec((SUBLANES, arr.shape[1]), lambda i: (jnp.maximum(i * per8 - 1, 0), 0)))
        ins.append(arr)
    for arr in nexts:
        in_specs.append(pl.BlockSpec((SUBLANES, arr.shape[1]),
                                     lambda i: (jnp.minimum((i + 1) * per8, T // SUBLANES - 1), 0)))
        ins.append(arr)
    for arr in consts:
        in_specs.append(pl.BlockSpec(arr.shape, lambda i, nd=arr.ndim: (0,) * nd))
        ins.append(arr)
    out_specs, out_shapes = [], []
    for o in outs:
        if o[0] == "row":
            out_specs.append(pl.BlockSpec((tT, o[1]), lambda i: (i, 0)))
            out_shapes.append(jax.ShapeDtypeStruct((T, o[1]), o[2]))
        else:
            out_specs.append(pl.BlockSpec(o[1], lambda i: (0, 0)))
            out_shapes.append(jax.ShapeDtypeStruct(o[1], F32))
    nr, npv, nnx, nc = len(rows), len(prevs), len(nexts), len(consts)
    n_in = nr + npv + nnx + nc

    def body(*refs):
        i = pl.program_id(0)
        vals = [r[...] for r in refs[:n_in]]
        res = fn(i, n, vals[:nr], vals[nr:nr + npv], vals[nr + npv:nr + npv + nnx], vals[nr + npv + nnx:])
        for o, o_ref, val in zip(outs, refs[n_in:], res, strict=True):
            if o[0] == "row":
                o_ref[...] = val.astype(o_ref.dtype)
            else:
                @pl.when(i == 0)
                def _(o_ref=o_ref, val=val):
                    o_ref[...] = val.astype(F32)

                @pl.when(i > 0)
                def _(o_ref=o_ref, val=val):
                    o_ref[...] += val.astype(F32)

    res = pl.pallas_call(
        body, name=name, grid=(n,), in_specs=in_specs, out_specs=out_specs, out_shape=out_shapes,
        compiler_params=_params(("arbitrary",)),
    )(*ins)
    return list(res)


def _shift_down(x, prev8, i, s):
    rolled = pltpu.roll(x, s, 0)
    head = pltpu.roll(prev8, s, 0)
    head = jnp.where(i == 0, jnp.zeros_like(head), head)
    rid = lax.broadcasted_iota(jnp.int32, head.shape, 0)
    first = jnp.where(rid < s, head, rolled[:SUBLANES])
    return jnp.concatenate([first, rolled[SUBLANES:]], axis=0)


def _shift_up(x, next8, i, n, s):
    tT = x.shape[0]
    rolled = pltpu.roll(x, tT - s, 0)
    tail = pltpu.roll(next8, SUBLANES - s, 0)
    tail = jnp.where(i == n - 1, jnp.zeros_like(tail), tail)
    rid = lax.broadcasted_iota(jnp.int32, tail.shape, 0)
    last = jnp.where(rid >= SUBLANES - s, tail, rolled[tT - SUBLANES:])
    return jnp.concatenate([rolled[:tT - SUBLANES], last], axis=0)


def _colsum(x):
    return jnp.sum(x, axis=0, keepdims=True)


def _segsum(x, bd):
    return jnp.dot(x, bd, precision=lax.Precision.HIGHEST, preferred_element_type=F32)


def _block_diag_ones(width, seg):
    idx = np.arange(width) // seg
    return jnp.asarray((idx[:, None] == idx[None, :]).astype(np.float32))


def _sigmoid(z):
    return 1.0 / (1.0 + jnp.exp(-z))


def _softplus(z):
    return jnp.maximum(z, 0.0) + jnp.log(1.0 + jnp.exp(-jnp.abs(z)))


def _rms_fwd(x, g):
    r = lax.rsqrt(jnp.mean(x * x, axis=-1, keepdims=True) + NORM_EPS)
    return x * r * g


def _rms_bwd(x, g, dy):
    r = lax.rsqrt(jnp.mean(x * x, axis=-1, keepdims=True) + NORM_EPS)
    gdy = dy * g
    dx = r * (gdy - x * (r * r) * jnp.mean(x * gdy, axis=-1, keepdims=True))
    return dx, dy * x * r


GELU_C = math.sqrt(2.0 / math.pi)


def _gelu(x):
    return 0.5 * x * (1.0 + jnp.tanh(GELU_C * (x + 0.044715 * x * x * x)))


def _gelu_grad(x):
    th = jnp.tanh(GELU_C * (x + 0.044715 * x * x * x))
    return 0.5 * (1.0 + th) + 0.5 * x * (1.0 - th * th) * GELU_C * (1.0 + 3.0 * 0.044715 * x * x)


RW_CHUNK = 64
NN = (((1,), (0,)), ((), ()))
NT = (((1,), (1,)), ((), ()))
TN = (((0,), (0,)), ((), ()))


def _hdot(a, b, dims):
    return lax.dot_general(a, b, dims, precision=lax.Precision.HIGHEST, preferred_element_type=F32)


def _chunk_masks():
    ti = lax.broadcasted_iota(jnp.int32, (RW_CHUNK, RW_CHUNK), 0)
    tj = lax.broadcasted_iota(jnp.int32, (RW_CHUNK, RW_CHUNK), 1)
    return tj <= ti, tj < ti, (ti == tj).astype(F32)


def _chunk_terms(r, lw, k, a, b, incl, strict):
    c = _hdot(incl.astype(F32), lw, NN)
    e_prev, e_neg, e_pos = jnp.exp(c - lw), jnp.exp(-c), jnp.exp(c)
    At, Bt, Kt, Rt = a * e_prev, b * e_neg, k * e_neg, r * e_pos
    A1 = jnp.where(strict, _hdot(At, Bt, NT), 0.0)
    A2 = jnp.where(strict, _hdot(At, Kt, NT), 0.0)
    W1 = jnp.where(incl, _hdot(Rt, Bt, NT), 0.0)
    W2 = jnp.where(incl, _hdot(Rt, Kt, NT), 0.0)
    return dict(At=At, Bt=Bt, Kt=Kt, Rt=Rt, A1=A1, A2=A2, W1=W1, W2=W2, e_prev=e_prev, e_neg=e_neg, e_pos=e_pos,
                p_last=e_pos[RW_CHUNK - 1:RW_CHUNK, :])


def _rwkv_chunk_fwd(r, lw, k, a, b, v):
    T = r.shape[0]
    nC = T // RW_CHUNK
    H, N = RW_HEADS, RW_HEAD_DIM

    def body(r_ref, lw_ref, k_ref, a_ref, b_ref, v_ref, y_ref, s0_ref, ti_ref, sa_ref, S_ref):
        @pl.when(pl.program_id(0) == 0)
        def _():
            S_ref[...] = jnp.zeros_like(S_ref)

        incl, strict, eye = _chunk_masks()
        ys, sas = [], []
        for h in range(H):
            sl = slice(h * N, (h + 1) * N)
            q = _chunk_terms(r_ref[:, sl], lw_ref[:, sl], k_ref[:, sl], a_ref[:, sl], b_ref[:, sl], incl, strict)
            V = v_ref[:, sl]
            S0 = S_ref[h]
            s0_ref[0, h] = S0
            tinv, pw = eye + q["A1"], q["A1"]
            for _ in range(5):
                pw = _hdot(pw, pw, NN)
                tinv = tinv + _hdot(tinv, pw, NN)
            ti_ref[0, h] = tinv
            Sa = _hdot(tinv, _hdot(q["At"], S0, NT) + _hdot(q["A2"], V, NN), NN)
            ys.append(_hdot(q["Rt"], S0, NT) + _hdot(q["W1"], Sa, NN) + _hdot(q["W2"], V, NN))
            sas.append(Sa)
            S_ref[h] = (S0 + _hdot(Sa, q["Bt"], TN) + _hdot(V, q["Kt"], TN)) * q["p_last"]
        y_ref[...] = jnp.concatenate(ys, axis=1)
        sa_ref[...] = jnp.concatenate(sas, axis=1)

    row_spec = pl.BlockSpec((RW_CHUNK, RW_WIDTH), lambda n: (n, 0))
    st_spec = pl.BlockSpec((1, H, N, N), lambda n: (n, 0, 0, 0))
    st_shape = jax.ShapeDtypeStruct((nC, H, N, N), F32)
    return pl.pallas_call(
        body, name="rwkv_chunk_fwd", grid=(nC,),
        in_specs=[row_spec] * 6, out_specs=[row_spec, st_spec, st_spec, row_spec],
        out_shape=[jax.ShapeDtypeStruct((T, RW_WIDTH), F32), st_shape, st_shape,
                   jax.ShapeDtypeStruct((T, RW_WIDTH), F32)],
        scratch_shapes=[pltpu.VMEM((H, N, N), F32)],
        compiler_params=_params(("arbitrary",)),
    )(r, lw, k, a, b, v)


def _rwkv_chunk_bwd(r, lw, k, a, b, v, dy, s0, tinv, sa):
    T = r.shape[0]
    nC = T // RW_CHUNK
    H, N = RW_HEADS, RW_HEAD_DIM

    def body(r_ref, lw_ref, k_ref, a_ref, b_ref, v_ref, dy_ref, s0_ref, ti_ref, sa_ref,
             dr_ref, dlw_ref, dk_ref, da_ref, db_ref, dv_ref, dS_ref):
        @pl.when(pl.program_id(0) == 0)
        def _():
            dS_ref[...] = jnp.zeros_like(dS_ref)

        incl, strict, _ = _chunk_masks()
        incl_f = incl.astype(F32)
        last_row = lax.broadcasted_iota(jnp.int32, (RW_CHUNK, N), 0) == RW_CHUNK - 1
        outs = [[] for _ in range(6)]
        for h in range(H):
            sl = slice(h * N, (h + 1) * N)
            lw_h = lw_ref[:, sl]
            q = _chunk_terms(r_ref[:, sl], lw_h, k_ref[:, sl], a_ref[:, sl], b_ref[:, sl], incl, strict)
            At, Bt, Kt, Rt = q["At"], q["Bt"], q["Kt"], q["Rt"]
            V, dY, Sa, S0, Ti = v_ref[:, sl], dy_ref[:, sl], sa_ref[:, sl], s0_ref[0, h], ti_ref[0, h]
            G = dS_ref[h] * q["p_last"]
            X = S0 + _hdot(Sa, Bt, TN) + _hdot(V, Kt, TN)
            dc_last = jnp.sum(G * X, axis=0, keepdims=True)
            dSa = _hdot(Bt, G, NT) + _hdot(q["W1"], dY, TN)
            dZ = _hdot(Ti, dSa, TN)
            dA1 = jnp.where(strict, _hdot(dZ, Sa, NT), 0.0)
            dA2 = jnp.where(strict, _hdot(dZ, V, NT), 0.0)
            dW1 = jnp.where(incl, _hdot(dY, Sa, NT), 0.0)
            dW2 = jnp.where(incl, _hdot(dY, V, NT), 0.0)
            dV = _hdot(q["A2"], dZ, TN) + _hdot(q["W2"], dY, TN) + _hdot(Kt, G, NT)
            dS_ref[h] = G + _hdot(dZ, At, TN) + _hdot(dY, Rt, TN)
            dAt = _hdot(dA1, Bt, NN) + _hdot(dA2, Kt, NN) + _hdot(dZ, S0, NN)
            dBt = _hdot(dA1, At, TN) + _hdot(dW1, Rt, TN) + _hdot(Sa, G, NN)
            dKt = _hdot(dA2, At, TN) + _hdot(dW2, Rt, TN) + _hdot(V, G, NN)
            dRt = _hdot(dY, S0, NN) + _hdot(dW1, Bt, NN) + _hdot(dW2, Kt, NN)
            dc_prev = dAt * At
            dc = dc_prev + dRt * Rt - dBt * Bt - dKt * Kt + jnp.where(last_row, dc_last, 0.0)
            dlw = _hdot(incl_f, dc, TN) - dc_prev
            for lst, val in zip(outs, (dRt * q["e_pos"], dlw, dKt * q["e_neg"], dAt * q["e_prev"],
                                       dBt * q["e_neg"], dV), strict=True):
                lst.append(val)
        for o_ref, lst in zip((dr_ref, dlw_ref, dk_ref, da_ref, db_ref, dv_ref), outs, strict=True):
            o_ref[...] = jnp.concatenate(lst, axis=1)

    rev = lambda n: nC - 1 - n
    row_spec = pl.BlockSpec((RW_CHUNK, RW_WIDTH), lambda n: (rev(n), 0))
    st_spec = pl.BlockSpec((1, H, N, N), lambda n: (rev(n), 0, 0, 0))
    row_shape = jax.ShapeDtypeStruct((T, RW_WIDTH), F32)
    return pl.pallas_call(
        body, name="rwkv_chunk_bwd", grid=(nC,),
        in_specs=[row_spec] * 7 + [st_spec, st_spec, row_spec], out_specs=[row_spec] * 6,
        out_shape=[row_shape] * 6, scratch_shapes=[pltpu.VMEM((H, N, N), F32)],
        compiler_params=_params(("arbitrary",)),
    )(r, lw, k, a, b, v, dy, s0, tinv, sa)


def _alibi_slope(head):
    return float(np.float32(2.0 ** (-8.0 * (head + 1) / ATT_HEADS)))


def _att_masks():
    qi = lax.broadcasted_iota(jnp.int32, (ATT_BLOCK, ATT_BLOCK), 0)
    kj = lax.broadcasted_iota(jnp.int32, (ATT_BLOCK, ATT_BLOCK), 1)
    return qi, kj


NEG = -1e30


def _att_logits(q, k, slope_d, steps, valid):
    s = lax.dot_general(q.astype(BF16), k.astype(BF16), (((1,), (1,)), ((), ())),
                        preferred_element_type=F32) * (ATT_HEAD_DIM ** -0.5)
    return jnp.where(valid, s - slope_d * steps.astype(F32), NEG)


def _att_fwd(p_att, g):
    T = p_att.shape[0]
    d = ATT_GROUP_DILATION[g]
    nbs = T // (ATT_BLOCK * d)
    W = ATT_GROUP_WIDTH
    view = p_att.reshape(T // d, d * 3 * ATT_WIDTH)
    cols = 3 * ATT_WIDTH // W

    def body(q_ref, kc_ref, kp_ref, vc_ref, vp_ref, o_ref, l_ref):
        m = pl.program_id(1)
        qi, kj = _att_masks()
        has_prev = m > 0
        outs, lses = [], []
        for j in range(4):
            sl = slice(j * ATT_HEAD_DIM, (j + 1) * ATT_HEAD_DIM)
            slope_d = _alibi_slope(4 * g + j) * d
            q = q_ref[:, sl]
            lc = _att_logits(q, kc_ref[:, sl], slope_d, qi - kj, kj <= qi)
            lp = _att_logits(q, kp_ref[:, sl], slope_d, qi - kj + ATT_BLOCK, (kj >= qi) & has_prev)
            mx = jnp.maximum(jnp.max(lc, axis=1, keepdims=True), jnp.max(lp, axis=1, keepdims=True))
            ec, ep = jnp.exp(lc - mx), jnp.exp(lp - mx)
            den = jnp.sum(ec, axis=1, keepdims=True) + jnp.sum(ep, axis=1, keepdims=True)
            lse = mx + jnp.log(den)
            pc, pp = jnp.exp(lc - lse), jnp.exp(lp - lse)
            o = (jnp.dot(pc.astype(BF16), vc_ref[:, sl].astype(BF16), preferred_element_type=F32)
                 + jnp.dot(pp.astype(BF16), vp_ref[:, sl].astype(BF16), preferred_element_type=F32))
            outs.append(o)
            lses.append(jnp.broadcast_to(lse, (ATT_BLOCK, ATT_HEAD_DIM)))
        o_ref[...] = jnp.concatenate(outs, axis=1)
        l_ref[...] = jnp.concatenate(lses, axis=1)

    def spec(col0, prev):
        if prev:
            return pl.BlockSpec((ATT_BLOCK, W), lambda r, m: (jnp.maximum(m - 1, 0), r * cols + col0 + g))
        return pl.BlockSpec((ATT_BLOCK, W), lambda r, m: (m, r * cols + col0 + g))

    o_spec = pl.BlockSpec((ATT_BLOCK, W), lambda r, m: (m, r))
    o, l = pl.pallas_call(
        body, name=f"att_fwd_g{g}", grid=(d, nbs),
        in_specs=[spec(0, False), spec(3, False), spec(3, True), spec(6, False), spec(6, True)],
        out_specs=[o_spec, o_spec],
        out_shape=[jax.ShapeDtypeStruct((T // d, d * W), F32)] * 2,
        compiler_params=_params(("parallel", "arbitrary")),
    )(view, view, view, view, view)
    return o.reshape(T, W), l.reshape(T, W)


def _att_bwd(p_att, o, l, do, dl, g):
    T = p_att.shape[0]
    d = ATT_GROUP_DILATION[g]
    nbs = T // (ATT_BLOCK * d)
    W = ATT_GROUP_WIDTH
    view = p_att.reshape(T // d, d * 3 * ATT_WIDTH)
    cols = 3 * ATT_WIDTH // W
    ov, lv, dov, dlv = (z.reshape(T // d, d * W) for z in (o, l, do, dl))
    scale = ATT_HEAD_DIM ** -0.5

    def body(q_ref, k_ref, v_ref, o_ref, l_ref, do_ref, dl_ref,
             qn_ref, on_ref, ln_ref, don_ref, dln_ref, dq_ref, dk_ref, dv_ref, carry_ref):
        m = pl.program_id(1)
        qi, kj = _att_masks()
        has_next = m < nbs - 1

        @pl.when(m == 0)
        def _():
            carry_ref[...] = jnp.zeros_like(carry_ref)

        dqs, dks, dvs, carries = [], [], [], []
        for j in range(4):
            sl = slice(j * ATT_HEAD_DIM, (j + 1) * ATT_HEAD_DIM)
            slope_d = _alibi_slope(4 * g + j) * d
            k, v = k_ref[:, sl], v_ref[:, sl]
            kb, vb = k.astype(BF16), v.astype(BF16)

            def side(q, o_, lse, do_, dlse, steps, valid):
                lg = _att_logits(q, k, slope_d, steps, valid)
                p = jnp.exp(lg - lse)
                dp = lax.dot_general(do_.astype(BF16), vb, (((1,), (1,)), ((), ())), preferred_element_type=F32)
                dsum = jnp.sum(do_ * o_, axis=1, keepdims=True)
                ds = p * (dp - dsum + dlse)
                dv_ = jnp.dot(p.T.astype(BF16), do_.astype(BF16), preferred_element_type=F32)
                dk_ = jnp.dot(ds.T.astype(BF16), q.astype(BF16), preferred_element_type=F32) * scale
                dq_ = jnp.dot(ds.astype(BF16), kb, preferred_element_type=F32) * scale
                return dq_, dk_, dv_

            dq_c, dk_c, dv_c = side(q_ref[:, sl], o_ref[:, sl], l_ref[:, sl][:, :1], do_ref[:, sl],
                                    dl_ref[:, sl][:, :1], qi - kj, kj <= qi)
            dq_n, dk_n, dv_n = side(qn_ref[:, sl], on_ref[:, sl], ln_ref[:, sl][:, :1], don_ref[:, sl],
                                    dln_ref[:, sl][:, :1], qi - kj + ATT_BLOCK, (kj >= qi) & has_next)
            dqs.append(dq_c)
            carries.append(dq_n)
            dks.append(dk_c + dk_n)
            dvs.append(dv_c + dv_n)
        dq_ref[...] = jnp.concatenate(dqs, axis=1) + carry_ref[...]
        carry_ref[...] = jnp.concatenate(carries, axis=1)
        dk_ref[...] = jnp.concatenate(dks, axis=1)
        dv_ref[...] = jnp.concatenate(dvs, axis=1)

    nxt = lambda m: jnp.minimum(m + 1, nbs - 1)
    cur_p = lambda col0: pl.BlockSpec((ATT_BLOCK, W), lambda r, m: (m, r * cols + col0 + g))
    nxt_p = lambda col0: pl.BlockSpec((ATT_BLOCK, W), lambda r, m: (nxt(m), r * cols + col0 + g))
    cur_o = pl.BlockSpec((ATT_BLOCK, W), lambda r, m: (m, r))
    nxt_o = pl.BlockSpec((ATT_BLOCK, W), lambda r, m: (nxt(m), r))
    dq, dk, dv = pl.pallas_call(
        body, name=f"att_bwd_g{g}", grid=(d, nbs),
        in_specs=[cur_p(0), cur_p(3), cur_p(6), cur_o, cur_o, cur_o, cur_o,
                  nxt_p(0), nxt_o, nxt_o, nxt_o, nxt_o],
        out_specs=[cur_o, cur_o, cur_o],
        out_shape=[jax.ShapeDtypeStruct((T // d, d * W), F32)] * 3,
        scratch_shapes=[pltpu.VMEM((ATT_BLOCK, W), F32)],
        compiler_params=_params(("parallel", "arbitrary")),
    )(view, view, view, ov, lv, dov, dlv, view, ov, lv, dov, dlv)
    return dq.reshape(T, W), dk.reshape(T, W), dv.reshape(T, W)


RKV = 3 * RW_WIDTH
WA = 128
XG = 160
RW_COLS = RKV + WA + XG


def _local_step(x, p, W, target):
    T = x.shape[0]
    tT = 256
    bd512 = _block_diag_ones(RW_WIDTH, RW_HEAD_DIM)
    bd256 = _block_diag_ones(ATT_GROUP_WIDTH, ATT_HEAD_DIM)
    G = {}

    w_in = W["w_in"]
    w_rkv, w_wa, w_xg, w_att = (w_in[:, :RKV], w_in[:, RKV:RKV + WA], w_in[:, RKV + WA:RW_COLS],
                                w_in[:, RW_COLS:])
    mu = W["rw_mu"]
    mu_rkv, mu_wa, mu_xg = mu[:, :RKV], mu[:, RKV:RKV + WA], mu[:, RKV + WA:]
    zpad = jnp.zeros((64, RW_WIDTH), W["rw_w_up"].dtype)
    w_up_pad = jnp.concatenate([W["rw_w_up"], zpad], axis=0)
    a_up_pad = jnp.concatenate([zpad, W["rw_a_up"]], axis=0)
    r_k = W["rw_r_k"].reshape(1, RW_WIDTH)

    (h,) = _rowwise("norm_mix", lambda i, n, r, pv, nx, c: [_rms_fwd(r[0], c[0])], T, tT,
                    rows=[x], consts=[W["g_mix"]], outs=[("row", D_MODEL, BF16)])
    p_rkv = _mm("proj_rkv", h, w_rkv, "nn")
    p_wa = _mm("proj_wa", h, w_wa, "nn")
    p_xg = _mm("proj_xg", h, w_xg, "nn")
    p_att = _mm("proj_att", h, w_att, "nn", tn=768)
    z_gate = _mm("proj_gate", h, W["w_gate"], "nn")

    def rw_pre_core(i, rows, prevs, consts):
        prkv, pwa, pxg = rows[:3]
        (mrkv, mwa, mxg, w0, a0, k_k, k_a, wup, aup, gup, bd) = consts[:11]
        m_rkv = prkv + (_shift_down(prkv, prevs[0], i, 1) - prkv) * mrkv
        m_wa = pwa + (_shift_down(pwa, prevs[1], i, 1) - pwa) * mwa
        m_xg = pxg + (_shift_down(pxg, prevs[2], i, 1) - pxg) * mxg
        r, k, v = m_rkv[:, :RW_WIDTH], m_rkv[:, RW_WIDTH:2 * RW_WIDTH], m_rkv[:, 2 * RW_WIDTH:]
        tw = jnp.tanh(m_wa)
        lw = w0 + jnp.dot(tw.astype(BF16), wup.astype(BF16), preferred_element_type=F32)
        wlog = -_softplus(-lw) - 0.5
        log_decay = -jnp.exp(wlog)
        a = _sigmoid(a0 + jnp.dot(m_wa.astype(BF16), aup.astype(BF16), preferred_element_type=F32))
        sg = _sigmoid(m_xg)
        gate = jnp.dot(sg.astype(BF16), gup.astype(BF16), preferred_element_type=F32)
        kkp = k * k_k
        nrm = jnp.sqrt(_segsum(kkp * kkp, bd))
        nrm_c = jnp.maximum(nrm, 1e-12)
        kk = kkp / nrm_c
        k2 = k * (1.0 + (a - 1.0) * k_a)
        return dict(r=r, k=k, v=v, tw=tw, lw=lw, wlog=wlog, log_decay=log_decay, a=a, sg=sg, gate=gate, kkp=kkp,
                    nrm=nrm, nrm_c=nrm_c, kk=kk, k2=k2, m_rkv=m_rkv, m_wa=m_wa, m_xg=m_xg)

    pre_consts = [mu_rkv, mu_wa, mu_xg, W["rw_w0"], W["rw_a0"], W["rw_k_k"], W["rw_k_a"],
                  w_up_pad, a_up_pad, W["rw_g_up"], bd512]

    def rw_pre(i, n, rows, prevs, nexts, consts):
        q = rw_pre_core(i, rows, prevs, consts)
        return [q["r"], q["log_decay"], q["k2"], q["v"], -q["kk"], q["kk"] * q["a"], q["gate"]]

    r_s, w_s, k_s, v_s, a_s, b_s, gate_s = _rowwise(
        "rwkv_pre", rw_pre, T, tT, rows=[p_rkv, p_wa, p_xg], prevs=[p_rkv, p_wa, p_xg], consts=pre_consts,
        outs=[("row", RW_WIDTH, F32)] * 7)
    y_scan, s0_s, tinv_s, sa_s = _rwkv_chunk_fwd(r_s, w_s, k_s, a_s, b_s, v_s)

    def rw_post_core(rows, consts):
        y, r, k2, v, gate = rows[:5]
        ln_g, ln_b, rk, bd = consts[:4]
        mean = _segsum(y, bd) * (1.0 / RW_HEAD_DIM)
        yc = y - mean
        var = _segsum(yc * yc, bd) * (1.0 / RW_HEAD_DIM)
        rstd = lax.rsqrt(var + RW_LN_EPS)
        yn = yc * rstd
        s = _segsum(r * k2 * rk, bd)
        return dict(yn=yn, rstd=rstd, s=s, pre=yn * ln_g + ln_b + s * v)

    post_consts = [W["rw_ln_g"], W["rw_ln_b"], r_k, bd512]
    (y_a,) = _rowwise("rwkv_post", lambda i, n, r, pv, nx, c: [rw_post_core(r, c)["pre"] * r[4]], T, tT,
                      rows=[y_scan, r_s, k_s, v_s, gate_s], consts=post_consts, outs=[("row", RW_WIDTH, BF16)])

    att = [_att_fwd(p_att, g) for g in range(3)]

    def comb_weights(ls):
        mx = jnp.maximum(jnp.maximum(ls[0], ls[1]), ls[2])
        es = [jnp.exp(l - mx) for l in ls]
        den = es[0] + es[1] + es[2]
        return [e / den for e in es]

    def att_comb(i, n, rows, pv, nx, c):
        wts = comb_weights(rows[3:6])
        return [wts[0] * rows[0] + wts[1] * rows[1] + wts[2] * rows[2]]

    (y_b,) = _rowwise("att_combine", att_comb, T, tT, rows=[att[0][0], att[1][0], att[2][0], att[0][1], att[1][1],
                                                            att[2][1]], outs=[("row", ATT_GROUP_WIDTH, BF16)])

    br_a = _mm("branch_a", y_a, W["w_branch_a"], "nn")
    br_b = _mm("branch_b", y_b, W["w_branch_b"], "nn")

    def merge(i, n, rows, pv, nx, c):
        gates = _sigmoid(rows[0] + c[0])
        return [gates[:, :D_MODEL] * rows[1] + gates[:, D_MODEL:] * rows[2]]

    (merged,) = _rowwise("merge", merge, T, tT, rows=[z_gate, br_a, br_b], consts=[W["b_gate"]],
                         outs=[("row", D_MODEL, BF16)])
    x1 = _mm("mix_out", merged, W["w_out"], "nn", add=x)

    (h2,) = _rowwise("norm_ffn", lambda i, n, r, pv, nx, c: [_rms_fwd(r[0], c[0])], T, tT,
                     rows=[x1], consts=[W["g_ffn"]], outs=[("row", D_MODEL, BF16)])
    u = _mm("ffn_up", h2, W["w_up"], "nn")

    def conv_core(i, rows, prevs, consts):
        uu, cw, cb = rows[0], consts[0], consts[1]
        u1 = _shift_down(uu, prevs[0], i, 1)
        u2 = _shift_down(uu, prevs[0], i, 2)
        uc = cb + cw[0:1] * uu + cw[1:2] * u1 + cw[2:3] * u2
        return uc[:, :D_FF], uc[:, D_FF:], u1, u2

    def glu(i, n, rows, prevs, nx, consts):
        gate, val, _, _ = conv_core(i, rows, prevs, consts)
        return [_gelu(gate) * val]

    tF = 128
    (act,) = _rowwise("conv_glu", glu, T, tF, rows=[u], prevs=[u], consts=[W["conv_w"], W["conv_b"]],
                      outs=[("row", D_FF, BF16)])
    x2 = _mm("ffn_down", act, W["w_down"], "nn", add=x1)

    (h3,) = _rowwise("norm_ple", lambda i, n, r, pv, nx, c: [_rms_fwd(r[0], c[0])], T, tT,
                     rows=[x2], consts=[W["g_ple"]], outs=[("row", D_MODEL, BF16)])
    z_ple = _mm("ple_gate", h3, W["w_ple_gate"], "nn")
    e_ple = _mm("ple_emb", p, W["w_ple"], "nn")

    def head(i, n, rows, pv, nx, consts):
        x2_, z, e, tgt = rows
        pg = _sigmoid(z)
        x3 = x2_ + pg * e
        y = _rms_fwd(x3, consts[0])
        err = y - tgt
        loss = 0.5 * jnp.sum(jnp.sum(err * err, axis=1, keepdims=True) * (1.0 / D_MODEL), axis=0, keepdims=True)
        dy = err * (1.0 / D_MODEL)
        dx3, dgf = _rms_bwd(x3, consts[0], dy)
        return [dx3, dx3 * pg, dx3 * e * pg * (1.0 - pg), jnp.broadcast_to(loss, (1, LANES)), _colsum(dgf)]

    dx3, de, dz, loss_acc, G["g_final"] = _rowwise(
        "loss_head", head, T, tT, rows=[x2, z_ple, e_ple, target], consts=[W["g_final"].reshape(1, D_MODEL)],
        outs=[("row", D_MODEL, F32), ("row", D_MODEL, BF16), ("row", D_MODEL, BF16), ("acc", (1, LANES)),
              ("acc", (1, D_MODEL))])
    G["w_ple"] = _mm("d_w_ple", p, de, "tn")
    G["w_ple_gate"] = _mm("d_w_ple_gate", h3, dz, "tn")
    dh3 = _mm("d_h3", dz, W["w_ple_gate"], "nt")

    def norm_bwd(i, n, rows, pv, nx, consts):
        dx, dg = _rms_bwd(rows[0], consts[0], rows[1])
        return [rows[2] + dx, _colsum(dg)]

    dx2, G["g_ple"] = _rowwise("d_norm_ple", norm_bwd, T, tT, rows=[x2, dh3, dx3], consts=[W["g_ple"]],
                               outs=[("row", D_MODEL, F32), ("acc", (1, D_MODEL))])

    dact = _mm("d_act", dx2, W["w_down"], "nt")
    G["w_down"] = _mm("d_w_down", act, dx2, "tn")

    def glu_bwd(i, n, rows, prevs, nx, consts):
        gate, val, u1, u2 = conv_core(i, rows, prevs, consts)
        da = rows[1]
        duc = jnp.concatenate([da * val * _gelu_grad(gate), da * _gelu(gate)], axis=1)
        dcw = jnp.concatenate([_colsum(duc * rows[0]), _colsum(duc * u1), _colsum(duc * u2)], axis=0)
        return [duc, _colsum(duc), dcw]

    duc, G["conv_b"], G["conv_w"] = _rowwise(
        "d_conv_glu", glu_bwd, T, tF, rows=[u, dact], prevs=[u], consts=[W["conv_w"], W["conv_b"]],
        outs=[("row", 2 * D_FF, F32), ("acc", (1, 2 * D_FF)), ("acc", (3, 2 * D_FF))])

    def conv_bwd(i, n, rows, pv, nexts, consts):
        cw = consts[0]
        return [cw[0:1] * rows[0] + cw[1:2] * _shift_up(rows[0], nexts[0], i, n, 1)
                + cw[2:3] * _shift_up(rows[0], nexts[0], i, n, 2)]

    (du,) = _rowwise("d_conv", conv_bwd, T, tF, rows=[duc], nexts=[duc], consts=[W["conv_w"]],
                     outs=[("row", 2 * D_FF, BF16)])
    G["w_up"] = _mm("d_w_up", h2, du, "tn")
    dh2 = _mm("d_h2", du, W["w_up"], "nt")
    dx1, G["g_ffn"] = _rowwise("d_norm_ffn", norm_bwd, T, tT, rows=[x1, dh2, dx2], consts=[W["g_ffn"]],
                               outs=[("row", D_MODEL, F32), ("acc", (1, D_MODEL))])

    dmerged = _mm("d_merged", dx1, W["w_out"], "nt")
    G["w_out"] = _mm("d_w_out", merged, dx1, "tn")

    def merge_bwd(i, n, rows, pv, nx, consts):
        z, a_, b_, dm = rows
        gates = _sigmoid(z + consts[0])
        ga, gb = gates[:, :D_MODEL], gates[:, D_MODEL:]
        dz_ = jnp.concatenate([dm * a_ * ga * (1.0 - ga), dm * b_ * gb * (1.0 - gb)], axis=1)
        return [dm * ga, dm * gb, dz_, _colsum(dz_)]

    d_br_a, d_br_b, dz_gate, G["b_gate"] = _rowwise(
        "d_merge", merge_bwd, T, tT, rows=[z_gate, br_a, br_b, dmerged], consts=[W["b_gate"]],
        outs=[("row", D_MODEL, BF16), ("row", D_MODEL, BF16), ("row", 2 * D_MODEL, BF16), ("acc", (1, 2 * D_MODEL))])
    G["w_branch_a"] = _mm("d_w_branch_a", y_a, d_br_a, "tn")
    G["w_branch_b"] = _mm("d_w_branch_b", y_b, d_br_b, "tn")
    G["w_gate"] = _mm("d_w_gate", h, dz_gate, "tn")
    dy_a = _mm("d_y_a", d_br_a, W["w_branch_a"], "nt")
    dy_b = _mm("d_y_b", d_br_b, W["w_branch_b"], "nt")

    def att_comb_bwd(i, n, rows, pv, nx, consts):
        os_, ls, dy = rows[0:3], rows[3:6], rows[6]
        wts = comb_weights(ls)
        dws = [_segsum(dy * o_, consts[0]) for o_ in os_]
        mix = wts[0] * dws[0] + wts[1] * dws[1] + wts[2] * dws[2]
        return [wts[g_] * dy for g_ in range(3)] + [wts[g_] * (dws[g_] - mix) for g_ in range(3)]

    comb = _rowwise("d_att_combine", att_comb_bwd, T, tT,
                    rows=[att[0][0], att[1][0], att[2][0], att[0][1], att[1][1], att[2][1], dy_b], consts=[bd256],
                    outs=[("row", ATT_GROUP_WIDTH, F32)] * 6)
    dqkv = [_att_bwd(p_att, att[g][0], att[g][1], comb[g], comb[3 + g], g) for g in range(3)]
    dp_att = jnp.concatenate([dqkv[g][part] for part in range(3) for g in range(3)], axis=1).astype(BF16)

    def rw_post_bwd(i, n, rows, pv, nx, consts):
        y, r, k2, v, gate, dya = rows
        ln_g, ln_b, rk, bd = consts
        q = rw_post_core(rows, consts)
        dpre = dya * gate
        dgate = dya * q["pre"]
        dyn = dpre * ln_g
        inv = 1.0 / RW_HEAD_DIM
        dy_scan = q["rstd"] * (dyn - _segsum(dyn, bd) * inv - q["yn"] * (_segsum(dyn * q["yn"], bd) * inv))
        ds = _segsum(dpre * v, bd)
        return [dy_scan, dgate, ds * k2 * rk, ds * r * rk, dpre * q["s"],
                _colsum(dpre * q["yn"]), _colsum(dpre), _colsum(ds * r * k2)]

    dy_scan, dgate, dr_b, dk2_b, dv_b, G["rw_ln_g"], G["rw_ln_b"], d_rk = _rowwise(
        "d_rwkv_post", rw_post_bwd, T, tT, rows=[y_scan, r_s, k_s, v_s, gate_s, dy_a], consts=post_consts,
        outs=[("row", RW_WIDTH, F32)] * 5 + [("acc", (1, RW_WIDTH))] * 3)
    G["rw_r_k"] = d_rk.reshape(RW_HEADS, RW_HEAD_DIM)

    dr_s, dw_s, dk_s, da_s, db_s, dv_s = _rwkv_chunk_bwd(r_s, w_s, k_s, a_s, b_s, v_s, dy_scan, s0_s, tinv_s, sa_s)

    def rw_pre_bwd(i, n, rows, prevs, nx, consts):
        q = rw_pre_core(i, rows, prevs, consts)
        (mrkv, mwa, mxg, w0, a0, k_k, k_a, wup, aup, gup, bd) = consts
        dr, dlogdecay, dk2, dv, dav, dbv, dgate_ = rows[3:10]
        dr = dr + rows[10]
        dk2 = dk2 + rows[11]
        dv = dv + rows[12]
        a, k, kk = q["a"], q["k"], q["kk"]
        dk = dk2 * (1.0 + (a - 1.0) * k_a)
        da = dk2 * k * k_a + dbv * kk
        dkk = dbv * a - dav
        live = q["nrm"] > 1e-12
        dkkp = jnp.where(live, dkk - kk * _segsum(dkk * kk, bd), dkk) / q["nrm_c"]
        dk = dk + dkkp * k_k
        dlw = dlogdecay * q["log_decay"] * _sigmoid(-q["lw"])
        dla = da * a * (1.0 - a)
        nt = (((1,), (1,)), ((), ()))
        dtw = lax.dot_general(dlw.astype(BF16), wup.astype(BF16), nt, preferred_element_type=F32)
        dxa = lax.dot_general(dla.astype(BF16), aup.astype(BF16), nt, preferred_element_type=F32)
        dm_wa = dtw * (1.0 - q["tw"] * q["tw"]) + dxa
        dsg = lax.dot_general(dgate_.astype(BF16), gup.astype(BF16), nt, preferred_element_type=F32)
        dm_xg = dsg * q["sg"] * (1.0 - q["sg"])
        dm_rkv = jnp.concatenate([dr, dk, dv], axis=1)
        prkv, pwa, pxg = rows[:3]
        dmu = jnp.concatenate([_colsum(dm_rkv * (_shift_down(prkv, prevs[0], i, 1) - prkv)),
                               _colsum(dm_wa * (_shift_down(pwa, prevs[1], i, 1) - pwa)),
                               _colsum(dm_xg * (_shift_down(pxg, prevs[2], i, 1) - pxg))], axis=1)
        return [dm_rkv, dm_wa, dm_xg, dlw, dla, q["tw"], q["m_wa"], q["sg"], dmu,
                _colsum(dlw), _colsum(dla), _colsum(dkkp * k), _colsum(dk2 * k * (a - 1.0))]

    (dm_rkv, dm_wa, dm_xg, dlw, dla, tw_s, mwa_s, sg_s, G["rw_mu"], G["rw_w0"], G["rw_a0"], G["rw_k_k"],
     G["rw_k_a"]) = _rowwise(
        "d_rwkv_pre", rw_pre_bwd, T, tT,
        rows=[p_rkv, p_wa, p_xg, dr_s, dw_s, dk_s, dv_s, da_s, db_s, dgate, dr_b, dk2_b, dv_b],
        prevs=[p_rkv, p_wa, p_xg], consts=pre_consts,
        outs=[("row", RKV, F32), ("row", WA, F32), ("row", XG, F32), ("row", RW_WIDTH, BF16),
              ("row", RW_WIDTH, BF16), ("row", WA, BF16), ("row", WA, BF16), ("row", XG, BF16),
              ("acc", (1, RW_COLS))] + [("acc", (1, RW_WIDTH))] * 4)
    G["rw_w_up"] = _mm("d_rw_w_up", tw_s, dlw, "tn")[:64]
    G["rw_a_up"] = _mm("d_rw_a_up", mwa_s, dla, "tn")[64:]
    G["rw_g_up"] = _mm("d_rw_g_up", sg_s, dgate, "tn")

    def shift_bwd(i, n, rows, pv, nexts, consts):
        return [rows[j] * (1.0 - consts[j]) + _shift_up(rows[j], nexts[j], i, n, 1) * consts[j] for j in range(3)]

    dp_rkv, dp_wa, dp_xg = _rowwise(
        "d_token_shift", shift_bwd, T, tT, rows=[dm_rkv, dm_wa, dm_xg], nexts=[dm_rkv, dm_wa, dm_xg],
        consts=[mu_rkv, mu_wa, mu_xg], outs=[("row", RKV, BF16), ("row", WA, BF16), ("row", XG, BF16)])

    G["w_in"] = jnp.concatenate([_mm("d_w_rkv", h, dp_rkv, "tn"), _mm("d_w_wa", h, dp_wa, "tn"),
                                 _mm("d_w_xg", h, dp_xg, "tn"), _mm("d_w_att", h, dp_att, "tn", tn=768)], axis=1)
    dh = _mm("d_h_gate", dz_gate, W["w_gate"], "nt")
    dh = _mm("d_h_rkv", dp_rkv, w_rkv, "nt", add=dh)
    dh = _mm("d_h_wa", dp_wa, w_wa, "nt", add=dh)
    dh = _mm("d_h_xg", dp_xg, w_xg, "nt", add=dh)
    dh = _mm("d_h_att", dp_att, w_att, "nt", add=dh)
    dx, G["g_mix"] = _rowwise("d_norm_mix", norm_bwd, T, tT, rows=[x, dh, dx1], consts=[W["g_mix"]],
                              outs=[("row", D_MODEL, F32), ("acc", (1, D_MODEL))])
    return loss_acc[:, :1], dx, G


HBM_SPEC = pl.BlockSpec(memory_space=pltpu.HBM)


def _place():
    x, y, c = lax.axis_index("x"), lax.axis_index("y"), lax.axis_index("c")
    return x, y, c, [(1 - x, y), (x, 1 - y), (1 - x, 1 - y)]


def _remote(src, dst, send_sems, recv_sems, k, to):
    return pltpu.make_async_remote_copy(src_ref=src, dst_ref=dst, send_sem=send_sems.at[k], recv_sem=recv_sems.at[k],
                                        device_id=to, device_id_type=MESH)


ROW_ALIGN = 16


def _half_rows(ref_rows, c, first):
    half = ref_rows // 2
    which = c if first else 1 - c
    return pl.ds(pl.multiple_of(which * half, ROW_ALIGN), half)


def _gather_chips(shards):
    n = len(shards)
    split = [s.shape[0] % (2 * ROW_ALIGN) == 0 for s in shards]

    def body(*refs):
        w_refs, out_refs = refs[:n], refs[n:2 * n]
        send_sems, recv_sems, local_sems = refs[2 * n:]
        x, y, c, chips = _place()
        me = 2 * x + y
        locals_, sends, passed = [], [], []
        for i in range(n):
            cp = pltpu.make_async_copy(w_refs[i], out_refs[i].at[me], local_sems.at[i])
            cp.start()
            locals_.append(cp)
            for j, (px, py) in enumerate(chips):
                if split[i]:
                    mine = _half_rows(w_refs[i].shape[0], c, True)
                    cp = _remote(w_refs[i].at[mine], out_refs[i].at[me, mine], send_sems, recv_sems, 6 * i + j,
                                 (px, py, c))
                else:
                    cp = _remote(w_refs[i], out_refs[i].at[me], send_sems, recv_sems, 6 * i + j, (px, py, c))
                cp.start()
                sends.append(cp)
        for i in range(n):
            for j, (px, py) in enumerate(chips):
                if split[i]:
                    landed = out_refs[i].at[2 * px + py, _half_rows(w_refs[i].shape[0], c, True)]
                    _remote(landed, landed, send_sems, recv_sems, 6 * i + j, (px, py, c)).wait_recv()
                    cp = _remote(landed, landed, send_sems, recv_sems, 6 * i + 3 + j, (x, y, 1 - c))
                    cp.start()
                    passed.append(cp)
                else:
                    landed = out_refs[i].at[2 * px + py]
                    _remote(landed, landed, send_sems, recv_sems, 6 * i + j, (px, py, c)).wait_recv()
        for i in range(n):
            if split[i]:
                for j, (px, py) in enumerate(chips):
                    landed = out_refs[i].at[2 * px + py, _half_rows(w_refs[i].shape[0], c, False)]
                    _remote(landed, landed, send_sems, recv_sems, 6 * i + 3 + j, (x, y, 1 - c)).wait_recv()
        for cp in sends + passed:
            cp.wait_send()
        for cp in locals_:
            cp.wait()

    return pl.pallas_call(
        body, name="gather_weights", in_specs=[HBM_SPEC] * n, out_specs=[HBM_SPEC] * n,
        out_shape=[jax.ShapeDtypeStruct((N_CHIPS,) + s.shape, s.dtype) for s in shards],
        scratch_shapes=[pltpu.SemaphoreType.DMA((6 * n,)), pltpu.SemaphoreType.DMA((6 * n,)),
                        pltpu.SemaphoreType.DMA((n,))],
    )(*shards)


def _swap_halves(gs):
    n = len(gs)

    def body(*refs):
        g_refs, out_refs = refs[:n], refs[n:2 * n]
        send_sems, recv_sems = refs[2 * n:]
        x, y, c, _ = _place()
        cps = []
        for i in range(n):
            theirs = _half_rows(g_refs[i].shape[1], c, False)
            cp = _remote(g_refs[i].at[:, theirs, :], out_refs[i], send_sems, recv_sems, i, (x, y, 1 - c))
            cp.start()
            cps.append(cp)
        for cp in cps:
            cp.wait()

    return pl.pallas_call(
        body, name="swap_halves", in_specs=[HBM_SPEC] * n, out_specs=[HBM_SPEC] * n,
        out_shape=[jax.ShapeDtypeStruct((N_CHIPS, g.shape[1] // 2, g.shape[2]), g.dtype) for g in gs],
        scratch_shapes=[pltpu.SemaphoreType.DMA((n,)), pltpu.SemaphoreType.DMA((n,))],
    )(*gs)


def _scatter_chips(parts):
    n = len(parts)

    def body(*refs):
        p_refs, out_refs = refs[:n], refs[n:2 * n]
        send_sems, recv_sems, local_sems = refs[2 * n:]
        x, y, c, chips = _place()
        me = 2 * x + y
        locals_, sends = [], []
        for i in range(n):
            cp = pltpu.make_async_copy(p_refs[i].at[me], out_refs[i].at[me], local_sems.at[i])
            cp.start()
            locals_.append(cp)
            for j, (px, py) in enumerate(chips):
                cp = _remote(p_refs[i].at[2 * px + py], out_refs[i].at[me], send_sems, recv_sems, 3 * i + j,
                             (px, py, c))
                cp.start()
                sends.append(cp)
        for i in range(n):
            for j, (px, py) in enumerate(chips):
                landed = out_refs[i].at[2 * px + py]
                _remote(landed, landed, send_sems, recv_sems, 3 * i + j, (px, py, c)).wait_recv()
        for cp in sends:
            cp.wait_send()
        for cp in locals_:
            cp.wait()

    return pl.pallas_call(
        body, name="scatter_grads", in_specs=[HBM_SPEC] * n, out_specs=[HBM_SPEC] * n,
        out_shape=[jax.ShapeDtypeStruct(p.shape, p.dtype) for p in parts],
        scratch_shapes=[pltpu.SemaphoreType.DMA((3 * n,)), pltpu.SemaphoreType.DMA((3 * n,)),
                        pltpu.SemaphoreType.DMA((n,))],
    )(*parts)


def _join_halves(reds):
    n = len(reds)

    def body(*refs):
        r_refs, out_refs = refs[:n], refs[n:2 * n]
        send_sems, recv_sems, local_sems = refs[2 * n:]
        x, y, c, _ = _place()
        cps = []
        for i in range(n):
            mine = _half_rows(out_refs[i].shape[0], c, True)
            local = pltpu.make_async_copy(r_refs[i], out_refs[i].at[mine], local_sems.at[i])
            local.start()
            cp = _remote(r_refs[i], out_refs[i].at[mine], send_sems, recv_sems, i, (x, y, 1 - c))
            cp.start()
            cps.append((local, cp))
        for local, cp in cps:
            cp.wait()
            local.wait()

    return pl.pallas_call(
        body, name="join_halves", in_specs=[HBM_SPEC] * n, out_specs=[HBM_SPEC] * n,
        out_shape=[jax.ShapeDtypeStruct((2 * r.shape[0], r.shape[1]), r.dtype) for r in reds],
        scratch_shapes=[pltpu.SemaphoreType.DMA((n,)), pltpu.SemaphoreType.DMA((n,)), pltpu.SemaphoreType.DMA((n,))],
    )(*reds)


def _gather_all(vec):
    R = vec.shape[0]

    def body(v_ref, out_ref, send_sems, recv_sems, local_sem):
        x, y, c, _ = _place()
        me = 4 * x + 2 * y + c
        local = pltpu.make_async_copy(v_ref, out_ref.at[me], local_sem)
        local.start()
        peers = [(x ^ (k >> 2), y ^ ((k >> 1) & 1), c ^ (k & 1)) for k in range(1, N_DEV)]
        sends = [_remote(v_ref, out_ref.at[me], send_sems, recv_sems, k, to) for k, to in enumerate(peers)]
        for cp in sends:
            cp.start()
        for k, (px, py, pc) in enumerate(peers):
            landed = out_ref.at[4 * px + 2 * py + pc]
            _remote(landed, landed, send_sems, recv_sems, k, (px, py, pc)).wait_recv()
        for cp in sends:
            cp.wait_send()
        local.wait()

    return pl.pallas_call(
        body, name="gather_small", in_specs=[HBM_SPEC], out_specs=HBM_SPEC,
        out_shape=jax.ShapeDtypeStruct((N_DEV, R, LANES), vec.dtype),
        scratch_shapes=[pltpu.SemaphoreType.DMA((7,)), pltpu.SemaphoreType.DMA((7,)), pltpu.SemaphoreType.DMA],
    )(vec)


SUM_TILE_BYTES = 4 * 1024 * 1024


def _sum_rows(half, cols):
    best = ROW_ALIGN
    for t in range(ROW_ALIGN, half + 1, ROW_ALIGN):
        if half % t == 0 and N_CHIPS * t * cols * 4 <= SUM_TILE_BYTES:
            best = t
    return best


def _sum_cores(name, g, theirs, core):
    _, R, C = g.shape
    half = R // 2
    tr = _sum_rows(half, C)
    nb = half // tr

    def body(core_ref, g_ref, t_ref, o_ref):
        o_ref[...] = (g_ref[...] + t_ref[...]).astype(o_ref.dtype)

    grid_spec = pltpu.PrefetchScalarGridSpec(
        num_scalar_prefetch=1, grid=(nb,),
        in_specs=[pl.BlockSpec((N_CHIPS, tr, C), lambda i, core_ref: (0, core_ref[0] * nb + i, 0)),
                  pl.BlockSpec((N_CHIPS, tr, C), lambda i, core_ref: (0, i, 0))],
        out_specs=pl.BlockSpec((N_CHIPS, tr, C), lambda i, core_ref: (0, i, 0)))
    return pl.pallas_call(
        body, name=name, grid_spec=grid_spec, out_shape=jax.ShapeDtypeStruct((N_CHIPS, half, C), BF16),
        compiler_params=_params(("parallel",)),
    )(core, g, theirs)


def _sum_chips(name, parts):
    _, H, C = parts.shape
    tr = _sum_rows(H, C)

    def body(p_ref, o_ref):
        acc = p_ref[0].astype(F32)
        for k in range(1, N_CHIPS):
            acc = acc + p_ref[k].astype(F32)
        o_ref[...] = acc

    return pl.pallas_call(
        body, name=name, grid=(H // tr,),
        in_specs=[pl.BlockSpec((N_CHIPS, tr, C), lambda i: (0, i, 0))],
        out_specs=pl.BlockSpec((tr, C), lambda i: (i, 0)),
        out_shape=jax.ShapeDtypeStruct((H, C), F32),
        compiler_params=_params(("parallel",)),
    )(parts)


def _adamw_math(w, g, m, v):
    m = ADAM_B1 * m + (1.0 - ADAM_B1) * g
    v = ADAM_B2 * v + (1.0 - ADAM_B2) * (g * g)
    m_hat = m / (1.0 - ADAM_B1 ** ADAM_STEP)
    v_hat = v / (1.0 - ADAM_B2 ** ADAM_STEP)
    delta = -ADAM_LR * (m_hat / (jnp.sqrt(v_hat) + ADAM_EPS) + ADAM_WD * w)
    return delta, m, v


def _adamw(name, w, g, m, v):
    R, C = w.shape
    tr = R
    if R % SUBLANES == 0:
        for cand in range(SUBLANES, min(R, 256) + 1, SUBLANES):
            if R % cand == 0:
                tr = cand

    def body(w_ref, g_ref, m_ref, v_ref, d_ref, nm_ref, nv_ref):
        d, nm, nv = _adamw_math(w_ref[...], g_ref[...], m_ref[...], v_ref[...])
        d_ref[...] = d
        nm_ref[...] = nm
        nv_ref[...] = nv

    spec = pl.BlockSpec((tr, C), lambda i: (i, 0))
    shape = jax.ShapeDtypeStruct((R, C), F32)
    return pl.pallas_call(
        body, name=name, grid=(R // tr,), in_specs=[spec] * 4, out_specs=[spec] * 3, out_shape=[shape] * 3,
        compiler_params=_params(("parallel",)),
    )(w, g, m, v)


def _adamw_small(parts, w, m, v):
    n = parts.shape[0]

    def body(p_ref, w_ref, m_ref, v_ref, g_ref, d_ref, nm_ref, nv_ref):
        g = p_ref[0]
        for k in range(1, n):
            g = g + p_ref[k]
        d, nm, nv = _adamw_math(w_ref[...], g, m_ref[...], v_ref[...])
        g_ref[...] = g
        d_ref[...] = d
        nm_ref[...] = nm
        nv_ref[...] = nv

    shape = jax.ShapeDtypeStruct(w.shape, F32)
    return pl.pallas_call(body, name="adamw_small", out_shape=[shape] * 4, compiler_params=_params())(parts, w, m, v)


WEIGHTS = ['g_mix', 'w_in', 'rw_mu', 'rw_w0', 'rw_w_up', 'rw_a0', 'rw_a_up', 'rw_g_up', 'rw_k_k', 'rw_k_a',
           'rw_r_k', 'rw_ln_g', 'rw_ln_b', 'w_branch_a', 'w_branch_b', 'w_gate', 'b_gate', 'w_out', 'g_ffn', 'w_up',
           'conv_w', 'conv_b', 'w_down', 'g_ple', 'w_ple_gate', 'w_ple', 'g_final']
ARG_NAMES = (['x', 'p'] + WEIGHTS + ['loss_target'] + ['m_' + n for n in WEIGHTS] + ['v_' + n for n in WEIGHTS])
SHARDED = {'w_in': 1, 'rw_w_up': 1, 'rw_a_up': 1, 'rw_g_up': 1, 'w_branch_a': 1, 'w_branch_b': 1, 'w_gate': 1,
           'w_out': 0, 'w_up': 1, 'conv_w': 1, 'w_down': 0, 'w_ple_gate': 0, 'w_ple': 1}
SMALL = [n for n in WEIGHTS if n not in SHARDED]
WHOLE = ['conv_w']
SPLIT = [n for n in SHARDED if n not in WHOLE]
PACK_ALIGN = SUBLANES * LANES


def _pack_rows(flat_parts):
    flat = jnp.concatenate(flat_parts, axis=1)
    n = flat.shape[1]
    padded = -(-n // PACK_ALIGN) * PACK_ALIGN
    flat = jnp.pad(flat, ((0, 0), (0, padded - n)))
    return flat.reshape(padded // LANES, LANES)


def _full_from_shards(stack, axis):
    _, R, C = stack.shape
    if axis == 0:
        return stack.reshape(N_CHIPS * R, C)
    return stack.transpose(1, 0, 2).reshape(R, N_CHIPS * C)


def _shards_from_full(full, axis):
    R, C = full.shape
    if axis == 0:
        return full.reshape(N_CHIPS, R // N_CHIPS, C)
    return full.reshape(R, N_CHIPS, C // N_CHIPS).transpose(1, 0, 2)


def kernel(x, p, g_mix, w_in, rw_mu, rw_w0, rw_w_up, rw_a0, rw_a_up, rw_g_up, rw_k_k, rw_k_a, rw_r_k, rw_ln_g, rw_ln_b, w_branch_a, w_branch_b, w_gate, b_gate, w_out, g_ffn, w_up, conv_w, conv_b, w_down, g_ple, w_ple_gate, w_ple, g_final, loss_target, m_g_mix, m_w_in, m_rw_mu, m_rw_w0, m_rw_w_up, m_rw_a0, m_rw_a_up, m_rw_g_up, m_rw_k_k, m_rw_k_a, m_rw_r_k, m_rw_ln_g, m_rw_ln_b, m_w_branch_a, m_w_branch_b, m_w_gate, m_b_gate, m_w_out, m_g_ffn, m_w_up, m_conv_w, m_conv_b, m_w_down, m_g_ple, m_w_ple_gate, m_w_ple, m_g_final, v_g_mix, v_w_in, v_rw_mu, v_rw_w0, v_rw_w_up, v_rw_a0, v_rw_a_up, v_rw_g_up, v_rw_k_k, v_rw_k_a, v_rw_r_k, v_rw_ln_g, v_rw_ln_b, v_w_branch_a, v_w_branch_b, v_w_gate, v_b_gate, v_w_out, v_g_ffn, v_w_up, v_conv_w, v_conv_b, v_w_down, v_g_ple, v_w_ple_gate, v_w_ple, v_g_final):
    given = dict(zip(ARG_NAMES, (x, p, g_mix, w_in, rw_mu, rw_w0, rw_w_up, rw_a0, rw_a_up, rw_g_up, rw_k_k, rw_k_a, rw_r_k, rw_ln_g, rw_ln_b, w_branch_a, w_branch_b, w_gate, b_gate, w_out, g_ffn, w_up, conv_w, conv_b, w_down, g_ple, w_ple_gate, w_ple, g_final, loss_target, m_g_mix, m_w_in, m_rw_mu, m_rw_w0, m_rw_w_up, m_rw_a0, m_rw_a_up, m_rw_g_up, m_rw_k_k, m_rw_k_a, m_rw_r_k, m_rw_ln_g, m_rw_ln_b, m_w_branch_a, m_w_branch_b, m_w_gate, m_b_gate, m_w_out, m_g_ffn, m_w_up, m_conv_w, m_conv_b, m_w_down, m_g_ple, m_w_ple_gate, m_w_ple, m_g_final, v_g_mix, v_w_in, v_rw_mu, v_rw_w0, v_rw_w_up, v_rw_a0, v_rw_a_up, v_rw_g_up, v_rw_k_k, v_rw_k_a, v_rw_r_k, v_rw_ln_g, v_rw_ln_b, v_w_branch_a, v_w_branch_b, v_w_gate, v_b_gate, v_w_out, v_g_ffn, v_w_up, v_conv_w, v_conv_b, v_w_down, v_g_ple, v_w_ple_gate, v_w_ple, v_g_final), strict=True))

    def two_d(name, prefix=""):
        a = given[prefix + name]
        if name == "g_final":
            return a.reshape(1, D_MODEL)
        if name == "rw_r_k":
            return a.reshape(1, RW_WIDTH)
        return a[0] if a.ndim == 3 else a

    gathered = _gather_chips([two_d(n) if n in WHOLE else two_d(n).astype(BF16) for n in SHARDED])
    W = {n: _full_from_shards(g, SHARDED[n]) for n, g in zip(SHARDED, gathered, strict=True)}
    for n in SMALL:
        W[n] = two_d(n)
    W["rw_r_k"] = W["rw_r_k"].reshape(RW_HEADS, RW_HEAD_DIM)

    loss_part, grad_x, G = _local_step(x[0], p[0, 0], W, loss_target[0])

    core = lax.axis_index("c").astype(jnp.int32).reshape(1)
    by_chip = [_shards_from_full(G[n], SHARDED[n]) for n in SPLIT]
    theirs = _swap_halves(by_chip)
    pair = [_sum_cores("sum_cores_" + n, g, t, core) for n, g, t in zip(SPLIT, by_chip, theirs, strict=True)]
    landed = _scatter_chips(pair)
    reduced = [_sum_chips("sum_chips_" + n, q) for n, q in zip(SPLIT, landed, strict=True)]
    shard_grads = dict(zip(SPLIT, _join_halves(reduced), strict=True))

    small_sizes = {n: two_d(n).shape[1] for n in SMALL}
    n_small = sum(small_sizes.values())
    whole_sizes = {n: G[n].shape[0] * G[n].shape[1] for n in WHOLE}
    n_whole = sum(whole_sizes.values())

    def pack_small(parts, rest):
        return _pack_rows([a.reshape(1, -1) for a in parts] + [rest])

    G["rw_r_k"] = G["rw_r_k"].reshape(1, RW_WIDTH)
    rest = jnp.zeros((1, n_whole + 1), F32)
    all_small = _gather_all(pack_small([G[n] for n in SMALL] + [G[n] for n in WHOLE], loss_part))
    gs, ds, nms, nvs = _adamw_small(all_small, pack_small([two_d(n) for n in SMALL], rest),
                                    pack_small([two_d(n, "m_") for n in SMALL], rest),
                                    pack_small([two_d(n, "v_") for n in SMALL], rest))
    gs, ds, nms, nvs = (a.reshape(-1) for a in (gs, ds, nms, nvs))
    loss = gs[n_small + n_whole]
    chip = 2 * lax.axis_index("x") + lax.axis_index("y")
    off = n_small
    for n in WHOLE:
        full = gs[off:off + whole_sizes[n]].reshape(G[n].shape)
        off += whole_sizes[n]
        width = two_d(n).shape[1]
        shard_grads[n] = lax.dynamic_slice_in_dim(full, chip * width, width, axis=1)

    grads, deltas, new_m, new_v = {}, {}, {}, {}
    for n in SHARDED:
        g = shard_grads[n]
        d, nm, nv = _adamw("adamw_" + n, two_d(n), g, two_d(n, "m_"), two_d(n, "v_"))
        grads[n], deltas[n], new_m[n], new_v[n] = g, d, nm, nv
    off = 0
    for n in SMALL:
        sl = slice(off, off + small_sizes[n])
        off += small_sizes[n]
        grads[n], deltas[n], new_m[n], new_v[n] = gs[sl], ds[sl], nms[sl], nvs[sl]
    outs = [loss, grad_x[None]]
    for table in (grads, deltas, new_m, new_v):
        outs += [table[n].reshape(given[n].shape) for n in WEIGHTS]
    return tuple(outs)
```

```python
import math

import jax
import jax.numpy as jnp
import numpy as np
from jax import lax
from jax.experimental import pallas as pl
from jax.experimental.pallas import tpu as pltpu

F32 = jnp.float32
BF16 = jnp.bfloat16

D_MODEL = 1024
NORM_EPS = 1e-6
RW_HEADS = 8
RW_HEAD_DIM = 64
RW_WIDTH = 512
RW_LN_EPS = 64e-5
ATT_GROUP_DILATION = (1, 4, 16)
ATT_BLOCK = 128
ATT_HEADS = 12
ATT_HEAD_DIM = 64
ATT_GROUP_WIDTH = 256
ATT_WIDTH = 768
D_FF = 3072

ADAM_LR = 0.001
ADAM_B1 = 0.9
ADAM_B2 = 0.999
ADAM_EPS = 1e-08
ADAM_WD = 0.01
ADAM_STEP = 10

SUBLANES = 8
LANES = 128
VMEM_LIMIT = 56 * 1024 * 1024
N_CHIPS = 4
N_DEV = 8
MESH = pl.DeviceIdType.MESH


def _params(sem=None):
    return pltpu.CompilerParams(dimension_semantics=sem, vmem_limit_bytes=VMEM_LIMIT)


def _pick(dim, pref):
    if dim % LANES != 0 or dim <= pref:
        return dim
    best = LANES
    for t in range(LANES, pref + 1, LANES):
        if dim % t == 0:
            best = t
    return best


def _mm(name, a, b, mode, out_dtype=F32, add=None, tm=1024, tn=1024, tk=1024):
    if mode == "nn":
        (M, K), (K2, N) = a.shape, b.shape
    elif mode == "nt":
        (M, K), (N, K2) = a.shape, b.shape
    else:
        (K, M), (K2, N) = a.shape, b.shape
    assert K == K2, (name, a.shape, b.shape, mode)
    tm, tn, tk = _pick(M, tm), _pick(N, tn), _pick(K, tk)
    nk = K // tk
    if mode == "nn":
        a_spec = pl.BlockSpec((tm, tk), lambda i, j, k: (i, k))
        b_spec = pl.BlockSpec((tk, tn), lambda i, j, k: (k, j))
        dims = (((1,), (0,)), ((), ()))
    elif mode == "nt":
        a_spec = pl.BlockSpec((tm, tk), lambda i, j, k: (i, k))
        b_spec = pl.BlockSpec((tn, tk), lambda i, j, k: (j, k))
        dims = (((1,), (1,)), ((), ()))
    else:
        a_spec = pl.BlockSpec((tk, tm), lambda i, j, k: (k, i))
        b_spec = pl.BlockSpec((tk, tn), lambda i, j, k: (k, j))
        dims = (((0,), (0,)), ((), ()))
    o_spec = pl.BlockSpec((tm, tn), lambda i, j, k: (i, j))
    has_add = add is not None

    def body(*refs):
        if has_add:
            a_ref, b_ref, add_ref, o_ref, acc_ref = refs
        else:
            a_ref, b_ref, o_ref, acc_ref = refs
        k = pl.program_id(2)
        part = lax.dot_general(a_ref[...].astype(BF16), b_ref[...].astype(BF16), dims,
                               preferred_element_type=F32)

        @pl.when(k == 0)
        def _():
            acc_ref[...] = part

        @pl.when(k > 0)
        def _():
            acc_ref[...] += part

        @pl.when(k == nk - 1)
        def _():
            res = acc_ref[...]
            if has_add:
                res = res + add_ref[...].astype(F32)
            o_ref[...] = res.astype(o_ref.dtype)

    ins = [a, b] + ([add] if has_add else [])
    in_specs = [a_spec, b_spec] + ([o_spec] if has_add else [])
    return pl.pallas_call(
        body, name=name, grid=(M // tm, N // tn, nk),
        in_specs=in_specs, out_specs=o_spec,
        out_shape=jax.ShapeDtypeStruct((M, N), out_dtype),
        scratch_shapes=[pltpu.VMEM((tm, tn), F32)],
        compiler_params=_params(("parallel", "parallel", "arbitrary")),
    )(*ins)


def _rowwise(name, fn, T, tT, rows=(), prevs=(), nexts=(), consts=(), outs=()):
    n = T // tT
    per8 = tT // SUBLANES
    in_specs, ins = [], []
    for arr in rows:
        in_specs.append(pl.BlockSpec((tT, arr.shape[1]), lambda i: (i, 0)))
        ins.append(arr)
    for arr in prevs:
        in_specs.append(pl.BlockSpec((SUBLANES, arr.shape[1]), lambda i: (jnp.maximum(i * per8 - 1, 0), 0)))
        ins.append(arr)
    for arr in nexts:
        in_specs.append(pl.BlockSpec((SUBLANES, arr.shape[1]),
                                     lambda i: (jnp.minimum((i + 1) * per8, T // SUBLANES - 1), 0)))
        ins.append(arr)
    for arr in consts:
        in_specs.append(pl.BlockSpec(arr.shape, lambda i, nd=arr.ndim: (0,) * nd))
        ins.append(arr)
    out_specs, out_shapes = [], []
    for o in outs:
        if o[0] == "row":
            out_specs.append(pl.BlockSpec((tT, o[1]), lambda i: (i, 0)))
            out_shapes.append(jax.ShapeDtypeStruct((T, o[1]), o[2]))
        else:
            out_specs.append(pl.BlockSpec(o[1], lambda i: (0, 0)))
            out_shapes.append(jax.ShapeDtypeStruct(o[1], F32))
    nr, npv, nnx, nc = len(rows), len(prevs), len(nexts), len(consts)
    n_in = nr + npv + nnx + nc

    def body(*refs):
        i = pl.program_id(0)
        vals = [r[...] for r in refs[:n_in]]
        res = fn(i, n, vals[:nr], vals[nr:nr + npv], vals[nr + npv:nr + npv + nnx], vals[nr + npv + nnx:])
        for o, o_ref, val in zip(outs, refs[n_in:], res, strict=True):
            if o[0] == "row":
                o_ref[...] = val.astype(o_ref.dtype)
            else:
                @pl.when(i == 0)
                def _(o_ref=o_ref, val=val):
                    o_ref[...] = val.astype(F32)

                @pl.when(i > 0)
                def _(o_ref=o_ref, val=val):
                    o_ref[...] += val.astype(F32)

    res = pl.pallas_call(
        body, name=name, grid=(n,), in_specs=in_specs, out_specs=out_specs, out_shape=out_shapes,
        compiler_params=_params(("arbitrary",)),
    )(*ins)
    return list(res)


def _shift_down(x, prev8, i, s):
    rolled = pltpu.roll(x, s, 0)
    head = pltpu.roll(prev8, s, 0)
    head = jnp.where(i == 0, jnp.zeros_like(head), head)
    rid = lax.broadcasted_iota(jnp.int32, head.shape, 0)
    first = jnp.where(rid < s, head, rolled[:SUBLANES])
    return jnp.concatenate([first, rolled[SUBLANES:]], axis=0)


def _shift_up(x, next8, i, n, s):
    tT = x.shape[0]
    rolled = pltpu.roll(x, tT - s, 0)
    tail = pltpu.roll(next8, SUBLANES - s, 0)
    tail = jnp.where(i == n - 1, jnp.zeros_like(tail), tail)
    rid = lax.broadcasted_iota(jnp.int32, tail.shape, 0)
    last = jnp.where(rid >= SUBLANES - s, tail, rolled[tT - SUBLANES:])
    return jnp.concatenate([rolled[:tT - SUBLANES], last], axis=0)


def _colsum(x):
    return jnp.sum(x, axis=0, keepdims=True)


def _segsum(x, bd):
    return jnp.dot(x, bd, precision=lax.Precision.HIGHEST, preferred_element_type=F32)


def _block_diag_ones(width, seg):
    idx = np.arange(width) // seg
    return jnp.asarray((idx[:, None] == idx[None, :]).astype(np.float32))


def _sigmoid(z):
    return 1.0 / (1.0 + jnp.exp(-z))


def _softplus(z):
    return jnp.maximum(z, 0.0) + jnp.log(1.0 + jnp.exp(-jnp.abs(z)))


def _rms_fwd(x, g):
    r = lax.rsqrt(jnp.mean(x * x, axis=-1, keepdims=True) + NORM_EPS)
    return x * r * g


def _rms_bwd(x, g, dy):
    r = lax.rsqrt(jnp.mean(x * x, axis=-1, keepdims=True) + NORM_EPS)
    gdy = dy * g
    dx = r * (gdy - x * (r * r) * jnp.mean(x * gdy, axis=-1, keepdims=True))
    return dx, dy * x * r


GELU_C = math.sqrt(2.0 / math.pi)


def _gelu(x):
    return 0.5 * x * (1.0 + jnp.tanh(GELU_C * (x + 0.044715 * x * x * x)))


def _gelu_grad(x):
    th = jnp.tanh(GELU_C * (x + 0.044715 * x * x * x))
    return 0.5 * (1.0 + th) + 0.5 * x * (1.0 - th * th) * GELU_C * (1.0 + 3.0 * 0.044715 * x * x)


RW_CHUNK = 64
NN = (((1,), (0,)), ((), ()))
NT = (((1,), (1,)), ((), ()))
TN = (((0,), (0,)), ((), ()))


def _hdot(a, b, dims):
    return lax.dot_general(a, b, dims, precision=lax.Precision.HIGH, preferred_element_type=F32)


def _chunk_masks():
    ti = lax.broadcasted_iota(jnp.int32, (RW_CHUNK, RW_CHUNK), 0)
    tj = lax.broadcasted_iota(jnp.int32, (RW_CHUNK, RW_CHUNK), 1)
    return tj <= ti, tj < ti, (ti == tj).astype(F32)


def _head(x, h):
    return x[:, h * RW_HEAD_DIM:(h + 1) * RW_HEAD_DIM]


def _heads(fn):
    return [fn(h) for h in range(RW_HEADS)]


def _chunk_rows(r, lw, k, a, b, incl_f):
    c = _hdot(incl_f, lw, NN)
    e_prev, e_neg, e_pos = jnp.exp(c - lw), jnp.exp(-c), jnp.exp(c)
    return dict(At=a * e_prev, Bt=b * e_neg, Kt=k * e_neg, Rt=r * e_pos, e_prev=e_prev, e_neg=e_neg, e_pos=e_pos)


def _chunk_coeffs(q, incl, strict):
    A1 = _heads(lambda h: jnp.where(strict, _hdot(_head(q["At"], h), _head(q["Bt"], h), NT), 0.0))
    A2 = _heads(lambda h: jnp.where(strict, _hdot(_head(q["At"], h), _head(q["Kt"], h), NT), 0.0))
    W1 = _heads(lambda h: jnp.where(incl, _hdot(_head(q["Rt"], h), _head(q["Bt"], h), NT), 0.0))
    W2 = _heads(lambda h: jnp.where(incl, _hdot(_head(q["Rt"], h), _head(q["Kt"], h), NT), 0.0))
    return A1, A2, W1, W2


def _rwkv_chunk_prep(r, lw, k, a, b, v):
    T = r.shape[0]
    nC = T // RW_CHUNK
    H, N = RW_HEADS, RW_HEAD_DIM

    def body(r_ref, lw_ref, k_ref, a_ref, b_ref, v_ref,
             at_ref, bt_ref, kt_ref, rt_ref, a2v_ref, w2v_ref, ti_ref, w1_ref, pl_ref):
        incl, strict, eye = _chunk_masks()
        q = _chunk_rows(r_ref[...], lw_ref[...], k_ref[...], a_ref[...], b_ref[...], incl.astype(F32))
        at_ref[...], bt_ref[...], kt_ref[...], rt_ref[...] = q["At"], q["Bt"], q["Kt"], q["Rt"]
        pl_ref[0] = jnp.broadcast_to(q["e_pos"][RW_CHUNK - 1:RW_CHUNK, :], (SUBLANES, RW_WIDTH))
        A1, A2, W1, W2 = _chunk_coeffs(q, incl, strict)
        V = v_ref[...]
        a2v_ref[...] = jnp.concatenate(_heads(lambda h: _hdot(A2[h], _head(V, h), NN)), axis=1)
        w2v_ref[...] = jnp.concatenate(_heads(lambda h: _hdot(W2[h], _head(V, h), NN)), axis=1)
        tinv, pw = [eye + m for m in A1], A1
        for _ in range(5):
            pw = [_hdot(m, m, NN) for m in pw]
            tinv = [t + _hdot(t, m, NN) for t, m in zip(tinv, pw, strict=True)]
        for h in range(H):
            ti_ref[0, h] = tinv[h]
            w1_ref[0, h] = W1[h]

    row_spec = pl.BlockSpec((RW_CHUNK, RW_WIDTH), lambda n: (n, 0))
    st_spec = pl.BlockSpec((1, H, N, N), lambda n: (n, 0, 0, 0))
    row_shape = jax.ShapeDtypeStruct((T, RW_WIDTH), F32)
    st_shape = jax.ShapeDtypeStruct((nC, H, N, N), F32)
    return pl.pallas_call(
        body, name="rwkv_chunk_prep", grid=(nC,),
        in_specs=[row_spec] * 6,
        out_specs=[row_spec] * 6 + [st_spec, st_spec, pl.BlockSpec((1, SUBLANES, RW_WIDTH), lambda n: (n, 0, 0))],
        out_shape=[row_shape] * 6 + [st_shape, st_shape, jax.ShapeDtypeStruct((nC, SUBLANES, RW_WIDTH), F32)],
        compiler_params=_params(("parallel",)),
    )(r, lw, k, a, b, v)


def _rwkv_chunk_fwd(v, at, bt, kt, rt, a2v, w2v, tinv, w1, plast):
    T = v.shape[0]
    nC = T // RW_CHUNK
    H, N = RW_HEADS, RW_HEAD_DIM

    def body(v_ref, at_ref, bt_ref, kt_ref, rt_ref, a2v_ref, w2v_ref, ti_ref, w1_ref, pl_ref,
             y_ref, sa_ref, s0_ref, S_ref):
        @pl.when(pl.program_id(0) == 0)
        def _():
            S_ref[...] = jnp.zeros_like(S_ref)

        V, At, Bt, Kt, Rt = v_ref[...], at_ref[...], bt_ref[...], kt_ref[...], rt_ref[...]
        A2V, W2V, p_last = a2v_ref[...], w2v_ref[...], pl_ref[0, 0:1, :]
        S0 = _heads(lambda h: S_ref[h])
        for h in range(H):
            s0_ref[0, h] = S0[h]
        Z = _heads(lambda h: _hdot(_head(At, h), S0[h], NT) + _head(A2V, h))
        Sa = _heads(lambda h: _hdot(ti_ref[0, h], Z[h], NN))
        X = _heads(lambda h: S0[h] + _hdot(Sa[h], _head(Bt, h), TN) + _hdot(_head(V, h), _head(Kt, h), TN))
        for h in range(H):
            S_ref[h] = X[h] * _head(p_last, h)
        Y = _heads(lambda h: _hdot(_head(Rt, h), S0[h], NT) + _hdot(w1_ref[0, h], Sa[h], NN) + _head(W2V, h))
        y_ref[...] = jnp.concatenate(Y, axis=1)
        sa_ref[...] = jnp.concatenate(Sa, axis=1)

    row_spec = pl.BlockSpec((RW_CHUNK, RW_WIDTH), lambda n: (n, 0))
    st_spec = pl.BlockSpec((1, H, N, N), lambda n: (n, 0, 0, 0))
    row_shape = jax.ShapeDtypeStruct((T, RW_WIDTH), F32)
    return pl.pallas_call(
        body, name="rwkv_chunk_fwd", grid=(nC,),
        in_specs=[row_spec] * 7 + [st_spec, st_spec, pl.BlockSpec((1, SUBLANES, RW_WIDTH), lambda n: (n, 0, 0))],
        out_specs=[row_spec, row_spec, st_spec],
        out_shape=[row_shape, row_shape, jax.ShapeDtypeStruct((nC, H, N, N), F32)],
        scratch_shapes=[pltpu.VMEM((H, N, N), F32)],
        compiler_params=_params(("arbitrary",)),
    )(v, at, bt, kt, rt, a2v, w2v, tinv, w1, plast)


def _rwkv_chunk_bwd(r, lw, k, a, b, v, dy, s0, tinv, sa):
    T = r.shape[0]
    nC = T // RW_CHUNK
    H, N = RW_HEADS, RW_HEAD_DIM

    def body(r_ref, lw_ref, k_ref, a_ref, b_ref, v_ref, dy_ref, s0_ref, ti_ref, sa_ref,
             dr_ref, dlw_ref, dk_ref, da_ref, db_ref, dv_ref, dS_ref):
        @pl.when(pl.program_id(0) == 0)
        def _():
            dS_ref[...] = jnp.zeros_like(dS_ref)

        incl, strict, _ = _chunk_masks()
        incl_f = incl.astype(F32)
        q = _chunk_rows(r_ref[...], lw_ref[...], k_ref[...], a_ref[...], b_ref[...], incl_f)
        At, Bt, Kt, Rt = q["At"], q["Bt"], q["Kt"], q["Rt"]
        _, A2, W1, W2 = _chunk_coeffs(q, incl, strict)
        V, dY, Sa = v_ref[...], dy_ref[...], sa_ref[...]
        hd = _head
        p_last = q["e_pos"][RW_CHUNK - 1:RW_CHUNK, :]
        S0 = _heads(lambda h: s0_ref[0, h])
        G = _heads(lambda h: dS_ref[h] * hd(p_last, h))
        X = _heads(lambda h: S0[h] + _hdot(hd(Sa, h), hd(Bt, h), TN) + _hdot(hd(V, h), hd(Kt, h), TN))
        dc_last = jnp.concatenate(_heads(lambda h: jnp.sum(G[h] * X[h], axis=0, keepdims=True)), axis=1)
        dSa = _heads(lambda h: _hdot(hd(Bt, h), G[h], NT) + _hdot(W1[h], hd(dY, h), TN))
        dZ = _heads(lambda h: _hdot(ti_ref[0, h], dSa[h], TN))
        for h in range(H):
            dS_ref[h] = G[h] + _hdot(dZ[h], hd(At, h), TN) + _hdot(hd(dY, h), hd(Rt, h), TN)
        dA1 = _heads(lambda h: jnp.where(strict, _hdot(dZ[h], hd(Sa, h), NT), 0.0))
        dA2 = _heads(lambda h: jnp.where(strict, _hdot(dZ[h], hd(V, h), NT), 0.0))
        dW1 = _heads(lambda h: jnp.where(incl, _hdot(hd(dY, h), hd(Sa, h), NT), 0.0))
        dW2 = _heads(lambda h: jnp.where(incl, _hdot(hd(dY, h), hd(V, h), NT), 0.0))
        cat = lambda fn: jnp.concatenate(_heads(fn), axis=1)
        dV = cat(lambda h: _hdot(A2[h], dZ[h], TN) + _hdot(W2[h], hd(dY, h), TN) + _hdot(hd(Kt, h), G[h], NT))
        dAt = cat(lambda h: _hdot(dA1[h], hd(Bt, h), NN) + _hdot(dA2[h], hd(Kt, h), NN) + _hdot(dZ[h], S0[h], NN))
        dBt = cat(lambda h: _hdot(dA1[h], hd(At, h), TN) + _hdot(dW1[h], hd(Rt, h), TN) + _hdot(hd(Sa, h), G[h], NN))
        dKt = cat(lambda h: _hdot(dA2[h], hd(At, h), TN) + _hdot(dW2[h], hd(Rt, h), TN) + _hdot(hd(V, h), G[h], NN))
        dRt = cat(lambda h: _hdot(hd(dY, h), S0[h], NN) + _hdot(dW1[h], hd(Bt, h), NN) + _hdot(dW2[h], hd(Kt, h), NN))
        last_row = lax.broadcasted_iota(jnp.int32, (RW_CHUNK, RW_WIDTH), 0) == RW_CHUNK - 1
        dc_prev = dAt * At
        dc = dc_prev + dRt * Rt - dBt * Bt - dKt * Kt + jnp.where(last_row, dc_last, 0.0)
        dr_ref[...] = dRt * q["e_pos"]
        dlw_ref[...] = _hdot(incl_f, dc, TN) - dc_prev
        dk_ref[...] = dKt * q["e_neg"]
        da_ref[...] = dAt * q["e_prev"]
        db_ref[...] = dBt * q["e_neg"]
        dv_ref[...] = dV

    rev = lambda n: nC - 1 - n
    row_spec = pl.BlockSpec((RW_CHUNK, RW_WIDTH), lambda n: (rev(n), 0))
    st_spec = pl.BlockSpec((1, H, N, N), lambda n: (rev(n), 0, 0, 0))
    row_shape = jax.ShapeDtypeStruct((T, RW_WIDTH), F32)
    return pl.pallas_call(
        body, name="rwkv_chunk_bwd", grid=(nC,),
        in_specs=[row_spec] * 7 + [st_spec, st_spec, row_spec], out_specs=[row_spec] * 6,
        out_shape=[row_shape] * 6, scratch_shapes=[pltpu.VMEM((H, N, N), F32)],
        compiler_params=_params(("arbitrary",)),
    )(r, lw, k, a, b, v, dy, s0, tinv, sa)


def _alibi_slope(head):
    return float(np.float32(2.0 ** (-8.0 * (head + 1) / ATT_HEADS)))


def _att_masks():
    qi = lax.broadcasted_iota(jnp.int32, (ATT_BLOCK, ATT_BLOCK), 0)
    kj = lax.broadcasted_iota(jnp.int32, (ATT_BLOCK, ATT_BLOCK), 1)
    return qi, kj


NEG = -1e30


def _att_logits(q, k, slope_d, steps, valid):
    s = lax.dot_general(q.astype(BF16), k.astype(BF16), (((1,), (1,)), ((), ())),
                        preferred_element_type=F32) * (ATT_HEAD_DIM ** -0.5)
    return jnp.where(valid, s - slope_d * steps.astype(F32), NEG)


def _att_fwd(p_att, g):
    T = p_att.shape[0]
    d = ATT_GROUP_DILATION[g]
    nbs = T // (ATT_BLOCK * d)
    W = ATT_GROUP_WIDTH
    view = p_att.reshape(T // d, d * 3 * ATT_WIDTH)
    cols = 3 * ATT_WIDTH // W

    def body(q_ref, kc_ref, kp_ref, vc_ref, vp_ref, o_ref, l_ref):
        m = pl.program_id(1)
        qi, kj = _att_masks()
        has_prev = m > 0
        outs, lses = [], []
        for j in range(4):
            sl = slice(j * ATT_HEAD_DIM, (j + 1) * ATT_HEAD_DIM)
            slope_d = _alibi_slope(4 * g + j) * d
            q = q_ref[:, sl]
            lc = _att_logits(q, kc_ref[:, sl], slope_d, qi - kj, kj <= qi)
            lp = _att_logits(q, kp_ref[:, sl], slope_d, qi - kj + ATT_BLOCK, (kj >= qi) & has_prev)
            mx = jnp.maximum(jnp.max(lc, axis=1, keepdims=True), jnp.max(lp, axis=1, keepdims=True))
            ec, ep = jnp.exp(lc - mx), jnp.exp(lp - mx)
            den = jnp.sum(ec, axis=1, keepdims=True) + jnp.sum(ep, axis=1, keepdims=True)
            lse = mx + jnp.log(den)
            pc, pp = jnp.exp(lc - lse), jnp.exp(lp - lse)
            o = (jnp.dot(pc.astype(BF16), vc_ref[:, sl].astype(BF16), preferred_element_type=F32)
                 + jnp.dot(pp.astype(BF16), vp_ref[:, sl].astype(BF16), preferred_element_type=F32))
            outs.append(o)
            lses.append(jnp.broadcast_to(lse, (ATT_BLOCK, ATT_HEAD_DIM)))
        o_ref[...] = jnp.concatenate(outs, axis=1)
        l_ref[...] = jnp.concatenate(lses, axis=1)

    def spec(col0, prev):
        if prev:
            return pl.BlockSpec((ATT_BLOCK, W), lambda r, m: (jnp.maximum(m - 1, 0), r * cols + col0 + g))
        return pl.BlockSpec((ATT_BLOCK, W), lambda r, m: (m, r * cols + col0 + g))

    o_spec = pl.BlockSpec((ATT_BLOCK, W), lambda r, m: (m, r))
    o, l = pl.pallas_call(
        body, name=f"att_fwd_g{g}", grid=(d, nbs),
        in_specs=[spec(0, False), spec(3, False), spec(3, True), spec(6, False), spec(6, True)],
        out_specs=[o_spec, o_spec],
        out_shape=[jax.ShapeDtypeStruct((T // d, d * W), F32)] * 2,
        compiler_params=_params(("parallel", "arbitrary")),
    )(view, view, view, view, view)
    return o.reshape(T, W), l.reshape(T, W)


def _att_bwd(p_att, o, l, do, dl, g):
    T = p_att.shape[0]
    d = ATT_GROUP_DILATION[g]
    nbs = T // (ATT_BLOCK * d)
    W = ATT_GROUP_WIDTH
    view = p_att.reshape(T // d, d * 3 * ATT_WIDTH)
    cols = 3 * ATT_WIDTH // W
    ov, lv, dov, dlv = (z.reshape(T // d, d * W) for z in (o, l, do, dl))
    scale = ATT_HEAD_DIM ** -0.5

    def body(q_ref, k_ref, v_ref, o_ref, l_ref, do_ref, dl_ref,
             qn_ref, on_ref, ln_ref, don_ref, dln_ref, dq_ref, dk_ref, dv_ref, carry_ref):
        m = pl.program_id(1)
        qi, kj = _att_masks()
        has_next = m < nbs - 1

        @pl.when(m == 0)
        def _():
            carry_ref[...] = jnp.zeros_like(carry_ref)

        dqs, dks, dvs, carries = [], [], [], []
        for j in range(4):
            sl = slice(j * ATT_HEAD_DIM, (j + 1) * ATT_HEAD_DIM)
            slope_d = _alibi_slope(4 * g + j) * d
            k, v = k_ref[:, sl], v_ref[:, sl]
            kb, vb = k.astype(BF16), v.astype(BF16)

            def side(q, o_, lse, do_, dlse, steps, valid):
                lg = _att_logits(q, k, slope_d, steps, valid)
                p = jnp.exp(lg - lse)
                dp = lax.dot_general(do_.astype(BF16), vb, (((1,), (1,)), ((), ())), preferred_element_type=F32)
                dsum = jnp.sum(do_ * o_, axis=1, keepdims=True)
                ds = p * (dp - dsum + dlse)
                dv_ = jnp.dot(p.T.astype(BF16), do_.astype(BF16), preferred_element_type=F32)
                dk_ = jnp.dot(ds.T.astype(BF16), q.astype(BF16), preferred_element_type=F32) * scale
                dq_ = jnp.dot(ds.astype(BF16), kb, preferred_element_type=F32) * scale
                return dq_, dk_, dv_

            dq_c, dk_c, dv_c = side(q_ref[:, sl], o_ref[:, sl], l_ref[:, sl][:, :1], do_ref[:, sl],
                                    dl_ref[:, sl][:, :1], qi - kj, kj <= qi)
            dq_n, dk_n, dv_n = side(qn_ref[:, sl], on_ref[:, sl], ln_ref[:, sl][:, :1], don_ref[:, sl],
                                    dln_ref[:, sl][:, :1], qi - kj + ATT_BLOCK, (kj >= qi) & has_next)
            dqs.append(dq_c)
            carries.append(dq_n)
            dks.append(dk_c + dk_n)
            dvs.append(dv_c + dv_n)
        dq_ref[...] = jnp.concatenate(dqs, axis=1) + carry_ref[...]
        carry_ref[...] = jnp.concatenate(carries, axis=1)
        dk_ref[...] = jnp.concatenate(dks, axis=1)
        dv_ref[...] = jnp.concatenate(dvs, axis=1)

    nxt = lambda m: jnp.minimum(m + 1, nbs - 1)
    cur_p = lambda col0: pl.BlockSpec((ATT_BLOCK, W), lambda r, m: (m, r * cols + col0 + g))
    nxt_p = lambda col0: pl.BlockSpec((ATT_BLOCK, W), lambda r, m: (nxt(m), r * cols + col0 + g))
    cur_o = pl.BlockSpec((ATT_BLOCK, W), lambda r, m: (m, r))
    nxt_o = pl.BlockSpec((ATT_BLOCK, W), lambda r, m: (nxt(m), r))
    dq, dk, dv = pl.pallas_call(
        body, name=f"att_bwd_g{g}", grid=(d, nbs),
        in_specs=[cur_p(0), cur_p(3), cur_p(6), cur_o, cur_o, cur_o, cur_o,
                  nxt_p(0), nxt_o, nxt_o, nxt_o, nxt_o],
        out_specs=[cur_o, cur_o, cur_o],
        out_shape=[jax.ShapeDtypeStruct((T // d, d * W), F32)] * 3,
        scratch_shapes=[pltpu.VMEM((ATT_BLOCK, W), F32)],
        compiler_params=_params(("parallel", "arbitrary")),
    )(view, view, view, ov, lv, dov, dlv, view, ov, lv, dov, dlv)
    return dq.reshape(T, W), dk.reshape(T, W), dv.reshape(T, W)


RKV = 3 * RW_WIDTH
WA = 128
XG = 160
RW_COLS = RKV + WA + XG


def _local_step(x, p, W, target):
    T = x.shape[0]
    tT = 256
    bd512 = _block_diag_ones(RW_WIDTH, RW_HEAD_DIM)
    bd256 = _block_diag_ones(ATT_GROUP_WIDTH, ATT_HEAD_DIM)
    G = {}

    w_in = W["w_in"]
    w_rkv, w_wa, w_xg, w_att = (w_in[:, :RKV], w_in[:, RKV:RKV + WA], w_in[:, RKV + WA:RW_COLS],
                                w_in[:, RW_COLS:])
    mu = W["rw_mu"]
    mu_rkv, mu_wa, mu_xg = mu[:, :RKV], mu[:, RKV:RKV + WA], mu[:, RKV + WA:]
    zpad = jnp.zeros((64, RW_WIDTH), W["rw_w_up"].dtype)
    w_up_pad = jnp.concatenate([W["rw_w_up"], zpad], axis=0)
    a_up_pad = jnp.concatenate([zpad, W["rw_a_up"]], axis=0)
    r_k = W["rw_r_k"].reshape(1, RW_WIDTH)

    (h,) = _rowwise("norm_mix", lambda i, n, r, pv, nx, c: [_rms_fwd(r[0], c[0])], T, tT,
                    rows=[x], consts=[W["g_mix"]], outs=[("row", D_MODEL, BF16)])
    p_rkv = _mm("proj_rkv", h, w_rkv, "nn")
    p_wa = _mm("proj_wa", h, w_wa, "nn")
    p_xg = _mm("proj_xg", h, w_xg, "nn")
    p_att = _mm("proj_att", h, w_att, "nn", tn=768)
    z_gate = _mm("proj_gate", h, W["w_gate"], "nn")

    def rw_pre_core(i, rows, prevs, consts):
        prkv, pwa, pxg = rows[:3]
        (mrkv, mwa, mxg, w0, a0, k_k, k_a, wup, aup, gup, bd) = consts[:11]
        m_rkv = prkv + (_shift_down(prkv, prevs[0], i, 1) - prkv) * mrkv
        m_wa = pwa + (_shift_down(pwa, prevs[1], i, 1) - pwa) * mwa
        m_xg = pxg + (_shift_down(pxg, prevs[2], i, 1) - pxg) * mxg
        r, k, v = m_rkv[:, :RW_WIDTH], m_rkv[:, RW_WIDTH:2 * RW_WIDTH], m_rkv[:, 2 * RW_WIDTH:]
        tw = jnp.tanh(m_wa)
        lw = w0 + jnp.dot(tw.astype(BF16), wup.astype(BF16), preferred_element_type=F32)
        wlog = -_softplus(-lw) - 0.5
        log_decay = -jnp.exp(wlog)
        a = _sigmoid(a0 + jnp.dot(m_wa.astype(BF16), aup.astype(BF16), preferred_element_type=F32))
        sg = _sigmoid(m_xg)
        gate = jnp.dot(sg.astype(BF16), gup.astype(BF16), preferred_element_type=F32)
        kkp = k * k_k
        nrm = jnp.sqrt(_segsum(kkp * kkp, bd))
        nrm_c = jnp.maximum(nrm, 1e-12)
        kk = kkp / nrm_c
        k2 = k * (1.0 + (a - 1.0) * k_a)
        return dict(r=r, k=k, v=v, tw=tw, lw=lw, wlog=wlog, log_decay=log_decay, a=a, sg=sg, gate=gate, kkp=kkp,
                    nrm=nrm, nrm_c=nrm_c, kk=kk, k2=k2, m_rkv=m_rkv, m_wa=m_wa, m_xg=m_xg)

    pre_consts = [mu_rkv, mu_wa, mu_xg, W["rw_w0"], W["rw_a0"], W["rw_k_k"], W["rw_k_a"],
                  w_up_pad, a_up_pad, W["rw_g_up"], bd512]

    def rw_pre(i, n, rows, prevs, nexts, consts):
        q = rw_pre_core(i, rows, prevs, consts)
        return [q["r"], q["log_decay"], q["k2"], q["v"], -q["kk"], q["kk"] * q["a"], q["gate"]]

    r_s, w_s, k_s, v_s, a_s, b_s, gate_s = _rowwise(
        "rwkv_pre", rw_pre, T, tT, rows=[p_rkv, p_wa, p_xg], prevs=[p_rkv, p_wa, p_xg], consts=pre_consts,
        outs=[("row", RW_WIDTH, F32)] * 7)
    at_s, bt_s, kt_s, rt_s, a2v_s, w2v_s, tinv_s, w1_s, plast_s = _rwkv_chunk_prep(r_s, w_s, k_s, a_s, b_s, v_s)
    y_scan, sa_s, s0_s = _rwkv_chunk_fwd(v_s, at_s, bt_s, kt_s, rt_s, a2v_s, w2v_s, tinv_s, w1_s, plast_s)

    def rw_post_core(rows, consts):
        y, r, k2, v, gate = rows[:5]
        ln_g, ln_b, rk, bd = consts[:4]
        mean = _segsum(y, bd) * (1.0 / RW_HEAD_DIM)
        yc = y - mean
        var = _segsum(yc * yc, bd) * (1.0 / RW_HEAD_DIM)
        rstd = lax.rsqrt(var + RW_LN_EPS)
        yn = yc * rstd
        s = _segsum(r * k2 * rk, bd)
        return dict(yn=yn, rstd=rstd, s=s, pre=yn * ln_g + ln_b + s * v)

    post_consts = [W["rw_ln_g"], W["rw_ln_b"], r_k, bd512]
    (y_a,) = _rowwise("rwkv_post", lambda i, n, r, pv, nx, c: [rw_post_core(r, c)["pre"] * r[4]], T, tT,
                      rows=[y_scan, r_s, k_s, v_s, gate_s], consts=post_consts, outs=[("row", RW_WIDTH, BF16)])

    att = [_att_fwd(p_att, g) for g in range(3)]

    def comb_weights(ls):
        mx = jnp.maximum(jnp.maximum(ls[0], ls[1]), ls[2])
        es = [jnp.exp(l - mx) for l in ls]
        den = es[0] + es[1] + es[2]
        return [e / den for e in es]

    def att_comb(i, n, rows, pv, nx, c):
        wts = comb_weights(rows[3:6])
        return [wts[0] * rows[0] + wts[1] * rows[1] + wts[2] * rows[2]]

    (y_b,) = _rowwise("att_combine", att_comb, T, tT, rows=[att[0][0], att[1][0], att[2][0], att[0][1], att[1][1],
                                                            att[2][1]], outs=[("row", ATT_GROUP_WIDTH, BF16)])

    br_a = _mm("branch_a", y_a, W["w_branch_a"], "nn")
    br_b = _mm("branch_b", y_b, W["w_branch_b"], "nn")

    def merge(i, n, rows, pv, nx, c):
        gates = _sigmoid(rows[0] + c[0])
        return [gates[:, :D_MODEL] * rows[1] + gates[:, D_MODEL:] * rows[2]]

    (merged,) = _rowwise("merge", merge, T, tT, rows=[z_gate, br_a, br_b], consts=[W["b_gate"]],
                         outs=[("row", D_MODEL, BF16)])
    x1 = _mm("mix_out", merged, W["w_out"], "nn", add=x)

    (h2,) = _rowwise("norm_ffn", lambda i, n, r, pv, nx, c: [_rms_fwd(r[0], c[0])], T, tT,
                     rows=[x1], consts=[W["g_ffn"]], outs=[("row", D_MODEL, BF16)])
    u = _mm("ffn_up", h2, W["w_up"], "nn")

    def conv_core(i, rows, prevs, consts):
        uu, cw, cb = rows[0], consts[0], consts[1]
        u1 = _shift_down(uu, prevs[0], i, 1)
        u2 = _shift_down(uu, prevs[0], i, 2)
        uc = cb + cw[0:1] * uu + cw[1:2] * u1 + cw[2:3] * u2
        return uc[:, :D_FF], uc[:, D_FF:], u1, u2

    def glu(i, n, rows, prevs, nx, consts):
        gate, val, _, _ = conv_core(i, rows, prevs, consts)
        return [_gelu(gate) * val]

    tF = 128
    (act,) = _rowwise("conv_glu", glu, T, tF, rows=[u], prevs=[u], consts=[W["conv_w"], W["conv_b"]],
                      outs=[("row", D_FF, BF16)])
    x2 = _mm("ffn_down", act, W["w_down"], "nn", add=x1)

    (h3,) = _rowwise("norm_ple", lambda i, n, r, pv, nx, c: [_rms_fwd(r[0], c[0])], T, tT,
                     rows=[x2], consts=[W["g_ple"]], outs=[("row", D_MODEL, BF16)])
    z_ple = _mm("ple_gate", h3, W["w_ple_gate"], "nn")
    e_ple = _mm("ple_emb", p, W["w_ple"], "nn")

    def head(i, n, rows, pv, nx, consts):
        x2_, z, e, tgt = rows
        pg = _sigmoid(z)
        x3 = x2_ + pg * e
        y = _rms_fwd(x3, consts[0])
        err = y - tgt
        loss = 0.5 * jnp.sum(jnp.sum(err * err, axis=1, keepdims=True) * (1.0 / D_MODEL), axis=0, keepdims=True)
        dy = err * (1.0 / D_MODEL)
        dx3, dgf = _rms_bwd(x3, consts[0], dy)
        return [dx3, dx3 * pg, dx3 * e * pg * (1.0 - pg), jnp.broadcast_to(loss, (1, LANES)), _colsum(dgf)]

    dx3, de, dz, loss_acc, G["g_final"] = _rowwise(
        "loss_head", head, T, tT, rows=[x2, z_ple, e_ple, target], consts=[W["g_final"].reshape(1, D_MODEL)],
        outs=[("row", D_MODEL, F32), ("row", D_MODEL, BF16), ("row", D_MODEL, BF16), ("acc", (1, LANES)),
              ("acc", (1, D_MODEL))])
    G["w_ple"] = _mm("d_w_ple", p, de, "tn")
    G["w_ple_gate"] = _mm("d_w_ple_gate", h3, dz, "tn")
    dh3 = _mm("d_h3", dz, W["w_ple_gate"], "nt")

    def norm_bwd(i, n, rows, pv, nx, consts):
        dx, dg = _rms_bwd(rows[0], consts[0], rows[1])
        return [rows[2] + dx, _colsum(dg)]

    dx2, G["g_ple"] = _rowwise("d_norm_ple", norm_bwd, T, tT, rows=[x2, dh3, dx3], consts=[W["g_ple"]],
                               outs=[("row", D_MODEL, F32), ("acc", (1, D_MODEL))])

    dact = _mm("d_act", dx2, W["w_down"], "nt")
    G["w_down"] = _mm("d_w_down", act, dx2, "tn")

    def glu_bwd(i, n, rows, prevs, nx, consts):
        gate, val, u1, u2 = conv_core(i, rows, prevs, consts)
        da = rows[1]
        duc = jnp.concatenate([da * val * _gelu_grad(gate), da * _gelu(gate)], axis=1)
        dcw = jnp.concatenate([_colsum(duc * rows[0]), _colsum(duc * u1), _colsum(duc * u2)], axis=0)
        return [duc, _colsum(duc), dcw]

    duc, G["conv_b"], G["conv_w"] = _rowwise(
        "d_conv_glu", glu_bwd, T, tF, rows=[u, dact], prevs=[u], consts=[W["conv_w"], W["conv_b"]],
        outs=[("row", 2 * D_FF, F32), ("acc", (1, 2 * D_FF)), ("acc", (3, 2 * D_FF))])

    def conv_bwd(i, n, rows, pv, nexts, consts):
        cw = consts[0]
        return [cw[0:1] * rows[0] + cw[1:2] * _shift_up(rows[0], nexts[0], i, n, 1)
                + cw[2:3] * _shift_up(rows[0], nexts[0], i, n, 2)]

    (du,) = _rowwise("d_conv", conv_bwd, T, tF, rows=[duc], nexts=[duc], consts=[W["conv_w"]],
                     outs=[("row", 2 * D_FF, BF16)])
    G["w_up"] = _mm("d_w_up", h2, du, "tn")
    dh2 = _mm("d_h2", du, W["w_up"], "nt")
    dx1, G["g_ffn"] = _rowwise("d_norm_ffn", norm_bwd, T, tT, rows=[x1, dh2, dx2], consts=[W["g_ffn"]],
                               outs=[("row", D_MODEL, F32), ("acc", (1, D_MODEL))])

    dmerged = _mm("d_merged", dx1, W["w_out"], "nt")
    G["w_out"] = _mm("d_w_out", merged, dx1, "tn")

    def merge_bwd(i, n, rows, pv, nx, consts):
        z, a_, b_, dm = rows
        gates = _sigmoid(z + consts[0])
        ga, gb = gates[:, :D_MODEL], gates[:, D_MODEL:]
        dz_ = jnp.concatenate([dm * a_ * ga * (1.0 - ga), dm * b_ * gb * (1.0 - gb)], axis=1)
        return [dm * ga, dm * gb, dz_, _colsum(dz_)]

    d_br_a, d_br_b, dz_gate, G["b_gate"] = _rowwise(
        "d_merge", merge_bwd, T, tT, rows=[z_gate, br_a, br_b, dmerged], consts=[W["b_gate"]],
        outs=[("row", D_MODEL, BF16), ("row", D_MODEL, BF16), ("row", 2 * D_MODEL, BF16), ("acc", (1, 2 * D_MODEL))])
    G["w_branch_a"] = _mm("d_w_branch_a", y_a, d_br_a, "tn")
    G["w_branch_b"] = _mm("d_w_branch_b", y_b, d_br_b, "tn")
    G["w_gate"] = _mm("d_w_gate", h, dz_gate, "tn")
    dy_a = _mm("d_y_a", d_br_a, W["w_branch_a"], "nt")
    dy_b = _mm("d_y_b", d_br_b, W["w_branch_b"], "nt")

    def att_comb_bwd(i, n, rows, pv, nx, consts):
        os_, ls, dy = rows[0:3], rows[3:6], rows[6]
        wts = comb_weights(ls)
        dws = [_segsum(dy * o_, consts[0]) for o_ in os_]
        mix = wts[0] * dws[0] + wts[1] * dws[1] + wts[2] * dws[2]
        return [wts[g_] * dy for g_ in range(3)] + [wts[g_] * (dws[g_] - mix) for g_ in range(3)]

    comb = _rowwise("d_att_combine", att_comb_bwd, T, tT,
                    rows=[att[0][0], att[1][0], att[2][0], att[0][1], att[1][1], att[2][1], dy_b], consts=[bd256],
                    outs=[("row", ATT_GROUP_WIDTH, F32)] * 6)
    dqkv = [_att_bwd(p_att, att[g][0], att[g][1], comb[g], comb[3 + g], g) for g in range(3)]
    dp_att = jnp.concatenate([dqkv[g][part] for part in range(3) for g in range(3)], axis=1).astype(BF16)

    def rw_post_bwd(i, n, rows, pv, nx, consts):
        y, r, k2, v, gate, dya = rows
        ln_g, ln_b, rk, bd = consts
        q = rw_post_core(rows, consts)
        dpre = dya * gate
        dgate = dya * q["pre"]
        dyn = dpre * ln_g
        inv = 1.0 / RW_HEAD_DIM
        dy_scan = q["rstd"] * (dyn - _segsum(dyn, bd) * inv - q["yn"] * (_segsum(dyn * q["yn"], bd) * inv))
        ds = _segsum(dpre * v, bd)
        return [dy_scan, dgate, ds * k2 * rk, ds * r * rk, dpre * q["s"],
                _colsum(dpre * q["yn"]), _colsum(dpre), _colsum(ds * r * k2)]

    dy_scan, dgate, dr_b, dk2_b, dv_b, G["rw_ln_g"], G["rw_ln_b"], d_rk = _rowwise(
        "d_rwkv_post", rw_post_bwd, T, tT, rows=[y_scan, r_s, k_s, v_s, gate_s, dy_a], consts=post_consts,
        outs=[("row", RW_WIDTH, F32)] * 5 + [("acc", (1, RW_WIDTH))] * 3)
    G["rw_r_k"] = d_rk.reshape(RW_HEADS, RW_HEAD_DIM)

    dr_s, dw_s, dk_s, da_s, db_s, dv_s = _rwkv_chunk_bwd(r_s, w_s, k_s, a_s, b_s, v_s, dy_scan, s0_s, tinv_s, sa_s)

    def rw_pre_bwd(i, n, rows, prevs, nx, consts):
        q = rw_pre_core(i, rows, prevs, consts)
        (mrkv, mwa, mxg, w0, a0, k_k, k_a, wup, aup, gup, bd) = consts
        dr, dlogdecay, dk2, dv, dav, dbv, dgate_ = rows[3:10]
        dr = dr + rows[10]
        dk2 = dk2 + rows[11]
        dv = dv + rows[12]
        a, k, kk = q["a"], q["k"], q["kk"]
        dk = dk2 * (1.0 + (a - 1.0) * k_a)
        da = dk2 * k * k_a + dbv * kk
        dkk = dbv * a - dav
        live = q["nrm"] > 1e-12
        dkkp = jnp.where(live, dkk - kk * _segsum(dkk * kk, bd), dkk) / q["nrm_c"]
        dk = dk + dkkp * k_k
        dlw = dlogdecay * q["log_decay"] * _sigmoid(-q["lw"])
        dla = da * a * (1.0 - a)
        nt = (((1,), (1,)), ((), ()))
        dtw = lax.dot_general(dlw.astype(BF16), wup.astype(BF16), nt, preferred_element_type=F32)
        dxa = lax.dot_general(dla.astype(BF16), aup.astype(BF16), nt, preferred_element_type=F32)
        dm_wa = dtw * (1.0 - q["tw"] * q["tw"]) + dxa
        dsg = lax.dot_general(dgate_.astype(BF16), gup.astype(BF16), nt, preferred_element_type=F32)
        dm_xg = dsg * q["sg"] * (1.0 - q["sg"])
        dm_rkv = jnp.concatenate([dr, dk, dv], axis=1)
        prkv, pwa, pxg = rows[:3]
        dmu = jnp.concatenate([_colsum(dm_rkv * (_shift_down(prkv, prevs[0], i, 1) - prkv)),
                               _colsum(dm_wa * (_shift_down(pwa, prevs[1], i, 1) - pwa)),
                               _colsum(dm_xg * (_shift_down(pxg, prevs[2], i, 1) - pxg))], axis=1)
        return [dm_rkv, dm_wa, dm_xg, dlw, dla, q["tw"], q["m_wa"], q["sg"], dmu,
                _colsum(dlw), _colsum(dla), _colsum(dkkp * k), _colsum(dk2 * k * (a - 1.0))]

    (dm_rkv, dm_wa, dm_xg, dlw, dla, tw_s, mwa_s, sg_s, G["rw_mu"], G["rw_w0"], G["rw_a0"], G["rw_k_k"],
     G["rw_k_a"]) = _rowwise(
        "d_rwkv_pre", rw_pre_bwd, T, tT,
        rows=[p_rkv, p_wa, p_xg, dr_s, dw_s, dk_s, dv_s, da_s, db_s, dgate, dr_b, dk2_b, dv_b],
        prevs=[p_rkv, p_wa, p_xg], consts=pre_consts,
        outs=[("row", RKV, F32), ("row", WA, F32), ("row", XG, F32), ("row", RW_WIDTH, BF16),
              ("row", RW_WIDTH, BF16), ("row", WA, BF16), ("row", WA, BF16), ("row", XG, BF16),
              ("acc", (1, RW_COLS))] + [("acc", (1, RW_WIDTH))] * 4)
    G["rw_w_up"] = _mm("d_rw_w_up", tw_s, dlw, "tn")[:64]
    G["rw_a_up"] = _mm("d_rw_a_up", mwa_s, dla, "tn")[64:]
    G["rw_g_up"] = _mm("d_rw_g_up", sg_s, dgate, "tn")

    def shift_bwd(i, n, rows, pv, nexts, consts):
        return [rows[j] * (1.0 - consts[j]) + _shift_up(rows[j], nexts[j], i, n, 1) * consts[j] for j in range(3)]

    dp_rkv, dp_wa, dp_xg = _rowwise(
        "d_token_shift", shift_bwd, T, tT, rows=[dm_rkv, dm_wa, dm_xg], nexts=[dm_rkv, dm_wa, dm_xg],
        consts=[mu_rkv, mu_wa, mu_xg], outs=[("row", RKV, BF16), ("row", WA, BF16), ("row", XG, BF16)])

    G["w_in"] = jnp.concatenate([_mm("d_w_rkv", h, dp_rkv, "tn"), _mm("d_w_wa", h, dp_wa, "tn"),
                                 _mm("d_w_xg", h, dp_xg, "tn"), _mm("d_w_att", h, dp_att, "tn", tn=768)], axis=1)
    dh = _mm("d_h_gate", dz_gate, W["w_gate"], "nt")
    dh = _mm("d_h_rkv", dp_rkv, w_rkv, "nt", add=dh)
    dh = _mm("d_h_wa", dp_wa, w_wa, "nt", add=dh)
    dh = _mm("d_h_xg", dp_xg, w_xg, "nt", add=dh)
    dh = _mm("d_h_att", dp_att, w_att, "nt", add=dh)
    dx, G["g_mix"] = _rowwise("d_norm_mix", norm_bwd, T, tT, rows=[x, dh, dx1], consts=[W["g_mix"]],
                              outs=[("row", D_MODEL, F32), ("acc", (1, D_MODEL))])
    return loss_acc[:, :1], dx, G


HBM_SPEC = pl.BlockSpec(memory_space=pltpu.HBM)


def _place():
    x, y, c = lax.axis_index("x"), lax.axis_index("y"), lax.axis_index("c")
    return x, y, c, [(1 - x, y), (x, 1 - y), (1 - x, 1 - y)]


def _remote(src, dst, send_sems, recv_sems, k, to):
    return pltpu.make_async_remote_copy(src_ref=src, dst_ref=dst, send_sem=send_sems.at[k], recv_sem=recv_sems.at[k],
                                        device_id=to, device_id_type=MESH)


ROW_ALIGN = 16


def _half_rows(ref_rows, c, first):
    half = ref_rows // 2
    which = c if first else 1 - c
    return pl.ds(pl.multiple_of(which * half, ROW_ALIGN), half)


def _gather_chips(shards):
    n = len(shards)
    split = [s.shape[0] % (2 * ROW_ALIGN) == 0 for s in shards]

    def body(*refs):
        w_refs, out_refs = refs[:n], refs[n:2 * n]
        send_sems, recv_sems = refs[2 * n:]
        x, y, c, chips = _place()
        me = 2 * x + y
        sends, passed = [], []
        for i in range(n):
            for j, (px, py) in enumerate(chips):
                if split[i]:
                    mine = _half_rows(w_refs[i].shape[0], c, True)
                    cp = _remote(w_refs[i].at[mine], out_refs[i].at[me, mine], send_sems, recv_sems, 6 * i + j,
                                 (px, py, c))
                else:
                    cp = _remote(w_refs[i], out_refs[i].at[me], send_sems, recv_sems, 6 * i + j, (px, py, c))
                cp.start()
                sends.append(cp)
        for i in range(n):
            for j, (px, py) in enumerate(chips):
                if split[i]:
                    landed = out_refs[i].at[2 * px + py, _half_rows(w_refs[i].shape[0], c, True)]
                    _remote(landed, landed, send_sems, recv_sems, 6 * i + j, (px, py, c)).wait_recv()
                    cp = _remote(landed, landed, send_sems, recv_sems, 6 * i + 3 + j, (x, y, 1 - c))
                    cp.start()
                    passed.append(cp)
                else:
                    landed = out_refs[i].at[2 * px + py]
                    _remote(landed, landed, send_sems, recv_sems, 6 * i + j, (px, py, c)).wait_recv()
        for i in range(n):
            if split[i]:
                for j, (px, py) in enumerate(chips):
                    landed = out_refs[i].at[2 * px + py, _half_rows(w_refs[i].shape[0], c, False)]
                    _remote(landed, landed, send_sems, recv_sems, 6 * i + 3 + j, (x, y, 1 - c)).wait_recv()
        for cp in sends + passed:
            cp.wait_send()

    outs = pl.pallas_call(
        body, name="gather_weights", in_specs=[HBM_SPEC] * n, out_specs=[HBM_SPEC] * n,
        out_shape=[jax.ShapeDtypeStruct((N_CHIPS,) + s.shape, s.dtype) for s in shards],
        scratch_shapes=[pltpu.SemaphoreType.DMA((6 * n,)), pltpu.SemaphoreType.DMA((6 * n,))],
    )(*shards)
    me = 2 * lax.axis_index("x") + lax.axis_index("y")
    return [lax.dynamic_update_slice(o, s[None], (me, 0, 0)) for o, s in zip(outs, shards, strict=True)]


def _swap_halves(gs):
    n = len(gs)

    def body(*refs):
        g_refs, out_refs = refs[:n], refs[n:2 * n]
        send_sems, recv_sems = refs[2 * n:]
        x, y, c, _ = _place()
        cps = []
        for i in range(n):
            theirs = _half_rows(g_refs[i].shape[1], c, False)
            cp = _remote(g_refs[i].at[:, theirs, :], out_refs[i], send_sems, recv_sems, i, (x, y, 1 - c))
            cp.start()
            cps.append(cp)
        for cp in cps:
            cp.wait()

    return pl.pallas_call(
        body, name="swap_halves", in_specs=[HBM_SPEC] * n, out_specs=[HBM_SPEC] * n,
        out_shape=[jax.ShapeDtypeStruct((N_CHIPS, g.shape[1] // 2, g.shape[2]), g.dtype) for g in gs],
        scratch_shapes=[pltpu.SemaphoreType.DMA((n,)), pltpu.SemaphoreType.DMA((n,))],
    )(*gs)


def _scatter_chips(parts):
    n = len(parts)

    def body(*refs):
        p_refs, out_refs = refs[:n], refs[n:2 * n]
        send_sems, recv_sems = refs[2 * n:]
        x, y, c, chips = _place()
        me = 2 * x + y
        sends = []
        for i in range(n):
            for j, (px, py) in enumerate(chips):
                cp = _remote(p_refs[i].at[2 * px + py], out_refs[i].at[me], send_sems, recv_sems, 3 * i + j,
                             (px, py, c))
                cp.start()
                sends.append(cp)
        for i in range(n):
            for j, (px, py) in enumerate(chips):
                landed = out_refs[i].at[2 * px + py]
                _remote(landed, landed, send_sems, recv_sems, 3 * i + j, (px, py, c)).wait_recv()
        for cp in sends:
            cp.wait_send()

    outs = pl.pallas_call(
        body, name="scatter_grads", in_specs=[HBM_SPEC] * n, out_specs=[HBM_SPEC] * n,
        out_shape=[jax.ShapeDtypeStruct(p.shape, p.dtype) for p in parts],
        scratch_shapes=[pltpu.SemaphoreType.DMA((3 * n,)), pltpu.SemaphoreType.DMA((3 * n,))],
    )(*parts)
    me = 2 * lax.axis_index("x") + lax.axis_index("y")
    own = [lax.dynamic_slice_in_dim(p, me, 1, axis=0) for p in parts]
    return [lax.dynamic_update_slice(o, s, (me, 0, 0)) for o, s in zip(outs, own, strict=True)]


def _join_halves(reds):
    n = len(reds)

    def body(*refs):
        r_refs, out_refs = refs[:n], refs[n:2 * n]
        send_sems, recv_sems = refs[2 * n:]
        x, y, c, _ = _place()
        cps = []
        for i in range(n):
            mine = _half_rows(out_refs[i].shape[0], c, True)
            cp = _remote(r_refs[i], out_refs[i].at[mine], send_sems, recv_sems, i, (x, y, 1 - c))
            cp.start()
            cps.append(cp)
        for cp in cps:
            cp.wait()

    outs = pl.pallas_call(
        body, name="join_halves", in_specs=[HBM_SPEC] * n, out_specs=[HBM_SPEC] * n,
        out_shape=[jax.ShapeDtypeStruct((2 * r.shape[0], r.shape[1]), r.dtype) for r in reds],
        scratch_shapes=[pltpu.SemaphoreType.DMA((n,)), pltpu.SemaphoreType.DMA((n,))],
    )(*reds)
    c = lax.axis_index("c")
    return [lax.dynamic_update_slice(o, r, (c * r.shape[0], 0)) for o, r in zip(outs, reds, strict=True)]


def _gather_all(vec):
    R = vec.shape[0]

    def body(v_ref, out_ref, send_sems, recv_sems, local_sem):
        x, y, c, _ = _place()
        me = 4 * x + 2 * y + c
        local = pltpu.make_async_copy(v_ref, out_ref.at[me], local_sem)
        local.start()
        peers = [(x ^ (k >> 2), y ^ ((k >> 1) & 1), c ^ (k & 1)) for k in range(1, N_DEV)]
        sends = [_remote(v_ref, out_ref.at[me], send_sems, recv_sems, k, to) for k, to in enumerate(peers)]
        for cp in sends:
            cp.start()
        for k, (px, py, pc) in enumerate(peers):
            landed = out_ref.at[4 * px + 2 * py + pc]
            _remote(landed, landed, send_sems, recv_sems, k, (px, py, pc)).wait_recv()
        for cp in sends:
            cp.wait_send()
        local.wait()

    return pl.pallas_call(
        body, name="gather_small", in_specs=[HBM_SPEC], out_specs=HBM_SPEC,
        out_shape=jax.ShapeDtypeStruct((N_DEV, R, LANES), vec.dtype),
        scratch_shapes=[pltpu.SemaphoreType.DMA((7,)), pltpu.SemaphoreType.DMA((7,)), pltpu.SemaphoreType.DMA],
    )(vec)


SUM_TILE_BYTES = 4 * 1024 * 1024


def _sum_rows(half, cols):
    best = ROW_ALIGN
    for t in range(ROW_ALIGN, half + 1, ROW_ALIGN):
        if half % t == 0 and N_CHIPS * t * cols * 4 <= SUM_TILE_BYTES:
            best = t
    return best


def _sum_cores(name, g, theirs, core):
    _, R, C = g.shape
    half = R // 2
    tr = _sum_rows(half, C)
    nb = half // tr

    def body(core_ref, g_ref, t_ref, o_ref):
        o_ref[...] = (g_ref[...] + t_ref[...]).astype(o_ref.dtype)

    grid_spec = pltpu.PrefetchScalarGridSpec(
        num_scalar_prefetch=1, grid=(nb,),
        in_specs=[pl.BlockSpec((N_CHIPS, tr, C), lambda i, core_ref: (0, core_ref[0] * nb + i, 0)),
                  pl.BlockSpec((N_CHIPS, tr, C), lambda i, core_ref: (0, i, 0))],
        out_specs=pl.BlockSpec((N_CHIPS, tr, C), lambda i, core_ref: (0, i, 0)))
    return pl.pallas_call(
        body, name=name, grid_spec=grid_spec, out_shape=jax.ShapeDtypeStruct((N_CHIPS, half, C), BF16),
        compiler_params=_params(("parallel",)),
    )(core, g, theirs)


def _sum_chips(name, parts):
    _, H, C = parts.shape
    tr = _sum_rows(H, C)

    def body(p_ref, o_ref):
        acc = p_ref[0].astype(F32)
        for k in range(1, N_CHIPS):
            acc = acc + p_ref[k].astype(F32)
        o_ref[...] = acc

    return pl.pallas_call(
        body, name=name, grid=(H // tr,),
        in_specs=[pl.BlockSpec((N_CHIPS, tr, C), lambda i: (0, i, 0))],
        out_specs=pl.BlockSpec((tr, C), lambda i: (i, 0)),
        out_shape=jax.ShapeDtypeStruct((H, C), F32),
        compiler_params=_params(("parallel",)),
    )(parts)


def _adamw_math(w, g, m, v):
    m = ADAM_B1 * m + (1.0 - ADAM_B1) * g
    v = ADAM_B2 * v + (1.0 - ADAM_B2) * (g * g)
    m_hat = m / (1.0 - ADAM_B1 ** ADAM_STEP)
    v_hat = v / (1.0 - ADAM_B2 ** ADAM_STEP)
    delta = -ADAM_LR * (m_hat / (jnp.sqrt(v_hat) + ADAM_EPS) + ADAM_WD * w)
    return delta, m, v


def _adamw(name, w, g, m, v):
    R, C = w.shape
    tr = R
    if R % SUBLANES == 0:
        for cand in range(SUBLANES, min(R, 256) + 1, SUBLANES):
            if R % cand == 0:
                tr = cand

    def body(w_ref, g_ref, m_ref, v_ref, d_ref, nm_ref, nv_ref):
        d, nm, nv = _adamw_math(w_ref[...], g_ref[...], m_ref[...], v_ref[...])
        d_ref[...] = d
        nm_ref[...] = nm
        nv_ref[...] = nv

    spec = pl.BlockSpec((tr, C), lambda i: (i, 0))
    shape = jax.ShapeDtypeStruct((R, C), F32)
    return pl.pallas_call(
        body, name=name, grid=(R // tr,), in_specs=[spec] * 4, out_specs=[spec] * 3, out_shape=[shape] * 3,
        compiler_params=_params(("parallel",)),
    )(w, g, m, v)


def _adamw_small(parts, w, m, v):
    n = parts.shape[0]

    def body(p_ref, w_ref, m_ref, v_ref, g_ref, d_ref, nm_ref, nv_ref):
        g = p_ref[0]
        for k in range(1, n):
            g = g + p_ref[k]
        d, nm, nv = _adamw_math(w_ref[...], g, m_ref[...], v_ref[...])
        g_ref[...] = g
        d_ref[...] = d
        nm_ref[...] = nm
        nv_ref[...] = nv

    shape = jax.ShapeDtypeStruct(w.shape, F32)
    return pl.pallas_call(body, name="adamw_small", out_shape=[shape] * 4, compiler_params=_params())(parts, w, m, v)


WEIGHTS = ['g_mix', 'w_in', 'rw_mu', 'rw_w0', 'rw_w_up', 'rw_a0', 'rw_a_up', 'rw_g_up', 'rw_k_k', 'rw_k_a',
           'rw_r_k', 'rw_ln_g', 'rw_ln_b', 'w_branch_a', 'w_branch_b', 'w_gate', 'b_gate', 'w_out', 'g_ffn', 'w_up',
           'conv_w', 'conv_b', 'w_down', 'g_ple', 'w_ple_gate', 'w_ple', 'g_final']
ARG_NAMES = (['x', 'p'] + WEIGHTS + ['loss_target'] + ['m_' + n for n in WEIGHTS] + ['v_' + n for n in WEIGHTS])
SHARDED = {'w_in': 1, 'rw_w_up': 1, 'rw_a_up': 1, 'rw_g_up': 1, 'w_branch_a': 1, 'w_branch_b': 1, 'w_gate': 1,
           'w_out': 0, 'w_up': 1, 'conv_w': 1, 'w_down': 0, 'w_ple_gate': 0, 'w_ple': 1}
SMALL = [n for n in WEIGHTS if n not in SHARDED]
WHOLE = ['conv_w']
SPLIT = [n for n in SHARDED if n not in WHOLE]
PACK_ALIGN = SUBLANES * LANES


def _pack_rows(flat_parts):
    flat = jnp.concatenate(flat_parts, axis=1)
    n = flat.shape[1]
    padded = -(-n // PACK_ALIGN) * PACK_ALIGN
    flat = jnp.pad(flat, ((0, 0), (0, padded - n)))
    return flat.reshape(padded // LANES, LANES)


def _full_from_shards(stack, axis):
    _, R, C = stack.shape
    if axis == 0:
        return stack.reshape(N_CHIPS * R, C)
    return stack.transpose(1, 0, 2).reshape(R, N_CHIPS * C)


def _shards_from_full(full, axis):
    R, C = full.shape
    if axis == 0:
        return full.reshape(N_CHIPS, R // N_CHIPS, C)
    return full.reshape(R, N_CHIPS, C // N_CHIPS).transpose(1, 0, 2)


def kernel(x, p, g_mix, w_in, rw_mu, rw_w0, rw_w_up, rw_a0, rw_a_up, rw_g_up, rw_k_k, rw_k_a, rw_r_k, rw_ln_g, rw_ln_b, w_branch_a, w_branch_b, w_gate, b_gate, w_out, g_ffn, w_up, conv_w, conv_b, w_down, g_ple, w_ple_gate, w_ple, g_final, loss_target, m_g_mix, m_w_in, m_rw_mu, m_rw_w0, m_rw_w_up, m_rw_a0, m_rw_a_up, m_rw_g_up, m_rw_k_k, m_rw_k_a, m_rw_r_k, m_rw_ln_g, m_rw_ln_b, m_w_branch_a, m_w_branch_b, m_w_gate, m_b_gate, m_w_out, m_g_ffn, m_w_up, m_conv_w, m_conv_b, m_w_down, m_g_ple, m_w_ple_gate, m_w_ple, m_g_final, v_g_mix, v_w_in, v_rw_mu, v_rw_w0, v_rw_w_up, v_rw_a0, v_rw_a_up, v_rw_g_up, v_rw_k_k, v_rw_k_a, v_rw_r_k, v_rw_ln_g, v_rw_ln_b, v_w_branch_a, v_w_branch_b, v_w_gate, v_b_gate, v_w_out, v_g_ffn, v_w_up, v_conv_w, v_conv_b, v_w_down, v_g_ple, v_w_ple_gate, v_w_ple, v_g_final):
    given = dict(zip(ARG_NAMES, (x, p, g_mix, w_in, rw_mu, rw_w0, rw_w_up, rw_a0, rw_a_up, rw_g_up, rw_k_k, rw_k_a, rw_r_k, rw_ln_g, rw_ln_b, w_branch_a, w_branch_b, w_gate, b_gate, w_out, g_ffn, w_up, conv_w, conv_b, w_down, g_ple, w_ple_gate, w_ple, g_final, loss_target, m_g_mix, m_w_in, m_rw_mu, m_rw_w0, m_rw_w_up, m_rw_a0, m_rw_a_up, m_rw_g_up, m_rw_k_k, m_rw_k_a, m_rw_r_k, m_rw_ln_g, m_rw_ln_b, m_w_branch_a, m_w_branch_b, m_w_gate, m_b_gate, m_w_out, m_g_ffn, m_w_up, m_conv_w, m_conv_b, m_w_down, m_g_ple, m_w_ple_gate, m_w_ple, m_g_final, v_g_mix, v_w_in, v_rw_mu, v_rw_w0, v_rw_w_up, v_rw_a0, v_rw_a_up, v_rw_g_up, v_rw_k_k, v_rw_k_a, v_rw_r_k, v_rw_ln_g, v_rw_ln_b, v_w_branch_a, v_w_branch_b, v_w_gate, v_b_gate, v_w_out, v_g_ffn, v_w_up, v_conv_w, v_conv_b, v_w_down, v_g_ple, v_w_ple_gate, v_w_ple, v_g_final), strict=True))

    def two_d(name, prefix=""):
        a = given[prefix + name]
        if name == "g_final":
            return a.reshape(1, D_MODEL)
        if name == "rw_r_k":
            return a.reshape(1, RW_WIDTH)
        return a[0] if a.ndim == 3 else a

    gathered = _gather_chips([two_d(n) if n in WHOLE else two_d(n).astype(BF16) for n in SHARDED])
    W = {n: _full_from_shards(g, SHARDED[n]) for n, g in zip(SHARDED, gathered, strict=True)}
    for n in SMALL:
        W[n] = two_d(n)
    W["rw_r_k"] = W["rw_r_k"].reshape(RW_HEADS, RW_HEAD_DIM)

    loss_part, grad_x, G = _local_step(x[0], p[0, 0], W, loss_target[0])

    core = lax.axis_index("c").astype(jnp.int32).reshape(1)
    by_chip = [_shards_from_full(G[n], SHARDED[n]) for n in SPLIT]
    theirs = _swap_halves(by_chip)
    pair = [_sum_cores("sum_cores_" + n, g, t, core) for n, g, t in zip(SPLIT, by_chip, theirs, strict=True)]
    landed = _scatter_chips(pair)
    reduced = [_sum_chips("sum_chips_" + n, q) for n, q in zip(SPLIT, landed, strict=True)]
    shard_grads = dict(zip(SPLIT, _join_halves(reduced), strict=True))

    small_sizes = {n: two_d(n).shape[1] for n in SMALL}
    n_small = sum(small_sizes.values())
    whole_sizes = {n: G[n].shape[0] * G[n].shape[1] for n in WHOLE}
    n_whole = sum(whole_sizes.values())

    def pack_small(parts, rest):
        return _pack_rows([a.reshape(1, -1) for a in parts] + [rest])

    G["rw_r_k"] = G["rw_r_k"].reshape(1, RW_WIDTH)
    rest = jnp.zeros((1, n_whole + 1), F32)
    all_small = _gather_all(pack_small([G[n] for n in SMALL] + [G[n] for n in WHOLE], loss_part))
    gs, ds, nms, nvs = _adamw_small(all_small, pack_small([two_d(n) for n in SMALL], rest),
                                    pack_small([two_d(n, "m_") for n in SMALL], rest),
                                    pack_small([two_d(n, "v_") for n in SMALL], rest))
    gs, ds, nms, nvs = (a.reshape(-1) for a in (gs, ds, nms, nvs))
    loss = gs[n_small + n_whole]
    chip = 2 * lax.axis_index("x") + lax.axis_index("y")
    off = n_small
    for n in WHOLE:
        full = gs[off:off + whole_sizes[n]].reshape(G[n].shape)
        off += whole_sizes[n]
        width = two_d(n).shape[1]
        shard_grads[n] = lax.dynamic_slice_in_dim(full, chip * width, width, axis=1)

    grads, deltas, new_m, new_v = {}, {}, {}, {}
    for n in SHARDED:
        g = shard_grads[n]
        d, nm, nv = _adamw("adamw_" + n, two_d(n), g, two_d(n, "m_"), two_d(n, "v_"))
        grads[n], deltas[n], new_m[n], new_v[n] = g, d, nm, nv
    off = 0
    for n in SMALL:
        sl = slice(off, off + small_sizes[n])
        off += small_sizes[n]
        grads[n], deltas[n], new_m[n], new_v[n] = gs[sl], ds[sl], nms[sl], nvs[sl]
    outs = [loss, grad_x[None]]
    for table in (grads, deltas, new_m, new_v):
        outs += [table[n].reshape(given[n].shape) for n in WEIGHTS]
    return tuple(outs)
```

```python
import math

import jax
import jax.numpy as jnp
import numpy as np
from jax import lax
from jax.experimental import pallas as pl
from jax.experimental.pallas import tpu as pltpu

F32 = jnp.float32
BF16 = jnp.bfloat16

D_MODEL = 1024
NORM_EPS = 1e-6
RW_HEADS = 8
RW_HEAD_DIM = 64
RW_WIDTH = 512
RW_LN_EPS = 64e-5
ATT_GROUP_DILATION = (1, 4, 16)
ATT_BLOCK = 128
ATT_HEADS = 12
ATT_HEAD_DIM = 64
ATT_GROUP_WIDTH = 256
ATT_WIDTH = 768
D_FF = 3072

ADAM_LR = 0.001
ADAM_B1 = 0.9
ADAM_B2 = 0.999
ADAM_EPS = 1e-08
ADAM_WD = 0.01
ADAM_STEP = 10

SUBLANES = 8
LANES = 128
VMEM_LIMIT = 56 * 1024 * 1024
N_CHIPS = 4
N_DEV = 8
MESH = pl.DeviceIdType.MESH


def _params(sem=None):
    return pltpu.CompilerParams(dimension_semantics=sem, vmem_limit_bytes=VMEM_LIMIT)


def _pick(dim, pref):
    if dim % LANES != 0 or dim <= pref:
        return dim
    best = LANES
    for t in range(LANES, pref + 1, LANES):
        if dim % t == 0:
            best = t
    return best


def _mm(name, a, b, mode, out_dtype=F32, add=None, tm=1024, tn=1024, tk=1024):
    if mode == "nn":
        (M, K), (K2, N) = a.shape, b.shape
    elif mode == "nt":
        (M, K), (N, K2) = a.shape, b.shape
    else:
        (K, M), (K2, N) = a.shape, b.shape
    assert K == K2, (name, a.shape, b.shape, mode)
    tm, tn, tk = _pick(M, tm), _pick(N, tn), _pick(K, tk)
    nk = K // tk
    if mode == "nn":
        a_spec = pl.BlockSpec((tm, tk), lambda i, j, k: (i, k))
        b_spec = pl.BlockSpec((tk, tn), lambda i, j, k: (k, j))
        dims = (((1,), (0,)), ((), ()))
    elif mode == "nt":
        a_spec = pl.BlockSpec((tm, tk), lambda i, j, k: (i, k))
        b_spec = pl.BlockSpec((tn, tk), lambda i, j, k: (j, k))
        dims = (((1,), (1,)), ((), ()))
    else:
        a_spec = pl.BlockSpec((tk, tm), lambda i, j, k: (k, i))
        b_spec = pl.BlockSpec((tk, tn), lambda i, j, k: (k, j))
        dims = (((0,), (0,)), ((), ()))
    o_spec = pl.BlockSpec((tm, tn), lambda i, j, k: (i, j))
    has_add = add is not None

    def body(*refs):
        if has_add:
            a_ref, b_ref, add_ref, o_ref, acc_ref = refs
        else:
            a_ref, b_ref, o_ref, acc_ref = refs
        k = pl.program_id(2)
        part = lax.dot_general(a_ref[...].astype(BF16), b_ref[...].astype(BF16), dims,
                               preferred_element_type=F32)

        @pl.when(k == 0)
        def _():
            acc_ref[...] = part

        @pl.when(k > 0)
        def _():
            acc_ref[...] += part

        @pl.when(k == nk - 1)
        def _():
            res = acc_ref[...]
            if has_add:
                res = res + add_ref[...].astype(F32)
            o_ref[...] = res.astype(o_ref.dtype)

    ins = [a, b] + ([add] if has_add else [])
    in_specs = [a_spec, b_spec] + ([o_spec] if has_add else [])
    return pl.pallas_call(
        body, name=name, grid=(M // tm, N // tn, nk),
        in_specs=in_specs, out_specs=o_spec,
        out_shape=jax.ShapeDtypeStruct((M, N), out_dtype),
        scratch_shapes=[pltpu.VMEM((tm, tn), F32)],
        compiler_params=_params(("parallel", "parallel", "arbitrary")),
    )(*ins)


def _rowwise(name, fn, T, tT, rows=(), prevs=(), nexts=(), consts=(), outs=()):
    n = T // tT
    per8 = tT // SUBLANES
    in_specs, ins = [], []
    for arr in rows:
        in_specs.append(pl.BlockSpec((tT, arr.shape[1]), lambda i: (i, 0)))
        ins.append(arr)
    for arr in prevs:
        in_specs.append(pl.BlockSpec((SUBLANES, arr.shape[1]), lambda i: (jnp.maximum(i * per8 - 1, 0), 0)))
        ins.append(arr)
    for arr in nexts:
        in_specs.append(pl.BlockSpec((SUBLANES, arr.shape[1]),
                                     lambda i: (jnp.minimum((i + 1) * per8, T // SUBLANES - 1), 0)))
        ins.append(arr)
    for arr in consts:
        in_specs.append(pl.BlockSpec(arr.shape, lambda i, nd=arr.ndim: (0,) * nd))
        ins.append(arr)
    out_specs, out_shapes = [], []
    for o in outs:
        if o[0] == "row":
            out_specs.append(pl.BlockSpec((tT, o[1]), lambda i: (i, 0)))
            out_shapes.append(jax.ShapeDtypeStruct((T, o[1]), o[2]))
        else:
            out_specs.append(pl.BlockSpec(o[1], lambda i: (0, 0)))
            out_shapes.append(jax.ShapeDtypeStruct(o[1], F32))
    nr, npv, nnx, nc = len(rows), len(prevs), len(nexts), len(consts)
    n_in = nr + npv + nnx + nc

    def body(*refs):
        i = pl.program_id(0)
        vals = [r[...] for r in refs[:n_in]]
        res = fn(i, n, vals[:nr], vals[nr:nr + npv], vals[nr + npv:nr + npv + nnx], vals[nr + npv + nnx:])
        for o, o_ref, val in zip(outs, refs[n_in:], res, strict=True):
            if o[0] == "row":
                o_ref[...] = val.astype(o_ref.dtype)
            else:
                @pl.when(i == 0)
                def _(o_ref=o_ref, val=val):
                    o_ref[...] = val.astype(F32)

                @pl.when(i > 0)
                def _(o_ref=o_ref, val=val):
                    o_ref[...] += val.astype(F32)

    res = pl.pallas_call(
        body, name=name, grid=(n,), in_specs=in_specs, out_specs=out_specs, out_shape=out_shapes,
        compiler_params=_params(("arbitrary",)),
    )(*ins)
    return list(res)


def _shift_down(x, prev8, i, s):
    rolled = pltpu.roll(x, s, 0)
    head = pltpu.roll(prev8, s, 0)
    head = jnp.where(i == 0, jnp.zeros_like(head), head)
    rid = lax.broadcasted_iota(jnp.int32, head.shape, 0)
    first = jnp.where(rid < s, head, rolled[:SUBLANES])
    if x.shape[0] == SUBLANES:
        return first
    return jnp.concatenate([first, rolled[SUBLANES:]], axis=0)


def _shift_up(x, next8, i, n, s):
    tT = x.shape[0]
    rolled = pltpu.roll(x, tT - s, 0)
    tail = pltpu.roll(next8, SUBLANES - s, 0)
    tail = jnp.where(i == n - 1, jnp.zeros_like(tail), tail)
    rid = lax.broadcasted_iota(jnp.int32, tail.shape, 0)
    last = jnp.where(rid >= SUBLANES - s, tail, rolled[tT - SUBLANES:])
    return jnp.concatenate([rolled[:tT - SUBLANES], last], axis=0)


def _colsum(x):
    return jnp.sum(x, axis=0, keepdims=True)


def _segsum(x, bd):
    return jnp.dot(x, bd, precision=lax.Precision.HIGHEST, preferred_element_type=F32)


def _block_diag_ones(width, seg):
    idx = np.arange(width) // seg
    return jnp.asarray((idx[:, None] == idx[None, :]).astype(np.float32))


def _sigmoid(z):
    return 1.0 / (1.0 + jnp.exp(-z))


def _softplus(z):
    return jnp.maximum(z, 0.0) + jnp.log(1.0 + jnp.exp(-jnp.abs(z)))


def _rms_fwd(x, g):
    r = lax.rsqrt(jnp.mean(x * x, axis=-1, keepdims=True) + NORM_EPS)
    return x * r * g


def _rms_bwd(x, g, dy):
    r = lax.rsqrt(jnp.mean(x * x, axis=-1, keepdims=True) + NORM_EPS)
    gdy = dy * g
    dx = r * (gdy - x * (r * r) * jnp.mean(x * gdy, axis=-1, keepdims=True))
    return dx, dy * x * r


GELU_C = math.sqrt(2.0 / math.pi)


def _gelu(x):
    return 0.5 * x * (1.0 + jnp.tanh(GELU_C * (x + 0.044715 * x * x * x)))


def _gelu_grad(x):
    th = jnp.tanh(GELU_C * (x + 0.044715 * x * x * x))
    return 0.5 * (1.0 + th) + 0.5 * x * (1.0 - th * th) * GELU_C * (1.0 + 3.0 * 0.044715 * x * x)


RW_CHUNK = 64
NN = (((1,), (0,)), ((), ()))
NT = (((1,), (1,)), ((), ()))
TN = (((0,), (0,)), ((), ()))


def _hdot(a, b, dims):
    return lax.dot_general(a, b, dims, precision=lax.Precision.HIGH, preferred_element_type=F32)


def _chunk_masks():
    ti = lax.broadcasted_iota(jnp.int32, (RW_CHUNK, RW_CHUNK), 0)
    tj = lax.broadcasted_iota(jnp.int32, (RW_CHUNK, RW_CHUNK), 1)
    return tj <= ti, tj < ti, (ti == tj).astype(F32)


def _head(x, h):
    return x[:, h * RW_HEAD_DIM:(h + 1) * RW_HEAD_DIM]


def _heads(fn):
    return [fn(h) for h in range(RW_HEADS)]


def _chunk_rows(r, lw, k, a, b, incl_f):
    c = _hdot(incl_f, lw, NN)
    e_prev, e_neg, e_pos = jnp.exp(c - lw), jnp.exp(-c), jnp.exp(c)
    return dict(At=a * e_prev, Bt=b * e_neg, Kt=k * e_neg, Rt=r * e_pos, e_prev=e_prev, e_neg=e_neg, e_pos=e_pos)


def _chunk_coeffs(q, incl, strict):
    A1 = _heads(lambda h: jnp.where(strict, _hdot(_head(q["At"], h), _head(q["Bt"], h), NT), 0.0))
    A2 = _heads(lambda h: jnp.where(strict, _hdot(_head(q["At"], h), _head(q["Kt"], h), NT), 0.0))
    W1 = _heads(lambda h: jnp.where(incl, _hdot(_head(q["Rt"], h), _head(q["Bt"], h), NT), 0.0))
    W2 = _heads(lambda h: jnp.where(incl, _hdot(_head(q["Rt"], h), _head(q["Kt"], h), NT), 0.0))
    return A1, A2, W1, W2


def _rwkv_chunk_prep(r, lw, k, a, b, v):
    T = r.shape[0]
    nC = T // RW_CHUNK
    H, N = RW_HEADS, RW_HEAD_DIM

    def body(r_ref, lw_ref, k_ref, a_ref, b_ref, v_ref,
             at_ref, bt_ref, kt_ref, rt_ref, a2v_ref, w2v_ref, ti_ref, w1_ref, pl_ref):
        incl, strict, eye = _chunk_masks()
        q = _chunk_rows(r_ref[...], lw_ref[...], k_ref[...], a_ref[...], b_ref[...], incl.astype(F32))
        at_ref[...], bt_ref[...], kt_ref[...], rt_ref[...] = q["At"], q["Bt"], q["Kt"], q["Rt"]
        pl_ref[0] = jnp.broadcast_to(q["e_pos"][RW_CHUNK - 1:RW_CHUNK, :], (SUBLANES, RW_WIDTH))
        A1, A2, W1, W2 = _chunk_coeffs(q, incl, strict)
        V = v_ref[...]
        a2v_ref[...] = jnp.concatenate(_heads(lambda h: _hdot(A2[h], _head(V, h), NN)), axis=1)
        w2v_ref[...] = jnp.concatenate(_heads(lambda h: _hdot(W2[h], _head(V, h), NN)), axis=1)
        tinv, pw = [eye + m for m in A1], A1
        for _ in range(5):
            pw = [_hdot(m, m, NN) for m in pw]
            tinv = [t + _hdot(t, m, NN) for t, m in zip(tinv, pw, strict=True)]
        for h in range(H):
            ti_ref[0, h] = tinv[h]
            w1_ref[0, h] = W1[h]

    row_spec = pl.BlockSpec((RW_CHUNK, RW_WIDTH), lambda n: (n, 0))
    st_spec = pl.BlockSpec((1, H, N, N), lambda n: (n, 0, 0, 0))
    row_shape = jax.ShapeDtypeStruct((T, RW_WIDTH), F32)
    st_shape = jax.ShapeDtypeStruct((nC, H, N, N), F32)
    return pl.pallas_call(
        body, name="rwkv_chunk_prep", grid=(nC,),
        in_specs=[row_spec] * 6,
        out_specs=[row_spec] * 6 + [st_spec, st_spec, pl.BlockSpec((1, SUBLANES, RW_WIDTH), lambda n: (n, 0, 0))],
        out_shape=[row_shape] * 6 + [st_shape, st_shape, jax.ShapeDtypeStruct((nC, SUBLANES, RW_WIDTH), F32)],
        compiler_params=_params(("parallel",)),
    )(r, lw, k, a, b, v)


def _rwkv_chunk_fwd(v, at, bt, kt, rt, a2v, w2v, tinv, w1, plast):
    T = v.shape[0]
    nC = T // RW_CHUNK
    H, N = RW_HEADS, RW_HEAD_DIM

    def body(v_ref, at_ref, bt_ref, kt_ref, rt_ref, a2v_ref, w2v_ref, ti_ref, w1_ref, pl_ref,
             y_ref, sa_ref, s0_ref, S_ref):
        @pl.when(pl.program_id(0) == 0)
        def _():
            S_ref[...] = jnp.zeros_like(S_ref)

        V, At, Bt, Kt, Rt = v_ref[...], at_ref[...], bt_ref[...], kt_ref[...], rt_ref[...]
        A2V, W2V, p_last = a2v_ref[...], w2v_ref[...], pl_ref[0, 0:1, :]
        S0 = _heads(lambda h: S_ref[h])
        for h in range(H):
            s0_ref[0, h] = S0[h]
        Z = _heads(lambda h: _hdot(_head(At, h), S0[h], NT) + _head(A2V, h))
        Sa = _heads(lambda h: _hdot(ti_ref[0, h], Z[h], NN))
        X = _heads(lambda h: S0[h] + _hdot(Sa[h], _head(Bt, h), TN) + _hdot(_head(V, h), _head(Kt, h), TN))
        for h in range(H):
            S_ref[h] = X[h] * _head(p_last, h)
        Y = _heads(lambda h: _hdot(_head(Rt, h), S0[h], NT) + _hdot(w1_ref[0, h], Sa[h], NN) + _head(W2V, h))
        y_ref[...] = jnp.concatenate(Y, axis=1)
        sa_ref[...] = jnp.concatenate(Sa, axis=1)

    row_spec = pl.BlockSpec((RW_CHUNK, RW_WIDTH), lambda n: (n, 0))
    st_spec = pl.BlockSpec((1, H, N, N), lambda n: (n, 0, 0, 0))
    row_shape = jax.ShapeDtypeStruct((T, RW_WIDTH), F32)
    return pl.pallas_call(
        body, name="rwkv_chunk_fwd", grid=(nC,),
        in_specs=[row_spec] * 7 + [st_spec, st_spec, pl.BlockSpec((1, SUBLANES, RW_WIDTH), lambda n: (n, 0, 0))],
        out_specs=[row_spec, row_spec, st_spec],
        out_shape=[row_shape, row_shape, jax.ShapeDtypeStruct((nC, H, N, N), F32)],
        scratch_shapes=[pltpu.VMEM((H, N, N), F32)],
        compiler_params=_params(("arbitrary",)),
    )(v, at, bt, kt, rt, a2v, w2v, tinv, w1, plast)


def _rwkv_chunk_bwd(r, lw, k, a, b, v, dy, s0, tinv, sa):
    T = r.shape[0]
    nC = T // RW_CHUNK
    H, N = RW_HEADS, RW_HEAD_DIM

    def body(r_ref, lw_ref, k_ref, a_ref, b_ref, v_ref, dy_ref, s0_ref, ti_ref, sa_ref,
             dr_ref, dlw_ref, dk_ref, da_ref, db_ref, dv_ref, dS_ref):
        @pl.when(pl.program_id(0) == 0)
        def _():
            dS_ref[...] = jnp.zeros_like(dS_ref)

        incl, strict, _ = _chunk_masks()
        incl_f = incl.astype(F32)
        q = _chunk_rows(r_ref[...], lw_ref[...], k_ref[...], a_ref[...], b_ref[...], incl_f)
        At, Bt, Kt, Rt = q["At"], q["Bt"], q["Kt"], q["Rt"]
        _, A2, W1, W2 = _chunk_coeffs(q, incl, strict)
        V, dY, Sa = v_ref[...], dy_ref[...], sa_ref[...]
        hd = _head
        p_last = q["e_pos"][RW_CHUNK - 1:RW_CHUNK, :]
        S0 = _heads(lambda h: s0_ref[0, h])
        G = _heads(lambda h: dS_ref[h] * hd(p_last, h))
        X = _heads(lambda h: S0[h] + _hdot(hd(Sa, h), hd(Bt, h), TN) + _hdot(hd(V, h), hd(Kt, h), TN))
        dc_last = jnp.concatenate(_heads(lambda h: jnp.sum(G[h] * X[h], axis=0, keepdims=True)), axis=1)
        dSa = _heads(lambda h: _hdot(hd(Bt, h), G[h], NT) + _hdot(W1[h], hd(dY, h), TN))
        dZ = _heads(lambda h: _hdot(ti_ref[0, h], dSa[h], TN))
        for h in range(H):
            dS_ref[h] = G[h] + _hdot(dZ[h], hd(At, h), TN) + _hdot(hd(dY, h), hd(Rt, h), TN)
        dA1 = _heads(lambda h: jnp.where(strict, _hdot(dZ[h], hd(Sa, h), NT), 0.0))
        dA2 = _heads(lambda h: jnp.where(strict, _hdot(dZ[h], hd(V, h), NT), 0.0))
        dW1 = _heads(lambda h: jnp.where(incl, _hdot(hd(dY, h), hd(Sa, h), NT), 0.0))
        dW2 = _heads(lambda h: jnp.where(incl, _hdot(hd(dY, h), hd(V, h), NT), 0.0))
        cat = lambda fn: jnp.concatenate(_heads(fn), axis=1)
        dV = cat(lambda h: _hdot(A2[h], dZ[h], TN) + _hdot(W2[h], hd(dY, h), TN) + _hdot(hd(Kt, h), G[h], NT))
        dAt = cat(lambda h: _hdot(dA1[h], hd(Bt, h), NN) + _hdot(dA2[h], hd(Kt, h), NN) + _hdot(dZ[h], S0[h], NN))
        dBt = cat(lambda h: _hdot(dA1[h], hd(At, h), TN) + _hdot(dW1[h], hd(Rt, h), TN) + _hdot(hd(Sa, h), G[h], NN))
        dKt = cat(lambda h: _hdot(dA2[h], hd(At, h), TN) + _hdot(dW2[h], hd(Rt, h), TN) + _hdot(hd(V, h), G[h], NN))
        dRt = cat(lambda h: _hdot(hd(dY, h), S0[h], NN) + _hdot(dW1[h], hd(Bt, h), NN) + _hdot(dW2[h], hd(Kt, h), NN))
        last_row = lax.broadcasted_iota(jnp.int32, (RW_CHUNK, RW_WIDTH), 0) == RW_CHUNK - 1
        dc_prev = dAt * At
        dc = dc_prev + dRt * Rt - dBt * Bt - dKt * Kt + jnp.where(last_row, dc_last, 0.0)
        dr_ref[...] = dRt * q["e_pos"]
        dlw_ref[...] = _hdot(incl_f, dc, TN) - dc_prev
        dk_ref[...] = dKt * q["e_neg"]
        da_ref[...] = dAt * q["e_prev"]
        db_ref[...] = dBt * q["e_neg"]
        dv_ref[...] = dV

    rev = lambda n: nC - 1 - n
    row_spec = pl.BlockSpec((RW_CHUNK, RW_WIDTH), lambda n: (rev(n), 0))
    st_spec = pl.BlockSpec((1, H, N, N), lambda n: (rev(n), 0, 0, 0))
    row_shape = jax.ShapeDtypeStruct((T, RW_WIDTH), F32)
    return pl.pallas_call(
        body, name="rwkv_chunk_bwd", grid=(nC,),
        in_specs=[row_spec] * 7 + [st_spec, st_spec, row_spec], out_specs=[row_spec] * 6,
        out_shape=[row_shape] * 6, scratch_shapes=[pltpu.VMEM((H, N, N), F32)],
        compiler_params=_params(("arbitrary",)),
    )(r, lw, k, a, b, v, dy, s0, tinv, sa)


def _alibi_slope(head):
    return float(np.float32(2.0 ** (-8.0 * (head + 1) / ATT_HEADS)))


ATT_SPAN = ATT_BLOCK * max(ATT_GROUP_DILATION)
ATT_PAIR_WIDTH = 2 * ATT_HEAD_DIM
ATT_SIDE_BY_SIDE = 8


def _pair_slope(g, hp, j):
    return jnp.where(hp == 0, _alibi_slope(4 * g + j), _alibi_slope(4 * g + 2 + j))


def _att_rows(mi, r, d):
    start = mi * ATT_BLOCK * d + r
    return pl.ds(start, ATT_BLOCK) if d == 1 else pl.ds(start, ATT_BLOCK, stride=d)


def _att_masks():
    qi = lax.broadcasted_iota(jnp.int32, (ATT_BLOCK, ATT_BLOCK), 0)
    kj = lax.broadcasted_iota(jnp.int32, (ATT_BLOCK, ATT_BLOCK), 1)
    return qi, kj


NEG = -1e30


def _att_logits(q, k, slope_d, steps, valid):
    s = lax.dot_general(q.astype(BF16), k.astype(BF16), (((1,), (1,)), ((), ())),
                        preferred_element_type=F32) * (ATT_HEAD_DIM ** -0.5)
    return jnp.where(valid, s - slope_d * steps.astype(F32), NEG)


def _att_fwd(p_att, g):
    T = p_att.shape[0]
    d = ATT_GROUP_DILATION[g]
    W = ATT_PAIR_WIDTH
    nb = T // ATT_SPAN
    mb = ATT_SPAN // (ATT_BLOCK * d)

    def body(q_ref, kc_ref, kp_ref, vc_ref, vp_ref, o_ref, l_ref):
        hp, n = pl.program_id(0), pl.program_id(1)
        qi, kj = _att_masks()
        slopes = [_pair_slope(g, hp, j) * d for j in range(2)]
        blocks = [(r, mi) for r in range(d) for mi in range(mb)]
        for at in range(0, len(blocks), ATT_SIDE_BY_SIDE):
            tasks = []
            for r, mi in blocks[at:at + ATT_SIDE_BY_SIDE]:
                rows = _att_rows(mi, r, d)
                if mi > 0:
                    prev = _att_rows(mi - 1, r, d)
                    kp, vp, has_prev = kc_ref[prev, :], vc_ref[prev, :], True
                else:
                    prev = _att_rows(mb - 1, r, d)
                    kp, vp, has_prev = kp_ref[prev, :], vp_ref[prev, :], n > 0
                q, kc, vc = q_ref[rows, :], kc_ref[rows, :], vc_ref[rows, :]
                for j in range(2):
                    sl = slice(j * ATT_HEAD_DIM, (j + 1) * ATT_HEAD_DIM)
                    tasks.append((q[:, sl], kc[:, sl], kp[:, sl], vc[:, sl], vp[:, sl], has_prev, slopes[j]))
            lc = [_att_logits(t[0], t[1], t[6], qi - kj, kj <= qi) for t in tasks]
            lp = [_att_logits(t[0], t[2], t[6], qi - kj + ATT_BLOCK, (kj >= qi) & t[5]) for t in tasks]
            mx = [jnp.maximum(jnp.max(a, axis=1, keepdims=True), jnp.max(b, axis=1, keepdims=True))
                  for a, b in zip(lc, lp, strict=True)]
            ec = [jnp.exp(a - m) for a, m in zip(lc, mx, strict=True)]
            ep = [jnp.exp(b - m) for b, m in zip(lp, mx, strict=True)]
            den = [jnp.sum(a, axis=1, keepdims=True) + jnp.sum(b, axis=1, keepdims=True)
                   for a, b in zip(ec, ep, strict=True)]
            inv = [1.0 / s for s in den]
            outs = [jnp.dot((a * i).astype(BF16), t[3].astype(BF16), preferred_element_type=F32)
                    + jnp.dot((b * i).astype(BF16), t[4].astype(BF16), preferred_element_type=F32)
                    for a, b, i, t in zip(ec, ep, inv, tasks, strict=True)]
            lses = [jnp.broadcast_to(m + jnp.log(s), (ATT_BLOCK, ATT_HEAD_DIM)) for m, s in zip(mx, den, strict=True)]
            for i, (r, mi) in enumerate(blocks[at:at + ATT_SIDE_BY_SIDE]):
                rows = _att_rows(mi, r, d)
                o_ref[rows, :] = jnp.concatenate(outs[2 * i:2 * i + 2], axis=1)
                l_ref[rows, :] = jnp.concatenate(lses[2 * i:2 * i + 2], axis=1)

    def spec(col0, prev):
        if prev:
            return pl.BlockSpec((ATT_SPAN, W), lambda hp, n: (jnp.maximum(n - 1, 0), col0 + 2 * g + hp))
        return pl.BlockSpec((ATT_SPAN, W), lambda hp, n: (n, col0 + 2 * g + hp))

    o_spec = pl.BlockSpec((ATT_SPAN, W), lambda hp, n: (n, hp))
    o, l = pl.pallas_call(
        body, name=f"att_fwd_g{g}", grid=(2, nb),
        in_specs=[spec(0, False), spec(6, False), spec(6, True), spec(12, False), spec(12, True)],
        out_specs=[o_spec, o_spec],
        out_shape=[jax.ShapeDtypeStruct((T, ATT_GROUP_WIDTH), F32)] * 2,
        compiler_params=_params(("parallel", "arbitrary")),
    )(p_att, p_att, p_att, p_att, p_att)
    return o, l


def _att_bwd(p_att, o, l, do, dl, g):
    T = p_att.shape[0]
    d = ATT_GROUP_DILATION[g]
    W = ATT_PAIR_WIDTH
    nb = T // ATT_SPAN
    mb = ATT_SPAN // (ATT_BLOCK * d)
    scale = ATT_HEAD_DIM ** -0.5

    def body(q_ref, k_ref, v_ref, o_ref, l_ref, do_ref, dl_ref,
             qn_ref, on_ref, ln_ref, don_ref, dln_ref, dq_ref, dk_ref, dv_ref, carry_ref):
        hp, n = pl.program_id(0), pl.program_id(1)
        qi, kj = _att_masks()

        @pl.when(n == 0)
        def _():
            carry_ref[...] = jnp.zeros_like(carry_ref)

        slopes = [_pair_slope(g, hp, j) * d for j in range(2)]
        blocks = [(r, mi) for r in range(d) for mi in range(mb)]
        side_by_side = ATT_SIDE_BY_SIDE // 2
        carry = None
        for at in range(0, len(blocks), side_by_side):
            tasks = []
            for r, mi in blocks[at:at + side_by_side]:
                rows = _att_rows(mi, r, d)
                if mi < mb - 1:
                    nrows = _att_rows(mi + 1, r, d)
                    nxt = (q_ref[nrows, :], o_ref[nrows, :], l_ref[nrows, :], do_ref[nrows, :], dl_ref[nrows, :])
                    has_next = True
                else:
                    nrows = _att_rows(0, r, d)
                    nxt = (qn_ref[nrows, :], on_ref[nrows, :], ln_ref[nrows, :], don_ref[nrows, :],
                           dln_ref[nrows, :])
                    has_next = n < nb - 1
                cur = (q_ref[rows, :], o_ref[rows, :], l_ref[rows, :], do_ref[rows, :], dl_ref[rows, :])
                k_all, v_all = k_ref[rows, :], v_ref[rows, :]
                for j in range(2):
                    sl = slice(j * ATT_HEAD_DIM, (j + 1) * ATT_HEAD_DIM)
                    for blk, steps, valid in ((cur, qi - kj, kj <= qi),
                                              (nxt, qi - kj + ATT_BLOCK, (kj >= qi) & has_next)):
                        q, o_, lse, do_, dlse = (z[:, sl] for z in blk)
                        tasks.append(dict(q=q, o=o_, lse=lse[:, :1], do=do_, dlse=dlse[:, :1], steps=steps,
                                          valid=valid, k=k_all[:, sl], vb=v_all[:, sl].astype(BF16),
                                          slope=slopes[j]))
            p = [jnp.exp(_att_logits(t["q"], t["k"], t["slope"], t["steps"], t["valid"]) - t["lse"]) for t in tasks]
            dp = [lax.dot_general(t["do"].astype(BF16), t["vb"], (((1,), (1,)), ((), ())),
                                  preferred_element_type=F32) for t in tasks]
            dsum = [jnp.sum(t["do"] * t["o"], axis=1, keepdims=True) for t in tasks]
            ds = [a * (b - s + t["dlse"]) for a, b, s, t in zip(p, dp, dsum, tasks, strict=True)]
            dv_ = [jnp.dot(a.T.astype(BF16), t["do"].astype(BF16), preferred_element_type=F32)
                   for a, t in zip(p, tasks, strict=True)]
            dk_ = [jnp.dot(a.T.astype(BF16), t["q"].astype(BF16), preferred_element_type=F32) * scale
                   for a, t in zip(ds, tasks, strict=True)]
            dq_ = [jnp.dot(a.astype(BF16), t["k"].astype(BF16), preferred_element_type=F32) * scale
                   for a, t in zip(ds, tasks, strict=True)]
            for i, (r, mi) in enumerate(blocks[at:at + side_by_side]):
                rows = _att_rows(mi, r, d)
                b = 4 * i
                if mi == 0:
                    carry = carry_ref[r]
                dq_ref[rows, :] = jnp.concatenate([dq_[b], dq_[b + 2]], axis=1) + carry
                carry = jnp.concatenate([dq_[b + 1], dq_[b + 3]], axis=1)
                if mi == mb - 1:
                    carry_ref[r] = carry
                dk_ref[rows, :] = jnp.concatenate([dk_[b] + dk_[b + 1], dk_[b + 2] + dk_[b + 3]], axis=1)
                dv_ref[rows, :] = jnp.concatenate([dv_[b] + dv_[b + 1], dv_[b + 2] + dv_[b + 3]], axis=1)

    head_rows = ATT_BLOCK * d
    nxt_n = lambda n: jnp.minimum((n + 1) * mb, T // head_rows - 1)
    cur_p = lambda col0: pl.BlockSpec((ATT_SPAN, W), lambda hp, n: (n, col0 + 2 * g + hp))
    cur_o = pl.BlockSpec((ATT_SPAN, W), lambda hp, n: (n, hp))
    nxt_o = pl.BlockSpec((head_rows, W), lambda hp, n: (nxt_n(n), hp))
    dq, dk, dv = pl.pallas_call(
        body, name=f"att_bwd_g{g}", grid=(2, nb),
        in_specs=[cur_p(0), cur_p(6), cur_p(12), cur_o, cur_o, cur_o, cur_o,
                  pl.BlockSpec((head_rows, W), lambda hp, n: (nxt_n(n), 2 * g + hp)), nxt_o, nxt_o, nxt_o, nxt_o],
        out_specs=[cur_o, cur_o, cur_o],
        out_shape=[jax.ShapeDtypeStruct((T, ATT_GROUP_WIDTH), F32)] * 3,
        scratch_shapes=[pltpu.VMEM((d, ATT_BLOCK, W), F32)],
        compiler_params=_params(("parallel", "arbitrary")),
    )(p_att, p_att, p_att, o, l, do, dl, p_att, o, l, do, dl)
    return dq, dk, dv


RKV = 3 * RW_WIDTH
WA = 128
XG = 160
RW_COLS = RKV + WA + XG


def _local_step(x, p, W, target):
    T = x.shape[0]
    tT = 256
    bd512 = _block_diag_ones(RW_WIDTH, RW_HEAD_DIM)
    bd256 = _block_diag_ones(ATT_GROUP_WIDTH, ATT_HEAD_DIM)
    G = {}

    w_in = W["w_in"]
    w_rkv, w_wa, w_xg, w_att = (w_in[:, :RKV], w_in[:, RKV:RKV + WA], w_in[:, RKV + WA:RW_COLS],
                                w_in[:, RW_COLS:])
    mu = W["rw_mu"]
    mu_rkv, mu_wa, mu_xg = mu[:, :RKV], mu[:, RKV:RKV + WA], mu[:, RKV + WA:]
    zpad = jnp.zeros((64, RW_WIDTH), W["rw_w_up"].dtype)
    w_up_pad = jnp.concatenate([W["rw_w_up"], zpad], axis=0)
    a_up_pad = jnp.concatenate([zpad, W["rw_a_up"]], axis=0)
    r_k = W["rw_r_k"].reshape(1, RW_WIDTH)

    (h,) = _rowwise("norm_mix", lambda i, n, r, pv, nx, c: [_rms_fwd(r[0], c[0])], T, tT,
                    rows=[x], consts=[W["g_mix"]], outs=[("row", D_MODEL, BF16)])
    p_rkv = _mm("proj_rkv", h, w_rkv, "nn")
    p_wa = _mm("proj_wa", h, w_wa, "nn")
    p_xg = _mm("proj_xg", h, w_xg, "nn")
    p_att = _mm("proj_att", h, w_att, "nn", tn=768)
    z_gate = _mm("proj_gate", h, W["w_gate"], "nn")

    def rw_pre_core(i, rows, prevs, consts):
        prkv, pwa, pxg = rows[:3]
        (mrkv, mwa, mxg, w0, a0, k_k, k_a, wup, aup, gup, bd) = consts[:11]
        m_rkv = prkv + (_shift_down(prkv, prevs[0], i, 1) - prkv) * mrkv
        m_wa = pwa + (_shift_down(pwa, prevs[1], i, 1) - pwa) * mwa
        m_xg = pxg + (_shift_down(pxg, prevs[2], i, 1) - pxg) * mxg
        r, k, v = m_rkv[:, :RW_WIDTH], m_rkv[:, RW_WIDTH:2 * RW_WIDTH], m_rkv[:, 2 * RW_WIDTH:]
        tw = jnp.tanh(m_wa)
        lw = w0 + jnp.dot(tw.astype(BF16), wup.astype(BF16), preferred_element_type=F32)
        wlog = -_softplus(-lw) - 0.5
        log_decay = -jnp.exp(wlog)
        a = _sigmoid(a0 + jnp.dot(m_wa.astype(BF16), aup.astype(BF16), preferred_element_type=F32))
        sg = _sigmoid(m_xg)
        gate = jnp.dot(sg.astype(BF16), gup.astype(BF16), preferred_element_type=F32)
        kkp = k * k_k
        nrm = jnp.sqrt(_segsum(kkp * kkp, bd))
        nrm_c = jnp.maximum(nrm, 1e-12)
        kk = kkp / nrm_c
        k2 = k * (1.0 + (a - 1.0) * k_a)
        return dict(r=r, k=k, v=v, tw=tw, lw=lw, wlog=wlog, log_decay=log_decay, a=a, sg=sg, gate=gate, kkp=kkp,
                    nrm=nrm, nrm_c=nrm_c, kk=kk, k2=k2, m_rkv=m_rkv, m_wa=m_wa, m_xg=m_xg)

    pre_consts = [mu_rkv, mu_wa, mu_xg, W["rw_w0"], W["rw_a0"], W["rw_k_k"], W["rw_k_a"],
                  w_up_pad, a_up_pad, W["rw_g_up"], bd512]

    def rw_pre(i, n, rows, prevs, nexts, consts):
        q = rw_pre_core(i, rows, prevs, consts)
        return [q["r"], q["log_decay"], q["k2"], q["v"], -q["kk"], q["kk"] * q["a"], q["gate"]]

    r_s, w_s, k_s, v_s, a_s, b_s, gate_s = _rowwise(
        "rwkv_pre", rw_pre, T, tT, rows=[p_rkv, p_wa, p_xg], prevs=[p_rkv, p_wa, p_xg], consts=pre_consts,
        outs=[("row", RW_WIDTH, F32)] * 7)
    at_s, bt_s, kt_s, rt_s, a2v_s, w2v_s, tinv_s, w1_s, plast_s = _rwkv_chunk_prep(r_s, w_s, k_s, a_s, b_s, v_s)
    y_scan, sa_s, s0_s = _rwkv_chunk_fwd(v_s, at_s, bt_s, kt_s, rt_s, a2v_s, w2v_s, tinv_s, w1_s, plast_s)

    def rw_post_core(rows, consts):
        y, r, k2, v, gate = rows[:5]
        ln_g, ln_b, rk, bd = consts[:4]
        mean = _segsum(y, bd) * (1.0 / RW_HEAD_DIM)
        yc = y - mean
        var = _segsum(yc * yc, bd) * (1.0 / RW_HEAD_DIM)
        rstd = lax.rsqrt(var + RW_LN_EPS)
        yn = yc * rstd
        s = _segsum(r * k2 * rk, bd)
        return dict(yn=yn, rstd=rstd, s=s, pre=yn * ln_g + ln_b + s * v)

    post_consts = [W["rw_ln_g"], W["rw_ln_b"], r_k, bd512]
    (y_a,) = _rowwise("rwkv_post", lambda i, n, r, pv, nx, c: [rw_post_core(r, c)["pre"] * r[4]], T, tT,
                      rows=[y_scan, r_s, k_s, v_s, gate_s], consts=post_consts, outs=[("row", RW_WIDTH, BF16)])

    att = [_att_fwd(p_att, g) for g in range(3)]

    def comb_weights(ls):
        mx = jnp.maximum(jnp.maximum(ls[0], ls[1]), ls[2])
        es = [jnp.exp(l - mx) for l in ls]
        den = es[0] + es[1] + es[2]
        return [e / den for e in es]

    def att_comb(i, n, rows, pv, nx, c):
        wts = comb_weights(rows[3:6])
        return [wts[0] * rows[0] + wts[1] * rows[1] + wts[2] * rows[2]]

    (y_b,) = _rowwise("att_combine", att_comb, T, tT, rows=[att[0][0], att[1][0], att[2][0], att[0][1], att[1][1],
                                                            att[2][1]], outs=[("row", ATT_GROUP_WIDTH, BF16)])

    br_a = _mm("branch_a", y_a, W["w_branch_a"], "nn")
    br_b = _mm("branch_b", y_b, W["w_branch_b"], "nn")

    def merge(i, n, rows, pv, nx, c):
        gates = _sigmoid(rows[0] + c[0])
        return [gates[:, :D_MODEL] * rows[1] + gates[:, D_MODEL:] * rows[2]]

    (merged,) = _rowwise("merge", merge, T, tT, rows=[z_gate, br_a, br_b], consts=[W["b_gate"]],
                         outs=[("row", D_MODEL, BF16)])
    x1 = _mm("mix_out", merged, W["w_out"], "nn", add=x)

    (h2,) = _rowwise("norm_ffn", lambda i, n, r, pv, nx, c: [_rms_fwd(r[0], c[0])], T, tT,
                     rows=[x1], consts=[W["g_ffn"]], outs=[("row", D_MODEL, BF16)])
    u = _mm("ffn_up", h2, W["w_up"], "nn")

    def conv_core(i, rows, prevs, consts):
        uu, cw, cb = rows[0], consts[0], consts[1]
        u1 = _shift_down(uu, prevs[0], i, 1)
        u2 = _shift_down(uu, prevs[0], i, 2)
        uc = cb + cw[0:1] * uu + cw[1:2] * u1 + cw[2:3] * u2
        return uc[:, :D_FF], uc[:, D_FF:], u1, u2

    def glu(i, n, rows, prevs, nx, consts):
        gate, val, _, _ = conv_core(i, rows, prevs, consts)
        return [_gelu(gate) * val]

    tF = 128
    (act,) = _rowwise("conv_glu", glu, T, tF, rows=[u], prevs=[u], consts=[W["conv_w"], W["conv_b"]],
                      outs=[("row", D_FF, BF16)])
    x2 = _mm("ffn_down", act, W["w_down"], "nn", add=x1)

    (h3,) = _rowwise("norm_ple", lambda i, n, r, pv, nx, c: [_rms_fwd(r[0], c[0])], T, tT,
                     rows=[x2], consts=[W["g_ple"]], outs=[("row", D_MODEL, BF16)])
    z_ple = _mm("ple_gate", h3, W["w_ple_gate"], "nn")
    e_ple = _mm("ple_emb", p, W["w_ple"], "nn")

    def head(i, n, rows, pv, nx, consts):
        x2_, z, e, tgt = rows
        pg = _sigmoid(z)
        x3 = x2_ + pg * e
        y = _rms_fwd(x3, consts[0])
        err = y - tgt
        loss = 0.5 * jnp.sum(jnp.sum(err * err, axis=1, keepdims=True) * (1.0 / D_MODEL), axis=0, keepdims=True)
        dy = err * (1.0 / D_MODEL)
        dx3, dgf = _rms_bwd(x3, consts[0], dy)
        return [dx3, dx3 * pg, dx3 * e * pg * (1.0 - pg), jnp.broadcast_to(loss, (1, LANES)), _colsum(dgf)]

    dx3, de, dz, loss_acc, G["g_final"] = _rowwise(
        "loss_head", head, T, tT, rows=[x2, z_ple, e_ple, target], consts=[W["g_final"].reshape(1, D_MODEL)],
        outs=[("row", D_MODEL, F32), ("row", D_MODEL, BF16), ("row", D_MODEL, BF16), ("acc", (1, LANES)),
              ("acc", (1, D_MODEL))])
    G["w_ple"] = _mm("d_w_ple", p, de, "tn")
    G["w_ple_gate"] = _mm("d_w_ple_gate", h3, dz, "tn")
    dh3 = _mm("d_h3", dz, W["w_ple_gate"], "nt")

    def norm_bwd(i, n, rows, pv, nx, consts):
        dx, dg = _rms_bwd(rows[0], consts[0], rows[1])
        return [rows[2] + dx, _colsum(dg)]

    dx2, G["g_ple"] = _rowwise("d_norm_ple", norm_bwd, T, tT, rows=[x2, dh3, dx3], consts=[W["g_ple"]],
                               outs=[("row", D_MODEL, F32), ("acc", (1, D_MODEL))])

    dact = _mm("d_act", dx2, W["w_down"], "nt")
    G["w_down"] = _mm("d_w_down", act, dx2, "tn")

    def glu_grad(gate, val, da):
        return jnp.concatenate([da * val * _gelu_grad(gate), da * _gelu(gate)], axis=1)

    def glu_bwd(i, n, rows, prevs, nexts, consts):
        uu, da = rows
        cw = consts[0]
        gate, val, u1, u2 = conv_core(i, rows, prevs, consts)
        duc = glu_grad(gate, val, da)
        dcw = jnp.concatenate([_colsum(duc * uu), _colsum(duc * u1), _colsum(duc * u2)], axis=0)
        gate_n, val_n, _, _ = conv_core(1, [nexts[0]], [uu[tF - SUBLANES:]], consts)
        duc_n = glu_grad(gate_n, val_n, nexts[1])
        du = (cw[0:1] * duc + cw[1:2] * _shift_up(duc, duc_n, i, n, 1) + cw[2:3] * _shift_up(duc, duc_n, i, n, 2))
        return [du, _colsum(duc), dcw]

    du, G["conv_b"], G["conv_w"] = _rowwise(
        "d_conv_glu", glu_bwd, T, tF, rows=[u, dact], prevs=[u], nexts=[u, dact],
        consts=[W["conv_w"], W["conv_b"]],
        outs=[("row", 2 * D_FF, BF16), ("acc", (1, 2 * D_FF)), ("acc", (3, 2 * D_FF))])
    G["w_up"] = _mm("d_w_up", h2, du, "tn")
    dh2 = _mm("d_h2", du, W["w_up"], "nt")
    dx1, G["g_ffn"] = _rowwise("d_norm_ffn", norm_bwd, T, tT, rows=[x1, dh2, dx2], consts=[W["g_ffn"]],
                               outs=[("row", D_MODEL, F32), ("acc", (1, D_MODEL))])

    dmerged = _mm("d_merged", dx1, W["w_out"], "nt")
    G["w_out"] = _mm("d_w_out", merged, dx1, "tn")

    def merge_bwd(i, n, rows, pv, nx, consts):
        z, a_, b_, dm = rows
        gates = _sigmoid(z + consts[0])
        ga, gb = gates[:, :D_MODEL], gates[:, D_MODEL:]
        dz_ = jnp.concatenate([dm * a_ * ga * (1.0 - ga), dm * b_ * gb * (1.0 - gb)], axis=1)
        return [dm * ga, dm * gb, dz_, _colsum(dz_)]

    d_br_a, d_br_b, dz_gate, G["b_gate"] = _rowwise(
        "d_merge", merge_bwd, T, tT, rows=[z_gate, br_a, br_b, dmerged], consts=[W["b_gate"]],
        outs=[("row", D_MODEL, BF16), ("row", D_MODEL, BF16), ("row", 2 * D_MODEL, BF16), ("acc", (1, 2 * D_MODEL))])
    G["w_branch_a"] = _mm("d_w_branch_a", y_a, d_br_a, "tn")
    G["w_branch_b"] = _mm("d_w_branch_b", y_b, d_br_b, "tn")
    G["w_gate"] = _mm("d_w_gate", h, dz_gate, "tn")
    dy_a = _mm("d_y_a", d_br_a, W["w_branch_a"], "nt")
    dy_b = _mm("d_y_b", d_br_b, W["w_branch_b"], "nt")

    def att_comb_bwd(i, n, rows, pv, nx, consts):
        os_, ls, dy = rows[0:3], rows[3:6], rows[6]
        wts = comb_weights(ls)
        dws = [_segsum(dy * o_, consts[0]) for o_ in os_]
        mix = wts[0] * dws[0] + wts[1] * dws[1] + wts[2] * dws[2]
        return [wts[g_] * dy for g_ in range(3)] + [wts[g_] * (dws[g_] - mix) for g_ in range(3)]

    comb = _rowwise("d_att_combine", att_comb_bwd, T, tT,
                    rows=[att[0][0], att[1][0], att[2][0], att[0][1], att[1][1], att[2][1], dy_b], consts=[bd256],
                    outs=[("row", ATT_GROUP_WIDTH, F32)] * 6)
    dqkv = [_att_bwd(p_att, att[g][0], att[g][1], comb[g], comb[3 + g], g) for g in range(3)]
    dp_att = jnp.concatenate([dqkv[g][part] for part in range(3) for g in range(3)], axis=1).astype(BF16)

    def rw_post_bwd(i, n, rows, pv, nx, consts):
        y, r, k2, v, gate, dya = rows
        ln_g, ln_b, rk, bd = consts
        q = rw_post_core(rows, consts)
        dpre = dya * gate
        dgate = dya * q["pre"]
        dyn = dpre * ln_g
        inv = 1.0 / RW_HEAD_DIM
        dy_scan = q["rstd"] * (dyn - _segsum(dyn, bd) * inv - q["yn"] * (_segsum(dyn * q["yn"], bd) * inv))
        ds = _segsum(dpre * v, bd)
        return [dy_scan, dgate, ds * k2 * rk, ds * r * rk, dpre * q["s"],
                _colsum(dpre * q["yn"]), _colsum(dpre), _colsum(ds * r * k2)]

    dy_scan, dgate, dr_b, dk2_b, dv_b, G["rw_ln_g"], G["rw_ln_b"], d_rk = _rowwise(
        "d_rwkv_post", rw_post_bwd, T, tT, rows=[y_scan, r_s, k_s, v_s, gate_s, dy_a], consts=post_consts,
        outs=[("row", RW_WIDTH, F32)] * 5 + [("acc", (1, RW_WIDTH))] * 3)
    G["rw_r_k"] = d_rk.reshape(RW_HEADS, RW_HEAD_DIM)

    dr_s, dw_s, dk_s, da_s, db_s, dv_s = _rwkv_chunk_bwd(r_s, w_s, k_s, a_s, b_s, v_s, dy_scan, s0_s, tinv_s, sa_s)

    def rw_pre_bwd(i, n, rows, prevs, nx, consts):
        q = rw_pre_core(i, rows, prevs, consts)
        (mrkv, mwa, mxg, w0, a0, k_k, k_a, wup, aup, gup, bd) = consts
        dr, dlogdecay, dk2, dv, dav, dbv, dgate_ = rows[3:10]
        dr = dr + rows[10]
        dk2 = dk2 + rows[11]
        dv = dv + rows[12]
        a, k, kk = q["a"], q["k"], q["kk"]
        dk = dk2 * (1.0 + (a - 1.0) * k_a)
        da = dk2 * k * k_a + dbv * kk
        dkk = dbv * a - dav
        live = q["nrm"] > 1e-12
        dkkp = jnp.where(live, dkk - kk * _segsum(dkk * kk, bd), dkk) / q["nrm_c"]
        dk = dk + dkkp * k_k
        dlw = dlogdecay * q["log_decay"] * _sigmoid(-q["lw"])
        dla = da * a * (1.0 - a)
        nt = (((1,), (1,)), ((), ()))
        dtw = lax.dot_general(dlw.astype(BF16), wup.astype(BF16), nt, preferred_element_type=F32)
        dxa = lax.dot_general(dla.astype(BF16), aup.astype(BF16), nt, preferred_element_type=F32)
        dm_wa = dtw * (1.0 - q["tw"] * q["tw"]) + dxa
        dsg = lax.dot_general(dgate_.astype(BF16), gup.astype(BF16), nt, preferred_element_type=F32)
        dm_xg = dsg * q["sg"] * (1.0 - q["sg"])
        dm_rkv = jnp.concatenate([dr, dk, dv], axis=1)
        prkv, pwa, pxg = rows[:3]
        dmu = jnp.concatenate([_colsum(dm_rkv * (_shift_down(prkv, prevs[0], i, 1) - prkv)),
                               _colsum(dm_wa * (_shift_down(pwa, prevs[1], i, 1) - pwa)),
                               _colsum(dm_xg * (_shift_down(pxg, prevs[2], i, 1) - pxg))], axis=1)
        return [dm_rkv, dm_wa, dm_xg, dlw, dla, q["tw"], q["m_wa"], q["sg"], dmu,
                _colsum(dlw), _colsum(dla), _colsum(dkkp * k), _colsum(dk2 * k * (a - 1.0))]

    (dm_rkv, dm_wa, dm_xg, dlw, dla, tw_s, mwa_s, sg_s, G["rw_mu"], G["rw_w0"], G["rw_a0"], G["rw_k_k"],
     G["rw_k_a"]) = _rowwise(
        "d_rwkv_pre", rw_pre_bwd, T, tT,
        rows=[p_rkv, p_wa, p_xg, dr_s, dw_s, dk_s, dv_s, da_s, db_s, dgate, dr_b, dk2_b, dv_b],
        prevs=[p_rkv, p_wa, p_xg], consts=pre_consts,
        outs=[("row", RKV, F32), ("row", WA, F32), ("row", XG, F32), ("row", RW_WIDTH, BF16),
              ("row", RW_WIDTH, BF16), ("row", WA, BF16), ("row", WA, BF16), ("row", XG, BF16),
              ("acc", (1, RW_COLS))] + [("acc", (1, RW_WIDTH))] * 4)
    G["rw_w_up"] = _mm("d_rw_w_up", tw_s, dlw, "tn")[:64]
    G["rw_a_up"] = _mm("d_rw_a_up", mwa_s, dla, "tn")[64:]
    G["rw_g_up"] = _mm("d_rw_g_up", sg_s, dgate, "tn")

    def shift_bwd(i, n, rows, pv, nexts, consts):
        return [rows[j] * (1.0 - consts[j]) + _shift_up(rows[j], nexts[j], i, n, 1) * consts[j] for j in range(3)]

    dp_rkv, dp_wa, dp_xg = _rowwise(
        "d_token_shift", shift_bwd, T, tT, rows=[dm_rkv, dm_wa, dm_xg], nexts=[dm_rkv, dm_wa, dm_xg],
        consts=[mu_rkv, mu_wa, mu_xg], outs=[("row", RKV, BF16), ("row", WA, BF16), ("row", XG, BF16)])

    G["w_in"] = jnp.concatenate([_mm("d_w_rkv", h, dp_rkv, "tn"), _mm("d_w_wa", h, dp_wa, "tn"),
                                 _mm("d_w_xg", h, dp_xg, "tn"), _mm("d_w_att", h, dp_att, "tn", tn=768)], axis=1)
    dh = _mm("d_h_gate", dz_gate, W["w_gate"], "nt")
    dh = _mm("d_h_rkv", dp_rkv, w_rkv, "nt", add=dh)
    dh = _mm("d_h_wa", dp_wa, w_wa, "nt", add=dh)
    dh = _mm("d_h_xg", dp_xg, w_xg, "nt", add=dh)
    dh = _mm("d_h_att", dp_att, w_att, "nt", add=dh)
    dx, G["g_mix"] = _rowwise("d_norm_mix", norm_bwd, T, tT, rows=[x, dh, dx1], consts=[W["g_mix"]],
                              outs=[("row", D_MODEL, F32), ("acc", (1, D_MODEL))])
    return loss_acc[:, :1], dx, G


HBM_SPEC = pl.BlockSpec(memory_space=pltpu.HBM)


def _place():
    x, y, c = lax.axis_index("x"), lax.axis_index("y"), lax.axis_index("c")
    return x, y, c, [(1 - x, y), (x, 1 - y), (1 - x, 1 - y)]


def _remote(src, dst, send_sems, recv_sems, k, to):
    return pltpu.make_async_remote_copy(src_ref=src, dst_ref=dst, send_sem=send_sems.at[k], recv_sem=recv_sems.at[k],
                                        device_id=to, device_id_type=MESH)


ROW_ALIGN = 16


def _half_rows(ref_rows, c, first):
    half = ref_rows // 2
    which = c if first else 1 - c
    return pl.ds(pl.multiple_of(which * half, ROW_ALIGN), half)


def _gather_chips(shards):
    n = len(shards)
    split = [s.shape[0] % (2 * ROW_ALIGN) == 0 for s in shards]

    def body(*refs):
        w_refs, out_refs = refs[:n], refs[n:2 * n]
        send_sems, recv_sems = refs[2 * n:]
        x, y, c, chips = _place()
        me = 2 * x + y
        sends, passed = [], []
        for i in range(n):
            for j, (px, py) in enumerate(chips):
                if split[i]:
                    mine = _half_rows(w_refs[i].shape[0], c, True)
                    cp = _remote(w_refs[i].at[mine], out_refs[i].at[me, mine], send_sems, recv_sems, 6 * i + j,
                                 (px, py, c))
                else:
                    cp = _remote(w_refs[i], out_refs[i].at[me], send_sems, recv_sems, 6 * i + j, (px, py, c))
                cp.start()
                sends.append(cp)
        for i in range(n):
            for j, (px, py) in enumerate(chips):
                if split[i]:
                    landed = out_refs[i].at[2 * px + py, _half_rows(w_refs[i].shape[0], c, True)]
                    _remote(landed, landed, send_sems, recv_sems, 6 * i + j, (px, py, c)).wait_recv()
                    cp = _remote(landed, landed, send_sems, recv_sems, 6 * i + 3 + j, (x, y, 1 - c))
                    cp.start()
                    passed.append(cp)
                else:
                    landed = out_refs[i].at[2 * px + py]
                    _remote(landed, landed, send_sems, recv_sems, 6 * i + j, (px, py, c)).wait_recv()
        for i in range(n):
            if split[i]:
                for j, (px, py) in enumerate(chips):
                    landed = out_refs[i].at[2 * px + py, _half_rows(w_refs[i].shape[0], c, False)]
                    _remote(landed, landed, send_sems, recv_sems, 6 * i + 3 + j, (x, y, 1 - c)).wait_recv()
        for cp in sends + passed:
            cp.wait_send()

    outs = pl.pallas_call(
        body, name="gather_weights", in_specs=[HBM_SPEC] * n, out_specs=[HBM_SPEC] * n,
        out_shape=[jax.ShapeDtypeStruct((N_CHIPS,) + s.shape, s.dtype) for s in shards],
        scratch_shapes=[pltpu.SemaphoreType.DMA((6 * n,)), pltpu.SemaphoreType.DMA((6 * n,))],
    )(*shards)
    me = 2 * lax.axis_index("x") + lax.axis_index("y")
    return [lax.dynamic_update_slice(o, s[None], (me, 0, 0)) for o, s in zip(outs, shards, strict=True)]


def _swap_halves(gs):
    n = len(gs)

    def body(*refs):
        g_refs, out_refs = refs[:n], refs[n:2 * n]
        send_sems, recv_sems = refs[2 * n:]
        x, y, c, _ = _place()
        cps = []
        for i in range(n):
            theirs = _half_rows(g_refs[i].shape[1], c, False)
            cp = _remote(g_refs[i].at[:, theirs, :], out_refs[i], send_sems, recv_sems, i, (x, y, 1 - c))
            cp.start()
            cps.append(cp)
        for cp in cps:
            cp.wait()

    return pl.pallas_call(
        body, name="swap_halves", in_specs=[HBM_SPEC] * n, out_specs=[HBM_SPEC] * n,
        out_shape=[jax.ShapeDtypeStruct((N_CHIPS, g.shape[1] // 2, g.shape[2]), g.dtype) for g in gs],
        scratch_shapes=[pltpu.SemaphoreType.DMA((n,)), pltpu.SemaphoreType.DMA((n,))],
    )(*gs)


def _scatter_chips(parts):
    n = len(parts)

    def body(*refs):
        p_refs, out_refs = refs[:n], refs[n:2 * n]
        send_sems, recv_sems = refs[2 * n:]
        x, y, c, chips = _place()
        me = 2 * x + y
        sends = []
        for i in range(n):
            for j, (px, py) in enumerate(chips):
                cp = _remote(p_refs[i].at[2 * px + py], out_refs[i].at[me], send_sems, recv_sems, 3 * i + j,
                             (px, py, c))
                cp.start()
                sends.append(cp)
        for i in range(n):
            for j, (px, py) in enumerate(chips):
                landed = out_refs[i].at[2 * px + py]
                _remote(landed, landed, send_sems, recv_sems, 3 * i + j, (px, py, c)).wait_recv()
        for cp in sends:
            cp.wait_send()

    outs = pl.pallas_call(
        body, name="scatter_grads", in_specs=[HBM_SPEC] * n, out_specs=[HBM_SPEC] * n,
        out_shape=[jax.ShapeDtypeStruct(p.shape, p.dtype) for p in parts],
        scratch_shapes=[pltpu.SemaphoreType.DMA((3 * n,)), pltpu.SemaphoreType.DMA((3 * n,))],
    )(*parts)
    me = 2 * lax.axis_index("x") + lax.axis_index("y")
    own = [lax.dynamic_slice_in_dim(p, me, 1, axis=0) for p in parts]
    return [lax.dynamic_update_slice(o, s, (me, 0, 0)) for o, s in zip(outs, own, strict=True)]


def _join_halves(reds):
    n = len(reds)

    def body(*refs):
        r_refs, out_refs = refs[:n], refs[n:2 * n]
        send_sems, recv_sems = refs[2 * n:]
        x, y, c, _ = _place()
        cps = []
        for i in range(n):
            mine = _half_rows(out_refs[i].shape[0], c, True)
            cp = _remote(r_refs[i], out_refs[i].at[mine], send_sems, recv_sems, i, (x, y, 1 - c))
            cp.start()
            cps.append(cp)
        for cp in cps:
            cp.wait()

    outs = pl.pallas_call(
        body, name="join_halves", in_specs=[HBM_SPEC] * n, out_specs=[HBM_SPEC] * n,
        out_shape=[jax.ShapeDtypeStruct((2 * r.shape[0], r.shape[1]), r.dtype) for r in reds],
        scratch_shapes=[pltpu.SemaphoreType.DMA((n,)), pltpu.SemaphoreType.DMA((n,))],
    )(*reds)
    c = lax.axis_index("c")
    return [lax.dynamic_update_slice(o, r, (c * r.shape[0], 0)) for o, r in zip(outs, reds, strict=True)]


def _gather_all(vec):
    R = vec.shape[0]

    def body(v_ref, out_ref, send_sems, recv_sems, local_sem):
        x, y, c, _ = _place()
        me = 4 * x + 2 * y + c
        local = pltpu.make_async_copy(v_ref, out_ref.at[me], local_sem)
        local.start()
        peers = [(x ^ (k >> 2), y ^ ((k >> 1) & 1), c ^ (k & 1)) for k in range(1, N_DEV)]
        sends = [_remote(v_ref, out_ref.at[me], send_sems, recv_sems, k, to) for k, to in enumerate(peers)]
        for cp in sends:
            cp.start()
        for k, (px, py, pc) in enumerate(peers):
            landed = out_ref.at[4 * px + 2 * py + pc]
            _remote(landed, landed, send_sems, recv_sems, k, (px, py, pc)).wait_recv()
        for cp in sends:
            cp.wait_send()
        local.wait()

    return pl.pallas_call(
        body, name="gather_small", in_specs=[HBM_SPEC], out_specs=HBM_SPEC,
        out_shape=jax.ShapeDtypeStruct((N_DEV, R, LANES), vec.dtype),
        scratch_shapes=[pltpu.SemaphoreType.DMA((7,)), pltpu.SemaphoreType.DMA((7,)), pltpu.SemaphoreType.DMA],
    )(vec)


SUM_TILE_BYTES = 4 * 1024 * 1024


def _sum_rows(half, cols):
    best = ROW_ALIGN
    for t in range(ROW_ALIGN, half + 1, ROW_ALIGN):
        if half % t == 0 and N_CHIPS * t * cols * 4 <= SUM_TILE_BYTES:
            best = t
    return best


def _sum_cores(name, g, theirs, core):
    _, R, C = g.shape
    half = R // 2
    tr = _sum_rows(half, C)
    nb = half // tr

    def body(core_ref, g_ref, t_ref, o_ref):
        o_ref[...] = (g_ref[...] + t_ref[...]).astype(o_ref.dtype)

    grid_spec = pltpu.PrefetchScalarGridSpec(
        num_scalar_prefetch=1, grid=(nb,),
        in_specs=[pl.BlockSpec((N_CHIPS, tr, C), lambda i, core_ref: (0, core_ref[0] * nb + i, 0)),
                  pl.BlockSpec((N_CHIPS, tr, C), lambda i, core_ref: (0, i, 0))],
        out_specs=pl.BlockSpec((N_CHIPS, tr, C), lambda i, core_ref: (0, i, 0)))
    return pl.pallas_call(
        body, name=name, grid_spec=grid_spec, out_shape=jax.ShapeDtypeStruct((N_CHIPS, half, C), BF16),
        compiler_params=_params(("parallel",)),
    )(core, g, theirs)


def _sum_chips(name, parts):
    _, H, C = parts.shape
    tr = _sum_rows(H, C)

    def body(p_ref, o_ref):
        acc = p_ref[0].astype(F32)
        for k in range(1, N_CHIPS):
            acc = acc + p_ref[k].astype(F32)
        o_ref[...] = acc

    return pl.pallas_call(
        body, name=name, grid=(H // tr,),
        in_specs=[pl.BlockSpec((N_CHIPS, tr, C), lambda i: (0, i, 0))],
        out_specs=pl.BlockSpec((tr, C), lambda i: (i, 0)),
        out_shape=jax.ShapeDtypeStruct((H, C), F32),
        compiler_params=_params(("parallel",)),
    )(parts)


def _adamw_math(w, g, m, v):
    m = ADAM_B1 * m + (1.0 - ADAM_B1) * g
    v = ADAM_B2 * v + (1.0 - ADAM_B2) * (g * g)
    m_hat = m / (1.0 - ADAM_B1 ** ADAM_STEP)
    v_hat = v / (1.0 - ADAM_B2 ** ADAM_STEP)
    delta = -ADAM_LR * (m_hat / (jnp.sqrt(v_hat) + ADAM_EPS) + ADAM_WD * w)
    return delta, m, v


def _adamw(name, w, g, m, v):
    R, C = w.shape
    tr = R
    if R % SUBLANES == 0:
        for cand in range(SUBLANES, min(R, 256) + 1, SUBLANES):
            if R % cand == 0:
                tr = cand

    def body(w_ref, g_ref, m_ref, v_ref, d_ref, nm_ref, nv_ref):
        d, nm, nv = _adamw_math(w_ref[...], g_ref[...], m_ref[...], v_ref[...])
        d_ref[...] = d
        nm_ref[...] = nm
        nv_ref[...] = nv

    spec = pl.BlockSpec((tr, C), lambda i: (i, 0))
    shape = jax.ShapeDtypeStruct((R, C), F32)
    return pl.pallas_call(
        body, name=name, grid=(R // tr,), in_specs=[spec] * 4, out_specs=[spec] * 3, out_shape=[shape] * 3,
        compiler_params=_params(("parallel",)),
    )(w, g, m, v)


def _adamw_small(parts, w, m, v):
    n = parts.shape[0]

    def body(p_ref, w_ref, m_ref, v_ref, g_ref, d_ref, nm_ref, nv_ref):
        g = p_ref[0]
        for k in range(1, n):
            g = g + p_ref[k]
        d, nm, nv = _adamw_math(w_ref[...], g, m_ref[...], v_ref[...])
        g_ref[...] = g
        d_ref[...] = d
        nm_ref[...] = nm
        nv_ref[...] = nv

    shape = jax.ShapeDtypeStruct(w.shape, F32)
    return pl.pallas_call(body, name="adamw_small", out_shape=[shape] * 4, compiler_params=_params())(parts, w, m, v)


WEIGHTS = ['g_mix', 'w_in', 'rw_mu', 'rw_w0', 'rw_w_up', 'rw_a0', 'rw_a_up', 'rw_g_up', 'rw_k_k', 'rw_k_a',
           'rw_r_k', 'rw_ln_g', 'rw_ln_b', 'w_branch_a', 'w_branch_b', 'w_gate', 'b_gate', 'w_out', 'g_ffn', 'w_up',
           'conv_w', 'conv_b', 'w_down', 'g_ple', 'w_ple_gate', 'w_ple', 'g_final']
ARG_NAMES = (['x', 'p'] + WEIGHTS + ['loss_target'] + ['m_' + n for n in WEIGHTS] + ['v_' + n for n in WEIGHTS])
SHARDED = {'w_in': 1, 'rw_w_up': 1, 'rw_a_up': 1, 'rw_g_up': 1, 'w_branch_a': 1, 'w_branch_b': 1, 'w_gate': 1,
           'w_out': 0, 'w_up': 1, 'conv_w': 1, 'w_down': 0, 'w_ple_gate': 0, 'w_ple': 1}
SMALL = [n for n in WEIGHTS if n not in SHARDED]
WHOLE = ['conv_w']
SPLIT = [n for n in SHARDED if n not in WHOLE]
PACK_ALIGN = SUBLANES * LANES


def _pack_rows(flat_parts):
    flat = jnp.concatenate(flat_parts, axis=1)
    n = flat.shape[1]
    padded = -(-n // PACK_ALIGN) * PACK_ALIGN
    flat = jnp.pad(flat, ((0, 0), (0, padded - n)))
    return flat.reshape(padded // LANES, LANES)


def _full_from_shards(stack, axis):
    _, R, C = stack.shape
    if axis == 0:
        return stack.reshape(N_CHIPS * R, C)
    return stack.transpose(1, 0, 2).reshape(R, N_CHIPS * C)


def _shards_from_full(full, axis):
    R, C = full.shape
    if axis == 0:
        return full.reshape(N_CHIPS, R // N_CHIPS, C)
    return full.reshape(R, N_CHIPS, C // N_CHIPS).transpose(1, 0, 2)


def kernel(x, p, g_mix, w_in, rw_mu, rw_w0, rw_w_up, rw_a0, rw_a_up, rw_g_up, rw_k_k, rw_k_a, rw_r_k, rw_ln_g, rw_ln_b, w_branch_a, w_branch_b, w_gate, b_gate, w_out, g_ffn, w_up, conv_w, conv_b, w_down, g_ple, w_ple_gate, w_ple, g_final, loss_target, m_g_mix, m_w_in, m_rw_mu, m_rw_w0, m_rw_w_up, m_rw_a0, m_rw_a_up, m_rw_g_up, m_rw_k_k, m_rw_k_a, m_rw_r_k, m_rw_ln_g, m_rw_ln_b, m_w_branch_a, m_w_branch_b, m_w_gate, m_b_gate, m_w_out, m_g_ffn, m_w_up, m_conv_w, m_conv_b, m_w_down, m_g_ple, m_w_ple_gate, m_w_ple, m_g_final, v_g_mix, v_w_in, v_rw_mu, v_rw_w0, v_rw_w_up, v_rw_a0, v_rw_a_up, v_rw_g_up, v_rw_k_k, v_rw_k_a, v_rw_r_k, v_rw_ln_g, v_rw_ln_b, v_w_branch_a, v_w_branch_b, v_w_gate, v_b_gate, v_w_out, v_g_ffn, v_w_up, v_conv_w, v_conv_b, v_w_down, v_g_ple, v_w_ple_gate, v_w_ple, v_g_final):
    given = dict(zip(ARG_NAMES, (x, p, g_mix, w_in, rw_mu, rw_w0, rw_w_up, rw_a0, rw_a_up, rw_g_up, rw_k_k, rw_k_a, rw_r_k, rw_ln_g, rw_ln_b, w_branch_a, w_branch_b, w_gate, b_gate, w_out, g_ffn, w_up, conv_w, conv_b, w_down, g_ple, w_ple_gate, w_ple, g_final, loss_target, m_g_mix, m_w_in, m_rw_mu, m_rw_w0, m_rw_w_up, m_rw_a0, m_rw_a_up, m_rw_g_up, m_rw_k_k, m_rw_k_a, m_rw_r_k, m_rw_ln_g, m_rw_ln_b, m_w_branch_a, m_w_branch_b, m_w_gate, m_b_gate, m_w_out, m_g_ffn, m_w_up, m_conv_w, m_conv_b, m_w_down, m_g_ple, m_w_ple_gate, m_w_ple, m_g_final, v_g_mix, v_w_in, v_rw_mu, v_rw_w0, v_rw_w_up, v_rw_a0, v_rw_a_up, v_rw_g_up, v_rw_k_k, v_rw_k_a, v_rw_r_k, v_rw_ln_g, v_rw_ln_b, v_w_branch_a, v_w_branch_b, v_w_gate, v_b_gate, v_w_out, v_g_ffn, v_w_up, v_conv_w, v_conv_b, v_w_down, v_g_ple, v_w_ple_gate, v_w_ple, v_g_final), strict=True))

    def two_d(name, prefix=""):
        a = given[prefix + name]
        if name == "g_final":
            return a.reshape(1, D_MODEL)
        if name == "rw_r_k":
            return a.reshape(1, RW_WIDTH)
        return a[0] if a.ndim == 3 else a

    gathered = _gather_chips([two_d(n) if n in WHOLE else two_d(n).astype(BF16) for n in SHARDED])
    W = {n: _full_from_shards(g, SHARDED[n]) for n, g in zip(SHARDED, gathered, strict=True)}
    for n in SMALL:
        W[n] = two_d(n)
    W["rw_r_k"] = W["rw_r_k"].reshape(RW_HEADS, RW_HEAD_DIM)

    loss_part, grad_x, G = _local_step(x[0], p[0, 0], W, loss_target[0])

    core = lax.axis_index("c").astype(jnp.int32).reshape(1)
    by_chip = [_shards_from_full(G[n], SHARDED[n]) for n in SPLIT]
    theirs = _swap_halves(by_chip)
    pair = [_sum_cores("sum_cores_" + n, g, t, core) for n, g, t in zip(SPLIT, by_chip, theirs, strict=True)]
    landed = _scatter_chips(pair)
    reduced = [_sum_chips("sum_chips_" + n, q) for n, q in zip(SPLIT, landed, strict=True)]
    shard_grads = dict(zip(SPLIT, _join_halves(reduced), strict=True))

    small_sizes = {n: two_d(n).shape[1] for n in SMALL}
    n_small = sum(small_sizes.values())
    whole_sizes = {n: G[n].shape[0] * G[n].shape[1] for n in WHOLE}
    n_whole = sum(whole_sizes.values())

    def pack_small(parts, rest):
        return _pack_rows([a.reshape(1, -1) for a in parts] + [rest])

    G["rw_r_k"] = G["rw_r_k"].reshape(1, RW_WIDTH)
    rest = jnp.zeros((1, n_whole + 1), F32)
    all_small = _gather_all(pack_small([G[n] for n in SMALL] + [G[n] for n in WHOLE], loss_part))
    gs, ds, nms, nvs = _adamw_small(all_small, pack_small([two_d(n) for n in SMALL], rest),
                                    pack_small([two_d(n, "m_") for n in SMALL], rest),
                                    pack_small([two_d(n, "v_") for n in SMALL], rest))
    gs, ds, nms, nvs = (a.reshape(-1) for a in (gs, ds, nms, nvs))
    loss = gs[n_small + n_whole]
    chip = 2 * lax.axis_index("x") + lax.axis_index("y")
    off = n_small
    for n in WHOLE:
        full = gs[off:off + whole_sizes[n]].reshape(G[n].shape)
        off += whole_sizes[n]
        width = two_d(n).shape[1]
        shard_grads[n] = lax.dynamic_slice_in_dim(full, chip * width, width, axis=1)

    grads, deltas, new_m, new_v = {}, {}, {}, {}
    for n in SHARDED:
        g = shard_grads[n]
        d, nm, nv = _adamw("adamw_" + n, two_d(n), g, two_d(n, "m_"), two_d(n, "v_"))
        grads[n], deltas[n], new_m[n], new_v[n] = g, d, nm, nv
    off = 0
    for n in SMALL:
        sl = slice(off, off + small_sizes[n])
        off += small_sizes[n]
        grads[n], deltas[n], new_m[n], new_v[n] = gs[sl], ds[sl], nms[sl], nvs[sl]
    outs = [loss, grad_x[None]]
    for table in (grads, deltas, new_m, new_v):
        outs += [table[n].reshape(given[n].shape) for n in WEIGHTS]
    return tuple(outs)
```

```python
import math

import jax
import jax.numpy as jnp
import numpy as np
from jax import lax
from jax.experimental import pallas as pl
from jax.experimental.pallas import tpu as pltpu

F32 = jnp.float32
BF16 = jnp.bfloat16

D_MODEL = 1024
NORM_EPS = 1e-6
RW_HEADS = 8
RW_HEAD_DIM = 64
RW_WIDTH = 512
RW_LN_EPS = 64e-5
ATT_GROUP_DILATION = (1, 4, 16)
ATT_BLOCK = 128
ATT_HEADS = 12
ATT_HEAD_DIM = 64
ATT_GROUP_WIDTH = 256
ATT_WIDTH = 768
D_FF = 3072

ADAM_LR = 0.001
ADAM_B1 = 0.9
ADAM_B2 = 0.999
ADAM_EPS = 1e-08
ADAM_WD = 0.01
ADAM_STEP = 10

SUBLANES = 8
LANES = 128
VMEM_LIMIT = 56 * 1024 * 1024
N_CHIPS = 4
N_DEV = 8
MESH = pl.DeviceIdType.MESH


def _params(sem=None):
    return pltpu.CompilerParams(dimension_semantics=sem, vmem_limit_bytes=VMEM_LIMIT)


def _pick(dim, pref):
    if dim % LANES != 0 or dim <= pref:
        return dim
    best = LANES
    for t in range(LANES, pref + 1, LANES):
        if dim % t == 0:
            best = t
    return best


def _mm(name, a, b, mode, out_dtype=F32, add=None, tm=1024, tn=1024, tk=1024):
    if mode == "nn":
        (M, K), (K2, N) = a.shape, b.shape
    elif mode == "nt":
        (M, K), (N, K2) = a.shape, b.shape
    else:
        (K, M), (K2, N) = a.shape, b.shape
    assert K == K2, (name, a.shape, b.shape, mode)
    tm, tn, tk = _pick(M, tm), _pick(N, tn), _pick(K, tk)
    nk = K // tk
    if mode == "nn":
        a_spec = pl.BlockSpec((tm, tk), lambda i, j, k: (i, k))
        b_spec = pl.BlockSpec((tk, tn), lambda i, j, k: (k, j))
        dims = (((1,), (0,)), ((), ()))
    elif mode == "nt":
        a_spec = pl.BlockSpec((tm, tk), lambda i, j, k: (i, k))
        b_spec = pl.BlockSpec((tn, tk), lambda i, j, k: (j, k))
        dims = (((1,), (1,)), ((), ()))
    else:
        a_spec = pl.BlockSpec((tk, tm), lambda i, j, k: (k, i))
        b_spec = pl.BlockSpec((tk, tn), lambda i, j, k: (k, j))
        dims = (((0,), (0,)), ((), ()))
    o_spec = pl.BlockSpec((tm, tn), lambda i, j, k: (i, j))
    has_add = add is not None

    def body(*refs):
        if has_add:
            a_ref, b_ref, add_ref, o_ref, acc_ref = refs
        else:
            a_ref, b_ref, o_ref, acc_ref = refs
        k = pl.program_id(2)
        part = lax.dot_general(a_ref[...].astype(BF16), b_ref[...].astype(BF16), dims,
                               preferred_element_type=F32)

        @pl.when(k == 0)
        def _():
            acc_ref[...] = part

        @pl.when(k > 0)
        def _():
            acc_ref[...] += part

        @pl.when(k == nk - 1)
        def _():
            res = acc_ref[...]
            if has_add:
                res = res + add_ref[...].astype(F32)
            o_ref[...] = res.astype(o_ref.dtype)

    ins = [a, b] + ([add] if has_add else [])
    in_specs = [a_spec, b_spec] + ([o_spec] if has_add else [])
    return pl.pallas_call(
        body, name=name, grid=(M // tm, N // tn, nk),
        in_specs=in_specs, out_specs=o_spec,
        out_shape=jax.ShapeDtypeStruct((M, N), out_dtype),
        scratch_shapes=[pltpu.VMEM((tm, tn), F32)],
        compiler_params=_params(("parallel", "parallel", "arbitrary")),
    )(*ins)


def _rowwise(name, fn, T, tT, rows=(), prevs=(), nexts=(), consts=(), outs=()):
    n = T // tT
    per8 = tT // SUBLANES
    in_specs, ins = [], []
    for arr in rows:
        in_specs.append(pl.BlockSpec((tT, arr.shape[1]), lambda i: (i, 0)))
        ins.append(arr)
    for arr in prevs:
        in_specs.append(pl.BlockSpec((SUBLANES, arr.shape[1]), lambda i: (jnp.maximum(i * per8 - 1, 0), 0)))
        ins.append(arr)
    for arr in nexts:
        in_specs.append(pl.BlockSpec((SUBLANES, arr.shape[1]),
                                     lambda i: (jnp.minimum((i + 1) * per8, T // SUBLANES - 1), 0)))
        ins.append(arr)
    for arr in consts:
        in_specs.append(pl.BlockSpec(arr.shape, lambda i, nd=arr.ndim: (0,) * nd))
        ins.append(arr)
    out_specs, out_shapes = [], []
    for o in outs:
        if o[0] == "row":
            out_specs.append(pl.BlockSpec((tT, o[1]), lambda i: (i, 0)))
            out_shapes.append(jax.ShapeDtypeStruct((T, o[1]), o[2]))
        else:
            out_specs.append(pl.BlockSpec(o[1], lambda i: (0, 0)))
            out_shapes.append(jax.ShapeDtypeStruct(o[1], F32))
    nr, npv, nnx, nc = len(rows), len(prevs), len(nexts), len(consts)
    n_in = nr + npv + nnx + nc

    def body(*refs):
        i = pl.program_id(0)
        vals = [r[...] for r in refs[:n_in]]
        res = fn(i, n, vals[:nr], vals[nr:nr + npv], vals[nr + npv:nr + npv + nnx], vals[nr + npv + nnx:])
        for o, o_ref, val in zip(outs, refs[n_in:], res, strict=True):
            if o[0] == "row":
                o_ref[...] = val.astype(o_ref.dtype)
            else:
                @pl.when(i == 0)
                def _(o_ref=o_ref, val=val):
                    o_ref[...] = val.astype(F32)

                @pl.when(i > 0)
                def _(o_ref=o_ref, val=val):
                    o_ref[...] += val.astype(F32)

    res = pl.pallas_call(
        body, name=name, grid=(n,), in_specs=in_specs, out_specs=out_specs, out_shape=out_shapes,
        compiler_params=_params(("arbitrary",)),
    )(*ins)
    return list(res)


def _shift_down(x, prev8, i, s):
    rolled = pltpu.roll(x, s, 0)
    head = pltpu.roll(prev8, s, 0)
    head = jnp.where(i == 0, jnp.zeros_like(head), head)
    rid = lax.broadcasted_iota(jnp.int32, head.shape, 0)
    first = jnp.where(rid < s, head, rolled[:SUBLANES])
    if x.shape[0] == SUBLANES:
        return first
    return jnp.concatenate([first, rolled[SUBLANES:]], axis=0)


def _shift_up(x, next8, i, n, s):
    tT = x.shape[0]
    rolled = pltpu.roll(x, tT - s, 0)
    tail = pltpu.roll(next8, SUBLANES - s, 0)
    tail = jnp.where(i == n - 1, jnp.zeros_like(tail), tail)
    rid = lax.broadcasted_iota(jnp.int32, tail.shape, 0)
    last = jnp.where(rid >= SUBLANES - s, tail, rolled[tT - SUBLANES:])
    return jnp.concatenate([rolled[:tT - SUBLANES], last], axis=0)


def _colsum(x):
    return jnp.sum(x, axis=0, keepdims=True)


def _segsum(x, bd):
    return jnp.dot(x, bd, precision=lax.Precision.HIGH, preferred_element_type=F32)


def _block_diag_ones(width, seg):
    idx = np.arange(width) // seg
    return jnp.asarray((idx[:, None] == idx[None, :]).astype(np.float32))


def _sigmoid(z):
    return 1.0 / (1.0 + jnp.exp(-z))


def _softplus(z):
    return jnp.maximum(z, 0.0) + jnp.log(1.0 + jnp.exp(-jnp.abs(z)))


def _rms_fwd(x, g):
    r = lax.rsqrt(jnp.mean(x * x, axis=-1, keepdims=True) + NORM_EPS)
    return x * r * g


def _rms_bwd(x, g, dy):
    r = lax.rsqrt(jnp.mean(x * x, axis=-1, keepdims=True) + NORM_EPS)
    gdy = dy * g
    dx = r * (gdy - x * (r * r) * jnp.mean(x * gdy, axis=-1, keepdims=True))
    return dx, dy * x * r


GELU_C = math.sqrt(2.0 / math.pi)


def _gelu(x):
    return 0.5 * x * (1.0 + jnp.tanh(GELU_C * (x + 0.044715 * x * x * x)))


def _gelu_and_grad(x):
    th = jnp.tanh(GELU_C * (x + 0.044715 * x * x * x))
    half = 0.5 * (1.0 + th)
    return x * half, half + 0.5 * x * (1.0 - th * th) * GELU_C * (1.0 + 3.0 * 0.044715 * x * x)


RW_CHUNK = 64
NN = (((1,), (0,)), ((), ()))
NT = (((1,), (1,)), ((), ()))
TN = (((0,), (0,)), ((), ()))


def _hdot(a, b, dims):
    return lax.dot_general(a, b, dims, precision=lax.Precision.HIGH, preferred_element_type=F32)


def _chunk_masks():
    ti = lax.broadcasted_iota(jnp.int32, (RW_CHUNK, RW_CHUNK), 0)
    tj = lax.broadcasted_iota(jnp.int32, (RW_CHUNK, RW_CHUNK), 1)
    return tj <= ti, tj < ti, (ti == tj).astype(F32)


def _head(x, h):
    return x[:, h * RW_HEAD_DIM:(h + 1) * RW_HEAD_DIM]


def _heads(fn):
    return [fn(h) for h in range(RW_HEADS)]


def _chunk_rows(r, lw, k, a, b, incl_f):
    c = _hdot(incl_f, lw, NN)
    e_prev, e_neg, e_pos = jnp.exp(c - lw), jnp.exp(-c), jnp.exp(c)
    return dict(At=a * e_prev, Bt=b * e_neg, Kt=k * e_neg, Rt=r * e_pos, e_prev=e_prev, e_neg=e_neg, e_pos=e_pos)


def _chunk_coeffs(q, incl, strict):
    A1 = _heads(lambda h: jnp.where(strict, _hdot(_head(q["At"], h), _head(q["Bt"], h), NT), 0.0))
    A2 = _heads(lambda h: jnp.where(strict, _hdot(_head(q["At"], h), _head(q["Kt"], h), NT), 0.0))
    W1 = _heads(lambda h: jnp.where(incl, _hdot(_head(q["Rt"], h), _head(q["Bt"], h), NT), 0.0))
    W2 = _heads(lambda h: jnp.where(incl, _hdot(_head(q["Rt"], h), _head(q["Kt"], h), NT), 0.0))
    return A1, A2, W1, W2


def _rwkv_chunk_prep(r, lw, k, a, b, v):
    T = r.shape[0]
    nC = T // RW_CHUNK
    H, N = RW_HEADS, RW_HEAD_DIM

    def body(r_ref, lw_ref, k_ref, a_ref, b_ref, v_ref,
             at_ref, bt_ref, kt_ref, rt_ref, a2v_ref, w2v_ref, ti_ref, w1_ref, a2_ref, w2_ref, pl_ref):
        incl, strict, eye = _chunk_masks()
        q = _chunk_rows(r_ref[...], lw_ref[...], k_ref[...], a_ref[...], b_ref[...], incl.astype(F32))
        at_ref[...], bt_ref[...], kt_ref[...], rt_ref[...] = q["At"], q["Bt"], q["Kt"], q["Rt"]
        pl_ref[0] = jnp.broadcast_to(q["e_pos"][RW_CHUNK - 1:RW_CHUNK, :], (SUBLANES, RW_WIDTH))
        A1, A2, W1, W2 = _chunk_coeffs(q, incl, strict)
        V = v_ref[...]
        a2v_ref[...] = jnp.concatenate(_heads(lambda h: _hdot(A2[h], _head(V, h), NN)), axis=1)
        w2v_ref[...] = jnp.concatenate(_heads(lambda h: _hdot(W2[h], _head(V, h), NN)), axis=1)
        tinv, pw = [eye + m for m in A1], A1
        for _ in range(5):
            pw = [_hdot(m, m, NN) for m in pw]
            tinv = [t + _hdot(t, m, NN) for t, m in zip(tinv, pw, strict=True)]
        for h in range(H):
            ti_ref[0, h] = tinv[h]
            w1_ref[0, h] = W1[h]
            a2_ref[0, h] = A2[h]
            w2_ref[0, h] = W2[h]

    row_spec = pl.BlockSpec((RW_CHUNK, RW_WIDTH), lambda n: (n, 0))
    st_spec = pl.BlockSpec((1, H, N, N), lambda n: (n, 0, 0, 0))
    row_shape = jax.ShapeDtypeStruct((T, RW_WIDTH), F32)
    st_shape = jax.ShapeDtypeStruct((nC, H, N, N), F32)
    return pl.pallas_call(
        body, name="rwkv_chunk_prep", grid=(nC,),
        in_specs=[row_spec] * 6,
        out_specs=[row_spec] * 6 + [st_spec] * 4 + [pl.BlockSpec((1, SUBLANES, RW_WIDTH), lambda n: (n, 0, 0))],
        out_shape=[row_shape] * 6 + [st_shape] * 4 + [jax.ShapeDtypeStruct((nC, SUBLANES, RW_WIDTH), F32)],
        compiler_params=_params(("parallel",)),
    )(r, lw, k, a, b, v)


def _rwkv_chunk_fwd(v, at, bt, kt, rt, a2v, w2v, tinv, w1, plast):
    T = v.shape[0]
    nC = T // RW_CHUNK
    H, N = RW_HEADS, RW_HEAD_DIM

    def body(v_ref, at_ref, bt_ref, kt_ref, rt_ref, a2v_ref, w2v_ref, ti_ref, w1_ref, pl_ref,
             y_ref, sa_ref, s0_ref, S_ref):
        @pl.when(pl.program_id(0) == 0)
        def _():
            S_ref[...] = jnp.zeros_like(S_ref)

        V, At, Bt, Kt, Rt = v_ref[...], at_ref[...], bt_ref[...], kt_ref[...], rt_ref[...]
        A2V, W2V, p_last = a2v_ref[...], w2v_ref[...], pl_ref[0, 0:1, :]
        S0 = _heads(lambda h: S_ref[h])
        for h in range(H):
            s0_ref[0, h] = S0[h]
        Z = _heads(lambda h: _hdot(_head(At, h), S0[h], NT) + _head(A2V, h))
        Sa = _heads(lambda h: _hdot(ti_ref[0, h], Z[h], NN))
        X = _heads(lambda h: S0[h] + _hdot(Sa[h], _head(Bt, h), TN) + _hdot(_head(V, h), _head(Kt, h), TN))
        for h in range(H):
            S_ref[h] = X[h] * _head(p_last, h)
        Y = _heads(lambda h: _hdot(_head(Rt, h), S0[h], NT) + _hdot(w1_ref[0, h], Sa[h], NN) + _head(W2V, h))
        y_ref[...] = jnp.concatenate(Y, axis=1)
        sa_ref[...] = jnp.concatenate(Sa, axis=1)

    row_spec = pl.BlockSpec((RW_CHUNK, RW_WIDTH), lambda n: (n, 0))
    st_spec = pl.BlockSpec((1, H, N, N), lambda n: (n, 0, 0, 0))
    row_shape = jax.ShapeDtypeStruct((T, RW_WIDTH), F32)
    return pl.pallas_call(
        body, name="rwkv_chunk_fwd", grid=(nC,),
        in_specs=[row_spec] * 7 + [st_spec, st_spec, pl.BlockSpec((1, SUBLANES, RW_WIDTH), lambda n: (n, 0, 0))],
        out_specs=[row_spec, row_spec, st_spec],
        out_shape=[row_shape, row_shape, jax.ShapeDtypeStruct((nC, H, N, N), F32)],
        scratch_shapes=[pltpu.VMEM((H, N, N), F32)],
        compiler_params=_params(("arbitrary",)),
    )(v, at, bt, kt, rt, a2v, w2v, tinv, w1, plast)


def _rwkv_chunk_bwd(r, lw, k, a, b, v, dy, s0, tinv, w1, a2, w2, sa):
    T = r.shape[0]
    nC = T // RW_CHUNK
    H, N = RW_HEADS, RW_HEAD_DIM

    def body(r_ref, lw_ref, k_ref, a_ref, b_ref, v_ref, dy_ref, s0_ref, ti_ref, w1_ref, a2_ref, w2_ref, sa_ref,
             dr_ref, dlw_ref, dk_ref, da_ref, db_ref, dv_ref, dS_ref):
        @pl.when(pl.program_id(0) == 0)
        def _():
            dS_ref[...] = jnp.zeros_like(dS_ref)

        incl, strict, _ = _chunk_masks()
        incl_f = incl.astype(F32)
        q = _chunk_rows(r_ref[...], lw_ref[...], k_ref[...], a_ref[...], b_ref[...], incl_f)
        At, Bt, Kt, Rt = q["At"], q["Bt"], q["Kt"], q["Rt"]
        A2, W1, W2 = (_heads(lambda h, ref=ref: ref[0, h]) for ref in (a2_ref, w1_ref, w2_ref))
        V, dY, Sa = v_ref[...], dy_ref[...], sa_ref[...]
        hd = _head
        p_last = q["e_pos"][RW_CHUNK - 1:RW_CHUNK, :]
        S0 = _heads(lambda h: s0_ref[0, h])
        G = _heads(lambda h: dS_ref[h] * hd(p_last, h))
        X = _heads(lambda h: S0[h] + _hdot(hd(Sa, h), hd(Bt, h), TN) + _hdot(hd(V, h), hd(Kt, h), TN))
        dc_last = jnp.concatenate(_heads(lambda h: jnp.sum(G[h] * X[h], axis=0, keepdims=True)), axis=1)
        dSa = _heads(lambda h: _hdot(hd(Bt, h), G[h], NT) + _hdot(W1[h], hd(dY, h), TN))
        dZ = _heads(lambda h: _hdot(ti_ref[0, h], dSa[h], TN))
        for h in range(H):
            dS_ref[h] = G[h] + _hdot(dZ[h], hd(At, h), TN) + _hdot(hd(dY, h), hd(Rt, h), TN)
        dA1 = _heads(lambda h: jnp.where(strict, _hdot(dZ[h], hd(Sa, h), NT), 0.0))
        dA2 = _heads(lambda h: jnp.where(strict, _hdot(dZ[h], hd(V, h), NT), 0.0))
        dW1 = _heads(lambda h: jnp.where(incl, _hdot(hd(dY, h), hd(Sa, h), NT), 0.0))
        dW2 = _heads(lambda h: jnp.where(incl, _hdot(hd(dY, h), hd(V, h), NT), 0.0))
        cat = lambda fn: jnp.concatenate(_heads(fn), axis=1)
        dV = cat(lambda h: _hdot(A2[h], dZ[h], TN) + _hdot(W2[h], hd(dY, h), TN) + _hdot(hd(Kt, h), G[h], NT))
        dAt = cat(lambda h: _hdot(dA1[h], hd(Bt, h), NN) + _hdot(dA2[h], hd(Kt, h), NN) + _hdot(dZ[h], S0[h], NN))
        dBt = cat(lambda h: _hdot(dA1[h], hd(At, h), TN) + _hdot(dW1[h], hd(Rt, h), TN) + _hdot(hd(Sa, h), G[h], NN))
        dKt = cat(lambda h: _hdot(dA2[h], hd(At, h), TN) + _hdot(dW2[h], hd(Rt, h), TN) + _hdot(hd(V, h), G[h], NN))
        dRt = cat(lambda h: _hdot(hd(dY, h), S0[h], NN) + _hdot(dW1[h], hd(Bt, h), NN) + _hdot(dW2[h], hd(Kt, h), NN))
        last_row = lax.broadcasted_iota(jnp.int32, (RW_CHUNK, RW_WIDTH), 0) == RW_CHUNK - 1
        dc_prev = dAt * At
        dc = dc_prev + dRt * Rt - dBt * Bt - dKt * Kt + jnp.where(last_row, dc_last, 0.0)
        dr_ref[...] = dRt * q["e_pos"]
        dlw_ref[...] = _hdot(incl_f, dc, TN) - dc_prev
        dk_ref[...] = dKt * q["e_neg"]
        da_ref[...] = dAt * q["e_prev"]
        db_ref[...] = dBt * q["e_neg"]
        dv_ref[...] = dV

    rev = lambda n: nC - 1 - n
    row_spec = pl.BlockSpec((RW_CHUNK, RW_WIDTH), lambda n: (rev(n), 0))
    st_spec = pl.BlockSpec((1, H, N, N), lambda n: (rev(n), 0, 0, 0))
    row_shape = jax.ShapeDtypeStruct((T, RW_WIDTH), F32)
    return pl.pallas_call(
        body, name="rwkv_chunk_bwd", grid=(nC,),
        in_specs=[row_spec] * 7 + [st_spec] * 5 + [row_spec], out_specs=[row_spec] * 6,
        out_shape=[row_shape] * 6, scratch_shapes=[pltpu.VMEM((H, N, N), F32)],
        compiler_params=_params(("arbitrary",)),
    )(r, lw, k, a, b, v, dy, s0, tinv, w1, a2, w2, sa)


def _alibi_slope(head):
    return float(np.float32(2.0 ** (-8.0 * (head + 1) / ATT_HEADS)))


ATT_SPAN = ATT_BLOCK * max(ATT_GROUP_DILATION)
ATT_PAIR_WIDTH = 2 * ATT_HEAD_DIM
ATT_SIDE_BY_SIDE = 8


def _pair_slope(g, hp, j):
    return jnp.where(hp == 0, _alibi_slope(4 * g + j), _alibi_slope(4 * g + 2 + j))


def _att_rows(mi, r, d):
    start = mi * ATT_BLOCK * d + r
    return pl.ds(start, ATT_BLOCK) if d == 1 else pl.ds(start, ATT_BLOCK, stride=d)


def _att_masks():
    qi = lax.broadcasted_iota(jnp.int32, (ATT_BLOCK, ATT_BLOCK), 0)
    kj = lax.broadcasted_iota(jnp.int32, (ATT_BLOCK, ATT_BLOCK), 1)
    return qi, kj


NEG = -1e30


def _att_logits(q, k, slope_d, steps, valid):
    s = lax.dot_general(q.astype(BF16), k.astype(BF16), (((1,), (1,)), ((), ())),
                        preferred_element_type=F32) * (ATT_HEAD_DIM ** -0.5)
    return jnp.where(valid, s - slope_d * steps.astype(F32), NEG)


def _att_fwd(p_att, g):
    T = p_att.shape[0]
    d = ATT_GROUP_DILATION[g]
    W = ATT_PAIR_WIDTH
    nb = T // ATT_SPAN
    mb = ATT_SPAN // (ATT_BLOCK * d)

    def body(q_ref, kc_ref, kp_ref, vc_ref, vp_ref, o_ref, l_ref):
        hp, n = pl.program_id(0), pl.program_id(1)
        qi, kj = _att_masks()
        slopes = [_pair_slope(g, hp, j) * d for j in range(2)]
        blocks = [(r, mi) for r in range(d) for mi in range(mb)]
        for at in range(0, len(blocks), ATT_SIDE_BY_SIDE):
            tasks = []
            for r, mi in blocks[at:at + ATT_SIDE_BY_SIDE]:
                rows = _att_rows(mi, r, d)
                if mi > 0:
                    prev = _att_rows(mi - 1, r, d)
                    kp, vp, has_prev = kc_ref[prev, :], vc_ref[prev, :], True
                else:
                    prev = _att_rows(mb - 1, r, d)
                    kp, vp, has_prev = kp_ref[prev, :], vp_ref[prev, :], n > 0
                q, kc, vc = q_ref[rows, :], kc_ref[rows, :], vc_ref[rows, :]
                for j in range(2):
                    sl = slice(j * ATT_HEAD_DIM, (j + 1) * ATT_HEAD_DIM)
                    tasks.append((q[:, sl], kc[:, sl], kp[:, sl], vc[:, sl], vp[:, sl], has_prev, slopes[j]))
            lc = [_att_logits(t[0], t[1], t[6], qi - kj, kj <= qi) for t in tasks]
            lp = [_att_logits(t[0], t[2], t[6], qi - kj + ATT_BLOCK, (kj >= qi) & t[5]) for t in tasks]
            mx = [jnp.maximum(jnp.max(a, axis=1, keepdims=True), jnp.max(b, axis=1, keepdims=True))
                  for a, b in zip(lc, lp, strict=True)]
            ec = [jnp.exp(a - m) for a, m in zip(lc, mx, strict=True)]
            ep = [jnp.exp(b - m) for b, m in zip(lp, mx, strict=True)]
            den = [jnp.sum(a, axis=1, keepdims=True) + jnp.sum(b, axis=1, keepdims=True)
                   for a, b in zip(ec, ep, strict=True)]
            inv = [1.0 / s for s in den]
            outs = [jnp.dot((a * i).astype(BF16), t[3].astype(BF16), preferred_element_type=F32)
                    + jnp.dot((b * i).astype(BF16), t[4].astype(BF16), preferred_element_type=F32)
                    for a, b, i, t in zip(ec, ep, inv, tasks, strict=True)]
            lses = [jnp.broadcast_to(m + jnp.log(s), (ATT_BLOCK, ATT_HEAD_DIM)) for m, s in zip(mx, den, strict=True)]
            for i, (r, mi) in enumerate(blocks[at:at + ATT_SIDE_BY_SIDE]):
                rows = _att_rows(mi, r, d)
                o_ref[rows, :] = jnp.concatenate(outs[2 * i:2 * i + 2], axis=1)
                l_ref[rows, :] = jnp.concatenate(lses[2 * i:2 * i + 2], axis=1)

    def spec(col0, prev):
        if prev:
            return pl.BlockSpec((ATT_SPAN, W), lambda hp, n: (jnp.maximum(n - 1, 0), col0 + 2 * g + hp))
        return pl.BlockSpec((ATT_SPAN, W), lambda hp, n: (n, col0 + 2 * g + hp))

    o_spec = pl.BlockSpec((ATT_SPAN, W), lambda hp, n: (n, hp))
    o, l = pl.pallas_call(
        body, name=f"att_fwd_g{g}", grid=(2, nb),
        in_specs=[spec(0, False), spec(6, False), spec(6, True), spec(12, False), spec(12, True)],
        out_specs=[o_spec, o_spec],
        out_shape=[jax.ShapeDtypeStruct((T, ATT_GROUP_WIDTH), F32)] * 2,
        compiler_params=_params(("parallel", "arbitrary")),
    )(p_att, p_att, p_att, p_att, p_att)
    return o, l


def _att_bwd(p_att, o, l, do, dl, g):
    T = p_att.shape[0]
    d = ATT_GROUP_DILATION[g]
    W = ATT_PAIR_WIDTH
    nb = T // ATT_SPAN
    mb = ATT_SPAN // (ATT_BLOCK * d)
    scale = ATT_HEAD_DIM ** -0.5

    def body(q_ref, k_ref, v_ref, o_ref, l_ref, do_ref, dl_ref,
             qn_ref, on_ref, ln_ref, don_ref, dln_ref, dq_ref, dk_ref, dv_ref, carry_ref):
        hp, n = pl.program_id(0), pl.program_id(1)
        qi, kj = _att_masks()

        @pl.when(n == 0)
        def _():
            carry_ref[...] = jnp.zeros_like(carry_ref)

        slopes = [_pair_slope(g, hp, j) * d for j in range(2)]
        blocks = [(r, mi) for r in range(d) for mi in range(mb)]
        side_by_side = ATT_SIDE_BY_SIDE // 2
        carry = None
        for at in range(0, len(blocks), side_by_side):
            tasks = []
            for r, mi in blocks[at:at + side_by_side]:
                rows = _att_rows(mi, r, d)
                if mi < mb - 1:
                    nrows = _att_rows(mi + 1, r, d)
                    nxt = (q_ref[nrows, :], o_ref[nrows, :], l_ref[nrows, :], do_ref[nrows, :], dl_ref[nrows, :])
                    has_next = True
                else:
                    nrows = _att_rows(0, r, d)
                    nxt = (qn_ref[nrows, :], on_ref[nrows, :], ln_ref[nrows, :], don_ref[nrows, :],
                           dln_ref[nrows, :])
                    has_next = n < nb - 1
                cur = (q_ref[rows, :], o_ref[rows, :], l_ref[rows, :], do_ref[rows, :], dl_ref[rows, :])
                k_all, v_all = k_ref[rows, :], v_ref[rows, :]
                for j in range(2):
                    sl = slice(j * ATT_HEAD_DIM, (j + 1) * ATT_HEAD_DIM)
                    for blk, steps, valid in ((cur, qi - kj, kj <= qi),
                                              (nxt, qi - kj + ATT_BLOCK, (kj >= qi) & has_next)):
                        q, o_, lse, do_, dlse = (z[:, sl] for z in blk)
                        tasks.append(dict(q=q, o=o_, lse=lse[:, :1], do=do_, dlse=dlse[:, :1], steps=steps,
                                          valid=valid, k=k_all[:, sl], vb=v_all[:, sl].astype(BF16),
                                          slope=slopes[j]))
            p = [jnp.exp(_att_logits(t["q"], t["k"], t["slope"], t["steps"], t["valid"]) - t["lse"]) for t in tasks]
            dp = [lax.dot_general(t["do"].astype(BF16), t["vb"], (((1,), (1,)), ((), ())),
                                  preferred_element_type=F32) for t in tasks]
            dsum = [jnp.sum(t["do"] * t["o"], axis=1, keepdims=True) for t in tasks]
            ds = [a * (b - s + t["dlse"]) for a, b, s, t in zip(p, dp, dsum, tasks, strict=True)]
            dv_ = [jnp.dot(a.T.astype(BF16), t["do"].astype(BF16), preferred_element_type=F32)
                   for a, t in zip(p, tasks, strict=True)]
            dk_ = [jnp.dot(a.T.astype(BF16), t["q"].astype(BF16), preferred_element_type=F32) * scale
                   for a, t in zip(ds, tasks, strict=True)]
            dq_ = [jnp.dot(a.astype(BF16), t["k"].astype(BF16), preferred_element_type=F32) * scale
                   for a, t in zip(ds, tasks, strict=True)]
            for i, (r, mi) in enumerate(blocks[at:at + side_by_side]):
                rows = _att_rows(mi, r, d)
                b = 4 * i
                if mi == 0:
                    carry = carry_ref[r]
                dq_ref[rows, :] = jnp.concatenate([dq_[b], dq_[b + 2]], axis=1) + carry
                carry = jnp.concatenate([dq_[b + 1], dq_[b + 3]], axis=1)
                if mi == mb - 1:
                    carry_ref[r] = carry
                dk_ref[rows, :] = jnp.concatenate([dk_[b] + dk_[b + 1], dk_[b + 2] + dk_[b + 3]], axis=1)
                dv_ref[rows, :] = jnp.concatenate([dv_[b] + dv_[b + 1], dv_[b + 2] + dv_[b + 3]], axis=1)

    head_rows = ATT_BLOCK * d
    nxt_n = lambda n: jnp.minimum((n + 1) * mb, T // head_rows - 1)
    cur_p = lambda col0: pl.BlockSpec((ATT_SPAN, W), lambda hp, n: (n, col0 + 2 * g + hp))
    cur_o = pl.BlockSpec((ATT_SPAN, W), lambda hp, n: (n, hp))
    nxt_o = pl.BlockSpec((head_rows, W), lambda hp, n: (nxt_n(n), hp))
    dq, dk, dv = pl.pallas_call(
        body, name=f"att_bwd_g{g}", grid=(2, nb),
        in_specs=[cur_p(0), cur_p(6), cur_p(12), cur_o, cur_o, cur_o, cur_o,
                  pl.BlockSpec((head_rows, W), lambda hp, n: (nxt_n(n), 2 * g + hp)), nxt_o, nxt_o, nxt_o, nxt_o],
        out_specs=[cur_o, cur_o, cur_o],
        out_shape=[jax.ShapeDtypeStruct((T, ATT_GROUP_WIDTH), F32)] * 3,
        scratch_shapes=[pltpu.VMEM((d, ATT_BLOCK, W), F32)],
        compiler_params=_params(("parallel", "arbitrary")),
    )(p_att, p_att, p_att, o, l, do, dl, p_att, o, l, do, dl)
    return dq, dk, dv


RKV = 3 * RW_WIDTH
WA = 128
XG = 160
RW_COLS = RKV + WA + XG


def _local_step(x, p, W, target):
    T = x.shape[0]
    tT = 256
    bd512 = _block_diag_ones(RW_WIDTH, RW_HEAD_DIM)
    bd256 = _block_diag_ones(ATT_GROUP_WIDTH, ATT_HEAD_DIM)
    G = {}

    w_in = W["w_in"]
    w_rkv, w_wa, w_xg, w_att = (w_in[:, :RKV], w_in[:, RKV:RKV + WA], w_in[:, RKV + WA:RW_COLS],
                                w_in[:, RW_COLS:])
    mu = W["rw_mu"]
    mu_rkv, mu_wa, mu_xg = mu[:, :RKV], mu[:, RKV:RKV + WA], mu[:, RKV + WA:]
    zpad = jnp.zeros((64, RW_WIDTH), W["rw_w_up"].dtype)
    w_up_pad = jnp.concatenate([W["rw_w_up"], zpad], axis=0)
    a_up_pad = jnp.concatenate([zpad, W["rw_a_up"]], axis=0)
    r_k = W["rw_r_k"].reshape(1, RW_WIDTH)

    (h,) = _rowwise("norm_mix", lambda i, n, r, pv, nx, c: [_rms_fwd(r[0], c[0])], T, tT,
                    rows=[x], consts=[W["g_mix"]], outs=[("row", D_MODEL, BF16)])
    p_rkv = _mm("proj_rkv", h, w_rkv, "nn")
    p_wa = _mm("proj_wa", h, w_wa, "nn")
    p_xg = _mm("proj_xg", h, w_xg, "nn")
    p_att = _mm("proj_att", h, w_att, "nn", tn=768)
    z_gate = _mm("proj_gate", h, W["w_gate"], "nn")

    def rw_pre_core(i, rows, prevs, consts):
        prkv, pwa, pxg = rows[:3]
        (mrkv, mwa, mxg, w0, a0, k_k, k_a, wup, aup, gup, bd) = consts[:11]
        m_rkv = prkv + (_shift_down(prkv, prevs[0], i, 1) - prkv) * mrkv
        m_wa = pwa + (_shift_down(pwa, prevs[1], i, 1) - pwa) * mwa
        m_xg = pxg + (_shift_down(pxg, prevs[2], i, 1) - pxg) * mxg
        r, k, v = m_rkv[:, :RW_WIDTH], m_rkv[:, RW_WIDTH:2 * RW_WIDTH], m_rkv[:, 2 * RW_WIDTH:]
        tw = jnp.tanh(m_wa)
        lw = w0 + jnp.dot(tw.astype(BF16), wup.astype(BF16), preferred_element_type=F32)
        wlog = -_softplus(-lw) - 0.5
        log_decay = -jnp.exp(wlog)
        a = _sigmoid(a0 + jnp.dot(m_wa.astype(BF16), aup.astype(BF16), preferred_element_type=F32))
        sg = _sigmoid(m_xg)
        gate = jnp.dot(sg.astype(BF16), gup.astype(BF16), preferred_element_type=F32)
        kkp = k * k_k
        nrm = jnp.sqrt(_segsum(kkp * kkp, bd))
        nrm_c = jnp.maximum(nrm, 1e-12)
        kk = kkp / nrm_c
        k2 = k * (1.0 + (a - 1.0) * k_a)
        return dict(r=r, k=k, v=v, tw=tw, lw=lw, wlog=wlog, log_decay=log_decay, a=a, sg=sg, gate=gate, kkp=kkp,
                    nrm=nrm, nrm_c=nrm_c, kk=kk, k2=k2, m_rkv=m_rkv, m_wa=m_wa, m_xg=m_xg)

    pre_consts = [mu_rkv, mu_wa, mu_xg, W["rw_w0"], W["rw_a0"], W["rw_k_k"], W["rw_k_a"],
                  w_up_pad, a_up_pad, W["rw_g_up"], bd512]

    def rw_pre(i, n, rows, prevs, nexts, consts):
        q = rw_pre_core(i, rows, prevs, consts)
        return [q["r"], q["log_decay"], q["k2"], q["v"], -q["kk"], q["kk"] * q["a"], q["gate"]]

    r_s, w_s, k_s, v_s, a_s, b_s, gate_s = _rowwise(
        "rwkv_pre", rw_pre, T, tT, rows=[p_rkv, p_wa, p_xg], prevs=[p_rkv, p_wa, p_xg], consts=pre_consts,
        outs=[("row", RW_WIDTH, F32)] * 7)
    (at_s, bt_s, kt_s, rt_s, a2v_s, w2v_s, tinv_s, w1_s, a2_s, w2_s,
     plast_s) = _rwkv_chunk_prep(r_s, w_s, k_s, a_s, b_s, v_s)
    y_scan, sa_s, s0_s = _rwkv_chunk_fwd(v_s, at_s, bt_s, kt_s, rt_s, a2v_s, w2v_s, tinv_s, w1_s, plast_s)

    def rw_post_core(rows, consts):
        y, r, k2, v, gate = rows[:5]
        ln_g, ln_b, rk, bd = consts[:4]
        mean = _segsum(y, bd) * (1.0 / RW_HEAD_DIM)
        yc = y - mean
        var = _segsum(yc * yc, bd) * (1.0 / RW_HEAD_DIM)
        rstd = lax.rsqrt(var + RW_LN_EPS)
        yn = yc * rstd
        s = _segsum(r * k2 * rk, bd)
        return dict(yn=yn, rstd=rstd, s=s, pre=yn * ln_g + ln_b + s * v)

    post_consts = [W["rw_ln_g"], W["rw_ln_b"], r_k, bd512]
    (y_a,) = _rowwise("rwkv_post", lambda i, n, r, pv, nx, c: [rw_post_core(r, c)["pre"] * r[4]], T, tT,
                      rows=[y_scan, r_s, k_s, v_s, gate_s], consts=post_consts, outs=[("row", RW_WIDTH, BF16)])

    att = [_att_fwd(p_att, g) for g in range(3)]

    def comb_weights(ls):
        mx = jnp.maximum(jnp.maximum(ls[0], ls[1]), ls[2])
        es = [jnp.exp(l - mx) for l in ls]
        den = es[0] + es[1] + es[2]
        return [e / den for e in es]

    def att_comb(i, n, rows, pv, nx, c):
        wts = comb_weights(rows[3:6])
        return [wts[0] * rows[0] + wts[1] * rows[1] + wts[2] * rows[2]]

    (y_b,) = _rowwise("att_combine", att_comb, T, tT, rows=[att[0][0], att[1][0], att[2][0], att[0][1], att[1][1],
                                                            att[2][1]], outs=[("row", ATT_GROUP_WIDTH, BF16)])

    br_a = _mm("branch_a", y_a, W["w_branch_a"], "nn")
    br_b = _mm("branch_b", y_b, W["w_branch_b"], "nn")

    def merge(i, n, rows, pv, nx, c):
        gates = _sigmoid(rows[0] + c[0])
        return [gates[:, :D_MODEL] * rows[1] + gates[:, D_MODEL:] * rows[2]]

    (merged,) = _rowwise("merge", merge, T, tT, rows=[z_gate, br_a, br_b], consts=[W["b_gate"]],
                         outs=[("row", D_MODEL, BF16)])
    x1 = _mm("mix_out", merged, W["w_out"], "nn", add=x)

    (h2,) = _rowwise("norm_ffn", lambda i, n, r, pv, nx, c: [_rms_fwd(r[0], c[0])], T, tT,
                     rows=[x1], consts=[W["g_ffn"]], outs=[("row", D_MODEL, BF16)])
    u = _mm("ffn_up", h2, W["w_up"], "nn")

    def conv_core(i, rows, prevs, consts):
        uu, cw, cb = rows[0], consts[0], consts[1]
        u1 = _shift_down(uu, prevs[0], i, 1)
        u2 = _shift_down(uu, prevs[0], i, 2)
        uc = cb + cw[0:1] * uu + cw[1:2] * u1 + cw[2:3] * u2
        return uc[:, :D_FF], uc[:, D_FF:], u1, u2

    def glu(i, n, rows, prevs, nx, consts):
        gate, val, _, _ = conv_core(i, rows, prevs, consts)
        return [_gelu(gate) * val]

    tF = 128
    (act,) = _rowwise("conv_glu", glu, T, tF, rows=[u], prevs=[u], consts=[W["conv_w"], W["conv_b"]],
                      outs=[("row", D_FF, BF16)])
    x2 = _mm("ffn_down", act, W["w_down"], "nn", add=x1)

    (h3,) = _rowwise("norm_ple", lambda i, n, r, pv, nx, c: [_rms_fwd(r[0], c[0])], T, tT,
                     rows=[x2], consts=[W["g_ple"]], outs=[("row", D_MODEL, BF16)])
    z_ple = _mm("ple_gate", h3, W["w_ple_gate"], "nn")
    e_ple = _mm("ple_emb", p, W["w_ple"], "nn")

    def head(i, n, rows, pv, nx, consts):
        x2_, z, e, tgt = rows
        pg = _sigmoid(z)
        x3 = x2_ + pg * e
        y = _rms_fwd(x3, consts[0])
        err = y - tgt
        loss = 0.5 * jnp.sum(jnp.sum(err * err, axis=1, keepdims=True) * (1.0 / D_MODEL), axis=0, keepdims=True)
        dy = err * (1.0 / D_MODEL)
        dx3, dgf = _rms_bwd(x3, consts[0], dy)
        return [dx3, dx3 * pg, dx3 * e * pg * (1.0 - pg), jnp.broadcast_to(loss, (1, LANES)), _colsum(dgf)]

    dx3, de, dz, loss_acc, G["g_final"] = _rowwise(
        "loss_head", head, T, tT, rows=[x2, z_ple, e_ple, target], consts=[W["g_final"].reshape(1, D_MODEL)],
        outs=[("row", D_MODEL, F32), ("row", D_MODEL, BF16), ("row", D_MODEL, BF16), ("acc", (1, LANES)),
              ("acc", (1, D_MODEL))])
    G["w_ple"] = _mm("d_w_ple", p, de, "tn")
    G["w_ple_gate"] = _mm("d_w_ple_gate", h3, dz, "tn")
    dh3 = _mm("d_h3", dz, W["w_ple_gate"], "nt")

    def norm_bwd(i, n, rows, pv, nx, consts):
        dx, dg = _rms_bwd(rows[0], consts[0], rows[1])
        return [rows[2] + dx, _colsum(dg)]

    dx2, G["g_ple"] = _rowwise("d_norm_ple", norm_bwd, T, tT, rows=[x2, dh3, dx3], consts=[W["g_ple"]],
                               outs=[("row", D_MODEL, F32), ("acc", (1, D_MODEL))])

    dact = _mm("d_act", dx2, W["w_down"], "nt")
    G["w_down"] = _mm("d_w_down", act, dx2, "tn")

    def glu_grad(gate, val, da):
        act_, slope = _gelu_and_grad(gate)
        return jnp.concatenate([da * val * slope, da * act_], axis=1)

    def glu_bwd(i, n, rows, prevs, nexts, consts):
        uu, da = rows
        cw = consts[0]
        gate, val, u1, u2 = conv_core(i, rows, prevs, consts)
        duc = glu_grad(gate, val, da)
        dcw = jnp.concatenate([_colsum(duc * uu), _colsum(duc * u1), _colsum(duc * u2)], axis=0)
        gate_n, val_n, _, _ = conv_core(1, [nexts[0]], [uu[tF - SUBLANES:]], consts)
        duc_n = glu_grad(gate_n, val_n, nexts[1])
        du = (cw[0:1] * duc + cw[1:2] * _shift_up(duc, duc_n, i, n, 1) + cw[2:3] * _shift_up(duc, duc_n, i, n, 2))
        return [du, _colsum(duc), dcw]

    du, G["conv_b"], G["conv_w"] = _rowwise(
        "d_conv_glu", glu_bwd, T, tF, rows=[u, dact], prevs=[u], nexts=[u, dact],
        consts=[W["conv_w"], W["conv_b"]],
        outs=[("row", 2 * D_FF, BF16), ("acc", (1, 2 * D_FF)), ("acc", (3, 2 * D_FF))])
    G["w_up"] = _mm("d_w_up", h2, du, "tn")
    dh2 = _mm("d_h2", du, W["w_up"], "nt")
    dx1, G["g_ffn"] = _rowwise("d_norm_ffn", norm_bwd, T, tT, rows=[x1, dh2, dx2], consts=[W["g_ffn"]],
                               outs=[("row", D_MODEL, F32), ("acc", (1, D_MODEL))])

    dmerged = _mm("d_merged", dx1, W["w_out"], "nt")
    G["w_out"] = _mm("d_w_out", merged, dx1, "tn")

    def merge_bwd(i, n, rows, pv, nx, consts):
        z, a_, b_, dm = rows
        gates = _sigmoid(z + consts[0])
        ga, gb = gates[:, :D_MODEL], gates[:, D_MODEL:]
        dz_ = jnp.concatenate([dm * a_ * ga * (1.0 - ga), dm * b_ * gb * (1.0 - gb)], axis=1)
        return [dm * ga, dm * gb, dz_, _colsum(dz_)]

    d_br_a, d_br_b, dz_gate, G["b_gate"] = _rowwise(
        "d_merge", merge_bwd, T, tT, rows=[z_gate, br_a, br_b, dmerged], consts=[W["b_gate"]],
        outs=[("row", D_MODEL, BF16), ("row", D_MODEL, BF16), ("row", 2 * D_MODEL, BF16), ("acc", (1, 2 * D_MODEL))])
    G["w_branch_a"] = _mm("d_w_branch_a", y_a, d_br_a, "tn")
    G["w_branch_b"] = _mm("d_w_branch_b", y_b, d_br_b, "tn")
    G["w_gate"] = _mm("d_w_gate", h, dz_gate, "tn")
    dy_a = _mm("d_y_a", d_br_a, W["w_branch_a"], "nt")
    dy_b = _mm("d_y_b", d_br_b, W["w_branch_b"], "nt")

    def att_comb_bwd(i, n, rows, pv, nx, consts):
        os_, ls, dy = rows[0:3], rows[3:6], rows[6]
        wts = comb_weights(ls)
        dws = [_segsum(dy * o_, consts[0]) for o_ in os_]
        mix = wts[0] * dws[0] + wts[1] * dws[1] + wts[2] * dws[2]
        return [wts[g_] * dy for g_ in range(3)] + [wts[g_] * (dws[g_] - mix) for g_ in range(3)]

    comb = _rowwise("d_att_combine", att_comb_bwd, T, tT,
                    rows=[att[0][0], att[1][0], att[2][0], att[0][1], att[1][1], att[2][1], dy_b], consts=[bd256],
                    outs=[("row", ATT_GROUP_WIDTH, F32)] * 6)
    dqkv = [_att_bwd(p_att, att[g][0], att[g][1], comb[g], comb[3 + g], g) for g in range(3)]
    dp_att = jnp.concatenate([dqkv[g][part] for part in range(3) for g in range(3)], axis=1).astype(BF16)

    def rw_post_bwd(i, n, rows, pv, nx, consts):
        y, r, k2, v, gate, dya = rows
        ln_g, ln_b, rk, bd = consts
        q = rw_post_core(rows, consts)
        dpre = dya * gate
        dgate = dya * q["pre"]
        dyn = dpre * ln_g
        inv = 1.0 / RW_HEAD_DIM
        dy_scan = q["rstd"] * (dyn - _segsum(dyn, bd) * inv - q["yn"] * (_segsum(dyn * q["yn"], bd) * inv))
        ds = _segsum(dpre * v, bd)
        return [dy_scan, dgate, ds * k2 * rk, ds * r * rk, dpre * q["s"],
                _colsum(dpre * q["yn"]), _colsum(dpre), _colsum(ds * r * k2)]

    dy_scan, dgate, dr_b, dk2_b, dv_b, G["rw_ln_g"], G["rw_ln_b"], d_rk = _rowwise(
        "d_rwkv_post", rw_post_bwd, T, tT, rows=[y_scan, r_s, k_s, v_s, gate_s, dy_a], consts=post_consts,
        outs=[("row", RW_WIDTH, F32)] * 5 + [("acc", (1, RW_WIDTH))] * 3)
    G["rw_r_k"] = d_rk.reshape(RW_HEADS, RW_HEAD_DIM)

    dr_s, dw_s, dk_s, da_s, db_s, dv_s = _rwkv_chunk_bwd(r_s, w_s, k_s, a_s, b_s, v_s, dy_scan, s0_s, tinv_s, w1_s,
                                                         a2_s, w2_s, sa_s)

    def rw_pre_bwd(i, n, rows, prevs, nx, consts):
        q = rw_pre_core(i, rows, prevs, consts)
        (mrkv, mwa, mxg, w0, a0, k_k, k_a, wup, aup, gup, bd) = consts
        dr, dlogdecay, dk2, dv, dav, dbv, dgate_ = rows[3:10]
        dr = dr + rows[10]
        dk2 = dk2 + rows[11]
        dv = dv + rows[12]
        a, k, kk = q["a"], q["k"], q["kk"]
        dk = dk2 * (1.0 + (a - 1.0) * k_a)
        da = dk2 * k * k_a + dbv * kk
        dkk = dbv * a - dav
        live = q["nrm"] > 1e-12
        dkkp = jnp.where(live, dkk - kk * _segsum(dkk * kk, bd), dkk) / q["nrm_c"]
        dk = dk + dkkp * k_k
        dlw = dlogdecay * q["log_decay"] * _sigmoid(-q["lw"])
        dla = da * a * (1.0 - a)
        nt = (((1,), (1,)), ((), ()))
        dtw = lax.dot_general(dlw.astype(BF16), wup.astype(BF16), nt, preferred_element_type=F32)
        dxa = lax.dot_general(dla.astype(BF16), aup.astype(BF16), nt, preferred_element_type=F32)
        dm_wa = dtw * (1.0 - q["tw"] * q["tw"]) + dxa
        dsg = lax.dot_general(dgate_.astype(BF16), gup.astype(BF16), nt, preferred_element_type=F32)
        dm_xg = dsg * q["sg"] * (1.0 - q["sg"])
        dm_rkv = jnp.concatenate([dr, dk, dv], axis=1)
        prkv, pwa, pxg = rows[:3]
        dmu = jnp.concatenate([_colsum(dm_rkv * (_shift_down(prkv, prevs[0], i, 1) - prkv)),
                               _colsum(dm_wa * (_shift_down(pwa, prevs[1], i, 1) - pwa)),
                               _colsum(dm_xg * (_shift_down(pxg, prevs[2], i, 1) - pxg))], axis=1)
        return [dm_rkv, dm_wa, dm_xg, dlw, dla, q["tw"], q["m_wa"], q["sg"], dmu,
                _colsum(dlw), _colsum(dla), _colsum(dkkp * k), _colsum(dk2 * k * (a - 1.0))]

    (dm_rkv, dm_wa, dm_xg, dlw, dla, tw_s, mwa_s, sg_s, G["rw_mu"], G["rw_w0"], G["rw_a0"], G["rw_k_k"],
     G["rw_k_a"]) = _rowwise(
        "d_rwkv_pre", rw_pre_bwd, T, tT,
        rows=[p_rkv, p_wa, p_xg, dr_s, dw_s, dk_s, dv_s, da_s, db_s, dgate, dr_b, dk2_b, dv_b],
        prevs=[p_rkv, p_wa, p_xg], consts=pre_consts,
        outs=[("row", RKV, F32), ("row", WA, F32), ("row", XG, F32), ("row", RW_WIDTH, BF16),
              ("row", RW_WIDTH, BF16), ("row", WA, BF16), ("row", WA, BF16), ("row", XG, BF16),
              ("acc", (1, RW_COLS))] + [("acc", (1, RW_WIDTH))] * 4)
    G["rw_w_up"] = _mm("d_rw_w_up", tw_s, dlw, "tn")[:64]
    G["rw_a_up"] = _mm("d_rw_a_up", mwa_s, dla, "tn")[64:]
    G["rw_g_up"] = _mm("d_rw_g_up", sg_s, dgate, "tn")

    def shift_bwd(i, n, rows, pv, nexts, consts):
        return [rows[j] * (1.0 - consts[j]) + _shift_up(rows[j], nexts[j], i, n, 1) * consts[j] for j in range(3)]

    dp_rkv, dp_wa, dp_xg = _rowwise(
        "d_token_shift", shift_bwd, T, tT, rows=[dm_rkv, dm_wa, dm_xg], nexts=[dm_rkv, dm_wa, dm_xg],
        consts=[mu_rkv, mu_wa, mu_xg], outs=[("row", RKV, BF16), ("row", WA, BF16), ("row", XG, BF16)])

    G["w_in"] = jnp.concatenate([_mm("d_w_rkv", h, dp_rkv, "tn"), _mm("d_w_wa", h, dp_wa, "tn"),
                                 _mm("d_w_xg", h, dp_xg, "tn"), _mm("d_w_att", h, dp_att, "tn", tn=768)], axis=1)
    dh = _mm("d_h_gate", dz_gate, W["w_gate"], "nt")
    dh = _mm("d_h_rkv", dp_rkv, w_rkv, "nt", add=dh)
    dh = _mm("d_h_wa", dp_wa, w_wa, "nt", add=dh)
    dh = _mm("d_h_xg", dp_xg, w_xg, "nt", add=dh)
    dh = _mm("d_h_att", dp_att, w_att, "nt", add=dh)
    dx, G["g_mix"] = _rowwise("d_norm_mix", norm_bwd, T, tT, rows=[x, dh, dx1], consts=[W["g_mix"]],
                              outs=[("row", D_MODEL, F32), ("acc", (1, D_MODEL))])
    return loss_acc[:, :1], dx, G


HBM_SPEC = pl.BlockSpec(memory_space=pltpu.HBM)


def _place():
    x, y, c = lax.axis_index("x"), lax.axis_index("y"), lax.axis_index("c")
    return x, y, c, [(1 - x, y), (x, 1 - y), (1 - x, 1 - y)]


def _remote(src, dst, send_sems, recv_sems, k, to):
    return pltpu.make_async_remote_copy(src_ref=src, dst_ref=dst, send_sem=send_sems.at[k], recv_sem=recv_sems.at[k],
                                        device_id=to, device_id_type=MESH)


ROW_ALIGN = 16


def _half_rows(ref_rows, c, first):
    half = ref_rows // 2
    which = c if first else 1 - c
    return pl.ds(pl.multiple_of(which * half, ROW_ALIGN), half)


def _gather_chips(shards):
    n = len(shards)
    split = [s.shape[0] % (2 * ROW_ALIGN) == 0 for s in shards]

    def body(*refs):
        w_refs, out_refs = refs[:n], refs[n:2 * n]
        send_sems, recv_sems = refs[2 * n:]
        x, y, c, chips = _place()
        me = 2 * x + y
        sends, passed = [], []
        for i in range(n):
            for j, (px, py) in enumerate(chips):
                if split[i]:
                    mine = _half_rows(w_refs[i].shape[0], c, True)
                    cp = _remote(w_refs[i].at[mine], out_refs[i].at[me, mine], send_sems, recv_sems, 6 * i + j,
                                 (px, py, c))
                else:
                    cp = _remote(w_refs[i], out_refs[i].at[me], send_sems, recv_sems, 6 * i + j, (px, py, c))
                cp.start()
                sends.append(cp)
        for i in range(n):
            for j, (px, py) in enumerate(chips):
                if split[i]:
                    landed = out_refs[i].at[2 * px + py, _half_rows(w_refs[i].shape[0], c, True)]
                    _remote(landed, landed, send_sems, recv_sems, 6 * i + j, (px, py, c)).wait_recv()
                    cp = _remote(landed, landed, send_sems, recv_sems, 6 * i + 3 + j, (x, y, 1 - c))
                    cp.start()
                    passed.append(cp)
                else:
                    landed = out_refs[i].at[2 * px + py]
                    _remote(landed, landed, send_sems, recv_sems, 6 * i + j, (px, py, c)).wait_recv()
        for i in range(n):
            if split[i]:
                for j, (px, py) in enumerate(chips):
                    landed = out_refs[i].at[2 * px + py, _half_rows(w_refs[i].shape[0], c, False)]
                    _remote(landed, landed, send_sems, recv_sems, 6 * i + 3 + j, (x, y, 1 - c)).wait_recv()
        for cp in sends + passed:
            cp.wait_send()

    outs = pl.pallas_call(
        body, name="gather_weights", in_specs=[HBM_SPEC] * n, out_specs=[HBM_SPEC] * n,
        out_shape=[jax.ShapeDtypeStruct((N_CHIPS,) + s.shape, s.dtype) for s in shards],
        scratch_shapes=[pltpu.SemaphoreType.DMA((6 * n,)), pltpu.SemaphoreType.DMA((6 * n,))],
    )(*shards)
    me = 2 * lax.axis_index("x") + lax.axis_index("y")
    return [lax.dynamic_update_slice(o, s[None], (me, 0, 0)) for o, s in zip(outs, shards, strict=True)]


def _swap_halves(gs):
    n = len(gs)

    def body(*refs):
        g_refs, out_refs = refs[:n], refs[n:2 * n]
        send_sems, recv_sems = refs[2 * n:]
        x, y, c, _ = _place()
        cps = []
        for i in range(n):
            theirs = _half_rows(g_refs[i].shape[1], c, False)
            cp = _remote(g_refs[i].at[:, theirs, :], out_refs[i], send_sems, recv_sems, i, (x, y, 1 - c))
            cp.start()
            cps.append(cp)
        for cp in cps:
            cp.wait()

    return pl.pallas_call(
        body, name="swap_halves", in_specs=[HBM_SPEC] * n, out_specs=[HBM_SPEC] * n,
        out_shape=[jax.ShapeDtypeStruct((N_CHIPS, g.shape[1] // 2, g.shape[2]), g.dtype) for g in gs],
        scratch_shapes=[pltpu.SemaphoreType.DMA((n,)), pltpu.SemaphoreType.DMA((n,))],
    )(*gs)


def _scatter_chips(parts):
    n = len(parts)

    def body(*refs):
        p_refs, out_refs = refs[:n], refs[n:2 * n]
        send_sems, recv_sems = refs[2 * n:]
        x, y, c, chips = _place()
        me = 2 * x + y
        sends = []
        for i in range(n):
            for j, (px, py) in enumerate(chips):
                cp = _remote(p_refs[i].at[2 * px + py], out_refs[i].at[me], send_sems, recv_sems, 3 * i + j,
                             (px, py, c))
                cp.start()
                sends.append(cp)
        for i in range(n):
            for j, (px, py) in enumerate(chips):
                landed = out_refs[i].at[2 * px + py]
                _remote(landed, landed, send_sems, recv_sems, 3 * i + j, (px, py, c)).wait_recv()
        for cp in sends:
            cp.wait_send()

    outs = pl.pallas_call(
        body, name="scatter_grads", in_specs=[HBM_SPEC] * n, out_specs=[HBM_SPEC] * n,
        out_shape=[jax.ShapeDtypeStruct(p.shape, p.dtype) for p in parts],
        scratch_shapes=[pltpu.SemaphoreType.DMA((3 * n,)), pltpu.SemaphoreType.DMA((3 * n,))],
    )(*parts)
    me = 2 * lax.axis_index("x") + lax.axis_index("y")
    own = [lax.dynamic_slice_in_dim(p, me, 1, axis=0) for p in parts]
    return [lax.dynamic_update_slice(o, s, (me, 0, 0)) for o, s in zip(outs, own, strict=True)]


def _join_halves(reds):
    n = len(reds)

    def body(*refs):
        r_refs, out_refs = refs[:n], refs[n:2 * n]
        send_sems, recv_sems = refs[2 * n:]
        x, y, c, _ = _place()
        cps = []
        for i in range(n):
            mine = _half_rows(out_refs[i].shape[0], c, True)
            cp = _remote(r_refs[i], out_refs[i].at[mine], send_sems, recv_sems, i, (x, y, 1 - c))
            cp.start()
            cps.append(cp)
        for cp in cps:
            cp.wait()

    outs = pl.pallas_call(
        body, name="join_halves", in_specs=[HBM_SPEC] * n, out_specs=[HBM_SPEC] * n,
        out_shape=[jax.ShapeDtypeStruct((2 * r.shape[0], r.shape[1]), r.dtype) for r in reds],
        scratch_shapes=[pltpu.SemaphoreType.DMA((n,)), pltpu.SemaphoreType.DMA((n,))],
    )(*reds)
    c = lax.axis_index("c")
    return [lax.dynamic_update_slice(o, r, (c * r.shape[0], 0)) for o, r in zip(outs, reds, strict=True)]


def _gather_all(vec):
    R = vec.shape[0]

    def body(v_ref, out_ref, send_sems, recv_sems, local_sem):
        x, y, c, _ = _place()
        me = 4 * x + 2 * y + c
        local = pltpu.make_async_copy(v_ref, out_ref.at[me], local_sem)
        local.start()
        peers = [(x ^ (k >> 2), y ^ ((k >> 1) & 1), c ^ (k & 1)) for k in range(1, N_DEV)]
        sends = [_remote(v_ref, out_ref.at[me], send_sems, recv_sems, k, to) for k, to in enumerate(peers)]
        for cp in sends:
            cp.start()
        for k, (px, py, pc) in enumerate(peers):
            landed = out_ref.at[4 * px + 2 * py + pc]
            _remote(landed, landed, send_sems, recv_sems, k, (px, py, pc)).wait_recv()
        for cp in sends:
            cp.wait_send()
        local.wait()

    return pl.pallas_call(
        body, name="gather_small", in_specs=[HBM_SPEC], out_specs=HBM_SPEC,
        out_shape=jax.ShapeDtypeStruct((N_DEV, R, LANES), vec.dtype),
        scratch_shapes=[pltpu.SemaphoreType.DMA((7,)), pltpu.SemaphoreType.DMA((7,)), pltpu.SemaphoreType.DMA],
    )(vec)


SUM_TILE_BYTES = 4 * 1024 * 1024


def _sum_rows(half, cols):
    best = ROW_ALIGN
    for t in range(ROW_ALIGN, half + 1, ROW_ALIGN):
        if half % t == 0 and N_CHIPS * t * cols * 4 <= SUM_TILE_BYTES:
            best = t
    return best


def _sum_cores(name, g, theirs, core):
    _, R, C = g.shape
    half = R // 2
    tr = _sum_rows(half, C)
    nb = half // tr

    def body(core_ref, g_ref, t_ref, o_ref):
        o_ref[...] = (g_ref[...] + t_ref[...]).astype(o_ref.dtype)

    grid_spec = pltpu.PrefetchScalarGridSpec(
        num_scalar_prefetch=1, grid=(nb,),
        in_specs=[pl.BlockSpec((N_CHIPS, tr, C), lambda i, core_ref: (0, core_ref[0] * nb + i, 0)),
                  pl.BlockSpec((N_CHIPS, tr, C), lambda i, core_ref: (0, i, 0))],
        out_specs=pl.BlockSpec((N_CHIPS, tr, C), lambda i, core_ref: (0, i, 0)))
    return pl.pallas_call(
        body, name=name, grid_spec=grid_spec, out_shape=jax.ShapeDtypeStruct((N_CHIPS, half, C), BF16),
        compiler_params=_params(("parallel",)),
    )(core, g, theirs)


def _sum_chips(name, parts):
    _, H, C = parts.shape
    tr = _sum_rows(H, C)

    def body(p_ref, o_ref):
        acc = p_ref[0].astype(F32)
        for k in range(1, N_CHIPS):
            acc = acc + p_ref[k].astype(F32)
        o_ref[...] = acc

    return pl.pallas_call(
        body, name=name, grid=(H // tr,),
        in_specs=[pl.BlockSpec((N_CHIPS, tr, C), lambda i: (0, i, 0))],
        out_specs=pl.BlockSpec((tr, C), lambda i: (i, 0)),
        out_shape=jax.ShapeDtypeStruct((H, C), F32),
        compiler_params=_params(("parallel",)),
    )(parts)


def _adamw_math(w, g, m, v):
    m = ADAM_B1 * m + (1.0 - ADAM_B1) * g
    v = ADAM_B2 * v + (1.0 - ADAM_B2) * (g * g)
    m_hat = m / (1.0 - ADAM_B1 ** ADAM_STEP)
    v_hat = v / (1.0 - ADAM_B2 ** ADAM_STEP)
    delta = -ADAM_LR * (m_hat / (jnp.sqrt(v_hat) + ADAM_EPS) + ADAM_WD * w)
    return delta, m, v


def _adamw(name, w, g, m, v):
    R, C = w.shape
    tr = R
    if R % SUBLANES == 0:
        for cand in range(SUBLANES, min(R, 256) + 1, SUBLANES):
            if R % cand == 0:
                tr = cand

    def body(w_ref, g_ref, m_ref, v_ref, d_ref, nm_ref, nv_ref):
        d, nm, nv = _adamw_math(w_ref[...], g_ref[...], m_ref[...], v_ref[...])
        d_ref[...] = d
        nm_ref[...] = nm
        nv_ref[...] = nv

    spec = pl.BlockSpec((tr, C), lambda i: (i, 0))
    shape = jax.ShapeDtypeStruct((R, C), F32)
    return pl.pallas_call(
        body, name=name, grid=(R // tr,), in_specs=[spec] * 4, out_specs=[spec] * 3, out_shape=[shape] * 3,
        compiler_params=_params(("parallel",)),
    )(w, g, m, v)


def _adamw_small(parts, w, m, v):
    n = parts.shape[0]

    def body(p_ref, w_ref, m_ref, v_ref, g_ref, d_ref, nm_ref, nv_ref):
        g = p_ref[0]
        for k in range(1, n):
            g = g + p_ref[k]
        d, nm, nv = _adamw_math(w_ref[...], g, m_ref[...], v_ref[...])
        g_ref[...] = g
        d_ref[...] = d
        nm_ref[...] = nm
        nv_ref[...] = nv

    shape = jax.ShapeDtypeStruct(w.shape, F32)
    return pl.pallas_call(body, name="adamw_small", out_shape=[shape] * 4, compiler_params=_params())(parts, w, m, v)


WEIGHTS = ['g_mix', 'w_in', 'rw_mu', 'rw_w0', 'rw_w_up', 'rw_a0', 'rw_a_up', 'rw_g_up', 'rw_k_k', 'rw_k_a',
           'rw_r_k', 'rw_ln_g', 'rw_ln_b', 'w_branch_a', 'w_branch_b', 'w_gate', 'b_gate', 'w_out', 'g_ffn', 'w_up',
           'conv_w', 'conv_b', 'w_down', 'g_ple', 'w_ple_gate', 'w_ple', 'g_final']
ARG_NAMES = (['x', 'p'] + WEIGHTS + ['loss_target'] + ['m_' + n for n in WEIGHTS] + ['v_' + n for n in WEIGHTS])
SHARDED = {'w_in': 1, 'rw_w_up': 1, 'rw_a_up': 1, 'rw_g_up': 1, 'w_branch_a': 1, 'w_branch_b': 1, 'w_gate': 1,
           'w_out': 0, 'w_up': 1, 'conv_w': 1, 'w_down': 0, 'w_ple_gate': 0, 'w_ple': 1}
SMALL = [n for n in WEIGHTS if n not in SHARDED]
WHOLE = ['conv_w']
SPLIT = [n for n in SHARDED if n not in WHOLE]
PACK_ALIGN = SUBLANES * LANES


def _pack_rows(flat_parts):
    flat = jnp.concatenate(flat_parts, axis=1)
    n = flat.shape[1]
    padded = -(-n // PACK_ALIGN) * PACK_ALIGN
    flat = jnp.pad(flat, ((0, 0), (0, padded - n)))
    return flat.reshape(padded // LANES, LANES)


def _full_from_shards(stack, axis):
    _, R, C = stack.shape
    if axis == 0:
        return stack.reshape(N_CHIPS * R, C)
    return stack.transpose(1, 0, 2).reshape(R, N_CHIPS * C)


def _shards_from_full(full, axis):
    R, C = full.shape
    if axis == 0:
        return full.reshape(N_CHIPS, R // N_CHIPS, C)
    return full.reshape(R, N_CHIPS, C // N_CHIPS).transpose(1, 0, 2)


def kernel(x, p, g_mix, w_in, rw_mu, rw_w0, rw_w_up, rw_a0, rw_a_up, rw_g_up, rw_k_k, rw_k_a, rw_r_k, rw_ln_g, rw_ln_b, w_branch_a, w_branch_b, w_gate, b_gate, w_out, g_ffn, w_up, conv_w, conv_b, w_down, g_ple, w_ple_gate, w_ple, g_final, loss_target, m_g_mix, m_w_in, m_rw_mu, m_rw_w0, m_rw_w_up, m_rw_a0, m_rw_a_up, m_rw_g_up, m_rw_k_k, m_rw_k_a, m_rw_r_k, m_rw_ln_g, m_rw_ln_b, m_w_branch_a, m_w_branch_b, m_w_gate, m_b_gate, m_w_out, m_g_ffn, m_w_up, m_conv_w, m_conv_b, m_w_down, m_g_ple, m_w_ple_gate, m_w_ple, m_g_final, v_g_mix, v_w_in, v_rw_mu, v_rw_w0, v_rw_w_up, v_rw_a0, v_rw_a_up, v_rw_g_up, v_rw_k_k, v_rw_k_a, v_rw_r_k, v_rw_ln_g, v_rw_ln_b, v_w_branch_a, v_w_branch_b, v_w_gate, v_b_gate, v_w_out, v_g_ffn, v_w_up, v_conv_w, v_conv_b, v_w_down, v_g_ple, v_w_ple_gate, v_w_ple, v_g_final):
    given = dict(zip(ARG_NAMES, (x, p, g_mix, w_in, rw_mu, rw_w0, rw_w_up, rw_a0, rw_a_up, rw_g_up, rw_k_k, rw_k_a, rw_r_k, rw_ln_g, rw_ln_b, w_branch_a, w_branch_b, w_gate, b_gate, w_out, g_ffn, w_up, conv_w, conv_b, w_down, g_ple, w_ple_gate, w_ple, g_final, loss_target, m_g_mix, m_w_in, m_rw_mu, m_rw_w0, m_rw_w_up, m_rw_a0, m_rw_a_up, m_rw_g_up, m_rw_k_k, m_rw_k_a, m_rw_r_k, m_rw_ln_g, m_rw_ln_b, m_w_branch_a, m_w_branch_b, m_w_gate, m_b_gate, m_w_out, m_g_ffn, m_w_up, m_conv_w, m_conv_b, m_w_down, m_g_ple, m_w_ple_gate, m_w_ple, m_g_final, v_g_mix, v_w_in, v_rw_mu, v_rw_w0, v_rw_w_up, v_rw_a0, v_rw_a_up, v_rw_g_up, v_rw_k_k, v_rw_k_a, v_rw_r_k, v_rw_ln_g, v_rw_ln_b, v_w_branch_a, v_w_branch_b, v_w_gate, v_b_gate, v_w_out, v_g_ffn, v_w_up, v_conv_w, v_conv_b, v_w_down, v_g_ple, v_w_ple_gate, v_w_ple, v_g_final), strict=True))

    def two_d(name, prefix=""):
        a = given[prefix + name]
        if name == "g_final":
            return a.reshape(1, D_MODEL)
        if name == "rw_r_k":
            return a.reshape(1, RW_WIDTH)
        return a[0] if a.ndim == 3 else a

    gathered = _gather_chips([two_d(n) if n in WHOLE else two_d(n).astype(BF16) for n in SHARDED])
    W = {n: _full_from_shards(g, SHARDED[n]) for n, g in zip(SHARDED, gathered, strict=True)}
    for n in SMALL:
        W[n] = two_d(n)
    W["rw_r_k"] = W["rw_r_k"].reshape(RW_HEADS, RW_HEAD_DIM)

    loss_part, grad_x, G = _local_step(x[0], p[0, 0], W, loss_target[0])

    core = lax.axis_index("c").astype(jnp.int32).reshape(1)
    by_chip = [_shards_from_full(G[n], SHARDED[n]) for n in SPLIT]
    theirs = _swap_halves(by_chip)
    pair = [_sum_cores("sum_cores_" + n, g, t, core) for n, g, t in zip(SPLIT, by_chip, theirs, strict=True)]
    landed = _scatter_chips(pair)
    reduced = [_sum_chips("sum_chips_" + n, q) for n, q in zip(SPLIT, landed, strict=True)]
    shard_grads = dict(zip(SPLIT, _join_halves(reduced), strict=True))

    small_sizes = {n: two_d(n).shape[1] for n in SMALL}
    n_small = sum(small_sizes.values())
    whole_sizes = {n: G[n].shape[0] * G[n].shape[1] for n in WHOLE}
    n_whole = sum(whole_sizes.values())

    def pack_small(parts, rest):
        return _pack_rows([a.reshape(1, -1) for a in parts] + [rest])

    G["rw_r_k"] = G["rw_r_k"].reshape(1, RW_WIDTH)
    rest = jnp.zeros((1, n_whole + 1), F32)
    all_small = _gather_all(pack_small([G[n] for n in SMALL] + [G[n] for n in WHOLE], loss_part))
    gs, ds, nms, nvs = _adamw_small(all_small, pack_small([two_d(n) for n in SMALL], rest),
                                    pack_small([two_d(n, "m_") for n in SMALL], rest),
                                    pack_small([two_d(n, "v_") for n in SMALL], rest))
    gs, ds, nms, nvs = (a.reshape(-1) for a in (gs, ds, nms, nvs))
    loss = gs[n_small + n_whole]
    chip = 2 * lax.axis_index("x") + lax.axis_index("y")
    off = n_small
    for n in WHOLE:
        full = gs[off:off + whole_sizes[n]].reshape(G[n].shape)
        off += whole_sizes[n]
        width = two_d(n).shape[1]
        shard_grads[n] = lax.dynamic_slice_in_dim(full, chip * width, width, axis=1)

    grads, deltas, new_m, new_v = {}, {}, {}, {}
    for n in SHARDED:
        g = shard_grads[n]
        d, nm, nv = _adamw("adamw_" + n, two_d(n), g, two_d(n, "m_"), two_d(n, "v_"))
        grads[n], deltas[n], new_m[n], new_v[n] = g, d, nm, nv
    off = 0
    for n in SMALL:
        sl = slice(off, off + small_sizes[n])
        off += small_sizes[n]
        grads[n], deltas[n], new_m[n], new_v[n] = gs[sl], ds[sl], nms[sl], nvs[sl]
    outs = [loss, grad_x[None]]
    for table in (grads, deltas, new_m, new_v):
        outs += [table[n].reshape(given[n].shape) for n in WEIGHTS]
    return tuple(outs)
```

```python
import math

import jax
import jax.numpy as jnp
import numpy as np
from jax import lax
from jax.experimental import pallas as pl
from jax.experimental.pallas import tpu as pltpu

F32 = jnp.float32
BF16 = jnp.bfloat16

D_MODEL = 1024
NORM_EPS = 1e-6
RW_HEADS = 8
RW_HEAD_DIM = 64
RW_WIDTH = 512
RW_LN_EPS = 64e-5
ATT_GROUP_DILATION = (1, 4, 16)
ATT_BLOCK = 128
ATT_HEADS = 12
ATT_HEAD_DIM = 64
ATT_GROUP_WIDTH = 256
ATT_WIDTH = 768
D_FF = 3072

ADAM_LR = 0.001
ADAM_B1 = 0.9
ADAM_B2 = 0.999
ADAM_EPS = 1e-08
ADAM_WD = 0.01
ADAM_STEP = 10

SUBLANES = 8
LANES = 128
VMEM_LIMIT = 56 * 1024 * 1024
N_CHIPS = 4
N_DEV = 8
MESH = pl.DeviceIdType.MESH


def _params(sem=None):
    return pltpu.CompilerParams(dimension_semantics=sem, vmem_limit_bytes=VMEM_LIMIT)


def _pick(dim, pref):
    if dim % LANES != 0 or dim <= pref:
        return dim
    best = LANES
    for t in range(LANES, pref + 1, LANES):
        if dim % t == 0:
            best = t
    return best


def _mm(name, a, b, mode, out_dtype=F32, add=None, tm=1024, tn=1024, tk=1024):
    if mode == "nn":
        (M, K), (K2, N) = a.shape, b.shape
    elif mode == "nt":
        (M, K), (N, K2) = a.shape, b.shape
    else:
        (K, M), (K2, N) = a.shape, b.shape
    assert K == K2, (name, a.shape, b.shape, mode)
    tm, tn, tk = _pick(M, tm), _pick(N, tn), _pick(K, tk)
    nk = K // tk
    if mode == "nn":
        a_spec = pl.BlockSpec((tm, tk), lambda i, j, k: (i, k))
        b_spec = pl.BlockSpec((tk, tn), lambda i, j, k: (k, j))
        dims = (((1,), (0,)), ((), ()))
    elif mode == "nt":
        a_spec = pl.BlockSpec((tm, tk), lambda i, j, k: (i, k))
        b_spec = pl.BlockSpec((tn, tk), lambda i, j, k: (j, k))
        dims = (((1,), (1,)), ((), ()))
    else:
        a_spec = pl.BlockSpec((tk, tm), lambda i, j, k: (k, i))
        b_spec = pl.BlockSpec((tk, tn), lambda i, j, k: (k, j))
        dims = (((0,), (0,)), ((), ()))
    o_spec = pl.BlockSpec((tm, tn), lambda i, j, k: (i, j))
    has_add = add is not None

    def body(*refs):
        if has_add:
            a_ref, b_ref, add_ref, o_ref, acc_ref = refs
        else:
            a_ref, b_ref, o_ref, acc_ref = refs
        k = pl.program_id(2)
        part = lax.dot_general(a_ref[...].astype(BF16), b_ref[...].astype(BF16), dims,
                               preferred_element_type=F32)

        @pl.when(k == 0)
        def _():
            acc_ref[...] = part

        @pl.when(k > 0)
        def _():
            acc_ref[...] += part

        @pl.when(k == nk - 1)
        def _():
            res = acc_ref[...]
            if has_add:
                res = res + add_ref[...].astype(F32)
            o_ref[...] = res.astype(o_ref.dtype)

    ins = [a, b] + ([add] if has_add else [])
    in_specs = [a_spec, b_spec] + ([o_spec] if has_add else [])
    return pl.pallas_call(
        body, name=name, grid=(M // tm, N // tn, nk),
        in_specs=in_specs, out_specs=o_spec,
        out_shape=jax.ShapeDtypeStruct((M, N), out_dtype),
        scratch_shapes=[pltpu.VMEM((tm, tn), F32)],
        compiler_params=_params(("parallel", "parallel", "arbitrary")),
    )(*ins)


def _rowwise(name, fn, T, tT, rows=(), prevs=(), nexts=(), consts=(), outs=()):
    n = T // tT
    per8 = tT // SUBLANES
    in_specs, ins = [], []
    for arr in rows:
        in_specs.append(pl.BlockSpec((tT, arr.shape[1]), lambda i: (i, 0)))
        ins.append(arr)
    for arr in prevs:
        in_specs.append(pl.BlockSpec((SUBLANES, arr.shape[1]), lambda i: (jnp.maximum(i * per8 - 1, 0), 0)))
        ins.append(arr)
    for arr in nexts:
        in_specs.append(pl.BlockSpec((SUBLANES, arr.shape[1]),
                                     lambda i: (jnp.minimum((i + 1) * per8, T // SUBLANES - 1), 0)))
        ins.append(arr)
    for arr in consts:
        in_specs.append(pl.BlockSpec(arr.shape, lambda i, nd=arr.ndim: (0,) * nd))
        ins.append(arr)
    out_specs, out_shapes = [], []
    for o in outs:
        if o[0] == "row":
            out_specs.append(pl.BlockSpec((tT, o[1]), lambda i: (i, 0)))
            out_shapes.append(jax.ShapeDtypeStruct((T, o[1]), o[2]))
        else:
            out_specs.append(pl.BlockSpec(o[1], lambda i: (0, 0)))
            out_shapes.append(jax.ShapeDtypeStruct(o[1], F32))
    nr, npv, nnx, nc = len(rows), len(prevs), len(nexts), len(consts)
    n_in = nr + npv + nnx + nc

    def body(*refs):
        i = pl.program_id(0)
        vals = [r[...] for r in refs[:n_in]]
        res = fn(i, n, vals[:nr], vals[nr:nr + npv], vals[nr + npv:nr + npv + nnx], vals[nr + npv + nnx:])
        for o, o_ref, val in zip(outs, refs[n_in:], res, strict=True):
            if o[0] == "row":
                o_ref[...] = val.astype(o_ref.dtype)
            else:
                @pl.when(i == 0)
                def _(o_ref=o_ref, val=val):
                    o_ref[...] = val.astype(F32)

                @pl.when(i > 0)
                def _(o_ref=o_ref, val=val):
                    o_ref[...] += val.astype(F32)

    res = pl.pallas_call(
        body, name=name, grid=(n,), in_specs=in_specs, out_specs=out_specs, out_shape=out_shapes,
        compiler_params=_params(("arbitrary",)),
    )(*ins)
    return list(res)


def _shift_down(x, prev8, i, s):
    rolled = pltpu.roll(x, s, 0)
    head = pltpu.roll(prev8, s, 0)
    head = jnp.where(i == 0, jnp.zeros_like(head), head)
    rid = lax.broadcasted_iota(jnp.int32, head.shape, 0)
    first = jnp.where(rid < s, head, rolled[:SUBLANES])
    if x.shape[0] == SUBLANES:
        return first
    return jnp.concatenate([first, rolled[SUBLANES:]], axis=0)


def _shift_up(x, next8, i, n, s):
    tT = x.shape[0]
    rolled = pltpu.roll(x, tT - s, 0)
    tail = pltpu.roll(next8, SUBLANES - s, 0)
    tail = jnp.where(i == n - 1, jnp.zeros_like(tail), tail)
    rid = lax.broadcasted_iota(jnp.int32, tail.shape, 0)
    last = jnp.where(rid >= SUBLANES - s, tail, rolled[tT - SUBLANES:])
    return jnp.concatenate([rolled[:tT - SUBLANES], last], axis=0)


def _colsum(x):
    return jnp.sum(x, axis=0, keepdims=True)


def _segsum(x, bd):
    return jnp.dot(x, bd, precision=lax.Precision.HIGH, preferred_element_type=F32)


def _block_diag_ones(width, seg):
    idx = np.arange(width) // seg
    return jnp.asarray((idx[:, None] == idx[None, :]).astype(np.float32))


def _sigmoid(z):
    return 1.0 / (1.0 + jnp.exp(-z))


def _softplus(z):
    return jnp.maximum(z, 0.0) + jnp.log(1.0 + jnp.exp(-jnp.abs(z)))


def _rms_fwd(x, g):
    r = lax.rsqrt(jnp.mean(x * x, axis=-1, keepdims=True) + NORM_EPS)
    return x * r * g


def _rms_bwd(x, g, dy):
    r = lax.rsqrt(jnp.mean(x * x, axis=-1, keepdims=True) + NORM_EPS)
    gdy = dy * g
    dx = r * (gdy - x * (r * r) * jnp.mean(x * gdy, axis=-1, keepdims=True))
    return dx, dy * x * r


GELU_C = math.sqrt(2.0 / math.pi)


def _gelu(x):
    return 0.5 * x * (1.0 + jnp.tanh(GELU_C * (x + 0.044715 * x * x * x)))


def _gelu_and_grad(x):
    th = jnp.tanh(GELU_C * (x + 0.044715 * x * x * x))
    half = 0.5 * (1.0 + th)
    return x * half, half + 0.5 * x * (1.0 - th * th) * GELU_C * (1.0 + 3.0 * 0.044715 * x * x)


RW_CHUNK = 64
NN = (((1,), (0,)), ((), ()))
NT = (((1,), (1,)), ((), ()))
TN = (((0,), (0,)), ((), ()))


def _hdot(a, b, dims):
    return lax.dot_general(a, b, dims, precision=lax.Precision.HIGH, preferred_element_type=F32)


def _chunk_masks():
    ti = lax.broadcasted_iota(jnp.int32, (RW_CHUNK, RW_CHUNK), 0)
    tj = lax.broadcasted_iota(jnp.int32, (RW_CHUNK, RW_CHUNK), 1)
    return tj <= ti, tj < ti, (ti == tj).astype(F32)


def _head(x, h):
    return x[:, h * RW_HEAD_DIM:(h + 1) * RW_HEAD_DIM]


def _heads(fn):
    return [fn(h) for h in range(RW_HEADS)]


def _chunk_rows(r, lw, k, a, b, incl_f):
    c = _hdot(incl_f, lw, NN)
    e_prev, e_neg, e_pos = jnp.exp(c - lw), jnp.exp(-c), jnp.exp(c)
    return dict(At=a * e_prev, Bt=b * e_neg, Kt=k * e_neg, Rt=r * e_pos, e_prev=e_prev, e_neg=e_neg, e_pos=e_pos)


def _chunk_coeffs(q, incl, strict):
    A1 = _heads(lambda h: jnp.where(strict, _hdot(_head(q["At"], h), _head(q["Bt"], h), NT), 0.0))
    A2 = _heads(lambda h: jnp.where(strict, _hdot(_head(q["At"], h), _head(q["Kt"], h), NT), 0.0))
    W1 = _heads(lambda h: jnp.where(incl, _hdot(_head(q["Rt"], h), _head(q["Bt"], h), NT), 0.0))
    W2 = _heads(lambda h: jnp.where(incl, _hdot(_head(q["Rt"], h), _head(q["Kt"], h), NT), 0.0))
    return A1, A2, W1, W2


def _rwkv_chunk_prep(r, lw, k, a, b, v):
    T = r.shape[0]
    nC = T // RW_CHUNK
    H, N = RW_HEADS, RW_HEAD_DIM

    def body(r_ref, lw_ref, k_ref, a_ref, b_ref, v_ref,
             at_ref, bt_ref, kt_ref, rt_ref, a2v_ref, w2v_ref, ti_ref, w1_ref, a2_ref, w2_ref, pl_ref):
        incl, strict, eye = _chunk_masks()
        q = _chunk_rows(r_ref[...], lw_ref[...], k_ref[...], a_ref[...], b_ref[...], incl.astype(F32))
        at_ref[...], bt_ref[...], kt_ref[...], rt_ref[...] = q["At"], q["Bt"], q["Kt"], q["Rt"]
        pl_ref[0] = jnp.broadcast_to(q["e_pos"][RW_CHUNK - 1:RW_CHUNK, :], (SUBLANES, RW_WIDTH))
        A1, A2, W1, W2 = _chunk_coeffs(q, incl, strict)
        V = v_ref[...]
        a2v_ref[...] = jnp.concatenate(_heads(lambda h: _hdot(A2[h], _head(V, h), NN)), axis=1)
        w2v_ref[...] = jnp.concatenate(_heads(lambda h: _hdot(W2[h], _head(V, h), NN)), axis=1)
        tinv, pw = [eye + m for m in A1], A1
        for _ in range(5):
            pw = [_hdot(m, m, NN) for m in pw]
            tinv = [t + _hdot(t, m, NN) for t, m in zip(tinv, pw, strict=True)]
        for h in range(H):
            ti_ref[0, h] = tinv[h]
            w1_ref[0, h] = W1[h]
            a2_ref[0, h] = A2[h]
            w2_ref[0, h] = W2[h]

    row_spec = pl.BlockSpec((RW_CHUNK, RW_WIDTH), lambda n: (n, 0))
    st_spec = pl.BlockSpec((1, H, N, N), lambda n: (n, 0, 0, 0))
    row_shape = jax.ShapeDtypeStruct((T, RW_WIDTH), F32)
    st_shape = jax.ShapeDtypeStruct((nC, H, N, N), F32)
    return pl.pallas_call(
        body, name="rwkv_chunk_prep", grid=(nC,),
        in_specs=[row_spec] * 6,
        out_specs=[row_spec] * 6 + [st_spec] * 4 + [pl.BlockSpec((1, SUBLANES, RW_WIDTH), lambda n: (n, 0, 0))],
        out_shape=[row_shape] * 6 + [st_shape] * 4 + [jax.ShapeDtypeStruct((nC, SUBLANES, RW_WIDTH), F32)],
        compiler_params=_params(("parallel",)),
    )(r, lw, k, a, b, v)


def _rwkv_chunk_fwd(v, at, bt, kt, rt, a2v, w2v, tinv, w1, plast):
    T = v.shape[0]
    nC = T // RW_CHUNK
    H, N = RW_HEADS, RW_HEAD_DIM

    def body(v_ref, at_ref, bt_ref, kt_ref, rt_ref, a2v_ref, w2v_ref, ti_ref, w1_ref, pl_ref,
             y_ref, sa_ref, s0_ref, S_ref):
        @pl.when(pl.program_id(0) == 0)
        def _():
            S_ref[...] = jnp.zeros_like(S_ref)

        V, At, Bt, Kt, Rt = v_ref[...], at_ref[...], bt_ref[...], kt_ref[...], rt_ref[...]
        A2V, W2V, p_last = a2v_ref[...], w2v_ref[...], pl_ref[0, 0:1, :]
        S0 = _heads(lambda h: S_ref[h])
        for h in range(H):
            s0_ref[0, h] = S0[h]
        Z = _heads(lambda h: _hdot(_head(At, h), S0[h], NT) + _head(A2V, h))
        Sa = _heads(lambda h: _hdot(ti_ref[0, h], Z[h], NN))
        X = _heads(lambda h: S0[h] + _hdot(Sa[h], _head(Bt, h), TN) + _hdot(_head(V, h), _head(Kt, h), TN))
        for h in range(H):
            S_ref[h] = X[h] * _head(p_last, h)
        Y = _heads(lambda h: _hdot(_head(Rt, h), S0[h], NT) + _hdot(w1_ref[0, h], Sa[h], NN) + _head(W2V, h))
        y_ref[...] = jnp.concatenate(Y, axis=1)
        sa_ref[...] = jnp.concatenate(Sa, axis=1)

    row_spec = pl.BlockSpec((RW_CHUNK, RW_WIDTH), lambda n: (n, 0))
    st_spec = pl.BlockSpec((1, H, N, N), lambda n: (n, 0, 0, 0))
    row_shape = jax.ShapeDtypeStruct((T, RW_WIDTH), F32)
    return pl.pallas_call(
        body, name="rwkv_chunk_fwd", grid=(nC,),
        in_specs=[row_spec] * 7 + [st_spec, st_spec, pl.BlockSpec((1, SUBLANES, RW_WIDTH), lambda n: (n, 0, 0))],
        out_specs=[row_spec, row_spec, st_spec],
        out_shape=[row_shape, row_shape, jax.ShapeDtypeStruct((nC, H, N, N), F32)],
        scratch_shapes=[pltpu.VMEM((H, N, N), F32)],
        compiler_params=_params(("arbitrary",)),
    )(v, at, bt, kt, rt, a2v, w2v, tinv, w1, plast)


def _rwkv_chunk_bwd(r, lw, k, a, b, v, dy, s0, tinv, w1, a2, w2, sa):
    T = r.shape[0]
    nC = T // RW_CHUNK
    H, N = RW_HEADS, RW_HEAD_DIM

    def body(r_ref, lw_ref, k_ref, a_ref, b_ref, v_ref, dy_ref, s0_ref, ti_ref, w1_ref, a2_ref, w2_ref, sa_ref,
             dr_ref, dlw_ref, dk_ref, da_ref, db_ref, dv_ref, dS_ref):
        @pl.when(pl.program_id(0) == 0)
        def _():
            dS_ref[...] = jnp.zeros_like(dS_ref)

        incl, strict, _ = _chunk_masks()
        incl_f = incl.astype(F32)
        q = _chunk_rows(r_ref[...], lw_ref[...], k_ref[...], a_ref[...], b_ref[...], incl_f)
        At, Bt, Kt, Rt = q["At"], q["Bt"], q["Kt"], q["Rt"]
        A2, W1, W2 = (_heads(lambda h, ref=ref: ref[0, h]) for ref in (a2_ref, w1_ref, w2_ref))
        V, dY, Sa = v_ref[...], dy_ref[...], sa_ref[...]
        hd = _head
        p_last = q["e_pos"][RW_CHUNK - 1:RW_CHUNK, :]
        S0 = _heads(lambda h: s0_ref[0, h])
        G = _heads(lambda h: dS_ref[h] * hd(p_last, h))
        X = _heads(lambda h: S0[h] + _hdot(hd(Sa, h), hd(Bt, h), TN) + _hdot(hd(V, h), hd(Kt, h), TN))
        dc_last = jnp.concatenate(_heads(lambda h: jnp.sum(G[h] * X[h], axis=0, keepdims=True)), axis=1)
        dSa = _heads(lambda h: _hdot(hd(Bt, h), G[h], NT) + _hdot(W1[h], hd(dY, h), TN))
        dZ = _heads(lambda h: _hdot(ti_ref[0, h], dSa[h], TN))
        for h in range(H):
            dS_ref[h] = G[h] + _hdot(dZ[h], hd(At, h), TN) + _hdot(hd(dY, h), hd(Rt, h), TN)
        dA1 = _heads(lambda h: jnp.where(strict, _hdot(dZ[h], hd(Sa, h), NT), 0.0))
        dA2 = _heads(lambda h: jnp.where(strict, _hdot(dZ[h], hd(V, h), NT), 0.0))
        dW1 = _heads(lambda h: jnp.where(incl, _hdot(hd(dY, h), hd(Sa, h), NT), 0.0))
        dW2 = _heads(lambda h: jnp.where(incl, _hdot(hd(dY, h), hd(V, h), NT), 0.0))
        cat = lambda fn: jnp.concatenate(_heads(fn), axis=1)
        dV = cat(lambda h: _hdot(A2[h], dZ[h], TN) + _hdot(W2[h], hd(dY, h), TN) + _hdot(hd(Kt, h), G[h], NT))
        dAt = cat(lambda h: _hdot(dA1[h], hd(Bt, h), NN) + _hdot(dA2[h], hd(Kt, h), NN) + _hdot(dZ[h], S0[h], NN))
        dBt = cat(lambda h: _hdot(dA1[h], hd(At, h), TN) + _hdot(dW1[h], hd(Rt, h), TN) + _hdot(hd(Sa, h), G[h], NN))
        dKt = cat(lambda h: _hdot(dA2[h], hd(At, h), TN) + _hdot(dW2[h], hd(Rt, h), TN) + _hdot(hd(V, h), G[h], NN))
        dRt = cat(lambda h: _hdot(hd(dY, h), S0[h], NN) + _hdot(dW1[h], hd(Bt, h), NN) + _hdot(dW2[h], hd(Kt, h), NN))
        last_row = lax.broadcasted_iota(jnp.int32, (RW_CHUNK, RW_WIDTH), 0) == RW_CHUNK - 1
        dc_prev = dAt * At
        dc = dc_prev + dRt * Rt - dBt * Bt - dKt * Kt + jnp.where(last_row, dc_last, 0.0)
        dr_ref[...] = dRt * q["e_pos"]
        dlw_ref[...] = _hdot(incl_f, dc, TN) - dc_prev
        dk_ref[...] = dKt * q["e_neg"]
        da_ref[...] = dAt * q["e_prev"]
        db_ref[...] = dBt * q["e_neg"]
        dv_ref[...] = dV

    rev = lambda n: nC - 1 - n
    row_spec = pl.BlockSpec((RW_CHUNK, RW_WIDTH), lambda n: (rev(n), 0))
    st_spec = pl.BlockSpec((1, H, N, N), lambda n: (rev(n), 0, 0, 0))
    row_shape = jax.ShapeDtypeStruct((T, RW_WIDTH), F32)
    return pl.pallas_call(
        body, name="rwkv_chunk_bwd", grid=(nC,),
        in_specs=[row_spec] * 7 + [st_spec] * 5 + [row_spec], out_specs=[row_spec] * 6,
        out_shape=[row_shape] * 6, scratch_shapes=[pltpu.VMEM((H, N, N), F32)],
        compiler_params=_params(("arbitrary",)),
    )(r, lw, k, a, b, v, dy, s0, tinv, w1, a2, w2, sa)


def _alibi_slope(head):
    return float(np.float32(2.0 ** (-8.0 * (head + 1) / ATT_HEADS)))


ATT_SPAN = ATT_BLOCK * max(ATT_GROUP_DILATION)
ATT_PAIR_WIDTH = 2 * ATT_HEAD_DIM
ATT_SIDE_BY_SIDE = 8


def _pair_slope(g, hp, j):
    return jnp.where(hp == 0, _alibi_slope(4 * g + j), _alibi_slope(4 * g + 2 + j))


def _att_rows(mi, r, d):
    start = mi * ATT_BLOCK * d + r
    return pl.ds(start, ATT_BLOCK) if d == 1 else pl.ds(start, ATT_BLOCK, stride=d)


def _att_masks():
    qi = lax.broadcasted_iota(jnp.int32, (ATT_BLOCK, ATT_BLOCK), 0)
    kj = lax.broadcasted_iota(jnp.int32, (ATT_BLOCK, ATT_BLOCK), 1)
    return qi, kj


NEG = -1e30


def _att_logits(q, k, slope_d, steps, valid):
    s = lax.dot_general(q.astype(BF16), k.astype(BF16), (((1,), (1,)), ((), ())),
                        preferred_element_type=F32) * (ATT_HEAD_DIM ** -0.5)
    return jnp.where(valid, s - slope_d * steps.astype(F32), NEG)


def _att_fwd(p_att, g):
    T = p_att.shape[0]
    d = ATT_GROUP_DILATION[g]
    W = ATT_PAIR_WIDTH
    nb = T // ATT_SPAN
    mb = ATT_SPAN // (ATT_BLOCK * d)

    def body(q_ref, kc_ref, kp_ref, vc_ref, vp_ref, o_ref, l_ref):
        hp, n = pl.program_id(0), pl.program_id(1)
        qi, kj = _att_masks()
        slopes = [_pair_slope(g, hp, j) * d for j in range(2)]
        blocks = [(r, mi) for r in range(d) for mi in range(mb)]
        for at in range(0, len(blocks), ATT_SIDE_BY_SIDE):
            tasks = []
            for r, mi in blocks[at:at + ATT_SIDE_BY_SIDE]:
                rows = _att_rows(mi, r, d)
                if mi > 0:
                    prev = _att_rows(mi - 1, r, d)
                    kp, vp, has_prev = kc_ref[prev, :], vc_ref[prev, :], True
                else:
                    prev = _att_rows(mb - 1, r, d)
                    kp, vp, has_prev = kp_ref[prev, :], vp_ref[prev, :], n > 0
                q, kc, vc = q_ref[rows, :], kc_ref[rows, :], vc_ref[rows, :]
                for j in range(2):
                    sl = slice(j * ATT_HEAD_DIM, (j + 1) * ATT_HEAD_DIM)
                    tasks.append((q[:, sl], kc[:, sl], kp[:, sl], vc[:, sl], vp[:, sl], has_prev, slopes[j]))
            lc = [_att_logits(t[0], t[1], t[6], qi - kj, kj <= qi) for t in tasks]
            lp = [_att_logits(t[0], t[2], t[6], qi - kj + ATT_BLOCK, (kj >= qi) & t[5]) for t in tasks]
            mx = [jnp.maximum(jnp.max(a, axis=1, keepdims=True), jnp.max(b, axis=1, keepdims=True))
                  for a, b in zip(lc, lp, strict=True)]
            ec = [jnp.exp(a - m) for a, m in zip(lc, mx, strict=True)]
            ep = [jnp.exp(b - m) for b, m in zip(lp, mx, strict=True)]
            den = [jnp.sum(a, axis=1, keepdims=True) + jnp.sum(b, axis=1, keepdims=True)
                   for a, b in zip(ec, ep, strict=True)]
            inv = [1.0 / s for s in den]
            outs = [jnp.dot((a * i).astype(BF16), t[3].astype(BF16), preferred_element_type=F32)
                    + jnp.dot((b * i).astype(BF16), t[4].astype(BF16), preferred_element_type=F32)
                    for a, b, i, t in zip(ec, ep, inv, tasks, strict=True)]
            lses = [jnp.broadcast_to(m + jnp.log(s), (ATT_BLOCK, ATT_HEAD_DIM)) for m, s in zip(mx, den, strict=True)]
            for i, (r, mi) in enumerate(blocks[at:at + ATT_SIDE_BY_SIDE]):
                rows = _att_rows(mi, r, d)
                o_ref[rows, :] = jnp.concatenate(outs[2 * i:2 * i + 2], axis=1)
                l_ref[rows, :] = jnp.concatenate(lses[2 * i:2 * i + 2], axis=1)

    def spec(col0, prev):
        if prev:
            return pl.BlockSpec((ATT_SPAN, W), lambda hp, n: (jnp.maximum(n - 1, 0), col0 + 2 * g + hp))
        return pl.BlockSpec((ATT_SPAN, W), lambda hp, n: (n, col0 + 2 * g + hp))

    o_spec = pl.BlockSpec((ATT_SPAN, W), lambda hp, n: (n, hp))
    o, l = pl.pallas_call(
        body, name=f"att_fwd_g{g}", grid=(2, nb),
        in_specs=[spec(0, False), spec(6, False), spec(6, True), spec(12, False), spec(12, True)],
        out_specs=[o_spec, o_spec],
        out_shape=[jax.ShapeDtypeStruct((T, ATT_GROUP_WIDTH), F32)] * 2,
        compiler_params=_params(("parallel", "arbitrary")),
    )(p_att, p_att, p_att, p_att, p_att)
    return o, l


def _att_bwd(p_att, o, l, do, dl, g):
    T = p_att.shape[0]
    d = ATT_GROUP_DILATION[g]
    W = ATT_PAIR_WIDTH
    nb = T // ATT_SPAN
    mb = ATT_SPAN // (ATT_BLOCK * d)
    scale = ATT_HEAD_DIM ** -0.5

    def body(q_ref, k_ref, v_ref, o_ref, l_ref, do_ref, dl_ref,
             qn_ref, on_ref, ln_ref, don_ref, dln_ref, dq_ref, dk_ref, dv_ref, carry_ref):
        hp, n = pl.program_id(0), pl.program_id(1)
        qi, kj = _att_masks()

        @pl.when(n == 0)
        def _():
            carry_ref[...] = jnp.zeros_like(carry_ref)

        slopes = [_pair_slope(g, hp, j) * d for j in range(2)]
        blocks = [(r, mi) for r in range(d) for mi in range(mb)]
        side_by_side = ATT_SIDE_BY_SIDE // 2
        carry = None
        for at in range(0, len(blocks), side_by_side):
            tasks = []
            for r, mi in blocks[at:at + side_by_side]:
                rows = _att_rows(mi, r, d)
                if mi < mb - 1:
                    nrows = _att_rows(mi + 1, r, d)
                    nxt = (q_ref[nrows, :], o_ref[nrows, :], l_ref[nrows, :], do_ref[nrows, :], dl_ref[nrows, :])
                    has_next = True
                else:
                    nrows = _att_rows(0, r, d)
                    nxt = (qn_ref[nrows, :], on_ref[nrows, :], ln_ref[nrows, :], don_ref[nrows, :],
                           dln_ref[nrows, :])
                    has_next = n < nb - 1
                cur = (q_ref[rows, :], o_ref[rows, :], l_ref[rows, :], do_ref[rows, :], dl_ref[rows, :])
                k_all, v_all = k_ref[rows, :], v_ref[rows, :]
                for j in range(2):
                    sl = slice(j * ATT_HEAD_DIM, (j + 1) * ATT_HEAD_DIM)
                    for blk, steps, valid in ((cur, qi - kj, kj <= qi),
                                              (nxt, qi - kj + ATT_BLOCK, (kj >= qi) & has_next)):
                        q, o_, lse, do_, dlse = (z[:, sl] for z in blk)
                        tasks.append(dict(q=q, o=o_, lse=lse[:, :1], do=do_, dlse=dlse[:, :1], steps=steps,
                                          valid=valid, k=k_all[:, sl], vb=v_all[:, sl].astype(BF16),
                                          slope=slopes[j]))
            p = [jnp.exp(_att_logits(t["q"], t["k"], t["slope"], t["steps"], t["valid"]) - t["lse"]) for t in tasks]
            dp = [lax.dot_general(t["do"].astype(BF16), t["vb"], (((1,), (1,)), ((), ())),
                                  preferred_element_type=F32) for t in tasks]
            dsum = [jnp.sum(t["do"] * t["o"], axis=1, keepdims=True) for t in tasks]
            ds = [a * (b - s + t["dlse"]) for a, b, s, t in zip(p, dp, dsum, tasks, strict=True)]
            dv_ = [jnp.dot(a.T.astype(BF16), t["do"].astype(BF16), preferred_element_type=F32)
                   for a, t in zip(p, tasks, strict=True)]
            dk_ = [jnp.dot(a.T.astype(BF16), t["q"].astype(BF16), preferred_element_type=F32) * scale
                   for a, t in zip(ds, tasks, strict=True)]
            dq_ = [jnp.dot(a.astype(BF16), t["k"].astype(BF16), preferred_element_type=F32) * scale
                   for a, t in zip(ds, tasks, strict=True)]
            for i, (r, mi) in enumerate(blocks[at:at + side_by_side]):
                rows = _att_rows(mi, r, d)
                b = 4 * i
                if mi == 0:
                    carry = carry_ref[r]
                dq_ref[rows, :] = jnp.concatenate([dq_[b], dq_[b + 2]], axis=1) + carry
                carry = jnp.concatenate([dq_[b + 1], dq_[b + 3]], axis=1)
                if mi == mb - 1:
                    carry_ref[r] = carry
                dk_ref[rows, :] = jnp.concatenate([dk_[b] + dk_[b + 1], dk_[b + 2] + dk_[b + 3]], axis=1)
                dv_ref[rows, :] = jnp.concatenate([dv_[b] + dv_[b + 1], dv_[b + 2] + dv_[b + 3]], axis=1)

    head_rows = ATT_BLOCK * d
    nxt_n = lambda n: jnp.minimum((n + 1) * mb, T // head_rows - 1)
    cur_p = lambda col0: pl.BlockSpec((ATT_SPAN, W), lambda hp, n: (n, col0 + 2 * g + hp))
    cur_o = pl.BlockSpec((ATT_SPAN, W), lambda hp, n: (n, hp))
    nxt_o = pl.BlockSpec((head_rows, W), lambda hp, n: (nxt_n(n), hp))
    dq, dk, dv = pl.pallas_call(
        body, name=f"att_bwd_g{g}", grid=(2, nb),
        in_specs=[cur_p(0), cur_p(6), cur_p(12), cur_o, cur_o, cur_o, cur_o,
                  pl.BlockSpec((head_rows, W), lambda hp, n: (nxt_n(n), 2 * g + hp)), nxt_o, nxt_o, nxt_o, nxt_o],
        out_specs=[cur_o, cur_o, cur_o],
        out_shape=[jax.ShapeDtypeStruct((T, ATT_GROUP_WIDTH), F32)] * 3,
        scratch_shapes=[pltpu.VMEM((d, ATT_BLOCK, W), F32)],
        compiler_params=_params(("parallel", "arbitrary")),
    )(p_att, p_att, p_att, o, l, do, dl, p_att, o, l, do, dl)
    return dq, dk, dv


RKV = 3 * RW_WIDTH
WA = 128
XG = 160
RW_COLS = RKV + WA + XG


def _local_step(x, p, W, target, late_weights=None, early_grads=None):
    T = x.shape[0]
    tT = 256
    bd512 = _block_diag_ones(RW_WIDTH, RW_HEAD_DIM)
    bd256 = _block_diag_ones(ATT_GROUP_WIDTH, ATT_HEAD_DIM)
    G = {}
    W = dict(W)

    w_in = W["w_in"]
    w_rkv, w_wa, w_xg, w_att = (w_in[:, :RKV], w_in[:, RKV:RKV + WA], w_in[:, RKV + WA:RW_COLS],
                                w_in[:, RW_COLS:])
    mu = W["rw_mu"]
    mu_rkv, mu_wa, mu_xg = mu[:, :RKV], mu[:, RKV:RKV + WA], mu[:, RKV + WA:]
    zpad = jnp.zeros((64, RW_WIDTH), W["rw_w_up"].dtype)
    w_up_pad = jnp.concatenate([W["rw_w_up"], zpad], axis=0)
    a_up_pad = jnp.concatenate([zpad, W["rw_a_up"]], axis=0)
    r_k = W["rw_r_k"].reshape(1, RW_WIDTH)

    (h,) = _rowwise("norm_mix", lambda i, n, r, pv, nx, c: [_rms_fwd(r[0], c[0])], T, tT,
                    rows=[x], consts=[W["g_mix"]], outs=[("row", D_MODEL, BF16)])
    p_rkv = _mm("proj_rkv", h, w_rkv, "nn")
    p_wa = _mm("proj_wa", h, w_wa, "nn")
    p_xg = _mm("proj_xg", h, w_xg, "nn")
    p_att = _mm("proj_att", h, w_att, "nn", tn=768)
    z_gate = _mm("proj_gate", h, W["w_gate"], "nn")

    def rw_pre_core(i, rows, prevs, consts):
        prkv, pwa, pxg = rows[:3]
        (mrkv, mwa, mxg, w0, a0, k_k, k_a, wup, aup, gup, bd) = consts[:11]
        m_rkv = prkv + (_shift_down(prkv, prevs[0], i, 1) - prkv) * mrkv
        m_wa = pwa + (_shift_down(pwa, prevs[1], i, 1) - pwa) * mwa
        m_xg = pxg + (_shift_down(pxg, prevs[2], i, 1) - pxg) * mxg
        r, k, v = m_rkv[:, :RW_WIDTH], m_rkv[:, RW_WIDTH:2 * RW_WIDTH], m_rkv[:, 2 * RW_WIDTH:]
        tw = jnp.tanh(m_wa)
        lw = w0 + jnp.dot(tw.astype(BF16), wup.astype(BF16), preferred_element_type=F32)
        wlog = -_softplus(-lw) - 0.5
        log_decay = -jnp.exp(wlog)
        a = _sigmoid(a0 + jnp.dot(m_wa.astype(BF16), aup.astype(BF16), preferred_element_type=F32))
        sg = _sigmoid(m_xg)
        gate = jnp.dot(sg.astype(BF16), gup.astype(BF16), preferred_element_type=F32)
        kkp = k * k_k
        nrm = jnp.sqrt(_segsum(kkp * kkp, bd))
        nrm_c = jnp.maximum(nrm, 1e-12)
        kk = kkp / nrm_c
        k2 = k * (1.0 + (a - 1.0) * k_a)
        return dict(r=r, k=k, v=v, tw=tw, lw=lw, wlog=wlog, log_decay=log_decay, a=a, sg=sg, gate=gate, kkp=kkp,
                    nrm=nrm, nrm_c=nrm_c, kk=kk, k2=k2, m_rkv=m_rkv, m_wa=m_wa, m_xg=m_xg)

    pre_consts = [mu_rkv, mu_wa, mu_xg, W["rw_w0"], W["rw_a0"], W["rw_k_k"], W["rw_k_a"],
                  w_up_pad, a_up_pad, W["rw_g_up"], bd512]

    def rw_pre(i, n, rows, prevs, nexts, consts):
        q = rw_pre_core(i, rows, prevs, consts)
        return [q["r"], q["log_decay"], q["k2"], q["v"], -q["kk"], q["kk"] * q["a"], q["gate"]]

    r_s, w_s, k_s, v_s, a_s, b_s, gate_s = _rowwise(
        "rwkv_pre", rw_pre, T, tT, rows=[p_rkv, p_wa, p_xg], prevs=[p_rkv, p_wa, p_xg], consts=pre_consts,
        outs=[("row", RW_WIDTH, F32)] * 7)
    (at_s, bt_s, kt_s, rt_s, a2v_s, w2v_s, tinv_s, w1_s, a2_s, w2_s,
     plast_s) = _rwkv_chunk_prep(r_s, w_s, k_s, a_s, b_s, v_s)
    y_scan, sa_s, s0_s = _rwkv_chunk_fwd(v_s, at_s, bt_s, kt_s, rt_s, a2v_s, w2v_s, tinv_s, w1_s, plast_s)

    def rw_post_core(rows, consts):
        y, r, k2, v, gate = rows[:5]
        ln_g, ln_b, rk, bd = consts[:4]
        mean = _segsum(y, bd) * (1.0 / RW_HEAD_DIM)
        yc = y - mean
        var = _segsum(yc * yc, bd) * (1.0 / RW_HEAD_DIM)
        rstd = lax.rsqrt(var + RW_LN_EPS)
        yn = yc * rstd
        s = _segsum(r * k2 * rk, bd)
        return dict(yn=yn, rstd=rstd, s=s, pre=yn * ln_g + ln_b + s * v)

    post_consts = [W["rw_ln_g"], W["rw_ln_b"], r_k, bd512]
    (y_a,) = _rowwise("rwkv_post", lambda i, n, r, pv, nx, c: [rw_post_core(r, c)["pre"] * r[4]], T, tT,
                      rows=[y_scan, r_s, k_s, v_s, gate_s], consts=post_consts, outs=[("row", RW_WIDTH, BF16)])

    att = [_att_fwd(p_att, g) for g in range(3)]

    def comb_weights(ls):
        mx = jnp.maximum(jnp.maximum(ls[0], ls[1]), ls[2])
        es = [jnp.exp(l - mx) for l in ls]
        den = es[0] + es[1] + es[2]
        return [e / den for e in es]

    def att_comb(i, n, rows, pv, nx, c):
        wts = comb_weights(rows[3:6])
        return [wts[0] * rows[0] + wts[1] * rows[1] + wts[2] * rows[2]]

    (y_b,) = _rowwise("att_combine", att_comb, T, tT, rows=[att[0][0], att[1][0], att[2][0], att[0][1], att[1][1],
                                                            att[2][1]], outs=[("row", ATT_GROUP_WIDTH, BF16)])

    if late_weights is not None:
        W.update(late_weights(y_b))
    br_a = _mm("branch_a", y_a, W["w_branch_a"], "nn")
    br_b = _mm("branch_b", y_b, W["w_branch_b"], "nn")

    def merge(i, n, rows, pv, nx, c):
        gates = _sigmoid(rows[0] + c[0])
        return [gates[:, :D_MODEL] * rows[1] + gates[:, D_MODEL:] * rows[2]]

    (merged,) = _rowwise("merge", merge, T, tT, rows=[z_gate, br_a, br_b], consts=[W["b_gate"]],
                         outs=[("row", D_MODEL, BF16)])
    x1 = _mm("mix_out", merged, W["w_out"], "nn", add=x)

    (h2,) = _rowwise("norm_ffn", lambda i, n, r, pv, nx, c: [_rms_fwd(r[0], c[0])], T, tT,
                     rows=[x1], consts=[W["g_ffn"]], outs=[("row", D_MODEL, BF16)])
    u = _mm("ffn_up", h2, W["w_up"], "nn")

    def conv_core(i, rows, prevs, consts):
        uu, cw, cb = rows[0], consts[0], consts[1]
        u1 = _shift_down(uu, prevs[0], i, 1)
        u2 = _shift_down(uu, prevs[0], i, 2)
        uc = cb + cw[0:1] * uu + cw[1:2] * u1 + cw[2:3] * u2
        return uc[:, :D_FF], uc[:, D_FF:], u1, u2

    def glu(i, n, rows, prevs, nx, consts):
        gate, val, _, _ = conv_core(i, rows, prevs, consts)
        return [_gelu(gate) * val]

    tF = 128
    (act,) = _rowwise("conv_glu", glu, T, tF, rows=[u], prevs=[u], consts=[W["conv_w"], W["conv_b"]],
                      outs=[("row", D_FF, BF16)])
    x2 = _mm("ffn_down", act, W["w_down"], "nn", add=x1)

    (h3,) = _rowwise("norm_ple", lambda i, n, r, pv, nx, c: [_rms_fwd(r[0], c[0])], T, tT,
                     rows=[x2], consts=[W["g_ple"]], outs=[("row", D_MODEL, BF16)])
    z_ple = _mm("ple_gate", h3, W["w_ple_gate"], "nn")
    e_ple = _mm("ple_emb", p, W["w_ple"], "nn")

    def head(i, n, rows, pv, nx, consts):
        x2_, z, e, tgt = rows
        pg = _sigmoid(z)
        x3 = x2_ + pg * e
        y = _rms_fwd(x3, consts[0])
        err = y - tgt
        loss = 0.5 * jnp.sum(jnp.sum(err * err, axis=1, keepdims=True) * (1.0 / D_MODEL), axis=0, keepdims=True)
        dy = err * (1.0 / D_MODEL)
        dx3, dgf = _rms_bwd(x3, consts[0], dy)
        return [dx3, dx3 * pg, dx3 * e * pg * (1.0 - pg), jnp.broadcast_to(loss, (1, LANES)), _colsum(dgf)]

    dx3, de, dz, loss_acc, G["g_final"] = _rowwise(
        "loss_head", head, T, tT, rows=[x2, z_ple, e_ple, target], consts=[W["g_final"].reshape(1, D_MODEL)],
        outs=[("row", D_MODEL, F32), ("row", D_MODEL, BF16), ("row", D_MODEL, BF16), ("acc", (1, LANES)),
              ("acc", (1, D_MODEL))])
    G["w_ple"] = _mm("d_w_ple", p, de, "tn")
    G["w_ple_gate"] = _mm("d_w_ple_gate", h3, dz, "tn")
    dh3 = _mm("d_h3", dz, W["w_ple_gate"], "nt")

    def norm_bwd(i, n, rows, pv, nx, consts):
        dx, dg = _rms_bwd(rows[0], consts[0], rows[1])
        return [rows[2] + dx, _colsum(dg)]

    dx2, G["g_ple"] = _rowwise("d_norm_ple", norm_bwd, T, tT, rows=[x2, dh3, dx3], consts=[W["g_ple"]],
                               outs=[("row", D_MODEL, F32), ("acc", (1, D_MODEL))])

    dact = _mm("d_act", dx2, W["w_down"], "nt")
    G["w_down"] = _mm("d_w_down", act, dx2, "tn")

    def glu_grad(gate, val, da):
        act_, slope = _gelu_and_grad(gate)
        return jnp.concatenate([da * val * slope, da * act_], axis=1)

    def glu_bwd(i, n, rows, prevs, nexts, consts):
        uu, da = rows
        cw = consts[0]
        gate, val, u1, u2 = conv_core(i, rows, prevs, consts)
        duc = glu_grad(gate, val, da)
        dcw = jnp.concatenate([_colsum(duc * uu), _colsum(duc * u1), _colsum(duc * u2)], axis=0)
        gate_n, val_n, _, _ = conv_core(1, [nexts[0]], [uu[tF - SUBLANES:]], consts)
        duc_n = glu_grad(gate_n, val_n, nexts[1])
        du = (cw[0:1] * duc + cw[1:2] * _shift_up(duc, duc_n, i, n, 1) + cw[2:3] * _shift_up(duc, duc_n, i, n, 2))
        return [du, _colsum(duc), dcw]

    du, G["conv_b"], G["conv_w"] = _rowwise(
        "d_conv_glu", glu_bwd, T, tF, rows=[u, dact], prevs=[u], nexts=[u, dact],
        consts=[W["conv_w"], W["conv_b"]],
        outs=[("row", 2 * D_FF, BF16), ("acc", (1, 2 * D_FF)), ("acc", (3, 2 * D_FF))])
    G["w_up"] = _mm("d_w_up", h2, du, "tn")
    dh2 = _mm("d_h2", du, W["w_up"], "nt")
    dx1, G["g_ffn"] = _rowwise("d_norm_ffn", norm_bwd, T, tT, rows=[x1, dh2, dx2], consts=[W["g_ffn"]],
                               outs=[("row", D_MODEL, F32), ("acc", (1, D_MODEL))])

    b_gate = W["b_gate"]
    if early_grads is not None:
        b_gate = b_gate + early_grads(G)[0:1, 0:1]
    dmerged = _mm("d_merged", dx1, W["w_out"], "nt")
    G["w_out"] = _mm("d_w_out", merged, dx1, "tn")

    def merge_bwd(i, n, rows, pv, nx, consts):
        z, a_, b_, dm = rows
        gates = _sigmoid(z + consts[0])
        ga, gb = gates[:, :D_MODEL], gates[:, D_MODEL:]
        dz_ = jnp.concatenate([dm * a_ * ga * (1.0 - ga), dm * b_ * gb * (1.0 - gb)], axis=1)
        return [dm * ga, dm * gb, dz_, _colsum(dz_)]

    d_br_a, d_br_b, dz_gate, G["b_gate"] = _rowwise(
        "d_merge", merge_bwd, T, tT, rows=[z_gate, br_a, br_b, dmerged], consts=[b_gate],
        outs=[("row", D_MODEL, BF16), ("row", D_MODEL, BF16), ("row", 2 * D_MODEL, BF16), ("acc", (1, 2 * D_MODEL))])
    G["w_branch_a"] = _mm("d_w_branch_a", y_a, d_br_a, "tn")
    G["w_branch_b"] = _mm("d_w_branch_b", y_b, d_br_b, "tn")
    G["w_gate"] = _mm("d_w_gate", h, dz_gate, "tn")
    dy_a = _mm("d_y_a", d_br_a, W["w_branch_a"], "nt")
    dy_b = _mm("d_y_b", d_br_b, W["w_branch_b"], "nt")

    def att_comb_bwd(i, n, rows, pv, nx, consts):
        os_, ls, dy = rows[0:3], rows[3:6], rows[6]
        wts = comb_weights(ls)
        dws = [_segsum(dy * o_, consts[0]) for o_ in os_]
        mix = wts[0] * dws[0] + wts[1] * dws[1] + wts[2] * dws[2]
        return [wts[g_] * dy for g_ in range(3)] + [wts[g_] * (dws[g_] - mix) for g_ in range(3)]

    comb = _rowwise("d_att_combine", att_comb_bwd, T, tT,
                    rows=[att[0][0], att[1][0], att[2][0], att[0][1], att[1][1], att[2][1], dy_b], consts=[bd256],
                    outs=[("row", ATT_GROUP_WIDTH, F32)] * 6)
    dqkv = [_att_bwd(p_att, att[g][0], att[g][1], comb[g], comb[3 + g], g) for g in range(3)]
    dp_att = jnp.concatenate([dqkv[g][part] for part in range(3) for g in range(3)], axis=1).astype(BF16)

    def rw_post_bwd(i, n, rows, pv, nx, consts):
        y, r, k2, v, gate, dya = rows
        ln_g, ln_b, rk, bd = consts
        q = rw_post_core(rows, consts)
        dpre = dya * gate
        dgate = dya * q["pre"]
        dyn = dpre * ln_g
        inv = 1.0 / RW_HEAD_DIM
        dy_scan = q["rstd"] * (dyn - _segsum(dyn, bd) * inv - q["yn"] * (_segsum(dyn * q["yn"], bd) * inv))
        ds = _segsum(dpre * v, bd)
        return [dy_scan, dgate, ds * k2 * rk, ds * r * rk, dpre * q["s"],
                _colsum(dpre * q["yn"]), _colsum(dpre), _colsum(ds * r * k2)]

    dy_scan, dgate, dr_b, dk2_b, dv_b, G["rw_ln_g"], G["rw_ln_b"], d_rk = _rowwise(
        "d_rwkv_post", rw_post_bwd, T, tT, rows=[y_scan, r_s, k_s, v_s, gate_s, dy_a], consts=post_consts,
        outs=[("row", RW_WIDTH, F32)] * 5 + [("acc", (1, RW_WIDTH))] * 3)
    G["rw_r_k"] = d_rk.reshape(RW_HEADS, RW_HEAD_DIM)

    dr_s, dw_s, dk_s, da_s, db_s, dv_s = _rwkv_chunk_bwd(r_s, w_s, k_s, a_s, b_s, v_s, dy_scan, s0_s, tinv_s, w1_s,
                                                         a2_s, w2_s, sa_s)

    def rw_pre_bwd(i, n, rows, prevs, nx, consts):
        q = rw_pre_core(i, rows, prevs, consts)
        (mrkv, mwa, mxg, w0, a0, k_k, k_a, wup, aup, gup, bd) = consts
        dr, dlogdecay, dk2, dv, dav, dbv, dgate_ = rows[3:10]
        dr = dr + rows[10]
        dk2 = dk2 + rows[11]
        dv = dv + rows[12]
        a, k, kk = q["a"], q["k"], q["kk"]
        dk = dk2 * (1.0 + (a - 1.0) * k_a)
        da = dk2 * k * k_a + dbv * kk
        dkk = dbv * a - dav
        live = q["nrm"] > 1e-12
        dkkp = jnp.where(live, dkk - kk * _segsum(dkk * kk, bd), dkk) / q["nrm_c"]
        dk = dk + dkkp * k_k
        dlw = dlogdecay * q["log_decay"] * _sigmoid(-q["lw"])
        dla = da * a * (1.0 - a)
        nt = (((1,), (1,)), ((), ()))
        dtw = lax.dot_general(dlw.astype(BF16), wup.astype(BF16), nt, preferred_element_type=F32)
        dxa = lax.dot_general(dla.astype(BF16), aup.astype(BF16), nt, preferred_element_type=F32)
        dm_wa = dtw * (1.0 - q["tw"] * q["tw"]) + dxa
        dsg = lax.dot_general(dgate_.astype(BF16), gup.astype(BF16), nt, preferred_element_type=F32)
        dm_xg = dsg * q["sg"] * (1.0 - q["sg"])
        dm_rkv = jnp.concatenate([dr, dk, dv], axis=1)
        prkv, pwa, pxg = rows[:3]
        dmu = jnp.concatenate([_colsum(dm_rkv * (_shift_down(prkv, prevs[0], i, 1) - prkv)),
                               _colsum(dm_wa * (_shift_down(pwa, prevs[1], i, 1) - pwa)),
                               _colsum(dm_xg * (_shift_down(pxg, prevs[2], i, 1) - pxg))], axis=1)
        return [dm_rkv, dm_wa, dm_xg, dlw, dla, q["tw"], q["m_wa"], q["sg"], dmu,
                _colsum(dlw), _colsum(dla), _colsum(dkkp * k), _colsum(dk2 * k * (a - 1.0))]

    (dm_rkv, dm_wa, dm_xg, dlw, dla, tw_s, mwa_s, sg_s, G["rw_mu"], G["rw_w0"], G["rw_a0"], G["rw_k_k"],
     G["rw_k_a"]) = _rowwise(
        "d_rwkv_pre", rw_pre_bwd, T, tT,
        rows=[p_rkv, p_wa, p_xg, dr_s, dw_s, dk_s, dv_s, da_s, db_s, dgate, dr_b, dk2_b, dv_b],
        prevs=[p_rkv, p_wa, p_xg], consts=pre_consts,
        outs=[("row", RKV, F32), ("row", WA, F32), ("row", XG, F32), ("row", RW_WIDTH, BF16),
              ("row", RW_WIDTH, BF16), ("row", WA, BF16), ("row", WA, BF16), ("row", XG, BF16),
              ("acc", (1, RW_COLS))] + [("acc", (1, RW_WIDTH))] * 4)
    G["rw_w_up"] = _mm("d_rw_w_up", tw_s, dlw, "tn")[:64]
    G["rw_a_up"] = _mm("d_rw_a_up", mwa_s, dla, "tn")[64:]
    G["rw_g_up"] = _mm("d_rw_g_up", sg_s, dgate, "tn")

    def shift_bwd(i, n, rows, pv, nexts, consts):
        return [rows[j] * (1.0 - consts[j]) + _shift_up(rows[j], nexts[j], i, n, 1) * consts[j] for j in range(3)]

    dp_rkv, dp_wa, dp_xg = _rowwise(
        "d_token_shift", shift_bwd, T, tT, rows=[dm_rkv, dm_wa, dm_xg], nexts=[dm_rkv, dm_wa, dm_xg],
        consts=[mu_rkv, mu_wa, mu_xg], outs=[("row", RKV, BF16), ("row", WA, BF16), ("row", XG, BF16)])

    G["w_in"] = jnp.concatenate([_mm("d_w_rkv", h, dp_rkv, "tn"), _mm("d_w_wa", h, dp_wa, "tn"),
                                 _mm("d_w_xg", h, dp_xg, "tn"), _mm("d_w_att", h, dp_att, "tn", tn=768)], axis=1)
    dh = _mm("d_h_gate", dz_gate, W["w_gate"], "nt")
    dh = _mm("d_h_rkv", dp_rkv, w_rkv, "nt", add=dh)
    dh = _mm("d_h_wa", dp_wa, w_wa, "nt", add=dh)
    dh = _mm("d_h_xg", dp_xg, w_xg, "nt", add=dh)
    dh = _mm("d_h_att", dp_att, w_att, "nt", add=dh)
    dx, G["g_mix"] = _rowwise("d_norm_mix", norm_bwd, T, tT, rows=[x, dh, dx1], consts=[W["g_mix"]],
                              outs=[("row", D_MODEL, F32), ("acc", (1, D_MODEL))])
    return loss_acc[:, :1], dx, G


HBM_SPEC = pl.BlockSpec(memory_space=pltpu.HBM)


def _place():
    x, y, c = lax.axis_index("x"), lax.axis_index("y"), lax.axis_index("c")
    return x, y, c, [(1 - x, y), (x, 1 - y), (1 - x, 1 - y)]


def _remote(src, dst, send_sems, recv_sems, k, to):
    return pltpu.make_async_remote_copy(src_ref=src, dst_ref=dst, send_sem=send_sems.at[k], recv_sem=recv_sems.at[k],
                                        device_id=to, device_id_type=MESH)


ROW_ALIGN = 16


def _half_rows(ref_rows, c, first):
    half = ref_rows // 2
    which = c if first else 1 - c
    return pl.ds(pl.multiple_of(which * half, ROW_ALIGN), half)


def _gather_chips(shards):
    n = len(shards)
    split = [s.shape[0] % (2 * ROW_ALIGN) == 0 for s in shards]

    def body(*refs):
        w_refs, out_refs = refs[:n], refs[n:2 * n]
        send_sems, recv_sems = refs[2 * n:]
        x, y, c, chips = _place()
        me = 2 * x + y
        sends, passed = [], []
        for i in range(n):
            for j, (px, py) in enumerate(chips):
                if split[i]:
                    mine = _half_rows(w_refs[i].shape[0], c, True)
                    cp = _remote(w_refs[i].at[mine], out_refs[i].at[me, mine], send_sems, recv_sems, 6 * i + j,
                                 (px, py, c))
                else:
                    cp = _remote(w_refs[i], out_refs[i].at[me], send_sems, recv_sems, 6 * i + j, (px, py, c))
                cp.start()
                sends.append(cp)
        for i in range(n):
            for j, (px, py) in enumerate(chips):
                if split[i]:
                    landed = out_refs[i].at[2 * px + py, _half_rows(w_refs[i].shape[0], c, True)]
                    _remote(landed, landed, send_sems, recv_sems, 6 * i + j, (px, py, c)).wait_recv()
                    cp = _remote(landed, landed, send_sems, recv_sems, 6 * i + 3 + j, (x, y, 1 - c))
                    cp.start()
                    passed.append(cp)
                else:
                    landed = out_refs[i].at[2 * px + py]
                    _remote(landed, landed, send_sems, recv_sems, 6 * i + j, (px, py, c)).wait_recv()
        for i in range(n):
            if split[i]:
                for j, (px, py) in enumerate(chips):
                    landed = out_refs[i].at[2 * px + py, _half_rows(w_refs[i].shape[0], c, False)]
                    _remote(landed, landed, send_sems, recv_sems, 6 * i + 3 + j, (x, y, 1 - c)).wait_recv()
        for cp in sends + passed:
            cp.wait_send()

    outs = pl.pallas_call(
        body, name="gather_weights", in_specs=[HBM_SPEC] * n, out_specs=[HBM_SPEC] * n,
        out_shape=[jax.ShapeDtypeStruct((N_CHIPS,) + s.shape, s.dtype) for s in shards],
        scratch_shapes=[pltpu.SemaphoreType.DMA((6 * n,)), pltpu.SemaphoreType.DMA((6 * n,))],
    )(*shards)
    me = 2 * lax.axis_index("x") + lax.axis_index("y")
    return [lax.dynamic_update_slice(o, s[None], (me, 0, 0)) for o, s in zip(outs, shards, strict=True)]


def _swap_halves(name, gs):
    n = len(gs)

    def body(*refs):
        g_refs, out_refs = refs[:n], refs[n:2 * n]
        send_sems, recv_sems = refs[2 * n:]
        x, y, c, _ = _place()
        cps = []
        for i in range(n):
            theirs = _half_rows(g_refs[i].shape[1], c, False)
            cp = _remote(g_refs[i].at[:, theirs, :], out_refs[i], send_sems, recv_sems, i, (x, y, 1 - c))
            cp.start()
            cps.append(cp)
        for cp in cps:
            cp.wait()

    return pl.pallas_call(
        body, name=name, in_specs=[HBM_SPEC] * n, out_specs=[HBM_SPEC] * n,
        out_shape=[jax.ShapeDtypeStruct((N_CHIPS, g.shape[1] // 2, g.shape[2]), g.dtype) for g in gs],
        scratch_shapes=[pltpu.SemaphoreType.DMA((n,)), pltpu.SemaphoreType.DMA((n,))],
    )(*gs)


def _scatter_chips(parts):
    n = len(parts)

    def body(*refs):
        p_refs, out_refs = refs[:n], refs[n:2 * n]
        send_sems, recv_sems = refs[2 * n:]
        x, y, c, chips = _place()
        me = 2 * x + y
        sends = []
        for i in range(n):
            for j, (px, py) in enumerate(chips):
                cp = _remote(p_refs[i].at[2 * px + py], out_refs[i].at[me], send_sems, recv_sems, 3 * i + j,
                             (px, py, c))
                cp.start()
                sends.append(cp)
        for i in range(n):
            for j, (px, py) in enumerate(chips):
                landed = out_refs[i].at[2 * px + py]
                _remote(landed, landed, send_sems, recv_sems, 3 * i + j, (px, py, c)).wait_recv()
        for cp in sends:
            cp.wait_send()

    outs = pl.pallas_call(
        body, name="scatter_grads", in_specs=[HBM_SPEC] * n, out_specs=[HBM_SPEC] * n,
        out_shape=[jax.ShapeDtypeStruct(p.shape, p.dtype) for p in parts],
        scratch_shapes=[pltpu.SemaphoreType.DMA((3 * n,)), pltpu.SemaphoreType.DMA((3 * n,))],
    )(*parts)
    me = 2 * lax.axis_index("x") + lax.axis_index("y")
    own = [lax.dynamic_slice_in_dim(p, me, 1, axis=0) for p in parts]
    return [lax.dynamic_update_slice(o, s, (me, 0, 0)) for o, s in zip(outs, own, strict=True)]


def _join_halves(reds):
    n = len(reds)

    def body(*refs):
        r_refs, out_refs = refs[:n], refs[n:2 * n]
        send_sems, recv_sems = refs[2 * n:]
        x, y, c, _ = _place()
        cps = []
        for i in range(n):
            mine = _half_rows(out_refs[i].shape[0], c, True)
            cp = _remote(r_refs[i], out_refs[i].at[mine], send_sems, recv_sems, i, (x, y, 1 - c))
            cp.start()
            cps.append(cp)
        for cp in cps:
            cp.wait()

    outs = pl.pallas_call(
        body, name="join_halves", in_specs=[HBM_SPEC] * n, out_specs=[HBM_SPEC] * n,
        out_shape=[jax.ShapeDtypeStruct((2 * r.shape[0], r.shape[1]), r.dtype) for r in reds],
        scratch_shapes=[pltpu.SemaphoreType.DMA((n,)), pltpu.SemaphoreType.DMA((n,))],
    )(*reds)
    c = lax.axis_index("c")
    return [lax.dynamic_update_slice(o, r, (c * r.shape[0], 0)) for o, r in zip(outs, reds, strict=True)]


def _gather_all(vec):
    R = vec.shape[0]

    def body(v_ref, out_ref, send_sems, recv_sems, local_sem):
        x, y, c, _ = _place()
        me = 4 * x + 2 * y + c
        local = pltpu.make_async_copy(v_ref, out_ref.at[me], local_sem)
        local.start()
        peers = [(x ^ (k >> 2), y ^ ((k >> 1) & 1), c ^ (k & 1)) for k in range(1, N_DEV)]
        sends = [_remote(v_ref, out_ref.at[me], send_sems, recv_sems, k, to) for k, to in enumerate(peers)]
        for cp in sends:
            cp.start()
        for k, (px, py, pc) in enumerate(peers):
            landed = out_ref.at[4 * px + 2 * py + pc]
            _remote(landed, landed, send_sems, recv_sems, k, (px, py, pc)).wait_recv()
        for cp in sends:
            cp.wait_send()
        local.wait()

    return pl.pallas_call(
        body, name="gather_small", in_specs=[HBM_SPEC], out_specs=HBM_SPEC,
        out_shape=jax.ShapeDtypeStruct((N_DEV, R, LANES), vec.dtype),
        scratch_shapes=[pltpu.SemaphoreType.DMA((7,)), pltpu.SemaphoreType.DMA((7,)), pltpu.SemaphoreType.DMA],
    )(vec)


SEM_SPEC = pl.BlockSpec(memory_space=pltpu.SEMAPHORE)
DATAFLOW = pltpu.SideEffectType.DATAFLOW_SIDE_EFFECTING


def _travel_copies(mode, src_refs, land_refs, send_sems, recv_sems):
    x, y, c, chips = _place()
    me = 2 * x + y
    pairs = []
    for i, (src, land) in enumerate(zip(src_refs, land_refs, strict=True)):
        for j, (px, py) in enumerate(chips):
            peer = 2 * px + py
            mine = src if mode == "gather" else src.at[peer]
            send = _remote(mine, land.at[me], send_sems, recv_sems, 3 * i + j, (px, py, c))
            arrival = _remote(mine, land.at[peer], send_sems, recv_sems, 3 * i + j, (px, py, c))
            pairs.append((send, arrival))
    return pairs


def _travel_start(name, mode, srcs):
    n = len(srcs)
    lands = [lax.empty((N_CHIPS,) + (s.shape if mode == "gather" else s.shape[1:]), s.dtype) for s in srcs]

    def body(*refs):
        src_refs, land_refs = refs[:n], refs[n:2 * n]
        send_sems, recv_sems = refs[2 * n], refs[2 * n + 1]
        token = refs[-1]
        for send, _ in _travel_copies(mode, src_refs, land_refs, send_sems, recv_sems):
            send.start()
        token[...] = jnp.zeros_like(token)

    hbm = lambda a: pltpu.HBM(a.shape, a.dtype)
    outs = pl.pallas_call(
        body, name=name,
        out_shape=(pltpu.SemaphoreType.DMA((3 * n,)), pltpu.SemaphoreType.DMA((3 * n,)), *[hbm(s) for s in srcs],
                   *[hbm(a) for a in lands], jax.ShapeDtypeStruct((SUBLANES, LANES), F32)),
        in_specs=[HBM_SPEC] * (2 * n),
        out_specs=(SEM_SPEC, SEM_SPEC, *[HBM_SPEC] * (2 * n), pl.BlockSpec(memory_space=pltpu.VMEM)),
        input_output_aliases={i: 2 + i for i in range(2 * n)},
        compiler_params=pltpu.CompilerParams(has_side_effects=DATAFLOW),
    )(*[pltpu.with_memory_space_constraint(a, pltpu.HBM) for a in list(srcs) + lands])
    return outs[0], outs[1], list(outs[2:2 + n]), list(outs[2 + n:2 + 2 * n]), outs[-1]


def _travel_wait(name, mode, send_sems, recv_sems, srcs, lands, after):
    n = len(srcs)

    def body(*refs):
        src_refs, land_refs = refs[:n], refs[n:2 * n]
        send_sems_, recv_sems_ = refs[2 * n], refs[2 * n + 1]
        for send, arrival in _travel_copies(mode, src_refs, land_refs, send_sems_, recv_sems_):
            send.wait_send()
            arrival.wait_recv()

    hbm = lambda a: pltpu.HBM(a.shape, a.dtype)
    outs = pl.pallas_call(
        body, name=name, out_shape=tuple(hbm(a) for a in list(srcs) + list(lands)),
        in_specs=[HBM_SPEC] * (2 * n) + [SEM_SPEC, SEM_SPEC, pl.BlockSpec(memory_space=pl.ANY)],
        out_specs=tuple([HBM_SPEC] * (2 * n)), input_output_aliases={i: i for i in range(2 * n)},
        compiler_params=pltpu.CompilerParams(has_side_effects=DATAFLOW),
    )(*srcs, *lands, send_sems, recv_sems, after)
    me = 2 * lax.axis_index("x") + lax.axis_index("y")
    own = [s[None] if mode == "gather" else lax.dynamic_slice_in_dim(s, me, 1, axis=0) for s in outs[:n]]
    return [lax.dynamic_update_slice(a, o, (me,) + (0,) * (a.ndim - 1)) for a, o in zip(outs[n:], own, strict=True)]


SUM_TILE_BYTES = 4 * 1024 * 1024


def _sum_rows(half, cols):
    best = ROW_ALIGN
    for t in range(ROW_ALIGN, half + 1, ROW_ALIGN):
        if half % t == 0 and N_CHIPS * t * cols * 4 <= SUM_TILE_BYTES:
            best = t
    return best


def _sum_cores(name, g, theirs, core):
    _, R, C = g.shape
    half = R // 2
    tr = _sum_rows(half, C)
    nb = half // tr

    def body(core_ref, g_ref, t_ref, o_ref):
        o_ref[...] = (g_ref[...] + t_ref[...]).astype(o_ref.dtype)

    grid_spec = pltpu.PrefetchScalarGridSpec(
        num_scalar_prefetch=1, grid=(nb,),
        in_specs=[pl.BlockSpec((N_CHIPS, tr, C), lambda i, core_ref: (0, core_ref[0] * nb + i, 0)),
                  pl.BlockSpec((N_CHIPS, tr, C), lambda i, core_ref: (0, i, 0))],
        out_specs=pl.BlockSpec((N_CHIPS, tr, C), lambda i, core_ref: (0, i, 0)))
    return pl.pallas_call(
        body, name=name, grid_spec=grid_spec, out_shape=jax.ShapeDtypeStruct((N_CHIPS, half, C), BF16),
        compiler_params=_params(("parallel",)),
    )(core, g, theirs)


def _sum_chips(name, parts):
    _, H, C = parts.shape
    tr = _sum_rows(H, C)

    def body(p_ref, o_ref):
        acc = p_ref[0].astype(F32)
        for k in range(1, N_CHIPS):
            acc = acc + p_ref[k].astype(F32)
        o_ref[...] = acc

    return pl.pallas_call(
        body, name=name, grid=(H // tr,),
        in_specs=[pl.BlockSpec((N_CHIPS, tr, C), lambda i: (0, i, 0))],
        out_specs=pl.BlockSpec((tr, C), lambda i: (i, 0)),
        out_shape=jax.ShapeDtypeStruct((H, C), F32),
        compiler_params=_params(("parallel",)),
    )(parts)


def _adamw_math(w, g, m, v):
    m = ADAM_B1 * m + (1.0 - ADAM_B1) * g
    v = ADAM_B2 * v + (1.0 - ADAM_B2) * (g * g)
    m_hat = m / (1.0 - ADAM_B1 ** ADAM_STEP)
    v_hat = v / (1.0 - ADAM_B2 ** ADAM_STEP)
    delta = -ADAM_LR * (m_hat / (jnp.sqrt(v_hat) + ADAM_EPS) + ADAM_WD * w)
    return delta, m, v


def _adamw(name, w, g, m, v):
    R, C = w.shape
    tr = R
    if R % SUBLANES == 0:
        for cand in range(SUBLANES, min(R, 256) + 1, SUBLANES):
            if R % cand == 0:
                tr = cand

    def body(w_ref, g_ref, m_ref, v_ref, d_ref, nm_ref, nv_ref):
        d, nm, nv = _adamw_math(w_ref[...], g_ref[...], m_ref[...], v_ref[...])
        d_ref[...] = d
        nm_ref[...] = nm
        nv_ref[...] = nv

    spec = pl.BlockSpec((tr, C), lambda i: (i, 0))
    shape = jax.ShapeDtypeStruct((R, C), F32)
    return pl.pallas_call(
        body, name=name, grid=(R // tr,), in_specs=[spec] * 4, out_specs=[spec] * 3, out_shape=[shape] * 3,
        compiler_params=_params(("parallel",)),
    )(w, g, m, v)


def _adamw_small(parts, w, m, v):
    n = parts.shape[0]

    def body(p_ref, w_ref, m_ref, v_ref, g_ref, d_ref, nm_ref, nv_ref):
        g = p_ref[0]
        for k in range(1, n):
            g = g + p_ref[k]
        d, nm, nv = _adamw_math(w_ref[...], g, m_ref[...], v_ref[...])
        g_ref[...] = g
        d_ref[...] = d
        nm_ref[...] = nm
        nv_ref[...] = nv

    shape = jax.ShapeDtypeStruct(w.shape, F32)
    return pl.pallas_call(body, name="adamw_small", out_shape=[shape] * 4, compiler_params=_params())(parts, w, m, v)


WEIGHTS = ['g_mix', 'w_in', 'rw_mu', 'rw_w0', 'rw_w_up', 'rw_a0', 'rw_a_up', 'rw_g_up', 'rw_k_k', 'rw_k_a',
           'rw_r_k', 'rw_ln_g', 'rw_ln_b', 'w_branch_a', 'w_branch_b', 'w_gate', 'b_gate', 'w_out', 'g_ffn', 'w_up',
           'conv_w', 'conv_b', 'w_down', 'g_ple', 'w_ple_gate', 'w_ple', 'g_final']
ARG_NAMES = (['x', 'p'] + WEIGHTS + ['loss_target'] + ['m_' + n for n in WEIGHTS] + ['v_' + n for n in WEIGHTS])
SHARDED = {'w_in': 1, 'rw_w_up': 1, 'rw_a_up': 1, 'rw_g_up': 1, 'w_branch_a': 1, 'w_branch_b': 1, 'w_gate': 1,
           'w_out': 0, 'w_up': 1, 'conv_w': 1, 'w_down': 0, 'w_ple_gate': 0, 'w_ple': 1}
SMALL = [n for n in WEIGHTS if n not in SHARDED]
WHOLE = ['conv_w']
FIRST_USED = ['w_in', 'rw_w_up', 'rw_a_up', 'rw_g_up', 'w_gate']
FIRST_DONE = ['w_up', 'w_down', 'w_ple_gate', 'w_ple']
SPLIT = [n for n in SHARDED if n not in WHOLE]
PACK_ALIGN = SUBLANES * LANES


def _pack_rows(flat_parts):
    flat = jnp.concatenate(flat_parts, axis=1)
    n = flat.shape[1]
    padded = -(-n // PACK_ALIGN) * PACK_ALIGN
    flat = jnp.pad(flat, ((0, 0), (0, padded - n)))
    return flat.reshape(padded // LANES, LANES)


def _full_from_shards(stack, axis):
    _, R, C = stack.shape
    if axis == 0:
        return stack.reshape(N_CHIPS * R, C)
    return stack.transpose(1, 0, 2).reshape(R, N_CHIPS * C)


def _shards_from_full(full, axis):
    R, C = full.shape
    if axis == 0:
        return full.reshape(N_CHIPS, R // N_CHIPS, C)
    return full.reshape(R, N_CHIPS, C // N_CHIPS).transpose(1, 0, 2)


def kernel(x, p, g_mix, w_in, rw_mu, rw_w0, rw_w_up, rw_a0, rw_a_up, rw_g_up, rw_k_k, rw_k_a, rw_r_k, rw_ln_g, rw_ln_b, w_branch_a, w_branch_b, w_gate, b_gate, w_out, g_ffn, w_up, conv_w, conv_b, w_down, g_ple, w_ple_gate, w_ple, g_final, loss_target, m_g_mix, m_w_in, m_rw_mu, m_rw_w0, m_rw_w_up, m_rw_a0, m_rw_a_up, m_rw_g_up, m_rw_k_k, m_rw_k_a, m_rw_r_k, m_rw_ln_g, m_rw_ln_b, m_w_branch_a, m_w_branch_b, m_w_gate, m_b_gate, m_w_out, m_g_ffn, m_w_up, m_conv_w, m_conv_b, m_w_down, m_g_ple, m_w_ple_gate, m_w_ple, m_g_final, v_g_mix, v_w_in, v_rw_mu, v_rw_w0, v_rw_w_up, v_rw_a0, v_rw_a_up, v_rw_g_up, v_rw_k_k, v_rw_k_a, v_rw_r_k, v_rw_ln_g, v_rw_ln_b, v_w_branch_a, v_w_branch_b, v_w_gate, v_b_gate, v_w_out, v_g_ffn, v_w_up, v_conv_w, v_conv_b, v_w_down, v_g_ple, v_w_ple_gate, v_w_ple, v_g_final):
    given = dict(zip(ARG_NAMES, (x, p, g_mix, w_in, rw_mu, rw_w0, rw_w_up, rw_a0, rw_a_up, rw_g_up, rw_k_k, rw_k_a, rw_r_k, rw_ln_g, rw_ln_b, w_branch_a, w_branch_b, w_gate, b_gate, w_out, g_ffn, w_up, conv_w, conv_b, w_down, g_ple, w_ple_gate, w_ple, g_final, loss_target, m_g_mix, m_w_in, m_rw_mu, m_rw_w0, m_rw_w_up, m_rw_a0, m_rw_a_up, m_rw_g_up, m_rw_k_k, m_rw_k_a, m_rw_r_k, m_rw_ln_g, m_rw_ln_b, m_w_branch_a, m_w_branch_b, m_w_gate, m_b_gate, m_w_out, m_g_ffn, m_w_up, m_conv_w, m_conv_b, m_w_down, m_g_ple, m_w_ple_gate, m_w_ple, m_g_final, v_g_mix, v_w_in, v_rw_mu, v_rw_w0, v_rw_w_up, v_rw_a0, v_rw_a_up, v_rw_g_up, v_rw_k_k, v_rw_k_a, v_rw_r_k, v_rw_ln_g, v_rw_ln_b, v_w_branch_a, v_w_branch_b, v_w_gate, v_b_gate, v_w_out, v_g_ffn, v_w_up, v_conv_w, v_conv_b, v_w_down, v_g_ple, v_w_ple_gate, v_w_ple, v_g_final), strict=True))

    def two_d(name, prefix=""):
        a = given[prefix + name]
        if name == "g_final":
            return a.reshape(1, D_MODEL)
        if name == "rw_r_k":
            return a.reshape(1, RW_WIDTH)
        return a[0] if a.ndim == 3 else a

    cast = lambda n: two_d(n) if n in WHOLE else two_d(n).astype(BF16)
    whole = lambda names, stacks: {n: _full_from_shards(g, SHARDED[n]) for n, g in zip(names, stacks, strict=True)}
    late_names = [n for n in SHARDED if n not in FIRST_USED]
    late_sends, late_recvs, late_srcs, late_lands, token = _travel_start(
        "gather_late_start", "gather", [cast(n) for n in late_names])
    W = whole(FIRST_USED, _gather_chips([cast(n) for n in FIRST_USED]))
    for n in SMALL:
        W[n] = two_d(n)
    W["rw_r_k"] = W["rw_r_k"].reshape(RW_HEADS, RW_HEAD_DIM)
    W["g_mix"] = W["g_mix"] + token[0:1, 0:1]

    def late_weights(after):
        return whole(late_names, _travel_wait("gather_late_wait", "gather", late_sends, late_recvs, late_srcs,
                                             late_lands, after))

    core = lax.axis_index("c").astype(jnp.int32).reshape(1)
    early_names = [n for n in SPLIT if n in FIRST_DONE]
    rest_names = [n for n in SPLIT if n not in FIRST_DONE]
    travelling = {}

    def core_sums(tag, names, G):
        by_chip = [_shards_from_full(G[n], SHARDED[n]) for n in names]
        theirs = _swap_halves("swap_halves_" + tag, by_chip)
        return [_sum_cores("sum_cores_" + n, g, t, core) for n, g, t in zip(names, by_chip, theirs, strict=True)]

    def early_grads(G):
        sends, recvs, srcs, lands, started = _travel_start("scatter_early_start", "scatter",
                                                           core_sums("early", early_names, G))
        travelling.update(sends=sends, recvs=recvs, srcs=srcs, lands=lands)
        return started

    loss_part, grad_x, G = _local_step(x[0], p[0, 0], W, loss_target[0], late_weights, early_grads)

    landed = dict(zip(rest_names, _scatter_chips(core_sums("rest", rest_names, G)), strict=True))
    landed.update(zip(early_names, _travel_wait("scatter_early_wait", "scatter", travelling["sends"],
                                                travelling["recvs"], travelling["srcs"], travelling["lands"],
                                                landed[rest_names[0]]), strict=True))
    reduced = [_sum_chips("sum_chips_" + n, landed[n]) for n in SPLIT]
    shard_grads = dict(zip(SPLIT, _join_halves(reduced), strict=True))

    small_sizes = {n: two_d(n).shape[1] for n in SMALL}
    n_small = sum(small_sizes.values())
    whole_sizes = {n: G[n].shape[0] * G[n].shape[1] for n in WHOLE}
    n_whole = sum(whole_sizes.values())

    def pack_small(parts, rest):
        return _pack_rows([a.reshape(1, -1) for a in parts] + [rest])

    G["rw_r_k"] = G["rw_r_k"].reshape(1, RW_WIDTH)
    rest = jnp.zeros((1, n_whole + 1), F32)
    all_small = _gather_all(pack_small([G[n] for n in SMALL] + [G[n] for n in WHOLE], loss_part))
    gs, ds, nms, nvs = _adamw_small(all_small, pack_small([two_d(n) for n in SMALL], rest),
                                    pack_small([two_d(n, "m_") for n in SMALL], rest),
                                    pack_small([two_d(n, "v_") for n in SMALL], rest))
    gs, ds, nms, nvs = (a.reshape(-1) for a in (gs, ds, nms, nvs))
    loss = gs[n_small + n_whole]
    chip = 2 * lax.axis_index("x") + lax.axis_index("y")
    off = n_small
    for n in WHOLE:
        full = gs[off:off + whole_sizes[n]].reshape(G[n].shape)
        off += whole_sizes[n]
        width = two_d(n).shape[1]
        shard_grads[n] = lax.dynamic_slice_in_dim(full, chip * width, width, axis=1)

    grads, deltas, new_m, new_v = {}, {}, {}, {}
    for n in SHARDED:
        g = shard_grads[n]
        d, nm, nv = _adamw("adamw_" + n, two_d(n), g, two_d(n, "m_"), two_d(n, "v_"))
        grads[n], deltas[n], new_m[n], new_v[n] = g, d, nm, nv
    off = 0
    for n in SMALL:
        sl = slice(off, off + small_sizes[n])
        off += small_sizes[n]
        grads[n], deltas[n], new_m[n], new_v[n] = gs[sl], ds[sl], nms[sl], nvs[sl]
    outs = [loss, grad_x[None]]
    for table in (grads, deltas, new_m, new_v):
        outs += [table[n].reshape(given[n].shape) for n in WEIGHTS]
    return tuple(outs)
```

```python
import math

import jax
import jax.numpy as jnp
import numpy as np
from jax import lax
from jax.experimental import pallas as pl
from jax.experimental.pallas import tpu as pltpu

F32 = jnp.float32
BF16 = jnp.bfloat16

D_MODEL = 1024
NORM_EPS = 1e-6
RW_HEADS = 8
RW_HEAD_DIM = 64
RW_WIDTH = 512
RW_LN_EPS = 64e-5
ATT_GROUP_DILATION = (1, 4, 16)
ATT_BLOCK = 128
ATT_HEADS = 12
ATT_HEAD_DIM = 64
ATT_GROUP_WIDTH = 256
ATT_WIDTH = 768
D_FF = 3072

ADAM_LR = 0.001
ADAM_B1 = 0.9
ADAM_B2 = 0.999
ADAM_EPS = 1e-08
ADAM_WD = 0.01
ADAM_STEP = 10

SUBLANES = 8
LANES = 128
VMEM_LIMIT = 56 * 1024 * 1024
N_CHIPS = 4
N_DEV = 8
MESH = pl.DeviceIdType.MESH


def _params(sem=None):
    return pltpu.CompilerParams(dimension_semantics=sem, vmem_limit_bytes=VMEM_LIMIT)


def _pick(dim, pref):
    if dim % LANES != 0 or dim <= pref:
        return dim
    best = LANES
    for t in range(LANES, pref + 1, LANES):
        if dim % t == 0:
            best = t
    return best


def _mm(name, a, b, mode, out_dtype=F32, add=None, tm=1024, tn=1024, tk=1024):
    if mode == "nn":
        (M, K), (K2, N) = a.shape, b.shape
    elif mode == "nt":
        (M, K), (N, K2) = a.shape, b.shape
    else:
        (K, M), (K2, N) = a.shape, b.shape
    assert K == K2, (name, a.shape, b.shape, mode)
    tm, tn, tk = _pick(M, tm), _pick(N, tn), _pick(K, tk)
    nk = K // tk
    if mode == "nn":
        a_spec = pl.BlockSpec((tm, tk), lambda i, j, k: (i, k))
        b_spec = pl.BlockSpec((tk, tn), lambda i, j, k: (k, j))
        dims = (((1,), (0,)), ((), ()))
    elif mode == "nt":
        a_spec = pl.BlockSpec((tm, tk), lambda i, j, k: (i, k))
        b_spec = pl.BlockSpec((tn, tk), lambda i, j, k: (j, k))
        dims = (((1,), (1,)), ((), ()))
    else:
        a_spec = pl.BlockSpec((tk, tm), lambda i, j, k: (k, i))
        b_spec = pl.BlockSpec((tk, tn), lambda i, j, k: (k, j))
        dims = (((0,), (0,)), ((), ()))
    o_spec = pl.BlockSpec((tm, tn), lambda i, j, k: (i, j))
    has_add = add is not None

    def body(*refs):
        if has_add:
            a_ref, b_ref, add_ref, o_ref, acc_ref = refs
        else:
            a_ref, b_ref, o_ref, acc_ref = refs
        k = pl.program_id(2)
        part = lax.dot_general(a_ref[...].astype(BF16), b_ref[...].astype(BF16), dims,
                               preferred_element_type=F32)

        @pl.when(k == 0)
        def _():
            acc_ref[...] = part

        @pl.when(k > 0)
        def _():
            acc_ref[...] += part

        @pl.when(k == nk - 1)
        def _():
            res = acc_ref[...]
            if has_add:
                res = res + add_ref[...].astype(F32)
            o_ref[...] = res.astype(o_ref.dtype)

    ins = [a, b] + ([add] if has_add else [])
    in_specs = [a_spec, b_spec] + ([o_spec] if has_add else [])
    return pl.pallas_call(
        body, name=name, grid=(M // tm, N // tn, nk),
        in_specs=in_specs, out_specs=o_spec,
        out_shape=jax.ShapeDtypeStruct((M, N), out_dtype),
        scratch_shapes=[pltpu.VMEM((tm, tn), F32)],
        compiler_params=_params(("parallel", "parallel", "arbitrary")),
    )(*ins)


def _rowwise(name, fn, T, tT, rows=(), prevs=(), nexts=(), consts=(), outs=()):
    n = T // tT
    per8 = tT // SUBLANES
    in_specs, ins = [], []
    for arr in rows:
        in_specs.append(pl.BlockSpec((tT, arr.shape[1]), lambda i: (i, 0)))
        ins.append(arr)
    for arr in prevs:
        in_specs.append(pl.BlockSpec((SUBLANES, arr.shape[1]), lambda i: (jnp.maximum(i * per8 - 1, 0), 0)))
        ins.append(arr)
    for arr in nexts:
        in_specs.append(pl.BlockSpec((SUBLANES, arr.shape[1]),
                                     lambda i: (jnp.minimum((i + 1) * per8, T // SUBLANES - 1), 0)))
        ins.append(arr)
    for arr in consts:
        in_specs.append(pl.BlockSpec(arr.shape, lambda i, nd=arr.ndim: (0,) * nd))
        ins.append(arr)
    out_specs, out_shapes = [], []
    for o in outs:
        if o[0] == "row":
            out_specs.append(pl.BlockSpec((tT, o[1]), lambda i: (i, 0)))
            out_shapes.append(jax.ShapeDtypeStruct((T, o[1]), o[2]))
        else:
            out_specs.append(pl.BlockSpec(o[1], lambda i: (0, 0)))
            out_shapes.append(jax.ShapeDtypeStruct(o[1], F32))
    nr, npv, nnx, nc = len(rows), len(prevs), len(nexts), len(consts)
    n_in = nr + npv + nnx + nc

    def body(*refs):
        i = pl.program_id(0)
        vals = [r[...] for r in refs[:n_in]]
        res = fn(i, n, vals[:nr], vals[nr:nr + npv], vals[nr + npv:nr + npv + nnx], vals[nr + npv + nnx:])
        for o, o_ref, val in zip(outs, refs[n_in:], res, strict=True):
            if o[0] == "row":
                o_ref[...] = val.astype(o_ref.dtype)
            else:
                @pl.when(i == 0)
                def _(o_ref=o_ref, val=val):
                    o_ref[...] = val.astype(F32)

                @pl.when(i > 0)
                def _(o_ref=o_ref, val=val):
                    o_ref[...] += val.astype(F32)

    res = pl.pallas_call(
        body, name=name, grid=(n,), in_specs=in_specs, out_specs=out_specs, out_shape=out_shapes,
        compiler_params=_params(("arbitrary",)),
    )(*ins)
    return list(res)


def _shift_down(x, prev8, i, s):
    rolled = pltpu.roll(x, s, 0)
    head = pltpu.roll(prev8, s, 0)
    head = jnp.where(i == 0, jnp.zeros_like(head), head)
    rid = lax.broadcasted_iota(jnp.int32, head.shape, 0)
    first = jnp.where(rid < s, head, rolled[:SUBLANES])
    if x.shape[0] == SUBLANES:
        return first
    return jnp.concatenate([first, rolled[SUBLANES:]], axis=0)


def _shift_up(x, next8, i, n, s):
    tT = x.shape[0]
    rolled = pltpu.roll(x, tT - s, 0)
    tail = pltpu.roll(next8, SUBLANES - s, 0)
    tail = jnp.where(i == n - 1, jnp.zeros_like(tail), tail)
    rid = lax.broadcasted_iota(jnp.int32, tail.shape, 0)
    last = jnp.where(rid >= SUBLANES - s, tail, rolled[tT - SUBLANES:])
    return jnp.concatenate([rolled[:tT - SUBLANES], last], axis=0)


def _colsum(x):
    return jnp.sum(x, axis=0, keepdims=True)


def _segsum(x, bd):
    return jnp.dot(x, bd, precision=lax.Precision.HIGH, preferred_element_type=F32)


def _block_diag_ones(width, seg):
    idx = np.arange(width) // seg
    return jnp.asarray((idx[:, None] == idx[None, :]).astype(np.float32))


def _sigmoid(z):
    return 1.0 / (1.0 + jnp.exp(-z))


def _softplus(z):
    return jnp.maximum(z, 0.0) + jnp.log(1.0 + jnp.exp(-jnp.abs(z)))


def _rms_fwd(x, g):
    r = lax.rsqrt(jnp.mean(x * x, axis=-1, keepdims=True) + NORM_EPS)
    return x * r * g


def _rms_bwd(x, g, dy):
    r = lax.rsqrt(jnp.mean(x * x, axis=-1, keepdims=True) + NORM_EPS)
    gdy = dy * g
    dx = r * (gdy - x * (r * r) * jnp.mean(x * gdy, axis=-1, keepdims=True))
    return dx, dy * x * r


GELU_C = math.sqrt(2.0 / math.pi)


def _gelu(x):
    return 0.5 * x * (1.0 + jnp.tanh(GELU_C * (x + 0.044715 * x * x * x)))


def _gelu_and_grad(x):
    th = jnp.tanh(GELU_C * (x + 0.044715 * x * x * x))
    half = 0.5 * (1.0 + th)
    return x * half, half + 0.5 * x * (1.0 - th * th) * GELU_C * (1.0 + 3.0 * 0.044715 * x * x)


RW_CHUNK = 64
NN = (((1,), (0,)), ((), ()))
NT = (((1,), (1,)), ((), ()))
TN = (((0,), (0,)), ((), ()))


def _hdot(a, b, dims):
    return lax.dot_general(a, b, dims, precision=lax.Precision.HIGH, preferred_element_type=F32)


def _chunk_masks():
    ti = lax.broadcasted_iota(jnp.int32, (RW_CHUNK, RW_CHUNK), 0)
    tj = lax.broadcasted_iota(jnp.int32, (RW_CHUNK, RW_CHUNK), 1)
    return tj <= ti, tj < ti, (ti == tj).astype(F32)


def _head(x, h):
    return x[:, h * RW_HEAD_DIM:(h + 1) * RW_HEAD_DIM]


def _heads(fn):
    return [fn(h) for h in range(RW_HEADS)]


def _chunk_rows(r, lw, k, a, b, incl_f):
    c = _hdot(incl_f, lw, NN)
    e_prev, e_neg, e_pos = jnp.exp(c - lw), jnp.exp(-c), jnp.exp(c)
    return dict(At=a * e_prev, Bt=b * e_neg, Kt=k * e_neg, Rt=r * e_pos, e_prev=e_prev, e_neg=e_neg, e_pos=e_pos)


def _chunk_coeffs(q, incl, strict):
    A1 = _heads(lambda h: jnp.where(strict, _hdot(_head(q["At"], h), _head(q["Bt"], h), NT), 0.0))
    A2 = _heads(lambda h: jnp.where(strict, _hdot(_head(q["At"], h), _head(q["Kt"], h), NT), 0.0))
    W1 = _heads(lambda h: jnp.where(incl, _hdot(_head(q["Rt"], h), _head(q["Bt"], h), NT), 0.0))
    W2 = _heads(lambda h: jnp.where(incl, _hdot(_head(q["Rt"], h), _head(q["Kt"], h), NT), 0.0))
    return A1, A2, W1, W2


def _rwkv_chunk_prep(r, lw, k, a, b, v):
    T = r.shape[0]
    nC = T // RW_CHUNK
    H, N = RW_HEADS, RW_HEAD_DIM

    def body(r_ref, lw_ref, k_ref, a_ref, b_ref, v_ref,
             at_ref, bt_ref, kt_ref, rt_ref, a2v_ref, w2v_ref, ti_ref, w1_ref, a2_ref, w2_ref, pl_ref):
        incl, strict, eye = _chunk_masks()
        q = _chunk_rows(r_ref[...], lw_ref[...], k_ref[...], a_ref[...], b_ref[...], incl.astype(F32))
        at_ref[...], bt_ref[...], kt_ref[...], rt_ref[...] = q["At"], q["Bt"], q["Kt"], q["Rt"]
        pl_ref[0] = jnp.broadcast_to(q["e_pos"][RW_CHUNK - 1:RW_CHUNK, :], (SUBLANES, RW_WIDTH))
        A1, A2, W1, W2 = _chunk_coeffs(q, incl, strict)
        V = v_ref[...]
        a2v_ref[...] = jnp.concatenate(_heads(lambda h: _hdot(A2[h], _head(V, h), NN)), axis=1)
        w2v_ref[...] = jnp.concatenate(_heads(lambda h: _hdot(W2[h], _head(V, h), NN)), axis=1)
        tinv, pw = [eye + m for m in A1], A1
        for _ in range(5):
            pw = [_hdot(m, m, NN) for m in pw]
            tinv = [t + _hdot(t, m, NN) for t, m in zip(tinv, pw, strict=True)]
        for h in range(H):
            ti_ref[0, h] = tinv[h]
            w1_ref[0, h] = W1[h]
            a2_ref[0, h] = A2[h]
            w2_ref[0, h] = W2[h]

    row_spec = pl.BlockSpec((RW_CHUNK, RW_WIDTH), lambda n: (n, 0))
    st_spec = pl.BlockSpec((1, H, N, N), lambda n: (n, 0, 0, 0))
    row_shape = jax.ShapeDtypeStruct((T, RW_WIDTH), F32)
    st_shape = jax.ShapeDtypeStruct((nC, H, N, N), F32)
    return pl.pallas_call(
        body, name="rwkv_chunk_prep", grid=(nC,),
        in_specs=[row_spec] * 6,
        out_specs=[row_spec] * 6 + [st_spec] * 4 + [pl.BlockSpec((1, SUBLANES, RW_WIDTH), lambda n: (n, 0, 0))],
        out_shape=[row_shape] * 6 + [st_shape] * 4 + [jax.ShapeDtypeStruct((nC, SUBLANES, RW_WIDTH), F32)],
        compiler_params=_params(("parallel",)),
    )(r, lw, k, a, b, v)


def _rwkv_chunk_fwd(v, at, bt, kt, rt, a2v, w2v, tinv, w1, plast):
    T = v.shape[0]
    nC = T // RW_CHUNK
    H, N = RW_HEADS, RW_HEAD_DIM

    def body(v_ref, at_ref, bt_ref, kt_ref, rt_ref, a2v_ref, w2v_ref, ti_ref, w1_ref, pl_ref,
             y_ref, sa_ref, s0_ref, S_ref):
        @pl.when(pl.program_id(0) == 0)
        def _():
            S_ref[...] = jnp.zeros_like(S_ref)

        V, At, Bt, Kt, Rt = v_ref[...], at_ref[...], bt_ref[...], kt_ref[...], rt_ref[...]
        A2V, W2V, p_last = a2v_ref[...], w2v_ref[...], pl_ref[0, 0:1, :]
        S0 = _heads(lambda h: S_ref[h])
        for h in range(H):
            s0_ref[0, h] = S0[h]
        Z = _heads(lambda h: _hdot(_head(At, h), S0[h], NT) + _head(A2V, h))
        Sa = _heads(lambda h: _hdot(ti_ref[0, h], Z[h], NN))
        X = _heads(lambda h: S0[h] + _hdot(Sa[h], _head(Bt, h), TN) + _hdot(_head(V, h), _head(Kt, h), TN))
        for h in range(H):
            S_ref[h] = X[h] * _head(p_last, h)
        Y = _heads(lambda h: _hdot(_head(Rt, h), S0[h], NT) + _hdot(w1_ref[0, h], Sa[h], NN) + _head(W2V, h))
        y_ref[...] = jnp.concatenate(Y, axis=1)
        sa_ref[...] = jnp.concatenate(Sa, axis=1)

    row_spec = pl.BlockSpec((RW_CHUNK, RW_WIDTH), lambda n: (n, 0))
    st_spec = pl.BlockSpec((1, H, N, N), lambda n: (n, 0, 0, 0))
    row_shape = jax.ShapeDtypeStruct((T, RW_WIDTH), F32)
    return pl.pallas_call(
        body, name="rwkv_chunk_fwd", grid=(nC,),
        in_specs=[row_spec] * 7 + [st_spec, st_spec, pl.BlockSpec((1, SUBLANES, RW_WIDTH), lambda n: (n, 0, 0))],
        out_specs=[row_spec, row_spec, st_spec],
        out_shape=[row_shape, row_shape, jax.ShapeDtypeStruct((nC, H, N, N), F32)],
        scratch_shapes=[pltpu.VMEM((H, N, N), F32)],
        compiler_params=_params(("arbitrary",)),
    )(v, at, bt, kt, rt, a2v, w2v, tinv, w1, plast)


def _rwkv_chunk_bwd(r, lw, k, a, b, v, dy, s0, tinv, w1, a2, w2, sa):
    T = r.shape[0]
    nC = T // RW_CHUNK
    H, N = RW_HEADS, RW_HEAD_DIM

    def body(r_ref, lw_ref, k_ref, a_ref, b_ref, v_ref, dy_ref, s0_ref, ti_ref, w1_ref, a2_ref, w2_ref, sa_ref,
             dr_ref, dlw_ref, dk_ref, da_ref, db_ref, dv_ref, dS_ref):
        @pl.when(pl.program_id(0) == 0)
        def _():
            dS_ref[...] = jnp.zeros_like(dS_ref)

        incl, strict, _ = _chunk_masks()
        incl_f = incl.astype(F32)
        q = _chunk_rows(r_ref[...], lw_ref[...], k_ref[...], a_ref[...], b_ref[...], incl_f)
        At, Bt, Kt, Rt = q["At"], q["Bt"], q["Kt"], q["Rt"]
        A2, W1, W2 = (_heads(lambda h, ref=ref: ref[0, h]) for ref in (a2_ref, w1_ref, w2_ref))
        V, dY, Sa = v_ref[...], dy_ref[...], sa_ref[...]
        hd = _head
        p_last = q["e_pos"][RW_CHUNK - 1:RW_CHUNK, :]
        S0 = _heads(lambda h: s0_ref[0, h])
        G = _heads(lambda h: dS_ref[h] * hd(p_last, h))
        X = _heads(lambda h: S0[h] + _hdot(hd(Sa, h), hd(Bt, h), TN) + _hdot(hd(V, h), hd(Kt, h), TN))
        dc_last = jnp.concatenate(_heads(lambda h: jnp.sum(G[h] * X[h], axis=0, keepdims=True)), axis=1)
        dSa = _heads(lambda h: _hdot(hd(Bt, h), G[h], NT) + _hdot(W1[h], hd(dY, h), TN))
        dZ = _heads(lambda h: _hdot(ti_ref[0, h], dSa[h], TN))
        for h in range(H):
            dS_ref[h] = G[h] + _hdot(dZ[h], hd(At, h), TN) + _hdot(hd(dY, h), hd(Rt, h), TN)
        dA1 = _heads(lambda h: jnp.where(strict, _hdot(dZ[h], hd(Sa, h), NT), 0.0))
        dA2 = _heads(lambda h: jnp.where(strict, _hdot(dZ[h], hd(V, h), NT), 0.0))
        dW1 = _heads(lambda h: jnp.where(incl, _hdot(hd(dY, h), hd(Sa, h), NT), 0.0))
        dW2 = _heads(lambda h: jnp.where(incl, _hdot(hd(dY, h), hd(V, h), NT), 0.0))
        cat = lambda fn: jnp.concatenate(_heads(fn), axis=1)
        dV = cat(lambda h: _hdot(A2[h], dZ[h], TN) + _hdot(W2[h], hd(dY, h), TN) + _hdot(hd(Kt, h), G[h], NT))
        dAt = cat(lambda h: _hdot(dA1[h], hd(Bt, h), NN) + _hdot(dA2[h], hd(Kt, h), NN) + _hdot(dZ[h], S0[h], NN))
        dBt = cat(lambda h: _hdot(dA1[h], hd(At, h), TN) + _hdot(dW1[h], hd(Rt, h), TN) + _hdot(hd(Sa, h), G[h], NN))
        dKt = cat(lambda h: _hdot(dA2[h], hd(At, h), TN) + _hdot(dW2[h], hd(Rt, h), TN) + _hdot(hd(V, h), G[h], NN))
        dRt = cat(lambda h: _hdot(hd(dY, h), S0[h], NN) + _hdot(dW1[h], hd(Bt, h), NN) + _hdot(dW2[h], hd(Kt, h), NN))
        last_row = lax.broadcasted_iota(jnp.int32, (RW_CHUNK, RW_WIDTH), 0) == RW_CHUNK - 1
        dc_prev = dAt * At
        dc = dc_prev + dRt * Rt - dBt * Bt - dKt * Kt + jnp.where(last_row, dc_last, 0.0)
        dr_ref[...] = dRt * q["e_pos"]
        dlw_ref[...] = _hdot(incl_f, dc, TN) - dc_prev
        dk_ref[...] = dKt * q["e_neg"]
        da_ref[...] = dAt * q["e_prev"]
        db_ref[...] = dBt * q["e_neg"]
        dv_ref[...] = dV

    rev = lambda n: nC - 1 - n
    row_spec = pl.BlockSpec((RW_CHUNK, RW_WIDTH), lambda n: (rev(n), 0))
    st_spec = pl.BlockSpec((1, H, N, N), lambda n: (rev(n), 0, 0, 0))
    row_shape = jax.ShapeDtypeStruct((T, RW_WIDTH), F32)
    return pl.pallas_call(
        body, name="rwkv_chunk_bwd", grid=(nC,),
        in_specs=[row_spec] * 7 + [st_spec] * 5 + [row_spec], out_specs=[row_spec] * 6,
        out_shape=[row_shape] * 6, scratch_shapes=[pltpu.VMEM((H, N, N), F32)],
        compiler_params=_params(("arbitrary",)),
    )(r, lw, k, a, b, v, dy, s0, tinv, w1, a2, w2, sa)


def _alibi_slope(head):
    return float(np.float32(2.0 ** (-8.0 * (head + 1) / ATT_HEADS)))


ATT_SPAN = ATT_BLOCK * max(ATT_GROUP_DILATION)
ATT_PAIR_WIDTH = 2 * ATT_HEAD_DIM
ATT_SIDE_BY_SIDE = 8


def _pair_slope(g, hp, j):
    return jnp.where(hp == 0, _alibi_slope(4 * g + j), _alibi_slope(4 * g + 2 + j))


def _att_rows(mi, r, d):
    start = mi * ATT_BLOCK * d + r
    return pl.ds(start, ATT_BLOCK) if d == 1 else pl.ds(start, ATT_BLOCK, stride=d)


def _att_masks():
    qi = lax.broadcasted_iota(jnp.int32, (ATT_BLOCK, ATT_BLOCK), 0)
    kj = lax.broadcasted_iota(jnp.int32, (ATT_BLOCK, ATT_BLOCK), 1)
    return qi, kj


NEG = -1e30


def _att_logits(q, k, slope_d, steps, valid):
    s = lax.dot_general(q.astype(BF16), k.astype(BF16), (((1,), (1,)), ((), ())),
                        preferred_element_type=F32) * (ATT_HEAD_DIM ** -0.5)
    return jnp.where(valid, s - slope_d * steps.astype(F32), NEG)


def _att_fwd(p_att, g):
    T = p_att.shape[0]
    d = ATT_GROUP_DILATION[g]
    W = ATT_PAIR_WIDTH
    nb = T // ATT_SPAN
    mb = ATT_SPAN // (ATT_BLOCK * d)

    def body(q_ref, kc_ref, kp_ref, vc_ref, vp_ref, o_ref, l_ref):
        hp, n = pl.program_id(0), pl.program_id(1)
        qi, kj = _att_masks()
        slopes = [_pair_slope(g, hp, j) * d for j in range(2)]
        blocks = [(r, mi) for r in range(d) for mi in range(mb)]
        for at in range(0, len(blocks), ATT_SIDE_BY_SIDE):
            tasks = []
            for r, mi in blocks[at:at + ATT_SIDE_BY_SIDE]:
                rows = _att_rows(mi, r, d)
                if mi > 0:
                    prev = _att_rows(mi - 1, r, d)
                    kp, vp, has_prev = kc_ref[prev, :], vc_ref[prev, :], True
                else:
                    prev = _att_rows(mb - 1, r, d)
                    kp, vp, has_prev = kp_ref[prev, :], vp_ref[prev, :], n > 0
                q, kc, vc = q_ref[rows, :], kc_ref[rows, :], vc_ref[rows, :]
                for j in range(2):
                    sl = slice(j * ATT_HEAD_DIM, (j + 1) * ATT_HEAD_DIM)
                    tasks.append((q[:, sl], kc[:, sl], kp[:, sl], vc[:, sl], vp[:, sl], has_prev, slopes[j]))
            lc = [_att_logits(t[0], t[1], t[6], qi - kj, kj <= qi) for t in tasks]
            lp = [_att_logits(t[0], t[2], t[6], qi - kj + ATT_BLOCK, (kj >= qi) & t[5]) for t in tasks]
            mx = [jnp.maximum(jnp.max(a, axis=1, keepdims=True), jnp.max(b, axis=1, keepdims=True))
                  for a, b in zip(lc, lp, strict=True)]
            ec = [jnp.exp(a - m) for a, m in zip(lc, mx, strict=True)]
            ep = [jnp.exp(b - m) for b, m in zip(lp, mx, strict=True)]
            den = [jnp.sum(a, axis=1, keepdims=True) + jnp.sum(b, axis=1, keepdims=True)
                   for a, b in zip(ec, ep, strict=True)]
            inv = [1.0 / s for s in den]
            outs = [jnp.dot((a * i).astype(BF16), t[3].astype(BF16), preferred_element_type=F32)
                    + jnp.dot((b * i).astype(BF16), t[4].astype(BF16), preferred_element_type=F32)
                    for a, b, i, t in zip(ec, ep, inv, tasks, strict=True)]
            lses = [jnp.broadcast_to(m + jnp.log(s), (ATT_BLOCK, ATT_HEAD_DIM)) for m, s in zip(mx, den, strict=True)]
            for i, (r, mi) in enumerate(blocks[at:at + ATT_SIDE_BY_SIDE]):
                rows = _att_rows(mi, r, d)
                o_ref[rows, :] = jnp.concatenate(outs[2 * i:2 * i + 2], axis=1)
                l_ref[rows, :] = jnp.concatenate(lses[2 * i:2 * i + 2], axis=1)

    def spec(col0, prev):
        if prev:
            return pl.BlockSpec((ATT_SPAN, W), lambda hp, n: (jnp.maximum(n - 1, 0), col0 + 2 * g + hp))
        return pl.BlockSpec((ATT_SPAN, W), lambda hp, n: (n, col0 + 2 * g + hp))

    o_spec = pl.BlockSpec((ATT_SPAN, W), lambda hp, n: (n, hp))
    o, l = pl.pallas_call(
        body, name=f"att_fwd_g{g}", grid=(2, nb),
        in_specs=[spec(0, False), spec(6, False), spec(6, True), spec(12, False), spec(12, True)],
        out_specs=[o_spec, o_spec],
        out_shape=[jax.ShapeDtypeStruct((T, ATT_GROUP_WIDTH), F32)] * 2,
        compiler_params=_params(("parallel", "arbitrary")),
    )(p_att, p_att, p_att, p_att, p_att)
    return o, l


def _att_bwd(p_att, o, l, do, dl, g):
    T = p_att.shape[0]
    d = ATT_GROUP_DILATION[g]
    W = ATT_PAIR_WIDTH
    nb = T // ATT_SPAN
    mb = ATT_SPAN // (ATT_BLOCK * d)
    scale = ATT_HEAD_DIM ** -0.5

    def body(q_ref, k_ref, v_ref, o_ref, l_ref, do_ref, dl_ref,
             qn_ref, on_ref, ln_ref, don_ref, dln_ref, dq_ref, dk_ref, dv_ref, carry_ref):
        hp, n = pl.program_id(0), pl.program_id(1)
        qi, kj = _att_masks()

        @pl.when(n == 0)
        def _():
            carry_ref[...] = jnp.zeros_like(carry_ref)

        slopes = [_pair_slope(g, hp, j) * d for j in range(2)]
        blocks = [(r, mi) for r in range(d) for mi in range(mb)]
        side_by_side = ATT_SIDE_BY_SIDE // 2
        carry = None
        for at in range(0, len(blocks), side_by_side):
            tasks = []
            for r, mi in blocks[at:at + side_by_side]:
                rows = _att_rows(mi, r, d)
                if mi < mb - 1:
                    nrows = _att_rows(mi + 1, r, d)
                    nxt = (q_ref[nrows, :], o_ref[nrows, :], l_ref[nrows, :], do_ref[nrows, :], dl_ref[nrows, :])
                    has_next = True
                else:
                    nrows = _att_rows(0, r, d)
                    nxt = (qn_ref[nrows, :], on_ref[nrows, :], ln_ref[nrows, :], don_ref[nrows, :],
                           dln_ref[nrows, :])
                    has_next = n < nb - 1
                cur = (q_ref[rows, :], o_ref[rows, :], l_ref[rows, :], do_ref[rows, :], dl_ref[rows, :])
                k_all, v_all = k_ref[rows, :], v_ref[rows, :]
                for j in range(2):
                    sl = slice(j * ATT_HEAD_DIM, (j + 1) * ATT_HEAD_DIM)
                    for blk, steps, valid in ((cur, qi - kj, kj <= qi),
                                              (nxt, qi - kj + ATT_BLOCK, (kj >= qi) & has_next)):
                        q, o_, lse, do_, dlse = (z[:, sl] for z in blk)
                        tasks.append(dict(q=q, o=o_, lse=lse[:, :1], do=do_, dlse=dlse[:, :1], steps=steps,
                                          valid=valid, k=k_all[:, sl], vb=v_all[:, sl].astype(BF16),
                                          slope=slopes[j]))
            p = [jnp.exp(_att_logits(t["q"], t["k"], t["slope"], t["steps"], t["valid"]) - t["lse"]) for t in tasks]
            dp = [lax.dot_general(t["do"].astype(BF16), t["vb"], (((1,), (1,)), ((), ())),
                                  preferred_element_type=F32) for t in tasks]
            dsum = [jnp.sum(t["do"] * t["o"], axis=1, keepdims=True) for t in tasks]
            ds = [a * (b - s + t["dlse"]) for a, b, s, t in zip(p, dp, dsum, tasks, strict=True)]
            dv_ = [jnp.dot(a.T.astype(BF16), t["do"].astype(BF16), preferred_element_type=F32)
                   for a, t in zip(p, tasks, strict=True)]
            dk_ = [jnp.dot(a.T.astype(BF16), t["q"].astype(BF16), preferred_element_type=F32) * scale
                   for a, t in zip(ds, tasks, strict=True)]
            dq_ = [jnp.dot(a.astype(BF16), t["k"].astype(BF16), preferred_element_type=F32) * scale
                   for a, t in zip(ds, tasks, strict=True)]
            for i, (r, mi) in enumerate(blocks[at:at + side_by_side]):
                rows = _att_rows(mi, r, d)
                b = 4 * i
                if mi == 0:
                    carry = carry_ref[r]
                dq_ref[rows, :] = jnp.concatenate([dq_[b], dq_[b + 2]], axis=1) + carry
                carry = jnp.concatenate([dq_[b + 1], dq_[b + 3]], axis=1)
                if mi == mb - 1:
                    carry_ref[r] = carry
                dk_ref[rows, :] = jnp.concatenate([dk_[b] + dk_[b + 1], dk_[b + 2] + dk_[b + 3]], axis=1)
                dv_ref[rows, :] = jnp.concatenate([dv_[b] + dv_[b + 1], dv_[b + 2] + dv_[b + 3]], axis=1)

    head_rows = ATT_BLOCK * d
    nxt_n = lambda n: jnp.minimum((n + 1) * mb, T // head_rows - 1)
    cur_p = lambda col0: pl.BlockSpec((ATT_SPAN, W), lambda hp, n: (n, col0 + 2 * g + hp))
    cur_o = pl.BlockSpec((ATT_SPAN, W), lambda hp, n: (n, hp))
    nxt_o = pl.BlockSpec((head_rows, W), lambda hp, n: (nxt_n(n), hp))
    dq, dk, dv = pl.pallas_call(
        body, name=f"att_bwd_g{g}", grid=(2, nb),
        in_specs=[cur_p(0), cur_p(6), cur_p(12), cur_o, cur_o, cur_o, cur_o,
                  pl.BlockSpec((head_rows, W), lambda hp, n: (nxt_n(n), 2 * g + hp)), nxt_o, nxt_o, nxt_o, nxt_o],
        out_specs=[cur_o, cur_o, cur_o],
        out_shape=[jax.ShapeDtypeStruct((T, ATT_GROUP_WIDTH), F32)] * 3,
        scratch_shapes=[pltpu.VMEM((d, ATT_BLOCK, W), F32)],
        compiler_params=_params(("parallel", "arbitrary")),
    )(p_att, p_att, p_att, o, l, do, dl, p_att, o, l, do, dl)
    return dq, dk, dv


RKV = 3 * RW_WIDTH
WA = 128
XG = 160
RW_COLS = RKV + WA + XG


def _local_step(x, p, W, target, late_weights=None, early_grads=None):
    T = x.shape[0]
    tT = 256
    bd512 = _block_diag_ones(RW_WIDTH, RW_HEAD_DIM)
    bd256 = _block_diag_ones(ATT_GROUP_WIDTH, ATT_HEAD_DIM)
    G = {}
    W = dict(W)

    w_in = W["w_in"]
    w_rkv, w_wa, w_xg, w_att = (w_in[:, :RKV], w_in[:, RKV:RKV + WA], w_in[:, RKV + WA:RW_COLS],
                                w_in[:, RW_COLS:])
    mu = W["rw_mu"]
    mu_rkv, mu_wa, mu_xg = mu[:, :RKV], mu[:, RKV:RKV + WA], mu[:, RKV + WA:]
    zpad = jnp.zeros((64, RW_WIDTH), W["rw_w_up"].dtype)
    w_up_pad = jnp.concatenate([W["rw_w_up"], zpad], axis=0)
    a_up_pad = jnp.concatenate([zpad, W["rw_a_up"]], axis=0)
    r_k = W["rw_r_k"].reshape(1, RW_WIDTH)

    (h,) = _rowwise("norm_mix", lambda i, n, r, pv, nx, c: [_rms_fwd(r[0], c[0])], T, tT,
                    rows=[x], consts=[W["g_mix"]], outs=[("row", D_MODEL, BF16)])
    p_rkv = _mm("proj_rkv", h, w_rkv, "nn")
    p_wa = _mm("proj_wa", h, w_wa, "nn")
    p_xg = _mm("proj_xg", h, w_xg, "nn")
    p_att = _mm("proj_att", h, w_att, "nn", tn=768)
    z_gate = _mm("proj_gate", h, W["w_gate"], "nn")

    def rw_pre_core(i, rows, prevs, consts):
        prkv, pwa, pxg = rows[:3]
        (mrkv, mwa, mxg, w0, a0, k_k, k_a, wup, aup, gup, bd) = consts[:11]
        m_rkv = prkv + (_shift_down(prkv, prevs[0], i, 1) - prkv) * mrkv
        m_wa = pwa + (_shift_down(pwa, prevs[1], i, 1) - pwa) * mwa
        m_xg = pxg + (_shift_down(pxg, prevs[2], i, 1) - pxg) * mxg
        r, k, v = m_rkv[:, :RW_WIDTH], m_rkv[:, RW_WIDTH:2 * RW_WIDTH], m_rkv[:, 2 * RW_WIDTH:]
        tw = jnp.tanh(m_wa)
        lw = w0 + jnp.dot(tw.astype(BF16), wup.astype(BF16), preferred_element_type=F32)
        wlog = -_softplus(-lw) - 0.5
        log_decay = -jnp.exp(wlog)
        a = _sigmoid(a0 + jnp.dot(m_wa.astype(BF16), aup.astype(BF16), preferred_element_type=F32))
        sg = _sigmoid(m_xg)
        gate = jnp.dot(sg.astype(BF16), gup.astype(BF16), preferred_element_type=F32)
        kkp = k * k_k
        nrm = jnp.sqrt(_segsum(kkp * kkp, bd))
        nrm_c = jnp.maximum(nrm, 1e-12)
        kk = kkp / nrm_c
        k2 = k * (1.0 + (a - 1.0) * k_a)
        return dict(r=r, k=k, v=v, tw=tw, lw=lw, wlog=wlog, log_decay=log_decay, a=a, sg=sg, gate=gate, kkp=kkp,
                    nrm=nrm, nrm_c=nrm_c, kk=kk, k2=k2, m_rkv=m_rkv, m_wa=m_wa, m_xg=m_xg)

    pre_consts = [mu_rkv, mu_wa, mu_xg, W["rw_w0"], W["rw_a0"], W["rw_k_k"], W["rw_k_a"],
                  w_up_pad, a_up_pad, W["rw_g_up"], bd512]

    def rw_pre(i, n, rows, prevs, nexts, consts):
        q = rw_pre_core(i, rows, prevs, consts)
        return [q["r"], q["log_decay"], q["k2"], q["v"], -q["kk"], q["kk"] * q["a"], q["gate"]]

    r_s, w_s, k_s, v_s, a_s, b_s, gate_s = _rowwise(
        "rwkv_pre", rw_pre, T, tT, rows=[p_rkv, p_wa, p_xg], prevs=[p_rkv, p_wa, p_xg], consts=pre_consts,
        outs=[("row", RW_WIDTH, F32)] * 7)
    (at_s, bt_s, kt_s, rt_s, a2v_s, w2v_s, tinv_s, w1_s, a2_s, w2_s,
     plast_s) = _rwkv_chunk_prep(r_s, w_s, k_s, a_s, b_s, v_s)
    y_scan, sa_s, s0_s = _rwkv_chunk_fwd(v_s, at_s, bt_s, kt_s, rt_s, a2v_s, w2v_s, tinv_s, w1_s, plast_s)

    def rw_post_core(rows, consts):
        y, r, k2, v, gate = rows[:5]
        ln_g, ln_b, rk, bd = consts[:4]
        mean = _segsum(y, bd) * (1.0 / RW_HEAD_DIM)
        yc = y - mean
        var = _segsum(yc * yc, bd) * (1.0 / RW_HEAD_DIM)
        rstd = lax.rsqrt(var + RW_LN_EPS)
        yn = yc * rstd
        s = _segsum(r * k2 * rk, bd)
        return dict(yn=yn, rstd=rstd, s=s, pre=yn * ln_g + ln_b + s * v)

    post_consts = [W["rw_ln_g"], W["rw_ln_b"], r_k, bd512]
    (y_a,) = _rowwise("rwkv_post", lambda i, n, r, pv, nx, c: [rw_post_core(r, c)["pre"] * r[4]], T, tT,
                      rows=[y_scan, r_s, k_s, v_s, gate_s], consts=post_consts, outs=[("row", RW_WIDTH, BF16)])

    att = [_att_fwd(p_att, g) for g in range(3)]

    def comb_weights(ls):
        mx = jnp.maximum(jnp.maximum(ls[0], ls[1]), ls[2])
        es = [jnp.exp(l - mx) for l in ls]
        den = es[0] + es[1] + es[2]
        return [e / den for e in es]

    def att_comb(i, n, rows, pv, nx, c):
        wts = comb_weights(rows[3:6])
        return [wts[0] * rows[0] + wts[1] * rows[1] + wts[2] * rows[2]]

    (y_b,) = _rowwise("att_combine", att_comb, T, tT, rows=[att[0][0], att[1][0], att[2][0], att[0][1], att[1][1],
                                                            att[2][1]], outs=[("row", ATT_GROUP_WIDTH, BF16)])

    if late_weights is not None:
        W.update(late_weights(y_b))
    br_a = _mm("branch_a", y_a, W["w_branch_a"], "nn")
    br_b = _mm("branch_b", y_b, W["w_branch_b"], "nn")

    def merge(i, n, rows, pv, nx, c):
        gates = _sigmoid(rows[0] + c[0])
        return [gates[:, :D_MODEL] * rows[1] + gates[:, D_MODEL:] * rows[2]]

    (merged,) = _rowwise("merge", merge, T, tT, rows=[z_gate, br_a, br_b], consts=[W["b_gate"]],
                         outs=[("row", D_MODEL, BF16)])
    x1 = _mm("mix_out", merged, W["w_out"], "nn", add=x)

    (h2,) = _rowwise("norm_ffn", lambda i, n, r, pv, nx, c: [_rms_fwd(r[0], c[0])], T, tT,
                     rows=[x1], consts=[W["g_ffn"]], outs=[("row", D_MODEL, BF16)])
    u = _mm("ffn_up", h2, W["w_up"], "nn")

    def conv_core(i, rows, prevs, consts):
        uu, cw, cb = rows[0], consts[0], consts[1]
        u1 = _shift_down(uu, prevs[0], i, 1)
        u2 = _shift_down(uu, prevs[0], i, 2)
        uc = cb + cw[0:1] * uu + cw[1:2] * u1 + cw[2:3] * u2
        return uc[:, :D_FF], uc[:, D_FF:], u1, u2

    def glu(i, n, rows, prevs, nx, consts):
        gate, val, _, _ = conv_core(i, rows, prevs, consts)
        return [_gelu(gate) * val]

    tF = 128
    (act,) = _rowwise("conv_glu", glu, T, tF, rows=[u], prevs=[u], consts=[W["conv_w"], W["conv_b"]],
                      outs=[("row", D_FF, BF16)])
    x2 = _mm("ffn_down", act, W["w_down"], "nn", add=x1)

    (h3,) = _rowwise("norm_ple", lambda i, n, r, pv, nx, c: [_rms_fwd(r[0], c[0])], T, tT,
                     rows=[x2], consts=[W["g_ple"]], outs=[("row", D_MODEL, BF16)])
    z_ple = _mm("ple_gate", h3, W["w_ple_gate"], "nn")
    e_ple = _mm("ple_emb", p, W["w_ple"], "nn")

    def head(i, n, rows, pv, nx, consts):
        x2_, z, e, tgt = rows
        pg = _sigmoid(z)
        x3 = x2_ + pg * e
        y = _rms_fwd(x3, consts[0])
        err = y - tgt
        loss = 0.5 * jnp.sum(jnp.sum(err * err, axis=1, keepdims=True) * (1.0 / D_MODEL), axis=0, keepdims=True)
        dy = err * (1.0 / D_MODEL)
        dx3, dgf = _rms_bwd(x3, consts[0], dy)
        return [dx3, dx3 * pg, dx3 * e * pg * (1.0 - pg), jnp.broadcast_to(loss, (1, LANES)), _colsum(dgf)]

    dx3, de, dz, loss_acc, G["g_final"] = _rowwise(
        "loss_head", head, T, tT, rows=[x2, z_ple, e_ple, target], consts=[W["g_final"].reshape(1, D_MODEL)],
        outs=[("row", D_MODEL, F32), ("row", D_MODEL, BF16), ("row", D_MODEL, BF16), ("acc", (1, LANES)),
              ("acc", (1, D_MODEL))])
    G["w_ple"] = _mm("d_w_ple", p, de, "tn")
    G["w_ple_gate"] = _mm("d_w_ple_gate", h3, dz, "tn")
    dh3 = _mm("d_h3", dz, W["w_ple_gate"], "nt")

    def norm_bwd(i, n, rows, pv, nx, consts):
        dx, dg = _rms_bwd(rows[0], consts[0], rows[1])
        return [rows[2] + dx, _colsum(dg)]

    dx2, G["g_ple"] = _rowwise("d_norm_ple", norm_bwd, T, tT, rows=[x2, dh3, dx3], consts=[W["g_ple"]],
                               outs=[("row", D_MODEL, F32), ("acc", (1, D_MODEL))])

    dact = _mm("d_act", dx2, W["w_down"], "nt")
    G["w_down"] = _mm("d_w_down", act, dx2, "tn")

    def glu_grad(gate, val, da):
        act_, slope = _gelu_and_grad(gate)
        return jnp.concatenate([da * val * slope, da * act_], axis=1)

    def glu_bwd(i, n, rows, prevs, nexts, consts):
        uu, da = rows
        cw = consts[0]
        gate, val, u1, u2 = conv_core(i, rows, prevs, consts)
        duc = glu_grad(gate, val, da)
        dcw = jnp.concatenate([_colsum(duc * uu), _colsum(duc * u1), _colsum(duc * u2)], axis=0)
        gate_n, val_n, _, _ = conv_core(1, [nexts[0]], [uu[tF - SUBLANES:]], consts)
        duc_n = glu_grad(gate_n, val_n, nexts[1])
        du = (cw[0:1] * duc + cw[1:2] * _shift_up(duc, duc_n, i, n, 1) + cw[2:3] * _shift_up(duc, duc_n, i, n, 2))
        return [du, _colsum(duc), dcw]

    du, G["conv_b"], G["conv_w"] = _rowwise(
        "d_conv_glu", glu_bwd, T, tF, rows=[u, dact], prevs=[u], nexts=[u, dact],
        consts=[W["conv_w"], W["conv_b"]],
        outs=[("row", 2 * D_FF, BF16), ("acc", (1, 2 * D_FF)), ("acc", (3, 2 * D_FF))])
    G["w_up"] = _mm("d_w_up", h2, du, "tn")
    dh2 = _mm("d_h2", du, W["w_up"], "nt")
    dx1, G["g_ffn"] = _rowwise("d_norm_ffn", norm_bwd, T, tT, rows=[x1, dh2, dx2], consts=[W["g_ffn"]],
                               outs=[("row", D_MODEL, F32), ("acc", (1, D_MODEL))])

    b_gate = W["b_gate"]
    if early_grads is not None:
        b_gate = b_gate + early_grads(G, 0)[0:1, 0:1]
    dmerged = _mm("d_merged", dx1, W["w_out"], "nt")
    G["w_out"] = _mm("d_w_out", merged, dx1, "tn")

    def merge_bwd(i, n, rows, pv, nx, consts):
        z, a_, b_, dm = rows
        gates = _sigmoid(z + consts[0])
        ga, gb = gates[:, :D_MODEL], gates[:, D_MODEL:]
        dz_ = jnp.concatenate([dm * a_ * ga * (1.0 - ga), dm * b_ * gb * (1.0 - gb)], axis=1)
        return [dm * ga, dm * gb, dz_, _colsum(dz_)]

    d_br_a, d_br_b, dz_gate, G["b_gate"] = _rowwise(
        "d_merge", merge_bwd, T, tT, rows=[z_gate, br_a, br_b, dmerged], consts=[b_gate],
        outs=[("row", D_MODEL, BF16), ("row", D_MODEL, BF16), ("row", 2 * D_MODEL, BF16), ("acc", (1, 2 * D_MODEL))])
    G["w_branch_a"] = _mm("d_w_branch_a", y_a, d_br_a, "tn")
    G["w_branch_b"] = _mm("d_w_branch_b", y_b, d_br_b, "tn")
    G["w_gate"] = _mm("d_w_gate", h, dz_gate, "tn")
    if early_grads is not None:
        post_consts = [post_consts[0] + early_grads(G, 1)[0:1, 0:1]] + post_consts[1:]
    dy_a = _mm("d_y_a", d_br_a, W["w_branch_a"], "nt")
    dy_b = _mm("d_y_b", d_br_b, W["w_branch_b"], "nt")

    def att_comb_bwd(i, n, rows, pv, nx, consts):
        os_, ls, dy = rows[0:3], rows[3:6], rows[6]
        wts = comb_weights(ls)
        dws = [_segsum(dy * o_, consts[0]) for o_ in os_]
        mix = wts[0] * dws[0] + wts[1] * dws[1] + wts[2] * dws[2]
        return [wts[g_] * dy for g_ in range(3)] + [wts[g_] * (dws[g_] - mix) for g_ in range(3)]

    comb = _rowwise("d_att_combine", att_comb_bwd, T, tT,
                    rows=[att[0][0], att[1][0], att[2][0], att[0][1], att[1][1], att[2][1], dy_b], consts=[bd256],
                    outs=[("row", ATT_GROUP_WIDTH, F32)] * 6)
    dqkv = [_att_bwd(p_att, att[g][0], att[g][1], comb[g], comb[3 + g], g) for g in range(3)]
    dp_att = jnp.concatenate([dqkv[g][part] for part in range(3) for g in range(3)], axis=1).astype(BF16)

    def rw_post_bwd(i, n, rows, pv, nx, consts):
        y, r, k2, v, gate, dya = rows
        ln_g, ln_b, rk, bd = consts
        q = rw_post_core(rows, consts)
        dpre = dya * gate
        dgate = dya * q["pre"]
        dyn = dpre * ln_g
        inv = 1.0 / RW_HEAD_DIM
        dy_scan = q["rstd"] * (dyn - _segsum(dyn, bd) * inv - q["yn"] * (_segsum(dyn * q["yn"], bd) * inv))
        ds = _segsum(dpre * v, bd)
        return [dy_scan, dgate, ds * k2 * rk, ds * r * rk, dpre * q["s"],
                _colsum(dpre * q["yn"]), _colsum(dpre), _colsum(ds * r * k2)]

    dy_scan, dgate, dr_b, dk2_b, dv_b, G["rw_ln_g"], G["rw_ln_b"], d_rk = _rowwise(
        "d_rwkv_post", rw_post_bwd, T, tT, rows=[y_scan, r_s, k_s, v_s, gate_s, dy_a], consts=post_consts,
        outs=[("row", RW_WIDTH, F32)] * 5 + [("acc", (1, RW_WIDTH))] * 3)
    G["rw_r_k"] = d_rk.reshape(RW_HEADS, RW_HEAD_DIM)

    dr_s, dw_s, dk_s, da_s, db_s, dv_s = _rwkv_chunk_bwd(r_s, w_s, k_s, a_s, b_s, v_s, dy_scan, s0_s, tinv_s, w1_s,
                                                         a2_s, w2_s, sa_s)

    def rw_pre_bwd(i, n, rows, prevs, nx, consts):
        q = rw_pre_core(i, rows, prevs, consts)
        (mrkv, mwa, mxg, w0, a0, k_k, k_a, wup, aup, gup, bd) = consts
        dr, dlogdecay, dk2, dv, dav, dbv, dgate_ = rows[3:10]
        dr = dr + rows[10]
        dk2 = dk2 + rows[11]
        dv = dv + rows[12]
        a, k, kk = q["a"], q["k"], q["kk"]
        dk = dk2 * (1.0 + (a - 1.0) * k_a)
        da = dk2 * k * k_a + dbv * kk
        dkk = dbv * a - dav
        live = q["nrm"] > 1e-12
        dkkp = jnp.where(live, dkk - kk * _segsum(dkk * kk, bd), dkk) / q["nrm_c"]
        dk = dk + dkkp * k_k
        dlw = dlogdecay * q["log_decay"] * _sigmoid(-q["lw"])
        dla = da * a * (1.0 - a)
        nt = (((1,), (1,)), ((), ()))
        dtw = lax.dot_general(dlw.astype(BF16), wup.astype(BF16), nt, preferred_element_type=F32)
        dxa = lax.dot_general(dla.astype(BF16), aup.astype(BF16), nt, preferred_element_type=F32)
        dm_wa = dtw * (1.0 - q["tw"] * q["tw"]) + dxa
        dsg = lax.dot_general(dgate_.astype(BF16), gup.astype(BF16), nt, preferred_element_type=F32)
        dm_xg = dsg * q["sg"] * (1.0 - q["sg"])
        dm_rkv = jnp.concatenate([dr, dk, dv], axis=1)
        prkv, pwa, pxg = rows[:3]
        dmu = jnp.concatenate([_colsum(dm_rkv * (_shift_down(prkv, prevs[0], i, 1) - prkv)),
                               _colsum(dm_wa * (_shift_down(pwa, prevs[1], i, 1) - pwa)),
                               _colsum(dm_xg * (_shift_down(pxg, prevs[2], i, 1) - pxg))], axis=1)
        return [dm_rkv, dm_wa, dm_xg, dlw, dla, q["tw"], q["m_wa"], q["sg"], dmu,
                _colsum(dlw), _colsum(dla), _colsum(dkkp * k), _colsum(dk2 * k * (a - 1.0))]

    (dm_rkv, dm_wa, dm_xg, dlw, dla, tw_s, mwa_s, sg_s, G["rw_mu"], G["rw_w0"], G["rw_a0"], G["rw_k_k"],
     G["rw_k_a"]) = _rowwise(
        "d_rwkv_pre", rw_pre_bwd, T, tT,
        rows=[p_rkv, p_wa, p_xg, dr_s, dw_s, dk_s, dv_s, da_s, db_s, dgate, dr_b, dk2_b, dv_b],
        prevs=[p_rkv, p_wa, p_xg], consts=pre_consts,
        outs=[("row", RKV, F32), ("row", WA, F32), ("row", XG, F32), ("row", RW_WIDTH, BF16),
              ("row", RW_WIDTH, BF16), ("row", WA, BF16), ("row", WA, BF16), ("row", XG, BF16),
              ("acc", (1, RW_COLS))] + [("acc", (1, RW_WIDTH))] * 4)
    G["rw_w_up"] = _mm("d_rw_w_up", tw_s, dlw, "tn")[:64]
    G["rw_a_up"] = _mm("d_rw_a_up", mwa_s, dla, "tn")[64:]
    G["rw_g_up"] = _mm("d_rw_g_up", sg_s, dgate, "tn")

    def shift_bwd(i, n, rows, pv, nexts, consts):
        return [rows[j] * (1.0 - consts[j]) + _shift_up(rows[j], nexts[j], i, n, 1) * consts[j] for j in range(3)]

    dp_rkv, dp_wa, dp_xg = _rowwise(
        "d_token_shift", shift_bwd, T, tT, rows=[dm_rkv, dm_wa, dm_xg], nexts=[dm_rkv, dm_wa, dm_xg],
        consts=[mu_rkv, mu_wa, mu_xg], outs=[("row", RKV, BF16), ("row", WA, BF16), ("row", XG, BF16)])

    G["w_in"] = jnp.concatenate([_mm("d_w_rkv", h, dp_rkv, "tn"), _mm("d_w_wa", h, dp_wa, "tn"),
                                 _mm("d_w_xg", h, dp_xg, "tn"), _mm("d_w_att", h, dp_att, "tn", tn=768)], axis=1)
    dh = _mm("d_h_gate", dz_gate, W["w_gate"], "nt")
    dh = _mm("d_h_rkv", dp_rkv, w_rkv, "nt", add=dh)
    dh = _mm("d_h_wa", dp_wa, w_wa, "nt", add=dh)
    dh = _mm("d_h_xg", dp_xg, w_xg, "nt", add=dh)
    dh = _mm("d_h_att", dp_att, w_att, "nt", add=dh)
    dx, G["g_mix"] = _rowwise("d_norm_mix", norm_bwd, T, tT, rows=[x, dh, dx1], consts=[W["g_mix"]],
                              outs=[("row", D_MODEL, F32), ("acc", (1, D_MODEL))])
    return loss_acc[:, :1], dx, G


HBM_SPEC = pl.BlockSpec(memory_space=pltpu.HBM)


def _place():
    x, y, c = lax.axis_index("x"), lax.axis_index("y"), lax.axis_index("c")
    return x, y, c, [(1 - x, y), (x, 1 - y), (1 - x, 1 - y)]


def _remote(src, dst, send_sems, recv_sems, k, to):
    return pltpu.make_async_remote_copy(src_ref=src, dst_ref=dst, send_sem=send_sems.at[k], recv_sem=recv_sems.at[k],
                                        device_id=to, device_id_type=MESH)


ROW_ALIGN = 16


def _splits(rows):
    return rows % (2 * ROW_ALIGN) == 0


def _half_rows(ref_rows, c, first):
    half = ref_rows // 2
    which = c if first else 1 - c
    return pl.ds(pl.multiple_of(which * half, ROW_ALIGN), half)


def _gather_chips(shards):
    n = len(shards)
    split = [_splits(s.shape[0]) for s in shards]

    def body(*refs):
        w_refs, out_refs = refs[:n], refs[n:2 * n]
        send_sems, recv_sems = refs[2 * n:]
        x, y, c, chips = _place()
        me = 2 * x + y
        sends, passed = [], []
        for i in range(n):
            for j, (px, py) in enumerate(chips):
                if split[i]:
                    mine = _half_rows(w_refs[i].shape[0], c, True)
                    cp = _remote(w_refs[i].at[mine], out_refs[i].at[me, mine], send_sems, recv_sems, 6 * i + j,
                                 (px, py, c))
                else:
                    cp = _remote(w_refs[i], out_refs[i].at[me], send_sems, recv_sems, 6 * i + j, (px, py, c))
                cp.start()
                sends.append(cp)
        for i in range(n):
            for j, (px, py) in enumerate(chips):
                if split[i]:
                    landed = out_refs[i].at[2 * px + py, _half_rows(w_refs[i].shape[0], c, True)]
                    _remote(landed, landed, send_sems, recv_sems, 6 * i + j, (px, py, c)).wait_recv()
                    cp = _remote(landed, landed, send_sems, recv_sems, 6 * i + 3 + j, (x, y, 1 - c))
                    cp.start()
                    passed.append(cp)
                else:
                    landed = out_refs[i].at[2 * px + py]
                    _remote(landed, landed, send_sems, recv_sems, 6 * i + j, (px, py, c)).wait_recv()
        for i in range(n):
            if split[i]:
                for j, (px, py) in enumerate(chips):
                    landed = out_refs[i].at[2 * px + py, _half_rows(w_refs[i].shape[0], c, False)]
                    _remote(landed, landed, send_sems, recv_sems, 6 * i + 3 + j, (x, y, 1 - c)).wait_recv()
        for cp in sends + passed:
            cp.wait_send()

    outs = pl.pallas_call(
        body, name="gather_weights", in_specs=[HBM_SPEC] * n, out_specs=[HBM_SPEC] * n,
        out_shape=[jax.ShapeDtypeStruct((N_CHIPS,) + s.shape, s.dtype) for s in shards],
        scratch_shapes=[pltpu.SemaphoreType.DMA((6 * n,)), pltpu.SemaphoreType.DMA((6 * n,))],
    )(*shards)
    me = 2 * lax.axis_index("x") + lax.axis_index("y")
    return [lax.dynamic_update_slice(o, s[None], (me, 0, 0)) for o, s in zip(outs, shards, strict=True)]


def _swap_halves(name, gs):
    n = len(gs)

    def body(*refs):
        g_refs, out_refs = refs[:n], refs[n:2 * n]
        send_sems, recv_sems = refs[2 * n:]
        x, y, c, _ = _place()
        cps = []
        for i in range(n):
            theirs = _half_rows(g_refs[i].shape[1], c, False)
            cp = _remote(g_refs[i].at[:, theirs, :], out_refs[i], send_sems, recv_sems, i, (x, y, 1 - c))
            cp.start()
            cps.append(cp)
        for cp in cps:
            cp.wait()

    return pl.pallas_call(
        body, name=name, in_specs=[HBM_SPEC] * n, out_specs=[HBM_SPEC] * n,
        out_shape=[jax.ShapeDtypeStruct((N_CHIPS, g.shape[1] // 2, g.shape[2]), g.dtype) for g in gs],
        scratch_shapes=[pltpu.SemaphoreType.DMA((n,)), pltpu.SemaphoreType.DMA((n,))],
    )(*gs)


def _scatter_chips(parts):
    n = len(parts)

    def body(*refs):
        p_refs, out_refs = refs[:n], refs[n:2 * n]
        send_sems, recv_sems = refs[2 * n:]
        x, y, c, chips = _place()
        me = 2 * x + y
        sends = []
        for i in range(n):
            for j, (px, py) in enumerate(chips):
                cp = _remote(p_refs[i].at[2 * px + py], out_refs[i].at[me], send_sems, recv_sems, 3 * i + j,
                             (px, py, c))
                cp.start()
                sends.append(cp)
        for i in range(n):
            for j, (px, py) in enumerate(chips):
                landed = out_refs[i].at[2 * px + py]
                _remote(landed, landed, send_sems, recv_sems, 3 * i + j, (px, py, c)).wait_recv()
        for cp in sends:
            cp.wait_send()

    outs = pl.pallas_call(
        body, name="scatter_grads", in_specs=[HBM_SPEC] * n, out_specs=[HBM_SPEC] * n,
        out_shape=[jax.ShapeDtypeStruct(p.shape, p.dtype) for p in parts],
        scratch_shapes=[pltpu.SemaphoreType.DMA((3 * n,)), pltpu.SemaphoreType.DMA((3 * n,))],
    )(*parts)
    me = 2 * lax.axis_index("x") + lax.axis_index("y")
    own = [lax.dynamic_slice_in_dim(p, me, 1, axis=0) for p in parts]
    return [lax.dynamic_update_slice(o, s, (me, 0, 0)) for o, s in zip(outs, own, strict=True)]


def _join_halves(reds):
    n = len(reds)

    def body(*refs):
        r_refs, out_refs = refs[:n], refs[n:2 * n]
        send_sems, recv_sems = refs[2 * n:]
        x, y, c, _ = _place()
        cps = []
        for i in range(n):
            mine = _half_rows(out_refs[i].shape[0], c, True)
            cp = _remote(r_refs[i], out_refs[i].at[mine], send_sems, recv_sems, i, (x, y, 1 - c))
            cp.start()
            cps.append(cp)
        for cp in cps:
            cp.wait()

    outs = pl.pallas_call(
        body, name="join_halves", in_specs=[HBM_SPEC] * n, out_specs=[HBM_SPEC] * n,
        out_shape=[jax.ShapeDtypeStruct((2 * r.shape[0], r.shape[1]), r.dtype) for r in reds],
        scratch_shapes=[pltpu.SemaphoreType.DMA((n,)), pltpu.SemaphoreType.DMA((n,))],
    )(*reds)
    c = lax.axis_index("c")
    return [lax.dynamic_update_slice(o, r, (c * r.shape[0], 0)) for o, r in zip(outs, reds, strict=True)]


def _gather_all(vec):
    R = vec.shape[0]

    def body(v_ref, out_ref, send_sems, recv_sems, local_sem):
        x, y, c, _ = _place()
        me = 4 * x + 2 * y + c
        local = pltpu.make_async_copy(v_ref, out_ref.at[me], local_sem)
        local.start()
        peers = [(x ^ (k >> 2), y ^ ((k >> 1) & 1), c ^ (k & 1)) for k in range(1, N_DEV)]
        sends = [_remote(v_ref, out_ref.at[me], send_sems, recv_sems, k, to) for k, to in enumerate(peers)]
        for cp in sends:
            cp.start()
        for k, (px, py, pc) in enumerate(peers):
            landed = out_ref.at[4 * px + 2 * py + pc]
            _remote(landed, landed, send_sems, recv_sems, k, (px, py, pc)).wait_recv()
        for cp in sends:
            cp.wait_send()
        local.wait()

    return pl.pallas_call(
        body, name="gather_small", in_specs=[HBM_SPEC], out_specs=HBM_SPEC,
        out_shape=jax.ShapeDtypeStruct((N_DEV, R, LANES), vec.dtype),
        scratch_shapes=[pltpu.SemaphoreType.DMA((7,)), pltpu.SemaphoreType.DMA((7,)), pltpu.SemaphoreType.DMA],
    )(vec)


SEM_SPEC = pl.BlockSpec(memory_space=pltpu.SEMAPHORE)
DATAFLOW = pltpu.SideEffectType.DATAFLOW_SIDE_EFFECTING


def _travel_copies(mode, src_refs, land_refs, send_sems, recv_sems):
    x, y, c, chips = _place()
    me = 2 * x + y
    pairs = []
    for i, (src, land) in enumerate(zip(src_refs, land_refs, strict=True)):
        for j, (px, py) in enumerate(chips):
            peer = 2 * px + py
            if mode == "scatter":
                mine, there, here = src.at[peer], land.at[me], land.at[peer]
            elif _splits(src.shape[0]):
                rows = _half_rows(src.shape[0], c, True)
                mine, there, here = src.at[rows], land.at[me, rows], land.at[peer, rows]
            else:
                mine, there, here = src, land.at[me], land.at[peer]
            send = _remote(mine, there, send_sems, recv_sems, 3 * i + j, (px, py, c))
            arrival = _remote(mine, here, send_sems, recv_sems, 3 * i + j, (px, py, c))
            pairs.append((send, arrival))
    return pairs


def _share_halves(name, lands):
    idx = [i for i, a in enumerate(lands) if _splits(a.shape[1])]
    n = len(idx)

    def body(*refs):
        in_refs, out_refs = refs[:n], refs[n:2 * n]
        send_sems, recv_sems = refs[2 * n:]
        x, y, c, chips = _place()
        cps = []
        for i, (src, dst) in enumerate(zip(in_refs, out_refs, strict=True)):
            for j, (px, py) in enumerate(chips):
                mine = _half_rows(src.shape[1], c, True)
                cp = _remote(src.at[2 * px + py, mine], dst.at[2 * px + py, mine], send_sems, recv_sems, 3 * i + j,
                             (x, y, 1 - c))
                cp.start()
                cps.append(cp)
        for i, dst in enumerate(out_refs):
            for j, (px, py) in enumerate(chips):
                theirs = dst.at[2 * px + py, _half_rows(dst.shape[1], c, False)]
                _remote(theirs, theirs, send_sems, recv_sems, 3 * i + j, (x, y, 1 - c)).wait_recv()
        for cp in cps:
            cp.wait_send()

    outs = pl.pallas_call(
        body, name=name, in_specs=[HBM_SPEC] * n, out_specs=[HBM_SPEC] * n,
        out_shape=[jax.ShapeDtypeStruct(lands[i].shape, lands[i].dtype) for i in idx],
        input_output_aliases={i: i for i in range(n)},
        scratch_shapes=[pltpu.SemaphoreType.DMA((3 * n,)), pltpu.SemaphoreType.DMA((3 * n,))],
    )(*[lands[i] for i in idx])
    done = list(lands)
    for i, o in zip(idx, outs, strict=True):
        done[i] = o
    return done


def _travel_start(name, mode, srcs):
    n = len(srcs)
    lands = [lax.empty((N_CHIPS,) + (s.shape if mode == "gather" else s.shape[1:]), s.dtype) for s in srcs]

    def body(*refs):
        src_refs, land_refs = refs[:n], refs[n:2 * n]
        send_sems, recv_sems = refs[2 * n], refs[2 * n + 1]
        token = refs[-1]
        for send, _ in _travel_copies(mode, src_refs, land_refs, send_sems, recv_sems):
            send.start()
        token[...] = jnp.zeros_like(token)

    hbm = lambda a: pltpu.HBM(a.shape, a.dtype)
    outs = pl.pallas_call(
        body, name=name,
        out_shape=(pltpu.SemaphoreType.DMA((3 * n,)), pltpu.SemaphoreType.DMA((3 * n,)), *[hbm(s) for s in srcs],
                   *[hbm(a) for a in lands], jax.ShapeDtypeStruct((SUBLANES, LANES), F32)),
        in_specs=[HBM_SPEC] * (2 * n),
        out_specs=(SEM_SPEC, SEM_SPEC, *[HBM_SPEC] * (2 * n), pl.BlockSpec(memory_space=pltpu.VMEM)),
        input_output_aliases={i: 2 + i for i in range(2 * n)},
        compiler_params=pltpu.CompilerParams(has_side_effects=DATAFLOW),
    )(*[pltpu.with_memory_space_constraint(a, pltpu.HBM) for a in list(srcs) + lands])
    return outs[0], outs[1], list(outs[2:2 + n]), list(outs[2 + n:2 + 2 * n]), outs[-1]


def _travel_wait(name, mode, send_sems, recv_sems, srcs, lands, after):
    n = len(srcs)

    def body(*refs):
        src_refs, land_refs = refs[:n], refs[n:2 * n]
        send_sems_, recv_sems_ = refs[2 * n], refs[2 * n + 1]
        for send, arrival in _travel_copies(mode, src_refs, land_refs, send_sems_, recv_sems_):
            send.wait_send()
            arrival.wait_recv()

    hbm = lambda a: pltpu.HBM(a.shape, a.dtype)
    outs = pl.pallas_call(
        body, name=name, out_shape=tuple(hbm(a) for a in list(srcs) + list(lands)),
        in_specs=[HBM_SPEC] * (2 * n) + [SEM_SPEC, SEM_SPEC, pl.BlockSpec(memory_space=pl.ANY)],
        out_specs=tuple([HBM_SPEC] * (2 * n)), input_output_aliases={i: i for i in range(2 * n)},
        compiler_params=pltpu.CompilerParams(has_side_effects=DATAFLOW),
    )(*srcs, *lands, send_sems, recv_sems, after)
    me = 2 * lax.axis_index("x") + lax.axis_index("y")
    own = [s[None] if mode == "gather" else lax.dynamic_slice_in_dim(s, me, 1, axis=0) for s in outs[:n]]
    return [lax.dynamic_update_slice(a, o, (me,) + (0,) * (a.ndim - 1)) for a, o in zip(outs[n:], own, strict=True)]


SUM_TILE_BYTES = 4 * 1024 * 1024


def _sum_rows(half, cols):
    best = ROW_ALIGN
    for t in range(ROW_ALIGN, half + 1, ROW_ALIGN):
        if half % t == 0 and N_CHIPS * t * cols * 4 <= SUM_TILE_BYTES:
            best = t
    return best


def _sum_cores(name, g, theirs, core):
    _, R, C = g.shape
    half = R // 2
    tr = _sum_rows(half, C)
    nb = half // tr

    def body(core_ref, g_ref, t_ref, o_ref):
        o_ref[...] = (g_ref[...] + t_ref[...]).astype(o_ref.dtype)

    grid_spec = pltpu.PrefetchScalarGridSpec(
        num_scalar_prefetch=1, grid=(nb,),
        in_specs=[pl.BlockSpec((N_CHIPS, tr, C), lambda i, core_ref: (0, core_ref[0] * nb + i, 0)),
                  pl.BlockSpec((N_CHIPS, tr, C), lambda i, core_ref: (0, i, 0))],
        out_specs=pl.BlockSpec((N_CHIPS, tr, C), lambda i, core_ref: (0, i, 0)))
    return pl.pallas_call(
        body, name=name, grid_spec=grid_spec, out_shape=jax.ShapeDtypeStruct((N_CHIPS, half, C), BF16),
        compiler_params=_params(("parallel",)),
    )(core, g, theirs)


def _sum_chips(name, parts):
    _, H, C = parts.shape
    tr = _sum_rows(H, C)

    def body(p_ref, o_ref):
        acc = p_ref[0].astype(F32)
        for k in range(1, N_CHIPS):
            acc = acc + p_ref[k].astype(F32)
        o_ref[...] = acc

    return pl.pallas_call(
        body, name=name, grid=(H // tr,),
        in_specs=[pl.BlockSpec((N_CHIPS, tr, C), lambda i: (0, i, 0))],
        out_specs=pl.BlockSpec((tr, C), lambda i: (i, 0)),
        out_shape=jax.ShapeDtypeStruct((H, C), F32),
        compiler_params=_params(("parallel",)),
    )(parts)


def _adamw_math(w, g, m, v):
    m = ADAM_B1 * m + (1.0 - ADAM_B1) * g
    v = ADAM_B2 * v + (1.0 - ADAM_B2) * (g * g)
    m_hat = m / (1.0 - ADAM_B1 ** ADAM_STEP)
    v_hat = v / (1.0 - ADAM_B2 ** ADAM_STEP)
    delta = -ADAM_LR * (m_hat / (jnp.sqrt(v_hat) + ADAM_EPS) + ADAM_WD * w)
    return delta, m, v


def _adamw(name, w, g, m, v):
    R, C = w.shape
    tr = R
    if R % SUBLANES == 0:
        for cand in range(SUBLANES, min(R, 256) + 1, SUBLANES):
            if R % cand == 0:
                tr = cand

    def body(w_ref, g_ref, m_ref, v_ref, d_ref, nm_ref, nv_ref):
        d, nm, nv = _adamw_math(w_ref[...], g_ref[...], m_ref[...], v_ref[...])
        d_ref[...] = d
        nm_ref[...] = nm
        nv_ref[...] = nv

    spec = pl.BlockSpec((tr, C), lambda i: (i, 0))
    shape = jax.ShapeDtypeStruct((R, C), F32)
    return pl.pallas_call(
        body, name=name, grid=(R // tr,), in_specs=[spec] * 4, out_specs=[spec] * 3, out_shape=[shape] * 3,
        compiler_params=_params(("parallel",)),
    )(w, g, m, v)


def _adamw_small(parts, w, m, v):
    n = parts.shape[0]

    def body(p_ref, w_ref, m_ref, v_ref, g_ref, d_ref, nm_ref, nv_ref):
        g = p_ref[0]
        for k in range(1, n):
            g = g + p_ref[k]
        d, nm, nv = _adamw_math(w_ref[...], g, m_ref[...], v_ref[...])
        g_ref[...] = g
        d_ref[...] = d
        nm_ref[...] = nm
        nv_ref[...] = nv

    shape = jax.ShapeDtypeStruct(w.shape, F32)
    return pl.pallas_call(body, name="adamw_small", out_shape=[shape] * 4, compiler_params=_params())(parts, w, m, v)


WEIGHTS = ['g_mix', 'w_in', 'rw_mu', 'rw_w0', 'rw_w_up', 'rw_a0', 'rw_a_up', 'rw_g_up', 'rw_k_k', 'rw_k_a',
           'rw_r_k', 'rw_ln_g', 'rw_ln_b', 'w_branch_a', 'w_branch_b', 'w_gate', 'b_gate', 'w_out', 'g_ffn', 'w_up',
           'conv_w', 'conv_b', 'w_down', 'g_ple', 'w_ple_gate', 'w_ple', 'g_final']
ARG_NAMES = (['x', 'p'] + WEIGHTS + ['loss_target'] + ['m_' + n for n in WEIGHTS] + ['v_' + n for n in WEIGHTS])
SHARDED = {'w_in': 1, 'rw_w_up': 1, 'rw_a_up': 1, 'rw_g_up': 1, 'w_branch_a': 1, 'w_branch_b': 1, 'w_gate': 1,
           'w_out': 0, 'w_up': 1, 'conv_w': 1, 'w_down': 0, 'w_ple_gate': 0, 'w_ple': 1}
SMALL = [n for n in WEIGHTS if n not in SHARDED]
WHOLE = ['conv_w']
FIRST_USED = ['w_in', 'rw_w_up', 'rw_a_up', 'rw_g_up', 'w_gate']
FIRST_DONE = [['w_up', 'w_down', 'w_ple_gate', 'w_ple'], ['w_out', 'w_branch_a', 'w_branch_b', 'w_gate']]
SPLIT = [n for n in SHARDED if n not in WHOLE]
PACK_ALIGN = SUBLANES * LANES


def _pack_rows(flat_parts):
    flat = jnp.concatenate(flat_parts, axis=1)
    n = flat.shape[1]
    padded = -(-n // PACK_ALIGN) * PACK_ALIGN
    flat = jnp.pad(flat, ((0, 0), (0, padded - n)))
    return flat.reshape(padded // LANES, LANES)


def _full_from_shards(stack, axis):
    _, R, C = stack.shape
    if axis == 0:
        return stack.reshape(N_CHIPS * R, C)
    return stack.transpose(1, 0, 2).reshape(R, N_CHIPS * C)


def _shards_from_full(full, axis):
    R, C = full.shape
    if axis == 0:
        return full.reshape(N_CHIPS, R // N_CHIPS, C)
    return full.reshape(R, N_CHIPS, C // N_CHIPS).transpose(1, 0, 2)


def kernel(x, p, g_mix, w_in, rw_mu, rw_w0, rw_w_up, rw_a0, rw_a_up, rw_g_up, rw_k_k, rw_k_a, rw_r_k, rw_ln_g, rw_ln_b, w_branch_a, w_branch_b, w_gate, b_gate, w_out, g_ffn, w_up, conv_w, conv_b, w_down, g_ple, w_ple_gate, w_ple, g_final, loss_target, m_g_mix, m_w_in, m_rw_mu, m_rw_w0, m_rw_w_up, m_rw_a0, m_rw_a_up, m_rw_g_up, m_rw_k_k, m_rw_k_a, m_rw_r_k, m_rw_ln_g, m_rw_ln_b, m_w_branch_a, m_w_branch_b, m_w_gate, m_b_gate, m_w_out, m_g_ffn, m_w_up, m_conv_w, m_conv_b, m_w_down, m_g_ple, m_w_ple_gate, m_w_ple, m_g_final, v_g_mix, v_w_in, v_rw_mu, v_rw_w0, v_rw_w_up, v_rw_a0, v_rw_a_up, v_rw_g_up, v_rw_k_k, v_rw_k_a, v_rw_r_k, v_rw_ln_g, v_rw_ln_b, v_w_branch_a, v_w_branch_b, v_w_gate, v_b_gate, v_w_out, v_g_ffn, v_w_up, v_conv_w, v_conv_b, v_w_down, v_g_ple, v_w_ple_gate, v_w_ple, v_g_final):
    given = dict(zip(ARG_NAMES, (x, p, g_mix, w_in, rw_mu, rw_w0, rw_w_up, rw_a0, rw_a_up, rw_g_up, rw_k_k, rw_k_a, rw_r_k, rw_ln_g, rw_ln_b, w_branch_a, w_branch_b, w_gate, b_gate, w_out, g_ffn, w_up, conv_w, conv_b, w_down, g_ple, w_ple_gate, w_ple, g_final, loss_target, m_g_mix, m_w_in, m_rw_mu, m_rw_w0, m_rw_w_up, m_rw_a0, m_rw_a_up, m_rw_g_up, m_rw_k_k, m_rw_k_a, m_rw_r_k, m_rw_ln_g, m_rw_ln_b, m_w_branch_a, m_w_branch_b, m_w_gate, m_b_gate, m_w_out, m_g_ffn, m_w_up, m_conv_w, m_conv_b, m_w_down, m_g_ple, m_w_ple_gate, m_w_ple, m_g_final, v_g_mix, v_w_in, v_rw_mu, v_rw_w0, v_rw_w_up, v_rw_a0, v_rw_a_up, v_rw_g_up, v_rw_k_k, v_rw_k_a, v_rw_r_k, v_rw_ln_g, v_rw_ln_b, v_w_branch_a, v_w_branch_b, v_w_gate, v_b_gate, v_w_out, v_g_ffn, v_w_up, v_conv_w, v_conv_b, v_w_down, v_g_ple, v_w_ple_gate, v_w_ple, v_g_final), strict=True))

    def two_d(name, prefix=""):
        a = given[prefix + name]
        if name == "g_final":
            return a.reshape(1, D_MODEL)
        if name == "rw_r_k":
            return a.reshape(1, RW_WIDTH)
        return a[0] if a.ndim == 3 else a

    cast = lambda n: two_d(n) if n in WHOLE else two_d(n).astype(BF16)
    whole = lambda names, stacks: {n: _full_from_shards(g, SHARDED[n]) for n, g in zip(names, stacks, strict=True)}
    late_names = [n for n in SHARDED if n not in FIRST_USED]
    late_sends, late_recvs, late_srcs, late_lands, token = _travel_start(
        "gather_late_start", "gather", [cast(n) for n in late_names])
    W = whole(FIRST_USED, _gather_chips([cast(n) for n in FIRST_USED]))
    for n in SMALL:
        W[n] = two_d(n)
    W["rw_r_k"] = W["rw_r_k"].reshape(RW_HEADS, RW_HEAD_DIM)
    W["g_mix"] = W["g_mix"] + token[0:1, 0:1]

    def late_weights(after):
        lands = _travel_wait("gather_late_wait", "gather", late_sends, late_recvs, late_srcs, late_lands, after)
        return whole(late_names, _share_halves("share_late", lands))

    core = lax.axis_index("c").astype(jnp.int32).reshape(1)
    early_names = [[n for n in SPLIT if n in group] for group in FIRST_DONE]
    rest_names = [n for n in SPLIT if not any(n in group for group in FIRST_DONE)]
    travelling = []

    def core_sums(tag, names, G):
        by_chip = [_shards_from_full(G[n], SHARDED[n]) for n in names]
        theirs = _swap_halves("swap_halves_" + tag, by_chip)
        return [_sum_cores("sum_cores_" + n, g, t, core) for n, g, t in zip(names, by_chip, theirs, strict=True)]

    def early_grads(G, stage):
        sends, recvs, srcs, lands, started = _travel_start(f"scatter_early{stage}_start", "scatter",
                                                           core_sums(f"early{stage}", early_names[stage], G))
        travelling.append((sends, recvs, srcs, lands))
        return started

    loss_part, grad_x, G = _local_step(x[0], p[0, 0], W, loss_target[0], late_weights, early_grads)

    landed = dict(zip(rest_names, _scatter_chips(core_sums("rest", rest_names, G)), strict=True))
    for stage, (sends, recvs, srcs, lands) in enumerate(travelling):
        landed.update(zip(early_names[stage], _travel_wait(f"scatter_early{stage}_wait", "scatter", sends, recvs,
                                                           srcs, lands, landed[rest_names[0]]), strict=True))
    reduced = [_sum_chips("sum_chips_" + n, landed[n]) for n in SPLIT]
    shard_grads = dict(zip(SPLIT, _join_halves(reduced), strict=True))

    small_sizes = {n: two_d(n).shape[1] for n in SMALL}
    n_small = sum(small_sizes.values())
    whole_sizes = {n: G[n].shape[0] * G[n].shape[1] for n in WHOLE}
    n_whole = sum(whole_sizes.values())

    def pack_small(parts, rest):
        return _pack_rows([a.reshape(1, -1) for a in parts] + [rest])

    G["rw_r_k"] = G["rw_r_k"].reshape(1, RW_WIDTH)
    rest = jnp.zeros((1, n_whole + 1), F32)
    all_small = _gather_all(pack_small([G[n] for n in SMALL] + [G[n] for n in WHOLE], loss_part))
    gs, ds, nms, nvs = _adamw_small(all_small, pack_small([two_d(n) for n in SMALL], rest),
                                    pack_small([two_d(n, "m_") for n in SMALL], rest),
                                    pack_small([two_d(n, "v_") for n in SMALL], rest))
    gs, ds, nms, nvs = (a.reshape(-1) for a in (gs, ds, nms, nvs))
    loss = gs[n_small + n_whole]
    chip = 2 * lax.axis_index("x") + lax.axis_index("y")
    off = n_small
    for n in WHOLE:
        full = gs[off:off + whole_sizes[n]].reshape(G[n].shape)
        off += whole_sizes[n]
        width = two_d(n).shape[1]
        shard_grads[n] = lax.dynamic_slice_in_dim(full, chip * width, width, axis=1)

    grads, deltas, new_m, new_v = {}, {}, {}, {}
    for n in SHARDED:
        g = shard_grads[n]
        d, nm, nv = _adamw("adamw_" + n, two_d(n), g, two_d(n, "m_"), two_d(n, "v_"))
        grads[n], deltas[n], new_m[n], new_v[n] = g, d, nm, nv
    off = 0
    for n in SMALL:
        sl = slice(off, off + small_sizes[n])
        off += small_sizes[n]
        grads[n], deltas[n], new_m[n], new_v[n] = gs[sl], ds[sl], nms[sl], nvs[sl]
    outs = [loss, grad_x[None]]
    for table in (grads, deltas, new_m, new_v):
        outs += [table[n].reshape(given[n].shape) for n in WEIGHTS]
    return tuple(outs)
```

```python
import functools
import math

import jax
import jax.numpy as jnp
import numpy as np
from jax import lax
from jax.experimental import pallas as pl
from jax.experimental.pallas import tpu as pltpu

F32 = jnp.float32
BF16 = jnp.bfloat16

D_MODEL = 1024
NORM_EPS = 1e-6
RW_HEADS = 8
RW_HEAD_DIM = 64
RW_WIDTH = 512
RW_LN_EPS = 64e-5
ATT_GROUP_DILATION = (1, 4, 16)
ATT_BLOCK = 128
ATT_HEADS = 12
ATT_HEAD_DIM = 64
ATT_GROUP_WIDTH = 256
ATT_WIDTH = 768
D_FF = 3072

ADAM_LR = 0.001
ADAM_B1 = 0.9
ADAM_B2 = 0.999
ADAM_EPS = 1e-08
ADAM_WD = 0.01
ADAM_STEP = 10

SUBLANES = 8
LANES = 128
VMEM_LIMIT = 56 * 1024 * 1024
N_CHIPS = 4
N_DEV = 8
MESH = pl.DeviceIdType.MESH


def _params(sem=None):
    return pltpu.CompilerParams(dimension_semantics=sem, vmem_limit_bytes=VMEM_LIMIT)


def _pick(dim, pref):
    if dim % LANES != 0 or dim <= pref:
        return dim
    best = LANES
    for t in range(LANES, pref + 1, LANES):
        if dim % t == 0:
            best = t
    return best


def _mm(name, a, b, mode, out_dtype=F32, add=None, tm=1024, tn=1024, tk=1024, out_by_chip=False):
    by_chip = b.ndim == 3
    b_rows, b_cols = (b.shape[1], N_CHIPS * b.shape[2]) if by_chip else b.shape
    if mode == "nn":
        (M, K), (K2, N) = a.shape, (b_rows, b_cols)
    elif mode == "nt":
        (M, K), (N, K2) = a.shape, (b_rows, b_cols)
    else:
        (K, M), (K2, N) = a.shape, (b_rows, b_cols)
    assert K == K2, (name, a.shape, b.shape, mode)
    assert not (by_chip and mode == "tn") and not (out_by_chip and add is not None), name
    tm = _pick(M, tm)
    n_cut, k_cut = out_by_chip or (by_chip and mode == "nn"), by_chip and mode == "nt"
    tn = _pick(N // N_CHIPS, tn) if n_cut else _pick(N, tn)
    tk = _pick(K // N_CHIPS, tk) if k_cut else _pick(K, tk)
    nk = K // tk
    per_n = (N // N_CHIPS) // tn if n_cut else 1
    per_k = (K // N_CHIPS) // tk if k_cut else 1
    if mode == "nn":
        a_spec = pl.BlockSpec((tm, tk), lambda i, j, k: (i, k))
        b_spec = (pl.BlockSpec((None, tk, tn), lambda i, j, k: (j // per_n, k, j % per_n)) if by_chip
                  else pl.BlockSpec((tk, tn), lambda i, j, k: (k, j)))
        dims = (((1,), (0,)), ((), ()))
    elif mode == "nt":
        a_spec = pl.BlockSpec((tm, tk), lambda i, j, k: (i, k))
        b_spec = (pl.BlockSpec((None, tn, tk), lambda i, j, k: (k // per_k, j, k % per_k)) if by_chip
                  else pl.BlockSpec((tn, tk), lambda i, j, k: (j, k)))
        dims = (((1,), (1,)), ((), ()))
    else:
        a_spec = pl.BlockSpec((tk, tm), lambda i, j, k: (k, i))
        b_spec = pl.BlockSpec((tk, tn), lambda i, j, k: (k, j))
        dims = (((0,), (0,)), ((), ()))
    if out_by_chip:
        o_spec = pl.BlockSpec((None, tm, tn), lambda i, j, k: (j // per_n, i, j % per_n))
        out_shape = jax.ShapeDtypeStruct((N_CHIPS, M, N // N_CHIPS), out_dtype)
    else:
        o_spec = pl.BlockSpec((tm, tn), lambda i, j, k: (i, j))
        out_shape = jax.ShapeDtypeStruct((M, N), out_dtype)
    has_add = add is not None

    def body(*refs):
        if has_add:
            a_ref, b_ref, add_ref, o_ref, acc_ref = refs
        else:
            a_ref, b_ref, o_ref, acc_ref = refs
        k = pl.program_id(2)
        part = lax.dot_general(a_ref[...].astype(BF16), b_ref[...].astype(BF16), dims,
                               preferred_element_type=F32)

        @pl.when(k == 0)
        def _():
            acc_ref[...] = part

        @pl.when(k > 0)
        def _():
            acc_ref[...] += part

        @pl.when(k == nk - 1)
        def _():
            res = acc_ref[...]
            if has_add:
                res = res + add_ref[...].astype(F32)
            o_ref[...] = res.astype(o_ref.dtype)

    ins = [a, b] + ([add] if has_add else [])
    in_specs = [a_spec, b_spec] + ([o_spec] if has_add else [])
    return pl.pallas_call(
        body, name=name, grid=(M // tm, N // tn, nk),
        in_specs=in_specs, out_specs=o_spec, out_shape=out_shape,
        scratch_shapes=[pltpu.VMEM((tm, tn), F32)],
        compiler_params=_params(("parallel", "parallel", "arbitrary")),
    )(*ins)


def _rowwise(name, fn, T, tT, rows=(), prevs=(), nexts=(), consts=(), outs=()):
    n = T // tT
    per8 = tT // SUBLANES
    in_specs, ins = [], []
    for arr in rows:
        in_specs.append(pl.BlockSpec((tT, arr.shape[1]), lambda i: (i, 0)))
        ins.append(arr)
    for arr in prevs:
        in_specs.append(pl.BlockSpec((SUBLANES, arr.shape[1]), lambda i: (jnp.maximum(i * per8 - 1, 0), 0)))
        ins.append(arr)
    for arr in nexts:
        in_specs.append(pl.BlockSpec((SUBLANES, arr.shape[1]),
                                     lambda i: (jnp.minimum((i + 1) * per8, T // SUBLANES - 1), 0)))
        ins.append(arr)
    for arr in consts:
        in_specs.append(pl.BlockSpec(arr.shape, lambda i, nd=arr.ndim: (0,) * nd))
        ins.append(arr)
    out_specs, out_shapes = [], []
    for o in outs:
        if o[0] == "row":
            out_specs.append(pl.BlockSpec((tT, o[1]), lambda i: (i, 0)))
            out_shapes.append(jax.ShapeDtypeStruct((T, o[1]), o[2]))
        else:
            out_specs.append(pl.BlockSpec(o[1], lambda i: (0, 0)))
            out_shapes.append(jax.ShapeDtypeStruct(o[1], F32))
    nr, npv, nnx, nc = len(rows), len(prevs), len(nexts), len(consts)
    n_in = nr + npv + nnx + nc

    def body(*refs):
        i = pl.program_id(0)
        vals = [r[...] for r in refs[:n_in]]
        res = fn(i, n, vals[:nr], vals[nr:nr + npv], vals[nr + npv:nr + npv + nnx], vals[nr + npv + nnx:])
        for o, o_ref, val in zip(outs, refs[n_in:], res, strict=True):
            if o[0] == "row":
                o_ref[...] = val.astype(o_ref.dtype)
            else:
                @pl.when(i == 0)
                def _(o_ref=o_ref, val=val):
                    o_ref[...] = val.astype(F32)

                @pl.when(i > 0)
                def _(o_ref=o_ref, val=val):
                    o_ref[...] += val.astype(F32)

    res = pl.pallas_call(
        body, name=name, grid=(n,), in_specs=in_specs, out_specs=out_specs, out_shape=out_shapes,
        compiler_params=_params(("arbitrary",)),
    )(*ins)
    return list(res)


def _shift_down(x, prev8, i, s):
    rolled = pltpu.roll(x, s, 0)
    head = pltpu.roll(prev8, s, 0)
    head = jnp.where(i == 0, jnp.zeros_like(head), head)
    rid = lax.broadcasted_iota(jnp.int32, head.shape, 0)
    first = jnp.where(rid < s, head, rolled[:SUBLANES])
    if x.shape[0] == SUBLANES:
        return first
    return jnp.concatenate([first, rolled[SUBLANES:]], axis=0)


def _shift_up(x, next8, i, n, s):
    tT = x.shape[0]
    rolled = pltpu.roll(x, tT - s, 0)
    tail = pltpu.roll(next8, SUBLANES - s, 0)
    tail = jnp.where(i == n - 1, jnp.zeros_like(tail), tail)
    rid = lax.broadcasted_iota(jnp.int32, tail.shape, 0)
    last = jnp.where(rid >= SUBLANES - s, tail, rolled[tT - SUBLANES:])
    return jnp.concatenate([rolled[:tT - SUBLANES], last], axis=0)


def _colsum(x):
    return jnp.sum(x, axis=0, keepdims=True)


def _segsum(x, bd):
    return jnp.dot(x, bd, precision=lax.Precision.HIGH, preferred_element_type=F32)


def _block_diag_ones(width, seg):
    idx = np.arange(width) // seg
    return jnp.asarray((idx[:, None] == idx[None, :]).astype(np.float32))


def _sigmoid(z):
    return 1.0 / (1.0 + jnp.exp(-z))


def _softplus(z):
    return jnp.maximum(z, 0.0) + jnp.log(1.0 + jnp.exp(-jnp.abs(z)))


def _rms_fwd(x, g):
    r = lax.rsqrt(jnp.mean(x * x, axis=-1, keepdims=True) + NORM_EPS)
    return x * r * g


def _rms_bwd(x, g, dy):
    r = lax.rsqrt(jnp.mean(x * x, axis=-1, keepdims=True) + NORM_EPS)
    gdy = dy * g
    dx = r * (gdy - x * (r * r) * jnp.mean(x * gdy, axis=-1, keepdims=True))
    return dx, dy * x * r


GELU_C = math.sqrt(2.0 / math.pi)


def _gelu(x):
    return 0.5 * x * (1.0 + jnp.tanh(GELU_C * (x + 0.044715 * x * x * x)))


def _gelu_and_grad(x):
    th = jnp.tanh(GELU_C * (x + 0.044715 * x * x * x))
    half = 0.5 * (1.0 + th)
    return x * half, half + 0.5 * x * (1.0 - th * th) * GELU_C * (1.0 + 3.0 * 0.044715 * x * x)


RW_CHUNK = 64
NN = (((1,), (0,)), ((), ()))
NT = (((1,), (1,)), ((), ()))
TN = (((0,), (0,)), ((), ()))


def _hdot(a, b, dims):
    return lax.dot_general(a, b, dims, precision=lax.Precision.HIGH, preferred_element_type=F32)


def _ldot(a, b, dims):
    return lax.dot_general(a.astype(BF16), b.astype(BF16), dims, preferred_element_type=F32)


def _chunk_masks():
    ti = lax.broadcasted_iota(jnp.int32, (RW_CHUNK, RW_CHUNK), 0)
    tj = lax.broadcasted_iota(jnp.int32, (RW_CHUNK, RW_CHUNK), 1)
    return tj <= ti, tj < ti, (ti == tj).astype(F32)


def _head(x, h):
    return x[:, h * RW_HEAD_DIM:(h + 1) * RW_HEAD_DIM]


def _heads(fn):
    return [fn(h) for h in range(RW_HEADS)]


def _chunk_rows(r, lw, k, a, b, incl_f):
    c = _hdot(incl_f, lw, NN)
    e_prev, e_neg, e_pos = jnp.exp(c - lw), jnp.exp(-c), jnp.exp(c)
    return dict(At=a * e_prev, Bt=b * e_neg, Kt=k * e_neg, Rt=r * e_pos, e_prev=e_prev, e_neg=e_neg, e_pos=e_pos)


def _chunk_coeffs(q, incl, strict):
    A1 = _heads(lambda h: jnp.where(strict, _hdot(_head(q["At"], h), _head(q["Bt"], h), NT), 0.0))
    A2 = _heads(lambda h: jnp.where(strict, _hdot(_head(q["At"], h), _head(q["Kt"], h), NT), 0.0))
    W1 = _heads(lambda h: jnp.where(incl, _hdot(_head(q["Rt"], h), _head(q["Bt"], h), NT), 0.0))
    W2 = _heads(lambda h: jnp.where(incl, _hdot(_head(q["Rt"], h), _head(q["Kt"], h), NT), 0.0))
    return A1, A2, W1, W2


def _rwkv_chunk_prep(r, lw, k, a, b, v):
    T = r.shape[0]
    nC = T // RW_CHUNK
    H, N = RW_HEADS, RW_HEAD_DIM

    def body(r_ref, lw_ref, k_ref, a_ref, b_ref, v_ref,
             at_ref, bt_ref, kt_ref, rt_ref, a2v_ref, w2v_ref, ti_ref, w1_ref, a2_ref, w2_ref, pl_ref):
        incl, strict, eye = _chunk_masks()
        q = _chunk_rows(r_ref[...], lw_ref[...], k_ref[...], a_ref[...], b_ref[...], incl.astype(F32))
        at_ref[...], bt_ref[...], kt_ref[...], rt_ref[...] = q["At"], q["Bt"], q["Kt"], q["Rt"]
        pl_ref[0] = jnp.broadcast_to(q["e_pos"][RW_CHUNK - 1:RW_CHUNK, :], (SUBLANES, RW_WIDTH))
        A1, A2, W1, W2 = _chunk_coeffs(q, incl, strict)
        V = v_ref[...]
        a2v_ref[...] = jnp.concatenate(_heads(lambda h: _hdot(A2[h], _head(V, h), NN)), axis=1)
        w2v_ref[...] = jnp.concatenate(_heads(lambda h: _hdot(W2[h], _head(V, h), NN)), axis=1)
        tinv, pw = [eye + m for m in A1], A1
        for _ in range(5):
            pw = [_hdot(m, m, NN) for m in pw]
            tinv = [t + _hdot(t, m, NN) for t, m in zip(tinv, pw, strict=True)]
        for h in range(H):
            ti_ref[0, h] = tinv[h]
            w1_ref[0, h] = W1[h]
            a2_ref[0, h] = A2[h]
            w2_ref[0, h] = W2[h]

    row_spec = pl.BlockSpec((RW_CHUNK, RW_WIDTH), lambda n: (n, 0))
    st_spec = pl.BlockSpec((1, H, N, N), lambda n: (n, 0, 0, 0))
    row_shape = jax.ShapeDtypeStruct((T, RW_WIDTH), F32)
    st_shape = jax.ShapeDtypeStruct((nC, H, N, N), F32)
    return pl.pallas_call(
        body, name="rwkv_chunk_prep", grid=(nC,),
        in_specs=[row_spec] * 6,
        out_specs=[row_spec] * 6 + [st_spec] * 4 + [pl.BlockSpec((1, SUBLANES, RW_WIDTH), lambda n: (n, 0, 0))],
        out_shape=[row_shape] * 6 + [st_shape] * 4 + [jax.ShapeDtypeStruct((nC, SUBLANES, RW_WIDTH), F32)],
        compiler_params=_params(("parallel",)),
    )(r, lw, k, a, b, v)


def _rwkv_chunk_fwd(v, at, bt, kt, rt, a2v, w2v, tinv, w1, plast):
    T = v.shape[0]
    nC = T // RW_CHUNK
    H, N = RW_HEADS, RW_HEAD_DIM

    def body(v_ref, at_ref, bt_ref, kt_ref, rt_ref, a2v_ref, w2v_ref, ti_ref, w1_ref, pl_ref,
             y_ref, sa_ref, s0_ref, S_ref):
        @pl.when(pl.program_id(0) == 0)
        def _():
            S_ref[...] = jnp.zeros_like(S_ref)

        V, At, Bt, Kt, Rt = v_ref[...], at_ref[...], bt_ref[...], kt_ref[...], rt_ref[...]
        A2V, W2V, p_last = a2v_ref[...], w2v_ref[...], pl_ref[0, 0:1, :]
        S0 = _heads(lambda h: S_ref[h])
        for h in range(H):
            s0_ref[0, h] = S0[h]
        Z = _heads(lambda h: _hdot(_head(At, h), S0[h], NT) + _head(A2V, h))
        Sa = _heads(lambda h: _hdot(ti_ref[0, h], Z[h], NN))
        X = _heads(lambda h: S0[h] + _hdot(Sa[h], _head(Bt, h), TN) + _hdot(_head(V, h), _head(Kt, h), TN))
        for h in range(H):
            S_ref[h] = X[h] * _head(p_last, h)
        Y = _heads(lambda h: _hdot(_head(Rt, h), S0[h], NT) + _hdot(w1_ref[0, h], Sa[h], NN) + _head(W2V, h))
        y_ref[...] = jnp.concatenate(Y, axis=1)
        sa_ref[...] = jnp.concatenate(Sa, axis=1)

    row_spec = pl.BlockSpec((RW_CHUNK, RW_WIDTH), lambda n: (n, 0))
    st_spec = pl.BlockSpec((1, H, N, N), lambda n: (n, 0, 0, 0))
    row_shape = jax.ShapeDtypeStruct((T, RW_WIDTH), F32)
    return pl.pallas_call(
        body, name="rwkv_chunk_fwd", grid=(nC,),
        in_specs=[row_spec] * 7 + [st_spec, st_spec, pl.BlockSpec((1, SUBLANES, RW_WIDTH), lambda n: (n, 0, 0))],
        out_specs=[row_spec, row_spec, st_spec],
        out_shape=[row_shape, row_shape, jax.ShapeDtypeStruct((nC, H, N, N), F32)],
        scratch_shapes=[pltpu.VMEM((H, N, N), F32)],
        compiler_params=_params(("arbitrary",)),
    )(v, at, bt, kt, rt, a2v, w2v, tinv, w1, plast)


def _rwkv_chunk_bwd(r, lw, k, a, b, v, dy, s0, tinv, w1, a2, w2, sa):
    T = r.shape[0]
    nC = T // RW_CHUNK
    H, N = RW_HEADS, RW_HEAD_DIM

    def body(r_ref, lw_ref, k_ref, a_ref, b_ref, v_ref, dy_ref, s0_ref, ti_ref, w1_ref, a2_ref, w2_ref, sa_ref,
             dr_ref, dlw_ref, dk_ref, da_ref, db_ref, dv_ref, dS_ref):
        @pl.when(pl.program_id(0) == 0)
        def _():
            dS_ref[...] = jnp.zeros_like(dS_ref)

        incl, strict, _ = _chunk_masks()
        incl_f = incl.astype(F32)
        q = _chunk_rows(r_ref[...], lw_ref[...], k_ref[...], a_ref[...], b_ref[...], incl_f)
        At, Bt, Kt, Rt = q["At"], q["Bt"], q["Kt"], q["Rt"]
        A2, W1, W2 = (_heads(lambda h, ref=ref: ref[0, h]) for ref in (a2_ref, w1_ref, w2_ref))
        V, dY, Sa = v_ref[...], dy_ref[...], sa_ref[...]
        hd = _head
        p_last = q["e_pos"][RW_CHUNK - 1:RW_CHUNK, :]
        S0 = _heads(lambda h: s0_ref[0, h])
        G = _heads(lambda h: dS_ref[h] * hd(p_last, h))
        X = _heads(lambda h: S0[h] + _hdot(hd(Sa, h), hd(Bt, h), TN) + _hdot(hd(V, h), hd(Kt, h), TN))
        dc_last = jnp.concatenate(_heads(lambda h: jnp.sum(G[h] * X[h], axis=0, keepdims=True)), axis=1)
        dSa = _heads(lambda h: _hdot(hd(Bt, h), G[h], NT) + _hdot(W1[h], hd(dY, h), TN))
        dZ = _heads(lambda h: _hdot(ti_ref[0, h], dSa[h], TN))
        for h in range(H):
            dS_ref[h] = G[h] + _hdot(dZ[h], hd(At, h), TN) + _hdot(hd(dY, h), hd(Rt, h), TN)
        dA1 = _heads(lambda h: jnp.where(strict, _ldot(dZ[h], hd(Sa, h), NT), 0.0))
        dA2 = _heads(lambda h: jnp.where(strict, _ldot(dZ[h], hd(V, h), NT), 0.0))
        dW1 = _heads(lambda h: jnp.where(incl, _ldot(hd(dY, h), hd(Sa, h), NT), 0.0))
        dW2 = _heads(lambda h: jnp.where(incl, _ldot(hd(dY, h), hd(V, h), NT), 0.0))
        cat = lambda fn: jnp.concatenate(_heads(fn), axis=1)
        dV = cat(lambda h: _ldot(A2[h], dZ[h], TN) + _ldot(W2[h], hd(dY, h), TN) + _ldot(hd(Kt, h), G[h], NT))
        dAt = cat(lambda h: _ldot(dA1[h], hd(Bt, h), NN) + _ldot(dA2[h], hd(Kt, h), NN) + _ldot(dZ[h], S0[h], NN))
        dBt = cat(lambda h: _ldot(dA1[h], hd(At, h), TN) + _ldot(dW1[h], hd(Rt, h), TN) + _ldot(hd(Sa, h), G[h], NN))
        dKt = cat(lambda h: _ldot(dA2[h], hd(At, h), TN) + _ldot(dW2[h], hd(Rt, h), TN) + _ldot(hd(V, h), G[h], NN))
        dRt = cat(lambda h: _ldot(hd(dY, h), S0[h], NN) + _ldot(dW1[h], hd(Bt, h), NN) + _ldot(dW2[h], hd(Kt, h), NN))
        last_row = lax.broadcasted_iota(jnp.int32, (RW_CHUNK, RW_WIDTH), 0) == RW_CHUNK - 1
        dc_prev = dAt * At
        dc = dc_prev + dRt * Rt - dBt * Bt - dKt * Kt + jnp.where(last_row, dc_last, 0.0)
        dr_ref[...] = dRt * q["e_pos"]
        dlw_ref[...] = _hdot(incl_f, dc, TN) - dc_prev
        dk_ref[...] = dKt * q["e_neg"]
        da_ref[...] = dAt * q["e_prev"]
        db_ref[...] = dBt * q["e_neg"]
        dv_ref[...] = dV

    rev = lambda n: nC - 1 - n
    row_spec = pl.BlockSpec((RW_CHUNK, RW_WIDTH), lambda n: (rev(n), 0))
    st_spec = pl.BlockSpec((1, H, N, N), lambda n: (rev(n), 0, 0, 0))
    row_shape = jax.ShapeDtypeStruct((T, RW_WIDTH), F32)
    return pl.pallas_call(
        body, name="rwkv_chunk_bwd", grid=(nC,),
        in_specs=[row_spec] * 7 + [st_spec] * 5 + [row_spec], out_specs=[row_spec] * 6,
        out_shape=[row_shape] * 6, scratch_shapes=[pltpu.VMEM((H, N, N), F32)],
        compiler_params=_params(("arbitrary",)),
    )(r, lw, k, a, b, v, dy, s0, tinv, w1, a2, w2, sa)


def _alibi_slope(head):
    return float(np.float32(2.0 ** (-8.0 * (head + 1) / ATT_HEADS)))


ATT_SPAN = ATT_BLOCK * max(ATT_GROUP_DILATION)
ATT_PAIR_WIDTH = 2 * ATT_HEAD_DIM
ATT_SIDE_BY_SIDE = 8


def _pair_slope(g, hp, j):
    return jnp.where(hp == 0, _alibi_slope(4 * g + j), _alibi_slope(4 * g + 2 + j))


def _att_rows(mi, r, d):
    start = mi * ATT_BLOCK * d + r
    return pl.ds(start, ATT_BLOCK) if d == 1 else pl.ds(start, ATT_BLOCK, stride=d)


def _att_masks():
    qi = lax.broadcasted_iota(jnp.int32, (ATT_BLOCK, ATT_BLOCK), 0)
    kj = lax.broadcasted_iota(jnp.int32, (ATT_BLOCK, ATT_BLOCK), 1)
    return qi, kj


NEG = -1e30


def _att_logits(q, k, slope_d, steps, valid):
    s = lax.dot_general(q.astype(BF16), k.astype(BF16), (((1,), (1,)), ((), ())),
                        preferred_element_type=F32) * (ATT_HEAD_DIM ** -0.5)
    return jnp.where(valid, s - slope_d * steps.astype(F32), NEG)


def _att_fwd(p_att, g):
    T = p_att.shape[0]
    d = ATT_GROUP_DILATION[g]
    W = ATT_PAIR_WIDTH
    nb = T // ATT_SPAN
    mb = ATT_SPAN // (ATT_BLOCK * d)

    def body(q_ref, kc_ref, kp_ref, vc_ref, vp_ref, o_ref, l_ref):
        hp, n = pl.program_id(0), pl.program_id(1)
        qi, kj = _att_masks()
        slopes = [_pair_slope(g, hp, j) * d for j in range(2)]
        blocks = [(r, mi) for r in range(d) for mi in range(mb)]
        for at in range(0, len(blocks), ATT_SIDE_BY_SIDE):
            tasks = []
            for r, mi in blocks[at:at + ATT_SIDE_BY_SIDE]:
                rows = _att_rows(mi, r, d)
                if mi > 0:
                    prev = _att_rows(mi - 1, r, d)
                    kp, vp, has_prev = kc_ref[prev, :], vc_ref[prev, :], True
                else:
                    prev = _att_rows(mb - 1, r, d)
                    kp, vp, has_prev = kp_ref[prev, :], vp_ref[prev, :], n > 0
                q, kc, vc = q_ref[rows, :], kc_ref[rows, :], vc_ref[rows, :]
                for j in range(2):
                    sl = slice(j * ATT_HEAD_DIM, (j + 1) * ATT_HEAD_DIM)
                    tasks.append((q[:, sl], kc[:, sl], kp[:, sl], vc[:, sl], vp[:, sl], has_prev, slopes[j]))
            lc = [_att_logits(t[0], t[1], t[6], qi - kj, kj <= qi) for t in tasks]
            lp = [_att_logits(t[0], t[2], t[6], qi - kj + ATT_BLOCK, (kj >= qi) & t[5]) for t in tasks]
            mx = [jnp.maximum(jnp.max(a, axis=1, keepdims=True), jnp.max(b, axis=1, keepdims=True))
                  for a, b in zip(lc, lp, strict=True)]
            ec = [jnp.exp(a - m) for a, m in zip(lc, mx, strict=True)]
            ep = [jnp.exp(b - m) for b, m in zip(lp, mx, strict=True)]
            den = [jnp.sum(a, axis=1, keepdims=True) + jnp.sum(b, axis=1, keepdims=True)
                   for a, b in zip(ec, ep, strict=True)]
            inv = [1.0 / s for s in den]
            outs = [jnp.dot((a * i).astype(BF16), t[3].astype(BF16), preferred_element_type=F32)
                    + jnp.dot((b * i).astype(BF16), t[4].astype(BF16), preferred_element_type=F32)
                    for a, b, i, t in zip(ec, ep, inv, tasks, strict=True)]
            lses = [jnp.broadcast_to(m + jnp.log(s), (ATT_BLOCK, ATT_HEAD_DIM)) for m, s in zip(mx, den, strict=True)]
            for i, (r, mi) in enumerate(blocks[at:at + ATT_SIDE_BY_SIDE]):
                rows = _att_rows(mi, r, d)
                o_ref[rows, :] = jnp.concatenate(outs[2 * i:2 * i + 2], axis=1)
                l_ref[rows, :] = jnp.concatenate(lses[2 * i:2 * i + 2], axis=1)

    def spec(col0, prev):
        if prev:
            return pl.BlockSpec((ATT_SPAN, W), lambda hp, n: (jnp.maximum(n - 1, 0), col0 + 2 * g + hp))
        return pl.BlockSpec((ATT_SPAN, W), lambda hp, n: (n, col0 + 2 * g + hp))

    o_spec = pl.BlockSpec((ATT_SPAN, W), lambda hp, n: (n, hp))
    o, l = pl.pallas_call(
        body, name=f"att_fwd_g{g}", grid=(2, nb),
        in_specs=[spec(0, False), spec(6, False), spec(6, True), spec(12, False), spec(12, True)],
        out_specs=[o_spec, o_spec],
        out_shape=[jax.ShapeDtypeStruct((T, ATT_GROUP_WIDTH), F32)] * 2,
        compiler_params=_params(("parallel", "arbitrary")),
    )(p_att, p_att, p_att, p_att, p_att)
    return o, l


def _att_bwd(p_att, o, l, do, dl, g):
    T = p_att.shape[0]
    d = ATT_GROUP_DILATION[g]
    W = ATT_PAIR_WIDTH
    nb = T // ATT_SPAN
    mb = ATT_SPAN // (ATT_BLOCK * d)
    scale = ATT_HEAD_DIM ** -0.5

    def body(q_ref, k_ref, v_ref, o_ref, l_ref, do_ref, dl_ref,
             qn_ref, on_ref, ln_ref, don_ref, dln_ref, dq_ref, dk_ref, dv_ref, carry_ref):
        hp, n = pl.program_id(0), pl.program_id(1)
        qi, kj = _att_masks()

        @pl.when(n == 0)
        def _():
            carry_ref[...] = jnp.zeros_like(carry_ref)

        slopes = [_pair_slope(g, hp, j) * d for j in range(2)]
        blocks = [(r, mi) for r in range(d) for mi in range(mb)]
        side_by_side = ATT_SIDE_BY_SIDE // 2
        carry = None
        for at in range(0, len(blocks), side_by_side):
            tasks = []
            for r, mi in blocks[at:at + side_by_side]:
                rows = _att_rows(mi, r, d)
                if mi < mb - 1:
                    nrows = _att_rows(mi + 1, r, d)
                    nxt = (q_ref[nrows, :], o_ref[nrows, :], l_ref[nrows, :], do_ref[nrows, :], dl_ref[nrows, :])
                    has_next = True
                else:
                    nrows = _att_rows(0, r, d)
                    nxt = (qn_ref[nrows, :], on_ref[nrows, :], ln_ref[nrows, :], don_ref[nrows, :],
                           dln_ref[nrows, :])
                    has_next = n < nb - 1
                cur = (q_ref[rows, :], o_ref[rows, :], l_ref[rows, :], do_ref[rows, :], dl_ref[rows, :])
                k_all, v_all = k_ref[rows, :], v_ref[rows, :]
                for j in range(2):
                    sl = slice(j * ATT_HEAD_DIM, (j + 1) * ATT_HEAD_DIM)
                    for blk, steps, valid in ((cur, qi - kj, kj <= qi),
                                              (nxt, qi - kj + ATT_BLOCK, (kj >= qi) & has_next)):
                        q, o_, lse, do_, dlse = (z[:, sl] for z in blk)
                        tasks.append(dict(q=q, o=o_, lse=lse[:, :1], do=do_, dlse=dlse[:, :1], steps=steps,
                                          valid=valid, k=k_all[:, sl], vb=v_all[:, sl].astype(BF16),
                                          slope=slopes[j]))
            p = [jnp.exp(_att_logits(t["q"], t["k"], t["slope"], t["steps"], t["valid"]) - t["lse"]) for t in tasks]
            dp = [lax.dot_general(t["do"].astype(BF16), t["vb"], (((1,), (1,)), ((), ())),
                                  preferred_element_type=F32) for t in tasks]
            dsum = [jnp.sum(t["do"] * t["o"], axis=1, keepdims=True) for t in tasks]
            ds = [a * (b - s + t["dlse"]) for a, b, s, t in zip(p, dp, dsum, tasks, strict=True)]
            dv_ = [jnp.dot(a.T.astype(BF16), t["do"].astype(BF16), preferred_element_type=F32)
                   for a, t in zip(p, tasks, strict=True)]
            dk_ = [jnp.dot(a.T.astype(BF16), t["q"].astype(BF16), preferred_element_type=F32) * scale
                   for a, t in zip(ds, tasks, strict=True)]
            dq_ = [jnp.dot(a.astype(BF16), t["k"].astype(BF16), preferred_element_type=F32) * scale
                   for a, t in zip(ds, tasks, strict=True)]
            for i, (r, mi) in enumerate(blocks[at:at + side_by_side]):
                rows = _att_rows(mi, r, d)
                b = 4 * i
                if mi == 0:
                    carry = carry_ref[r]
                dq_ref[rows, :] = jnp.concatenate([dq_[b], dq_[b + 2]], axis=1) + carry
                carry = jnp.concatenate([dq_[b + 1], dq_[b + 3]], axis=1)
                if mi == mb - 1:
                    carry_ref[r] = carry
                dk_ref[rows, :] = jnp.concatenate([dk_[b] + dk_[b + 1], dk_[b + 2] + dk_[b + 3]], axis=1)
                dv_ref[rows, :] = jnp.concatenate([dv_[b] + dv_[b + 1], dv_[b + 2] + dv_[b + 3]], axis=1)

    head_rows = ATT_BLOCK * d
    nxt_n = lambda n: jnp.minimum((n + 1) * mb, T // head_rows - 1)
    cur_p = lambda col0: pl.BlockSpec((ATT_SPAN, W), lambda hp, n: (n, col0 + 2 * g + hp))
    cur_o = pl.BlockSpec((ATT_SPAN, W), lambda hp, n: (n, hp))
    nxt_o = pl.BlockSpec((head_rows, W), lambda hp, n: (nxt_n(n), hp))
    dq, dk, dv = pl.pallas_call(
        body, name=f"att_bwd_g{g}", grid=(2, nb),
        in_specs=[cur_p(0), cur_p(6), cur_p(12), cur_o, cur_o, cur_o, cur_o,
                  pl.BlockSpec((head_rows, W), lambda hp, n: (nxt_n(n), 2 * g + hp)), nxt_o, nxt_o, nxt_o, nxt_o],
        out_specs=[cur_o, cur_o, cur_o],
        out_shape=[jax.ShapeDtypeStruct((T, ATT_GROUP_WIDTH), F32)] * 3,
        scratch_shapes=[pltpu.VMEM((d, ATT_BLOCK, W), F32)],
        compiler_params=_params(("parallel", "arbitrary")),
    )(p_att, p_att, p_att, o, l, do, dl, p_att, o, l, do, dl)
    return dq, dk, dv


RKV = 3 * RW_WIDTH
WA = 128
XG = 160
RW_COLS = RKV + WA + XG


def _local_step(x, p, W, target, late_weights=None, early_grads=None, by_chip=False):
    T = x.shape[0]
    tT = 256
    bd512 = _block_diag_ones(RW_WIDTH, RW_HEAD_DIM)
    bd256 = _block_diag_ones(ATT_GROUP_WIDTH, ATT_HEAD_DIM)
    G = {}
    W = dict(W)

    w_in = W["w_in"]
    w_rkv, w_wa, w_xg, w_att = (w_in[:, :RKV], w_in[:, RKV:RKV + WA], w_in[:, RKV + WA:RW_COLS],
                                w_in[:, RW_COLS:])
    mu = W["rw_mu"]
    mu_rkv, mu_wa, mu_xg = mu[:, :RKV], mu[:, RKV:RKV + WA], mu[:, RKV + WA:]
    zpad = jnp.zeros((64, RW_WIDTH), W["rw_w_up"].dtype)
    w_up_pad = jnp.concatenate([W["rw_w_up"], zpad], axis=0)
    a_up_pad = jnp.concatenate([zpad, W["rw_a_up"]], axis=0)
    r_k = W["rw_r_k"].reshape(1, RW_WIDTH)

    (h,) = _rowwise("norm_mix", lambda i, n, r, pv, nx, c: [_rms_fwd(r[0], c[0])], T, tT,
                    rows=[x], consts=[W["g_mix"]], outs=[("row", D_MODEL, BF16)])
    p_rkv = _mm("proj_rkv", h, w_rkv, "nn")
    p_wa = _mm("proj_wa", h, w_wa, "nn")
    p_xg = _mm("proj_xg", h, w_xg, "nn")
    p_att = _mm("proj_att", h, w_att, "nn", tn=768)
    z_gate = _mm("proj_gate", h, W["w_gate"], "nn")

    def rw_pre_core(i, rows, prevs, consts):
        prkv, pwa, pxg = rows[:3]
        (mrkv, mwa, mxg, w0, a0, k_k, k_a, wup, aup, gup, bd) = consts[:11]
        m_rkv = prkv + (_shift_down(prkv, prevs[0], i, 1) - prkv) * mrkv
        m_wa = pwa + (_shift_down(pwa, prevs[1], i, 1) - pwa) * mwa
        m_xg = pxg + (_shift_down(pxg, prevs[2], i, 1) - pxg) * mxg
        r, k, v = m_rkv[:, :RW_WIDTH], m_rkv[:, RW_WIDTH:2 * RW_WIDTH], m_rkv[:, 2 * RW_WIDTH:]
        tw = jnp.tanh(m_wa)
        lw = w0 + jnp.dot(tw.astype(BF16), wup.astype(BF16), preferred_element_type=F32)
        wlog = -_softplus(-lw) - 0.5
        log_decay = -jnp.exp(wlog)
        a = _sigmoid(a0 + jnp.dot(m_wa.astype(BF16), aup.astype(BF16), preferred_element_type=F32))
        sg = _sigmoid(m_xg)
        gate = jnp.dot(sg.astype(BF16), gup.astype(BF16), preferred_element_type=F32)
        kkp = k * k_k
        nrm = jnp.sqrt(_segsum(kkp * kkp, bd))
        nrm_c = jnp.maximum(nrm, 1e-12)
        kk = kkp / nrm_c
        k2 = k * (1.0 + (a - 1.0) * k_a)
        return dict(r=r, k=k, v=v, tw=tw, lw=lw, wlog=wlog, log_decay=log_decay, a=a, sg=sg, gate=gate, kkp=kkp,
                    nrm=nrm, nrm_c=nrm_c, kk=kk, k2=k2, m_rkv=m_rkv, m_wa=m_wa, m_xg=m_xg)

    pre_consts = [mu_rkv, mu_wa, mu_xg, W["rw_w0"], W["rw_a0"], W["rw_k_k"], W["rw_k_a"],
                  w_up_pad, a_up_pad, W["rw_g_up"], bd512]

    def rw_pre(i, n, rows, prevs, nexts, consts):
        q = rw_pre_core(i, rows, prevs, consts)
        return [q["r"], q["log_decay"], q["k2"], q["v"], -q["kk"], q["kk"] * q["a"], q["gate"]]

    r_s, w_s, k_s, v_s, a_s, b_s, gate_s = _rowwise(
        "rwkv_pre", rw_pre, T, tT, rows=[p_rkv, p_wa, p_xg], prevs=[p_rkv, p_wa, p_xg], consts=pre_consts,
        outs=[("row", RW_WIDTH, F32)] * 7)
    (at_s, bt_s, kt_s, rt_s, a2v_s, w2v_s, tinv_s, w1_s, a2_s, w2_s,
     plast_s) = _rwkv_chunk_prep(r_s, w_s, k_s, a_s, b_s, v_s)
    y_scan, sa_s, s0_s = _rwkv_chunk_fwd(v_s, at_s, bt_s, kt_s, rt_s, a2v_s, w2v_s, tinv_s, w1_s, plast_s)

    def rw_post_core(rows, consts):
        y, r, k2, v, gate = rows[:5]
        ln_g, ln_b, rk, bd = consts[:4]
        mean = _segsum(y, bd) * (1.0 / RW_HEAD_DIM)
        yc = y - mean
        var = _segsum(yc * yc, bd) * (1.0 / RW_HEAD_DIM)
        rstd = lax.rsqrt(var + RW_LN_EPS)
        yn = yc * rstd
        s = _segsum(r * k2 * rk, bd)
        return dict(yn=yn, rstd=rstd, s=s, pre=yn * ln_g + ln_b + s * v)

    post_consts = [W["rw_ln_g"], W["rw_ln_b"], r_k, bd512]
    (y_a,) = _rowwise("rwkv_post", lambda i, n, r, pv, nx, c: [rw_post_core(r, c)["pre"] * r[4]], T, tT,
                      rows=[y_scan, r_s, k_s, v_s, gate_s], consts=post_consts, outs=[("row", RW_WIDTH, BF16)])

    att = [_att_fwd(p_att, g) for g in range(3)]

    def comb_weights(ls):
        mx = jnp.maximum(jnp.maximum(ls[0], ls[1]), ls[2])
        es = [jnp.exp(l - mx) for l in ls]
        den = es[0] + es[1] + es[2]
        return [e / den for e in es]

    def att_comb(i, n, rows, pv, nx, c):
        wts = comb_weights(rows[3:6])
        return [wts[0] * rows[0] + wts[1] * rows[1] + wts[2] * rows[2]]

    (y_b,) = _rowwise("att_combine", att_comb, T, tT, rows=[att[0][0], att[1][0], att[2][0], att[0][1], att[1][1],
                                                            att[2][1]], outs=[("row", ATT_GROUP_WIDTH, BF16)])

    if late_weights is not None:
        W.update(late_weights(y_b))
    br_a = _mm("branch_a", y_a, W["w_branch_a"], "nn")
    br_b = _mm("branch_b", y_b, W["w_branch_b"], "nn")

    def merge(i, n, rows, pv, nx, c):
        gates = _sigmoid(rows[0] + c[0])
        return [gates[:, :D_MODEL] * rows[1] + gates[:, D_MODEL:] * rows[2]]

    (merged,) = _rowwise("merge", merge, T, tT, rows=[z_gate, br_a, br_b], consts=[W["b_gate"]],
                         outs=[("row", D_MODEL, BF16)])
    x1 = _mm("mix_out", merged, W["w_out"], "nn", add=x)

    (h2,) = _rowwise("norm_ffn", lambda i, n, r, pv, nx, c: [_rms_fwd(r[0], c[0])], T, tT,
                     rows=[x1], consts=[W["g_ffn"]], outs=[("row", D_MODEL, BF16)])
    u = _mm("ffn_up", h2, W["w_up"], "nn")

    def conv_core(i, rows, prevs, consts):
        uu, cw, cb = rows[0], consts[0], consts[1]
        u1 = _shift_down(uu, prevs[0], i, 1)
        u2 = _shift_down(uu, prevs[0], i, 2)
        uc = cb + cw[0:1] * uu + cw[1:2] * u1 + cw[2:3] * u2
        return uc[:, :D_FF], uc[:, D_FF:], u1, u2

    def glu(i, n, rows, prevs, nx, consts):
        gate, val, _, _ = conv_core(i, rows, prevs, consts)
        return [_gelu(gate) * val]

    tF = 128
    (act,) = _rowwise("conv_glu", glu, T, tF, rows=[u], prevs=[u], consts=[W["conv_w"], W["conv_b"]],
                      outs=[("row", D_FF, BF16)])
    x2 = _mm("ffn_down", act, W["w_down"], "nn", add=x1)

    (h3,) = _rowwise("norm_ple", lambda i, n, r, pv, nx, c: [_rms_fwd(r[0], c[0])], T, tT,
                     rows=[x2], consts=[W["g_ple"]], outs=[("row", D_MODEL, BF16)])
    z_ple = _mm("ple_gate", h3, W["w_ple_gate"], "nn")
    e_ple = _mm("ple_emb", p, W["w_ple"], "nn")

    def head(i, n, rows, pv, nx, consts):
        x2_, z, e, tgt = rows
        pg = _sigmoid(z)
        x3 = x2_ + pg * e
        y = _rms_fwd(x3, consts[0])
        err = y - tgt
        loss = 0.5 * jnp.sum(jnp.sum(err * err, axis=1, keepdims=True) * (1.0 / D_MODEL), axis=0, keepdims=True)
        dy = err * (1.0 / D_MODEL)
        dx3, dgf = _rms_bwd(x3, consts[0], dy)
        return [dx3, dx3 * pg, dx3 * e * pg * (1.0 - pg), jnp.broadcast_to(loss, (1, LANES)), _colsum(dgf)]

    dx3, de, dz, loss_acc, G["g_final"] = _rowwise(
        "loss_head", head, T, tT, rows=[x2, z_ple, e_ple, target], consts=[W["g_final"].reshape(1, D_MODEL)],
        outs=[("row", D_MODEL, F32), ("row", D_MODEL, BF16), ("row", D_MODEL, BF16), ("acc", (1, LANES)),
              ("acc", (1, D_MODEL))])
    G["w_ple"] = _mm("d_w_ple", p, de, "tn", out_by_chip=by_chip)
    G["w_ple_gate"] = _mm("d_w_ple_gate", h3, dz, "tn")
    dh3 = _mm("d_h3", dz, W["w_ple_gate"], "nt")

    def norm_bwd(i, n, rows, pv, nx, consts):
        dx, dg = _rms_bwd(rows[0], consts[0], rows[1])
        return [rows[2] + dx, _colsum(dg)]

    dx2, G["g_ple"] = _rowwise("d_norm_ple", norm_bwd, T, tT, rows=[x2, dh3, dx3], consts=[W["g_ple"]],
                               outs=[("row", D_MODEL, F32), ("acc", (1, D_MODEL))])

    dact = _mm("d_act", dx2, W["w_down"], "nt")
    G["w_down"] = _mm("d_w_down", act, dx2, "tn")

    def glu_grad(gate, val, da):
        act_, slope = _gelu_and_grad(gate)
        return jnp.concatenate([da * val * slope, da * act_], axis=1)

    def glu_bwd(i, n, rows, prevs, nexts, consts):
        uu, da = rows
        cw = consts[0]
        gate, val, u1, u2 = conv_core(i, rows, prevs, consts)
        duc = glu_grad(gate, val, da)
        dcw = jnp.concatenate([_colsum(duc * uu), _colsum(duc * u1), _colsum(duc * u2)], axis=0)
        gate_n, val_n, _, _ = conv_core(1, [nexts[0]], [uu[tF - SUBLANES:]], consts)
        duc_n = glu_grad(gate_n, val_n, nexts[1])
        du = (cw[0:1] * duc + cw[1:2] * _shift_up(duc, duc_n, i, n, 1) + cw[2:3] * _shift_up(duc, duc_n, i, n, 2))
        return [du, _colsum(duc), dcw]

    du, G["conv_b"], G["conv_w"] = _rowwise(
        "d_conv_glu", glu_bwd, T, tF, rows=[u, dact], prevs=[u], nexts=[u, dact],
        consts=[W["conv_w"], W["conv_b"]],
        outs=[("row", 2 * D_FF, BF16), ("acc", (1, 2 * D_FF)), ("acc", (3, 2 * D_FF))])
    G["w_up"] = _mm("d_w_up", h2, du, "tn", out_by_chip=by_chip)
    dh2 = _mm("d_h2", du, W["w_up"], "nt")
    dx1, G["g_ffn"] = _rowwise("d_norm_ffn", norm_bwd, T, tT, rows=[x1, dh2, dx2], consts=[W["g_ffn"]],
                               outs=[("row", D_MODEL, F32), ("acc", (1, D_MODEL))])

    b_gate = W["b_gate"]
    if early_grads is not None:
        b_gate = b_gate + early_grads(G, 0)[0:1, 0:1]
    dmerged = _mm("d_merged", dx1, W["w_out"], "nt")
    G["w_out"] = _mm("d_w_out", merged, dx1, "tn")

    def merge_bwd(i, n, rows, pv, nx, consts):
        z, a_, b_, dm = rows
        gates = _sigmoid(z + consts[0])
        ga, gb = gates[:, :D_MODEL], gates[:, D_MODEL:]
        dz_ = jnp.concatenate([dm * a_ * ga * (1.0 - ga), dm * b_ * gb * (1.0 - gb)], axis=1)
        return [dm * ga, dm * gb, dz_, _colsum(dz_)]

    d_br_a, d_br_b, dz_gate, G["b_gate"] = _rowwise(
        "d_merge", merge_bwd, T, tT, rows=[z_gate, br_a, br_b, dmerged], consts=[b_gate],
        outs=[("row", D_MODEL, BF16), ("row", D_MODEL, BF16), ("row", 2 * D_MODEL, BF16), ("acc", (1, 2 * D_MODEL))])
    G["w_branch_a"] = _mm("d_w_branch_a", y_a, d_br_a, "tn", out_by_chip=by_chip)
    G["w_branch_b"] = _mm("d_w_branch_b", y_b, d_br_b, "tn", out_by_chip=by_chip)
    G["w_gate"] = _mm("d_w_gate", h, dz_gate, "tn", out_by_chip=by_chip)
    if early_grads is not None:
        post_consts = [post_consts[0] + early_grads(G, 1)[0:1, 0:1]] + post_consts[1:]
    dy_a = _mm("d_y_a", d_br_a, W["w_branch_a"], "nt")
    dy_b = _mm("d_y_b", d_br_b, W["w_branch_b"], "nt")

    def att_comb_bwd(i, n, rows, pv, nx, consts):
        os_, ls, dy = rows[0:3], rows[3:6], rows[6]
        wts = comb_weights(ls)
        dws = [_segsum(dy * o_, consts[0]) for o_ in os_]
        mix = wts[0] * dws[0] + wts[1] * dws[1] + wts[2] * dws[2]
        return [wts[g_] * dy for g_ in range(3)] + [wts[g_] * (dws[g_] - mix) for g_ in range(3)]

    comb = _rowwise("d_att_combine", att_comb_bwd, T, tT,
                    rows=[att[0][0], att[1][0], att[2][0], att[0][1], att[1][1], att[2][1], dy_b], consts=[bd256],
                    outs=[("row", ATT_GROUP_WIDTH, F32)] * 6)
    dqkv = [_att_bwd(p_att, att[g][0], att[g][1], comb[g], comb[3 + g], g) for g in range(3)]
    dp_att = jnp.concatenate([dqkv[g][part] for part in range(3) for g in range(3)], axis=1).astype(BF16)

    def rw_post_bwd(i, n, rows, pv, nx, consts):
        y, r, k2, v, gate, dya = rows
        ln_g, ln_b, rk, bd = consts
        q = rw_post_core(rows, consts)
        dpre = dya * gate
        dgate = dya * q["pre"]
        dyn = dpre * ln_g
        inv = 1.0 / RW_HEAD_DIM
        dy_scan = q["rstd"] * (dyn - _segsum(dyn, bd) * inv - q["yn"] * (_segsum(dyn * q["yn"], bd) * inv))
        ds = _segsum(dpre * v, bd)
        return [dy_scan, dgate, ds * k2 * rk, ds * r * rk, dpre * q["s"],
                _colsum(dpre * q["yn"]), _colsum(dpre), _colsum(ds * r * k2)]

    dy_scan, dgate, dr_b, dk2_b, dv_b, G["rw_ln_g"], G["rw_ln_b"], d_rk = _rowwise(
        "d_rwkv_post", rw_post_bwd, T, tT, rows=[y_scan, r_s, k_s, v_s, gate_s, dy_a], consts=post_consts,
        outs=[("row", RW_WIDTH, F32)] * 5 + [("acc", (1, RW_WIDTH))] * 3)
    G["rw_r_k"] = d_rk.reshape(RW_HEADS, RW_HEAD_DIM)

    dr_s, dw_s, dk_s, da_s, db_s, dv_s = _rwkv_chunk_bwd(r_s, w_s, k_s, a_s, b_s, v_s, dy_scan, s0_s, tinv_s, w1_s,
                                                         a2_s, w2_s, sa_s)

    def rw_pre_bwd(i, n, rows, prevs, nx, consts):
        q = rw_pre_core(i, rows, prevs, consts)
        (mrkv, mwa, mxg, w0, a0, k_k, k_a, wup, aup, gup, bd) = consts
        dr, dlogdecay, dk2, dv, dav, dbv, dgate_ = rows[3:10]
        dr = dr + rows[10]
        dk2 = dk2 + rows[11]
        dv = dv + rows[12]
        a, k, kk = q["a"], q["k"], q["kk"]
        dk = dk2 * (1.0 + (a - 1.0) * k_a)
        da = dk2 * k * k_a + dbv * kk
        dkk = dbv * a - dav
        live = q["nrm"] > 1e-12
        dkkp = jnp.where(live, dkk - kk * _segsum(dkk * kk, bd), dkk) / q["nrm_c"]
        dk = dk + dkkp * k_k
        dlw = dlogdecay * q["log_decay"] * _sigmoid(-q["lw"])
        dla = da * a * (1.0 - a)
        nt = (((1,), (1,)), ((), ()))
        dtw = lax.dot_general(dlw.astype(BF16), wup.astype(BF16), nt, preferred_element_type=F32)
        dxa = lax.dot_general(dla.astype(BF16), aup.astype(BF16), nt, preferred_element_type=F32)
        dm_wa = dtw * (1.0 - q["tw"] * q["tw"]) + dxa
        dsg = lax.dot_general(dgate_.astype(BF16), gup.astype(BF16), nt, preferred_element_type=F32)
        dm_xg = dsg * q["sg"] * (1.0 - q["sg"])
        dm_rkv = jnp.concatenate([dr, dk, dv], axis=1)
        prkv, pwa, pxg = rows[:3]
        dmu = jnp.concatenate([_colsum(dm_rkv * (_shift_down(prkv, prevs[0], i, 1) - prkv)),
                               _colsum(dm_wa * (_shift_down(pwa, prevs[1], i, 1) - pwa)),
                               _colsum(dm_xg * (_shift_down(pxg, prevs[2], i, 1) - pxg))], axis=1)
        return [dm_rkv, dm_wa, dm_xg, dlw, dla, q["tw"], q["m_wa"], q["sg"], dmu,
                _colsum(dlw), _colsum(dla), _colsum(dkkp * k), _colsum(dk2 * k * (a - 1.0))]

    (dm_rkv, dm_wa, dm_xg, dlw, dla, tw_s, mwa_s, sg_s, G["rw_mu"], G["rw_w0"], G["rw_a0"], G["rw_k_k"],
     G["rw_k_a"]) = _rowwise(
        "d_rwkv_pre", rw_pre_bwd, T, tT,
        rows=[p_rkv, p_wa, p_xg, dr_s, dw_s, dk_s, dv_s, da_s, db_s, dgate, dr_b, dk2_b, dv_b],
        prevs=[p_rkv, p_wa, p_xg], consts=pre_consts,
        outs=[("row", RKV, F32), ("row", WA, F32), ("row", XG, F32), ("row", RW_WIDTH, BF16),
              ("row", RW_WIDTH, BF16), ("row", WA, BF16), ("row", WA, BF16), ("row", XG, BF16),
              ("acc", (1, RW_COLS))] + [("acc", (1, RW_WIDTH))] * 4)
    G["rw_w_up"] = _mm("d_rw_w_up", tw_s, dlw, "tn")[:64]
    G["rw_a_up"] = _mm("d_rw_a_up", mwa_s, dla, "tn")[64:]
    G["rw_g_up"] = _mm("d_rw_g_up", sg_s, dgate, "tn")

    def shift_bwd(i, n, rows, pv, nexts, consts):
        return [rows[j] * (1.0 - consts[j]) + _shift_up(rows[j], nexts[j], i, n, 1) * consts[j] for j in range(3)]

    dp_rkv, dp_wa, dp_xg = _rowwise(
        "d_token_shift", shift_bwd, T, tT, rows=[dm_rkv, dm_wa, dm_xg], nexts=[dm_rkv, dm_wa, dm_xg],
        consts=[mu_rkv, mu_wa, mu_xg], outs=[("row", RKV, BF16), ("row", WA, BF16), ("row", XG, BF16)])

    G["w_in"] = jnp.concatenate([_mm("d_w_rkv", h, dp_rkv, "tn"), _mm("d_w_wa", h, dp_wa, "tn"),
                                 _mm("d_w_xg", h, dp_xg, "tn"), _mm("d_w_att", h, dp_att, "tn", tn=768)], axis=1)
    dh = _mm("d_h_gate", dz_gate, W["w_gate"], "nt")
    dh = _mm("d_h_rkv", dp_rkv, w_rkv, "nt", add=dh)
    dh = _mm("d_h_wa", dp_wa, w_wa, "nt", add=dh)
    dh = _mm("d_h_xg", dp_xg, w_xg, "nt", add=dh)
    dh = _mm("d_h_att", dp_att, w_att, "nt", add=dh)
    dx, G["g_mix"] = _rowwise("d_norm_mix", norm_bwd, T, tT, rows=[x, dh, dx1], consts=[W["g_mix"]],
                              outs=[("row", D_MODEL, F32), ("acc", (1, D_MODEL))])
    return loss_acc[:, :1], dx, G


HBM_SPEC = pl.BlockSpec(memory_space=pltpu.HBM)


def _place():
    x, y, c = lax.axis_index("x"), lax.axis_index("y"), lax.axis_index("c")
    return x, y, c, [(1 - x, y), (x, 1 - y), (1 - x, 1 - y)]


def _remote(src, dst, send_sems, recv_sems, k, to):
    return pltpu.make_async_remote_copy(src_ref=src, dst_ref=dst, send_sem=send_sems.at[k], recv_sem=recv_sems.at[k],
                                        device_id=to, device_id_type=MESH)


ROW_ALIGN = 16


def _splits(rows):
    return rows % (2 * ROW_ALIGN) == 0


def _half_rows(ref_rows, c, first):
    half = ref_rows // 2
    which = c if first else 1 - c
    return pl.ds(pl.multiple_of(which * half, ROW_ALIGN), half)


def _gather_chips(shards):
    n = len(shards)
    split = [_splits(s.shape[0]) for s in shards]

    def body(*refs):
        w_refs, out_refs = refs[:n], refs[n:2 * n]
        send_sems, recv_sems = refs[2 * n:]
        x, y, c, chips = _place()
        me = 2 * x + y
        sends, passed = [], []
        for i in range(n):
            for j, (px, py) in enumerate(chips):
                if split[i]:
                    mine = _half_rows(w_refs[i].shape[0], c, True)
                    cp = _remote(w_refs[i].at[mine], out_refs[i].at[me, mine], send_sems, recv_sems, 6 * i + j,
                                 (px, py, c))
                else:
                    cp = _remote(w_refs[i], out_refs[i].at[me], send_sems, recv_sems, 6 * i + j, (px, py, c))
                cp.start()
                sends.append(cp)
        for i in range(n):
            for j, (px, py) in enumerate(chips):
                if split[i]:
                    landed = out_refs[i].at[2 * px + py, _half_rows(w_refs[i].shape[0], c, True)]
                    _remote(landed, landed, send_sems, recv_sems, 6 * i + j, (px, py, c)).wait_recv()
                    cp = _remote(landed, landed, send_sems, recv_sems, 6 * i + 3 + j, (x, y, 1 - c))
                    cp.start()
                    passed.append(cp)
                else:
                    landed = out_refs[i].at[2 * px + py]
                    _remote(landed, landed, send_sems, recv_sems, 6 * i + j, (px, py, c)).wait_recv()
        for i in range(n):
            if split[i]:
                for j, (px, py) in enumerate(chips):
                    landed = out_refs[i].at[2 * px + py, _half_rows(w_refs[i].shape[0], c, False)]
                    _remote(landed, landed, send_sems, recv_sems, 6 * i + 3 + j, (x, y, 1 - c)).wait_recv()
        for cp in sends + passed:
            cp.wait_send()

    outs = pl.pallas_call(
        body, name="gather_weights", in_specs=[HBM_SPEC] * n, out_specs=[HBM_SPEC] * n,
        out_shape=[jax.ShapeDtypeStruct((N_CHIPS,) + s.shape, s.dtype) for s in shards],
        scratch_shapes=[pltpu.SemaphoreType.DMA((6 * n,)), pltpu.SemaphoreType.DMA((6 * n,))],
    )(*shards)
    me = 2 * lax.axis_index("x") + lax.axis_index("y")
    return [lax.dynamic_update_slice(o, s[None], (me, 0, 0)) for o, s in zip(outs, shards, strict=True)]


def _swap_halves(name, gs):
    n = len(gs)

    def body(*refs):
        g_refs, out_refs = refs[:n], refs[n:2 * n]
        send_sems, recv_sems = refs[2 * n:]
        x, y, c, _ = _place()
        cps = []
        for i in range(n):
            theirs = _half_rows(g_refs[i].shape[1], c, False)
            cp = _remote(g_refs[i].at[:, theirs, :], out_refs[i], send_sems, recv_sems, i, (x, y, 1 - c))
            cp.start()
            cps.append(cp)
        for cp in cps:
            cp.wait()

    return pl.pallas_call(
        body, name=name, in_specs=[HBM_SPEC] * n, out_specs=[HBM_SPEC] * n,
        out_shape=[jax.ShapeDtypeStruct((N_CHIPS, g.shape[1] // 2, g.shape[2]), g.dtype) for g in gs],
        scratch_shapes=[pltpu.SemaphoreType.DMA((n,)), pltpu.SemaphoreType.DMA((n,))],
    )(*gs)


def _scatter_chips(parts):
    n = len(parts)

    def body(*refs):
        p_refs, out_refs = refs[:n], refs[n:2 * n]
        send_sems, recv_sems = refs[2 * n:]
        x, y, c, chips = _place()
        me = 2 * x + y
        sends = []
        for i in range(n):
            for j, (px, py) in enumerate(chips):
                cp = _remote(p_refs[i].at[2 * px + py], out_refs[i].at[me], send_sems, recv_sems, 3 * i + j,
                             (px, py, c))
                cp.start()
                sends.append(cp)
        for i in range(n):
            for j, (px, py) in enumerate(chips):
                landed = out_refs[i].at[2 * px + py]
                _remote(landed, landed, send_sems, recv_sems, 3 * i + j, (px, py, c)).wait_recv()
        for cp in sends:
            cp.wait_send()

    outs = pl.pallas_call(
        body, name="scatter_grads", in_specs=[HBM_SPEC] * n, out_specs=[HBM_SPEC] * n,
        out_shape=[jax.ShapeDtypeStruct(p.shape, p.dtype) for p in parts],
        scratch_shapes=[pltpu.SemaphoreType.DMA((3 * n,)), pltpu.SemaphoreType.DMA((3 * n,))],
    )(*parts)
    me = 2 * lax.axis_index("x") + lax.axis_index("y")
    own = [lax.dynamic_slice_in_dim(p, me, 1, axis=0) for p in parts]
    return [lax.dynamic_update_slice(o, s, (me, 0, 0)) for o, s in zip(outs, own, strict=True)]


def _join_halves(reds):
    n = len(reds)

    def body(*refs):
        r_refs, out_refs = refs[:n], refs[n:2 * n]
        send_sems, recv_sems = refs[2 * n:]
        x, y, c, _ = _place()
        cps = []
        for i in range(n):
            mine = _half_rows(out_refs[i].shape[0], c, True)
            cp = _remote(r_refs[i], out_refs[i].at[mine], send_sems, recv_sems, i, (x, y, 1 - c))
            cp.start()
            cps.append(cp)
        for cp in cps:
            cp.wait()

    outs = pl.pallas_call(
        body, name="join_halves", in_specs=[HBM_SPEC] * n, out_specs=[HBM_SPEC] * n,
        out_shape=[jax.ShapeDtypeStruct((2 * r.shape[0], r.shape[1]), r.dtype) for r in reds],
        scratch_shapes=[pltpu.SemaphoreType.DMA((n,)), pltpu.SemaphoreType.DMA((n,))],
    )(*reds)
    c = lax.axis_index("c")
    return [lax.dynamic_update_slice(o, r, (c * r.shape[0], 0)) for o, r in zip(outs, reds, strict=True)]


def _gather_all(vec):
    R = vec.shape[0]

    def body(v_ref, out_ref, send_sems, recv_sems, local_sem):
        x, y, c, _ = _place()
        me = 4 * x + 2 * y + c
        local = pltpu.make_async_copy(v_ref, out_ref.at[me], local_sem)
        local.start()
        peers = [(x ^ (k >> 2), y ^ ((k >> 1) & 1), c ^ (k & 1)) for k in range(1, N_DEV)]
        sends = [_remote(v_ref, out_ref.at[me], send_sems, recv_sems, k, to) for k, to in enumerate(peers)]
        for cp in sends:
            cp.start()
        for k, (px, py, pc) in enumerate(peers):
            landed = out_ref.at[4 * px + 2 * py + pc]
            _remote(landed, landed, send_sems, recv_sems, k, (px, py, pc)).wait_recv()
        for cp in sends:
            cp.wait_send()
        local.wait()

    return pl.pallas_call(
        body, name="gather_small", in_specs=[HBM_SPEC], out_specs=HBM_SPEC,
        out_shape=jax.ShapeDtypeStruct((N_DEV, R, LANES), vec.dtype),
        scratch_shapes=[pltpu.SemaphoreType.DMA((7,)), pltpu.SemaphoreType.DMA((7,)), pltpu.SemaphoreType.DMA],
    )(vec)


SEM_SPEC = pl.BlockSpec(memory_space=pltpu.SEMAPHORE)
DATAFLOW = pltpu.SideEffectType.DATAFLOW_SIDE_EFFECTING


def _travel_copies(mode, src_refs, land_refs, send_sems, recv_sems):
    x, y, c, chips = _place()
    me = 2 * x + y
    pairs = []
    for i, (src, land) in enumerate(zip(src_refs, land_refs, strict=True)):
        for j, (px, py) in enumerate(chips):
            peer = 2 * px + py
            if mode == "scatter":
                mine, there, here = src.at[peer], land.at[me], land.at[peer]
            elif _splits(src.shape[0]):
                rows = _half_rows(src.shape[0], c, True)
                mine, there, here = src.at[rows], land.at[me, rows], land.at[peer, rows]
            else:
                mine, there, here = src, land.at[me], land.at[peer]
            send = functools.partial(_remote, mine, there, send_sems, recv_sems, 3 * i + j, (px, py, c))
            arrival = functools.partial(_remote, mine, here, send_sems, recv_sems, 3 * i + j, (px, py, c))
            pairs.append((send, arrival))
    return pairs


def _share_halves(name, lands):
    idx = [i for i, a in enumerate(lands) if _splits(a.shape[1])]
    n = len(idx)

    def body(*refs):
        in_refs, out_refs = refs[:n], refs[n:2 * n]
        send_sems, recv_sems = refs[2 * n:]
        x, y, c, chips = _place()
        cps = []
        for i, (src, dst) in enumerate(zip(in_refs, out_refs, strict=True)):
            for j, (px, py) in enumerate(chips):
                mine = _half_rows(src.shape[1], c, True)
                cp = _remote(src.at[2 * px + py, mine], dst.at[2 * px + py, mine], send_sems, recv_sems, 3 * i + j,
                             (x, y, 1 - c))
                cp.start()
                cps.append(cp)
        for i, dst in enumerate(out_refs):
            for j, (px, py) in enumerate(chips):
                theirs = dst.at[2 * px + py, _half_rows(dst.shape[1], c, False)]
                _remote(theirs, theirs, send_sems, recv_sems, 3 * i + j, (x, y, 1 - c)).wait_recv()
        for cp in cps:
            cp.wait_send()

    outs = pl.pallas_call(
        body, name=name, in_specs=[HBM_SPEC] * n, out_specs=[HBM_SPEC] * n,
        out_shape=[jax.ShapeDtypeStruct(lands[i].shape, lands[i].dtype) for i in idx],
        input_output_aliases={i: i for i in range(n)},
        scratch_shapes=[pltpu.SemaphoreType.DMA((3 * n,)), pltpu.SemaphoreType.DMA((3 * n,))],
    )(*[lands[i] for i in idx])
    done = list(lands)
    for i, o in zip(idx, outs, strict=True):
        done[i] = o
    return done


def _travel_start(name, mode, srcs):
    n = len(srcs)
    lands = [lax.empty((N_CHIPS,) + (s.shape if mode == "gather" else s.shape[1:]), s.dtype) for s in srcs]

    def body(*refs):
        src_refs, land_refs = refs[:n], refs[n:2 * n]
        send_sems, recv_sems = refs[2 * n], refs[2 * n + 1]
        token = refs[-1]
        for send, _ in _travel_copies(mode, src_refs, land_refs, send_sems, recv_sems):
            send().start()
        token[...] = jnp.zeros_like(token)

    hbm = lambda a: pltpu.HBM(a.shape, a.dtype)
    outs = pl.pallas_call(
        body, name=name,
        out_shape=(pltpu.SemaphoreType.DMA((3 * n,)), pltpu.SemaphoreType.DMA((3 * n,)), *[hbm(s) for s in srcs],
                   *[hbm(a) for a in lands], jax.ShapeDtypeStruct((SUBLANES, LANES), F32)),
        in_specs=[HBM_SPEC] * (2 * n),
        out_specs=(SEM_SPEC, SEM_SPEC, *[HBM_SPEC] * (2 * n), pl.BlockSpec(memory_space=pltpu.VMEM)),
        input_output_aliases={i: 2 + i for i in range(2 * n)},
        compiler_params=pltpu.CompilerParams(has_side_effects=DATAFLOW),
    )(*[pltpu.with_memory_space_constraint(a, pltpu.HBM) for a in list(srcs) + lands])
    return outs[0], outs[1], list(outs[2:2 + n]), list(outs[2 + n:2 + 2 * n]), outs[-1]


def _travel_wait(name, mode, send_sems, recv_sems, srcs, lands, after):
    n = len(srcs)

    def body(*refs):
        src_refs, land_refs = refs[:n], refs[n:2 * n]
        send_sems_, recv_sems_ = refs[2 * n], refs[2 * n + 1]
        for send, arrival in _travel_copies(mode, src_refs, land_refs, send_sems_, recv_sems_):
            send().wait_send()
            arrival().wait_recv()

    hbm = lambda a: pltpu.HBM(a.shape, a.dtype)
    outs = pl.pallas_call(
        body, name=name, out_shape=tuple(hbm(a) for a in list(srcs) + list(lands)),
        in_specs=[HBM_SPEC] * (2 * n) + [SEM_SPEC, SEM_SPEC, pl.BlockSpec(memory_space=pl.ANY)],
        out_specs=tuple([HBM_SPEC] * (2 * n)), input_output_aliases={i: i for i in range(2 * n)},
        compiler_params=pltpu.CompilerParams(has_side_effects=DATAFLOW),
    )(*srcs, *lands, send_sems, recv_sems, after)
    me = 2 * lax.axis_index("x") + lax.axis_index("y")
    own = [s[None] if mode == "gather" else lax.dynamic_slice_in_dim(s, me, 1, axis=0) for s in outs[:n]]
    return [lax.dynamic_update_slice(a, o, (me,) + (0,) * (a.ndim - 1)) for a, o in zip(outs[n:], own, strict=True)]


SUM_TILE_BYTES = 4 * 1024 * 1024


def _sum_rows(half, cols):
    best = ROW_ALIGN
    for t in range(ROW_ALIGN, half + 1, ROW_ALIGN):
        if half % t == 0 and N_CHIPS * t * cols * 4 <= SUM_TILE_BYTES:
            best = t
    return best


def _sum_cores(name, g, theirs, core):
    _, R, C = g.shape
    half = R // 2
    tr = _sum_rows(half, C)
    nb = half // tr

    def body(core_ref, g_ref, t_ref, o_ref):
        o_ref[...] = (g_ref[...] + t_ref[...]).astype(o_ref.dtype)

    grid_spec = pltpu.PrefetchScalarGridSpec(
        num_scalar_prefetch=1, grid=(nb,),
        in_specs=[pl.BlockSpec((N_CHIPS, tr, C), lambda i, core_ref: (0, core_ref[0] * nb + i, 0)),
                  pl.BlockSpec((N_CHIPS, tr, C), lambda i, core_ref: (0, i, 0))],
        out_specs=pl.BlockSpec((N_CHIPS, tr, C), lambda i, core_ref: (0, i, 0)))
    return pl.pallas_call(
        body, name=name, grid_spec=grid_spec, out_shape=jax.ShapeDtypeStruct((N_CHIPS, half, C), BF16),
        compiler_params=_params(("parallel",)),
    )(core, g, theirs)


def _sum_chips(name, parts):
    _, H, C = parts.shape
    tr = _sum_rows(H, C)

    def body(p_ref, o_ref):
        acc = p_ref[0].astype(F32)
        for k in range(1, N_CHIPS):
            acc = acc + p_ref[k].astype(F32)
        o_ref[...] = acc

    return pl.pallas_call(
        body, name=name, grid=(H // tr,),
        in_specs=[pl.BlockSpec((N_CHIPS, tr, C), lambda i: (0, i, 0))],
        out_specs=pl.BlockSpec((tr, C), lambda i: (i, 0)),
        out_shape=jax.ShapeDtypeStruct((H, C), F32),
        compiler_params=_params(("parallel",)),
    )(parts)


def _adamw_math(w, g, m, v):
    m = ADAM_B1 * m + (1.0 - ADAM_B1) * g
    v = ADAM_B2 * v + (1.0 - ADAM_B2) * (g * g)
    m_hat = m / (1.0 - ADAM_B1 ** ADAM_STEP)
    v_hat = v / (1.0 - ADAM_B2 ** ADAM_STEP)
    delta = -ADAM_LR * (m_hat / (jnp.sqrt(v_hat) + ADAM_EPS) + ADAM_WD * w)
    return delta, m, v


def _adamw(name, w, g, m, v):
    R, C = w.shape
    tr = R
    if R % SUBLANES == 0:
        for cand in range(SUBLANES, min(R, 256) + 1, SUBLANES):
            if R % cand == 0:
                tr = cand

    def body(w_ref, g_ref, m_ref, v_ref, d_ref, nm_ref, nv_ref):
        d, nm, nv = _adamw_math(w_ref[...], g_ref[...], m_ref[...], v_ref[...])
        d_ref[...] = d
        nm_ref[...] = nm
        nv_ref[...] = nv

    spec = pl.BlockSpec((tr, C), lambda i: (i, 0))
    shape = jax.ShapeDtypeStruct((R, C), F32)
    return pl.pallas_call(
        body, name=name, grid=(R // tr,), in_specs=[spec] * 4, out_specs=[spec] * 3, out_shape=[shape] * 3,
        compiler_params=_params(("parallel",)),
    )(w, g, m, v)


def _adamw_small(parts, w, m, v):
    n = parts.shape[0]

    def body(p_ref, w_ref, m_ref, v_ref, g_ref, d_ref, nm_ref, nv_ref):
        g = p_ref[0]
        for k in range(1, n):
            g = g + p_ref[k]
        d, nm, nv = _adamw_math(w_ref[...], g, m_ref[...], v_ref[...])
        g_ref[...] = g
        d_ref[...] = d
        nm_ref[...] = nm
        nv_ref[...] = nv

    shape = jax.ShapeDtypeStruct(w.shape, F32)
    return pl.pallas_call(body, name="adamw_small", out_shape=[shape] * 4, compiler_params=_params())(parts, w, m, v)


WEIGHTS = ['g_mix', 'w_in', 'rw_mu', 'rw_w0', 'rw_w_up', 'rw_a0', 'rw_a_up', 'rw_g_up', 'rw_k_k', 'rw_k_a',
           'rw_r_k', 'rw_ln_g', 'rw_ln_b', 'w_branch_a', 'w_branch_b', 'w_gate', 'b_gate', 'w_out', 'g_ffn', 'w_up',
           'conv_w', 'conv_b', 'w_down', 'g_ple', 'w_ple_gate', 'w_ple', 'g_final']
ARG_NAMES = (['x', 'p'] + WEIGHTS + ['loss_target'] + ['m_' + n for n in WEIGHTS] + ['v_' + n for n in WEIGHTS])
SHARDED = {'w_in': 1, 'rw_w_up': 1, 'rw_a_up': 1, 'rw_g_up': 1, 'w_branch_a': 1, 'w_branch_b': 1, 'w_gate': 1,
           'w_out': 0, 'w_up': 1, 'conv_w': 1, 'w_down': 0, 'w_ple_gate': 0, 'w_ple': 1}
SMALL = [n for n in WEIGHTS if n not in SHARDED]
WHOLE = ['conv_w']
FIRST_USED = ['w_in', 'rw_w_up', 'rw_a_up', 'rw_g_up', 'w_gate']
READ_BY_CHIP = ['w_gate', 'w_branch_a', 'w_branch_b', 'w_up', 'w_ple']
FIRST_DONE = [['w_up', 'w_down', 'w_ple_gate', 'w_ple'], ['w_out', 'w_branch_a', 'w_branch_b', 'w_gate']]
SPLIT = [n for n in SHARDED if n not in WHOLE]
PACK_ALIGN = SUBLANES * LANES


def _pack_rows(flat_parts):
    flat = jnp.concatenate(flat_parts, axis=1)
    n = flat.shape[1]
    padded = -(-n // PACK_ALIGN) * PACK_ALIGN
    flat = jnp.pad(flat, ((0, 0), (0, padded - n)))
    return flat.reshape(padded // LANES, LANES)


def _full_from_shards(stack, axis):
    _, R, C = stack.shape
    if axis == 0:
        return stack.reshape(N_CHIPS * R, C)
    return stack.transpose(1, 0, 2).reshape(R, N_CHIPS * C)


def _shards_from_full(full, axis):
    R, C = full.shape
    if axis == 0:
        return full.reshape(N_CHIPS, R // N_CHIPS, C)
    return full.reshape(R, N_CHIPS, C // N_CHIPS).transpose(1, 0, 2)


def kernel(x, p, g_mix, w_in, rw_mu, rw_w0, rw_w_up, rw_a0, rw_a_up, rw_g_up, rw_k_k, rw_k_a, rw_r_k, rw_ln_g, rw_ln_b, w_branch_a, w_branch_b, w_gate, b_gate, w_out, g_ffn, w_up, conv_w, conv_b, w_down, g_ple, w_ple_gate, w_ple, g_final, loss_target, m_g_mix, m_w_in, m_rw_mu, m_rw_w0, m_rw_w_up, m_rw_a0, m_rw_a_up, m_rw_g_up, m_rw_k_k, m_rw_k_a, m_rw_r_k, m_rw_ln_g, m_rw_ln_b, m_w_branch_a, m_w_branch_b, m_w_gate, m_b_gate, m_w_out, m_g_ffn, m_w_up, m_conv_w, m_conv_b, m_w_down, m_g_ple, m_w_ple_gate, m_w_ple, m_g_final, v_g_mix, v_w_in, v_rw_mu, v_rw_w0, v_rw_w_up, v_rw_a0, v_rw_a_up, v_rw_g_up, v_rw_k_k, v_rw_k_a, v_rw_r_k, v_rw_ln_g, v_rw_ln_b, v_w_branch_a, v_w_branch_b, v_w_gate, v_b_gate, v_w_out, v_g_ffn, v_w_up, v_conv_w, v_conv_b, v_w_down, v_g_ple, v_w_ple_gate, v_w_ple, v_g_final):
    given = dict(zip(ARG_NAMES, (x, p, g_mix, w_in, rw_mu, rw_w0, rw_w_up, rw_a0, rw_a_up, rw_g_up, rw_k_k, rw_k_a, rw_r_k, rw_ln_g, rw_ln_b, w_branch_a, w_branch_b, w_gate, b_gate, w_out, g_ffn, w_up, conv_w, conv_b, w_down, g_ple, w_ple_gate, w_ple, g_final, loss_target, m_g_mix, m_w_in, m_rw_mu, m_rw_w0, m_rw_w_up, m_rw_a0, m_rw_a_up, m_rw_g_up, m_rw_k_k, m_rw_k_a, m_rw_r_k, m_rw_ln_g, m_rw_ln_b, m_w_branch_a, m_w_branch_b, m_w_gate, m_b_gate, m_w_out, m_g_ffn, m_w_up, m_conv_w, m_conv_b, m_w_down, m_g_ple, m_w_ple_gate, m_w_ple, m_g_final, v_g_mix, v_w_in, v_rw_mu, v_rw_w0, v_rw_w_up, v_rw_a0, v_rw_a_up, v_rw_g_up, v_rw_k_k, v_rw_k_a, v_rw_r_k, v_rw_ln_g, v_rw_ln_b, v_w_branch_a, v_w_branch_b, v_w_gate, v_b_gate, v_w_out, v_g_ffn, v_w_up, v_conv_w, v_conv_b, v_w_down, v_g_ple, v_w_ple_gate, v_w_ple, v_g_final), strict=True))

    def two_d(name, prefix=""):
        a = given[prefix + name]
        if name == "g_final":
            return a.reshape(1, D_MODEL)
        if name == "rw_r_k":
            return a.reshape(1, RW_WIDTH)
        return a[0] if a.ndim == 3 else a

    cast = lambda n: two_d(n) if n in WHOLE else two_d(n).astype(BF16)
    whole = lambda names, stacks: {n: g if n in READ_BY_CHIP else _full_from_shards(g, SHARDED[n])
                                   for n, g in zip(names, stacks, strict=True)}
    late_names = [n for n in SHARDED if n not in FIRST_USED]
    late_sends, late_recvs, late_srcs, late_lands, token = _travel_start(
        "gather_late_start", "gather", [cast(n) for n in late_names])
    W = whole(FIRST_USED, _gather_chips([cast(n) for n in FIRST_USED]))
    for n in SMALL:
        W[n] = two_d(n)
    W["rw_r_k"] = W["rw_r_k"].reshape(RW_HEADS, RW_HEAD_DIM)
    W["g_mix"] = W["g_mix"] + token[0:1, 0:1]

    def late_weights(after):
        lands = _travel_wait("gather_late_wait", "gather", late_sends, late_recvs, late_srcs, late_lands, after)
        return whole(late_names, _share_halves("share_late", lands))

    core = lax.axis_index("c").astype(jnp.int32).reshape(1)
    early_names = [[n for n in SPLIT if n in group] for group in FIRST_DONE]
    rest_names = [n for n in SPLIT if not any(n in group for group in FIRST_DONE)]
    travelling = []

    def core_sums(tag, names, G):
        by_chip = [G[n] if n in READ_BY_CHIP else _shards_from_full(G[n], SHARDED[n]) for n in names]
        theirs = _swap_halves("swap_halves_" + tag, by_chip)
        return [_sum_cores("sum_cores_" + n, g, t, core) for n, g, t in zip(names, by_chip, theirs, strict=True)]

    def early_grads(G, stage):
        sends, recvs, srcs, lands, started = _travel_start(f"scatter_early{stage}_start", "scatter",
                                                           core_sums(f"early{stage}", early_names[stage], G))
        travelling.append((sends, recvs, srcs, lands))
        return started

    loss_part, grad_x, G = _local_step(x[0], p[0, 0], W, loss_target[0], late_weights, early_grads, by_chip=True)

    landed = dict(zip(rest_names, _scatter_chips(core_sums("rest", rest_names, G)), strict=True))
    for stage, (sends, recvs, srcs, lands) in enumerate(travelling):
        landed.update(zip(early_names[stage], _travel_wait(f"scatter_early{stage}_wait", "scatter", sends, recvs,
                                                           srcs, lands, landed[rest_names[0]]), strict=True))
    reduced = [_sum_chips("sum_chips_" + n, landed[n]) for n in SPLIT]
    shard_grads = dict(zip(SPLIT, _join_halves(reduced), strict=True))

    small_sizes = {n: two_d(n).shape[1] for n in SMALL}
    n_small = sum(small_sizes.values())
    whole_sizes = {n: G[n].shape[0] * G[n].shape[1] for n in WHOLE}
    n_whole = sum(whole_sizes.values())

    def pack_small(parts, rest):
        return _pack_rows([a.reshape(1, -1) for a in parts] + [rest])

    G["rw_r_k"] = G["rw_r_k"].reshape(1, RW_WIDTH)
    rest = jnp.zeros((1, n_whole + 1), F32)
    all_small = _gather_all(pack_small([G[n] for n in SMALL] + [G[n] for n in WHOLE], loss_part))
    gs, ds, nms, nvs = _adamw_small(all_small, pack_small([two_d(n) for n in SMALL], rest),
                                    pack_small([two_d(n, "m_") for n in SMALL], rest),
                                    pack_small([two_d(n, "v_") for n in SMALL], rest))
    gs, ds, nms, nvs = (a.reshape(-1) for a in (gs, ds, nms, nvs))
    loss = gs[n_small + n_whole]
    chip = 2 * lax.axis_index("x") + lax.axis_index("y")
    off = n_small
    for n in WHOLE:
        full = gs[off:off + whole_sizes[n]].reshape(G[n].shape)
        off += whole_sizes[n]
        width = two_d(n).shape[1]
        shard_grads[n] = lax.dynamic_slice_in_dim(full, chip * width, width, axis=1)

    grads, deltas, new_m, new_v = {}, {}, {}, {}
    for n in SHARDED:
        g = shard_grads[n]
        d, nm, nv = _adamw("adamw_" + n, two_d(n), g, two_d(n, "m_"), two_d(n, "v_"))
        grads[n], deltas[n], new_m[n], new_v[n] = g, d, nm, nv
    off = 0
    for n in SMALL:
        sl = slice(off, off + small_sizes[n])
        off += small_sizes[n]
        grads[n], deltas[n], new_m[n], new_v[n] = gs[sl], ds[sl], nms[sl], nvs[sl]
    outs = [loss, grad_x[None]]
    for table in (grads, deltas, new_m, new_v):
        outs += [table[n].reshape(given[n].shape) for n in WEIGHTS]
    return tuple(outs)
```

```python
import functools
import math

import jax
import jax.numpy as jnp
import numpy as np
from jax import lax
from jax.experimental import pallas as pl
from jax.experimental.pallas import tpu as pltpu

F32 = jnp.float32
BF16 = jnp.bfloat16

D_MODEL = 1024
NORM_EPS = 1e-6
RW_HEADS = 8
RW_HEAD_DIM = 64
RW_WIDTH = 512
RW_LN_EPS = 64e-5
ATT_GROUP_DILATION = (1, 4, 16)
ATT_BLOCK = 128
ATT_HEADS = 12
ATT_HEAD_DIM = 64
ATT_GROUP_WIDTH = 256
ATT_WIDTH = 768
D_FF = 3072

ADAM_LR = 0.001
ADAM_B1 = 0.9
ADAM_B2 = 0.999
ADAM_EPS = 1e-08
ADAM_WD = 0.01
ADAM_STEP = 10

SUBLANES = 8
LANES = 128
VMEM_LIMIT = 56 * 1024 * 1024
N_CHIPS = 4
N_DEV = 8
MESH = pl.DeviceIdType.MESH


def _params(sem=None):
    return pltpu.CompilerParams(dimension_semantics=sem, vmem_limit_bytes=VMEM_LIMIT)


def _pick(dim, pref):
    if dim % LANES != 0 or dim <= pref:
        return dim
    best = LANES
    for t in range(LANES, pref + 1, LANES):
        if dim % t == 0:
            best = t
    return best


def _mm(name, a, b, mode, out_dtype=F32, add=None, tm=1024, tn=1024, tk=1024, out_by_chip=False):
    by_chip = b.ndim == 3
    b_rows, b_cols = (b.shape[1], N_CHIPS * b.shape[2]) if by_chip else b.shape
    if mode == "nn":
        (M, K), (K2, N) = a.shape, (b_rows, b_cols)
    elif mode == "nt":
        (M, K), (N, K2) = a.shape, (b_rows, b_cols)
    else:
        (K, M), (K2, N) = a.shape, (b_rows, b_cols)
    assert K == K2, (name, a.shape, b.shape, mode)
    assert not (by_chip and mode == "tn") and not (out_by_chip and add is not None), name
    tm = _pick(M, tm)
    n_cut, k_cut = out_by_chip or (by_chip and mode == "nn"), by_chip and mode == "nt"
    tn = _pick(N // N_CHIPS, tn) if n_cut else _pick(N, tn)
    tk = _pick(K // N_CHIPS, tk) if k_cut else _pick(K, tk)
    nk = K // tk
    per_n = (N // N_CHIPS) // tn if n_cut else 1
    per_k = (K // N_CHIPS) // tk if k_cut else 1
    if mode == "nn":
        a_spec = pl.BlockSpec((tm, tk), lambda i, j, k: (i, k))
        b_spec = (pl.BlockSpec((None, tk, tn), lambda i, j, k: (j // per_n, k, j % per_n)) if by_chip
                  else pl.BlockSpec((tk, tn), lambda i, j, k: (k, j)))
        dims = (((1,), (0,)), ((), ()))
    elif mode == "nt":
        a_spec = pl.BlockSpec((tm, tk), lambda i, j, k: (i, k))
        b_spec = (pl.BlockSpec((None, tn, tk), lambda i, j, k: (k // per_k, j, k % per_k)) if by_chip
                  else pl.BlockSpec((tn, tk), lambda i, j, k: (j, k)))
        dims = (((1,), (1,)), ((), ()))
    else:
        a_spec = pl.BlockSpec((tk, tm), lambda i, j, k: (k, i))
        b_spec = pl.BlockSpec((tk, tn), lambda i, j, k: (k, j))
        dims = (((0,), (0,)), ((), ()))
    if out_by_chip:
        o_spec = pl.BlockSpec((None, tm, tn), lambda i, j, k: (j // per_n, i, j % per_n))
        out_shape = jax.ShapeDtypeStruct((N_CHIPS, M, N // N_CHIPS), out_dtype)
    else:
        o_spec = pl.BlockSpec((tm, tn), lambda i, j, k: (i, j))
        out_shape = jax.ShapeDtypeStruct((M, N), out_dtype)
    has_add = add is not None

    def body(*refs):
        if has_add:
            a_ref, b_ref, add_ref, o_ref, acc_ref = refs
        else:
            a_ref, b_ref, o_ref, acc_ref = refs
        k = pl.program_id(2)
        part = lax.dot_general(a_ref[...].astype(BF16), b_ref[...].astype(BF16), dims,
                               preferred_element_type=F32)

        @pl.when(k == 0)
        def _():
            acc_ref[...] = part

        @pl.when(k > 0)
        def _():
            acc_ref[...] += part

        @pl.when(k == nk - 1)
        def _():
            res = acc_ref[...]
            if has_add:
                res = res + add_ref[...].astype(F32)
            o_ref[...] = res.astype(o_ref.dtype)

    ins = [a, b] + ([add] if has_add else [])
    in_specs = [a_spec, b_spec] + ([o_spec] if has_add else [])
    return pl.pallas_call(
        body, name=name, grid=(M // tm, N // tn, nk),
        in_specs=in_specs, out_specs=o_spec, out_shape=out_shape,
        scratch_shapes=[pltpu.VMEM((tm, tn), F32)],
        compiler_params=_params(("parallel", "parallel", "arbitrary")),
    )(*ins)


def _rowwise(name, fn, T, tT, rows=(), prevs=(), nexts=(), consts=(), outs=()):
    n = T // tT
    per8 = tT // SUBLANES
    in_specs, ins = [], []
    for arr in rows:
        in_specs.append(pl.BlockSpec((tT, arr.shape[1]), lambda i: (i, 0)))
        ins.append(arr)
    for arr in prevs:
        in_specs.append(pl.BlockSpec((SUBLANES, arr.shape[1]), lambda i: (jnp.maximum(i * per8 - 1, 0), 0)))
        ins.append(arr)
    for arr in nexts:
        in_specs.append(pl.BlockSpec((SUBLANES, arr.shape[1]),
                                     lambda i: (jnp.minimum((i + 1) * per8, T // SUBLANES - 1), 0)))
        ins.append(arr)
    for arr in consts:
        in_specs.append(pl.BlockSpec(arr.shape, lambda i, nd=arr.ndim: (0,) * nd))
        ins.append(arr)
    out_specs, out_shapes = [], []
    for o in outs:
        if o[0] == "row":
            out_specs.append(pl.BlockSpec((tT, o[1]), lambda i: (i, 0)))
            out_shapes.append(jax.ShapeDtypeStruct((T, o[1]), o[2]))
        else:
            out_specs.append(pl.BlockSpec(o[1], lambda i: (0, 0)))
            out_shapes.append(jax.ShapeDtypeStruct(o[1], F32))
    nr, npv, nnx, nc = len(rows), len(prevs), len(nexts), len(consts)
    n_in = nr + npv + nnx + nc

    def body(*refs):
        i = pl.program_id(0)
        vals = [r[...] for r in refs[:n_in]]
        res = fn(i, n, vals[:nr], vals[nr:nr + npv], vals[nr + npv:nr + npv + nnx], vals[nr + npv + nnx:])
        for o, o_ref, val in zip(outs, refs[n_in:], res, strict=True):
            if o[0] == "row":
                o_ref[...] = val.astype(o_ref.dtype)
            else:
                @pl.when(i == 0)
                def _(o_ref=o_ref, val=val):
                    o_ref[...] = val.astype(F32)

                @pl.when(i > 0)
                def _(o_ref=o_ref, val=val):
                    o_ref[...] += val.astype(F32)

    res = pl.pallas_call(
        body, name=name, grid=(n,), in_specs=in_specs, out_specs=out_specs, out_shape=out_shapes,
        compiler_params=_params(("arbitrary",)),
    )(*ins)
    return list(res)


def _shift_down(x, prev8, i, s):
    rolled = pltpu.roll(x, s, 0)
    head = pltpu.roll(prev8, s, 0)
    head = jnp.where(i == 0, jnp.zeros_like(head), head)
    rid = lax.broadcasted_iota(jnp.int32, head.shape, 0)
    first = jnp.where(rid < s, head, rolled[:SUBLANES])
    if x.shape[0] == SUBLANES:
        return first
    return jnp.concatenate([first, rolled[SUBLANES:]], axis=0)


def _shift_up(x, next8, i, n, s):
    tT = x.shape[0]
    rolled = pltpu.roll(x, tT - s, 0)
    tail = pltpu.roll(next8, SUBLANES - s, 0)
    tail = jnp.where(i == n - 1, jnp.zeros_like(tail), tail)
    rid = lax.broadcasted_iota(jnp.int32, tail.shape, 0)
    last = jnp.where(rid >= SUBLANES - s, tail, rolled[tT - SUBLANES:])
    return jnp.concatenate([rolled[:tT - SUBLANES], last], axis=0)


def _colsum(x):
    return jnp.sum(x, axis=0, keepdims=True)


def _segsum(x, bd):
    return jnp.dot(x, bd, precision=lax.Precision.HIGH, preferred_element_type=F32)


def _block_diag_ones(width, seg):
    idx = np.arange(width) // seg
    return jnp.asarray((idx[:, None] == idx[None, :]).astype(np.float32))


def _sigmoid(z):
    return 1.0 / (1.0 + jnp.exp(-z))


def _softplus(z):
    return jnp.maximum(z, 0.0) + jnp.log(1.0 + jnp.exp(-jnp.abs(z)))


def _rms_fwd(x, g):
    r = lax.rsqrt(jnp.mean(x * x, axis=-1, keepdims=True) + NORM_EPS)
    return x * r * g


def _rms_bwd(x, g, dy):
    r = lax.rsqrt(jnp.mean(x * x, axis=-1, keepdims=True) + NORM_EPS)
    gdy = dy * g
    dx = r * (gdy - x * (r * r) * jnp.mean(x * gdy, axis=-1, keepdims=True))
    return dx, dy * x * r


GELU_C = math.sqrt(2.0 / math.pi)


def _gelu(x):
    return 0.5 * x * (1.0 + jnp.tanh(GELU_C * (x + 0.044715 * x * x * x)))


def _gelu_and_grad(x):
    th = jnp.tanh(GELU_C * (x + 0.044715 * x * x * x))
    half = 0.5 * (1.0 + th)
    return x * half, half + 0.5 * x * (1.0 - th * th) * GELU_C * (1.0 + 3.0 * 0.044715 * x * x)


RW_CHUNK = 64
NN = (((1,), (0,)), ((), ()))
NT = (((1,), (1,)), ((), ()))
TN = (((0,), (0,)), ((), ()))


def _hdot(a, b, dims):
    return lax.dot_general(a, b, dims, precision=lax.Precision.HIGH, preferred_element_type=F32)


def _ldot(a, b, dims):
    return lax.dot_general(a.astype(BF16), b.astype(BF16), dims, preferred_element_type=F32)


def _chunk_masks():
    ti = lax.broadcasted_iota(jnp.int32, (RW_CHUNK, RW_CHUNK), 0)
    tj = lax.broadcasted_iota(jnp.int32, (RW_CHUNK, RW_CHUNK), 1)
    return tj <= ti, tj < ti, (ti == tj).astype(F32)


def _head(x, h):
    return x[:, h * RW_HEAD_DIM:(h + 1) * RW_HEAD_DIM]


def _heads(fn):
    return [fn(h) for h in range(RW_HEADS)]


def _chunk_rows(r, lw, k, a, b, incl_f):
    c = _hdot(incl_f, lw, NN)
    e_prev, e_neg, e_pos = jnp.exp(c - lw), jnp.exp(-c), jnp.exp(c)
    return dict(At=a * e_prev, Bt=b * e_neg, Kt=k * e_neg, Rt=r * e_pos, e_prev=e_prev, e_neg=e_neg, e_pos=e_pos)


def _chunk_coeffs(q, incl, strict):
    A1 = _heads(lambda h: jnp.where(strict, _hdot(_head(q["At"], h), _head(q["Bt"], h), NT), 0.0))
    A2 = _heads(lambda h: jnp.where(strict, _hdot(_head(q["At"], h), _head(q["Kt"], h), NT), 0.0))
    W1 = _heads(lambda h: jnp.where(incl, _hdot(_head(q["Rt"], h), _head(q["Bt"], h), NT), 0.0))
    W2 = _heads(lambda h: jnp.where(incl, _hdot(_head(q["Rt"], h), _head(q["Kt"], h), NT), 0.0))
    return A1, A2, W1, W2


def _rwkv_chunk_prep(r, lw, k, a, b, v):
    T = r.shape[0]
    nC = T // RW_CHUNK
    H, N = RW_HEADS, RW_HEAD_DIM

    def body(r_ref, lw_ref, k_ref, a_ref, b_ref, v_ref,
             at_ref, bt_ref, kt_ref, rt_ref, a2v_ref, w2v_ref, ti_ref, w1_ref, a2_ref, w2_ref, pl_ref):
        incl, strict, eye = _chunk_masks()
        q = _chunk_rows(r_ref[...], lw_ref[...], k_ref[...], a_ref[...], b_ref[...], incl.astype(F32))
        at_ref[...], bt_ref[...], kt_ref[...], rt_ref[...] = q["At"], q["Bt"], q["Kt"], q["Rt"]
        pl_ref[0] = jnp.broadcast_to(q["e_pos"][RW_CHUNK - 1:RW_CHUNK, :], (SUBLANES, RW_WIDTH))
        A1, A2, W1, W2 = _chunk_coeffs(q, incl, strict)
        V = v_ref[...]
        a2v_ref[...] = jnp.concatenate(_heads(lambda h: _hdot(A2[h], _head(V, h), NN)), axis=1)
        w2v_ref[...] = jnp.concatenate(_heads(lambda h: _hdot(W2[h], _head(V, h), NN)), axis=1)
        tinv, pw = [eye + m for m in A1], A1
        for _ in range(5):
            pw = [_hdot(m, m, NN) for m in pw]
            tinv = [t + _hdot(t, m, NN) for t, m in zip(tinv, pw, strict=True)]
        for h in range(H):
            ti_ref[0, h] = tinv[h]
            w1_ref[0, h] = W1[h]
            a2_ref[0, h] = A2[h]
            w2_ref[0, h] = W2[h]

    row_spec = pl.BlockSpec((RW_CHUNK, RW_WIDTH), lambda n: (n, 0))
    st_spec = pl.BlockSpec((1, H, N, N), lambda n: (n, 0, 0, 0))
    row_shape = jax.ShapeDtypeStruct((T, RW_WIDTH), F32)
    st_shape = jax.ShapeDtypeStruct((nC, H, N, N), F32)
    return pl.pallas_call(
        body, name="rwkv_chunk_prep", grid=(nC,),
        in_specs=[row_spec] * 6,
        out_specs=[row_spec] * 6 + [st_spec] * 4 + [pl.BlockSpec((1, SUBLANES, RW_WIDTH), lambda n: (n, 0, 0))],
        out_shape=[row_shape] * 6 + [st_shape] * 4 + [jax.ShapeDtypeStruct((nC, SUBLANES, RW_WIDTH), F32)],
        compiler_params=_params(("parallel",)),
    )(r, lw, k, a, b, v)


def _rwkv_chunk_fwd(v, at, bt, kt, rt, a2v, w2v, tinv, w1, plast):
    T = v.shape[0]
    nC = T // RW_CHUNK
    H, N = RW_HEADS, RW_HEAD_DIM

    def body(v_ref, at_ref, bt_ref, kt_ref, rt_ref, a2v_ref, w2v_ref, ti_ref, w1_ref, pl_ref,
             y_ref, sa_ref, s0_ref, S_ref):
        @pl.when(pl.program_id(0) == 0)
        def _():
            S_ref[...] = jnp.zeros_like(S_ref)

        V, At, Bt, Kt, Rt = v_ref[...], at_ref[...], bt_ref[...], kt_ref[...], rt_ref[...]
        A2V, W2V, p_last = a2v_ref[...], w2v_ref[...], pl_ref[0, 0:1, :]
        S0 = _heads(lambda h: S_ref[h])
        for h in range(H):
            s0_ref[0, h] = S0[h]
        Z = _heads(lambda h: _hdot(_head(At, h), S0[h], NT) + _head(A2V, h))
        Sa = _heads(lambda h: _hdot(ti_ref[0, h], Z[h], NN))
        X = _heads(lambda h: S0[h] + _hdot(Sa[h], _head(Bt, h), TN) + _hdot(_head(V, h), _head(Kt, h), TN))
        for h in range(H):
            S_ref[h] = X[h] * _head(p_last, h)
        Y = _heads(lambda h: _hdot(_head(Rt, h), S0[h], NT) + _hdot(w1_ref[0, h], Sa[h], NN) + _head(W2V, h))
        y_ref[...] = jnp.concatenate(Y, axis=1)
        sa_ref[...] = jnp.concatenate(Sa, axis=1)

    row_spec = pl.BlockSpec((RW_CHUNK, RW_WIDTH), lambda n: (n, 0))
    st_spec = pl.BlockSpec((1, H, N, N), lambda n: (n, 0, 0, 0))
    row_shape = jax.ShapeDtypeStruct((T, RW_WIDTH), F32)
    return pl.pallas_call(
        body, name="rwkv_chunk_fwd", grid=(nC,),
        in_specs=[row_spec] * 7 + [st_spec, st_spec, pl.BlockSpec((1, SUBLANES, RW_WIDTH), lambda n: (n, 0, 0))],
        out_specs=[row_spec, row_spec, st_spec],
        out_shape=[row_shape, row_shape, jax.ShapeDtypeStruct((nC, H, N, N), F32)],
        scratch_shapes=[pltpu.VMEM((H, N, N), F32)],
        compiler_params=_params(("arbitrary",)),
    )(v, at, bt, kt, rt, a2v, w2v, tinv, w1, plast)


def _rwkv_chunk_bwd(r, lw, k, a, b, v, dy, s0, tinv, w1, a2, w2, sa):
    T = r.shape[0]
    nC = T // RW_CHUNK
    H, N = RW_HEADS, RW_HEAD_DIM

    def body(r_ref, lw_ref, k_ref, a_ref, b_ref, v_ref, dy_ref, s0_ref, ti_ref, w1_ref, a2_ref, w2_ref, sa_ref,
             dr_ref, dlw_ref, dk_ref, da_ref, db_ref, dv_ref, dS_ref):
        @pl.when(pl.program_id(0) == 0)
        def _():
            dS_ref[...] = jnp.zeros_like(dS_ref)

        incl, strict, _ = _chunk_masks()
        incl_f = incl.astype(F32)
        q = _chunk_rows(r_ref[...], lw_ref[...], k_ref[...], a_ref[...], b_ref[...], incl_f)
        At, Bt, Kt, Rt = q["At"], q["Bt"], q["Kt"], q["Rt"]
        A2, W1, W2 = (_heads(lambda h, ref=ref: ref[0, h]) for ref in (a2_ref, w1_ref, w2_ref))
        V, dY, Sa = v_ref[...], dy_ref[...], sa_ref[...]
        hd = _head
        p_last = q["e_pos"][RW_CHUNK - 1:RW_CHUNK, :]
        S0 = _heads(lambda h: s0_ref[0, h])
        G = _heads(lambda h: dS_ref[h] * hd(p_last, h))
        X = _heads(lambda h: S0[h] + _hdot(hd(Sa, h), hd(Bt, h), TN) + _hdot(hd(V, h), hd(Kt, h), TN))
        dc_last = jnp.concatenate(_heads(lambda h: jnp.sum(G[h] * X[h], axis=0, keepdims=True)), axis=1)
        dSa = _heads(lambda h: _hdot(hd(Bt, h), G[h], NT) + _hdot(W1[h], hd(dY, h), TN))
        dZ = _heads(lambda h: _hdot(ti_ref[0, h], dSa[h], TN))
        for h in range(H):
            dS_ref[h] = G[h] + _hdot(dZ[h], hd(At, h), TN) + _hdot(hd(dY, h), hd(Rt, h), TN)
        dA1 = _heads(lambda h: jnp.where(strict, _ldot(dZ[h], hd(Sa, h), NT), 0.0))
        dA2 = _heads(lambda h: jnp.where(strict, _ldot(dZ[h], hd(V, h), NT), 0.0))
        dW1 = _heads(lambda h: jnp.where(incl, _ldot(hd(dY, h), hd(Sa, h), NT), 0.0))
        dW2 = _heads(lambda h: jnp.where(incl, _ldot(hd(dY, h), hd(V, h), NT), 0.0))
        cat = lambda fn: jnp.concatenate(_heads(fn), axis=1)
        dV = cat(lambda h: _ldot(A2[h], dZ[h], TN) + _ldot(W2[h], hd(dY, h), TN) + _ldot(hd(Kt, h), G[h], NT))
        dAt = cat(lambda h: _ldot(dA1[h], hd(Bt, h), NN) + _ldot(dA2[h], hd(Kt, h), NN) + _ldot(dZ[h], S0[h], NN))
        dBt = cat(lambda h: _ldot(dA1[h], hd(At, h), TN) + _ldot(dW1[h], hd(Rt, h), TN) + _ldot(hd(Sa, h), G[h], NN))
        dKt = cat(lambda h: _ldot(dA2[h], hd(At, h), TN) + _ldot(dW2[h], hd(Rt, h), TN) + _ldot(hd(V, h), G[h], NN))
        dRt = cat(lambda h: _ldot(hd(dY, h), S0[h], NN) + _ldot(dW1[h], hd(Bt, h), NN) + _ldot(dW2[h], hd(Kt, h), NN))
        last_row = lax.broadcasted_iota(jnp.int32, (RW_CHUNK, RW_WIDTH), 0) == RW_CHUNK - 1
        dc_prev = dAt * At
        dc = dc_prev + dRt * Rt - dBt * Bt - dKt * Kt + jnp.where(last_row, dc_last, 0.0)
        dr_ref[...] = dRt * q["e_pos"]
        dlw_ref[...] = _hdot(incl_f, dc, TN) - dc_prev
        dk_ref[...] = dKt * q["e_neg"]
        da_ref[...] = dAt * q["e_prev"]
        db_ref[...] = dBt * q["e_neg"]
        dv_ref[...] = dV

    rev = lambda n: nC - 1 - n
    row_spec = pl.BlockSpec((RW_CHUNK, RW_WIDTH), lambda n: (rev(n), 0))
    st_spec = pl.BlockSpec((1, H, N, N), lambda n: (rev(n), 0, 0, 0))
    row_shape = jax.ShapeDtypeStruct((T, RW_WIDTH), F32)
    return pl.pallas_call(
        body, name="rwkv_chunk_bwd", grid=(nC,),
        in_specs=[row_spec] * 7 + [st_spec] * 5 + [row_spec], out_specs=[row_spec] * 6,
        out_shape=[row_shape] * 6, scratch_shapes=[pltpu.VMEM((H, N, N), F32)],
        compiler_params=_params(("arbitrary",)),
    )(r, lw, k, a, b, v, dy, s0, tinv, w1, a2, w2, sa)


def _alibi_slope(head):
    return float(np.float32(2.0 ** (-8.0 * (head + 1) / ATT_HEADS)))


ATT_SPAN = ATT_BLOCK * max(ATT_GROUP_DILATION)
ATT_PAIR_WIDTH = 2 * ATT_HEAD_DIM
ATT_SIDE_BY_SIDE = 8


def _pair_slope(g, hp, j):
    return jnp.where(hp == 0, _alibi_slope(4 * g + j), _alibi_slope(4 * g + 2 + j))


def _att_rows(mi, r, d):
    start = mi * ATT_BLOCK * d + r
    return pl.ds(start, ATT_BLOCK) if d == 1 else pl.ds(start, ATT_BLOCK, stride=d)


def _att_masks():
    qi = lax.broadcasted_iota(jnp.int32, (ATT_BLOCK, ATT_BLOCK), 0)
    kj = lax.broadcasted_iota(jnp.int32, (ATT_BLOCK, ATT_BLOCK), 1)
    return qi, kj


NEG = -1e30


def _att_logits(q, k, slope_d, steps, valid):
    s = lax.dot_general(q.astype(BF16), k.astype(BF16), (((1,), (1,)), ((), ())),
                        preferred_element_type=F32) * (ATT_HEAD_DIM ** -0.5)
    return jnp.where(valid, s - slope_d * steps.astype(F32), NEG)


def _att_fwd(p_att, g):
    T = p_att.shape[0]
    d = ATT_GROUP_DILATION[g]
    W = ATT_PAIR_WIDTH
    nb = T // ATT_SPAN
    mb = ATT_SPAN // (ATT_BLOCK * d)

    def body(q_ref, kc_ref, kp_ref, vc_ref, vp_ref, o_ref, l_ref):
        hp, n = pl.program_id(0), pl.program_id(1)
        qi, kj = _att_masks()
        slopes = [_pair_slope(g, hp, j) * d for j in range(2)]
        blocks = [(r, mi) for r in range(d) for mi in range(mb)]
        for at in range(0, len(blocks), ATT_SIDE_BY_SIDE):
            tasks = []
            for r, mi in blocks[at:at + ATT_SIDE_BY_SIDE]:
                rows = _att_rows(mi, r, d)
                if mi > 0:
                    prev = _att_rows(mi - 1, r, d)
                    kp, vp, has_prev = kc_ref[prev, :], vc_ref[prev, :], True
                else:
                    prev = _att_rows(mb - 1, r, d)
                    kp, vp, has_prev = kp_ref[prev, :], vp_ref[prev, :], n > 0
                q, kc, vc = q_ref[rows, :], kc_ref[rows, :], vc_ref[rows, :]
                for j in range(2):
                    sl = slice(j * ATT_HEAD_DIM, (j + 1) * ATT_HEAD_DIM)
                    tasks.append((q[:, sl], kc[:, sl], kp[:, sl], vc[:, sl], vp[:, sl], has_prev, slopes[j]))
            lc = [_att_logits(t[0], t[1], t[6], qi - kj, kj <= qi) for t in tasks]
            lp = [_att_logits(t[0], t[2], t[6], qi - kj + ATT_BLOCK, (kj >= qi) & t[5]) for t in tasks]
            mx = [jnp.maximum(jnp.max(a, axis=1, keepdims=True), jnp.max(b, axis=1, keepdims=True))
                  for a, b in zip(lc, lp, strict=True)]
            ec = [jnp.exp(a - m) for a, m in zip(lc, mx, strict=True)]
            ep = [jnp.exp(b - m) for b, m in zip(lp, mx, strict=True)]
            den = [jnp.sum(a, axis=1, keepdims=True) + jnp.sum(b, axis=1, keepdims=True)
                   for a, b in zip(ec, ep, strict=True)]
            inv = [1.0 / s for s in den]
            outs = [jnp.dot((a * i).astype(BF16), t[3].astype(BF16), preferred_element_type=F32)
                    + jnp.dot((b * i).astype(BF16), t[4].astype(BF16), preferred_element_type=F32)
                    for a, b, i, t in zip(ec, ep, inv, tasks, strict=True)]
            lses = [jnp.broadcast_to(m + jnp.log(s), (ATT_BLOCK, ATT_HEAD_DIM)) for m, s in zip(mx, den, strict=True)]
            for i, (r, mi) in enumerate(blocks[at:at + ATT_SIDE_BY_SIDE]):
                rows = _att_rows(mi, r, d)
                o_ref[rows, :] = jnp.concatenate(outs[2 * i:2 * i + 2], axis=1)
                l_ref[rows, :] = jnp.concatenate(lses[2 * i:2 * i + 2], axis=1)

    def spec(col0, prev):
        if prev:
            return pl.BlockSpec((ATT_SPAN, W), lambda hp, n: (jnp.maximum(n - 1, 0), col0 + 2 * g + hp))
        return pl.BlockSpec((ATT_SPAN, W), lambda hp, n: (n, col0 + 2 * g + hp))

    o_spec = pl.BlockSpec((ATT_SPAN, W), lambda hp, n: (n, hp))
    o, l = pl.pallas_call(
        body, name=f"att_fwd_g{g}", grid=(2, nb),
        in_specs=[spec(0, False), spec(6, False), spec(6, True), spec(12, False), spec(12, True)],
        out_specs=[o_spec, o_spec],
        out_shape=[jax.ShapeDtypeStruct((T, ATT_GROUP_WIDTH), F32)] * 2,
        compiler_params=_params(("parallel", "arbitrary")),
    )(p_att, p_att, p_att, p_att, p_att)
    return o, l


def _att_bwd(p_att, o, l, do, dl, g):
    T = p_att.shape[0]
    d = ATT_GROUP_DILATION[g]
    W = ATT_PAIR_WIDTH
    nb = T // ATT_SPAN
    mb = ATT_SPAN // (ATT_BLOCK * d)
    scale = ATT_HEAD_DIM ** -0.5

    def body(q_ref, k_ref, v_ref, o_ref, l_ref, do_ref, dl_ref,
             qn_ref, on_ref, ln_ref, don_ref, dln_ref, dq_ref, dk_ref, dv_ref, carry_ref):
        hp, n = pl.program_id(0), pl.program_id(1)
        qi, kj = _att_masks()

        @pl.when(n == 0)
        def _():
            carry_ref[...] = jnp.zeros_like(carry_ref)

        slopes = [_pair_slope(g, hp, j) * d for j in range(2)]
        blocks = [(r, mi) for r in range(d) for mi in range(mb)]
        side_by_side = ATT_SIDE_BY_SIDE // 2
        carry = None
        for at in range(0, len(blocks), side_by_side):
            tasks = []
            for r, mi in blocks[at:at + side_by_side]:
                rows = _att_rows(mi, r, d)
                if mi < mb - 1:
                    nrows = _att_rows(mi + 1, r, d)
                    nxt = (q_ref[nrows, :], o_ref[nrows, :], l_ref[nrows, :], do_ref[nrows, :], dl_ref[nrows, :])
                    has_next = True
                else:
                    nrows = _att_rows(0, r, d)
                    nxt = (qn_ref[nrows, :], on_ref[nrows, :], ln_ref[nrows, :], don_ref[nrows, :],
                           dln_ref[nrows, :])
                    has_next = n < nb - 1
                cur = (q_ref[rows, :], o_ref[rows, :], l_ref[rows, :], do_ref[rows, :], dl_ref[rows, :])
                k_all, v_all = k_ref[rows, :], v_ref[rows, :]
                for j in range(2):
                    sl = slice(j * ATT_HEAD_DIM, (j + 1) * ATT_HEAD_DIM)
                    for blk, steps, valid in ((cur, qi - kj, kj <= qi),
                                              (nxt, qi - kj + ATT_BLOCK, (kj >= qi) & has_next)):
                        q, o_, lse, do_, dlse = (z[:, sl] for z in blk)
                        tasks.append(dict(q=q, o=o_, lse=lse[:, :1], do=do_, dlse=dlse[:, :1], steps=steps,
                                          valid=valid, k=k_all[:, sl], vb=v_all[:, sl].astype(BF16),
                                          slope=slopes[j]))
            p = [jnp.exp(_att_logits(t["q"], t["k"], t["slope"], t["steps"], t["valid"]) - t["lse"]) for t in tasks]
            dp = [lax.dot_general(t["do"].astype(BF16), t["vb"], (((1,), (1,)), ((), ())),
                                  preferred_element_type=F32) for t in tasks]
            dsum = [jnp.sum(t["do"] * t["o"], axis=1, keepdims=True) for t in tasks]
            ds = [a * (b - s + t["dlse"]) for a, b, s, t in zip(p, dp, dsum, tasks, strict=True)]
            dv_ = [jnp.dot(a.T.astype(BF16), t["do"].astype(BF16), preferred_element_type=F32)
                   for a, t in zip(p, tasks, strict=True)]
            dk_ = [jnp.dot(a.T.astype(BF16), t["q"].astype(BF16), preferred_element_type=F32) * scale
                   for a, t in zip(ds, tasks, strict=True)]
            dq_ = [jnp.dot(a.astype(BF16), t["k"].astype(BF16), preferred_element_type=F32) * scale
                   for a, t in zip(ds, tasks, strict=True)]
            for i, (r, mi) in enumerate(blocks[at:at + side_by_side]):
                rows = _att_rows(mi, r, d)
                b = 4 * i
                if mi == 0:
                    carry = carry_ref[r]
                dq_ref[rows, :] = jnp.concatenate([dq_[b], dq_[b + 2]], axis=1) + carry
                carry = jnp.concatenate([dq_[b + 1], dq_[b + 3]], axis=1)
                if mi == mb - 1:
                    carry_ref[r] = carry
                dk_ref[rows, :] = jnp.concatenate([dk_[b] + dk_[b + 1], dk_[b + 2] + dk_[b + 3]], axis=1)
                dv_ref[rows, :] = jnp.concatenate([dv_[b] + dv_[b + 1], dv_[b + 2] + dv_[b + 3]], axis=1)

    head_rows = ATT_BLOCK * d
    nxt_n = lambda n: jnp.minimum((n + 1) * mb, T // head_rows - 1)
    cur_p = lambda col0: pl.BlockSpec((ATT_SPAN, W), lambda hp, n: (n, col0 + 2 * g + hp))
    cur_o = pl.BlockSpec((ATT_SPAN, W), lambda hp, n: (n, hp))
    nxt_o = pl.BlockSpec((head_rows, W), lambda hp, n: (nxt_n(n), hp))
    dq, dk, dv = pl.pallas_call(
        body, name=f"att_bwd_g{g}", grid=(2, nb),
        in_specs=[cur_p(0), cur_p(6), cur_p(12), cur_o, cur_o, cur_o, cur_o,
                  pl.BlockSpec((head_rows, W), lambda hp, n: (nxt_n(n), 2 * g + hp)), nxt_o, nxt_o, nxt_o, nxt_o],
        out_specs=[cur_o, cur_o, cur_o],
        out_shape=[jax.ShapeDtypeStruct((T, ATT_GROUP_WIDTH), F32)] * 3,
        scratch_shapes=[pltpu.VMEM((d, ATT_BLOCK, W), F32)],
        compiler_params=_params(("parallel", "arbitrary")),
    )(p_att, p_att, p_att, o, l, do, dl, p_att, o, l, do, dl)
    return dq, dk, dv


FFN_TILE = 2 * D_FF // N_CHIPS
RKV = 3 * RW_WIDTH
WA = 128
XG = 160
RW_COLS = RKV + WA + XG


def _local_step(x, p, W, target, late_weights=None, early_grads=None, by_chip=False):
    T = x.shape[0]
    tT = 256
    bd512 = _block_diag_ones(RW_WIDTH, RW_HEAD_DIM)
    bd256 = _block_diag_ones(ATT_GROUP_WIDTH, ATT_HEAD_DIM)
    G = {}
    W = dict(W)

    w_in = W["w_in"]
    w_rkv, w_wa, w_xg, w_att = (w_in[:, :RKV], w_in[:, RKV:RKV + WA], w_in[:, RKV + WA:RW_COLS],
                                w_in[:, RW_COLS:])
    mu = W["rw_mu"]
    mu_rkv, mu_wa, mu_xg = mu[:, :RKV], mu[:, RKV:RKV + WA], mu[:, RKV + WA:]
    zpad = jnp.zeros((64, RW_WIDTH), W["rw_w_up"].dtype)
    w_up_pad = jnp.concatenate([W["rw_w_up"], zpad], axis=0)
    a_up_pad = jnp.concatenate([zpad, W["rw_a_up"]], axis=0)
    r_k = W["rw_r_k"].reshape(1, RW_WIDTH)

    (h,) = _rowwise("norm_mix", lambda i, n, r, pv, nx, c: [_rms_fwd(r[0], c[0])], T, tT,
                    rows=[x], consts=[W["g_mix"]], outs=[("row", D_MODEL, BF16)])
    p_rkv = _mm("proj_rkv", h, w_rkv, "nn")
    p_wa = _mm("proj_wa", h, w_wa, "nn")
    p_xg = _mm("proj_xg", h, w_xg, "nn")
    p_att = _mm("proj_att", h, w_att, "nn", tn=768)
    z_gate = _mm("proj_gate", h, W["w_gate"], "nn")

    def rw_pre_core(i, rows, prevs, consts):
        prkv, pwa, pxg = rows[:3]
        (mrkv, mwa, mxg, w0, a0, k_k, k_a, wup, aup, gup, bd) = consts[:11]
        m_rkv = prkv + (_shift_down(prkv, prevs[0], i, 1) - prkv) * mrkv
        m_wa = pwa + (_shift_down(pwa, prevs[1], i, 1) - pwa) * mwa
        m_xg = pxg + (_shift_down(pxg, prevs[2], i, 1) - pxg) * mxg
        r, k, v = m_rkv[:, :RW_WIDTH], m_rkv[:, RW_WIDTH:2 * RW_WIDTH], m_rkv[:, 2 * RW_WIDTH:]
        tw = jnp.tanh(m_wa)
        lw = w0 + jnp.dot(tw.astype(BF16), wup.astype(BF16), preferred_element_type=F32)
        wlog = -_softplus(-lw) - 0.5
        log_decay = -jnp.exp(wlog)
        a = _sigmoid(a0 + jnp.dot(m_wa.astype(BF16), aup.astype(BF16), preferred_element_type=F32))
        sg = _sigmoid(m_xg)
        gate = jnp.dot(sg.astype(BF16), gup.astype(BF16), preferred_element_type=F32)
        kkp = k * k_k
        nrm = jnp.sqrt(_segsum(kkp * kkp, bd))
        nrm_c = jnp.maximum(nrm, 1e-12)
        kk = kkp / nrm_c
        k2 = k * (1.0 + (a - 1.0) * k_a)
        return dict(r=r, k=k, v=v, tw=tw, lw=lw, wlog=wlog, log_decay=log_decay, a=a, sg=sg, gate=gate, kkp=kkp,
                    nrm=nrm, nrm_c=nrm_c, kk=kk, k2=k2, m_rkv=m_rkv, m_wa=m_wa, m_xg=m_xg)

    pre_consts = [mu_rkv, mu_wa, mu_xg, W["rw_w0"], W["rw_a0"], W["rw_k_k"], W["rw_k_a"],
                  w_up_pad, a_up_pad, W["rw_g_up"], bd512]

    def rw_pre(i, n, rows, prevs, nexts, consts):
        q = rw_pre_core(i, rows, prevs, consts)
        return [q["r"], q["log_decay"], q["k2"], q["v"], -q["kk"], q["kk"] * q["a"], q["gate"]]

    r_s, w_s, k_s, v_s, a_s, b_s, gate_s = _rowwise(
        "rwkv_pre", rw_pre, T, tT, rows=[p_rkv, p_wa, p_xg], prevs=[p_rkv, p_wa, p_xg], consts=pre_consts,
        outs=[("row", RW_WIDTH, F32)] * 7)
    (at_s, bt_s, kt_s, rt_s, a2v_s, w2v_s, tinv_s, w1_s, a2_s, w2_s,
     plast_s) = _rwkv_chunk_prep(r_s, w_s, k_s, a_s, b_s, v_s)
    y_scan, sa_s, s0_s = _rwkv_chunk_fwd(v_s, at_s, bt_s, kt_s, rt_s, a2v_s, w2v_s, tinv_s, w1_s, plast_s)

    def rw_post_core(rows, consts):
        y, r, k2, v, gate = rows[:5]
        ln_g, ln_b, rk, bd = consts[:4]
        mean = _segsum(y, bd) * (1.0 / RW_HEAD_DIM)
        yc = y - mean
        var = _segsum(yc * yc, bd) * (1.0 / RW_HEAD_DIM)
        rstd = lax.rsqrt(var + RW_LN_EPS)
        yn = yc * rstd
        s = _segsum(r * k2 * rk, bd)
        return dict(yn=yn, rstd=rstd, s=s, pre=yn * ln_g + ln_b + s * v)

    post_consts = [W["rw_ln_g"], W["rw_ln_b"], r_k, bd512]
    (y_a,) = _rowwise("rwkv_post", lambda i, n, r, pv, nx, c: [rw_post_core(r, c)["pre"] * r[4]], T, tT,
                      rows=[y_scan, r_s, k_s, v_s, gate_s], consts=post_consts, outs=[("row", RW_WIDTH, BF16)])

    att = [_att_fwd(p_att, g) for g in range(3)]

    def comb_weights(ls):
        mx = jnp.maximum(jnp.maximum(ls[0], ls[1]), ls[2])
        es = [jnp.exp(l - mx) for l in ls]
        den = es[0] + es[1] + es[2]
        return [e / den for e in es]

    def att_comb(i, n, rows, pv, nx, c):
        wts = comb_weights(rows[3:6])
        return [wts[0] * rows[0] + wts[1] * rows[1] + wts[2] * rows[2]]

    (y_b,) = _rowwise("att_combine", att_comb, T, tT, rows=[att[0][0], att[1][0], att[2][0], att[0][1], att[1][1],
                                                            att[2][1]], outs=[("row", ATT_GROUP_WIDTH, BF16)])

    if late_weights is not None:
        W.update(late_weights(y_b))
    br_a = _mm("branch_a", y_a, W["w_branch_a"], "nn")
    br_b = _mm("branch_b", y_b, W["w_branch_b"], "nn")

    def merge(i, n, rows, pv, nx, c):
        gates = _sigmoid(rows[0] + c[0])
        return [gates[:, :D_MODEL] * rows[1] + gates[:, D_MODEL:] * rows[2]]

    (merged,) = _rowwise("merge", merge, T, tT, rows=[z_gate, br_a, br_b], consts=[W["b_gate"]],
                         outs=[("row", D_MODEL, BF16)])
    x1 = _mm("mix_out", merged, W["w_out"], "nn", add=x)

    (h2,) = _rowwise("norm_ffn", lambda i, n, r, pv, nx, c: [_rms_fwd(r[0], c[0])], T, tT,
                     rows=[x1], consts=[W["g_ffn"]], outs=[("row", D_MODEL, BF16)])
    u = _mm("ffn_up", h2, W["w_up"], "nn", tn=FFN_TILE)

    def conv_core(i, rows, prevs, consts):
        uu, cw, cb = rows[0], consts[0], consts[1]
        u1 = _shift_down(uu, prevs[0], i, 1)
        u2 = _shift_down(uu, prevs[0], i, 2)
        uc = cb + cw[0:1] * uu + cw[1:2] * u1 + cw[2:3] * u2
        return uc[:, :D_FF], uc[:, D_FF:], u1, u2

    def glu(i, n, rows, prevs, nx, consts):
        gate, val, _, _ = conv_core(i, rows, prevs, consts)
        return [_gelu(gate) * val]

    tF = 128
    (act,) = _rowwise("conv_glu", glu, T, tF, rows=[u], prevs=[u], consts=[W["conv_w"], W["conv_b"]],
                      outs=[("row", D_FF, BF16)])
    x2 = _mm("ffn_down", act, W["w_down"], "nn", add=x1)

    (h3,) = _rowwise("norm_ple", lambda i, n, r, pv, nx, c: [_rms_fwd(r[0], c[0])], T, tT,
                     rows=[x2], consts=[W["g_ple"]], outs=[("row", D_MODEL, BF16)])
    z_ple = _mm("ple_gate", h3, W["w_ple_gate"], "nn")
    e_ple = _mm("ple_emb", p, W["w_ple"], "nn")

    def head(i, n, rows, pv, nx, consts):
        x2_, z, e, tgt = rows
        pg = _sigmoid(z)
        x3 = x2_ + pg * e
        y = _rms_fwd(x3, consts[0])
        err = y - tgt
        loss = 0.5 * jnp.sum(jnp.sum(err * err, axis=1, keepdims=True) * (1.0 / D_MODEL), axis=0, keepdims=True)
        dy = err * (1.0 / D_MODEL)
        dx3, dgf = _rms_bwd(x3, consts[0], dy)
        return [dx3, dx3 * pg, dx3 * e * pg * (1.0 - pg), jnp.broadcast_to(loss, (1, LANES)), _colsum(dgf)]

    dx3, de, dz, loss_acc, G["g_final"] = _rowwise(
        "loss_head", head, T, tT, rows=[x2, z_ple, e_ple, target], consts=[W["g_final"].reshape(1, D_MODEL)],
        outs=[("row", D_MODEL, F32), ("row", D_MODEL, BF16), ("row", D_MODEL, BF16), ("acc", (1, LANES)),
              ("acc", (1, D_MODEL))])
    G["w_ple"] = _mm("d_w_ple", p, de, "tn", out_by_chip=by_chip)
    G["w_ple_gate"] = _mm("d_w_ple_gate", h3, dz, "tn")
    dh3 = _mm("d_h3", dz, W["w_ple_gate"], "nt")

    def norm_bwd(i, n, rows, pv, nx, consts):
        dx, dg = _rms_bwd(rows[0], consts[0], rows[1])
        return [rows[2] + dx, _colsum(dg)]

    dx2, G["g_ple"] = _rowwise("d_norm_ple", norm_bwd, T, tT, rows=[x2, dh3, dx3], consts=[W["g_ple"]],
                               outs=[("row", D_MODEL, F32), ("acc", (1, D_MODEL))])

    dact = _mm("d_act", dx2, W["w_down"], "nt")
    G["w_down"] = _mm("d_w_down", act, dx2, "tn")

    def glu_grad(gate, val, da):
        act_, slope = _gelu_and_grad(gate)
        return jnp.concatenate([da * val * slope, da * act_], axis=1)

    def glu_bwd(i, n, rows, prevs, nexts, consts):
        uu, da = rows
        cw = consts[0]
        gate, val, u1, u2 = conv_core(i, rows, prevs, consts)
        duc = glu_grad(gate, val, da)
        dcw = jnp.concatenate([_colsum(duc * uu), _colsum(duc * u1), _colsum(duc * u2)], axis=0)
        gate_n, val_n, _, _ = conv_core(1, [nexts[0]], [uu[tF - SUBLANES:]], consts)
        duc_n = glu_grad(gate_n, val_n, nexts[1])
        du = (cw[0:1] * duc + cw[1:2] * _shift_up(duc, duc_n, i, n, 1) + cw[2:3] * _shift_up(duc, duc_n, i, n, 2))
        return [du, _colsum(duc), dcw]

    du, G["conv_b"], G["conv_w"] = _rowwise(
        "d_conv_glu", glu_bwd, T, tF, rows=[u, dact], prevs=[u], nexts=[u, dact],
        consts=[W["conv_w"], W["conv_b"]],
        outs=[("row", 2 * D_FF, BF16), ("acc", (1, 2 * D_FF)), ("acc", (3, 2 * D_FF))])
    G["w_up"] = _mm("d_w_up", h2, du, "tn", out_by_chip=by_chip, tn=FFN_TILE)
    dh2 = _mm("d_h2", du, W["w_up"], "nt", tk=FFN_TILE)
    dx1, G["g_ffn"] = _rowwise("d_norm_ffn", norm_bwd, T, tT, rows=[x1, dh2, dx2], consts=[W["g_ffn"]],
                               outs=[("row", D_MODEL, F32), ("acc", (1, D_MODEL))])

    b_gate = W["b_gate"]
    if early_grads is not None:
        b_gate = b_gate + early_grads(G, 0)[0:1, 0:1]
    dmerged = _mm("d_merged", dx1, W["w_out"], "nt")
    G["w_out"] = _mm("d_w_out", merged, dx1, "tn")

    def merge_bwd(i, n, rows, pv, nx, consts):
        z, a_, b_, dm = rows
        gates = _sigmoid(z + consts[0])
        ga, gb = gates[:, :D_MODEL], gates[:, D_MODEL:]
        dz_ = jnp.concatenate([dm * a_ * ga * (1.0 - ga), dm * b_ * gb * (1.0 - gb)], axis=1)
        return [dm * ga, dm * gb, dz_, _colsum(dz_)]

    d_br_a, d_br_b, dz_gate, G["b_gate"] = _rowwise(
        "d_merge", merge_bwd, T, tT, rows=[z_gate, br_a, br_b, dmerged], consts=[b_gate],
        outs=[("row", D_MODEL, BF16), ("row", D_MODEL, BF16), ("row", 2 * D_MODEL, BF16), ("acc", (1, 2 * D_MODEL))])
    G["w_branch_a"] = _mm("d_w_branch_a", y_a, d_br_a, "tn", out_by_chip=by_chip)
    G["w_branch_b"] = _mm("d_w_branch_b", y_b, d_br_b, "tn", out_by_chip=by_chip)
    G["w_gate"] = _mm("d_w_gate", h, dz_gate, "tn", out_by_chip=by_chip)
    if early_grads is not None:
        post_consts = [post_consts[0] + early_grads(G, 1)[0:1, 0:1]] + post_consts[1:]
    dy_a = _mm("d_y_a", d_br_a, W["w_branch_a"], "nt")
    dy_b = _mm("d_y_b", d_br_b, W["w_branch_b"], "nt")

    def att_comb_bwd(i, n, rows, pv, nx, consts):
        os_, ls, dy = rows[0:3], rows[3:6], rows[6]
        wts = comb_weights(ls)
        dws = [_segsum(dy * o_, consts[0]) for o_ in os_]
        mix = wts[0] * dws[0] + wts[1] * dws[1] + wts[2] * dws[2]
        return [wts[g_] * dy for g_ in range(3)] + [wts[g_] * (dws[g_] - mix) for g_ in range(3)]

    comb = _rowwise("d_att_combine", att_comb_bwd, T, tT,
                    rows=[att[0][0], att[1][0], att[2][0], att[0][1], att[1][1], att[2][1], dy_b], consts=[bd256],
                    outs=[("row", ATT_GROUP_WIDTH, F32)] * 6)
    dqkv = [_att_bwd(p_att, att[g][0], att[g][1], comb[g], comb[3 + g], g) for g in range(3)]
    dp_att = jnp.concatenate([dqkv[g][part] for part in range(3) for g in range(3)], axis=1).astype(BF16)

    def rw_post_bwd(i, n, rows, pv, nx, consts):
        y, r, k2, v, gate, dya = rows
        ln_g, ln_b, rk, bd = consts
        q = rw_post_core(rows, consts)
        dpre = dya * gate
        dgate = dya * q["pre"]
        dyn = dpre * ln_g
        inv = 1.0 / RW_HEAD_DIM
        dy_scan = q["rstd"] * (dyn - _segsum(dyn, bd) * inv - q["yn"] * (_segsum(dyn * q["yn"], bd) * inv))
        ds = _segsum(dpre * v, bd)
        return [dy_scan, dgate, ds * k2 * rk, ds * r * rk, dpre * q["s"],
                _colsum(dpre * q["yn"]), _colsum(dpre), _colsum(ds * r * k2)]

    dy_scan, dgate, dr_b, dk2_b, dv_b, G["rw_ln_g"], G["rw_ln_b"], d_rk = _rowwise(
        "d_rwkv_post", rw_post_bwd, T, tT, rows=[y_scan, r_s, k_s, v_s, gate_s, dy_a], consts=post_consts,
        outs=[("row", RW_WIDTH, F32)] * 5 + [("acc", (1, RW_WIDTH))] * 3)
    G["rw_r_k"] = d_rk.reshape(RW_HEADS, RW_HEAD_DIM)

    dr_s, dw_s, dk_s, da_s, db_s, dv_s = _rwkv_chunk_bwd(r_s, w_s, k_s, a_s, b_s, v_s, dy_scan, s0_s, tinv_s, w1_s,
                                                         a2_s, w2_s, sa_s)

    def rw_pre_bwd(i, n, rows, prevs, nx, consts):
        q = rw_pre_core(i, rows, prevs, consts)
        (mrkv, mwa, mxg, w0, a0, k_k, k_a, wup, aup, gup, bd) = consts
        dr, dlogdecay, dk2, dv, dav, dbv, dgate_ = rows[3:10]
        dr = dr + rows[10]
        dk2 = dk2 + rows[11]
        dv = dv + rows[12]
        a, k, kk = q["a"], q["k"], q["kk"]
        dk = dk2 * (1.0 + (a - 1.0) * k_a)
        da = dk2 * k * k_a + dbv * kk
        dkk = dbv * a - dav
        live = q["nrm"] > 1e-12
        dkkp = jnp.where(live, dkk - kk * _segsum(dkk * kk, bd), dkk) / q["nrm_c"]
        dk = dk + dkkp * k_k
        dlw = dlogdecay * q["log_decay"] * _sigmoid(-q["lw"])
        dla = da * a * (1.0 - a)
        nt = (((1,), (1,)), ((), ()))
        dtw = lax.dot_general(dlw.astype(BF16), wup.astype(BF16), nt, preferred_element_type=F32)
        dxa = lax.dot_general(dla.astype(BF16), aup.astype(BF16), nt, preferred_element_type=F32)
        dm_wa = dtw * (1.0 - q["tw"] * q["tw"]) + dxa
        dsg = lax.dot_general(dgate_.astype(BF16), gup.astype(BF16), nt, preferred_element_type=F32)
        dm_xg = dsg * q["sg"] * (1.0 - q["sg"])
        dm_rkv = jnp.concatenate([dr, dk, dv], axis=1)
        prkv, pwa, pxg = rows[:3]
        dmu = jnp.concatenate([_colsum(dm_rkv * (_shift_down(prkv, prevs[0], i, 1) - prkv)),
                               _colsum(dm_wa * (_shift_down(pwa, prevs[1], i, 1) - pwa)),
                               _colsum(dm_xg * (_shift_down(pxg, prevs[2], i, 1) - pxg))], axis=1)
        return [dm_rkv, dm_wa, dm_xg, dlw, dla, q["tw"], q["m_wa"], q["sg"], dmu,
                _colsum(dlw), _colsum(dla), _colsum(dkkp * k), _colsum(dk2 * k * (a - 1.0))]

    (dm_rkv, dm_wa, dm_xg, dlw, dla, tw_s, mwa_s, sg_s, G["rw_mu"], G["rw_w0"], G["rw_a0"], G["rw_k_k"],
     G["rw_k_a"]) = _rowwise(
        "d_rwkv_pre", rw_pre_bwd, T, tT,
        rows=[p_rkv, p_wa, p_xg, dr_s, dw_s, dk_s, dv_s, da_s, db_s, dgate, dr_b, dk2_b, dv_b],
        prevs=[p_rkv, p_wa, p_xg], consts=pre_consts,
        outs=[("row", RKV, F32), ("row", WA, F32), ("row", XG, F32), ("row", RW_WIDTH, BF16),
              ("row", RW_WIDTH, BF16), ("row", WA, BF16), ("row", WA, BF16), ("row", XG, BF16),
              ("acc", (1, RW_COLS))] + [("acc", (1, RW_WIDTH))] * 4)
    G["rw_w_up"] = _mm("d_rw_w_up", tw_s, dlw, "tn")[:64]
    G["rw_a_up"] = _mm("d_rw_a_up", mwa_s, dla, "tn")[64:]
    G["rw_g_up"] = _mm("d_rw_g_up", sg_s, dgate, "tn")

    def shift_bwd(i, n, rows, pv, nexts, consts):
        return [rows[j] * (1.0 - consts[j]) + _shift_up(rows[j], nexts[j], i, n, 1) * consts[j] for j in range(3)]

    dp_rkv, dp_wa, dp_xg = _rowwise(
        "d_token_shift", shift_bwd, T, tT, rows=[dm_rkv, dm_wa, dm_xg], nexts=[dm_rkv, dm_wa, dm_xg],
        consts=[mu_rkv, mu_wa, mu_xg], outs=[("row", RKV, BF16), ("row", WA, BF16), ("row", XG, BF16)])

    G["w_in"] = jnp.concatenate([_mm("d_w_rkv", h, dp_rkv, "tn"), _mm("d_w_wa", h, dp_wa, "tn"),
                                 _mm("d_w_xg", h, dp_xg, "tn"), _mm("d_w_att", h, dp_att, "tn", tn=768)], axis=1)
    if early_grads is not None:
        w_wa = w_wa + early_grads(G, 2)[0:1, 0:1].astype(w_wa.dtype)
    dh = _mm("d_h_gate", dz_gate, W["w_gate"], "nt")
    dh = _mm("d_h_rkv", dp_rkv, w_rkv, "nt", add=dh)
    dh = _mm("d_h_wa", dp_wa, w_wa, "nt", add=dh)
    dh = _mm("d_h_xg", dp_xg, w_xg, "nt", add=dh)
    dh = _mm("d_h_att", dp_att, w_att, "nt", add=dh)
    dx, G["g_mix"] = _rowwise("d_norm_mix", norm_bwd, T, tT, rows=[x, dh, dx1], consts=[W["g_mix"]],
                              outs=[("row", D_MODEL, F32), ("acc", (1, D_MODEL))])
    return loss_acc[:, :1], dx, G


HBM_SPEC = pl.BlockSpec(memory_space=pltpu.HBM)


def _place():
    x, y, c = lax.axis_index("x"), lax.axis_index("y"), lax.axis_index("c")
    return x, y, c, [(1 - x, y), (x, 1 - y), (1 - x, 1 - y)]


def _remote(src, dst, send_sems, recv_sems, k, to):
    return pltpu.make_async_remote_copy(src_ref=src, dst_ref=dst, send_sem=send_sems.at[k], recv_sem=recv_sems.at[k],
                                        device_id=to, device_id_type=MESH)


ROW_ALIGN = 16


def _splits(rows):
    return rows % (2 * ROW_ALIGN) == 0


def _half_rows(ref_rows, c, first):
    half = ref_rows // 2
    which = c if first else 1 - c
    return pl.ds(pl.multiple_of(which * half, ROW_ALIGN), half)


def _gather_chips(shards):
    n = len(shards)
    split = [_splits(s.shape[0]) for s in shards]

    def body(*refs):
        w_refs, out_refs = refs[:n], refs[n:2 * n]
        send_sems, recv_sems = refs[2 * n:]
        x, y, c, chips = _place()
        me = 2 * x + y
        sends, passed = [], []
        for i in range(n):
            for j, (px, py) in enumerate(chips):
                if split[i]:
                    mine = _half_rows(w_refs[i].shape[0], c, True)
                    cp = _remote(w_refs[i].at[mine], out_refs[i].at[me, mine], send_sems, recv_sems, 6 * i + j,
                                 (px, py, c))
                else:
                    cp = _remote(w_refs[i], out_refs[i].at[me], send_sems, recv_sems, 6 * i + j, (px, py, c))
                cp.start()
                sends.append(cp)
        for i in range(n):
            for j, (px, py) in enumerate(chips):
                if split[i]:
                    landed = out_refs[i].at[2 * px + py, _half_rows(w_refs[i].shape[0], c, True)]
                    _remote(landed, landed, send_sems, recv_sems, 6 * i + j, (px, py, c)).wait_recv()
                    cp = _remote(landed, landed, send_sems, recv_sems, 6 * i + 3 + j, (x, y, 1 - c))
                    cp.start()
                    passed.append(cp)
                else:
                    landed = out_refs[i].at[2 * px + py]
                    _remote(landed, landed, send_sems, recv_sems, 6 * i + j, (px, py, c)).wait_recv()
        for i in range(n):
            if split[i]:
                for j, (px, py) in enumerate(chips):
                    landed = out_refs[i].at[2 * px + py, _half_rows(w_refs[i].shape[0], c, False)]
                    _remote(landed, landed, send_sems, recv_sems, 6 * i + 3 + j, (x, y, 1 - c)).wait_recv()
        for cp in sends + passed:
            cp.wait_send()

    outs = pl.pallas_call(
        body, name="gather_weights", in_specs=[HBM_SPEC] * n, out_specs=[HBM_SPEC] * n,
        out_shape=[jax.ShapeDtypeStruct((N_CHIPS,) + s.shape, s.dtype) for s in shards],
        scratch_shapes=[pltpu.SemaphoreType.DMA((6 * n,)), pltpu.SemaphoreType.DMA((6 * n,))],
    )(*shards)
    me = 2 * lax.axis_index("x") + lax.axis_index("y")
    return [lax.dynamic_update_slice(o, s[None], (me, 0, 0)) for o, s in zip(outs, shards, strict=True)]


def _swap_halves(name, gs):
    n = len(gs)

    def body(*refs):
        g_refs, out_refs = refs[:n], refs[n:2 * n]
        send_sems, recv_sems = refs[2 * n:]
        x, y, c, _ = _place()
        cps = []
        for i in range(n):
            theirs = _half_rows(g_refs[i].shape[1], c, False)
            cp = _remote(g_refs[i].at[:, theirs, :], out_refs[i], send_sems, recv_sems, i, (x, y, 1 - c))
            cp.start()
            cps.append(cp)
        for cp in cps:
            cp.wait()

    return pl.pallas_call(
        body, name=name, in_specs=[HBM_SPEC] * n, out_specs=[HBM_SPEC] * n,
        out_shape=[jax.ShapeDtypeStruct((N_CHIPS, g.shape[1] // 2, g.shape[2]), g.dtype) for g in gs],
        scratch_shapes=[pltpu.SemaphoreType.DMA((n,)), pltpu.SemaphoreType.DMA((n,))],
    )(*gs)


def _join_halves(reds):
    n = len(reds)

    def body(*refs):
        r_refs, out_refs = refs[:n], refs[n:2 * n]
        send_sems, recv_sems = refs[2 * n:]
        x, y, c, _ = _place()
        cps = []
        for i in range(n):
            mine = _half_rows(out_refs[i].shape[0], c, True)
            cp = _remote(r_refs[i], out_refs[i].at[mine], send_sems, recv_sems, i, (x, y, 1 - c))
            cp.start()
            cps.append(cp)
        for cp in cps:
            cp.wait()

    outs = pl.pallas_call(
        body, name="join_halves", in_specs=[HBM_SPEC] * n, out_specs=[HBM_SPEC] * n,
        out_shape=[jax.ShapeDtypeStruct((2 * r.shape[0], r.shape[1]), r.dtype) for r in reds],
        scratch_shapes=[pltpu.SemaphoreType.DMA((n,)), pltpu.SemaphoreType.DMA((n,))],
    )(*reds)
    c = lax.axis_index("c")
    return [lax.dynamic_update_slice(o, r, (c * r.shape[0], 0)) for o, r in zip(outs, reds, strict=True)]


def _gather_all(vec):
    R = vec.shape[0]

    def body(v_ref, out_ref, send_sems, recv_sems, local_sem):
        x, y, c, _ = _place()
        me = 4 * x + 2 * y + c
        local = pltpu.make_async_copy(v_ref, out_ref.at[me], local_sem)
        local.start()
        peers = [(x ^ (k >> 2), y ^ ((k >> 1) & 1), c ^ (k & 1)) for k in range(1, N_DEV)]
        sends = [_remote(v_ref, out_ref.at[me], send_sems, recv_sems, k, to) for k, to in enumerate(peers)]
        for cp in sends:
            cp.start()
        for k, (px, py, pc) in enumerate(peers):
            landed = out_ref.at[4 * px + 2 * py + pc]
            _remote(landed, landed, send_sems, recv_sems, k, (px, py, pc)).wait_recv()
        for cp in sends:
            cp.wait_send()
        local.wait()

    return pl.pallas_call(
        body, name="gather_small", in_specs=[HBM_SPEC], out_specs=HBM_SPEC,
        out_shape=jax.ShapeDtypeStruct((N_DEV, R, LANES), vec.dtype),
        scratch_shapes=[pltpu.SemaphoreType.DMA((7,)), pltpu.SemaphoreType.DMA((7,)), pltpu.SemaphoreType.DMA],
    )(vec)


SEM_SPEC = pl.BlockSpec(memory_space=pltpu.SEMAPHORE)
DATAFLOW = pltpu.SideEffectType.DATAFLOW_SIDE_EFFECTING


def _travel_copies(mode, src_refs, land_refs, send_sems, recv_sems):
    x, y, c, chips = _place()
    me = 2 * x + y
    pairs = []
    for i, (src, land) in enumerate(zip(src_refs, land_refs, strict=True)):
        for j, (px, py) in enumerate(chips):
            peer = 2 * px + py
            if mode == "scatter":
                mine, there, here = src.at[peer], land.at[me], land.at[peer]
            elif _splits(src.shape[0]):
                rows = _half_rows(src.shape[0], c, True)
                mine, there, here = src.at[rows], land.at[me, rows], land.at[peer, rows]
            else:
                mine, there, here = src, land.at[me], land.at[peer]
            send = functools.partial(_remote, mine, there, send_sems, recv_sems, 3 * i + j, (px, py, c))
            arrival = functools.partial(_remote, mine, here, send_sems, recv_sems, 3 * i + j, (px, py, c))
            pairs.append((send, arrival))
    return pairs


def _share_halves(name, lands):
    idx = [i for i, a in enumerate(lands) if _splits(a.shape[1])]
    n = len(idx)

    def body(*refs):
        in_refs, out_refs = refs[:n], refs[n:2 * n]
        send_sems, recv_sems = refs[2 * n:]
        x, y, c, chips = _place()
        cps = []
        for i, (src, dst) in enumerate(zip(in_refs, out_refs, strict=True)):
            for j, (px, py) in enumerate(chips):
                mine = _half_rows(src.shape[1], c, True)
                cp = _remote(src.at[2 * px + py, mine], dst.at[2 * px + py, mine], send_sems, recv_sems, 3 * i + j,
                             (x, y, 1 - c))
                cp.start()
                cps.append(cp)
        for i, dst in enumerate(out_refs):
            for j, (px, py) in enumerate(chips):
                theirs = dst.at[2 * px + py, _half_rows(dst.shape[1], c, False)]
                _remote(theirs, theirs, send_sems, recv_sems, 3 * i + j, (x, y, 1 - c)).wait_recv()
        for cp in cps:
            cp.wait_send()

    outs = pl.pallas_call(
        body, name=name, in_specs=[HBM_SPEC] * n, out_specs=[HBM_SPEC] * n,
        out_shape=[jax.ShapeDtypeStruct(lands[i].shape, lands[i].dtype) for i in idx],
        input_output_aliases={i: i for i in range(n)},
        scratch_shapes=[pltpu.SemaphoreType.DMA((3 * n,)), pltpu.SemaphoreType.DMA((3 * n,))],
    )(*[lands[i] for i in idx])
    done = list(lands)
    for i, o in zip(idx, outs, strict=True):
        done[i] = o
    return done


def _travel_start(name, mode, srcs):
    n = len(srcs)
    lands = [lax.empty((N_CHIPS,) + (s.shape if mode == "gather" else s.shape[1:]), s.dtype) for s in srcs]

    def body(*refs):
        src_refs, land_refs = refs[:n], refs[n:2 * n]
        send_sems, recv_sems = refs[2 * n], refs[2 * n + 1]
        token = refs[-1]
        for send, _ in _travel_copies(mode, src_refs, land_refs, send_sems, recv_sems):
            send().start()
        token[...] = jnp.zeros_like(token)

    hbm = lambda a: pltpu.HBM(a.shape, a.dtype)
    outs = pl.pallas_call(
        body, name=name,
        out_shape=(pltpu.SemaphoreType.DMA((3 * n,)), pltpu.SemaphoreType.DMA((3 * n,)), *[hbm(s) for s in srcs],
                   *[hbm(a) for a in lands], jax.ShapeDtypeStruct((SUBLANES, LANES), F32)),
        in_specs=[HBM_SPEC] * (2 * n),
        out_specs=(SEM_SPEC, SEM_SPEC, *[HBM_SPEC] * (2 * n), pl.BlockSpec(memory_space=pltpu.VMEM)),
        input_output_aliases={i: 2 + i for i in range(2 * n)},
        compiler_params=pltpu.CompilerParams(has_side_effects=DATAFLOW),
    )(*[pltpu.with_memory_space_constraint(a, pltpu.HBM) for a in list(srcs) + lands])
    return outs[0], outs[1], list(outs[2:2 + n]), list(outs[2 + n:2 + 2 * n]), outs[-1]


def _travel_wait(name, mode, send_sems, recv_sems, srcs, lands, after):
    n = len(srcs)

    def body(*refs):
        src_refs, land_refs = refs[:n], refs[n:2 * n]
        send_sems_, recv_sems_ = refs[2 * n], refs[2 * n + 1]
        for send, arrival in _travel_copies(mode, src_refs, land_refs, send_sems_, recv_sems_):
            send().wait_send()
            arrival().wait_recv()

    hbm = lambda a: pltpu.HBM(a.shape, a.dtype)
    outs = pl.pallas_call(
        body, name=name, out_shape=tuple(hbm(a) for a in list(srcs) + list(lands)),
        in_specs=[HBM_SPEC] * (2 * n) + [SEM_SPEC, SEM_SPEC, pl.BlockSpec(memory_space=pl.ANY)],
        out_specs=tuple([HBM_SPEC] * (2 * n)), input_output_aliases={i: i for i in range(2 * n)},
        compiler_params=pltpu.CompilerParams(has_side_effects=DATAFLOW),
    )(*srcs, *lands, send_sems, recv_sems, after)
    me = 2 * lax.axis_index("x") + lax.axis_index("y")
    own = [s[None] if mode == "gather" else lax.dynamic_slice_in_dim(s, me, 1, axis=0) for s in outs[:n]]
    return [lax.dynamic_update_slice(a, o, (me,) + (0,) * (a.ndim - 1)) for a, o in zip(outs[n:], own, strict=True)]


SUM_TILE_BYTES = 4 * 1024 * 1024


def _sum_rows(half, cols):
    best = ROW_ALIGN
    for t in range(ROW_ALIGN, half + 1, ROW_ALIGN):
        if half % t == 0 and N_CHIPS * t * cols * 4 <= SUM_TILE_BYTES:
            best = t
    return best


def _sum_cores(name, g, theirs, core):
    _, R, C = g.shape
    half = R // 2
    tr = _sum_rows(half, C)
    nb = half // tr

    def body(core_ref, g_ref, t_ref, o_ref):
        o_ref[...] = (g_ref[...] + t_ref[...]).astype(o_ref.dtype)

    grid_spec = pltpu.PrefetchScalarGridSpec(
        num_scalar_prefetch=1, grid=(nb,),
        in_specs=[pl.BlockSpec((N_CHIPS, tr, C), lambda i, core_ref: (0, core_ref[0] * nb + i, 0)),
                  pl.BlockSpec((N_CHIPS, tr, C), lambda i, core_ref: (0, i, 0))],
        out_specs=pl.BlockSpec((N_CHIPS, tr, C), lambda i, core_ref: (0, i, 0)))
    return pl.pallas_call(
        body, name=name, grid_spec=grid_spec, out_shape=jax.ShapeDtypeStruct((N_CHIPS, half, C), BF16),
        compiler_params=_params(("parallel",)),
    )(core, g, theirs)


def _sum_chips(name, parts):
    _, H, C = parts.shape
    tr = _sum_rows(H, C)

    def body(p_ref, o_ref):
        acc = p_ref[0].astype(F32)
        for k in range(1, N_CHIPS):
            acc = acc + p_ref[k].astype(F32)
        o_ref[...] = acc

    return pl.pallas_call(
        body, name=name, grid=(H // tr,),
        in_specs=[pl.BlockSpec((N_CHIPS, tr, C), lambda i: (0, i, 0))],
        out_specs=pl.BlockSpec((tr, C), lambda i: (i, 0)),
        out_shape=jax.ShapeDtypeStruct((H, C), F32),
        compiler_params=_params(("parallel",)),
    )(parts)


def _adamw_math(w, g, m, v):
    m = ADAM_B1 * m + (1.0 - ADAM_B1) * g
    v = ADAM_B2 * v + (1.0 - ADAM_B2) * (g * g)
    m_hat = m / (1.0 - ADAM_B1 ** ADAM_STEP)
    v_hat = v / (1.0 - ADAM_B2 ** ADAM_STEP)
    delta = -ADAM_LR * (m_hat / (jnp.sqrt(v_hat) + ADAM_EPS) + ADAM_WD * w)
    return delta, m, v


def _adamw(name, w, g, m, v):
    R, C = w.shape
    tr = R
    if R % SUBLANES == 0:
        for cand in range(SUBLANES, min(R, 256) + 1, SUBLANES):
            if R % cand == 0:
                tr = cand

    def body(w_ref, g_ref, m_ref, v_ref, d_ref, nm_ref, nv_ref):
        d, nm, nv = _adamw_math(w_ref[...], g_ref[...], m_ref[...], v_ref[...])
        d_ref[...] = d
        nm_ref[...] = nm
        nv_ref[...] = nv

    spec = pl.BlockSpec((tr, C), lambda i: (i, 0))
    shape = jax.ShapeDtypeStruct((R, C), F32)
    return pl.pallas_call(
        body, name=name, grid=(R // tr,), in_specs=[spec] * 4, out_specs=[spec] * 3, out_shape=[shape] * 3,
        compiler_params=_params(("parallel",)),
    )(w, g, m, v)


def _adamw_small(parts, w, m, v):
    n = parts.shape[0]

    def body(p_ref, w_ref, m_ref, v_ref, g_ref, d_ref, nm_ref, nv_ref):
        g = p_ref[0]
        for k in range(1, n):
            g = g + p_ref[k]
        d, nm, nv = _adamw_math(w_ref[...], g, m_ref[...], v_ref[...])
        g_ref[...] = g
        d_ref[...] = d
        nm_ref[...] = nm
        nv_ref[...] = nv

    shape = jax.ShapeDtypeStruct(w.shape, F32)
    return pl.pallas_call(body, name="adamw_small", out_shape=[shape] * 4, compiler_params=_params())(parts, w, m, v)


WEIGHTS = ['g_mix', 'w_in', 'rw_mu', 'rw_w0', 'rw_w_up', 'rw_a0', 'rw_a_up', 'rw_g_up', 'rw_k_k', 'rw_k_a',
           'rw_r_k', 'rw_ln_g', 'rw_ln_b', 'w_branch_a', 'w_branch_b', 'w_gate', 'b_gate', 'w_out', 'g_ffn', 'w_up',
           'conv_w', 'conv_b', 'w_down', 'g_ple', 'w_ple_gate', 'w_ple', 'g_final']
ARG_NAMES = (['x', 'p'] + WEIGHTS + ['loss_target'] + ['m_' + n for n in WEIGHTS] + ['v_' + n for n in WEIGHTS])
SHARDED = {'w_in': 1, 'rw_w_up': 1, 'rw_a_up': 1, 'rw_g_up': 1, 'w_branch_a': 1, 'w_branch_b': 1, 'w_gate': 1,
           'w_out': 0, 'w_up': 1, 'conv_w': 1, 'w_down': 0, 'w_ple_gate': 0, 'w_ple': 1}
SMALL = [n for n in WEIGHTS if n not in SHARDED]
WHOLE = ['conv_w']
FIRST_USED = ['w_in', 'rw_w_up', 'rw_a_up', 'rw_g_up', 'w_gate']
READ_BY_CHIP = ['w_gate', 'w_branch_a', 'w_branch_b', 'w_up', 'w_ple']
FIRST_DONE = [['w_up', 'w_down', 'w_ple_gate', 'w_ple'], ['w_out', 'w_branch_a', 'w_branch_b', 'w_gate'],
              ['w_in', 'rw_w_up', 'rw_a_up', 'rw_g_up']]
SPLIT = [n for n in SHARDED if n not in WHOLE]
PACK_ALIGN = SUBLANES * LANES


def _pack_rows(flat_parts):
    flat = jnp.concatenate(flat_parts, axis=1)
    n = flat.shape[1]
    padded = -(-n // PACK_ALIGN) * PACK_ALIGN
    flat = jnp.pad(flat, ((0, 0), (0, padded - n)))
    return flat.reshape(padded // LANES, LANES)


def _full_from_shards(stack, axis):
    _, R, C = stack.shape
    if axis == 0:
        return stack.reshape(N_CHIPS * R, C)
    return stack.transpose(1, 0, 2).reshape(R, N_CHIPS * C)


def _shards_from_full(full, axis):
    R, C = full.shape
    if axis == 0:
        return full.reshape(N_CHIPS, R // N_CHIPS, C)
    return full.reshape(R, N_CHIPS, C // N_CHIPS).transpose(1, 0, 2)


def kernel(x, p, g_mix, w_in, rw_mu, rw_w0, rw_w_up, rw_a0, rw_a_up, rw_g_up, rw_k_k, rw_k_a, rw_r_k, rw_ln_g, rw_ln_b, w_branch_a, w_branch_b, w_gate, b_gate, w_out, g_ffn, w_up, conv_w, conv_b, w_down, g_ple, w_ple_gate, w_ple, g_final, loss_target, m_g_mix, m_w_in, m_rw_mu, m_rw_w0, m_rw_w_up, m_rw_a0, m_rw_a_up, m_rw_g_up, m_rw_k_k, m_rw_k_a, m_rw_r_k, m_rw_ln_g, m_rw_ln_b, m_w_branch_a, m_w_branch_b, m_w_gate, m_b_gate, m_w_out, m_g_ffn, m_w_up, m_conv_w, m_conv_b, m_w_down, m_g_ple, m_w_ple_gate, m_w_ple, m_g_final, v_g_mix, v_w_in, v_rw_mu, v_rw_w0, v_rw_w_up, v_rw_a0, v_rw_a_up, v_rw_g_up, v_rw_k_k, v_rw_k_a, v_rw_r_k, v_rw_ln_g, v_rw_ln_b, v_w_branch_a, v_w_branch_b, v_w_gate, v_b_gate, v_w_out, v_g_ffn, v_w_up, v_conv_w, v_conv_b, v_w_down, v_g_ple, v_w_ple_gate, v_w_ple, v_g_final):
    given = dict(zip(ARG_NAMES, (x, p, g_mix, w_in, rw_mu, rw_w0, rw_w_up, rw_a0, rw_a_up, rw_g_up, rw_k_k, rw_k_a, rw_r_k, rw_ln_g, rw_ln_b, w_branch_a, w_branch_b, w_gate, b_gate, w_out, g_ffn, w_up, conv_w, conv_b, w_down, g_ple, w_ple_gate, w_ple, g_final, loss_target, m_g_mix, m_w_in, m_rw_mu, m_rw_w0, m_rw_w_up, m_rw_a0, m_rw_a_up, m_rw_g_up, m_rw_k_k, m_rw_k_a, m_rw_r_k, m_rw_ln_g, m_rw_ln_b, m_w_branch_a, m_w_branch_b, m_w_gate, m_b_gate, m_w_out, m_g_ffn, m_w_up, m_conv_w, m_conv_b, m_w_down, m_g_ple, m_w_ple_gate, m_w_ple, m_g_final, v_g_mix, v_w_in, v_rw_mu, v_rw_w0, v_rw_w_up, v_rw_a0, v_rw_a_up, v_rw_g_up, v_rw_k_k, v_rw_k_a, v_rw_r_k, v_rw_ln_g, v_rw_ln_b, v_w_branch_a, v_w_branch_b, v_w_gate, v_b_gate, v_w_out, v_g_ffn, v_w_up, v_conv_w, v_conv_b, v_w_down, v_g_ple, v_w_ple_gate, v_w_ple, v_g_final), strict=True))

    def two_d(name, prefix=""):
        a = given[prefix + name]
        if name == "g_final":
            return a.reshape(1, D_MODEL)
        if name == "rw_r_k":
            return a.reshape(1, RW_WIDTH)
        return a[0] if a.ndim == 3 else a

    cast = lambda n: two_d(n) if n in WHOLE else two_d(n).astype(BF16)
    whole = lambda names, stacks: {n: g if n in READ_BY_CHIP else _full_from_shards(g, SHARDED[n])
                                   for n, g in zip(names, stacks, strict=True)}
    late_names = [n for n in SHARDED if n not in FIRST_USED]
    late_sends, late_recvs, late_srcs, late_lands, token = _travel_start(
        "gather_late_start", "gather", [cast(n) for n in late_names])
    W = whole(FIRST_USED, _gather_chips([cast(n) for n in FIRST_USED]))
    for n in SMALL:
        W[n] = two_d(n)
    W["rw_r_k"] = W["rw_r_k"].reshape(RW_HEADS, RW_HEAD_DIM)
    W["g_mix"] = W["g_mix"] + token[0:1, 0:1]

    def late_weights(after):
        lands = _travel_wait("gather_late_wait", "gather", late_sends, late_recvs, late_srcs, late_lands, after)
        return whole(late_names, _share_halves("share_late", lands))

    core = lax.axis_index("c").astype(jnp.int32).reshape(1)
    early_names = [[n for n in SPLIT if n in group] for group in FIRST_DONE]
    rest_names = [n for n in SPLIT if not any(n in group for group in FIRST_DONE)]
    travelling = []

    def core_sums(tag, names, G):
        by_chip = [G[n] if n in READ_BY_CHIP else _shards_from_full(G[n], SHARDED[n]) for n in names]
        theirs = _swap_halves("swap_halves_" + tag, by_chip)
        return [_sum_cores("sum_cores_" + n, g, t, core) for n, g, t in zip(names, by_chip, theirs, strict=True)]

    def early_grads(G, stage):
        sends, recvs, srcs, lands, started = _travel_start(f"scatter_early{stage}_start", "scatter",
                                                           core_sums(f"early{stage}", early_names[stage], G))
        travelling.append((sends, recvs, srcs, lands))
        return started

    loss_part, grad_x, G = _local_step(x[0], p[0, 0], W, loss_target[0], late_weights, early_grads, by_chip=True)

    assert not rest_names, rest_names
    landed = {}
    for stage, (sends, recvs, srcs, lands) in enumerate(travelling):
        landed.update(zip(early_names[stage], _travel_wait(f"scatter_early{stage}_wait", "scatter", sends, recvs,
                                                           srcs, lands, grad_x), strict=True))
    reduced = [_sum_chips("sum_chips_" + n, landed[n]) for n in SPLIT]
    shard_grads = dict(zip(SPLIT, _join_halves(reduced), strict=True))

    small_sizes = {n: two_d(n).shape[1] for n in SMALL}
    n_small = sum(small_sizes.values())
    whole_sizes = {n: G[n].shape[0] * G[n].shape[1] for n in WHOLE}
    n_whole = sum(whole_sizes.values())

    def pack_small(parts, rest):
        return _pack_rows([a.reshape(1, -1) for a in parts] + [rest])

    G["rw_r_k"] = G["rw_r_k"].reshape(1, RW_WIDTH)
    rest = jnp.zeros((1, n_whole + 1), F32)
    all_small = _gather_all(pack_small([G[n] for n in SMALL] + [G[n] for n in WHOLE], loss_part))
    gs, ds, nms, nvs = _adamw_small(all_small, pack_small([two_d(n) for n in SMALL], rest),
                                    pack_small([two_d(n, "m_") for n in SMALL], rest),
                                    pack_small([two_d(n, "v_") for n in SMALL], rest))
    gs, ds, nms, nvs = (a.reshape(-1) for a in (gs, ds, nms, nvs))
    loss = gs[n_small + n_whole]
    chip = 2 * lax.axis_index("x") + lax.axis_index("y")
    off = n_small
    for n in WHOLE:
        full = gs[off:off + whole_sizes[n]].reshape(G[n].shape)
        off += whole_sizes[n]
        width = two_d(n).shape[1]
        shard_grads[n] = lax.dynamic_slice_in_dim(full, chip * width, width, axis=1)

    grads, deltas, new_m, new_v = {}, {}, {}, {}
    for n in SHARDED:
        g = shard_grads[n]
        d, nm, nv = _adamw("adamw_" + n, two_d(n), g, two_d(n, "m_"), two_d(n, "v_"))
        grads[n], deltas[n], new_m[n], new_v[n] = g, d, nm, nv
    off = 0
    for n in SMALL:
        sl = slice(off, off + small_sizes[n])
        off += small_sizes[n]
        grads[n], deltas[n], new_m[n], new_v[n] = gs[sl], ds[sl], nms[sl], nvs[sl]
    outs = [loss, grad_x[None]]
    for table in (grads, deltas, new_m, new_v):
        outs += [table[n].reshape(given[n].shape) for n in WEIGHTS]
    return tuple(outs)
```

```python
import functools
import math

import jax
import jax.numpy as jnp
import numpy as np
from jax import lax
from jax.experimental import pallas as pl
from jax.experimental.pallas import tpu as pltpu

F32 = jnp.float32
BF16 = jnp.bfloat16

D_MODEL = 1024
NORM_EPS = 1e-6
RW_HEADS = 8
RW_HEAD_DIM = 64
RW_WIDTH = 512
RW_LN_EPS = 64e-5
ATT_GROUP_DILATION = (1, 4, 16)
ATT_BLOCK = 128
ATT_HEADS = 12
ATT_HEAD_DIM = 64
ATT_GROUP_WIDTH = 256
ATT_WIDTH = 768
D_FF = 3072

ADAM_LR = 0.001
ADAM_B1 = 0.9
ADAM_B2 = 0.999
ADAM_EPS = 1e-08
ADAM_WD = 0.01
ADAM_STEP = 10

SUBLANES = 8
LANES = 128
VMEM_LIMIT = 56 * 1024 * 1024
N_CHIPS = 4
N_DEV = 8
MESH = pl.DeviceIdType.MESH


def _params(sem=None):
    return pltpu.CompilerParams(dimension_semantics=sem, vmem_limit_bytes=VMEM_LIMIT)


def _pick(dim, pref):
    if dim % LANES != 0 or dim <= pref:
        return dim
    best = LANES
    for t in range(LANES, pref + 1, LANES):
        if dim % t == 0:
            best = t
    return best


def _mm(name, a, b, mode, out_dtype=F32, add=None, tm=1024, tn=1024, tk=1024, out_by_chip=False):
    by_chip = b.ndim == 3
    b_rows, b_cols = (b.shape[1], N_CHIPS * b.shape[2]) if by_chip else b.shape
    if mode == "nn":
        (M, K), (K2, N) = a.shape, (b_rows, b_cols)
    elif mode == "nt":
        (M, K), (N, K2) = a.shape, (b_rows, b_cols)
    else:
        (K, M), (K2, N) = a.shape, (b_rows, b_cols)
    assert K == K2, (name, a.shape, b.shape, mode)
    assert not (by_chip and mode == "tn") and not (out_by_chip and add is not None), name
    tm = _pick(M, tm)
    n_cut, k_cut = out_by_chip or (by_chip and mode == "nn"), by_chip and mode == "nt"
    tn = _pick(N // N_CHIPS, tn) if n_cut else _pick(N, tn)
    tk = _pick(K // N_CHIPS, tk) if k_cut else _pick(K, tk)
    nk = K // tk
    per_n = (N // N_CHIPS) // tn if n_cut else 1
    per_k = (K // N_CHIPS) // tk if k_cut else 1
    if mode == "nn":
        a_spec = pl.BlockSpec((tm, tk), lambda i, j, k: (i, k))
        b_spec = (pl.BlockSpec((None, tk, tn), lambda i, j, k: (j // per_n, k, j % per_n)) if by_chip
                  else pl.BlockSpec((tk, tn), lambda i, j, k: (k, j)))
        dims = (((1,), (0,)), ((), ()))
    elif mode == "nt":
        a_spec = pl.BlockSpec((tm, tk), lambda i, j, k: (i, k))
        b_spec = (pl.BlockSpec((None, tn, tk), lambda i, j, k: (k // per_k, j, k % per_k)) if by_chip
                  else pl.BlockSpec((tn, tk), lambda i, j, k: (j, k)))
        dims = (((1,), (1,)), ((), ()))
    else:
        a_spec = pl.BlockSpec((tk, tm), lambda i, j, k: (k, i))
        b_spec = pl.BlockSpec((tk, tn), lambda i, j, k: (k, j))
        dims = (((0,), (0,)), ((), ()))
    if out_by_chip:
        o_spec = pl.BlockSpec((None, tm, tn), lambda i, j, k: (j // per_n, i, j % per_n))
        out_shape = jax.ShapeDtypeStruct((N_CHIPS, M, N // N_CHIPS), out_dtype)
    else:
        o_spec = pl.BlockSpec((tm, tn), lambda i, j, k: (i, j))
        out_shape = jax.ShapeDtypeStruct((M, N), out_dtype)
    has_add = add is not None

    def body(*refs):
        if has_add:
            a_ref, b_ref, add_ref, o_ref, acc_ref = refs
        else:
            a_ref, b_ref, o_ref, acc_ref = refs
        k = pl.program_id(2)
        part = lax.dot_general(a_ref[...].astype(BF16), b_ref[...].astype(BF16), dims,
                               preferred_element_type=F32)

        @pl.when(k == 0)
        def _():
            acc_ref[...] = part

        @pl.when(k > 0)
        def _():
            acc_ref[...] += part

        @pl.when(k == nk - 1)
        def _():
            res = acc_ref[...]
            if has_add:
                res = res + add_ref[...].astype(F32)
            o_ref[...] = res.astype(o_ref.dtype)

    ins = [a, b] + ([add] if has_add else [])
    in_specs = [a_spec, b_spec] + ([o_spec] if has_add else [])
    return pl.pallas_call(
        body, name=name, grid=(M // tm, N // tn, nk),
        in_specs=in_specs, out_specs=o_spec, out_shape=out_shape,
        scratch_shapes=[pltpu.VMEM((tm, tn), F32)],
        compiler_params=_params(("parallel", "parallel", "arbitrary")),
    )(*ins)


def _rowwise(name, fn, T, tT, rows=(), prevs=(), nexts=(), consts=(), outs=()):
    n = T // tT
    per8 = tT // SUBLANES
    in_specs, ins = [], []
    for arr in rows:
        in_specs.append(pl.BlockSpec((tT, arr.shape[1]), lambda i: (i, 0)))
        ins.append(arr)
    for arr in prevs:
        in_specs.append(pl.BlockSpec((SUBLANES, arr.shape[1]), lambda i: (jnp.maximum(i * per8 - 1, 0), 0)))
        ins.append(arr)
    for arr in nexts:
        in_specs.append(pl.BlockSpec((SUBLANES, arr.shape[1]),
                                     lambda i: (jnp.minimum((i + 1) * per8, T // SUBLANES - 1), 0)))
        ins.append(arr)
    for arr in consts:
        in_specs.append(pl.BlockSpec(arr.shape, lambda i, nd=arr.ndim: (0,) * nd))
        ins.append(arr)
    out_specs, out_shapes = [], []
    for o in outs:
        if o[0] == "row":
            out_specs.append(pl.BlockSpec((tT, o[1]), lambda i: (i, 0)))
            out_shapes.append(jax.ShapeDtypeStruct((T, o[1]), o[2]))
        else:
            out_specs.append(pl.BlockSpec(o[1], lambda i: (0, 0)))
            out_shapes.append(jax.ShapeDtypeStruct(o[1], F32))
    nr, npv, nnx, nc = len(rows), len(prevs), len(nexts), len(consts)
    n_in = nr + npv + nnx + nc

    def body(*refs):
        i = pl.program_id(0)
        vals = [r[...] for r in refs[:n_in]]
        res = fn(i, n, vals[:nr], vals[nr:nr + npv], vals[nr + npv:nr + npv + nnx], vals[nr + npv + nnx:])
        for o, o_ref, val in zip(outs, refs[n_in:], res, strict=True):
            if o[0] == "row":
                o_ref[...] = val.astype(o_ref.dtype)
            else:
                @pl.when(i == 0)
                def _(o_ref=o_ref, val=val):
                    o_ref[...] = val.astype(F32)

                @pl.when(i > 0)
                def _(o_ref=o_ref, val=val):
                    o_ref[...] += val.astype(F32)

    res = pl.pallas_call(
        body, name=name, grid=(n,), in_specs=in_specs, out_specs=out_specs, out_shape=out_shapes,
        compiler_params=_params(("arbitrary",)),
    )(*ins)
    return list(res)


def _shift_down(x, prev8, i, s):
    rolled = pltpu.roll(x, s, 0)
    head = pltpu.roll(prev8, s, 0)
    head = jnp.where(i == 0, jnp.zeros_like(head), head)
    rid = lax.broadcasted_iota(jnp.int32, head.shape, 0)
    first = jnp.where(rid < s, head, rolled[:SUBLANES])
    if x.shape[0] == SUBLANES:
        return first
    return jnp.concatenate([first, rolled[SUBLANES:]], axis=0)


def _shift_up(x, next8, i, n, s):
    tT = x.shape[0]
    rolled = pltpu.roll(x, tT - s, 0)
    tail = pltpu.roll(next8, SUBLANES - s, 0)
    tail = jnp.where(i == n - 1, jnp.zeros_like(tail), tail)
    rid = lax.broadcasted_iota(jnp.int32, tail.shape, 0)
    last = jnp.where(rid >= SUBLANES - s, tail, rolled[tT - SUBLANES:])
    return jnp.concatenate([rolled[:tT - SUBLANES], last], axis=0)


def _colsum(x):
    return jnp.sum(x, axis=0, keepdims=True)


def _segsum(x, bd):
    return jnp.dot(x, bd, precision=lax.Precision.HIGH, preferred_element_type=F32)


def _block_diag_ones(width, seg):
    idx = np.arange(width) // seg
    return jnp.asarray((idx[:, None] == idx[None, :]).astype(np.float32))


def _sigmoid(z):
    return 1.0 / (1.0 + jnp.exp(-z))


def _softplus(z):
    return jnp.maximum(z, 0.0) + jnp.log(1.0 + jnp.exp(-jnp.abs(z)))


def _rms_fwd(x, g):
    r = lax.rsqrt(jnp.mean(x * x, axis=-1, keepdims=True) + NORM_EPS)
    return x * r * g


def _rms_bwd(x, g, dy):
    r = lax.rsqrt(jnp.mean(x * x, axis=-1, keepdims=True) + NORM_EPS)
    gdy = dy * g
    dx = r * (gdy - x * (r * r) * jnp.mean(x * gdy, axis=-1, keepdims=True))
    return dx, dy * x * r


GELU_C = math.sqrt(2.0 / math.pi)


def _gelu(x):
    return 0.5 * x * (1.0 + jnp.tanh(GELU_C * (x + 0.044715 * x * x * x)))


def _gelu_and_grad(x):
    th = jnp.tanh(GELU_C * (x + 0.044715 * x * x * x))
    half = 0.5 * (1.0 + th)
    return x * half, half + 0.5 * x * (1.0 - th * th) * GELU_C * (1.0 + 3.0 * 0.044715 * x * x)


RW_CHUNK = 64
NN = (((1,), (0,)), ((), ()))
NT = (((1,), (1,)), ((), ()))
TN = (((0,), (0,)), ((), ()))


def _hdot(a, b, dims):
    return lax.dot_general(a, b, dims, precision=lax.Precision.HIGH, preferred_element_type=F32)


def _ldot(a, b, dims):
    return lax.dot_general(a.astype(BF16), b.astype(BF16), dims, preferred_element_type=F32)


def _chunk_masks():
    ti = lax.broadcasted_iota(jnp.int32, (RW_CHUNK, RW_CHUNK), 0)
    tj = lax.broadcasted_iota(jnp.int32, (RW_CHUNK, RW_CHUNK), 1)
    return tj <= ti, tj < ti, (ti == tj).astype(F32)


def _head(x, h):
    return x[:, h * RW_HEAD_DIM:(h + 1) * RW_HEAD_DIM]


def _heads(fn):
    return [fn(h) for h in range(RW_HEADS)]


def _chunk_rows(r, lw, k, a, b, incl_f):
    c = _hdot(incl_f, lw, NN)
    e_prev, e_neg, e_pos = jnp.exp(c - lw), jnp.exp(-c), jnp.exp(c)
    return dict(At=a * e_prev, Bt=b * e_neg, Kt=k * e_neg, Rt=r * e_pos, e_prev=e_prev, e_neg=e_neg, e_pos=e_pos)


def _chunk_coeffs(q, incl, strict):
    A1 = _heads(lambda h: jnp.where(strict, _hdot(_head(q["At"], h), _head(q["Bt"], h), NT), 0.0))
    A2 = _heads(lambda h: jnp.where(strict, _hdot(_head(q["At"], h), _head(q["Kt"], h), NT), 0.0))
    W1 = _heads(lambda h: jnp.where(incl, _hdot(_head(q["Rt"], h), _head(q["Bt"], h), NT), 0.0))
    W2 = _heads(lambda h: jnp.where(incl, _ldot(_head(q["Rt"], h), _head(q["Kt"], h), NT), 0.0))
    return A1, A2, W1, W2


def _rwkv_chunk_prep(r, lw, k, a, b, v):
    T = r.shape[0]
    nC = T // RW_CHUNK
    H, N = RW_HEADS, RW_HEAD_DIM

    def body(r_ref, lw_ref, k_ref, a_ref, b_ref, v_ref,
             at_ref, bt_ref, kt_ref, rt_ref, a2v_ref, w2v_ref, ti_ref, w1_ref, a2_ref, w2_ref, pl_ref):
        incl, strict, eye = _chunk_masks()
        q = _chunk_rows(r_ref[...], lw_ref[...], k_ref[...], a_ref[...], b_ref[...], incl.astype(F32))
        at_ref[...], bt_ref[...], kt_ref[...], rt_ref[...] = q["At"], q["Bt"], q["Kt"], q["Rt"]
        pl_ref[0] = jnp.broadcast_to(q["e_pos"][RW_CHUNK - 1:RW_CHUNK, :], (SUBLANES, RW_WIDTH))
        A1, A2, W1, W2 = _chunk_coeffs(q, incl, strict)
        V = v_ref[...]
        a2v_ref[...] = jnp.concatenate(_heads(lambda h: _hdot(A2[h], _head(V, h), NN)), axis=1)
        w2v_ref[...] = jnp.concatenate(_heads(lambda h: _ldot(W2[h], _head(V, h), NN)), axis=1)
        tinv, pw = [eye + m for m in A1], A1
        for stage in range(5):
            dot = _hdot if stage == 0 else _ldot
            pw = [dot(m, m, NN) for m in pw]
            tinv = [t + dot(t, m, NN) for t, m in zip(tinv, pw, strict=True)]
        for h in range(H):
            ti_ref[0, h] = tinv[h]
            w1_ref[0, h] = W1[h]
            a2_ref[0, h] = A2[h]
            w2_ref[0, h] = W2[h]

    row_spec = pl.BlockSpec((RW_CHUNK, RW_WIDTH), lambda n: (n, 0))
    st_spec = pl.BlockSpec((1, H, N, N), lambda n: (n, 0, 0, 0))
    row_shape = jax.ShapeDtypeStruct((T, RW_WIDTH), F32)
    st_shape = jax.ShapeDtypeStruct((nC, H, N, N), F32)
    return pl.pallas_call(
        body, name="rwkv_chunk_prep", grid=(nC,),
        in_specs=[row_spec] * 6,
        out_specs=[row_spec] * 6 + [st_spec] * 4 + [pl.BlockSpec((1, SUBLANES, RW_WIDTH), lambda n: (n, 0, 0))],
        out_shape=[row_shape] * 6 + [st_shape] * 4 + [jax.ShapeDtypeStruct((nC, SUBLANES, RW_WIDTH), F32)],
        compiler_params=_params(("parallel",)),
    )(r, lw, k, a, b, v)


def _rwkv_chunk_fwd(v, at, bt, kt, rt, a2v, w2v, tinv, w1, plast):
    T = v.shape[0]
    nC = T // RW_CHUNK
    H, N = RW_HEADS, RW_HEAD_DIM

    def body(v_ref, at_ref, bt_ref, kt_ref, rt_ref, a2v_ref, w2v_ref, ti_ref, w1_ref, pl_ref,
             y_ref, sa_ref, s0_ref, S_ref):
        @pl.when(pl.program_id(0) == 0)
        def _():
            S_ref[...] = jnp.zeros_like(S_ref)

        V, At, Bt, Kt, Rt = v_ref[...], at_ref[...], bt_ref[...], kt_ref[...], rt_ref[...]
        A2V, W2V, p_last = a2v_ref[...], w2v_ref[...], pl_ref[0, 0:1, :]
        S0 = _heads(lambda h: S_ref[h])
        for h in range(H):
            s0_ref[0, h] = S0[h]
        Z = _heads(lambda h: _hdot(_head(At, h), S0[h], NT) + _head(A2V, h))
        Sa = _heads(lambda h: _hdot(ti_ref[0, h], Z[h], NN))
        X = _heads(lambda h: S0[h] + _hdot(Sa[h], _head(Bt, h), TN) + _hdot(_head(V, h), _head(Kt, h), TN))
        for h in range(H):
            S_ref[h] = X[h] * _head(p_last, h)
        Y = _heads(lambda h: _ldot(_head(Rt, h), S0[h], NT) + _ldot(w1_ref[0, h], Sa[h], NN) + _head(W2V, h))
        y_ref[...] = jnp.concatenate(Y, axis=1)
        sa_ref[...] = jnp.concatenate(Sa, axis=1)

    row_spec = pl.BlockSpec((RW_CHUNK, RW_WIDTH), lambda n: (n, 0))
    st_spec = pl.BlockSpec((1, H, N, N), lambda n: (n, 0, 0, 0))
    row_shape = jax.ShapeDtypeStruct((T, RW_WIDTH), F32)
    return pl.pallas_call(
        body, name="rwkv_chunk_fwd", grid=(nC,),
        in_specs=[row_spec] * 7 + [st_spec, st_spec, pl.BlockSpec((1, SUBLANES, RW_WIDTH), lambda n: (n, 0, 0))],
        out_specs=[row_spec, row_spec, st_spec],
        out_shape=[row_shape, row_shape, jax.ShapeDtypeStruct((nC, H, N, N), F32)],
        scratch_shapes=[pltpu.VMEM((H, N, N), F32)],
        compiler_params=_params(("arbitrary",)),
    )(v, at, bt, kt, rt, a2v, w2v, tinv, w1, plast)


def _rwkv_chunk_bwd(r, lw, k, a, b, v, dy, s0, tinv, w1, a2, w2, sa):
    T = r.shape[0]
    nC = T // RW_CHUNK
    H, N = RW_HEADS, RW_HEAD_DIM

    def body(r_ref, lw_ref, k_ref, a_ref, b_ref, v_ref, dy_ref, s0_ref, ti_ref, w1_ref, a2_ref, w2_ref, sa_ref,
             dr_ref, dlw_ref, dk_ref, da_ref, db_ref, dv_ref, dS_ref):
        @pl.when(pl.program_id(0) == 0)
        def _():
            dS_ref[...] = jnp.zeros_like(dS_ref)

        incl, strict, _ = _chunk_masks()
        incl_f = incl.astype(F32)
        q = _chunk_rows(r_ref[...], lw_ref[...], k_ref[...], a_ref[...], b_ref[...], incl_f)
        At, Bt, Kt, Rt = q["At"], q["Bt"], q["Kt"], q["Rt"]
        A2, W1, W2 = (_heads(lambda h, ref=ref: ref[0, h]) for ref in (a2_ref, w1_ref, w2_ref))
        V, dY, Sa = v_ref[...], dy_ref[...], sa_ref[...]
        hd = _head
        p_last = q["e_pos"][RW_CHUNK - 1:RW_CHUNK, :]
        S0 = _heads(lambda h: s0_ref[0, h])
        G = _heads(lambda h: dS_ref[h] * hd(p_last, h))
        X = _heads(lambda h: S0[h] + _hdot(hd(Sa, h), hd(Bt, h), TN) + _hdot(hd(V, h), hd(Kt, h), TN))
        dc_last = jnp.concatenate(_heads(lambda h: jnp.sum(G[h] * X[h], axis=0, keepdims=True)), axis=1)
        dSa = _heads(lambda h: _hdot(hd(Bt, h), G[h], NT) + _hdot(W1[h], hd(dY, h), TN))
        dZ = _heads(lambda h: _hdot(ti_ref[0, h], dSa[h], TN))
        for h in range(H):
            dS_ref[h] = G[h] + _hdot(dZ[h], hd(At, h), TN) + _hdot(hd(dY, h), hd(Rt, h), TN)
        dA1 = _heads(lambda h: jnp.where(strict, _ldot(dZ[h], hd(Sa, h), NT), 0.0))
        dA2 = _heads(lambda h: jnp.where(strict, _ldot(dZ[h], hd(V, h), NT), 0.0))
        dW1 = _heads(lambda h: jnp.where(incl, _ldot(hd(dY, h), hd(Sa, h), NT), 0.0))
        dW2 = _heads(lambda h: jnp.where(incl, _ldot(hd(dY, h), hd(V, h), NT), 0.0))
        cat = lambda fn: jnp.concatenate(_heads(fn), axis=1)
        dV = cat(lambda h: _ldot(A2[h], dZ[h], TN) + _ldot(W2[h], hd(dY, h), TN) + _ldot(hd(Kt, h), G[h], NT))
        dAt = cat(lambda h: _ldot(dA1[h], hd(Bt, h), NN) + _ldot(dA2[h], hd(Kt, h), NN) + _ldot(dZ[h], S0[h], NN))
        dBt = cat(lambda h: _ldot(dA1[h], hd(At, h), TN) + _ldot(dW1[h], hd(Rt, h), TN) + _ldot(hd(Sa, h), G[h], NN))
        dKt = cat(lambda h: _ldot(dA2[h], hd(At, h), TN) + _ldot(dW2[h], hd(Rt, h), TN) + _ldot(hd(V, h), G[h], NN))
        dRt = cat(lambda h: _ldot(hd(dY, h), S0[h], NN) + _ldot(dW1[h], hd(Bt, h), NN) + _ldot(dW2[h], hd(Kt, h), NN))
        last_row = lax.broadcasted_iota(jnp.int32, (RW_CHUNK, RW_WIDTH), 0) == RW_CHUNK - 1
        dc_prev = dAt * At
        dc = dc_prev + dRt * Rt - dBt * Bt - dKt * Kt + jnp.where(last_row, dc_last, 0.0)
        dr_ref[...] = dRt * q["e_pos"]
        dlw_ref[...] = _hdot(incl_f, dc, TN) - dc_prev
        dk_ref[...] = dKt * q["e_neg"]
        da_ref[...] = dAt * q["e_prev"]
        db_ref[...] = dBt * q["e_neg"]
        dv_ref[...] = dV

    rev = lambda n: nC - 1 - n
    row_spec = pl.BlockSpec((RW_CHUNK, RW_WIDTH), lambda n: (rev(n), 0))
    st_spec = pl.BlockSpec((1, H, N, N), lambda n: (rev(n), 0, 0, 0))
    row_shape = jax.ShapeDtypeStruct((T, RW_WIDTH), F32)
    return pl.pallas_call(
        body, name="rwkv_chunk_bwd", grid=(nC,),
        in_specs=[row_spec] * 7 + [st_spec] * 5 + [row_spec], out_specs=[row_spec] * 6,
        out_shape=[row_shape] * 6, scratch_shapes=[pltpu.VMEM((H, N, N), F32)],
        compiler_params=_params(("arbitrary",)),
    )(r, lw, k, a, b, v, dy, s0, tinv, w1, a2, w2, sa)


def _alibi_slope(head):
    return float(np.float32(2.0 ** (-8.0 * (head + 1) / ATT_HEADS)))


ATT_SPAN = ATT_BLOCK * max(ATT_GROUP_DILATION)
ATT_PAIR_WIDTH = 2 * ATT_HEAD_DIM
ATT_SIDE_BY_SIDE = 8


def _pair_slope(g, hp, j):
    return jnp.where(hp == 0, _alibi_slope(4 * g + j), _alibi_slope(4 * g + 2 + j))


def _att_rows(mi, r, d):
    start = mi * ATT_BLOCK * d + r
    return pl.ds(start, ATT_BLOCK) if d == 1 else pl.ds(start, ATT_BLOCK, stride=d)


def _att_masks():
    qi = lax.broadcasted_iota(jnp.int32, (ATT_BLOCK, ATT_BLOCK), 0)
    kj = lax.broadcasted_iota(jnp.int32, (ATT_BLOCK, ATT_BLOCK), 1)
    return qi, kj


NEG = -1e30


def _att_logits(q, k, slope_d, steps, valid):
    s = lax.dot_general(q.astype(BF16), k.astype(BF16), (((1,), (1,)), ((), ())),
                        preferred_element_type=F32) * (ATT_HEAD_DIM ** -0.5)
    return jnp.where(valid, s - slope_d * steps.astype(F32), NEG)


def _att_fwd(p_att, g):
    T = p_att.shape[0]
    d = ATT_GROUP_DILATION[g]
    W = ATT_PAIR_WIDTH
    nb = T // ATT_SPAN
    mb = ATT_SPAN // (ATT_BLOCK * d)

    def body(q_ref, kc_ref, kp_ref, vc_ref, vp_ref, o_ref, l_ref):
        hp, n = pl.program_id(0), pl.program_id(1)
        qi, kj = _att_masks()
        slopes = [_pair_slope(g, hp, j) * d for j in range(2)]
        blocks = [(r, mi) for r in range(d) for mi in range(mb)]
        for at in range(0, len(blocks), ATT_SIDE_BY_SIDE):
            tasks = []
            for r, mi in blocks[at:at + ATT_SIDE_BY_SIDE]:
                rows = _att_rows(mi, r, d)
                if mi > 0:
                    prev = _att_rows(mi - 1, r, d)
                    kp, vp, has_prev = kc_ref[prev, :], vc_ref[prev, :], True
                else:
                    prev = _att_rows(mb - 1, r, d)
                    kp, vp, has_prev = kp_ref[prev, :], vp_ref[prev, :], n > 0
                q, kc, vc = q_ref[rows, :], kc_ref[rows, :], vc_ref[rows, :]
                for j in range(2):
                    sl = slice(j * ATT_HEAD_DIM, (j + 1) * ATT_HEAD_DIM)
                    tasks.append((q[:, sl], kc[:, sl], kp[:, sl], vc[:, sl], vp[:, sl], has_prev, slopes[j]))
            lc = [_att_logits(t[0], t[1], t[6], qi - kj, kj <= qi) for t in tasks]
            lp = [_att_logits(t[0], t[2], t[6], qi - kj + ATT_BLOCK, (kj >= qi) & t[5]) for t in tasks]
            mx = [jnp.maximum(jnp.max(a, axis=1, keepdims=True), jnp.max(b, axis=1, keepdims=True))
                  for a, b in zip(lc, lp, strict=True)]
            ec = [jnp.exp(a - m) for a, m in zip(lc, mx, strict=True)]
            ep = [jnp.exp(b - m) for b, m in zip(lp, mx, strict=True)]
            den = [jnp.sum(a, axis=1, keepdims=True) + jnp.sum(b, axis=1, keepdims=True)
                   for a, b in zip(ec, ep, strict=True)]
            inv = [1.0 / s for s in den]
            outs = [jnp.dot((a * i).astype(BF16), t[3].astype(BF16), preferred_element_type=F32)
                    + jnp.dot((b * i).astype(BF16), t[4].astype(BF16), preferred_element_type=F32)
                    for a, b, i, t in zip(ec, ep, inv, tasks, strict=True)]
            lses = [jnp.broadcast_to(m + jnp.log(s), (ATT_BLOCK, ATT_HEAD_DIM)) for m, s in zip(mx, den, strict=True)]
            for i, (r, mi) in enumerate(blocks[at:at + ATT_SIDE_BY_SIDE]):
                rows = _att_rows(mi, r, d)
                o_ref[rows, :] = jnp.concatenate(outs[2 * i:2 * i + 2], axis=1)
                l_ref[rows, :] = jnp.concatenate(lses[2 * i:2 * i + 2], axis=1)

    def spec(col0, prev):
        if prev:
            return pl.BlockSpec((ATT_SPAN, W), lambda hp, n: (jnp.maximum(n - 1, 0), col0 + 2 * g + hp))
        return pl.BlockSpec((ATT_SPAN, W), lambda hp, n: (n, col0 + 2 * g + hp))

    o_spec = pl.BlockSpec((ATT_SPAN, W), lambda hp, n: (n, hp))
    o, l = pl.pallas_call(
        body, name=f"att_fwd_g{g}", grid=(2, nb),
        in_specs=[spec(0, False), spec(6, False), spec(6, True), spec(12, False), spec(12, True)],
        out_specs=[o_spec, o_spec],
        out_shape=[jax.ShapeDtypeStruct((T, ATT_GROUP_WIDTH), F32)] * 2,
        compiler_params=_params(("parallel", "arbitrary")),
    )(p_att, p_att, p_att, p_att, p_att)
    return o, l


def _att_bwd(p_att, o, l, do, dl, g):
    T = p_att.shape[0]
    d = ATT_GROUP_DILATION[g]
    W = ATT_PAIR_WIDTH
    nb = T // ATT_SPAN
    mb = ATT_SPAN // (ATT_BLOCK * d)
    scale = ATT_HEAD_DIM ** -0.5

    def body(q_ref, k_ref, v_ref, o_ref, l_ref, do_ref, dl_ref,
             qn_ref, on_ref, ln_ref, don_ref, dln_ref, dq_ref, dk_ref, dv_ref, carry_ref):
        hp, n = pl.program_id(0), pl.program_id(1)
        qi, kj = _att_masks()

        @pl.when(n == 0)
        def _():
            carry_ref[...] = jnp.zeros_like(carry_ref)

        slopes = [_pair_slope(g, hp, j) * d for j in range(2)]
        blocks = [(r, mi) for r in range(d) for mi in range(mb)]
        side_by_side = ATT_SIDE_BY_SIDE // 2
        carry = None
        for at in range(0, len(blocks), side_by_side):
            tasks = []
            for r, mi in blocks[at:at + side_by_side]:
                rows = _att_rows(mi, r, d)
                if mi < mb - 1:
                    nrows = _att_rows(mi + 1, r, d)
                    nxt = (q_ref[nrows, :], o_ref[nrows, :], l_ref[nrows, :], do_ref[nrows, :], dl_ref[nrows, :])
                    has_next = True
                else:
                    nrows = _att_rows(0, r, d)
                    nxt = (qn_ref[nrows, :], on_ref[nrows, :], ln_ref[nrows, :], don_ref[nrows, :],
                           dln_ref[nrows, :])
                    has_next = n < nb - 1
                cur = (q_ref[rows, :], o_ref[rows, :], l_ref[rows, :], do_ref[rows, :], dl_ref[rows, :])
                k_all, v_all = k_ref[rows, :], v_ref[rows, :]
                for j in range(2):
                    sl = slice(j * ATT_HEAD_DIM, (j + 1) * ATT_HEAD_DIM)
                    for blk, steps, valid in ((cur, qi - kj, kj <= qi),
                                              (nxt, qi - kj + ATT_BLOCK, (kj >= qi) & has_next)):
                        q, o_, lse, do_, dlse = (z[:, sl] for z in blk)
                        tasks.append(dict(q=q, o=o_, lse=lse[:, :1], do=do_, dlse=dlse[:, :1], steps=steps,
                                          valid=valid, k=k_all[:, sl], vb=v_all[:, sl].astype(BF16),
                                          slope=slopes[j]))
            p = [jnp.exp(_att_logits(t["q"], t["k"], t["slope"], t["steps"], t["valid"]) - t["lse"]) for t in tasks]
            dp = [lax.dot_general(t["do"].astype(BF16), t["vb"], (((1,), (1,)), ((), ())),
                                  preferred_element_type=F32) for t in tasks]
            dsum = [jnp.sum(t["do"] * t["o"], axis=1, keepdims=True) for t in tasks]
            ds = [a * (b - s + t["dlse"]) for a, b, s, t in zip(p, dp, dsum, tasks, strict=True)]
            dv_ = [jnp.dot(a.T.astype(BF16), t["do"].astype(BF16), preferred_element_type=F32)
                   for a, t in zip(p, tasks, strict=True)]
            dk_ = [jnp.dot(a.T.astype(BF16), t["q"].astype(BF16), preferred_element_type=F32) * scale
                   for a, t in zip(ds, tasks, strict=True)]
            dq_ = [jnp.dot(a.astype(BF16), t["k"].astype(BF16), preferred_element_type=F32) * scale
                   for a, t in zip(ds, tasks, strict=True)]
            for i, (r, mi) in enumerate(blocks[at:at + side_by_side]):
                rows = _att_rows(mi, r, d)
                b = 4 * i
                if mi == 0:
                    carry = carry_ref[r]
                dq_ref[rows, :] = jnp.concatenate([dq_[b], dq_[b + 2]], axis=1) + carry
                carry = jnp.concatenate([dq_[b + 1], dq_[b + 3]], axis=1)
                if mi == mb - 1:
                    carry_ref[r] = carry
                dk_ref[rows, :] = jnp.concatenate([dk_[b] + dk_[b + 1], dk_[b + 2] + dk_[b + 3]], axis=1)
                dv_ref[rows, :] = jnp.concatenate([dv_[b] + dv_[b + 1], dv_[b + 2] + dv_[b + 3]], axis=1)

    head_rows = ATT_BLOCK * d
    nxt_n = lambda n: jnp.minimum((n + 1) * mb, T // head_rows - 1)
    cur_p = lambda col0: pl.BlockSpec((ATT_SPAN, W), lambda hp, n: (n, col0 + 2 * g + hp))
    cur_o = pl.BlockSpec((ATT_SPAN, W), lambda hp, n: (n, hp))
    nxt_o = pl.BlockSpec((head_rows, W), lambda hp, n: (nxt_n(n), hp))
    dq, dk, dv = pl.pallas_call(
        body, name=f"att_bwd_g{g}", grid=(2, nb),
        in_specs=[cur_p(0), cur_p(6), cur_p(12), cur_o, cur_o, cur_o, cur_o,
                  pl.BlockSpec((head_rows, W), lambda hp, n: (nxt_n(n), 2 * g + hp)), nxt_o, nxt_o, nxt_o, nxt_o],
        out_specs=[cur_o, cur_o, cur_o],
        out_shape=[jax.ShapeDtypeStruct((T, ATT_GROUP_WIDTH), F32)] * 3,
        scratch_shapes=[pltpu.VMEM((d, ATT_BLOCK, W), F32)],
        compiler_params=_params(("parallel", "arbitrary")),
    )(p_att, p_att, p_att, o, l, do, dl, p_att, o, l, do, dl)
    return dq, dk, dv


FFN_TILE = 2 * D_FF // N_CHIPS
RKV = 3 * RW_WIDTH
WA = 128
XG = 160
RW_COLS = RKV + WA + XG


def _local_step(x, p, W, target, late_weights=None, early_grads=None, by_chip=False):
    T = x.shape[0]
    tT = 256
    bd512 = _block_diag_ones(RW_WIDTH, RW_HEAD_DIM)
    bd256 = _block_diag_ones(ATT_GROUP_WIDTH, ATT_HEAD_DIM)
    G = {}
    W = dict(W)

    w_in = W["w_in"]
    w_rkv, w_wa, w_xg, w_att = (w_in[:, :RKV], w_in[:, RKV:RKV + WA], w_in[:, RKV + WA:RW_COLS],
                                w_in[:, RW_COLS:])
    mu = W["rw_mu"]
    mu_rkv, mu_wa, mu_xg = mu[:, :RKV], mu[:, RKV:RKV + WA], mu[:, RKV + WA:]
    zpad = jnp.zeros((64, RW_WIDTH), W["rw_w_up"].dtype)
    w_up_pad = jnp.concatenate([W["rw_w_up"], zpad], axis=0)
    a_up_pad = jnp.concatenate([zpad, W["rw_a_up"]], axis=0)
    r_k = W["rw_r_k"].reshape(1, RW_WIDTH)

    (h,) = _rowwise("norm_mix", lambda i, n, r, pv, nx, c: [_rms_fwd(r[0], c[0])], T, tT,
                    rows=[x], consts=[W["g_mix"]], outs=[("row", D_MODEL, BF16)])
    p_rkv = _mm("proj_rkv", h, w_rkv, "nn")
    p_wa = _mm("proj_wa", h, w_wa, "nn")
    p_xg = _mm("proj_xg", h, w_xg, "nn")
    p_att = _mm("proj_att", h, w_att, "nn", tn=768)
    z_gate = _mm("proj_gate", h, W["w_gate"], "nn")

    def rw_pre_core(i, rows, prevs, consts):
        prkv, pwa, pxg = rows[:3]
        (mrkv, mwa, mxg, w0, a0, k_k, k_a, wup, aup, gup, bd) = consts[:11]
        m_rkv = prkv + (_shift_down(prkv, prevs[0], i, 1) - prkv) * mrkv
        m_wa = pwa + (_shift_down(pwa, prevs[1], i, 1) - pwa) * mwa
        m_xg = pxg + (_shift_down(pxg, prevs[2], i, 1) - pxg) * mxg
        r, k, v = m_rkv[:, :RW_WIDTH], m_rkv[:, RW_WIDTH:2 * RW_WIDTH], m_rkv[:, 2 * RW_WIDTH:]
        tw = jnp.tanh(m_wa)
        lw = w0 + jnp.dot(tw.astype(BF16), wup.astype(BF16), preferred_element_type=F32)
        wlog = -_softplus(-lw) - 0.5
        log_decay = -jnp.exp(wlog)
        a = _sigmoid(a0 + jnp.dot(m_wa.astype(BF16), aup.astype(BF16), preferred_element_type=F32))
        sg = _sigmoid(m_xg)
        gate = jnp.dot(sg.astype(BF16), gup.astype(BF16), preferred_element_type=F32)
        kkp = k * k_k
        nrm = jnp.sqrt(_segsum(kkp * kkp, bd))
        nrm_c = jnp.maximum(nrm, 1e-12)
        kk = kkp / nrm_c
        k2 = k * (1.0 + (a - 1.0) * k_a)
        return dict(r=r, k=k, v=v, tw=tw, lw=lw, wlog=wlog, log_decay=log_decay, a=a, sg=sg, gate=gate, kkp=kkp,
                    nrm=nrm, nrm_c=nrm_c, kk=kk, k2=k2, m_rkv=m_rkv, m_wa=m_wa, m_xg=m_xg)

    pre_consts = [mu_rkv, mu_wa, mu_xg, W["rw_w0"], W["rw_a0"], W["rw_k_k"], W["rw_k_a"],
                  w_up_pad, a_up_pad, W["rw_g_up"], bd512]

    def rw_pre(i, n, rows, prevs, nexts, consts):
        q = rw_pre_core(i, rows, prevs, consts)
        return [q["r"], q["log_decay"], q["k2"], q["v"], -q["kk"], q["kk"] * q["a"], q["gate"]]

    r_s, w_s, k_s, v_s, a_s, b_s, gate_s = _rowwise(
        "rwkv_pre", rw_pre, T, tT, rows=[p_rkv, p_wa, p_xg], prevs=[p_rkv, p_wa, p_xg], consts=pre_consts,
        outs=[("row", RW_WIDTH, F32)] * 7)
    (at_s, bt_s, kt_s, rt_s, a2v_s, w2v_s, tinv_s, w1_s, a2_s, w2_s,
     plast_s) = _rwkv_chunk_prep(r_s, w_s, k_s, a_s, b_s, v_s)
    y_scan, sa_s, s0_s = _rwkv_chunk_fwd(v_s, at_s, bt_s, kt_s, rt_s, a2v_s, w2v_s, tinv_s, w1_s, plast_s)

    def rw_post_core(rows, consts):
        y, r, k2, v, gate = rows[:5]
        ln_g, ln_b, rk, bd = consts[:4]
        mean = _segsum(y, bd) * (1.0 / RW_HEAD_DIM)
        yc = y - mean
        var = _segsum(yc * yc, bd) * (1.0 / RW_HEAD_DIM)
        rstd = lax.rsqrt(var + RW_LN_EPS)
        yn = yc * rstd
        s = _segsum(r * k2 * rk, bd)
        return dict(yn=yn, rstd=rstd, s=s, pre=yn * ln_g + ln_b + s * v)

    post_consts = [W["rw_ln_g"], W["rw_ln_b"], r_k, bd512]
    (y_a,) = _rowwise("rwkv_post", lambda i, n, r, pv, nx, c: [rw_post_core(r, c)["pre"] * r[4]], T, tT,
                      rows=[y_scan, r_s, k_s, v_s, gate_s], consts=post_consts, outs=[("row", RW_WIDTH, BF16)])

    att = [_att_fwd(p_att, g) for g in range(3)]

    def comb_weights(ls):
        mx = jnp.maximum(jnp.maximum(ls[0], ls[1]), ls[2])
        es = [jnp.exp(l - mx) for l in ls]
        den = es[0] + es[1] + es[2]
        return [e / den for e in es]

    def att_comb(i, n, rows, pv, nx, c):
        wts = comb_weights(rows[3:6])
        return [wts[0] * rows[0] + wts[1] * rows[1] + wts[2] * rows[2]]

    (y_b,) = _rowwise("att_combine", att_comb, T, tT, rows=[att[0][0], att[1][0], att[2][0], att[0][1], att[1][1],
                                                            att[2][1]], outs=[("row", ATT_GROUP_WIDTH, BF16)])

    if late_weights is not None:
        W.update(late_weights(y_b))
    br_a = _mm("branch_a", y_a, W["w_branch_a"], "nn")
    br_b = _mm("branch_b", y_b, W["w_branch_b"], "nn")

    def merge(i, n, rows, pv, nx, c):
        gates = _sigmoid(rows[0] + c[0])
        return [gates[:, :D_MODEL] * rows[1] + gates[:, D_MODEL:] * rows[2]]

    (merged,) = _rowwise("merge", merge, T, tT, rows=[z_gate, br_a, br_b], consts=[W["b_gate"]],
                         outs=[("row", D_MODEL, BF16)])
    x1 = _mm("mix_out", merged, W["w_out"], "nn", add=x)

    (h2,) = _rowwise("norm_ffn", lambda i, n, r, pv, nx, c: [_rms_fwd(r[0], c[0])], T, tT,
                     rows=[x1], consts=[W["g_ffn"]], outs=[("row", D_MODEL, BF16)])
    u = _mm("ffn_up", h2, W["w_up"], "nn", tn=FFN_TILE)

    def conv_core(i, rows, prevs, consts):
        uu, cw, cb = rows[0], consts[0], consts[1]
        u1 = _shift_down(uu, prevs[0], i, 1)
        u2 = _shift_down(uu, prevs[0], i, 2)
        uc = cb + cw[0:1] * uu + cw[1:2] * u1 + cw[2:3] * u2
        return uc[:, :D_FF], uc[:, D_FF:], u1, u2

    def glu(i, n, rows, prevs, nx, consts):
        gate, val, _, _ = conv_core(i, rows, prevs, consts)
        return [_gelu(gate) * val]

    tF = 128
    (act,) = _rowwise("conv_glu", glu, T, tF, rows=[u], prevs=[u], consts=[W["conv_w"], W["conv_b"]],
                      outs=[("row", D_FF, BF16)])
    x2 = _mm("ffn_down", act, W["w_down"], "nn", add=x1)

    (h3,) = _rowwise("norm_ple", lambda i, n, r, pv, nx, c: [_rms_fwd(r[0], c[0])], T, tT,
                     rows=[x2], consts=[W["g_ple"]], outs=[("row", D_MODEL, BF16)])
    z_ple = _mm("ple_gate", h3, W["w_ple_gate"], "nn")
    e_ple = _mm("ple_emb", p, W["w_ple"], "nn")

    def head(i, n, rows, pv, nx, consts):
        x2_, z, e, tgt = rows
        pg = _sigmoid(z)
        x3 = x2_ + pg * e
        y = _rms_fwd(x3, consts[0])
        err = y - tgt
        loss = 0.5 * jnp.sum(jnp.sum(err * err, axis=1, keepdims=True) * (1.0 / D_MODEL), axis=0, keepdims=True)
        dy = err * (1.0 / D_MODEL)
        dx3, dgf = _rms_bwd(x3, consts[0], dy)
        return [dx3, dx3 * pg, dx3 * e * pg * (1.0 - pg), jnp.broadcast_to(loss, (1, LANES)), _colsum(dgf)]

    dx3, de, dz, loss_acc, G["g_final"] = _rowwise(
        "loss_head", head, T, tT, rows=[x2, z_ple, e_ple, target], consts=[W["g_final"].reshape(1, D_MODEL)],
        outs=[("row", D_MODEL, F32), ("row", D_MODEL, BF16), ("row", D_MODEL, BF16), ("acc", (1, LANES)),
              ("acc", (1, D_MODEL))])
    G["w_ple"] = _mm("d_w_ple", p, de, "tn", out_by_chip=by_chip)
    G["w_ple_gate"] = _mm("d_w_ple_gate", h3, dz, "tn")
    dh3 = _mm("d_h3", dz, W["w_ple_gate"], "nt")

    def norm_bwd(i, n, rows, pv, nx, consts):
        dx, dg = _rms_bwd(rows[0], consts[0], rows[1])
        return [rows[2] + dx, _colsum(dg)]

    dx2, G["g_ple"] = _rowwise("d_norm_ple", norm_bwd, T, tT, rows=[x2, dh3, dx3], consts=[W["g_ple"]],
                               outs=[("row", D_MODEL, F32), ("acc", (1, D_MODEL))])

    dact = _mm("d_act", dx2, W["w_down"], "nt")
    G["w_down"] = _mm("d_w_down", act, dx2, "tn")

    def glu_grad(gate, val, da):
        act_, slope = _gelu_and_grad(gate)
        return jnp.concatenate([da * val * slope, da * act_], axis=1)

    def glu_bwd(i, n, rows, prevs, nexts, consts):
        uu, da = rows
        cw = consts[0]
        gate, val, u1, u2 = conv_core(i, rows, prevs, consts)
        duc = glu_grad(gate, val, da)
        dcw = jnp.concatenate([_colsum(duc * uu), _colsum(duc * u1), _colsum(duc * u2)], axis=0)
        gate_n, val_n, _, _ = conv_core(1, [nexts[0]], [uu[tF - SUBLANES:]], consts)
        duc_n = glu_grad(gate_n, val_n, nexts[1])
        du = (cw[0:1] * duc + cw[1:2] * _shift_up(duc, duc_n, i, n, 1) + cw[2:3] * _shift_up(duc, duc_n, i, n, 2))
        return [du, _colsum(duc), dcw]

    du, G["conv_b"], G["conv_w"] = _rowwise(
        "d_conv_glu", glu_bwd, T, tF, rows=[u, dact], prevs=[u], nexts=[u, dact],
        consts=[W["conv_w"], W["conv_b"]],
        outs=[("row", 2 * D_FF, BF16), ("acc", (1, 2 * D_FF)), ("acc", (3, 2 * D_FF))])
    G["w_up"] = _mm("d_w_up", h2, du, "tn", out_by_chip=by_chip, tn=FFN_TILE)
    dh2 = _mm("d_h2", du, W["w_up"], "nt", tk=FFN_TILE)
    dx1, G["g_ffn"] = _rowwise("d_norm_ffn", norm_bwd, T, tT, rows=[x1, dh2, dx2], consts=[W["g_ffn"]],
                               outs=[("row", D_MODEL, F32), ("acc", (1, D_MODEL))])

    b_gate = W["b_gate"]
    if early_grads is not None:
        b_gate = b_gate + early_grads(G, 0)[0:1, 0:1]
    dmerged = _mm("d_merged", dx1, W["w_out"], "nt")
    G["w_out"] = _mm("d_w_out", merged, dx1, "tn")

    def merge_bwd(i, n, rows, pv, nx, consts):
        z, a_, b_, dm = rows
        gates = _sigmoid(z + consts[0])
        ga, gb = gates[:, :D_MODEL], gates[:, D_MODEL:]
        dz_ = jnp.concatenate([dm * a_ * ga * (1.0 - ga), dm * b_ * gb * (1.0 - gb)], axis=1)
        return [dm * ga, dm * gb, dz_, _colsum(dz_)]

    d_br_a, d_br_b, dz_gate, G["b_gate"] = _rowwise(
        "d_merge", merge_bwd, T, tT, rows=[z_gate, br_a, br_b, dmerged], consts=[b_gate],
        outs=[("row", D_MODEL, BF16), ("row", D_MODEL, BF16), ("row", 2 * D_MODEL, BF16), ("acc", (1, 2 * D_MODEL))])
    G["w_branch_a"] = _mm("d_w_branch_a", y_a, d_br_a, "tn", out_by_chip=by_chip)
    G["w_branch_b"] = _mm("d_w_branch_b", y_b, d_br_b, "tn", out_by_chip=by_chip)
    G["w_gate"] = _mm("d_w_gate", h, dz_gate, "tn", out_by_chip=by_chip)
    if early_grads is not None:
        post_consts = [post_consts[0] + early_grads(G, 1)[0:1, 0:1]] + post_consts[1:]
    dy_a = _mm("d_y_a", d_br_a, W["w_branch_a"], "nt")
    dy_b = _mm("d_y_b", d_br_b, W["w_branch_b"], "nt")

    def att_comb_bwd(i, n, rows, pv, nx, consts):
        os_, ls, dy = rows[0:3], rows[3:6], rows[6]
        wts = comb_weights(ls)
        dws = [_segsum(dy * o_, consts[0]) for o_ in os_]
        mix = wts[0] * dws[0] + wts[1] * dws[1] + wts[2] * dws[2]
        return [wts[g_] * dy for g_ in range(3)] + [wts[g_] * (dws[g_] - mix) for g_ in range(3)]

    comb = _rowwise("d_att_combine", att_comb_bwd, T, tT,
                    rows=[att[0][0], att[1][0], att[2][0], att[0][1], att[1][1], att[2][1], dy_b], consts=[bd256],
                    outs=[("row", ATT_GROUP_WIDTH, F32)] * 6)
    dqkv = [_att_bwd(p_att, att[g][0], att[g][1], comb[g], comb[3 + g], g) for g in range(3)]
    dp_att = jnp.concatenate([dqkv[g][part] for part in range(3) for g in range(3)], axis=1).astype(BF16)

    def rw_post_bwd(i, n, rows, pv, nx, consts):
        y, r, k2, v, gate, dya = rows
        ln_g, ln_b, rk, bd = consts
        q = rw_post_core(rows, consts)
        dpre = dya * gate
        dgate = dya * q["pre"]
        dyn = dpre * ln_g
        inv = 1.0 / RW_HEAD_DIM
        dy_scan = q["rstd"] * (dyn - _segsum(dyn, bd) * inv - q["yn"] * (_segsum(dyn * q["yn"], bd) * inv))
        ds = _segsum(dpre * v, bd)
        return [dy_scan, dgate, ds * k2 * rk, ds * r * rk, dpre * q["s"],
                _colsum(dpre * q["yn"]), _colsum(dpre), _colsum(ds * r * k2)]

    dy_scan, dgate, dr_b, dk2_b, dv_b, G["rw_ln_g"], G["rw_ln_b"], d_rk = _rowwise(
        "d_rwkv_post", rw_post_bwd, T, tT, rows=[y_scan, r_s, k_s, v_s, gate_s, dy_a], consts=post_consts,
        outs=[("row", RW_WIDTH, F32)] * 5 + [("acc", (1, RW_WIDTH))] * 3)
    G["rw_r_k"] = d_rk.reshape(RW_HEADS, RW_HEAD_DIM)

    dr_s, dw_s, dk_s, da_s, db_s, dv_s = _rwkv_chunk_bwd(r_s, w_s, k_s, a_s, b_s, v_s, dy_scan, s0_s, tinv_s, w1_s,
                                                         a2_s, w2_s, sa_s)

    def rw_pre_bwd(i, n, rows, prevs, nx, consts):
        q = rw_pre_core(i, rows, prevs, consts)
        (mrkv, mwa, mxg, w0, a0, k_k, k_a, wup, aup, gup, bd) = consts
        dr, dlogdecay, dk2, dv, dav, dbv, dgate_ = rows[3:10]
        dr = dr + rows[10]
        dk2 = dk2 + rows[11]
        dv = dv + rows[12]
        a, k, kk = q["a"], q["k"], q["kk"]
        dk = dk2 * (1.0 + (a - 1.0) * k_a)
        da = dk2 * k * k_a + dbv * kk
        dkk = dbv * a - dav
        live = q["nrm"] > 1e-12
        dkkp = jnp.where(live, dkk - kk * _segsum(dkk * kk, bd), dkk) / q["nrm_c"]
        dk = dk + dkkp * k_k
        dlw = dlogdecay * q["log_decay"] * _sigmoid(-q["lw"])
        dla = da * a * (1.0 - a)
        nt = (((1,), (1,)), ((), ()))
        dtw = lax.dot_general(dlw.astype(BF16), wup.astype(BF16), nt, preferred_element_type=F32)
        dxa = lax.dot_general(dla.astype(BF16), aup.astype(BF16), nt, preferred_element_type=F32)
        dm_wa = dtw * (1.0 - q["tw"] * q["tw"]) + dxa
        dsg = lax.dot_general(dgate_.astype(BF16), gup.astype(BF16), nt, preferred_element_type=F32)
        dm_xg = dsg * q["sg"] * (1.0 - q["sg"])
        dm_rkv = jnp.concatenate([dr, dk, dv], axis=1)
        prkv, pwa, pxg = rows[:3]
        dmu = jnp.concatenate([_colsum(dm_rkv * (_shift_down(prkv, prevs[0], i, 1) - prkv)),
                               _colsum(dm_wa * (_shift_down(pwa, prevs[1], i, 1) - pwa)),
                               _colsum(dm_xg * (_shift_down(pxg, prevs[2], i, 1) - pxg))], axis=1)
        return [dm_rkv, dm_wa, dm_xg, dlw, dla, q["tw"], q["m_wa"], q["sg"], dmu,
                _colsum(dlw), _colsum(dla), _colsum(dkkp * k), _colsum(dk2 * k * (a - 1.0))]

    (dm_rkv, dm_wa, dm_xg, dlw, dla, tw_s, mwa_s, sg_s, G["rw_mu"], G["rw_w0"], G["rw_a0"], G["rw_k_k"],
     G["rw_k_a"]) = _rowwise(
        "d_rwkv_pre", rw_pre_bwd, T, tT,
        rows=[p_rkv, p_wa, p_xg, dr_s, dw_s, dk_s, dv_s, da_s, db_s, dgate, dr_b, dk2_b, dv_b],
        prevs=[p_rkv, p_wa, p_xg], consts=pre_consts,
        outs=[("row", RKV, F32), ("row", WA, F32), ("row", XG, F32), ("row", RW_WIDTH, BF16),
              ("row", RW_WIDTH, BF16), ("row", WA, BF16), ("row", WA, BF16), ("row", XG, BF16),
              ("acc", (1, RW_COLS))] + [("acc", (1, RW_WIDTH))] * 4)
    G["rw_w_up"] = _mm("d_rw_w_up", tw_s, dlw, "tn")[:64]
    G["rw_a_up"] = _mm("d_rw_a_up", mwa_s, dla, "tn")[64:]
    G["rw_g_up"] = _mm("d_rw_g_up", sg_s, dgate, "tn")

    def shift_bwd(i, n, rows, pv, nexts, consts):
        return [rows[j] * (1.0 - consts[j]) + _shift_up(rows[j], nexts[j], i, n, 1) * consts[j] for j in range(3)]

    dp_rkv, dp_wa, dp_xg = _rowwise(
        "d_token_shift", shift_bwd, T, tT, rows=[dm_rkv, dm_wa, dm_xg], nexts=[dm_rkv, dm_wa, dm_xg],
        consts=[mu_rkv, mu_wa, mu_xg], outs=[("row", RKV, BF16), ("row", WA, BF16), ("row", XG, BF16)])

    G["w_in"] = jnp.concatenate([_mm("d_w_rkv", h, dp_rkv, "tn"), _mm("d_w_wa", h, dp_wa, "tn"),
                                 _mm("d_w_xg", h, dp_xg, "tn"), _mm("d_w_att", h, dp_att, "tn", tn=768)], axis=1)
    if early_grads is not None:
        w_wa = w_wa + early_grads(G, 2)[0:1, 0:1].astype(w_wa.dtype)
    dh = _mm("d_h_gate", dz_gate, W["w_gate"], "nt")
    dh = _mm("d_h_rkv", dp_rkv, w_rkv, "nt", add=dh)
    dh = _mm("d_h_wa", dp_wa, w_wa, "nt", add=dh)
    dh = _mm("d_h_xg", dp_xg, w_xg, "nt", add=dh)
    dh = _mm("d_h_att", dp_att, w_att, "nt", add=dh)
    dx, G["g_mix"] = _rowwise("d_norm_mix", norm_bwd, T, tT, rows=[x, dh, dx1], consts=[W["g_mix"]],
                              outs=[("row", D_MODEL, F32), ("acc", (1, D_MODEL))])
    return loss_acc[:, :1], dx, G


HBM_SPEC = pl.BlockSpec(memory_space=pltpu.HBM)


def _place():
    x, y, c = lax.axis_index("x"), lax.axis_index("y"), lax.axis_index("c")
    return x, y, c, [(1 - x, y), (x, 1 - y), (1 - x, 1 - y)]


def _remote(src, dst, send_sems, recv_sems, k, to):
    return pltpu.make_async_remote_copy(src_ref=src, dst_ref=dst, send_sem=send_sems.at[k], recv_sem=recv_sems.at[k],
                                        device_id=to, device_id_type=MESH)


ROW_ALIGN = 16


def _splits(rows):
    return rows % (2 * ROW_ALIGN) == 0


def _half_rows(ref_rows, c, first):
    half = ref_rows // 2
    which = c if first else 1 - c
    return pl.ds(pl.multiple_of(which * half, ROW_ALIGN), half)


def _gather_chips(shards):
    n = len(shards)
    split = [_splits(s.shape[0]) for s in shards]

    def body(*refs):
        w_refs, out_refs = refs[:n], refs[n:2 * n]
        send_sems, recv_sems = refs[2 * n:]
        x, y, c, chips = _place()
        me = 2 * x + y
        sends, passed = [], []
        for i in range(n):
            for j, (px, py) in enumerate(chips):
                if split[i]:
                    mine = _half_rows(w_refs[i].shape[0], c, True)
                    cp = _remote(w_refs[i].at[mine], out_refs[i].at[me, mine], send_sems, recv_sems, 6 * i + j,
                                 (px, py, c))
                else:
                    cp = _remote(w_refs[i], out_refs[i].at[me], send_sems, recv_sems, 6 * i + j, (px, py, c))
                cp.start()
                sends.append(cp)
        for i in range(n):
            for j, (px, py) in enumerate(chips):
                if split[i]:
                    landed = out_refs[i].at[2 * px + py, _half_rows(w_refs[i].shape[0], c, True)]
                    _remote(landed, landed, send_sems, recv_sems, 6 * i + j, (px, py, c)).wait_recv()
                    cp = _remote(landed, landed, send_sems, recv_sems, 6 * i + 3 + j, (x, y, 1 - c))
                    cp.start()
                    passed.append(cp)
                else:
                    landed = out_refs[i].at[2 * px + py]
                    _remote(landed, landed, send_sems, recv_sems, 6 * i + j, (px, py, c)).wait_recv()
        for i in range(n):
            if split[i]:
                for j, (px, py) in enumerate(chips):
                    landed = out_refs[i].at[2 * px + py, _half_rows(w_refs[i].shape[0], c, False)]
                    _remote(landed, landed, send_sems, recv_sems, 6 * i + 3 + j, (x, y, 1 - c)).wait_recv()
        for cp in sends + passed:
            cp.wait_send()

    outs = pl.pallas_call(
        body, name="gather_weights", in_specs=[HBM_SPEC] * n, out_specs=[HBM_SPEC] * n,
        out_shape=[jax.ShapeDtypeStruct((N_CHIPS,) + s.shape, s.dtype) for s in shards],
        scratch_shapes=[pltpu.SemaphoreType.DMA((6 * n,)), pltpu.SemaphoreType.DMA((6 * n,))],
    )(*shards)
    me = 2 * lax.axis_index("x") + lax.axis_index("y")
    return [lax.dynamic_update_slice(o, s[None], (me, 0, 0)) for o, s in zip(outs, shards, strict=True)]


def _swap_halves(name, gs):
    n = len(gs)

    def body(*refs):
        g_refs, out_refs = refs[:n], refs[n:2 * n]
        send_sems, recv_sems = refs[2 * n:]
        x, y, c, _ = _place()
        cps = []
        for i in range(n):
            theirs = _half_rows(g_refs[i].shape[1], c, False)
            cp = _remote(g_refs[i].at[:, theirs, :], out_refs[i], send_sems, recv_sems, i, (x, y, 1 - c))
            cp.start()
            cps.append(cp)
        for cp in cps:
            cp.wait()

    return pl.pallas_call(
        body, name=name, in_specs=[HBM_SPEC] * n, out_specs=[HBM_SPEC] * n,
        out_shape=[jax.ShapeDtypeStruct((N_CHIPS, g.shape[1] // 2, g.shape[2]), g.dtype) for g in gs],
        scratch_shapes=[pltpu.SemaphoreType.DMA((n,)), pltpu.SemaphoreType.DMA((n,))],
    )(*gs)


def _join_halves(reds):
    n = len(reds)

    def body(*refs):
        r_refs, out_refs = refs[:n], refs[n:2 * n]
        send_sems, recv_sems = refs[2 * n:]
        x, y, c, _ = _place()
        cps = []
        for i in range(n):
            mine = _half_rows(out_refs[i].shape[0], c, True)
            cp = _remote(r_refs[i], out_refs[i].at[mine], send_sems, recv_sems, i, (x, y, 1 - c))
            cp.start()
            cps.append(cp)
        for cp in cps:
            cp.wait()

    outs = pl.pallas_call(
        body, name="join_halves", in_specs=[HBM_SPEC] * n, out_specs=[HBM_SPEC] * n,
        out_shape=[jax.ShapeDtypeStruct((2 * r.shape[0], r.shape[1]), r.dtype) for r in reds],
        scratch_shapes=[pltpu.SemaphoreType.DMA((n,)), pltpu.SemaphoreType.DMA((n,))],
    )(*reds)
    c = lax.axis_index("c")
    return [lax.dynamic_update_slice(o, r, (c * r.shape[0], 0)) for o, r in zip(outs, reds, strict=True)]


def _gather_all(vec):
    R = vec.shape[0]

    def body(v_ref, out_ref, send_sems, recv_sems, local_sem):
        x, y, c, _ = _place()
        me = 4 * x + 2 * y + c
        local = pltpu.make_async_copy(v_ref, out_ref.at[me], local_sem)
        local.start()
        peers = [(x ^ (k >> 2), y ^ ((k >> 1) & 1), c ^ (k & 1)) for k in range(1, N_DEV)]
        sends = [_remote(v_ref, out_ref.at[me], send_sems, recv_sems, k, to) for k, to in enumerate(peers)]
        for cp in sends:
            cp.start()
        for k, (px, py, pc) in enumerate(peers):
            landed = out_ref.at[4 * px + 2 * py + pc]
            _remote(landed, landed, send_sems, recv_sems, k, (px, py, pc)).wait_recv()
        for cp in sends:
            cp.wait_send()
        local.wait()

    return pl.pallas_call(
        body, name="gather_small", in_specs=[HBM_SPEC], out_specs=HBM_SPEC,
        out_shape=jax.ShapeDtypeStruct((N_DEV, R, LANES), vec.dtype),
        scratch_shapes=[pltpu.SemaphoreType.DMA((7,)), pltpu.SemaphoreType.DMA((7,)), pltpu.SemaphoreType.DMA],
    )(vec)


SEM_SPEC = pl.BlockSpec(memory_space=pltpu.SEMAPHORE)
DATAFLOW = pltpu.SideEffectType.DATAFLOW_SIDE_EFFECTING


def _travel_copies(mode, src_refs, land_refs, send_sems, recv_sems):
    x, y, c, chips = _place()
    me = 2 * x + y
    pairs = []
    for i, (src, land) in enumerate(zip(src_refs, land_refs, strict=True)):
        for j, (px, py) in enumerate(chips):
            peer = 2 * px + py
            if mode == "scatter":
                mine, there, here = src.at[peer], land.at[me], land.at[peer]
            elif _splits(src.shape[0]):
                rows = _half_rows(src.shape[0], c, True)
                mine, there, here = src.at[rows], land.at[me, rows], land.at[peer, rows]
            else:
                mine, there, here = src, land.at[me], land.at[peer]
            send = functools.partial(_remote, mine, there, send_sems, recv_sems, 3 * i + j, (px, py, c))
            arrival = functools.partial(_remote, mine, here, send_sems, recv_sems, 3 * i + j, (px, py, c))
            pairs.append((send, arrival))
    return pairs


def _share_halves(name, lands):
    idx = [i for i, a in enumerate(lands) if _splits(a.shape[1])]
    n = len(idx)

    def body(*refs):
        in_refs, out_refs = refs[:n], refs[n:2 * n]
        send_sems, recv_sems = refs[2 * n:]
        x, y, c, chips = _place()
        cps = []
        for i, (src, dst) in enumerate(zip(in_refs, out_refs, strict=True)):
            for j, (px, py) in enumerate(chips):
                mine = _half_rows(src.shape[1], c, True)
                cp = _remote(src.at[2 * px + py, mine], dst.at[2 * px + py, mine], send_sems, recv_sems, 3 * i + j,
                             (x, y, 1 - c))
                cp.start()
                cps.append(cp)
        for i, dst in enumerate(out_refs):
            for j, (px, py) in enumerate(chips):
                theirs = dst.at[2 * px + py, _half_rows(dst.shape[1], c, False)]
                _remote(theirs, theirs, send_sems, recv_sems, 3 * i + j, (x, y, 1 - c)).wait_recv()
        for cp in cps:
            cp.wait_send()

    outs = pl.pallas_call(
        body, name=name, in_specs=[HBM_SPEC] * n, out_specs=[HBM_SPEC] * n,
        out_shape=[jax.ShapeDtypeStruct(lands[i].shape, lands[i].dtype) for i in idx],
        input_output_aliases={i: i for i in range(n)},
        scratch_shapes=[pltpu.SemaphoreType.DMA((3 * n,)), pltpu.SemaphoreType.DMA((3 * n,))],
    )(*[lands[i] for i in idx])
    done = list(lands)
    for i, o in zip(idx, outs, strict=True):
        done[i] = o
    return done


def _travel_start(name, mode, srcs):
    n = len(srcs)
    lands = [lax.empty((N_CHIPS,) + (s.shape if mode == "gather" else s.shape[1:]), s.dtype) for s in srcs]

    def body(*refs):
        src_refs, land_refs = refs[:n], refs[n:2 * n]
        send_sems, recv_sems = refs[2 * n], refs[2 * n + 1]
        token = refs[-1]
        for send, _ in _travel_copies(mode, src_refs, land_refs, send_sems, recv_sems):
            send().start()
        token[...] = jnp.zeros_like(token)

    hbm = lambda a: pltpu.HBM(a.shape, a.dtype)
    outs = pl.pallas_call(
        body, name=name,
        out_shape=(pltpu.SemaphoreType.DMA((3 * n,)), pltpu.SemaphoreType.DMA((3 * n,)), *[hbm(s) for s in srcs],
                   *[hbm(a) for a in lands], jax.ShapeDtypeStruct((SUBLANES, LANES), F32)),
        in_specs=[HBM_SPEC] * (2 * n),
        out_specs=(SEM_SPEC, SEM_SPEC, *[HBM_SPEC] * (2 * n), pl.BlockSpec(memory_space=pltpu.VMEM)),
        input_output_aliases={i: 2 + i for i in range(2 * n)},
        compiler_params=pltpu.CompilerParams(has_side_effects=DATAFLOW),
    )(*[pltpu.with_memory_space_constraint(a, pltpu.HBM) for a in list(srcs) + lands])
    return outs[0], outs[1], list(outs[2:2 + n]), list(outs[2 + n:2 + 2 * n]), outs[-1]


def _travel_wait(name, mode, send_sems, recv_sems, srcs, lands, after):
    n = len(srcs)

    def body(*refs):
        src_refs, land_refs = refs[:n], refs[n:2 * n]
        send_sems_, recv_sems_ = refs[2 * n], refs[2 * n + 1]
        for send, arrival in _travel_copies(mode, src_refs, land_refs, send_sems_, recv_sems_):
            send().wait_send()
            arrival().wait_recv()

    hbm = lambda a: pltpu.HBM(a.shape, a.dtype)
    outs = pl.pallas_call(
        body, name=name, out_shape=tuple(hbm(a) for a in list(srcs) + list(lands)),
        in_specs=[HBM_SPEC] * (2 * n) + [SEM_SPEC, SEM_SPEC, pl.BlockSpec(memory_space=pl.ANY)],
        out_specs=tuple([HBM_SPEC] * (2 * n)), input_output_aliases={i: i for i in range(2 * n)},
        compiler_params=pltpu.CompilerParams(has_side_effects=DATAFLOW),
    )(*srcs, *lands, send_sems, recv_sems, after)
    me = 2 * lax.axis_index("x") + lax.axis_index("y")
    own = [s[None] if mode == "gather" else lax.dynamic_slice_in_dim(s, me, 1, axis=0) for s in outs[:n]]
    return [lax.dynamic_update_slice(a, o, (me,) + (0,) * (a.ndim - 1)) for a, o in zip(outs[n:], own, strict=True)]


SUM_TILE_BYTES = 4 * 1024 * 1024


def _sum_rows(half, cols):
    best = ROW_ALIGN
    for t in range(ROW_ALIGN, half + 1, ROW_ALIGN):
        if half % t == 0 and N_CHIPS * t * cols * 4 <= SUM_TILE_BYTES:
            best = t
    return best


def _sum_cores(name, g, theirs, core):
    _, R, C = g.shape
    half = R // 2
    tr = _sum_rows(half, C)
    nb = half // tr

    def body(core_ref, g_ref, t_ref, o_ref):
        o_ref[...] = (g_ref[...] + t_ref[...]).astype(o_ref.dtype)

    grid_spec = pltpu.PrefetchScalarGridSpec(
        num_scalar_prefetch=1, grid=(nb,),
        in_specs=[pl.BlockSpec((N_CHIPS, tr, C), lambda i, core_ref: (0, core_ref[0] * nb + i, 0)),
                  pl.BlockSpec((N_CHIPS, tr, C), lambda i, core_ref: (0, i, 0))],
        out_specs=pl.BlockSpec((N_CHIPS, tr, C), lambda i, core_ref: (0, i, 0)))
    return pl.pallas_call(
        body, name=name, grid_spec=grid_spec, out_shape=jax.ShapeDtypeStruct((N_CHIPS, half, C), BF16),
        compiler_params=_params(("parallel",)),
    )(core, g, theirs)


def _sum_chips(name, parts):
    _, H, C = parts.shape
    tr = _sum_rows(H, C)

    def body(p_ref, o_ref):
        acc = p_ref[0].astype(F32)
        for k in range(1, N_CHIPS):
            acc = acc + p_ref[k].astype(F32)
        o_ref[...] = acc

    return pl.pallas_call(
        body, name=name, grid=(H // tr,),
        in_specs=[pl.BlockSpec((N_CHIPS, tr, C), lambda i: (0, i, 0))],
        out_specs=pl.BlockSpec((tr, C), lambda i: (i, 0)),
        out_shape=jax.ShapeDtypeStruct((H, C), F32),
        compiler_params=_params(("parallel",)),
    )(parts)


def _adamw_math(w, g, m, v):
    m = ADAM_B1 * m + (1.0 - ADAM_B1) * g
    v = ADAM_B2 * v + (1.0 - ADAM_B2) * (g * g)
    m_hat = m / (1.0 - ADAM_B1 ** ADAM_STEP)
    v_hat = v / (1.0 - ADAM_B2 ** ADAM_STEP)
    delta = -ADAM_LR * (m_hat / (jnp.sqrt(v_hat) + ADAM_EPS) + ADAM_WD * w)
    return delta, m, v


def _adamw(name, w, g, m, v):
    R, C = w.shape
    tr = R
    if R % SUBLANES == 0:
        for cand in range(SUBLANES, min(R, 256) + 1, SUBLANES):
            if R % cand == 0:
                tr = cand

    def body(w_ref, g_ref, m_ref, v_ref, d_ref, nm_ref, nv_ref):
        d, nm, nv = _adamw_math(w_ref[...], g_ref[...], m_ref[...], v_ref[...])
        d_ref[...] = d
        nm_ref[...] = nm
        nv_ref[...] = nv

    spec = pl.BlockSpec((tr, C), lambda i: (i, 0))
    shape = jax.ShapeDtypeStruct((R, C), F32)
    return pl.pallas_call(
        body, name=name, grid=(R // tr,), in_specs=[spec] * 4, out_specs=[spec] * 3, out_shape=[shape] * 3,
        compiler_params=_params(("parallel",)),
    )(w, g, m, v)


def _adamw_small(parts, w, m, v):
    n = parts.shape[0]

    def body(p_ref, w_ref, m_ref, v_ref, g_ref, d_ref, nm_ref, nv_ref):
        g = p_ref[0]
        for k in range(1, n):
            g = g + p_ref[k]
        d, nm, nv = _adamw_math(w_ref[...], g, m_ref[...], v_ref[...])
        g_ref[...] = g
        d_ref[...] = d
        nm_ref[...] = nm
        nv_ref[...] = nv

    shape = jax.ShapeDtypeStruct(w.shape, F32)
    return pl.pallas_call(body, name="adamw_small", out_shape=[shape] * 4, compiler_params=_params())(parts, w, m, v)


WEIGHTS = ['g_mix', 'w_in', 'rw_mu', 'rw_w0', 'rw_w_up', 'rw_a0', 'rw_a_up', 'rw_g_up', 'rw_k_k', 'rw_k_a',
           'rw_r_k', 'rw_ln_g', 'rw_ln_b', 'w_branch_a', 'w_branch_b', 'w_gate', 'b_gate', 'w_out', 'g_ffn', 'w_up',
           'conv_w', 'conv_b', 'w_down', 'g_ple', 'w_ple_gate', 'w_ple', 'g_final']
ARG_NAMES = (['x', 'p'] + WEIGHTS + ['loss_target'] + ['m_' + n for n in WEIGHTS] + ['v_' + n for n in WEIGHTS])
SHARDED = {'w_in': 1, 'rw_w_up': 1, 'rw_a_up': 1, 'rw_g_up': 1, 'w_branch_a': 1, 'w_branch_b': 1, 'w_gate': 1,
           'w_out': 0, 'w_up': 1, 'conv_w': 1, 'w_down': 0, 'w_ple_gate': 0, 'w_ple': 1}
SMALL = [n for n in WEIGHTS if n not in SHARDED]
WHOLE = ['conv_w']
FIRST_USED = ['w_in', 'rw_w_up', 'rw_a_up', 'rw_g_up', 'w_gate']
READ_BY_CHIP = ['w_gate', 'w_branch_a', 'w_branch_b', 'w_up', 'w_ple']
FIRST_DONE = [['w_up', 'w_down', 'w_ple_gate', 'w_ple'], ['w_out', 'w_branch_a', 'w_branch_b', 'w_gate'],
              ['w_in', 'rw_w_up', 'rw_a_up', 'rw_g_up']]
SPLIT = [n for n in SHARDED if n not in WHOLE]
PACK_ALIGN = SUBLANES * LANES


def _pack_rows(flat_parts):
    flat = jnp.concatenate(flat_parts, axis=1)
    n = flat.shape[1]
    padded = -(-n // PACK_ALIGN) * PACK_ALIGN
    flat = jnp.pad(flat, ((0, 0), (0, padded - n)))
    return flat.reshape(padded // LANES, LANES)


def _full_from_shards(stack, axis):
    _, R, C = stack.shape
    if axis == 0:
        return stack.reshape(N_CHIPS * R, C)
    return stack.transpose(1, 0, 2).reshape(R, N_CHIPS * C)


def _shards_from_full(full, axis):
    R, C = full.shape
    if axis == 0:
        return full.reshape(N_CHIPS, R // N_CHIPS, C)
    return full.reshape(R, N_CHIPS, C // N_CHIPS).transpose(1, 0, 2)


def kernel(x, p, g_mix, w_in, rw_mu, rw_w0, rw_w_up, rw_a0, rw_a_up, rw_g_up, rw_k_k, rw_k_a, rw_r_k, rw_ln_g, rw_ln_b, w_branch_a, w_branch_b, w_gate, b_gate, w_out, g_ffn, w_up, conv_w, conv_b, w_down, g_ple, w_ple_gate, w_ple, g_final, loss_target, m_g_mix, m_w_in, m_rw_mu, m_rw_w0, m_rw_w_up, m_rw_a0, m_rw_a_up, m_rw_g_up, m_rw_k_k, m_rw_k_a, m_rw_r_k, m_rw_ln_g, m_rw_ln_b, m_w_branch_a, m_w_branch_b, m_w_gate, m_b_gate, m_w_out, m_g_ffn, m_w_up, m_conv_w, m_conv_b, m_w_down, m_g_ple, m_w_ple_gate, m_w_ple, m_g_final, v_g_mix, v_w_in, v_rw_mu, v_rw_w0, v_rw_w_up, v_rw_a0, v_rw_a_up, v_rw_g_up, v_rw_k_k, v_rw_k_a, v_rw_r_k, v_rw_ln_g, v_rw_ln_b, v_w_branch_a, v_w_branch_b, v_w_gate, v_b_gate, v_w_out, v_g_ffn, v_w_up, v_conv_w, v_conv_b, v_w_down, v_g_ple, v_w_ple_gate, v_w_ple, v_g_final):
    given = dict(zip(ARG_NAMES, (x, p, g_mix, w_in, rw_mu, rw_w0, rw_w_up, rw_a0, rw_a_up, rw_g_up, rw_k_k, rw_k_a, rw_r_k, rw_ln_g, rw_ln_b, w_branch_a, w_branch_b, w_gate, b_gate, w_out, g_ffn, w_up, conv_w, conv_b, w_down, g_ple, w_ple_gate, w_ple, g_final, loss_target, m_g_mix, m_w_in, m_rw_mu, m_rw_w0, m_rw_w_up, m_rw_a0, m_rw_a_up, m_rw_g_up, m_rw_k_k, m_rw_k_a, m_rw_r_k, m_rw_ln_g, m_rw_ln_b, m_w_branch_a, m_w_branch_b, m_w_gate, m_b_gate, m_w_out, m_g_ffn, m_w_up, m_conv_w, m_conv_b, m_w_down, m_g_ple, m_w_ple_gate, m_w_ple, m_g_final, v_g_mix, v_w_in, v_rw_mu, v_rw_w0, v_rw_w_up, v_rw_a0, v_rw_a_up, v_rw_g_up, v_rw_k_k, v_rw_k_a, v_rw_r_k, v_rw_ln_g, v_rw_ln_b, v_w_branch_a, v_w_branch_b, v_w_gate, v_b_gate, v_w_out, v_g_ffn, v_w_up, v_conv_w, v_conv_b, v_w_down, v_g_ple, v_w_ple_gate, v_w_ple, v_g_final), strict=True))

    def two_d(name, prefix=""):
        a = given[prefix + name]
        if name == "g_final":
            return a.reshape(1, D_MODEL)
        if name == "rw_r_k":
            return a.reshape(1, RW_WIDTH)
        return a[0] if a.ndim == 3 else a

    cast = lambda n: two_d(n) if n in WHOLE else two_d(n).astype(BF16)
    whole = lambda names, stacks: {n: g if n in READ_BY_CHIP else _full_from_shards(g, SHARDED[n])
                                   for n, g in zip(names, stacks, strict=True)}
    late_names = [n for n in SHARDED if n not in FIRST_USED]
    late_sends, late_recvs, late_srcs, late_lands, token = _travel_start(
        "gather_late_start", "gather", [cast(n) for n in late_names])
    W = whole(FIRST_USED, _gather_chips([cast(n) for n in FIRST_USED]))
    for n in SMALL:
        W[n] = two_d(n)
    W["rw_r_k"] = W["rw_r_k"].reshape(RW_HEADS, RW_HEAD_DIM)
    W["g_mix"] = W["g_mix"] + token[0:1, 0:1]

    def late_weights(after):
        lands = _travel_wait("gather_late_wait", "gather", late_sends, late_recvs, late_srcs, late_lands, after)
        return whole(late_names, _share_halves("share_late", lands))

    core = lax.axis_index("c").astype(jnp.int32).reshape(1)
    early_names = [[n for n in SPLIT if n in group] for group in FIRST_DONE]
    rest_names = [n for n in SPLIT if not any(n in group for group in FIRST_DONE)]
    travelling = []

    def core_sums(tag, names, G):
        by_chip = [G[n] if n in READ_BY_CHIP else _shards_from_full(G[n], SHARDED[n]) for n in names]
        theirs = _swap_halves("swap_halves_" + tag, by_chip)
        return [_sum_cores("sum_cores_" + n, g, t, core) for n, g, t in zip(names, by_chip, theirs, strict=True)]

    def early_grads(G, stage):
        sends, recvs, srcs, lands, started = _travel_start(f"scatter_early{stage}_start", "scatter",
                                                           core_sums(f"early{stage}", early_names[stage], G))
        travelling.append((sends, recvs, srcs, lands))
        return started

    loss_part, grad_x, G = _local_step(x[0], p[0, 0], W, loss_target[0], late_weights, early_grads, by_chip=True)

    assert not rest_names, rest_names
    landed = {}
    for stage, (sends, recvs, srcs, lands) in enumerate(travelling):
        landed.update(zip(early_names[stage], _travel_wait(f"scatter_early{stage}_wait", "scatter", sends, recvs,
                                                           srcs, lands, grad_x), strict=True))
    reduced = [_sum_chips("sum_chips_" + n, landed[n]) for n in SPLIT]
    shard_grads = dict(zip(SPLIT, _join_halves(reduced), strict=True))

    small_sizes = {n: two_d(n).shape[1] for n in SMALL}
    n_small = sum(small_sizes.values())
    whole_sizes = {n: G[n].shape[0] * G[n].shape[1] for n in WHOLE}
    n_whole = sum(whole_sizes.values())

    def pack_small(parts, rest):
        return _pack_rows([a.reshape(1, -1) for a in parts] + [rest])

    G["rw_r_k"] = G["rw_r_k"].reshape(1, RW_WIDTH)
    rest = jnp.zeros((1, n_whole + 1), F32)
    all_small = _gather_all(pack_small([G[n] for n in SMALL] + [G[n] for n in WHOLE], loss_part))
    gs, ds, nms, nvs = _adamw_small(all_small, pack_small([two_d(n) for n in SMALL], rest),
                                    pack_small([two_d(n, "m_") for n in SMALL], rest),
                                    pack_small([two_d(n, "v_") for n in SMALL], rest))
    gs, ds, nms, nvs = (a.reshape(-1) for a in (gs, ds, nms, nvs))
    loss = gs[n_small + n_whole]
    chip = 2 * lax.axis_index("x") + lax.axis_index("y")
    off = n_small
    for n in WHOLE:
        full = gs[off:off + whole_sizes[n]].reshape(G[n].shape)
        off += whole_sizes[n]
        width = two_d(n).shape[1]
        shard_grads[n] = lax.dynamic_slice_in_dim(full, chip * width, width, axis=1)

    grads, deltas, new_m, new_v = {}, {}, {}, {}
    for n in SHARDED:
        g = shard_grads[n]
        d, nm, nv = _adamw("adamw_" + n, two_d(n), g, two_d(n, "m_"), two_d(n, "v_"))
        grads[n], deltas[n], new_m[n], new_v[n] = g, d, nm, nv
    off = 0
    for n in SMALL:
        sl = slice(off, off + small_sizes[n])
        off += small_sizes[n]
        grads[n], deltas[n], new_m[n], new_v[n] = gs[sl], ds[sl], nms[sl], nvs[sl]
    outs = [loss, grad_x[None]]
    for table in (grads, deltas, new_m, new_v):
        outs += [table[n].reshape(given[n].shape) for n in WEIGHTS]
    return tuple(outs)
```

```python
import functools
import math

import jax
import jax.numpy as jnp
import numpy as np
from jax import lax
from jax.experimental import pallas as pl
from jax.experimental.pallas import tpu as pltpu

F32 = jnp.float32
BF16 = jnp.bfloat16

D_MODEL = 1024
NORM_EPS = 1e-6
RW_HEADS = 8
RW_HEAD_DIM = 64
RW_WIDTH = 512
RW_LN_EPS = 64e-5
ATT_GROUP_DILATION = (1, 4, 16)
ATT_BLOCK = 128
ATT_HEADS = 12
ATT_HEAD_DIM = 64
ATT_GROUP_WIDTH = 256
ATT_WIDTH = 768
D_FF = 3072

ADAM_LR = 0.001
ADAM_B1 = 0.9
ADAM_B2 = 0.999
ADAM_EPS = 1e-08
ADAM_WD = 0.01
ADAM_STEP = 10

SUBLANES = 8
LANES = 128
VMEM_LIMIT = 56 * 1024 * 1024
N_CHIPS = 4
N_DEV = 8
MESH = pl.DeviceIdType.MESH


def _params(sem=None):
    return pltpu.CompilerParams(dimension_semantics=sem, vmem_limit_bytes=VMEM_LIMIT)


def _pick(dim, pref):
    if dim % LANES != 0 or dim <= pref:
        return dim
    best = LANES
    for t in range(LANES, pref + 1, LANES):
        if dim % t == 0:
            best = t
    return best


def _mm(name, a, b, mode, out_dtype=F32, add=None, tm=1024, tn=1024, tk=1024, out_by_chip=False):
    by_chip = b.ndim == 3
    b_rows, b_cols = (b.shape[1], N_CHIPS * b.shape[2]) if by_chip else b.shape
    if mode == "nn":
        (M, K), (K2, N) = a.shape, (b_rows, b_cols)
    elif mode == "nt":
        (M, K), (N, K2) = a.shape, (b_rows, b_cols)
    else:
        (K, M), (K2, N) = a.shape, (b_rows, b_cols)
    assert K == K2, (name, a.shape, b.shape, mode)
    assert not (by_chip and mode == "tn") and not (out_by_chip and add is not None), name
    tm = _pick(M, tm)
    n_cut, k_cut = out_by_chip or (by_chip and mode == "nn"), by_chip and mode == "nt"
    tn = _pick(N // N_CHIPS, tn) if n_cut else _pick(N, tn)
    tk = _pick(K // N_CHIPS, tk) if k_cut else _pick(K, tk)
    nk = K // tk
    per_n = (N // N_CHIPS) // tn if n_cut else 1
    per_k = (K // N_CHIPS) // tk if k_cut else 1
    if mode == "nn":
        a_spec = pl.BlockSpec((tm, tk), lambda i, j, k: (i, k))
        b_spec = (pl.BlockSpec((None, tk, tn), lambda i, j, k: (j // per_n, k, j % per_n)) if by_chip
                  else pl.BlockSpec((tk, tn), lambda i, j, k: (k, j)))
        dims = (((1,), (0,)), ((), ()))
    elif mode == "nt":
        a_spec = pl.BlockSpec((tm, tk), lambda i, j, k: (i, k))
        b_spec = (pl.BlockSpec((None, tn, tk), lambda i, j, k: (k // per_k, j, k % per_k)) if by_chip
                  else pl.BlockSpec((tn, tk), lambda i, j, k: (j, k)))
        dims = (((1,), (1,)), ((), ()))
    else:
        a_spec = pl.BlockSpec((tk, tm), lambda i, j, k: (k, i))
        b_spec = pl.BlockSpec((tk, tn), lambda i, j, k: (k, j))
        dims = (((0,), (0,)), ((), ()))
    if out_by_chip:
        o_spec = pl.BlockSpec((None, tm, tn), lambda i, j, k: (j // per_n, i, j % per_n))
        out_shape = jax.ShapeDtypeStruct((N_CHIPS, M, N // N_CHIPS), out_dtype)
    else:
        o_spec = pl.BlockSpec((tm, tn), lambda i, j, k: (i, j))
        out_shape = jax.ShapeDtypeStruct((M, N), out_dtype)
    has_add = add is not None

    def body(*refs):
        if has_add:
            a_ref, b_ref, add_ref, o_ref, acc_ref = refs
        else:
            a_ref, b_ref, o_ref, acc_ref = refs
        k = pl.program_id(2)
        part = lax.dot_general(a_ref[...].astype(BF16), b_ref[...].astype(BF16), dims,
                               preferred_element_type=F32)

        @pl.when(k == 0)
        def _():
            acc_ref[...] = part

        @pl.when(k > 0)
        def _():
            acc_ref[...] += part

        @pl.when(k == nk - 1)
        def _():
            res = acc_ref[...]
            if has_add:
                res = res + add_ref[...].astype(F32)
            o_ref[...] = res.astype(o_ref.dtype)

    ins = [a, b] + ([add] if has_add else [])
    in_specs = [a_spec, b_spec] + ([o_spec] if has_add else [])
    return pl.pallas_call(
        body, name=name, grid=(M // tm, N // tn, nk),
        in_specs=in_specs, out_specs=o_spec, out_shape=out_shape,
        scratch_shapes=[pltpu.VMEM((tm, tn), F32)],
        compiler_params=_params(("parallel", "parallel", "arbitrary")),
    )(*ins)


def _rowwise(name, fn, T, tT, rows=(), prevs=(), nexts=(), consts=(), outs=()):
    n = T // tT
    per8 = tT // SUBLANES
    in_specs, ins = [], []
    for arr in rows:
        in_specs.append(pl.BlockSpec((tT, arr.shape[1]), lambda i: (i, 0)))
        ins.append(arr)
    for arr in prevs:
        in_specs.append(pl.BlockSpec((SUBLANES, arr.shape[1]), lambda i: (jnp.maximum(i * per8 - 1, 0), 0)))
        ins.append(arr)
    for arr in nexts:
        in_specs.append(pl.BlockSpec((SUBLANES, arr.shape[1]),
                                     lambda i: (jnp.minimum((i + 1) * per8, T // SUBLANES - 1), 0)))
        ins.append(arr)
    for arr in consts:
        in_specs.append(pl.BlockSpec(arr.shape, lambda i, nd=arr.ndim: (0,) * nd))
        ins.append(arr)
    out_specs, out_shapes = [], []
    for o in outs:
        if o[0] == "row":
            out_specs.append(pl.BlockSpec((tT, o[1]), lambda i: (i, 0)))
            out_shapes.append(jax.ShapeDtypeStruct((T, o[1]), o[2]))
        else:
            out_specs.append(pl.BlockSpec(o[1], lambda i: (0, 0)))
            out_shapes.append(jax.ShapeDtypeStruct(o[1], F32))
    nr, npv, nnx, nc = len(rows), len(prevs), len(nexts), len(consts)
    n_in = nr + npv + nnx + nc

    def body(*refs):
        i = pl.program_id(0)
        vals = [r[...] for r in refs[:n_in]]
        res = fn(i, n, vals[:nr], vals[nr:nr + npv], vals[nr + npv:nr + npv + nnx], vals[nr + npv + nnx:])
        for o, o_ref, val in zip(outs, refs[n_in:], res, strict=True):
            if o[0] == "row":
                o_ref[...] = val.astype(o_ref.dtype)
            else:
                @pl.when(i == 0)
                def _(o_ref=o_ref, val=val):
                    o_ref[...] = val.astype(F32)

                @pl.when(i > 0)
                def _(o_ref=o_ref, val=val):
                    o_ref[...] += val.astype(F32)

    res = pl.pallas_call(
        body, name=name, grid=(n,), in_specs=in_specs, out_specs=out_specs, out_shape=out_shapes,
        compiler_params=_params(("arbitrary",)),
    )(*ins)
    return list(res)


def _shift_down(x, prev8, i, s):
    rolled = pltpu.roll(x, s, 0)
    head = pltpu.roll(prev8, s, 0)
    head = jnp.where(i == 0, jnp.zeros_like(head), head)
    rid = lax.broadcasted_iota(jnp.int32, head.shape, 0)
    first = jnp.where(rid < s, head, rolled[:SUBLANES])
    if x.shape[0] == SUBLANES:
        return first
    return jnp.concatenate([first, rolled[SUBLANES:]], axis=0)


def _shift_up(x, next8, i, n, s):
    tT = x.shape[0]
    rolled = pltpu.roll(x, tT - s, 0)
    tail = pltpu.roll(next8, SUBLANES - s, 0)
    tail = jnp.where(i == n - 1, jnp.zeros_like(tail), tail)
    rid = lax.broadcasted_iota(jnp.int32, tail.shape, 0)
    last = jnp.where(rid >= SUBLANES - s, tail, rolled[tT - SUBLANES:])
    return jnp.concatenate([rolled[:tT - SUBLANES], last], axis=0)


def _colsum(x):
    return jnp.sum(x, axis=0, keepdims=True)


def _segsum(x, bd):
    return jnp.dot(x, bd, precision=lax.Precision.HIGH, preferred_element_type=F32)


def _block_diag_ones(width, seg):
    idx = np.arange(width) // seg
    return jnp.asarray((idx[:, None] == idx[None, :]).astype(np.float32))


def _sigmoid(z):
    return 1.0 / (1.0 + jnp.exp(-z))


def _softplus(z):
    return jnp.maximum(z, 0.0) + jnp.log(1.0 + jnp.exp(-jnp.abs(z)))


def _rms_fwd(x, g):
    r = lax.rsqrt(jnp.mean(x * x, axis=-1, keepdims=True) + NORM_EPS)
    return x * r * g


def _rms_bwd(x, g, dy):
    r = lax.rsqrt(jnp.mean(x * x, axis=-1, keepdims=True) + NORM_EPS)
    gdy = dy * g
    dx = r * (gdy - x * (r * r) * jnp.mean(x * gdy, axis=-1, keepdims=True))
    return dx, dy * x * r


GELU_C = math.sqrt(2.0 / math.pi)


def _gelu(x):
    return 0.5 * x * (1.0 + jnp.tanh(GELU_C * (x + 0.044715 * x * x * x)))


def _gelu_and_grad(x):
    th = jnp.tanh(GELU_C * (x + 0.044715 * x * x * x))
    half = 0.5 * (1.0 + th)
    return x * half, half + 0.5 * x * (1.0 - th * th) * GELU_C * (1.0 + 3.0 * 0.044715 * x * x)


RW_CHUNK = 64
NN = (((1,), (0,)), ((), ()))
NT = (((1,), (1,)), ((), ()))
TN = (((0,), (0,)), ((), ()))


def _hdot(a, b, dims):
    return lax.dot_general(a, b, dims, precision=lax.Precision.HIGH, preferred_element_type=F32)


def _ldot(a, b, dims):
    return lax.dot_general(a.astype(BF16), b.astype(BF16), dims, preferred_element_type=F32)


def _chunk_masks():
    ti = lax.broadcasted_iota(jnp.int32, (RW_CHUNK, RW_CHUNK), 0)
    tj = lax.broadcasted_iota(jnp.int32, (RW_CHUNK, RW_CHUNK), 1)
    return tj <= ti, tj < ti, (ti == tj).astype(F32)


def _head(x, h):
    return x[:, h * RW_HEAD_DIM:(h + 1) * RW_HEAD_DIM]


def _heads(fn):
    return [fn(h) for h in range(RW_HEADS)]


def _chunk_rows(r, lw, k, a, b, incl_f):
    c = _hdot(incl_f, lw, NN)
    e_prev, e_neg, e_pos = jnp.exp(c - lw), jnp.exp(-c), jnp.exp(c)
    return dict(At=a * e_prev, Bt=b * e_neg, Kt=k * e_neg, Rt=r * e_pos, e_prev=e_prev, e_neg=e_neg, e_pos=e_pos)


def _chunk_coeffs(q, incl, strict):
    A1 = _heads(lambda h: jnp.where(strict, _hdot(_head(q["At"], h), _head(q["Bt"], h), NT), 0.0))
    A2 = _heads(lambda h: jnp.where(strict, _hdot(_head(q["At"], h), _head(q["Kt"], h), NT), 0.0))
    W1 = _heads(lambda h: jnp.where(incl, _hdot(_head(q["Rt"], h), _head(q["Bt"], h), NT), 0.0))
    W2 = _heads(lambda h: jnp.where(incl, _ldot(_head(q["Rt"], h), _head(q["Kt"], h), NT), 0.0))
    return A1, A2, W1, W2


def _rwkv_chunk_prep(r, lw, k, a, b, v):
    T = r.shape[0]
    nC = T // RW_CHUNK
    H, N = RW_HEADS, RW_HEAD_DIM

    def body(r_ref, lw_ref, k_ref, a_ref, b_ref, v_ref,
             at_ref, bt_ref, kt_ref, rt_ref, a2v_ref, w2v_ref, ti_ref, w1_ref, a2_ref, w2_ref, pl_ref):
        incl, strict, eye = _chunk_masks()
        q = _chunk_rows(r_ref[...], lw_ref[...], k_ref[...], a_ref[...], b_ref[...], incl.astype(F32))
        at_ref[...], bt_ref[...], kt_ref[...], rt_ref[...] = q["At"], q["Bt"], q["Kt"], q["Rt"]
        pl_ref[0] = jnp.broadcast_to(q["e_pos"][RW_CHUNK - 1:RW_CHUNK, :], (SUBLANES, RW_WIDTH))
        A1, A2, W1, W2 = _chunk_coeffs(q, incl, strict)
        V = v_ref[...]
        a2v_ref[...] = jnp.concatenate(_heads(lambda h: _hdot(A2[h], _head(V, h), NN)), axis=1)
        w2v_ref[...] = jnp.concatenate(_heads(lambda h: _ldot(W2[h], _head(V, h), NN)), axis=1)
        tinv, pw = [eye + m for m in A1], A1
        for stage in range(5):
            dot = _hdot if stage == 0 else _ldot
            pw = [dot(m, m, NN) for m in pw]
            tinv = [t + dot(t, m, NN) for t, m in zip(tinv, pw, strict=True)]
        for h in range(H):
            ti_ref[0, h] = tinv[h]
            w1_ref[0, h] = W1[h]
            a2_ref[0, h] = A2[h]
            w2_ref[0, h] = W2[h]

    row_spec = pl.BlockSpec((RW_CHUNK, RW_WIDTH), lambda n: (n, 0))
    st_spec = pl.BlockSpec((1, H, N, N), lambda n: (n, 0, 0, 0))
    row_shape = jax.ShapeDtypeStruct((T, RW_WIDTH), F32)
    st_shape = jax.ShapeDtypeStruct((nC, H, N, N), F32)
    return pl.pallas_call(
        body, name="rwkv_chunk_prep", grid=(nC,),
        in_specs=[row_spec] * 6,
        out_specs=[row_spec] * 6 + [st_spec] * 4 + [pl.BlockSpec((1, SUBLANES, RW_WIDTH), lambda n: (n, 0, 0))],
        out_shape=[row_shape] * 6 + [st_shape] * 4 + [jax.ShapeDtypeStruct((nC, SUBLANES, RW_WIDTH), F32)],
        compiler_params=_params(("parallel",)),
    )(r, lw, k, a, b, v)


def _rwkv_chunk_fwd(v, at, bt, kt, rt, a2v, w2v, tinv, w1, plast):
    T = v.shape[0]
    nC = T // RW_CHUNK
    H, N = RW_HEADS, RW_HEAD_DIM

    def body(v_ref, at_ref, bt_ref, kt_ref, rt_ref, a2v_ref, w2v_ref, ti_ref, w1_ref, pl_ref,
             y_ref, sa_ref, s0_ref, S_ref):
        @pl.when(pl.program_id(0) == 0)
        def _():
            S_ref[...] = jnp.zeros_like(S_ref)

        V, At, Bt, Kt, Rt = v_ref[...], at_ref[...], bt_ref[...], kt_ref[...], rt_ref[...]
        A2V, W2V, p_last = a2v_ref[...], w2v_ref[...], pl_ref[0, 0:1, :]
        S0 = _heads(lambda h: S_ref[h])
        for h in range(H):
            s0_ref[0, h] = S0[h]
        Z = _heads(lambda h: _hdot(_head(At, h), S0[h], NT) + _head(A2V, h))
        Sa = _heads(lambda h: _hdot(ti_ref[0, h], Z[h], NN))
        X = _heads(lambda h: S0[h] + _hdot(Sa[h], _head(Bt, h), TN) + _hdot(_head(V, h), _head(Kt, h), TN))
        for h in range(H):
            S_ref[h] = X[h] * _head(p_last, h)
        Y = _heads(lambda h: _ldot(_head(Rt, h), S0[h], NT) + _ldot(w1_ref[0, h], Sa[h], NN) + _head(W2V, h))
        y_ref[...] = jnp.concatenate(Y, axis=1)
        sa_ref[...] = jnp.concatenate(Sa, axis=1)

    row_spec = pl.BlockSpec((RW_CHUNK, RW_WIDTH), lambda n: (n, 0))
    st_spec = pl.BlockSpec((1, H, N, N), lambda n: (n, 0, 0, 0))
    row_shape = jax.ShapeDtypeStruct((T, RW_WIDTH), F32)
    return pl.pallas_call(
        body, name="rwkv_chunk_fwd", grid=(nC,),
        in_specs=[row_spec] * 7 + [st_spec, st_spec, pl.BlockSpec((1, SUBLANES, RW_WIDTH), lambda n: (n, 0, 0))],
        out_specs=[row_spec, row_spec, st_spec],
        out_shape=[row_shape, row_shape, jax.ShapeDtypeStruct((nC, H, N, N), F32)],
        scratch_shapes=[pltpu.VMEM((H, N, N), F32)],
        compiler_params=_params(("arbitrary",)),
    )(v, at, bt, kt, rt, a2v, w2v, tinv, w1, plast)


def _rwkv_chunk_bwd(r, lw, k, a, b, v, dy, s0, tinv, w1, a2, w2, sa):
    T = r.shape[0]
    nC = T // RW_CHUNK
    H, N = RW_HEADS, RW_HEAD_DIM

    def body(r_ref, lw_ref, k_ref, a_ref, b_ref, v_ref, dy_ref, s0_ref, ti_ref, w1_ref, a2_ref, w2_ref, sa_ref,
             dr_ref, dlw_ref, dk_ref, da_ref, db_ref, dv_ref, dS_ref):
        @pl.when(pl.program_id(0) == 0)
        def _():
            dS_ref[...] = jnp.zeros_like(dS_ref)

        incl, strict, _ = _chunk_masks()
        incl_f = incl.astype(F32)
        q = _chunk_rows(r_ref[...], lw_ref[...], k_ref[...], a_ref[...], b_ref[...], incl_f)
        At, Bt, Kt, Rt = q["At"], q["Bt"], q["Kt"], q["Rt"]
        A2, W1, W2 = (_heads(lambda h, ref=ref: ref[0, h]) for ref in (a2_ref, w1_ref, w2_ref))
        V, dY, Sa = v_ref[...], dy_ref[...], sa_ref[...]
        hd = _head
        p_last = q["e_pos"][RW_CHUNK - 1:RW_CHUNK, :]
        S0 = _heads(lambda h: s0_ref[0, h])
        G = _heads(lambda h: dS_ref[h] * hd(p_last, h))
        X = _heads(lambda h: S0[h] + _hdot(hd(Sa, h), hd(Bt, h), TN) + _hdot(hd(V, h), hd(Kt, h), TN))
        dc_last = jnp.concatenate(_heads(lambda h: jnp.sum(G[h] * X[h], axis=0, keepdims=True)), axis=1)
        dSa = _heads(lambda h: _hdot(hd(Bt, h), G[h], NT) + _hdot(W1[h], hd(dY, h), TN))
        dZ = _heads(lambda h: _hdot(ti_ref[0, h], dSa[h], TN))
        for h in range(H):
            dS_ref[h] = G[h] + _hdot(dZ[h], hd(At, h), TN) + _hdot(hd(dY, h), hd(Rt, h), TN)
        dA1 = _heads(lambda h: jnp.where(strict, _ldot(dZ[h], hd(Sa, h), NT), 0.0))
        dA2 = _heads(lambda h: jnp.where(strict, _ldot(dZ[h], hd(V, h), NT), 0.0))
        dW1 = _heads(lambda h: jnp.where(incl, _ldot(hd(dY, h), hd(Sa, h), NT), 0.0))
        dW2 = _heads(lambda h: jnp.where(incl, _ldot(hd(dY, h), hd(V, h), NT), 0.0))
        cat = lambda fn: jnp.concatenate(_heads(fn), axis=1)
        dV = cat(lambda h: _ldot(A2[h], dZ[h], TN) + _ldot(W2[h], hd(dY, h), TN) + _ldot(hd(Kt, h), G[h], NT))
        dAt = cat(lambda h: _ldot(dA1[h], hd(Bt, h), NN) + _ldot(dA2[h], hd(Kt, h), NN) + _ldot(dZ[h], S0[h], NN))
        dBt = cat(lambda h: _ldot(dA1[h], hd(At, h), TN) + _ldot(dW1[h], hd(Rt, h), TN) + _ldot(hd(Sa, h), G[h], NN))
        dKt = cat(lambda h: _ldot(dA2[h], hd(At, h), TN) + _ldot(dW2[h], hd(Rt, h), TN) + _ldot(hd(V, h), G[h], NN))
        dRt = cat(lambda h: _ldot(hd(dY, h), S0[h], NN) + _ldot(dW1[h], hd(Bt, h), NN) + _ldot(dW2[h], hd(Kt, h), NN))
        last_row = lax.broadcasted_iota(jnp.int32, (RW_CHUNK, RW_WIDTH), 0) == RW_CHUNK - 1
        dc_prev = dAt * At
        dc = dc_prev + dRt * Rt - dBt * Bt - dKt * Kt + jnp.where(last_row, dc_last, 0.0)
        dr_ref[...] = dRt * q["e_pos"]
        dlw_ref[...] = _hdot(incl_f, dc, TN) - dc_prev
        dk_ref[...] = dKt * q["e_neg"]
        da_ref[...] = dAt * q["e_prev"]
        db_ref[...] = dBt * q["e_neg"]
        dv_ref[...] = dV

    rev = lambda n: nC - 1 - n
    row_spec = pl.BlockSpec((RW_CHUNK, RW_WIDTH), lambda n: (rev(n), 0))
    st_spec = pl.BlockSpec((1, H, N, N), lambda n: (rev(n), 0, 0, 0))
    row_shape = jax.ShapeDtypeStruct((T, RW_WIDTH), F32)
    return pl.pallas_call(
        body, name="rwkv_chunk_bwd", grid=(nC,),
        in_specs=[row_spec] * 7 + [st_spec] * 5 + [row_spec], out_specs=[row_spec] * 6,
        out_shape=[row_shape] * 6, scratch_shapes=[pltpu.VMEM((H, N, N), F32)],
        compiler_params=_params(("arbitrary",)),
    )(r, lw, k, a, b, v, dy, s0, tinv, w1, a2, w2, sa)


def _alibi_slope(head):
    return float(np.float32(2.0 ** (-8.0 * (head + 1) / ATT_HEADS)))


ATT_SPAN = ATT_BLOCK * max(ATT_GROUP_DILATION)
ATT_PAIR_WIDTH = 2 * ATT_HEAD_DIM
ATT_SIDE_BY_SIDE = 8


def _pair_slope(g, hp, j):
    return jnp.where(hp == 0, _alibi_slope(4 * g + j), _alibi_slope(4 * g + 2 + j))


def _att_rows(mi, r, d):
    start = mi * ATT_BLOCK * d + r
    return pl.ds(start, ATT_BLOCK) if d == 1 else pl.ds(start, ATT_BLOCK, stride=d)


def _att_masks():
    qi = lax.broadcasted_iota(jnp.int32, (ATT_BLOCK, ATT_BLOCK), 0)
    kj = lax.broadcasted_iota(jnp.int32, (ATT_BLOCK, ATT_BLOCK), 1)
    return qi, kj


NEG = -1e30


def _att_logits(q, k, slope_d, steps, valid):
    s = lax.dot_general(q.astype(BF16), k.astype(BF16), (((1,), (1,)), ((), ())),
                        preferred_element_type=F32) * (ATT_HEAD_DIM ** -0.5)
    return jnp.where(valid, s - slope_d * steps.astype(F32), NEG)


def _att_fwd(p_att, g):
    T = p_att.shape[0]
    d = ATT_GROUP_DILATION[g]
    W = ATT_PAIR_WIDTH
    nb = T // ATT_SPAN
    mb = ATT_SPAN // (ATT_BLOCK * d)

    def body(q_ref, kc_ref, kp_ref, vc_ref, vp_ref, o_ref, l_ref):
        hp, n = pl.program_id(0), pl.program_id(1)
        qi, kj = _att_masks()
        slopes = [_pair_slope(g, hp, j) * d for j in range(2)]
        blocks = [(r, mi) for r in range(d) for mi in range(mb)]
        for at in range(0, len(blocks), ATT_SIDE_BY_SIDE):
            tasks = []
            for r, mi in blocks[at:at + ATT_SIDE_BY_SIDE]:
                rows = _att_rows(mi, r, d)
                if mi > 0:
                    prev = _att_rows(mi - 1, r, d)
                    kp, vp, has_prev = kc_ref[prev, :], vc_ref[prev, :], True
                else:
                    prev = _att_rows(mb - 1, r, d)
                    kp, vp, has_prev = kp_ref[prev, :], vp_ref[prev, :], n > 0
                q, kc, vc = q_ref[rows, :], kc_ref[rows, :], vc_ref[rows, :]
                for j in range(2):
                    sl = slice(j * ATT_HEAD_DIM, (j + 1) * ATT_HEAD_DIM)
                    tasks.append((q[:, sl], kc[:, sl], kp[:, sl], vc[:, sl], vp[:, sl], has_prev, slopes[j]))
            lc = [_att_logits(t[0], t[1], t[6], qi - kj, kj <= qi) for t in tasks]
            lp = [_att_logits(t[0], t[2], t[6], qi - kj + ATT_BLOCK, (kj >= qi) & t[5]) for t in tasks]
            mx = [jnp.maximum(jnp.max(a, axis=1, keepdims=True), jnp.max(b, axis=1, keepdims=True))
                  for a, b in zip(lc, lp, strict=True)]
            ec = [jnp.exp(a - m) for a, m in zip(lc, mx, strict=True)]
            ep = [jnp.exp(b - m) for b, m in zip(lp, mx, strict=True)]
            den = [jnp.sum(a, axis=1, keepdims=True) + jnp.sum(b, axis=1, keepdims=True)
                   for a, b in zip(ec, ep, strict=True)]
            inv = [1.0 / s for s in den]
            outs = [jnp.dot((a * i).astype(BF16), t[3].astype(BF16), preferred_element_type=F32)
                    + jnp.dot((b * i).astype(BF16), t[4].astype(BF16), preferred_element_type=F32)
                    for a, b, i, t in zip(ec, ep, inv, tasks, strict=True)]
            lses = [jnp.broadcast_to(m + jnp.log(s), (ATT_BLOCK, ATT_HEAD_DIM)) for m, s in zip(mx, den, strict=True)]
            for i, (r, mi) in enumerate(blocks[at:at + ATT_SIDE_BY_SIDE]):
                rows = _att_rows(mi, r, d)
                o_ref[rows, :] = jnp.concatenate(outs[2 * i:2 * i + 2], axis=1)
                l_ref[rows, :] = jnp.concatenate(lses[2 * i:2 * i + 2], axis=1)

    def spec(col0, prev):
        if prev:
            return pl.BlockSpec((ATT_SPAN, W), lambda hp, n: (jnp.maximum(n - 1, 0), col0 + 2 * g + hp))
        return pl.BlockSpec((ATT_SPAN, W), lambda hp, n: (n, col0 + 2 * g + hp))

    o_spec = pl.BlockSpec((ATT_SPAN, W), lambda hp, n: (n, hp))
    o, l = pl.pallas_call(
        body, name=f"att_fwd_g{g}", grid=(2, nb),
        in_specs=[spec(0, False), spec(6, False), spec(6, True), spec(12, False), spec(12, True)],
        out_specs=[o_spec, o_spec],
        out_shape=[jax.ShapeDtypeStruct((T, ATT_GROUP_WIDTH), F32)] * 2,
        compiler_params=_params(("parallel", "arbitrary")),
    )(p_att, p_att, p_att, p_att, p_att)
    return o, l


def _att_bwd(p_att, o, l, do, dl, g):
    T = p_att.shape[0]
    d = ATT_GROUP_DILATION[g]
    W = ATT_PAIR_WIDTH
    nb = T // ATT_SPAN
    mb = ATT_SPAN // (ATT_BLOCK * d)
    scale = ATT_HEAD_DIM ** -0.5

    def body(q_ref, k_ref, v_ref, o_ref, l_ref, do_ref, dl_ref,
             qn_ref, on_ref, ln_ref, don_ref, dln_ref, dq_ref, dk_ref, dv_ref, carry_ref):
        hp, n = pl.program_id(0), pl.program_id(1)
        qi, kj = _att_masks()

        @pl.when(n == 0)
        def _():
            carry_ref[...] = jnp.zeros_like(carry_ref)

        slopes = [_pair_slope(g, hp, j) * d for j in range(2)]
        blocks = [(r, mi) for r in range(d) for mi in range(mb)]
        side_by_side = ATT_SIDE_BY_SIDE // 2
        carry = None
        for at in range(0, len(blocks), side_by_side):
            tasks = []
            for r, mi in blocks[at:at + side_by_side]:
                rows = _att_rows(mi, r, d)
                if mi < mb - 1:
                    nrows = _att_rows(mi + 1, r, d)
                    nxt = (q_ref[nrows, :], o_ref[nrows, :], l_ref[nrows, :], do_ref[nrows, :], dl_ref[nrows, :])
                    has_next = True
                else:
                    nrows = _att_rows(0, r, d)
                    nxt = (qn_ref[nrows, :], on_ref[nrows, :], ln_ref[nrows, :], don_ref[nrows, :],
                           dln_ref[nrows, :])
                    has_next = n < nb - 1
                cur = (q_ref[rows, :], o_ref[rows, :], l_ref[rows, :], do_ref[rows, :], dl_ref[rows, :])
                k_all, v_all = k_ref[rows, :], v_ref[rows, :]
                for j in range(2):
                    sl = slice(j * ATT_HEAD_DIM, (j + 1) * ATT_HEAD_DIM)
                    for blk, steps, valid in ((cur, qi - kj, kj <= qi),
                                              (nxt, qi - kj + ATT_BLOCK, (kj >= qi) & has_next)):
                        q, o_, lse, do_, dlse = (z[:, sl] for z in blk)
                        tasks.append(dict(q=q, o=o_, lse=lse[:, :1], do=do_, dlse=dlse[:, :1], steps=steps,
                                          valid=valid, k=k_all[:, sl], vb=v_all[:, sl].astype(BF16),
                                          slope=slopes[j]))
            p = [jnp.exp(_att_logits(t["q"], t["k"], t["slope"], t["steps"], t["valid"]) - t["lse"]) for t in tasks]
            dp = [lax.dot_general(t["do"].astype(BF16), t["vb"], (((1,), (1,)), ((), ())),
                                  preferred_element_type=F32) for t in tasks]
            dsum = [jnp.sum(t["do"] * t["o"], axis=1, keepdims=True) for t in tasks]
            ds = [a * (b - s + t["dlse"]) for a, b, s, t in zip(p, dp, dsum, tasks, strict=True)]
            dv_ = [jnp.dot(a.T.astype(BF16), t["do"].astype(BF16), preferred_element_type=F32)
                   for a, t in zip(p, tasks, strict=True)]
            dk_ = [jnp.dot(a.T.astype(BF16), t["q"].astype(BF16), preferred_element_type=F32) * scale
                   for a, t in zip(ds, tasks, strict=True)]
            dq_ = [jnp.dot(a.astype(BF16), t["k"].astype(BF16), preferred_element_type=F32) * scale
                   for a, t in zip(ds, tasks, strict=True)]
            for i, (r, mi) in enumerate(blocks[at:at + side_by_side]):
                rows = _att_rows(mi, r, d)
                b = 4 * i
                if mi == 0:
                    carry = carry_ref[r]
                dq_ref[rows, :] = jnp.concatenate([dq_[b], dq_[b + 2]], axis=1) + carry
                carry = jnp.concatenate([dq_[b + 1], dq_[b + 3]], axis=1)
                if mi == mb - 1:
                    carry_ref[r] = carry
                dk_ref[rows, :] = jnp.concatenate([dk_[b] + dk_[b + 1], dk_[b + 2] + dk_[b + 3]], axis=1)
                dv_ref[rows, :] = jnp.concatenate([dv_[b] + dv_[b + 1], dv_[b + 2] + dv_[b + 3]], axis=1)

    head_rows = ATT_BLOCK * d
    nxt_n = lambda n: jnp.minimum((n + 1) * mb, T // head_rows - 1)
    cur_p = lambda col0: pl.BlockSpec((ATT_SPAN, W), lambda hp, n: (n, col0 + 2 * g + hp))
    cur_o = pl.BlockSpec((ATT_SPAN, W), lambda hp, n: (n, hp))
    nxt_o = pl.BlockSpec((head_rows, W), lambda hp, n: (nxt_n(n), hp))
    dq, dk, dv = pl.pallas_call(
        body, name=f"att_bwd_g{g}", grid=(2, nb),
        in_specs=[cur_p(0), cur_p(6), cur_p(12), cur_o, cur_o, cur_o, cur_o,
                  pl.BlockSpec((head_rows, W), lambda hp, n: (nxt_n(n), 2 * g + hp)), nxt_o, nxt_o, nxt_o, nxt_o],
        out_specs=[cur_o, cur_o, cur_o],
        out_shape=[jax.ShapeDtypeStruct((T, ATT_GROUP_WIDTH), F32)] * 3,
        scratch_shapes=[pltpu.VMEM((d, ATT_BLOCK, W), F32)],
        compiler_params=_params(("parallel", "arbitrary")),
    )(p_att, p_att, p_att, o, l, do, dl, p_att, o, l, do, dl)
    return dq, dk, dv


FFN_TILE = 2 * D_FF // N_CHIPS
RKV = 3 * RW_WIDTH
WA = 128
XG = 160
RW_COLS = RKV + WA + XG


def _local_step(x, p, W, target, late_weights=None, early_grads=None, by_chip=False):
    T = x.shape[0]
    tT = 256
    bd512 = _block_diag_ones(RW_WIDTH, RW_HEAD_DIM)
    bd256 = _block_diag_ones(ATT_GROUP_WIDTH, ATT_HEAD_DIM)
    G = {}
    W = dict(W)

    w_in = W["w_in"]
    w_rkv, w_wa, w_xg, w_att = (w_in[:, :RKV], w_in[:, RKV:RKV + WA], w_in[:, RKV + WA:RW_COLS],
                                w_in[:, RW_COLS:])
    mu = W["rw_mu"]
    mu_rkv, mu_wa, mu_xg = mu[:, :RKV], mu[:, RKV:RKV + WA], mu[:, RKV + WA:]
    zpad = jnp.zeros((64, RW_WIDTH), W["rw_w_up"].dtype)
    w_up_pad = jnp.concatenate([W["rw_w_up"], zpad], axis=0)
    a_up_pad = jnp.concatenate([zpad, W["rw_a_up"]], axis=0)
    r_k = W["rw_r_k"].reshape(1, RW_WIDTH)

    (h,) = _rowwise("norm_mix", lambda i, n, r, pv, nx, c: [_rms_fwd(r[0], c[0])], T, tT,
                    rows=[x], consts=[W["g_mix"]], outs=[("row", D_MODEL, BF16)])
    p_rkv = _mm("proj_rkv", h, w_rkv, "nn")
    p_wa = _mm("proj_wa", h, w_wa, "nn")
    p_xg = _mm("proj_xg", h, w_xg, "nn")
    p_att = _mm("proj_att", h, w_att, "nn", tn=768)
    z_gate = _mm("proj_gate", h, W["w_gate"], "nn")

    def rw_pre_core(i, rows, prevs, consts):
        prkv, pwa, pxg = rows[:3]
        (mrkv, mwa, mxg, w0, a0, k_k, k_a, wup, aup, gup, bd) = consts[:11]
        m_rkv = prkv + (_shift_down(prkv, prevs[0], i, 1) - prkv) * mrkv
        m_wa = pwa + (_shift_down(pwa, prevs[1], i, 1) - pwa) * mwa
        m_xg = pxg + (_shift_down(pxg, prevs[2], i, 1) - pxg) * mxg
        r, k, v = m_rkv[:, :RW_WIDTH], m_rkv[:, RW_WIDTH:2 * RW_WIDTH], m_rkv[:, 2 * RW_WIDTH:]
        tw = jnp.tanh(m_wa)
        lw = w0 + jnp.dot(tw.astype(BF16), wup.astype(BF16), preferred_element_type=F32)
        wlog = -_softplus(-lw) - 0.5
        log_decay = -jnp.exp(wlog)
        a = _sigmoid(a0 + jnp.dot(m_wa.astype(BF16), aup.astype(BF16), preferred_element_type=F32))
        sg = _sigmoid(m_xg)
        gate = jnp.dot(sg.astype(BF16), gup.astype(BF16), preferred_element_type=F32)
        kkp = k * k_k
        nrm = jnp.sqrt(_segsum(kkp * kkp, bd))
        nrm_c = jnp.maximum(nrm, 1e-12)
        kk = kkp / nrm_c
        k2 = k * (1.0 + (a - 1.0) * k_a)
        return dict(r=r, k=k, v=v, tw=tw, lw=lw, wlog=wlog, log_decay=log_decay, a=a, sg=sg, gate=gate, kkp=kkp,
                    nrm=nrm, nrm_c=nrm_c, kk=kk, k2=k2, m_rkv=m_rkv, m_wa=m_wa, m_xg=m_xg)

    pre_consts = [mu_rkv, mu_wa, mu_xg, W["rw_w0"], W["rw_a0"], W["rw_k_k"], W["rw_k_a"],
                  w_up_pad, a_up_pad, W["rw_g_up"], bd512]

    def rw_pre(i, n, rows, prevs, nexts, consts):
        q = rw_pre_core(i, rows, prevs, consts)
        return [q["r"], q["log_decay"], q["k2"], q["v"], -q["kk"], q["kk"] * q["a"], q["gate"]]

    r_s, w_s, k_s, v_s, a_s, b_s, gate_s = _rowwise(
        "rwkv_pre", rw_pre, T, tT, rows=[p_rkv, p_wa, p_xg], prevs=[p_rkv, p_wa, p_xg], consts=pre_consts,
        outs=[("row", RW_WIDTH, F32)] * 7)
    (at_s, bt_s, kt_s, rt_s, a2v_s, w2v_s, tinv_s, w1_s, a2_s, w2_s,
     plast_s) = _rwkv_chunk_prep(r_s, w_s, k_s, a_s, b_s, v_s)
    y_scan, sa_s, s0_s = _rwkv_chunk_fwd(v_s, at_s, bt_s, kt_s, rt_s, a2v_s, w2v_s, tinv_s, w1_s, plast_s)

    def rw_post_core(rows, consts):
        y, r, k2, v, gate = rows[:5]
        ln_g, ln_b, rk, bd = consts[:4]
        mean = _segsum(y, bd) * (1.0 / RW_HEAD_DIM)
        yc = y - mean
        var = _segsum(yc * yc, bd) * (1.0 / RW_HEAD_DIM)
        rstd = lax.rsqrt(var + RW_LN_EPS)
        yn = yc * rstd
        s = _segsum(r * k2 * rk, bd)
        return dict(yn=yn, rstd=rstd, s=s, pre=yn * ln_g + ln_b + s * v)

    post_consts = [W["rw_ln_g"], W["rw_ln_b"], r_k, bd512]
    (y_a,) = _rowwise("rwkv_post", lambda i, n, r, pv, nx, c: [rw_post_core(r, c)["pre"] * r[4]], T, tT,
                      rows=[y_scan, r_s, k_s, v_s, gate_s], consts=post_consts, outs=[("row", RW_WIDTH, BF16)])

    att = [_att_fwd(p_att, g) for g in range(3)]

    def comb_weights(ls):
        mx = jnp.maximum(jnp.maximum(ls[0], ls[1]), ls[2])
        es = [jnp.exp(l - mx) for l in ls]
        den = es[0] + es[1] + es[2]
        return [e / den for e in es]

    def att_comb(i, n, rows, pv, nx, c):
        wts = comb_weights(rows[3:6])
        return [wts[0] * rows[0] + wts[1] * rows[1] + wts[2] * rows[2]]

    (y_b,) = _rowwise("att_combine", att_comb, T, tT, rows=[att[0][0], att[1][0], att[2][0], att[0][1], att[1][1],
                                                            att[2][1]], outs=[("row", ATT_GROUP_WIDTH, BF16)])

    if late_weights is not None:
        W.update(late_weights(y_b))
    br_a = _mm("branch_a", y_a, W["w_branch_a"], "nn")
    br_b = _mm("branch_b", y_b, W["w_branch_b"], "nn")

    def merge(i, n, rows, pv, nx, c):
        gates = _sigmoid(rows[0] + c[0])
        return [gates[:, :D_MODEL] * rows[1] + gates[:, D_MODEL:] * rows[2]]

    (merged,) = _rowwise("merge", merge, T, tT, rows=[z_gate, br_a, br_b], consts=[W["b_gate"]],
                         outs=[("row", D_MODEL, BF16)])
    x1 = _mm("mix_out", merged, W["w_out"], "nn", add=x)

    (h2,) = _rowwise("norm_ffn", lambda i, n, r, pv, nx, c: [_rms_fwd(r[0], c[0])], T, tT,
                     rows=[x1], consts=[W["g_ffn"]], outs=[("row", D_MODEL, BF16)])
    u = _mm("ffn_up", h2, W["w_up"], "nn", tn=FFN_TILE)

    def conv_core(i, rows, prevs, consts):
        uu, cw, cb = rows[0], consts[0], consts[1]
        u1 = _shift_down(uu, prevs[0], i, 1)
        u2 = _shift_down(uu, prevs[0], i, 2)
        uc = cb + cw[0:1] * uu + cw[1:2] * u1 + cw[2:3] * u2
        return uc[:, :D_FF], uc[:, D_FF:], u1, u2

    def glu(i, n, rows, prevs, nx, consts):
        gate, val, _, _ = conv_core(i, rows, prevs, consts)
        return [_gelu(gate) * val]

    tF = 128
    (act,) = _rowwise("conv_glu", glu, T, tF, rows=[u], prevs=[u], consts=[W["conv_w"], W["conv_b"]],
                      outs=[("row", D_FF, BF16)])
    x2 = _mm("ffn_down", act, W["w_down"], "nn", add=x1)

    (h3,) = _rowwise("norm_ple", lambda i, n, r, pv, nx, c: [_rms_fwd(r[0], c[0])], T, tT,
                     rows=[x2], consts=[W["g_ple"]], outs=[("row", D_MODEL, BF16)])
    z_ple = _mm("ple_gate", h3, W["w_ple_gate"], "nn")
    e_ple = _mm("ple_emb", p, W["w_ple"], "nn")

    def head(i, n, rows, pv, nx, consts):
        x2_, z, e, tgt = rows
        pg = _sigmoid(z)
        x3 = x2_ + pg * e
        y = _rms_fwd(x3, consts[0])
        err = y - tgt
        loss = 0.5 * jnp.sum(jnp.sum(err * err, axis=1, keepdims=True) * (1.0 / D_MODEL), axis=0, keepdims=True)
        dy = err * (1.0 / D_MODEL)
        dx3, dgf = _rms_bwd(x3, consts[0], dy)
        return [dx3, dx3 * pg, dx3 * e * pg * (1.0 - pg), jnp.broadcast_to(loss, (1, LANES)), _colsum(dgf)]

    dx3, de, dz, loss_acc, G["g_final"] = _rowwise(
        "loss_head", head, T, tT, rows=[x2, z_ple, e_ple, target], consts=[W["g_final"].reshape(1, D_MODEL)],
        outs=[("row", D_MODEL, F32), ("row", D_MODEL, BF16), ("row", D_MODEL, BF16), ("acc", (1, LANES)),
              ("acc", (1, D_MODEL))])
    G["w_ple"] = _mm("d_w_ple", p, de, "tn", out_by_chip=by_chip)
    G["w_ple_gate"] = _mm("d_w_ple_gate", h3, dz, "tn")
    dh3 = _mm("d_h3", dz, W["w_ple_gate"], "nt")

    def norm_bwd(i, n, rows, pv, nx, consts):
        dx, dg = _rms_bwd(rows[0], consts[0], rows[1])
        return [rows[2] + dx, _colsum(dg)]

    dx2, G["g_ple"] = _rowwise("d_norm_ple", norm_bwd, T, tT, rows=[x2, dh3, dx3], consts=[W["g_ple"]],
                               outs=[("row", D_MODEL, F32), ("acc", (1, D_MODEL))])

    dact = _mm("d_act", dx2, W["w_down"], "nt")
    G["w_down"] = _mm("d_w_down", act, dx2, "tn")

    def glu_grad(gate, val, da):
        act_, slope = _gelu_and_grad(gate)
        return jnp.concatenate([da * val * slope, da * act_], axis=1)

    def glu_bwd(i, n, rows, prevs, nexts, consts):
        uu, da = rows
        cw = consts[0]
        gate, val, u1, u2 = conv_core(i, rows, prevs, consts)
        duc = glu_grad(gate, val, da)
        dcw = jnp.concatenate([_colsum(duc * uu), _colsum(duc * u1), _colsum(duc * u2)], axis=0)
        gate_n, val_n, _, _ = conv_core(1, [nexts[0]], [uu[tF - SUBLANES:]], consts)
        duc_n = glu_grad(gate_n, val_n, nexts[1])
        du = (cw[0:1] * duc + cw[1:2] * _shift_up(duc, duc_n, i, n, 1) + cw[2:3] * _shift_up(duc, duc_n, i, n, 2))
        return [du, _colsum(duc), dcw]

    du, G["conv_b"], G["conv_w"] = _rowwise(
        "d_conv_glu", glu_bwd, T, tF, rows=[u, dact], prevs=[u], nexts=[u, dact],
        consts=[W["conv_w"], W["conv_b"]],
        outs=[("row", 2 * D_FF, BF16), ("acc", (1, 2 * D_FF)), ("acc", (3, 2 * D_FF))])
    G["w_up"] = _mm("d_w_up", h2, du, "tn", out_by_chip=by_chip, tn=FFN_TILE)
    dh2 = _mm("d_h2", du, W["w_up"], "nt", tk=FFN_TILE)
    dx1, G["g_ffn"] = _rowwise("d_norm_ffn", norm_bwd, T, tT, rows=[x1, dh2, dx2], consts=[W["g_ffn"]],
                               outs=[("row", D_MODEL, F32), ("acc", (1, D_MODEL))])

    b_gate = W["b_gate"]
    if early_grads is not None:
        b_gate = b_gate + early_grads(G, 0)[0:1, 0:1]
    dmerged = _mm("d_merged", dx1, W["w_out"], "nt")
    G["w_out"] = _mm("d_w_out", merged, dx1, "tn")

    def merge_bwd(i, n, rows, pv, nx, consts):
        z, a_, b_, dm = rows
        gates = _sigmoid(z + consts[0])
        ga, gb = gates[:, :D_MODEL], gates[:, D_MODEL:]
        dz_ = jnp.concatenate([dm * a_ * ga * (1.0 - ga), dm * b_ * gb * (1.0 - gb)], axis=1)
        return [dm * ga, dm * gb, dz_, _colsum(dz_)]

    d_br_a, d_br_b, dz_gate, G["b_gate"] = _rowwise(
        "d_merge", merge_bwd, T, tT, rows=[z_gate, br_a, br_b, dmerged], consts=[b_gate],
        outs=[("row", D_MODEL, BF16), ("row", D_MODEL, BF16), ("row", 2 * D_MODEL, BF16), ("acc", (1, 2 * D_MODEL))])
    G["w_branch_a"] = _mm("d_w_branch_a", y_a, d_br_a, "tn", out_by_chip=by_chip)
    G["w_branch_b"] = _mm("d_w_branch_b", y_b, d_br_b, "tn", out_by_chip=by_chip)
    G["w_gate"] = _mm("d_w_gate", h, dz_gate, "tn", out_by_chip=by_chip)
    if early_grads is not None:
        post_consts = [post_consts[0] + early_grads(G, 1)[0:1, 0:1]] + post_consts[1:]
    dy_a = _mm("d_y_a", d_br_a, W["w_branch_a"], "nt")
    dy_b = _mm("d_y_b", d_br_b, W["w_branch_b"], "nt")

    def att_comb_bwd(i, n, rows, pv, nx, consts):
        os_, ls, dy = rows[0:3], rows[3:6], rows[6]
        wts = comb_weights(ls)
        dws = [_segsum(dy * o_, consts[0]) for o_ in os_]
        mix = wts[0] * dws[0] + wts[1] * dws[1] + wts[2] * dws[2]
        return [wts[g_] * dy for g_ in range(3)] + [wts[g_] * (dws[g_] - mix) for g_ in range(3)]

    comb = _rowwise("d_att_combine", att_comb_bwd, T, tT,
                    rows=[att[0][0], att[1][0], att[2][0], att[0][1], att[1][1], att[2][1], dy_b], consts=[bd256],
                    outs=[("row", ATT_GROUP_WIDTH, F32)] * 6)
    dqkv = [_att_bwd(p_att, att[g][0], att[g][1], comb[g], comb[3 + g], g) for g in range(3)]
    dp_att = jnp.concatenate([dqkv[g][part] for part in range(3) for g in range(3)], axis=1).astype(BF16)

    def rw_post_bwd(i, n, rows, pv, nx, consts):
        y, r, k2, v, gate, dya = rows
        ln_g, ln_b, rk, bd = consts
        q = rw_post_core(rows, consts)
        dpre = dya * gate
        dgate = dya * q["pre"]
        dyn = dpre * ln_g
        inv = 1.0 / RW_HEAD_DIM
        dy_scan = q["rstd"] * (dyn - _segsum(dyn, bd) * inv - q["yn"] * (_segsum(dyn * q["yn"], bd) * inv))
        ds = _segsum(dpre * v, bd)
        return [dy_scan, dgate, ds * k2 * rk, ds * r * rk, dpre * q["s"],
                _colsum(dpre * q["yn"]), _colsum(dpre), _colsum(ds * r * k2)]

    dy_scan, dgate, dr_b, dk2_b, dv_b, G["rw_ln_g"], G["rw_ln_b"], d_rk = _rowwise(
        "d_rwkv_post", rw_post_bwd, T, tT, rows=[y_scan, r_s, k_s, v_s, gate_s, dy_a], consts=post_consts,
        outs=[("row", RW_WIDTH, F32)] * 5 + [("acc", (1, RW_WIDTH))] * 3)
    G["rw_r_k"] = d_rk.reshape(RW_HEADS, RW_HEAD_DIM)

    dr_s, dw_s, dk_s, da_s, db_s, dv_s = _rwkv_chunk_bwd(r_s, w_s, k_s, a_s, b_s, v_s, dy_scan, s0_s, tinv_s, w1_s,
                                                         a2_s, w2_s, sa_s)

    def rw_pre_bwd(i, n, rows, prevs, nx, consts):
        q = rw_pre_core(i, rows, prevs, consts)
        (mrkv, mwa, mxg, w0, a0, k_k, k_a, wup, aup, gup, bd) = consts
        dr, dlogdecay, dk2, dv, dav, dbv, dgate_ = rows[3:10]
        dr = dr + rows[10]
        dk2 = dk2 + rows[11]
        dv = dv + rows[12]
        a, k, kk = q["a"], q["k"], q["kk"]
        dk = dk2 * (1.0 + (a - 1.0) * k_a)
        da = dk2 * k * k_a + dbv * kk
        dkk = dbv * a - dav
        live = q["nrm"] > 1e-12
        dkkp = jnp.where(live, dkk - kk * _segsum(dkk * kk, bd), dkk) / q["nrm_c"]
        dk = dk + dkkp * k_k
        dlw = dlogdecay * q["log_decay"] * _sigmoid(-q["lw"])
        dla = da * a * (1.0 - a)
        nt = (((1,), (1,)), ((), ()))
        dtw = lax.dot_general(dlw.astype(BF16), wup.astype(BF16), nt, preferred_element_type=F32)
        dxa = lax.dot_general(dla.astype(BF16), aup.astype(BF16), nt, preferred_element_type=F32)
        dm_wa = dtw * (1.0 - q["tw"] * q["tw"]) + dxa
        dsg = lax.dot_general(dgate_.astype(BF16), gup.astype(BF16), nt, preferred_element_type=F32)
        dm_xg = dsg * q["sg"] * (1.0 - q["sg"])
        dm_rkv = jnp.concatenate([dr, dk, dv], axis=1)
        prkv, pwa, pxg = rows[:3]
        dmu = jnp.concatenate([_colsum(dm_rkv * (_shift_down(prkv, prevs[0], i, 1) - prkv)),
                               _colsum(dm_wa * (_shift_down(pwa, prevs[1], i, 1) - pwa)),
                               _colsum(dm_xg * (_shift_down(pxg, prevs[2], i, 1) - pxg))], axis=1)
        return [dm_rkv, dm_wa, dm_xg, dlw, dla, q["tw"], q["m_wa"], q["sg"], dmu,
                _colsum(dlw), _colsum(dla), _colsum(dkkp * k), _colsum(dk2 * k * (a - 1.0))]

    (dm_rkv, dm_wa, dm_xg, dlw, dla, tw_s, mwa_s, sg_s, G["rw_mu"], G["rw_w0"], G["rw_a0"], G["rw_k_k"],
     G["rw_k_a"]) = _rowwise(
        "d_rwkv_pre", rw_pre_bwd, T, tT,
        rows=[p_rkv, p_wa, p_xg, dr_s, dw_s, dk_s, dv_s, da_s, db_s, dgate, dr_b, dk2_b, dv_b],
        prevs=[p_rkv, p_wa, p_xg], consts=pre_consts,
        outs=[("row", RKV, F32), ("row", WA, F32), ("row", XG, F32), ("row", RW_WIDTH, BF16),
              ("row", RW_WIDTH, BF16), ("row", WA, BF16), ("row", WA, BF16), ("row", XG, BF16),
              ("acc", (1, RW_COLS))] + [("acc", (1, RW_WIDTH))] * 4)
    G["rw_w_up"] = _mm("d_rw_w_up", tw_s, dlw, "tn")[:64]
    G["rw_a_up"] = _mm("d_rw_a_up", mwa_s, dla, "tn")[64:]
    G["rw_g_up"] = _mm("d_rw_g_up", sg_s, dgate, "tn")

    def shift_bwd(i, n, rows, pv, nexts, consts):
        return [rows[j] * (1.0 - consts[j]) + _shift_up(rows[j], nexts[j], i, n, 1) * consts[j] for j in range(3)]

    dp_rkv, dp_wa, dp_xg = _rowwise(
        "d_token_shift", shift_bwd, T, tT, rows=[dm_rkv, dm_wa, dm_xg], nexts=[dm_rkv, dm_wa, dm_xg],
        consts=[mu_rkv, mu_wa, mu_xg], outs=[("row", RKV, BF16), ("row", WA, BF16), ("row", XG, BF16)])

    G["w_in"] = jnp.concatenate([_mm("d_w_rkv", h, dp_rkv, "tn"), _mm("d_w_wa", h, dp_wa, "tn"),
                                 _mm("d_w_xg", h, dp_xg, "tn"), _mm("d_w_att", h, dp_att, "tn", tn=768)], axis=1)
    if early_grads is not None:
        w_wa = w_wa + early_grads(G, 2)[0:1, 0:1].astype(w_wa.dtype)
    dh = _mm("d_h_gate", dz_gate, W["w_gate"], "nt")
    dh = _mm("d_h_rkv", dp_rkv, w_rkv, "nt", add=dh)
    dh = _mm("d_h_wa", dp_wa, w_wa, "nt", add=dh)
    dh = _mm("d_h_xg", dp_xg, w_xg, "nt", add=dh)
    dh = _mm("d_h_att", dp_att, w_att, "nt", add=dh)
    dx, G["g_mix"] = _rowwise("d_norm_mix", norm_bwd, T, tT, rows=[x, dh, dx1], consts=[W["g_mix"]],
                              outs=[("row", D_MODEL, F32), ("acc", (1, D_MODEL))])
    return loss_acc[:, :1], dx, G


HBM_SPEC = pl.BlockSpec(memory_space=pltpu.HBM)


def _place():
    x, y, c = lax.axis_index("x"), lax.axis_index("y"), lax.axis_index("c")
    return x, y, c, [(1 - x, y), (x, 1 - y), (1 - x, 1 - y)]


def _remote(src, dst, send_sems, recv_sems, k, to):
    return pltpu.make_async_remote_copy(src_ref=src, dst_ref=dst, send_sem=send_sems.at[k], recv_sem=recv_sems.at[k],
                                        device_id=to, device_id_type=MESH)


ROW_ALIGN = 16


def _splits(rows):
    return rows % (2 * ROW_ALIGN) == 0


def _half_rows(ref_rows, c, first):
    half = ref_rows // 2
    which = c if first else 1 - c
    return pl.ds(pl.multiple_of(which * half, ROW_ALIGN), half)


def _gather_chips(shards):
    n = len(shards)
    split = [_splits(s.shape[0]) for s in shards]

    def body(*refs):
        w_refs, out_refs = refs[:n], refs[n:2 * n]
        send_sems, recv_sems = refs[2 * n:]
        x, y, c, chips = _place()
        me = 2 * x + y
        sends, passed = [], []
        for i in range(n):
            for j, (px, py) in enumerate(chips):
                if split[i]:
                    mine = _half_rows(w_refs[i].shape[0], c, True)
                    cp = _remote(w_refs[i].at[mine], out_refs[i].at[me, mine], send_sems, recv_sems, 6 * i + j,
                                 (px, py, c))
                else:
                    cp = _remote(w_refs[i], out_refs[i].at[me], send_sems, recv_sems, 6 * i + j, (px, py, c))
                cp.start()
                sends.append(cp)
        for i in range(n):
            for j, (px, py) in enumerate(chips):
                if split[i]:
                    landed = out_refs[i].at[2 * px + py, _half_rows(w_refs[i].shape[0], c, True)]
                    _remote(landed, landed, send_sems, recv_sems, 6 * i + j, (px, py, c)).wait_recv()
                    cp = _remote(landed, landed, send_sems, recv_sems, 6 * i + 3 + j, (x, y, 1 - c))
                    cp.start()
                    passed.append(cp)
                else:
                    landed = out_refs[i].at[2 * px + py]
                    _remote(landed, landed, send_sems, recv_sems, 6 * i + j, (px, py, c)).wait_recv()
        for i in range(n):
            if split[i]:
                for j, (px, py) in enumerate(chips):
                    landed = out_refs[i].at[2 * px + py, _half_rows(w_refs[i].shape[0], c, False)]
                    _remote(landed, landed, send_sems, recv_sems, 6 * i + 3 + j, (x, y, 1 - c)).wait_recv()
        for cp in sends + passed:
            cp.wait_send()

    outs = pl.pallas_call(
        body, name="gather_weights", in_specs=[HBM_SPEC] * n, out_specs=[HBM_SPEC] * n,
        out_shape=[jax.ShapeDtypeStruct((N_CHIPS,) + s.shape, s.dtype) for s in shards],
        scratch_shapes=[pltpu.SemaphoreType.DMA((6 * n,)), pltpu.SemaphoreType.DMA((6 * n,))],
    )(*shards)
    me = 2 * lax.axis_index("x") + lax.axis_index("y")
    return [lax.dynamic_update_slice(o, s[None], (me, 0, 0)) for o, s in zip(outs, shards, strict=True)]


def _swap_halves(name, gs):
    n = len(gs)

    def body(*refs):
        g_refs, out_refs = refs[:n], refs[n:2 * n]
        send_sems, recv_sems = refs[2 * n:]
        x, y, c, _ = _place()
        cps = []
        for i in range(n):
            theirs = _half_rows(g_refs[i].shape[1], c, False)
            cp = _remote(g_refs[i].at[:, theirs, :], out_refs[i], send_sems, recv_sems, i, (x, y, 1 - c))
            cp.start()
            cps.append(cp)
        for cp in cps:
            cp.wait()

    return pl.pallas_call(
        body, name=name, in_specs=[HBM_SPEC] * n, out_specs=[HBM_SPEC] * n,
        out_shape=[jax.ShapeDtypeStruct((N_CHIPS, g.shape[1] // 2, g.shape[2]), g.dtype) for g in gs],
        scratch_shapes=[pltpu.SemaphoreType.DMA((n,)), pltpu.SemaphoreType.DMA((n,))],
    )(*gs)


def _join_halves(reds):
    n = len(reds)

    def body(*refs):
        r_refs, out_refs = refs[:n], refs[n:2 * n]
        send_sems, recv_sems = refs[2 * n:]
        x, y, c, _ = _place()
        cps = []
        for i in range(n):
            mine = _half_rows(out_refs[i].shape[0], c, True)
            cp = _remote(r_refs[i], out_refs[i].at[mine], send_sems, recv_sems, i, (x, y, 1 - c))
            cp.start()
            cps.append(cp)
        for cp in cps:
            cp.wait()

    outs = pl.pallas_call(
        body, name="join_halves", in_specs=[HBM_SPEC] * n, out_specs=[HBM_SPEC] * n,
        out_shape=[jax.ShapeDtypeStruct((2 * r.shape[0], r.shape[1]), r.dtype) for r in reds],
        scratch_shapes=[pltpu.SemaphoreType.DMA((n,)), pltpu.SemaphoreType.DMA((n,))],
    )(*reds)
    c = lax.axis_index("c")
    return [lax.dynamic_update_slice(o, r, (c * r.shape[0], 0)) for o, r in zip(outs, reds, strict=True)]


def _gather_all(vec):
    def body(v_ref, out_ref, send_sems, recv_sems, local_sem):
        x, y, c, _ = _place()
        me = 4 * x + 2 * y + c
        local = pltpu.make_async_copy(v_ref, out_ref.at[me], local_sem)
        local.start()
        peers = [(x ^ (k >> 2), y ^ ((k >> 1) & 1), c ^ (k & 1)) for k in range(1, N_DEV)]
        sends = [_remote(v_ref, out_ref.at[me], send_sems, recv_sems, k, to) for k, to in enumerate(peers)]
        for cp in sends:
            cp.start()
        for k, (px, py, pc) in enumerate(peers):
            landed = out_ref.at[4 * px + 2 * py + pc]
            _remote(landed, landed, send_sems, recv_sems, k, (px, py, pc)).wait_recv()
        for cp in sends:
            cp.wait_send()
        local.wait()

    return pl.pallas_call(
        body, name="gather_small", in_specs=[HBM_SPEC], out_specs=HBM_SPEC,
        out_shape=jax.ShapeDtypeStruct((N_DEV,) + vec.shape, vec.dtype),
        scratch_shapes=[pltpu.SemaphoreType.DMA((7,)), pltpu.SemaphoreType.DMA((7,)), pltpu.SemaphoreType.DMA],
    )(vec)


SEM_SPEC = pl.BlockSpec(memory_space=pltpu.SEMAPHORE)
DATAFLOW = pltpu.SideEffectType.DATAFLOW_SIDE_EFFECTING


def _travel_copies(mode, src_refs, land_refs, send_sems, recv_sems):
    x, y, c, chips = _place()
    me = 2 * x + y
    pairs = []
    for i, (src, land) in enumerate(zip(src_refs, land_refs, strict=True)):
        for j, (px, py) in enumerate(chips):
            peer = 2 * px + py
            if mode == "scatter":
                mine, there, here = src.at[peer], land.at[me], land.at[peer]
            elif _splits(src.shape[0]):
                rows = _half_rows(src.shape[0], c, True)
                mine, there, here = src.at[rows], land.at[me, rows], land.at[peer, rows]
            else:
                mine, there, here = src, land.at[me], land.at[peer]
            send = functools.partial(_remote, mine, there, send_sems, recv_sems, 3 * i + j, (px, py, c))
            arrival = functools.partial(_remote, mine, here, send_sems, recv_sems, 3 * i + j, (px, py, c))
            pairs.append((send, arrival))
    return pairs


def _share_halves(name, lands):
    idx = [i for i, a in enumerate(lands) if _splits(a.shape[1])]
    n = len(idx)

    def body(*refs):
        in_refs, out_refs = refs[:n], refs[n:2 * n]
        send_sems, recv_sems = refs[2 * n:]
        x, y, c, chips = _place()
        cps = []
        for i, (src, dst) in enumerate(zip(in_refs, out_refs, strict=True)):
            for j, (px, py) in enumerate(chips):
                mine = _half_rows(src.shape[1], c, True)
                cp = _remote(src.at[2 * px + py, mine], dst.at[2 * px + py, mine], send_sems, recv_sems, 3 * i + j,
                             (x, y, 1 - c))
                cp.start()
                cps.append(cp)
        for i, dst in enumerate(out_refs):
            for j, (px, py) in enumerate(chips):
                theirs = dst.at[2 * px + py, _half_rows(dst.shape[1], c, False)]
                _remote(theirs, theirs, send_sems, recv_sems, 3 * i + j, (x, y, 1 - c)).wait_recv()
        for cp in cps:
            cp.wait_send()

    outs = pl.pallas_call(
        body, name=name, in_specs=[HBM_SPEC] * n, out_specs=[HBM_SPEC] * n,
        out_shape=[jax.ShapeDtypeStruct(lands[i].shape, lands[i].dtype) for i in idx],
        input_output_aliases={i: i for i in range(n)},
        scratch_shapes=[pltpu.SemaphoreType.DMA((3 * n,)), pltpu.SemaphoreType.DMA((3 * n,))],
    )(*[lands[i] for i in idx])
    done = list(lands)
    for i, o in zip(idx, outs, strict=True):
        done[i] = o
    return done


def _travel_start(name, mode, srcs):
    n = len(srcs)
    lands = [lax.empty((N_CHIPS,) + (s.shape if mode == "gather" else s.shape[1:]), s.dtype) for s in srcs]

    def body(*refs):
        src_refs, land_refs = refs[:n], refs[n:2 * n]
        send_sems, recv_sems = refs[2 * n], refs[2 * n + 1]
        token = refs[-1]
        for send, _ in _travel_copies(mode, src_refs, land_refs, send_sems, recv_sems):
            send().start()
        token[...] = jnp.zeros_like(token)

    hbm = lambda a: pltpu.HBM(a.shape, a.dtype)
    outs = pl.pallas_call(
        body, name=name,
        out_shape=(pltpu.SemaphoreType.DMA((3 * n,)), pltpu.SemaphoreType.DMA((3 * n,)), *[hbm(s) for s in srcs],
                   *[hbm(a) for a in lands], jax.ShapeDtypeStruct((SUBLANES, LANES), F32)),
        in_specs=[HBM_SPEC] * (2 * n),
        out_specs=(SEM_SPEC, SEM_SPEC, *[HBM_SPEC] * (2 * n), pl.BlockSpec(memory_space=pltpu.VMEM)),
        input_output_aliases={i: 2 + i for i in range(2 * n)},
        compiler_params=pltpu.CompilerParams(has_side_effects=DATAFLOW),
    )(*[pltpu.with_memory_space_constraint(a, pltpu.HBM) for a in list(srcs) + lands])
    return outs[0], outs[1], list(outs[2:2 + n]), list(outs[2 + n:2 + 2 * n]), outs[-1]


def _travel_wait(name, mode, send_sems, recv_sems, srcs, lands, after):
    n = len(srcs)

    def body(*refs):
        src_refs, land_refs = refs[:n], refs[n:2 * n]
        send_sems_, recv_sems_ = refs[2 * n], refs[2 * n + 1]
        for send, arrival in _travel_copies(mode, src_refs, land_refs, send_sems_, recv_sems_):
            send().wait_send()
            arrival().wait_recv()

    hbm = lambda a: pltpu.HBM(a.shape, a.dtype)
    outs = pl.pallas_call(
        body, name=name, out_shape=tuple(hbm(a) for a in list(srcs) + list(lands)),
        in_specs=[HBM_SPEC] * (2 * n) + [SEM_SPEC, SEM_SPEC, pl.BlockSpec(memory_space=pl.ANY)],
        out_specs=tuple([HBM_SPEC] * (2 * n)), input_output_aliases={i: i for i in range(2 * n)},
        compiler_params=pltpu.CompilerParams(has_side_effects=DATAFLOW),
    )(*srcs, *lands, send_sems, recv_sems, after)
    me = 2 * lax.axis_index("x") + lax.axis_index("y")
    own = [s[None] if mode == "gather" else lax.dynamic_slice_in_dim(s, me, 1, axis=0) for s in outs[:n]]
    return [lax.dynamic_update_slice(a, o, (me,) + (0,) * (a.ndim - 1)) for a, o in zip(outs[n:], own, strict=True)]


SUM_TILE_BYTES = 4 * 1024 * 1024


def _sum_rows(half, cols):
    best = ROW_ALIGN
    for t in range(ROW_ALIGN, half + 1, ROW_ALIGN):
        if half % t == 0 and N_CHIPS * t * cols * 4 <= SUM_TILE_BYTES:
            best = t
    return best


def _sum_cores(name, g, theirs, core):
    _, R, C = g.shape
    half = R // 2
    tr = _sum_rows(half, C)
    nb = half // tr

    def body(core_ref, g_ref, t_ref, o_ref):
        o_ref[...] = (g_ref[...] + t_ref[...]).astype(o_ref.dtype)

    grid_spec = pltpu.PrefetchScalarGridSpec(
        num_scalar_prefetch=1, grid=(nb,),
        in_specs=[pl.BlockSpec((N_CHIPS, tr, C), lambda i, core_ref: (0, core_ref[0] * nb + i, 0)),
                  pl.BlockSpec((N_CHIPS, tr, C), lambda i, core_ref: (0, i, 0))],
        out_specs=pl.BlockSpec((N_CHIPS, tr, C), lambda i, core_ref: (0, i, 0)))
    return pl.pallas_call(
        body, name=name, grid_spec=grid_spec, out_shape=jax.ShapeDtypeStruct((N_CHIPS, half, C), BF16),
        compiler_params=_params(("parallel",)),
    )(core, g, theirs)


def _sum_chips(name, parts):
    _, H, C = parts.shape
    tr = _sum_rows(H, C)

    def body(p_ref, o_ref):
        acc = p_ref[0].astype(F32)
        for k in range(1, N_CHIPS):
            acc = acc + p_ref[k].astype(F32)
        o_ref[...] = acc

    return pl.pallas_call(
        body, name=name, grid=(H // tr,),
        in_specs=[pl.BlockSpec((N_CHIPS, tr, C), lambda i: (0, i, 0))],
        out_specs=pl.BlockSpec((tr, C), lambda i: (i, 0)),
        out_shape=jax.ShapeDtypeStruct((H, C), F32),
        compiler_params=_params(("parallel",)),
    )(parts)


def _adamw_math(w, g, m, v):
    m = ADAM_B1 * m + (1.0 - ADAM_B1) * g
    v = ADAM_B2 * v + (1.0 - ADAM_B2) * (g * g)
    m_hat = m / (1.0 - ADAM_B1 ** ADAM_STEP)
    v_hat = v / (1.0 - ADAM_B2 ** ADAM_STEP)
    delta = -ADAM_LR * (m_hat / (jnp.sqrt(v_hat) + ADAM_EPS) + ADAM_WD * w)
    return delta, m, v


def _adamw(name, w, g, m, v):
    R, C = w.shape
    tr = R
    if R % SUBLANES == 0:
        for cand in range(SUBLANES, min(R, 256) + 1, SUBLANES):
            if R % cand == 0:
                tr = cand

    def body(w_ref, g_ref, m_ref, v_ref, d_ref, nm_ref, nv_ref):
        d, nm, nv = _adamw_math(w_ref[...], g_ref[...], m_ref[...], v_ref[...])
        d_ref[...] = d
        nm_ref[...] = nm
        nv_ref[...] = nv

    spec = pl.BlockSpec((tr, C), lambda i: (i, 0))
    shape = jax.ShapeDtypeStruct((R, C), F32)
    return pl.pallas_call(
        body, name=name, grid=(R // tr,), in_specs=[spec] * 4, out_specs=[spec] * 3, out_shape=[shape] * 3,
        compiler_params=_params(("parallel",)),
    )(w, g, m, v)


SMALL_ROW = 2048


def _small_layout(shapes):
    places, row = [], 0
    for R, C in shapes:
        pieces = []
        for r in range(R):
            for c0 in range(0, C, SMALL_ROW):
                pieces.append((r, c0, min(C, c0 + SMALL_ROW), row))
                row += 1
        places.append(pieces)
    return places, -(-row // SUBLANES) * SUBLANES


def _put_rows(block_ref, refs, places):
    block_ref[...] = jnp.zeros_like(block_ref)
    for ref, pieces in zip(refs, places, strict=True):
        for r, c0, c1, row in pieces:
            block_ref[row:row + 1, 0:c1 - c0] = ref[r:r + 1, c0:c1]


def _take_rows(block, refs, places):
    for ref, pieces in zip(refs, places, strict=True):
        for r, c0, c1, row in pieces:
            ref[r:r + 1, c0:c1] = block[row:row + 1, 0:c1 - c0]


def _pack_small(arrs):
    places, rows = _small_layout([a.shape for a in arrs])

    def body(*refs):
        _put_rows(refs[-1], refs[:-1], places)

    return pl.pallas_call(body, name="pack_small", out_shape=jax.ShapeDtypeStruct((rows, SMALL_ROW), F32),
                          compiler_params=_params())(*arrs)


def _adamw_small(parts, ws, ms, vs, extra_shapes):
    n_dev, rows, _ = parts.shape
    n = len(ws)
    places, rows_ = _small_layout([w.shape for w in ws] + list(extra_shapes))
    assert rows_ == rows, (rows_, rows)

    def body(*refs):
        p_ref = refs[0]
        w_refs, m_refs, v_refs = refs[1:1 + n], refs[1 + n:1 + 2 * n], refs[1 + 2 * n:1 + 3 * n]
        outs = refs[1 + 3 * n:-3]
        wb, mb, vb = refs[-3:]
        for block, srcs in ((wb, w_refs), (mb, m_refs), (vb, v_refs)):
            _put_rows(block, srcs, places[:n])
        g = p_ref[0]
        for k in range(1, n_dev):
            g = g + p_ref[k]
        d, nm, nv = _adamw_math(wb[...], g, mb[...], vb[...])
        _take_rows(g, outs[0:n], places[:n])
        _take_rows(d, outs[n:2 * n], places[:n])
        _take_rows(nm, outs[2 * n:3 * n], places[:n])
        _take_rows(nv, outs[3 * n:4 * n], places[:n])
        _take_rows(g, outs[4 * n:], places[n:])

    shapes = [jax.ShapeDtypeStruct(w.shape, F32) for w in ws]
    res = pl.pallas_call(
        body, name="adamw_small", out_shape=shapes * 4 + [jax.ShapeDtypeStruct(s, F32) for s in extra_shapes],
        scratch_shapes=[pltpu.VMEM((rows, SMALL_ROW), F32)] * 3, compiler_params=_params(),
    )(parts, *ws, *ms, *vs)
    return res[0:n], res[n:2 * n], res[2 * n:3 * n], res[3 * n:4 * n], res[4 * n:]


WEIGHTS = ['g_mix', 'w_in', 'rw_mu', 'rw_w0', 'rw_w_up', 'rw_a0', 'rw_a_up', 'rw_g_up', 'rw_k_k', 'rw_k_a',
           'rw_r_k', 'rw_ln_g', 'rw_ln_b', 'w_branch_a', 'w_branch_b', 'w_gate', 'b_gate', 'w_out', 'g_ffn', 'w_up',
           'conv_w', 'conv_b', 'w_down', 'g_ple', 'w_ple_gate', 'w_ple', 'g_final']
ARG_NAMES = (['x', 'p'] + WEIGHTS + ['loss_target'] + ['m_' + n for n in WEIGHTS] + ['v_' + n for n in WEIGHTS])
SHARDED = {'w_in': 1, 'rw_w_up': 1, 'rw_a_up': 1, 'rw_g_up': 1, 'w_branch_a': 1, 'w_branch_b': 1, 'w_gate': 1,
           'w_out': 0, 'w_up': 1, 'conv_w': 1, 'w_down': 0, 'w_ple_gate': 0, 'w_ple': 1}
SMALL = [n for n in WEIGHTS if n not in SHARDED]
WHOLE = ['conv_w']
FIRST_USED = ['w_in', 'rw_w_up', 'rw_a_up', 'rw_g_up', 'w_gate']
READ_BY_CHIP = ['w_gate', 'w_branch_a', 'w_branch_b', 'w_up', 'w_ple']
FIRST_DONE = [['w_up', 'w_down', 'w_ple_gate', 'w_ple'], ['w_out', 'w_branch_a', 'w_branch_b', 'w_gate'],
              ['w_in', 'rw_w_up', 'rw_a_up', 'rw_g_up']]
SPLIT = [n for n in SHARDED if n not in WHOLE]


def _full_from_shards(stack, axis):
    _, R, C = stack.shape
    if axis == 0:
        return stack.reshape(N_CHIPS * R, C)
    return stack.transpose(1, 0, 2).reshape(R, N_CHIPS * C)


def _shards_from_full(full, axis):
    R, C = full.shape
    if axis == 0:
        return full.reshape(N_CHIPS, R // N_CHIPS, C)
    return full.reshape(R, N_CHIPS, C // N_CHIPS).transpose(1, 0, 2)


def kernel(x, p, g_mix, w_in, rw_mu, rw_w0, rw_w_up, rw_a0, rw_a_up, rw_g_up, rw_k_k, rw_k_a, rw_r_k, rw_ln_g, rw_ln_b, w_branch_a, w_branch_b, w_gate, b_gate, w_out, g_ffn, w_up, conv_w, conv_b, w_down, g_ple, w_ple_gate, w_ple, g_final, loss_target, m_g_mix, m_w_in, m_rw_mu, m_rw_w0, m_rw_w_up, m_rw_a0, m_rw_a_up, m_rw_g_up, m_rw_k_k, m_rw_k_a, m_rw_r_k, m_rw_ln_g, m_rw_ln_b, m_w_branch_a, m_w_branch_b, m_w_gate, m_b_gate, m_w_out, m_g_ffn, m_w_up, m_conv_w, m_conv_b, m_w_down, m_g_ple, m_w_ple_gate, m_w_ple, m_g_final, v_g_mix, v_w_in, v_rw_mu, v_rw_w0, v_rw_w_up, v_rw_a0, v_rw_a_up, v_rw_g_up, v_rw_k_k, v_rw_k_a, v_rw_r_k, v_rw_ln_g, v_rw_ln_b, v_w_branch_a, v_w_branch_b, v_w_gate, v_b_gate, v_w_out, v_g_ffn, v_w_up, v_conv_w, v_conv_b, v_w_down, v_g_ple, v_w_ple_gate, v_w_ple, v_g_final):
    given = dict(zip(ARG_NAMES, (x, p, g_mix, w_in, rw_mu, rw_w0, rw_w_up, rw_a0, rw_a_up, rw_g_up, rw_k_k, rw_k_a, rw_r_k, rw_ln_g, rw_ln_b, w_branch_a, w_branch_b, w_gate, b_gate, w_out, g_ffn, w_up, conv_w, conv_b, w_down, g_ple, w_ple_gate, w_ple, g_final, loss_target, m_g_mix, m_w_in, m_rw_mu, m_rw_w0, m_rw_w_up, m_rw_a0, m_rw_a_up, m_rw_g_up, m_rw_k_k, m_rw_k_a, m_rw_r_k, m_rw_ln_g, m_rw_ln_b, m_w_branch_a, m_w_branch_b, m_w_gate, m_b_gate, m_w_out, m_g_ffn, m_w_up, m_conv_w, m_conv_b, m_w_down, m_g_ple, m_w_ple_gate, m_w_ple, m_g_final, v_g_mix, v_w_in, v_rw_mu, v_rw_w0, v_rw_w_up, v_rw_a0, v_rw_a_up, v_rw_g_up, v_rw_k_k, v_rw_k_a, v_rw_r_k, v_rw_ln_g, v_rw_ln_b, v_w_branch_a, v_w_branch_b, v_w_gate, v_b_gate, v_w_out, v_g_ffn, v_w_up, v_conv_w, v_conv_b, v_w_down, v_g_ple, v_w_ple_gate, v_w_ple, v_g_final), strict=True))

    def two_d(name, prefix=""):
        a = given[prefix + name]
        if name == "g_final":
            return a.reshape(1, D_MODEL)
        if name == "rw_r_k":
            return a.reshape(1, RW_WIDTH)
        return a[0] if a.ndim == 3 else a

    cast = lambda n: two_d(n) if n in WHOLE else two_d(n).astype(BF16)
    whole = lambda names, stacks: {n: g if n in READ_BY_CHIP else _full_from_shards(g, SHARDED[n])
                                   for n, g in zip(names, stacks, strict=True)}
    late_names = [n for n in SHARDED if n not in FIRST_USED]
    late_sends, late_recvs, late_srcs, late_lands, token = _travel_start(
        "gather_late_start", "gather", [cast(n) for n in late_names])
    W = whole(FIRST_USED, _gather_chips([cast(n) for n in FIRST_USED]))
    for n in SMALL:
        W[n] = two_d(n)
    W["rw_r_k"] = W["rw_r_k"].reshape(RW_HEADS, RW_HEAD_DIM)
    W["g_mix"] = W["g_mix"] + token[0:1, 0:1]

    def late_weights(after):
        lands = _travel_wait("gather_late_wait", "gather", late_sends, late_recvs, late_srcs, late_lands, after)
        return whole(late_names, _share_halves("share_late", lands))

    core = lax.axis_index("c").astype(jnp.int32).reshape(1)
    early_names = [[n for n in SPLIT if n in group] for group in FIRST_DONE]
    rest_names = [n for n in SPLIT if not any(n in group for group in FIRST_DONE)]
    travelling = []

    def core_sums(tag, names, G):
        by_chip = [G[n] if n in READ_BY_CHIP else _shards_from_full(G[n], SHARDED[n]) for n in names]
        theirs = _swap_halves("swap_halves_" + tag, by_chip)
        return [_sum_cores("sum_cores_" + n, g, t, core) for n, g, t in zip(names, by_chip, theirs, strict=True)]

    def early_grads(G, stage):
        sends, recvs, srcs, lands, started = _travel_start(f"scatter_early{stage}_start", "scatter",
                                                           core_sums(f"early{stage}", early_names[stage], G))
        travelling.append((sends, recvs, srcs, lands))
        return started

    loss_part, grad_x, G = _local_step(x[0], p[0, 0], W, loss_target[0], late_weights, early_grads, by_chip=True)

    assert not rest_names, rest_names
    landed = {}
    for stage, (sends, recvs, srcs, lands) in enumerate(travelling):
        landed.update(zip(early_names[stage], _travel_wait(f"scatter_early{stage}_wait", "scatter", sends, recvs,
                                                           srcs, lands, grad_x), strict=True))
    reduced = [_sum_chips("sum_chips_" + n, landed[n]) for n in SPLIT]
    shard_grads = dict(zip(SPLIT, _join_halves(reduced), strict=True))

    G["rw_r_k"] = G["rw_r_k"].reshape(1, RW_WIDTH)
    extras = [G[n] for n in WHOLE] + [loss_part]
    all_small = _gather_all(_pack_small([G[n] for n in SMALL] + extras))
    gs, ds, nms, nvs, summed = _adamw_small(all_small, [two_d(n) for n in SMALL], [two_d(n, "m_") for n in SMALL],
                                            [two_d(n, "v_") for n in SMALL], [e.shape for e in extras])
    loss = summed[-1][0, 0]
    chip = 2 * lax.axis_index("x") + lax.axis_index("y")
    for n, full in zip(WHOLE, summed[:-1], strict=True):
        width = two_d(n).shape[1]
        shard_grads[n] = lax.dynamic_slice_in_dim(full, chip * width, width, axis=1)

    grads, deltas, new_m, new_v = {}, {}, {}, {}
    for n in SHARDED:
        g = shard_grads[n]
        d, nm, nv = _adamw("adamw_" + n, two_d(n), g, two_d(n, "m_"), two_d(n, "v_"))
        grads[n], deltas[n], new_m[n], new_v[n] = g, d, nm, nv
    for i, n in enumerate(SMALL):
        grads[n], deltas[n], new_m[n], new_v[n] = gs[i], ds[i], nms[i], nvs[i]
    outs = [loss, grad_x[None]]
    for table in (grads, deltas, new_m, new_v):
        outs += [table[n].reshape(given[n].shape) for n in WEIGHTS]
    return tuple(outs)
```

```python
import functools
import math

import jax
import jax.numpy as jnp
import numpy as np
from jax import lax
from jax.experimental import pallas as pl
from jax.experimental.pallas import tpu as pltpu

F32 = jnp.float32
BF16 = jnp.bfloat16

D_MODEL = 1024
NORM_EPS = 1e-6
RW_HEADS = 8
RW_HEAD_DIM = 64
RW_WIDTH = 512
RW_LN_EPS = 64e-5
ATT_GROUP_DILATION = (1, 4, 16)
ATT_BLOCK = 128
ATT_HEADS = 12
ATT_HEAD_DIM = 64
ATT_GROUP_WIDTH = 256
ATT_WIDTH = 768
D_FF = 3072

ADAM_LR = 0.001
ADAM_B1 = 0.9
ADAM_B2 = 0.999
ADAM_EPS = 1e-08
ADAM_WD = 0.01
ADAM_STEP = 10

SUBLANES = 8
LANES = 128
VMEM_LIMIT = 56 * 1024 * 1024
N_CHIPS = 4
N_DEV = 8
MESH = pl.DeviceIdType.MESH


def _params(sem=None):
    return pltpu.CompilerParams(dimension_semantics=sem, vmem_limit_bytes=VMEM_LIMIT)


def _pick(dim, pref):
    if dim % LANES != 0 or dim <= pref:
        return dim
    best = LANES
    for t in range(LANES, pref + 1, LANES):
        if dim % t == 0:
            best = t
    return best


def _mm(name, a, b, mode, out_dtype=F32, add=None, tm=1024, tn=1024, tk=1024, out_by_chip=False):
    by_chip = b.ndim == 3
    b_rows, b_cols = (b.shape[1], N_CHIPS * b.shape[2]) if by_chip else b.shape
    if mode == "nn":
        (M, K), (K2, N) = a.shape, (b_rows, b_cols)
    elif mode == "nt":
        (M, K), (N, K2) = a.shape, (b_rows, b_cols)
    else:
        (K, M), (K2, N) = a.shape, (b_rows, b_cols)
    assert K == K2, (name, a.shape, b.shape, mode)
    assert not (by_chip and mode == "tn") and not (out_by_chip and add is not None), name
    tm = _pick(M, tm)
    n_cut, k_cut = out_by_chip or (by_chip and mode == "nn"), by_chip and mode == "nt"
    tn = _pick(N // N_CHIPS, tn) if n_cut else _pick(N, tn)
    tk = _pick(K // N_CHIPS, tk) if k_cut else _pick(K, tk)
    nk = K // tk
    per_n = (N // N_CHIPS) // tn if n_cut else 1
    per_k = (K // N_CHIPS) // tk if k_cut else 1
    if mode == "nn":
        a_spec = pl.BlockSpec((tm, tk), lambda i, j, k: (i, k))
        b_spec = (pl.BlockSpec((None, tk, tn), lambda i, j, k: (j // per_n, k, j % per_n)) if by_chip
                  else pl.BlockSpec((tk, tn), lambda i, j, k: (k, j)))
        dims = (((1,), (0,)), ((), ()))
    elif mode == "nt":
        a_spec = pl.BlockSpec((tm, tk), lambda i, j, k: (i, k))
        b_spec = (pl.BlockSpec((None, tn, tk), lambda i, j, k: (k // per_k, j, k % per_k)) if by_chip
                  else pl.BlockSpec((tn, tk), lambda i, j, k: (j, k)))
        dims = (((1,), (1,)), ((), ()))
    else:
        a_spec = pl.BlockSpec((tk, tm), lambda i, j, k: (k, i))
        b_spec = pl.BlockSpec((tk, tn), lambda i, j, k: (k, j))
        dims = (((0,), (0,)), ((), ()))
    if out_by_chip:
        o_spec = pl.BlockSpec((None, tm, tn), lambda i, j, k: (j // per_n, i, j % per_n))
        out_shape = jax.ShapeDtypeStruct((N_CHIPS, M, N // N_CHIPS), out_dtype)
    else:
        o_spec = pl.BlockSpec((tm, tn), lambda i, j, k: (i, j))
        out_shape = jax.ShapeDtypeStruct((M, N), out_dtype)
    has_add = add is not None

    def body(*refs):
        if has_add:
            a_ref, b_ref, add_ref, o_ref, acc_ref = refs
        else:
            a_ref, b_ref, o_ref, acc_ref = refs
        k = pl.program_id(2)
        part = lax.dot_general(a_ref[...].astype(BF16), b_ref[...].astype(BF16), dims,
                               preferred_element_type=F32)

        @pl.when(k == 0)
        def _():
            acc_ref[...] = part

        @pl.when(k > 0)
        def _():
            acc_ref[...] += part

        @pl.when(k == nk - 1)
        def _():
            res = acc_ref[...]
            if has_add:
                res = res + add_ref[...].astype(F32)
            o_ref[...] = res.astype(o_ref.dtype)

    ins = [a, b] + ([add] if has_add else [])
    in_specs = [a_spec, b_spec] + ([o_spec] if has_add else [])
    return pl.pallas_call(
        body, name=name, grid=(M // tm, N // tn, nk),
        in_specs=in_specs, out_specs=o_spec, out_shape=out_shape,
        scratch_shapes=[pltpu.VMEM((tm, tn), F32)],
        compiler_params=_params(("parallel", "parallel", "arbitrary")),
    )(*ins)


def _rowwise(name, fn, T, tT, rows=(), prevs=(), nexts=(), consts=(), outs=()):
    n = T // tT
    per8 = tT // SUBLANES
    in_specs, ins = [], []
    for arr in rows:
        in_specs.append(pl.BlockSpec((tT, arr.shape[1]), lambda i: (i, 0)))
        ins.append(arr)
    for arr in prevs:
        in_specs.append(pl.BlockSpec((SUBLANES, arr.shape[1]), lambda i: (jnp.maximum(i * per8 - 1, 0), 0)))
        ins.append(arr)
    for arr in nexts:
        in_specs.append(pl.BlockSpec((SUBLANES, arr.shape[1]),
                                     lambda i: (jnp.minimum((i + 1) * per8, T // SUBLANES - 1), 0)))
        ins.append(arr)
    for arr in consts:
        in_specs.append(pl.BlockSpec(arr.shape, lambda i, nd=arr.ndim: (0,) * nd))
        ins.append(arr)
    out_specs, out_shapes = [], []
    for o in outs:
        if o[0] == "row":
            out_specs.append(pl.BlockSpec((tT, o[1]), lambda i: (i, 0)))
            out_shapes.append(jax.ShapeDtypeStruct((T, o[1]), o[2]))
        else:
            out_specs.append(pl.BlockSpec(o[1], lambda i: (0, 0)))
            out_shapes.append(jax.ShapeDtypeStruct(o[1], F32))
    nr, npv, nnx, nc = len(rows), len(prevs), len(nexts), len(consts)
    n_in = nr + npv + nnx + nc

    def body(*refs):
        i = pl.program_id(0)
        vals = [r[...] for r in refs[:n_in]]
        res = fn(i, n, vals[:nr], vals[nr:nr + npv], vals[nr + npv:nr + npv + nnx], vals[nr + npv + nnx:])
        for o, o_ref, val in zip(outs, refs[n_in:], res, strict=True):
            if o[0] == "row":
                o_ref[...] = val.astype(o_ref.dtype)
            else:
                @pl.when(i == 0)
                def _(o_ref=o_ref, val=val):
                    o_ref[...] = val.astype(F32)

                @pl.when(i > 0)
                def _(o_ref=o_ref, val=val):
                    o_ref[...] += val.astype(F32)

    res = pl.pallas_call(
        body, name=name, grid=(n,), in_specs=in_specs, out_specs=out_specs, out_shape=out_shapes,
        compiler_params=_params(("arbitrary",)),
    )(*ins)
    return list(res)


def _shift_down(x, prev8, i, s):
    rolled = pltpu.roll(x, s, 0)
    head = pltpu.roll(prev8, s, 0)
    head = jnp.where(i == 0, jnp.zeros_like(head), head)
    rid = lax.broadcasted_iota(jnp.int32, head.shape, 0)
    first = jnp.where(rid < s, head, rolled[:SUBLANES])
    if x.shape[0] == SUBLANES:
        return first
    return jnp.concatenate([first, rolled[SUBLANES:]], axis=0)


def _shift_up(x, next8, i, n, s):
    tT = x.shape[0]
    rolled = pltpu.roll(x, tT - s, 0)
    tail = pltpu.roll(next8, SUBLANES - s, 0)
    tail = jnp.where(i == n - 1, jnp.zeros_like(tail), tail)
    rid = lax.broadcasted_iota(jnp.int32, tail.shape, 0)
    last = jnp.where(rid >= SUBLANES - s, tail, rolled[tT - SUBLANES:])
    return jnp.concatenate([rolled[:tT - SUBLANES], last], axis=0)


def _colsum(x):
    return jnp.sum(x, axis=0, keepdims=True)


def _segsum(x, bd):
    return jnp.dot(x, bd, precision=lax.Precision.HIGH, preferred_element_type=F32)


def _block_diag_ones(width, seg):
    idx = np.arange(width) // seg
    return jnp.asarray((idx[:, None] == idx[None, :]).astype(np.float32))


def _sigmoid(z):
    return 1.0 / (1.0 + jnp.exp(-z))


def _softplus(z):
    return jnp.maximum(z, 0.0) + jnp.log(1.0 + jnp.exp(-jnp.abs(z)))


def _rms_fwd(x, g):
    r = lax.rsqrt(jnp.mean(x * x, axis=-1, keepdims=True) + NORM_EPS)
    return x * r * g


def _rms_bwd(x, g, dy):
    r = lax.rsqrt(jnp.mean(x * x, axis=-1, keepdims=True) + NORM_EPS)
    gdy = dy * g
    dx = r * (gdy - x * (r * r) * jnp.mean(x * gdy, axis=-1, keepdims=True))
    return dx, dy * x * r


GELU_C = math.sqrt(2.0 / math.pi)


def _gelu(x):
    return 0.5 * x * (1.0 + jnp.tanh(GELU_C * (x + 0.044715 * x * x * x)))


def _gelu_and_grad(x):
    th = jnp.tanh(GELU_C * (x + 0.044715 * x * x * x))
    half = 0.5 * (1.0 + th)
    return x * half, half + 0.5 * x * (1.0 - th * th) * GELU_C * (1.0 + 3.0 * 0.044715 * x * x)


RW_CHUNK = 64
NN = (((1,), (0,)), ((), ()))
NT = (((1,), (1,)), ((), ()))
TN = (((0,), (0,)), ((), ()))


def _hdot(a, b, dims):
    return lax.dot_general(a, b, dims, precision=lax.Precision.HIGH, preferred_element_type=F32)


def _ldot(a, b, dims):
    return lax.dot_general(a.astype(BF16), b.astype(BF16), dims, preferred_element_type=F32)


def _chunk_masks():
    ti = lax.broadcasted_iota(jnp.int32, (RW_CHUNK, RW_CHUNK), 0)
    tj = lax.broadcasted_iota(jnp.int32, (RW_CHUNK, RW_CHUNK), 1)
    return tj <= ti, tj < ti, (ti == tj).astype(F32)


def _head(x, h):
    return x[:, h * RW_HEAD_DIM:(h + 1) * RW_HEAD_DIM]


def _heads(fn):
    return [fn(h) for h in range(RW_HEADS)]


def _chunk_rows(r, lw, k, a, b, incl_f):
    c = _hdot(incl_f, lw, NN)
    e_prev, e_neg, e_pos = jnp.exp(c - lw), jnp.exp(-c), jnp.exp(c)
    return dict(At=a * e_prev, Bt=b * e_neg, Kt=k * e_neg, Rt=r * e_pos, e_prev=e_prev, e_neg=e_neg, e_pos=e_pos)


def _chunk_coeffs(q, incl, strict):
    A1 = _heads(lambda h: jnp.where(strict, _hdot(_head(q["At"], h), _head(q["Bt"], h), NT), 0.0))
    A2 = _heads(lambda h: jnp.where(strict, _hdot(_head(q["At"], h), _head(q["Kt"], h), NT), 0.0))
    W1 = _heads(lambda h: jnp.where(incl, _hdot(_head(q["Rt"], h), _head(q["Bt"], h), NT), 0.0))
    W2 = _heads(lambda h: jnp.where(incl, _ldot(_head(q["Rt"], h), _head(q["Kt"], h), NT), 0.0))
    return A1, A2, W1, W2


def _rwkv_chunk_prep(r, lw, k, a, b, v):
    T = r.shape[0]
    nC = T // RW_CHUNK
    H, N = RW_HEADS, RW_HEAD_DIM

    def body(r_ref, lw_ref, k_ref, a_ref, b_ref, v_ref,
             at_ref, bt_ref, kt_ref, rt_ref, a2v_ref, w2v_ref, ti_ref, w1_ref, a2_ref, w2_ref, pl_ref):
        incl, strict, eye = _chunk_masks()
        q = _chunk_rows(r_ref[...], lw_ref[...], k_ref[...], a_ref[...], b_ref[...], incl.astype(F32))
        at_ref[...], bt_ref[...], kt_ref[...], rt_ref[...] = q["At"], q["Bt"], q["Kt"], q["Rt"]
        pl_ref[0] = jnp.broadcast_to(q["e_pos"][RW_CHUNK - 1:RW_CHUNK, :], (SUBLANES, RW_WIDTH))
        A1, A2, W1, W2 = _chunk_coeffs(q, incl, strict)
        V = v_ref[...]
        a2v_ref[...] = jnp.concatenate(_heads(lambda h: _hdot(A2[h], _head(V, h), NN)), axis=1)
        w2v_ref[...] = jnp.concatenate(_heads(lambda h: _ldot(W2[h], _head(V, h), NN)), axis=1)
        tinv, pw = [eye + m for m in A1], A1
        for stage in range(5):
            dot = _hdot if stage == 0 else _ldot
            pw = [dot(m, m, NN) for m in pw]
            tinv = [t + dot(t, m, NN) for t, m in zip(tinv, pw, strict=True)]
        for h in range(H):
            ti_ref[0, h] = tinv[h]
            w1_ref[0, h] = W1[h]
            a2_ref[0, h] = A2[h]
            w2_ref[0, h] = W2[h]

    row_spec = pl.BlockSpec((RW_CHUNK, RW_WIDTH), lambda n: (n, 0))
    st_spec = pl.BlockSpec((1, H, N, N), lambda n: (n, 0, 0, 0))
    row_shape = jax.ShapeDtypeStruct((T, RW_WIDTH), F32)
    st_shape = jax.ShapeDtypeStruct((nC, H, N, N), F32)
    return pl.pallas_call(
        body, name="rwkv_chunk_prep", grid=(nC,),
        in_specs=[row_spec] * 6,
        out_specs=[row_spec] * 6 + [st_spec] * 4 + [pl.BlockSpec((1, SUBLANES, RW_WIDTH), lambda n: (n, 0, 0))],
        out_shape=[row_shape] * 6 + [st_shape] * 4 + [jax.ShapeDtypeStruct((nC, SUBLANES, RW_WIDTH), F32)],
        compiler_params=_params(("parallel",)),
    )(r, lw, k, a, b, v)


def _rwkv_chunk_fwd(v, at, bt, kt, rt, a2v, w2v, tinv, w1, plast):
    T = v.shape[0]
    nC = T // RW_CHUNK
    H, N = RW_HEADS, RW_HEAD_DIM

    def body(v_ref, at_ref, bt_ref, kt_ref, rt_ref, a2v_ref, w2v_ref, ti_ref, w1_ref, pl_ref,
             y_ref, sa_ref, s0_ref, S_ref):
        @pl.when(pl.program_id(0) == 0)
        def _():
            S_ref[...] = jnp.zeros_like(S_ref)

        V, At, Bt, Kt, Rt = v_ref[...], at_ref[...], bt_ref[...], kt_ref[...], rt_ref[...]
        A2V, W2V, p_last = a2v_ref[...], w2v_ref[...], pl_ref[0, 0:1, :]
        S0 = _heads(lambda h: S_ref[h])
        for h in range(H):
            s0_ref[0, h] = S0[h]
        Z = _heads(lambda h: _hdot(_head(At, h), S0[h], NT) + _head(A2V, h))
        Sa = _heads(lambda h: _hdot(ti_ref[0, h], Z[h], NN))
        X = _heads(lambda h: S0[h] + _hdot(Sa[h], _head(Bt, h), TN) + _hdot(_head(V, h), _head(Kt, h), TN))
        for h in range(H):
            S_ref[h] = X[h] * _head(p_last, h)
        Y = _heads(lambda h: _ldot(_head(Rt, h), S0[h], NT) + _ldot(w1_ref[0, h], Sa[h], NN) + _head(W2V, h))
        y_ref[...] = jnp.concatenate(Y, axis=1)
        sa_ref[...] = jnp.concatenate(Sa, axis=1)

    row_spec = pl.BlockSpec((RW_CHUNK, RW_WIDTH), lambda n: (n, 0))
    st_spec = pl.BlockSpec((1, H, N, N), lambda n: (n, 0, 0, 0))
    row_shape = jax.ShapeDtypeStruct((T, RW_WIDTH), F32)
    return pl.pallas_call(
        body, name="rwkv_chunk_fwd", grid=(nC,),
        in_specs=[row_spec] * 7 + [st_spec, st_spec, pl.BlockSpec((1, SUBLANES, RW_WIDTH), lambda n: (n, 0, 0))],
        out_specs=[row_spec, row_spec, st_spec],
        out_shape=[row_shape, row_shape, jax.ShapeDtypeStruct((nC, H, N, N), F32)],
        scratch_shapes=[pltpu.VMEM((H, N, N), F32)],
        compiler_params=_params(("arbitrary",)),
    )(v, at, bt, kt, rt, a2v, w2v, tinv, w1, plast)


def _rwkv_chunk_bwd(r, lw, k, a, b, v, dy, s0, tinv, w1, a2, w2, sa):
    T = r.shape[0]
    nC = T // RW_CHUNK
    H, N = RW_HEADS, RW_HEAD_DIM

    def body(r_ref, lw_ref, k_ref, a_ref, b_ref, v_ref, dy_ref, s0_ref, ti_ref, w1_ref, a2_ref, w2_ref, sa_ref,
             dr_ref, dlw_ref, dk_ref, da_ref, db_ref, dv_ref, dS_ref):
        @pl.when(pl.program_id(0) == 0)
        def _():
            dS_ref[...] = jnp.zeros_like(dS_ref)

        incl, strict, _ = _chunk_masks()
        incl_f = incl.astype(F32)
        q = _chunk_rows(r_ref[...], lw_ref[...], k_ref[...], a_ref[...], b_ref[...], incl_f)
        At, Bt, Kt, Rt = q["At"], q["Bt"], q["Kt"], q["Rt"]
        A2, W1, W2 = (_heads(lambda h, ref=ref: ref[0, h]) for ref in (a2_ref, w1_ref, w2_ref))
        V, dY, Sa = v_ref[...], dy_ref[...], sa_ref[...]
        hd = _head
        p_last = q["e_pos"][RW_CHUNK - 1:RW_CHUNK, :]
        S0 = _heads(lambda h: s0_ref[0, h])
        G = _heads(lambda h: dS_ref[h] * hd(p_last, h))
        X = _heads(lambda h: S0[h] + _hdot(hd(Sa, h), hd(Bt, h), TN) + _hdot(hd(V, h), hd(Kt, h), TN))
        dc_last = jnp.concatenate(_heads(lambda h: jnp.sum(G[h] * X[h], axis=0, keepdims=True)), axis=1)
        dSa = _heads(lambda h: _hdot(hd(Bt, h), G[h], NT) + _hdot(W1[h], hd(dY, h), TN))
        dZ = _heads(lambda h: _hdot(ti_ref[0, h], dSa[h], TN))
        for h in range(H):
            dS_ref[h] = G[h] + _hdot(dZ[h], hd(At, h), TN) + _hdot(hd(dY, h), hd(Rt, h), TN)
        dA1 = _heads(lambda h: jnp.where(strict, _ldot(dZ[h], hd(Sa, h), NT), 0.0))
        dA2 = _heads(lambda h: jnp.where(strict, _ldot(dZ[h], hd(V, h), NT), 0.0))
        dW1 = _heads(lambda h: jnp.where(incl, _ldot(hd(dY, h), hd(Sa, h), NT), 0.0))
        dW2 = _heads(lambda h: jnp.where(incl, _ldot(hd(dY, h), hd(V, h), NT), 0.0))
        cat = lambda fn: jnp.concatenate(_heads(fn), axis=1)
        dV = cat(lambda h: _ldot(A2[h], dZ[h], TN) + _ldot(W2[h], hd(dY, h), TN) + _ldot(hd(Kt, h), G[h], NT))
        dAt = cat(lambda h: _ldot(dA1[h], hd(Bt, h), NN) + _ldot(dA2[h], hd(Kt, h), NN) + _ldot(dZ[h], S0[h], NN))
        dBt = cat(lambda h: _ldot(dA1[h], hd(At, h), TN) + _ldot(dW1[h], hd(Rt, h), TN) + _ldot(hd(Sa, h), G[h], NN))
        dKt = cat(lambda h: _ldot(dA2[h], hd(At, h), TN) + _ldot(dW2[h], hd(Rt, h), TN) + _ldot(hd(V, h), G[h], NN))
        dRt = cat(lambda h: _ldot(hd(dY, h), S0[h], NN) + _ldot(dW1[h], hd(Bt, h), NN) + _ldot(dW2[h], hd(Kt, h), NN))
        last_row = lax.broadcasted_iota(jnp.int32, (RW_CHUNK, RW_WIDTH), 0) == RW_CHUNK - 1
        dc_prev = dAt * At
        dc = dc_prev + dRt * Rt - dBt * Bt - dKt * Kt + jnp.where(last_row, dc_last, 0.0)
        dr_ref[...] = dRt * q["e_pos"]
        dlw_ref[...] = _hdot(incl_f, dc, TN) - dc_prev
        dk_ref[...] = dKt * q["e_neg"]
        da_ref[...] = dAt * q["e_prev"]
        db_ref[...] = dBt * q["e_neg"]
        dv_ref[...] = dV

    rev = lambda n: nC - 1 - n
    row_spec = pl.BlockSpec((RW_CHUNK, RW_WIDTH), lambda n: (rev(n), 0))
    st_spec = pl.BlockSpec((1, H, N, N), lambda n: (rev(n), 0, 0, 0))
    row_shape = jax.ShapeDtypeStruct((T, RW_WIDTH), F32)
    return pl.pallas_call(
        body, name="rwkv_chunk_bwd", grid=(nC,),
        in_specs=[row_spec] * 7 + [st_spec] * 5 + [row_spec], out_specs=[row_spec] * 6,
        out_shape=[row_shape] * 6, scratch_shapes=[pltpu.VMEM((H, N, N), F32)],
        compiler_params=_params(("arbitrary",)),
    )(r, lw, k, a, b, v, dy, s0, tinv, w1, a2, w2, sa)


def _alibi_slope(head):
    return float(np.float32(2.0 ** (-8.0 * (head + 1) / ATT_HEADS)))


ATT_SPAN = ATT_BLOCK * max(ATT_GROUP_DILATION)
ATT_PAIR_WIDTH = 2 * ATT_HEAD_DIM
ATT_SIDE_BY_SIDE = 8


def _pair_slope(g, hp, j):
    return jnp.where(hp == 0, _alibi_slope(4 * g + j), _alibi_slope(4 * g + 2 + j))


def _att_rows(mi, r, d):
    start = mi * ATT_BLOCK * d + r
    return pl.ds(start, ATT_BLOCK) if d == 1 else pl.ds(start, ATT_BLOCK, stride=d)


def _att_masks():
    qi = lax.broadcasted_iota(jnp.int32, (ATT_BLOCK, ATT_BLOCK), 0)
    kj = lax.broadcasted_iota(jnp.int32, (ATT_BLOCK, ATT_BLOCK), 1)
    return qi, kj


NEG = -1e30


def _att_logits(q, k, slope_d, steps, valid):
    s = lax.dot_general(q.astype(BF16), k.astype(BF16), (((1,), (1,)), ((), ())),
                        preferred_element_type=F32) * (ATT_HEAD_DIM ** -0.5)
    return jnp.where(valid, s - slope_d * steps.astype(F32), NEG)


def _att_fwd(p_att, g):
    T = p_att.shape[0]
    d = ATT_GROUP_DILATION[g]
    W = ATT_PAIR_WIDTH
    nb = T // ATT_SPAN
    mb = ATT_SPAN // (ATT_BLOCK * d)

    def body(q_ref, kc_ref, kp_ref, vc_ref, vp_ref, o_ref, l_ref):
        hp, n = pl.program_id(0), pl.program_id(1)
        qi, kj = _att_masks()
        slopes = [_pair_slope(g, hp, j) * d for j in range(2)]
        blocks = [(r, mi) for r in range(d) for mi in range(mb)]
        for at in range(0, len(blocks), ATT_SIDE_BY_SIDE):
            tasks = []
            for r, mi in blocks[at:at + ATT_SIDE_BY_SIDE]:
                rows = _att_rows(mi, r, d)
                if mi > 0:
                    prev = _att_rows(mi - 1, r, d)
                    kp, vp, has_prev = kc_ref[prev, :], vc_ref[prev, :], True
                else:
                    prev = _att_rows(mb - 1, r, d)
                    kp, vp, has_prev = kp_ref[prev, :], vp_ref[prev, :], n > 0
                q, kc, vc = q_ref[rows, :], kc_ref[rows, :], vc_ref[rows, :]
                for j in range(2):
                    sl = slice(j * ATT_HEAD_DIM, (j + 1) * ATT_HEAD_DIM)
                    tasks.append((q[:, sl], kc[:, sl], kp[:, sl], vc[:, sl], vp[:, sl], has_prev, slopes[j]))
            lc = [_att_logits(t[0], t[1], t[6], qi - kj, kj <= qi) for t in tasks]
            lp = [_att_logits(t[0], t[2], t[6], qi - kj + ATT_BLOCK, (kj >= qi) & t[5]) for t in tasks]
            mx = [jnp.maximum(jnp.max(a, axis=1, keepdims=True), jnp.max(b, axis=1, keepdims=True))
                  for a, b in zip(lc, lp, strict=True)]
            ec = [jnp.exp(a - m) for a, m in zip(lc, mx, strict=True)]
            ep = [jnp.exp(b - m) for b, m in zip(lp, mx, strict=True)]
            den = [jnp.sum(a, axis=1, keepdims=True) + jnp.sum(b, axis=1, keepdims=True)
                   for a, b in zip(ec, ep, strict=True)]
            inv = [1.0 / s for s in den]
            outs = [jnp.dot((a * i).astype(BF16), t[3].astype(BF16), preferred_element_type=F32)
                    + jnp.dot((b * i).astype(BF16), t[4].astype(BF16), preferred_element_type=F32)
                    for a, b, i, t in zip(ec, ep, inv, tasks, strict=True)]
            lses = [jnp.broadcast_to(m + jnp.log(s), (ATT_BLOCK, ATT_HEAD_DIM)) for m, s in zip(mx, den, strict=True)]
            for i, (r, mi) in enumerate(blocks[at:at + ATT_SIDE_BY_SIDE]):
                rows = _att_rows(mi, r, d)
                o_ref[rows, :] = jnp.concatenate(outs[2 * i:2 * i + 2], axis=1)
                l_ref[rows, :] = jnp.concatenate(lses[2 * i:2 * i + 2], axis=1)

    def spec(col0, prev):
        if prev:
            return pl.BlockSpec((ATT_SPAN, W), lambda hp, n: (jnp.maximum(n - 1, 0), col0 + 2 * g + hp))
        return pl.BlockSpec((ATT_SPAN, W), lambda hp, n: (n, col0 + 2 * g + hp))

    o_spec = pl.BlockSpec((ATT_SPAN, W), lambda hp, n: (n, hp))
    o, l = pl.pallas_call(
        body, name=f"att_fwd_g{g}", grid=(2, nb),
        in_specs=[spec(0, False), spec(6, False), spec(6, True), spec(12, False), spec(12, True)],
        out_specs=[o_spec, o_spec],
        out_shape=[jax.ShapeDtypeStruct((T, ATT_GROUP_WIDTH), F32)] * 2,
        compiler_params=_params(("parallel", "arbitrary")),
    )(p_att, p_att, p_att, p_att, p_att)
    return o, l


def _att_bwd(p_att, o, l, do, dl, g):
    T = p_att.shape[0]
    d = ATT_GROUP_DILATION[g]
    W = ATT_PAIR_WIDTH
    nb = T // ATT_SPAN
    mb = ATT_SPAN // (ATT_BLOCK * d)
    scale = ATT_HEAD_DIM ** -0.5

    def body(q_ref, k_ref, v_ref, o_ref, l_ref, do_ref, dl_ref,
             qn_ref, on_ref, ln_ref, don_ref, dln_ref, dq_ref, dk_ref, dv_ref, carry_ref):
        hp, n = pl.program_id(0), pl.program_id(1)
        qi, kj = _att_masks()

        @pl.when(n == 0)
        def _():
            carry_ref[...] = jnp.zeros_like(carry_ref)

        slopes = [_pair_slope(g, hp, j) * d for j in range(2)]
        blocks = [(r, mi) for r in range(d) for mi in range(mb)]
        side_by_side = ATT_SIDE_BY_SIDE // 2
        carry = None
        for at in range(0, len(blocks), side_by_side):
            tasks = []
            for r, mi in blocks[at:at + side_by_side]:
                rows = _att_rows(mi, r, d)
                if mi < mb - 1:
                    nrows = _att_rows(mi + 1, r, d)
                    nxt = (q_ref[nrows, :], o_ref[nrows, :], l_ref[nrows, :], do_ref[nrows, :], dl_ref[nrows, :])
                    has_next = True
                else:
                    nrows = _att_rows(0, r, d)
                    nxt = (qn_ref[nrows, :], on_ref[nrows, :], ln_ref[nrows, :], don_ref[nrows, :],
                           dln_ref[nrows, :])
                    has_next = n < nb - 1
                cur = (q_ref[rows, :], o_ref[rows, :], l_ref[rows, :], do_ref[rows, :], dl_ref[rows, :])
                k_all, v_all = k_ref[rows, :], v_ref[rows, :]
                for j in range(2):
                    sl = slice(j * ATT_HEAD_DIM, (j + 1) * ATT_HEAD_DIM)
                    for blk, steps, valid in ((cur, qi - kj, kj <= qi),
                                              (nxt, qi - kj + ATT_BLOCK, (kj >= qi) & has_next)):
                        q, o_, lse, do_, dlse = (z[:, sl] for z in blk)
                        tasks.append(dict(q=q, o=o_, lse=lse[:, :1], do=do_, dlse=dlse[:, :1], steps=steps,
                                          valid=valid, k=k_all[:, sl], vb=v_all[:, sl].astype(BF16),
                                          slope=slopes[j]))
            p = [jnp.exp(_att_logits(t["q"], t["k"], t["slope"], t["steps"], t["valid"]) - t["lse"]) for t in tasks]
            dp = [lax.dot_general(t["do"].astype(BF16), t["vb"], (((1,), (1,)), ((), ())),
                                  preferred_element_type=F32) for t in tasks]
            dsum = [jnp.sum(t["do"] * t["o"], axis=1, keepdims=True) for t in tasks]
            ds = [a * (b - s + t["dlse"]) for a, b, s, t in zip(p, dp, dsum, tasks, strict=True)]
            dv_ = [jnp.dot(a.T.astype(BF16), t["do"].astype(BF16), preferred_element_type=F32)
                   for a, t in zip(p, tasks, strict=True)]
            dk_ = [jnp.dot(a.T.astype(BF16), t["q"].astype(BF16), preferred_element_type=F32) * scale
                   for a, t in zip(ds, tasks, strict=True)]
            dq_ = [jnp.dot(a.astype(BF16), t["k"].astype(BF16), preferred_element_type=F32) * scale
                   for a, t in zip(ds, tasks, strict=True)]
            for i, (r, mi) in enumerate(blocks[at:at + side_by_side]):
                rows = _att_rows(mi, r, d)
                b = 4 * i
                if mi == 0:
                    carry = carry_ref[r]
                dq_ref[rows, :] = jnp.concatenate([dq_[b], dq_[b + 2]], axis=1) + carry
                carry = jnp.concatenate([dq_[b + 1], dq_[b + 3]], axis=1)
                if mi == mb - 1:
                    carry_ref[r] = carry
                dk_ref[rows, :] = jnp.concatenate([dk_[b] + dk_[b + 1], dk_[b + 2] + dk_[b + 3]], axis=1)
                dv_ref[rows, :] = jnp.concatenate([dv_[b] + dv_[b + 1], dv_[b + 2] + dv_[b + 3]], axis=1)

    head_rows = ATT_BLOCK * d
    nxt_n = lambda n: jnp.minimum((n + 1) * mb, T // head_rows - 1)
    cur_p = lambda col0: pl.BlockSpec((ATT_SPAN, W), lambda hp, n: (n, col0 + 2 * g + hp))
    cur_o = pl.BlockSpec((ATT_SPAN, W), lambda hp, n: (n, hp))
    nxt_o = pl.BlockSpec((head_rows, W), lambda hp, n: (nxt_n(n), hp))
    dq, dk, dv = pl.pallas_call(
        body, name=f"att_bwd_g{g}", grid=(2, nb),
        in_specs=[cur_p(0), cur_p(6), cur_p(12), cur_o, cur_o, cur_o, cur_o,
                  pl.BlockSpec((head_rows, W), lambda hp, n: (nxt_n(n), 2 * g + hp)), nxt_o, nxt_o, nxt_o, nxt_o],
        out_specs=[cur_o, cur_o, cur_o],
        out_shape=[jax.ShapeDtypeStruct((T, ATT_GROUP_WIDTH), F32)] * 3,
        scratch_shapes=[pltpu.VMEM((d, ATT_BLOCK, W), F32)],
        compiler_params=_params(("parallel", "arbitrary")),
    )(p_att, p_att, p_att, o, l, do, dl, p_att, o, l, do, dl)
    return dq, dk, dv


FFN_TILE = 2 * D_FF // N_CHIPS
RKV = 3 * RW_WIDTH
WA = 128
XG = 160
RW_COLS = RKV + WA + XG


def _local_step(x, p, W, target, late_weights=None, early_grads=None, by_chip=False, grad_dtype=F32):
    T = x.shape[0]
    tT = 256
    bd512 = _block_diag_ones(RW_WIDTH, RW_HEAD_DIM)
    bd256 = _block_diag_ones(ATT_GROUP_WIDTH, ATT_HEAD_DIM)
    G = {}
    W = dict(W)

    w_in = W["w_in"]
    w_rkv, w_wa, w_xg, w_att = (w_in[:, :RKV], w_in[:, RKV:RKV + WA], w_in[:, RKV + WA:RW_COLS],
                                w_in[:, RW_COLS:])
    mu = W["rw_mu"]
    mu_rkv, mu_wa, mu_xg = mu[:, :RKV], mu[:, RKV:RKV + WA], mu[:, RKV + WA:]
    zpad = jnp.zeros((64, RW_WIDTH), W["rw_w_up"].dtype)
    w_up_pad = jnp.concatenate([W["rw_w_up"], zpad], axis=0)
    a_up_pad = jnp.concatenate([zpad, W["rw_a_up"]], axis=0)
    r_k = W["rw_r_k"].reshape(1, RW_WIDTH)

    (h,) = _rowwise("norm_mix", lambda i, n, r, pv, nx, c: [_rms_fwd(r[0], c[0])], T, tT,
                    rows=[x], consts=[W["g_mix"]], outs=[("row", D_MODEL, BF16)])
    p_rkv = _mm("proj_rkv", h, w_rkv, "nn")
    p_wa = _mm("proj_wa", h, w_wa, "nn")
    p_xg = _mm("proj_xg", h, w_xg, "nn")
    p_att = _mm("proj_att", h, w_att, "nn", tn=768)
    z_gate = _mm("proj_gate", h, W["w_gate"], "nn")

    def rw_pre_core(i, rows, prevs, consts):
        prkv, pwa, pxg = rows[:3]
        (mrkv, mwa, mxg, w0, a0, k_k, k_a, wup, aup, gup, bd) = consts[:11]
        m_rkv = prkv + (_shift_down(prkv, prevs[0], i, 1) - prkv) * mrkv
        m_wa = pwa + (_shift_down(pwa, prevs[1], i, 1) - pwa) * mwa
        m_xg = pxg + (_shift_down(pxg, prevs[2], i, 1) - pxg) * mxg
        r, k, v = m_rkv[:, :RW_WIDTH], m_rkv[:, RW_WIDTH:2 * RW_WIDTH], m_rkv[:, 2 * RW_WIDTH:]
        tw = jnp.tanh(m_wa)
        lw = w0 + jnp.dot(tw.astype(BF16), wup.astype(BF16), preferred_element_type=F32)
        wlog = -_softplus(-lw) - 0.5
        log_decay = -jnp.exp(wlog)
        a = _sigmoid(a0 + jnp.dot(m_wa.astype(BF16), aup.astype(BF16), preferred_element_type=F32))
        sg = _sigmoid(m_xg)
        gate = jnp.dot(sg.astype(BF16), gup.astype(BF16), preferred_element_type=F32)
        kkp = k * k_k
        nrm = jnp.sqrt(_segsum(kkp * kkp, bd))
        nrm_c = jnp.maximum(nrm, 1e-12)
        kk = kkp / nrm_c
        k2 = k * (1.0 + (a - 1.0) * k_a)
        return dict(r=r, k=k, v=v, tw=tw, lw=lw, wlog=wlog, log_decay=log_decay, a=a, sg=sg, gate=gate, kkp=kkp,
                    nrm=nrm, nrm_c=nrm_c, kk=kk, k2=k2, m_rkv=m_rkv, m_wa=m_wa, m_xg=m_xg)

    pre_consts = [mu_rkv, mu_wa, mu_xg, W["rw_w0"], W["rw_a0"], W["rw_k_k"], W["rw_k_a"],
                  w_up_pad, a_up_pad, W["rw_g_up"], bd512]

    def rw_pre(i, n, rows, prevs, nexts, consts):
        q = rw_pre_core(i, rows, prevs, consts)
        return [q["r"], q["log_decay"], q["k2"], q["v"], -q["kk"], q["kk"] * q["a"], q["gate"]]

    r_s, w_s, k_s, v_s, a_s, b_s, gate_s = _rowwise(
        "rwkv_pre", rw_pre, T, tT, rows=[p_rkv, p_wa, p_xg], prevs=[p_rkv, p_wa, p_xg], consts=pre_consts,
        outs=[("row", RW_WIDTH, F32)] * 7)
    (at_s, bt_s, kt_s, rt_s, a2v_s, w2v_s, tinv_s, w1_s, a2_s, w2_s,
     plast_s) = _rwkv_chunk_prep(r_s, w_s, k_s, a_s, b_s, v_s)
    y_scan, sa_s, s0_s = _rwkv_chunk_fwd(v_s, at_s, bt_s, kt_s, rt_s, a2v_s, w2v_s, tinv_s, w1_s, plast_s)

    def rw_post_core(rows, consts):
        y, r, k2, v, gate = rows[:5]
        ln_g, ln_b, rk, bd = consts[:4]
        mean = _segsum(y, bd) * (1.0 / RW_HEAD_DIM)
        yc = y - mean
        var = _segsum(yc * yc, bd) * (1.0 / RW_HEAD_DIM)
        rstd = lax.rsqrt(var + RW_LN_EPS)
        yn = yc * rstd
        s = _segsum(r * k2 * rk, bd)
        return dict(yn=yn, rstd=rstd, s=s, pre=yn * ln_g + ln_b + s * v)

    post_consts = [W["rw_ln_g"], W["rw_ln_b"], r_k, bd512]
    (y_a,) = _rowwise("rwkv_post", lambda i, n, r, pv, nx, c: [rw_post_core(r, c)["pre"] * r[4]], T, tT,
                      rows=[y_scan, r_s, k_s, v_s, gate_s], consts=post_consts, outs=[("row", RW_WIDTH, BF16)])

    att = [_att_fwd(p_att, g) for g in range(3)]

    def comb_weights(ls):
        mx = jnp.maximum(jnp.maximum(ls[0], ls[1]), ls[2])
        es = [jnp.exp(l - mx) for l in ls]
        den = es[0] + es[1] + es[2]
        return [e / den for e in es]

    def att_comb(i, n, rows, pv, nx, c):
        wts = comb_weights(rows[3:6])
        return [wts[0] * rows[0] + wts[1] * rows[1] + wts[2] * rows[2]]

    (y_b,) = _rowwise("att_combine", att_comb, T, tT, rows=[att[0][0], att[1][0], att[2][0], att[0][1], att[1][1],
                                                            att[2][1]], outs=[("row", ATT_GROUP_WIDTH, BF16)])

    if late_weights is not None:
        W.update(late_weights(y_b))
    br_a = _mm("branch_a", y_a, W["w_branch_a"], "nn")
    br_b = _mm("branch_b", y_b, W["w_branch_b"], "nn")

    def merge(i, n, rows, pv, nx, c):
        gates = _sigmoid(rows[0] + c[0])
        return [gates[:, :D_MODEL] * rows[1] + gates[:, D_MODEL:] * rows[2]]

    (merged,) = _rowwise("merge", merge, T, tT, rows=[z_gate, br_a, br_b], consts=[W["b_gate"]],
                         outs=[("row", D_MODEL, BF16)])
    x1 = _mm("mix_out", merged, W["w_out"], "nn", add=x)

    (h2,) = _rowwise("norm_ffn", lambda i, n, r, pv, nx, c: [_rms_fwd(r[0], c[0])], T, tT,
                     rows=[x1], consts=[W["g_ffn"]], outs=[("row", D_MODEL, BF16)])
    u = _mm("ffn_up", h2, W["w_up"], "nn", tn=FFN_TILE)

    def conv_core(i, rows, prevs, consts):
        uu, cw, cb = rows[0], consts[0], consts[1]
        u1 = _shift_down(uu, prevs[0], i, 1)
        u2 = _shift_down(uu, prevs[0], i, 2)
        uc = cb + cw[0:1] * uu + cw[1:2] * u1 + cw[2:3] * u2
        return uc[:, :D_FF], uc[:, D_FF:], u1, u2

    def glu(i, n, rows, prevs, nx, consts):
        gate, val, _, _ = conv_core(i, rows, prevs, consts)
        return [_gelu(gate) * val]

    tF = 128
    (act,) = _rowwise("conv_glu", glu, T, tF, rows=[u], prevs=[u], consts=[W["conv_w"], W["conv_b"]],
                      outs=[("row", D_FF, BF16)])
    x2 = _mm("ffn_down", act, W["w_down"], "nn", add=x1)

    (h3,) = _rowwise("norm_ple", lambda i, n, r, pv, nx, c: [_rms_fwd(r[0], c[0])], T, tT,
                     rows=[x2], consts=[W["g_ple"]], outs=[("row", D_MODEL, BF16)])
    z_ple = _mm("ple_gate", h3, W["w_ple_gate"], "nn")
    e_ple = _mm("ple_emb", p, W["w_ple"], "nn")

    def head(i, n, rows, pv, nx, consts):
        x2_, z, e, tgt = rows
        pg = _sigmoid(z)
        x3 = x2_ + pg * e
        y = _rms_fwd(x3, consts[0])
        err = y - tgt
        loss = 0.5 * jnp.sum(jnp.sum(err * err, axis=1, keepdims=True) * (1.0 / D_MODEL), axis=0, keepdims=True)
        dy = err * (1.0 / D_MODEL)
        dx3, dgf = _rms_bwd(x3, consts[0], dy)
        return [dx3, dx3 * pg, dx3 * e * pg * (1.0 - pg), jnp.broadcast_to(loss, (1, LANES)), _colsum(dgf)]

    dx3, de, dz, loss_acc, G["g_final"] = _rowwise(
        "loss_head", head, T, tT, rows=[x2, z_ple, e_ple, target], consts=[W["g_final"].reshape(1, D_MODEL)],
        outs=[("row", D_MODEL, F32), ("row", D_MODEL, BF16), ("row", D_MODEL, BF16), ("acc", (1, LANES)),
              ("acc", (1, D_MODEL))])
    G["w_ple"] = _mm("d_w_ple", p, de, "tn", grad_dtype, out_by_chip=by_chip)
    G["w_ple_gate"] = _mm("d_w_ple_gate", h3, dz, "tn", grad_dtype)
    dh3 = _mm("d_h3", dz, W["w_ple_gate"], "nt")

    def norm_bwd(i, n, rows, pv, nx, consts):
        dx, dg = _rms_bwd(rows[0], consts[0], rows[1])
        return [rows[2] + dx, _colsum(dg)]

    dx2, G["g_ple"] = _rowwise("d_norm_ple", norm_bwd, T, tT, rows=[x2, dh3, dx3], consts=[W["g_ple"]],
                               outs=[("row", D_MODEL, F32), ("acc", (1, D_MODEL))])

    dact = _mm("d_act", dx2, W["w_down"], "nt")
    G["w_down"] = _mm("d_w_down", act, dx2, "tn", grad_dtype)

    def glu_grad(gate, val, da):
        act_, slope = _gelu_and_grad(gate)
        return jnp.concatenate([da * val * slope, da * act_], axis=1)

    def glu_bwd(i, n, rows, prevs, nexts, consts):
        uu, da = rows
        cw = consts[0]
        gate, val, u1, u2 = conv_core(i, rows, prevs, consts)
        duc = glu_grad(gate, val, da)
        dcw = jnp.concatenate([_colsum(duc * uu), _colsum(duc * u1), _colsum(duc * u2)], axis=0)
        gate_n, val_n, _, _ = conv_core(1, [nexts[0]], [uu[tF - SUBLANES:]], consts)
        duc_n = glu_grad(gate_n, val_n, nexts[1])
        du = (cw[0:1] * duc + cw[1:2] * _shift_up(duc, duc_n, i, n, 1) + cw[2:3] * _shift_up(duc, duc_n, i, n, 2))
        return [du, _colsum(duc), dcw]

    du, G["conv_b"], G["conv_w"] = _rowwise(
        "d_conv_glu", glu_bwd, T, tF, rows=[u, dact], prevs=[u], nexts=[u, dact],
        consts=[W["conv_w"], W["conv_b"]],
        outs=[("row", 2 * D_FF, BF16), ("acc", (1, 2 * D_FF)), ("acc", (3, 2 * D_FF))])
    G["w_up"] = _mm("d_w_up", h2, du, "tn", grad_dtype, out_by_chip=by_chip, tn=FFN_TILE)
    dh2 = _mm("d_h2", du, W["w_up"], "nt", tk=FFN_TILE)
    dx1, G["g_ffn"] = _rowwise("d_norm_ffn", norm_bwd, T, tT, rows=[x1, dh2, dx2], consts=[W["g_ffn"]],
                               outs=[("row", D_MODEL, F32), ("acc", (1, D_MODEL))])

    b_gate = W["b_gate"]
    if early_grads is not None:
        b_gate = b_gate + early_grads(G, 0)[0:1, 0:1]
    dmerged = _mm("d_merged", dx1, W["w_out"], "nt")
    G["w_out"] = _mm("d_w_out", merged, dx1, "tn", grad_dtype)

    def merge_bwd(i, n, rows, pv, nx, consts):
        z, a_, b_, dm = rows
        gates = _sigmoid(z + consts[0])
        ga, gb = gates[:, :D_MODEL], gates[:, D_MODEL:]
        dz_ = jnp.concatenate([dm * a_ * ga * (1.0 - ga), dm * b_ * gb * (1.0 - gb)], axis=1)
        return [dm * ga, dm * gb, dz_, _colsum(dz_)]

    d_br_a, d_br_b, dz_gate, G["b_gate"] = _rowwise(
        "d_merge", merge_bwd, T, tT, rows=[z_gate, br_a, br_b, dmerged], consts=[b_gate],
        outs=[("row", D_MODEL, BF16), ("row", D_MODEL, BF16), ("row", 2 * D_MODEL, BF16), ("acc", (1, 2 * D_MODEL))])
    G["w_branch_a"] = _mm("d_w_branch_a", y_a, d_br_a, "tn", grad_dtype, out_by_chip=by_chip)
    G["w_branch_b"] = _mm("d_w_branch_b", y_b, d_br_b, "tn", grad_dtype, out_by_chip=by_chip)
    G["w_gate"] = _mm("d_w_gate", h, dz_gate, "tn", grad_dtype, out_by_chip=by_chip)
    if early_grads is not None:
        post_consts = [post_consts[0] + early_grads(G, 1)[0:1, 0:1]] + post_consts[1:]
    dy_a = _mm("d_y_a", d_br_a, W["w_branch_a"], "nt")
    dy_b = _mm("d_y_b", d_br_b, W["w_branch_b"], "nt")

    def att_comb_bwd(i, n, rows, pv, nx, consts):
        os_, ls, dy = rows[0:3], rows[3:6], rows[6]
        wts = comb_weights(ls)
        dws = [_segsum(dy * o_, consts[0]) for o_ in os_]
        mix = wts[0] * dws[0] + wts[1] * dws[1] + wts[2] * dws[2]
        return [wts[g_] * dy for g_ in range(3)] + [wts[g_] * (dws[g_] - mix) for g_ in range(3)]

    comb = _rowwise("d_att_combine", att_comb_bwd, T, tT,
                    rows=[att[0][0], att[1][0], att[2][0], att[0][1], att[1][1], att[2][1], dy_b], consts=[bd256],
                    outs=[("row", ATT_GROUP_WIDTH, F32)] * 6)
    dqkv = [_att_bwd(p_att, att[g][0], att[g][1], comb[g], comb[3 + g], g) for g in range(3)]
    dp_att = jnp.concatenate([dqkv[g][part] for part in range(3) for g in range(3)], axis=1).astype(BF16)

    def rw_post_bwd(i, n, rows, pv, nx, consts):
        y, r, k2, v, gate, dya = rows
        ln_g, ln_b, rk, bd = consts
        q = rw_post_core(rows, consts)
        dpre = dya * gate
        dgate = dya * q["pre"]
        dyn = dpre * ln_g
        inv = 1.0 / RW_HEAD_DIM
        dy_scan = q["rstd"] * (dyn - _segsum(dyn, bd) * inv - q["yn"] * (_segsum(dyn * q["yn"], bd) * inv))
        ds = _segsum(dpre * v, bd)
        return [dy_scan, dgate, ds * k2 * rk, ds * r * rk, dpre * q["s"],
                _colsum(dpre * q["yn"]), _colsum(dpre), _colsum(ds * r * k2)]

    dy_scan, dgate, dr_b, dk2_b, dv_b, G["rw_ln_g"], G["rw_ln_b"], d_rk = _rowwise(
        "d_rwkv_post", rw_post_bwd, T, tT, rows=[y_scan, r_s, k_s, v_s, gate_s, dy_a], consts=post_consts,
        outs=[("row", RW_WIDTH, F32)] * 5 + [("acc", (1, RW_WIDTH))] * 3)
    G["rw_r_k"] = d_rk.reshape(RW_HEADS, RW_HEAD_DIM)

    dr_s, dw_s, dk_s, da_s, db_s, dv_s = _rwkv_chunk_bwd(r_s, w_s, k_s, a_s, b_s, v_s, dy_scan, s0_s, tinv_s, w1_s,
                                                         a2_s, w2_s, sa_s)

    def rw_pre_bwd(i, n, rows, prevs, nx, consts):
        q = rw_pre_core(i, rows, prevs, consts)
        (mrkv, mwa, mxg, w0, a0, k_k, k_a, wup, aup, gup, bd) = consts
        dr, dlogdecay, dk2, dv, dav, dbv, dgate_ = rows[3:10]
        dr = dr + rows[10]
        dk2 = dk2 + rows[11]
        dv = dv + rows[12]
        a, k, kk = q["a"], q["k"], q["kk"]
        dk = dk2 * (1.0 + (a - 1.0) * k_a)
        da = dk2 * k * k_a + dbv * kk
        dkk = dbv * a - dav
        live = q["nrm"] > 1e-12
        dkkp = jnp.where(live, dkk - kk * _segsum(dkk * kk, bd), dkk) / q["nrm_c"]
        dk = dk + dkkp * k_k
        dlw = dlogdecay * q["log_decay"] * _sigmoid(-q["lw"])
        dla = da * a * (1.0 - a)
        nt = (((1,), (1,)), ((), ()))
        dtw = lax.dot_general(dlw.astype(BF16), wup.astype(BF16), nt, preferred_element_type=F32)
        dxa = lax.dot_general(dla.astype(BF16), aup.astype(BF16), nt, preferred_element_type=F32)
        dm_wa = dtw * (1.0 - q["tw"] * q["tw"]) + dxa
        dsg = lax.dot_general(dgate_.astype(BF16), gup.astype(BF16), nt, preferred_element_type=F32)
        dm_xg = dsg * q["sg"] * (1.0 - q["sg"])
        dm_rkv = jnp.concatenate([dr, dk, dv], axis=1)
        prkv, pwa, pxg = rows[:3]
        dmu = jnp.concatenate([_colsum(dm_rkv * (_shift_down(prkv, prevs[0], i, 1) - prkv)),
                               _colsum(dm_wa * (_shift_down(pwa, prevs[1], i, 1) - pwa)),
                               _colsum(dm_xg * (_shift_down(pxg, prevs[2], i, 1) - pxg))], axis=1)
        return [dm_rkv, dm_wa, dm_xg, dlw, dla, q["tw"], q["m_wa"], q["sg"], dmu,
                _colsum(dlw), _colsum(dla), _colsum(dkkp * k), _colsum(dk2 * k * (a - 1.0))]

    (dm_rkv, dm_wa, dm_xg, dlw, dla, tw_s, mwa_s, sg_s, G["rw_mu"], G["rw_w0"], G["rw_a0"], G["rw_k_k"],
     G["rw_k_a"]) = _rowwise(
        "d_rwkv_pre", rw_pre_bwd, T, tT,
        rows=[p_rkv, p_wa, p_xg, dr_s, dw_s, dk_s, dv_s, da_s, db_s, dgate, dr_b, dk2_b, dv_b],
        prevs=[p_rkv, p_wa, p_xg], consts=pre_consts,
        outs=[("row", RKV, F32), ("row", WA, F32), ("row", XG, F32), ("row", RW_WIDTH, BF16),
              ("row", RW_WIDTH, BF16), ("row", WA, BF16), ("row", WA, BF16), ("row", XG, BF16),
              ("acc", (1, RW_COLS))] + [("acc", (1, RW_WIDTH))] * 4)
    G["rw_w_up"] = _mm("d_rw_w_up", tw_s, dlw, "tn", grad_dtype)[:64]
    G["rw_a_up"] = _mm("d_rw_a_up", mwa_s, dla, "tn", grad_dtype)[64:]
    G["rw_g_up"] = _mm("d_rw_g_up", sg_s, dgate, "tn", grad_dtype)

    def shift_bwd(i, n, rows, pv, nexts, consts):
        return [rows[j] * (1.0 - consts[j]) + _shift_up(rows[j], nexts[j], i, n, 1) * consts[j] for j in range(3)]

    dp_rkv, dp_wa, dp_xg = _rowwise(
        "d_token_shift", shift_bwd, T, tT, rows=[dm_rkv, dm_wa, dm_xg], nexts=[dm_rkv, dm_wa, dm_xg],
        consts=[mu_rkv, mu_wa, mu_xg], outs=[("row", RKV, BF16), ("row", WA, BF16), ("row", XG, BF16)])

    G["w_in"] = jnp.concatenate([_mm("d_w_rkv", h, dp_rkv, "tn", grad_dtype), _mm("d_w_wa", h, dp_wa, "tn", grad_dtype),
                                 _mm("d_w_xg", h, dp_xg, "tn", grad_dtype), _mm("d_w_att", h, dp_att, "tn", grad_dtype, tn=768)], axis=1)
    if early_grads is not None:
        w_wa = w_wa + early_grads(G, 2)[0:1, 0:1].astype(w_wa.dtype)
    dh = _mm("d_h_gate", dz_gate, W["w_gate"], "nt")
    dh = _mm("d_h_rkv", dp_rkv, w_rkv, "nt", add=dh)
    dh = _mm("d_h_wa", dp_wa, w_wa, "nt", add=dh)
    dh = _mm("d_h_xg", dp_xg, w_xg, "nt", add=dh)
    dh = _mm("d_h_att", dp_att, w_att, "nt", add=dh)
    dx, G["g_mix"] = _rowwise("d_norm_mix", norm_bwd, T, tT, rows=[x, dh, dx1], consts=[W["g_mix"]],
                              outs=[("row", D_MODEL, F32), ("acc", (1, D_MODEL))])
    return loss_acc[:, :1], dx, G


HBM_SPEC = pl.BlockSpec(memory_space=pltpu.HBM)


def _place():
    x, y, c = lax.axis_index("x"), lax.axis_index("y"), lax.axis_index("c")
    return x, y, c, [(1 - x, y), (x, 1 - y), (1 - x, 1 - y)]


def _remote(src, dst, send_sems, recv_sems, k, to):
    return pltpu.make_async_remote_copy(src_ref=src, dst_ref=dst, send_sem=send_sems.at[k], recv_sem=recv_sems.at[k],
                                        device_id=to, device_id_type=MESH)


ROW_ALIGN = 16


def _splits(rows):
    return rows % (2 * ROW_ALIGN) == 0


def _half_rows(ref_rows, c, first):
    half = ref_rows // 2
    which = c if first else 1 - c
    return pl.ds(pl.multiple_of(which * half, ROW_ALIGN), half)


def _gather_chips(shards):
    n = len(shards)
    split = [_splits(s.shape[0]) for s in shards]

    def body(*refs):
        w_refs, out_refs = refs[:n], refs[n:2 * n]
        send_sems, recv_sems = refs[2 * n:]
        x, y, c, chips = _place()
        me = 2 * x + y
        sends, passed = [], []
        for i in range(n):
            for j, (px, py) in enumerate(chips):
                if split[i]:
                    mine = _half_rows(w_refs[i].shape[0], c, True)
                    cp = _remote(w_refs[i].at[mine], out_refs[i].at[me, mine], send_sems, recv_sems, 6 * i + j,
                                 (px, py, c))
                else:
                    cp = _remote(w_refs[i], out_refs[i].at[me], send_sems, recv_sems, 6 * i + j, (px, py, c))
                cp.start()
                sends.append(cp)
        for i in range(n):
            for j, (px, py) in enumerate(chips):
                if split[i]:
                    landed = out_refs[i].at[2 * px + py, _half_rows(w_refs[i].shape[0], c, True)]
                    _remote(landed, landed, send_sems, recv_sems, 6 * i + j, (px, py, c)).wait_recv()
                    cp = _remote(landed, landed, send_sems, recv_sems, 6 * i + 3 + j, (x, y, 1 - c))
                    cp.start()
                    passed.append(cp)
                else:
                    landed = out_refs[i].at[2 * px + py]
                    _remote(landed, landed, send_sems, recv_sems, 6 * i + j, (px, py, c)).wait_recv()
        for i in range(n):
            if split[i]:
                for j, (px, py) in enumerate(chips):
                    landed = out_refs[i].at[2 * px + py, _half_rows(w_refs[i].shape[0], c, False)]
                    _remote(landed, landed, send_sems, recv_sems, 6 * i + 3 + j, (x, y, 1 - c)).wait_recv()
        for cp in sends + passed:
            cp.wait_send()

    outs = pl.pallas_call(
        body, name="gather_weights", in_specs=[HBM_SPEC] * n, out_specs=[HBM_SPEC] * n,
        out_shape=[jax.ShapeDtypeStruct((N_CHIPS,) + s.shape, s.dtype) for s in shards],
        scratch_shapes=[pltpu.SemaphoreType.DMA((6 * n,)), pltpu.SemaphoreType.DMA((6 * n,))],
    )(*shards)
    me = 2 * lax.axis_index("x") + lax.axis_index("y")
    return [lax.dynamic_update_slice(o, s[None], (me, 0, 0)) for o, s in zip(outs, shards, strict=True)]


def _join_halves(reds):
    n = len(reds)

    def body(*refs):
        r_refs, out_refs = refs[:n], refs[n:2 * n]
        send_sems, recv_sems = refs[2 * n:]
        x, y, c, _ = _place()
        cps = []
        for i in range(n):
            mine = _half_rows(out_refs[i].shape[0], c, True)
            cp = _remote(r_refs[i], out_refs[i].at[mine], send_sems, recv_sems, i, (x, y, 1 - c))
            cp.start()
            cps.append(cp)
        for cp in cps:
            cp.wait()

    outs = pl.pallas_call(
        body, name="join_halves", in_specs=[HBM_SPEC] * n, out_specs=[HBM_SPEC] * n,
        out_shape=[jax.ShapeDtypeStruct((2 * r.shape[0], r.shape[1]), r.dtype) for r in reds],
        scratch_shapes=[pltpu.SemaphoreType.DMA((n,)), pltpu.SemaphoreType.DMA((n,))],
    )(*reds)
    c = lax.axis_index("c")
    return [lax.dynamic_update_slice(o, r, (c * r.shape[0], 0)) for o, r in zip(outs, reds, strict=True)]


def _gather_all(vec):
    def body(v_ref, out_ref, send_sems, recv_sems, local_sem):
        x, y, c, _ = _place()
        me = 4 * x + 2 * y + c
        local = pltpu.make_async_copy(v_ref, out_ref.at[me], local_sem)
        local.start()
        peers = [(x ^ (k >> 2), y ^ ((k >> 1) & 1), c ^ (k & 1)) for k in range(1, N_DEV)]
        sends = [_remote(v_ref, out_ref.at[me], send_sems, recv_sems, k, to) for k, to in enumerate(peers)]
        for cp in sends:
            cp.start()
        for k, (px, py, pc) in enumerate(peers):
            landed = out_ref.at[4 * px + 2 * py + pc]
            _remote(landed, landed, send_sems, recv_sems, k, (px, py, pc)).wait_recv()
        for cp in sends:
            cp.wait_send()
        local.wait()

    return pl.pallas_call(
        body, name="gather_small", in_specs=[HBM_SPEC], out_specs=HBM_SPEC,
        out_shape=jax.ShapeDtypeStruct((N_DEV,) + vec.shape, vec.dtype),
        scratch_shapes=[pltpu.SemaphoreType.DMA((7,)), pltpu.SemaphoreType.DMA((7,)), pltpu.SemaphoreType.DMA],
    )(vec)


SEM_SPEC = pl.BlockSpec(memory_space=pltpu.SEMAPHORE)
PEERS = N_DEV - 1
DATAFLOW = pltpu.SideEffectType.DATAFLOW_SIDE_EFFECTING


def _travel_copies(mode, src_refs, land_refs, send_sems, recv_sems):
    x, y, c, chips = _place()
    me = 2 * x + y
    pairs = []
    for i, (src, land) in enumerate(zip(src_refs, land_refs, strict=True)):
        if mode == "scatter":
            for k in range(1, N_DEV):
                px, py, pc = x ^ (k >> 2), y ^ ((k >> 1) & 1), c ^ (k & 1)
                mine = src.at[2 * px + py, _half_rows(src.shape[1], pc, True)]
                there, here = land.at[4 * x + 2 * y + c], land.at[4 * px + 2 * py + pc]
                send = functools.partial(_remote, mine, there, send_sems, recv_sems, PEERS * i + k - 1, (px, py, pc))
                arrival = functools.partial(_remote, mine, here, send_sems, recv_sems, PEERS * i + k - 1, (px, py, pc))
                pairs.append((send, arrival))
            continue
        for j, (px, py) in enumerate(chips):
            peer = 2 * px + py
            if _splits(src.shape[0]):
                rows = _half_rows(src.shape[0], c, True)
                mine, there, here = src.at[rows], land.at[me, rows], land.at[peer, rows]
            else:
                mine, there, here = src, land.at[me], land.at[peer]
            send = functools.partial(_remote, mine, there, send_sems, recv_sems, PEERS * i + j, (px, py, c))
            arrival = functools.partial(_remote, mine, here, send_sems, recv_sems, PEERS * i + j, (px, py, c))
            pairs.append((send, arrival))
    return pairs


def _share_halves(name, lands):
    idx = [i for i, a in enumerate(lands) if _splits(a.shape[1])]
    n = len(idx)

    def body(*refs):
        in_refs, out_refs = refs[:n], refs[n:2 * n]
        send_sems, recv_sems = refs[2 * n:]
        x, y, c, chips = _place()
        cps = []
        for i, (src, dst) in enumerate(zip(in_refs, out_refs, strict=True)):
            for j, (px, py) in enumerate(chips):
                mine = _half_rows(src.shape[1], c, True)
                cp = _remote(src.at[2 * px + py, mine], dst.at[2 * px + py, mine], send_sems, recv_sems, 3 * i + j,
                             (x, y, 1 - c))
                cp.start()
                cps.append(cp)
        for i, dst in enumerate(out_refs):
            for j, (px, py) in enumerate(chips):
                theirs = dst.at[2 * px + py, _half_rows(dst.shape[1], c, False)]
                _remote(theirs, theirs, send_sems, recv_sems, 3 * i + j, (x, y, 1 - c)).wait_recv()
        for cp in cps:
            cp.wait_send()

    outs = pl.pallas_call(
        body, name=name, in_specs=[HBM_SPEC] * n, out_specs=[HBM_SPEC] * n,
        out_shape=[jax.ShapeDtypeStruct(lands[i].shape, lands[i].dtype) for i in idx],
        input_output_aliases={i: i for i in range(n)},
        scratch_shapes=[pltpu.SemaphoreType.DMA((3 * n,)), pltpu.SemaphoreType.DMA((3 * n,))],
    )(*[lands[i] for i in idx])
    done = list(lands)
    for i, o in zip(idx, outs, strict=True):
        done[i] = o
    return done


def _travel_start(name, mode, srcs):
    n = len(srcs)
    lands = [lax.empty((N_CHIPS,) + s.shape if mode == "gather" else (N_DEV, s.shape[1] // 2, s.shape[2]), s.dtype)
             for s in srcs]

    def body(*refs):
        src_refs, land_refs = refs[:n], refs[n:2 * n]
        send_sems, recv_sems = refs[2 * n], refs[2 * n + 1]
        token = refs[-1]
        for send, _ in _travel_copies(mode, src_refs, land_refs, send_sems, recv_sems):
            send().start()
        token[...] = jnp.zeros_like(token)

    hbm = lambda a: pltpu.HBM(a.shape, a.dtype)
    outs = pl.pallas_call(
        body, name=name,
        out_shape=(pltpu.SemaphoreType.DMA((PEERS * n,)), pltpu.SemaphoreType.DMA((PEERS * n,)),
                   *[hbm(s) for s in srcs],
                   *[hbm(a) for a in lands], jax.ShapeDtypeStruct((SUBLANES, LANES), F32)),
        in_specs=[HBM_SPEC] * (2 * n),
        out_specs=(SEM_SPEC, SEM_SPEC, *[HBM_SPEC] * (2 * n), pl.BlockSpec(memory_space=pltpu.VMEM)),
        input_output_aliases={i: 2 + i for i in range(2 * n)},
        compiler_params=pltpu.CompilerParams(has_side_effects=DATAFLOW),
    )(*[pltpu.with_memory_space_constraint(a, pltpu.HBM) for a in list(srcs) + lands])
    return outs[0], outs[1], list(outs[2:2 + n]), list(outs[2 + n:2 + 2 * n]), outs[-1]


def _travel_wait(name, mode, send_sems, recv_sems, srcs, lands, after):
    n = len(srcs)

    def body(*refs):
        src_refs, land_refs = refs[:n], refs[n:2 * n]
        send_sems_, recv_sems_ = refs[2 * n], refs[2 * n + 1]
        for send, arrival in _travel_copies(mode, src_refs, land_refs, send_sems_, recv_sems_):
            send().wait_send()
            arrival().wait_recv()

    hbm = lambda a: pltpu.HBM(a.shape, a.dtype)
    outs = pl.pallas_call(
        body, name=name, out_shape=tuple(hbm(a) for a in list(srcs) + list(lands)),
        in_specs=[HBM_SPEC] * (2 * n) + [SEM_SPEC, SEM_SPEC, pl.BlockSpec(memory_space=pl.ANY)],
        out_specs=tuple([HBM_SPEC] * (2 * n)), input_output_aliases={i: i for i in range(2 * n)},
        compiler_params=pltpu.CompilerParams(has_side_effects=DATAFLOW),
    )(*srcs, *lands, send_sems, recv_sems, after)
    c = lax.axis_index("c")
    me = 2 * lax.axis_index("x") + lax.axis_index("y")
    if mode == "gather":
        slot, own = me, [s[None] for s in outs[:n]]
    else:
        slot = 2 * me + c
        own = [lax.dynamic_slice(s, (me, c * (s.shape[1] // 2), 0), (1, s.shape[1] // 2, s.shape[2])) for s in outs[:n]]
    return [lax.dynamic_update_slice(a, o, (slot,) + (0,) * (a.ndim - 1)) for a, o in zip(outs[n:], own, strict=True)]


SUM_TILE_BYTES = 4 * 1024 * 1024


def _sum_rows(half, cols):
    best = ROW_ALIGN
    for t in range(ROW_ALIGN, half + 1, ROW_ALIGN):
        if half % t == 0 and N_CHIPS * t * cols * 4 <= SUM_TILE_BYTES:
            best = t
    return best


def _sum_devices(name, parts):
    n, H, C = parts.shape
    tr = _sum_rows(H, C)

    def body(p_ref, o_ref):
        acc = p_ref[0].astype(F32)
        for k in range(1, n):
            acc = acc + p_ref[k].astype(F32)
        o_ref[...] = acc

    return pl.pallas_call(
        body, name=name, grid=(H // tr,),
        in_specs=[pl.BlockSpec((n, tr, C), lambda i: (0, i, 0))],
        out_specs=pl.BlockSpec((tr, C), lambda i: (i, 0)),
        out_shape=jax.ShapeDtypeStruct((H, C), F32),
        compiler_params=_params(("parallel",)),
    )(parts)


def _adamw_math(w, g, m, v):
    m = ADAM_B1 * m + (1.0 - ADAM_B1) * g
    v = ADAM_B2 * v + (1.0 - ADAM_B2) * (g * g)
    m_hat = m / (1.0 - ADAM_B1 ** ADAM_STEP)
    v_hat = v / (1.0 - ADAM_B2 ** ADAM_STEP)
    delta = -ADAM_LR * (m_hat / (jnp.sqrt(v_hat) + ADAM_EPS) + ADAM_WD * w)
    return delta, m, v


def _adamw(name, w, g, m, v):
    R, C = w.shape
    tr = R
    if R % SUBLANES == 0:
        for cand in range(SUBLANES, min(R, 256) + 1, SUBLANES):
            if R % cand == 0:
                tr = cand

    def body(w_ref, g_ref, m_ref, v_ref, d_ref, nm_ref, nv_ref):
        d, nm, nv = _adamw_math(w_ref[...], g_ref[...], m_ref[...], v_ref[...])
        d_ref[...] = d
        nm_ref[...] = nm
        nv_ref[...] = nv

    spec = pl.BlockSpec((tr, C), lambda i: (i, 0))
    shape = jax.ShapeDtypeStruct((R, C), F32)
    return pl.pallas_call(
        body, name=name, grid=(R // tr,), in_specs=[spec] * 4, out_specs=[spec] * 3, out_shape=[shape] * 3,
        compiler_params=_params(("parallel",)),
    )(w, g, m, v)


SMALL_ROW = 2048


def _small_layout(shapes):
    places, row = [], 0
    for R, C in shapes:
        pieces = []
        for r in range(R):
            for c0 in range(0, C, SMALL_ROW):
                pieces.append((r, c0, min(C, c0 + SMALL_ROW), row))
                row += 1
        places.append(pieces)
    return places, -(-row // SUBLANES) * SUBLANES


def _put_rows(block_ref, refs, places):
    block_ref[...] = jnp.zeros_like(block_ref)
    for ref, pieces in zip(refs, places, strict=True):
        for r, c0, c1, row in pieces:
            block_ref[row:row + 1, 0:c1 - c0] = ref[r:r + 1, c0:c1]


def _take_rows(block, refs, places):
    for ref, pieces in zip(refs, places, strict=True):
        for r, c0, c1, row in pieces:
            ref[r:r + 1, c0:c1] = block[row:row + 1, 0:c1 - c0]


def _pack_small(arrs):
    places, rows = _small_layout([a.shape for a in arrs])

    def body(*refs):
        _put_rows(refs[-1], refs[:-1], places)

    return pl.pallas_call(body, name="pack_small", out_shape=jax.ShapeDtypeStruct((rows, SMALL_ROW), F32),
                          compiler_params=_params())(*arrs)


def _adamw_small(parts, ws, ms, vs, extra_shapes):
    n_dev, rows, _ = parts.shape
    n = len(ws)
    places, rows_ = _small_layout([w.shape for w in ws] + list(extra_shapes))
    assert rows_ == rows, (rows_, rows)

    def body(*refs):
        p_ref = refs[0]
        w_refs, m_refs, v_refs = refs[1:1 + n], refs[1 + n:1 + 2 * n], refs[1 + 2 * n:1 + 3 * n]
        outs = refs[1 + 3 * n:-3]
        wb, mb, vb = refs[-3:]
        for block, srcs in ((wb, w_refs), (mb, m_refs), (vb, v_refs)):
            _put_rows(block, srcs, places[:n])
        g = p_ref[0]
        for k in range(1, n_dev):
            g = g + p_ref[k]
        d, nm, nv = _adamw_math(wb[...], g, mb[...], vb[...])
        _take_rows(g, outs[0:n], places[:n])
        _take_rows(d, outs[n:2 * n], places[:n])
        _take_rows(nm, outs[2 * n:3 * n], places[:n])
        _take_rows(nv, outs[3 * n:4 * n], places[:n])
        _take_rows(g, outs[4 * n:], places[n:])

    shapes = [jax.ShapeDtypeStruct(w.shape, F32) for w in ws]
    res = pl.pallas_call(
        body, name="adamw_small", out_shape=shapes * 4 + [jax.ShapeDtypeStruct(s, F32) for s in extra_shapes],
        scratch_shapes=[pltpu.VMEM((rows, SMALL_ROW), F32)] * 3, compiler_params=_params(),
    )(parts, *ws, *ms, *vs)
    return res[0:n], res[n:2 * n], res[2 * n:3 * n], res[3 * n:4 * n], res[4 * n:]


WEIGHTS = ['g_mix', 'w_in', 'rw_mu', 'rw_w0', 'rw_w_up', 'rw_a0', 'rw_a_up', 'rw_g_up', 'rw_k_k', 'rw_k_a',
           'rw_r_k', 'rw_ln_g', 'rw_ln_b', 'w_branch_a', 'w_branch_b', 'w_gate', 'b_gate', 'w_out', 'g_ffn', 'w_up',
           'conv_w', 'conv_b', 'w_down', 'g_ple', 'w_ple_gate', 'w_ple', 'g_final']
ARG_NAMES = (['x', 'p'] + WEIGHTS + ['loss_target'] + ['m_' + n for n in WEIGHTS] + ['v_' + n for n in WEIGHTS])
SHARDED = {'w_in': 1, 'rw_w_up': 1, 'rw_a_up': 1, 'rw_g_up': 1, 'w_branch_a': 1, 'w_branch_b': 1, 'w_gate': 1,
           'w_out': 0, 'w_up': 1, 'conv_w': 1, 'w_down': 0, 'w_ple_gate': 0, 'w_ple': 1}
SMALL = [n for n in WEIGHTS if n not in SHARDED]
WHOLE = ['conv_w']
FIRST_USED = ['w_in', 'rw_w_up', 'rw_a_up', 'rw_g_up', 'w_gate']
READ_BY_CHIP = ['w_gate', 'w_branch_a', 'w_branch_b', 'w_up', 'w_ple']
FIRST_DONE = [['w_up', 'w_down', 'w_ple_gate', 'w_ple'], ['w_out', 'w_branch_a', 'w_branch_b', 'w_gate'],
              ['w_in', 'rw_w_up', 'rw_a_up', 'rw_g_up']]
SPLIT = [n for n in SHARDED if n not in WHOLE]


def _full_from_shards(stack, axis):
    _, R, C = stack.shape
    if axis == 0:
        return stack.reshape(N_CHIPS * R, C)
    return stack.transpose(1, 0, 2).reshape(R, N_CHIPS * C)


def _shards_from_full(full, axis):
    R, C = full.shape
    if axis == 0:
        return full.reshape(N_CHIPS, R // N_CHIPS, C)
    return full.reshape(R, N_CHIPS, C // N_CHIPS).transpose(1, 0, 2)


def kernel(x, p, g_mix, w_in, rw_mu, rw_w0, rw_w_up, rw_a0, rw_a_up, rw_g_up, rw_k_k, rw_k_a, rw_r_k, rw_ln_g, rw_ln_b, w_branch_a, w_branch_b, w_gate, b_gate, w_out, g_ffn, w_up, conv_w, conv_b, w_down, g_ple, w_ple_gate, w_ple, g_final, loss_target, m_g_mix, m_w_in, m_rw_mu, m_rw_w0, m_rw_w_up, m_rw_a0, m_rw_a_up, m_rw_g_up, m_rw_k_k, m_rw_k_a, m_rw_r_k, m_rw_ln_g, m_rw_ln_b, m_w_branch_a, m_w_branch_b, m_w_gate, m_b_gate, m_w_out, m_g_ffn, m_w_up, m_conv_w, m_conv_b, m_w_down, m_g_ple, m_w_ple_gate, m_w_ple, m_g_final, v_g_mix, v_w_in, v_rw_mu, v_rw_w0, v_rw_w_up, v_rw_a0, v_rw_a_up, v_rw_g_up, v_rw_k_k, v_rw_k_a, v_rw_r_k, v_rw_ln_g, v_rw_ln_b, v_w_branch_a, v_w_branch_b, v_w_gate, v_b_gate, v_w_out, v_g_ffn, v_w_up, v_conv_w, v_conv_b, v_w_down, v_g_ple, v_w_ple_gate, v_w_ple, v_g_final):
    given = dict(zip(ARG_NAMES, (x, p, g_mix, w_in, rw_mu, rw_w0, rw_w_up, rw_a0, rw_a_up, rw_g_up, rw_k_k, rw_k_a, rw_r_k, rw_ln_g, rw_ln_b, w_branch_a, w_branch_b, w_gate, b_gate, w_out, g_ffn, w_up, conv_w, conv_b, w_down, g_ple, w_ple_gate, w_ple, g_final, loss_target, m_g_mix, m_w_in, m_rw_mu, m_rw_w0, m_rw_w_up, m_rw_a0, m_rw_a_up, m_rw_g_up, m_rw_k_k, m_rw_k_a, m_rw_r_k, m_rw_ln_g, m_rw_ln_b, m_w_branch_a, m_w_branch_b, m_w_gate, m_b_gate, m_w_out, m_g_ffn, m_w_up, m_conv_w, m_conv_b, m_w_down, m_g_ple, m_w_ple_gate, m_w_ple, m_g_final, v_g_mix, v_w_in, v_rw_mu, v_rw_w0, v_rw_w_up, v_rw_a0, v_rw_a_up, v_rw_g_up, v_rw_k_k, v_rw_k_a, v_rw_r_k, v_rw_ln_g, v_rw_ln_b, v_w_branch_a, v_w_branch_b, v_w_gate, v_b_gate, v_w_out, v_g_ffn, v_w_up, v_conv_w, v_conv_b, v_w_down, v_g_ple, v_w_ple_gate, v_w_ple, v_g_final), strict=True))

    def two_d(name, prefix=""):
        a = given[prefix + name]
        if name == "g_final":
            return a.reshape(1, D_MODEL)
        if name == "rw_r_k":
            return a.reshape(1, RW_WIDTH)
        return a[0] if a.ndim == 3 else a

    cast = lambda n: two_d(n) if n in WHOLE else two_d(n).astype(BF16)
    whole = lambda names, stacks: {n: g if n in READ_BY_CHIP else _full_from_shards(g, SHARDED[n])
                                   for n, g in zip(names, stacks, strict=True)}
    late_names = [n for n in SHARDED if n not in FIRST_USED]
    late_sends, late_recvs, late_srcs, late_lands, token = _travel_start(
        "gather_late_start", "gather", [cast(n) for n in late_names])
    W = whole(FIRST_USED, _gather_chips([cast(n) for n in FIRST_USED]))
    for n in SMALL:
        W[n] = two_d(n)
    W["rw_r_k"] = W["rw_r_k"].reshape(RW_HEADS, RW_HEAD_DIM)
    W["g_mix"] = W["g_mix"] + token[0:1, 0:1]

    def late_weights(after):
        lands = _travel_wait("gather_late_wait", "gather", late_sends, late_recvs, late_srcs, late_lands, after)
        return whole(late_names, _share_halves("share_late", lands))

    early_names = [[n for n in SPLIT if n in group] for group in FIRST_DONE]
    assert sorted(sum(early_names, [])) == sorted(SPLIT)
    travelling = []

    def early_grads(G, stage):
        by_chip = [G[n] if n in READ_BY_CHIP else _shards_from_full(G[n], SHARDED[n]) for n in early_names[stage]]
        sends, recvs, srcs, lands, started = _travel_start(f"scatter{stage}_start", "scatter", by_chip)
        travelling.append((sends, recvs, srcs, lands))
        return started

    loss_part, grad_x, G = _local_step(x[0], p[0, 0], W, loss_target[0], late_weights, early_grads, by_chip=True,
                                       grad_dtype=BF16)

    landed = {}
    for stage, (sends, recvs, srcs, lands) in enumerate(travelling):
        landed.update(zip(early_names[stage], _travel_wait(f"scatter{stage}_wait", "scatter", sends, recvs, srcs,
                                                           lands, grad_x), strict=True))
    reduced = [_sum_devices("sum_devices_" + n, landed[n]) for n in SPLIT]
    shard_grads = dict(zip(SPLIT, _join_halves(reduced), strict=True))

    G["rw_r_k"] = G["rw_r_k"].reshape(1, RW_WIDTH)
    extras = [G[n] for n in WHOLE] + [loss_part]
    all_small = _gather_all(_pack_small([G[n] for n in SMALL] + extras))
    gs, ds, nms, nvs, summed = _adamw_small(all_small, [two_d(n) for n in SMALL], [two_d(n, "m_") for n in SMALL],
                                            [two_d(n, "v_") for n in SMALL], [e.shape for e in extras])
    loss = summed[-1][0, 0]
    chip = 2 * lax.axis_index("x") + lax.axis_index("y")
    for n, full in zip(WHOLE, summed[:-1], strict=True):
        width = two_d(n).shape[1]
        shard_grads[n] = lax.dynamic_slice_in_dim(full, chip * width, width, axis=1)

    grads, deltas, new_m, new_v = {}, {}, {}, {}
    for n in SHARDED:
        g = shard_grads[n]
        d, nm, nv = _adamw("adamw_" + n, two_d(n), g, two_d(n, "m_"), two_d(n, "v_"))
        grads[n], deltas[n], new_m[n], new_v[n] = g, d, nm, nv
    for i, n in enumerate(SMALL):
        grads[n], deltas[n], new_m[n], new_v[n] = gs[i], ds[i], nms[i], nvs[i]
    outs = [loss, grad_x[None]]
    for table in (grads, deltas, new_m, new_v):
        outs += [table[n].reshape(given[n].shape) for n in WEIGHTS]
    return tuple(outs)
```

```python
import functools
import math

import jax
import jax.numpy as jnp
import numpy as np
from jax import lax
from jax.experimental import pallas as pl
from jax.experimental.pallas import tpu as pltpu

F32 = jnp.float32
BF16 = jnp.bfloat16

D_MODEL = 1024
NORM_EPS = 1e-6
RW_HEADS = 8
RW_HEAD_DIM = 64
RW_WIDTH = 512
RW_LN_EPS = 64e-5
ATT_GROUP_DILATION = (1, 4, 16)
ATT_BLOCK = 128
ATT_HEADS = 12
ATT_HEAD_DIM = 64
ATT_GROUP_WIDTH = 256
ATT_WIDTH = 768
D_FF = 3072

ADAM_LR = 0.001
ADAM_B1 = 0.9
ADAM_B2 = 0.999
ADAM_EPS = 1e-08
ADAM_WD = 0.01
ADAM_STEP = 10

SUBLANES = 8
LANES = 128
VMEM_LIMIT = 56 * 1024 * 1024
N_CHIPS = 4
N_DEV = 8
MESH = pl.DeviceIdType.MESH


def _params(sem=None):
    return pltpu.CompilerParams(dimension_semantics=sem, vmem_limit_bytes=VMEM_LIMIT)


def _pick(dim, pref):
    if dim % LANES != 0 or dim <= pref:
        return dim
    best = LANES
    for t in range(LANES, pref + 1, LANES):
        if dim % t == 0:
            best = t
    return best


def _mm(name, a, b, mode, out_dtype=F32, add=None, tm=1024, tn=1024, tk=1024, out_by_chip=False):
    by_chip = b.ndim == 3
    b_rows, b_cols = (b.shape[1], N_CHIPS * b.shape[2]) if by_chip else b.shape
    if mode == "nn":
        (M, K), (K2, N) = a.shape, (b_rows, b_cols)
    elif mode == "nt":
        (M, K), (N, K2) = a.shape, (b_rows, b_cols)
    else:
        (K, M), (K2, N) = a.shape, (b_rows, b_cols)
    assert K == K2, (name, a.shape, b.shape, mode)
    assert not (by_chip and mode == "tn") and not (out_by_chip and add is not None), name
    tm = _pick(M, tm)
    n_cut, k_cut = out_by_chip or (by_chip and mode == "nn"), by_chip and mode == "nt"
    tn = _pick(N // N_CHIPS, tn) if n_cut else _pick(N, tn)
    tk = _pick(K // N_CHIPS, tk) if k_cut else _pick(K, tk)
    nk = K // tk
    per_n = (N // N_CHIPS) // tn if n_cut else 1
    per_k = (K // N_CHIPS) // tk if k_cut else 1
    if mode == "nn":
        a_spec = pl.BlockSpec((tm, tk), lambda i, j, k: (i, k))
        b_spec = (pl.BlockSpec((None, tk, tn), lambda i, j, k: (j // per_n, k, j % per_n)) if by_chip
                  else pl.BlockSpec((tk, tn), lambda i, j, k: (k, j)))
        dims = (((1,), (0,)), ((), ()))
    elif mode == "nt":
        a_spec = pl.BlockSpec((tm, tk), lambda i, j, k: (i, k))
        b_spec = (pl.BlockSpec((None, tn, tk), lambda i, j, k: (k // per_k, j, k % per_k)) if by_chip
                  else pl.BlockSpec((tn, tk), lambda i, j, k: (j, k)))
        dims = (((1,), (1,)), ((), ()))
    else:
        a_spec = pl.BlockSpec((tk, tm), lambda i, j, k: (k, i))
        b_spec = pl.BlockSpec((tk, tn), lambda i, j, k: (k, j))
        dims = (((0,), (0,)), ((), ()))
    if out_by_chip:
        o_spec = pl.BlockSpec((None, tm, tn), lambda i, j, k: (j // per_n, i, j % per_n))
        out_shape = jax.ShapeDtypeStruct((N_CHIPS, M, N // N_CHIPS), out_dtype)
    else:
        o_spec = pl.BlockSpec((tm, tn), lambda i, j, k: (i, j))
        out_shape = jax.ShapeDtypeStruct((M, N), out_dtype)
    has_add = add is not None

    def body(*refs):
        if has_add:
            a_ref, b_ref, add_ref, o_ref, acc_ref = refs
        else:
            a_ref, b_ref, o_ref, acc_ref = refs
        k = pl.program_id(2)
        part = lax.dot_general(a_ref[...].astype(BF16), b_ref[...].astype(BF16), dims,
                               preferred_element_type=F32)

        @pl.when(k == 0)
        def _():
            acc_ref[...] = part

        @pl.when(k > 0)
        def _():
            acc_ref[...] += part

        @pl.when(k == nk - 1)
        def _():
            res = acc_ref[...]
            if has_add:
                res = res + add_ref[...].astype(F32)
            o_ref[...] = res.astype(o_ref.dtype)

    ins = [a, b] + ([add] if has_add else [])
    in_specs = [a_spec, b_spec] + ([o_spec] if has_add else [])
    return pl.pallas_call(
        body, name=name, grid=(M // tm, N // tn, nk),
        in_specs=in_specs, out_specs=o_spec, out_shape=out_shape,
        scratch_shapes=[pltpu.VMEM((tm, tn), F32)],
        compiler_params=_params(("parallel", "parallel", "arbitrary")),
    )(*ins)


def _rowwise(name, fn, T, tT, rows=(), prevs=(), nexts=(), consts=(), outs=()):
    n = T // tT
    per8 = tT // SUBLANES
    in_specs, ins = [], []
    for arr in rows:
        in_specs.append(pl.BlockSpec((tT, arr.shape[1]), lambda i: (i, 0)))
        ins.append(arr)
    for arr in prevs:
        in_specs.append(pl.BlockSpec((SUBLANES, arr.shape[1]), lambda i: (jnp.maximum(i * per8 - 1, 0), 0)))
        ins.append(arr)
    for arr in nexts:
        in_specs.append(pl.BlockSpec((SUBLANES, arr.shape[1]),
                                     lambda i: (jnp.minimum((i + 1) * per8, T // SUBLANES - 1), 0)))
        ins.append(arr)
    for arr in consts:
        in_specs.append(pl.BlockSpec(arr.shape, lambda i, nd=arr.ndim: (0,) * nd))
        ins.append(arr)
    out_specs, out_shapes = [], []
    for o in outs:
        if o[0] == "row":
            out_specs.append(pl.BlockSpec((tT, o[1]), lambda i: (i, 0)))
            out_shapes.append(jax.ShapeDtypeStruct((T, o[1]), o[2]))
        else:
            out_specs.append(pl.BlockSpec(o[1], lambda i: (0, 0)))
            out_shapes.append(jax.ShapeDtypeStruct(o[1], F32))
    nr, npv, nnx, nc = len(rows), len(prevs), len(nexts), len(consts)
    n_in = nr + npv + nnx + nc

    def body(*refs):
        i = pl.program_id(0)
        vals = [r[...] for r in refs[:n_in]]
        res = fn(i, n, vals[:nr], vals[nr:nr + npv], vals[nr + npv:nr + npv + nnx], vals[nr + npv + nnx:])
        for o, o_ref, val in zip(outs, refs[n_in:], res, strict=True):
            if o[0] == "row":
                o_ref[...] = val.astype(o_ref.dtype)
            else:
                @pl.when(i == 0)
                def _(o_ref=o_ref, val=val):
                    o_ref[...] = val.astype(F32)

                @pl.when(i > 0)
                def _(o_ref=o_ref, val=val):
                    o_ref[...] += val.astype(F32)

    res = pl.pallas_call(
        body, name=name, grid=(n,), in_specs=in_specs, out_specs=out_specs, out_shape=out_shapes,
        compiler_params=_params(("arbitrary",)),
    )(*ins)
    return list(res)


def _shift_down(x, prev8, i, s):
    rolled = pltpu.roll(x, s, 0)
    head = pltpu.roll(prev8, s, 0)
    head = jnp.where(i == 0, jnp.zeros_like(head), head)
    rid = lax.broadcasted_iota(jnp.int32, head.shape, 0)
    first = jnp.where(rid < s, head, rolled[:SUBLANES])
    if x.shape[0] == SUBLANES:
        return first
    return jnp.concatenate([first, rolled[SUBLANES:]], axis=0)


def _shift_up(x, next8, i, n, s):
    tT = x.shape[0]
    rolled = pltpu.roll(x, tT - s, 0)
    tail = pltpu.roll(next8, SUBLANES - s, 0)
    tail = jnp.where(i == n - 1, jnp.zeros_like(tail), tail)
    rid = lax.broadcasted_iota(jnp.int32, tail.shape, 0)
    last = jnp.where(rid >= SUBLANES - s, tail, rolled[tT - SUBLANES:])
    return jnp.concatenate([rolled[:tT - SUBLANES], last], axis=0)


def _colsum(x):
    return jnp.sum(x, axis=0, keepdims=True)


def _segsum(x, bd):
    return jnp.dot(x, bd, precision=lax.Precision.HIGH, preferred_element_type=F32)


def _block_diag_ones(width, seg):
    idx = np.arange(width) // seg
    return jnp.asarray((idx[:, None] == idx[None, :]).astype(np.float32))


def _sigmoid(z):
    return 1.0 / (1.0 + jnp.exp(-z))


def _softplus(z):
    return jnp.maximum(z, 0.0) + jnp.log(1.0 + jnp.exp(-jnp.abs(z)))


def _rms_fwd(x, g):
    r = lax.rsqrt(jnp.mean(x * x, axis=-1, keepdims=True) + NORM_EPS)
    return x * r * g


def _rms_bwd(x, g, dy):
    r = lax.rsqrt(jnp.mean(x * x, axis=-1, keepdims=True) + NORM_EPS)
    gdy = dy * g
    dx = r * (gdy - x * (r * r) * jnp.mean(x * gdy, axis=-1, keepdims=True))
    return dx, dy * x * r


GELU_C = math.sqrt(2.0 / math.pi)


def _gelu(x):
    return 0.5 * x * (1.0 + jnp.tanh(GELU_C * (x + 0.044715 * x * x * x)))


def _gelu_and_grad(x):
    th = jnp.tanh(GELU_C * (x + 0.044715 * x * x * x))
    half = 0.5 * (1.0 + th)
    return x * half, half + 0.5 * x * (1.0 - th * th) * GELU_C * (1.0 + 3.0 * 0.044715 * x * x)


RW_CHUNK = 64
NN = (((1,), (0,)), ((), ()))
NT = (((1,), (1,)), ((), ()))
TN = (((0,), (0,)), ((), ()))


def _hdot(a, b, dims):
    return lax.dot_general(a, b, dims, precision=lax.Precision.HIGH, preferred_element_type=F32)


def _ldot(a, b, dims):
    return lax.dot_general(a.astype(BF16), b.astype(BF16), dims, preferred_element_type=F32)


def _chunk_masks():
    ti = lax.broadcasted_iota(jnp.int32, (RW_CHUNK, RW_CHUNK), 0)
    tj = lax.broadcasted_iota(jnp.int32, (RW_CHUNK, RW_CHUNK), 1)
    return tj <= ti, tj < ti, (ti == tj).astype(F32)


def _head(x, h):
    return x[:, h * RW_HEAD_DIM:(h + 1) * RW_HEAD_DIM]


def _heads(fn):
    return [fn(h) for h in range(RW_HEADS)]


def _chunk_rows(r, lw, k, a, b, incl_f):
    c = _hdot(incl_f, lw, NN)
    e_prev, e_neg, e_pos = jnp.exp(c - lw), jnp.exp(-c), jnp.exp(c)
    return dict(At=a * e_prev, Bt=b * e_neg, Kt=k * e_neg, Rt=r * e_pos, e_prev=e_prev, e_neg=e_neg, e_pos=e_pos)


def _chunk_coeffs(q, incl, strict):
    A1 = _heads(lambda h: jnp.where(strict, _hdot(_head(q["At"], h), _head(q["Bt"], h), NT), 0.0))
    A2 = _heads(lambda h: jnp.where(strict, _hdot(_head(q["At"], h), _head(q["Kt"], h), NT), 0.0))
    W1 = _heads(lambda h: jnp.where(incl, _hdot(_head(q["Rt"], h), _head(q["Bt"], h), NT), 0.0))
    W2 = _heads(lambda h: jnp.where(incl, _ldot(_head(q["Rt"], h), _head(q["Kt"], h), NT), 0.0))
    return A1, A2, W1, W2


def _rwkv_chunk_prep(r, lw, k, a, b, v):
    T = r.shape[0]
    nC = T // RW_CHUNK
    H, N = RW_HEADS, RW_HEAD_DIM

    def body(r_ref, lw_ref, k_ref, a_ref, b_ref, v_ref,
             at_ref, bt_ref, kt_ref, rt_ref, a2v_ref, w2v_ref, ti_ref, w1_ref, a2_ref, w2_ref, pl_ref):
        incl, strict, eye = _chunk_masks()
        q = _chunk_rows(r_ref[...], lw_ref[...], k_ref[...], a_ref[...], b_ref[...], incl.astype(F32))
        at_ref[...], bt_ref[...], kt_ref[...], rt_ref[...] = q["At"], q["Bt"], q["Kt"], q["Rt"]
        pl_ref[0] = jnp.broadcast_to(q["e_pos"][RW_CHUNK - 1:RW_CHUNK, :], (SUBLANES, RW_WIDTH))
        A1, A2, W1, W2 = _chunk_coeffs(q, incl, strict)
        V = v_ref[...]
        a2v_ref[...] = jnp.concatenate(_heads(lambda h: _hdot(A2[h], _head(V, h), NN)), axis=1)
        w2v_ref[...] = jnp.concatenate(_heads(lambda h: _ldot(W2[h], _head(V, h), NN)), axis=1)
        tinv, pw = [eye + m for m in A1], A1
        for stage in range(5):
            dot = _hdot if stage == 0 else _ldot
            pw = [dot(m, m, NN) for m in pw]
            tinv = [t + dot(t, m, NN) for t, m in zip(tinv, pw, strict=True)]
        for h in range(H):
            ti_ref[0, h] = tinv[h]
            w1_ref[0, h] = W1[h]
            a2_ref[0, h] = A2[h]
            w2_ref[0, h] = W2[h]

    row_spec = pl.BlockSpec((RW_CHUNK, RW_WIDTH), lambda n: (n, 0))
    st_spec = pl.BlockSpec((1, H, N, N), lambda n: (n, 0, 0, 0))
    row_shape = jax.ShapeDtypeStruct((T, RW_WIDTH), F32)
    st_shape = jax.ShapeDtypeStruct((nC, H, N, N), F32)
    return pl.pallas_call(
        body, name="rwkv_chunk_prep", grid=(nC,),
        in_specs=[row_spec] * 6,
        out_specs=[row_spec] * 6 + [st_spec] * 4 + [pl.BlockSpec((1, SUBLANES, RW_WIDTH), lambda n: (n, 0, 0))],
        out_shape=[row_shape] * 6 + [st_shape] * 4 + [jax.ShapeDtypeStruct((nC, SUBLANES, RW_WIDTH), F32)],
        compiler_params=_params(("parallel",)),
    )(r, lw, k, a, b, v)


def _rwkv_chunk_fwd(v, at, bt, kt, rt, a2v, w2v, tinv, w1, plast):
    T = v.shape[0]
    nC = T // RW_CHUNK
    H, N = RW_HEADS, RW_HEAD_DIM

    def body(v_ref, at_ref, bt_ref, kt_ref, rt_ref, a2v_ref, w2v_ref, ti_ref, w1_ref, pl_ref,
             y_ref, sa_ref, s0_ref, S_ref):
        @pl.when(pl.program_id(0) == 0)
        def _():
            S_ref[...] = jnp.zeros_like(S_ref)

        V, At, Bt, Kt, Rt = v_ref[...], at_ref[...], bt_ref[...], kt_ref[...], rt_ref[...]
        A2V, W2V, p_last = a2v_ref[...], w2v_ref[...], pl_ref[0, 0:1, :]
        S0 = _heads(lambda h: S_ref[h])
        for h in range(H):
            s0_ref[0, h] = S0[h]
        Z = _heads(lambda h: _hdot(_head(At, h), S0[h], NT) + _head(A2V, h))
        Sa = _heads(lambda h: _hdot(ti_ref[0, h], Z[h], NN))
        X = _heads(lambda h: S0[h] + _hdot(Sa[h], _head(Bt, h), TN) + _hdot(_head(V, h), _head(Kt, h), TN))
        for h in range(H):
            S_ref[h] = X[h] * _head(p_last, h)
        Y = _heads(lambda h: _ldot(_head(Rt, h), S0[h], NT) + _ldot(w1_ref[0, h], Sa[h], NN) + _head(W2V, h))
        y_ref[...] = jnp.concatenate(Y, axis=1)
        sa_ref[...] = jnp.concatenate(Sa, axis=1)

    row_spec = pl.BlockSpec((RW_CHUNK, RW_WIDTH), lambda n: (n, 0))
    st_spec = pl.BlockSpec((1, H, N, N), lambda n: (n, 0, 0, 0))
    row_shape = jax.ShapeDtypeStruct((T, RW_WIDTH), F32)
    return pl.pallas_call(
        body, name="rwkv_chunk_fwd", grid=(nC,),
        in_specs=[row_spec] * 7 + [st_spec, st_spec, pl.BlockSpec((1, SUBLANES, RW_WIDTH), lambda n: (n, 0, 0))],
        out_specs=[row_spec, row_spec, st_spec],
        out_shape=[row_shape, row_shape, jax.ShapeDtypeStruct((nC, H, N, N), F32)],
        scratch_shapes=[pltpu.VMEM((H, N, N), F32)],
        compiler_params=_params(("arbitrary",)),
    )(v, at, bt, kt, rt, a2v, w2v, tinv, w1, plast)


def _rwkv_chunk_bwd(r, lw, k, a, b, v, dy, s0, tinv, w1, a2, w2, sa):
    T = r.shape[0]
    nC = T // RW_CHUNK
    H, N = RW_HEADS, RW_HEAD_DIM

    def body(r_ref, lw_ref, k_ref, a_ref, b_ref, v_ref, dy_ref, s0_ref, ti_ref, w1_ref, a2_ref, w2_ref, sa_ref,
             dr_ref, dlw_ref, dk_ref, da_ref, db_ref, dv_ref, dS_ref):
        @pl.when(pl.program_id(0) == 0)
        def _():
            dS_ref[...] = jnp.zeros_like(dS_ref)

        incl, strict, _ = _chunk_masks()
        incl_f = incl.astype(F32)
        q = _chunk_rows(r_ref[...], lw_ref[...], k_ref[...], a_ref[...], b_ref[...], incl_f)
        At, Bt, Kt, Rt = q["At"], q["Bt"], q["Kt"], q["Rt"]
        A2, W1, W2 = (_heads(lambda h, ref=ref: ref[0, h]) for ref in (a2_ref, w1_ref, w2_ref))
        V, dY, Sa = v_ref[...], dy_ref[...], sa_ref[...]
        hd = _head
        p_last = q["e_pos"][RW_CHUNK - 1:RW_CHUNK, :]
        S0 = _heads(lambda h: s0_ref[0, h])
        G = _heads(lambda h: dS_ref[h] * hd(p_last, h))
        X = _heads(lambda h: S0[h] + _ldot(hd(Sa, h), hd(Bt, h), TN) + _ldot(hd(V, h), hd(Kt, h), TN))
        dc_last = jnp.concatenate(_heads(lambda h: jnp.sum(G[h] * X[h], axis=0, keepdims=True)), axis=1)
        dSa = _heads(lambda h: _ldot(hd(Bt, h), G[h], NT) + _ldot(W1[h], hd(dY, h), TN))
        dZ = _heads(lambda h: _ldot(ti_ref[0, h], dSa[h], TN))
        for h in range(H):
            dS_ref[h] = G[h] + _ldot(dZ[h], hd(At, h), TN) + _ldot(hd(dY, h), hd(Rt, h), TN)
        dA1 = _heads(lambda h: jnp.where(strict, _ldot(dZ[h], hd(Sa, h), NT), 0.0))
        dA2 = _heads(lambda h: jnp.where(strict, _ldot(dZ[h], hd(V, h), NT), 0.0))
        dW1 = _heads(lambda h: jnp.where(incl, _ldot(hd(dY, h), hd(Sa, h), NT), 0.0))
        dW2 = _heads(lambda h: jnp.where(incl, _ldot(hd(dY, h), hd(V, h), NT), 0.0))
        cat = lambda fn: jnp.concatenate(_heads(fn), axis=1)
        dV = cat(lambda h: _ldot(A2[h], dZ[h], TN) + _ldot(W2[h], hd(dY, h), TN) + _ldot(hd(Kt, h), G[h], NT))
        dAt = cat(lambda h: _ldot(dA1[h], hd(Bt, h), NN) + _ldot(dA2[h], hd(Kt, h), NN) + _ldot(dZ[h], S0[h], NN))
        dBt = cat(lambda h: _ldot(dA1[h], hd(At, h), TN) + _ldot(dW1[h], hd(Rt, h), TN) + _ldot(hd(Sa, h), G[h], NN))
        dKt = cat(lambda h: _ldot(dA2[h], hd(At, h), TN) + _ldot(dW2[h], hd(Rt, h), TN) + _ldot(hd(V, h), G[h], NN))
        dRt = cat(lambda h: _ldot(hd(dY, h), S0[h], NN) + _ldot(dW1[h], hd(Bt, h), NN) + _ldot(dW2[h], hd(Kt, h), NN))
        last_row = lax.broadcasted_iota(jnp.int32, (RW_CHUNK, RW_WIDTH), 0) == RW_CHUNK - 1
        dc_prev = dAt * At
        dc = dc_prev + dRt * Rt - dBt * Bt - dKt * Kt + jnp.where(last_row, dc_last, 0.0)
        dr_ref[...] = dRt * q["e_pos"]
        dlw_ref[...] = _hdot(incl_f, dc, TN) - dc_prev
        dk_ref[...] = dKt * q["e_neg"]
        da_ref[...] = dAt * q["e_prev"]
        db_ref[...] = dBt * q["e_neg"]
        dv_ref[...] = dV

    rev = lambda n: nC - 1 - n
    row_spec = pl.BlockSpec((RW_CHUNK, RW_WIDTH), lambda n: (rev(n), 0))
    st_spec = pl.BlockSpec((1, H, N, N), lambda n: (rev(n), 0, 0, 0))
    row_shape = jax.ShapeDtypeStruct((T, RW_WIDTH), F32)
    return pl.pallas_call(
        body, name="rwkv_chunk_bwd", grid=(nC,),
        in_specs=[row_spec] * 7 + [st_spec] * 5 + [row_spec], out_specs=[row_spec] * 6,
        out_shape=[row_shape] * 6, scratch_shapes=[pltpu.VMEM((H, N, N), F32)],
        compiler_params=_params(("arbitrary",)),
    )(r, lw, k, a, b, v, dy, s0, tinv, w1, a2, w2, sa)


def _alibi_slope(head):
    return float(np.float32(2.0 ** (-8.0 * (head + 1) / ATT_HEADS)))


ATT_SPAN = ATT_BLOCK * max(ATT_GROUP_DILATION)
ATT_PAIR_WIDTH = 2 * ATT_HEAD_DIM
ATT_SIDE_BY_SIDE = 16


def _pair_slope(g, hp, j):
    return jnp.where(hp == 0, _alibi_slope(4 * g + j), _alibi_slope(4 * g + 2 + j))


def _att_rows(mi, r, d):
    start = mi * ATT_BLOCK * d + r
    return pl.ds(start, ATT_BLOCK) if d == 1 else pl.ds(start, ATT_BLOCK, stride=d)


def _att_masks():
    qi = lax.broadcasted_iota(jnp.int32, (ATT_BLOCK, ATT_BLOCK), 0)
    kj = lax.broadcasted_iota(jnp.int32, (ATT_BLOCK, ATT_BLOCK), 1)
    return qi, kj


NEG = -1e30


def _att_logits(q, k, slope_d, steps, valid):
    s = lax.dot_general(q.astype(BF16), k.astype(BF16), (((1,), (1,)), ((), ())),
                        preferred_element_type=F32) * (ATT_HEAD_DIM ** -0.5)
    return jnp.where(valid, s - slope_d * steps.astype(F32), NEG)


def _att_fwd(p_att, g):
    T = p_att.shape[0]
    d = ATT_GROUP_DILATION[g]
    W = ATT_PAIR_WIDTH
    nb = T // ATT_SPAN
    mb = ATT_SPAN // (ATT_BLOCK * d)

    def body(q_ref, kc_ref, kp_ref, vc_ref, vp_ref, o_ref, l_ref):
        hp, n = pl.program_id(0), pl.program_id(1)
        qi, kj = _att_masks()
        slopes = [_pair_slope(g, hp, j) * d for j in range(2)]
        blocks = [(r, mi) for r in range(d) for mi in range(mb)]
        for at in range(0, len(blocks), ATT_SIDE_BY_SIDE):
            tasks = []
            for r, mi in blocks[at:at + ATT_SIDE_BY_SIDE]:
                rows = _att_rows(mi, r, d)
                if mi > 0:
                    prev = _att_rows(mi - 1, r, d)
                    kp, vp, has_prev = kc_ref[prev, :], vc_ref[prev, :], True
                else:
                    prev = _att_rows(mb - 1, r, d)
                    kp, vp, has_prev = kp_ref[prev, :], vp_ref[prev, :], n > 0
                q, kc, vc = q_ref[rows, :], kc_ref[rows, :], vc_ref[rows, :]
                for j in range(2):
                    sl = slice(j * ATT_HEAD_DIM, (j + 1) * ATT_HEAD_DIM)
                    tasks.append((q[:, sl], kc[:, sl], kp[:, sl], vc[:, sl], vp[:, sl], has_prev, slopes[j]))
            lc = [_att_logits(t[0], t[1], t[6], qi - kj, kj <= qi) for t in tasks]
            lp = [_att_logits(t[0], t[2], t[6], qi - kj + ATT_BLOCK, (kj >= qi) & t[5]) for t in tasks]
            mx = [jnp.maximum(jnp.max(a, axis=1, keepdims=True), jnp.max(b, axis=1, keepdims=True))
                  for a, b in zip(lc, lp, strict=True)]
            ec = [jnp.exp(a - m) for a, m in zip(lc, mx, strict=True)]
            ep = [jnp.exp(b - m) for b, m in zip(lp, mx, strict=True)]
            den = [jnp.sum(a, axis=1, keepdims=True) + jnp.sum(b, axis=1, keepdims=True)
                   for a, b in zip(ec, ep, strict=True)]
            inv = [1.0 / s for s in den]
            outs = [jnp.dot((a * i).astype(BF16), t[3].astype(BF16), preferred_element_type=F32)
                    + jnp.dot((b * i).astype(BF16), t[4].astype(BF16), preferred_element_type=F32)
                    for a, b, i, t in zip(ec, ep, inv, tasks, strict=True)]
            lses = [jnp.broadcast_to(m + jnp.log(s), (ATT_BLOCK, ATT_HEAD_DIM)) for m, s in zip(mx, den, strict=True)]
            for i, (r, mi) in enumerate(blocks[at:at + ATT_SIDE_BY_SIDE]):
                rows = _att_rows(mi, r, d)
                o_ref[rows, :] = jnp.concatenate(outs[2 * i:2 * i + 2], axis=1)
                l_ref[rows, :] = jnp.concatenate(lses[2 * i:2 * i + 2], axis=1)

    def spec(col0, prev):
        if prev:
            return pl.BlockSpec((ATT_SPAN, W), lambda hp, n: (jnp.maximum(n - 1, 0), col0 + 2 * g + hp))
        return pl.BlockSpec((ATT_SPAN, W), lambda hp, n: (n, col0 + 2 * g + hp))

    o_spec = pl.BlockSpec((ATT_SPAN, W), lambda hp, n: (n, hp))
    o, l = pl.pallas_call(
        body, name=f"att_fwd_g{g}", grid=(2, nb),
        in_specs=[spec(0, False), spec(6, False), spec(6, True), spec(12, False), spec(12, True)],
        out_specs=[o_spec, o_spec],
        out_shape=[jax.ShapeDtypeStruct((T, ATT_GROUP_WIDTH), F32)] * 2,
        compiler_params=_params(("parallel", "arbitrary")),
    )(p_att, p_att, p_att, p_att, p_att)
    return o, l


def _att_bwd(p_att, o, l, do, dl, g):
    T = p_att.shape[0]
    d = ATT_GROUP_DILATION[g]
    W = ATT_PAIR_WIDTH
    nb = T // ATT_SPAN
    mb = ATT_SPAN // (ATT_BLOCK * d)
    scale = ATT_HEAD_DIM ** -0.5

    def body(q_ref, k_ref, v_ref, o_ref, l_ref, do_ref, dl_ref,
             qn_ref, on_ref, ln_ref, don_ref, dln_ref, dq_ref, dk_ref, dv_ref, carry_ref):
        hp, n = pl.program_id(0), pl.program_id(1)
        qi, kj = _att_masks()

        @pl.when(n == 0)
        def _():
            carry_ref[...] = jnp.zeros_like(carry_ref)

        slopes = [_pair_slope(g, hp, j) * d for j in range(2)]
        blocks = [(r, mi) for r in range(d) for mi in range(mb)]
        side_by_side = ATT_SIDE_BY_SIDE // 2
        carry = None
        for at in range(0, len(blocks), side_by_side):
            tasks = []
            for r, mi in blocks[at:at + side_by_side]:
                rows = _att_rows(mi, r, d)
                if mi < mb - 1:
                    nrows = _att_rows(mi + 1, r, d)
                    nxt = (q_ref[nrows, :], o_ref[nrows, :], l_ref[nrows, :], do_ref[nrows, :], dl_ref[nrows, :])
                    has_next = True
                else:
                    nrows = _att_rows(0, r, d)
                    nxt = (qn_ref[nrows, :], on_ref[nrows, :], ln_ref[nrows, :], don_ref[nrows, :],
                           dln_ref[nrows, :])
                    has_next = n < nb - 1
                cur = (q_ref[rows, :], o_ref[rows, :], l_ref[rows, :], do_ref[rows, :], dl_ref[rows, :])
                k_all, v_all = k_ref[rows, :], v_ref[rows, :]
                for j in range(2):
                    sl = slice(j * ATT_HEAD_DIM, (j + 1) * ATT_HEAD_DIM)
                    for blk, steps, valid in ((cur, qi - kj, kj <= qi),
                                              (nxt, qi - kj + ATT_BLOCK, (kj >= qi) & has_next)):
                        q, o_, lse, do_, dlse = (z[:, sl] for z in blk)
                        tasks.append(dict(q=q, o=o_, lse=lse[:, :1], do=do_, dlse=dlse[:, :1], steps=steps,
                                          valid=valid, k=k_all[:, sl], vb=v_all[:, sl].astype(BF16),
                                          slope=slopes[j]))
            p = [jnp.exp(_att_logits(t["q"], t["k"], t["slope"], t["steps"], t["valid"]) - t["lse"]) for t in tasks]
            dp = [lax.dot_general(t["do"].astype(BF16), t["vb"], (((1,), (1,)), ((), ())),
                                  preferred_element_type=F32) for t in tasks]
            dsum = [jnp.sum(t["do"] * t["o"], axis=1, keepdims=True) for t in tasks]
            ds = [a * (b - s + t["dlse"]) for a, b, s, t in zip(p, dp, dsum, tasks, strict=True)]
            dv_ = [jnp.dot(a.T.astype(BF16), t["do"].astype(BF16), preferred_element_type=F32)
                   for a, t in zip(p, tasks, strict=True)]
            dk_ = [jnp.dot(a.T.astype(BF16), t["q"].astype(BF16), preferred_element_type=F32) * scale
                   for a, t in zip(ds, tasks, strict=True)]
            dq_ = [jnp.dot(a.astype(BF16), t["k"].astype(BF16), preferred_element_type=F32) * scale
                   for a, t in zip(ds, tasks, strict=True)]
            for i, (r, mi) in enumerate(blocks[at:at + side_by_side]):
                rows = _att_rows(mi, r, d)
                b = 4 * i
                if mi == 0:
                    carry = carry_ref[r]
                dq_ref[rows, :] = jnp.concatenate([dq_[b], dq_[b + 2]], axis=1) + carry
                carry = jnp.concatenate([dq_[b + 1], dq_[b + 3]], axis=1)
                if mi == mb - 1:
                    carry_ref[r] = carry
                dk_ref[rows, :] = jnp.concatenate([dk_[b] + dk_[b + 1], dk_[b + 2] + dk_[b + 3]], axis=1)
                dv_ref[rows, :] = jnp.concatenate([dv_[b] + dv_[b + 1], dv_[b + 2] + dv_[b + 3]], axis=1)

    head_rows = ATT_BLOCK * d
    nxt_n = lambda n: jnp.minimum((n + 1) * mb, T // head_rows - 1)
    cur_p = lambda col0: pl.BlockSpec((ATT_SPAN, W), lambda hp, n: (n, col0 + 2 * g + hp))
    cur_o = pl.BlockSpec((ATT_SPAN, W), lambda hp, n: (n, hp))
    nxt_o = pl.BlockSpec((head_rows, W), lambda hp, n: (nxt_n(n), hp))
    dq, dk, dv = pl.pallas_call(
        body, name=f"att_bwd_g{g}", grid=(2, nb),
        in_specs=[cur_p(0), cur_p(6), cur_p(12), cur_o, cur_o, cur_o, cur_o,
                  pl.BlockSpec((head_rows, W), lambda hp, n: (nxt_n(n), 2 * g + hp)), nxt_o, nxt_o, nxt_o, nxt_o],
        out_specs=[cur_o, cur_o, cur_o],
        out_shape=[jax.ShapeDtypeStruct((T, ATT_GROUP_WIDTH), F32)] * 3,
        scratch_shapes=[pltpu.VMEM((d, ATT_BLOCK, W), F32)],
        compiler_params=_params(("parallel", "arbitrary")),
    )(p_att, p_att, p_att, o, l, do, dl, p_att, o, l, do, dl)
    return dq, dk, dv


FFN_TILE = 2 * D_FF // N_CHIPS
RKV = 3 * RW_WIDTH
WA = 128
XG = 160
RW_COLS = RKV + WA + XG


def _local_step(x, p, W, target, late_weights=None, early_grads=None, by_chip=False, grad_dtype=F32):
    T = x.shape[0]
    tT = 256
    bd512 = _block_diag_ones(RW_WIDTH, RW_HEAD_DIM)
    bd256 = _block_diag_ones(ATT_GROUP_WIDTH, ATT_HEAD_DIM)
    G = {}
    W = dict(W)

    w_in = W["w_in"]
    w_rkv, w_wa, w_xg, w_att = (w_in[:, :RKV], w_in[:, RKV:RKV + WA], w_in[:, RKV + WA:RW_COLS],
                                w_in[:, RW_COLS:])
    mu = W["rw_mu"]
    mu_rkv, mu_wa, mu_xg = mu[:, :RKV], mu[:, RKV:RKV + WA], mu[:, RKV + WA:]
    zpad = jnp.zeros((64, RW_WIDTH), W["rw_w_up"].dtype)
    w_up_pad = jnp.concatenate([W["rw_w_up"], zpad], axis=0)
    a_up_pad = jnp.concatenate([zpad, W["rw_a_up"]], axis=0)
    r_k = W["rw_r_k"].reshape(1, RW_WIDTH)

    (h,) = _rowwise("norm_mix", lambda i, n, r, pv, nx, c: [_rms_fwd(r[0], c[0])], T, tT,
                    rows=[x], consts=[W["g_mix"]], outs=[("row", D_MODEL, BF16)])
    p_rkv = _mm("proj_rkv", h, w_rkv, "nn")
    p_wa = _mm("proj_wa", h, w_wa, "nn")
    p_xg = _mm("proj_xg", h, w_xg, "nn")
    p_att = _mm("proj_att", h, w_att, "nn", tn=768)
    z_gate = _mm("proj_gate", h, W["w_gate"], "nn")

    def rw_pre_core(i, rows, prevs, consts):
        prkv, pwa, pxg = rows[:3]
        (mrkv, mwa, mxg, w0, a0, k_k, k_a, wup, aup, gup, bd) = consts[:11]
        m_rkv = prkv + (_shift_down(prkv, prevs[0], i, 1) - prkv) * mrkv
        m_wa = pwa + (_shift_down(pwa, prevs[1], i, 1) - pwa) * mwa
        m_xg = pxg + (_shift_down(pxg, prevs[2], i, 1) - pxg) * mxg
        r, k, v = m_rkv[:, :RW_WIDTH], m_rkv[:, RW_WIDTH:2 * RW_WIDTH], m_rkv[:, 2 * RW_WIDTH:]
        tw = jnp.tanh(m_wa)
        lw = w0 + jnp.dot(tw.astype(BF16), wup.astype(BF16), preferred_element_type=F32)
        wlog = -_softplus(-lw) - 0.5
        log_decay = -jnp.exp(wlog)
        a = _sigmoid(a0 + jnp.dot(m_wa.astype(BF16), aup.astype(BF16), preferred_element_type=F32))
        sg = _sigmoid(m_xg)
        gate = jnp.dot(sg.astype(BF16), gup.astype(BF16), preferred_element_type=F32)
        kkp = k * k_k
        nrm = jnp.sqrt(_segsum(kkp * kkp, bd))
        nrm_c = jnp.maximum(nrm, 1e-12)
        kk = kkp / nrm_c
        k2 = k * (1.0 + (a - 1.0) * k_a)
        return dict(r=r, k=k, v=v, tw=tw, lw=lw, wlog=wlog, log_decay=log_decay, a=a, sg=sg, gate=gate, kkp=kkp,
                    nrm=nrm, nrm_c=nrm_c, kk=kk, k2=k2, m_rkv=m_rkv, m_wa=m_wa, m_xg=m_xg)

    pre_consts = [mu_rkv, mu_wa, mu_xg, W["rw_w0"], W["rw_a0"], W["rw_k_k"], W["rw_k_a"],
                  w_up_pad, a_up_pad, W["rw_g_up"], bd512]

    def rw_pre(i, n, rows, prevs, nexts, consts):
        q = rw_pre_core(i, rows, prevs, consts)
        return [q["r"], q["log_decay"], q["k2"], q["v"], -q["kk"], q["kk"] * q["a"], q["gate"]]

    r_s, w_s, k_s, v_s, a_s, b_s, gate_s = _rowwise(
        "rwkv_pre", rw_pre, T, tT, rows=[p_rkv, p_wa, p_xg], prevs=[p_rkv, p_wa, p_xg], consts=pre_consts,
        outs=[("row", RW_WIDTH, F32)] * 7)
    (at_s, bt_s, kt_s, rt_s, a2v_s, w2v_s, tinv_s, w1_s, a2_s, w2_s,
     plast_s) = _rwkv_chunk_prep(r_s, w_s, k_s, a_s, b_s, v_s)
    y_scan, sa_s, s0_s = _rwkv_chunk_fwd(v_s, at_s, bt_s, kt_s, rt_s, a2v_s, w2v_s, tinv_s, w1_s, plast_s)

    def rw_post_core(rows, consts):
        y, r, k2, v, gate = rows[:5]
        ln_g, ln_b, rk, bd = consts[:4]
        mean = _segsum(y, bd) * (1.0 / RW_HEAD_DIM)
        yc = y - mean
        var = _segsum(yc * yc, bd) * (1.0 / RW_HEAD_DIM)
        rstd = lax.rsqrt(var + RW_LN_EPS)
        yn = yc * rstd
        s = _segsum(r * k2 * rk, bd)
        return dict(yn=yn, rstd=rstd, s=s, pre=yn * ln_g + ln_b + s * v)

    post_consts = [W["rw_ln_g"], W["rw_ln_b"], r_k, bd512]
    (y_a,) = _rowwise("rwkv_post", lambda i, n, r, pv, nx, c: [rw_post_core(r, c)["pre"] * r[4]], T, tT,
                      rows=[y_scan, r_s, k_s, v_s, gate_s], consts=post_consts, outs=[("row", RW_WIDTH, BF16)])

    att = [_att_fwd(p_att, g) for g in range(3)]

    def comb_weights(ls):
        mx = jnp.maximum(jnp.maximum(ls[0], ls[1]), ls[2])
        es = [jnp.exp(l - mx) for l in ls]
        den = es[0] + es[1] + es[2]
        return [e / den for e in es]

    def att_comb(i, n, rows, pv, nx, c):
        wts = comb_weights(rows[3:6])
        return [wts[0] * rows[0] + wts[1] * rows[1] + wts[2] * rows[2]]

    (y_b,) = _rowwise("att_combine", att_comb, T, tT, rows=[att[0][0], att[1][0], att[2][0], att[0][1], att[1][1],
                                                            att[2][1]], outs=[("row", ATT_GROUP_WIDTH, BF16)])

    if late_weights is not None:
        W.update(late_weights(y_b))
    br_a = _mm("branch_a", y_a, W["w_branch_a"], "nn")
    br_b = _mm("branch_b", y_b, W["w_branch_b"], "nn")

    def merge(i, n, rows, pv, nx, c):
        gates = _sigmoid(rows[0] + c[0])
        return [gates[:, :D_MODEL] * rows[1] + gates[:, D_MODEL:] * rows[2]]

    (merged,) = _rowwise("merge", merge, T, tT, rows=[z_gate, br_a, br_b], consts=[W["b_gate"]],
                         outs=[("row", D_MODEL, BF16)])
    x1 = _mm("mix_out", merged, W["w_out"], "nn", add=x)

    (h2,) = _rowwise("norm_ffn", lambda i, n, r, pv, nx, c: [_rms_fwd(r[0], c[0])], T, tT,
                     rows=[x1], consts=[W["g_ffn"]], outs=[("row", D_MODEL, BF16)])
    u = _mm("ffn_up", h2, W["w_up"], "nn", tn=FFN_TILE)

    def conv_core(i, rows, prevs, consts):
        uu, cw, cb = rows[0], consts[0], consts[1]
        u1 = _shift_down(uu, prevs[0], i, 1)
        u2 = _shift_down(uu, prevs[0], i, 2)
        uc = cb + cw[0:1] * uu + cw[1:2] * u1 + cw[2:3] * u2
        return uc[:, :D_FF], uc[:, D_FF:], u1, u2

    def glu(i, n, rows, prevs, nx, consts):
        gate, val, _, _ = conv_core(i, rows, prevs, consts)
        return [_gelu(gate) * val]

    tF = 128
    (act,) = _rowwise("conv_glu", glu, T, tF, rows=[u], prevs=[u], consts=[W["conv_w"], W["conv_b"]],
                      outs=[("row", D_FF, BF16)])
    x2 = _mm("ffn_down", act, W["w_down"], "nn", add=x1)

    (h3,) = _rowwise("norm_ple", lambda i, n, r, pv, nx, c: [_rms_fwd(r[0], c[0])], T, tT,
                     rows=[x2], consts=[W["g_ple"]], outs=[("row", D_MODEL, BF16)])
    z_ple = _mm("ple_gate", h3, W["w_ple_gate"], "nn")
    e_ple = _mm("ple_emb", p, W["w_ple"], "nn")

    def head(i, n, rows, pv, nx, consts):
        x2_, z, e, tgt = rows
        pg = _sigmoid(z)
        x3 = x2_ + pg * e
        y = _rms_fwd(x3, consts[0])
        err = y - tgt
        loss = 0.5 * jnp.sum(jnp.sum(err * err, axis=1, keepdims=True) * (1.0 / D_MODEL), axis=0, keepdims=True)
        dy = err * (1.0 / D_MODEL)
        dx3, dgf = _rms_bwd(x3, consts[0], dy)
        return [dx3, dx3 * pg, dx3 * e * pg * (1.0 - pg), jnp.broadcast_to(loss, (1, LANES)), _colsum(dgf)]

    dx3, de, dz, loss_acc, G["g_final"] = _rowwise(
        "loss_head", head, T, tT, rows=[x2, z_ple, e_ple, target], consts=[W["g_final"].reshape(1, D_MODEL)],
        outs=[("row", D_MODEL, F32), ("row", D_MODEL, BF16), ("row", D_MODEL, BF16), ("acc", (1, LANES)),
              ("acc", (1, D_MODEL))])
    G["w_ple"] = _mm("d_w_ple", p, de, "tn", grad_dtype, out_by_chip=by_chip)
    G["w_ple_gate"] = _mm("d_w_ple_gate", h3, dz, "tn", grad_dtype)
    dh3 = _mm("d_h3", dz, W["w_ple_gate"], "nt")

    def norm_bwd(i, n, rows, pv, nx, consts):
        dx, dg = _rms_bwd(rows[0], consts[0], rows[1])
        return [rows[2] + dx, _colsum(dg)]

    dx2, G["g_ple"] = _rowwise("d_norm_ple", norm_bwd, T, tT, rows=[x2, dh3, dx3], consts=[W["g_ple"]],
                               outs=[("row", D_MODEL, F32), ("acc", (1, D_MODEL))])

    dact = _mm("d_act", dx2, W["w_down"], "nt")
    G["w_down"] = _mm("d_w_down", act, dx2, "tn", grad_dtype)

    def glu_grad(gate, val, da):
        act_, slope = _gelu_and_grad(gate)
        return jnp.concatenate([da * val * slope, da * act_], axis=1)

    def glu_bwd(i, n, rows, prevs, nexts, consts):
        uu, da = rows
        cw = consts[0]
        gate, val, u1, u2 = conv_core(i, rows, prevs, consts)
        duc = glu_grad(gate, val, da)
        dcw = jnp.concatenate([_colsum(duc * uu), _colsum(duc * u1), _colsum(duc * u2)], axis=0)
        gate_n, val_n, _, _ = conv_core(1, [nexts[0]], [uu[tF - SUBLANES:]], consts)
        duc_n = glu_grad(gate_n, val_n, nexts[1])
        du = (cw[0:1] * duc + cw[1:2] * _shift_up(duc, duc_n, i, n, 1) + cw[2:3] * _shift_up(duc, duc_n, i, n, 2))
        return [du, _colsum(duc), dcw]

    du, G["conv_b"], G["conv_w"] = _rowwise(
        "d_conv_glu", glu_bwd, T, tF, rows=[u, dact], prevs=[u], nexts=[u, dact],
        consts=[W["conv_w"], W["conv_b"]],
        outs=[("row", 2 * D_FF, BF16), ("acc", (1, 2 * D_FF)), ("acc", (3, 2 * D_FF))])
    G["w_up"] = _mm("d_w_up", h2, du, "tn", grad_dtype, out_by_chip=by_chip, tn=FFN_TILE)
    dh2 = _mm("d_h2", du, W["w_up"], "nt", tk=FFN_TILE)
    dx1, G["g_ffn"] = _rowwise("d_norm_ffn", norm_bwd, T, tT, rows=[x1, dh2, dx2], consts=[W["g_ffn"]],
                               outs=[("row", D_MODEL, F32), ("acc", (1, D_MODEL))])

    b_gate = W["b_gate"]
    if early_grads is not None:
        b_gate = b_gate + early_grads(G, 0)[0:1, 0:1]
    dmerged = _mm("d_merged", dx1, W["w_out"], "nt")
    G["w_out"] = _mm("d_w_out", merged, dx1, "tn", grad_dtype)

    def merge_bwd(i, n, rows, pv, nx, consts):
        z, a_, b_, dm = rows
        gates = _sigmoid(z + consts[0])
        ga, gb = gates[:, :D_MODEL], gates[:, D_MODEL:]
        dz_ = jnp.concatenate([dm * a_ * ga * (1.0 - ga), dm * b_ * gb * (1.0 - gb)], axis=1)
        return [dm * ga, dm * gb, dz_, _colsum(dz_)]

    d_br_a, d_br_b, dz_gate, G["b_gate"] = _rowwise(
        "d_merge", merge_bwd, T, tT, rows=[z_gate, br_a, br_b, dmerged], consts=[b_gate],
        outs=[("row", D_MODEL, BF16), ("row", D_MODEL, BF16), ("row", 2 * D_MODEL, BF16), ("acc", (1, 2 * D_MODEL))])
    G["w_branch_a"] = _mm("d_w_branch_a", y_a, d_br_a, "tn", grad_dtype, out_by_chip=by_chip)
    G["w_branch_b"] = _mm("d_w_branch_b", y_b, d_br_b, "tn", grad_dtype, out_by_chip=by_chip)
    G["w_gate"] = _mm("d_w_gate", h, dz_gate, "tn", grad_dtype, out_by_chip=by_chip)
    if early_grads is not None:
        post_consts = [post_consts[0] + early_grads(G, 1)[0:1, 0:1]] + post_consts[1:]
    dy_a = _mm("d_y_a", d_br_a, W["w_branch_a"], "nt")
    dy_b = _mm("d_y_b", d_br_b, W["w_branch_b"], "nt")

    def att_comb_bwd(i, n, rows, pv, nx, consts):
        os_, ls, dy = rows[0:3], rows[3:6], rows[6]
        wts = comb_weights(ls)
        dws = [_segsum(dy * o_, consts[0]) for o_ in os_]
        mix = wts[0] * dws[0] + wts[1] * dws[1] + wts[2] * dws[2]
        return [wts[g_] * dy for g_ in range(3)] + [wts[g_] * (dws[g_] - mix) for g_ in range(3)]

    comb = _rowwise("d_att_combine", att_comb_bwd, T, tT,
                    rows=[att[0][0], att[1][0], att[2][0], att[0][1], att[1][1], att[2][1], dy_b], consts=[bd256],
                    outs=[("row", ATT_GROUP_WIDTH, F32)] * 6)
    dqkv = [_att_bwd(p_att, att[g][0], att[g][1], comb[g], comb[3 + g], g) for g in range(3)]
    dp_att = jnp.concatenate([dqkv[g][part] for part in range(3) for g in range(3)], axis=1).astype(BF16)

    def rw_post_bwd(i, n, rows, pv, nx, consts):
        y, r, k2, v, gate, dya = rows
        ln_g, ln_b, rk, bd = consts
        q = rw_post_core(rows, consts)
        dpre = dya * gate
        dgate = dya * q["pre"]
        dyn = dpre * ln_g
        inv = 1.0 / RW_HEAD_DIM
        dy_scan = q["rstd"] * (dyn - _segsum(dyn, bd) * inv - q["yn"] * (_segsum(dyn * q["yn"], bd) * inv))
        ds = _segsum(dpre * v, bd)
        return [dy_scan, dgate, ds * k2 * rk, ds * r * rk, dpre * q["s"],
                _colsum(dpre * q["yn"]), _colsum(dpre), _colsum(ds * r * k2)]

    dy_scan, dgate, dr_b, dk2_b, dv_b, G["rw_ln_g"], G["rw_ln_b"], d_rk = _rowwise(
        "d_rwkv_post", rw_post_bwd, T, tT, rows=[y_scan, r_s, k_s, v_s, gate_s, dy_a], consts=post_consts,
        outs=[("row", RW_WIDTH, F32)] * 5 + [("acc", (1, RW_WIDTH))] * 3)
    G["rw_r_k"] = d_rk.reshape(RW_HEADS, RW_HEAD_DIM)

    dr_s, dw_s, dk_s, da_s, db_s, dv_s = _rwkv_chunk_bwd(r_s, w_s, k_s, a_s, b_s, v_s, dy_scan, s0_s, tinv_s, w1_s,
                                                         a2_s, w2_s, sa_s)

    def rw_pre_bwd(i, n, rows, prevs, nx, consts):
        q = rw_pre_core(i, rows, prevs, consts)
        (mrkv, mwa, mxg, w0, a0, k_k, k_a, wup, aup, gup, bd) = consts
        dr, dlogdecay, dk2, dv, dav, dbv, dgate_ = rows[3:10]
        dr = dr + rows[10]
        dk2 = dk2 + rows[11]
        dv = dv + rows[12]
        a, k, kk = q["a"], q["k"], q["kk"]
        dk = dk2 * (1.0 + (a - 1.0) * k_a)
        da = dk2 * k * k_a + dbv * kk
        dkk = dbv * a - dav
        live = q["nrm"] > 1e-12
        dkkp = jnp.where(live, dkk - kk * _segsum(dkk * kk, bd), dkk) / q["nrm_c"]
        dk = dk + dkkp * k_k
        dlw = dlogdecay * q["log_decay"] * _sigmoid(-q["lw"])
        dla = da * a * (1.0 - a)
        nt = (((1,), (1,)), ((), ()))
        dtw = lax.dot_general(dlw.astype(BF16), wup.astype(BF16), nt, preferred_element_type=F32)
        dxa = lax.dot_general(dla.astype(BF16), aup.astype(BF16), nt, preferred_element_type=F32)
        dm_wa = dtw * (1.0 - q["tw"] * q["tw"]) + dxa
        dsg = lax.dot_general(dgate_.astype(BF16), gup.astype(BF16), nt, preferred_element_type=F32)
        dm_xg = dsg * q["sg"] * (1.0 - q["sg"])
        dm_rkv = jnp.concatenate([dr, dk, dv], axis=1)
        prkv, pwa, pxg = rows[:3]
        dmu = jnp.concatenate([_colsum(dm_rkv * (_shift_down(prkv, prevs[0], i, 1) - prkv)),
                               _colsum(dm_wa * (_shift_down(pwa, prevs[1], i, 1) - pwa)),
                               _colsum(dm_xg * (_shift_down(pxg, prevs[2], i, 1) - pxg))], axis=1)
        return [dm_rkv, dm_wa, dm_xg, dlw, dla, q["tw"], q["m_wa"], q["sg"], dmu,
                _colsum(dlw), _colsum(dla), _colsum(dkkp * k), _colsum(dk2 * k * (a - 1.0))]

    (dm_rkv, dm_wa, dm_xg, dlw, dla, tw_s, mwa_s, sg_s, G["rw_mu"], G["rw_w0"], G["rw_a0"], G["rw_k_k"],
     G["rw_k_a"]) = _rowwise(
        "d_rwkv_pre", rw_pre_bwd, T, tT,
        rows=[p_rkv, p_wa, p_xg, dr_s, dw_s, dk_s, dv_s, da_s, db_s, dgate, dr_b, dk2_b, dv_b],
        prevs=[p_rkv, p_wa, p_xg], consts=pre_consts,
        outs=[("row", RKV, F32), ("row", WA, F32), ("row", XG, F32), ("row", RW_WIDTH, BF16),
              ("row", RW_WIDTH, BF16), ("row", WA, BF16), ("row", WA, BF16), ("row", XG, BF16),
              ("acc", (1, RW_COLS))] + [("acc", (1, RW_WIDTH))] * 4)
    G["rw_w_up"] = _mm("d_rw_w_up", tw_s, dlw, "tn", grad_dtype)[:64]
    G["rw_a_up"] = _mm("d_rw_a_up", mwa_s, dla, "tn", grad_dtype)[64:]
    G["rw_g_up"] = _mm("d_rw_g_up", sg_s, dgate, "tn", grad_dtype)

    def shift_bwd(i, n, rows, pv, nexts, consts):
        return [rows[j] * (1.0 - consts[j]) + _shift_up(rows[j], nexts[j], i, n, 1) * consts[j] for j in range(3)]

    dp_rkv, dp_wa, dp_xg = _rowwise(
        "d_token_shift", shift_bwd, T, tT, rows=[dm_rkv, dm_wa, dm_xg], nexts=[dm_rkv, dm_wa, dm_xg],
        consts=[mu_rkv, mu_wa, mu_xg], outs=[("row", RKV, BF16), ("row", WA, BF16), ("row", XG, BF16)])

    G["w_in"] = jnp.concatenate([_mm("d_w_rkv", h, dp_rkv, "tn", grad_dtype), _mm("d_w_wa", h, dp_wa, "tn", grad_dtype),
                                 _mm("d_w_xg", h, dp_xg, "tn", grad_dtype), _mm("d_w_att", h, dp_att, "tn", grad_dtype, tn=768)], axis=1)
    if early_grads is not None:
        w_wa = w_wa + early_grads(G, 2)[0:1, 0:1].astype(w_wa.dtype)
    dh = _mm("d_h_gate", dz_gate, W["w_gate"], "nt")
    dh = _mm("d_h_rkv", dp_rkv, w_rkv, "nt", add=dh)
    dh = _mm("d_h_wa", dp_wa, w_wa, "nt", add=dh)
    dh = _mm("d_h_xg", dp_xg, w_xg, "nt", add=dh)
    dh = _mm("d_h_att", dp_att, w_att, "nt", add=dh)
    dx, G["g_mix"] = _rowwise("d_norm_mix", norm_bwd, T, tT, rows=[x, dh, dx1], consts=[W["g_mix"]],
                              outs=[("row", D_MODEL, F32), ("acc", (1, D_MODEL))])
    return loss_acc[:, :1], dx, G


HBM_SPEC = pl.BlockSpec(memory_space=pltpu.HBM)


def _place():
    x, y, c = lax.axis_index("x"), lax.axis_index("y"), lax.axis_index("c")
    return x, y, c, [(1 - x, y), (x, 1 - y), (1 - x, 1 - y)]


def _remote(src, dst, send_sems, recv_sems, k, to):
    return pltpu.make_async_remote_copy(src_ref=src, dst_ref=dst, send_sem=send_sems.at[k], recv_sem=recv_sems.at[k],
                                        device_id=to, device_id_type=MESH)


ROW_ALIGN = 16


def _splits(rows):
    return rows % (2 * ROW_ALIGN) == 0


def _half_rows(ref_rows, c, first):
    half = ref_rows // 2
    which = c if first else 1 - c
    return pl.ds(pl.multiple_of(which * half, ROW_ALIGN), half)


def _gather_chips(shards):
    n = len(shards)
    split = [_splits(s.shape[0]) for s in shards]

    def body(*refs):
        w_refs, out_refs = refs[:n], refs[n:2 * n]
        send_sems, recv_sems = refs[2 * n:]
        x, y, c, chips = _place()
        me = 2 * x + y
        sends, passed = [], []
        for i in range(n):
            for j, (px, py) in enumerate(chips):
                if split[i]:
                    mine = _half_rows(w_refs[i].shape[0], c, True)
                    cp = _remote(w_refs[i].at[mine], out_refs[i].at[me, mine], send_sems, recv_sems, 6 * i + j,
                                 (px, py, c))
                else:
                    cp = _remote(w_refs[i], out_refs[i].at[me], send_sems, recv_sems, 6 * i + j, (px, py, c))
                cp.start()
                sends.append(cp)
        for i in range(n):
            for j, (px, py) in enumerate(chips):
                if split[i]:
                    landed = out_refs[i].at[2 * px + py, _half_rows(w_refs[i].shape[0], c, True)]
                    _remote(landed, landed, send_sems, recv_sems, 6 * i + j, (px, py, c)).wait_recv()
                    cp = _remote(landed, landed, send_sems, recv_sems, 6 * i + 3 + j, (x, y, 1 - c))
                    cp.start()
                    passed.append(cp)
                else:
                    landed = out_refs[i].at[2 * px + py]
                    _remote(landed, landed, send_sems, recv_sems, 6 * i + j, (px, py, c)).wait_recv()
        for i in range(n):
            if split[i]:
                for j, (px, py) in enumerate(chips):
                    landed = out_refs[i].at[2 * px + py, _half_rows(w_refs[i].shape[0], c, False)]
                    _remote(landed, landed, send_sems, recv_sems, 6 * i + 3 + j, (x, y, 1 - c)).wait_recv()
        for cp in sends + passed:
            cp.wait_send()

    outs = pl.pallas_call(
        body, name="gather_weights", in_specs=[HBM_SPEC] * n, out_specs=[HBM_SPEC] * n,
        out_shape=[jax.ShapeDtypeStruct((N_CHIPS,) + s.shape, s.dtype) for s in shards],
        scratch_shapes=[pltpu.SemaphoreType.DMA((6 * n,)), pltpu.SemaphoreType.DMA((6 * n,))],
    )(*shards)
    me = 2 * lax.axis_index("x") + lax.axis_index("y")
    return [lax.dynamic_update_slice(o, s[None], (me, 0, 0)) for o, s in zip(outs, shards, strict=True)]


def _join_halves(reds):
    n = len(reds)

    def body(*refs):
        r_refs, out_refs = refs[:n], refs[n:2 * n]
        send_sems, recv_sems = refs[2 * n:]
        x, y, c, _ = _place()
        cps = []
        for i in range(n):
            mine = _half_rows(out_refs[i].shape[0], c, True)
            cp = _remote(r_refs[i], out_refs[i].at[mine], send_sems, recv_sems, i, (x, y, 1 - c))
            cp.start()
            cps.append(cp)
        for cp in cps:
            cp.wait()

    outs = pl.pallas_call(
        body, name="join_halves", in_specs=[HBM_SPEC] * n, out_specs=[HBM_SPEC] * n,
        out_shape=[jax.ShapeDtypeStruct((2 * r.shape[0], r.shape[1]), r.dtype) for r in reds],
        scratch_shapes=[pltpu.SemaphoreType.DMA((n,)), pltpu.SemaphoreType.DMA((n,))],
    )(*reds)
    c = lax.axis_index("c")
    return [lax.dynamic_update_slice(o, r, (c * r.shape[0], 0)) for o, r in zip(outs, reds, strict=True)]


def _gather_all(vec):
    def body(v_ref, out_ref, send_sems, recv_sems, local_sem):
        x, y, c, _ = _place()
        me = 4 * x + 2 * y + c
        local = pltpu.make_async_copy(v_ref, out_ref.at[me], local_sem)
        local.start()
        peers = [(x ^ (k >> 2), y ^ ((k >> 1) & 1), c ^ (k & 1)) for k in range(1, N_DEV)]
        sends = [_remote(v_ref, out_ref.at[me], send_sems, recv_sems, k, to) for k, to in enumerate(peers)]
        for cp in sends:
            cp.start()
        for k, (px, py, pc) in enumerate(peers):
            landed = out_ref.at[4 * px + 2 * py + pc]
            _remote(landed, landed, send_sems, recv_sems, k, (px, py, pc)).wait_recv()
        for cp in sends:
            cp.wait_send()
        local.wait()

    return pl.pallas_call(
        body, name="gather_small", in_specs=[HBM_SPEC], out_specs=HBM_SPEC,
        out_shape=jax.ShapeDtypeStruct((N_DEV,) + vec.shape, vec.dtype),
        scratch_shapes=[pltpu.SemaphoreType.DMA((7,)), pltpu.SemaphoreType.DMA((7,)), pltpu.SemaphoreType.DMA],
    )(vec)


SEM_SPEC = pl.BlockSpec(memory_space=pltpu.SEMAPHORE)
PEERS = N_DEV - 1
DATAFLOW = pltpu.SideEffectType.DATAFLOW_SIDE_EFFECTING


def _travel_copies(mode, src_refs, land_refs, send_sems, recv_sems):
    x, y, c, chips = _place()
    me = 2 * x + y
    pairs = []
    for i, (src, land) in enumerate(zip(src_refs, land_refs, strict=True)):
        if mode == "scatter":
            for k in range(1, N_DEV):
                px, py, pc = x ^ (k >> 2), y ^ ((k >> 1) & 1), c ^ (k & 1)
                mine = src.at[2 * px + py, _half_rows(src.shape[1], pc, True)]
                there, here = land.at[4 * x + 2 * y + c], land.at[4 * px + 2 * py + pc]
                send = functools.partial(_remote, mine, there, send_sems, recv_sems, PEERS * i + k - 1, (px, py, pc))
                arrival = functools.partial(_remote, mine, here, send_sems, recv_sems, PEERS * i + k - 1, (px, py, pc))
                pairs.append((send, arrival))
            continue
        for j, (px, py) in enumerate(chips):
            peer = 2 * px + py
            if _splits(src.shape[0]):
                rows = _half_rows(src.shape[0], c, True)
                mine, there, here = src.at[rows], land.at[me, rows], land.at[peer, rows]
            else:
                mine, there, here = src, land.at[me], land.at[peer]
            send = functools.partial(_remote, mine, there, send_sems, recv_sems, PEERS * i + j, (px, py, c))
            arrival = functools.partial(_remote, mine, here, send_sems, recv_sems, PEERS * i + j, (px, py, c))
            pairs.append((send, arrival))
    return pairs


def _share_halves(name, lands):
    idx = [i for i, a in enumerate(lands) if _splits(a.shape[1])]
    n = len(idx)

    def body(*refs):
        in_refs, out_refs = refs[:n], refs[n:2 * n]
        send_sems, recv_sems = refs[2 * n:]
        x, y, c, chips = _place()
        cps = []
        for i, (src, dst) in enumerate(zip(in_refs, out_refs, strict=True)):
            for j, (px, py) in enumerate(chips):
                mine = _half_rows(src.shape[1], c, True)
                cp = _remote(src.at[2 * px + py, mine], dst.at[2 * px + py, mine], send_sems, recv_sems, 3 * i + j,
                             (x, y, 1 - c))
                cp.start()
                cps.append(cp)
        for i, dst in enumerate(out_refs):
            for j, (px, py) in enumerate(chips):
                theirs = dst.at[2 * px + py, _half_rows(dst.shape[1], c, False)]
                _remote(theirs, theirs, send_sems, recv_sems, 3 * i + j, (x, y, 1 - c)).wait_recv()
        for cp in cps:
            cp.wait_send()

    outs = pl.pallas_call(
        body, name=name, in_specs=[HBM_SPEC] * n, out_specs=[HBM_SPEC] * n,
        out_shape=[jax.ShapeDtypeStruct(lands[i].shape, lands[i].dtype) for i in idx],
        input_output_aliases={i: i for i in range(n)},
        scratch_shapes=[pltpu.SemaphoreType.DMA((3 * n,)), pltpu.SemaphoreType.DMA((3 * n,))],
    )(*[lands[i] for i in idx])
    done = list(lands)
    for i, o in zip(idx, outs, strict=True):
        done[i] = o
    return done


def _travel_start(name, mode, srcs):
    n = len(srcs)
    lands = [lax.empty((N_CHIPS,) + s.shape if mode == "gather" else (N_DEV, s.shape[1] // 2, s.shape[2]), s.dtype)
             for s in srcs]

    def body(*refs):
        src_refs, land_refs = refs[:n], refs[n:2 * n]
        send_sems, recv_sems = refs[2 * n], refs[2 * n + 1]
        token = refs[-1]
        for send, _ in _travel_copies(mode, src_refs, land_refs, send_sems, recv_sems):
            send().start()
        token[...] = jnp.zeros_like(token)

    hbm = lambda a: pltpu.HBM(a.shape, a.dtype)
    outs = pl.pallas_call(
        body, name=name,
        out_shape=(pltpu.SemaphoreType.DMA((PEERS * n,)), pltpu.SemaphoreType.DMA((PEERS * n,)),
                   *[hbm(s) for s in srcs],
                   *[hbm(a) for a in lands], jax.ShapeDtypeStruct((SUBLANES, LANES), F32)),
        in_specs=[HBM_SPEC] * (2 * n),
        out_specs=(SEM_SPEC, SEM_SPEC, *[HBM_SPEC] * (2 * n), pl.BlockSpec(memory_space=pltpu.VMEM)),
        input_output_aliases={i: 2 + i for i in range(2 * n)},
        compiler_params=pltpu.CompilerParams(has_side_effects=DATAFLOW),
    )(*[pltpu.with_memory_space_constraint(a, pltpu.HBM) for a in list(srcs) + lands])
    return outs[0], outs[1], list(outs[2:2 + n]), list(outs[2 + n:2 + 2 * n]), outs[-1]


def _travel_wait(name, mode, send_sems, recv_sems, srcs, lands, after):
    n = len(srcs)

    def body(*refs):
        src_refs, land_refs = refs[:n], refs[n:2 * n]
        send_sems_, recv_sems_ = refs[2 * n], refs[2 * n + 1]
        for send, arrival in _travel_copies(mode, src_refs, land_refs, send_sems_, recv_sems_):
            send().wait_send()
            arrival().wait_recv()

    hbm = lambda a: pltpu.HBM(a.shape, a.dtype)
    outs = pl.pallas_call(
        body, name=name, out_shape=tuple(hbm(a) for a in list(srcs) + list(lands)),
        in_specs=[HBM_SPEC] * (2 * n) + [SEM_SPEC, SEM_SPEC, pl.BlockSpec(memory_space=pl.ANY)],
        out_specs=tuple([HBM_SPEC] * (2 * n)), input_output_aliases={i: i for i in range(2 * n)},
        compiler_params=pltpu.CompilerParams(has_side_effects=DATAFLOW),
    )(*srcs, *lands, send_sems, recv_sems, after)
    c = lax.axis_index("c")
    me = 2 * lax.axis_index("x") + lax.axis_index("y")
    if mode == "gather":
        slot, own = me, [s[None] for s in outs[:n]]
    else:
        slot = 2 * me + c
        own = [lax.dynamic_slice(s, (me, c * (s.shape[1] // 2), 0), (1, s.shape[1] // 2, s.shape[2])) for s in outs[:n]]
    return [lax.dynamic_update_slice(a, o, (slot,) + (0,) * (a.ndim - 1)) for a, o in zip(outs[n:], own, strict=True)]


SUM_TILE_BYTES = 4 * 1024 * 1024


def _sum_rows(half, cols):
    best = ROW_ALIGN
    for t in range(ROW_ALIGN, half + 1, ROW_ALIGN):
        if half % t == 0 and N_CHIPS * t * cols * 4 <= SUM_TILE_BYTES:
            best = t
    return best


def _sum_devices(name, parts):
    n, H, C = parts.shape
    tr = _sum_rows(H, C)

    def body(p_ref, o_ref):
        acc = p_ref[0].astype(F32)
        for k in range(1, n):
            acc = acc + p_ref[k].astype(F32)
        o_ref[...] = acc

    return pl.pallas_call(
        body, name=name, grid=(H // tr,),
        in_specs=[pl.BlockSpec((n, tr, C), lambda i: (0, i, 0))],
        out_specs=pl.BlockSpec((tr, C), lambda i: (i, 0)),
        out_shape=jax.ShapeDtypeStruct((H, C), F32),
        compiler_params=_params(("parallel",)),
    )(parts)


def _adamw_math(w, g, m, v):
    m = ADAM_B1 * m + (1.0 - ADAM_B1) * g
    v = ADAM_B2 * v + (1.0 - ADAM_B2) * (g * g)
    m_hat = m / (1.0 - ADAM_B1 ** ADAM_STEP)
    v_hat = v / (1.0 - ADAM_B2 ** ADAM_STEP)
    delta = -ADAM_LR * (m_hat / (jnp.sqrt(v_hat) + ADAM_EPS) + ADAM_WD * w)
    return delta, m, v


def _adamw(name, w, g, m, v):
    R, C = w.shape
    tr = R
    if R % SUBLANES == 0:
        for cand in range(SUBLANES, min(R, 256) + 1, SUBLANES):
            if R % cand == 0:
                tr = cand

    def body(w_ref, g_ref, m_ref, v_ref, d_ref, nm_ref, nv_ref):
        d, nm, nv = _adamw_math(w_ref[...], g_ref[...], m_ref[...], v_ref[...])
        d_ref[...] = d
        nm_ref[...] = nm
        nv_ref[...] = nv

    spec = pl.BlockSpec((tr, C), lambda i: (i, 0))
    shape = jax.ShapeDtypeStruct((R, C), F32)
    return pl.pallas_call(
        body, name=name, grid=(R // tr,), in_specs=[spec] * 4, out_specs=[spec] * 3, out_shape=[shape] * 3,
        compiler_params=_params(("parallel",)),
    )(w, g, m, v)


SMALL_ROW = 2048


def _small_layout(shapes):
    places, row = [], 0
    for R, C in shapes:
        pieces = []
        for r in range(R):
            for c0 in range(0, C, SMALL_ROW):
                pieces.append((r, c0, min(C, c0 + SMALL_ROW), row))
                row += 1
        places.append(pieces)
    return places, -(-row // SUBLANES) * SUBLANES


def _put_rows(block_ref, refs, places):
    block_ref[...] = jnp.zeros_like(block_ref)
    for ref, pieces in zip(refs, places, strict=True):
        for r, c0, c1, row in pieces:
            block_ref[row:row + 1, 0:c1 - c0] = ref[r:r + 1, c0:c1]


def _take_rows(block, refs, places):
    for ref, pieces in zip(refs, places, strict=True):
        for r, c0, c1, row in pieces:
            ref[r:r + 1, c0:c1] = block[row:row + 1, 0:c1 - c0]


def _pack_small(arrs):
    places, rows = _small_layout([a.shape for a in arrs])

    def body(*refs):
        _put_rows(refs[-1], refs[:-1], places)

    return pl.pallas_call(body, name="pack_small", out_shape=jax.ShapeDtypeStruct((rows, SMALL_ROW), F32),
                          compiler_params=_params())(*arrs)


def _adamw_small(parts, ws, ms, vs, extra_shapes):
    n_dev, rows, _ = parts.shape
    n = len(ws)
    places, rows_ = _small_layout([w.shape for w in ws] + list(extra_shapes))
    assert rows_ == rows, (rows_, rows)

    def body(*refs):
        p_ref = refs[0]
        w_refs, m_refs, v_refs = refs[1:1 + n], refs[1 + n:1 + 2 * n], refs[1 + 2 * n:1 + 3 * n]
        outs = refs[1 + 3 * n:-3]
        wb, mb, vb = refs[-3:]
        for block, srcs in ((wb, w_refs), (mb, m_refs), (vb, v_refs)):
            _put_rows(block, srcs, places[:n])
        g = p_ref[0]
        for k in range(1, n_dev):
            g = g + p_ref[k]
        d, nm, nv = _adamw_math(wb[...], g, mb[...], vb[...])
        _take_rows(g, outs[0:n], places[:n])
        _take_rows(d, outs[n:2 * n], places[:n])
        _take_rows(nm, outs[2 * n:3 * n], places[:n])
        _take_rows(nv, outs[3 * n:4 * n], places[:n])
        _take_rows(g, outs[4 * n:], places[n:])

    shapes = [jax.ShapeDtypeStruct(w.shape, F32) for w in ws]
    res = pl.pallas_call(
        body, name="adamw_small", out_shape=shapes * 4 + [jax.ShapeDtypeStruct(s, F32) for s in extra_shapes],
        scratch_shapes=[pltpu.VMEM((rows, SMALL_ROW), F32)] * 3, compiler_params=_params(),
    )(parts, *ws, *ms, *vs)
    return res[0:n], res[n:2 * n], res[2 * n:3 * n], res[3 * n:4 * n], res[4 * n:]


WEIGHTS = ['g_mix', 'w_in', 'rw_mu', 'rw_w0', 'rw_w_up', 'rw_a0', 'rw_a_up', 'rw_g_up', 'rw_k_k', 'rw_k_a',
           'rw_r_k', 'rw_ln_g', 'rw_ln_b', 'w_branch_a', 'w_branch_b', 'w_gate', 'b_gate', 'w_out', 'g_ffn', 'w_up',
           'conv_w', 'conv_b', 'w_down', 'g_ple', 'w_ple_gate', 'w_ple', 'g_final']
ARG_NAMES = (['x', 'p'] + WEIGHTS + ['loss_target'] + ['m_' + n for n in WEIGHTS] + ['v_' + n for n in WEIGHTS])
SHARDED = {'w_in': 1, 'rw_w_up': 1, 'rw_a_up': 1, 'rw_g_up': 1, 'w_branch_a': 1, 'w_branch_b': 1, 'w_gate': 1,
           'w_out': 0, 'w_up': 1, 'conv_w': 1, 'w_down': 0, 'w_ple_gate': 0, 'w_ple': 1}
SMALL = [n for n in WEIGHTS if n not in SHARDED]
WHOLE = ['conv_w']
FIRST_USED = ['w_in', 'rw_w_up', 'rw_a_up', 'rw_g_up', 'w_gate']
READ_BY_CHIP = ['w_gate', 'w_branch_a', 'w_branch_b', 'w_up', 'w_ple']
FIRST_DONE = [['w_up', 'w_down', 'w_ple_gate', 'w_ple'], ['w_out', 'w_branch_a', 'w_branch_b', 'w_gate'],
              ['w_in', 'rw_w_up', 'rw_a_up', 'rw_g_up']]
SPLIT = [n for n in SHARDED if n not in WHOLE]


def _full_from_shards(stack, axis):
    _, R, C = stack.shape
    if axis == 0:
        return stack.reshape(N_CHIPS * R, C)
    return stack.transpose(1, 0, 2).reshape(R, N_CHIPS * C)


def _shards_from_full(full, axis):
    R, C = full.shape
    if axis == 0:
        return full.reshape(N_CHIPS, R // N_CHIPS, C)
    return full.reshape(R, N_CHIPS, C // N_CHIPS).transpose(1, 0, 2)


def kernel(x, p, g_mix, w_in, rw_mu, rw_w0, rw_w_up, rw_a0, rw_a_up, rw_g_up, rw_k_k, rw_k_a, rw_r_k, rw_ln_g, rw_ln_b, w_branch_a, w_branch_b, w_gate, b_gate, w_out, g_ffn, w_up, conv_w, conv_b, w_down, g_ple, w_ple_gate, w_ple, g_final, loss_target, m_g_mix, m_w_in, m_rw_mu, m_rw_w0, m_rw_w_up, m_rw_a0, m_rw_a_up, m_rw_g_up, m_rw_k_k, m_rw_k_a, m_rw_r_k, m_rw_ln_g, m_rw_ln_b, m_w_branch_a, m_w_branch_b, m_w_gate, m_b_gate, m_w_out, m_g_ffn, m_w_up, m_conv_w, m_conv_b, m_w_down, m_g_ple, m_w_ple_gate, m_w_ple, m_g_final, v_g_mix, v_w_in, v_rw_mu, v_rw_w0, v_rw_w_up, v_rw_a0, v_rw_a_up, v_rw_g_up, v_rw_k_k, v_rw_k_a, v_rw_r_k, v_rw_ln_g, v_rw_ln_b, v_w_branch_a, v_w_branch_b, v_w_gate, v_b_gate, v_w_out, v_g_ffn, v_w_up, v_conv_w, v_conv_b, v_w_down, v_g_ple, v_w_ple_gate, v_w_ple, v_g_final):
    given = dict(zip(ARG_NAMES, (x, p, g_mix, w_in, rw_mu, rw_w0, rw_w_up, rw_a0, rw_a_up, rw_g_up, rw_k_k, rw_k_a, rw_r_k, rw_ln_g, rw_ln_b, w_branch_a, w_branch_b, w_gate, b_gate, w_out, g_ffn, w_up, conv_w, conv_b, w_down, g_ple, w_ple_gate, w_ple, g_final, loss_target, m_g_mix, m_w_in, m_rw_mu, m_rw_w0, m_rw_w_up, m_rw_a0, m_rw_a_up, m_rw_g_up, m_rw_k_k, m_rw_k_a, m_rw_r_k, m_rw_ln_g, m_rw_ln_b, m_w_branch_a, m_w_branch_b, m_w_gate, m_b_gate, m_w_out, m_g_ffn, m_w_up, m_conv_w, m_conv_b, m_w_down, m_g_ple, m_w_ple_gate, m_w_ple, m_g_final, v_g_mix, v_w_in, v_rw_mu, v_rw_w0, v_rw_w_up, v_rw_a0, v_rw_a_up, v_rw_g_up, v_rw_k_k, v_rw_k_a, v_rw_r_k, v_rw_ln_g, v_rw_ln_b, v_w_branch_a, v_w_branch_b, v_w_gate, v_b_gate, v_w_out, v_g_ffn, v_w_up, v_conv_w, v_conv_b, v_w_down, v_g_ple, v_w_ple_gate, v_w_ple, v_g_final), strict=True))

    def two_d(name, prefix=""):
        a = given[prefix + name]
        if name == "g_final":
            return a.reshape(1, D_MODEL)
        if name == "rw_r_k":
            return a.reshape(1, RW_WIDTH)
        return a[0] if a.ndim == 3 else a

    cast = lambda n: two_d(n) if n in WHOLE else two_d(n).astype(BF16)
    whole = lambda names, stacks: {n: g if n in READ_BY_CHIP else _full_from_shards(g, SHARDED[n])
                                   for n, g in zip(names, stacks, strict=True)}
    late_names = [n for n in SHARDED if n not in FIRST_USED]
    late_sends, late_recvs, late_srcs, late_lands, token = _travel_start(
        "gather_late_start", "gather", [cast(n) for n in late_names])
    W = whole(FIRST_USED, _gather_chips([cast(n) for n in FIRST_USED]))
    for n in SMALL:
        W[n] = two_d(n)
    W["rw_r_k"] = W["rw_r_k"].reshape(RW_HEADS, RW_HEAD_DIM)
    W["g_mix"] = W["g_mix"] + token[0:1, 0:1]

    def late_weights(after):
        lands = _travel_wait("gather_late_wait", "gather", late_sends, late_recvs, late_srcs, late_lands, after)
        return whole(late_names, _share_halves("share_late", lands))

    early_names = [[n for n in SPLIT if n in group] for group in FIRST_DONE]
    assert sorted(sum(early_names, [])) == sorted(SPLIT)
    travelling = []

    def early_grads(G, stage):
        by_chip = [G[n] if n in READ_BY_CHIP else _shards_from_full(G[n], SHARDED[n]) for n in early_names[stage]]
        sends, recvs, srcs, lands, started = _travel_start(f"scatter{stage}_start", "scatter", by_chip)
        travelling.append((sends, recvs, srcs, lands))
        return started

    loss_part, grad_x, G = _local_step(x[0], p[0, 0], W, loss_target[0], late_weights, early_grads, by_chip=True,
                                       grad_dtype=BF16)

    landed = {}
    for stage, (sends, recvs, srcs, lands) in enumerate(travelling):
        landed.update(zip(early_names[stage], _travel_wait(f"scatter{stage}_wait", "scatter", sends, recvs, srcs,
                                                           lands, grad_x), strict=True))
    reduced = [_sum_devices("sum_devices_" + n, landed[n]) for n in SPLIT]
    shard_grads = dict(zip(SPLIT, _join_halves(reduced), strict=True))

    G["rw_r_k"] = G["rw_r_k"].reshape(1, RW_WIDTH)
    extras = [G[n] for n in WHOLE] + [loss_part]
    all_small = _gather_all(_pack_small([G[n] for n in SMALL] + extras))
    gs, ds, nms, nvs, summed = _adamw_small(all_small, [two_d(n) for n in SMALL], [two_d(n, "m_") for n in SMALL],
                                            [two_d(n, "v_") for n in SMALL], [e.shape for e in extras])
    loss = summed[-1][0, 0]
    chip = 2 * lax.axis_index("x") + lax.axis_index("y")
    for n, full in zip(WHOLE, summed[:-1], strict=True):
        width = two_d(n).shape[1]
        shard_grads[n] = lax.dynamic_slice_in_dim(full, chip * width, width, axis=1)

    grads, deltas, new_m, new_v = {}, {}, {}, {}
    for n in SHARDED:
        g = shard_grads[n]
        d, nm, nv = _adamw("adamw_" + n, two_d(n), g, two_d(n, "m_"), two_d(n, "v_"))
        grads[n], deltas[n], new_m[n], new_v[n] = g, d, nm, nv
    for i, n in enumerate(SMALL):
        grads[n], deltas[n], new_m[n], new_v[n] = gs[i], ds[i], nms[i], nvs[i]
    outs = [loss, grad_x[None]]
    for table in (grads, deltas, new_m, new_v):
        outs += [table[n].reshape(given[n].shape) for n in WEIGHTS]
    return tuple(outs)
```

```python
import functools
import math

import jax
import jax.numpy as jnp
import numpy as np
from jax import lax
from jax.experimental import pallas as pl
from jax.experimental.pallas import tpu as pltpu

F32 = jnp.float32
BF16 = jnp.bfloat16

D_MODEL = 1024
NORM_EPS = 1e-6
RW_HEADS = 8
RW_HEAD_DIM = 64
RW_WIDTH = 512
RW_LN_EPS = 64e-5
ATT_GROUP_DILATION = (1, 4, 16)
ATT_BLOCK = 128
ATT_HEADS = 12
ATT_HEAD_DIM = 64
ATT_GROUP_WIDTH = 256
ATT_WIDTH = 768
D_FF = 3072

ADAM_LR = 0.001
ADAM_B1 = 0.9
ADAM_B2 = 0.999
ADAM_EPS = 1e-08
ADAM_WD = 0.01
ADAM_STEP = 10

SUBLANES = 8
LANES = 128
VMEM_LIMIT = 56 * 1024 * 1024
N_CHIPS = 4
N_DEV = 8
MESH = pl.DeviceIdType.MESH


def _params(sem=None):
    return pltpu.CompilerParams(dimension_semantics=sem, vmem_limit_bytes=VMEM_LIMIT)


def _pick(dim, pref):
    if dim % LANES != 0 or dim <= pref:
        return dim
    best = LANES
    for t in range(LANES, pref + 1, LANES):
        if dim % t == 0:
            best = t
    return best


def _mm(name, a, b, mode, out_dtype=F32, add=None, tm=1024, tn=1024, tk=1024, out_by_chip=False, post=None):
    by_chip = b.ndim == 3
    b_rows, b_cols = (b.shape[1], N_CHIPS * b.shape[2]) if by_chip else b.shape
    if mode == "nn":
        (M, K), (K2, N) = a.shape, (b_rows, b_cols)
    elif mode == "nt":
        (M, K), (N, K2) = a.shape, (b_rows, b_cols)
    else:
        (K, M), (K2, N) = a.shape, (b_rows, b_cols)
    assert K == K2, (name, a.shape, b.shape, mode)
    assert not (by_chip and mode == "tn") and not (out_by_chip and add is not None), name
    tm = _pick(M, tm)
    n_cut, k_cut = out_by_chip or (by_chip and mode == "nn"), by_chip and mode == "nt"
    tn = _pick(N // N_CHIPS, tn) if n_cut else _pick(N, tn)
    tk = _pick(K // N_CHIPS, tk) if k_cut else _pick(K, tk)
    nk = K // tk
    per_n = (N // N_CHIPS) // tn if n_cut else 1
    per_k = (K // N_CHIPS) // tk if k_cut else 1
    if mode == "nn":
        a_spec = pl.BlockSpec((tm, tk), lambda i, j, k: (i, k))
        b_spec = (pl.BlockSpec((None, tk, tn), lambda i, j, k: (j // per_n, k, j % per_n)) if by_chip
                  else pl.BlockSpec((tk, tn), lambda i, j, k: (k, j)))
        dims = (((1,), (0,)), ((), ()))
    elif mode == "nt":
        a_spec = pl.BlockSpec((tm, tk), lambda i, j, k: (i, k))
        b_spec = (pl.BlockSpec((None, tn, tk), lambda i, j, k: (k // per_k, j, k % per_k)) if by_chip
                  else pl.BlockSpec((tn, tk), lambda i, j, k: (j, k)))
        dims = (((1,), (1,)), ((), ()))
    else:
        a_spec = pl.BlockSpec((tk, tm), lambda i, j, k: (k, i))
        b_spec = pl.BlockSpec((tk, tn), lambda i, j, k: (k, j))
        dims = (((0,), (0,)), ((), ()))
    if out_by_chip:
        o_spec = pl.BlockSpec((None, tm, tn), lambda i, j, k: (j // per_n, i, j % per_n))
        out_shape = jax.ShapeDtypeStruct((N_CHIPS, M, N // N_CHIPS), out_dtype)
    else:
        o_spec = pl.BlockSpec((tm, tn), lambda i, j, k: (i, j))
        out_shape = jax.ShapeDtypeStruct((M, N), out_dtype)
    has_add = add is not None
    ins = [a, b] + ([add] if has_add else [])
    in_specs = [a_spec, b_spec] + ([o_spec] if has_add else [])
    n_main = len(ins)
    semantics = ("parallel", "parallel", "arbitrary")
    if post is not None:
        post_fn, post_rows, post_consts, post_outs = post
        assert tn == N and not out_by_chip, name
        ins += list(post_rows) + list(post_consts)
        in_specs += [pl.BlockSpec((tm, r.shape[1]), lambda i, j, k: (i, 0)) for r in post_rows]
        in_specs += [pl.BlockSpec(c.shape, lambda i, j, k, nd=c.ndim: (0,) * nd) for c in post_consts]
        o_spec = [pl.BlockSpec((tm, o[1]), lambda i, j, k: (i, 0)) if o[0] == "row"
                  else pl.BlockSpec(o[1], lambda i, j, k: (0, 0)) for o in post_outs]
        out_shape = [jax.ShapeDtypeStruct((M, o[1]), o[2]) if o[0] == "row" else jax.ShapeDtypeStruct(o[1], F32)
                     for o in post_outs]
        if any(o[0] == "acc" for o in post_outs):
            semantics = ("arbitrary", "arbitrary", "arbitrary")
    n_in = len(ins)

    def body(*refs):
        a_ref, b_ref = refs[:2]
        out_refs, acc_ref = refs[n_in:-1], refs[-1]
        i, k = pl.program_id(0), pl.program_id(2)
        part = lax.dot_general(a_ref[...].astype(BF16), b_ref[...].astype(BF16), dims,
                               preferred_element_type=F32)

        @pl.when(k == 0)
        def _():
            acc_ref[...] = part

        @pl.when(k > 0)
        def _():
            acc_ref[...] += part

        @pl.when(k == nk - 1)
        def _():
            res = acc_ref[...]
            if has_add:
                res = res + refs[2][...].astype(F32)
            if post is None:
                out_refs[0][...] = res.astype(out_refs[0].dtype)
                return
            n_rows = len(post_rows)
            vals = post_fn(res, [r[...] for r in refs[n_main:n_main + n_rows]],
                           [c[...] for c in refs[n_main + n_rows:n_in]])
            for o, o_ref, val in zip(post_outs, out_refs, vals, strict=True):
                if o[0] == "row":
                    o_ref[...] = val.astype(o_ref.dtype)
                else:
                    @pl.when(i == 0)
                    def _(o_ref=o_ref, val=val):
                        o_ref[...] = val.astype(F32)

                    @pl.when(i > 0)
                    def _(o_ref=o_ref, val=val):
                        o_ref[...] += val.astype(F32)

    return pl.pallas_call(
        body, name=name, grid=(M // tm, N // tn, nk),
        in_specs=in_specs, out_specs=o_spec, out_shape=out_shape,
        scratch_shapes=[pltpu.VMEM((tm, tn), F32)],
        compiler_params=_params(semantics),
    )(*ins)


def _rowwise(name, fn, T, tT, rows=(), prevs=(), nexts=(), consts=(), outs=()):
    n = T // tT
    per8 = tT // SUBLANES
    in_specs, ins = [], []
    for arr in rows:
        in_specs.append(pl.BlockSpec((tT, arr.shape[1]), lambda i: (i, 0)))
        ins.append(arr)
    for arr in prevs:
        in_specs.append(pl.BlockSpec((SUBLANES, arr.shape[1]), lambda i: (jnp.maximum(i * per8 - 1, 0), 0)))
        ins.append(arr)
    for arr in nexts:
        in_specs.append(pl.BlockSpec((SUBLANES, arr.shape[1]),
                                     lambda i: (jnp.minimum((i + 1) * per8, T // SUBLANES - 1), 0)))
        ins.append(arr)
    for arr in consts:
        in_specs.append(pl.BlockSpec(arr.shape, lambda i, nd=arr.ndim: (0,) * nd))
        ins.append(arr)
    out_specs, out_shapes = [], []
    for o in outs:
        if o[0] == "row":
            out_specs.append(pl.BlockSpec((tT, o[1]), lambda i: (i, 0)))
            out_shapes.append(jax.ShapeDtypeStruct((T, o[1]), o[2]))
        else:
            out_specs.append(pl.BlockSpec(o[1], lambda i: (0, 0)))
            out_shapes.append(jax.ShapeDtypeStruct(o[1], F32))
    nr, npv, nnx, nc = len(rows), len(prevs), len(nexts), len(consts)
    n_in = nr + npv + nnx + nc

    def body(*refs):
        i = pl.program_id(0)
        vals = [r[...] for r in refs[:n_in]]
        res = fn(i, n, vals[:nr], vals[nr:nr + npv], vals[nr + npv:nr + npv + nnx], vals[nr + npv + nnx:])
        for o, o_ref, val in zip(outs, refs[n_in:], res, strict=True):
            if o[0] == "row":
                o_ref[...] = val.astype(o_ref.dtype)
            else:
                @pl.when(i == 0)
                def _(o_ref=o_ref, val=val):
                    o_ref[...] = val.astype(F32)

                @pl.when(i > 0)
                def _(o_ref=o_ref, val=val):
                    o_ref[...] += val.astype(F32)

    res = pl.pallas_call(
        body, name=name, grid=(n,), in_specs=in_specs, out_specs=out_specs, out_shape=out_shapes,
        compiler_params=_params(("arbitrary",)),
    )(*ins)
    return list(res)


def _shift_down(x, prev8, i, s):
    rolled = pltpu.roll(x, s, 0)
    head = pltpu.roll(prev8, s, 0)
    head = jnp.where(i == 0, jnp.zeros_like(head), head)
    rid = lax.broadcasted_iota(jnp.int32, head.shape, 0)
    first = jnp.where(rid < s, head, rolled[:SUBLANES])
    if x.shape[0] == SUBLANES:
        return first
    return jnp.concatenate([first, rolled[SUBLANES:]], axis=0)


def _shift_up(x, next8, i, n, s):
    tT = x.shape[0]
    rolled = pltpu.roll(x, tT - s, 0)
    tail = pltpu.roll(next8, SUBLANES - s, 0)
    tail = jnp.where(i == n - 1, jnp.zeros_like(tail), tail)
    rid = lax.broadcasted_iota(jnp.int32, tail.shape, 0)
    last = jnp.where(rid >= SUBLANES - s, tail, rolled[tT - SUBLANES:])
    return jnp.concatenate([rolled[:tT - SUBLANES], last], axis=0)


def _colsum(x):
    return jnp.sum(x, axis=0, keepdims=True)


def _segsum(x, bd):
    return jnp.dot(x, bd, precision=lax.Precision.HIGH, preferred_element_type=F32)


def _block_diag_ones(width, seg):
    idx = np.arange(width) // seg
    return jnp.asarray((idx[:, None] == idx[None, :]).astype(np.float32))


def _sigmoid(z):
    return 1.0 / (1.0 + jnp.exp(-z))


def _softplus(z):
    return jnp.maximum(z, 0.0) + jnp.log(1.0 + jnp.exp(-jnp.abs(z)))


def _rms_fwd(x, g):
    r = lax.rsqrt(jnp.mean(x * x, axis=-1, keepdims=True) + NORM_EPS)
    return x * r * g


def _rms_bwd(x, g, dy):
    r = lax.rsqrt(jnp.mean(x * x, axis=-1, keepdims=True) + NORM_EPS)
    gdy = dy * g
    dx = r * (gdy - x * (r * r) * jnp.mean(x * gdy, axis=-1, keepdims=True))
    return dx, dy * x * r


GELU_C = math.sqrt(2.0 / math.pi)


def _gelu(x):
    return 0.5 * x * (1.0 + jnp.tanh(GELU_C * (x + 0.044715 * x * x * x)))


def _gelu_and_grad(x):
    th = jnp.tanh(GELU_C * (x + 0.044715 * x * x * x))
    half = 0.5 * (1.0 + th)
    return x * half, half + 0.5 * x * (1.0 - th * th) * GELU_C * (1.0 + 3.0 * 0.044715 * x * x)


RW_CHUNK = 64
NN = (((1,), (0,)), ((), ()))
NT = (((1,), (1,)), ((), ()))
TN = (((0,), (0,)), ((), ()))


def _hdot(a, b, dims):
    return lax.dot_general(a, b, dims, precision=lax.Precision.HIGH, preferred_element_type=F32)


def _ldot(a, b, dims):
    return lax.dot_general(a.astype(BF16), b.astype(BF16), dims, preferred_element_type=F32)


def _chunk_masks():
    ti = lax.broadcasted_iota(jnp.int32, (RW_CHUNK, RW_CHUNK), 0)
    tj = lax.broadcasted_iota(jnp.int32, (RW_CHUNK, RW_CHUNK), 1)
    return tj <= ti, tj < ti, (ti == tj).astype(F32)


def _head(x, h):
    return x[:, h * RW_HEAD_DIM:(h + 1) * RW_HEAD_DIM]


def _heads(fn):
    return [fn(h) for h in range(RW_HEADS)]


def _chunk_rows(r, lw, k, a, b, incl_f):
    c = _hdot(incl_f, lw, NN)
    e_prev, e_neg, e_pos = jnp.exp(c - lw), jnp.exp(-c), jnp.exp(c)
    return dict(At=a * e_prev, Bt=b * e_neg, Kt=k * e_neg, Rt=r * e_pos, e_prev=e_prev, e_neg=e_neg, e_pos=e_pos)


def _chunk_coeffs(q, incl, strict):
    A1 = _heads(lambda h: jnp.where(strict, _hdot(_head(q["At"], h), _head(q["Bt"], h), NT), 0.0))
    A2 = _heads(lambda h: jnp.where(strict, _hdot(_head(q["At"], h), _head(q["Kt"], h), NT), 0.0))
    W1 = _heads(lambda h: jnp.where(incl, _hdot(_head(q["Rt"], h), _head(q["Bt"], h), NT), 0.0))
    W2 = _heads(lambda h: jnp.where(incl, _ldot(_head(q["Rt"], h), _head(q["Kt"], h), NT), 0.0))
    return A1, A2, W1, W2


def _rwkv_chunk_prep(r, lw, k, a, b, v):
    T = r.shape[0]
    nC = T // RW_CHUNK
    H, N = RW_HEADS, RW_HEAD_DIM

    def body(r_ref, lw_ref, k_ref, a_ref, b_ref, v_ref,
             at_ref, bt_ref, kt_ref, rt_ref, a2v_ref, w2v_ref, ti_ref, w1_ref, a2_ref, w2_ref, pl_ref):
        incl, strict, eye = _chunk_masks()
        q = _chunk_rows(r_ref[...], lw_ref[...], k_ref[...], a_ref[...], b_ref[...], incl.astype(F32))
        at_ref[...], bt_ref[...], kt_ref[...], rt_ref[...] = q["At"], q["Bt"], q["Kt"], q["Rt"]
        pl_ref[0] = jnp.broadcast_to(q["e_pos"][RW_CHUNK - 1:RW_CHUNK, :], (SUBLANES, RW_WIDTH))
        A1, A2, W1, W2 = _chunk_coeffs(q, incl, strict)
        V = v_ref[...]
        a2v_ref[...] = jnp.concatenate(_heads(lambda h: _hdot(A2[h], _head(V, h), NN)), axis=1)
        w2v_ref[...] = jnp.concatenate(_heads(lambda h: _ldot(W2[h], _head(V, h), NN)), axis=1)
        tinv, pw = [eye + m for m in A1], A1
        for stage in range(5):
            dot = _hdot if stage == 0 else _ldot
            pw = [dot(m, m, NN) for m in pw]
            tinv = [t + dot(t, m, NN) for t, m in zip(tinv, pw, strict=True)]
        for h in range(H):
            ti_ref[0, h] = tinv[h]
            w1_ref[0, h] = W1[h]
            a2_ref[0, h] = A2[h]
            w2_ref[0, h] = W2[h]

    row_spec = pl.BlockSpec((RW_CHUNK, RW_WIDTH), lambda n: (n, 0))
    st_spec = pl.BlockSpec((1, H, N, N), lambda n: (n, 0, 0, 0))
    row_shape = jax.ShapeDtypeStruct((T, RW_WIDTH), F32)
    st_shape = jax.ShapeDtypeStruct((nC, H, N, N), F32)
    return pl.pallas_call(
        body, name="rwkv_chunk_prep", grid=(nC,),
        in_specs=[row_spec] * 6,
        out_specs=[row_spec] * 6 + [st_spec] * 4 + [pl.BlockSpec((1, SUBLANES, RW_WIDTH), lambda n: (n, 0, 0))],
        out_shape=[row_shape] * 6 + [st_shape] * 4 + [jax.ShapeDtypeStruct((nC, SUBLANES, RW_WIDTH), F32)],
        compiler_params=_params(("parallel",)),
    )(r, lw, k, a, b, v)


def _rwkv_chunk_fwd(v, at, bt, kt, rt, a2v, w2v, tinv, w1, plast):
    T = v.shape[0]
    nC = T // RW_CHUNK
    H, N = RW_HEADS, RW_HEAD_DIM

    def body(v_ref, at_ref, bt_ref, kt_ref, rt_ref, a2v_ref, w2v_ref, ti_ref, w1_ref, pl_ref,
             y_ref, sa_ref, s0_ref, S_ref):
        @pl.when(pl.program_id(0) == 0)
        def _():
            S_ref[...] = jnp.zeros_like(S_ref)

        V, At, Bt, Kt, Rt = v_ref[...], at_ref[...], bt_ref[...], kt_ref[...], rt_ref[...]
        A2V, W2V, p_last = a2v_ref[...], w2v_ref[...], pl_ref[0, 0:1, :]
        S0 = _heads(lambda h: S_ref[h])
        for h in range(H):
            s0_ref[0, h] = S0[h]
        Z = _heads(lambda h: _hdot(_head(At, h), S0[h], NT) + _head(A2V, h))
        Sa = _heads(lambda h: _hdot(ti_ref[0, h], Z[h], NN))
        X = _heads(lambda h: S0[h] + _hdot(Sa[h], _head(Bt, h), TN) + _hdot(_head(V, h), _head(Kt, h), TN))
        for h in range(H):
            S_ref[h] = X[h] * _head(p_last, h)
        Y = _heads(lambda h: _ldot(_head(Rt, h), S0[h], NT) + _ldot(w1_ref[0, h], Sa[h], NN) + _head(W2V, h))
        y_ref[...] = jnp.concatenate(Y, axis=1)
        sa_ref[...] = jnp.concatenate(Sa, axis=1)

    row_spec = pl.BlockSpec((RW_CHUNK, RW_WIDTH), lambda n: (n, 0))
    st_spec = pl.BlockSpec((1, H, N, N), lambda n: (n, 0, 0, 0))
    row_shape = jax.ShapeDtypeStruct((T, RW_WIDTH), F32)
    return pl.pallas_call(
        body, name="rwkv_chunk_fwd", grid=(nC,),
        in_specs=[row_spec] * 7 + [st_spec, st_spec, pl.BlockSpec((1, SUBLANES, RW_WIDTH), lambda n: (n, 0, 0))],
        out_specs=[row_spec, row_spec, st_spec],
        out_shape=[row_shape, row_shape, jax.ShapeDtypeStruct((nC, H, N, N), F32)],
        scratch_shapes=[pltpu.VMEM((H, N, N), F32)],
        compiler_params=_params(("arbitrary",)),
    )(v, at, bt, kt, rt, a2v, w2v, tinv, w1, plast)


def _rwkv_chunk_bwd(r, lw, k, a, b, v, dy, s0, tinv, w1, a2, w2, sa):
    T = r.shape[0]
    nC = T // RW_CHUNK
    H, N = RW_HEADS, RW_HEAD_DIM

    def body(r_ref, lw_ref, k_ref, a_ref, b_ref, v_ref, dy_ref, s0_ref, ti_ref, w1_ref, a2_ref, w2_ref, sa_ref,
             dr_ref, dlw_ref, dk_ref, da_ref, db_ref, dv_ref, dS_ref):
        @pl.when(pl.program_id(0) == 0)
        def _():
            dS_ref[...] = jnp.zeros_like(dS_ref)

        incl, strict, _ = _chunk_masks()
        incl_f = incl.astype(F32)
        q = _chunk_rows(r_ref[...], lw_ref[...], k_ref[...], a_ref[...], b_ref[...], incl_f)
        At, Bt, Kt, Rt = q["At"], q["Bt"], q["Kt"], q["Rt"]
        A2, W1, W2 = (_heads(lambda h, ref=ref: ref[0, h]) for ref in (a2_ref, w1_ref, w2_ref))
        V, dY, Sa = v_ref[...], dy_ref[...], sa_ref[...]
        hd = _head
        p_last = q["e_pos"][RW_CHUNK - 1:RW_CHUNK, :]
        S0 = _heads(lambda h: s0_ref[0, h])
        G = _heads(lambda h: dS_ref[h] * hd(p_last, h))
        X = _heads(lambda h: S0[h] + _ldot(hd(Sa, h), hd(Bt, h), TN) + _ldot(hd(V, h), hd(Kt, h), TN))
        dc_last = jnp.concatenate(_heads(lambda h: jnp.sum(G[h] * X[h], axis=0, keepdims=True)), axis=1)
        dSa = _heads(lambda h: _ldot(hd(Bt, h), G[h], NT) + _ldot(W1[h], hd(dY, h), TN))
        dZ = _heads(lambda h: _ldot(ti_ref[0, h], dSa[h], TN))
        for h in range(H):
            dS_ref[h] = G[h] + _ldot(dZ[h], hd(At, h), TN) + _ldot(hd(dY, h), hd(Rt, h), TN)
        dA1 = _heads(lambda h: jnp.where(strict, _ldot(dZ[h], hd(Sa, h), NT), 0.0))
        dA2 = _heads(lambda h: jnp.where(strict, _ldot(dZ[h], hd(V, h), NT), 0.0))
        dW1 = _heads(lambda h: jnp.where(incl, _ldot(hd(dY, h), hd(Sa, h), NT), 0.0))
        dW2 = _heads(lambda h: jnp.where(incl, _ldot(hd(dY, h), hd(V, h), NT), 0.0))
        cat = lambda fn: jnp.concatenate(_heads(fn), axis=1)
        dV = cat(lambda h: _ldot(A2[h], dZ[h], TN) + _ldot(W2[h], hd(dY, h), TN) + _ldot(hd(Kt, h), G[h], NT))
        dAt = cat(lambda h: _ldot(dA1[h], hd(Bt, h), NN) + _ldot(dA2[h], hd(Kt, h), NN) + _ldot(dZ[h], S0[h], NN))
        dBt = cat(lambda h: _ldot(dA1[h], hd(At, h), TN) + _ldot(dW1[h], hd(Rt, h), TN) + _ldot(hd(Sa, h), G[h], NN))
        dKt = cat(lambda h: _ldot(dA2[h], hd(At, h), TN) + _ldot(dW2[h], hd(Rt, h), TN) + _ldot(hd(V, h), G[h], NN))
        dRt = cat(lambda h: _ldot(hd(dY, h), S0[h], NN) + _ldot(dW1[h], hd(Bt, h), NN) + _ldot(dW2[h], hd(Kt, h), NN))
        last_row = lax.broadcasted_iota(jnp.int32, (RW_CHUNK, RW_WIDTH), 0) == RW_CHUNK - 1
        dc_prev = dAt * At
        dc = dc_prev + dRt * Rt - dBt * Bt - dKt * Kt + jnp.where(last_row, dc_last, 0.0)
        dr_ref[...] = dRt * q["e_pos"]
        dlw_ref[...] = _hdot(incl_f, dc, TN) - dc_prev
        dk_ref[...] = dKt * q["e_neg"]
        da_ref[...] = dAt * q["e_prev"]
        db_ref[...] = dBt * q["e_neg"]
        dv_ref[...] = dV

    rev = lambda n: nC - 1 - n
    row_spec = pl.BlockSpec((RW_CHUNK, RW_WIDTH), lambda n: (rev(n), 0))
    st_spec = pl.BlockSpec((1, H, N, N), lambda n: (rev(n), 0, 0, 0))
    row_shape = jax.ShapeDtypeStruct((T, RW_WIDTH), F32)
    return pl.pallas_call(
        body, name="rwkv_chunk_bwd", grid=(nC,),
        in_specs=[row_spec] * 7 + [st_spec] * 5 + [row_spec], out_specs=[row_spec] * 6,
        out_shape=[row_shape] * 6, scratch_shapes=[pltpu.VMEM((H, N, N), F32)],
        compiler_params=_params(("arbitrary",)),
    )(r, lw, k, a, b, v, dy, s0, tinv, w1, a2, w2, sa)


def _alibi_slope(head):
    return float(np.float32(2.0 ** (-8.0 * (head + 1) / ATT_HEADS)))


ATT_SPAN = ATT_BLOCK * max(ATT_GROUP_DILATION)
ATT_PAIR_WIDTH = 2 * ATT_HEAD_DIM
ATT_SIDE_BY_SIDE = 16


def _pair_slope(g, hp, j):
    return jnp.where(hp == 0, _alibi_slope(4 * g + j), _alibi_slope(4 * g + 2 + j))


def _att_rows(mi, r, d):
    start = mi * ATT_BLOCK * d + r
    return pl.ds(start, ATT_BLOCK) if d == 1 else pl.ds(start, ATT_BLOCK, stride=d)


def _att_masks():
    qi = lax.broadcasted_iota(jnp.int32, (ATT_BLOCK, ATT_BLOCK), 0)
    kj = lax.broadcasted_iota(jnp.int32, (ATT_BLOCK, ATT_BLOCK), 1)
    return qi, kj


NEG = -1e30


def _att_logits(q, k, slope_d, steps, valid):
    s = lax.dot_general(q.astype(BF16), k.astype(BF16), (((1,), (1,)), ((), ())),
                        preferred_element_type=F32) * (ATT_HEAD_DIM ** -0.5)
    return jnp.where(valid, s - slope_d * steps.astype(F32), NEG)


def _att_fwd(p_att, g):
    T = p_att.shape[0]
    d = ATT_GROUP_DILATION[g]
    W = ATT_PAIR_WIDTH
    nb = T // ATT_SPAN
    mb = ATT_SPAN // (ATT_BLOCK * d)

    def body(q_ref, kc_ref, kp_ref, vc_ref, vp_ref, o_ref, l_ref):
        hp, n = pl.program_id(0), pl.program_id(1)
        qi, kj = _att_masks()
        slopes = [_pair_slope(g, hp, j) * d for j in range(2)]
        blocks = [(r, mi) for r in range(d) for mi in range(mb)]
        for at in range(0, len(blocks), ATT_SIDE_BY_SIDE):
            tasks = []
            for r, mi in blocks[at:at + ATT_SIDE_BY_SIDE]:
                rows = _att_rows(mi, r, d)
                if mi > 0:
                    prev = _att_rows(mi - 1, r, d)
                    kp, vp, has_prev = kc_ref[prev, :], vc_ref[prev, :], True
                else:
                    prev = _att_rows(mb - 1, r, d)
                    kp, vp, has_prev = kp_ref[prev, :], vp_ref[prev, :], n > 0
                q, kc, vc = q_ref[rows, :], kc_ref[rows, :], vc_ref[rows, :]
                for j in range(2):
                    sl = slice(j * ATT_HEAD_DIM, (j + 1) * ATT_HEAD_DIM)
                    tasks.append((q[:, sl], kc[:, sl], kp[:, sl], vc[:, sl], vp[:, sl], has_prev, slopes[j]))
            lc = [_att_logits(t[0], t[1], t[6], qi - kj, kj <= qi) for t in tasks]
            lp = [_att_logits(t[0], t[2], t[6], qi - kj + ATT_BLOCK, (kj >= qi) & t[5]) for t in tasks]
            mx = [jnp.maximum(jnp.max(a, axis=1, keepdims=True), jnp.max(b, axis=1, keepdims=True))
                  for a, b in zip(lc, lp, strict=True)]
            ec = [jnp.exp(a - m) for a, m in zip(lc, mx, strict=True)]
            ep = [jnp.exp(b - m) for b, m in zip(lp, mx, strict=True)]
            den = [jnp.sum(a, axis=1, keepdims=True) + jnp.sum(b, axis=1, keepdims=True)
                   for a, b in zip(ec, ep, strict=True)]
            inv = [1.0 / s for s in den]
            outs = [jnp.dot((a * i).astype(BF16), t[3].astype(BF16), preferred_element_type=F32)
                    + jnp.dot((b * i).astype(BF16), t[4].astype(BF16), preferred_element_type=F32)
                    for a, b, i, t in zip(ec, ep, inv, tasks, strict=True)]
            lses = [jnp.broadcast_to(m + jnp.log(s), (ATT_BLOCK, ATT_HEAD_DIM)) for m, s in zip(mx, den, strict=True)]
            for i, (r, mi) in enumerate(blocks[at:at + ATT_SIDE_BY_SIDE]):
                rows = _att_rows(mi, r, d)
                o_ref[rows, :] = jnp.concatenate(outs[2 * i:2 * i + 2], axis=1)
                l_ref[rows, :] = jnp.concatenate(lses[2 * i:2 * i + 2], axis=1)

    def spec(col0, prev):
        if prev:
            return pl.BlockSpec((ATT_SPAN, W), lambda hp, n: (jnp.maximum(n - 1, 0), col0 + 2 * g + hp))
        return pl.BlockSpec((ATT_SPAN, W), lambda hp, n: (n, col0 + 2 * g + hp))

    o_spec = pl.BlockSpec((ATT_SPAN, W), lambda hp, n: (n, hp))
    o, l = pl.pallas_call(
        body, name=f"att_fwd_g{g}", grid=(2, nb),
        in_specs=[spec(0, False), spec(6, False), spec(6, True), spec(12, False), spec(12, True)],
        out_specs=[o_spec, o_spec],
        out_shape=[jax.ShapeDtypeStruct((T, ATT_GROUP_WIDTH), F32)] * 2,
        compiler_params=_params(("parallel", "arbitrary")),
    )(p_att, p_att, p_att, p_att, p_att)
    return o, l


def _att_bwd(p_att, o, l, do, dl, g):
    T = p_att.shape[0]
    d = ATT_GROUP_DILATION[g]
    W = ATT_PAIR_WIDTH
    nb = T // ATT_SPAN
    mb = ATT_SPAN // (ATT_BLOCK * d)
    scale = ATT_HEAD_DIM ** -0.5

    def body(q_ref, k_ref, v_ref, o_ref, l_ref, do_ref, dl_ref,
             qn_ref, on_ref, ln_ref, don_ref, dln_ref, dq_ref, dk_ref, dv_ref, carry_ref):
        hp, n = pl.program_id(0), pl.program_id(1)
        qi, kj = _att_masks()

        @pl.when(n == 0)
        def _():
            carry_ref[...] = jnp.zeros_like(carry_ref)

        slopes = [_pair_slope(g, hp, j) * d for j in range(2)]
        blocks = [(r, mi) for r in range(d) for mi in range(mb)]
        side_by_side = ATT_SIDE_BY_SIDE // 2
        carry = None
        for at in range(0, len(blocks), side_by_side):
            tasks = []
            for r, mi in blocks[at:at + side_by_side]:
                rows = _att_rows(mi, r, d)
                if mi < mb - 1:
                    nrows = _att_rows(mi + 1, r, d)
                    nxt = (q_ref[nrows, :], o_ref[nrows, :], l_ref[nrows, :], do_ref[nrows, :], dl_ref[nrows, :])
                    has_next = True
                else:
                    nrows = _att_rows(0, r, d)
                    nxt = (qn_ref[nrows, :], on_ref[nrows, :], ln_ref[nrows, :], don_ref[nrows, :],
                           dln_ref[nrows, :])
                    has_next = n < nb - 1
                cur = (q_ref[rows, :], o_ref[rows, :], l_ref[rows, :], do_ref[rows, :], dl_ref[rows, :])
                k_all, v_all = k_ref[rows, :], v_ref[rows, :]
                for j in range(2):
                    sl = slice(j * ATT_HEAD_DIM, (j + 1) * ATT_HEAD_DIM)
                    for blk, steps, valid in ((cur, qi - kj, kj <= qi),
                                              (nxt, qi - kj + ATT_BLOCK, (kj >= qi) & has_next)):
                        q, o_, lse, do_, dlse = (z[:, sl] for z in blk)
                        tasks.append(dict(q=q, o=o_, lse=lse[:, :1], do=do_, dlse=dlse[:, :1], steps=steps,
                                          valid=valid, k=k_all[:, sl], vb=v_all[:, sl].astype(BF16),
                                          slope=slopes[j]))
            p = [jnp.exp(_att_logits(t["q"], t["k"], t["slope"], t["steps"], t["valid"]) - t["lse"]) for t in tasks]
            dp = [lax.dot_general(t["do"].astype(BF16), t["vb"], (((1,), (1,)), ((), ())),
                                  preferred_element_type=F32) for t in tasks]
            dsum = [jnp.sum(t["do"] * t["o"], axis=1, keepdims=True) for t in tasks]
            ds = [a * (b - s + t["dlse"]) for a, b, s, t in zip(p, dp, dsum, tasks, strict=True)]
            dv_ = [jnp.dot(a.T.astype(BF16), t["do"].astype(BF16), preferred_element_type=F32)
                   for a, t in zip(p, tasks, strict=True)]
            dk_ = [jnp.dot(a.T.astype(BF16), t["q"].astype(BF16), preferred_element_type=F32) * scale
                   for a, t in zip(ds, tasks, strict=True)]
            dq_ = [jnp.dot(a.astype(BF16), t["k"].astype(BF16), preferred_element_type=F32) * scale
                   for a, t in zip(ds, tasks, strict=True)]
            for i, (r, mi) in enumerate(blocks[at:at + side_by_side]):
                rows = _att_rows(mi, r, d)
                b = 4 * i
                if mi == 0:
                    carry = carry_ref[r]
                dq_ref[rows, :] = jnp.concatenate([dq_[b], dq_[b + 2]], axis=1) + carry
                carry = jnp.concatenate([dq_[b + 1], dq_[b + 3]], axis=1)
                if mi == mb - 1:
                    carry_ref[r] = carry
                dk_ref[rows, :] = jnp.concatenate([dk_[b] + dk_[b + 1], dk_[b + 2] + dk_[b + 3]], axis=1)
                dv_ref[rows, :] = jnp.concatenate([dv_[b] + dv_[b + 1], dv_[b + 2] + dv_[b + 3]], axis=1)

    head_rows = ATT_BLOCK * d
    nxt_n = lambda n: jnp.minimum((n + 1) * mb, T // head_rows - 1)
    cur_p = lambda col0: pl.BlockSpec((ATT_SPAN, W), lambda hp, n: (n, col0 + 2 * g + hp))
    cur_o = pl.BlockSpec((ATT_SPAN, W), lambda hp, n: (n, hp))
    nxt_o = pl.BlockSpec((head_rows, W), lambda hp, n: (nxt_n(n), hp))
    dq, dk, dv = pl.pallas_call(
        body, name=f"att_bwd_g{g}", grid=(2, nb),
        in_specs=[cur_p(0), cur_p(6), cur_p(12), cur_o, cur_o, cur_o, cur_o,
                  pl.BlockSpec((head_rows, W), lambda hp, n: (nxt_n(n), 2 * g + hp)), nxt_o, nxt_o, nxt_o, nxt_o],
        out_specs=[cur_o, cur_o, cur_o],
        out_shape=[jax.ShapeDtypeStruct((T, ATT_GROUP_WIDTH), F32)] * 3,
        scratch_shapes=[pltpu.VMEM((d, ATT_BLOCK, W), F32)],
        compiler_params=_params(("parallel", "arbitrary")),
    )(p_att, p_att, p_att, o, l, do, dl, p_att, o, l, do, dl)
    return dq, dk, dv


FFN_TILE = 2 * D_FF // N_CHIPS
RKV = 3 * RW_WIDTH
WA = 128
XG = 160
RW_COLS = RKV + WA + XG


def _local_step(x, p, W, target, late_weights=None, early_grads=None, by_chip=False, grad_dtype=F32):
    T = x.shape[0]
    tT = 256
    bd512 = _block_diag_ones(RW_WIDTH, RW_HEAD_DIM)
    bd256 = _block_diag_ones(ATT_GROUP_WIDTH, ATT_HEAD_DIM)
    G = {}
    W = dict(W)

    w_in = W["w_in"]
    w_rkv, w_wa, w_xg, w_att = (w_in[:, :RKV], w_in[:, RKV:RKV + WA], w_in[:, RKV + WA:RW_COLS],
                                w_in[:, RW_COLS:])
    mu = W["rw_mu"]
    mu_rkv, mu_wa, mu_xg = mu[:, :RKV], mu[:, RKV:RKV + WA], mu[:, RKV + WA:]
    zpad = jnp.zeros((64, RW_WIDTH), W["rw_w_up"].dtype)
    w_up_pad = jnp.concatenate([W["rw_w_up"], zpad], axis=0)
    a_up_pad = jnp.concatenate([zpad, W["rw_a_up"]], axis=0)
    r_k = W["rw_r_k"].reshape(1, RW_WIDTH)

    (h,) = _rowwise("norm_mix", lambda i, n, r, pv, nx, c: [_rms_fwd(r[0], c[0])], T, tT,
                    rows=[x], consts=[W["g_mix"]], outs=[("row", D_MODEL, BF16)])
    p_rkv = _mm("proj_rkv", h, w_rkv, "nn")
    p_wa = _mm("proj_wa", h, w_wa, "nn")
    p_xg = _mm("proj_xg", h, w_xg, "nn")
    p_att = _mm("proj_att", h, w_att, "nn", tn=768)
    z_gate = _mm("proj_gate", h, W["w_gate"], "nn")

    def rw_pre_core(i, rows, prevs, consts):
        prkv, pwa, pxg = rows[:3]
        (mrkv, mwa, mxg, w0, a0, k_k, k_a, wup, aup, gup, bd) = consts[:11]
        m_rkv = prkv + (_shift_down(prkv, prevs[0], i, 1) - prkv) * mrkv
        m_wa = pwa + (_shift_down(pwa, prevs[1], i, 1) - pwa) * mwa
        m_xg = pxg + (_shift_down(pxg, prevs[2], i, 1) - pxg) * mxg
        r, k, v = m_rkv[:, :RW_WIDTH], m_rkv[:, RW_WIDTH:2 * RW_WIDTH], m_rkv[:, 2 * RW_WIDTH:]
        tw = jnp.tanh(m_wa)
        lw = w0 + jnp.dot(tw.astype(BF16), wup.astype(BF16), preferred_element_type=F32)
        wlog = -_softplus(-lw) - 0.5
        log_decay = -jnp.exp(wlog)
        a = _sigmoid(a0 + jnp.dot(m_wa.astype(BF16), aup.astype(BF16), preferred_element_type=F32))
        sg = _sigmoid(m_xg)
        gate = jnp.dot(sg.astype(BF16), gup.astype(BF16), preferred_element_type=F32)
        kkp = k * k_k
        nrm = jnp.sqrt(_segsum(kkp * kkp, bd))
        nrm_c = jnp.maximum(nrm, 1e-12)
        kk = kkp / nrm_c
        k2 = k * (1.0 + (a - 1.0) * k_a)
        return dict(r=r, k=k, v=v, tw=tw, lw=lw, wlog=wlog, log_decay=log_decay, a=a, sg=sg, gate=gate, kkp=kkp,
                    nrm=nrm, nrm_c=nrm_c, kk=kk, k2=k2, m_rkv=m_rkv, m_wa=m_wa, m_xg=m_xg)

    pre_consts = [mu_rkv, mu_wa, mu_xg, W["rw_w0"], W["rw_a0"], W["rw_k_k"], W["rw_k_a"],
                  w_up_pad, a_up_pad, W["rw_g_up"], bd512]

    def rw_pre(i, n, rows, prevs, nexts, consts):
        q = rw_pre_core(i, rows, prevs, consts)
        return [q["r"], q["log_decay"], q["k2"], q["v"], -q["kk"], q["kk"] * q["a"], q["gate"]]

    r_s, w_s, k_s, v_s, a_s, b_s, gate_s = _rowwise(
        "rwkv_pre", rw_pre, T, tT, rows=[p_rkv, p_wa, p_xg], prevs=[p_rkv, p_wa, p_xg], consts=pre_consts,
        outs=[("row", RW_WIDTH, F32)] * 7)
    (at_s, bt_s, kt_s, rt_s, a2v_s, w2v_s, tinv_s, w1_s, a2_s, w2_s,
     plast_s) = _rwkv_chunk_prep(r_s, w_s, k_s, a_s, b_s, v_s)
    y_scan, sa_s, s0_s = _rwkv_chunk_fwd(v_s, at_s, bt_s, kt_s, rt_s, a2v_s, w2v_s, tinv_s, w1_s, plast_s)

    def rw_post_core(rows, consts):
        y, r, k2, v, gate = rows[:5]
        ln_g, ln_b, rk, bd = consts[:4]
        mean = _segsum(y, bd) * (1.0 / RW_HEAD_DIM)
        yc = y - mean
        var = _segsum(yc * yc, bd) * (1.0 / RW_HEAD_DIM)
        rstd = lax.rsqrt(var + RW_LN_EPS)
        yn = yc * rstd
        s = _segsum(r * k2 * rk, bd)
        return dict(yn=yn, rstd=rstd, s=s, pre=yn * ln_g + ln_b + s * v)

    post_consts = [W["rw_ln_g"], W["rw_ln_b"], r_k, bd512]
    (y_a,) = _rowwise("rwkv_post", lambda i, n, r, pv, nx, c: [rw_post_core(r, c)["pre"] * r[4]], T, tT,
                      rows=[y_scan, r_s, k_s, v_s, gate_s], consts=post_consts, outs=[("row", RW_WIDTH, BF16)])

    att = [_att_fwd(p_att, g) for g in range(3)]

    def comb_weights(ls):
        mx = jnp.maximum(jnp.maximum(ls[0], ls[1]), ls[2])
        es = [jnp.exp(l - mx) for l in ls]
        den = es[0] + es[1] + es[2]
        return [e / den for e in es]

    def att_comb(i, n, rows, pv, nx, c):
        wts = comb_weights(rows[3:6])
        return [wts[0] * rows[0] + wts[1] * rows[1] + wts[2] * rows[2]]

    (y_b,) = _rowwise("att_combine", att_comb, T, tT, rows=[att[0][0], att[1][0], att[2][0], att[0][1], att[1][1],
                                                            att[2][1]], outs=[("row", ATT_GROUP_WIDTH, BF16)])

    if late_weights is not None:
        W.update(late_weights(y_b))
    br_a = _mm("branch_a", y_a, W["w_branch_a"], "nn")
    br_b = _mm("branch_b", y_b, W["w_branch_b"], "nn")

    def merge(i, n, rows, pv, nx, c):
        gates = _sigmoid(rows[0] + c[0])
        return [gates[:, :D_MODEL] * rows[1] + gates[:, D_MODEL:] * rows[2]]

    (merged,) = _rowwise("merge", merge, T, tT, rows=[z_gate, br_a, br_b], consts=[W["b_gate"]],
                         outs=[("row", D_MODEL, BF16)])
    with_norm = lambda res, rows, consts: [res, _rms_fwd(res, consts[0])]
    stream_and_norm = [("row", D_MODEL, F32), ("row", D_MODEL, BF16)]
    x1, h2 = _mm("mix_out", merged, W["w_out"], "nn", add=x, post=(with_norm, [], [W["g_ffn"]], stream_and_norm))

    u = _mm("ffn_up", h2, W["w_up"], "nn", tn=FFN_TILE)

    def conv_core(i, rows, prevs, consts):
        uu, cw, cb = rows[0], consts[0], consts[1]
        u1 = _shift_down(uu, prevs[0], i, 1)
        u2 = _shift_down(uu, prevs[0], i, 2)
        uc = cb + cw[0:1] * uu + cw[1:2] * u1 + cw[2:3] * u2
        return uc[:, :D_FF], uc[:, D_FF:], u1, u2

    def glu(i, n, rows, prevs, nx, consts):
        gate, val, _, _ = conv_core(i, rows, prevs, consts)
        return [_gelu(gate) * val]

    tF = 128
    (act,) = _rowwise("conv_glu", glu, T, tF, rows=[u], prevs=[u], consts=[W["conv_w"], W["conv_b"]],
                      outs=[("row", D_FF, BF16)])
    x2, h3 = _mm("ffn_down", act, W["w_down"], "nn", add=x1, post=(with_norm, [], [W["g_ple"]], stream_and_norm))

    z_ple = _mm("ple_gate", h3, W["w_ple_gate"], "nn")
    e_ple = _mm("ple_emb", p, W["w_ple"], "nn")

    def head(i, n, rows, pv, nx, consts):
        x2_, z, e, tgt = rows
        pg = _sigmoid(z)
        x3 = x2_ + pg * e
        y = _rms_fwd(x3, consts[0])
        err = y - tgt
        loss = 0.5 * jnp.sum(jnp.sum(err * err, axis=1, keepdims=True) * (1.0 / D_MODEL), axis=0, keepdims=True)
        dy = err * (1.0 / D_MODEL)
        dx3, dgf = _rms_bwd(x3, consts[0], dy)
        return [dx3, dx3 * pg, dx3 * e * pg * (1.0 - pg), jnp.broadcast_to(loss, (1, LANES)), _colsum(dgf)]

    dx3, de, dz, loss_acc, G["g_final"] = _rowwise(
        "loss_head", head, T, tT, rows=[x2, z_ple, e_ple, target], consts=[W["g_final"].reshape(1, D_MODEL)],
        outs=[("row", D_MODEL, F32), ("row", D_MODEL, BF16), ("row", D_MODEL, BF16), ("acc", (1, LANES)),
              ("acc", (1, D_MODEL))])
    G["w_ple"] = _mm("d_w_ple", p, de, "tn", grad_dtype, out_by_chip=by_chip)
    G["w_ple_gate"] = _mm("d_w_ple_gate", h3, dz, "tn", grad_dtype)
    def norm_bwd(i, n, rows, pv, nx, consts):
        dx, dg = _rms_bwd(rows[0], consts[0], rows[1])
        return [rows[2] + dx, _colsum(dg)]

    through_norm = lambda res, rows, consts: norm_bwd(0, 0, [rows[0], res, rows[1]], [], [], consts)
    stream_and_gain = [("row", D_MODEL, F32), ("acc", (1, D_MODEL))]
    dx2, G["g_ple"] = _mm("d_h3", dz, W["w_ple_gate"], "nt", tm=512,
                          post=(through_norm, [x2, dx3], [W["g_ple"]], stream_and_gain))

    dact = _mm("d_act", dx2, W["w_down"], "nt")
    G["w_down"] = _mm("d_w_down", act, dx2, "tn", grad_dtype)

    def glu_grad(gate, val, da):
        act_, slope = _gelu_and_grad(gate)
        return jnp.concatenate([da * val * slope, da * act_], axis=1)

    def glu_bwd(i, n, rows, prevs, nexts, consts):
        uu, da = rows
        cw = consts[0]
        gate, val, u1, u2 = conv_core(i, rows, prevs, consts)
        duc = glu_grad(gate, val, da)
        dcw = jnp.concatenate([_colsum(duc * uu), _colsum(duc * u1), _colsum(duc * u2)], axis=0)
        gate_n, val_n, _, _ = conv_core(1, [nexts[0]], [uu[tF - SUBLANES:]], consts)
        duc_n = glu_grad(gate_n, val_n, nexts[1])
        du = (cw[0:1] * duc + cw[1:2] * _shift_up(duc, duc_n, i, n, 1) + cw[2:3] * _shift_up(duc, duc_n, i, n, 2))
        return [du, _colsum(duc), dcw]

    du, G["conv_b"], G["conv_w"] = _rowwise(
        "d_conv_glu", glu_bwd, T, tF, rows=[u, dact], prevs=[u], nexts=[u, dact],
        consts=[W["conv_w"], W["conv_b"]],
        outs=[("row", 2 * D_FF, BF16), ("acc", (1, 2 * D_FF)), ("acc", (3, 2 * D_FF))])
    G["w_up"] = _mm("d_w_up", h2, du, "tn", grad_dtype, out_by_chip=by_chip, tn=FFN_TILE)
    dh2 = _mm("d_h2", du, W["w_up"], "nt", tk=FFN_TILE)
    dx1, G["g_ffn"] = _rowwise("d_norm_ffn", norm_bwd, T, tT, rows=[x1, dh2, dx2], consts=[W["g_ffn"]],
                               outs=[("row", D_MODEL, F32), ("acc", (1, D_MODEL))])

    b_gate = W["b_gate"]
    if early_grads is not None:
        b_gate = b_gate + early_grads(G, 0)[0:1, 0:1]
    dmerged = _mm("d_merged", dx1, W["w_out"], "nt")
    G["w_out"] = _mm("d_w_out", merged, dx1, "tn", grad_dtype)

    def merge_bwd(i, n, rows, pv, nx, consts):
        z, a_, b_, dm = rows
        gates = _sigmoid(z + consts[0])
        ga, gb = gates[:, :D_MODEL], gates[:, D_MODEL:]
        dz_ = jnp.concatenate([dm * a_ * ga * (1.0 - ga), dm * b_ * gb * (1.0 - gb)], axis=1)
        return [dm * ga, dm * gb, dz_, _colsum(dz_)]

    d_br_a, d_br_b, dz_gate, G["b_gate"] = _rowwise(
        "d_merge", merge_bwd, T, tT, rows=[z_gate, br_a, br_b, dmerged], consts=[b_gate],
        outs=[("row", D_MODEL, BF16), ("row", D_MODEL, BF16), ("row", 2 * D_MODEL, BF16), ("acc", (1, 2 * D_MODEL))])
    G["w_branch_a"] = _mm("d_w_branch_a", y_a, d_br_a, "tn", grad_dtype, out_by_chip=by_chip)
    G["w_branch_b"] = _mm("d_w_branch_b", y_b, d_br_b, "tn", grad_dtype, out_by_chip=by_chip)
    G["w_gate"] = _mm("d_w_gate", h, dz_gate, "tn", grad_dtype, out_by_chip=by_chip)
    if early_grads is not None:
        post_consts = [post_consts[0] + early_grads(G, 1)[0:1, 0:1]] + post_consts[1:]
    dy_a = _mm("d_y_a", d_br_a, W["w_branch_a"], "nt")
    dy_b = _mm("d_y_b", d_br_b, W["w_branch_b"], "nt")

    def att_comb_bwd(i, n, rows, pv, nx, consts):
        os_, ls, dy = rows[0:3], rows[3:6], rows[6]
        wts = comb_weights(ls)
        dws = [_segsum(dy * o_, consts[0]) for o_ in os_]
        mix = wts[0] * dws[0] + wts[1] * dws[1] + wts[2] * dws[2]
        return [wts[g_] * dy for g_ in range(3)] + [wts[g_] * (dws[g_] - mix) for g_ in range(3)]

    comb = _rowwise("d_att_combine", att_comb_bwd, T, tT,
                    rows=[att[0][0], att[1][0], att[2][0], att[0][1], att[1][1], att[2][1], dy_b], consts=[bd256],
                    outs=[("row", ATT_GROUP_WIDTH, F32)] * 6)
    dqkv = [_att_bwd(p_att, att[g][0], att[g][1], comb[g], comb[3 + g], g) for g in range(3)]
    dp_att = jnp.concatenate([dqkv[g][part] for part in range(3) for g in range(3)], axis=1).astype(BF16)

    def rw_post_bwd(i, n, rows, pv, nx, consts):
        y, r, k2, v, gate, dya = rows
        ln_g, ln_b, rk, bd = consts
        q = rw_post_core(rows, consts)
        dpre = dya * gate
        dgate = dya * q["pre"]
        dyn = dpre * ln_g
        inv = 1.0 / RW_HEAD_DIM
        dy_scan = q["rstd"] * (dyn - _segsum(dyn, bd) * inv - q["yn"] * (_segsum(dyn * q["yn"], bd) * inv))
        ds = _segsum(dpre * v, bd)
        return [dy_scan, dgate, ds * k2 * rk, ds * r * rk, dpre * q["s"],
                _colsum(dpre * q["yn"]), _colsum(dpre), _colsum(ds * r * k2)]

    dy_scan, dgate, dr_b, dk2_b, dv_b, G["rw_ln_g"], G["rw_ln_b"], d_rk = _rowwise(
        "d_rwkv_post", rw_post_bwd, T, tT, rows=[y_scan, r_s, k_s, v_s, gate_s, dy_a], consts=post_consts,
        outs=[("row", RW_WIDTH, F32)] * 5 + [("acc", (1, RW_WIDTH))] * 3)
    G["rw_r_k"] = d_rk.reshape(RW_HEADS, RW_HEAD_DIM)

    dr_s, dw_s, dk_s, da_s, db_s, dv_s = _rwkv_chunk_bwd(r_s, w_s, k_s, a_s, b_s, v_s, dy_scan, s0_s, tinv_s, w1_s,
                                                         a2_s, w2_s, sa_s)

    def rw_pre_bwd(i, n, rows, prevs, nx, consts):
        q = rw_pre_core(i, rows, prevs, consts)
        (mrkv, mwa, mxg, w0, a0, k_k, k_a, wup, aup, gup, bd) = consts
        dr, dlogdecay, dk2, dv, dav, dbv, dgate_ = rows[3:10]
        dr = dr + rows[10]
        dk2 = dk2 + rows[11]
        dv = dv + rows[12]
        a, k, kk = q["a"], q["k"], q["kk"]
        dk = dk2 * (1.0 + (a - 1.0) * k_a)
        da = dk2 * k * k_a + dbv * kk
        dkk = dbv * a - dav
        live = q["nrm"] > 1e-12
        dkkp = jnp.where(live, dkk - kk * _segsum(dkk * kk, bd), dkk) / q["nrm_c"]
        dk = dk + dkkp * k_k
        dlw = dlogdecay * q["log_decay"] * _sigmoid(-q["lw"])
        dla = da * a * (1.0 - a)
        nt = (((1,), (1,)), ((), ()))
        dtw = lax.dot_general(dlw.astype(BF16), wup.astype(BF16), nt, preferred_element_type=F32)
        dxa = lax.dot_general(dla.astype(BF16), aup.astype(BF16), nt, preferred_element_type=F32)
        dm_wa = dtw * (1.0 - q["tw"] * q["tw"]) + dxa
        dsg = lax.dot_general(dgate_.astype(BF16), gup.astype(BF16), nt, preferred_element_type=F32)
        dm_xg = dsg * q["sg"] * (1.0 - q["sg"])
        dm_rkv = jnp.concatenate([dr, dk, dv], axis=1)
        prkv, pwa, pxg = rows[:3]
        dmu = jnp.concatenate([_colsum(dm_rkv * (_shift_down(prkv, prevs[0], i, 1) - prkv)),
                               _colsum(dm_wa * (_shift_down(pwa, prevs[1], i, 1) - pwa)),
                               _colsum(dm_xg * (_shift_down(pxg, prevs[2], i, 1) - pxg))], axis=1)
        return [dm_rkv, dm_wa, dm_xg, dlw, dla, q["tw"], q["m_wa"], q["sg"], dmu,
                _colsum(dlw), _colsum(dla), _colsum(dkkp * k), _colsum(dk2 * k * (a - 1.0))]

    (dm_rkv, dm_wa, dm_xg, dlw, dla, tw_s, mwa_s, sg_s, G["rw_mu"], G["rw_w0"], G["rw_a0"], G["rw_k_k"],
     G["rw_k_a"]) = _rowwise(
        "d_rwkv_pre", rw_pre_bwd, T, tT,
        rows=[p_rkv, p_wa, p_xg, dr_s, dw_s, dk_s, dv_s, da_s, db_s, dgate, dr_b, dk2_b, dv_b],
        prevs=[p_rkv, p_wa, p_xg], consts=pre_consts,
        outs=[("row", RKV, F32), ("row", WA, F32), ("row", XG, F32), ("row", RW_WIDTH, BF16),
              ("row", RW_WIDTH, BF16), ("row", WA, BF16), ("row", WA, BF16), ("row", XG, BF16),
              ("acc", (1, RW_COLS))] + [("acc", (1, RW_WIDTH))] * 4)
    G["rw_w_up"] = _mm("d_rw_w_up", tw_s, dlw, "tn", grad_dtype)[:64]
    G["rw_a_up"] = _mm("d_rw_a_up", mwa_s, dla, "tn", grad_dtype)[64:]
    G["rw_g_up"] = _mm("d_rw_g_up", sg_s, dgate, "tn", grad_dtype)

    def shift_bwd(i, n, rows, pv, nexts, consts):
        return [rows[j] * (1.0 - consts[j]) + _shift_up(rows[j], nexts[j], i, n, 1) * consts[j] for j in range(3)]

    dp_rkv, dp_wa, dp_xg = _rowwise(
        "d_token_shift", shift_bwd, T, tT, rows=[dm_rkv, dm_wa, dm_xg], nexts=[dm_rkv, dm_wa, dm_xg],
        consts=[mu_rkv, mu_wa, mu_xg], outs=[("row", RKV, BF16), ("row", WA, BF16), ("row", XG, BF16)])

    G["w_in"] = jnp.concatenate([_mm("d_w_rkv", h, dp_rkv, "tn", grad_dtype), _mm("d_w_wa", h, dp_wa, "tn", grad_dtype),
                                 _mm("d_w_xg", h, dp_xg, "tn", grad_dtype), _mm("d_w_att", h, dp_att, "tn", grad_dtype, tn=768)], axis=1)
    if early_grads is not None:
        w_wa = w_wa + early_grads(G, 2)[0:1, 0:1].astype(w_wa.dtype)
    dh = _mm("d_h_gate", dz_gate, W["w_gate"], "nt")
    dh = _mm("d_h_rkv", dp_rkv, w_rkv, "nt", add=dh)
    dh = _mm("d_h_wa", dp_wa, w_wa, "nt", add=dh)
    dh = _mm("d_h_xg", dp_xg, w_xg, "nt", add=dh)
    dx, G["g_mix"] = _mm("d_h_att", dp_att, w_att, "nt", add=dh, tm=512,
                         post=(through_norm, [x, dx1], [W["g_mix"]], stream_and_gain))
    return loss_acc[:, :1], dx, G


HBM_SPEC = pl.BlockSpec(memory_space=pltpu.HBM)


def _place():
    x, y, c = lax.axis_index("x"), lax.axis_index("y"), lax.axis_index("c")
    return x, y, c, [(1 - x, y), (x, 1 - y), (1 - x, 1 - y)]


def _remote(src, dst, send_sems, recv_sems, k, to):
    return pltpu.make_async_remote_copy(src_ref=src, dst_ref=dst, send_sem=send_sems.at[k], recv_sem=recv_sems.at[k],
                                        device_id=to, device_id_type=MESH)


ROW_ALIGN = 16


def _splits(rows):
    return rows % (2 * ROW_ALIGN) == 0


def _half_rows(ref_rows, c, first):
    half = ref_rows // 2
    which = c if first else 1 - c
    return pl.ds(pl.multiple_of(which * half, ROW_ALIGN), half)


def _gather_chips(shards):
    n = len(shards)
    split = [_splits(s.shape[0]) for s in shards]

    def body(*refs):
        w_refs, out_refs = refs[:n], refs[n:2 * n]
        send_sems, recv_sems = refs[2 * n:]
        x, y, c, chips = _place()
        me = 2 * x + y
        sends, passed = [], []
        for i in range(n):
            for j, (px, py) in enumerate(chips):
                if split[i]:
                    mine = _half_rows(w_refs[i].shape[0], c, True)
                    cp = _remote(w_refs[i].at[mine], out_refs[i].at[me, mine], send_sems, recv_sems, 6 * i + j,
                                 (px, py, c))
                else:
                    cp = _remote(w_refs[i], out_refs[i].at[me], send_sems, recv_sems, 6 * i + j, (px, py, c))
                cp.start()
                sends.append(cp)
        for i in range(n):
            for j, (px, py) in enumerate(chips):
                if split[i]:
                    landed = out_refs[i].at[2 * px + py, _half_rows(w_refs[i].shape[0], c, True)]
                    _remote(landed, landed, send_sems, recv_sems, 6 * i + j, (px, py, c)).wait_recv()
                    cp = _remote(landed, landed, send_sems, recv_sems, 6 * i + 3 + j, (x, y, 1 - c))
                    cp.start()
                    passed.append(cp)
                else:
                    landed = out_refs[i].at[2 * px + py]
                    _remote(landed, landed, send_sems, recv_sems, 6 * i + j, (px, py, c)).wait_recv()
        for i in range(n):
            if split[i]:
                for j, (px, py) in enumerate(chips):
                    landed = out_refs[i].at[2 * px + py, _half_rows(w_refs[i].shape[0], c, False)]
                    _remote(landed, landed, send_sems, recv_sems, 6 * i + 3 + j, (x, y, 1 - c)).wait_recv()
        for cp in sends + passed:
            cp.wait_send()

    outs = pl.pallas_call(
        body, name="gather_weights", in_specs=[HBM_SPEC] * n, out_specs=[HBM_SPEC] * n,
        out_shape=[jax.ShapeDtypeStruct((N_CHIPS,) + s.shape, s.dtype) for s in shards],
        scratch_shapes=[pltpu.SemaphoreType.DMA((6 * n,)), pltpu.SemaphoreType.DMA((6 * n,))],
    )(*shards)
    me = 2 * lax.axis_index("x") + lax.axis_index("y")
    return [lax.dynamic_update_slice(o, s[None], (me, 0, 0)) for o, s in zip(outs, shards, strict=True)]


def _join_halves(reds):
    n = len(reds)

    def body(*refs):
        r_refs, out_refs = refs[:n], refs[n:2 * n]
        send_sems, recv_sems = refs[2 * n:]
        x, y, c, _ = _place()
        cps = []
        for i in range(n):
            mine = _half_rows(out_refs[i].shape[0], c, True)
            cp = _remote(r_refs[i], out_refs[i].at[mine], send_sems, recv_sems, i, (x, y, 1 - c))
            cp.start()
            cps.append(cp)
        for cp in cps:
            cp.wait()

    outs = pl.pallas_call(
        body, name="join_halves", in_specs=[HBM_SPEC] * n, out_specs=[HBM_SPEC] * n,
        out_shape=[jax.ShapeDtypeStruct((2 * r.shape[0], r.shape[1]), r.dtype) for r in reds],
        scratch_shapes=[pltpu.SemaphoreType.DMA((n,)), pltpu.SemaphoreType.DMA((n,))],
    )(*reds)
    c = lax.axis_index("c")
    return [lax.dynamic_update_slice(o, r, (c * r.shape[0], 0)) for o, r in zip(outs, reds, strict=True)]


def _gather_all(vec):
    def body(v_ref, out_ref, send_sems, recv_sems, local_sem):
        x, y, c, _ = _place()
        me = 4 * x + 2 * y + c
        local = pltpu.make_async_copy(v_ref, out_ref.at[me], local_sem)
        local.start()
        peers = [(x ^ (k >> 2), y ^ ((k >> 1) & 1), c ^ (k & 1)) for k in range(1, N_DEV)]
        sends = [_remote(v_ref, out_ref.at[me], send_sems, recv_sems, k, to) for k, to in enumerate(peers)]
        for cp in sends:
            cp.start()
        for k, (px, py, pc) in enumerate(peers):
            landed = out_ref.at[4 * px + 2 * py + pc]
            _remote(landed, landed, send_sems, recv_sems, k, (px, py, pc)).wait_recv()
        for cp in sends:
            cp.wait_send()
        local.wait()

    return pl.pallas_call(
        body, name="gather_small", in_specs=[HBM_SPEC], out_specs=HBM_SPEC,
        out_shape=jax.ShapeDtypeStruct((N_DEV,) + vec.shape, vec.dtype),
        scratch_shapes=[pltpu.SemaphoreType.DMA((7,)), pltpu.SemaphoreType.DMA((7,)), pltpu.SemaphoreType.DMA],
    )(vec)


SEM_SPEC = pl.BlockSpec(memory_space=pltpu.SEMAPHORE)
PEERS = N_DEV - 1
DATAFLOW = pltpu.SideEffectType.DATAFLOW_SIDE_EFFECTING


def _travel_copies(mode, src_refs, land_refs, send_sems, recv_sems):
    x, y, c, chips = _place()
    me = 2 * x + y
    pairs = []
    for i, (src, land) in enumerate(zip(src_refs, land_refs, strict=True)):
        if mode == "scatter":
            for k in range(1, N_DEV):
                px, py, pc = x ^ (k >> 2), y ^ ((k >> 1) & 1), c ^ (k & 1)
                mine = src.at[2 * px + py, _half_rows(src.shape[1], pc, True)]
                there, here = land.at[4 * x + 2 * y + c], land.at[4 * px + 2 * py + pc]
                send = functools.partial(_remote, mine, there, send_sems, recv_sems, PEERS * i + k - 1, (px, py, pc))
                arrival = functools.partial(_remote, mine, here, send_sems, recv_sems, PEERS * i + k - 1, (px, py, pc))
                pairs.append((send, arrival))
            continue
        for j, (px, py) in enumerate(chips):
            peer = 2 * px + py
            if _splits(src.shape[0]):
                rows = _half_rows(src.shape[0], c, True)
                mine, there, here = src.at[rows], land.at[me, rows], land.at[peer, rows]
            else:
                mine, there, here = src, land.at[me], land.at[peer]
            send = functools.partial(_remote, mine, there, send_sems, recv_sems, PEERS * i + j, (px, py, c))
            arrival = functools.partial(_remote, mine, here, send_sems, recv_sems, PEERS * i + j, (px, py, c))
            pairs.append((send, arrival))
    return pairs


def _share_halves(name, lands):
    idx = [i for i, a in enumerate(lands) if _splits(a.shape[1])]
    n = len(idx)

    def body(*refs):
        in_refs, out_refs = refs[:n], refs[n:2 * n]
        send_sems, recv_sems = refs[2 * n:]
        x, y, c, chips = _place()
        cps = []
        for i, (src, dst) in enumerate(zip(in_refs, out_refs, strict=True)):
            for j, (px, py) in enumerate(chips):
                mine = _half_rows(src.shape[1], c, True)
                cp = _remote(src.at[2 * px + py, mine], dst.at[2 * px + py, mine], send_sems, recv_sems, 3 * i + j,
                             (x, y, 1 - c))
                cp.start()
                cps.append(cp)
        for i, dst in enumerate(out_refs):
            for j, (px, py) in enumerate(chips):
                theirs = dst.at[2 * px + py, _half_rows(dst.shape[1], c, False)]
                _remote(theirs, theirs, send_sems, recv_sems, 3 * i + j, (x, y, 1 - c)).wait_recv()
        for cp in cps:
            cp.wait_send()

    outs = pl.pallas_call(
        body, name=name, in_specs=[HBM_SPEC] * n, out_specs=[HBM_SPEC] * n,
        out_shape=[jax.ShapeDtypeStruct(lands[i].shape, lands[i].dtype) for i in idx],
        input_output_aliases={i: i for i in range(n)},
        scratch_shapes=[pltpu.SemaphoreType.DMA((3 * n,)), pltpu.SemaphoreType.DMA((3 * n,))],
    )(*[lands[i] for i in idx])
    done = list(lands)
    for i, o in zip(idx, outs, strict=True):
        done[i] = o
    return done


def _travel_start(name, mode, srcs):
    n = len(srcs)
    lands = [lax.empty((N_CHIPS,) + s.shape if mode == "gather" else (N_DEV, s.shape[1] // 2, s.shape[2]), s.dtype)
             for s in srcs]

    def body(*refs):
        src_refs, land_refs = refs[:n], refs[n:2 * n]
        send_sems, recv_sems = refs[2 * n], refs[2 * n + 1]
        token = refs[-1]
        for send, _ in _travel_copies(mode, src_refs, land_refs, send_sems, recv_sems):
            send().start()
        token[...] = jnp.zeros_like(token)

    hbm = lambda a: pltpu.HBM(a.shape, a.dtype)
    outs = pl.pallas_call(
        body, name=name,
        out_shape=(pltpu.SemaphoreType.DMA((PEERS * n,)), pltpu.SemaphoreType.DMA((PEERS * n,)),
                   *[hbm(s) for s in srcs],
                   *[hbm(a) for a in lands], jax.ShapeDtypeStruct((SUBLANES, LANES), F32)),
        in_specs=[HBM_SPEC] * (2 * n),
        out_specs=(SEM_SPEC, SEM_SPEC, *[HBM_SPEC] * (2 * n), pl.BlockSpec(memory_space=pltpu.VMEM)),
        input_output_aliases={i: 2 + i for i in range(2 * n)},
        compiler_params=pltpu.CompilerParams(has_side_effects=DATAFLOW),
    )(*[pltpu.with_memory_space_constraint(a, pltpu.HBM) for a in list(srcs) + lands])
    return outs[0], outs[1], list(outs[2:2 + n]), list(outs[2 + n:2 + 2 * n]), outs[-1]


def _travel_wait(name, mode, send_sems, recv_sems, srcs, lands, after):
    n = len(srcs)

    def body(*refs):
        src_refs, land_refs = refs[:n], refs[n:2 * n]
        send_sems_, recv_sems_ = refs[2 * n], refs[2 * n + 1]
        for send, arrival in _travel_copies(mode, src_refs, land_refs, send_sems_, recv_sems_):
            send().wait_send()
            arrival().wait_recv()

    hbm = lambda a: pltpu.HBM(a.shape, a.dtype)
    outs = pl.pallas_call(
        body, name=name, out_shape=tuple(hbm(a) for a in list(srcs) + list(lands)),
        in_specs=[HBM_SPEC] * (2 * n) + [SEM_SPEC, SEM_SPEC, pl.BlockSpec(memory_space=pl.ANY)],
        out_specs=tuple([HBM_SPEC] * (2 * n)), input_output_aliases={i: i for i in range(2 * n)},
        compiler_params=pltpu.CompilerParams(has_side_effects=DATAFLOW),
    )(*srcs, *lands, send_sems, recv_sems, after)
    c = lax.axis_index("c")
    me = 2 * lax.axis_index("x") + lax.axis_index("y")
    if mode == "gather":
        slot, own = me, [s[None] for s in outs[:n]]
    else:
        slot = 2 * me + c
        own = [lax.dynamic_slice(s, (me, c * (s.shape[1] // 2), 0), (1, s.shape[1] // 2, s.shape[2])) for s in outs[:n]]
    return [lax.dynamic_update_slice(a, o, (slot,) + (0,) * (a.ndim - 1)) for a, o in zip(outs[n:], own, strict=True)]


SUM_TILE_BYTES = 4 * 1024 * 1024


def _sum_rows(half, cols):
    best = ROW_ALIGN
    for t in range(ROW_ALIGN, half + 1, ROW_ALIGN):
        if half % t == 0 and N_CHIPS * t * cols * 4 <= SUM_TILE_BYTES:
            best = t
    return best


def _sum_devices(name, parts):
    n, H, C = parts.shape
    tr = _sum_rows(H, C)

    def body(p_ref, o_ref):
        acc = p_ref[0].astype(F32)
        for k in range(1, n):
            acc = acc + p_ref[k].astype(F32)
        o_ref[...] = acc

    return pl.pallas_call(
        body, name=name, grid=(H // tr,),
        in_specs=[pl.BlockSpec((n, tr, C), lambda i: (0, i, 0))],
        out_specs=pl.BlockSpec((tr, C), lambda i: (i, 0)),
        out_shape=jax.ShapeDtypeStruct((H, C), F32),
        compiler_params=_params(("parallel",)),
    )(parts)


def _adamw_math(w, g, m, v):
    m = ADAM_B1 * m + (1.0 - ADAM_B1) * g
    v = ADAM_B2 * v + (1.0 - ADAM_B2) * (g * g)
    m_hat = m / (1.0 - ADAM_B1 ** ADAM_STEP)
    v_hat = v / (1.0 - ADAM_B2 ** ADAM_STEP)
    delta = -ADAM_LR * (m_hat / (jnp.sqrt(v_hat) + ADAM_EPS) + ADAM_WD * w)
    return delta, m, v


def _adamw(name, w, g, m, v):
    R, C = w.shape
    tr = R
    if R % SUBLANES == 0:
        for cand in range(SUBLANES, min(R, 256) + 1, SUBLANES):
            if R % cand == 0:
                tr = cand

    def body(w_ref, g_ref, m_ref, v_ref, d_ref, nm_ref, nv_ref):
        d, nm, nv = _adamw_math(w_ref[...], g_ref[...], m_ref[...], v_ref[...])
        d_ref[...] = d
        nm_ref[...] = nm
        nv_ref[...] = nv

    spec = pl.BlockSpec((tr, C), lambda i: (i, 0))
    shape = jax.ShapeDtypeStruct((R, C), F32)
    return pl.pallas_call(
        body, name=name, grid=(R // tr,), in_specs=[spec] * 4, out_specs=[spec] * 3, out_shape=[shape] * 3,
        compiler_params=_params(("parallel",)),
    )(w, g, m, v)


SMALL_ROW = 2048


def _small_layout(shapes):
    places, row = [], 0
    for R, C in shapes:
        pieces = []
        for r in range(R):
            for c0 in range(0, C, SMALL_ROW):
                pieces.append((r, c0, min(C, c0 + SMALL_ROW), row))
                row += 1
        places.append(pieces)
    return places, -(-row // SUBLANES) * SUBLANES


def _put_rows(block_ref, refs, places):
    block_ref[...] = jnp.zeros_like(block_ref)
    for ref, pieces in zip(refs, places, strict=True):
        for r, c0, c1, row in pieces:
            block_ref[row:row + 1, 0:c1 - c0] = ref[r:r + 1, c0:c1]


def _take_rows(block, refs, places):
    for ref, pieces in zip(refs, places, strict=True):
        for r, c0, c1, row in pieces:
            ref[r:r + 1, c0:c1] = block[row:row + 1, 0:c1 - c0]


def _pack_small(arrs):
    places, rows = _small_layout([a.shape for a in arrs])

    def body(*refs):
        _put_rows(refs[-1], refs[:-1], places)

    return pl.pallas_call(body, name="pack_small", out_shape=jax.ShapeDtypeStruct((rows, SMALL_ROW), F32),
                          compiler_params=_params())(*arrs)


def _adamw_small(parts, ws, ms, vs, extra_shapes):
    n_dev, rows, _ = parts.shape
    n = len(ws)
    places, rows_ = _small_layout([w.shape for w in ws] + list(extra_shapes))
    assert rows_ == rows, (rows_, rows)

    def body(*refs):
        p_ref = refs[0]
        w_refs, m_refs, v_refs = refs[1:1 + n], refs[1 + n:1 + 2 * n], refs[1 + 2 * n:1 + 3 * n]
        outs = refs[1 + 3 * n:-3]
        wb, mb, vb = refs[-3:]
        for block, srcs in ((wb, w_refs), (mb, m_refs), (vb, v_refs)):
            _put_rows(block, srcs, places[:n])
        g = p_ref[0]
        for k in range(1, n_dev):
            g = g + p_ref[k]
        d, nm, nv = _adamw_math(wb[...], g, mb[...], vb[...])
        _take_rows(g, outs[0:n], places[:n])
        _take_rows(d, outs[n:2 * n], places[:n])
        _take_rows(nm, outs[2 * n:3 * n], places[:n])
        _take_rows(nv, outs[3 * n:4 * n], places[:n])
        _take_rows(g, outs[4 * n:], places[n:])

    shapes = [jax.ShapeDtypeStruct(w.shape, F32) for w in ws]
    res = pl.pallas_call(
        body, name="adamw_small", out_shape=shapes * 4 + [jax.ShapeDtypeStruct(s, F32) for s in extra_shapes],
        scratch_shapes=[pltpu.VMEM((rows, SMALL_ROW), F32)] * 3, compiler_params=_params(),
    )(parts, *ws, *ms, *vs)
    return res[0:n], res[n:2 * n], res[2 * n:3 * n], res[3 * n:4 * n], res[4 * n:]


WEIGHTS = ['g_mix', 'w_in', 'rw_mu', 'rw_w0', 'rw_w_up', 'rw_a0', 'rw_a_up', 'rw_g_up', 'rw_k_k', 'rw_k_a',
           'rw_r_k', 'rw_ln_g', 'rw_ln_b', 'w_branch_a', 'w_branch_b', 'w_gate', 'b_gate', 'w_out', 'g_ffn', 'w_up',
           'conv_w', 'conv_b', 'w_down', 'g_ple', 'w_ple_gate', 'w_ple', 'g_final']
ARG_NAMES = (['x', 'p'] + WEIGHTS + ['loss_target'] + ['m_' + n for n in WEIGHTS] + ['v_' + n for n in WEIGHTS])
SHARDED = {'w_in': 1, 'rw_w_up': 1, 'rw_a_up': 1, 'rw_g_up': 1, 'w_branch_a': 1, 'w_branch_b': 1, 'w_gate': 1,
           'w_out': 0, 'w_up': 1, 'conv_w': 1, 'w_down': 0, 'w_ple_gate': 0, 'w_ple': 1}
SMALL = [n for n in WEIGHTS if n not in SHARDED]
WHOLE = ['conv_w']
FIRST_USED = ['w_in', 'rw_w_up', 'rw_a_up', 'rw_g_up', 'w_gate']
READ_BY_CHIP = ['w_gate', 'w_branch_a', 'w_branch_b', 'w_up', 'w_ple']
FIRST_DONE = [['w_up', 'w_down', 'w_ple_gate', 'w_ple'], ['w_out', 'w_branch_a', 'w_branch_b', 'w_gate'],
              ['w_in', 'rw_w_up', 'rw_a_up', 'rw_g_up']]
SPLIT = [n for n in SHARDED if n not in WHOLE]


def _full_from_shards(stack, axis):
    _, R, C = stack.shape
    if axis == 0:
        return stack.reshape(N_CHIPS * R, C)
    return stack.transpose(1, 0, 2).reshape(R, N_CHIPS * C)


def _shards_from_full(full, axis):
    R, C = full.shape
    if axis == 0:
        return full.reshape(N_CHIPS, R // N_CHIPS, C)
    return full.reshape(R, N_CHIPS, C // N_CHIPS).transpose(1, 0, 2)


def kernel(x, p, g_mix, w_in, rw_mu, rw_w0, rw_w_up, rw_a0, rw_a_up, rw_g_up, rw_k_k, rw_k_a, rw_r_k, rw_ln_g, rw_ln_b, w_branch_a, w_branch_b, w_gate, b_gate, w_out, g_ffn, w_up, conv_w, conv_b, w_down, g_ple, w_ple_gate, w_ple, g_final, loss_target, m_g_mix, m_w_in, m_rw_mu, m_rw_w0, m_rw_w_up, m_rw_a0, m_rw_a_up, m_rw_g_up, m_rw_k_k, m_rw_k_a, m_rw_r_k, m_rw_ln_g, m_rw_ln_b, m_w_branch_a, m_w_branch_b, m_w_gate, m_b_gate, m_w_out, m_g_ffn, m_w_up, m_conv_w, m_conv_b, m_w_down, m_g_ple, m_w_ple_gate, m_w_ple, m_g_final, v_g_mix, v_w_in, v_rw_mu, v_rw_w0, v_rw_w_up, v_rw_a0, v_rw_a_up, v_rw_g_up, v_rw_k_k, v_rw_k_a, v_rw_r_k, v_rw_ln_g, v_rw_ln_b, v_w_branch_a, v_w_branch_b, v_w_gate, v_b_gate, v_w_out, v_g_ffn, v_w_up, v_conv_w, v_conv_b, v_w_down, v_g_ple, v_w_ple_gate, v_w_ple, v_g_final):
    given = dict(zip(ARG_NAMES, (x, p, g_mix, w_in, rw_mu, rw_w0, rw_w_up, rw_a0, rw_a_up, rw_g_up, rw_k_k, rw_k_a, rw_r_k, rw_ln_g, rw_ln_b, w_branch_a, w_branch_b, w_gate, b_gate, w_out, g_ffn, w_up, conv_w, conv_b, w_down, g_ple, w_ple_gate, w_ple, g_final, loss_target, m_g_mix, m_w_in, m_rw_mu, m_rw_w0, m_rw_w_up, m_rw_a0, m_rw_a_up, m_rw_g_up, m_rw_k_k, m_rw_k_a, m_rw_r_k, m_rw_ln_g, m_rw_ln_b, m_w_branch_a, m_w_branch_b, m_w_gate, m_b_gate, m_w_out, m_g_ffn, m_w_up, m_conv_w, m_conv_b, m_w_down, m_g_ple, m_w_ple_gate, m_w_ple, m_g_final, v_g_mix, v_w_in, v_rw_mu, v_rw_w0, v_rw_w_up, v_rw_a0, v_rw_a_up, v_rw_g_up, v_rw_k_k, v_rw_k_a, v_rw_r_k, v_rw_ln_g, v_rw_ln_b, v_w_branch_a, v_w_branch_b, v_w_gate, v_b_gate, v_w_out, v_g_ffn, v_w_up, v_conv_w, v_conv_b, v_w_down, v_g_ple, v_w_ple_gate, v_w_ple, v_g_final), strict=True))

    def two_d(name, prefix=""):
        a = given[prefix + name]
        if name == "g_final":
            return a.reshape(1, D_MODEL)
        if name == "rw_r_k":
            return a.reshape(1, RW_WIDTH)
        return a[0] if a.ndim == 3 else a

    cast = lambda n: two_d(n) if n in WHOLE else two_d(n).astype(BF16)
    whole = lambda names, stacks: {n: g if n in READ_BY_CHIP else _full_from_shards(g, SHARDED[n])
                                   for n, g in zip(names, stacks, strict=True)}
    late_names = [n for n in SHARDED if n not in FIRST_USED]
    late_sends, late_recvs, late_srcs, late_lands, token = _travel_start(
        "gather_late_start", "gather", [cast(n) for n in late_names])
    W = whole(FIRST_USED, _gather_chips([cast(n) for n in FIRST_USED]))
    for n in SMALL:
        W[n] = two_d(n)
    W["rw_r_k"] = W["rw_r_k"].reshape(RW_HEADS, RW_HEAD_DIM)
    W["g_mix"] = W["g_mix"] + token[0:1, 0:1]

    def late_weights(after):
        lands = _travel_wait("gather_late_wait", "gather", late_sends, late_recvs, late_srcs, late_lands, after)
        return whole(late_names, _share_halves("share_late", lands))

    early_names = [[n for n in SPLIT if n in group] for group in FIRST_DONE]
    assert sorted(sum(early_names, [])) == sorted(SPLIT)
    travelling = []

    def early_grads(G, stage):
        by_chip = [G[n] if n in READ_BY_CHIP else _shards_from_full(G[n], SHARDED[n]) for n in early_names[stage]]
        sends, recvs, srcs, lands, started = _travel_start(f"scatter{stage}_start", "scatter", by_chip)
        travelling.append((sends, recvs, srcs, lands))
        return started

    loss_part, grad_x, G = _local_step(x[0], p[0, 0], W, loss_target[0], late_weights, early_grads, by_chip=True,
                                       grad_dtype=BF16)

    landed = {}
    for stage, (sends, recvs, srcs, lands) in enumerate(travelling):
        landed.update(zip(early_names[stage], _travel_wait(f"scatter{stage}_wait", "scatter", sends, recvs, srcs,
                                                           lands, grad_x), strict=True))
    reduced = [_sum_devices("sum_devices_" + n, landed[n]) for n in SPLIT]
    shard_grads = dict(zip(SPLIT, _join_halves(reduced), strict=True))

    G["rw_r_k"] = G["rw_r_k"].reshape(1, RW_WIDTH)
    extras = [G[n] for n in WHOLE] + [loss_part]
    all_small = _gather_all(_pack_small([G[n] for n in SMALL] + extras))
    gs, ds, nms, nvs, summed = _adamw_small(all_small, [two_d(n) for n in SMALL], [two_d(n, "m_") for n in SMALL],
                                            [two_d(n, "v_") for n in SMALL], [e.shape for e in extras])
    loss = summed[-1][0, 0]
    chip = 2 * lax.axis_index("x") + lax.axis_index("y")
    for n, full in zip(WHOLE, summed[:-1], strict=True):
        width = two_d(n).shape[1]
        shard_grads[n] = lax.dynamic_slice_in_dim(full, chip * width, width, axis=1)

    grads, deltas, new_m, new_v = {}, {}, {}, {}
    for n in SHARDED:
        g = shard_grads[n]
        d, nm, nv = _adamw("adamw_" + n, two_d(n), g, two_d(n, "m_"), two_d(n, "v_"))
        grads[n], deltas[n], new_m[n], new_v[n] = g, d, nm, nv
    for i, n in enumerate(SMALL):
        grads[n], deltas[n], new_m[n], new_v[n] = gs[i], ds[i], nms[i], nvs[i]
    outs = [loss, grad_x[None]]
    for table in (grads, deltas, new_m, new_v):
        outs += [table[n].reshape(given[n].shape) for n in WEIGHTS]
    return tuple(outs)
```

```python
import functools
import math

import jax
import jax.numpy as jnp
import numpy as np
from jax import lax
from jax.experimental import pallas as pl
from jax.experimental.pallas import tpu as pltpu

F32 = jnp.float32
BF16 = jnp.bfloat16

D_MODEL = 1024
NORM_EPS = 1e-6
RW_HEADS = 8
RW_HEAD_DIM = 64
RW_WIDTH = 512
RW_LN_EPS = 64e-5
ATT_GROUP_DILATION = (1, 4, 16)
ATT_BLOCK = 128
ATT_HEADS = 12
ATT_HEAD_DIM = 64
ATT_GROUP_WIDTH = 256
ATT_WIDTH = 768
D_FF = 3072

ADAM_LR = 0.001
ADAM_B1 = 0.9
ADAM_B2 = 0.999
ADAM_EPS = 1e-08
ADAM_WD = 0.01
ADAM_STEP = 10

SUBLANES = 8
LANES = 128
VMEM_LIMIT = 56 * 1024 * 1024
N_CHIPS = 4
N_DEV = 8
MESH = pl.DeviceIdType.MESH


def _params(sem=None):
    return pltpu.CompilerParams(dimension_semantics=sem, vmem_limit_bytes=VMEM_LIMIT)


def _pick(dim, pref):
    if dim % LANES != 0 or dim <= pref:
        return dim
    best = LANES
    for t in range(LANES, pref + 1, LANES):
        if dim % t == 0:
            best = t
    return best


def _mm(name, a, b, mode, out_dtype=F32, add=None, tm=1024, tn=1024, tk=1024, out_by_chip=False, post=None):
    by_chip = b.ndim == 3
    b_rows, b_cols = (b.shape[1], N_CHIPS * b.shape[2]) if by_chip else b.shape
    if mode == "nn":
        (M, K), (K2, N) = a.shape, (b_rows, b_cols)
    elif mode == "nt":
        (M, K), (N, K2) = a.shape, (b_rows, b_cols)
    else:
        (K, M), (K2, N) = a.shape, (b_rows, b_cols)
    assert K == K2, (name, a.shape, b.shape, mode)
    assert not (by_chip and mode == "tn") and not (out_by_chip and add is not None), name
    tm = _pick(M, tm)
    n_cut, k_cut = out_by_chip or (by_chip and mode == "nn"), by_chip and mode == "nt"
    tn = _pick(N // N_CHIPS, tn) if n_cut else _pick(N, tn)
    tk = _pick(K // N_CHIPS, tk) if k_cut else _pick(K, tk)
    nk = K // tk
    per_n = (N // N_CHIPS) // tn if n_cut else 1
    per_k = (K // N_CHIPS) // tk if k_cut else 1
    if mode == "nn":
        a_spec = pl.BlockSpec((tm, tk), lambda i, j, k: (i, k))
        b_spec = (pl.BlockSpec((None, tk, tn), lambda i, j, k: (j // per_n, k, j % per_n)) if by_chip
                  else pl.BlockSpec((tk, tn), lambda i, j, k: (k, j)))
        dims = (((1,), (0,)), ((), ()))
    elif mode == "nt":
        a_spec = pl.BlockSpec((tm, tk), lambda i, j, k: (i, k))
        b_spec = (pl.BlockSpec((None, tn, tk), lambda i, j, k: (k // per_k, j, k % per_k)) if by_chip
                  else pl.BlockSpec((tn, tk), lambda i, j, k: (j, k)))
        dims = (((1,), (1,)), ((), ()))
    else:
        a_spec = pl.BlockSpec((tk, tm), lambda i, j, k: (k, i))
        b_spec = pl.BlockSpec((tk, tn), lambda i, j, k: (k, j))
        dims = (((0,), (0,)), ((), ()))
    if out_by_chip:
        o_spec = pl.BlockSpec((None, tm, tn), lambda i, j, k: (j // per_n, i, j % per_n))
        out_shape = jax.ShapeDtypeStruct((N_CHIPS, M, N // N_CHIPS), out_dtype)
    else:
        o_spec = pl.BlockSpec((tm, tn), lambda i, j, k: (i, j))
        out_shape = jax.ShapeDtypeStruct((M, N), out_dtype)
    has_add = add is not None
    ins = [a, b] + ([add] if has_add else [])
    in_specs = [a_spec, b_spec] + ([o_spec] if has_add else [])
    n_main = len(ins)
    semantics = ("parallel", "parallel", "arbitrary")
    if post is not None:
        post_fn, post_rows, post_consts, post_outs = post
        assert tn == N and not out_by_chip, name
        ins += list(post_rows) + list(post_consts)
        in_specs += [pl.BlockSpec((tm, r.shape[1]), lambda i, j, k: (i, 0)) for r in post_rows]
        in_specs += [pl.BlockSpec(c.shape, lambda i, j, k, nd=c.ndim: (0,) * nd) for c in post_consts]
        o_spec = [pl.BlockSpec((tm, o[1]), lambda i, j, k: (i, 0)) if o[0] == "row"
                  else pl.BlockSpec(o[1], lambda i, j, k: (0, 0)) for o in post_outs]
        out_shape = [jax.ShapeDtypeStruct((M, o[1]), o[2]) if o[0] == "row" else jax.ShapeDtypeStruct(o[1], F32)
                     for o in post_outs]
        if any(o[0] == "acc" for o in post_outs):
            semantics = ("arbitrary", "arbitrary", "arbitrary")
    n_in = len(ins)

    def body(*refs):
        a_ref, b_ref = refs[:2]
        out_refs, acc_ref = refs[n_in:-1], refs[-1]
        i, k = pl.program_id(0), pl.program_id(2)
        part = lax.dot_general(a_ref[...].astype(BF16), b_ref[...].astype(BF16), dims,
                               preferred_element_type=F32)

        @pl.when(k == 0)
        def _():
            acc_ref[...] = part

        @pl.when(k > 0)
        def _():
            acc_ref[...] += part

        @pl.when(k == nk - 1)
        def _():
            res = acc_ref[...]
            if has_add:
                res = res + refs[2][...].astype(F32)
            if post is None:
                out_refs[0][...] = res.astype(out_refs[0].dtype)
                return
            n_rows = len(post_rows)
            vals = post_fn(res, [r[...] for r in refs[n_main:n_main + n_rows]],
                           [c[...] for c in refs[n_main + n_rows:n_in]])
            for o, o_ref, val in zip(post_outs, out_refs, vals, strict=True):
                if o[0] == "row":
                    o_ref[...] = val.astype(o_ref.dtype)
                else:
                    @pl.when(i == 0)
                    def _(o_ref=o_ref, val=val):
                        o_ref[...] = val.astype(F32)

                    @pl.when(i > 0)
                    def _(o_ref=o_ref, val=val):
                        o_ref[...] += val.astype(F32)

    return pl.pallas_call(
        body, name=name, grid=(M // tm, N // tn, nk),
        in_specs=in_specs, out_specs=o_spec, out_shape=out_shape,
        scratch_shapes=[pltpu.VMEM((tm, tn), F32)],
        compiler_params=_params(semantics),
    )(*ins)


def _rowwise(name, fn, T, tT, rows=(), prevs=(), nexts=(), consts=(), outs=()):
    n = T // tT
    per8 = tT // SUBLANES
    in_specs, ins = [], []
    for arr in rows:
        in_specs.append(pl.BlockSpec((tT, arr.shape[1]), lambda i: (i, 0)))
        ins.append(arr)
    for arr in prevs:
        in_specs.append(pl.BlockSpec((SUBLANES, arr.shape[1]), lambda i: (jnp.maximum(i * per8 - 1, 0), 0)))
        ins.append(arr)
    for arr in nexts:
        in_specs.append(pl.BlockSpec((SUBLANES, arr.shape[1]),
                                     lambda i: (jnp.minimum((i + 1) * per8, T // SUBLANES - 1), 0)))
        ins.append(arr)
    for arr in consts:
        in_specs.append(pl.BlockSpec(arr.shape, lambda i, nd=arr.ndim: (0,) * nd))
        ins.append(arr)
    out_specs, out_shapes = [], []
    for o in outs:
        if o[0] == "row":
            out_specs.append(pl.BlockSpec((tT, o[1]), lambda i: (i, 0)))
            out_shapes.append(jax.ShapeDtypeStruct((T, o[1]), o[2]))
        else:
            out_specs.append(pl.BlockSpec(o[1], lambda i: (0, 0)))
            out_shapes.append(jax.ShapeDtypeStruct(o[1], F32))
    nr, npv, nnx, nc = len(rows), len(prevs), len(nexts), len(consts)
    n_in = nr + npv + nnx + nc

    def body(*refs):
        i = pl.program_id(0)
        vals = [r[...] for r in refs[:n_in]]
        res = fn(i, n, vals[:nr], vals[nr:nr + npv], vals[nr + npv:nr + npv + nnx], vals[nr + npv + nnx:])
        for o, o_ref, val in zip(outs, refs[n_in:], res, strict=True):
            if o[0] == "row":
                o_ref[...] = val.astype(o_ref.dtype)
            else:
                @pl.when(i == 0)
                def _(o_ref=o_ref, val=val):
                    o_ref[...] = val.astype(F32)

                @pl.when(i > 0)
                def _(o_ref=o_ref, val=val):
                    o_ref[...] += val.astype(F32)

    res = pl.pallas_call(
        body, name=name, grid=(n,), in_specs=in_specs, out_specs=out_specs, out_shape=out_shapes,
        compiler_params=_params(("arbitrary",)),
    )(*ins)
    return list(res)


def _shift_down(x, prev8, i, s):
    rolled = pltpu.roll(x, s, 0)
    head = pltpu.roll(prev8, s, 0)
    head = jnp.where(i == 0, jnp.zeros_like(head), head)
    rid = lax.broadcasted_iota(jnp.int32, head.shape, 0)
    first = jnp.where(rid < s, head, rolled[:SUBLANES])
    if x.shape[0] == SUBLANES:
        return first
    return jnp.concatenate([first, rolled[SUBLANES:]], axis=0)


def _shift_up(x, next8, i, n, s):
    tT = x.shape[0]
    rolled = pltpu.roll(x, tT - s, 0)
    tail = pltpu.roll(next8, SUBLANES - s, 0)
    tail = jnp.where(i == n - 1, jnp.zeros_like(tail), tail)
    rid = lax.broadcasted_iota(jnp.int32, tail.shape, 0)
    last = jnp.where(rid >= SUBLANES - s, tail, rolled[tT - SUBLANES:])
    return jnp.concatenate([rolled[:tT - SUBLANES], last], axis=0)


def _colsum(x):
    return jnp.sum(x, axis=0, keepdims=True)


def _segsum(x, bd):
    return jnp.dot(x, bd, precision=lax.Precision.HIGH, preferred_element_type=F32)


def _block_diag_ones(width, seg):
    idx = np.arange(width) // seg
    return jnp.asarray((idx[:, None] == idx[None, :]).astype(np.float32))


def _sigmoid(z):
    return 1.0 / (1.0 + jnp.exp(-z))


def _softplus(z):
    return jnp.maximum(z, 0.0) + jnp.log(1.0 + jnp.exp(-jnp.abs(z)))


def _rms_fwd(x, g):
    r = lax.rsqrt(jnp.mean(x * x, axis=-1, keepdims=True) + NORM_EPS)
    return x * r * g


def _rms_bwd(x, g, dy):
    r = lax.rsqrt(jnp.mean(x * x, axis=-1, keepdims=True) + NORM_EPS)
    gdy = dy * g
    dx = r * (gdy - x * (r * r) * jnp.mean(x * gdy, axis=-1, keepdims=True))
    return dx, dy * x * r


GELU_C = math.sqrt(2.0 / math.pi)


def _gelu(x):
    return 0.5 * x * (1.0 + jnp.tanh(GELU_C * (x + 0.044715 * x * x * x)))


def _gelu_and_grad(x):
    th = jnp.tanh(GELU_C * (x + 0.044715 * x * x * x))
    half = 0.5 * (1.0 + th)
    return x * half, half + 0.5 * x * (1.0 - th * th) * GELU_C * (1.0 + 3.0 * 0.044715 * x * x)


RW_CHUNK = 64
NN = (((1,), (0,)), ((), ()))
NT = (((1,), (1,)), ((), ()))
TN = (((0,), (0,)), ((), ()))


def _hdot(a, b, dims):
    return lax.dot_general(a, b, dims, precision=lax.Precision.HIGH, preferred_element_type=F32)


def _ldot(a, b, dims):
    return lax.dot_general(a.astype(BF16), b.astype(BF16), dims, preferred_element_type=F32)


def _chunk_masks():
    ti = lax.broadcasted_iota(jnp.int32, (RW_CHUNK, RW_CHUNK), 0)
    tj = lax.broadcasted_iota(jnp.int32, (RW_CHUNK, RW_CHUNK), 1)
    return tj <= ti, tj < ti, (ti == tj).astype(F32)


def _head(x, h):
    return x[:, h * RW_HEAD_DIM:(h + 1) * RW_HEAD_DIM]


def _heads(fn):
    return [fn(h) for h in range(RW_HEADS)]


def _chunk_rows(r, lw, k, a, b, incl_f):
    c = _hdot(incl_f, lw, NN)
    e_prev, e_neg, e_pos = jnp.exp(c - lw), jnp.exp(-c), jnp.exp(c)
    return dict(At=a * e_prev, Bt=b * e_neg, Kt=k * e_neg, Rt=r * e_pos, e_prev=e_prev, e_neg=e_neg, e_pos=e_pos)


def _stack(top, bottom, h):
    return jnp.concatenate([_head(top, h), _head(bottom, h)], axis=0)


def _chunk_coeffs(q, incl, strict):
    C = RW_CHUNK
    ar = _heads(lambda h: _stack(q["At"], q["Rt"], h))
    pb = _heads(lambda h: _hdot(ar[h], _head(q["Bt"], h), NT))
    pk = _heads(lambda h: _hdot(ar[h], _head(q["Kt"], h), NT))
    A1, W1 = [jnp.where(strict, m[:C], 0.0) for m in pb], [jnp.where(incl, m[C:], 0.0) for m in pb]
    A2, W2 = [jnp.where(strict, m[:C], 0.0) for m in pk], [jnp.where(incl, m[C:], 0.0) for m in pk]
    return A1, A2, W1, W2


def _rwkv_chunk_prep(r, lw, k, a, b, v):
    T = r.shape[0]
    nC = T // RW_CHUNK
    H, N = RW_HEADS, RW_HEAD_DIM

    def body(r_ref, lw_ref, k_ref, a_ref, b_ref, v_ref,
             at_ref, bt_ref, kt_ref, rt_ref, a2v_ref, w2v_ref, ti_ref, w1_ref, a2_ref, w2_ref, pl_ref):
        incl, strict, eye = _chunk_masks()
        q = _chunk_rows(r_ref[...], lw_ref[...], k_ref[...], a_ref[...], b_ref[...], incl.astype(F32))
        at_ref[...], bt_ref[...], kt_ref[...], rt_ref[...] = q["At"], q["Bt"], q["Kt"], q["Rt"]
        pl_ref[0] = jnp.broadcast_to(q["e_pos"][RW_CHUNK - 1:RW_CHUNK, :], (SUBLANES, RW_WIDTH))
        A1, A2, W1, W2 = _chunk_coeffs(q, incl, strict)
        V = v_ref[...]
        a2v_ref[...] = jnp.concatenate(_heads(lambda h: _hdot(A2[h], _head(V, h), NN)), axis=1)
        w2v_ref[...] = jnp.concatenate(_heads(lambda h: _ldot(W2[h], _head(V, h), NN)), axis=1)
        tinv, pw = [eye + m for m in A1], A1
        for stage in range(5):
            dot = _hdot if stage == 0 else _ldot
            pw = [dot(m, m, NN) for m in pw]
            tinv = [t + dot(t, m, NN) for t, m in zip(tinv, pw, strict=True)]
        for h in range(H):
            ti_ref[0, h] = tinv[h]
            w1_ref[0, h] = W1[h]
            a2_ref[0, h] = A2[h]
            w2_ref[0, h] = W2[h]

    row_spec = pl.BlockSpec((RW_CHUNK, RW_WIDTH), lambda n: (n, 0))
    st_spec = pl.BlockSpec((1, H, N, N), lambda n: (n, 0, 0, 0))
    row_shape = jax.ShapeDtypeStruct((T, RW_WIDTH), F32)
    st_shape = jax.ShapeDtypeStruct((nC, H, N, N), F32)
    return pl.pallas_call(
        body, name="rwkv_chunk_prep", grid=(nC,),
        in_specs=[row_spec] * 6,
        out_specs=[row_spec] * 6 + [st_spec] * 4 + [pl.BlockSpec((1, SUBLANES, RW_WIDTH), lambda n: (n, 0, 0))],
        out_shape=[row_shape] * 6 + [st_shape] * 4 + [jax.ShapeDtypeStruct((nC, SUBLANES, RW_WIDTH), F32)],
        compiler_params=_params(("parallel",)),
    )(r, lw, k, a, b, v)


def _rwkv_chunk_fwd(v, at, bt, kt, rt, a2v, w2v, tinv, w1, plast):
    T = v.shape[0]
    nC = T // RW_CHUNK
    H, N = RW_HEADS, RW_HEAD_DIM

    def body(v_ref, at_ref, bt_ref, kt_ref, rt_ref, a2v_ref, w2v_ref, ti_ref, w1_ref, pl_ref,
             y_ref, sa_ref, s0_ref, S_ref):
        @pl.when(pl.program_id(0) == 0)
        def _():
            S_ref[...] = jnp.zeros_like(S_ref)

        V, At, Bt, Kt, Rt = v_ref[...], at_ref[...], bt_ref[...], kt_ref[...], rt_ref[...]
        A2V, W2V, p_last = a2v_ref[...], w2v_ref[...], pl_ref[0, 0:1, :]
        S0 = _heads(lambda h: S_ref[h])
        for h in range(H):
            s0_ref[0, h] = S0[h]
        C = RW_CHUNK
        on_state = _heads(lambda h: _hdot(_stack(At, Rt, h), S0[h], NT))
        Sa = _heads(lambda h: _hdot(ti_ref[0, h], on_state[h][:C] + _head(A2V, h), NN))
        X = _heads(lambda h: S0[h] + _hdot(jnp.concatenate([Sa[h], _head(V, h)], axis=0), _stack(Bt, Kt, h), TN))
        for h in range(H):
            S_ref[h] = X[h] * _head(p_last, h)
        Y = _heads(lambda h: on_state[h][C:] + _ldot(w1_ref[0, h], Sa[h], NN) + _head(W2V, h))
        y_ref[...] = jnp.concatenate(Y, axis=1)
        sa_ref[...] = jnp.concatenate(Sa, axis=1)

    row_spec = pl.BlockSpec((RW_CHUNK, RW_WIDTH), lambda n: (n, 0))
    st_spec = pl.BlockSpec((1, H, N, N), lambda n: (n, 0, 0, 0))
    row_shape = jax.ShapeDtypeStruct((T, RW_WIDTH), F32)
    return pl.pallas_call(
        body, name="rwkv_chunk_fwd", grid=(nC,),
        in_specs=[row_spec] * 7 + [st_spec, st_spec, pl.BlockSpec((1, SUBLANES, RW_WIDTH), lambda n: (n, 0, 0))],
        out_specs=[row_spec, row_spec, st_spec],
        out_shape=[row_shape, row_shape, jax.ShapeDtypeStruct((nC, H, N, N), F32)],
        scratch_shapes=[pltpu.VMEM((H, N, N), F32)],
        compiler_params=_params(("arbitrary",)),
    )(v, at, bt, kt, rt, a2v, w2v, tinv, w1, plast)


def _rwkv_chunk_bwd(r, lw, k, a, b, v, dy, s0, tinv, w1, a2, w2, sa):
    T = r.shape[0]
    nC = T // RW_CHUNK
    H, N = RW_HEADS, RW_HEAD_DIM

    def body(r_ref, lw_ref, k_ref, a_ref, b_ref, v_ref, dy_ref, s0_ref, ti_ref, w1_ref, a2_ref, w2_ref, sa_ref,
             dr_ref, dlw_ref, dk_ref, da_ref, db_ref, dv_ref, dS_ref):
        @pl.when(pl.program_id(0) == 0)
        def _():
            dS_ref[...] = jnp.zeros_like(dS_ref)

        incl, strict, _ = _chunk_masks()
        incl_f = incl.astype(F32)
        q = _chunk_rows(r_ref[...], lw_ref[...], k_ref[...], a_ref[...], b_ref[...], incl_f)
        At, Bt, Kt, Rt = q["At"], q["Bt"], q["Kt"], q["Rt"]
        A2, W1, W2 = (_heads(lambda h, ref=ref: ref[0, h]) for ref in (a2_ref, w1_ref, w2_ref))
        V, dY, Sa = v_ref[...], dy_ref[...], sa_ref[...]
        hd = _head
        p_last = q["e_pos"][RW_CHUNK - 1:RW_CHUNK, :]
        S0 = _heads(lambda h: s0_ref[0, h])
        G = _heads(lambda h: dS_ref[h] * hd(p_last, h))
        C = RW_CHUNK
        AR = _heads(lambda h: _stack(At, Rt, h))
        BK = _heads(lambda h: _stack(Bt, Kt, h))
        X = _heads(lambda h: S0[h] + _ldot(_stack(Sa, V, h), BK[h], TN))
        dc_last = jnp.concatenate(_heads(lambda h: jnp.sum(G[h] * X[h], axis=0, keepdims=True)), axis=1)
        dSa = _heads(lambda h: _ldot(hd(Bt, h), G[h], NT) + _ldot(W1[h], hd(dY, h), TN))
        dZ = _heads(lambda h: _ldot(ti_ref[0, h], dSa[h], TN))
        D = _heads(lambda h: jnp.concatenate([dZ[h], hd(dY, h)], axis=0))
        for h in range(H):
            dS_ref[h] = G[h] + _ldot(D[h], AR[h], TN)
        both = jnp.concatenate([strict, incl], axis=0)
        E1 = _heads(lambda h: jnp.where(both, _ldot(D[h], hd(Sa, h), NT), 0.0))
        E2 = _heads(lambda h: jnp.where(both, _ldot(D[h], hd(V, h), NT), 0.0))
        cat = lambda fn: jnp.concatenate(_heads(fn), axis=1)
        dV = cat(lambda h: _ldot(jnp.concatenate([A2[h], W2[h]], axis=0), D[h], TN) + _ldot(hd(Kt, h), G[h], NT))
        dAR = _heads(lambda h: _ldot(E1[h], hd(Bt, h), NN) + _ldot(E2[h], hd(Kt, h), NN) + _ldot(D[h], S0[h], NN))
        dAt, dRt = cat(lambda h: dAR[h][:C]), cat(lambda h: dAR[h][C:])
        dBt = cat(lambda h: _ldot(E1[h], AR[h], TN) + _ldot(hd(Sa, h), G[h], NN))
        dKt = cat(lambda h: _ldot(E2[h], AR[h], TN) + _ldot(hd(V, h), G[h], NN))
        last_row = lax.broadcasted_iota(jnp.int32, (RW_CHUNK, RW_WIDTH), 0) == RW_CHUNK - 1
        dc_prev = dAt * At
        dc = dc_prev + dRt * Rt - dBt * Bt - dKt * Kt + jnp.where(last_row, dc_last, 0.0)
        dr_ref[...] = dRt * q["e_pos"]
        dlw_ref[...] = _hdot(incl_f, dc, TN) - dc_prev
        dk_ref[...] = dKt * q["e_neg"]
        da_ref[...] = dAt * q["e_prev"]
        db_ref[...] = dBt * q["e_neg"]
        dv_ref[...] = dV

    rev = lambda n: nC - 1 - n
    row_spec = pl.BlockSpec((RW_CHUNK, RW_WIDTH), lambda n: (rev(n), 0))
    st_spec = pl.BlockSpec((1, H, N, N), lambda n: (rev(n), 0, 0, 0))
    row_shape = jax.ShapeDtypeStruct((T, RW_WIDTH), F32)
    return pl.pallas_call(
        body, name="rwkv_chunk_bwd", grid=(nC,),
        in_specs=[row_spec] * 7 + [st_spec] * 5 + [row_spec], out_specs=[row_spec] * 6,
        out_shape=[row_shape] * 6, scratch_shapes=[pltpu.VMEM((H, N, N), F32)],
        compiler_params=_params(("arbitrary",)),
    )(r, lw, k, a, b, v, dy, s0, tinv, w1, a2, w2, sa)


def _alibi_slope(head):
    return float(np.float32(2.0 ** (-8.0 * (head + 1) / ATT_HEADS)))


ATT_SPAN = ATT_BLOCK * max(ATT_GROUP_DILATION)
ATT_PAIR_WIDTH = 2 * ATT_HEAD_DIM
ATT_SIDE_BY_SIDE = 16


def _pair_slope(g, hp, j):
    return jnp.where(hp == 0, _alibi_slope(4 * g + j), _alibi_slope(4 * g + 2 + j))


def _att_rows(mi, r, d):
    start = mi * ATT_BLOCK * d + r
    return pl.ds(start, ATT_BLOCK) if d == 1 else pl.ds(start, ATT_BLOCK, stride=d)


def _att_masks():
    qi = lax.broadcasted_iota(jnp.int32, (ATT_BLOCK, ATT_BLOCK), 0)
    kj = lax.broadcasted_iota(jnp.int32, (ATT_BLOCK, ATT_BLOCK), 1)
    return qi, kj


NEG = -1e30


def _att_logits(q, k, slope_d, steps, valid):
    s = lax.dot_general(q.astype(BF16), k.astype(BF16), (((1,), (1,)), ((), ())),
                        preferred_element_type=F32) * (ATT_HEAD_DIM ** -0.5)
    return jnp.where(valid, s - slope_d * steps.astype(F32), NEG)


def _att_fwd(p_att, g):
    T = p_att.shape[0]
    d = ATT_GROUP_DILATION[g]
    W = ATT_PAIR_WIDTH
    nb = T // ATT_SPAN
    mb = ATT_SPAN // (ATT_BLOCK * d)

    def body(q_ref, kc_ref, kp_ref, vc_ref, vp_ref, o_ref, l_ref):
        hp, n = pl.program_id(0), pl.program_id(1)
        qi, kj = _att_masks()
        slopes = [_pair_slope(g, hp, j) * d for j in range(2)]
        blocks = [(r, mi) for r in range(d) for mi in range(mb)]
        for at in range(0, len(blocks), ATT_SIDE_BY_SIDE):
            tasks = []
            for r, mi in blocks[at:at + ATT_SIDE_BY_SIDE]:
                rows = _att_rows(mi, r, d)
                if mi > 0:
                    prev = _att_rows(mi - 1, r, d)
                    kp, vp, has_prev = kc_ref[prev, :], vc_ref[prev, :], True
                else:
                    prev = _att_rows(mb - 1, r, d)
                    kp, vp, has_prev = kp_ref[prev, :], vp_ref[prev, :], n > 0
                q, kc, vc = q_ref[rows, :], kc_ref[rows, :], vc_ref[rows, :]
                for j in range(2):
                    sl = slice(j * ATT_HEAD_DIM, (j + 1) * ATT_HEAD_DIM)
                    tasks.append((q[:, sl], kc[:, sl], kp[:, sl], vc[:, sl], vp[:, sl], has_prev, slopes[j]))
            lc = [_att_logits(t[0], t[1], t[6], qi - kj, kj <= qi) for t in tasks]
            lp = [_att_logits(t[0], t[2], t[6], qi - kj + ATT_BLOCK, (kj >= qi) & t[5]) for t in tasks]
            mx = [jnp.maximum(jnp.max(a, axis=1, keepdims=True), jnp.max(b, axis=1, keepdims=True))
                  for a, b in zip(lc, lp, strict=True)]
            ec = [jnp.exp(a - m) for a, m in zip(lc, mx, strict=True)]
            ep = [jnp.exp(b - m) for b, m in zip(lp, mx, strict=True)]
            den = [jnp.sum(a, axis=1, keepdims=True) + jnp.sum(b, axis=1, keepdims=True)
                   for a, b in zip(ec, ep, strict=True)]
            inv = [1.0 / s for s in den]
            outs = [jnp.dot((a * i).astype(BF16), t[3].astype(BF16), preferred_element_type=F32)
                    + jnp.dot((b * i).astype(BF16), t[4].astype(BF16), preferred_element_type=F32)
                    for a, b, i, t in zip(ec, ep, inv, tasks, strict=True)]
            lses = [jnp.broadcast_to(m + jnp.log(s), (ATT_BLOCK, ATT_HEAD_DIM)) for m, s in zip(mx, den, strict=True)]
            for i, (r, mi) in enumerate(blocks[at:at + ATT_SIDE_BY_SIDE]):
                rows = _att_rows(mi, r, d)
                o_ref[rows, :] = jnp.concatenate(outs[2 * i:2 * i + 2], axis=1)
                l_ref[rows, :] = jnp.concatenate(lses[2 * i:2 * i + 2], axis=1)

    def spec(col0, prev):
        if prev:
            return pl.BlockSpec((ATT_SPAN, W), lambda hp, n: (jnp.maximum(n - 1, 0), col0 + 2 * g + hp))
        return pl.BlockSpec((ATT_SPAN, W), lambda hp, n: (n, col0 + 2 * g + hp))

    o_spec = pl.BlockSpec((ATT_SPAN, W), lambda hp, n: (n, hp))
    o, l = pl.pallas_call(
        body, name=f"att_fwd_g{g}", grid=(2, nb),
        in_specs=[spec(0, False), spec(6, False), spec(6, True), spec(12, False), spec(12, True)],
        out_specs=[o_spec, o_spec],
        out_shape=[jax.ShapeDtypeStruct((T, ATT_GROUP_WIDTH), F32)] * 2,
        compiler_params=_params(("parallel", "arbitrary")),
    )(p_att, p_att, p_att, p_att, p_att)
    return o, l


def _att_bwd(p_att, o, l, do, dl, g):
    T = p_att.shape[0]
    d = ATT_GROUP_DILATION[g]
    W = ATT_PAIR_WIDTH
    nb = T // ATT_SPAN
    mb = ATT_SPAN // (ATT_BLOCK * d)
    scale = ATT_HEAD_DIM ** -0.5

    def body(q_ref, k_ref, v_ref, o_ref, l_ref, do_ref, dl_ref,
             qn_ref, on_ref, ln_ref, don_ref, dln_ref, dq_ref, dk_ref, dv_ref, carry_ref):
        hp, n = pl.program_id(0), pl.program_id(1)
        qi, kj = _att_masks()

        @pl.when(n == 0)
        def _():
            carry_ref[...] = jnp.zeros_like(carry_ref)

        slopes = [_pair_slope(g, hp, j) * d for j in range(2)]
        blocks = [(r, mi) for r in range(d) for mi in range(mb)]
        side_by_side = ATT_SIDE_BY_SIDE // 2
        carry = None
        for at in range(0, len(blocks), side_by_side):
            tasks = []
            for r, mi in blocks[at:at + side_by_side]:
                rows = _att_rows(mi, r, d)
                if mi < mb - 1:
                    nrows = _att_rows(mi + 1, r, d)
                    nxt = (q_ref[nrows, :], o_ref[nrows, :], l_ref[nrows, :], do_ref[nrows, :], dl_ref[nrows, :])
                    has_next = True
                else:
                    nrows = _att_rows(0, r, d)
                    nxt = (qn_ref[nrows, :], on_ref[nrows, :], ln_ref[nrows, :], don_ref[nrows, :],
                           dln_ref[nrows, :])
                    has_next = n < nb - 1
                cur = (q_ref[rows, :], o_ref[rows, :], l_ref[rows, :], do_ref[rows, :], dl_ref[rows, :])
                k_all, v_all = k_ref[rows, :], v_ref[rows, :]
                for j in range(2):
                    sl = slice(j * ATT_HEAD_DIM, (j + 1) * ATT_HEAD_DIM)
                    for blk, steps, valid in ((cur, qi - kj, kj <= qi),
                                              (nxt, qi - kj + ATT_BLOCK, (kj >= qi) & has_next)):
                        q, o_, lse, do_, dlse = (z[:, sl] for z in blk)
                        tasks.append(dict(q=q, o=o_, lse=lse[:, :1], do=do_, dlse=dlse[:, :1], steps=steps,
                                          valid=valid, k=k_all[:, sl], vb=v_all[:, sl].astype(BF16),
                                          slope=slopes[j]))
            p = [jnp.exp(_att_logits(t["q"], t["k"], t["slope"], t["steps"], t["valid"]) - t["lse"]) for t in tasks]
            dp = [lax.dot_general(t["do"].astype(BF16), t["vb"], (((1,), (1,)), ((), ())),
                                  preferred_element_type=F32) for t in tasks]
            dsum = [jnp.sum(t["do"] * t["o"], axis=1, keepdims=True) for t in tasks]
            ds = [a * (b - s + t["dlse"]) for a, b, s, t in zip(p, dp, dsum, tasks, strict=True)]
            dv_ = [jnp.dot(a.T.astype(BF16), t["do"].astype(BF16), preferred_element_type=F32)
                   for a, t in zip(p, tasks, strict=True)]
            dk_ = [jnp.dot(a.T.astype(BF16), t["q"].astype(BF16), preferred_element_type=F32) * scale
                   for a, t in zip(ds, tasks, strict=True)]
            dq_ = [jnp.dot(a.astype(BF16), t["k"].astype(BF16), preferred_element_type=F32) * scale
                   for a, t in zip(ds, tasks, strict=True)]
            for i, (r, mi) in enumerate(blocks[at:at + side_by_side]):
                rows = _att_rows(mi, r, d)
                b = 4 * i
                if mi == 0:
                    carry = carry_ref[r]
                dq_ref[rows, :] = jnp.concatenate([dq_[b], dq_[b + 2]], axis=1) + carry
                carry = jnp.concatenate([dq_[b + 1], dq_[b + 3]], axis=1)
                if mi == mb - 1:
                    carry_ref[r] = carry
                dk_ref[rows, :] = jnp.concatenate([dk_[b] + dk_[b + 1], dk_[b + 2] + dk_[b + 3]], axis=1)
                dv_ref[rows, :] = jnp.concatenate([dv_[b] + dv_[b + 1], dv_[b + 2] + dv_[b + 3]], axis=1)

    head_rows = ATT_BLOCK * d
    nxt_n = lambda n: jnp.minimum((n + 1) * mb, T // head_rows - 1)
    cur_p = lambda col0: pl.BlockSpec((ATT_SPAN, W), lambda hp, n: (n, col0 + 2 * g + hp))
    cur_o = pl.BlockSpec((ATT_SPAN, W), lambda hp, n: (n, hp))
    nxt_o = pl.BlockSpec((head_rows, W), lambda hp, n: (nxt_n(n), hp))
    dq, dk, dv = pl.pallas_call(
        body, name=f"att_bwd_g{g}", grid=(2, nb),
        in_specs=[cur_p(0), cur_p(6), cur_p(12), cur_o, cur_o, cur_o, cur_o,
                  pl.BlockSpec((head_rows, W), lambda hp, n: (nxt_n(n), 2 * g + hp)), nxt_o, nxt_o, nxt_o, nxt_o],
        out_specs=[cur_o, cur_o, cur_o],
        out_shape=[jax.ShapeDtypeStruct((T, ATT_GROUP_WIDTH), F32)] * 3,
        scratch_shapes=[pltpu.VMEM((d, ATT_BLOCK, W), F32)],
        compiler_params=_params(("parallel", "arbitrary")),
    )(p_att, p_att, p_att, o, l, do, dl, p_att, o, l, do, dl)
    return dq, dk, dv


FFN_TILE = 2 * D_FF // N_CHIPS
RKV = 3 * RW_WIDTH
WA = 128
XG = 160
RW_COLS = RKV + WA + XG


def _local_step(x, p, W, target, late_weights=None, early_grads=None, by_chip=False, grad_dtype=F32):
    T = x.shape[0]
    tT = 256
    bd512 = _block_diag_ones(RW_WIDTH, RW_HEAD_DIM)
    bd256 = _block_diag_ones(ATT_GROUP_WIDTH, ATT_HEAD_DIM)
    G = {}
    W = dict(W)

    w_in = W["w_in"]
    w_rkv, w_wa, w_xg, w_att = (w_in[:, :RKV], w_in[:, RKV:RKV + WA], w_in[:, RKV + WA:RW_COLS],
                                w_in[:, RW_COLS:])
    mu = W["rw_mu"]
    mu_rkv, mu_wa, mu_xg = mu[:, :RKV], mu[:, RKV:RKV + WA], mu[:, RKV + WA:]
    zpad = jnp.zeros((64, RW_WIDTH), W["rw_w_up"].dtype)
    w_up_pad = jnp.concatenate([W["rw_w_up"], zpad], axis=0)
    a_up_pad = jnp.concatenate([zpad, W["rw_a_up"]], axis=0)
    r_k = W["rw_r_k"].reshape(1, RW_WIDTH)

    (h,) = _rowwise("norm_mix", lambda i, n, r, pv, nx, c: [_rms_fwd(r[0], c[0])], T, tT,
                    rows=[x], consts=[W["g_mix"]], outs=[("row", D_MODEL, BF16)])
    p_rkv = _mm("proj_rkv", h, w_rkv, "nn")
    p_wa = _mm("proj_wa", h, w_wa, "nn")
    p_xg = _mm("proj_xg", h, w_xg, "nn")
    p_att = _mm("proj_att", h, w_att, "nn", tn=768)
    z_gate = _mm("proj_gate", h, W["w_gate"], "nn")

    def rw_pre_core(i, rows, prevs, consts):
        prkv, pwa, pxg = rows[:3]
        (mrkv, mwa, mxg, w0, a0, k_k, k_a, wup, aup, gup, bd) = consts[:11]
        m_rkv = prkv + (_shift_down(prkv, prevs[0], i, 1) - prkv) * mrkv
        m_wa = pwa + (_shift_down(pwa, prevs[1], i, 1) - pwa) * mwa
        m_xg = pxg + (_shift_down(pxg, prevs[2], i, 1) - pxg) * mxg
        r, k, v = m_rkv[:, :RW_WIDTH], m_rkv[:, RW_WIDTH:2 * RW_WIDTH], m_rkv[:, 2 * RW_WIDTH:]
        tw = jnp.tanh(m_wa)
        lw = w0 + jnp.dot(tw.astype(BF16), wup.astype(BF16), preferred_element_type=F32)
        wlog = -_softplus(-lw) - 0.5
        log_decay = -jnp.exp(wlog)
        a = _sigmoid(a0 + jnp.dot(m_wa.astype(BF16), aup.astype(BF16), preferred_element_type=F32))
        sg = _sigmoid(m_xg)
        gate = jnp.dot(sg.astype(BF16), gup.astype(BF16), preferred_element_type=F32)
        kkp = k * k_k
        nrm = jnp.sqrt(_segsum(kkp * kkp, bd))
        nrm_c = jnp.maximum(nrm, 1e-12)
        kk = kkp / nrm_c
        k2 = k * (1.0 + (a - 1.0) * k_a)
        return dict(r=r, k=k, v=v, tw=tw, lw=lw, wlog=wlog, log_decay=log_decay, a=a, sg=sg, gate=gate, kkp=kkp,
                    nrm=nrm, nrm_c=nrm_c, kk=kk, k2=k2, m_rkv=m_rkv, m_wa=m_wa, m_xg=m_xg)

    pre_consts = [mu_rkv, mu_wa, mu_xg, W["rw_w0"], W["rw_a0"], W["rw_k_k"], W["rw_k_a"],
                  w_up_pad, a_up_pad, W["rw_g_up"], bd512]

    def rw_pre(i, n, rows, prevs, nexts, consts):
        q = rw_pre_core(i, rows, prevs, consts)
        return [q["r"], q["log_decay"], q["k2"], q["v"], -q["kk"], q["kk"] * q["a"], q["gate"]]

    r_s, w_s, k_s, v_s, a_s, b_s, gate_s = _rowwise(
        "rwkv_pre", rw_pre, T, tT, rows=[p_rkv, p_wa, p_xg], prevs=[p_rkv, p_wa, p_xg], consts=pre_consts,
        outs=[("row", RW_WIDTH, F32)] * 7)
    (at_s, bt_s, kt_s, rt_s, a2v_s, w2v_s, tinv_s, w1_s, a2_s, w2_s,
     plast_s) = _rwkv_chunk_prep(r_s, w_s, k_s, a_s, b_s, v_s)
    y_scan, sa_s, s0_s = _rwkv_chunk_fwd(v_s, at_s, bt_s, kt_s, rt_s, a2v_s, w2v_s, tinv_s, w1_s, plast_s)

    def rw_post_core(rows, consts):
        y, r, k2, v, gate = rows[:5]
        ln_g, ln_b, rk, bd = consts[:4]
        mean = _segsum(y, bd) * (1.0 / RW_HEAD_DIM)
        yc = y - mean
        var = _segsum(yc * yc, bd) * (1.0 / RW_HEAD_DIM)
        rstd = lax.rsqrt(var + RW_LN_EPS)
        yn = yc * rstd
        s = _segsum(r * k2 * rk, bd)
        return dict(yn=yn, rstd=rstd, s=s, pre=yn * ln_g + ln_b + s * v)

    post_consts = [W["rw_ln_g"], W["rw_ln_b"], r_k, bd512]
    (y_a,) = _rowwise("rwkv_post", lambda i, n, r, pv, nx, c: [rw_post_core(r, c)["pre"] * r[4]], T, tT,
                      rows=[y_scan, r_s, k_s, v_s, gate_s], consts=post_consts, outs=[("row", RW_WIDTH, BF16)])

    att = [_att_fwd(p_att, g) for g in range(3)]

    def comb_weights(ls):
        mx = jnp.maximum(jnp.maximum(ls[0], ls[1]), ls[2])
        es = [jnp.exp(l - mx) for l in ls]
        den = es[0] + es[1] + es[2]
        return [e / den for e in es]

    def att_comb(i, n, rows, pv, nx, c):
        wts = comb_weights(rows[3:6])
        return [wts[0] * rows[0] + wts[1] * rows[1] + wts[2] * rows[2]]

    (y_b,) = _rowwise("att_combine", att_comb, T, tT, rows=[att[0][0], att[1][0], att[2][0], att[0][1], att[1][1],
                                                            att[2][1]], outs=[("row", ATT_GROUP_WIDTH, BF16)])

    if late_weights is not None:
        W.update(late_weights(y_b))
    br_a = _mm("branch_a", y_a, W["w_branch_a"], "nn")
    br_b = _mm("branch_b", y_b, W["w_branch_b"], "nn")

    def merge(i, n, rows, pv, nx, c):
        gates = _sigmoid(rows[0] + c[0])
        return [gates[:, :D_MODEL] * rows[1] + gates[:, D_MODEL:] * rows[2]]

    (merged,) = _rowwise("merge", merge, T, tT, rows=[z_gate, br_a, br_b], consts=[W["b_gate"]],
                         outs=[("row", D_MODEL, BF16)])
    with_norm = lambda res, rows, consts: [res, _rms_fwd(res, consts[0])]
    stream_and_norm = [("row", D_MODEL, F32), ("row", D_MODEL, BF16)]
    x1, h2 = _mm("mix_out", merged, W["w_out"], "nn", add=x, post=(with_norm, [], [W["g_ffn"]], stream_and_norm))

    u = _mm("ffn_up", h2, W["w_up"], "nn", tn=FFN_TILE)

    def conv_core(i, rows, prevs, consts):
        uu, cw, cb = rows[0], consts[0], consts[1]
        u1 = _shift_down(uu, prevs[0], i, 1)
        u2 = _shift_down(uu, prevs[0], i, 2)
        uc = cb + cw[0:1] * uu + cw[1:2] * u1 + cw[2:3] * u2
        return uc[:, :D_FF], uc[:, D_FF:], u1, u2

    def glu(i, n, rows, prevs, nx, consts):
        gate, val, _, _ = conv_core(i, rows, prevs, consts)
        return [_gelu(gate) * val]

    tF = 128
    (act,) = _rowwise("conv_glu", glu, T, tF, rows=[u], prevs=[u], consts=[W["conv_w"], W["conv_b"]],
                      outs=[("row", D_FF, BF16)])
    x2, h3 = _mm("ffn_down", act, W["w_down"], "nn", add=x1, post=(with_norm, [], [W["g_ple"]], stream_and_norm))

    z_ple = _mm("ple_gate", h3, W["w_ple_gate"], "nn")
    e_ple = _mm("ple_emb", p, W["w_ple"], "nn")

    def head(i, n, rows, pv, nx, consts):
        x2_, z, e, tgt = rows
        pg = _sigmoid(z)
        x3 = x2_ + pg * e
        y = _rms_fwd(x3, consts[0])
        err = y - tgt
        loss = 0.5 * jnp.sum(jnp.sum(err * err, axis=1, keepdims=True) * (1.0 / D_MODEL), axis=0, keepdims=True)
        dy = err * (1.0 / D_MODEL)
        dx3, dgf = _rms_bwd(x3, consts[0], dy)
        return [dx3, dx3 * pg, dx3 * e * pg * (1.0 - pg), jnp.broadcast_to(loss, (1, LANES)), _colsum(dgf)]

    dx3, de, dz, loss_acc, G["g_final"] = _rowwise(
        "loss_head", head, T, tT, rows=[x2, z_ple, e_ple, target], consts=[W["g_final"].reshape(1, D_MODEL)],
        outs=[("row", D_MODEL, F32), ("row", D_MODEL, BF16), ("row", D_MODEL, BF16), ("acc", (1, LANES)),
              ("acc", (1, D_MODEL))])
    G["w_ple"] = _mm("d_w_ple", p, de, "tn", grad_dtype, out_by_chip=by_chip)
    G["w_ple_gate"] = _mm("d_w_ple_gate", h3, dz, "tn", grad_dtype)
    def norm_bwd(i, n, rows, pv, nx, consts):
        dx, dg = _rms_bwd(rows[0], consts[0], rows[1])
        return [rows[2] + dx, _colsum(dg)]

    through_norm = lambda res, rows, consts: norm_bwd(0, 0, [rows[0], res, rows[1]], [], [], consts)
    stream_and_gain = [("row", D_MODEL, F32), ("acc", (1, D_MODEL))]
    dx2, G["g_ple"] = _mm("d_h3", dz, W["w_ple_gate"], "nt", tm=512,
                          post=(through_norm, [x2, dx3], [W["g_ple"]], stream_and_gain))

    dact = _mm("d_act", dx2, W["w_down"], "nt")
    G["w_down"] = _mm("d_w_down", act, dx2, "tn", grad_dtype)

    def glu_grad(gate, val, da):
        act_, slope = _gelu_and_grad(gate)
        return jnp.concatenate([da * val * slope, da * act_], axis=1)

    def glu_bwd(i, n, rows, prevs, nexts, consts):
        uu, da = rows
        cw = consts[0]
        gate, val, u1, u2 = conv_core(i, rows, prevs, consts)
        duc = glu_grad(gate, val, da)
        dcw = jnp.concatenate([_colsum(duc * uu), _colsum(duc * u1), _colsum(duc * u2)], axis=0)
        gate_n, val_n, _, _ = conv_core(1, [nexts[0]], [uu[tF - SUBLANES:]], consts)
        duc_n = glu_grad(gate_n, val_n, nexts[1])
        du = (cw[0:1] * duc + cw[1:2] * _shift_up(duc, duc_n, i, n, 1) + cw[2:3] * _shift_up(duc, duc_n, i, n, 2))
        return [du, _colsum(duc), dcw]

    du, G["conv_b"], G["conv_w"] = _rowwise(
        "d_conv_glu", glu_bwd, T, tF, rows=[u, dact], prevs=[u], nexts=[u, dact],
        consts=[W["conv_w"], W["conv_b"]],
        outs=[("row", 2 * D_FF, BF16), ("acc", (1, 2 * D_FF)), ("acc", (3, 2 * D_FF))])
    G["w_up"] = _mm("d_w_up", h2, du, "tn", grad_dtype, out_by_chip=by_chip, tn=FFN_TILE)
    dh2 = _mm("d_h2", du, W["w_up"], "nt", tk=FFN_TILE)
    dx1, G["g_ffn"] = _rowwise("d_norm_ffn", norm_bwd, T, tT, rows=[x1, dh2, dx2], consts=[W["g_ffn"]],
                               outs=[("row", D_MODEL, F32), ("acc", (1, D_MODEL))])

    b_gate = W["b_gate"]
    if early_grads is not None:
        b_gate = b_gate + early_grads(G, 0)[0:1, 0:1]
    dmerged = _mm("d_merged", dx1, W["w_out"], "nt")
    G["w_out"] = _mm("d_w_out", merged, dx1, "tn", grad_dtype)

    def merge_bwd(i, n, rows, pv, nx, consts):
        z, a_, b_, dm = rows
        gates = _sigmoid(z + consts[0])
        ga, gb = gates[:, :D_MODEL], gates[:, D_MODEL:]
        dz_ = jnp.concatenate([dm * a_ * ga * (1.0 - ga), dm * b_ * gb * (1.0 - gb)], axis=1)
        return [dm * ga, dm * gb, dz_, _colsum(dz_)]

    d_br_a, d_br_b, dz_gate, G["b_gate"] = _rowwise(
        "d_merge", merge_bwd, T, tT, rows=[z_gate, br_a, br_b, dmerged], consts=[b_gate],
        outs=[("row", D_MODEL, BF16), ("row", D_MODEL, BF16), ("row", 2 * D_MODEL, BF16), ("acc", (1, 2 * D_MODEL))])
    G["w_branch_a"] = _mm("d_w_branch_a", y_a, d_br_a, "tn", grad_dtype, out_by_chip=by_chip)
    G["w_branch_b"] = _mm("d_w_branch_b", y_b, d_br_b, "tn", grad_dtype, out_by_chip=by_chip)
    G["w_gate"] = _mm("d_w_gate", h, dz_gate, "tn", grad_dtype, out_by_chip=by_chip)
    if early_grads is not None:
        post_consts = [post_consts[0] + early_grads(G, 1)[0:1, 0:1]] + post_consts[1:]
    dy_a = _mm("d_y_a", d_br_a, W["w_branch_a"], "nt")
    dy_b = _mm("d_y_b", d_br_b, W["w_branch_b"], "nt")

    def att_comb_bwd(i, n, rows, pv, nx, consts):
        os_, ls, dy = rows[0:3], rows[3:6], rows[6]
        wts = comb_weights(ls)
        dws = [_segsum(dy * o_, consts[0]) for o_ in os_]
        mix = wts[0] * dws[0] + wts[1] * dws[1] + wts[2] * dws[2]
        return [wts[g_] * dy for g_ in range(3)] + [wts[g_] * (dws[g_] - mix) for g_ in range(3)]

    comb = _rowwise("d_att_combine", att_comb_bwd, T, tT,
                    rows=[att[0][0], att[1][0], att[2][0], att[0][1], att[1][1], att[2][1], dy_b], consts=[bd256],
                    outs=[("row", ATT_GROUP_WIDTH, F32)] * 6)
    dqkv = [_att_bwd(p_att, att[g][0], att[g][1], comb[g], comb[3 + g], g) for g in range(3)]
    dp_att = jnp.concatenate([dqkv[g][part] for part in range(3) for g in range(3)], axis=1).astype(BF16)

    def rw_post_bwd(i, n, rows, pv, nx, consts):
        y, r, k2, v, gate, dya = rows
        ln_g, ln_b, rk, bd = consts
        q = rw_post_core(rows, consts)
        dpre = dya * gate
        dgate = dya * q["pre"]
        dyn = dpre * ln_g
        inv = 1.0 / RW_HEAD_DIM
        dy_scan = q["rstd"] * (dyn - _segsum(dyn, bd) * inv - q["yn"] * (_segsum(dyn * q["yn"], bd) * inv))
        ds = _segsum(dpre * v, bd)
        return [dy_scan, dgate, ds * k2 * rk, ds * r * rk, dpre * q["s"],
                _colsum(dpre * q["yn"]), _colsum(dpre), _colsum(ds * r * k2)]

    dy_scan, dgate, dr_b, dk2_b, dv_b, G["rw_ln_g"], G["rw_ln_b"], d_rk = _rowwise(
        "d_rwkv_post", rw_post_bwd, T, tT, rows=[y_scan, r_s, k_s, v_s, gate_s, dy_a], consts=post_consts,
        outs=[("row", RW_WIDTH, F32)] * 5 + [("acc", (1, RW_WIDTH))] * 3)
    G["rw_r_k"] = d_rk.reshape(RW_HEADS, RW_HEAD_DIM)

    dr_s, dw_s, dk_s, da_s, db_s, dv_s = _rwkv_chunk_bwd(r_s, w_s, k_s, a_s, b_s, v_s, dy_scan, s0_s, tinv_s, w1_s,
                                                         a2_s, w2_s, sa_s)

    def rw_pre_bwd(i, n, rows, prevs, nx, consts):
        q = rw_pre_core(i, rows, prevs, consts)
        (mrkv, mwa, mxg, w0, a0, k_k, k_a, wup, aup, gup, bd) = consts
        dr, dlogdecay, dk2, dv, dav, dbv, dgate_ = rows[3:10]
        dr = dr + rows[10]
        dk2 = dk2 + rows[11]
        dv = dv + rows[12]
        a, k, kk = q["a"], q["k"], q["kk"]
        dk = dk2 * (1.0 + (a - 1.0) * k_a)
        da = dk2 * k * k_a + dbv * kk
        dkk = dbv * a - dav
        live = q["nrm"] > 1e-12
        dkkp = jnp.where(live, dkk - kk * _segsum(dkk * kk, bd), dkk) / q["nrm_c"]
        dk = dk + dkkp * k_k
        dlw = dlogdecay * q["log_decay"] * _sigmoid(-q["lw"])
        dla = da * a * (1.0 - a)
        nt = (((1,), (1,)), ((), ()))
        dtw = lax.dot_general(dlw.astype(BF16), wup.astype(BF16), nt, preferred_element_type=F32)
        dxa = lax.dot_general(dla.astype(BF16), aup.astype(BF16), nt, preferred_element_type=F32)
        dm_wa = dtw * (1.0 - q["tw"] * q["tw"]) + dxa
        dsg = lax.dot_general(dgate_.astype(BF16), gup.astype(BF16), nt, preferred_element_type=F32)
        dm_xg = dsg * q["sg"] * (1.0 - q["sg"])
        dm_rkv = jnp.concatenate([dr, dk, dv], axis=1)
        prkv, pwa, pxg = rows[:3]
        dmu = jnp.concatenate([_colsum(dm_rkv * (_shift_down(prkv, prevs[0], i, 1) - prkv)),
                               _colsum(dm_wa * (_shift_down(pwa, prevs[1], i, 1) - pwa)),
                               _colsum(dm_xg * (_shift_down(pxg, prevs[2], i, 1) - pxg))], axis=1)
        return [dm_rkv, dm_wa, dm_xg, dlw, dla, q["tw"], q["m_wa"], q["sg"], dmu,
                _colsum(dlw), _colsum(dla), _colsum(dkkp * k), _colsum(dk2 * k * (a - 1.0))]

    (dm_rkv, dm_wa, dm_xg, dlw, dla, tw_s, mwa_s, sg_s, G["rw_mu"], G["rw_w0"], G["rw_a0"], G["rw_k_k"],
     G["rw_k_a"]) = _rowwise(
        "d_rwkv_pre", rw_pre_bwd, T, tT,
        rows=[p_rkv, p_wa, p_xg, dr_s, dw_s, dk_s, dv_s, da_s, db_s, dgate, dr_b, dk2_b, dv_b],
        prevs=[p_rkv, p_wa, p_xg], consts=pre_consts,
        outs=[("row", RKV, F32), ("row", WA, F32), ("row", XG, F32), ("row", RW_WIDTH, BF16),
              ("row", RW_WIDTH, BF16), ("row", WA, BF16), ("row", WA, BF16), ("row", XG, BF16),
              ("acc", (1, RW_COLS))] + [("acc", (1, RW_WIDTH))] * 4)
    G["rw_w_up"] = _mm("d_rw_w_up", tw_s, dlw, "tn", grad_dtype)[:64]
    G["rw_a_up"] = _mm("d_rw_a_up", mwa_s, dla, "tn", grad_dtype)[64:]
    G["rw_g_up"] = _mm("d_rw_g_up", sg_s, dgate, "tn", grad_dtype)

    def shift_bwd(i, n, rows, pv, nexts, consts):
        return [rows[j] * (1.0 - consts[j]) + _shift_up(rows[j], nexts[j], i, n, 1) * consts[j] for j in range(3)]

    dp_rkv, dp_wa, dp_xg = _rowwise(
        "d_token_shift", shift_bwd, T, tT, rows=[dm_rkv, dm_wa, dm_xg], nexts=[dm_rkv, dm_wa, dm_xg],
        consts=[mu_rkv, mu_wa, mu_xg], outs=[("row", RKV, BF16), ("row", WA, BF16), ("row", XG, BF16)])

    G["w_in"] = jnp.concatenate([_mm("d_w_rkv", h, dp_rkv, "tn", grad_dtype), _mm("d_w_wa", h, dp_wa, "tn", grad_dtype),
                                 _mm("d_w_xg", h, dp_xg, "tn", grad_dtype), _mm("d_w_att", h, dp_att, "tn", grad_dtype, tn=768)], axis=1)
    if early_grads is not None:
        w_wa = w_wa + early_grads(G, 2)[0:1, 0:1].astype(w_wa.dtype)
    dh = _mm("d_h_gate", dz_gate, W["w_gate"], "nt")
    dh = _mm("d_h_rkv", dp_rkv, w_rkv, "nt", add=dh)
    dh = _mm("d_h_wa", dp_wa, w_wa, "nt", add=dh)
    dh = _mm("d_h_xg", dp_xg, w_xg, "nt", add=dh)
    dx, G["g_mix"] = _mm("d_h_att", dp_att, w_att, "nt", add=dh, tm=512,
                         post=(through_norm, [x, dx1], [W["g_mix"]], stream_and_gain))
    return loss_acc[:, :1], dx, G


HBM_SPEC = pl.BlockSpec(memory_space=pltpu.HBM)


def _place():
    x, y, c = lax.axis_index("x"), lax.axis_index("y"), lax.axis_index("c")
    return x, y, c, [(1 - x, y), (x, 1 - y), (1 - x, 1 - y)]


def _remote(src, dst, send_sems, recv_sems, k, to):
    return pltpu.make_async_remote_copy(src_ref=src, dst_ref=dst, send_sem=send_sems.at[k], recv_sem=recv_sems.at[k],
                                        device_id=to, device_id_type=MESH)


ROW_ALIGN = 16


def _splits(rows):
    return rows % (2 * ROW_ALIGN) == 0


def _half_rows(ref_rows, c, first):
    half = ref_rows // 2
    which = c if first else 1 - c
    return pl.ds(pl.multiple_of(which * half, ROW_ALIGN), half)


def _gather_chips(shards):
    n = len(shards)
    split = [_splits(s.shape[0]) for s in shards]

    def body(*refs):
        w_refs, out_refs = refs[:n], refs[n:2 * n]
        send_sems, recv_sems = refs[2 * n:]
        x, y, c, chips = _place()
        me = 2 * x + y
        sends, passed = [], []
        for i in range(n):
            for j, (px, py) in enumerate(chips):
                if split[i]:
                    mine = _half_rows(w_refs[i].shape[0], c, True)
                    cp = _remote(w_refs[i].at[mine], out_refs[i].at[me, mine], send_sems, recv_sems, 6 * i + j,
                                 (px, py, c))
                else:
                    cp = _remote(w_refs[i], out_refs[i].at[me], send_sems, recv_sems, 6 * i + j, (px, py, c))
                cp.start()
                sends.append(cp)
        for i in range(n):
            for j, (px, py) in enumerate(chips):
                if split[i]:
                    landed = out_refs[i].at[2 * px + py, _half_rows(w_refs[i].shape[0], c, True)]
                    _remote(landed, landed, send_sems, recv_sems, 6 * i + j, (px, py, c)).wait_recv()
                    cp = _remote(landed, landed, send_sems, recv_sems, 6 * i + 3 + j, (x, y, 1 - c))
                    cp.start()
                    passed.append(cp)
                else:
                    landed = out_refs[i].at[2 * px + py]
                    _remote(landed, landed, send_sems, recv_sems, 6 * i + j, (px, py, c)).wait_recv()
        for i in range(n):
            if split[i]:
                for j, (px, py) in enumerate(chips):
                    landed = out_refs[i].at[2 * px + py, _half_rows(w_refs[i].shape[0], c, False)]
                    _remote(landed, landed, send_sems, recv_sems, 6 * i + 3 + j, (x, y, 1 - c)).wait_recv()
        for cp in sends + passed:
            cp.wait_send()

    outs = pl.pallas_call(
        body, name="gather_weights", in_specs=[HBM_SPEC] * n, out_specs=[HBM_SPEC] * n,
        out_shape=[jax.ShapeDtypeStruct((N_CHIPS,) + s.shape, s.dtype) for s in shards],
        scratch_shapes=[pltpu.SemaphoreType.DMA((6 * n,)), pltpu.SemaphoreType.DMA((6 * n,))],
    )(*shards)
    me = 2 * lax.axis_index("x") + lax.axis_index("y")
    return [lax.dynamic_update_slice(o, s[None], (me, 0, 0)) for o, s in zip(outs, shards, strict=True)]


def _join_halves(reds):
    n = len(reds)

    def body(*refs):
        r_refs, out_refs = refs[:n], refs[n:2 * n]
        send_sems, recv_sems = refs[2 * n:]
        x, y, c, _ = _place()
        cps = []
        for i in range(n):
            mine = _half_rows(out_refs[i].shape[0], c, True)
            cp = _remote(r_refs[i], out_refs[i].at[mine], send_sems, recv_sems, i, (x, y, 1 - c))
            cp.start()
            cps.append(cp)
        for cp in cps:
            cp.wait()

    outs = pl.pallas_call(
        body, name="join_halves", in_specs=[HBM_SPEC] * n, out_specs=[HBM_SPEC] * n,
        out_shape=[jax.ShapeDtypeStruct((2 * r.shape[0], r.shape[1]), r.dtype) for r in reds],
        scratch_shapes=[pltpu.SemaphoreType.DMA((n,)), pltpu.SemaphoreType.DMA((n,))],
    )(*reds)
    c = lax.axis_index("c")
    return [lax.dynamic_update_slice(o, r, (c * r.shape[0], 0)) for o, r in zip(outs, reds, strict=True)]


def _gather_all(vec):
    def body(v_ref, out_ref, send_sems, recv_sems, local_sem):
        x, y, c, _ = _place()
        me = 4 * x + 2 * y + c
        local = pltpu.make_async_copy(v_ref, out_ref.at[me], local_sem)
        local.start()
        peers = [(x ^ (k >> 2), y ^ ((k >> 1) & 1), c ^ (k & 1)) for k in range(1, N_DEV)]
        sends = [_remote(v_ref, out_ref.at[me], send_sems, recv_sems, k, to) for k, to in enumerate(peers)]
        for cp in sends:
            cp.start()
        for k, (px, py, pc) in enumerate(peers):
            landed = out_ref.at[4 * px + 2 * py + pc]
            _remote(landed, landed, send_sems, recv_sems, k, (px, py, pc)).wait_recv()
        for cp in sends:
            cp.wait_send()
        local.wait()

    return pl.pallas_call(
        body, name="gather_small", in_specs=[HBM_SPEC], out_specs=HBM_SPEC,
        out_shape=jax.ShapeDtypeStruct((N_DEV,) + vec.shape, vec.dtype),
        scratch_shapes=[pltpu.SemaphoreType.DMA((7,)), pltpu.SemaphoreType.DMA((7,)), pltpu.SemaphoreType.DMA],
    )(vec)


SEM_SPEC = pl.BlockSpec(memory_space=pltpu.SEMAPHORE)
PEERS = N_DEV - 1
DATAFLOW = pltpu.SideEffectType.DATAFLOW_SIDE_EFFECTING


def _travel_copies(mode, src_refs, land_refs, send_sems, recv_sems):
    x, y, c, chips = _place()
    me = 2 * x + y
    pairs = []
    for i, (src, land) in enumerate(zip(src_refs, land_refs, strict=True)):
        if mode == "scatter":
            for k in range(1, N_DEV):
                px, py, pc = x ^ (k >> 2), y ^ ((k >> 1) & 1), c ^ (k & 1)
                mine = src.at[2 * px + py, _half_rows(src.shape[1], pc, True)]
                there, here = land.at[4 * x + 2 * y + c], land.at[4 * px + 2 * py + pc]
                send = functools.partial(_remote, mine, there, send_sems, recv_sems, PEERS * i + k - 1, (px, py, pc))
                arrival = functools.partial(_remote, mine, here, send_sems, recv_sems, PEERS * i + k - 1, (px, py, pc))
                pairs.append((send, arrival))
            continue
        for j, (px, py) in enumerate(chips):
            peer = 2 * px + py
            if _splits(src.shape[0]):
                rows = _half_rows(src.shape[0], c, True)
                mine, there, here = src.at[rows], land.at[me, rows], land.at[peer, rows]
            else:
                mine, there, here = src, land.at[me], land.at[peer]
            send = functools.partial(_remote, mine, there, send_sems, recv_sems, PEERS * i + j, (px, py, c))
            arrival = functools.partial(_remote, mine, here, send_sems, recv_sems, PEERS * i + j, (px, py, c))
            pairs.append((send, arrival))
    return pairs


def _share_halves(name, lands):
    idx = [i for i, a in enumerate(lands) if _splits(a.shape[1])]
    n = len(idx)

    def body(*refs):
        in_refs, out_refs = refs[:n], refs[n:2 * n]
        send_sems, recv_sems = refs[2 * n:]
        x, y, c, chips = _place()
        cps = []
        for i, (src, dst) in enumerate(zip(in_refs, out_refs, strict=True)):
            for j, (px, py) in enumerate(chips):
                mine = _half_rows(src.shape[1], c, True)
                cp = _remote(src.at[2 * px + py, mine], dst.at[2 * px + py, mine], send_sems, recv_sems, 3 * i + j,
                             (x, y, 1 - c))
                cp.start()
                cps.append(cp)
        for i, dst in enumerate(out_refs):
            for j, (px, py) in enumerate(chips):
                theirs = dst.at[2 * px + py, _half_rows(dst.shape[1], c, False)]
                _remote(theirs, theirs, send_sems, recv_sems, 3 * i + j, (x, y, 1 - c)).wait_recv()
        for cp in cps:
            cp.wait_send()

    outs = pl.pallas_call(
        body, name=name, in_specs=[HBM_SPEC] * n, out_specs=[HBM_SPEC] * n,
        out_shape=[jax.ShapeDtypeStruct(lands[i].shape, lands[i].dtype) for i in idx],
        input_output_aliases={i: i for i in range(n)},
        scratch_shapes=[pltpu.SemaphoreType.DMA((3 * n,)), pltpu.SemaphoreType.DMA((3 * n,))],
    )(*[lands[i] for i in idx])
    done = list(lands)
    for i, o in zip(idx, outs, strict=True):
        done[i] = o
    return done


def _travel_start(name, mode, srcs):
    n = len(srcs)
    lands = [lax.empty((N_CHIPS,) + s.shape if mode == "gather" else (N_DEV, s.shape[1] // 2, s.shape[2]), s.dtype)
             for s in srcs]

    def body(*refs):
        src_refs, land_refs = refs[:n], refs[n:2 * n]
        send_sems, recv_sems = refs[2 * n], refs[2 * n + 1]
        token = refs[-1]
        for send, _ in _travel_copies(mode, src_refs, land_refs, send_sems, recv_sems):
            send().start()
        token[...] = jnp.zeros_like(token)

    hbm = lambda a: pltpu.HBM(a.shape, a.dtype)
    outs = pl.pallas_call(
        body, name=name,
        out_shape=(pltpu.SemaphoreType.DMA((PEERS * n,)), pltpu.SemaphoreType.DMA((PEERS * n,)),
                   *[hbm(s) for s in srcs],
                   *[hbm(a) for a in lands], jax.ShapeDtypeStruct((SUBLANES, LANES), F32)),
        in_specs=[HBM_SPEC] * (2 * n),
        out_specs=(SEM_SPEC, SEM_SPEC, *[HBM_SPEC] * (2 * n), pl.BlockSpec(memory_space=pltpu.VMEM)),
        input_output_aliases={i: 2 + i for i in range(2 * n)},
        compiler_params=pltpu.CompilerParams(has_side_effects=DATAFLOW),
    )(*[pltpu.with_memory_space_constraint(a, pltpu.HBM) for a in list(srcs) + lands])
    return outs[0], outs[1], list(outs[2:2 + n]), list(outs[2 + n:2 + 2 * n]), outs[-1]


def _travel_wait(name, mode, send_sems, recv_sems, srcs, lands, after):
    n = len(srcs)

    def body(*refs):
        src_refs, land_refs = refs[:n], refs[n:2 * n]
        send_sems_, recv_sems_ = refs[2 * n], refs[2 * n + 1]
        for send, arrival in _travel_copies(mode, src_refs, land_refs, send_sems_, recv_sems_):
            send().wait_send()
            arrival().wait_recv()

    hbm = lambda a: pltpu.HBM(a.shape, a.dtype)
    outs = pl.pallas_call(
        body, name=name, out_shape=tuple(hbm(a) for a in list(srcs) + list(lands)),
        in_specs=[HBM_SPEC] * (2 * n) + [SEM_SPEC, SEM_SPEC, pl.BlockSpec(memory_space=pl.ANY)],
        out_specs=tuple([HBM_SPEC] * (2 * n)), input_output_aliases={i: i for i in range(2 * n)},
        compiler_params=pltpu.CompilerParams(has_side_effects=DATAFLOW),
    )(*srcs, *lands, send_sems, recv_sems, after)
    c = lax.axis_index("c")
    me = 2 * lax.axis_index("x") + lax.axis_index("y")
    if mode == "gather":
        slot, own = me, [s[None] for s in outs[:n]]
    else:
        slot = 2 * me + c
        own = [lax.dynamic_slice(s, (me, c * (s.shape[1] // 2), 0), (1, s.shape[1] // 2, s.shape[2])) for s in outs[:n]]
    return [lax.dynamic_update_slice(a, o, (slot,) + (0,) * (a.ndim - 1)) for a, o in zip(outs[n:], own, strict=True)]


SUM_TILE_BYTES = 4 * 1024 * 1024


def _sum_rows(half, cols):
    best = ROW_ALIGN
    for t in range(ROW_ALIGN, half + 1, ROW_ALIGN):
        if half % t == 0 and N_CHIPS * t * cols * 4 <= SUM_TILE_BYTES:
            best = t
    return best


def _sum_devices(name, parts):
    n, H, C = parts.shape
    tr = _sum_rows(H, C)

    def body(p_ref, o_ref):
        acc = p_ref[0].astype(F32)
        for k in range(1, n):
            acc = acc + p_ref[k].astype(F32)
        o_ref[...] = acc

    return pl.pallas_call(
        body, name=name, grid=(H // tr,),
        in_specs=[pl.BlockSpec((n, tr, C), lambda i: (0, i, 0))],
        out_specs=pl.BlockSpec((tr, C), lambda i: (i, 0)),
        out_shape=jax.ShapeDtypeStruct((H, C), F32),
        compiler_params=_params(("parallel",)),
    )(parts)


def _adamw_math(w, g, m, v):
    m = ADAM_B1 * m + (1.0 - ADAM_B1) * g
    v = ADAM_B2 * v + (1.0 - ADAM_B2) * (g * g)
    m_hat = m / (1.0 - ADAM_B1 ** ADAM_STEP)
    v_hat = v / (1.0 - ADAM_B2 ** ADAM_STEP)
    delta = -ADAM_LR * (m_hat / (jnp.sqrt(v_hat) + ADAM_EPS) + ADAM_WD * w)
    return delta, m, v


def _adamw(name, w, g, m, v):
    R, C = w.shape
    tr = R
    if R % SUBLANES == 0:
        for cand in range(SUBLANES, min(R, 256) + 1, SUBLANES):
            if R % cand == 0:
                tr = cand

    def body(w_ref, g_ref, m_ref, v_ref, d_ref, nm_ref, nv_ref):
        d, nm, nv = _adamw_math(w_ref[...], g_ref[...], m_ref[...], v_ref[...])
        d_ref[...] = d
        nm_ref[...] = nm
        nv_ref[...] = nv

    spec = pl.BlockSpec((tr, C), lambda i: (i, 0))
    shape = jax.ShapeDtypeStruct((R, C), F32)
    return pl.pallas_call(
        body, name=name, grid=(R // tr,), in_specs=[spec] * 4, out_specs=[spec] * 3, out_shape=[shape] * 3,
        compiler_params=_params(("parallel",)),
    )(w, g, m, v)


SMALL_ROW = 2048


def _small_layout(shapes):
    places, row = [], 0
    for R, C in shapes:
        pieces = []
        for r in range(R):
            for c0 in range(0, C, SMALL_ROW):
                pieces.append((r, c0, min(C, c0 + SMALL_ROW), row))
                row += 1
        places.append(pieces)
    return places, -(-row // SUBLANES) * SUBLANES


def _put_rows(block_ref, refs, places):
    block_ref[...] = jnp.zeros_like(block_ref)
    for ref, pieces in zip(refs, places, strict=True):
        for r, c0, c1, row in pieces:
            block_ref[row:row + 1, 0:c1 - c0] = ref[r:r + 1, c0:c1]


def _take_rows(block, refs, places):
    for ref, pieces in zip(refs, places, strict=True):
        for r, c0, c1, row in pieces:
            ref[r:r + 1, c0:c1] = block[row:row + 1, 0:c1 - c0]


def _pack_small(arrs):
    places, rows = _small_layout([a.shape for a in arrs])

    def body(*refs):
        _put_rows(refs[-1], refs[:-1], places)

    return pl.pallas_call(body, name="pack_small", out_shape=jax.ShapeDtypeStruct((rows, SMALL_ROW), F32),
                          compiler_params=_params())(*arrs)


def _adamw_small(parts, ws, ms, vs, extra_shapes):
    n_dev, rows, _ = parts.shape
    n = len(ws)
    places, rows_ = _small_layout([w.shape for w in ws] + list(extra_shapes))
    assert rows_ == rows, (rows_, rows)

    def body(*refs):
        p_ref = refs[0]
        w_refs, m_refs, v_refs = refs[1:1 + n], refs[1 + n:1 + 2 * n], refs[1 + 2 * n:1 + 3 * n]
        outs = refs[1 + 3 * n:-3]
        wb, mb, vb = refs[-3:]
        for block, srcs in ((wb, w_refs), (mb, m_refs), (vb, v_refs)):
            _put_rows(block, srcs, places[:n])
        g = p_ref[0]
        for k in range(1, n_dev):
            g = g + p_ref[k]
        d, nm, nv = _adamw_math(wb[...], g, mb[...], vb[...])
        _take_rows(g, outs[0:n], places[:n])
        _take_rows(d, outs[n:2 * n], places[:n])
        _take_rows(nm, outs[2 * n:3 * n], places[:n])
        _take_rows(nv, outs[3 * n:4 * n], places[:n])
        _take_rows(g, outs[4 * n:], places[n:])

    shapes = [jax.ShapeDtypeStruct(w.shape, F32) for w in ws]
    res = pl.pallas_call(
        body, name="adamw_small", out_shape=shapes * 4 + [jax.ShapeDtypeStruct(s, F32) for s in extra_shapes],
        scratch_shapes=[pltpu.VMEM((rows, SMALL_ROW), F32)] * 3, compiler_params=_params(),
    )(parts, *ws, *ms, *vs)
    return res[0:n], res[n:2 * n], res[2 * n:3 * n], res[3 * n:4 * n], res[4 * n:]


WEIGHTS = ['g_mix', 'w_in', 'rw_mu', 'rw_w0', 'rw_w_up', 'rw_a0', 'rw_a_up', 'rw_g_up', 'rw_k_k', 'rw_k_a',
           'rw_r_k', 'rw_ln_g', 'rw_ln_b', 'w_branch_a', 'w_branch_b', 'w_gate', 'b_gate', 'w_out', 'g_ffn', 'w_up',
           'conv_w', 'conv_b', 'w_down', 'g_ple', 'w_ple_gate', 'w_ple', 'g_final']
ARG_NAMES = (['x', 'p'] + WEIGHTS + ['loss_target'] + ['m_' + n for n in WEIGHTS] + ['v_' + n for n in WEIGHTS])
SHARDED = {'w_in': 1, 'rw_w_up': 1, 'rw_a_up': 1, 'rw_g_up': 1, 'w_branch_a': 1, 'w_branch_b': 1, 'w_gate': 1,
           'w_out': 0, 'w_up': 1, 'conv_w': 1, 'w_down': 0, 'w_ple_gate': 0, 'w_ple': 1}
SMALL = [n for n in WEIGHTS if n not in SHARDED]
WHOLE = ['conv_w']
FIRST_USED = ['w_in', 'rw_w_up', 'rw_a_up', 'rw_g_up', 'w_gate']
READ_BY_CHIP = ['w_gate', 'w_branch_a', 'w_branch_b', 'w_up', 'w_ple']
FIRST_DONE = [['w_up', 'w_down', 'w_ple_gate', 'w_ple'], ['w_out', 'w_branch_a', 'w_branch_b', 'w_gate'],
              ['w_in', 'rw_w_up', 'rw_a_up', 'rw_g_up']]
SPLIT = [n for n in SHARDED if n not in WHOLE]


def _full_from_shards(stack, axis):
    _, R, C = stack.shape
    if axis == 0:
        return stack.reshape(N_CHIPS * R, C)
    return stack.transpose(1, 0, 2).reshape(R, N_CHIPS * C)


def _shards_from_full(full, axis):
    R, C = full.shape
    if axis == 0:
        return full.reshape(N_CHIPS, R // N_CHIPS, C)
    return full.reshape(R, N_CHIPS, C // N_CHIPS).transpose(1, 0, 2)


def kernel(x, p, g_mix, w_in, rw_mu, rw_w0, rw_w_up, rw_a0, rw_a_up, rw_g_up, rw_k_k, rw_k_a, rw_r_k, rw_ln_g, rw_ln_b, w_branch_a, w_branch_b, w_gate, b_gate, w_out, g_ffn, w_up, conv_w, conv_b, w_down, g_ple, w_ple_gate, w_ple, g_final, loss_target, m_g_mix, m_w_in, m_rw_mu, m_rw_w0, m_rw_w_up, m_rw_a0, m_rw_a_up, m_rw_g_up, m_rw_k_k, m_rw_k_a, m_rw_r_k, m_rw_ln_g, m_rw_ln_b, m_w_branch_a, m_w_branch_b, m_w_gate, m_b_gate, m_w_out, m_g_ffn, m_w_up, m_conv_w, m_conv_b, m_w_down, m_g_ple, m_w_ple_gate, m_w_ple, m_g_final, v_g_mix, v_w_in, v_rw_mu, v_rw_w0, v_rw_w_up, v_rw_a0, v_rw_a_up, v_rw_g_up, v_rw_k_k, v_rw_k_a, v_rw_r_k, v_rw_ln_g, v_rw_ln_b, v_w_branch_a, v_w_branch_b, v_w_gate, v_b_gate, v_w_out, v_g_ffn, v_w_up, v_conv_w, v_conv_b, v_w_down, v_g_ple, v_w_ple_gate, v_w_ple, v_g_final):
    given = dict(zip(ARG_NAMES, (x, p, g_mix, w_in, rw_mu, rw_w0, rw_w_up, rw_a0, rw_a_up, rw_g_up, rw_k_k, rw_k_a, rw_r_k, rw_ln_g, rw_ln_b, w_branch_a, w_branch_b, w_gate, b_gate, w_out, g_ffn, w_up, conv_w, conv_b, w_down, g_ple, w_ple_gate, w_ple, g_final, loss_target, m_g_mix, m_w_in, m_rw_mu, m_rw_w0, m_rw_w_up, m_rw_a0, m_rw_a_up, m_rw_g_up, m_rw_k_k, m_rw_k_a, m_rw_r_k, m_rw_ln_g, m_rw_ln_b, m_w_branch_a, m_w_branch_b, m_w_gate, m_b_gate, m_w_out, m_g_ffn, m_w_up, m_conv_w, m_conv_b, m_w_down, m_g_ple, m_w_ple_gate, m_w_ple, m_g_final, v_g_mix, v_w_in, v_rw_mu, v_rw_w0, v_rw_w_up, v_rw_a0, v_rw_a_up, v_rw_g_up, v_rw_k_k, v_rw_k_a, v_rw_r_k, v_rw_ln_g, v_rw_ln_b, v_w_branch_a, v_w_branch_b, v_w_gate, v_b_gate, v_w_out, v_g_ffn, v_w_up, v_conv_w, v_conv_b, v_w_down, v_g_ple, v_w_ple_gate, v_w_ple, v_g_final), strict=True))

    def two_d(name, prefix=""):
        a = given[prefix + name]
        if name == "g_final":
            return a.reshape(1, D_MODEL)
        if name == "rw_r_k":
            return a.reshape(1, RW_WIDTH)
        return a[0] if a.ndim == 3 else a

    cast = lambda n: two_d(n) if n in WHOLE else two_d(n).astype(BF16)
    whole = lambda names, stacks: {n: g if n in READ_BY_CHIP else _full_from_shards(g, SHARDED[n])
                                   for n, g in zip(names, stacks, strict=True)}
    late_names = [n for n in SHARDED if n not in FIRST_USED]
    late_sends, late_recvs, late_srcs, late_lands, token = _travel_start(
        "gather_late_start", "gather", [cast(n) for n in late_names])
    W = whole(FIRST_USED, _gather_chips([cast(n) for n in FIRST_USED]))
    for n in SMALL:
        W[n] = two_d(n)
    W["rw_r_k"] = W["rw_r_k"].reshape(RW_HEADS, RW_HEAD_DIM)
    W["g_mix"] = W["g_mix"] + token[0:1, 0:1]

    def late_weights(after):
        lands = _travel_wait("gather_late_wait", "gather", late_sends, late_recvs, late_srcs, late_lands, after)
        return whole(late_names, _share_halves("share_late", lands))

    early_names = [[n for n in SPLIT if n in group] for group in FIRST_DONE]
    assert sorted(sum(early_names, [])) == sorted(SPLIT)
    travelling = []

    def early_grads(G, stage):
        by_chip = [G[n] if n in READ_BY_CHIP else _shards_from_full(G[n], SHARDED[n]) for n in early_names[stage]]
        sends, recvs, srcs, lands, started = _travel_start(f"scatter{stage}_start", "scatter", by_chip)
        travelling.append((sends, recvs, srcs, lands))
        return started

    loss_part, grad_x, G = _local_step(x[0], p[0, 0], W, loss_target[0], late_weights, early_grads, by_chip=True,
                                       grad_dtype=BF16)

    landed = {}
    for stage, (sends, recvs, srcs, lands) in enumerate(travelling):
        landed.update(zip(early_names[stage], _travel_wait(f"scatter{stage}_wait", "scatter", sends, recvs, srcs,
                                                           lands, grad_x), strict=True))
    reduced = [_sum_devices("sum_devices_" + n, landed[n]) for n in SPLIT]
    shard_grads = dict(zip(SPLIT, _join_halves(reduced), strict=True))

    G["rw_r_k"] = G["rw_r_k"].reshape(1, RW_WIDTH)
    extras = [G[n] for n in WHOLE] + [loss_part]
    all_small = _gather_all(_pack_small([G[n] for n in SMALL] + extras))
    gs, ds, nms, nvs, summed = _adamw_small(all_small, [two_d(n) for n in SMALL], [two_d(n, "m_") for n in SMALL],
                                            [two_d(n, "v_") for n in SMALL], [e.shape for e in extras])
    loss = summed[-1][0, 0]
    chip = 2 * lax.axis_index("x") + lax.axis_index("y")
    for n, full in zip(WHOLE, summed[:-1], strict=True):
        width = two_d(n).shape[1]
        shard_grads[n] = lax.dynamic_slice_in_dim(full, chip * width, width, axis=1)

    grads, deltas, new_m, new_v = {}, {}, {}, {}
    for n in SHARDED:
        g = shard_grads[n]
        d, nm, nv = _adamw("adamw_" + n, two_d(n), g, two_d(n, "m_"), two_d(n, "v_"))
        grads[n], deltas[n], new_m[n], new_v[n] = g, d, nm, nv
    for i, n in enumerate(SMALL):
        grads[n], deltas[n], new_m[n], new_v[n] = gs[i], ds[i], nms[i], nvs[i]
    outs = [loss, grad_x[None]]
    for table in (grads, deltas, new_m, new_v):
        outs += [table[n].reshape(given[n].shape) for n in WEIGHTS]
    return tuple(outs)
```

```python
import functools
import math

import jax
import jax.numpy as jnp
import numpy as np
from jax import lax
from jax.experimental import pallas as pl
from jax.experimental.pallas import tpu as pltpu

F32 = jnp.float32
BF16 = jnp.bfloat16

D_MODEL = 1024
NORM_EPS = 1e-6
RW_HEADS = 8
RW_HEAD_DIM = 64
RW_WIDTH = 512
RW_LN_EPS = 64e-5
ATT_GROUP_DILATION = (1, 4, 16)
ATT_BLOCK = 128
ATT_HEADS = 12
ATT_HEAD_DIM = 64
ATT_GROUP_WIDTH = 256
ATT_WIDTH = 768
D_FF = 3072

ADAM_LR = 0.001
ADAM_B1 = 0.9
ADAM_B2 = 0.999
ADAM_EPS = 1e-08
ADAM_WD = 0.01
ADAM_STEP = 10

SUBLANES = 8
LANES = 128
VMEM_LIMIT = 56 * 1024 * 1024
N_CHIPS = 4
N_DEV = 8
MESH = pl.DeviceIdType.MESH


def _params(sem=None):
    return pltpu.CompilerParams(dimension_semantics=sem, vmem_limit_bytes=VMEM_LIMIT)


def _pick(dim, pref):
    if dim % LANES != 0 or dim <= pref:
        return dim
    best = LANES
    for t in range(LANES, pref + 1, LANES):
        if dim % t == 0:
            best = t
    return best


def _mm(name, a, b, mode, out_dtype=F32, add=None, tm=1024, tn=1024, tk=1024, out_by_chip=False, post=None):
    by_chip = b.ndim == 3
    b_rows, b_cols = (b.shape[1], N_CHIPS * b.shape[2]) if by_chip else b.shape
    if mode == "nn":
        (M, K), (K2, N) = a.shape, (b_rows, b_cols)
    elif mode == "nt":
        (M, K), (N, K2) = a.shape, (b_rows, b_cols)
    else:
        (K, M), (K2, N) = a.shape, (b_rows, b_cols)
    assert K == K2, (name, a.shape, b.shape, mode)
    assert not (by_chip and mode == "tn") and not (out_by_chip and add is not None), name
    tm = _pick(M, tm)
    n_cut, k_cut = out_by_chip or (by_chip and mode == "nn"), by_chip and mode == "nt"
    tn = _pick(N // N_CHIPS, tn) if n_cut else _pick(N, tn)
    tk = _pick(K // N_CHIPS, tk) if k_cut else _pick(K, tk)
    nk = K // tk
    per_n = (N // N_CHIPS) // tn if n_cut else 1
    per_k = (K // N_CHIPS) // tk if k_cut else 1
    if mode == "nn":
        a_spec = pl.BlockSpec((tm, tk), lambda i, j, k: (i, k))
        b_spec = (pl.BlockSpec((None, tk, tn), lambda i, j, k: (j // per_n, k, j % per_n)) if by_chip
                  else pl.BlockSpec((tk, tn), lambda i, j, k: (k, j)))
        dims = (((1,), (0,)), ((), ()))
    elif mode == "nt":
        a_spec = pl.BlockSpec((tm, tk), lambda i, j, k: (i, k))
        b_spec = (pl.BlockSpec((None, tn, tk), lambda i, j, k: (k // per_k, j, k % per_k)) if by_chip
                  else pl.BlockSpec((tn, tk), lambda i, j, k: (j, k)))
        dims = (((1,), (1,)), ((), ()))
    else:
        a_spec = pl.BlockSpec((tk, tm), lambda i, j, k: (k, i))
        b_spec = pl.BlockSpec((tk, tn), lambda i, j, k: (k, j))
        dims = (((0,), (0,)), ((), ()))
    if out_by_chip:
        o_spec = pl.BlockSpec((None, tm, tn), lambda i, j, k: (j // per_n, i, j % per_n))
        out_shape = jax.ShapeDtypeStruct((N_CHIPS, M, N // N_CHIPS), out_dtype)
    else:
        o_spec = pl.BlockSpec((tm, tn), lambda i, j, k: (i, j))
        out_shape = jax.ShapeDtypeStruct((M, N), out_dtype)
    has_add = add is not None
    ins = [a, b] + ([add] if has_add else [])
    in_specs = [a_spec, b_spec] + ([o_spec] if has_add else [])
    n_main = len(ins)
    semantics = ("parallel", "parallel", "arbitrary")
    if post is not None:
        post_fn, post_rows, post_consts, post_outs = post
        assert tn == N and not out_by_chip, name
        ins += list(post_rows) + list(post_consts)
        in_specs += [pl.BlockSpec((tm, r.shape[1]), lambda i, j, k: (i, 0)) for r in post_rows]
        in_specs += [pl.BlockSpec(c.shape, lambda i, j, k, nd=c.ndim: (0,) * nd) for c in post_consts]
        o_spec = [pl.BlockSpec((tm, o[1]), lambda i, j, k: (i, 0)) if o[0] == "row"
                  else pl.BlockSpec(o[1], lambda i, j, k: (0, 0)) for o in post_outs]
        out_shape = [jax.ShapeDtypeStruct((M, o[1]), o[2]) if o[0] == "row" else jax.ShapeDtypeStruct(o[1], F32)
                     for o in post_outs]
        if any(o[0] == "acc" for o in post_outs):
            semantics = ("arbitrary", "arbitrary", "arbitrary")
    n_in = len(ins)

    def body(*refs):
        a_ref, b_ref = refs[:2]
        out_refs, acc_ref = refs[n_in:-1], refs[-1]
        i, k = pl.program_id(0), pl.program_id(2)
        part = lax.dot_general(a_ref[...].astype(BF16), b_ref[...].astype(BF16), dims,
                               preferred_element_type=F32)

        @pl.when(k == 0)
        def _():
            acc_ref[...] = part

        @pl.when(k > 0)
        def _():
            acc_ref[...] += part

        @pl.when(k == nk - 1)
        def _():
            res = acc_ref[...]
            if has_add:
                res = res + refs[2][...].astype(F32)
            if post is None:
                out_refs[0][...] = res.astype(out_refs[0].dtype)
                return
            n_rows = len(post_rows)
            vals = post_fn(res, [r[...] for r in refs[n_main:n_main + n_rows]],
                           [c[...] for c in refs[n_main + n_rows:n_in]])
            for o, o_ref, val in zip(post_outs, out_refs, vals, strict=True):
                if o[0] == "row":
                    o_ref[...] = val.astype(o_ref.dtype)
                else:
                    @pl.when(i == 0)
                    def _(o_ref=o_ref, val=val):
                        o_ref[...] = val.astype(F32)

                    @pl.when(i > 0)
                    def _(o_ref=o_ref, val=val):
                        o_ref[...] += val.astype(F32)

    return pl.pallas_call(
        body, name=name, grid=(M // tm, N // tn, nk),
        in_specs=in_specs, out_specs=o_spec, out_shape=out_shape,
        scratch_shapes=[pltpu.VMEM((tm, tn), F32)],
        compiler_params=_params(semantics),
    )(*ins)


def _rowwise(name, fn, T, tT, rows=(), prevs=(), nexts=(), consts=(), outs=()):
    n = T // tT
    per8 = tT // SUBLANES
    in_specs, ins = [], []
    for arr in rows:
        in_specs.append(pl.BlockSpec((tT, arr.shape[1]), lambda i: (i, 0)))
        ins.append(arr)
    for arr in prevs:
        in_specs.append(pl.BlockSpec((SUBLANES, arr.shape[1]), lambda i: (jnp.maximum(i * per8 - 1, 0), 0)))
        ins.append(arr)
    for arr in nexts:
        in_specs.append(pl.BlockSpec((SUBLANES, arr.shape[1]),
                                     lambda i: (jnp.minimum((i + 1) * per8, T // SUBLANES - 1), 0)))
        ins.append(arr)
    for arr in consts:
        in_specs.append(pl.BlockSpec(arr.shape, lambda i, nd=arr.ndim: (0,) * nd))
        ins.append(arr)
    out_specs, out_shapes = [], []
    for o in outs:
        if o[0] == "row":
            out_specs.append(pl.BlockSpec((tT, o[1]), lambda i: (i, 0)))
            out_shapes.append(jax.ShapeDtypeStruct((T, o[1]), o[2]))
        else:
            out_specs.append(pl.BlockSpec(o[1], lambda i: (0, 0)))
            out_shapes.append(jax.ShapeDtypeStruct(o[1], F32))
    nr, npv, nnx, nc = len(rows), len(prevs), len(nexts), len(consts)
    n_in = nr + npv + nnx + nc

    def body(*refs):
        i = pl.program_id(0)
        vals = [r[...] for r in refs[:n_in]]
        res = fn(i, n, vals[:nr], vals[nr:nr + npv], vals[nr + npv:nr + npv + nnx], vals[nr + npv + nnx:])
        for o, o_ref, val in zip(outs, refs[n_in:], res, strict=True):
            if o[0] == "row":
                o_ref[...] = val.astype(o_ref.dtype)
            else:
                @pl.when(i == 0)
                def _(o_ref=o_ref, val=val):
                    o_ref[...] = val.astype(F32)

                @pl.when(i > 0)
                def _(o_ref=o_ref, val=val):
                    o_ref[...] += val.astype(F32)

    res = pl.pallas_call(
        body, name=name, grid=(n,), in_specs=in_specs, out_specs=out_specs, out_shape=out_shapes,
        compiler_params=_params(("arbitrary",)),
    )(*ins)
    return list(res)


def _shift_down(x, prev8, i, s):
    rolled = pltpu.roll(x, s, 0)
    head = pltpu.roll(prev8, s, 0)
    head = jnp.where(i == 0, jnp.zeros_like(head), head)
    rid = lax.broadcasted_iota(jnp.int32, head.shape, 0)
    first = jnp.where(rid < s, head, rolled[:SUBLANES])
    if x.shape[0] == SUBLANES:
        return first
    return jnp.concatenate([first, rolled[SUBLANES:]], axis=0)


def _shift_up(x, next8, i, n, s):
    tT = x.shape[0]
    rolled = pltpu.roll(x, tT - s, 0)
    tail = pltpu.roll(next8, SUBLANES - s, 0)
    tail = jnp.where(i == n - 1, jnp.zeros_like(tail), tail)
    rid = lax.broadcasted_iota(jnp.int32, tail.shape, 0)
    last = jnp.where(rid >= SUBLANES - s, tail, rolled[tT - SUBLANES:])
    return jnp.concatenate([rolled[:tT - SUBLANES], last], axis=0)


def _colsum(x):
    return jnp.sum(x, axis=0, keepdims=True)


def _segsum(x, bd):
    return jnp.dot(x, bd, precision=lax.Precision.HIGH, preferred_element_type=F32)


def _block_diag_ones(width, seg):
    idx = np.arange(width) // seg
    return jnp.asarray((idx[:, None] == idx[None, :]).astype(np.float32))


def _sigmoid(z):
    return 1.0 / (1.0 + jnp.exp(-z))


def _softplus(z):
    return jnp.maximum(z, 0.0) + jnp.log(1.0 + jnp.exp(-jnp.abs(z)))


def _rms_fwd(x, g):
    r = lax.rsqrt(jnp.mean(x * x, axis=-1, keepdims=True) + NORM_EPS)
    return x * r * g


def _rms_bwd(x, g, dy):
    r = lax.rsqrt(jnp.mean(x * x, axis=-1, keepdims=True) + NORM_EPS)
    gdy = dy * g
    dx = r * (gdy - x * (r * r) * jnp.mean(x * gdy, axis=-1, keepdims=True))
    return dx, dy * x * r


GELU_C = math.sqrt(2.0 / math.pi)


def _gelu(x):
    return 0.5 * x * (1.0 + jnp.tanh(GELU_C * (x + 0.044715 * x * x * x)))


def _gelu_and_grad(x):
    th = jnp.tanh(GELU_C * (x + 0.044715 * x * x * x))
    half = 0.5 * (1.0 + th)
    return x * half, half + 0.5 * x * (1.0 - th * th) * GELU_C * (1.0 + 3.0 * 0.044715 * x * x)


RW_CHUNK = 64
NN = (((1,), (0,)), ((), ()))
NT = (((1,), (1,)), ((), ()))
TN = (((0,), (0,)), ((), ()))


def _hdot(a, b, dims):
    return lax.dot_general(a, b, dims, precision=lax.Precision.HIGH, preferred_element_type=F32)


def _ldot(a, b, dims):
    return lax.dot_general(a.astype(BF16), b.astype(BF16), dims, preferred_element_type=F32)


def _chunk_masks():
    ti = lax.broadcasted_iota(jnp.int32, (RW_CHUNK, RW_CHUNK), 0)
    tj = lax.broadcasted_iota(jnp.int32, (RW_CHUNK, RW_CHUNK), 1)
    return tj <= ti, tj < ti, (ti == tj).astype(F32)


def _head(x, h):
    return x[:, h * RW_HEAD_DIM:(h + 1) * RW_HEAD_DIM]


def _heads(fn):
    return [fn(h) for h in range(RW_HEADS)]


def _chunk_rows(r, lw, k, a, b, incl_f):
    c = _hdot(incl_f, lw, NN)
    e_prev, e_neg, e_pos = jnp.exp(c - lw), jnp.exp(-c), jnp.exp(c)
    return dict(At=a * e_prev, Bt=b * e_neg, Kt=k * e_neg, Rt=r * e_pos, e_prev=e_prev, e_neg=e_neg, e_pos=e_pos)


def _stack(top, bottom, h):
    return jnp.concatenate([_head(top, h), _head(bottom, h)], axis=0)


def _chunk_coeffs(q, incl, strict):
    C = RW_CHUNK
    ar = _heads(lambda h: _stack(q["At"], q["Rt"], h))
    pb = _heads(lambda h: _hdot(ar[h], _head(q["Bt"], h), NT))
    pk = _heads(lambda h: _hdot(ar[h], _head(q["Kt"], h), NT))
    A1, W1 = [jnp.where(strict, m[:C], 0.0) for m in pb], [jnp.where(incl, m[C:], 0.0) for m in pb]
    A2, W2 = [jnp.where(strict, m[:C], 0.0) for m in pk], [jnp.where(incl, m[C:], 0.0) for m in pk]
    return A1, A2, W1, W2


def _rwkv_chunk_prep(r, lw, k, a, b, v):
    T = r.shape[0]
    nC = T // RW_CHUNK
    H, N = RW_HEADS, RW_HEAD_DIM

    def body(r_ref, lw_ref, k_ref, a_ref, b_ref, v_ref,
             at_ref, bt_ref, kt_ref, rt_ref, a2v_ref, w2v_ref, ti_ref, w1_ref, a2_ref, w2_ref, pl_ref):
        incl, strict, eye = _chunk_masks()
        q = _chunk_rows(r_ref[...], lw_ref[...], k_ref[...], a_ref[...], b_ref[...], incl.astype(F32))
        at_ref[...], bt_ref[...], kt_ref[...], rt_ref[...] = q["At"], q["Bt"], q["Kt"], q["Rt"]
        pl_ref[0] = jnp.broadcast_to(q["e_pos"][RW_CHUNK - 1:RW_CHUNK, :], (SUBLANES, RW_WIDTH))
        A1, A2, W1, W2 = _chunk_coeffs(q, incl, strict)
        V = v_ref[...]
        a2v_ref[...] = jnp.concatenate(_heads(lambda h: _hdot(A2[h], _head(V, h), NN)), axis=1)
        w2v_ref[...] = jnp.concatenate(_heads(lambda h: _ldot(W2[h], _head(V, h), NN)), axis=1)
        tinv, pw = [eye + m for m in A1], A1
        for stage in range(5):
            dot = _hdot if stage == 0 else _ldot
            pw = [dot(m, m, NN) for m in pw]
            tinv = [t + dot(t, m, NN) for t, m in zip(tinv, pw, strict=True)]
        for h in range(H):
            ti_ref[0, h] = tinv[h]
            w1_ref[0, h] = W1[h]
            a2_ref[0, h] = A2[h]
            w2_ref[0, h] = W2[h]

    row_spec = pl.BlockSpec((RW_CHUNK, RW_WIDTH), lambda n: (n, 0))
    st_spec = pl.BlockSpec((1, H, N, N), lambda n: (n, 0, 0, 0))
    row_shape = jax.ShapeDtypeStruct((T, RW_WIDTH), F32)
    st_shape = jax.ShapeDtypeStruct((nC, H, N, N), F32)
    return pl.pallas_call(
        body, name="rwkv_chunk_prep", grid=(nC,),
        in_specs=[row_spec] * 6,
        out_specs=[row_spec] * 6 + [st_spec] * 4 + [pl.BlockSpec((1, SUBLANES, RW_WIDTH), lambda n: (n, 0, 0))],
        out_shape=[row_shape] * 6 + [st_shape] * 4 + [jax.ShapeDtypeStruct((nC, SUBLANES, RW_WIDTH), F32)],
        compiler_params=_params(("parallel",)),
    )(r, lw, k, a, b, v)


def _rwkv_chunk_fwd(v, at, bt, kt, rt, a2v, w2v, tinv, w1, plast):
    T = v.shape[0]
    nC = T // RW_CHUNK
    H, N = RW_HEADS, RW_HEAD_DIM

    def body(v_ref, at_ref, bt_ref, kt_ref, rt_ref, a2v_ref, w2v_ref, ti_ref, w1_ref, pl_ref,
             y_ref, sa_ref, s0_ref, S_ref):
        @pl.when(pl.program_id(0) == 0)
        def _():
            S_ref[...] = jnp.zeros_like(S_ref)

        V, At, Bt, Kt, Rt = v_ref[...], at_ref[...], bt_ref[...], kt_ref[...], rt_ref[...]
        A2V, W2V, p_last = a2v_ref[...], w2v_ref[...], pl_ref[0, 0:1, :]
        S0 = _heads(lambda h: S_ref[h])
        for h in range(H):
            s0_ref[0, h] = S0[h]
        C = RW_CHUNK
        on_state = _heads(lambda h: _hdot(_stack(At, Rt, h), S0[h], NT))
        Sa = _heads(lambda h: _hdot(ti_ref[0, h], on_state[h][:C] + _head(A2V, h), NN))
        X = _heads(lambda h: S0[h] + _hdot(jnp.concatenate([Sa[h], _head(V, h)], axis=0), _stack(Bt, Kt, h), TN))
        for h in range(H):
            S_ref[h] = X[h] * _head(p_last, h)
        Y = _heads(lambda h: on_state[h][C:] + _ldot(w1_ref[0, h], Sa[h], NN) + _head(W2V, h))
        y_ref[...] = jnp.concatenate(Y, axis=1)
        sa_ref[...] = jnp.concatenate(Sa, axis=1)

    row_spec = pl.BlockSpec((RW_CHUNK, RW_WIDTH), lambda n: (n, 0))
    st_spec = pl.BlockSpec((1, H, N, N), lambda n: (n, 0, 0, 0))
    row_shape = jax.ShapeDtypeStruct((T, RW_WIDTH), F32)
    return pl.pallas_call(
        body, name="rwkv_chunk_fwd", grid=(nC,),
        in_specs=[row_spec] * 7 + [st_spec, st_spec, pl.BlockSpec((1, SUBLANES, RW_WIDTH), lambda n: (n, 0, 0))],
        out_specs=[row_spec, row_spec, st_spec],
        out_shape=[row_shape, row_shape, jax.ShapeDtypeStruct((nC, H, N, N), F32)],
        scratch_shapes=[pltpu.VMEM((H, N, N), F32)],
        compiler_params=_params(("arbitrary",)),
    )(v, at, bt, kt, rt, a2v, w2v, tinv, w1, plast)


def _rwkv_chunk_bwd(r, lw, k, a, b, v, dy, s0, tinv, w1, a2, w2, sa):
    T = r.shape[0]
    nC = T // RW_CHUNK
    H, N = RW_HEADS, RW_HEAD_DIM

    def body(r_ref, lw_ref, k_ref, a_ref, b_ref, v_ref, dy_ref, s0_ref, ti_ref, w1_ref, a2_ref, w2_ref, sa_ref,
             dr_ref, dlw_ref, dk_ref, da_ref, db_ref, dv_ref, dS_ref):
        @pl.when(pl.program_id(0) == 0)
        def _():
            dS_ref[...] = jnp.zeros_like(dS_ref)

        incl, strict, _ = _chunk_masks()
        incl_f = incl.astype(F32)
        q = _chunk_rows(r_ref[...], lw_ref[...], k_ref[...], a_ref[...], b_ref[...], incl_f)
        At, Bt, Kt, Rt = q["At"], q["Bt"], q["Kt"], q["Rt"]
        A2, W1, W2 = (_heads(lambda h, ref=ref: ref[0, h]) for ref in (a2_ref, w1_ref, w2_ref))
        V, dY, Sa = v_ref[...], dy_ref[...], sa_ref[...]
        hd = _head
        p_last = q["e_pos"][RW_CHUNK - 1:RW_CHUNK, :]
        S0 = _heads(lambda h: s0_ref[0, h])
        G = _heads(lambda h: dS_ref[h] * hd(p_last, h))
        C = RW_CHUNK
        AR = _heads(lambda h: _stack(At, Rt, h))
        BK = _heads(lambda h: _stack(Bt, Kt, h))
        X = _heads(lambda h: S0[h] + _ldot(_stack(Sa, V, h), BK[h], TN))
        dc_last = jnp.concatenate(_heads(lambda h: jnp.sum(G[h] * X[h], axis=0, keepdims=True)), axis=1)
        dSa = _heads(lambda h: _ldot(hd(Bt, h), G[h], NT) + _ldot(W1[h], hd(dY, h), TN))
        dZ = _heads(lambda h: _ldot(ti_ref[0, h], dSa[h], TN))
        D = _heads(lambda h: jnp.concatenate([dZ[h], hd(dY, h)], axis=0))
        for h in range(H):
            dS_ref[h] = G[h] + _ldot(D[h], AR[h], TN)
        both = jnp.concatenate([strict, incl], axis=0)
        E1 = _heads(lambda h: jnp.where(both, _ldot(D[h], hd(Sa, h), NT), 0.0))
        E2 = _heads(lambda h: jnp.where(both, _ldot(D[h], hd(V, h), NT), 0.0))
        cat = lambda fn: jnp.concatenate(_heads(fn), axis=1)
        dV = cat(lambda h: _ldot(jnp.concatenate([A2[h], W2[h]], axis=0), D[h], TN) + _ldot(hd(Kt, h), G[h], NT))
        dAR = _heads(lambda h: _ldot(E1[h], hd(Bt, h), NN) + _ldot(E2[h], hd(Kt, h), NN) + _ldot(D[h], S0[h], NN))
        dAt, dRt = cat(lambda h: dAR[h][:C]), cat(lambda h: dAR[h][C:])
        dBt = cat(lambda h: _ldot(E1[h], AR[h], TN) + _ldot(hd(Sa, h), G[h], NN))
        dKt = cat(lambda h: _ldot(E2[h], AR[h], TN) + _ldot(hd(V, h), G[h], NN))
        last_row = lax.broadcasted_iota(jnp.int32, (RW_CHUNK, RW_WIDTH), 0) == RW_CHUNK - 1
        dc_prev = dAt * At
        dc = dc_prev + dRt * Rt - dBt * Bt - dKt * Kt + jnp.where(last_row, dc_last, 0.0)
        dr_ref[...] = dRt * q["e_pos"]
        dlw_ref[...] = _hdot(incl_f, dc, TN) - dc_prev
        dk_ref[...] = dKt * q["e_neg"]
        da_ref[...] = dAt * q["e_prev"]
        db_ref[...] = dBt * q["e_neg"]
        dv_ref[...] = dV

    rev = lambda n: nC - 1 - n
    row_spec = pl.BlockSpec((RW_CHUNK, RW_WIDTH), lambda n: (rev(n), 0))
    st_spec = pl.BlockSpec((1, H, N, N), lambda n: (rev(n), 0, 0, 0))
    row_shape = jax.ShapeDtypeStruct((T, RW_WIDTH), F32)
    return pl.pallas_call(
        body, name="rwkv_chunk_bwd", grid=(nC,),
        in_specs=[row_spec] * 7 + [st_spec] * 5 + [row_spec], out_specs=[row_spec] * 6,
        out_shape=[row_shape] * 6, scratch_shapes=[pltpu.VMEM((H, N, N), F32)],
        compiler_params=_params(("arbitrary",)),
    )(r, lw, k, a, b, v, dy, s0, tinv, w1, a2, w2, sa)


def _alibi_slope(head):
    return float(np.float32(2.0 ** (-8.0 * (head + 1) / ATT_HEADS)))


ATT_SPAN = ATT_BLOCK * max(ATT_GROUP_DILATION)
ATT_PAIR_WIDTH = 2 * ATT_HEAD_DIM
ATT_SIDE_BY_SIDE = 16


def _pair_slope(g, hp, j):
    return jnp.where(hp == 0, _alibi_slope(4 * g + j), _alibi_slope(4 * g + 2 + j))


def _att_rows(mi, r, d):
    start = mi * ATT_BLOCK * d + r
    return pl.ds(start, ATT_BLOCK) if d == 1 else pl.ds(start, ATT_BLOCK, stride=d)


def _att_masks():
    qi = lax.broadcasted_iota(jnp.int32, (ATT_BLOCK, ATT_BLOCK), 0)
    kj = lax.broadcasted_iota(jnp.int32, (ATT_BLOCK, ATT_BLOCK), 1)
    return qi, kj


NEG = -1e30


def _att_logits(q, k, slope_d, steps, valid):
    s = lax.dot_general(q.astype(BF16), k.astype(BF16), (((1,), (1,)), ((), ())),
                        preferred_element_type=F32) * (ATT_HEAD_DIM ** -0.5)
    return jnp.where(valid, s - slope_d * steps.astype(F32), NEG)


def _att_fwd(p_att, g):
    T = p_att.shape[0]
    d = ATT_GROUP_DILATION[g]
    W = ATT_PAIR_WIDTH
    nb = T // ATT_SPAN
    mb = ATT_SPAN // (ATT_BLOCK * d)

    def body(q_ref, kc_ref, kp_ref, vc_ref, vp_ref, o_ref, l_ref):
        hp, n = pl.program_id(0), pl.program_id(1)
        qi, kj = _att_masks()
        slopes = [_pair_slope(g, hp, j) * d for j in range(2)]
        blocks = [(r, mi) for r in range(d) for mi in range(mb)]
        for at in range(0, len(blocks), ATT_SIDE_BY_SIDE):
            tasks = []
            for r, mi in blocks[at:at + ATT_SIDE_BY_SIDE]:
                rows = _att_rows(mi, r, d)
                if mi > 0:
                    prev = _att_rows(mi - 1, r, d)
                    kp, vp, has_prev = kc_ref[prev, :], vc_ref[prev, :], True
                else:
                    prev = _att_rows(mb - 1, r, d)
                    kp, vp, has_prev = kp_ref[prev, :], vp_ref[prev, :], n > 0
                q, kc, vc = q_ref[rows, :], kc_ref[rows, :], vc_ref[rows, :]
                for j in range(2):
                    sl = slice(j * ATT_HEAD_DIM, (j + 1) * ATT_HEAD_DIM)
                    tasks.append((q[:, sl], kc[:, sl], kp[:, sl], vc[:, sl], vp[:, sl], has_prev, slopes[j]))
            lc = [_att_logits(t[0], t[1], t[6], qi - kj, kj <= qi) for t in tasks]
            lp = [_att_logits(t[0], t[2], t[6], qi - kj + ATT_BLOCK, (kj >= qi) & t[5]) for t in tasks]
            mx = [jnp.maximum(jnp.max(a, axis=1, keepdims=True), jnp.max(b, axis=1, keepdims=True))
                  for a, b in zip(lc, lp, strict=True)]
            ec = [jnp.exp(a - m) for a, m in zip(lc, mx, strict=True)]
            ep = [jnp.exp(b - m) for b, m in zip(lp, mx, strict=True)]
            den = [jnp.sum(a, axis=1, keepdims=True) + jnp.sum(b, axis=1, keepdims=True)
                   for a, b in zip(ec, ep, strict=True)]
            inv = [1.0 / s for s in den]
            outs = [jnp.dot((a * i).astype(BF16), t[3].astype(BF16), preferred_element_type=F32)
                    + jnp.dot((b * i).astype(BF16), t[4].astype(BF16), preferred_element_type=F32)
                    for a, b, i, t in zip(ec, ep, inv, tasks, strict=True)]
            lses = [jnp.broadcast_to(m + jnp.log(s), (ATT_BLOCK, ATT_HEAD_DIM)) for m, s in zip(mx, den, strict=True)]
            for i, (r, mi) in enumerate(blocks[at:at + ATT_SIDE_BY_SIDE]):
                rows = _att_rows(mi, r, d)
                o_ref[rows, :] = jnp.concatenate(outs[2 * i:2 * i + 2], axis=1)
                l_ref[rows, :] = jnp.concatenate(lses[2 * i:2 * i + 2], axis=1)

    def spec(col0, prev):
        if prev:
            return pl.BlockSpec((ATT_SPAN, W), lambda hp, n: (jnp.maximum(n - 1, 0), col0 + 2 * g + hp))
        return pl.BlockSpec((ATT_SPAN, W), lambda hp, n: (n, col0 + 2 * g + hp))

    o_spec = pl.BlockSpec((ATT_SPAN, W), lambda hp, n: (n, hp))
    o, l = pl.pallas_call(
        body, name=f"att_fwd_g{g}", grid=(2, nb),
        in_specs=[spec(0, False), spec(6, False), spec(6, True), spec(12, False), spec(12, True)],
        out_specs=[o_spec, o_spec],
        out_shape=[jax.ShapeDtypeStruct((T, ATT_GROUP_WIDTH), F32)] * 2,
        compiler_params=_params(("parallel", "arbitrary")),
    )(p_att, p_att, p_att, p_att, p_att)
    return o, l


def _att_bwd(p_att, o, l, do, dl, g):
    T = p_att.shape[0]
    d = ATT_GROUP_DILATION[g]
    W = ATT_PAIR_WIDTH
    nb = T // ATT_SPAN
    mb = ATT_SPAN // (ATT_BLOCK * d)
    scale = ATT_HEAD_DIM ** -0.5

    def body(q_ref, k_ref, v_ref, o_ref, l_ref, do_ref, dl_ref,
             qn_ref, on_ref, ln_ref, don_ref, dln_ref, dq_ref, dk_ref, dv_ref, carry_ref):
        hp, n = pl.program_id(0), pl.program_id(1)
        qi, kj = _att_masks()

        @pl.when(n == 0)
        def _():
            carry_ref[...] = jnp.zeros_like(carry_ref)

        slopes = [_pair_slope(g, hp, j) * d for j in range(2)]
        blocks = [(r, mi) for r in range(d) for mi in range(mb)]
        side_by_side = ATT_SIDE_BY_SIDE // 2
        carry = None
        for at in range(0, len(blocks), side_by_side):
            tasks = []
            for r, mi in blocks[at:at + side_by_side]:
                rows = _att_rows(mi, r, d)
                if mi < mb - 1:
                    nrows = _att_rows(mi + 1, r, d)
                    nxt = (q_ref[nrows, :], o_ref[nrows, :], l_ref[nrows, :], do_ref[nrows, :], dl_ref[nrows, :])
                    has_next = True
                else:
                    nrows = _att_rows(0, r, d)
                    nxt = (qn_ref[nrows, :], on_ref[nrows, :], ln_ref[nrows, :], don_ref[nrows, :],
                           dln_ref[nrows, :])
                    has_next = n < nb - 1
                cur = (q_ref[rows, :], o_ref[rows, :], l_ref[rows, :], do_ref[rows, :], dl_ref[rows, :])
                k_all, v_all = k_ref[rows, :], v_ref[rows, :]
                for j in range(2):
                    sl = slice(j * ATT_HEAD_DIM, (j + 1) * ATT_HEAD_DIM)
                    for blk, steps, valid in ((cur, qi - kj, kj <= qi),
                                              (nxt, qi - kj + ATT_BLOCK, (kj >= qi) & has_next)):
                        q, o_, lse, do_, dlse = (z[:, sl] for z in blk)
                        tasks.append(dict(q=q, o=o_, lse=lse[:, :1], do=do_, dlse=dlse[:, :1], steps=steps,
                                          valid=valid, k=k_all[:, sl], vb=v_all[:, sl].astype(BF16),
                                          slope=slopes[j]))
            p = [jnp.exp(_att_logits(t["q"], t["k"], t["slope"], t["steps"], t["valid"]) - t["lse"]) for t in tasks]
            dp = [lax.dot_general(t["do"].astype(BF16), t["vb"], (((1,), (1,)), ((), ())),
                                  preferred_element_type=F32) for t in tasks]
            dsum = [jnp.sum(t["do"] * t["o"], axis=1, keepdims=True) for t in tasks]
            ds = [a * (b - s + t["dlse"]) for a, b, s, t in zip(p, dp, dsum, tasks, strict=True)]
            dv_ = [jnp.dot(a.T.astype(BF16), t["do"].astype(BF16), preferred_element_type=F32)
                   for a, t in zip(p, tasks, strict=True)]
            dk_ = [jnp.dot(a.T.astype(BF16), t["q"].astype(BF16), preferred_element_type=F32) * scale
                   for a, t in zip(ds, tasks, strict=True)]
            dq_ = [jnp.dot(a.astype(BF16), t["k"].astype(BF16), preferred_element_type=F32) * scale
                   for a, t in zip(ds, tasks, strict=True)]
            for i, (r, mi) in enumerate(blocks[at:at + side_by_side]):
                rows = _att_rows(mi, r, d)
                b = 4 * i
                if mi == 0:
                    carry = carry_ref[r]
                dq_ref[rows, :] = jnp.concatenate([dq_[b], dq_[b + 2]], axis=1) + carry
                carry = jnp.concatenate([dq_[b + 1], dq_[b + 3]], axis=1)
                if mi == mb - 1:
                    carry_ref[r] = carry
                dk_ref[rows, :] = jnp.concatenate([dk_[b] + dk_[b + 1], dk_[b + 2] + dk_[b + 3]], axis=1)
                dv_ref[rows, :] = jnp.concatenate([dv_[b] + dv_[b + 1], dv_[b + 2] + dv_[b + 3]], axis=1)

    head_rows = ATT_BLOCK * d
    nxt_n = lambda n: jnp.minimum((n + 1) * mb, T // head_rows - 1)
    cur_p = lambda col0: pl.BlockSpec((ATT_SPAN, W), lambda hp, n: (n, col0 + 2 * g + hp))
    cur_o = pl.BlockSpec((ATT_SPAN, W), lambda hp, n: (n, hp))
    nxt_o = pl.BlockSpec((head_rows, W), lambda hp, n: (nxt_n(n), hp))
    dq, dk, dv = pl.pallas_call(
        body, name=f"att_bwd_g{g}", grid=(2, nb),
        in_specs=[cur_p(0), cur_p(6), cur_p(12), cur_o, cur_o, cur_o, cur_o,
                  pl.BlockSpec((head_rows, W), lambda hp, n: (nxt_n(n), 2 * g + hp)), nxt_o, nxt_o, nxt_o, nxt_o],
        out_specs=[cur_o, cur_o, cur_o],
        out_shape=[jax.ShapeDtypeStruct((T, ATT_GROUP_WIDTH), F32)] * 3,
        scratch_shapes=[pltpu.VMEM((d, ATT_BLOCK, W), F32)],
        compiler_params=_params(("parallel", "arbitrary")),
    )(p_att, p_att, p_att, o, l, do, dl, p_att, o, l, do, dl)
    return dq, dk, dv


FFN_TILE = 2 * D_FF // N_CHIPS
RKV = 3 * RW_WIDTH
WA = 128
XG = 160
RW_COLS = RKV + WA + XG


def _local_step(x, p, W, target, late_weights=None, early_grads=None, by_chip=False, grad_dtype=F32):
    T = x.shape[0]
    tT = 256
    bd512 = _block_diag_ones(RW_WIDTH, RW_HEAD_DIM)
    bd256 = _block_diag_ones(ATT_GROUP_WIDTH, ATT_HEAD_DIM)
    G = {}
    W = dict(W)

    w_in = W["w_in"]
    w_rkv, w_wa, w_xg, w_att = (w_in[:, :RKV], w_in[:, RKV:RKV + WA], w_in[:, RKV + WA:RW_COLS],
                                w_in[:, RW_COLS:])
    mu = W["rw_mu"]
    mu_rkv, mu_wa, mu_xg = mu[:, :RKV], mu[:, RKV:RKV + WA], mu[:, RKV + WA:]
    zpad = jnp.zeros((64, RW_WIDTH), W["rw_w_up"].dtype)
    w_up_pad = jnp.concatenate([W["rw_w_up"], zpad], axis=0)
    a_up_pad = jnp.concatenate([zpad, W["rw_a_up"]], axis=0)
    r_k = W["rw_r_k"].reshape(1, RW_WIDTH)

    (h,) = _rowwise("norm_mix", lambda i, n, r, pv, nx, c: [_rms_fwd(r[0], c[0])], T, tT,
                    rows=[x], consts=[W["g_mix"]], outs=[("row", D_MODEL, BF16)])
    p_rkv = _mm("proj_rkv", h, w_rkv, "nn")
    p_wa = _mm("proj_wa", h, w_wa, "nn")
    p_xg = _mm("proj_xg", h, w_xg, "nn")
    p_att = _mm("proj_att", h, w_att, "nn", tn=768)
    z_gate = _mm("proj_gate", h, W["w_gate"], "nn")

    def rw_pre_core(i, rows, prevs, consts):
        prkv, pwa, pxg = rows[:3]
        (mrkv, mwa, mxg, w0, a0, k_k, k_a, wup, aup, gup, bd) = consts[:11]
        m_rkv = prkv + (_shift_down(prkv, prevs[0], i, 1) - prkv) * mrkv
        m_wa = pwa + (_shift_down(pwa, prevs[1], i, 1) - pwa) * mwa
        m_xg = pxg + (_shift_down(pxg, prevs[2], i, 1) - pxg) * mxg
        r, k, v = m_rkv[:, :RW_WIDTH], m_rkv[:, RW_WIDTH:2 * RW_WIDTH], m_rkv[:, 2 * RW_WIDTH:]
        tw = jnp.tanh(m_wa)
        lw = w0 + jnp.dot(tw.astype(BF16), wup.astype(BF16), preferred_element_type=F32)
        wlog = -_softplus(-lw) - 0.5
        log_decay = -jnp.exp(wlog)
        a = _sigmoid(a0 + jnp.dot(m_wa.astype(BF16), aup.astype(BF16), preferred_element_type=F32))
        sg = _sigmoid(m_xg)
        gate = jnp.dot(sg.astype(BF16), gup.astype(BF16), preferred_element_type=F32)
        kkp = k * k_k
        nrm = jnp.sqrt(_segsum(kkp * kkp, bd))
        nrm_c = jnp.maximum(nrm, 1e-12)
        kk = kkp / nrm_c
        k2 = k * (1.0 + (a - 1.0) * k_a)
        return dict(r=r, k=k, v=v, tw=tw, lw=lw, wlog=wlog, log_decay=log_decay, a=a, sg=sg, gate=gate, kkp=kkp,
                    nrm=nrm, nrm_c=nrm_c, kk=kk, k2=k2, m_rkv=m_rkv, m_wa=m_wa, m_xg=m_xg)

    pre_consts = [mu_rkv, mu_wa, mu_xg, W["rw_w0"], W["rw_a0"], W["rw_k_k"], W["rw_k_a"],
                  w_up_pad, a_up_pad, W["rw_g_up"], bd512]

    def rw_pre(i, n, rows, prevs, nexts, consts):
        q = rw_pre_core(i, rows, prevs, consts)
        return [q["r"], q["log_decay"], q["k2"], q["v"], -q["kk"], q["kk"] * q["a"], q["gate"]]

    r_s, w_s, k_s, v_s, a_s, b_s, gate_s = _rowwise(
        "rwkv_pre", rw_pre, T, tT, rows=[p_rkv, p_wa, p_xg], prevs=[p_rkv, p_wa, p_xg], consts=pre_consts,
        outs=[("row", RW_WIDTH, F32)] * 7)
    (at_s, bt_s, kt_s, rt_s, a2v_s, w2v_s, tinv_s, w1_s, a2_s, w2_s,
     plast_s) = _rwkv_chunk_prep(r_s, w_s, k_s, a_s, b_s, v_s)
    y_scan, sa_s, s0_s = _rwkv_chunk_fwd(v_s, at_s, bt_s, kt_s, rt_s, a2v_s, w2v_s, tinv_s, w1_s, plast_s)

    def rw_post_core(rows, consts):
        y, r, k2, v, gate = rows[:5]
        ln_g, ln_b, rk, bd = consts[:4]
        mean = _segsum(y, bd) * (1.0 / RW_HEAD_DIM)
        yc = y - mean
        var = _segsum(yc * yc, bd) * (1.0 / RW_HEAD_DIM)
        rstd = lax.rsqrt(var + RW_LN_EPS)
        yn = yc * rstd
        s = _segsum(r * k2 * rk, bd)
        return dict(yn=yn, rstd=rstd, s=s, pre=yn * ln_g + ln_b + s * v)

    post_consts = [W["rw_ln_g"], W["rw_ln_b"], r_k, bd512]
    (y_a,) = _rowwise("rwkv_post", lambda i, n, r, pv, nx, c: [rw_post_core(r, c)["pre"] * r[4]], T, tT,
                      rows=[y_scan, r_s, k_s, v_s, gate_s], consts=post_consts, outs=[("row", RW_WIDTH, BF16)])

    att = [_att_fwd(p_att, g) for g in range(3)]

    def comb_weights(ls):
        mx = jnp.maximum(jnp.maximum(ls[0], ls[1]), ls[2])
        es = [jnp.exp(l - mx) for l in ls]
        den = es[0] + es[1] + es[2]
        return [e / den for e in es]

    def att_comb(i, n, rows, pv, nx, c):
        wts = comb_weights(rows[3:6])
        return [wts[0] * rows[0] + wts[1] * rows[1] + wts[2] * rows[2]]

    (y_b,) = _rowwise("att_combine", att_comb, T, tT, rows=[att[0][0], att[1][0], att[2][0], att[0][1], att[1][1],
                                                            att[2][1]], outs=[("row", ATT_GROUP_WIDTH, BF16)])

    if late_weights is not None:
        W.update(late_weights(y_b))
    br_a = _mm("branch_a", y_a, W["w_branch_a"], "nn")
    br_b = _mm("branch_b", y_b, W["w_branch_b"], "nn")

    def merge(i, n, rows, pv, nx, c):
        gates = _sigmoid(rows[0] + c[0])
        return [gates[:, :D_MODEL] * rows[1] + gates[:, D_MODEL:] * rows[2]]

    (merged,) = _rowwise("merge", merge, T, tT, rows=[z_gate, br_a, br_b], consts=[W["b_gate"]],
                         outs=[("row", D_MODEL, BF16)])
    with_norm = lambda res, rows, consts: [res, _rms_fwd(res, consts[0])]
    stream_and_norm = [("row", D_MODEL, F32), ("row", D_MODEL, BF16)]
    x1, h2 = _mm("mix_out", merged, W["w_out"], "nn", add=x, post=(with_norm, [], [W["g_ffn"]], stream_and_norm))

    u = _mm("ffn_up", h2, W["w_up"], "nn", tn=FFN_TILE)

    def conv_core(i, rows, prevs, consts):
        uu, cw, cb = rows[0], consts[0], consts[1]
        u1 = _shift_down(uu, prevs[0], i, 1)
        u2 = _shift_down(uu, prevs[0], i, 2)
        uc = cb + cw[0:1] * uu + cw[1:2] * u1 + cw[2:3] * u2
        return uc[:, :D_FF], uc[:, D_FF:], u1, u2

    def glu(i, n, rows, prevs, nx, consts):
        gate, val, _, _ = conv_core(i, rows, prevs, consts)
        return [_gelu(gate) * val]

    tF = 128
    (act,) = _rowwise("conv_glu", glu, T, tF, rows=[u], prevs=[u], consts=[W["conv_w"], W["conv_b"]],
                      outs=[("row", D_FF, BF16)])
    x2, h3 = _mm("ffn_down", act, W["w_down"], "nn", add=x1, post=(with_norm, [], [W["g_ple"]], stream_and_norm))

    e_ple = _mm("ple_emb", p, W["w_ple"], "nn")

    def head(i, n, rows, pv, nx, consts):
        x2_, z, e, tgt = rows
        pg = _sigmoid(z)
        x3 = x2_ + pg * e
        y = _rms_fwd(x3, consts[0])
        err = y - tgt
        loss = 0.5 * jnp.sum(jnp.sum(err * err, axis=1, keepdims=True) * (1.0 / D_MODEL), axis=0, keepdims=True)
        dy = err * (1.0 / D_MODEL)
        dx3, dgf = _rms_bwd(x3, consts[0], dy)
        return [dx3, dx3 * pg, dx3 * e * pg * (1.0 - pg), jnp.broadcast_to(loss, (1, LANES)), _colsum(dgf)]

    dx3, de, dz, loss_acc, G["g_final"] = _mm(
        "ple_gate_loss_head", h3, W["w_ple_gate"], "nn", tm=512,
        post=(lambda res, rows, consts: head(0, 0, [rows[0], res, rows[1], rows[2]], [], [], consts),
              [x2, e_ple, target], [W["g_final"].reshape(1, D_MODEL)],
              [("row", D_MODEL, F32), ("row", D_MODEL, BF16), ("row", D_MODEL, BF16), ("acc", (1, LANES)),
               ("acc", (1, D_MODEL))]))
    G["w_ple"] = _mm("d_w_ple", p, de, "tn", grad_dtype, out_by_chip=by_chip)
    G["w_ple_gate"] = _mm("d_w_ple_gate", h3, dz, "tn", grad_dtype)
    def norm_bwd(i, n, rows, pv, nx, consts):
        dx, dg = _rms_bwd(rows[0], consts[0], rows[1])
        return [rows[2] + dx, _colsum(dg)]

    through_norm = lambda res, rows, consts: norm_bwd(0, 0, [rows[0], res, rows[1]], [], [], consts)
    stream_and_gain = [("row", D_MODEL, F32), ("acc", (1, D_MODEL))]
    dx2, G["g_ple"] = _mm("d_h3", dz, W["w_ple_gate"], "nt", tm=512,
                          post=(through_norm, [x2, dx3], [W["g_ple"]], stream_and_gain))

    dact = _mm("d_act", dx2, W["w_down"], "nt")
    G["w_down"] = _mm("d_w_down", act, dx2, "tn", grad_dtype)

    def glu_grad(gate, val, da):
        act_, slope = _gelu_and_grad(gate)
        return jnp.concatenate([da * val * slope, da * act_], axis=1)

    def glu_bwd(i, n, rows, prevs, nexts, consts):
        uu, da = rows
        cw = consts[0]
        gate, val, u1, u2 = conv_core(i, rows, prevs, consts)
        duc = glu_grad(gate, val, da)
        dcw = jnp.concatenate([_colsum(duc * uu), _colsum(duc * u1), _colsum(duc * u2)], axis=0)
        gate_n, val_n, _, _ = conv_core(1, [nexts[0]], [uu[tF - SUBLANES:]], consts)
        duc_n = glu_grad(gate_n, val_n, nexts[1])
        du = (cw[0:1] * duc + cw[1:2] * _shift_up(duc, duc_n, i, n, 1) + cw[2:3] * _shift_up(duc, duc_n, i, n, 2))
        return [du, _colsum(duc), dcw]

    du, G["conv_b"], G["conv_w"] = _rowwise(
        "d_conv_glu", glu_bwd, T, tF, rows=[u, dact], prevs=[u], nexts=[u, dact],
        consts=[W["conv_w"], W["conv_b"]],
        outs=[("row", 2 * D_FF, BF16), ("acc", (1, 2 * D_FF)), ("acc", (3, 2 * D_FF))])
    G["w_up"] = _mm("d_w_up", h2, du, "tn", grad_dtype, out_by_chip=by_chip, tn=FFN_TILE)
    dh2 = _mm("d_h2", du, W["w_up"], "nt", tk=FFN_TILE)
    dx1, G["g_ffn"] = _rowwise("d_norm_ffn", norm_bwd, T, tT, rows=[x1, dh2, dx2], consts=[W["g_ffn"]],
                               outs=[("row", D_MODEL, F32), ("acc", (1, D_MODEL))])

    b_gate = W["b_gate"]
    if early_grads is not None:
        b_gate = b_gate + early_grads(G, 0)[0:1, 0:1]
    G["w_out"] = _mm("d_w_out", merged, dx1, "tn", grad_dtype)

    def merge_bwd(dm, rows, consts):
        z, a_, b_ = rows
        gates = _sigmoid(z + consts[0])
        ga, gb = gates[:, :D_MODEL], gates[:, D_MODEL:]
        dz_ = jnp.concatenate([dm * a_ * ga * (1.0 - ga), dm * b_ * gb * (1.0 - gb)], axis=1)
        return [dm * ga, dm * gb, dz_, _colsum(dz_)]

    d_br_a, d_br_b, dz_gate, G["b_gate"] = _mm(
        "d_merged", dx1, W["w_out"], "nt", tm=512,
        post=(merge_bwd, [z_gate, br_a, br_b], [b_gate],
              [("row", D_MODEL, BF16), ("row", D_MODEL, BF16), ("row", 2 * D_MODEL, BF16),
               ("acc", (1, 2 * D_MODEL))]))
    G["w_branch_a"] = _mm("d_w_branch_a", y_a, d_br_a, "tn", grad_dtype, out_by_chip=by_chip)
    G["w_branch_b"] = _mm("d_w_branch_b", y_b, d_br_b, "tn", grad_dtype, out_by_chip=by_chip)
    G["w_gate"] = _mm("d_w_gate", h, dz_gate, "tn", grad_dtype, out_by_chip=by_chip)
    if early_grads is not None:
        post_consts = [post_consts[0] + early_grads(G, 1)[0:1, 0:1]] + post_consts[1:]
    dy_a = _mm("d_y_a", d_br_a, W["w_branch_a"], "nt")
    dy_b = _mm("d_y_b", d_br_b, W["w_branch_b"], "nt")

    def att_comb_bwd(i, n, rows, pv, nx, consts):
        os_, ls, dy = rows[0:3], rows[3:6], rows[6]
        wts = comb_weights(ls)
        dws = [_segsum(dy * o_, consts[0]) for o_ in os_]
        mix = wts[0] * dws[0] + wts[1] * dws[1] + wts[2] * dws[2]
        return [wts[g_] * dy for g_ in range(3)] + [wts[g_] * (dws[g_] - mix) for g_ in range(3)]

    comb = _rowwise("d_att_combine", att_comb_bwd, T, tT,
                    rows=[att[0][0], att[1][0], att[2][0], att[0][1], att[1][1], att[2][1], dy_b], consts=[bd256],
                    outs=[("row", ATT_GROUP_WIDTH, F32)] * 6)
    dqkv = [_att_bwd(p_att, att[g][0], att[g][1], comb[g], comb[3 + g], g) for g in range(3)]
    dp_att = jnp.concatenate([dqkv[g][part] for part in range(3) for g in range(3)], axis=1).astype(BF16)

    def rw_post_bwd(i, n, rows, pv, nx, consts):
        y, r, k2, v, gate, dya = rows
        ln_g, ln_b, rk, bd = consts
        q = rw_post_core(rows, consts)
        dpre = dya * gate
        dgate = dya * q["pre"]
        dyn = dpre * ln_g
        inv = 1.0 / RW_HEAD_DIM
        dy_scan = q["rstd"] * (dyn - _segsum(dyn, bd) * inv - q["yn"] * (_segsum(dyn * q["yn"], bd) * inv))
        ds = _segsum(dpre * v, bd)
        return [dy_scan, dgate, ds * k2 * rk, ds * r * rk, dpre * q["s"],
                _colsum(dpre * q["yn"]), _colsum(dpre), _colsum(ds * r * k2)]

    dy_scan, dgate, dr_b, dk2_b, dv_b, G["rw_ln_g"], G["rw_ln_b"], d_rk = _rowwise(
        "d_rwkv_post", rw_post_bwd, T, tT, rows=[y_scan, r_s, k_s, v_s, gate_s, dy_a], consts=post_consts,
        outs=[("row", RW_WIDTH, F32)] * 5 + [("acc", (1, RW_WIDTH))] * 3)
    G["rw_r_k"] = d_rk.reshape(RW_HEADS, RW_HEAD_DIM)

    dr_s, dw_s, dk_s, da_s, db_s, dv_s = _rwkv_chunk_bwd(r_s, w_s, k_s, a_s, b_s, v_s, dy_scan, s0_s, tinv_s, w1_s,
                                                         a2_s, w2_s, sa_s)

    def rw_pre_bwd(i, n, rows, prevs, nx, consts):
        q = rw_pre_core(i, rows, prevs, consts)
        (mrkv, mwa, mxg, w0, a0, k_k, k_a, wup, aup, gup, bd) = consts
        dr, dlogdecay, dk2, dv, dav, dbv, dgate_ = rows[3:10]
        dr = dr + rows[10]
        dk2 = dk2 + rows[11]
        dv = dv + rows[12]
        a, k, kk = q["a"], q["k"], q["kk"]
        dk = dk2 * (1.0 + (a - 1.0) * k_a)
        da = dk2 * k * k_a + dbv * kk
        dkk = dbv * a - dav
        live = q["nrm"] > 1e-12
        dkkp = jnp.where(live, dkk - kk * _segsum(dkk * kk, bd), dkk) / q["nrm_c"]
        dk = dk + dkkp * k_k
        dlw = dlogdecay * q["log_decay"] * _sigmoid(-q["lw"])
        dla = da * a * (1.0 - a)
        nt = (((1,), (1,)), ((), ()))
        dtw = lax.dot_general(dlw.astype(BF16), wup.astype(BF16), nt, preferred_element_type=F32)
        dxa = lax.dot_general(dla.astype(BF16), aup.astype(BF16), nt, preferred_element_type=F32)
        dm_wa = dtw * (1.0 - q["tw"] * q["tw"]) + dxa
        dsg = lax.dot_general(dgate_.astype(BF16), gup.astype(BF16), nt, preferred_element_type=F32)
        dm_xg = dsg * q["sg"] * (1.0 - q["sg"])
        dm_rkv = jnp.concatenate([dr, dk, dv], axis=1)
        prkv, pwa, pxg = rows[:3]
        dmu = jnp.concatenate([_colsum(dm_rkv * (_shift_down(prkv, prevs[0], i, 1) - prkv)),
                               _colsum(dm_wa * (_shift_down(pwa, prevs[1], i, 1) - pwa)),
                               _colsum(dm_xg * (_shift_down(pxg, prevs[2], i, 1) - pxg))], axis=1)
        return [dm_rkv, dm_wa, dm_xg, dlw, dla, q["tw"], q["m_wa"], q["sg"], dmu,
                _colsum(dlw), _colsum(dla), _colsum(dkkp * k), _colsum(dk2 * k * (a - 1.0))]

    (dm_rkv, dm_wa, dm_xg, dlw, dla, tw_s, mwa_s, sg_s, G["rw_mu"], G["rw_w0"], G["rw_a0"], G["rw_k_k"],
     G["rw_k_a"]) = _rowwise(
        "d_rwkv_pre", rw_pre_bwd, T, tT,
        rows=[p_rkv, p_wa, p_xg, dr_s, dw_s, dk_s, dv_s, da_s, db_s, dgate, dr_b, dk2_b, dv_b],
        prevs=[p_rkv, p_wa, p_xg], consts=pre_consts,
        outs=[("row", RKV, F32), ("row", WA, F32), ("row", XG, F32), ("row", RW_WIDTH, BF16),
              ("row", RW_WIDTH, BF16), ("row", WA, BF16), ("row", WA, BF16), ("row", XG, BF16),
              ("acc", (1, RW_COLS))] + [("acc", (1, RW_WIDTH))] * 4)
    G["rw_w_up"] = _mm("d_rw_w_up", tw_s, dlw, "tn", grad_dtype)[:64]
    G["rw_a_up"] = _mm("d_rw_a_up", mwa_s, dla, "tn", grad_dtype)[64:]
    G["rw_g_up"] = _mm("d_rw_g_up", sg_s, dgate, "tn", grad_dtype)

    def shift_bwd(i, n, rows, pv, nexts, consts):
        return [rows[j] * (1.0 - consts[j]) + _shift_up(rows[j], nexts[j], i, n, 1) * consts[j] for j in range(3)]

    dp_rkv, dp_wa, dp_xg = _rowwise(
        "d_token_shift", shift_bwd, T, tT, rows=[dm_rkv, dm_wa, dm_xg], nexts=[dm_rkv, dm_wa, dm_xg],
        consts=[mu_rkv, mu_wa, mu_xg], outs=[("row", RKV, BF16), ("row", WA, BF16), ("row", XG, BF16)])

    G["w_in"] = jnp.concatenate([_mm("d_w_rkv", h, dp_rkv, "tn", grad_dtype), _mm("d_w_wa", h, dp_wa, "tn", grad_dtype),
                                 _mm("d_w_xg", h, dp_xg, "tn", grad_dtype), _mm("d_w_att", h, dp_att, "tn", grad_dtype, tn=768)], axis=1)
    if early_grads is not None:
        w_wa = w_wa + early_grads(G, 2)[0:1, 0:1].astype(w_wa.dtype)
    dh = _mm("d_h_gate", dz_gate, W["w_gate"], "nt")
    dh = _mm("d_h_rkv", dp_rkv, w_rkv, "nt", add=dh)
    dh = _mm("d_h_wa", dp_wa, w_wa, "nt", add=dh)
    dh = _mm("d_h_xg", dp_xg, w_xg, "nt", add=dh)
    dx, G["g_mix"] = _mm("d_h_att", dp_att, w_att, "nt", add=dh, tm=512,
                         post=(through_norm, [x, dx1], [W["g_mix"]], stream_and_gain))
    return loss_acc[:, :1], dx, G


HBM_SPEC = pl.BlockSpec(memory_space=pltpu.HBM)


def _place():
    x, y, c = lax.axis_index("x"), lax.axis_index("y"), lax.axis_index("c")
    return x, y, c, [(1 - x, y), (x, 1 - y), (1 - x, 1 - y)]


def _remote(src, dst, send_sems, recv_sems, k, to):
    return pltpu.make_async_remote_copy(src_ref=src, dst_ref=dst, send_sem=send_sems.at[k], recv_sem=recv_sems.at[k],
                                        device_id=to, device_id_type=MESH)


ROW_ALIGN = 16


def _splits(rows):
    return rows % (2 * ROW_ALIGN) == 0


def _half_rows(ref_rows, c, first):
    half = ref_rows // 2
    which = c if first else 1 - c
    return pl.ds(pl.multiple_of(which * half, ROW_ALIGN), half)


def _gather_chips(shards):
    n = len(shards)
    split = [_splits(s.shape[0]) for s in shards]

    def body(*refs):
        w_refs, out_refs = refs[:n], refs[n:2 * n]
        send_sems, recv_sems = refs[2 * n:]
        x, y, c, chips = _place()
        me = 2 * x + y
        sends, passed = [], []
        for i in range(n):
            for j, (px, py) in enumerate(chips):
                if split[i]:
                    mine = _half_rows(w_refs[i].shape[0], c, True)
                    cp = _remote(w_refs[i].at[mine], out_refs[i].at[me, mine], send_sems, recv_sems, 6 * i + j,
                                 (px, py, c))
                else:
                    cp = _remote(w_refs[i], out_refs[i].at[me], send_sems, recv_sems, 6 * i + j, (px, py, c))
                cp.start()
                sends.append(cp)
        for i in range(n):
            for j, (px, py) in enumerate(chips):
                if split[i]:
                    landed = out_refs[i].at[2 * px + py, _half_rows(w_refs[i].shape[0], c, True)]
                    _remote(landed, landed, send_sems, recv_sems, 6 * i + j, (px, py, c)).wait_recv()
                    cp = _remote(landed, landed, send_sems, recv_sems, 6 * i + 3 + j, (x, y, 1 - c))
                    cp.start()
                    passed.append(cp)
                else:
                    landed = out_refs[i].at[2 * px + py]
                    _remote(landed, landed, send_sems, recv_sems, 6 * i + j, (px, py, c)).wait_recv()
        for i in range(n):
            if split[i]:
                for j, (px, py) in enumerate(chips):
                    landed = out_refs[i].at[2 * px + py, _half_rows(w_refs[i].shape[0], c, False)]
                    _remote(landed, landed, send_sems, recv_sems, 6 * i + 3 + j, (x, y, 1 - c)).wait_recv()
        for cp in sends + passed:
            cp.wait_send()

    outs = pl.pallas_call(
        body, name="gather_weights", in_specs=[HBM_SPEC] * n, out_specs=[HBM_SPEC] * n,
        out_shape=[jax.ShapeDtypeStruct((N_CHIPS,) + s.shape, s.dtype) for s in shards],
        scratch_shapes=[pltpu.SemaphoreType.DMA((6 * n,)), pltpu.SemaphoreType.DMA((6 * n,))],
    )(*shards)
    me = 2 * lax.axis_index("x") + lax.axis_index("y")
    return [lax.dynamic_update_slice(o, s[None], (me, 0, 0)) for o, s in zip(outs, shards, strict=True)]


def _join_halves(reds):
    n = len(reds)

    def body(*refs):
        r_refs, out_refs = refs[:n], refs[n:2 * n]
        send_sems, recv_sems = refs[2 * n:]
        x, y, c, _ = _place()
        cps = []
        for i in range(n):
            mine = _half_rows(out_refs[i].shape[0], c, True)
            cp = _remote(r_refs[i], out_refs[i].at[mine], send_sems, recv_sems, i, (x, y, 1 - c))
            cp.start()
            cps.append(cp)
        for cp in cps:
            cp.wait()

    outs = pl.pallas_call(
        body, name="join_halves", in_specs=[HBM_SPEC] * n, out_specs=[HBM_SPEC] * n,
        out_shape=[jax.ShapeDtypeStruct((2 * r.shape[0], r.shape[1]), r.dtype) for r in reds],
        scratch_shapes=[pltpu.SemaphoreType.DMA((n,)), pltpu.SemaphoreType.DMA((n,))],
    )(*reds)
    c = lax.axis_index("c")
    return [lax.dynamic_update_slice(o, r, (c * r.shape[0], 0)) for o, r in zip(outs, reds, strict=True)]


def _gather_all(vec):
    def body(v_ref, out_ref, send_sems, recv_sems, local_sem):
        x, y, c, _ = _place()
        me = 4 * x + 2 * y + c
        local = pltpu.make_async_copy(v_ref, out_ref.at[me], local_sem)
        local.start()
        peers = [(x ^ (k >> 2), y ^ ((k >> 1) & 1), c ^ (k & 1)) for k in range(1, N_DEV)]
        sends = [_remote(v_ref, out_ref.at[me], send_sems, recv_sems, k, to) for k, to in enumerate(peers)]
        for cp in sends:
            cp.start()
        for k, (px, py, pc) in enumerate(peers):
            landed = out_ref.at[4 * px + 2 * py + pc]
            _remote(landed, landed, send_sems, recv_sems, k, (px, py, pc)).wait_recv()
        for cp in sends:
            cp.wait_send()
        local.wait()

    return pl.pallas_call(
        body, name="gather_small", in_specs=[HBM_SPEC], out_specs=HBM_SPEC,
        out_shape=jax.ShapeDtypeStruct((N_DEV,) + vec.shape, vec.dtype),
        scratch_shapes=[pltpu.SemaphoreType.DMA((7,)), pltpu.SemaphoreType.DMA((7,)), pltpu.SemaphoreType.DMA],
    )(vec)


SEM_SPEC = pl.BlockSpec(memory_space=pltpu.SEMAPHORE)
PEERS = N_DEV - 1
DATAFLOW = pltpu.SideEffectType.DATAFLOW_SIDE_EFFECTING


def _travel_copies(mode, src_refs, land_refs, send_sems, recv_sems):
    x, y, c, chips = _place()
    me = 2 * x + y
    pairs = []
    for i, (src, land) in enumerate(zip(src_refs, land_refs, strict=True)):
        if mode == "scatter":
            for k in range(1, N_DEV):
                px, py, pc = x ^ (k >> 2), y ^ ((k >> 1) & 1), c ^ (k & 1)
                mine = src.at[2 * px + py, _half_rows(src.shape[1], pc, True)]
                there, here = land.at[4 * x + 2 * y + c], land.at[4 * px + 2 * py + pc]
                send = functools.partial(_remote, mine, there, send_sems, recv_sems, PEERS * i + k - 1, (px, py, pc))
                arrival = functools.partial(_remote, mine, here, send_sems, recv_sems, PEERS * i + k - 1, (px, py, pc))
                pairs.append((send, arrival))
            continue
        for j, (px, py) in enumerate(chips):
            peer = 2 * px + py
            if _splits(src.shape[0]):
                rows = _half_rows(src.shape[0], c, True)
                mine, there, here = src.at[rows], land.at[me, rows], land.at[peer, rows]
            else:
                mine, there, here = src, land.at[me], land.at[peer]
            send = functools.partial(_remote, mine, there, send_sems, recv_sems, PEERS * i + j, (px, py, c))
            arrival = functools.partial(_remote, mine, here, send_sems, recv_sems, PEERS * i + j, (px, py, c))
            pairs.append((send, arrival))
    return pairs


def _share_halves(name, lands):
    idx = [i for i, a in enumerate(lands) if _splits(a.shape[1])]
    n = len(idx)

    def body(*refs):
        in_refs, out_refs = refs[:n], refs[n:2 * n]
        send_sems, recv_sems = refs[2 * n:]
        x, y, c, chips = _place()
        cps = []
        for i, (src, dst) in enumerate(zip(in_refs, out_refs, strict=True)):
            for j, (px, py) in enumerate(chips):
                mine = _half_rows(src.shape[1], c, True)
                cp = _remote(src.at[2 * px + py, mine], dst.at[2 * px + py, mine], send_sems, recv_sems, 3 * i + j,
                             (x, y, 1 - c))
                cp.start()
                cps.append(cp)
        for i, dst in enumerate(out_refs):
            for j, (px, py) in enumerate(chips):
                theirs = dst.at[2 * px + py, _half_rows(dst.shape[1], c, False)]
                _remote(theirs, theirs, send_sems, recv_sems, 3 * i + j, (x, y, 1 - c)).wait_recv()
        for cp in cps:
            cp.wait_send()

    outs = pl.pallas_call(
        body, name=name, in_specs=[HBM_SPEC] * n, out_specs=[HBM_SPEC] * n,
        out_shape=[jax.ShapeDtypeStruct(lands[i].shape, lands[i].dtype) for i in idx],
        input_output_aliases={i: i for i in range(n)},
        scratch_shapes=[pltpu.SemaphoreType.DMA((3 * n,)), pltpu.SemaphoreType.DMA((3 * n,))],
    )(*[lands[i] for i in idx])
    done = list(lands)
    for i, o in zip(idx, outs, strict=True):
        done[i] = o
    return done


def _travel_start(name, mode, srcs):
    n = len(srcs)
    lands = [lax.empty((N_CHIPS,) + s.shape if mode == "gather" else (N_DEV, s.shape[1] // 2, s.shape[2]), s.dtype)
             for s in srcs]

    def body(*refs):
        src_refs, land_refs = refs[:n], refs[n:2 * n]
        send_sems, recv_sems = refs[2 * n], refs[2 * n + 1]
        token = refs[-1]
        for send, _ in _travel_copies(mode, src_refs, land_refs, send_sems, recv_sems):
            send().start()
        token[...] = jnp.zeros_like(token)

    hbm = lambda a: pltpu.HBM(a.shape, a.dtype)
    outs = pl.pallas_call(
        body, name=name,
        out_shape=(pltpu.SemaphoreType.DMA((PEERS * n,)), pltpu.SemaphoreType.DMA((PEERS * n,)),
                   *[hbm(s) for s in srcs],
                   *[hbm(a) for a in lands], jax.ShapeDtypeStruct((SUBLANES, LANES), F32)),
        in_specs=[HBM_SPEC] * (2 * n),
        out_specs=(SEM_SPEC, SEM_SPEC, *[HBM_SPEC] * (2 * n), pl.BlockSpec(memory_space=pltpu.VMEM)),
        input_output_aliases={i: 2 + i for i in range(2 * n)},
        compiler_params=pltpu.CompilerParams(has_side_effects=DATAFLOW),
    )(*[pltpu.with_memory_space_constraint(a, pltpu.HBM) for a in list(srcs) + lands])
    return outs[0], outs[1], list(outs[2:2 + n]), list(outs[2 + n:2 + 2 * n]), outs[-1]


def _travel_wait(name, mode, send_sems, recv_sems, srcs, lands, after):
    n = len(srcs)

    def body(*refs):
        src_refs, land_refs = refs[:n], refs[n:2 * n]
        send_sems_, recv_sems_ = refs[2 * n], refs[2 * n + 1]
        for send, arrival in _travel_copies(mode, src_refs, land_refs, send_sems_, recv_sems_):
            send().wait_send()
            arrival().wait_recv()

    hbm = lambda a: pltpu.HBM(a.shape, a.dtype)
    outs = pl.pallas_call(
        body, name=name, out_shape=tuple(hbm(a) for a in list(srcs) + list(lands)),
        in_specs=[HBM_SPEC] * (2 * n) + [SEM_SPEC, SEM_SPEC, pl.BlockSpec(memory_space=pl.ANY)],
        out_specs=tuple([HBM_SPEC] * (2 * n)), input_output_aliases={i: i for i in range(2 * n)},
        compiler_params=pltpu.CompilerParams(has_side_effects=DATAFLOW),
    )(*srcs, *lands, send_sems, recv_sems, after)
    c = lax.axis_index("c")
    me = 2 * lax.axis_index("x") + lax.axis_index("y")
    if mode == "gather":
        slot, own = me, [s[None] for s in outs[:n]]
    else:
        slot = 2 * me + c
        own = [lax.dynamic_slice(s, (me, c * (s.shape[1] // 2), 0), (1, s.shape[1] // 2, s.shape[2])) for s in outs[:n]]
    return [lax.dynamic_update_slice(a, o, (slot,) + (0,) * (a.ndim - 1)) for a, o in zip(outs[n:], own, strict=True)]


SUM_TILE_BYTES = 4 * 1024 * 1024


def _sum_rows(half, cols):
    best = ROW_ALIGN
    for t in range(ROW_ALIGN, half + 1, ROW_ALIGN):
        if half % t == 0 and N_CHIPS * t * cols * 4 <= SUM_TILE_BYTES:
            best = t
    return best


def _sum_devices(name, parts):
    n, H, C = parts.shape
    tr = _sum_rows(H, C)

    def body(p_ref, o_ref):
        acc = p_ref[0].astype(F32)
        for k in range(1, n):
            acc = acc + p_ref[k].astype(F32)
        o_ref[...] = acc

    return pl.pallas_call(
        body, name=name, grid=(H // tr,),
        in_specs=[pl.BlockSpec((n, tr, C), lambda i: (0, i, 0))],
        out_specs=pl.BlockSpec((tr, C), lambda i: (i, 0)),
        out_shape=jax.ShapeDtypeStruct((H, C), F32),
        compiler_params=_params(("parallel",)),
    )(parts)


def _adamw_math(w, g, m, v):
    m = ADAM_B1 * m + (1.0 - ADAM_B1) * g
    v = ADAM_B2 * v + (1.0 - ADAM_B2) * (g * g)
    m_hat = m / (1.0 - ADAM_B1 ** ADAM_STEP)
    v_hat = v / (1.0 - ADAM_B2 ** ADAM_STEP)
    delta = -ADAM_LR * (m_hat / (jnp.sqrt(v_hat) + ADAM_EPS) + ADAM_WD * w)
    return delta, m, v


def _adamw(name, w, g, m, v):
    R, C = w.shape
    tr = R
    if R % SUBLANES == 0:
        for cand in range(SUBLANES, min(R, 256) + 1, SUBLANES):
            if R % cand == 0:
                tr = cand

    def body(w_ref, g_ref, m_ref, v_ref, d_ref, nm_ref, nv_ref):
        d, nm, nv = _adamw_math(w_ref[...], g_ref[...], m_ref[...], v_ref[...])
        d_ref[...] = d
        nm_ref[...] = nm
        nv_ref[...] = nv

    spec = pl.BlockSpec((tr, C), lambda i: (i, 0))
    shape = jax.ShapeDtypeStruct((R, C), F32)
    return pl.pallas_call(
        body, name=name, grid=(R // tr,), in_specs=[spec] * 4, out_specs=[spec] * 3, out_shape=[shape] * 3,
        compiler_params=_params(("parallel",)),
    )(w, g, m, v)


SMALL_ROW = 2048


def _small_layout(shapes):
    places, row = [], 0
    for R, C in shapes:
        pieces = []
        for r in range(R):
            for c0 in range(0, C, SMALL_ROW):
                pieces.append((r, c0, min(C, c0 + SMALL_ROW), row))
                row += 1
        places.append(pieces)
    return places, -(-row // SUBLANES) * SUBLANES


def _put_rows(block_ref, refs, places):
    block_ref[...] = jnp.zeros_like(block_ref)
    for ref, pieces in zip(refs, places, strict=True):
        for r, c0, c1, row in pieces:
            block_ref[row:row + 1, 0:c1 - c0] = ref[r:r + 1, c0:c1]


def _take_rows(block, refs, places):
    for ref, pieces in zip(refs, places, strict=True):
        for r, c0, c1, row in pieces:
            ref[r:r + 1, c0:c1] = block[row:row + 1, 0:c1 - c0]


def _pack_small(arrs):
    places, rows = _small_layout([a.shape for a in arrs])

    def body(*refs):
        _put_rows(refs[-1], refs[:-1], places)

    return pl.pallas_call(body, name="pack_small", out_shape=jax.ShapeDtypeStruct((rows, SMALL_ROW), F32),
                          compiler_params=_params())(*arrs)


def _adamw_small(parts, ws, ms, vs, extra_shapes):
    n_dev, rows, _ = parts.shape
    n = len(ws)
    places, rows_ = _small_layout([w.shape for w in ws] + list(extra_shapes))
    assert rows_ == rows, (rows_, rows)

    def body(*refs):
        p_ref = refs[0]
        w_refs, m_refs, v_refs = refs[1:1 + n], refs[1 + n:1 + 2 * n], refs[1 + 2 * n:1 + 3 * n]
        outs = refs[1 + 3 * n:-3]
        wb, mb, vb = refs[-3:]
        for block, srcs in ((wb, w_refs), (mb, m_refs), (vb, v_refs)):
            _put_rows(block, srcs, places[:n])
        g = p_ref[0]
        for k in range(1, n_dev):
            g = g + p_ref[k]
        d, nm, nv = _adamw_math(wb[...], g, mb[...], vb[...])
        _take_rows(g, outs[0:n], places[:n])
        _take_rows(d, outs[n:2 * n], places[:n])
        _take_rows(nm, outs[2 * n:3 * n], places[:n])
        _take_rows(nv, outs[3 * n:4 * n], places[:n])
        _take_rows(g, outs[4 * n:], places[n:])

    shapes = [jax.ShapeDtypeStruct(w.shape, F32) for w in ws]
    res = pl.pallas_call(
        body, name="adamw_small", out_shape=shapes * 4 + [jax.ShapeDtypeStruct(s, F32) for s in extra_shapes],
        scratch_shapes=[pltpu.VMEM((rows, SMALL_ROW), F32)] * 3, compiler_params=_params(),
    )(parts, *ws, *ms, *vs)
    return res[0:n], res[n:2 * n], res[2 * n:3 * n], res[3 * n:4 * n], res[4 * n:]


WEIGHTS = ['g_mix', 'w_in', 'rw_mu', 'rw_w0', 'rw_w_up', 'rw_a0', 'rw_a_up', 'rw_g_up', 'rw_k_k', 'rw_k_a',
           'rw_r_k', 'rw_ln_g', 'rw_ln_b', 'w_branch_a', 'w_branch_b', 'w_gate', 'b_gate', 'w_out', 'g_ffn', 'w_up',
           'conv_w', 'conv_b', 'w_down', 'g_ple', 'w_ple_gate', 'w_ple', 'g_final']
ARG_NAMES = (['x', 'p'] + WEIGHTS + ['loss_target'] + ['m_' + n for n in WEIGHTS] + ['v_' + n for n in WEIGHTS])
SHARDED = {'w_in': 1, 'rw_w_up': 1, 'rw_a_up': 1, 'rw_g_up': 1, 'w_branch_a': 1, 'w_branch_b': 1, 'w_gate': 1,
           'w_out': 0, 'w_up': 1, 'conv_w': 1, 'w_down': 0, 'w_ple_gate': 0, 'w_ple': 1}
SMALL = [n for n in WEIGHTS if n not in SHARDED]
WHOLE = ['conv_w']
FIRST_USED = ['w_in', 'rw_w_up', 'rw_a_up', 'rw_g_up', 'w_gate']
READ_BY_CHIP = ['w_gate', 'w_branch_a', 'w_branch_b', 'w_up', 'w_ple']
FIRST_DONE = [['w_up', 'w_down', 'w_ple_gate', 'w_ple'], ['w_out', 'w_branch_a', 'w_branch_b', 'w_gate'],
              ['w_in', 'rw_w_up', 'rw_a_up', 'rw_g_up']]
SPLIT = [n for n in SHARDED if n not in WHOLE]


def _full_from_shards(stack, axis):
    _, R, C = stack.shape
    if axis == 0:
        return stack.reshape(N_CHIPS * R, C)
    return stack.transpose(1, 0, 2).reshape(R, N_CHIPS * C)


def _shards_from_full(full, axis):
    R, C = full.shape
    if axis == 0:
        return full.reshape(N_CHIPS, R // N_CHIPS, C)
    return full.reshape(R, N_CHIPS, C // N_CHIPS).transpose(1, 0, 2)


def kernel(x, p, g_mix, w_in, rw_mu, rw_w0, rw_w_up, rw_a0, rw_a_up, rw_g_up, rw_k_k, rw_k_a, rw_r_k, rw_ln_g, rw_ln_b, w_branch_a, w_branch_b, w_gate, b_gate, w_out, g_ffn, w_up, conv_w, conv_b, w_down, g_ple, w_ple_gate, w_ple, g_final, loss_target, m_g_mix, m_w_in, m_rw_mu, m_rw_w0, m_rw_w_up, m_rw_a0, m_rw_a_up, m_rw_g_up, m_rw_k_k, m_rw_k_a, m_rw_r_k, m_rw_ln_g, m_rw_ln_b, m_w_branch_a, m_w_branch_b, m_w_gate, m_b_gate, m_w_out, m_g_ffn, m_w_up, m_conv_w, m_conv_b, m_w_down, m_g_ple, m_w_ple_gate, m_w_ple, m_g_final, v_g_mix, v_w_in, v_rw_mu, v_rw_w0, v_rw_w_up, v_rw_a0, v_rw_a_up, v_rw_g_up, v_rw_k_k, v_rw_k_a, v_rw_r_k, v_rw_ln_g, v_rw_ln_b, v_w_branch_a, v_w_branch_b, v_w_gate, v_b_gate, v_w_out, v_g_ffn, v_w_up, v_conv_w, v_conv_b, v_w_down, v_g_ple, v_w_ple_gate, v_w_ple, v_g_final):
    given = dict(zip(ARG_NAMES, (x, p, g_mix, w_in, rw_mu, rw_w0, rw_w_up, rw_a0, rw_a_up, rw_g_up, rw_k_k, rw_k_a, rw_r_k, rw_ln_g, rw_ln_b, w_branch_a, w_branch_b, w_gate, b_gate, w_out, g_ffn, w_up, conv_w, conv_b, w_down, g_ple, w_ple_gate, w_ple, g_final, loss_target, m_g_mix, m_w_in, m_rw_mu, m_rw_w0, m_rw_w_up, m_rw_a0, m_rw_a_up, m_rw_g_up, m_rw_k_k, m_rw_k_a, m_rw_r_k, m_rw_ln_g, m_rw_ln_b, m_w_branch_a, m_w_branch_b, m_w_gate, m_b_gate, m_w_out, m_g_ffn, m_w_up, m_conv_w, m_conv_b, m_w_down, m_g_ple, m_w_ple_gate, m_w_ple, m_g_final, v_g_mix, v_w_in, v_rw_mu, v_rw_w0, v_rw_w_up, v_rw_a0, v_rw_a_up, v_rw_g_up, v_rw_k_k, v_rw_k_a, v_rw_r_k, v_rw_ln_g, v_rw_ln_b, v_w_branch_a, v_w_branch_b, v_w_gate, v_b_gate, v_w_out, v_g_ffn, v_w_up, v_conv_w, v_conv_b, v_w_down, v_g_ple, v_w_ple_gate, v_w_ple, v_g_final), strict=True))

    def two_d(name, prefix=""):
        a = given[prefix + name]
        if name == "g_final":
            return a.reshape(1, D_MODEL)
        if name == "rw_r_k":
            return a.reshape(1, RW_WIDTH)
        return a[0] if a.ndim == 3 else a

    cast = lambda n: two_d(n) if n in WHOLE else two_d(n).astype(BF16)
    whole = lambda names, stacks: {n: g if n in READ_BY_CHIP else _full_from_shards(g, SHARDED[n])
                                   for n, g in zip(names, stacks, strict=True)}
    late_names = [n for n in SHARDED if n not in FIRST_USED]
    late_sends, late_recvs, late_srcs, late_lands, token = _travel_start(
        "gather_late_start", "gather", [cast(n) for n in late_names])
    W = whole(FIRST_USED, _gather_chips([cast(n) for n in FIRST_USED]))
    for n in SMALL:
        W[n] = two_d(n)
    W["rw_r_k"] = W["rw_r_k"].reshape(RW_HEADS, RW_HEAD_DIM)
    W["g_mix"] = W["g_mix"] + token[0:1, 0:1]

    def late_weights(after):
        lands = _travel_wait("gather_late_wait", "gather", late_sends, late_recvs, late_srcs, late_lands, after)
        return whole(late_names, _share_halves("share_late", lands))

    early_names = [[n for n in SPLIT if n in group] for group in FIRST_DONE]
    assert sorted(sum(early_names, [])) == sorted(SPLIT)
    travelling = []

    def early_grads(G, stage):
        by_chip = [G[n] if n in READ_BY_CHIP else _shards_from_full(G[n], SHARDED[n]) for n in early_names[stage]]
        sends, recvs, srcs, lands, started = _travel_start(f"scatter{stage}_start", "scatter", by_chip)
        travelling.append((sends, recvs, srcs, lands))
        return started

    loss_part, grad_x, G = _local_step(x[0], p[0, 0], W, loss_target[0], late_weights, early_grads, by_chip=True,
                                       grad_dtype=BF16)

    landed = {}
    for stage, (sends, recvs, srcs, lands) in enumerate(travelling):
        landed.update(zip(early_names[stage], _travel_wait(f"scatter{stage}_wait", "scatter", sends, recvs, srcs,
                                                           lands, grad_x), strict=True))
    reduced = [_sum_devices("sum_devices_" + n, landed[n]) for n in SPLIT]
    shard_grads = dict(zip(SPLIT, _join_halves(reduced), strict=True))

    G["rw_r_k"] = G["rw_r_k"].reshape(1, RW_WIDTH)
    extras = [G[n] for n in WHOLE] + [loss_part]
    all_small = _gather_all(_pack_small([G[n] for n in SMALL] + extras))
    gs, ds, nms, nvs, summed = _adamw_small(all_small, [two_d(n) for n in SMALL], [two_d(n, "m_") for n in SMALL],
                                            [two_d(n, "v_") for n in SMALL], [e.shape for e in extras])
    loss = summed[-1][0, 0]
    chip = 2 * lax.axis_index("x") + lax.axis_index("y")
    for n, full in zip(WHOLE, summed[:-1], strict=True):
        width = two_d(n).shape[1]
        shard_grads[n] = lax.dynamic_slice_in_dim(full, chip * width, width, axis=1)

    grads, deltas, new_m, new_v = {}, {}, {}, {}
    for n in SHARDED:
        g = shard_grads[n]
        d, nm, nv = _adamw("adamw_" + n, two_d(n), g, two_d(n, "m_"), two_d(n, "v_"))
        grads[n], deltas[n], new_m[n], new_v[n] = g, d, nm, nv
    for i, n in enumerate(SMALL):
        grads[n], deltas[n], new_m[n], new_v[n] = gs[i], ds[i], nms[i], nvs[i]
    outs = [loss, grad_x[None]]
    for table in (grads, deltas, new_m, new_v):
        outs += [table[n].reshape(given[n].shape) for n in WEIGHTS]
    return tuple(outs)
```

```python
import functools
import math

import jax
import jax.numpy as jnp
import numpy as np
from jax import lax
from jax.experimental import pallas as pl
from jax.experimental.pallas import tpu as pltpu

F32 = jnp.float32
BF16 = jnp.bfloat16

D_MODEL = 1024
NORM_EPS = 1e-6
RW_HEADS = 8
RW_HEAD_DIM = 64
RW_WIDTH = 512
RW_LN_EPS = 64e-5
ATT_GROUP_DILATION = (1, 4, 16)
ATT_BLOCK = 128
ATT_HEADS = 12
ATT_HEAD_DIM = 64
ATT_GROUP_WIDTH = 256
ATT_WIDTH = 768
D_FF = 3072

ADAM_LR = 0.001
ADAM_B1 = 0.9
ADAM_B2 = 0.999
ADAM_EPS = 1e-08
ADAM_WD = 0.01
ADAM_STEP = 10

SUBLANES = 8
LANES = 128
VMEM_LIMIT = 56 * 1024 * 1024
N_CHIPS = 4
N_DEV = 8
MESH = pl.DeviceIdType.MESH


def _params(sem=None):
    return pltpu.CompilerParams(dimension_semantics=sem, vmem_limit_bytes=VMEM_LIMIT)


def _pick(dim, pref):
    if dim % LANES != 0 or dim <= pref:
        return dim
    best = LANES
    for t in range(LANES, pref + 1, LANES):
        if dim % t == 0:
            best = t
    return best


def _mm(name, a, b, mode, out_dtype=F32, add=None, tm=1024, tn=1024, tk=1024, out_by_chip=False, post=None):
    by_chip = b.ndim == 3
    b_rows, b_cols = (b.shape[1], N_CHIPS * b.shape[2]) if by_chip else b.shape
    if mode == "nn":
        (M, K), (K2, N) = a.shape, (b_rows, b_cols)
    elif mode == "nt":
        (M, K), (N, K2) = a.shape, (b_rows, b_cols)
    else:
        (K, M), (K2, N) = a.shape, (b_rows, b_cols)
    assert K == K2, (name, a.shape, b.shape, mode)
    assert not (by_chip and mode == "tn") and not (out_by_chip and add is not None), name
    tm = _pick(M, tm)
    n_cut, k_cut = out_by_chip or (by_chip and mode == "nn"), by_chip and mode == "nt"
    tn = _pick(N // N_CHIPS, tn) if n_cut else _pick(N, tn)
    tk = _pick(K // N_CHIPS, tk) if k_cut else _pick(K, tk)
    nk = K // tk
    per_n = (N // N_CHIPS) // tn if n_cut else 1
    per_k = (K // N_CHIPS) // tk if k_cut else 1
    if mode == "nn":
        a_spec = pl.BlockSpec((tm, tk), lambda i, j, k: (i, k))
        b_spec = (pl.BlockSpec((None, tk, tn), lambda i, j, k: (j // per_n, k, j % per_n)) if by_chip
                  else pl.BlockSpec((tk, tn), lambda i, j, k: (k, j)))
        dims = (((1,), (0,)), ((), ()))
    elif mode == "nt":
        a_spec = pl.BlockSpec((tm, tk), lambda i, j, k: (i, k))
        b_spec = (pl.BlockSpec((None, tn, tk), lambda i, j, k: (k // per_k, j, k % per_k)) if by_chip
                  else pl.BlockSpec((tn, tk), lambda i, j, k: (j, k)))
        dims = (((1,), (1,)), ((), ()))
    else:
        a_spec = pl.BlockSpec((tk, tm), lambda i, j, k: (k, i))
        b_spec = pl.BlockSpec((tk, tn), lambda i, j, k: (k, j))
        dims = (((0,), (0,)), ((), ()))
    if out_by_chip:
        o_spec = pl.BlockSpec((None, tm, tn), lambda i, j, k: (j // per_n, i, j % per_n))
        out_shape = jax.ShapeDtypeStruct((N_CHIPS, M, N // N_CHIPS), out_dtype)
    else:
        o_spec = pl.BlockSpec((tm, tn), lambda i, j, k: (i, j))
        out_shape = jax.ShapeDtypeStruct((M, N), out_dtype)
    has_add = add is not None
    ins = [a, b] + ([add] if has_add else [])
    in_specs = [a_spec, b_spec] + ([o_spec] if has_add else [])
    n_main = len(ins)
    semantics = ("parallel", "parallel", "arbitrary")
    if post is not None:
        post_fn, post_rows, post_consts, post_outs = post
        assert tn == N and not out_by_chip, name
        ins += list(post_rows) + list(post_consts)
        in_specs += [pl.BlockSpec((tm, r.shape[1]), lambda i, j, k: (i, 0)) for r in post_rows]
        in_specs += [pl.BlockSpec(c.shape, lambda i, j, k, nd=c.ndim: (0,) * nd) for c in post_consts]
        o_spec = [pl.BlockSpec((tm, o[1]), lambda i, j, k: (i, 0)) if o[0] == "row"
                  else pl.BlockSpec(o[1], lambda i, j, k: (0, 0)) for o in post_outs]
        out_shape = [jax.ShapeDtypeStruct((M, o[1]), o[2]) if o[0] == "row" else jax.ShapeDtypeStruct(o[1], F32)
                     for o in post_outs]
        if any(o[0] == "acc" for o in post_outs):
            semantics = ("arbitrary", "arbitrary", "arbitrary")
    n_in = len(ins)

    def body(*refs):
        a_ref, b_ref = refs[:2]
        out_refs, acc_ref = refs[n_in:-1], refs[-1]
        i, k = pl.program_id(0), pl.program_id(2)
        part = lax.dot_general(a_ref[...].astype(BF16), b_ref[...].astype(BF16), dims,
                               preferred_element_type=F32)

        @pl.when(k == 0)
        def _():
            acc_ref[...] = part

        @pl.when(k > 0)
        def _():
            acc_ref[...] += part

        @pl.when(k == nk - 1)
        def _():
            res = acc_ref[...]
            if has_add:
                res = res + refs[2][...].astype(F32)
            if post is None:
                out_refs[0][...] = res.astype(out_refs[0].dtype)
                return
            n_rows = len(post_rows)
            vals = post_fn(res, [r[...] for r in refs[n_main:n_main + n_rows]],
                           [c[...] for c in refs[n_main + n_rows:n_in]])
            for o, o_ref, val in zip(post_outs, out_refs, vals, strict=True):
                if o[0] == "row":
                    o_ref[...] = val.astype(o_ref.dtype)
                else:
                    @pl.when(i == 0)
                    def _(o_ref=o_ref, val=val):
                        o_ref[...] = val.astype(F32)

                    @pl.when(i > 0)
                    def _(o_ref=o_ref, val=val):
                        o_ref[...] += val.astype(F32)

    return pl.pallas_call(
        body, name=name, grid=(M // tm, N // tn, nk),
        in_specs=in_specs, out_specs=o_spec, out_shape=out_shape,
        scratch_shapes=[pltpu.VMEM((tm, tn), F32)],
        compiler_params=_params(semantics),
    )(*ins)


def _rowwise(name, fn, T, tT, rows=(), prevs=(), nexts=(), consts=(), outs=()):
    n = T // tT
    per8 = tT // SUBLANES
    in_specs, ins = [], []
    for arr in rows:
        in_specs.append(pl.BlockSpec((tT, arr.shape[1]), lambda i: (i, 0)))
        ins.append(arr)
    for arr in prevs:
        in_specs.append(pl.BlockSpec((SUBLANES, arr.shape[1]), lambda i: (jnp.maximum(i * per8 - 1, 0), 0)))
        ins.append(arr)
    for arr in nexts:
        in_specs.append(pl.BlockSpec((SUBLANES, arr.shape[1]),
                                     lambda i: (jnp.minimum((i + 1) * per8, T // SUBLANES - 1), 0)))
        ins.append(arr)
    for arr in consts:
        in_specs.append(pl.BlockSpec(arr.shape, lambda i, nd=arr.ndim: (0,) * nd))
        ins.append(arr)
    out_specs, out_shapes = [], []
    for o in outs:
        if o[0] == "row":
            out_specs.append(pl.BlockSpec((tT, o[1]), lambda i: (i, 0)))
            out_shapes.append(jax.ShapeDtypeStruct((T, o[1]), o[2]))
        else:
            out_specs.append(pl.BlockSpec(o[1], lambda i: (0, 0)))
            out_shapes.append(jax.ShapeDtypeStruct(o[1], F32))
    nr, npv, nnx, nc = len(rows), len(prevs), len(nexts), len(consts)
    n_in = nr + npv + nnx + nc

    def body(*refs):
        i = pl.program_id(0)
        vals = [r[...] for r in refs[:n_in]]
        res = fn(i, n, vals[:nr], vals[nr:nr + npv], vals[nr + npv:nr + npv + nnx], vals[nr + npv + nnx:])
        for o, o_ref, val in zip(outs, refs[n_in:], res, strict=True):
            if o[0] == "row":
                o_ref[...] = val.astype(o_ref.dtype)
            else:
                @pl.when(i == 0)
                def _(o_ref=o_ref, val=val):
                    o_ref[...] = val.astype(F32)

                @pl.when(i > 0)
                def _(o_ref=o_ref, val=val):
                    o_ref[...] += val.astype(F32)

    res = pl.pallas_call(
        body, name=name, grid=(n,), in_specs=in_specs, out_specs=out_specs, out_shape=out_shapes,
        compiler_params=_params(("arbitrary",)),
    )(*ins)
    return list(res)


def _shift_down(x, prev8, i, s):
    rolled = pltpu.roll(x, s, 0)
    head = pltpu.roll(prev8, s, 0)
    head = jnp.where(i == 0, jnp.zeros_like(head), head)
    rid = lax.broadcasted_iota(jnp.int32, head.shape, 0)
    first = jnp.where(rid < s, head, rolled[:SUBLANES])
    if x.shape[0] == SUBLANES:
        return first
    return jnp.concatenate([first, rolled[SUBLANES:]], axis=0)


def _shift_up(x, next8, i, n, s):
    tT = x.shape[0]
    rolled = pltpu.roll(x, tT - s, 0)
    tail = pltpu.roll(next8, SUBLANES - s, 0)
    tail = jnp.where(i == n - 1, jnp.zeros_like(tail), tail)
    rid = lax.broadcasted_iota(jnp.int32, tail.shape, 0)
    last = jnp.where(rid >= SUBLANES - s, tail, rolled[tT - SUBLANES:])
    return jnp.concatenate([rolled[:tT - SUBLANES], last], axis=0)


def _colsum(x):
    return jnp.sum(x, axis=0, keepdims=True)


def _segsum(x, bd):
    return jnp.dot(x, bd, precision=lax.Precision.HIGH, preferred_element_type=F32)


def _block_diag_ones(width, seg):
    idx = np.arange(width) // seg
    return jnp.asarray((idx[:, None] == idx[None, :]).astype(np.float32))


def _sigmoid(z):
    return 1.0 / (1.0 + jnp.exp(-z))


def _softplus(z):
    return jnp.maximum(z, 0.0) + jnp.log(1.0 + jnp.exp(-jnp.abs(z)))


def _rms_fwd(x, g):
    r = lax.rsqrt(jnp.mean(x * x, axis=-1, keepdims=True) + NORM_EPS)
    return x * r * g


def _rms_bwd(x, g, dy):
    r = lax.rsqrt(jnp.mean(x * x, axis=-1, keepdims=True) + NORM_EPS)
    gdy = dy * g
    dx = r * (gdy - x * (r * r) * jnp.mean(x * gdy, axis=-1, keepdims=True))
    return dx, dy * x * r


GELU_C = math.sqrt(2.0 / math.pi)


def _gelu(x):
    return 0.5 * x * (1.0 + jnp.tanh(GELU_C * (x + 0.044715 * x * x * x)))


def _gelu_and_grad(x):
    th = jnp.tanh(GELU_C * (x + 0.044715 * x * x * x))
    half = 0.5 * (1.0 + th)
    return x * half, half + 0.5 * x * (1.0 - th * th) * GELU_C * (1.0 + 3.0 * 0.044715 * x * x)


RW_CHUNK = 64
NN = (((1,), (0,)), ((), ()))
NT = (((1,), (1,)), ((), ()))
TN = (((0,), (0,)), ((), ()))


def _hdot(a, b, dims):
    return lax.dot_general(a, b, dims, precision=lax.Precision.HIGH, preferred_element_type=F32)


def _ldot(a, b, dims):
    return lax.dot_general(a.astype(BF16), b.astype(BF16), dims, preferred_element_type=F32)


def _chunk_masks():
    ti = lax.broadcasted_iota(jnp.int32, (RW_CHUNK, RW_CHUNK), 0)
    tj = lax.broadcasted_iota(jnp.int32, (RW_CHUNK, RW_CHUNK), 1)
    return tj <= ti, tj < ti, (ti == tj).astype(F32)


def _head(x, h):
    return x[:, h * RW_HEAD_DIM:(h + 1) * RW_HEAD_DIM]


def _heads(fn):
    return [fn(h) for h in range(RW_HEADS)]


def _chunk_rows(r, lw, k, a, b, incl_f):
    c = _hdot(incl_f, lw, NN)
    e_prev, e_neg, e_pos = jnp.exp(c - lw), jnp.exp(-c), jnp.exp(c)
    return dict(At=a * e_prev, Bt=b * e_neg, Kt=k * e_neg, Rt=r * e_pos, e_prev=e_prev, e_neg=e_neg, e_pos=e_pos)


def _stack(top, bottom, h):
    return jnp.concatenate([_head(top, h), _head(bottom, h)], axis=0)


def _chunk_coeffs(q, incl, strict):
    C = RW_CHUNK
    ar = _heads(lambda h: _stack(q["At"], q["Rt"], h))
    pb = _heads(lambda h: _hdot(ar[h], _head(q["Bt"], h), NT))
    pk = _heads(lambda h: _hdot(ar[h], _head(q["Kt"], h), NT))
    A1, W1 = [jnp.where(strict, m[:C], 0.0) for m in pb], [jnp.where(incl, m[C:], 0.0) for m in pb]
    A2, W2 = [jnp.where(strict, m[:C], 0.0) for m in pk], [jnp.where(incl, m[C:], 0.0) for m in pk]
    return A1, A2, W1, W2


def _rwkv_chunk_prep(r, lw, k, a, b, v):
    T = r.shape[0]
    nC = T // RW_CHUNK
    H, N = RW_HEADS, RW_HEAD_DIM

    def body(r_ref, lw_ref, k_ref, a_ref, b_ref, v_ref,
             at_ref, bt_ref, kt_ref, rt_ref, a2v_ref, w2v_ref, ti_ref, w1_ref, a2_ref, w2_ref, pl_ref):
        incl, strict, eye = _chunk_masks()
        q = _chunk_rows(r_ref[...], lw_ref[...], k_ref[...], a_ref[...], b_ref[...], incl.astype(F32))
        at_ref[...], bt_ref[...], kt_ref[...], rt_ref[...] = q["At"], q["Bt"], q["Kt"], q["Rt"]
        pl_ref[0] = jnp.broadcast_to(q["e_pos"][RW_CHUNK - 1:RW_CHUNK, :], (SUBLANES, RW_WIDTH))
        A1, A2, W1, W2 = _chunk_coeffs(q, incl, strict)
        V = v_ref[...]
        a2v_ref[...] = jnp.concatenate(_heads(lambda h: _hdot(A2[h], _head(V, h), NN)), axis=1)
        w2v_ref[...] = jnp.concatenate(_heads(lambda h: _ldot(W2[h], _head(V, h), NN)), axis=1)
        tinv, pw = [eye + m for m in A1], A1
        for stage in range(5):
            dot = _hdot if stage == 0 else _ldot
            pw = [dot(m, m, NN) for m in pw]
            tinv = [t + dot(t, m, NN) for t, m in zip(tinv, pw, strict=True)]
        for h in range(H):
            ti_ref[0, h] = tinv[h]
            w1_ref[0, h] = W1[h]
            a2_ref[0, h] = A2[h]
            w2_ref[0, h] = W2[h]

    row_spec = pl.BlockSpec((RW_CHUNK, RW_WIDTH), lambda n: (n, 0))
    st_spec = pl.BlockSpec((1, H, N, N), lambda n: (n, 0, 0, 0))
    row_shape = jax.ShapeDtypeStruct((T, RW_WIDTH), F32)
    st_shape = jax.ShapeDtypeStruct((nC, H, N, N), F32)
    return pl.pallas_call(
        body, name="rwkv_chunk_prep", grid=(nC,),
        in_specs=[row_spec] * 6,
        out_specs=[row_spec] * 6 + [st_spec] * 4 + [pl.BlockSpec((1, SUBLANES, RW_WIDTH), lambda n: (n, 0, 0))],
        out_shape=[row_shape] * 6 + [st_shape] * 4 + [jax.ShapeDtypeStruct((nC, SUBLANES, RW_WIDTH), F32)],
        compiler_params=_params(("parallel",)),
    )(r, lw, k, a, b, v)


def _rwkv_chunk_fwd(v, at, bt, kt, rt, a2v, w2v, tinv, w1, plast):
    T = v.shape[0]
    nC = T // RW_CHUNK
    H, N = RW_HEADS, RW_HEAD_DIM

    def body(v_ref, at_ref, bt_ref, kt_ref, rt_ref, a2v_ref, w2v_ref, ti_ref, w1_ref, pl_ref,
             y_ref, sa_ref, s0_ref, S_ref):
        @pl.when(pl.program_id(0) == 0)
        def _():
            S_ref[...] = jnp.zeros_like(S_ref)

        V, At, Bt, Kt, Rt = v_ref[...], at_ref[...], bt_ref[...], kt_ref[...], rt_ref[...]
        A2V, W2V, p_last = a2v_ref[...], w2v_ref[...], pl_ref[0, 0:1, :]
        S0 = _heads(lambda h: S_ref[h])
        for h in range(H):
            s0_ref[0, h] = S0[h]
        C = RW_CHUNK
        on_state = _heads(lambda h: _hdot(_stack(At, Rt, h), S0[h], NT))
        Sa = _heads(lambda h: _hdot(ti_ref[0, h], on_state[h][:C] + _head(A2V, h), NN))
        X = _heads(lambda h: S0[h] + _hdot(jnp.concatenate([Sa[h], _head(V, h)], axis=0), _stack(Bt, Kt, h), TN))
        for h in range(H):
            S_ref[h] = X[h] * _head(p_last, h)
        Y = _heads(lambda h: on_state[h][C:] + _ldot(w1_ref[0, h], Sa[h], NN) + _head(W2V, h))
        y_ref[...] = jnp.concatenate(Y, axis=1)
        sa_ref[...] = jnp.concatenate(Sa, axis=1)

    row_spec = pl.BlockSpec((RW_CHUNK, RW_WIDTH), lambda n: (n, 0))
    st_spec = pl.BlockSpec((1, H, N, N), lambda n: (n, 0, 0, 0))
    row_shape = jax.ShapeDtypeStruct((T, RW_WIDTH), F32)
    return pl.pallas_call(
        body, name="rwkv_chunk_fwd", grid=(nC,),
        in_specs=[row_spec] * 7 + [st_spec, st_spec, pl.BlockSpec((1, SUBLANES, RW_WIDTH), lambda n: (n, 0, 0))],
        out_specs=[row_spec, row_spec, st_spec],
        out_shape=[row_shape, row_shape, jax.ShapeDtypeStruct((nC, H, N, N), F32)],
        scratch_shapes=[pltpu.VMEM((H, N, N), F32)],
        compiler_params=_params(("arbitrary",)),
    )(v, at, bt, kt, rt, a2v, w2v, tinv, w1, plast)


def _rwkv_chunk_bwd(r, lw, k, a, b, v, dy, s0, tinv, w1, a2, w2, sa):
    T = r.shape[0]
    nC = T // RW_CHUNK
    H, N = RW_HEADS, RW_HEAD_DIM

    def body(r_ref, lw_ref, k_ref, a_ref, b_ref, v_ref, dy_ref, s0_ref, ti_ref, w1_ref, a2_ref, w2_ref, sa_ref,
             dr_ref, dlw_ref, dk_ref, da_ref, db_ref, dv_ref, dS_ref):
        @pl.when(pl.program_id(0) == 0)
        def _():
            dS_ref[...] = jnp.zeros_like(dS_ref)

        incl, strict, _ = _chunk_masks()
        incl_f = incl.astype(F32)
        q = _chunk_rows(r_ref[...], lw_ref[...], k_ref[...], a_ref[...], b_ref[...], incl_f)
        At, Bt, Kt, Rt = q["At"], q["Bt"], q["Kt"], q["Rt"]
        A2, W1, W2 = (_heads(lambda h, ref=ref: ref[0, h]) for ref in (a2_ref, w1_ref, w2_ref))
        V, dY, Sa = v_ref[...], dy_ref[...], sa_ref[...]
        hd = _head
        p_last = q["e_pos"][RW_CHUNK - 1:RW_CHUNK, :]
        S0 = _heads(lambda h: s0_ref[0, h])
        G = _heads(lambda h: dS_ref[h] * hd(p_last, h))
        C = RW_CHUNK
        AR = _heads(lambda h: _stack(At, Rt, h))
        BK = _heads(lambda h: _stack(Bt, Kt, h))
        X = _heads(lambda h: S0[h] + _ldot(_stack(Sa, V, h), BK[h], TN))
        dc_last = jnp.concatenate(_heads(lambda h: jnp.sum(G[h] * X[h], axis=0, keepdims=True)), axis=1)
        dSa = _heads(lambda h: _ldot(hd(Bt, h), G[h], NT) + _ldot(W1[h], hd(dY, h), TN))
        dZ = _heads(lambda h: _ldot(ti_ref[0, h], dSa[h], TN))
        D = _heads(lambda h: jnp.concatenate([dZ[h], hd(dY, h)], axis=0))
        for h in range(H):
            dS_ref[h] = G[h] + _ldot(D[h], AR[h], TN)
        both = jnp.concatenate([strict, incl], axis=0)
        E1 = _heads(lambda h: jnp.where(both, _ldot(D[h], hd(Sa, h), NT), 0.0))
        E2 = _heads(lambda h: jnp.where(both, _ldot(D[h], hd(V, h), NT), 0.0))
        cat = lambda fn: jnp.concatenate(_heads(fn), axis=1)
        dV = cat(lambda h: _ldot(jnp.concatenate([A2[h], W2[h]], axis=0), D[h], TN) + _ldot(hd(Kt, h), G[h], NT))
        dAR = _heads(lambda h: _ldot(E1[h], hd(Bt, h), NN) + _ldot(E2[h], hd(Kt, h), NN) + _ldot(D[h], S0[h], NN))
        dAt, dRt = cat(lambda h: dAR[h][:C]), cat(lambda h: dAR[h][C:])
        dBt = cat(lambda h: _ldot(E1[h], AR[h], TN) + _ldot(hd(Sa, h), G[h], NN))
        dKt = cat(lambda h: _ldot(E2[h], AR[h], TN) + _ldot(hd(V, h), G[h], NN))
        last_row = lax.broadcasted_iota(jnp.int32, (RW_CHUNK, RW_WIDTH), 0) == RW_CHUNK - 1
        dc_prev = dAt * At
        dc = dc_prev + dRt * Rt - dBt * Bt - dKt * Kt + jnp.where(last_row, dc_last, 0.0)
        dr_ref[...] = dRt * q["e_pos"]
        dlw_ref[...] = _hdot(incl_f, dc, TN) - dc_prev
        dk_ref[...] = dKt * q["e_neg"]
        da_ref[...] = dAt * q["e_prev"]
        db_ref[...] = dBt * q["e_neg"]
        dv_ref[...] = dV

    rev = lambda n: nC - 1 - n
    row_spec = pl.BlockSpec((RW_CHUNK, RW_WIDTH), lambda n: (rev(n), 0))
    st_spec = pl.BlockSpec((1, H, N, N), lambda n: (rev(n), 0, 0, 0))
    row_shape = jax.ShapeDtypeStruct((T, RW_WIDTH), F32)
    return pl.pallas_call(
        body, name="rwkv_chunk_bwd", grid=(nC,),
        in_specs=[row_spec] * 7 + [st_spec] * 5 + [row_spec], out_specs=[row_spec] * 6,
        out_shape=[row_shape] * 6, scratch_shapes=[pltpu.VMEM((H, N, N), F32)],
        compiler_params=_params(("arbitrary",)),
    )(r, lw, k, a, b, v, dy, s0, tinv, w1, a2, w2, sa)


def _alibi_slope(head):
    return float(np.float32(2.0 ** (-8.0 * (head + 1) / ATT_HEADS)))


ATT_SPAN = ATT_BLOCK * max(ATT_GROUP_DILATION)
ATT_PAIR_WIDTH = 2 * ATT_HEAD_DIM
ATT_SIDE_BY_SIDE = 16


def _pair_slope(g, hp, j):
    return jnp.where(hp == 0, _alibi_slope(4 * g + j), _alibi_slope(4 * g + 2 + j))


def _att_rows(mi, r, d):
    start = mi * ATT_BLOCK * d + r
    return pl.ds(start, ATT_BLOCK) if d == 1 else pl.ds(start, ATT_BLOCK, stride=d)


def _att_masks():
    qi = lax.broadcasted_iota(jnp.int32, (ATT_BLOCK, ATT_BLOCK), 0)
    kj = lax.broadcasted_iota(jnp.int32, (ATT_BLOCK, ATT_BLOCK), 1)
    return qi, kj


NEG = -1e30


def _att_logits(q, k, slope_d, steps, valid):
    s = lax.dot_general(q.astype(BF16), k.astype(BF16), (((1,), (1,)), ((), ())),
                        preferred_element_type=F32) * (ATT_HEAD_DIM ** -0.5)
    return jnp.where(valid, s - slope_d * steps.astype(F32), NEG)


def _att_fwd(p_att, g):
    T = p_att.shape[0]
    d = ATT_GROUP_DILATION[g]
    W = ATT_PAIR_WIDTH
    nb = T // ATT_SPAN
    mb = ATT_SPAN // (ATT_BLOCK * d)

    def body(q_ref, kc_ref, kp_ref, vc_ref, vp_ref, o_ref, l_ref):
        hp, n = pl.program_id(0), pl.program_id(1)
        qi, kj = _att_masks()
        slopes = [_pair_slope(g, hp, j) * d for j in range(2)]
        blocks = [(r, mi) for r in range(d) for mi in range(mb)]
        for at in range(0, len(blocks), ATT_SIDE_BY_SIDE):
            tasks = []
            for r, mi in blocks[at:at + ATT_SIDE_BY_SIDE]:
                rows = _att_rows(mi, r, d)
                if mi > 0:
                    prev = _att_rows(mi - 1, r, d)
                    kp, vp, has_prev = kc_ref[prev, :], vc_ref[prev, :], True
                else:
                    prev = _att_rows(mb - 1, r, d)
                    kp, vp, has_prev = kp_ref[prev, :], vp_ref[prev, :], n > 0
                q, kc, vc = q_ref[rows, :], kc_ref[rows, :], vc_ref[rows, :]
                for j in range(2):
                    sl = slice(j * ATT_HEAD_DIM, (j + 1) * ATT_HEAD_DIM)
                    tasks.append((q[:, sl], kc[:, sl], kp[:, sl], vc[:, sl], vp[:, sl], has_prev, slopes[j]))
            lc = [_att_logits(t[0], t[1], t[6], qi - kj, kj <= qi) for t in tasks]
            lp = [_att_logits(t[0], t[2], t[6], qi - kj + ATT_BLOCK, (kj >= qi) & t[5]) for t in tasks]
            mx = [jnp.maximum(jnp.max(a, axis=1, keepdims=True), jnp.max(b, axis=1, keepdims=True))
                  for a, b in zip(lc, lp, strict=True)]
            ec = [jnp.exp(a - m) for a, m in zip(lc, mx, strict=True)]
            ep = [jnp.exp(b - m) for b, m in zip(lp, mx, strict=True)]
            den = [jnp.sum(a, axis=1, keepdims=True) + jnp.sum(b, axis=1, keepdims=True)
                   for a, b in zip(ec, ep, strict=True)]
            inv = [1.0 / s for s in den]
            outs = [jnp.dot((a * i).astype(BF16), t[3].astype(BF16), preferred_element_type=F32)
                    + jnp.dot((b * i).astype(BF16), t[4].astype(BF16), preferred_element_type=F32)
                    for a, b, i, t in zip(ec, ep, inv, tasks, strict=True)]
            lses = [jnp.broadcast_to(m + jnp.log(s), (ATT_BLOCK, ATT_HEAD_DIM)) for m, s in zip(mx, den, strict=True)]
            for i, (r, mi) in enumerate(blocks[at:at + ATT_SIDE_BY_SIDE]):
                rows = _att_rows(mi, r, d)
                o_ref[rows, :] = jnp.concatenate(outs[2 * i:2 * i + 2], axis=1)
                l_ref[rows, :] = jnp.concatenate(lses[2 * i:2 * i + 2], axis=1)

    def spec(col0, prev):
        if prev:
            return pl.BlockSpec((ATT_SPAN, W), lambda hp, n: (jnp.maximum(n - 1, 0), col0 + 2 * g + hp))
        return pl.BlockSpec((ATT_SPAN, W), lambda hp, n: (n, col0 + 2 * g + hp))

    o_spec = pl.BlockSpec((ATT_SPAN, W), lambda hp, n: (n, hp))
    o, l = pl.pallas_call(
        body, name=f"att_fwd_g{g}", grid=(2, nb),
        in_specs=[spec(0, False), spec(6, False), spec(6, True), spec(12, False), spec(12, True)],
        out_specs=[o_spec, o_spec],
        out_shape=[jax.ShapeDtypeStruct((T, ATT_GROUP_WIDTH), F32)] * 2,
        compiler_params=_params(("parallel", "arbitrary")),
    )(p_att, p_att, p_att, p_att, p_att)
    return o, l


def _att_bwd(p_att, o, l, do, dl, g):
    T = p_att.shape[0]
    d = ATT_GROUP_DILATION[g]
    W = ATT_PAIR_WIDTH
    nb = T // ATT_SPAN
    mb = ATT_SPAN // (ATT_BLOCK * d)
    scale = ATT_HEAD_DIM ** -0.5

    def body(q_ref, k_ref, v_ref, o_ref, l_ref, do_ref, dl_ref,
             qn_ref, on_ref, ln_ref, don_ref, dln_ref, dq_ref, dk_ref, dv_ref, carry_ref):
        hp, n = pl.program_id(0), pl.program_id(1)
        qi, kj = _att_masks()

        @pl.when(n == 0)
        def _():
            carry_ref[...] = jnp.zeros_like(carry_ref)

        slopes = [_pair_slope(g, hp, j) * d for j in range(2)]
        blocks = [(r, mi) for r in range(d) for mi in range(mb)]
        side_by_side = ATT_SIDE_BY_SIDE // 2
        carry = None
        for at in range(0, len(blocks), side_by_side):
            tasks = []
            for r, mi in blocks[at:at + side_by_side]:
                rows = _att_rows(mi, r, d)
                if mi < mb - 1:
                    nrows = _att_rows(mi + 1, r, d)
                    nxt = (q_ref[nrows, :], o_ref[nrows, :], l_ref[nrows, :], do_ref[nrows, :], dl_ref[nrows, :])
                    has_next = True
                else:
                    nrows = _att_rows(0, r, d)
                    nxt = (qn_ref[nrows, :], on_ref[nrows, :], ln_ref[nrows, :], don_ref[nrows, :],
                           dln_ref[nrows, :])
                    has_next = n < nb - 1
                cur = (q_ref[rows, :], o_ref[rows, :], l_ref[rows, :], do_ref[rows, :], dl_ref[rows, :])
                k_all, v_all = k_ref[rows, :], v_ref[rows, :]
                for j in range(2):
                    sl = slice(j * ATT_HEAD_DIM, (j + 1) * ATT_HEAD_DIM)
                    for blk, steps, valid in ((cur, qi - kj, kj <= qi),
                                              (nxt, qi - kj + ATT_BLOCK, (kj >= qi) & has_next)):
                        q, o_, lse, do_, dlse = (z[:, sl] for z in blk)
                        tasks.append(dict(q=q, o=o_, lse=lse[:, :1], do=do_, dlse=dlse[:, :1], steps=steps,
                                          valid=valid, k=k_all[:, sl], vb=v_all[:, sl].astype(BF16),
                                          slope=slopes[j]))
            p = [jnp.exp(_att_logits(t["q"], t["k"], t["slope"], t["steps"], t["valid"]) - t["lse"]) for t in tasks]
            dp = [lax.dot_general(t["do"].astype(BF16), t["vb"], (((1,), (1,)), ((), ())),
                                  preferred_element_type=F32) for t in tasks]
            dsum = [jnp.sum(t["do"] * t["o"], axis=1, keepdims=True) for t in tasks]
            ds = [a * (b - s + t["dlse"]) for a, b, s, t in zip(p, dp, dsum, tasks, strict=True)]
            dv_ = [jnp.dot(a.T.astype(BF16), t["do"].astype(BF16), preferred_element_type=F32)
                   for a, t in zip(p, tasks, strict=True)]
            dk_ = [jnp.dot(a.T.astype(BF16), t["q"].astype(BF16), preferred_element_type=F32) * scale
                   for a, t in zip(ds, tasks, strict=True)]
            dq_ = [jnp.dot(a.astype(BF16), t["k"].astype(BF16), preferred_element_type=F32) * scale
                   for a, t in zip(ds, tasks, strict=True)]
            for i, (r, mi) in enumerate(blocks[at:at + side_by_side]):
                rows = _att_rows(mi, r, d)
                b = 4 * i
                if mi == 0:
                    carry = carry_ref[r]
                dq_ref[rows, :] = jnp.concatenate([dq_[b], dq_[b + 2]], axis=1) + carry
                carry = jnp.concatenate([dq_[b + 1], dq_[b + 3]], axis=1)
                if mi == mb - 1:
                    carry_ref[r] = carry
                dk_ref[rows, :] = jnp.concatenate([dk_[b] + dk_[b + 1], dk_[b + 2] + dk_[b + 3]], axis=1)
                dv_ref[rows, :] = jnp.concatenate([dv_[b] + dv_[b + 1], dv_[b + 2] + dv_[b + 3]], axis=1)

    head_rows = ATT_BLOCK * d
    nxt_n = lambda n: jnp.minimum((n + 1) * mb, T // head_rows - 1)
    cur_p = lambda col0: pl.BlockSpec((ATT_SPAN, W), lambda hp, n: (n, col0 + 2 * g + hp))
    cur_o = pl.BlockSpec((ATT_SPAN, W), lambda hp, n: (n, hp))
    nxt_o = pl.BlockSpec((head_rows, W), lambda hp, n: (nxt_n(n), hp))
    dq, dk, dv = pl.pallas_call(
        body, name=f"att_bwd_g{g}", grid=(2, nb),
        in_specs=[cur_p(0), cur_p(6), cur_p(12), cur_o, cur_o, cur_o, cur_o,
                  pl.BlockSpec((head_rows, W), lambda hp, n: (nxt_n(n), 2 * g + hp)), nxt_o, nxt_o, nxt_o, nxt_o],
        out_specs=[cur_o, cur_o, cur_o],
        out_shape=[jax.ShapeDtypeStruct((T, ATT_GROUP_WIDTH), F32)] * 3,
        scratch_shapes=[pltpu.VMEM((d, ATT_BLOCK, W), F32)],
        compiler_params=_params(("parallel", "arbitrary")),
    )(p_att, p_att, p_att, o, l, do, dl, p_att, o, l, do, dl)
    return dq, dk, dv


FFN_TILE = 2 * D_FF // N_CHIPS
RKV = 3 * RW_WIDTH
WA = 128
XG = 160
RW_COLS = RKV + WA + XG


def _local_step(x, p, W, target, late_weights=None, early_grads=None, by_chip=False, grad_dtype=F32):
    T = x.shape[0]
    tT = 256
    bd512 = _block_diag_ones(RW_WIDTH, RW_HEAD_DIM)
    bd256 = _block_diag_ones(ATT_GROUP_WIDTH, ATT_HEAD_DIM)
    G = {}
    W = dict(W)

    w_in = W["w_in"]
    w_rkv, w_wa, w_xg, w_att = (w_in[:, :RKV], w_in[:, RKV:RKV + WA], w_in[:, RKV + WA:RW_COLS],
                                w_in[:, RW_COLS:])
    mu = W["rw_mu"]
    mu_rkv, mu_wa, mu_xg = mu[:, :RKV], mu[:, RKV:RKV + WA], mu[:, RKV + WA:]
    zpad = jnp.zeros((64, RW_WIDTH), W["rw_w_up"].dtype)
    w_up_pad = jnp.concatenate([W["rw_w_up"], zpad], axis=0)
    a_up_pad = jnp.concatenate([zpad, W["rw_a_up"]], axis=0)
    r_k = W["rw_r_k"].reshape(1, RW_WIDTH)

    (h,) = _rowwise("norm_mix", lambda i, n, r, pv, nx, c: [_rms_fwd(r[0], c[0])], T, tT,
                    rows=[x], consts=[W["g_mix"]], outs=[("row", D_MODEL, BF16)])
    p_rkv = _mm("proj_rkv", h, w_rkv, "nn")
    p_wa = _mm("proj_wa", h, w_wa, "nn")
    p_xg = _mm("proj_xg", h, w_xg, "nn")
    p_att = _mm("proj_att", h, w_att, "nn", tn=768)
    z_gate = _mm("proj_gate", h, W["w_gate"], "nn")

    def rw_pre_core(i, rows, prevs, consts):
        prkv, pwa, pxg = rows[:3]
        (mrkv, mwa, mxg, w0, a0, k_k, k_a, wup, aup, gup, bd) = consts[:11]
        m_rkv = prkv + (_shift_down(prkv, prevs[0], i, 1) - prkv) * mrkv
        m_wa = pwa + (_shift_down(pwa, prevs[1], i, 1) - pwa) * mwa
        m_xg = pxg + (_shift_down(pxg, prevs[2], i, 1) - pxg) * mxg
        r, k, v = m_rkv[:, :RW_WIDTH], m_rkv[:, RW_WIDTH:2 * RW_WIDTH], m_rkv[:, 2 * RW_WIDTH:]
        tw = jnp.tanh(m_wa)
        lw = w0 + jnp.dot(tw.astype(BF16), wup.astype(BF16), preferred_element_type=F32)
        wlog = -_softplus(-lw) - 0.5
        log_decay = -jnp.exp(wlog)
        a = _sigmoid(a0 + jnp.dot(m_wa.astype(BF16), aup.astype(BF16), preferred_element_type=F32))
        sg = _sigmoid(m_xg)
        gate = jnp.dot(sg.astype(BF16), gup.astype(BF16), preferred_element_type=F32)
        kkp = k * k_k
        nrm = jnp.sqrt(_segsum(kkp * kkp, bd))
        nrm_c = jnp.maximum(nrm, 1e-12)
        kk = kkp / nrm_c
        k2 = k * (1.0 + (a - 1.0) * k_a)
        return dict(r=r, k=k, v=v, tw=tw, lw=lw, wlog=wlog, log_decay=log_decay, a=a, sg=sg, gate=gate, kkp=kkp,
                    nrm=nrm, nrm_c=nrm_c, kk=kk, k2=k2, m_rkv=m_rkv, m_wa=m_wa, m_xg=m_xg)

    pre_consts = [mu_rkv, mu_wa, mu_xg, W["rw_w0"], W["rw_a0"], W["rw_k_k"], W["rw_k_a"],
                  w_up_pad, a_up_pad, W["rw_g_up"], bd512]

    def rw_pre(i, n, rows, prevs, nexts, consts):
        q = rw_pre_core(i, rows, prevs, consts)
        return [q["r"], q["log_decay"], q["k2"], q["v"], -q["kk"], q["kk"] * q["a"], q["gate"]]

    r_s, w_s, k_s, v_s, a_s, b_s, gate_s = _rowwise(
        "rwkv_pre", rw_pre, T, tT, rows=[p_rkv, p_wa, p_xg], prevs=[p_rkv, p_wa, p_xg], consts=pre_consts,
        outs=[("row", RW_WIDTH, F32)] * 7)
    (at_s, bt_s, kt_s, rt_s, a2v_s, w2v_s, tinv_s, w1_s, a2_s, w2_s,
     plast_s) = _rwkv_chunk_prep(r_s, w_s, k_s, a_s, b_s, v_s)
    y_scan, sa_s, s0_s = _rwkv_chunk_fwd(v_s, at_s, bt_s, kt_s, rt_s, a2v_s, w2v_s, tinv_s, w1_s, plast_s)

    def rw_post_core(rows, consts):
        y, r, k2, v, gate = rows[:5]
        ln_g, ln_b, rk, bd = consts[:4]
        mean = _segsum(y, bd) * (1.0 / RW_HEAD_DIM)
        yc = y - mean
        var = _segsum(yc * yc, bd) * (1.0 / RW_HEAD_DIM)
        rstd = lax.rsqrt(var + RW_LN_EPS)
        yn = yc * rstd
        s = _segsum(r * k2 * rk, bd)
        return dict(yn=yn, rstd=rstd, s=s, pre=yn * ln_g + ln_b + s * v)

    post_consts = [W["rw_ln_g"], W["rw_ln_b"], r_k, bd512]
    (y_a,) = _rowwise("rwkv_post", lambda i, n, r, pv, nx, c: [rw_post_core(r, c)["pre"] * r[4]], T, tT,
                      rows=[y_scan, r_s, k_s, v_s, gate_s], consts=post_consts, outs=[("row", RW_WIDTH, BF16)])

    att = [_att_fwd(p_att, g) for g in range(3)]

    def comb_weights(ls):
        mx = jnp.maximum(jnp.maximum(ls[0], ls[1]), ls[2])
        es = [jnp.exp(l - mx) for l in ls]
        den = es[0] + es[1] + es[2]
        return [e / den for e in es]

    def att_comb(i, n, rows, pv, nx, c):
        wts = comb_weights(rows[3:6])
        return [wts[0] * rows[0] + wts[1] * rows[1] + wts[2] * rows[2]]

    (y_b,) = _rowwise("att_combine", att_comb, T, tT, rows=[att[0][0], att[1][0], att[2][0], att[0][1], att[1][1],
                                                            att[2][1]], outs=[("row", ATT_GROUP_WIDTH, BF16)])

    if late_weights is not None:
        W.update(late_weights(y_b))
    br_a = _mm("branch_a", y_a, W["w_branch_a"], "nn")
    br_b = _mm("branch_b", y_b, W["w_branch_b"], "nn")

    def merge(i, n, rows, pv, nx, c):
        gates = _sigmoid(rows[0] + c[0])
        return [gates[:, :D_MODEL] * rows[1] + gates[:, D_MODEL:] * rows[2]]

    (merged,) = _rowwise("merge", merge, T, tT, rows=[z_gate, br_a, br_b], consts=[W["b_gate"]],
                         outs=[("row", D_MODEL, BF16)])
    with_norm = lambda res, rows, consts: [res, _rms_fwd(res, consts[0])]
    stream_and_norm = [("row", D_MODEL, F32), ("row", D_MODEL, BF16)]
    x1, h2 = _mm("mix_out", merged, W["w_out"], "nn", add=x, post=(with_norm, [], [W["g_ffn"]], stream_and_norm))

    u = _mm("ffn_up", h2, W["w_up"], "nn", tn=FFN_TILE)

    def conv_core(i, rows, prevs, consts):
        uu, cw, cb = rows[0], consts[0], consts[1]
        u1 = _shift_down(uu, prevs[0], i, 1)
        u2 = _shift_down(uu, prevs[0], i, 2)
        uc = cb + cw[0:1] * uu + cw[1:2] * u1 + cw[2:3] * u2
        return uc[:, :D_FF], uc[:, D_FF:], u1, u2

    def glu(i, n, rows, prevs, nx, consts):
        gate, val, _, _ = conv_core(i, rows, prevs, consts)
        return [_gelu(gate) * val]

    tF = 128
    (act,) = _rowwise("conv_glu", glu, T, tF, rows=[u], prevs=[u], consts=[W["conv_w"], W["conv_b"]],
                      outs=[("row", D_FF, BF16)])
    x2, h3 = _mm("ffn_down", act, W["w_down"], "nn", add=x1, post=(with_norm, [], [W["g_ple"]], stream_and_norm))

    e_ple = _mm("ple_emb", p, W["w_ple"], "nn")

    def head(i, n, rows, pv, nx, consts):
        x2_, z, e, tgt = rows
        pg = _sigmoid(z)
        x3 = x2_ + pg * e
        y = _rms_fwd(x3, consts[0])
        err = y - tgt
        loss = 0.5 * jnp.sum(jnp.sum(err * err, axis=1, keepdims=True) * (1.0 / D_MODEL), axis=0, keepdims=True)
        dy = err * (1.0 / D_MODEL)
        dx3, dgf = _rms_bwd(x3, consts[0], dy)
        return [dx3, dx3 * pg, dx3 * e * pg * (1.0 - pg), jnp.broadcast_to(loss, (1, LANES)), _colsum(dgf)]

    dx3, de, dz, loss_acc, G["g_final"] = _mm(
        "ple_gate_loss_head", h3, W["w_ple_gate"], "nn", tm=512,
        post=(lambda res, rows, consts: head(0, 0, [rows[0], res, rows[1], rows[2]], [], [], consts),
              [x2, e_ple, target], [W["g_final"].reshape(1, D_MODEL)],
              [("row", D_MODEL, F32), ("row", D_MODEL, BF16), ("row", D_MODEL, BF16), ("acc", (1, LANES)),
               ("acc", (1, D_MODEL))]))
    G["w_ple"] = _mm("d_w_ple", p, de, "tn", grad_dtype, out_by_chip=by_chip)
    G["w_ple_gate"] = _mm("d_w_ple_gate", h3, dz, "tn", grad_dtype)
    def norm_bwd(i, n, rows, pv, nx, consts):
        dx, dg = _rms_bwd(rows[0], consts[0], rows[1])
        return [rows[2] + dx, _colsum(dg)]

    through_norm = lambda res, rows, consts: norm_bwd(0, 0, [rows[0], res, rows[1]], [], [], consts)
    stream_and_gain = [("row", D_MODEL, F32), ("acc", (1, D_MODEL))]
    dx2, G["g_ple"] = _mm("d_h3", dz, W["w_ple_gate"], "nt", tm=512,
                          post=(through_norm, [x2, dx3], [W["g_ple"]], stream_and_gain))

    dact = _mm("d_act", dx2, W["w_down"], "nt")
    G["w_down"] = _mm("d_w_down", act, dx2, "tn", grad_dtype)

    def glu_grad(gate, val, da):
        act_, slope = _gelu_and_grad(gate)
        return jnp.concatenate([da * val * slope, da * act_], axis=1)

    def glu_bwd(i, n, rows, prevs, nexts, consts):
        uu, da = rows
        cw = consts[0]
        gate, val, u1, u2 = conv_core(i, rows, prevs, consts)
        duc = glu_grad(gate, val, da)
        dcw = jnp.concatenate([_colsum(duc * uu), _colsum(duc * u1), _colsum(duc * u2)], axis=0)
        gate_n, val_n, _, _ = conv_core(1, [nexts[0]], [uu[tF - SUBLANES:]], consts)
        duc_n = glu_grad(gate_n, val_n, nexts[1])
        du = (cw[0:1] * duc + cw[1:2] * _shift_up(duc, duc_n, i, n, 1) + cw[2:3] * _shift_up(duc, duc_n, i, n, 2))
        return [du, _colsum(duc), dcw]

    du, G["conv_b"], G["conv_w"] = _rowwise(
        "d_conv_glu", glu_bwd, T, tF, rows=[u, dact], prevs=[u], nexts=[u, dact],
        consts=[W["conv_w"], W["conv_b"]],
        outs=[("row", 2 * D_FF, BF16), ("acc", (1, 2 * D_FF)), ("acc", (3, 2 * D_FF))])
    G["w_up"] = _mm("d_w_up", h2, du, "tn", grad_dtype, out_by_chip=by_chip, tn=FFN_TILE)
    dh2 = _mm("d_h2", du, W["w_up"], "nt", tk=FFN_TILE)
    dx1, G["g_ffn"] = _rowwise("d_norm_ffn", norm_bwd, T, tT, rows=[x1, dh2, dx2], consts=[W["g_ffn"]],
                               outs=[("row", D_MODEL, F32), ("acc", (1, D_MODEL))])

    b_gate = W["b_gate"]
    if early_grads is not None:
        b_gate = b_gate + early_grads(G, 0)[0:1, 0:1]
    G["w_out"] = _mm("d_w_out", merged, dx1, "tn", grad_dtype)

    def merge_bwd(dm, rows, consts):
        z, a_, b_ = rows
        gates = _sigmoid(z + consts[0])
        ga, gb = gates[:, :D_MODEL], gates[:, D_MODEL:]
        dz_ = jnp.concatenate([dm * a_ * ga * (1.0 - ga), dm * b_ * gb * (1.0 - gb)], axis=1)
        return [dm * ga, dm * gb, dz_, _colsum(dz_)]

    d_br_a, d_br_b, dz_gate, G["b_gate"] = _mm(
        "d_merged", dx1, W["w_out"], "nt", tm=512,
        post=(merge_bwd, [z_gate, br_a, br_b], [b_gate],
              [("row", D_MODEL, BF16), ("row", D_MODEL, BF16), ("row", 2 * D_MODEL, BF16),
               ("acc", (1, 2 * D_MODEL))]))
    G["w_branch_a"] = _mm("d_w_branch_a", y_a, d_br_a, "tn", grad_dtype, out_by_chip=by_chip)
    G["w_branch_b"] = _mm("d_w_branch_b", y_b, d_br_b, "tn", grad_dtype, out_by_chip=by_chip)
    G["w_gate"] = _mm("d_w_gate", h, dz_gate, "tn", grad_dtype, out_by_chip=by_chip)
    if early_grads is not None:
        post_consts = [post_consts[0] + early_grads(G, 1)[0:1, 0:1]] + post_consts[1:]
    dy_a = _mm("d_y_a", d_br_a, W["w_branch_a"], "nt")
    dy_b = _mm("d_y_b", d_br_b, W["w_branch_b"], "nt")

    def att_comb_bwd(i, n, rows, pv, nx, consts):
        os_, ls, dy = rows[0:3], rows[3:6], rows[6]
        wts = comb_weights(ls)
        dws = [_segsum(dy * o_, consts[0]) for o_ in os_]
        mix = wts[0] * dws[0] + wts[1] * dws[1] + wts[2] * dws[2]
        return [wts[g_] * dy for g_ in range(3)] + [wts[g_] * (dws[g_] - mix) for g_ in range(3)]

    comb = _rowwise("d_att_combine", att_comb_bwd, T, tT,
                    rows=[att[0][0], att[1][0], att[2][0], att[0][1], att[1][1], att[2][1], dy_b], consts=[bd256],
                    outs=[("row", ATT_GROUP_WIDTH, F32)] * 6)
    dqkv = [_att_bwd(p_att, att[g][0], att[g][1], comb[g], comb[3 + g], g) for g in range(3)]
    dp_att = jnp.concatenate([dqkv[g][part] for part in range(3) for g in range(3)], axis=1).astype(BF16)

    def rw_post_bwd(i, n, rows, pv, nx, consts):
        y, r, k2, v, gate, dya = rows
        ln_g, ln_b, rk, bd = consts
        q = rw_post_core(rows, consts)
        dpre = dya * gate
        dgate = dya * q["pre"]
        dyn = dpre * ln_g
        inv = 1.0 / RW_HEAD_DIM
        dy_scan = q["rstd"] * (dyn - _segsum(dyn, bd) * inv - q["yn"] * (_segsum(dyn * q["yn"], bd) * inv))
        ds = _segsum(dpre * v, bd)
        return [dy_scan, dgate, ds * k2 * rk, ds * r * rk, dpre * q["s"],
                _colsum(dpre * q["yn"]), _colsum(dpre), _colsum(ds * r * k2)]

    dy_scan, dgate, dr_b, dk2_b, dv_b, G["rw_ln_g"], G["rw_ln_b"], d_rk = _rowwise(
        "d_rwkv_post", rw_post_bwd, T, tT, rows=[y_scan, r_s, k_s, v_s, gate_s, dy_a], consts=post_consts,
        outs=[("row", RW_WIDTH, F32)] * 5 + [("acc", (1, RW_WIDTH))] * 3)
    G["rw_r_k"] = d_rk.reshape(RW_HEADS, RW_HEAD_DIM)

    dr_s, dw_s, dk_s, da_s, db_s, dv_s = _rwkv_chunk_bwd(r_s, w_s, k_s, a_s, b_s, v_s, dy_scan, s0_s, tinv_s, w1_s,
                                                         a2_s, w2_s, sa_s)

    def rw_pre_bwd(i, n, rows, prevs, nx, consts):
        q = rw_pre_core(i, rows, prevs, consts)
        (mrkv, mwa, mxg, w0, a0, k_k, k_a, wup, aup, gup, bd) = consts
        dr, dlogdecay, dk2, dv, dav, dbv, dgate_ = rows[3:10]
        dr = dr + rows[10]
        dk2 = dk2 + rows[11]
        dv = dv + rows[12]
        a, k, kk = q["a"], q["k"], q["kk"]
        dk = dk2 * (1.0 + (a - 1.0) * k_a)
        da = dk2 * k * k_a + dbv * kk
        dkk = dbv * a - dav
        live = q["nrm"] > 1e-12
        dkkp = jnp.where(live, dkk - kk * _segsum(dkk * kk, bd), dkk) / q["nrm_c"]
        dk = dk + dkkp * k_k
        dlw = dlogdecay * q["log_decay"] * _sigmoid(-q["lw"])
        dla = da * a * (1.0 - a)
        nt = (((1,), (1,)), ((), ()))
        dtw = lax.dot_general(dlw.astype(BF16), wup.astype(BF16), nt, preferred_element_type=F32)
        dxa = lax.dot_general(dla.astype(BF16), aup.astype(BF16), nt, preferred_element_type=F32)
        dm_wa = dtw * (1.0 - q["tw"] * q["tw"]) + dxa
        dsg = lax.dot_general(dgate_.astype(BF16), gup.astype(BF16), nt, preferred_element_type=F32)
        dm_xg = dsg * q["sg"] * (1.0 - q["sg"])
        dm_rkv = jnp.concatenate([dr, dk, dv], axis=1)
        prkv, pwa, pxg = rows[:3]
        dmu = jnp.concatenate([_colsum(dm_rkv * (_shift_down(prkv, prevs[0], i, 1) - prkv)),
                               _colsum(dm_wa * (_shift_down(pwa, prevs[1], i, 1) - pwa)),
                               _colsum(dm_xg * (_shift_down(pxg, prevs[2], i, 1) - pxg))], axis=1)
        return [dm_rkv, dm_wa, dm_xg, dlw, dla, q["tw"], q["m_wa"], q["sg"], dmu,
                _colsum(dlw), _colsum(dla), _colsum(dkkp * k), _colsum(dk2 * k * (a - 1.0))]

    (dm_rkv, dm_wa, dm_xg, dlw, dla, tw_s, mwa_s, sg_s, G["rw_mu"], G["rw_w0"], G["rw_a0"], G["rw_k_k"],
     G["rw_k_a"]) = _rowwise(
        "d_rwkv_pre", rw_pre_bwd, T, tT,
        rows=[p_rkv, p_wa, p_xg, dr_s, dw_s, dk_s, dv_s, da_s, db_s, dgate, dr_b, dk2_b, dv_b],
        prevs=[p_rkv, p_wa, p_xg], consts=pre_consts,
        outs=[("row", RKV, F32), ("row", WA, F32), ("row", XG, F32), ("row", RW_WIDTH, BF16),
              ("row", RW_WIDTH, BF16), ("row", WA, BF16), ("row", WA, BF16), ("row", XG, BF16),
              ("acc", (1, RW_COLS))] + [("acc", (1, RW_WIDTH))] * 4)
    G["rw_w_up"] = _mm("d_rw_w_up", tw_s, dlw, "tn", grad_dtype)[:64]
    G["rw_a_up"] = _mm("d_rw_a_up", mwa_s, dla, "tn", grad_dtype)[64:]
    G["rw_g_up"] = _mm("d_rw_g_up", sg_s, dgate, "tn", grad_dtype)

    def shift_bwd(i, n, rows, pv, nexts, consts):
        return [rows[j] * (1.0 - consts[j]) + _shift_up(rows[j], nexts[j], i, n, 1) * consts[j] for j in range(3)]

    dp_rkv, dp_wa, dp_xg = _rowwise(
        "d_token_shift", shift_bwd, T, tT, rows=[dm_rkv, dm_wa, dm_xg], nexts=[dm_rkv, dm_wa, dm_xg],
        consts=[mu_rkv, mu_wa, mu_xg], outs=[("row", RKV, BF16), ("row", WA, BF16), ("row", XG, BF16)])

    G["w_in"] = jnp.concatenate([_mm("d_w_rkv", h, dp_rkv, "tn", grad_dtype), _mm("d_w_wa", h, dp_wa, "tn", grad_dtype),
                                 _mm("d_w_xg", h, dp_xg, "tn", grad_dtype), _mm("d_w_att", h, dp_att, "tn", grad_dtype, tn=768)], axis=1)
    if early_grads is not None:
        w_wa = w_wa + early_grads(G, 2)[0:1, 0:1].astype(w_wa.dtype)
    dh = _mm("d_h_gate", dz_gate, W["w_gate"], "nt")
    dh = _mm("d_h_rkv", dp_rkv, w_rkv, "nt", add=dh)
    dh = _mm("d_h_wa", dp_wa, w_wa, "nt", add=dh)
    dh = _mm("d_h_xg", dp_xg, w_xg, "nt", add=dh)
    dx, G["g_mix"] = _mm("d_h_att", dp_att, w_att, "nt", add=dh, tm=512,
                         post=(through_norm, [x, dx1], [W["g_mix"]], stream_and_gain))
    return loss_acc[:, :1], dx, G


HBM_SPEC = pl.BlockSpec(memory_space=pltpu.HBM)


def _place():
    x, y, c = lax.axis_index("x"), lax.axis_index("y"), lax.axis_index("c")
    return x, y, c, [(1 - x, y), (x, 1 - y), (1 - x, 1 - y)]


def _remote(src, dst, send_sems, recv_sems, k, to):
    return pltpu.make_async_remote_copy(src_ref=src, dst_ref=dst, send_sem=send_sems.at[k], recv_sem=recv_sems.at[k],
                                        device_id=to, device_id_type=MESH)


ROW_ALIGN = 16


def _splits(rows):
    return rows % (2 * ROW_ALIGN) == 0


def _half_rows(ref_rows, c, first):
    half = ref_rows // 2
    which = c if first else 1 - c
    return pl.ds(pl.multiple_of(which * half, ROW_ALIGN), half)


def _gather_chips(shards):
    n = len(shards)
    split = [_splits(s.shape[0]) for s in shards]

    def body(*refs):
        w_refs, out_refs = refs[:n], refs[n:2 * n]
        send_sems, recv_sems = refs[2 * n:]
        x, y, c, chips = _place()
        me = 2 * x + y
        sends, passed = [], []
        for i in range(n):
            for j, (px, py) in enumerate(chips):
                if split[i]:
                    mine = _half_rows(w_refs[i].shape[0], c, True)
                    cp = _remote(w_refs[i].at[mine], out_refs[i].at[me, mine], send_sems, recv_sems, 6 * i + j,
                                 (px, py, c))
                else:
                    cp = _remote(w_refs[i], out_refs[i].at[me], send_sems, recv_sems, 6 * i + j, (px, py, c))
                cp.start()
                sends.append(cp)
        for i in range(n):
            for j, (px, py) in enumerate(chips):
                if split[i]:
                    landed = out_refs[i].at[2 * px + py, _half_rows(w_refs[i].shape[0], c, True)]
                    _remote(landed, landed, send_sems, recv_sems, 6 * i + j, (px, py, c)).wait_recv()
                    cp = _remote(landed, landed, send_sems, recv_sems, 6 * i + 3 + j, (x, y, 1 - c))
                    cp.start()
                    passed.append(cp)
                else:
                    landed = out_refs[i].at[2 * px + py]
                    _remote(landed, landed, send_sems, recv_sems, 6 * i + j, (px, py, c)).wait_recv()
        for i in range(n):
            if split[i]:
                for j, (px, py) in enumerate(chips):
                    landed = out_refs[i].at[2 * px + py, _half_rows(w_refs[i].shape[0], c, False)]
                    _remote(landed, landed, send_sems, recv_sems, 6 * i + 3 + j, (x, y, 1 - c)).wait_recv()
        for cp in sends + passed:
            cp.wait_send()

    outs = pl.pallas_call(
        body, name="gather_weights", in_specs=[HBM_SPEC] * n, out_specs=[HBM_SPEC] * n,
        out_shape=[jax.ShapeDtypeStruct((N_CHIPS,) + s.shape, s.dtype) for s in shards],
        scratch_shapes=[pltpu.SemaphoreType.DMA((6 * n,)), pltpu.SemaphoreType.DMA((6 * n,))],
    )(*shards)
    me = 2 * lax.axis_index("x") + lax.axis_index("y")
    return [lax.dynamic_update_slice(o, s[None], (me, 0, 0)) for o, s in zip(outs, shards, strict=True)]


def _join_halves(reds):
    n = len(reds)

    def body(*refs):
        r_refs, out_refs = refs[:n], refs[n:2 * n]
        send_sems, recv_sems = refs[2 * n:]
        x, y, c, _ = _place()
        cps = []
        for i in range(n):
            mine = _half_rows(out_refs[i].shape[0], c, True)
            cp = _remote(r_refs[i], out_refs[i].at[mine], send_sems, recv_sems, i, (x, y, 1 - c))
            cp.start()
            cps.append(cp)
        for cp in cps:
            cp.wait()

    outs = pl.pallas_call(
        body, name="join_halves", in_specs=[HBM_SPEC] * n, out_specs=[HBM_SPEC] * n,
        out_shape=[jax.ShapeDtypeStruct((2 * r.shape[0], r.shape[1]), r.dtype) for r in reds],
        scratch_shapes=[pltpu.SemaphoreType.DMA((n,)), pltpu.SemaphoreType.DMA((n,))],
    )(*reds)
    c = lax.axis_index("c")
    return [lax.dynamic_update_slice(o, r, (c * r.shape[0], 0)) for o, r in zip(outs, reds, strict=True)]


SEM_SPEC = pl.BlockSpec(memory_space=pltpu.SEMAPHORE)
PEERS = N_DEV - 1
DATAFLOW = pltpu.SideEffectType.DATAFLOW_SIDE_EFFECTING


def _travel_copies(mode, src_refs, land_refs, send_sems, recv_sems):
    x, y, c, chips = _place()
    me = 2 * x + y
    pairs = []
    for i, (src, land) in enumerate(zip(src_refs, land_refs, strict=True)):
        if mode in ("scatter", "all"):
            for k in range(1, N_DEV):
                px, py, pc = x ^ (k >> 2), y ^ ((k >> 1) & 1), c ^ (k & 1)
                mine = src if mode == "all" else src.at[2 * px + py, _half_rows(src.shape[1], pc, True)]
                there, here = land.at[4 * x + 2 * y + c], land.at[4 * px + 2 * py + pc]
                send = functools.partial(_remote, mine, there, send_sems, recv_sems, PEERS * i + k - 1, (px, py, pc))
                arrival = functools.partial(_remote, mine, here, send_sems, recv_sems, PEERS * i + k - 1, (px, py, pc))
                pairs.append((send, arrival))
            continue
        for j, (px, py) in enumerate(chips):
            peer = 2 * px + py
            if _splits(src.shape[0]):
                rows = _half_rows(src.shape[0], c, True)
                mine, there, here = src.at[rows], land.at[me, rows], land.at[peer, rows]
            else:
                mine, there, here = src, land.at[me], land.at[peer]
            send = functools.partial(_remote, mine, there, send_sems, recv_sems, PEERS * i + j, (px, py, c))
            arrival = functools.partial(_remote, mine, here, send_sems, recv_sems, PEERS * i + j, (px, py, c))
            pairs.append((send, arrival))
    return pairs


def _share_halves(name, lands):
    idx = [i for i, a in enumerate(lands) if _splits(a.shape[1])]
    n = len(idx)

    def body(*refs):
        in_refs, out_refs = refs[:n], refs[n:2 * n]
        send_sems, recv_sems = refs[2 * n:]
        x, y, c, chips = _place()
        cps = []
        for i, (src, dst) in enumerate(zip(in_refs, out_refs, strict=True)):
            for j, (px, py) in enumerate(chips):
                mine = _half_rows(src.shape[1], c, True)
                cp = _remote(src.at[2 * px + py, mine], dst.at[2 * px + py, mine], send_sems, recv_sems, 3 * i + j,
                             (x, y, 1 - c))
                cp.start()
                cps.append(cp)
        for i, dst in enumerate(out_refs):
            for j, (px, py) in enumerate(chips):
                theirs = dst.at[2 * px + py, _half_rows(dst.shape[1], c, False)]
                _remote(theirs, theirs, send_sems, recv_sems, 3 * i + j, (x, y, 1 - c)).wait_recv()
        for cp in cps:
            cp.wait_send()

    outs = pl.pallas_call(
        body, name=name, in_specs=[HBM_SPEC] * n, out_specs=[HBM_SPEC] * n,
        out_shape=[jax.ShapeDtypeStruct(lands[i].shape, lands[i].dtype) for i in idx],
        input_output_aliases={i: i for i in range(n)},
        scratch_shapes=[pltpu.SemaphoreType.DMA((3 * n,)), pltpu.SemaphoreType.DMA((3 * n,))],
    )(*[lands[i] for i in idx])
    done = list(lands)
    for i, o in zip(idx, outs, strict=True):
        done[i] = o
    return done


def _travel_start(name, mode, srcs):
    n = len(srcs)
    land_shape = {"gather": lambda s: (N_CHIPS,) + s.shape, "all": lambda s: (N_DEV,) + s.shape,
                  "scatter": lambda s: (N_DEV, s.shape[1] // 2, s.shape[2])}[mode]
    lands = [lax.empty(land_shape(s), s.dtype) for s in srcs]

    def body(*refs):
        src_refs, land_refs = refs[:n], refs[n:2 * n]
        send_sems, recv_sems = refs[2 * n], refs[2 * n + 1]
        token = refs[-1]
        for send, _ in _travel_copies(mode, src_refs, land_refs, send_sems, recv_sems):
            send().start()
        token[...] = jnp.zeros_like(token)

    hbm = lambda a: pltpu.HBM(a.shape, a.dtype)
    outs = pl.pallas_call(
        body, name=name,
        out_shape=(pltpu.SemaphoreType.DMA((PEERS * n,)), pltpu.SemaphoreType.DMA((PEERS * n,)),
                   *[hbm(s) for s in srcs],
                   *[hbm(a) for a in lands], jax.ShapeDtypeStruct((SUBLANES, LANES), F32)),
        in_specs=[HBM_SPEC] * (2 * n),
        out_specs=(SEM_SPEC, SEM_SPEC, *[HBM_SPEC] * (2 * n), pl.BlockSpec(memory_space=pltpu.VMEM)),
        input_output_aliases={i: 2 + i for i in range(2 * n)},
        compiler_params=pltpu.CompilerParams(has_side_effects=DATAFLOW),
    )(*[pltpu.with_memory_space_constraint(a, pltpu.HBM) for a in list(srcs) + lands])
    return outs[0], outs[1], list(outs[2:2 + n]), list(outs[2 + n:2 + 2 * n]), outs[-1]


def _travel_wait(name, mode, send_sems, recv_sems, srcs, lands, after):
    n = len(srcs)

    def body(*refs):
        src_refs, land_refs = refs[:n], refs[n:2 * n]
        send_sems_, recv_sems_ = refs[2 * n], refs[2 * n + 1]
        for send, arrival in _travel_copies(mode, src_refs, land_refs, send_sems_, recv_sems_):
            send().wait_send()
            arrival().wait_recv()

    hbm = lambda a: pltpu.HBM(a.shape, a.dtype)
    outs = pl.pallas_call(
        body, name=name, out_shape=tuple(hbm(a) for a in list(srcs) + list(lands)),
        in_specs=[HBM_SPEC] * (2 * n) + [SEM_SPEC, SEM_SPEC, pl.BlockSpec(memory_space=pl.ANY)],
        out_specs=tuple([HBM_SPEC] * (2 * n)), input_output_aliases={i: i for i in range(2 * n)},
        compiler_params=pltpu.CompilerParams(has_side_effects=DATAFLOW),
    )(*srcs, *lands, send_sems, recv_sems, after)
    c = lax.axis_index("c")
    me = 2 * lax.axis_index("x") + lax.axis_index("y")
    if mode == "gather":
        slot, own = me, [s[None] for s in outs[:n]]
    elif mode == "all":
        slot, own = 2 * me + c, [s[None] for s in outs[:n]]
    else:
        slot = 2 * me + c
        own = [lax.dynamic_slice(s, (me, c * (s.shape[1] // 2), 0), (1, s.shape[1] // 2, s.shape[2])) for s in outs[:n]]
    return [lax.dynamic_update_slice(a, o, (slot,) + (0,) * (a.ndim - 1)) for a, o in zip(outs[n:], own, strict=True)]


SUM_TILE_BYTES = 4 * 1024 * 1024


def _sum_rows(half, cols):
    best = ROW_ALIGN
    for t in range(ROW_ALIGN, half + 1, ROW_ALIGN):
        if half % t == 0 and N_CHIPS * t * cols * 4 <= SUM_TILE_BYTES:
            best = t
    return best


def _sum_devices(name, parts):
    n, H, C = parts.shape
    tr = _sum_rows(H, C)

    def body(p_ref, o_ref):
        acc = p_ref[0].astype(F32)
        for k in range(1, n):
            acc = acc + p_ref[k].astype(F32)
        o_ref[...] = acc

    return pl.pallas_call(
        body, name=name, grid=(H // tr,),
        in_specs=[pl.BlockSpec((n, tr, C), lambda i: (0, i, 0))],
        out_specs=pl.BlockSpec((tr, C), lambda i: (i, 0)),
        out_shape=jax.ShapeDtypeStruct((H, C), F32),
        compiler_params=_params(("parallel",)),
    )(parts)


def _adamw_math(w, g, m, v):
    m = ADAM_B1 * m + (1.0 - ADAM_B1) * g
    v = ADAM_B2 * v + (1.0 - ADAM_B2) * (g * g)
    m_hat = m / (1.0 - ADAM_B1 ** ADAM_STEP)
    v_hat = v / (1.0 - ADAM_B2 ** ADAM_STEP)
    delta = -ADAM_LR * (m_hat / (jnp.sqrt(v_hat) + ADAM_EPS) + ADAM_WD * w)
    return delta, m, v


def _adamw(name, w, g, m, v):
    R, C = w.shape
    tr = R
    if R % SUBLANES == 0:
        for cand in range(SUBLANES, min(R, 256) + 1, SUBLANES):
            if R % cand == 0:
                tr = cand

    def body(w_ref, g_ref, m_ref, v_ref, d_ref, nm_ref, nv_ref):
        d, nm, nv = _adamw_math(w_ref[...], g_ref[...], m_ref[...], v_ref[...])
        d_ref[...] = d
        nm_ref[...] = nm
        nv_ref[...] = nv

    spec = pl.BlockSpec((tr, C), lambda i: (i, 0))
    shape = jax.ShapeDtypeStruct((R, C), F32)
    return pl.pallas_call(
        body, name=name, grid=(R // tr,), in_specs=[spec] * 4, out_specs=[spec] * 3, out_shape=[shape] * 3,
        compiler_params=_params(("parallel",)),
    )(w, g, m, v)


SMALL_ROW = 2048


def _small_layout(shapes):
    places, row = [], 0
    for R, C in shapes:
        pieces = []
        for r in range(R):
            for c0 in range(0, C, SMALL_ROW):
                pieces.append((r, c0, min(C, c0 + SMALL_ROW), row))
                row += 1
        places.append(pieces)
    return places, -(-row // SUBLANES) * SUBLANES


def _put_rows(block_ref, refs, places):
    block_ref[...] = jnp.zeros_like(block_ref)
    for ref, pieces in zip(refs, places, strict=True):
        for r, c0, c1, row in pieces:
            block_ref[row:row + 1, 0:c1 - c0] = ref[r:r + 1, c0:c1]


def _take_rows(block, refs, places):
    for ref, pieces in zip(refs, places, strict=True):
        for r, c0, c1, row in pieces:
            ref[r:r + 1, c0:c1] = block[row:row + 1, 0:c1 - c0]


def _pack_small(arrs):
    places, rows = _small_layout([a.shape for a in arrs])

    def body(*refs):
        _put_rows(refs[-1], refs[:-1], places)

    return pl.pallas_call(body, name="pack_small", out_shape=jax.ShapeDtypeStruct((rows, SMALL_ROW), F32),
                          compiler_params=_params())(*arrs)


def _adamw_small(parts, ws, ms, vs, extra_shapes):
    n_dev, rows, _ = parts.shape
    n = len(ws)
    places, rows_ = _small_layout([w.shape for w in ws] + list(extra_shapes))
    assert rows_ == rows, (rows_, rows)

    def body(*refs):
        p_ref = refs[0]
        w_refs, m_refs, v_refs = refs[1:1 + n], refs[1 + n:1 + 2 * n], refs[1 + 2 * n:1 + 3 * n]
        outs = refs[1 + 3 * n:-3]
        wb, mb, vb = refs[-3:]
        for block, srcs in ((wb, w_refs), (mb, m_refs), (vb, v_refs)):
            _put_rows(block, srcs, places[:n])
        g = p_ref[0]
        for k in range(1, n_dev):
            g = g + p_ref[k]
        d, nm, nv = _adamw_math(wb[...], g, mb[...], vb[...])
        _take_rows(g, outs[0:n], places[:n])
        _take_rows(d, outs[n:2 * n], places[:n])
        _take_rows(nm, outs[2 * n:3 * n], places[:n])
        _take_rows(nv, outs[3 * n:4 * n], places[:n])
        _take_rows(g, outs[4 * n:], places[n:])

    shapes = [jax.ShapeDtypeStruct(w.shape, F32) for w in ws]
    res = pl.pallas_call(
        body, name="adamw_small", out_shape=shapes * 4 + [jax.ShapeDtypeStruct(s, F32) for s in extra_shapes],
        scratch_shapes=[pltpu.VMEM((rows, SMALL_ROW), F32)] * 3, compiler_params=_params(),
    )(parts, *ws, *ms, *vs)
    return res[0:n], res[n:2 * n], res[2 * n:3 * n], res[3 * n:4 * n], res[4 * n:]


WEIGHTS = ['g_mix', 'w_in', 'rw_mu', 'rw_w0', 'rw_w_up', 'rw_a0', 'rw_a_up', 'rw_g_up', 'rw_k_k', 'rw_k_a',
           'rw_r_k', 'rw_ln_g', 'rw_ln_b', 'w_branch_a', 'w_branch_b', 'w_gate', 'b_gate', 'w_out', 'g_ffn', 'w_up',
           'conv_w', 'conv_b', 'w_down', 'g_ple', 'w_ple_gate', 'w_ple', 'g_final']
ARG_NAMES = (['x', 'p'] + WEIGHTS + ['loss_target'] + ['m_' + n for n in WEIGHTS] + ['v_' + n for n in WEIGHTS])
SHARDED = {'w_in': 1, 'rw_w_up': 1, 'rw_a_up': 1, 'rw_g_up': 1, 'w_branch_a': 1, 'w_branch_b': 1, 'w_gate': 1,
           'w_out': 0, 'w_up': 1, 'conv_w': 1, 'w_down': 0, 'w_ple_gate': 0, 'w_ple': 1}
SMALL = [n for n in WEIGHTS if n not in SHARDED]
WHOLE = ['conv_w']
FIRST_USED = ['w_in', 'rw_w_up', 'rw_a_up', 'rw_g_up', 'w_gate']
READ_BY_CHIP = ['w_gate', 'w_branch_a', 'w_branch_b', 'w_up', 'w_ple']
FIRST_DONE = [['w_up', 'w_down', 'w_ple_gate', 'w_ple'], ['w_out', 'w_branch_a', 'w_branch_b', 'w_gate'],
              ['w_in', 'rw_w_up', 'rw_a_up', 'rw_g_up']]
SPLIT = [n for n in SHARDED if n not in WHOLE]


def _full_from_shards(stack, axis):
    _, R, C = stack.shape
    if axis == 0:
        return stack.reshape(N_CHIPS * R, C)
    return stack.transpose(1, 0, 2).reshape(R, N_CHIPS * C)


def _shards_from_full(full, axis):
    R, C = full.shape
    if axis == 0:
        return full.reshape(N_CHIPS, R // N_CHIPS, C)
    return full.reshape(R, N_CHIPS, C // N_CHIPS).transpose(1, 0, 2)


def kernel(x, p, g_mix, w_in, rw_mu, rw_w0, rw_w_up, rw_a0, rw_a_up, rw_g_up, rw_k_k, rw_k_a, rw_r_k, rw_ln_g, rw_ln_b, w_branch_a, w_branch_b, w_gate, b_gate, w_out, g_ffn, w_up, conv_w, conv_b, w_down, g_ple, w_ple_gate, w_ple, g_final, loss_target, m_g_mix, m_w_in, m_rw_mu, m_rw_w0, m_rw_w_up, m_rw_a0, m_rw_a_up, m_rw_g_up, m_rw_k_k, m_rw_k_a, m_rw_r_k, m_rw_ln_g, m_rw_ln_b, m_w_branch_a, m_w_branch_b, m_w_gate, m_b_gate, m_w_out, m_g_ffn, m_w_up, m_conv_w, m_conv_b, m_w_down, m_g_ple, m_w_ple_gate, m_w_ple, m_g_final, v_g_mix, v_w_in, v_rw_mu, v_rw_w0, v_rw_w_up, v_rw_a0, v_rw_a_up, v_rw_g_up, v_rw_k_k, v_rw_k_a, v_rw_r_k, v_rw_ln_g, v_rw_ln_b, v_w_branch_a, v_w_branch_b, v_w_gate, v_b_gate, v_w_out, v_g_ffn, v_w_up, v_conv_w, v_conv_b, v_w_down, v_g_ple, v_w_ple_gate, v_w_ple, v_g_final):
    given = dict(zip(ARG_NAMES, (x, p, g_mix, w_in, rw_mu, rw_w0, rw_w_up, rw_a0, rw_a_up, rw_g_up, rw_k_k, rw_k_a, rw_r_k, rw_ln_g, rw_ln_b, w_branch_a, w_branch_b, w_gate, b_gate, w_out, g_ffn, w_up, conv_w, conv_b, w_down, g_ple, w_ple_gate, w_ple, g_final, loss_target, m_g_mix, m_w_in, m_rw_mu, m_rw_w0, m_rw_w_up, m_rw_a0, m_rw_a_up, m_rw_g_up, m_rw_k_k, m_rw_k_a, m_rw_r_k, m_rw_ln_g, m_rw_ln_b, m_w_branch_a, m_w_branch_b, m_w_gate, m_b_gate, m_w_out, m_g_ffn, m_w_up, m_conv_w, m_conv_b, m_w_down, m_g_ple, m_w_ple_gate, m_w_ple, m_g_final, v_g_mix, v_w_in, v_rw_mu, v_rw_w0, v_rw_w_up, v_rw_a0, v_rw_a_up, v_rw_g_up, v_rw_k_k, v_rw_k_a, v_rw_r_k, v_rw_ln_g, v_rw_ln_b, v_w_branch_a, v_w_branch_b, v_w_gate, v_b_gate, v_w_out, v_g_ffn, v_w_up, v_conv_w, v_conv_b, v_w_down, v_g_ple, v_w_ple_gate, v_w_ple, v_g_final), strict=True))

    def two_d(name, prefix=""):
        a = given[prefix + name]
        if name == "g_final":
            return a.reshape(1, D_MODEL)
        if name == "rw_r_k":
            return a.reshape(1, RW_WIDTH)
        return a[0] if a.ndim == 3 else a

    cast = lambda n: two_d(n) if n in WHOLE else two_d(n).astype(BF16)
    whole = lambda names, stacks: {n: g if n in READ_BY_CHIP else _full_from_shards(g, SHARDED[n])
                                   for n, g in zip(names, stacks, strict=True)}
    late_names = [n for n in SHARDED if n not in FIRST_USED]
    late_sends, late_recvs, late_srcs, late_lands, token = _travel_start(
        "gather_late_start", "gather", [cast(n) for n in late_names])
    W = whole(FIRST_USED, _gather_chips([cast(n) for n in FIRST_USED]))
    for n in SMALL:
        W[n] = two_d(n)
    W["rw_r_k"] = W["rw_r_k"].reshape(RW_HEADS, RW_HEAD_DIM)
    W["g_mix"] = W["g_mix"] + token[0:1, 0:1]

    def late_weights(after):
        lands = _travel_wait("gather_late_wait", "gather", late_sends, late_recvs, late_srcs, late_lands, after)
        return whole(late_names, _share_halves("share_late", lands))

    early_names = [[n for n in SPLIT if n in group] for group in FIRST_DONE]
    assert sorted(sum(early_names, [])) == sorted(SPLIT)
    travelling = []

    def early_grads(G, stage):
        by_chip = [G[n] if n in READ_BY_CHIP else _shards_from_full(G[n], SHARDED[n]) for n in early_names[stage]]
        sends, recvs, srcs, lands, started = _travel_start(f"scatter{stage}_start", "scatter", by_chip)
        travelling.append((sends, recvs, srcs, lands))
        return started

    loss_part, grad_x, G = _local_step(x[0], p[0, 0], W, loss_target[0], late_weights, early_grads, by_chip=True,
                                       grad_dtype=BF16)

    G["rw_r_k"] = G["rw_r_k"].reshape(1, RW_WIDTH)
    extras = [G[n] for n in WHOLE] + [loss_part]
    small_sends, small_recvs, small_srcs, small_lands, small_started = _travel_start(
        "gather_small_start", "all", [_pack_small([G[n] for n in SMALL] + extras)])

    landed = {}
    for stage, (sends, recvs, srcs, lands) in enumerate(travelling):
        landed.update(zip(early_names[stage], _travel_wait(f"scatter{stage}_wait", "scatter", sends, recvs, srcs,
                                                           lands, small_started), strict=True))
    reduced = [_sum_devices("sum_devices_" + n, landed[n]) for n in SPLIT]
    shard_grads = dict(zip(SPLIT, _join_halves(reduced), strict=True))

    grads, deltas, new_m, new_v = {}, {}, {}, {}

    def step(n):
        g = shard_grads[n]
        d, nm, nv = _adamw("adamw_" + n, two_d(n), g, two_d(n, "m_"), two_d(n, "v_"))
        grads[n], deltas[n], new_m[n], new_v[n] = g, d, nm, nv

    for n in SPLIT:
        step(n)

    (all_small,) = _travel_wait("gather_small_wait", "all", small_sends, small_recvs, small_srcs, small_lands,
                                deltas[SPLIT[-1]])
    gs, ds, nms, nvs, summed = _adamw_small(all_small, [two_d(n) for n in SMALL], [two_d(n, "m_") for n in SMALL],
                                            [two_d(n, "v_") for n in SMALL], [e.shape for e in extras])
    loss = summed[-1][0, 0]
    chip = 2 * lax.axis_index("x") + lax.axis_index("y")
    for n, full in zip(WHOLE, summed[:-1], strict=True):
        width = two_d(n).shape[1]
        shard_grads[n] = lax.dynamic_slice_in_dim(full, chip * width, width, axis=1)
        step(n)
    for i, n in enumerate(SMALL):
        grads[n], deltas[n], new_m[n], new_v[n] = gs[i], ds[i], nms[i], nvs[i]
    outs = [loss, grad_x[None]]
    for table in (grads, deltas, new_m, new_v):
        outs += [table[n].reshape(given[n].shape) for n in WEIGHTS]
    return tuple(outs)
```

```python
import functools
import math

import jax
import jax.numpy as jnp
import numpy as np
from jax import lax
from jax.experimental import pallas as pl
from jax.experimental.pallas import tpu as pltpu

F32 = jnp.float32
BF16 = jnp.bfloat16

D_MODEL = 1024
NORM_EPS = 1e-6
RW_HEADS = 8
RW_HEAD_DIM = 64
RW_WIDTH = 512
RW_LN_EPS = 64e-5
ATT_GROUP_DILATION = (1, 4, 16)
ATT_BLOCK = 128
ATT_HEADS = 12
ATT_HEAD_DIM = 64
ATT_GROUP_WIDTH = 256
ATT_WIDTH = 768
D_FF = 3072

ADAM_LR = 0.001
ADAM_B1 = 0.9
ADAM_B2 = 0.999
ADAM_EPS = 1e-08
ADAM_WD = 0.01
ADAM_STEP = 10

SUBLANES = 8
LANES = 128
VMEM_LIMIT = 56 * 1024 * 1024
N_CHIPS = 4
N_DEV = 8
MESH = pl.DeviceIdType.MESH


def _params(sem=None):
    return pltpu.CompilerParams(dimension_semantics=sem, vmem_limit_bytes=VMEM_LIMIT)


def _pick(dim, pref):
    if dim % LANES != 0 or dim <= pref:
        return dim
    best = LANES
    for t in range(LANES, pref + 1, LANES):
        if dim % t == 0:
            best = t
    return best


def _mm(name, a, b, mode, out_dtype=F32, add=None, tm=1024, tn=1024, tk=1024, out_by_chip=False, post=None):
    by_chip = b.ndim == 3
    b_rows, b_cols = (b.shape[1], N_CHIPS * b.shape[2]) if by_chip else b.shape
    if mode == "nn":
        (M, K), (K2, N) = a.shape, (b_rows, b_cols)
    elif mode == "nt":
        (M, K), (N, K2) = a.shape, (b_rows, b_cols)
    else:
        (K, M), (K2, N) = a.shape, (b_rows, b_cols)
    assert K == K2, (name, a.shape, b.shape, mode)
    assert not (by_chip and mode == "tn") and not (out_by_chip and add is not None), name
    tm = _pick(M, tm)
    n_cut, k_cut = out_by_chip or (by_chip and mode == "nn"), by_chip and mode == "nt"
    tn = _pick(N // N_CHIPS, tn) if n_cut else _pick(N, tn)
    tk = _pick(K // N_CHIPS, tk) if k_cut else _pick(K, tk)
    nk = K // tk
    per_n = (N // N_CHIPS) // tn if n_cut else 1
    per_k = (K // N_CHIPS) // tk if k_cut else 1
    if mode == "nn":
        a_spec = pl.BlockSpec((tm, tk), lambda i, j, k: (i, k))
        b_spec = (pl.BlockSpec((None, tk, tn), lambda i, j, k: (j // per_n, k, j % per_n)) if by_chip
                  else pl.BlockSpec((tk, tn), lambda i, j, k: (k, j)))
        dims = (((1,), (0,)), ((), ()))
    elif mode == "nt":
        a_spec = pl.BlockSpec((tm, tk), lambda i, j, k: (i, k))
        b_spec = (pl.BlockSpec((None, tn, tk), lambda i, j, k: (k // per_k, j, k % per_k)) if by_chip
                  else pl.BlockSpec((tn, tk), lambda i, j, k: (j, k)))
        dims = (((1,), (1,)), ((), ()))
    else:
        a_spec = pl.BlockSpec((tk, tm), lambda i, j, k: (k, i))
        b_spec = pl.BlockSpec((tk, tn), lambda i, j, k: (k, j))
        dims = (((0,), (0,)), ((), ()))
    if out_by_chip:
        o_spec = pl.BlockSpec((None, tm, tn), lambda i, j, k: (j // per_n, i, j % per_n))
        out_shape = jax.ShapeDtypeStruct((N_CHIPS, M, N // N_CHIPS), out_dtype)
    else:
        o_spec = pl.BlockSpec((tm, tn), lambda i, j, k: (i, j))
        out_shape = jax.ShapeDtypeStruct((M, N), out_dtype)
    has_add = add is not None
    ins = [a, b] + ([add] if has_add else [])
    in_specs = [a_spec, b_spec] + ([o_spec] if has_add else [])
    n_main = len(ins)
    semantics = ("parallel", "parallel", "arbitrary")
    if post is not None:
        post_fn, post_rows, post_consts, post_outs = post
        assert tn == N and not out_by_chip, name
        ins += list(post_rows) + list(post_consts)
        in_specs += [pl.BlockSpec((tm, r.shape[1]), lambda i, j, k: (i, 0)) for r in post_rows]
        in_specs += [pl.BlockSpec(c.shape, lambda i, j, k, nd=c.ndim: (0,) * nd) for c in post_consts]
        o_spec = [pl.BlockSpec((tm, o[1]), lambda i, j, k: (i, 0)) if o[0] == "row"
                  else pl.BlockSpec(o[1], lambda i, j, k: (0, 0)) for o in post_outs]
        out_shape = [jax.ShapeDtypeStruct((M, o[1]), o[2]) if o[0] == "row" else jax.ShapeDtypeStruct(o[1], F32)
                     for o in post_outs]
        if any(o[0] == "acc" for o in post_outs):
            semantics = ("arbitrary", "arbitrary", "arbitrary")
    n_in = len(ins)

    def body(*refs):
        a_ref, b_ref = refs[:2]
        out_refs, acc_ref = refs[n_in:-1], refs[-1]
        i, k = pl.program_id(0), pl.program_id(2)
        part = lax.dot_general(a_ref[...].astype(BF16), b_ref[...].astype(BF16), dims,
                               preferred_element_type=F32)

        @pl.when(k == 0)
        def _():
            acc_ref[...] = part

        @pl.when(k > 0)
        def _():
            acc_ref[...] += part

        @pl.when(k == nk - 1)
        def _():
            res = acc_ref[...]
            if has_add:
                res = res + refs[2][...].astype(F32)
            if post is None:
                out_refs[0][...] = res.astype(out_refs[0].dtype)
                return
            n_rows = len(post_rows)
            vals = post_fn(res, [r[...] for r in refs[n_main:n_main + n_rows]],
                           [c[...] for c in refs[n_main + n_rows:n_in]])
            for o, o_ref, val in zip(post_outs, out_refs, vals, strict=True):
                if o[0] == "row":
                    o_ref[...] = val.astype(o_ref.dtype)
                else:
                    @pl.when(i == 0)
                    def _(o_ref=o_ref, val=val):
                        o_ref[...] = val.astype(F32)

                    @pl.when(i > 0)
                    def _(o_ref=o_ref, val=val):
                        o_ref[...] += val.astype(F32)

    return pl.pallas_call(
        body, name=name, grid=(M // tm, N // tn, nk),
        in_specs=in_specs, out_specs=o_spec, out_shape=out_shape,
        scratch_shapes=[pltpu.VMEM((tm, tn), F32)],
        compiler_params=_params(semantics),
    )(*ins)


def _rowwise(name, fn, T, tT, rows=(), prevs=(), nexts=(), consts=(), outs=()):
    n = T // tT
    per8 = tT // SUBLANES
    in_specs, ins = [], []
    for arr in rows:
        in_specs.append(pl.BlockSpec((tT, arr.shape[1]), lambda i: (i, 0)))
        ins.append(arr)
    for arr in prevs:
        in_specs.append(pl.BlockSpec((SUBLANES, arr.shape[1]), lambda i: (jnp.maximum(i * per8 - 1, 0), 0)))
        ins.append(arr)
    for arr in nexts:
        in_specs.append(pl.BlockSpec((SUBLANES, arr.shape[1]),
                                     lambda i: (jnp.minimum((i + 1) * per8, T // SUBLANES - 1), 0)))
        ins.append(arr)
    for arr in consts:
        in_specs.append(pl.BlockSpec(arr.shape, lambda i, nd=arr.ndim: (0,) * nd))
        ins.append(arr)
    out_specs, out_shapes = [], []
    for o in outs:
        if o[0] == "row":
            out_specs.append(pl.BlockSpec((tT, o[1]), lambda i: (i, 0)))
            out_shapes.append(jax.ShapeDtypeStruct((T, o[1]), o[2]))
        else:
            out_specs.append(pl.BlockSpec(o[1], lambda i: (0, 0)))
            out_shapes.append(jax.ShapeDtypeStruct(o[1], F32))
    nr, npv, nnx, nc = len(rows), len(prevs), len(nexts), len(consts)
    n_in = nr + npv + nnx + nc

    def body(*refs):
        i = pl.program_id(0)
        vals = [r[...] for r in refs[:n_in]]
        res = fn(i, n, vals[:nr], vals[nr:nr + npv], vals[nr + npv:nr + npv + nnx], vals[nr + npv + nnx:])
        for o, o_ref, val in zip(outs, refs[n_in:], res, strict=True):
            if o[0] == "row":
                o_ref[...] = val.astype(o_ref.dtype)
            else:
                @pl.when(i == 0)
                def _(o_ref=o_ref, val=val):
                    o_ref[...] = val.astype(F32)

                @pl.when(i > 0)
                def _(o_ref=o_ref, val=val):
                    o_ref[...] += val.astype(F32)

    res = pl.pallas_call(
        body, name=name, grid=(n,), in_specs=in_specs, out_specs=out_specs, out_shape=out_shapes,
        compiler_params=_params(("arbitrary",)),
    )(*ins)
    return list(res)


def _shift_down(x, prev8, i, s):
    rolled = pltpu.roll(x, s, 0)
    head = pltpu.roll(prev8, s, 0)
    head = jnp.where(i == 0, jnp.zeros_like(head), head)
    rid = lax.broadcasted_iota(jnp.int32, head.shape, 0)
    first = jnp.where(rid < s, head, rolled[:SUBLANES])
    if x.shape[0] == SUBLANES:
        return first
    return jnp.concatenate([first, rolled[SUBLANES:]], axis=0)


def _shift_up(x, next8, i, n, s):
    tT = x.shape[0]
    rolled = pltpu.roll(x, tT - s, 0)
    tail = pltpu.roll(next8, SUBLANES - s, 0)
    tail = jnp.where(i == n - 1, jnp.zeros_like(tail), tail)
    rid = lax.broadcasted_iota(jnp.int32, tail.shape, 0)
    last = jnp.where(rid >= SUBLANES - s, tail, rolled[tT - SUBLANES:])
    return jnp.concatenate([rolled[:tT - SUBLANES], last], axis=0)


def _colsum(x):
    return jnp.sum(x, axis=0, keepdims=True)


def _segsum(x, bd):
    return jnp.dot(x, bd, precision=lax.Precision.HIGH, preferred_element_type=F32)


def _block_diag_ones(width, seg):
    idx = np.arange(width) // seg
    return jnp.asarray((idx[:, None] == idx[None, :]).astype(np.float32))


def _sigmoid(z):
    return 1.0 / (1.0 + jnp.exp(-z))


def _softplus(z):
    return jnp.maximum(z, 0.0) + jnp.log(1.0 + jnp.exp(-jnp.abs(z)))


def _rms_fwd(x, g):
    r = lax.rsqrt(jnp.mean(x * x, axis=-1, keepdims=True) + NORM_EPS)
    return x * r * g


def _rms_bwd(x, g, dy):
    r = lax.rsqrt(jnp.mean(x * x, axis=-1, keepdims=True) + NORM_EPS)
    gdy = dy * g
    dx = r * (gdy - x * (r * r) * jnp.mean(x * gdy, axis=-1, keepdims=True))
    return dx, dy * x * r


GELU_C = math.sqrt(2.0 / math.pi)


def _gelu(x):
    return 0.5 * x * (1.0 + jnp.tanh(GELU_C * (x + 0.044715 * x * x * x)))


def _gelu_and_grad(x):
    th = jnp.tanh(GELU_C * (x + 0.044715 * x * x * x))
    half = 0.5 * (1.0 + th)
    return x * half, half + 0.5 * x * (1.0 - th * th) * GELU_C * (1.0 + 3.0 * 0.044715 * x * x)


RW_CHUNK = 64
NN = (((1,), (0,)), ((), ()))
NT = (((1,), (1,)), ((), ()))
TN = (((0,), (0,)), ((), ()))


def _hdot(a, b, dims):
    return lax.dot_general(a, b, dims, precision=lax.Precision.HIGH, preferred_element_type=F32)


def _ldot(a, b, dims):
    return lax.dot_general(a.astype(BF16), b.astype(BF16), dims, preferred_element_type=F32)


def _chunk_masks():
    ti = lax.broadcasted_iota(jnp.int32, (RW_CHUNK, RW_CHUNK), 0)
    tj = lax.broadcasted_iota(jnp.int32, (RW_CHUNK, RW_CHUNK), 1)
    return tj <= ti, tj < ti, (ti == tj).astype(F32)


def _head(x, h):
    return x[:, h * RW_HEAD_DIM:(h + 1) * RW_HEAD_DIM]


def _heads(fn):
    return [fn(h) for h in range(RW_HEADS)]


def _chunk_rows(r, lw, k, a, b, incl_f):
    c = _hdot(incl_f, lw, NN)
    e_prev, e_neg, e_pos = jnp.exp(c - lw), jnp.exp(-c), jnp.exp(c)
    return dict(At=a * e_prev, Bt=b * e_neg, Kt=k * e_neg, Rt=r * e_pos, e_prev=e_prev, e_neg=e_neg, e_pos=e_pos)


def _stack(top, bottom, h):
    return jnp.concatenate([_head(top, h), _head(bottom, h)], axis=0)


def _chunk_coeffs(q, incl, strict):
    C = RW_CHUNK
    ar = _heads(lambda h: _stack(q["At"], q["Rt"], h))
    pb = _heads(lambda h: _hdot(ar[h], _head(q["Bt"], h), NT))
    pk = _heads(lambda h: _hdot(ar[h], _head(q["Kt"], h), NT))
    A1, W1 = [jnp.where(strict, m[:C], 0.0) for m in pb], [jnp.where(incl, m[C:], 0.0) for m in pb]
    A2, W2 = [jnp.where(strict, m[:C], 0.0) for m in pk], [jnp.where(incl, m[C:], 0.0) for m in pk]
    return A1, A2, W1, W2


def _rwkv_chunk_prep(r, lw, k, a, b, v):
    T = r.shape[0]
    nC = T // RW_CHUNK
    H, N = RW_HEADS, RW_HEAD_DIM

    def body(r_ref, lw_ref, k_ref, a_ref, b_ref, v_ref,
             at_ref, bt_ref, kt_ref, rt_ref, a2v_ref, w2v_ref, ti_ref, w1_ref, a2_ref, w2_ref, pl_ref):
        incl, strict, eye = _chunk_masks()
        q = _chunk_rows(r_ref[...], lw_ref[...], k_ref[...], a_ref[...], b_ref[...], incl.astype(F32))
        at_ref[...], bt_ref[...], kt_ref[...], rt_ref[...] = q["At"], q["Bt"], q["Kt"], q["Rt"]
        pl_ref[0] = jnp.broadcast_to(q["e_pos"][RW_CHUNK - 1:RW_CHUNK, :], (SUBLANES, RW_WIDTH))
        A1, A2, W1, W2 = _chunk_coeffs(q, incl, strict)
        V = v_ref[...]
        a2v_ref[...] = jnp.concatenate(_heads(lambda h: _hdot(A2[h], _head(V, h), NN)), axis=1)
        w2v_ref[...] = jnp.concatenate(_heads(lambda h: _ldot(W2[h], _head(V, h), NN)), axis=1)
        tinv, pw = [eye + m for m in A1], A1
        for stage in range(5):
            dot = _hdot if stage == 0 else _ldot
            pw = [dot(m, m, NN) for m in pw]
            tinv = [t + dot(t, m, NN) for t, m in zip(tinv, pw, strict=True)]
        for h in range(H):
            ti_ref[0, h] = tinv[h]
            w1_ref[0, h] = W1[h]
            a2_ref[0, h] = A2[h]
            w2_ref[0, h] = W2[h]

    row_spec = pl.BlockSpec((RW_CHUNK, RW_WIDTH), lambda n: (n, 0))
    st_spec = pl.BlockSpec((1, H, N, N), lambda n: (n, 0, 0, 0))
    row_shape = jax.ShapeDtypeStruct((T, RW_WIDTH), F32)
    st_shape = jax.ShapeDtypeStruct((nC, H, N, N), F32)
    return pl.pallas_call(
        body, name="rwkv_chunk_prep", grid=(nC,),
        in_specs=[row_spec] * 6,
        out_specs=[row_spec] * 6 + [st_spec] * 4 + [pl.BlockSpec((1, SUBLANES, RW_WIDTH), lambda n: (n, 0, 0))],
        out_shape=[row_shape] * 6 + [st_shape] * 4 + [jax.ShapeDtypeStruct((nC, SUBLANES, RW_WIDTH), F32)],
        compiler_params=_params(("parallel",)),
    )(r, lw, k, a, b, v)


def _rwkv_chunk_fwd(v, at, bt, kt, rt, a2v, w2v, tinv, w1, plast):
    T = v.shape[0]
    nC = T // RW_CHUNK
    H, N = RW_HEADS, RW_HEAD_DIM

    def body(v_ref, at_ref, bt_ref, kt_ref, rt_ref, a2v_ref, w2v_ref, ti_ref, w1_ref, pl_ref,
             y_ref, sa_ref, s0_ref, S_ref):
        @pl.when(pl.program_id(0) == 0)
        def _():
            S_ref[...] = jnp.zeros_like(S_ref)

        V, At, Bt, Kt, Rt = v_ref[...], at_ref[...], bt_ref[...], kt_ref[...], rt_ref[...]
        A2V, W2V, p_last = a2v_ref[...], w2v_ref[...], pl_ref[0, 0:1, :]
        S0 = _heads(lambda h: S_ref[h])
        for h in range(H):
            s0_ref[0, h] = S0[h]
        C = RW_CHUNK
        on_state = _heads(lambda h: _hdot(_stack(At, Rt, h), S0[h], NT))
        Sa = _heads(lambda h: _hdot(ti_ref[0, h], on_state[h][:C] + _head(A2V, h), NN))
        X = _heads(lambda h: S0[h] + _hdot(jnp.concatenate([Sa[h], _head(V, h)], axis=0), _stack(Bt, Kt, h), TN))
        for h in range(H):
            S_ref[h] = X[h] * _head(p_last, h)
        Y = _heads(lambda h: on_state[h][C:] + _ldot(w1_ref[0, h], Sa[h], NN) + _head(W2V, h))
        y_ref[...] = jnp.concatenate(Y, axis=1)
        sa_ref[...] = jnp.concatenate(Sa, axis=1)

    row_spec = pl.BlockSpec((RW_CHUNK, RW_WIDTH), lambda n: (n, 0))
    st_spec = pl.BlockSpec((1, H, N, N), lambda n: (n, 0, 0, 0))
    row_shape = jax.ShapeDtypeStruct((T, RW_WIDTH), F32)
    return pl.pallas_call(
        body, name="rwkv_chunk_fwd", grid=(nC,),
        in_specs=[row_spec] * 7 + [st_spec, st_spec, pl.BlockSpec((1, SUBLANES, RW_WIDTH), lambda n: (n, 0, 0))],
        out_specs=[row_spec, row_spec, st_spec],
        out_shape=[row_shape, row_shape, jax.ShapeDtypeStruct((nC, H, N, N), F32)],
        scratch_shapes=[pltpu.VMEM((H, N, N), F32)],
        compiler_params=_params(("arbitrary",)),
    )(v, at, bt, kt, rt, a2v, w2v, tinv, w1, plast)


def _rwkv_chunk_bwd(r, lw, k, a, b, v, dy, s0, tinv, w1, a2, w2, sa):
    T = r.shape[0]
    nC = T // RW_CHUNK
    H, N = RW_HEADS, RW_HEAD_DIM

    def body(r_ref, lw_ref, k_ref, a_ref, b_ref, v_ref, dy_ref, s0_ref, ti_ref, w1_ref, a2_ref, w2_ref, sa_ref,
             dr_ref, dlw_ref, dk_ref, da_ref, db_ref, dv_ref, dS_ref):
        @pl.when(pl.program_id(0) == 0)
        def _():
            dS_ref[...] = jnp.zeros_like(dS_ref)

        incl, strict, _ = _chunk_masks()
        incl_f = incl.astype(F32)
        q = _chunk_rows(r_ref[...], lw_ref[...], k_ref[...], a_ref[...], b_ref[...], incl_f)
        At, Bt, Kt, Rt = q["At"], q["Bt"], q["Kt"], q["Rt"]
        A2, W1, W2 = (_heads(lambda h, ref=ref: ref[0, h]) for ref in (a2_ref, w1_ref, w2_ref))
        V, dY, Sa = v_ref[...], dy_ref[...], sa_ref[...]
        hd = _head
        p_last = q["e_pos"][RW_CHUNK - 1:RW_CHUNK, :]
        S0 = _heads(lambda h: s0_ref[0, h])
        G = _heads(lambda h: dS_ref[h] * hd(p_last, h))
        C = RW_CHUNK
        AR = _heads(lambda h: _stack(At, Rt, h))
        BK = _heads(lambda h: _stack(Bt, Kt, h))
        X = _heads(lambda h: S0[h] + _ldot(_stack(Sa, V, h), BK[h], TN))
        dc_last = jnp.concatenate(_heads(lambda h: jnp.sum(G[h] * X[h], axis=0, keepdims=True)), axis=1)
        dSa = _heads(lambda h: _ldot(hd(Bt, h), G[h], NT) + _ldot(W1[h], hd(dY, h), TN))
        dZ = _heads(lambda h: _ldot(ti_ref[0, h], dSa[h], TN))
        D = _heads(lambda h: jnp.concatenate([dZ[h], hd(dY, h)], axis=0))
        for h in range(H):
            dS_ref[h] = G[h] + _ldot(D[h], AR[h], TN)
        both = jnp.concatenate([strict, incl], axis=0)
        E1 = _heads(lambda h: jnp.where(both, _ldot(D[h], hd(Sa, h), NT), 0.0))
        E2 = _heads(lambda h: jnp.where(both, _ldot(D[h], hd(V, h), NT), 0.0))
        cat = lambda fn: jnp.concatenate(_heads(fn), axis=1)
        dV = cat(lambda h: _ldot(jnp.concatenate([A2[h], W2[h]], axis=0), D[h], TN) + _ldot(hd(Kt, h), G[h], NT))
        dAR = _heads(lambda h: _ldot(E1[h], hd(Bt, h), NN) + _ldot(E2[h], hd(Kt, h), NN) + _ldot(D[h], S0[h], NN))
        dAt, dRt = cat(lambda h: dAR[h][:C]), cat(lambda h: dAR[h][C:])
        dBt = cat(lambda h: _ldot(E1[h], AR[h], TN) + _ldot(hd(Sa, h), G[h], NN))
        dKt = cat(lambda h: _ldot(E2[h], AR[h], TN) + _ldot(hd(V, h), G[h], NN))
        last_row = lax.broadcasted_iota(jnp.int32, (RW_CHUNK, RW_WIDTH), 0) == RW_CHUNK - 1
        dc_prev = dAt * At
        dc = dc_prev + dRt * Rt - dBt * Bt - dKt * Kt + jnp.where(last_row, dc_last, 0.0)
        dr_ref[...] = dRt * q["e_pos"]
        dlw_ref[...] = _hdot(incl_f, dc, TN) - dc_prev
        dk_ref[...] = dKt * q["e_neg"]
        da_ref[...] = dAt * q["e_prev"]
        db_ref[...] = dBt * q["e_neg"]
        dv_ref[...] = dV

    rev = lambda n: nC - 1 - n
    row_spec = pl.BlockSpec((RW_CHUNK, RW_WIDTH), lambda n: (rev(n), 0))
    st_spec = pl.BlockSpec((1, H, N, N), lambda n: (rev(n), 0, 0, 0))
    row_shape = jax.ShapeDtypeStruct((T, RW_WIDTH), F32)
    return pl.pallas_call(
        body, name="rwkv_chunk_bwd", grid=(nC,),
        in_specs=[row_spec] * 7 + [st_spec] * 5 + [row_spec], out_specs=[row_spec] * 6,
        out_shape=[row_shape] * 6, scratch_shapes=[pltpu.VMEM((H, N, N), F32)],
        compiler_params=_params(("arbitrary",)),
    )(r, lw, k, a, b, v, dy, s0, tinv, w1, a2, w2, sa)


def _alibi_slope(head):
    return float(np.float32(2.0 ** (-8.0 * (head + 1) / ATT_HEADS)))


ATT_SPAN = ATT_BLOCK * max(ATT_GROUP_DILATION)
ATT_PAIR_WIDTH = 2 * ATT_HEAD_DIM
ATT_SIDE_BY_SIDE = 16


def _pair_slope(g, hp, j):
    return jnp.where(hp == 0, _alibi_slope(4 * g + j), _alibi_slope(4 * g + 2 + j))


def _att_rows(mi, r, d):
    start = mi * ATT_BLOCK * d + r
    return pl.ds(start, ATT_BLOCK) if d == 1 else pl.ds(start, ATT_BLOCK, stride=d)


def _att_masks():
    qi = lax.broadcasted_iota(jnp.int32, (ATT_BLOCK, ATT_BLOCK), 0)
    kj = lax.broadcasted_iota(jnp.int32, (ATT_BLOCK, ATT_BLOCK), 1)
    return qi, kj


NEG = -1e30


def _att_logits(q, k, slope_d, steps, valid):
    s = lax.dot_general(q.astype(BF16), k.astype(BF16), (((1,), (1,)), ((), ())),
                        preferred_element_type=F32) * (ATT_HEAD_DIM ** -0.5)
    return jnp.where(valid, s - slope_d * steps.astype(F32), NEG)


def _att_fwd(p_att, g):
    T = p_att.shape[0]
    d = ATT_GROUP_DILATION[g]
    W = ATT_PAIR_WIDTH
    nb = T // ATT_SPAN
    mb = ATT_SPAN // (ATT_BLOCK * d)

    def body(q_ref, kc_ref, kp_ref, vc_ref, vp_ref, o_ref, l_ref):
        hp, n = pl.program_id(0), pl.program_id(1)
        qi, kj = _att_masks()
        slopes = [_pair_slope(g, hp, j) * d for j in range(2)]
        blocks = [(r, mi) for r in range(d) for mi in range(mb)]
        for at in range(0, len(blocks), ATT_SIDE_BY_SIDE):
            tasks = []
            for r, mi in blocks[at:at + ATT_SIDE_BY_SIDE]:
                rows = _att_rows(mi, r, d)
                if mi > 0:
                    prev = _att_rows(mi - 1, r, d)
                    kp, vp, has_prev = kc_ref[prev, :], vc_ref[prev, :], True
                else:
                    prev = _att_rows(mb - 1, r, d)
                    kp, vp, has_prev = kp_ref[prev, :], vp_ref[prev, :], n > 0
                q, kc, vc = q_ref[rows, :], kc_ref[rows, :], vc_ref[rows, :]
                for j in range(2):
                    sl = slice(j * ATT_HEAD_DIM, (j + 1) * ATT_HEAD_DIM)
                    tasks.append((q[:, sl], kc[:, sl], kp[:, sl], vc[:, sl], vp[:, sl], has_prev, slopes[j]))
            lc = [_att_logits(t[0], t[1], t[6], qi - kj, kj <= qi) for t in tasks]
            lp = [_att_logits(t[0], t[2], t[6], qi - kj + ATT_BLOCK, (kj >= qi) & t[5]) for t in tasks]
            mx = [jnp.maximum(jnp.max(a, axis=1, keepdims=True), jnp.max(b, axis=1, keepdims=True))
                  for a, b in zip(lc, lp, strict=True)]
            ec = [jnp.exp(a - m) for a, m in zip(lc, mx, strict=True)]
            ep = [jnp.exp(b - m) for b, m in zip(lp, mx, strict=True)]
            den = [jnp.sum(a, axis=1, keepdims=True) + jnp.sum(b, axis=1, keepdims=True)
                   for a, b in zip(ec, ep, strict=True)]
            inv = [1.0 / s for s in den]
            outs = [jnp.dot((a * i).astype(BF16), t[3].astype(BF16), preferred_element_type=F32)
                    + jnp.dot((b * i).astype(BF16), t[4].astype(BF16), preferred_element_type=F32)
                    for a, b, i, t in zip(ec, ep, inv, tasks, strict=True)]
            lses = [jnp.broadcast_to(m + jnp.log(s), (ATT_BLOCK, ATT_HEAD_DIM)) for m, s in zip(mx, den, strict=True)]
            for i, (r, mi) in enumerate(blocks[at:at + ATT_SIDE_BY_SIDE]):
                rows = _att_rows(mi, r, d)
                o_ref[rows, :] = jnp.concatenate(outs[2 * i:2 * i + 2], axis=1)
                l_ref[rows, :] = jnp.concatenate(lses[2 * i:2 * i + 2], axis=1)

    def spec(col0, prev):
        if prev:
            return pl.BlockSpec((ATT_SPAN, W), lambda hp, n: (jnp.maximum(n - 1, 0), col0 + 2 * g + hp))
        return pl.BlockSpec((ATT_SPAN, W), lambda hp, n: (n, col0 + 2 * g + hp))

    o_spec = pl.BlockSpec((ATT_SPAN, W), lambda hp, n: (n, hp))
    o, l = pl.pallas_call(
        body, name=f"att_fwd_g{g}", grid=(2, nb),
        in_specs=[spec(0, False), spec(6, False), spec(6, True), spec(12, False), spec(12, True)],
        out_specs=[o_spec, o_spec],
        out_shape=[jax.ShapeDtypeStruct((T, ATT_GROUP_WIDTH), F32)] * 2,
        compiler_params=_params(("parallel", "arbitrary")),
    )(p_att, p_att, p_att, p_att, p_att)
    return o, l


def _att_bwd(p_att, o, l, do, dl, g):
    T = p_att.shape[0]
    d = ATT_GROUP_DILATION[g]
    W = ATT_PAIR_WIDTH
    nb = T // ATT_SPAN
    mb = ATT_SPAN // (ATT_BLOCK * d)
    scale = ATT_HEAD_DIM ** -0.5

    def body(q_ref, k_ref, v_ref, o_ref, l_ref, do_ref, dl_ref,
             qn_ref, on_ref, ln_ref, don_ref, dln_ref, dq_ref, dk_ref, dv_ref, carry_ref):
        hp, n = pl.program_id(0), pl.program_id(1)
        qi, kj = _att_masks()

        @pl.when(n == 0)
        def _():
            carry_ref[...] = jnp.zeros_like(carry_ref)

        slopes = [_pair_slope(g, hp, j) * d for j in range(2)]
        blocks = [(r, mi) for r in range(d) for mi in range(mb)]
        side_by_side = ATT_SIDE_BY_SIDE // 2
        carry = None
        for at in range(0, len(blocks), side_by_side):
            tasks = []
            for r, mi in blocks[at:at + side_by_side]:
                rows = _att_rows(mi, r, d)
                if mi < mb - 1:
                    nrows = _att_rows(mi + 1, r, d)
                    nxt = (q_ref[nrows, :], o_ref[nrows, :], l_ref[nrows, :], do_ref[nrows, :], dl_ref[nrows, :])
                    has_next = True
                else:
                    nrows = _att_rows(0, r, d)
                    nxt = (qn_ref[nrows, :], on_ref[nrows, :], ln_ref[nrows, :], don_ref[nrows, :],
                           dln_ref[nrows, :])
                    has_next = n < nb - 1
                cur = (q_ref[rows, :], o_ref[rows, :], l_ref[rows, :], do_ref[rows, :], dl_ref[rows, :])
                k_all, v_all = k_ref[rows, :], v_ref[rows, :]
                for j in range(2):
                    sl = slice(j * ATT_HEAD_DIM, (j + 1) * ATT_HEAD_DIM)
                    for blk, steps, valid in ((cur, qi - kj, kj <= qi),
                                              (nxt, qi - kj + ATT_BLOCK, (kj >= qi) & has_next)):
                        q, o_, lse, do_, dlse = (z[:, sl] for z in blk)
                        tasks.append(dict(q=q, o=o_, lse=lse[:, :1], do=do_, dlse=dlse[:, :1], steps=steps,
                                          valid=valid, k=k_all[:, sl], vb=v_all[:, sl].astype(BF16),
                                          slope=slopes[j]))
            p = [jnp.exp(_att_logits(t["q"], t["k"], t["slope"], t["steps"], t["valid"]) - t["lse"]) for t in tasks]
            dp = [lax.dot_general(t["do"].astype(BF16), t["vb"], (((1,), (1,)), ((), ())),
                                  preferred_element_type=F32) for t in tasks]
            dsum = [jnp.sum(t["do"] * t["o"], axis=1, keepdims=True) for t in tasks]
            ds = [a * (b - s + t["dlse"]) for a, b, s, t in zip(p, dp, dsum, tasks, strict=True)]
            dv_ = [jnp.dot(a.T.astype(BF16), t["do"].astype(BF16), preferred_element_type=F32)
                   for a, t in zip(p, tasks, strict=True)]
            dk_ = [jnp.dot(a.T.astype(BF16), t["q"].astype(BF16), preferred_element_type=F32) * scale
                   for a, t in zip(ds, tasks, strict=True)]
            dq_ = [jnp.dot(a.astype(BF16), t["k"].astype(BF16), preferred_element_type=F32) * scale
                   for a, t in zip(ds, tasks, strict=True)]
            for i, (r, mi) in enumerate(blocks[at:at + side_by_side]):
                rows = _att_rows(mi, r, d)
                b = 4 * i
                if mi == 0:
                    carry = carry_ref[r]
                dq_ref[rows, :] = jnp.concatenate([dq_[b], dq_[b + 2]], axis=1) + carry
                carry = jnp.concatenate([dq_[b + 1], dq_[b + 3]], axis=1)
                if mi == mb - 1:
                    carry_ref[r] = carry
                dk_ref[rows, :] = jnp.concatenate([dk_[b] + dk_[b + 1], dk_[b + 2] + dk_[b + 3]], axis=1)
                dv_ref[rows, :] = jnp.concatenate([dv_[b] + dv_[b + 1], dv_[b + 2] + dv_[b + 3]], axis=1)

    head_rows = ATT_BLOCK * d
    nxt_n = lambda n: jnp.minimum((n + 1) * mb, T // head_rows - 1)
    cur_p = lambda col0: pl.BlockSpec((ATT_SPAN, W), lambda hp, n: (n, col0 + 2 * g + hp))
    cur_o = pl.BlockSpec((ATT_SPAN, W), lambda hp, n: (n, hp))
    nxt_o = pl.BlockSpec((head_rows, W), lambda hp, n: (nxt_n(n), hp))
    dq, dk, dv = pl.pallas_call(
        body, name=f"att_bwd_g{g}", grid=(2, nb),
        in_specs=[cur_p(0), cur_p(6), cur_p(12), cur_o, cur_o, cur_o, cur_o,
                  pl.BlockSpec((head_rows, W), lambda hp, n: (nxt_n(n), 2 * g + hp)), nxt_o, nxt_o, nxt_o, nxt_o],
        out_specs=[cur_o, cur_o, cur_o],
        out_shape=[jax.ShapeDtypeStruct((T, ATT_GROUP_WIDTH), F32)] * 3,
        scratch_shapes=[pltpu.VMEM((d, ATT_BLOCK, W), F32)],
        compiler_params=_params(("parallel", "arbitrary")),
    )(p_att, p_att, p_att, o, l, do, dl, p_att, o, l, do, dl)
    return dq, dk, dv


FFN_TILE = 2 * D_FF // N_CHIPS
RKV = 3 * RW_WIDTH
WA = 128
XG = 160
RW_COLS = RKV + WA + XG


def _local_step(x, p, W, target, late_weights=None, early_grads=None, by_chip=False, grad_dtype=F32):
    T = x.shape[0]
    tT = 256
    bd512 = _block_diag_ones(RW_WIDTH, RW_HEAD_DIM)
    bd256 = _block_diag_ones(ATT_GROUP_WIDTH, ATT_HEAD_DIM)
    G = {}
    W = dict(W)

    w_in = W["w_in"]
    w_rkv, w_wa, w_xg, w_att = (w_in[:, :RKV], w_in[:, RKV:RKV + WA], w_in[:, RKV + WA:RW_COLS],
                                w_in[:, RW_COLS:])
    mu = W["rw_mu"]
    mu_rkv, mu_wa, mu_xg = mu[:, :RKV], mu[:, RKV:RKV + WA], mu[:, RKV + WA:]
    zpad = jnp.zeros((64, RW_WIDTH), W["rw_w_up"].dtype)
    w_up_pad = jnp.concatenate([W["rw_w_up"], zpad], axis=0)
    a_up_pad = jnp.concatenate([zpad, W["rw_a_up"]], axis=0)
    r_k = W["rw_r_k"].reshape(1, RW_WIDTH)

    (h,) = _rowwise("norm_mix", lambda i, n, r, pv, nx, c: [_rms_fwd(r[0], c[0])], T, tT,
                    rows=[x], consts=[W["g_mix"]], outs=[("row", D_MODEL, BF16)])
    p_rkv = _mm("proj_rkv", h, w_rkv, "nn")
    p_wa = _mm("proj_wa", h, w_wa, "nn")
    p_xg = _mm("proj_xg", h, w_xg, "nn")
    p_att = _mm("proj_att", h, w_att, "nn", tn=768)
    z_gate = _mm("proj_gate", h, W["w_gate"], "nn")

    def rw_pre_core(i, rows, prevs, consts):
        prkv, pwa, pxg = rows[:3]
        (mrkv, mwa, mxg, w0, a0, k_k, k_a, wup, aup, gup, bd) = consts[:11]
        m_rkv = prkv + (_shift_down(prkv, prevs[0], i, 1) - prkv) * mrkv
        m_wa = pwa + (_shift_down(pwa, prevs[1], i, 1) - pwa) * mwa
        m_xg = pxg + (_shift_down(pxg, prevs[2], i, 1) - pxg) * mxg
        r, k, v = m_rkv[:, :RW_WIDTH], m_rkv[:, RW_WIDTH:2 * RW_WIDTH], m_rkv[:, 2 * RW_WIDTH:]
        tw = jnp.tanh(m_wa)
        lw = w0 + jnp.dot(tw.astype(BF16), wup.astype(BF16), preferred_element_type=F32)
        wlog = -_softplus(-lw) - 0.5
        log_decay = -jnp.exp(wlog)
        a = _sigmoid(a0 + jnp.dot(m_wa.astype(BF16), aup.astype(BF16), preferred_element_type=F32))
        sg = _sigmoid(m_xg)
        gate = jnp.dot(sg.astype(BF16), gup.astype(BF16), preferred_element_type=F32)
        kkp = k * k_k
        nrm = jnp.sqrt(_segsum(kkp * kkp, bd))
        nrm_c = jnp.maximum(nrm, 1e-12)
        kk = kkp / nrm_c
        k2 = k * (1.0 + (a - 1.0) * k_a)
        return dict(r=r, k=k, v=v, tw=tw, lw=lw, wlog=wlog, log_decay=log_decay, a=a, sg=sg, gate=gate, kkp=kkp,
                    nrm=nrm, nrm_c=nrm_c, kk=kk, k2=k2, m_rkv=m_rkv, m_wa=m_wa, m_xg=m_xg)

    pre_consts = [mu_rkv, mu_wa, mu_xg, W["rw_w0"], W["rw_a0"], W["rw_k_k"], W["rw_k_a"],
                  w_up_pad, a_up_pad, W["rw_g_up"], bd512]

    def rw_pre(i, n, rows, prevs, nexts, consts):
        q = rw_pre_core(i, rows, prevs, consts)
        return [q["r"], q["log_decay"], q["k2"], q["v"], -q["kk"], q["kk"] * q["a"], q["gate"]]

    r_s, w_s, k_s, v_s, a_s, b_s, gate_s = _rowwise(
        "rwkv_pre", rw_pre, T, tT, rows=[p_rkv, p_wa, p_xg], prevs=[p_rkv, p_wa, p_xg], consts=pre_consts,
        outs=[("row", RW_WIDTH, F32)] * 7)
    (at_s, bt_s, kt_s, rt_s, a2v_s, w2v_s, tinv_s, w1_s, a2_s, w2_s,
     plast_s) = _rwkv_chunk_prep(r_s, w_s, k_s, a_s, b_s, v_s)
    y_scan, sa_s, s0_s = _rwkv_chunk_fwd(v_s, at_s, bt_s, kt_s, rt_s, a2v_s, w2v_s, tinv_s, w1_s, plast_s)

    def rw_post_core(rows, consts):
        y, r, k2, v, gate = rows[:5]
        ln_g, ln_b, rk, bd = consts[:4]
        mean = _segsum(y, bd) * (1.0 / RW_HEAD_DIM)
        yc = y - mean
        var = _segsum(yc * yc, bd) * (1.0 / RW_HEAD_DIM)
        rstd = lax.rsqrt(var + RW_LN_EPS)
        yn = yc * rstd
        s = _segsum(r * k2 * rk, bd)
        return dict(yn=yn, rstd=rstd, s=s, pre=yn * ln_g + ln_b + s * v)

    post_consts = [W["rw_ln_g"], W["rw_ln_b"], r_k, bd512]
    (y_a,) = _rowwise("rwkv_post", lambda i, n, r, pv, nx, c: [rw_post_core(r, c)["pre"] * r[4]], T, tT,
                      rows=[y_scan, r_s, k_s, v_s, gate_s], consts=post_consts, outs=[("row", RW_WIDTH, BF16)])

    att = [_att_fwd(p_att, g) for g in range(3)]

    def comb_weights(ls):
        mx = jnp.maximum(jnp.maximum(ls[0], ls[1]), ls[2])
        es = [jnp.exp(l - mx) for l in ls]
        den = es[0] + es[1] + es[2]
        return [e / den for e in es]

    def att_comb(i, n, rows, pv, nx, c):
        wts = comb_weights(rows[3:6])
        return [wts[0] * rows[0] + wts[1] * rows[1] + wts[2] * rows[2]]

    (y_b,) = _rowwise("att_combine", att_comb, T, tT, rows=[att[0][0], att[1][0], att[2][0], att[0][1], att[1][1],
                                                            att[2][1]], outs=[("row", ATT_GROUP_WIDTH, BF16)])

    if late_weights is not None:
        W.update(late_weights(y_b))
    br_a = _mm("branch_a", y_a, W["w_branch_a"], "nn")
    br_b = _mm("branch_b", y_b, W["w_branch_b"], "nn")

    def merge(i, n, rows, pv, nx, c):
        gates = _sigmoid(rows[0] + c[0])
        return [gates[:, :D_MODEL] * rows[1] + gates[:, D_MODEL:] * rows[2]]

    (merged,) = _rowwise("merge", merge, T, tT, rows=[z_gate, br_a, br_b], consts=[W["b_gate"]],
                         outs=[("row", D_MODEL, BF16)])
    with_norm = lambda res, rows, consts: [res, _rms_fwd(res, consts[0])]
    stream_and_norm = [("row", D_MODEL, F32), ("row", D_MODEL, BF16)]
    x1, h2 = _mm("mix_out", merged, W["w_out"], "nn", add=x, post=(with_norm, [], [W["g_ffn"]], stream_and_norm))

    u = _mm("ffn_up", h2, W["w_up"], "nn", tn=FFN_TILE)

    def conv_core(i, rows, prevs, consts):
        uu, cw, cb = rows[0], consts[0], consts[1]
        u1 = _shift_down(uu, prevs[0], i, 1)
        u2 = _shift_down(uu, prevs[0], i, 2)
        uc = cb + cw[0:1] * uu + cw[1:2] * u1 + cw[2:3] * u2
        return uc[:, :D_FF], uc[:, D_FF:], u1, u2

    def glu(i, n, rows, prevs, nx, consts):
        gate, val, _, _ = conv_core(i, rows, prevs, consts)
        return [_gelu(gate) * val]

    tF = 128
    (act,) = _rowwise("conv_glu", glu, T, tF, rows=[u], prevs=[u], consts=[W["conv_w"], W["conv_b"]],
                      outs=[("row", D_FF, BF16)])
    x2, h3 = _mm("ffn_down", act, W["w_down"], "nn", add=x1, post=(with_norm, [], [W["g_ple"]], stream_and_norm))

    e_ple = _mm("ple_emb", p, W["w_ple"], "nn")

    def head(i, n, rows, pv, nx, consts):
        x2_, z, e, tgt = rows
        pg = _sigmoid(z)
        x3 = x2_ + pg * e
        y = _rms_fwd(x3, consts[0])
        err = y - tgt
        loss = 0.5 * jnp.sum(jnp.sum(err * err, axis=1, keepdims=True) * (1.0 / D_MODEL), axis=0, keepdims=True)
        dy = err * (1.0 / D_MODEL)
        dx3, dgf = _rms_bwd(x3, consts[0], dy)
        return [dx3, dx3 * pg, dx3 * e * pg * (1.0 - pg), jnp.broadcast_to(loss, (1, LANES)), _colsum(dgf)]

    dx3, de, dz, loss_acc, G["g_final"] = _mm(
        "ple_gate_loss_head", h3, W["w_ple_gate"], "nn", tm=512,
        post=(lambda res, rows, consts: head(0, 0, [rows[0], res, rows[1], rows[2]], [], [], consts),
              [x2, e_ple, target], [W["g_final"].reshape(1, D_MODEL)],
              [("row", D_MODEL, F32), ("row", D_MODEL, BF16), ("row", D_MODEL, BF16), ("acc", (1, LANES)),
               ("acc", (1, D_MODEL))]))
    G["w_ple"] = _mm("d_w_ple", p, de, "tn", grad_dtype, out_by_chip=by_chip)
    G["w_ple_gate"] = _mm("d_w_ple_gate", h3, dz, "tn", grad_dtype)
    def norm_bwd(i, n, rows, pv, nx, consts):
        dx, dg = _rms_bwd(rows[0], consts[0], rows[1])
        return [rows[2] + dx, _colsum(dg)]

    through_norm = lambda res, rows, consts: norm_bwd(0, 0, [rows[0], res, rows[1]], [], [], consts)
    stream_and_gain = [("row", D_MODEL, F32), ("acc", (1, D_MODEL))]
    dx2, G["g_ple"] = _mm("d_h3", dz, W["w_ple_gate"], "nt", tm=512,
                          post=(through_norm, [x2, dx3], [W["g_ple"]], stream_and_gain))

    dact = _mm("d_act", dx2, W["w_down"], "nt")
    G["w_down"] = _mm("d_w_down", act, dx2, "tn", grad_dtype)

    def glu_grad(gate, val, da):
        act_, slope = _gelu_and_grad(gate)
        return jnp.concatenate([da * val * slope, da * act_], axis=1)

    def glu_bwd(i, n, rows, prevs, nexts, consts):
        uu, da = rows
        cw = consts[0]
        gate, val, u1, u2 = conv_core(i, rows, prevs, consts)
        duc = glu_grad(gate, val, da)
        dcw = jnp.concatenate([_colsum(duc * uu), _colsum(duc * u1), _colsum(duc * u2)], axis=0)
        gate_n, val_n, _, _ = conv_core(1, [nexts[0]], [uu[tF - SUBLANES:]], consts)
        duc_n = glu_grad(gate_n, val_n, nexts[1])
        du = (cw[0:1] * duc + cw[1:2] * _shift_up(duc, duc_n, i, n, 1) + cw[2:3] * _shift_up(duc, duc_n, i, n, 2))
        return [du, _colsum(duc), dcw]

    du, G["conv_b"], G["conv_w"] = _rowwise(
        "d_conv_glu", glu_bwd, T, tF, rows=[u, dact], prevs=[u], nexts=[u, dact],
        consts=[W["conv_w"], W["conv_b"]],
        outs=[("row", 2 * D_FF, BF16), ("acc", (1, 2 * D_FF)), ("acc", (3, 2 * D_FF))])
    G["w_up"] = _mm("d_w_up", h2, du, "tn", grad_dtype, out_by_chip=by_chip, tn=FFN_TILE)
    dh2 = _mm("d_h2", du, W["w_up"], "nt", tk=FFN_TILE)
    dx1, G["g_ffn"] = _rowwise("d_norm_ffn", norm_bwd, T, tT, rows=[x1, dh2, dx2], consts=[W["g_ffn"]],
                               outs=[("row", D_MODEL, F32), ("acc", (1, D_MODEL))])

    b_gate = W["b_gate"]
    if early_grads is not None:
        b_gate = b_gate + early_grads(G, 0)[0:1, 0:1]
    G["w_out"] = _mm("d_w_out", merged, dx1, "tn", grad_dtype)

    def merge_bwd(dm, rows, consts):
        z, a_, b_ = rows
        gates = _sigmoid(z + consts[0])
        ga, gb = gates[:, :D_MODEL], gates[:, D_MODEL:]
        dz_ = jnp.concatenate([dm * a_ * ga * (1.0 - ga), dm * b_ * gb * (1.0 - gb)], axis=1)
        return [dm * ga, dm * gb, dz_, _colsum(dz_)]

    d_br_a, d_br_b, dz_gate, G["b_gate"] = _mm(
        "d_merged", dx1, W["w_out"], "nt", tm=512,
        post=(merge_bwd, [z_gate, br_a, br_b], [b_gate],
              [("row", D_MODEL, BF16), ("row", D_MODEL, BF16), ("row", 2 * D_MODEL, BF16),
               ("acc", (1, 2 * D_MODEL))]))
    G["w_branch_a"] = _mm("d_w_branch_a", y_a, d_br_a, "tn", grad_dtype, out_by_chip=by_chip)
    G["w_branch_b"] = _mm("d_w_branch_b", y_b, d_br_b, "tn", grad_dtype, out_by_chip=by_chip)
    G["w_gate"] = _mm("d_w_gate", h, dz_gate, "tn", grad_dtype, out_by_chip=by_chip)
    if early_grads is not None:
        post_consts = [post_consts[0] + early_grads(G, 1)[0:1, 0:1]] + post_consts[1:]
    dy_a = _mm("d_y_a", d_br_a, W["w_branch_a"], "nt")
    def att_comb_bwd(dy, rows, consts):
        os_, ls = rows[0:3], rows[3:6]
        wts = comb_weights(ls)
        dws = [_segsum(dy * o_, consts[0]) for o_ in os_]
        mix = wts[0] * dws[0] + wts[1] * dws[1] + wts[2] * dws[2]
        return [wts[g_] * dy for g_ in range(3)] + [wts[g_] * (dws[g_] - mix) for g_ in range(3)]

    comb = _mm("d_y_b", d_br_b, W["w_branch_b"], "nt",
               post=(att_comb_bwd, [att[0][0], att[1][0], att[2][0], att[0][1], att[1][1], att[2][1]], [bd256],
                     [("row", ATT_GROUP_WIDTH, F32)] * 6))
    dqkv = [_att_bwd(p_att, att[g][0], att[g][1], comb[g], comb[3 + g], g) for g in range(3)]
    dp_att = jnp.concatenate([dqkv[g][part] for part in range(3) for g in range(3)], axis=1).astype(BF16)

    def rw_post_bwd(i, n, rows, pv, nx, consts):
        y, r, k2, v, gate, dya = rows
        ln_g, ln_b, rk, bd = consts
        q = rw_post_core(rows, consts)
        dpre = dya * gate
        dgate = dya * q["pre"]
        dyn = dpre * ln_g
        inv = 1.0 / RW_HEAD_DIM
        dy_scan = q["rstd"] * (dyn - _segsum(dyn, bd) * inv - q["yn"] * (_segsum(dyn * q["yn"], bd) * inv))
        ds = _segsum(dpre * v, bd)
        return [dy_scan, dgate, ds * k2 * rk, ds * r * rk, dpre * q["s"],
                _colsum(dpre * q["yn"]), _colsum(dpre), _colsum(ds * r * k2)]

    dy_scan, dgate, dr_b, dk2_b, dv_b, G["rw_ln_g"], G["rw_ln_b"], d_rk = _rowwise(
        "d_rwkv_post", rw_post_bwd, T, tT, rows=[y_scan, r_s, k_s, v_s, gate_s, dy_a], consts=post_consts,
        outs=[("row", RW_WIDTH, F32)] * 5 + [("acc", (1, RW_WIDTH))] * 3)
    G["rw_r_k"] = d_rk.reshape(RW_HEADS, RW_HEAD_DIM)

    dr_s, dw_s, dk_s, da_s, db_s, dv_s = _rwkv_chunk_bwd(r_s, w_s, k_s, a_s, b_s, v_s, dy_scan, s0_s, tinv_s, w1_s,
                                                         a2_s, w2_s, sa_s)

    def rw_pre_bwd(i, n, rows, prevs, nx, consts):
        q = rw_pre_core(i, rows, prevs, consts)
        (mrkv, mwa, mxg, w0, a0, k_k, k_a, wup, aup, gup, bd) = consts
        dr, dlogdecay, dk2, dv, dav, dbv, dgate_ = rows[3:10]
        dr = dr + rows[10]
        dk2 = dk2 + rows[11]
        dv = dv + rows[12]
        a, k, kk = q["a"], q["k"], q["kk"]
        dk = dk2 * (1.0 + (a - 1.0) * k_a)
        da = dk2 * k * k_a + dbv * kk
        dkk = dbv * a - dav
        live = q["nrm"] > 1e-12
        dkkp = jnp.where(live, dkk - kk * _segsum(dkk * kk, bd), dkk) / q["nrm_c"]
        dk = dk + dkkp * k_k
        dlw = dlogdecay * q["log_decay"] * _sigmoid(-q["lw"])
        dla = da * a * (1.0 - a)
        nt = (((1,), (1,)), ((), ()))
        dtw = lax.dot_general(dlw.astype(BF16), wup.astype(BF16), nt, preferred_element_type=F32)
        dxa = lax.dot_general(dla.astype(BF16), aup.astype(BF16), nt, preferred_element_type=F32)
        dm_wa = dtw * (1.0 - q["tw"] * q["tw"]) + dxa
        dsg = lax.dot_general(dgate_.astype(BF16), gup.astype(BF16), nt, preferred_element_type=F32)
        dm_xg = dsg * q["sg"] * (1.0 - q["sg"])
        dm_rkv = jnp.concatenate([dr, dk, dv], axis=1)
        prkv, pwa, pxg = rows[:3]
        dmu = jnp.concatenate([_colsum(dm_rkv * (_shift_down(prkv, prevs[0], i, 1) - prkv)),
                               _colsum(dm_wa * (_shift_down(pwa, prevs[1], i, 1) - pwa)),
                               _colsum(dm_xg * (_shift_down(pxg, prevs[2], i, 1) - pxg))], axis=1)
        return [dm_rkv, dm_wa, dm_xg, dlw, dla, q["tw"], q["m_wa"], q["sg"], dmu,
                _colsum(dlw), _colsum(dla), _colsum(dkkp * k), _colsum(dk2 * k * (a - 1.0))]

    (dm_rkv, dm_wa, dm_xg, dlw, dla, tw_s, mwa_s, sg_s, G["rw_mu"], G["rw_w0"], G["rw_a0"], G["rw_k_k"],
     G["rw_k_a"]) = _rowwise(
        "d_rwkv_pre", rw_pre_bwd, T, tT,
        rows=[p_rkv, p_wa, p_xg, dr_s, dw_s, dk_s, dv_s, da_s, db_s, dgate, dr_b, dk2_b, dv_b],
        prevs=[p_rkv, p_wa, p_xg], consts=pre_consts,
        outs=[("row", RKV, F32), ("row", WA, F32), ("row", XG, F32), ("row", RW_WIDTH, BF16),
              ("row", RW_WIDTH, BF16), ("row", WA, BF16), ("row", WA, BF16), ("row", XG, BF16),
              ("acc", (1, RW_COLS))] + [("acc", (1, RW_WIDTH))] * 4)
    G["rw_w_up"] = _mm("d_rw_w_up", tw_s, dlw, "tn", grad_dtype)[:64]
    G["rw_a_up"] = _mm("d_rw_a_up", mwa_s, dla, "tn", grad_dtype)[64:]
    G["rw_g_up"] = _mm("d_rw_g_up", sg_s, dgate, "tn", grad_dtype)

    def shift_bwd(i, n, rows, pv, nexts, consts):
        return [rows[j] * (1.0 - consts[j]) + _shift_up(rows[j], nexts[j], i, n, 1) * consts[j] for j in range(3)]

    dp_rkv, dp_wa, dp_xg = _rowwise(
        "d_token_shift", shift_bwd, T, tT, rows=[dm_rkv, dm_wa, dm_xg], nexts=[dm_rkv, dm_wa, dm_xg],
        consts=[mu_rkv, mu_wa, mu_xg], outs=[("row", RKV, BF16), ("row", WA, BF16), ("row", XG, BF16)])

    G["w_in"] = jnp.concatenate([_mm("d_w_rkv", h, dp_rkv, "tn", grad_dtype), _mm("d_w_wa", h, dp_wa, "tn", grad_dtype),
                                 _mm("d_w_xg", h, dp_xg, "tn", grad_dtype), _mm("d_w_att", h, dp_att, "tn", grad_dtype, tn=768)], axis=1)
    if early_grads is not None:
        w_wa = w_wa + early_grads(G, 2)[0:1, 0:1].astype(w_wa.dtype)
    dh = _mm("d_h_gate", dz_gate, W["w_gate"], "nt")
    dh = _mm("d_h_rkv", dp_rkv, w_rkv, "nt", add=dh)
    dh = _mm("d_h_wa", dp_wa, w_wa, "nt", add=dh)
    dh = _mm("d_h_xg", dp_xg, w_xg, "nt", add=dh)
    dx, G["g_mix"] = _mm("d_h_att", dp_att, w_att, "nt", add=dh, tm=512,
                         post=(through_norm, [x, dx1], [W["g_mix"]], stream_and_gain))
    return loss_acc[:, :1], dx, G


HBM_SPEC = pl.BlockSpec(memory_space=pltpu.HBM)


def _place():
    x, y, c = lax.axis_index("x"), lax.axis_index("y"), lax.axis_index("c")
    return x, y, c, [(1 - x, y), (x, 1 - y), (1 - x, 1 - y)]


def _remote(src, dst, send_sems, recv_sems, k, to):
    return pltpu.make_async_remote_copy(src_ref=src, dst_ref=dst, send_sem=send_sems.at[k], recv_sem=recv_sems.at[k],
                                        device_id=to, device_id_type=MESH)


ROW_ALIGN = 16


def _splits(rows):
    return rows % (2 * ROW_ALIGN) == 0


def _half_rows(ref_rows, c, first):
    half = ref_rows // 2
    which = c if first else 1 - c
    return pl.ds(pl.multiple_of(which * half, ROW_ALIGN), half)


def _gather_chips(shards):
    n = len(shards)
    split = [_splits(s.shape[0]) for s in shards]

    def body(*refs):
        w_refs, out_refs = refs[:n], refs[n:2 * n]
        send_sems, recv_sems = refs[2 * n:]
        x, y, c, chips = _place()
        me = 2 * x + y
        sends, passed = [], []
        for i in range(n):
            for j, (px, py) in enumerate(chips):
                if split[i]:
                    mine = _half_rows(w_refs[i].shape[0], c, True)
                    cp = _remote(w_refs[i].at[mine], out_refs[i].at[me, mine], send_sems, recv_sems, 6 * i + j,
                                 (px, py, c))
                else:
                    cp = _remote(w_refs[i], out_refs[i].at[me], send_sems, recv_sems, 6 * i + j, (px, py, c))
                cp.start()
                sends.append(cp)
        for i in range(n):
            for j, (px, py) in enumerate(chips):
                if split[i]:
                    landed = out_refs[i].at[2 * px + py, _half_rows(w_refs[i].shape[0], c, True)]
                    _remote(landed, landed, send_sems, recv_sems, 6 * i + j, (px, py, c)).wait_recv()
                    cp = _remote(landed, landed, send_sems, recv_sems, 6 * i + 3 + j, (x, y, 1 - c))
                    cp.start()
                    passed.append(cp)
                else:
                    landed = out_refs[i].at[2 * px + py]
                    _remote(landed, landed, send_sems, recv_sems, 6 * i + j, (px, py, c)).wait_recv()
        for i in range(n):
            if split[i]:
                for j, (px, py) in enumerate(chips):
                    landed = out_refs[i].at[2 * px + py, _half_rows(w_refs[i].shape[0], c, False)]
                    _remote(landed, landed, send_sems, recv_sems, 6 * i + 3 + j, (x, y, 1 - c)).wait_recv()
        for cp in sends + passed:
            cp.wait_send()

    outs = pl.pallas_call(
        body, name="gather_weights", in_specs=[HBM_SPEC] * n, out_specs=[HBM_SPEC] * n,
        out_shape=[jax.ShapeDtypeStruct((N_CHIPS,) + s.shape, s.dtype) for s in shards],
        scratch_shapes=[pltpu.SemaphoreType.DMA((6 * n,)), pltpu.SemaphoreType.DMA((6 * n,))],
    )(*shards)
    me = 2 * lax.axis_index("x") + lax.axis_index("y")
    return [lax.dynamic_update_slice(o, s[None], (me, 0, 0)) for o, s in zip(outs, shards, strict=True)]


def _join_halves(reds):
    n = len(reds)

    def body(*refs):
        r_refs, out_refs = refs[:n], refs[n:2 * n]
        send_sems, recv_sems = refs[2 * n:]
        x, y, c, _ = _place()
        cps = []
        for i in range(n):
            mine = _half_rows(out_refs[i].shape[0], c, True)
            cp = _remote(r_refs[i], out_refs[i].at[mine], send_sems, recv_sems, i, (x, y, 1 - c))
            cp.start()
            cps.append(cp)
        for cp in cps:
            cp.wait()

    outs = pl.pallas_call(
        body, name="join_halves", in_specs=[HBM_SPEC] * n, out_specs=[HBM_SPEC] * n,
        out_shape=[jax.ShapeDtypeStruct((2 * r.shape[0], r.shape[1]), r.dtype) for r in reds],
        scratch_shapes=[pltpu.SemaphoreType.DMA((n,)), pltpu.SemaphoreType.DMA((n,))],
    )(*reds)
    c = lax.axis_index("c")
    return [lax.dynamic_update_slice(o, r, (c * r.shape[0], 0)) for o, r in zip(outs, reds, strict=True)]


SEM_SPEC = pl.BlockSpec(memory_space=pltpu.SEMAPHORE)
PEERS = N_DEV - 1
DATAFLOW = pltpu.SideEffectType.DATAFLOW_SIDE_EFFECTING


def _travel_copies(mode, src_refs, land_refs, send_sems, recv_sems):
    x, y, c, chips = _place()
    me = 2 * x + y
    pairs = []
    for i, (src, land) in enumerate(zip(src_refs, land_refs, strict=True)):
        if mode in ("scatter", "all"):
            for k in range(1, N_DEV):
                px, py, pc = x ^ (k >> 2), y ^ ((k >> 1) & 1), c ^ (k & 1)
                mine = src if mode == "all" else src.at[2 * px + py, _half_rows(src.shape[1], pc, True)]
                there, here = land.at[4 * x + 2 * y + c], land.at[4 * px + 2 * py + pc]
                send = functools.partial(_remote, mine, there, send_sems, recv_sems, PEERS * i + k - 1, (px, py, pc))
                arrival = functools.partial(_remote, mine, here, send_sems, recv_sems, PEERS * i + k - 1, (px, py, pc))
                pairs.append((send, arrival))
            continue
        for j, (px, py) in enumerate(chips):
            peer = 2 * px + py
            if _splits(src.shape[0]):
                rows = _half_rows(src.shape[0], c, True)
                mine, there, here = src.at[rows], land.at[me, rows], land.at[peer, rows]
            else:
                mine, there, here = src, land.at[me], land.at[peer]
            send = functools.partial(_remote, mine, there, send_sems, recv_sems, PEERS * i + j, (px, py, c))
            arrival = functools.partial(_remote, mine, here, send_sems, recv_sems, PEERS * i + j, (px, py, c))
            pairs.append((send, arrival))
    return pairs


def _share_halves(name, lands):
    idx = [i for i, a in enumerate(lands) if _splits(a.shape[1])]
    n = len(idx)

    def body(*refs):
        in_refs, out_refs = refs[:n], refs[n:2 * n]
        send_sems, recv_sems = refs[2 * n:]
        x, y, c, chips = _place()
        cps = []
        for i, (src, dst) in enumerate(zip(in_refs, out_refs, strict=True)):
            for j, (px, py) in enumerate(chips):
                mine = _half_rows(src.shape[1], c, True)
                cp = _remote(src.at[2 * px + py, mine], dst.at[2 * px + py, mine], send_sems, recv_sems, 3 * i + j,
                             (x, y, 1 - c))
                cp.start()
                cps.append(cp)
        for i, dst in enumerate(out_refs):
            for j, (px, py) in enumerate(chips):
                theirs = dst.at[2 * px + py, _half_rows(dst.shape[1], c, False)]
                _remote(theirs, theirs, send_sems, recv_sems, 3 * i + j, (x, y, 1 - c)).wait_recv()
        for cp in cps:
            cp.wait_send()

    outs = pl.pallas_call(
        body, name=name, in_specs=[HBM_SPEC] * n, out_specs=[HBM_SPEC] * n,
        out_shape=[jax.ShapeDtypeStruct(lands[i].shape, lands[i].dtype) for i in idx],
        input_output_aliases={i: i for i in range(n)},
        scratch_shapes=[pltpu.SemaphoreType.DMA((3 * n,)), pltpu.SemaphoreType.DMA((3 * n,))],
    )(*[lands[i] for i in idx])
    done = list(lands)
    for i, o in zip(idx, outs, strict=True):
        done[i] = o
    return done


def _travel_start(name, mode, srcs):
    n = len(srcs)
    land_shape = {"gather": lambda s: (N_CHIPS,) + s.shape, "all": lambda s: (N_DEV,) + s.shape,
                  "scatter": lambda s: (N_DEV, s.shape[1] // 2, s.shape[2])}[mode]
    lands = [lax.empty(land_shape(s), s.dtype) for s in srcs]

    def body(*refs):
        src_refs, land_refs = refs[:n], refs[n:2 * n]
        send_sems, recv_sems = refs[2 * n], refs[2 * n + 1]
        token = refs[-1]
        for send, _ in _travel_copies(mode, src_refs, land_refs, send_sems, recv_sems):
            send().start()
        token[...] = jnp.zeros_like(token)

    hbm = lambda a: pltpu.HBM(a.shape, a.dtype)
    outs = pl.pallas_call(
        body, name=name,
        out_shape=(pltpu.SemaphoreType.DMA((PEERS * n,)), pltpu.SemaphoreType.DMA((PEERS * n,)),
                   *[hbm(s) for s in srcs],
                   *[hbm(a) for a in lands], jax.ShapeDtypeStruct((SUBLANES, LANES), F32)),
        in_specs=[HBM_SPEC] * (2 * n),
        out_specs=(SEM_SPEC, SEM_SPEC, *[HBM_SPEC] * (2 * n), pl.BlockSpec(memory_space=pltpu.VMEM)),
        input_output_aliases={i: 2 + i for i in range(2 * n)},
        compiler_params=pltpu.CompilerParams(has_side_effects=DATAFLOW),
    )(*[pltpu.with_memory_space_constraint(a, pltpu.HBM) for a in list(srcs) + lands])
    return outs[0], outs[1], list(outs[2:2 + n]), list(outs[2 + n:2 + 2 * n]), outs[-1]


def _travel_wait(name, mode, send_sems, recv_sems, srcs, lands, after):
    n = len(srcs)

    def body(*refs):
        src_refs, land_refs = refs[:n], refs[n:2 * n]
        send_sems_, recv_sems_ = refs[2 * n], refs[2 * n + 1]
        for send, arrival in _travel_copies(mode, src_refs, land_refs, send_sems_, recv_sems_):
            send().wait_send()
            arrival().wait_recv()

    hbm = lambda a: pltpu.HBM(a.shape, a.dtype)
    outs = pl.pallas_call(
        body, name=name, out_shape=tuple(hbm(a) for a in list(srcs) + list(lands)),
        in_specs=[HBM_SPEC] * (2 * n) + [SEM_SPEC, SEM_SPEC, pl.BlockSpec(memory_space=pl.ANY)],
        out_specs=tuple([HBM_SPEC] * (2 * n)), input_output_aliases={i: i for i in range(2 * n)},
        compiler_params=pltpu.CompilerParams(has_side_effects=DATAFLOW),
    )(*srcs, *lands, send_sems, recv_sems, after)
    c = lax.axis_index("c")
    me = 2 * lax.axis_index("x") + lax.axis_index("y")
    if mode == "gather":
        slot, own = me, [s[None] for s in outs[:n]]
    elif mode == "all":
        slot, own = 2 * me + c, [s[None] for s in outs[:n]]
    else:
        slot = 2 * me + c
        own = [lax.dynamic_slice(s, (me, c * (s.shape[1] // 2), 0), (1, s.shape[1] // 2, s.shape[2])) for s in outs[:n]]
    return [lax.dynamic_update_slice(a, o, (slot,) + (0,) * (a.ndim - 1)) for a, o in zip(outs[n:], own, strict=True)]


SUM_TILE_BYTES = 4 * 1024 * 1024


def _sum_rows(half, cols):
    best = ROW_ALIGN
    for t in range(ROW_ALIGN, half + 1, ROW_ALIGN):
        if half % t == 0 and N_CHIPS * t * cols * 4 <= SUM_TILE_BYTES:
            best = t
    return best


def _sum_devices(name, parts):
    n, H, C = parts.shape
    tr = _sum_rows(H, C)

    def body(p_ref, o_ref):
        acc = p_ref[0].astype(F32)
        for k in range(1, n):
            acc = acc + p_ref[k].astype(F32)
        o_ref[...] = acc

    return pl.pallas_call(
        body, name=name, grid=(H // tr,),
        in_specs=[pl.BlockSpec((n, tr, C), lambda i: (0, i, 0))],
        out_specs=pl.BlockSpec((tr, C), lambda i: (i, 0)),
        out_shape=jax.ShapeDtypeStruct((H, C), F32),
        compiler_params=_params(("parallel",)),
    )(parts)


def _adamw_math(w, g, m, v):
    m = ADAM_B1 * m + (1.0 - ADAM_B1) * g
    v = ADAM_B2 * v + (1.0 - ADAM_B2) * (g * g)
    m_hat = m / (1.0 - ADAM_B1 ** ADAM_STEP)
    v_hat = v / (1.0 - ADAM_B2 ** ADAM_STEP)
    delta = -ADAM_LR * (m_hat / (jnp.sqrt(v_hat) + ADAM_EPS) + ADAM_WD * w)
    return delta, m, v


def _adamw(name, w, g, m, v):
    R, C = w.shape
    tr = R
    if R % SUBLANES == 0:
        for cand in range(SUBLANES, min(R, 256) + 1, SUBLANES):
            if R % cand == 0:
                tr = cand

    def body(w_ref, g_ref, m_ref, v_ref, d_ref, nm_ref, nv_ref):
        d, nm, nv = _adamw_math(w_ref[...], g_ref[...], m_ref[...], v_ref[...])
        d_ref[...] = d
        nm_ref[...] = nm
        nv_ref[...] = nv

    spec = pl.BlockSpec((tr, C), lambda i: (i, 0))
    shape = jax.ShapeDtypeStruct((R, C), F32)
    return pl.pallas_call(
        body, name=name, grid=(R // tr,), in_specs=[spec] * 4, out_specs=[spec] * 3, out_shape=[shape] * 3,
        compiler_params=_params(("parallel",)),
    )(w, g, m, v)


SMALL_ROW = 2048


def _small_layout(shapes):
    places, row = [], 0
    for R, C in shapes:
        pieces = []
        for r in range(R):
            for c0 in range(0, C, SMALL_ROW):
                pieces.append((r, c0, min(C, c0 + SMALL_ROW), row))
                row += 1
        places.append(pieces)
    return places, -(-row // SUBLANES) * SUBLANES


def _put_rows(block_ref, refs, places):
    block_ref[...] = jnp.zeros_like(block_ref)
    for ref, pieces in zip(refs, places, strict=True):
        for r, c0, c1, row in pieces:
            block_ref[row:row + 1, 0:c1 - c0] = ref[r:r + 1, c0:c1]


def _take_rows(block, refs, places):
    for ref, pieces in zip(refs, places, strict=True):
        for r, c0, c1, row in pieces:
            ref[r:r + 1, c0:c1] = block[row:row + 1, 0:c1 - c0]


def _pack_small(arrs):
    places, rows = _small_layout([a.shape for a in arrs])

    def body(*refs):
        _put_rows(refs[-1], refs[:-1], places)

    return pl.pallas_call(body, name="pack_small", out_shape=jax.ShapeDtypeStruct((rows, SMALL_ROW), F32),
                          compiler_params=_params())(*arrs)


def _adamw_small(parts, ws, ms, vs, extra_shapes):
    n_dev, rows, _ = parts.shape
    n = len(ws)
    places, rows_ = _small_layout([w.shape for w in ws] + list(extra_shapes))
    assert rows_ == rows, (rows_, rows)

    def body(*refs):
        p_ref = refs[0]
        w_refs, m_refs, v_refs = refs[1:1 + n], refs[1 + n:1 + 2 * n], refs[1 + 2 * n:1 + 3 * n]
        outs = refs[1 + 3 * n:-3]
        wb, mb, vb = refs[-3:]
        for block, srcs in ((wb, w_refs), (mb, m_refs), (vb, v_refs)):
            _put_rows(block, srcs, places[:n])
        g = p_ref[0]
        for k in range(1, n_dev):
            g = g + p_ref[k]
        d, nm, nv = _adamw_math(wb[...], g, mb[...], vb[...])
        _take_rows(g, outs[0:n], places[:n])
        _take_rows(d, outs[n:2 * n], places[:n])
        _take_rows(nm, outs[2 * n:3 * n], places[:n])
        _take_rows(nv, outs[3 * n:4 * n], places[:n])
        _take_rows(g, outs[4 * n:], places[n:])

    shapes = [jax.ShapeDtypeStruct(w.shape, F32) for w in ws]
    res = pl.pallas_call(
        body, name="adamw_small", out_shape=shapes * 4 + [jax.ShapeDtypeStruct(s, F32) for s in extra_shapes],
        scratch_shapes=[pltpu.VMEM((rows, SMALL_ROW), F32)] * 3, compiler_params=_params(),
    )(parts, *ws, *ms, *vs)
    return res[0:n], res[n:2 * n], res[2 * n:3 * n], res[3 * n:4 * n], res[4 * n:]


WEIGHTS = ['g_mix', 'w_in', 'rw_mu', 'rw_w0', 'rw_w_up', 'rw_a0', 'rw_a_up', 'rw_g_up', 'rw_k_k', 'rw_k_a',
           'rw_r_k', 'rw_ln_g', 'rw_ln_b', 'w_branch_a', 'w_branch_b', 'w_gate', 'b_gate', 'w_out', 'g_ffn', 'w_up',
           'conv_w', 'conv_b', 'w_down', 'g_ple', 'w_ple_gate', 'w_ple', 'g_final']
ARG_NAMES = (['x', 'p'] + WEIGHTS + ['loss_target'] + ['m_' + n for n in WEIGHTS] + ['v_' + n for n in WEIGHTS])
SHARDED = {'w_in': 1, 'rw_w_up': 1, 'rw_a_up': 1, 'rw_g_up': 1, 'w_branch_a': 1, 'w_branch_b': 1, 'w_gate': 1,
           'w_out': 0, 'w_up': 1, 'conv_w': 1, 'w_down': 0, 'w_ple_gate': 0, 'w_ple': 1}
SMALL = [n for n in WEIGHTS if n not in SHARDED]
WHOLE = ['conv_w']
FIRST_USED = ['w_in', 'rw_w_up', 'rw_a_up', 'rw_g_up', 'w_gate']
READ_BY_CHIP = ['w_gate', 'w_branch_a', 'w_branch_b', 'w_up', 'w_ple']
FIRST_DONE = [['w_up', 'w_down', 'w_ple_gate', 'w_ple'], ['w_out', 'w_branch_a', 'w_branch_b', 'w_gate'],
              ['w_in', 'rw_w_up', 'rw_a_up', 'rw_g_up']]
SPLIT = [n for n in SHARDED if n not in WHOLE]


def _full_from_shards(stack, axis):
    _, R, C = stack.shape
    if axis == 0:
        return stack.reshape(N_CHIPS * R, C)
    return stack.transpose(1, 0, 2).reshape(R, N_CHIPS * C)


def _shards_from_full(full, axis):
    R, C = full.shape
    if axis == 0:
        return full.reshape(N_CHIPS, R // N_CHIPS, C)
    return full.reshape(R, N_CHIPS, C // N_CHIPS).transpose(1, 0, 2)


def kernel(x, p, g_mix, w_in, rw_mu, rw_w0, rw_w_up, rw_a0, rw_a_up, rw_g_up, rw_k_k, rw_k_a, rw_r_k, rw_ln_g, rw_ln_b, w_branch_a, w_branch_b, w_gate, b_gate, w_out, g_ffn, w_up, conv_w, conv_b, w_down, g_ple, w_ple_gate, w_ple, g_final, loss_target, m_g_mix, m_w_in, m_rw_mu, m_rw_w0, m_rw_w_up, m_rw_a0, m_rw_a_up, m_rw_g_up, m_rw_k_k, m_rw_k_a, m_rw_r_k, m_rw_ln_g, m_rw_ln_b, m_w_branch_a, m_w_branch_b, m_w_gate, m_b_gate, m_w_out, m_g_ffn, m_w_up, m_conv_w, m_conv_b, m_w_down, m_g_ple, m_w_ple_gate, m_w_ple, m_g_final, v_g_mix, v_w_in, v_rw_mu, v_rw_w0, v_rw_w_up, v_rw_a0, v_rw_a_up, v_rw_g_up, v_rw_k_k, v_rw_k_a, v_rw_r_k, v_rw_ln_g, v_rw_ln_b, v_w_branch_a, v_w_branch_b, v_w_gate, v_b_gate, v_w_out, v_g_ffn, v_w_up, v_conv_w, v_conv_b, v_w_down, v_g_ple, v_w_ple_gate, v_w_ple, v_g_final):
    given = dict(zip(ARG_NAMES, (x, p, g_mix, w_in, rw_mu, rw_w0, rw_w_up, rw_a0, rw_a_up, rw_g_up, rw_k_k, rw_k_a, rw_r_k, rw_ln_g, rw_ln_b, w_branch_a, w_branch_b, w_gate, b_gate, w_out, g_ffn, w_up, conv_w, conv_b, w_down, g_ple, w_ple_gate, w_ple, g_final, loss_target, m_g_mix, m_w_in, m_rw_mu, m_rw_w0, m_rw_w_up, m_rw_a0, m_rw_a_up, m_rw_g_up, m_rw_k_k, m_rw_k_a, m_rw_r_k, m_rw_ln_g, m_rw_ln_b, m_w_branch_a, m_w_branch_b, m_w_gate, m_b_gate, m_w_out, m_g_ffn, m_w_up, m_conv_w, m_conv_b, m_w_down, m_g_ple, m_w_ple_gate, m_w_ple, m_g_final, v_g_mix, v_w_in, v_rw_mu, v_rw_w0, v_rw_w_up, v_rw_a0, v_rw_a_up, v_rw_g_up, v_rw_k_k, v_rw_k_a, v_rw_r_k, v_rw_ln_g, v_rw_ln_b, v_w_branch_a, v_w_branch_b, v_w_gate, v_b_gate, v_w_out, v_g_ffn, v_w_up, v_conv_w, v_conv_b, v_w_down, v_g_ple, v_w_ple_gate, v_w_ple, v_g_final), strict=True))

    def two_d(name, prefix=""):
        a = given[prefix + name]
        if name == "g_final":
            return a.reshape(1, D_MODEL)
        if name == "rw_r_k":
            return a.reshape(1, RW_WIDTH)
        return a[0] if a.ndim == 3 else a

    cast = lambda n: two_d(n) if n in WHOLE else two_d(n).astype(BF16)
    whole = lambda names, stacks: {n: g if n in READ_BY_CHIP else _full_from_shards(g, SHARDED[n])
                                   for n, g in zip(names, stacks, strict=True)}
    late_names = [n for n in SHARDED if n not in FIRST_USED]
    late_sends, late_recvs, late_srcs, late_lands, token = _travel_start(
        "gather_late_start", "gather", [cast(n) for n in late_names])
    W = whole(FIRST_USED, _gather_chips([cast(n) for n in FIRST_USED]))
    for n in SMALL:
        W[n] = two_d(n)
    W["rw_r_k"] = W["rw_r_k"].reshape(RW_HEADS, RW_HEAD_DIM)
    W["g_mix"] = W["g_mix"] + token[0:1, 0:1]

    def late_weights(after):
        lands = _travel_wait("gather_late_wait", "gather", late_sends, late_recvs, late_srcs, late_lands, after)
        return whole(late_names, _share_halves("share_late", lands))

    early_names = [[n for n in SPLIT if n in group] for group in FIRST_DONE]
    assert sorted(sum(early_names, [])) == sorted(SPLIT)
    travelling = []

    def early_grads(G, stage):
        by_chip = [G[n] if n in READ_BY_CHIP else _shards_from_full(G[n], SHARDED[n]) for n in early_names[stage]]
        sends, recvs, srcs, lands, started = _travel_start(f"scatter{stage}_start", "scatter", by_chip)
        travelling.append((sends, recvs, srcs, lands))
        return started

    loss_part, grad_x, G = _local_step(x[0], p[0, 0], W, loss_target[0], late_weights, early_grads, by_chip=True,
                                       grad_dtype=BF16)

    G["rw_r_k"] = G["rw_r_k"].reshape(1, RW_WIDTH)
    extras = [G[n] for n in WHOLE] + [loss_part]
    small_sends, small_recvs, small_srcs, small_lands, small_started = _travel_start(
        "gather_small_start", "all", [_pack_small([G[n] for n in SMALL] + extras)])

    landed = {}
    for stage, (sends, recvs, srcs, lands) in enumerate(travelling):
        landed.update(zip(early_names[stage], _travel_wait(f"scatter{stage}_wait", "scatter", sends, recvs, srcs,
                                                           lands, small_started), strict=True))
    reduced = [_sum_devices("sum_devices_" + n, landed[n]) for n in SPLIT]
    shard_grads = dict(zip(SPLIT, _join_halves(reduced), strict=True))

    grads, deltas, new_m, new_v = {}, {}, {}, {}

    def step(n):
        g = shard_grads[n]
        d, nm, nv = _adamw("adamw_" + n, two_d(n), g, two_d(n, "m_"), two_d(n, "v_"))
        grads[n], deltas[n], new_m[n], new_v[n] = g, d, nm, nv

    for n in SPLIT:
        step(n)

    (all_small,) = _travel_wait("gather_small_wait", "all", small_sends, small_recvs, small_srcs, small_lands,
                                deltas[SPLIT[-1]])
    gs, ds, nms, nvs, summed = _adamw_small(all_small, [two_d(n) for n in SMALL], [two_d(n, "m_") for n in SMALL],
                                            [two_d(n, "v_") for n in SMALL], [e.shape for e in extras])
    loss = summed[-1][0, 0]
    chip = 2 * lax.axis_index("x") + lax.axis_index("y")
    for n, full in zip(WHOLE, summed[:-1], strict=True):
        width = two_d(n).shape[1]
        shard_grads[n] = lax.dynamic_slice_in_dim(full, chip * width, width, axis=1)
        step(n)
    for i, n in enumerate(SMALL):
        grads[n], deltas[n], new_m[n], new_v[n] = gs[i], ds[i], nms[i], nvs[i]
    outs = [loss, grad_x[None]]
    for table in (grads, deltas, new_m, new_v):
        outs += [table[n].reshape(given[n].shape) for n in WEIGHTS]
    return tuple(outs)
```

```python
import functools
import math

import jax
import jax.numpy as jnp
import numpy as np
from jax import lax
from jax.experimental import pallas as pl
from jax.experimental.pallas import tpu as pltpu

F32 = jnp.float32
BF16 = jnp.bfloat16

D_MODEL = 1024
NORM_EPS = 1e-6
RW_HEADS = 8
RW_HEAD_DIM = 64
RW_WIDTH = 512
RW_LN_EPS = 64e-5
ATT_GROUP_DILATION = (1, 4, 16)
ATT_BLOCK = 128
ATT_HEADS = 12
ATT_HEAD_DIM = 64
ATT_GROUP_WIDTH = 256
ATT_WIDTH = 768
D_FF = 3072

ADAM_LR = 0.001
ADAM_B1 = 0.9
ADAM_B2 = 0.999
ADAM_EPS = 1e-08
ADAM_WD = 0.01
ADAM_STEP = 10

SUBLANES = 8
LANES = 128
VMEM_LIMIT = 56 * 1024 * 1024
N_CHIPS = 4
N_DEV = 8
MESH = pl.DeviceIdType.MESH


def _params(sem=None):
    return pltpu.CompilerParams(dimension_semantics=sem, vmem_limit_bytes=VMEM_LIMIT)


def _pick(dim, pref):
    if dim % LANES != 0 or dim <= pref:
        return dim
    best = LANES
    for t in range(LANES, pref + 1, LANES):
        if dim % t == 0:
            best = t
    return best


def _mm(name, a, b, mode, out_dtype=F32, add=None, tm=1024, tn=1024, tk=1024, out_by_chip=False, post=None):
    by_chip = b.ndim == 3
    b_rows, b_cols = (b.shape[1], N_CHIPS * b.shape[2]) if by_chip else b.shape
    if mode == "nn":
        (M, K), (K2, N) = a.shape, (b_rows, b_cols)
    elif mode == "nt":
        (M, K), (N, K2) = a.shape, (b_rows, b_cols)
    else:
        (K, M), (K2, N) = a.shape, (b_rows, b_cols)
    assert K == K2, (name, a.shape, b.shape, mode)
    assert not (by_chip and mode == "tn") and not (out_by_chip and add is not None), name
    tm = _pick(M, tm)
    n_cut, k_cut = out_by_chip or (by_chip and mode == "nn"), by_chip and mode == "nt"
    tn = _pick(N // N_CHIPS, tn) if n_cut else _pick(N, tn)
    tk = _pick(K // N_CHIPS, tk) if k_cut else _pick(K, tk)
    nk = K // tk
    per_n = (N // N_CHIPS) // tn if n_cut else 1
    per_k = (K // N_CHIPS) // tk if k_cut else 1
    if mode == "nn":
        a_spec = pl.BlockSpec((tm, tk), lambda i, j, k: (i, k))
        b_spec = (pl.BlockSpec((None, tk, tn), lambda i, j, k: (j // per_n, k, j % per_n)) if by_chip
                  else pl.BlockSpec((tk, tn), lambda i, j, k: (k, j)))
        dims = (((1,), (0,)), ((), ()))
    elif mode == "nt":
        a_spec = pl.BlockSpec((tm, tk), lambda i, j, k: (i, k))
        b_spec = (pl.BlockSpec((None, tn, tk), lambda i, j, k: (k // per_k, j, k % per_k)) if by_chip
                  else pl.BlockSpec((tn, tk), lambda i, j, k: (j, k)))
        dims = (((1,), (1,)), ((), ()))
    else:
        a_spec = pl.BlockSpec((tk, tm), lambda i, j, k: (k, i))
        b_spec = pl.BlockSpec((tk, tn), lambda i, j, k: (k, j))
        dims = (((0,), (0,)), ((), ()))
    if out_by_chip:
        o_spec = pl.BlockSpec((None, tm, tn), lambda i, j, k: (j // per_n, i, j % per_n))
        out_shape = jax.ShapeDtypeStruct((N_CHIPS, M, N // N_CHIPS), out_dtype)
    else:
        o_spec = pl.BlockSpec((tm, tn), lambda i, j, k: (i, j))
        out_shape = jax.ShapeDtypeStruct((M, N), out_dtype)
    has_add = add is not None
    ins = [a, b] + ([add] if has_add else [])
    in_specs = [a_spec, b_spec] + ([o_spec] if has_add else [])
    n_main = len(ins)
    semantics = ("parallel", "parallel", "arbitrary")
    if post is not None:
        post_fn, post_rows, post_consts, post_outs = post
        assert tn == N and not out_by_chip, name
        ins += list(post_rows) + list(post_consts)
        in_specs += [pl.BlockSpec((tm, r.shape[1]), lambda i, j, k: (i, 0)) for r in post_rows]
        in_specs += [pl.BlockSpec(c.shape, lambda i, j, k, nd=c.ndim: (0,) * nd) for c in post_consts]
        o_spec = [pl.BlockSpec((tm, o[1]), lambda i, j, k: (i, 0)) if o[0] == "row"
                  else pl.BlockSpec(o[1], lambda i, j, k: (0, 0)) for o in post_outs]
        out_shape = [jax.ShapeDtypeStruct((M, o[1]), o[2]) if o[0] == "row" else jax.ShapeDtypeStruct(o[1], F32)
                     for o in post_outs]
        if any(o[0] == "acc" for o in post_outs):
            semantics = ("arbitrary", "arbitrary", "arbitrary")
    n_in = len(ins)

    def body(*refs):
        a_ref, b_ref = refs[:2]
        out_refs, acc_ref = refs[n_in:-1], refs[-1]
        i, k = pl.program_id(0), pl.program_id(2)
        part = lax.dot_general(a_ref[...].astype(BF16), b_ref[...].astype(BF16), dims,
                               preferred_element_type=F32)

        @pl.when(k == 0)
        def _():
            acc_ref[...] = part

        @pl.when(k > 0)
        def _():
            acc_ref[...] += part

        @pl.when(k == nk - 1)
        def _():
            res = acc_ref[...]
            if has_add:
                res = res + refs[2][...].astype(F32)
            if post is None:
                out_refs[0][...] = res.astype(out_refs[0].dtype)
                return
            n_rows = len(post_rows)
            vals = post_fn(res, [r[...] for r in refs[n_main:n_main + n_rows]],
                           [c[...] for c in refs[n_main + n_rows:n_in]])
            for o, o_ref, val in zip(post_outs, out_refs, vals, strict=True):
                if o[0] == "row":
                    o_ref[...] = val.astype(o_ref.dtype)
                else:
                    @pl.when(i == 0)
                    def _(o_ref=o_ref, val=val):
                        o_ref[...] = val.astype(F32)

                    @pl.when(i > 0)
                    def _(o_ref=o_ref, val=val):
                        o_ref[...] += val.astype(F32)

    return pl.pallas_call(
        body, name=name, grid=(M // tm, N // tn, nk),
        in_specs=in_specs, out_specs=o_spec, out_shape=out_shape,
        scratch_shapes=[pltpu.VMEM((tm, tn), F32)],
        compiler_params=_params(semantics),
    )(*ins)


def _rowwise(name, fn, T, tT, rows=(), prevs=(), nexts=(), consts=(), outs=()):
    n = T // tT
    per8 = tT // SUBLANES
    in_specs, ins = [], []
    for arr in rows:
        in_specs.append(pl.BlockSpec((tT, arr.shape[1]), lambda i: (i, 0)))
        ins.append(arr)
    for arr in prevs:
        in_specs.append(pl.BlockSpec((SUBLANES, arr.shape[1]), lambda i: (jnp.maximum(i * per8 - 1, 0), 0)))
        ins.append(arr)
    for arr in nexts:
        in_specs.append(pl.BlockSpec((SUBLANES, arr.shape[1]),
                                     lambda i: (jnp.minimum((i + 1) * per8, T // SUBLANES - 1), 0)))
        ins.append(arr)
    for arr in consts:
        in_specs.append(pl.BlockSpec(arr.shape, lambda i, nd=arr.ndim: (0,) * nd))
        ins.append(arr)
    out_specs, out_shapes = [], []
    for o in outs:
        if o[0] == "row":
            out_specs.append(pl.BlockSpec((tT, o[1]), lambda i: (i, 0)))
            out_shapes.append(jax.ShapeDtypeStruct((T, o[1]), o[2]))
        else:
            out_specs.append(pl.BlockSpec(o[1], lambda i: (0, 0)))
            out_shapes.append(jax.ShapeDtypeStruct(o[1], F32))
    nr, npv, nnx, nc = len(rows), len(prevs), len(nexts), len(consts)
    n_in = nr + npv + nnx + nc

    def body(*refs):
        i = pl.program_id(0)
        vals = [r[...] for r in refs[:n_in]]
        res = fn(i, n, vals[:nr], vals[nr:nr + npv], vals[nr + npv:nr + npv + nnx], vals[nr + npv + nnx:])
        for o, o_ref, val in zip(outs, refs[n_in:], res, strict=True):
            if o[0] == "row":
                o_ref[...] = val.astype(o_ref.dtype)
            else:
                @pl.when(i == 0)
                def _(o_ref=o_ref, val=val):
                    o_ref[...] = val.astype(F32)

                @pl.when(i > 0)
                def _(o_ref=o_ref, val=val):
                    o_ref[...] += val.astype(F32)

    res = pl.pallas_call(
        body, name=name, grid=(n,), in_specs=in_specs, out_specs=out_specs, out_shape=out_shapes,
        compiler_params=_params(("arbitrary",)),
    )(*ins)
    return list(res)


def _shift_down(x, prev8, i, s):
    rolled = pltpu.roll(x, s, 0)
    head = pltpu.roll(prev8, s, 0)
    head = jnp.where(i == 0, jnp.zeros_like(head), head)
    rid = lax.broadcasted_iota(jnp.int32, head.shape, 0)
    first = jnp.where(rid < s, head, rolled[:SUBLANES])
    if x.shape[0] == SUBLANES:
        return first
    return jnp.concatenate([first, rolled[SUBLANES:]], axis=0)


def _shift_up(x, next8, i, n, s):
    tT = x.shape[0]
    rolled = pltpu.roll(x, tT - s, 0)
    tail = pltpu.roll(next8, SUBLANES - s, 0)
    tail = jnp.where(i == n - 1, jnp.zeros_like(tail), tail)
    rid = lax.broadcasted_iota(jnp.int32, tail.shape, 0)
    last = jnp.where(rid >= SUBLANES - s, tail, rolled[tT - SUBLANES:])
    return jnp.concatenate([rolled[:tT - SUBLANES], last], axis=0)


def _colsum(x):
    return jnp.sum(x, axis=0, keepdims=True)


def _segsum(x, bd):
    return jnp.dot(x, bd, precision=lax.Precision.HIGH, preferred_element_type=F32)


def _block_diag_ones(width, seg):
    idx = np.arange(width) // seg
    return jnp.asarray((idx[:, None] == idx[None, :]).astype(np.float32))


def _sigmoid(z):
    return 1.0 / (1.0 + jnp.exp(-z))


def _softplus(z):
    return jnp.maximum(z, 0.0) + jnp.log(1.0 + jnp.exp(-jnp.abs(z)))


def _rms_fwd(x, g):
    r = lax.rsqrt(jnp.mean(x * x, axis=-1, keepdims=True) + NORM_EPS)
    return x * r * g


def _rms_bwd(x, g, dy):
    r = lax.rsqrt(jnp.mean(x * x, axis=-1, keepdims=True) + NORM_EPS)
    gdy = dy * g
    dx = r * (gdy - x * (r * r) * jnp.mean(x * gdy, axis=-1, keepdims=True))
    return dx, dy * x * r


GELU_C = math.sqrt(2.0 / math.pi)


def _gelu(x):
    return 0.5 * x * (1.0 + jnp.tanh(GELU_C * (x + 0.044715 * x * x * x)))


def _gelu_and_grad(x):
    th = jnp.tanh(GELU_C * (x + 0.044715 * x * x * x))
    half = 0.5 * (1.0 + th)
    return x * half, half + 0.5 * x * (1.0 - th * th) * GELU_C * (1.0 + 3.0 * 0.044715 * x * x)


RW_CHUNK = 64
NN = (((1,), (0,)), ((), ()))
NT = (((1,), (1,)), ((), ()))
TN = (((0,), (0,)), ((), ()))


def _hdot(a, b, dims):
    return lax.dot_general(a, b, dims, precision=lax.Precision.HIGH, preferred_element_type=F32)


def _ldot(a, b, dims):
    return lax.dot_general(a.astype(BF16), b.astype(BF16), dims, preferred_element_type=F32)


def _chunk_masks():
    ti = lax.broadcasted_iota(jnp.int32, (RW_CHUNK, RW_CHUNK), 0)
    tj = lax.broadcasted_iota(jnp.int32, (RW_CHUNK, RW_CHUNK), 1)
    return tj <= ti, tj < ti, (ti == tj).astype(F32)


def _head(x, h):
    return x[:, h * RW_HEAD_DIM:(h + 1) * RW_HEAD_DIM]


def _heads(fn):
    return [fn(h) for h in range(RW_HEADS)]


def _chunk_rows(r, lw, k, a, b, incl_f):
    c = _hdot(incl_f, lw, NN)
    e_prev, e_neg, e_pos = jnp.exp(c - lw), jnp.exp(-c), jnp.exp(c)
    return dict(At=a * e_prev, Bt=b * e_neg, Kt=k * e_neg, Rt=r * e_pos, e_prev=e_prev, e_neg=e_neg, e_pos=e_pos)


def _stack(top, bottom, h):
    return jnp.concatenate([_head(top, h), _head(bottom, h)], axis=0)


def _chunk_coeffs(q, incl, strict):
    C = RW_CHUNK
    ar = _heads(lambda h: _stack(q["At"], q["Rt"], h))
    pb = _heads(lambda h: _hdot(ar[h], _head(q["Bt"], h), NT))
    pk = _heads(lambda h: _hdot(ar[h], _head(q["Kt"], h), NT))
    A1, W1 = [jnp.where(strict, m[:C], 0.0) for m in pb], [jnp.where(incl, m[C:], 0.0) for m in pb]
    A2, W2 = [jnp.where(strict, m[:C], 0.0) for m in pk], [jnp.where(incl, m[C:], 0.0) for m in pk]
    return A1, A2, W1, W2


def _rwkv_chunk_prep(r, lw, k, a, b, v):
    T = r.shape[0]
    nC = T // RW_CHUNK
    H, N = RW_HEADS, RW_HEAD_DIM

    def body(r_ref, lw_ref, k_ref, a_ref, b_ref, v_ref,
             at_ref, bt_ref, kt_ref, rt_ref, a2v_ref, w2v_ref, ti_ref, w1_ref, a2_ref, w2_ref, pl_ref):
        incl, strict, eye = _chunk_masks()
        q = _chunk_rows(r_ref[...], lw_ref[...], k_ref[...], a_ref[...], b_ref[...], incl.astype(F32))
        at_ref[...], bt_ref[...], kt_ref[...], rt_ref[...] = q["At"], q["Bt"], q["Kt"], q["Rt"]
        pl_ref[0] = jnp.broadcast_to(q["e_pos"][RW_CHUNK - 1:RW_CHUNK, :], (SUBLANES, RW_WIDTH))
        A1, A2, W1, W2 = _chunk_coeffs(q, incl, strict)
        V = v_ref[...]
        a2v_ref[...] = jnp.concatenate(_heads(lambda h: _hdot(A2[h], _head(V, h), NN)), axis=1)
        w2v_ref[...] = jnp.concatenate(_heads(lambda h: _ldot(W2[h], _head(V, h), NN)), axis=1)
        tinv, pw = [eye + m for m in A1], A1
        for stage in range(5):
            dot = _hdot if stage == 0 else _ldot
            pw = [dot(m, m, NN) for m in pw]
            tinv = [t + dot(t, m, NN) for t, m in zip(tinv, pw, strict=True)]
        for h in range(H):
            ti_ref[0, h] = tinv[h]
            w1_ref[0, h] = W1[h]
            a2_ref[0, h] = A2[h]
            w2_ref[0, h] = W2[h]

    row_spec = pl.BlockSpec((RW_CHUNK, RW_WIDTH), lambda n: (n, 0))
    st_spec = pl.BlockSpec((1, H, N, N), lambda n: (n, 0, 0, 0))
    row_shape = jax.ShapeDtypeStruct((T, RW_WIDTH), F32)
    st_shape = jax.ShapeDtypeStruct((nC, H, N, N), F32)
    return pl.pallas_call(
        body, name="rwkv_chunk_prep", grid=(nC,),
        in_specs=[row_spec] * 6,
        out_specs=[row_spec] * 6 + [st_spec] * 4 + [pl.BlockSpec((1, SUBLANES, RW_WIDTH), lambda n: (n, 0, 0))],
        out_shape=[row_shape] * 6 + [st_shape] * 4 + [jax.ShapeDtypeStruct((nC, SUBLANES, RW_WIDTH), F32)],
        compiler_params=_params(("parallel",)),
    )(r, lw, k, a, b, v)


def _rwkv_chunk_fwd(v, at, bt, kt, rt, a2v, w2v, tinv, w1, plast):
    T = v.shape[0]
    nC = T // RW_CHUNK
    H, N = RW_HEADS, RW_HEAD_DIM

    def body(v_ref, at_ref, bt_ref, kt_ref, rt_ref, a2v_ref, w2v_ref, ti_ref, w1_ref, pl_ref,
             y_ref, sa_ref, s0_ref, S_ref):
        @pl.when(pl.program_id(0) == 0)
        def _():
            S_ref[...] = jnp.zeros_like(S_ref)

        V, At, Bt, Kt, Rt = v_ref[...], at_ref[...], bt_ref[...], kt_ref[...], rt_ref[...]
        A2V, W2V, p_last = a2v_ref[...], w2v_ref[...], pl_ref[0, 0:1, :]
        S0 = _heads(lambda h: S_ref[h])
        for h in range(H):
            s0_ref[0, h] = S0[h]
        C = RW_CHUNK
        on_state = _heads(lambda h: _hdot(_stack(At, Rt, h), S0[h], NT))
        Sa = _heads(lambda h: _hdot(ti_ref[0, h], on_state[h][:C] + _head(A2V, h), NN))
        X = _heads(lambda h: S0[h] + _hdot(jnp.concatenate([Sa[h], _head(V, h)], axis=0), _stack(Bt, Kt, h), TN))
        for h in range(H):
            S_ref[h] = X[h] * _head(p_last, h)
        Y = _heads(lambda h: on_state[h][C:] + _ldot(w1_ref[0, h], Sa[h], NN) + _head(W2V, h))
        y_ref[...] = jnp.concatenate(Y, axis=1)
        sa_ref[...] = jnp.concatenate(Sa, axis=1)

    row_spec = pl.BlockSpec((RW_CHUNK, RW_WIDTH), lambda n: (n, 0))
    st_spec = pl.BlockSpec((1, H, N, N), lambda n: (n, 0, 0, 0))
    row_shape = jax.ShapeDtypeStruct((T, RW_WIDTH), F32)
    return pl.pallas_call(
        body, name="rwkv_chunk_fwd", grid=(nC,),
        in_specs=[row_spec] * 7 + [st_spec, st_spec, pl.BlockSpec((1, SUBLANES, RW_WIDTH), lambda n: (n, 0, 0))],
        out_specs=[row_spec, row_spec, st_spec],
        out_shape=[row_shape, row_shape, jax.ShapeDtypeStruct((nC, H, N, N), F32)],
        scratch_shapes=[pltpu.VMEM((H, N, N), F32)],
        compiler_params=_params(("arbitrary",)),
    )(v, at, bt, kt, rt, a2v, w2v, tinv, w1, plast)


def _rwkv_chunk_bwd(r, lw, k, a, b, v, dy, s0, tinv, w1, a2, w2, sa):
    T = r.shape[0]
    nC = T // RW_CHUNK
    H, N = RW_HEADS, RW_HEAD_DIM

    def body(r_ref, lw_ref, k_ref, a_ref, b_ref, v_ref, dy_ref, s0_ref, ti_ref, w1_ref, a2_ref, w2_ref, sa_ref,
             dr_ref, dlw_ref, dk_ref, da_ref, db_ref, dv_ref, dS_ref):
        @pl.when(pl.program_id(0) == 0)
        def _():
            dS_ref[...] = jnp.zeros_like(dS_ref)

        incl, strict, _ = _chunk_masks()
        incl_f = incl.astype(F32)
        q = _chunk_rows(r_ref[...], lw_ref[...], k_ref[...], a_ref[...], b_ref[...], incl_f)
        At, Bt, Kt, Rt = q["At"], q["Bt"], q["Kt"], q["Rt"]
        A2, W1, W2 = (_heads(lambda h, ref=ref: ref[0, h]) for ref in (a2_ref, w1_ref, w2_ref))
        V, dY, Sa = v_ref[...], dy_ref[...], sa_ref[...]
        hd = _head
        p_last = q["e_pos"][RW_CHUNK - 1:RW_CHUNK, :]
        S0 = _heads(lambda h: s0_ref[0, h])
        G = _heads(lambda h: dS_ref[h] * hd(p_last, h))
        C = RW_CHUNK
        AR = _heads(lambda h: _stack(At, Rt, h))
        BK = _heads(lambda h: _stack(Bt, Kt, h))
        X = _heads(lambda h: S0[h] + _ldot(_stack(Sa, V, h), BK[h], TN))
        dc_last = jnp.concatenate(_heads(lambda h: jnp.sum(G[h] * X[h], axis=0, keepdims=True)), axis=1)
        dSa = _heads(lambda h: _ldot(hd(Bt, h), G[h], NT) + _ldot(W1[h], hd(dY, h), TN))
        dZ = _heads(lambda h: _ldot(ti_ref[0, h], dSa[h], TN))
        D = _heads(lambda h: jnp.concatenate([dZ[h], hd(dY, h)], axis=0))
        for h in range(H):
            dS_ref[h] = G[h] + _ldot(D[h], AR[h], TN)
        both = jnp.concatenate([strict, incl], axis=0)
        E1 = _heads(lambda h: jnp.where(both, _ldot(D[h], hd(Sa, h), NT), 0.0))
        E2 = _heads(lambda h: jnp.where(both, _ldot(D[h], hd(V, h), NT), 0.0))
        cat = lambda fn: jnp.concatenate(_heads(fn), axis=1)
        dV = cat(lambda h: _ldot(jnp.concatenate([A2[h], W2[h]], axis=0), D[h], TN) + _ldot(hd(Kt, h), G[h], NT))
        dAR = _heads(lambda h: _ldot(E1[h], hd(Bt, h), NN) + _ldot(E2[h], hd(Kt, h), NN) + _ldot(D[h], S0[h], NN))
        dAt, dRt = cat(lambda h: dAR[h][:C]), cat(lambda h: dAR[h][C:])
        dBt = cat(lambda h: _ldot(E1[h], AR[h], TN) + _ldot(hd(Sa, h), G[h], NN))
        dKt = cat(lambda h: _ldot(E2[h], AR[h], TN) + _ldot(hd(V, h), G[h], NN))
        last_row = lax.broadcasted_iota(jnp.int32, (RW_CHUNK, RW_WIDTH), 0) == RW_CHUNK - 1
        dc_prev = dAt * At
        dc = dc_prev + dRt * Rt - dBt * Bt - dKt * Kt + jnp.where(last_row, dc_last, 0.0)
        dr_ref[...] = dRt * q["e_pos"]
        dlw_ref[...] = _hdot(incl_f, dc, TN) - dc_prev
        dk_ref[...] = dKt * q["e_neg"]
        da_ref[...] = dAt * q["e_prev"]
        db_ref[...] = dBt * q["e_neg"]
        dv_ref[...] = dV

    rev = lambda n: nC - 1 - n
    row_spec = pl.BlockSpec((RW_CHUNK, RW_WIDTH), lambda n: (rev(n), 0))
    st_spec = pl.BlockSpec((1, H, N, N), lambda n: (rev(n), 0, 0, 0))
    row_shape = jax.ShapeDtypeStruct((T, RW_WIDTH), F32)
    return pl.pallas_call(
        body, name="rwkv_chunk_bwd", grid=(nC,),
        in_specs=[row_spec] * 7 + [st_spec] * 5 + [row_spec], out_specs=[row_spec] * 6,
        out_shape=[row_shape] * 6, scratch_shapes=[pltpu.VMEM((H, N, N), F32)],
        compiler_params=_params(("arbitrary",)),
    )(r, lw, k, a, b, v, dy, s0, tinv, w1, a2, w2, sa)


def _alibi_slope(head):
    return float(np.float32(2.0 ** (-8.0 * (head + 1) / ATT_HEADS)))


ATT_SPAN = ATT_BLOCK * max(ATT_GROUP_DILATION)
ATT_PAIR_WIDTH = 2 * ATT_HEAD_DIM
ATT_SIDE_BY_SIDE = 16


def _pair_slope(g, hp, j):
    return jnp.where(hp == 0, _alibi_slope(4 * g + j), _alibi_slope(4 * g + 2 + j))


def _att_rows(mi, r, d):
    start = mi * ATT_BLOCK * d + r
    return pl.ds(start, ATT_BLOCK) if d == 1 else pl.ds(start, ATT_BLOCK, stride=d)


def _att_masks():
    qi = lax.broadcasted_iota(jnp.int32, (ATT_BLOCK, ATT_BLOCK), 0)
    kj = lax.broadcasted_iota(jnp.int32, (ATT_BLOCK, ATT_BLOCK), 1)
    return qi, kj


NEG = -1e30


def _att_logits(q, k, slope_d, steps, valid):
    s = lax.dot_general(q.astype(BF16), k.astype(BF16), (((1,), (1,)), ((), ())),
                        preferred_element_type=F32) * (ATT_HEAD_DIM ** -0.5)
    return jnp.where(valid, s - slope_d * steps.astype(F32), NEG)


def _att_fwd(p_att, g):
    T = p_att.shape[0]
    d = ATT_GROUP_DILATION[g]
    W = ATT_PAIR_WIDTH
    nb = T // ATT_SPAN
    mb = ATT_SPAN // (ATT_BLOCK * d)

    def body(q_ref, kc_ref, kp_ref, vc_ref, vp_ref, o_ref, l_ref):
        hp, n = pl.program_id(0), pl.program_id(1)
        qi, kj = _att_masks()
        slopes = [_pair_slope(g, hp, j) * d for j in range(2)]
        blocks = [(r, mi) for r in range(d) for mi in range(mb)]
        for at in range(0, len(blocks), ATT_SIDE_BY_SIDE):
            tasks = []
            for r, mi in blocks[at:at + ATT_SIDE_BY_SIDE]:
                rows = _att_rows(mi, r, d)
                if mi > 0:
                    prev = _att_rows(mi - 1, r, d)
                    kp, vp, has_prev = kc_ref[prev, :], vc_ref[prev, :], True
                else:
                    prev = _att_rows(mb - 1, r, d)
                    kp, vp, has_prev = kp_ref[prev, :], vp_ref[prev, :], n > 0
                q, kc, vc = q_ref[rows, :], kc_ref[rows, :], vc_ref[rows, :]
                for j in range(2):
                    sl = slice(j * ATT_HEAD_DIM, (j + 1) * ATT_HEAD_DIM)
                    tasks.append((q[:, sl], kc[:, sl], kp[:, sl], vc[:, sl], vp[:, sl], has_prev, slopes[j]))
            lc = [_att_logits(t[0], t[1], t[6], qi - kj, kj <= qi) for t in tasks]
            lp = [_att_logits(t[0], t[2], t[6], qi - kj + ATT_BLOCK, (kj >= qi) & t[5]) for t in tasks]
            mx = [jnp.maximum(jnp.max(a, axis=1, keepdims=True), jnp.max(b, axis=1, keepdims=True))
                  for a, b in zip(lc, lp, strict=True)]
            ec = [jnp.exp(a - m) for a, m in zip(lc, mx, strict=True)]
            ep = [jnp.exp(b - m) for b, m in zip(lp, mx, strict=True)]
            den = [jnp.sum(a, axis=1, keepdims=True) + jnp.sum(b, axis=1, keepdims=True)
                   for a, b in zip(ec, ep, strict=True)]
            inv = [1.0 / s for s in den]
            outs = [jnp.dot((a * i).astype(BF16), t[3].astype(BF16), preferred_element_type=F32)
                    + jnp.dot((b * i).astype(BF16), t[4].astype(BF16), preferred_element_type=F32)
                    for a, b, i, t in zip(ec, ep, inv, tasks, strict=True)]
            lses = [jnp.broadcast_to(m + jnp.log(s), (ATT_BLOCK, ATT_HEAD_DIM)) for m, s in zip(mx, den, strict=True)]
            for i, (r, mi) in enumerate(blocks[at:at + ATT_SIDE_BY_SIDE]):
                rows = _att_rows(mi, r, d)
                o_ref[rows, :] = jnp.concatenate(outs[2 * i:2 * i + 2], axis=1)
                l_ref[rows, :] = jnp.concatenate(lses[2 * i:2 * i + 2], axis=1)

    def spec(col0, prev):
        if prev:
            return pl.BlockSpec((ATT_SPAN, W), lambda hp, n: (jnp.maximum(n - 1, 0), col0 + 2 * g + hp))
        return pl.BlockSpec((ATT_SPAN, W), lambda hp, n: (n, col0 + 2 * g + hp))

    o_spec = pl.BlockSpec((ATT_SPAN, W), lambda hp, n: (n, hp))
    o, l = pl.pallas_call(
        body, name=f"att_fwd_g{g}", grid=(2, nb),
        in_specs=[spec(0, False), spec(6, False), spec(6, True), spec(12, False), spec(12, True)],
        out_specs=[o_spec, o_spec],
        out_shape=[jax.ShapeDtypeStruct((T, ATT_GROUP_WIDTH), F32)] * 2,
        compiler_params=_params(("parallel", "arbitrary")),
    )(p_att, p_att, p_att, p_att, p_att)
    return o, l


def _att_bwd(p_att, o, l, do, dl, g):
    T = p_att.shape[0]
    d = ATT_GROUP_DILATION[g]
    W = ATT_PAIR_WIDTH
    nb = T // ATT_SPAN
    mb = ATT_SPAN // (ATT_BLOCK * d)
    scale = ATT_HEAD_DIM ** -0.5

    def body(q_ref, k_ref, v_ref, o_ref, l_ref, do_ref, dl_ref,
             qn_ref, on_ref, ln_ref, don_ref, dln_ref, dq_ref, dk_ref, dv_ref, carry_ref):
        hp, n = pl.program_id(0), pl.program_id(1)
        qi, kj = _att_masks()

        @pl.when(n == 0)
        def _():
            carry_ref[...] = jnp.zeros_like(carry_ref)

        slopes = [_pair_slope(g, hp, j) * d for j in range(2)]
        blocks = [(r, mi) for r in range(d) for mi in range(mb)]
        side_by_side = ATT_SIDE_BY_SIDE // 2
        carry = None
        for at in range(0, len(blocks), side_by_side):
            tasks = []
            for r, mi in blocks[at:at + side_by_side]:
                rows = _att_rows(mi, r, d)
                if mi < mb - 1:
                    nrows = _att_rows(mi + 1, r, d)
                    nxt = (q_ref[nrows, :], o_ref[nrows, :], l_ref[nrows, :], do_ref[nrows, :], dl_ref[nrows, :])
                    has_next = True
                else:
                    nrows = _att_rows(0, r, d)
                    nxt = (qn_ref[nrows, :], on_ref[nrows, :], ln_ref[nrows, :], don_ref[nrows, :],
                           dln_ref[nrows, :])
                    has_next = n < nb - 1
                cur = (q_ref[rows, :], o_ref[rows, :], l_ref[rows, :], do_ref[rows, :], dl_ref[rows, :])
                k_all, v_all = k_ref[rows, :], v_ref[rows, :]
                for j in range(2):
                    sl = slice(j * ATT_HEAD_DIM, (j + 1) * ATT_HEAD_DIM)
                    for blk, steps, valid in ((cur, qi - kj, kj <= qi),
                                              (nxt, qi - kj + ATT_BLOCK, (kj >= qi) & has_next)):
                        q, o_, lse, do_, dlse = (z[:, sl] for z in blk)
                        tasks.append(dict(q=q, o=o_, lse=lse[:, :1], do=do_, dlse=dlse[:, :1], steps=steps,
                                          valid=valid, k=k_all[:, sl], vb=v_all[:, sl].astype(BF16),
                                          slope=slopes[j]))
            p = [jnp.exp(_att_logits(t["q"], t["k"], t["slope"], t["steps"], t["valid"]) - t["lse"]) for t in tasks]
            dp = [lax.dot_general(t["do"].astype(BF16), t["vb"], (((1,), (1,)), ((), ())),
                                  preferred_element_type=F32) for t in tasks]
            dsum = [jnp.sum(t["do"] * t["o"], axis=1, keepdims=True) for t in tasks]
            ds = [a * (b - s + t["dlse"]) for a, b, s, t in zip(p, dp, dsum, tasks, strict=True)]
            dv_ = [jnp.dot(a.T.astype(BF16), t["do"].astype(BF16), preferred_element_type=F32)
                   for a, t in zip(p, tasks, strict=True)]
            dk_ = [jnp.dot(a.T.astype(BF16), t["q"].astype(BF16), preferred_element_type=F32) * scale
                   for a, t in zip(ds, tasks, strict=True)]
            dq_ = [jnp.dot(a.astype(BF16), t["k"].astype(BF16), preferred_element_type=F32) * scale
                   for a, t in zip(ds, tasks, strict=True)]
            for i, (r, mi) in enumerate(blocks[at:at + side_by_side]):
                rows = _att_rows(mi, r, d)
                b = 4 * i
                if mi == 0:
                    carry = carry_ref[r]
                dq_ref[rows, :] = jnp.concatenate([dq_[b], dq_[b + 2]], axis=1) + carry
                carry = jnp.concatenate([dq_[b + 1], dq_[b + 3]], axis=1)
                if mi == mb - 1:
                    carry_ref[r] = carry
                dk_ref[rows, :] = jnp.concatenate([dk_[b] + dk_[b + 1], dk_[b + 2] + dk_[b + 3]], axis=1)
                dv_ref[rows, :] = jnp.concatenate([dv_[b] + dv_[b + 1], dv_[b + 2] + dv_[b + 3]], axis=1)

    head_rows = ATT_BLOCK * d
    nxt_n = lambda n: jnp.minimum((n + 1) * mb, T // head_rows - 1)
    cur_p = lambda col0: pl.BlockSpec((ATT_SPAN, W), lambda hp, n: (n, col0 + 2 * g + hp))
    cur_o = pl.BlockSpec((ATT_SPAN, W), lambda hp, n: (n, hp))
    nxt_o = pl.BlockSpec((head_rows, W), lambda hp, n: (nxt_n(n), hp))
    dq, dk, dv = pl.pallas_call(
        body, name=f"att_bwd_g{g}", grid=(2, nb),
        in_specs=[cur_p(0), cur_p(6), cur_p(12), cur_o, cur_o, cur_o, cur_o,
                  pl.BlockSpec((head_rows, W), lambda hp, n: (nxt_n(n), 2 * g + hp)), nxt_o, nxt_o, nxt_o, nxt_o],
        out_specs=[cur_o, cur_o, cur_o],
        out_shape=[jax.ShapeDtypeStruct((T, ATT_GROUP_WIDTH), F32)] * 3,
        scratch_shapes=[pltpu.VMEM((d, ATT_BLOCK, W), F32)],
        compiler_params=_params(("parallel", "arbitrary")),
    )(p_att, p_att, p_att, o, l, do, dl, p_att, o, l, do, dl)
    return dq, dk, dv


FFN_TILE = 2 * D_FF // N_CHIPS
RKV = 3 * RW_WIDTH
WA = 128
XG = 160
RW_COLS = RKV + WA + XG


def _local_step(x, p, W, target, late_weights=None, early_grads=None, by_chip=False, grad_dtype=F32):
    T = x.shape[0]
    tT = 256
    bd512 = _block_diag_ones(RW_WIDTH, RW_HEAD_DIM)
    bd256 = _block_diag_ones(ATT_GROUP_WIDTH, ATT_HEAD_DIM)
    G = {}
    W = dict(W)

    w_in = W["w_in"]
    w_rkv, w_wa, w_xg, w_att = (w_in[:, :RKV], w_in[:, RKV:RKV + WA], w_in[:, RKV + WA:RW_COLS],
                                w_in[:, RW_COLS:])
    mu = W["rw_mu"]
    mu_rkv, mu_wa, mu_xg = mu[:, :RKV], mu[:, RKV:RKV + WA], mu[:, RKV + WA:]
    zpad = jnp.zeros((64, RW_WIDTH), W["rw_w_up"].dtype)
    w_up_pad = jnp.concatenate([W["rw_w_up"], zpad], axis=0)
    a_up_pad = jnp.concatenate([zpad, W["rw_a_up"]], axis=0)
    r_k = W["rw_r_k"].reshape(1, RW_WIDTH)

    (h,) = _rowwise("norm_mix", lambda i, n, r, pv, nx, c: [_rms_fwd(r[0], c[0])], T, tT,
                    rows=[x], consts=[W["g_mix"]], outs=[("row", D_MODEL, BF16)])
    p_rkv = _mm("proj_rkv", h, w_rkv, "nn")
    p_wa = _mm("proj_wa", h, w_wa, "nn")
    p_xg = _mm("proj_xg", h, w_xg, "nn")
    p_att = _mm("proj_att", h, w_att, "nn", tn=768)
    z_gate = _mm("proj_gate", h, W["w_gate"], "nn")

    def rw_pre_core(i, rows, prevs, consts):
        prkv, pwa, pxg = rows[:3]
        (mrkv, mwa, mxg, w0, a0, k_k, k_a, wup, aup, gup, bd) = consts[:11]
        m_rkv = prkv + (_shift_down(prkv, prevs[0], i, 1) - prkv) * mrkv
        m_wa = pwa + (_shift_down(pwa, prevs[1], i, 1) - pwa) * mwa
        m_xg = pxg + (_shift_down(pxg, prevs[2], i, 1) - pxg) * mxg
        r, k, v = m_rkv[:, :RW_WIDTH], m_rkv[:, RW_WIDTH:2 * RW_WIDTH], m_rkv[:, 2 * RW_WIDTH:]
        tw = jnp.tanh(m_wa)
        lw = w0 + jnp.dot(tw.astype(BF16), wup.astype(BF16), preferred_element_type=F32)
        wlog = -_softplus(-lw) - 0.5
        log_decay = -jnp.exp(wlog)
        a = _sigmoid(a0 + jnp.dot(m_wa.astype(BF16), aup.astype(BF16), preferred_element_type=F32))
        sg = _sigmoid(m_xg)
        gate = jnp.dot(sg.astype(BF16), gup.astype(BF16), preferred_element_type=F32)
        kkp = k * k_k
        nrm = jnp.sqrt(_segsum(kkp * kkp, bd))
        nrm_c = jnp.maximum(nrm, 1e-12)
        kk = kkp / nrm_c
        k2 = k * (1.0 + (a - 1.0) * k_a)
        return dict(r=r, k=k, v=v, tw=tw, lw=lw, wlog=wlog, log_decay=log_decay, a=a, sg=sg, gate=gate, kkp=kkp,
                    nrm=nrm, nrm_c=nrm_c, kk=kk, k2=k2, m_rkv=m_rkv, m_wa=m_wa, m_xg=m_xg)

    pre_consts = [mu_rkv, mu_wa, mu_xg, W["rw_w0"], W["rw_a0"], W["rw_k_k"], W["rw_k_a"],
                  w_up_pad, a_up_pad, W["rw_g_up"], bd512]

    def rw_pre(i, n, rows, prevs, nexts, consts):
        q = rw_pre_core(i, rows, prevs, consts)
        return [q["r"], q["log_decay"], q["k2"], q["v"], -q["kk"], q["kk"] * q["a"], q["gate"]]

    r_s, w_s, k_s, v_s, a_s, b_s, gate_s = _rowwise(
        "rwkv_pre", rw_pre, T, tT, rows=[p_rkv, p_wa, p_xg], prevs=[p_rkv, p_wa, p_xg], consts=pre_consts,
        outs=[("row", RW_WIDTH, F32)] * 7)
    (at_s, bt_s, kt_s, rt_s, a2v_s, w2v_s, tinv_s, w1_s, a2_s, w2_s,
     plast_s) = _rwkv_chunk_prep(r_s, w_s, k_s, a_s, b_s, v_s)
    y_scan, sa_s, s0_s = _rwkv_chunk_fwd(v_s, at_s, bt_s, kt_s, rt_s, a2v_s, w2v_s, tinv_s, w1_s, plast_s)

    def rw_post_core(rows, consts):
        y, r, k2, v, gate = rows[:5]
        ln_g, ln_b, rk, bd = consts[:4]
        mean = _segsum(y, bd) * (1.0 / RW_HEAD_DIM)
        yc = y - mean
        var = _segsum(yc * yc, bd) * (1.0 / RW_HEAD_DIM)
        rstd = lax.rsqrt(var + RW_LN_EPS)
        yn = yc * rstd
        s = _segsum(r * k2 * rk, bd)
        return dict(yn=yn, rstd=rstd, s=s, pre=yn * ln_g + ln_b + s * v)

    post_consts = [W["rw_ln_g"], W["rw_ln_b"], r_k, bd512]
    (y_a,) = _rowwise("rwkv_post", lambda i, n, r, pv, nx, c: [rw_post_core(r, c)["pre"] * r[4]], T, tT,
                      rows=[y_scan, r_s, k_s, v_s, gate_s], consts=post_consts, outs=[("row", RW_WIDTH, BF16)])

    att = [_att_fwd(p_att, g) for g in range(3)]

    def comb_weights(ls):
        mx = jnp.maximum(jnp.maximum(ls[0], ls[1]), ls[2])
        es = [jnp.exp(l - mx) for l in ls]
        den = es[0] + es[1] + es[2]
        return [e / den for e in es]

    def att_comb(i, n, rows, pv, nx, c):
        wts = comb_weights(rows[3:6])
        return [wts[0] * rows[0] + wts[1] * rows[1] + wts[2] * rows[2]]

    (y_b,) = _rowwise("att_combine", att_comb, T, tT, rows=[att[0][0], att[1][0], att[2][0], att[0][1], att[1][1],
                                                            att[2][1]], outs=[("row", ATT_GROUP_WIDTH, BF16)])

    if late_weights is not None:
        W.update(late_weights(y_b))
    br_a = _mm("branch_a", y_a, W["w_branch_a"], "nn")
    br_b = _mm("branch_b", y_b, W["w_branch_b"], "nn")

    def merge(i, n, rows, pv, nx, c):
        gates = _sigmoid(rows[0] + c[0])
        return [gates[:, :D_MODEL] * rows[1] + gates[:, D_MODEL:] * rows[2]]

    (merged,) = _rowwise("merge", merge, T, tT, rows=[z_gate, br_a, br_b], consts=[W["b_gate"]],
                         outs=[("row", D_MODEL, BF16)])
    with_norm = lambda res, rows, consts: [res, _rms_fwd(res, consts[0])]
    stream_and_norm = [("row", D_MODEL, F32), ("row", D_MODEL, BF16)]
    x1, h2 = _mm("mix_out", merged, W["w_out"], "nn", add=x, post=(with_norm, [], [W["g_ffn"]], stream_and_norm))

    u = _mm("ffn_up", h2, W["w_up"], "nn", tn=FFN_TILE)

    def conv_core(i, rows, prevs, consts):
        uu, cw, cb = rows[0], consts[0], consts[1]
        u1 = _shift_down(uu, prevs[0], i, 1)
        u2 = _shift_down(uu, prevs[0], i, 2)
        uc = cb + cw[0:1] * uu + cw[1:2] * u1 + cw[2:3] * u2
        return uc[:, :D_FF], uc[:, D_FF:], u1, u2

    def glu(i, n, rows, prevs, nx, consts):
        gate, val, _, _ = conv_core(i, rows, prevs, consts)
        return [_gelu(gate) * val]

    tF = 128
    (act,) = _rowwise("conv_glu", glu, T, tF, rows=[u], prevs=[u], consts=[W["conv_w"], W["conv_b"]],
                      outs=[("row", D_FF, BF16)])
    x2, h3 = _mm("ffn_down", act, W["w_down"], "nn", add=x1, post=(with_norm, [], [W["g_ple"]], stream_and_norm))

    e_ple = _mm("ple_emb", p, W["w_ple"], "nn")

    def head(i, n, rows, pv, nx, consts):
        x2_, z, e, tgt = rows
        pg = _sigmoid(z)
        x3 = x2_ + pg * e
        y = _rms_fwd(x3, consts[0])
        err = y - tgt
        loss = 0.5 * jnp.sum(jnp.sum(err * err, axis=1, keepdims=True) * (1.0 / D_MODEL), axis=0, keepdims=True)
        dy = err * (1.0 / D_MODEL)
        dx3, dgf = _rms_bwd(x3, consts[0], dy)
        return [dx3, dx3 * pg, dx3 * e * pg * (1.0 - pg), jnp.broadcast_to(loss, (1, LANES)), _colsum(dgf)]

    dx3, de, dz, loss_acc, G["g_final"] = _mm(
        "ple_gate_loss_head", h3, W["w_ple_gate"], "nn", tm=512,
        post=(lambda res, rows, consts: head(0, 0, [rows[0], res, rows[1], rows[2]], [], [], consts),
              [x2, e_ple, target], [W["g_final"].reshape(1, D_MODEL)],
              [("row", D_MODEL, F32), ("row", D_MODEL, BF16), ("row", D_MODEL, BF16), ("acc", (1, LANES)),
               ("acc", (1, D_MODEL))]))
    G["w_ple"] = _mm("d_w_ple", p, de, "tn", grad_dtype, out_by_chip=by_chip)
    G["w_ple_gate"] = _mm("d_w_ple_gate", h3, dz, "tn", grad_dtype)
    def norm_bwd(i, n, rows, pv, nx, consts):
        dx, dg = _rms_bwd(rows[0], consts[0], rows[1])
        return [rows[2] + dx, _colsum(dg)]

    through_norm = lambda res, rows, consts: norm_bwd(0, 0, [rows[0], res, rows[1]], [], [], consts)
    stream_and_gain = [("row", D_MODEL, F32), ("acc", (1, D_MODEL))]
    dx2, G["g_ple"] = _mm("d_h3", dz, W["w_ple_gate"], "nt", tm=512,
                          post=(through_norm, [x2, dx3], [W["g_ple"]], stream_and_gain))

    dact = _mm("d_act", dx2, W["w_down"], "nt")
    G["w_down"] = _mm("d_w_down", act, dx2, "tn", grad_dtype)

    def glu_grad(gate, val, da):
        act_, slope = _gelu_and_grad(gate)
        return jnp.concatenate([da * val * slope, da * act_], axis=1)

    def glu_bwd(i, n, rows, prevs, nexts, consts):
        uu, da = rows
        cw = consts[0]
        gate, val, u1, u2 = conv_core(i, rows, prevs, consts)
        duc = glu_grad(gate, val, da)
        dcw = jnp.concatenate([_colsum(duc * uu), _colsum(duc * u1), _colsum(duc * u2)], axis=0)
        gate_n, val_n, _, _ = conv_core(1, [nexts[0]], [uu[tF - SUBLANES:]], consts)
        duc_n = glu_grad(gate_n, val_n, nexts[1])
        du = (cw[0:1] * duc + cw[1:2] * _shift_up(duc, duc_n, i, n, 1) + cw[2:3] * _shift_up(duc, duc_n, i, n, 2))
        return [du, _colsum(duc), dcw]

    du, G["conv_b"], G["conv_w"] = _rowwise(
        "d_conv_glu", glu_bwd, T, tF, rows=[u, dact], prevs=[u], nexts=[u, dact],
        consts=[W["conv_w"], W["conv_b"]],
        outs=[("row", 2 * D_FF, BF16), ("acc", (1, 2 * D_FF)), ("acc", (3, 2 * D_FF))])
    G["w_up"] = _mm("d_w_up", h2, du, "tn", grad_dtype, out_by_chip=by_chip, tn=FFN_TILE)
    dx1, G["g_ffn"] = _mm("d_h2", du, W["w_up"], "nt", tk=FFN_TILE, tm=512,
                          post=(through_norm, [x1, dx2], [W["g_ffn"]], stream_and_gain))

    b_gate = W["b_gate"]
    if early_grads is not None:
        b_gate = b_gate + early_grads(G, 0)[0:1, 0:1]
    G["w_out"] = _mm("d_w_out", merged, dx1, "tn", grad_dtype)

    def merge_bwd(dm, rows, consts):
        z, a_, b_ = rows
        gates = _sigmoid(z + consts[0])
        ga, gb = gates[:, :D_MODEL], gates[:, D_MODEL:]
        dz_ = jnp.concatenate([dm * a_ * ga * (1.0 - ga), dm * b_ * gb * (1.0 - gb)], axis=1)
        return [dm * ga, dm * gb, dz_, _colsum(dz_)]

    d_br_a, d_br_b, dz_gate, G["b_gate"] = _mm(
        "d_merged", dx1, W["w_out"], "nt", tm=512,
        post=(merge_bwd, [z_gate, br_a, br_b], [b_gate],
              [("row", D_MODEL, BF16), ("row", D_MODEL, BF16), ("row", 2 * D_MODEL, BF16),
               ("acc", (1, 2 * D_MODEL))]))
    G["w_branch_a"] = _mm("d_w_branch_a", y_a, d_br_a, "tn", grad_dtype, out_by_chip=by_chip)
    G["w_branch_b"] = _mm("d_w_branch_b", y_b, d_br_b, "tn", grad_dtype, out_by_chip=by_chip)
    G["w_gate"] = _mm("d_w_gate", h, dz_gate, "tn", grad_dtype, out_by_chip=by_chip)
    if early_grads is not None:
        post_consts = [post_consts[0] + early_grads(G, 1)[0:1, 0:1]] + post_consts[1:]
    dy_a = _mm("d_y_a", d_br_a, W["w_branch_a"], "nt")
    def att_comb_bwd(dy, rows, consts):
        os_, ls = rows[0:3], rows[3:6]
        wts = comb_weights(ls)
        dws = [_segsum(dy * o_, consts[0]) for o_ in os_]
        mix = wts[0] * dws[0] + wts[1] * dws[1] + wts[2] * dws[2]
        return [wts[g_] * dy for g_ in range(3)] + [wts[g_] * (dws[g_] - mix) for g_ in range(3)]

    comb = _mm("d_y_b", d_br_b, W["w_branch_b"], "nt",
               post=(att_comb_bwd, [att[0][0], att[1][0], att[2][0], att[0][1], att[1][1], att[2][1]], [bd256],
                     [("row", ATT_GROUP_WIDTH, F32)] * 6))
    dqkv = [_att_bwd(p_att, att[g][0], att[g][1], comb[g], comb[3 + g], g) for g in range(3)]
    dp_att = jnp.concatenate([dqkv[g][part] for part in range(3) for g in range(3)], axis=1).astype(BF16)

    def rw_post_bwd(i, n, rows, pv, nx, consts):
        y, r, k2, v, gate, dya = rows
        ln_g, ln_b, rk, bd = consts
        q = rw_post_core(rows, consts)
        dpre = dya * gate
        dgate = dya * q["pre"]
        dyn = dpre * ln_g
        inv = 1.0 / RW_HEAD_DIM
        dy_scan = q["rstd"] * (dyn - _segsum(dyn, bd) * inv - q["yn"] * (_segsum(dyn * q["yn"], bd) * inv))
        ds = _segsum(dpre * v, bd)
        return [dy_scan, dgate, ds * k2 * rk, ds * r * rk, dpre * q["s"],
                _colsum(dpre * q["yn"]), _colsum(dpre), _colsum(ds * r * k2)]

    dy_scan, dgate, dr_b, dk2_b, dv_b, G["rw_ln_g"], G["rw_ln_b"], d_rk = _rowwise(
        "d_rwkv_post", rw_post_bwd, T, tT, rows=[y_scan, r_s, k_s, v_s, gate_s, dy_a], consts=post_consts,
        outs=[("row", RW_WIDTH, F32)] * 5 + [("acc", (1, RW_WIDTH))] * 3)
    G["rw_r_k"] = d_rk.reshape(RW_HEADS, RW_HEAD_DIM)

    dr_s, dw_s, dk_s, da_s, db_s, dv_s = _rwkv_chunk_bwd(r_s, w_s, k_s, a_s, b_s, v_s, dy_scan, s0_s, tinv_s, w1_s,
                                                         a2_s, w2_s, sa_s)

    def rw_pre_bwd(i, n, rows, prevs, nx, consts):
        q = rw_pre_core(i, rows, prevs, consts)
        (mrkv, mwa, mxg, w0, a0, k_k, k_a, wup, aup, gup, bd) = consts
        dr, dlogdecay, dk2, dv, dav, dbv, dgate_ = rows[3:10]
        dr = dr + rows[10]
        dk2 = dk2 + rows[11]
        dv = dv + rows[12]
        a, k, kk = q["a"], q["k"], q["kk"]
        dk = dk2 * (1.0 + (a - 1.0) * k_a)
        da = dk2 * k * k_a + dbv * kk
        dkk = dbv * a - dav
        live = q["nrm"] > 1e-12
        dkkp = jnp.where(live, dkk - kk * _segsum(dkk * kk, bd), dkk) / q["nrm_c"]
        dk = dk + dkkp * k_k
        dlw = dlogdecay * q["log_decay"] * _sigmoid(-q["lw"])
        dla = da * a * (1.0 - a)
        nt = (((1,), (1,)), ((), ()))
        dtw = lax.dot_general(dlw.astype(BF16), wup.astype(BF16), nt, preferred_element_type=F32)
        dxa = lax.dot_general(dla.astype(BF16), aup.astype(BF16), nt, preferred_element_type=F32)
        dm_wa = dtw * (1.0 - q["tw"] * q["tw"]) + dxa
        dsg = lax.dot_general(dgate_.astype(BF16), gup.astype(BF16), nt, preferred_element_type=F32)
        dm_xg = dsg * q["sg"] * (1.0 - q["sg"])
        dm_rkv = jnp.concatenate([dr, dk, dv], axis=1)
        prkv, pwa, pxg = rows[:3]
        dmu = jnp.concatenate([_colsum(dm_rkv * (_shift_down(prkv, prevs[0], i, 1) - prkv)),
                               _colsum(dm_wa * (_shift_down(pwa, prevs[1], i, 1) - pwa)),
                               _colsum(dm_xg * (_shift_down(pxg, prevs[2], i, 1) - pxg))], axis=1)
        return [dm_rkv, dm_wa, dm_xg, dlw, dla, q["tw"], q["m_wa"], q["sg"], dmu,
                _colsum(dlw), _colsum(dla), _colsum(dkkp * k), _colsum(dk2 * k * (a - 1.0))]

    (dm_rkv, dm_wa, dm_xg, dlw, dla, tw_s, mwa_s, sg_s, G["rw_mu"], G["rw_w0"], G["rw_a0"], G["rw_k_k"],
     G["rw_k_a"]) = _rowwise(
        "d_rwkv_pre", rw_pre_bwd, T, tT,
        rows=[p_rkv, p_wa, p_xg, dr_s, dw_s, dk_s, dv_s, da_s, db_s, dgate, dr_b, dk2_b, dv_b],
        prevs=[p_rkv, p_wa, p_xg], consts=pre_consts,
        outs=[("row", RKV, F32), ("row", WA, F32), ("row", XG, F32), ("row", RW_WIDTH, BF16),
              ("row", RW_WIDTH, BF16), ("row", WA, BF16), ("row", WA, BF16), ("row", XG, BF16),
              ("acc", (1, RW_COLS))] + [("acc", (1, RW_WIDTH))] * 4)
    G["rw_w_up"] = _mm("d_rw_w_up", tw_s, dlw, "tn", grad_dtype)[:64]
    G["rw_a_up"] = _mm("d_rw_a_up", mwa_s, dla, "tn", grad_dtype)[64:]
    G["rw_g_up"] = _mm("d_rw_g_up", sg_s, dgate, "tn", grad_dtype)

    def shift_bwd(i, n, rows, pv, nexts, consts):
        return [rows[j] * (1.0 - consts[j]) + _shift_up(rows[j], nexts[j], i, n, 1) * consts[j] for j in range(3)]

    dp_rkv, dp_wa, dp_xg = _rowwise(
        "d_token_shift", shift_bwd, T, tT, rows=[dm_rkv, dm_wa, dm_xg], nexts=[dm_rkv, dm_wa, dm_xg],
        consts=[mu_rkv, mu_wa, mu_xg], outs=[("row", RKV, BF16), ("row", WA, BF16), ("row", XG, BF16)])

    G["w_in"] = jnp.concatenate([_mm("d_w_rkv", h, dp_rkv, "tn", grad_dtype), _mm("d_w_wa", h, dp_wa, "tn", grad_dtype),
                                 _mm("d_w_xg", h, dp_xg, "tn", grad_dtype), _mm("d_w_att", h, dp_att, "tn", grad_dtype, tn=768)], axis=1)
    if early_grads is not None:
        w_wa = w_wa + early_grads(G, 2)[0:1, 0:1].astype(w_wa.dtype)
    dh = _mm("d_h_gate", dz_gate, W["w_gate"], "nt")
    dh = _mm("d_h_rkv", dp_rkv, w_rkv, "nt", add=dh)
    dh = _mm("d_h_wa", dp_wa, w_wa, "nt", add=dh)
    dh = _mm("d_h_xg", dp_xg, w_xg, "nt", add=dh)
    dx, G["g_mix"] = _mm("d_h_att", dp_att, w_att, "nt", add=dh, tm=512,
                         post=(through_norm, [x, dx1], [W["g_mix"]], stream_and_gain))
    return loss_acc[:, :1], dx, G


HBM_SPEC = pl.BlockSpec(memory_space=pltpu.HBM)


def _place():
    x, y, c = lax.axis_index("x"), lax.axis_index("y"), lax.axis_index("c")
    return x, y, c, [(1 - x, y), (x, 1 - y), (1 - x, 1 - y)]


def _remote(src, dst, send_sems, recv_sems, k, to):
    return pltpu.make_async_remote_copy(src_ref=src, dst_ref=dst, send_sem=send_sems.at[k], recv_sem=recv_sems.at[k],
                                        device_id=to, device_id_type=MESH)


ROW_ALIGN = 16


def _splits(rows):
    return rows % (2 * ROW_ALIGN) == 0


def _half_rows(ref_rows, c, first):
    half = ref_rows // 2
    which = c if first else 1 - c
    return pl.ds(pl.multiple_of(which * half, ROW_ALIGN), half)


def _gather_chips(shards):
    n = len(shards)
    split = [_splits(s.shape[0]) for s in shards]

    def body(*refs):
        w_refs, out_refs = refs[:n], refs[n:2 * n]
        send_sems, recv_sems = refs[2 * n:]
        x, y, c, chips = _place()
        me = 2 * x + y
        sends, passed = [], []
        for i in range(n):
            for j, (px, py) in enumerate(chips):
                if split[i]:
                    mine = _half_rows(w_refs[i].shape[0], c, True)
                    cp = _remote(w_refs[i].at[mine], out_refs[i].at[me, mine], send_sems, recv_sems, 6 * i + j,
                                 (px, py, c))
                else:
                    cp = _remote(w_refs[i], out_refs[i].at[me], send_sems, recv_sems, 6 * i + j, (px, py, c))
                cp.start()
                sends.append(cp)
        for i in range(n):
            for j, (px, py) in enumerate(chips):
                if split[i]:
                    landed = out_refs[i].at[2 * px + py, _half_rows(w_refs[i].shape[0], c, True)]
                    _remote(landed, landed, send_sems, recv_sems, 6 * i + j, (px, py, c)).wait_recv()
                    cp = _remote(landed, landed, send_sems, recv_sems, 6 * i + 3 + j, (x, y, 1 - c))
                    cp.start()
                    passed.append(cp)
                else:
                    landed = out_refs[i].at[2 * px + py]
                    _remote(landed, landed, send_sems, recv_sems, 6 * i + j, (px, py, c)).wait_recv()
        for i in range(n):
            if split[i]:
                for j, (px, py) in enumerate(chips):
                    landed = out_refs[i].at[2 * px + py, _half_rows(w_refs[i].shape[0], c, False)]
                    _remote(landed, landed, send_sems, recv_sems, 6 * i + 3 + j, (x, y, 1 - c)).wait_recv()
        for cp in sends + passed:
            cp.wait_send()

    outs = pl.pallas_call(
        body, name="gather_weights", in_specs=[HBM_SPEC] * n, out_specs=[HBM_SPEC] * n,
        out_shape=[jax.ShapeDtypeStruct((N_CHIPS,) + s.shape, s.dtype) for s in shards],
        scratch_shapes=[pltpu.SemaphoreType.DMA((6 * n,)), pltpu.SemaphoreType.DMA((6 * n,))],
    )(*shards)
    me = 2 * lax.axis_index("x") + lax.axis_index("y")
    return [lax.dynamic_update_slice(o, s[None], (me, 0, 0)) for o, s in zip(outs, shards, strict=True)]


def _join_halves(reds):
    n = len(reds)

    def body(*refs):
        r_refs, out_refs = refs[:n], refs[n:2 * n]
        send_sems, recv_sems = refs[2 * n:]
        x, y, c, _ = _place()
        cps = []
        for i in range(n):
            mine = _half_rows(out_refs[i].shape[0], c, True)
            cp = _remote(r_refs[i], out_refs[i].at[mine], send_sems, recv_sems, i, (x, y, 1 - c))
            cp.start()
            cps.append(cp)
        for cp in cps:
            cp.wait()

    outs = pl.pallas_call(
        body, name="join_halves", in_specs=[HBM_SPEC] * n, out_specs=[HBM_SPEC] * n,
        out_shape=[jax.ShapeDtypeStruct((2 * r.shape[0], r.shape[1]), r.dtype) for r in reds],
        scratch_shapes=[pltpu.SemaphoreType.DMA((n,)), pltpu.SemaphoreType.DMA((n,))],
    )(*reds)
    c = lax.axis_index("c")
    return [lax.dynamic_update_slice(o, r, (c * r.shape[0], 0)) for o, r in zip(outs, reds, strict=True)]


SEM_SPEC = pl.BlockSpec(memory_space=pltpu.SEMAPHORE)
PEERS = N_DEV - 1
DATAFLOW = pltpu.SideEffectType.DATAFLOW_SIDE_EFFECTING


def _travel_copies(mode, src_refs, land_refs, send_sems, recv_sems):
    x, y, c, chips = _place()
    me = 2 * x + y
    pairs = []
    for i, (src, land) in enumerate(zip(src_refs, land_refs, strict=True)):
        if mode in ("scatter", "all"):
            for k in range(1, N_DEV):
                px, py, pc = x ^ (k >> 2), y ^ ((k >> 1) & 1), c ^ (k & 1)
                mine = src if mode == "all" else src.at[2 * px + py, _half_rows(src.shape[1], pc, True)]
                there, here = land.at[4 * x + 2 * y + c], land.at[4 * px + 2 * py + pc]
                send = functools.partial(_remote, mine, there, send_sems, recv_sems, PEERS * i + k - 1, (px, py, pc))
                arrival = functools.partial(_remote, mine, here, send_sems, recv_sems, PEERS * i + k - 1, (px, py, pc))
                pairs.append((send, arrival))
            continue
        for j, (px, py) in enumerate(chips):
            peer = 2 * px + py
            if _splits(src.shape[0]):
                rows = _half_rows(src.shape[0], c, True)
                mine, there, here = src.at[rows], land.at[me, rows], land.at[peer, rows]
            else:
                mine, there, here = src, land.at[me], land.at[peer]
            send = functools.partial(_remote, mine, there, send_sems, recv_sems, PEERS * i + j, (px, py, c))
            arrival = functools.partial(_remote, mine, here, send_sems, recv_sems, PEERS * i + j, (px, py, c))
            pairs.append((send, arrival))
    return pairs


def _share_halves(name, lands):
    idx = [i for i, a in enumerate(lands) if _splits(a.shape[1])]
    n = len(idx)

    def body(*refs):
        in_refs, out_refs = refs[:n], refs[n:2 * n]
        send_sems, recv_sems = refs[2 * n:]
        x, y, c, chips = _place()
        cps = []
        for i, (src, dst) in enumerate(zip(in_refs, out_refs, strict=True)):
            for j, (px, py) in enumerate(chips):
                mine = _half_rows(src.shape[1], c, True)
                cp = _remote(src.at[2 * px + py, mine], dst.at[2 * px + py, mine], send_sems, recv_sems, 3 * i + j,
                             (x, y, 1 - c))
                cp.start()
                cps.append(cp)
        for i, dst in enumerate(out_refs):
            for j, (px, py) in enumerate(chips):
                theirs = dst.at[2 * px + py, _half_rows(dst.shape[1], c, False)]
                _remote(theirs, theirs, send_sems, recv_sems, 3 * i + j, (x, y, 1 - c)).wait_recv()
        for cp in cps:
            cp.wait_send()

    outs = pl.pallas_call(
        body, name=name, in_specs=[HBM_SPEC] * n, out_specs=[HBM_SPEC] * n,
        out_shape=[jax.ShapeDtypeStruct(lands[i].shape, lands[i].dtype) for i in idx],
        input_output_aliases={i: i for i in range(n)},
        scratch_shapes=[pltpu.SemaphoreType.DMA((3 * n,)), pltpu.SemaphoreType.DMA((3 * n,))],
    )(*[lands[i] for i in idx])
    done = list(lands)
    for i, o in zip(idx, outs, strict=True):
        done[i] = o
    return done


def _travel_start(name, mode, srcs):
    n = len(srcs)
    land_shape = {"gather": lambda s: (N_CHIPS,) + s.shape, "all": lambda s: (N_DEV,) + s.shape,
                  "scatter": lambda s: (N_DEV, s.shape[1] // 2, s.shape[2])}[mode]
    lands = [lax.empty(land_shape(s), s.dtype) for s in srcs]

    def body(*refs):
        src_refs, land_refs = refs[:n], refs[n:2 * n]
        send_sems, recv_sems = refs[2 * n], refs[2 * n + 1]
        token = refs[-1]
        for send, _ in _travel_copies(mode, src_refs, land_refs, send_sems, recv_sems):
            send().start()
        token[...] = jnp.zeros_like(token)

    hbm = lambda a: pltpu.HBM(a.shape, a.dtype)
    outs = pl.pallas_call(
        body, name=name,
        out_shape=(pltpu.SemaphoreType.DMA((PEERS * n,)), pltpu.SemaphoreType.DMA((PEERS * n,)),
                   *[hbm(s) for s in srcs],
                   *[hbm(a) for a in lands], jax.ShapeDtypeStruct((SUBLANES, LANES), F32)),
        in_specs=[HBM_SPEC] * (2 * n),
        out_specs=(SEM_SPEC, SEM_SPEC, *[HBM_SPEC] * (2 * n), pl.BlockSpec(memory_space=pltpu.VMEM)),
        input_output_aliases={i: 2 + i for i in range(2 * n)},
        compiler_params=pltpu.CompilerParams(has_side_effects=DATAFLOW),
    )(*[pltpu.with_memory_space_constraint(a, pltpu.HBM) for a in list(srcs) + lands])
    return outs[0], outs[1], list(outs[2:2 + n]), list(outs[2 + n:2 + 2 * n]), outs[-1]


def _travel_wait(name, mode, send_sems, recv_sems, srcs, lands, after):
    n = len(srcs)

    def body(*refs):
        src_refs, land_refs = refs[:n], refs[n:2 * n]
        send_sems_, recv_sems_ = refs[2 * n], refs[2 * n + 1]
        for send, arrival in _travel_copies(mode, src_refs, land_refs, send_sems_, recv_sems_):
            send().wait_send()
            arrival().wait_recv()

    hbm = lambda a: pltpu.HBM(a.shape, a.dtype)
    outs = pl.pallas_call(
        body, name=name, out_shape=tuple(hbm(a) for a in list(srcs) + list(lands)),
        in_specs=[HBM_SPEC] * (2 * n) + [SEM_SPEC, SEM_SPEC, pl.BlockSpec(memory_space=pl.ANY)],
        out_specs=tuple([HBM_SPEC] * (2 * n)), input_output_aliases={i: i for i in range(2 * n)},
        compiler_params=pltpu.CompilerParams(has_side_effects=DATAFLOW),
    )(*srcs, *lands, send_sems, recv_sems, after)
    c = lax.axis_index("c")
    me = 2 * lax.axis_index("x") + lax.axis_index("y")
    if mode == "gather":
        slot, own = me, [s[None] for s in outs[:n]]
    elif mode == "all":
        slot, own = 2 * me + c, [s[None] for s in outs[:n]]
    else:
        slot = 2 * me + c
        own = [lax.dynamic_slice(s, (me, c * (s.shape[1] // 2), 0), (1, s.shape[1] // 2, s.shape[2])) for s in outs[:n]]
    return [lax.dynamic_update_slice(a, o, (slot,) + (0,) * (a.ndim - 1)) for a, o in zip(outs[n:], own, strict=True)]


SUM_TILE_BYTES = 4 * 1024 * 1024


def _sum_rows(half, cols):
    best = ROW_ALIGN
    for t in range(ROW_ALIGN, half + 1, ROW_ALIGN):
        if half % t == 0 and N_CHIPS * t * cols * 4 <= SUM_TILE_BYTES:
            best = t
    return best


def _sum_devices(name, parts):
    n, H, C = parts.shape
    tr = _sum_rows(H, C)

    def body(p_ref, o_ref):
        acc = p_ref[0].astype(F32)
        for k in range(1, n):
            acc = acc + p_ref[k].astype(F32)
        o_ref[...] = acc

    return pl.pallas_call(
        body, name=name, grid=(H // tr,),
        in_specs=[pl.BlockSpec((n, tr, C), lambda i: (0, i, 0))],
        out_specs=pl.BlockSpec((tr, C), lambda i: (i, 0)),
        out_shape=jax.ShapeDtypeStruct((H, C), F32),
        compiler_params=_params(("parallel",)),
    )(parts)


def _adamw_math(w, g, m, v):
    m = ADAM_B1 * m + (1.0 - ADAM_B1) * g
    v = ADAM_B2 * v + (1.0 - ADAM_B2) * (g * g)
    m_hat = m / (1.0 - ADAM_B1 ** ADAM_STEP)
    v_hat = v / (1.0 - ADAM_B2 ** ADAM_STEP)
    delta = -ADAM_LR * (m_hat / (jnp.sqrt(v_hat) + ADAM_EPS) + ADAM_WD * w)
    return delta, m, v


def _adamw(name, w, g, m, v):
    R, C = w.shape
    tr = R
    if R % SUBLANES == 0:
        for cand in range(SUBLANES, min(R, 256) + 1, SUBLANES):
            if R % cand == 0:
                tr = cand

    def body(w_ref, g_ref, m_ref, v_ref, d_ref, nm_ref, nv_ref):
        d, nm, nv = _adamw_math(w_ref[...], g_ref[...], m_ref[...], v_ref[...])
        d_ref[...] = d
        nm_ref[...] = nm
        nv_ref[...] = nv

    spec = pl.BlockSpec((tr, C), lambda i: (i, 0))
    shape = jax.ShapeDtypeStruct((R, C), F32)
    return pl.pallas_call(
        body, name=name, grid=(R // tr,), in_specs=[spec] * 4, out_specs=[spec] * 3, out_shape=[shape] * 3,
        compiler_params=_params(("parallel",)),
    )(w, g, m, v)


SMALL_ROW = 2048


def _small_layout(shapes):
    places, row = [], 0
    for R, C in shapes:
        pieces = []
        for r in range(R):
            for c0 in range(0, C, SMALL_ROW):
                pieces.append((r, c0, min(C, c0 + SMALL_ROW), row))
                row += 1
        places.append(pieces)
    return places, -(-row // SUBLANES) * SUBLANES


def _put_rows(block_ref, refs, places):
    block_ref[...] = jnp.zeros_like(block_ref)
    for ref, pieces in zip(refs, places, strict=True):
        for r, c0, c1, row in pieces:
            block_ref[row:row + 1, 0:c1 - c0] = ref[r:r + 1, c0:c1]


def _take_rows(block, refs, places):
    for ref, pieces in zip(refs, places, strict=True):
        for r, c0, c1, row in pieces:
            ref[r:r + 1, c0:c1] = block[row:row + 1, 0:c1 - c0]


def _pack_small(arrs):
    places, rows = _small_layout([a.shape for a in arrs])

    def body(*refs):
        _put_rows(refs[-1], refs[:-1], places)

    return pl.pallas_call(body, name="pack_small", out_shape=jax.ShapeDtypeStruct((rows, SMALL_ROW), F32),
                          compiler_params=_params())(*arrs)


def _adamw_small(parts, ws, ms, vs, extra_shapes):
    n_dev, rows, _ = parts.shape
    n = len(ws)
    places, rows_ = _small_layout([w.shape for w in ws] + list(extra_shapes))
    assert rows_ == rows, (rows_, rows)

    def body(*refs):
        p_ref = refs[0]
        w_refs, m_refs, v_refs = refs[1:1 + n], refs[1 + n:1 + 2 * n], refs[1 + 2 * n:1 + 3 * n]
        outs = refs[1 + 3 * n:-3]
        wb, mb, vb = refs[-3:]
        for block, srcs in ((wb, w_refs), (mb, m_refs), (vb, v_refs)):
            _put_rows(block, srcs, places[:n])
        g = p_ref[0]
        for k in range(1, n_dev):
            g = g + p_ref[k]
        d, nm, nv = _adamw_math(wb[...], g, mb[...], vb[...])
        _take_rows(g, outs[0:n], places[:n])
        _take_rows(d, outs[n:2 * n], places[:n])
        _take_rows(nm, outs[2 * n:3 * n], places[:n])
        _take_rows(nv, outs[3 * n:4 * n], places[:n])
        _take_rows(g, outs[4 * n:], places[n:])

    shapes = [jax.ShapeDtypeStruct(w.shape, F32) for w in ws]
    res = pl.pallas_call(
        body, name="adamw_small", out_shape=shapes * 4 + [jax.ShapeDtypeStruct(s, F32) for s in extra_shapes],
        scratch_shapes=[pltpu.VMEM((rows, SMALL_ROW), F32)] * 3, compiler_params=_params(),
    )(parts, *ws, *ms, *vs)
    return res[0:n], res[n:2 * n], res[2 * n:3 * n], res[3 * n:4 * n], res[4 * n:]


WEIGHTS = ['g_mix', 'w_in', 'rw_mu', 'rw_w0', 'rw_w_up', 'rw_a0', 'rw_a_up', 'rw_g_up', 'rw_k_k', 'rw_k_a',
           'rw_r_k', 'rw_ln_g', 'rw_ln_b', 'w_branch_a', 'w_branch_b', 'w_gate', 'b_gate', 'w_out', 'g_ffn', 'w_up',
           'conv_w', 'conv_b', 'w_down', 'g_ple', 'w_ple_gate', 'w_ple', 'g_final']
ARG_NAMES = (['x', 'p'] + WEIGHTS + ['loss_target'] + ['m_' + n for n in WEIGHTS] + ['v_' + n for n in WEIGHTS])
SHARDED = {'w_in': 1, 'rw_w_up': 1, 'rw_a_up': 1, 'rw_g_up': 1, 'w_branch_a': 1, 'w_branch_b': 1, 'w_gate': 1,
           'w_out': 0, 'w_up': 1, 'conv_w': 1, 'w_down': 0, 'w_ple_gate': 0, 'w_ple': 1}
SMALL = [n for n in WEIGHTS if n not in SHARDED]
WHOLE = ['conv_w']
FIRST_USED = ['w_in', 'rw_w_up', 'rw_a_up', 'rw_g_up', 'w_gate']
READ_BY_CHIP = ['w_gate', 'w_branch_a', 'w_branch_b', 'w_up', 'w_ple']
FIRST_DONE = [['w_up', 'w_down', 'w_ple_gate', 'w_ple'], ['w_out', 'w_branch_a', 'w_branch_b', 'w_gate'],
              ['w_in', 'rw_w_up', 'rw_a_up', 'rw_g_up']]
SPLIT = [n for n in SHARDED if n not in WHOLE]


def _full_from_shards(stack, axis):
    _, R, C = stack.shape
    if axis == 0:
        return stack.reshape(N_CHIPS * R, C)
    return stack.transpose(1, 0, 2).reshape(R, N_CHIPS * C)


def _shards_from_full(full, axis):
    R, C = full.shape
    if axis == 0:
        return full.reshape(N_CHIPS, R // N_CHIPS, C)
    return full.reshape(R, N_CHIPS, C // N_CHIPS).transpose(1, 0, 2)


def kernel(x, p, g_mix, w_in, rw_mu, rw_w0, rw_w_up, rw_a0, rw_a_up, rw_g_up, rw_k_k, rw_k_a, rw_r_k, rw_ln_g, rw_ln_b, w_branch_a, w_branch_b, w_gate, b_gate, w_out, g_ffn, w_up, conv_w, conv_b, w_down, g_ple, w_ple_gate, w_ple, g_final, loss_target, m_g_mix, m_w_in, m_rw_mu, m_rw_w0, m_rw_w_up, m_rw_a0, m_rw_a_up, m_rw_g_up, m_rw_k_k, m_rw_k_a, m_rw_r_k, m_rw_ln_g, m_rw_ln_b, m_w_branch_a, m_w_branch_b, m_w_gate, m_b_gate, m_w_out, m_g_ffn, m_w_up, m_conv_w, m_conv_b, m_w_down, m_g_ple, m_w_ple_gate, m_w_ple, m_g_final, v_g_mix, v_w_in, v_rw_mu, v_rw_w0, v_rw_w_up, v_rw_a0, v_rw_a_up, v_rw_g_up, v_rw_k_k, v_rw_k_a, v_rw_r_k, v_rw_ln_g, v_rw_ln_b, v_w_branch_a, v_w_branch_b, v_w_gate, v_b_gate, v_w_out, v_g_ffn, v_w_up, v_conv_w, v_conv_b, v_w_down, v_g_ple, v_w_ple_gate, v_w_ple, v_g_final):
    given = dict(zip(ARG_NAMES, (x, p, g_mix, w_in, rw_mu, rw_w0, rw_w_up, rw_a0, rw_a_up, rw_g_up, rw_k_k, rw_k_a, rw_r_k, rw_ln_g, rw_ln_b, w_branch_a, w_branch_b, w_gate, b_gate, w_out, g_ffn, w_up, conv_w, conv_b, w_down, g_ple, w_ple_gate, w_ple, g_final, loss_target, m_g_mix, m_w_in, m_rw_mu, m_rw_w0, m_rw_w_up, m_rw_a0, m_rw_a_up, m_rw_g_up, m_rw_k_k, m_rw_k_a, m_rw_r_k, m_rw_ln_g, m_rw_ln_b, m_w_branch_a, m_w_branch_b, m_w_gate, m_b_gate, m_w_out, m_g_ffn, m_w_up, m_conv_w, m_conv_b, m_w_down, m_g_ple, m_w_ple_gate, m_w_ple, m_g_final, v_g_mix, v_w_in, v_rw_mu, v_rw_w0, v_rw_w_up, v_rw_a0, v_rw_a_up, v_rw_g_up, v_rw_k_k, v_rw_k_a, v_rw_r_k, v_rw_ln_g, v_rw_ln_b, v_w_branch_a, v_w_branch_b, v_w_gate, v_b_gate, v_w_out, v_g_ffn, v_w_up, v_conv_w, v_conv_b, v_w_down, v_g_ple, v_w_ple_gate, v_w_ple, v_g_final), strict=True))

    def two_d(name, prefix=""):
        a = given[prefix + name]
        if name == "g_final":
            return a.reshape(1, D_MODEL)
        if name == "rw_r_k":
            return a.reshape(1, RW_WIDTH)
        return a[0] if a.ndim == 3 else a

    cast = lambda n: two_d(n) if n in WHOLE else two_d(n).astype(BF16)
    whole = lambda names, stacks: {n: g if n in READ_BY_CHIP else _full_from_shards(g, SHARDED[n])
                                   for n, g in zip(names, stacks, strict=True)}
    late_names = [n for n in SHARDED if n not in FIRST_USED]
    late_sends, late_recvs, late_srcs, late_lands, token = _travel_start(
        "gather_late_start", "gather", [cast(n) for n in late_names])
    W = whole(FIRST_USED, _gather_chips([cast(n) for n in FIRST_USED]))
    for n in SMALL:
        W[n] = two_d(n)
    W["rw_r_k"] = W["rw_r_k"].reshape(RW_HEADS, RW_HEAD_DIM)
    W["g_mix"] = W["g_mix"] + token[0:1, 0:1]

    def late_weights(after):
        lands = _travel_wait("gather_late_wait", "gather", late_sends, late_recvs, late_srcs, late_lands, after)
        return whole(late_names, _share_halves("share_late", lands))

    early_names = [[n for n in SPLIT if n in group] for group in FIRST_DONE]
    assert sorted(sum(early_names, [])) == sorted(SPLIT)
    travelling = []

    def early_grads(G, stage):
        by_chip = [G[n] if n in READ_BY_CHIP else _shards_from_full(G[n], SHARDED[n]) for n in early_names[stage]]
        sends, recvs, srcs, lands, started = _travel_start(f"scatter{stage}_start", "scatter", by_chip)
        travelling.append((sends, recvs, srcs, lands))
        return started

    loss_part, grad_x, G = _local_step(x[0], p[0, 0], W, loss_target[0], late_weights, early_grads, by_chip=True,
                                       grad_dtype=BF16)

    G["rw_r_k"] = G["rw_r_k"].reshape(1, RW_WIDTH)
    extras = [G[n] for n in WHOLE] + [loss_part]
    small_sends, small_recvs, small_srcs, small_lands, small_started = _travel_start(
        "gather_small_start", "all", [_pack_small([G[n] for n in SMALL] + extras)])

    landed = {}
    for stage, (sends, recvs, srcs, lands) in enumerate(travelling):
        landed.update(zip(early_names[stage], _travel_wait(f"scatter{stage}_wait", "scatter", sends, recvs, srcs,
                                                           lands, small_started), strict=True))
    reduced = [_sum_devices("sum_devices_" + n, landed[n]) for n in SPLIT]
    shard_grads = dict(zip(SPLIT, _join_halves(reduced), strict=True))

    grads, deltas, new_m, new_v = {}, {}, {}, {}

    def step(n):
        g = shard_grads[n]
        d, nm, nv = _adamw("adamw_" + n, two_d(n), g, two_d(n, "m_"), two_d(n, "v_"))
        grads[n], deltas[n], new_m[n], new_v[n] = g, d, nm, nv

    for n in SPLIT:
        step(n)

    (all_small,) = _travel_wait("gather_small_wait", "all", small_sends, small_recvs, small_srcs, small_lands,
                                deltas[SPLIT[-1]])
    gs, ds, nms, nvs, summed = _adamw_small(all_small, [two_d(n) for n in SMALL], [two_d(n, "m_") for n in SMALL],
                                            [two_d(n, "v_") for n in SMALL], [e.shape for e in extras])
    loss = summed[-1][0, 0]
    chip = 2 * lax.axis_index("x") + lax.axis_index("y")
    for n, full in zip(WHOLE, summed[:-1], strict=True):
        width = two_d(n).shape[1]
        shard_grads[n] = lax.dynamic_slice_in_dim(full, chip * width, width, axis=1)
        step(n)
    for i, n in enumerate(SMALL):
        grads[n], deltas[n], new_m[n], new_v[n] = gs[i], ds[i], nms[i], nvs[i]
    outs = [loss, grad_x[None]]
    for table in (grads, deltas, new_m, new_v):
        outs += [table[n].reshape(given[n].shape) for n in WEIGHTS]
    return tuple(outs)
```

```python
import functools
import math

import jax
import jax.numpy as jnp
import numpy as np
from jax import lax
from jax.experimental import pallas as pl
from jax.experimental.pallas import tpu as pltpu

F32 = jnp.float32
BF16 = jnp.bfloat16

D_MODEL = 1024
NORM_EPS = 1e-6
RW_HEADS = 8
RW_HEAD_DIM = 64
RW_WIDTH = 512
RW_LN_EPS = 64e-5
ATT_GROUP_DILATION = (1, 4, 16)
ATT_BLOCK = 128
ATT_HEADS = 12
ATT_HEAD_DIM = 64
ATT_GROUP_WIDTH = 256
ATT_WIDTH = 768
D_FF = 3072

ADAM_LR = 0.001
ADAM_B1 = 0.9
ADAM_B2 = 0.999
ADAM_EPS = 1e-08
ADAM_WD = 0.01
ADAM_STEP = 10

SUBLANES = 8
LANES = 128
VMEM_LIMIT = 56 * 1024 * 1024
N_CHIPS = 4
N_DEV = 8
MESH = pl.DeviceIdType.MESH


def _params(sem=None):
    return pltpu.CompilerParams(dimension_semantics=sem, vmem_limit_bytes=VMEM_LIMIT)


def _pick(dim, pref):
    if dim % LANES != 0 or dim <= pref:
        return dim
    best = LANES
    for t in range(LANES, pref + 1, LANES):
        if dim % t == 0:
            best = t
    return best


def _mm(name, a, b, mode, out_dtype=F32, add=None, tm=1024, tn=1024, tk=1024, out_by_chip=False, post=None):
    by_chip = b.ndim == 3
    b_rows, b_cols = (b.shape[1], N_CHIPS * b.shape[2]) if by_chip else b.shape
    if mode == "nn":
        (M, K), (K2, N) = a.shape, (b_rows, b_cols)
    elif mode == "nt":
        (M, K), (N, K2) = a.shape, (b_rows, b_cols)
    else:
        (K, M), (K2, N) = a.shape, (b_rows, b_cols)
    assert K == K2, (name, a.shape, b.shape, mode)
    assert not (by_chip and mode == "tn") and not (out_by_chip and add is not None), name
    tm = _pick(M, tm)
    n_cut, k_cut = out_by_chip or (by_chip and mode == "nn"), by_chip and mode == "nt"
    tn = _pick(N // N_CHIPS, tn) if n_cut else _pick(N, tn)
    tk = _pick(K // N_CHIPS, tk) if k_cut else _pick(K, tk)
    nk = K // tk
    per_n = (N // N_CHIPS) // tn if n_cut else 1
    per_k = (K // N_CHIPS) // tk if k_cut else 1
    if mode == "nn":
        a_spec = pl.BlockSpec((tm, tk), lambda i, j, k: (i, k))
        b_spec = (pl.BlockSpec((None, tk, tn), lambda i, j, k: (j // per_n, k, j % per_n)) if by_chip
                  else pl.BlockSpec((tk, tn), lambda i, j, k: (k, j)))
        dims = (((1,), (0,)), ((), ()))
    elif mode == "nt":
        a_spec = pl.BlockSpec((tm, tk), lambda i, j, k: (i, k))
        b_spec = (pl.BlockSpec((None, tn, tk), lambda i, j, k: (k // per_k, j, k % per_k)) if by_chip
                  else pl.BlockSpec((tn, tk), lambda i, j, k: (j, k)))
        dims = (((1,), (1,)), ((), ()))
    else:
        a_spec = pl.BlockSpec((tk, tm), lambda i, j, k: (k, i))
        b_spec = pl.BlockSpec((tk, tn), lambda i, j, k: (k, j))
        dims = (((0,), (0,)), ((), ()))
    if out_by_chip:
        o_spec = pl.BlockSpec((None, tm, tn), lambda i, j, k: (j // per_n, i, j % per_n))
        out_shape = jax.ShapeDtypeStruct((N_CHIPS, M, N // N_CHIPS), out_dtype)
    else:
        o_spec = pl.BlockSpec((tm, tn), lambda i, j, k: (i, j))
        out_shape = jax.ShapeDtypeStruct((M, N), out_dtype)
    has_add = add is not None
    ins = [a, b] + ([add] if has_add else [])
    in_specs = [a_spec, b_spec] + ([o_spec] if has_add else [])
    n_main = len(ins)
    semantics = ("parallel", "parallel", "arbitrary")
    if post is not None:
        post_fn, post_rows, post_consts, post_outs = post
        assert tn == N and not out_by_chip, name
        ins += list(post_rows) + list(post_consts)
        in_specs += [pl.BlockSpec((tm, r.shape[1]), lambda i, j, k: (i, 0)) for r in post_rows]
        in_specs += [pl.BlockSpec(c.shape, lambda i, j, k, nd=c.ndim: (0,) * nd) for c in post_consts]
        o_spec = [pl.BlockSpec((tm, o[1]), lambda i, j, k: (i, 0)) if o[0] == "row"
                  else pl.BlockSpec(o[1], lambda i, j, k: (0, 0)) for o in post_outs]
        out_shape = [jax.ShapeDtypeStruct((M, o[1]), o[2]) if o[0] == "row" else jax.ShapeDtypeStruct(o[1], F32)
                     for o in post_outs]
        if any(o[0] == "acc" for o in post_outs):
            semantics = ("arbitrary", "arbitrary", "arbitrary")
    n_in = len(ins)

    def body(*refs):
        a_ref, b_ref = refs[:2]
        out_refs, acc_ref = refs[n_in:-1], refs[-1]
        i, k = pl.program_id(0), pl.program_id(2)
        part = lax.dot_general(a_ref[...].astype(BF16), b_ref[...].astype(BF16), dims,
                               preferred_element_type=F32)

        @pl.when(k == 0)
        def _():
            acc_ref[...] = part

        @pl.when(k > 0)
        def _():
            acc_ref[...] += part

        @pl.when(k == nk - 1)
        def _():
            res = acc_ref[...]
            if has_add:
                res = res + refs[2][...].astype(F32)
            if post is None:
                out_refs[0][...] = res.astype(out_refs[0].dtype)
                return
            n_rows = len(post_rows)
            vals = post_fn(res, [r[...] for r in refs[n_main:n_main + n_rows]],
                           [c[...] for c in refs[n_main + n_rows:n_in]])
            for o, o_ref, val in zip(post_outs, out_refs, vals, strict=True):
                if o[0] == "row":
                    o_ref[...] = val.astype(o_ref.dtype)
                else:
                    @pl.when(i == 0)
                    def _(o_ref=o_ref, val=val):
                        o_ref[...] = val.astype(F32)

                    @pl.when(i > 0)
                    def _(o_ref=o_ref, val=val):
                        o_ref[...] += val.astype(F32)

    return pl.pallas_call(
        body, name=name, grid=(M // tm, N // tn, nk),
        in_specs=in_specs, out_specs=o_spec, out_shape=out_shape,
        scratch_shapes=[pltpu.VMEM((tm, tn), F32)],
        compiler_params=_params(semantics),
    )(*ins)


def _rowwise(name, fn, T, tT, rows=(), prevs=(), nexts=(), consts=(), outs=()):
    n = T // tT
    per8 = tT // SUBLANES
    in_specs, ins = [], []
    for arr in rows:
        in_specs.append(pl.BlockSpec((tT, arr.shape[1]), lambda i: (i, 0)))
        ins.append(arr)
    for arr in prevs:
        in_specs.append(pl.BlockSpec((SUBLANES, arr.shape[1]), lambda i: (jnp.maximum(i * per8 - 1, 0), 0)))
        ins.append(arr)
    for arr in nexts:
        in_specs.append(pl.BlockSpec((SUBLANES, arr.shape[1]),
                                     lambda i: (jnp.minimum((i + 1) * per8, T // SUBLANES - 1), 0)))
        ins.append(arr)
    for arr in consts:
        in_specs.append(pl.BlockSpec(arr.shape, lambda i, nd=arr.ndim: (0,) * nd))
        ins.append(arr)
    out_specs, out_shapes = [], []
    for o in outs:
        if o[0] == "row":
            out_specs.append(pl.BlockSpec((tT, o[1]), lambda i: (i, 0)))
            out_shapes.append(jax.ShapeDtypeStruct((T, o[1]), o[2]))
        else:
            out_specs.append(pl.BlockSpec(o[1], lambda i: (0, 0)))
            out_shapes.append(jax.ShapeDtypeStruct(o[1], F32))
    nr, npv, nnx, nc = len(rows), len(prevs), len(nexts), len(consts)
    n_in = nr + npv + nnx + nc

    def body(*refs):
        i = pl.program_id(0)
        vals = [r[...] for r in refs[:n_in]]
        res = fn(i, n, vals[:nr], vals[nr:nr + npv], vals[nr + npv:nr + npv + nnx], vals[nr + npv + nnx:])
        for o, o_ref, val in zip(outs, refs[n_in:], res, strict=True):
            if o[0] == "row":
                o_ref[...] = val.astype(o_ref.dtype)
            else:
                @pl.when(i == 0)
                def _(o_ref=o_ref, val=val):
                    o_ref[...] = val.astype(F32)

                @pl.when(i > 0)
                def _(o_ref=o_ref, val=val):
                    o_ref[...] += val.astype(F32)

    res = pl.pallas_call(
        body, name=name, grid=(n,), in_specs=in_specs, out_specs=out_specs, out_shape=out_shapes,
        compiler_params=_params(("arbitrary",)),
    )(*ins)
    return list(res)


def _shift_down(x, prev8, i, s):
    rolled = pltpu.roll(x, s, 0)
    head = pltpu.roll(prev8, s, 0)
    head = jnp.where(i == 0, jnp.zeros_like(head), head)
    rid = lax.broadcasted_iota(jnp.int32, head.shape, 0)
    first = jnp.where(rid < s, head, rolled[:SUBLANES])
    if x.shape[0] == SUBLANES:
        return first
    return jnp.concatenate([first, rolled[SUBLANES:]], axis=0)


def _shift_up(x, next8, i, n, s):
    tT = x.shape[0]
    rolled = pltpu.roll(x, tT - s, 0)
    tail = pltpu.roll(next8, SUBLANES - s, 0)
    tail = jnp.where(i == n - 1, jnp.zeros_like(tail), tail)
    rid = lax.broadcasted_iota(jnp.int32, tail.shape, 0)
    last = jnp.where(rid >= SUBLANES - s, tail, rolled[tT - SUBLANES:])
    return jnp.concatenate([rolled[:tT - SUBLANES], last], axis=0)


def _colsum(x):
    return jnp.sum(x, axis=0, keepdims=True)


def _segsum(x, bd):
    return jnp.dot(x, bd, precision=lax.Precision.HIGH, preferred_element_type=F32)


def _block_diag_ones(width, seg):
    idx = np.arange(width) // seg
    return jnp.asarray((idx[:, None] == idx[None, :]).astype(np.float32))


def _sigmoid(z):
    return 1.0 / (1.0 + jnp.exp(-z))


def _softplus(z):
    return jnp.maximum(z, 0.0) + jnp.log(1.0 + jnp.exp(-jnp.abs(z)))


def _rms_fwd(x, g):
    r = lax.rsqrt(jnp.mean(x * x, axis=-1, keepdims=True) + NORM_EPS)
    return x * r * g


def _rms_bwd(x, g, dy):
    r = lax.rsqrt(jnp.mean(x * x, axis=-1, keepdims=True) + NORM_EPS)
    gdy = dy * g
    dx = r * (gdy - x * (r * r) * jnp.mean(x * gdy, axis=-1, keepdims=True))
    return dx, dy * x * r


GELU_C = math.sqrt(2.0 / math.pi)


def _gelu(x):
    return 0.5 * x * (1.0 + jnp.tanh(GELU_C * (x + 0.044715 * x * x * x)))


def _gelu_and_grad(x):
    th = jnp.tanh(GELU_C * (x + 0.044715 * x * x * x))
    half = 0.5 * (1.0 + th)
    return x * half, half + 0.5 * x * (1.0 - th * th) * GELU_C * (1.0 + 3.0 * 0.044715 * x * x)


RW_CHUNK = 64
NN = (((1,), (0,)), ((), ()))
NT = (((1,), (1,)), ((), ()))
TN = (((0,), (0,)), ((), ()))


def _hdot(a, b, dims):
    return lax.dot_general(a, b, dims, precision=lax.Precision.HIGH, preferred_element_type=F32)


def _ldot(a, b, dims):
    return lax.dot_general(a.astype(BF16), b.astype(BF16), dims, preferred_element_type=F32)


def _chunk_masks():
    ti = lax.broadcasted_iota(jnp.int32, (RW_CHUNK, RW_CHUNK), 0)
    tj = lax.broadcasted_iota(jnp.int32, (RW_CHUNK, RW_CHUNK), 1)
    return tj <= ti, tj < ti, (ti == tj).astype(F32)


def _head(x, h):
    return x[:, h * RW_HEAD_DIM:(h + 1) * RW_HEAD_DIM]


def _heads(fn):
    return [fn(h) for h in range(RW_HEADS)]


def _chunk_rows(r, lw, k, a, b, incl_f):
    c = _hdot(incl_f, lw, NN)
    e_prev, e_neg, e_pos = jnp.exp(c - lw), jnp.exp(-c), jnp.exp(c)
    return dict(At=a * e_prev, Bt=b * e_neg, Kt=k * e_neg, Rt=r * e_pos, e_prev=e_prev, e_neg=e_neg, e_pos=e_pos)


def _stack(top, bottom, h):
    return jnp.concatenate([_head(top, h), _head(bottom, h)], axis=0)


def _chunk_coeffs(q, incl, strict):
    C = RW_CHUNK
    ar = _heads(lambda h: _stack(q["At"], q["Rt"], h))
    pb = _heads(lambda h: _hdot(ar[h], _head(q["Bt"], h), NT))
    pk = _heads(lambda h: _hdot(ar[h], _head(q["Kt"], h), NT))
    A1, W1 = [jnp.where(strict, m[:C], 0.0) for m in pb], [jnp.where(incl, m[C:], 0.0) for m in pb]
    A2, W2 = [jnp.where(strict, m[:C], 0.0) for m in pk], [jnp.where(incl, m[C:], 0.0) for m in pk]
    return A1, A2, W1, W2


def _rwkv_chunk_prep(r, lw, k, a, b, v):
    T = r.shape[0]
    nC = T // RW_CHUNK
    H, N = RW_HEADS, RW_HEAD_DIM

    def body(r_ref, lw_ref, k_ref, a_ref, b_ref, v_ref,
             at_ref, bt_ref, kt_ref, rt_ref, a2v_ref, w2v_ref, ti_ref, w1_ref, a2_ref, w2_ref, pl_ref):
        incl, strict, eye = _chunk_masks()
        q = _chunk_rows(r_ref[...], lw_ref[...], k_ref[...], a_ref[...], b_ref[...], incl.astype(F32))
        at_ref[...], bt_ref[...], kt_ref[...], rt_ref[...] = q["At"], q["Bt"], q["Kt"], q["Rt"]
        pl_ref[0] = jnp.broadcast_to(q["e_pos"][RW_CHUNK - 1:RW_CHUNK, :], (SUBLANES, RW_WIDTH))
        A1, A2, W1, W2 = _chunk_coeffs(q, incl, strict)
        V = v_ref[...]
        a2v_ref[...] = jnp.concatenate(_heads(lambda h: _hdot(A2[h], _head(V, h), NN)), axis=1)
        w2v_ref[...] = jnp.concatenate(_heads(lambda h: _ldot(W2[h], _head(V, h), NN)), axis=1)
        tinv, pw = [eye + m for m in A1], A1
        for stage in range(5):
            dot = _hdot if stage == 0 else _ldot
            pw = [dot(m, m, NN) for m in pw]
            tinv = [t + dot(t, m, NN) for t, m in zip(tinv, pw, strict=True)]
        for h in range(H):
            ti_ref[0, h] = tinv[h]
            w1_ref[0, h] = W1[h]
            a2_ref[0, h] = A2[h]
            w2_ref[0, h] = W2[h]

    row_spec = pl.BlockSpec((RW_CHUNK, RW_WIDTH), lambda n: (n, 0))
    st_spec = pl.BlockSpec((1, H, N, N), lambda n: (n, 0, 0, 0))
    row_shape = jax.ShapeDtypeStruct((T, RW_WIDTH), F32)
    st_shape = jax.ShapeDtypeStruct((nC, H, N, N), F32)
    return pl.pallas_call(
        body, name="rwkv_chunk_prep", grid=(nC,),
        in_specs=[row_spec] * 6,
        out_specs=[row_spec] * 6 + [st_spec] * 4 + [pl.BlockSpec((1, SUBLANES, RW_WIDTH), lambda n: (n, 0, 0))],
        out_shape=[row_shape] * 6 + [st_shape] * 4 + [jax.ShapeDtypeStruct((nC, SUBLANES, RW_WIDTH), F32)],
        compiler_params=_params(("parallel",)),
    )(r, lw, k, a, b, v)


def _rwkv_chunk_fwd(v, at, bt, kt, rt, a2v, w2v, tinv, w1, plast):
    T = v.shape[0]
    nC = T // RW_CHUNK
    H, N = RW_HEADS, RW_HEAD_DIM

    def body(v_ref, at_ref, bt_ref, kt_ref, rt_ref, a2v_ref, w2v_ref, ti_ref, w1_ref, pl_ref,
             y_ref, sa_ref, s0_ref, S_ref):
        @pl.when(pl.program_id(0) == 0)
        def _():
            S_ref[...] = jnp.zeros_like(S_ref)

        V, At, Bt, Kt, Rt = v_ref[...], at_ref[...], bt_ref[...], kt_ref[...], rt_ref[...]
        A2V, W2V, p_last = a2v_ref[...], w2v_ref[...], pl_ref[0, 0:1, :]
        S0 = _heads(lambda h: S_ref[h])
        for h in range(H):
            s0_ref[0, h] = S0[h]
        C = RW_CHUNK
        on_state = _heads(lambda h: _hdot(_stack(At, Rt, h), S0[h], NT))
        Sa = _heads(lambda h: _hdot(ti_ref[0, h], on_state[h][:C] + _head(A2V, h), NN))
        X = _heads(lambda h: S0[h] + _hdot(jnp.concatenate([Sa[h], _head(V, h)], axis=0), _stack(Bt, Kt, h), TN))
        for h in range(H):
            S_ref[h] = X[h] * _head(p_last, h)
        Y = _heads(lambda h: on_state[h][C:] + _ldot(w1_ref[0, h], Sa[h], NN) + _head(W2V, h))
        y_ref[...] = jnp.concatenate(Y, axis=1)
        sa_ref[...] = jnp.concatenate(Sa, axis=1)

    row_spec = pl.BlockSpec((RW_CHUNK, RW_WIDTH), lambda n: (n, 0))
    st_spec = pl.BlockSpec((1, H, N, N), lambda n: (n, 0, 0, 0))
    row_shape = jax.ShapeDtypeStruct((T, RW_WIDTH), F32)
    return pl.pallas_call(
        body, name="rwkv_chunk_fwd", grid=(nC,),
        in_specs=[row_spec] * 7 + [st_spec, st_spec, pl.BlockSpec((1, SUBLANES, RW_WIDTH), lambda n: (n, 0, 0))],
        out_specs=[row_spec, row_spec, st_spec],
        out_shape=[row_shape, row_shape, jax.ShapeDtypeStruct((nC, H, N, N), F32)],
        scratch_shapes=[pltpu.VMEM((H, N, N), F32)],
        compiler_params=_params(("arbitrary",)),
    )(v, at, bt, kt, rt, a2v, w2v, tinv, w1, plast)


def _rwkv_chunk_bwd(r, lw, k, a, b, v, dy, s0, tinv, w1, a2, w2, sa):
    T = r.shape[0]
    nC = T // RW_CHUNK
    H, N = RW_HEADS, RW_HEAD_DIM

    def body(r_ref, lw_ref, k_ref, a_ref, b_ref, v_ref, dy_ref, s0_ref, ti_ref, w1_ref, a2_ref, w2_ref, sa_ref,
             dr_ref, dlw_ref, dk_ref, da_ref, db_ref, dv_ref, dS_ref):
        @pl.when(pl.program_id(0) == 0)
        def _():
            dS_ref[...] = jnp.zeros_like(dS_ref)

        incl, strict, _ = _chunk_masks()
        incl_f = incl.astype(F32)
        q = _chunk_rows(r_ref[...], lw_ref[...], k_ref[...], a_ref[...], b_ref[...], incl_f)
        At, Bt, Kt, Rt = q["At"], q["Bt"], q["Kt"], q["Rt"]
        A2, W1, W2 = (_heads(lambda h, ref=ref: ref[0, h]) for ref in (a2_ref, w1_ref, w2_ref))
        V, dY, Sa = v_ref[...], dy_ref[...], sa_ref[...]
        hd = _head
        p_last = q["e_pos"][RW_CHUNK - 1:RW_CHUNK, :]
        S0 = _heads(lambda h: s0_ref[0, h])
        G = _heads(lambda h: dS_ref[h] * hd(p_last, h))
        C = RW_CHUNK
        AR = _heads(lambda h: _stack(At, Rt, h))
        BK = _heads(lambda h: _stack(Bt, Kt, h))
        X = _heads(lambda h: S0[h] + _ldot(_stack(Sa, V, h), BK[h], TN))
        dc_last = jnp.concatenate(_heads(lambda h: jnp.sum(G[h] * X[h], axis=0, keepdims=True)), axis=1)
        dSa = _heads(lambda h: _ldot(hd(Bt, h), G[h], NT) + _ldot(W1[h], hd(dY, h), TN))
        dZ = _heads(lambda h: _ldot(ti_ref[0, h], dSa[h], TN))
        D = _heads(lambda h: jnp.concatenate([dZ[h], hd(dY, h)], axis=0))
        for h in range(H):
            dS_ref[h] = G[h] + _ldot(D[h], AR[h], TN)
        both = jnp.concatenate([strict, incl], axis=0)
        E1 = _heads(lambda h: jnp.where(both, _ldot(D[h], hd(Sa, h), NT), 0.0))
        E2 = _heads(lambda h: jnp.where(both, _ldot(D[h], hd(V, h), NT), 0.0))
        cat = lambda fn: jnp.concatenate(_heads(fn), axis=1)
        dV = cat(lambda h: _ldot(jnp.concatenate([A2[h], W2[h]], axis=0), D[h], TN) + _ldot(hd(Kt, h), G[h], NT))
        dAR = _heads(lambda h: _ldot(E1[h], hd(Bt, h), NN) + _ldot(E2[h], hd(Kt, h), NN) + _ldot(D[h], S0[h], NN))
        dAt, dRt = cat(lambda h: dAR[h][:C]), cat(lambda h: dAR[h][C:])
        dBt = cat(lambda h: _ldot(E1[h], AR[h], TN) + _ldot(hd(Sa, h), G[h], NN))
        dKt = cat(lambda h: _ldot(E2[h], AR[h], TN) + _ldot(hd(V, h), G[h], NN))
        last_row = lax.broadcasted_iota(jnp.int32, (RW_CHUNK, RW_WIDTH), 0) == RW_CHUNK - 1
        dc_prev = dAt * At
        dc = dc_prev + dRt * Rt - dBt * Bt - dKt * Kt + jnp.where(last_row, dc_last, 0.0)
        dr_ref[...] = dRt * q["e_pos"]
        dlw_ref[...] = _hdot(incl_f, dc, TN) - dc_prev
        dk_ref[...] = dKt * q["e_neg"]
        da_ref[...] = dAt * q["e_prev"]
        db_ref[...] = dBt * q["e_neg"]
        dv_ref[...] = dV

    rev = lambda n: nC - 1 - n
    row_spec = pl.BlockSpec((RW_CHUNK, RW_WIDTH), lambda n: (rev(n), 0))
    st_spec = pl.BlockSpec((1, H, N, N), lambda n: (rev(n), 0, 0, 0))
    row_shape = jax.ShapeDtypeStruct((T, RW_WIDTH), F32)
    return pl.pallas_call(
        body, name="rwkv_chunk_bwd", grid=(nC,),
        in_specs=[row_spec] * 7 + [st_spec] * 5 + [row_spec], out_specs=[row_spec] * 6,
        out_shape=[row_shape] * 6, scratch_shapes=[pltpu.VMEM((H, N, N), F32)],
        compiler_params=_params(("arbitrary",)),
    )(r, lw, k, a, b, v, dy, s0, tinv, w1, a2, w2, sa)


def _alibi_slope(head):
    return float(np.float32(2.0 ** (-8.0 * (head + 1) / ATT_HEADS)))


ATT_SPAN = ATT_BLOCK * max(ATT_GROUP_DILATION)
ATT_PAIR_WIDTH = 2 * ATT_HEAD_DIM
ATT_SIDE_BY_SIDE = 16


def _pair_slope(g, hp, j):
    return jnp.where(hp == 0, _alibi_slope(4 * g + j), _alibi_slope(4 * g + 2 + j))


def _att_rows(mi, r, d):
    start = mi * ATT_BLOCK * d + r
    return pl.ds(start, ATT_BLOCK) if d == 1 else pl.ds(start, ATT_BLOCK, stride=d)


def _att_masks():
    qi = lax.broadcasted_iota(jnp.int32, (ATT_BLOCK, ATT_BLOCK), 0)
    kj = lax.broadcasted_iota(jnp.int32, (ATT_BLOCK, ATT_BLOCK), 1)
    return qi, kj


NEG = -1e30


def _att_logits(q, k, slope_d, steps, valid):
    s = lax.dot_general(q.astype(BF16), k.astype(BF16), (((1,), (1,)), ((), ())),
                        preferred_element_type=F32) * (ATT_HEAD_DIM ** -0.5)
    return jnp.where(valid, s - slope_d * steps.astype(F32), NEG)


def _att_fwd(p_att, g):
    T = p_att.shape[0]
    d = ATT_GROUP_DILATION[g]
    W = ATT_PAIR_WIDTH
    nb = T // ATT_SPAN
    mb = ATT_SPAN // (ATT_BLOCK * d)

    def body(q_ref, kc_ref, kp_ref, vc_ref, vp_ref, o_ref, l_ref):
        hp, n = pl.program_id(0), pl.program_id(1)
        qi, kj = _att_masks()
        slopes = [_pair_slope(g, hp, j) * d for j in range(2)]
        blocks = [(r, mi) for r in range(d) for mi in range(mb)]
        for at in range(0, len(blocks), ATT_SIDE_BY_SIDE):
            tasks = []
            for r, mi in blocks[at:at + ATT_SIDE_BY_SIDE]:
                rows = _att_rows(mi, r, d)
                if mi > 0:
                    prev = _att_rows(mi - 1, r, d)
                    kp, vp, has_prev = kc_ref[prev, :], vc_ref[prev, :], True
                else:
                    prev = _att_rows(mb - 1, r, d)
                    kp, vp, has_prev = kp_ref[prev, :], vp_ref[prev, :], n > 0
                q, kc, vc = q_ref[rows, :], kc_ref[rows, :], vc_ref[rows, :]
                for j in range(2):
                    sl = slice(j * ATT_HEAD_DIM, (j + 1) * ATT_HEAD_DIM)
                    tasks.append((q[:, sl], kc[:, sl], kp[:, sl], vc[:, sl], vp[:, sl], has_prev, slopes[j]))
            lc = [_att_logits(t[0], t[1], t[6], qi - kj, kj <= qi) for t in tasks]
            lp = [_att_logits(t[0], t[2], t[6], qi - kj + ATT_BLOCK, (kj >= qi) & t[5]) for t in tasks]
            mx = [jnp.maximum(jnp.max(a, axis=1, keepdims=True), jnp.max(b, axis=1, keepdims=True))
                  for a, b in zip(lc, lp, strict=True)]
            ec = [jnp.exp(a - m) for a, m in zip(lc, mx, strict=True)]
            ep = [jnp.exp(b - m) for b, m in zip(lp, mx, strict=True)]
            den = [jnp.sum(a, axis=1, keepdims=True) + jnp.sum(b, axis=1, keepdims=True)
                   for a, b in zip(ec, ep, strict=True)]
            inv = [1.0 / s for s in den]
            outs = [jnp.dot((a * i).astype(BF16), t[3].astype(BF16), preferred_element_type=F32)
                    + jnp.dot((b * i).astype(BF16), t[4].astype(BF16), preferred_element_type=F32)
                    for a, b, i, t in zip(ec, ep, inv, tasks, strict=True)]
            lses = [jnp.broadcast_to(m + jnp.log(s), (ATT_BLOCK, ATT_HEAD_DIM)) for m, s in zip(mx, den, strict=True)]
            for i, (r, mi) in enumerate(blocks[at:at + ATT_SIDE_BY_SIDE]):
                rows = _att_rows(mi, r, d)
                o_ref[rows, :] = jnp.concatenate(outs[2 * i:2 * i + 2], axis=1)
                l_ref[rows, :] = jnp.concatenate(lses[2 * i:2 * i + 2], axis=1)

    def spec(col0, prev):
        if prev:
            return pl.BlockSpec((ATT_SPAN, W), lambda hp, n: (jnp.maximum(n - 1, 0), col0 + 2 * g + hp))
        return pl.BlockSpec((ATT_SPAN, W), lambda hp, n: (n, col0 + 2 * g + hp))

    o_spec = pl.BlockSpec((ATT_SPAN, W), lambda hp, n: (n, hp))
    o, l = pl.pallas_call(
        body, name=f"att_fwd_g{g}", grid=(2, nb),
        in_specs=[spec(0, False), spec(6, False), spec(6, True), spec(12, False), spec(12, True)],
        out_specs=[o_spec, o_spec],
        out_shape=[jax.ShapeDtypeStruct((T, ATT_GROUP_WIDTH), F32)] * 2,
        compiler_params=_params(("parallel", "arbitrary")),
    )(p_att, p_att, p_att, p_att, p_att)
    return o, l


def _att_bwd(p_att, o, l, do, dl, g):
    T = p_att.shape[0]
    d = ATT_GROUP_DILATION[g]
    W = ATT_PAIR_WIDTH
    nb = T // ATT_SPAN
    mb = ATT_SPAN // (ATT_BLOCK * d)
    scale = ATT_HEAD_DIM ** -0.5

    def body(q_ref, k_ref, v_ref, o_ref, l_ref, do_ref, dl_ref,
             qn_ref, on_ref, ln_ref, don_ref, dln_ref, dq_ref, dk_ref, dv_ref, carry_ref):
        hp, n = pl.program_id(0), pl.program_id(1)
        qi, kj = _att_masks()

        @pl.when(n == 0)
        def _():
            carry_ref[...] = jnp.zeros_like(carry_ref)

        slopes = [_pair_slope(g, hp, j) * d for j in range(2)]
        blocks = [(r, mi) for r in range(d) for mi in range(mb)]
        side_by_side = ATT_SIDE_BY_SIDE // 2
        carry = None
        for at in range(0, len(blocks), side_by_side):
            tasks = []
            for r, mi in blocks[at:at + side_by_side]:
                rows = _att_rows(mi, r, d)
                if mi < mb - 1:
                    nrows = _att_rows(mi + 1, r, d)
                    nxt = (q_ref[nrows, :], o_ref[nrows, :], l_ref[nrows, :], do_ref[nrows, :], dl_ref[nrows, :])
                    has_next = True
                else:
                    nrows = _att_rows(0, r, d)
                    nxt = (qn_ref[nrows, :], on_ref[nrows, :], ln_ref[nrows, :], don_ref[nrows, :],
                           dln_ref[nrows, :])
                    has_next = n < nb - 1
                cur = (q_ref[rows, :], o_ref[rows, :], l_ref[rows, :], do_ref[rows, :], dl_ref[rows, :])
                k_all, v_all = k_ref[rows, :], v_ref[rows, :]
                for j in range(2):
                    sl = slice(j * ATT_HEAD_DIM, (j + 1) * ATT_HEAD_DIM)
                    for blk, steps, valid in ((cur, qi - kj, kj <= qi),
                                              (nxt, qi - kj + ATT_BLOCK, (kj >= qi) & has_next)):
                        q, o_, lse, do_, dlse = (z[:, sl] for z in blk)
                        tasks.append(dict(q=q, o=o_, lse=lse[:, :1], do=do_, dlse=dlse[:, :1], steps=steps,
                                          valid=valid, k=k_all[:, sl], vb=v_all[:, sl].astype(BF16),
                                          slope=slopes[j]))
            p = [jnp.exp(_att_logits(t["q"], t["k"], t["slope"], t["steps"], t["valid"]) - t["lse"]) for t in tasks]
            dp = [lax.dot_general(t["do"].astype(BF16), t["vb"], (((1,), (1,)), ((), ())),
                                  preferred_element_type=F32) for t in tasks]
            dsum = [jnp.sum(t["do"] * t["o"], axis=1, keepdims=True) for t in tasks]
            ds = [a * (b - s + t["dlse"]) for a, b, s, t in zip(p, dp, dsum, tasks, strict=True)]
            dv_ = [jnp.dot(a.T.astype(BF16), t["do"].astype(BF16), preferred_element_type=F32)
                   for a, t in zip(p, tasks, strict=True)]
            dk_ = [jnp.dot(a.T.astype(BF16), t["q"].astype(BF16), preferred_element_type=F32) * scale
                   for a, t in zip(ds, tasks, strict=True)]
            dq_ = [jnp.dot(a.astype(BF16), t["k"].astype(BF16), preferred_element_type=F32) * scale
                   for a, t in zip(ds, tasks, strict=True)]
            for i, (r, mi) in enumerate(blocks[at:at + side_by_side]):
                rows = _att_rows(mi, r, d)
                b = 4 * i
                if mi == 0:
                    carry = carry_ref[r]
                dq_ref[rows, :] = jnp.concatenate([dq_[b], dq_[b + 2]], axis=1) + carry
                carry = jnp.concatenate([dq_[b + 1], dq_[b + 3]], axis=1)
                if mi == mb - 1:
                    carry_ref[r] = carry
                dk_ref[rows, :] = jnp.concatenate([dk_[b] + dk_[b + 1], dk_[b + 2] + dk_[b + 3]], axis=1)
                dv_ref[rows, :] = jnp.concatenate([dv_[b] + dv_[b + 1], dv_[b + 2] + dv_[b + 3]], axis=1)

    head_rows = ATT_BLOCK * d
    nxt_n = lambda n: jnp.minimum((n + 1) * mb, T // head_rows - 1)
    cur_p = lambda col0: pl.BlockSpec((ATT_SPAN, W), lambda hp, n: (n, col0 + 2 * g + hp))
    cur_o = pl.BlockSpec((ATT_SPAN, W), lambda hp, n: (n, hp))
    nxt_o = pl.BlockSpec((head_rows, W), lambda hp, n: (nxt_n(n), hp))
    dq, dk, dv = pl.pallas_call(
        body, name=f"att_bwd_g{g}", grid=(2, nb),
        in_specs=[cur_p(0), cur_p(6), cur_p(12), cur_o, cur_o, cur_o, cur_o,
                  pl.BlockSpec((head_rows, W), lambda hp, n: (nxt_n(n), 2 * g + hp)), nxt_o, nxt_o, nxt_o, nxt_o],
        out_specs=[cur_o, cur_o, cur_o],
        out_shape=[jax.ShapeDtypeStruct((T, ATT_GROUP_WIDTH), F32)] * 3,
        scratch_shapes=[pltpu.VMEM((d, ATT_BLOCK, W), F32)],
        compiler_params=_params(("parallel", "arbitrary")),
    )(p_att, p_att, p_att, o, l, do, dl, p_att, o, l, do, dl)
    return dq, dk, dv


FFN_TILE = 2 * D_FF // N_CHIPS
RKV = 3 * RW_WIDTH
WA = 128
XG = 160
RW_COLS = RKV + WA + XG


def _local_step(x, p, W, target, late_weights=None, early_grads=None, by_chip=False, grad_dtype=F32):
    T = x.shape[0]
    tT = 512
    bd512 = _block_diag_ones(RW_WIDTH, RW_HEAD_DIM)
    bd256 = _block_diag_ones(ATT_GROUP_WIDTH, ATT_HEAD_DIM)
    G = {}
    W = dict(W)

    w_in = W["w_in"]
    w_rkv, w_wa, w_xg, w_att = (w_in[:, :RKV], w_in[:, RKV:RKV + WA], w_in[:, RKV + WA:RW_COLS],
                                w_in[:, RW_COLS:])
    mu = W["rw_mu"]
    mu_rkv, mu_wa, mu_xg = mu[:, :RKV], mu[:, RKV:RKV + WA], mu[:, RKV + WA:]
    zpad = jnp.zeros((64, RW_WIDTH), W["rw_w_up"].dtype)
    w_up_pad = jnp.concatenate([W["rw_w_up"], zpad], axis=0)
    a_up_pad = jnp.concatenate([zpad, W["rw_a_up"]], axis=0)
    r_k = W["rw_r_k"].reshape(1, RW_WIDTH)

    (h,) = _rowwise("norm_mix", lambda i, n, r, pv, nx, c: [_rms_fwd(r[0], c[0])], T, tT,
                    rows=[x], consts=[W["g_mix"]], outs=[("row", D_MODEL, BF16)])
    p_rkv = _mm("proj_rkv", h, w_rkv, "nn")
    p_wa = _mm("proj_wa", h, w_wa, "nn")
    p_xg = _mm("proj_xg", h, w_xg, "nn")
    p_att = _mm("proj_att", h, w_att, "nn", tn=768)
    z_gate = _mm("proj_gate", h, W["w_gate"], "nn")

    def rw_pre_core(i, rows, prevs, consts):
        prkv, pwa, pxg = rows[:3]
        (mrkv, mwa, mxg, w0, a0, k_k, k_a, wup, aup, gup, bd) = consts[:11]
        m_rkv = prkv + (_shift_down(prkv, prevs[0], i, 1) - prkv) * mrkv
        m_wa = pwa + (_shift_down(pwa, prevs[1], i, 1) - pwa) * mwa
        m_xg = pxg + (_shift_down(pxg, prevs[2], i, 1) - pxg) * mxg
        r, k, v = m_rkv[:, :RW_WIDTH], m_rkv[:, RW_WIDTH:2 * RW_WIDTH], m_rkv[:, 2 * RW_WIDTH:]
        tw = jnp.tanh(m_wa)
        lw = w0 + jnp.dot(tw.astype(BF16), wup.astype(BF16), preferred_element_type=F32)
        wlog = -_softplus(-lw) - 0.5
        log_decay = -jnp.exp(wlog)
        a = _sigmoid(a0 + jnp.dot(m_wa.astype(BF16), aup.astype(BF16), preferred_element_type=F32))
        sg = _sigmoid(m_xg)
        gate = jnp.dot(sg.astype(BF16), gup.astype(BF16), preferred_element_type=F32)
        kkp = k * k_k
        nrm = jnp.sqrt(_segsum(kkp * kkp, bd))
        nrm_c = jnp.maximum(nrm, 1e-12)
        kk = kkp / nrm_c
        k2 = k * (1.0 + (a - 1.0) * k_a)
        return dict(r=r, k=k, v=v, tw=tw, lw=lw, wlog=wlog, log_decay=log_decay, a=a, sg=sg, gate=gate, kkp=kkp,
                    nrm=nrm, nrm_c=nrm_c, kk=kk, k2=k2, m_rkv=m_rkv, m_wa=m_wa, m_xg=m_xg)

    pre_consts = [mu_rkv, mu_wa, mu_xg, W["rw_w0"], W["rw_a0"], W["rw_k_k"], W["rw_k_a"],
                  w_up_pad, a_up_pad, W["rw_g_up"], bd512]

    def rw_pre(i, n, rows, prevs, nexts, consts):
        q = rw_pre_core(i, rows, prevs, consts)
        return [q["r"], q["log_decay"], q["k2"], q["v"], -q["kk"], q["kk"] * q["a"], q["gate"]]

    r_s, w_s, k_s, v_s, a_s, b_s, gate_s = _rowwise(
        "rwkv_pre", rw_pre, T, tT, rows=[p_rkv, p_wa, p_xg], prevs=[p_rkv, p_wa, p_xg], consts=pre_consts,
        outs=[("row", RW_WIDTH, F32)] * 7)
    (at_s, bt_s, kt_s, rt_s, a2v_s, w2v_s, tinv_s, w1_s, a2_s, w2_s,
     plast_s) = _rwkv_chunk_prep(r_s, w_s, k_s, a_s, b_s, v_s)
    y_scan, sa_s, s0_s = _rwkv_chunk_fwd(v_s, at_s, bt_s, kt_s, rt_s, a2v_s, w2v_s, tinv_s, w1_s, plast_s)

    def rw_post_core(rows, consts):
        y, r, k2, v, gate = rows[:5]
        ln_g, ln_b, rk, bd = consts[:4]
        mean = _segsum(y, bd) * (1.0 / RW_HEAD_DIM)
        yc = y - mean
        var = _segsum(yc * yc, bd) * (1.0 / RW_HEAD_DIM)
        rstd = lax.rsqrt(var + RW_LN_EPS)
        yn = yc * rstd
        s = _segsum(r * k2 * rk, bd)
        return dict(yn=yn, rstd=rstd, s=s, pre=yn * ln_g + ln_b + s * v)

    post_consts = [W["rw_ln_g"], W["rw_ln_b"], r_k, bd512]
    (y_a,) = _rowwise("rwkv_post", lambda i, n, r, pv, nx, c: [rw_post_core(r, c)["pre"] * r[4]], T, tT,
                      rows=[y_scan, r_s, k_s, v_s, gate_s], consts=post_consts, outs=[("row", RW_WIDTH, BF16)])

    att = [_att_fwd(p_att, g) for g in range(3)]

    def comb_weights(ls):
        mx = jnp.maximum(jnp.maximum(ls[0], ls[1]), ls[2])
        es = [jnp.exp(l - mx) for l in ls]
        den = es[0] + es[1] + es[2]
        return [e / den for e in es]

    def att_comb(i, n, rows, pv, nx, c):
        wts = comb_weights(rows[3:6])
        return [wts[0] * rows[0] + wts[1] * rows[1] + wts[2] * rows[2]]

    (y_b,) = _rowwise("att_combine", att_comb, T, tT, rows=[att[0][0], att[1][0], att[2][0], att[0][1], att[1][1],
                                                            att[2][1]], outs=[("row", ATT_GROUP_WIDTH, BF16)])

    if late_weights is not None:
        W.update(late_weights(y_b))
    br_a = _mm("branch_a", y_a, W["w_branch_a"], "nn")
    br_b = _mm("branch_b", y_b, W["w_branch_b"], "nn")

    def merge(i, n, rows, pv, nx, c):
        gates = _sigmoid(rows[0] + c[0])
        return [gates[:, :D_MODEL] * rows[1] + gates[:, D_MODEL:] * rows[2]]

    (merged,) = _rowwise("merge", merge, T, tT, rows=[z_gate, br_a, br_b], consts=[W["b_gate"]],
                         outs=[("row", D_MODEL, BF16)])
    with_norm = lambda res, rows, consts: [res, _rms_fwd(res, consts[0])]
    stream_and_norm = [("row", D_MODEL, F32), ("row", D_MODEL, BF16)]
    x1, h2 = _mm("mix_out", merged, W["w_out"], "nn", add=x, post=(with_norm, [], [W["g_ffn"]], stream_and_norm))

    u = _mm("ffn_up", h2, W["w_up"], "nn", tn=FFN_TILE)

    def conv_core(i, rows, prevs, consts):
        uu, cw, cb = rows[0], consts[0], consts[1]
        u1 = _shift_down(uu, prevs[0], i, 1)
        u2 = _shift_down(uu, prevs[0], i, 2)
        uc = cb + cw[0:1] * uu + cw[1:2] * u1 + cw[2:3] * u2
        return uc[:, :D_FF], uc[:, D_FF:], u1, u2

    def glu(i, n, rows, prevs, nx, consts):
        gate, val, _, _ = conv_core(i, rows, prevs, consts)
        return [_gelu(gate) * val]

    tF = 128
    (act,) = _rowwise("conv_glu", glu, T, tF, rows=[u], prevs=[u], consts=[W["conv_w"], W["conv_b"]],
                      outs=[("row", D_FF, BF16)])
    x2, h3 = _mm("ffn_down", act, W["w_down"], "nn", add=x1, post=(with_norm, [], [W["g_ple"]], stream_and_norm))

    e_ple = _mm("ple_emb", p, W["w_ple"], "nn")

    def head(i, n, rows, pv, nx, consts):
        x2_, z, e, tgt = rows
        pg = _sigmoid(z)
        x3 = x2_ + pg * e
        y = _rms_fwd(x3, consts[0])
        err = y - tgt
        loss = 0.5 * jnp.sum(jnp.sum(err * err, axis=1, keepdims=True) * (1.0 / D_MODEL), axis=0, keepdims=True)
        dy = err * (1.0 / D_MODEL)
        dx3, dgf = _rms_bwd(x3, consts[0], dy)
        return [dx3, dx3 * pg, dx3 * e * pg * (1.0 - pg), jnp.broadcast_to(loss, (1, LANES)), _colsum(dgf)]

    dx3, de, dz, loss_acc, G["g_final"] = _mm(
        "ple_gate_loss_head", h3, W["w_ple_gate"], "nn", tm=512,
        post=(lambda res, rows, consts: head(0, 0, [rows[0], res, rows[1], rows[2]], [], [], consts),
              [x2, e_ple, target], [W["g_final"].reshape(1, D_MODEL)],
              [("row", D_MODEL, F32), ("row", D_MODEL, BF16), ("row", D_MODEL, BF16), ("acc", (1, LANES)),
               ("acc", (1, D_MODEL))]))
    G["w_ple"] = _mm("d_w_ple", p, de, "tn", grad_dtype, out_by_chip=by_chip)
    G["w_ple_gate"] = _mm("d_w_ple_gate", h3, dz, "tn", grad_dtype)
    def norm_bwd(i, n, rows, pv, nx, consts):
        dx, dg = _rms_bwd(rows[0], consts[0], rows[1])
        return [rows[2] + dx, _colsum(dg)]

    through_norm = lambda res, rows, consts: norm_bwd(0, 0, [rows[0], res, rows[1]], [], [], consts)
    stream_and_gain = [("row", D_MODEL, F32), ("acc", (1, D_MODEL))]
    dx2, G["g_ple"] = _mm("d_h3", dz, W["w_ple_gate"], "nt", tm=512,
                          post=(through_norm, [x2, dx3], [W["g_ple"]], stream_and_gain))

    dact = _mm("d_act", dx2, W["w_down"], "nt")
    G["w_down"] = _mm("d_w_down", act, dx2, "tn", grad_dtype)

    def glu_grad(gate, val, da):
        act_, slope = _gelu_and_grad(gate)
        return jnp.concatenate([da * val * slope, da * act_], axis=1)

    def glu_bwd(i, n, rows, prevs, nexts, consts):
        uu, da = rows
        cw = consts[0]
        gate, val, u1, u2 = conv_core(i, rows, prevs, consts)
        duc = glu_grad(gate, val, da)
        dcw = jnp.concatenate([_colsum(duc * uu), _colsum(duc * u1), _colsum(duc * u2)], axis=0)
        gate_n, val_n, _, _ = conv_core(1, [nexts[0]], [uu[tF - SUBLANES:]], consts)
        duc_n = glu_grad(gate_n, val_n, nexts[1])
        du = (cw[0:1] * duc + cw[1:2] * _shift_up(duc, duc_n, i, n, 1) + cw[2:3] * _shift_up(duc, duc_n, i, n, 2))
        return [du, _colsum(duc), dcw]

    du, G["conv_b"], G["conv_w"] = _rowwise(
        "d_conv_glu", glu_bwd, T, tF, rows=[u, dact], prevs=[u], nexts=[u, dact],
        consts=[W["conv_w"], W["conv_b"]],
        outs=[("row", 2 * D_FF, BF16), ("acc", (1, 2 * D_FF)), ("acc", (3, 2 * D_FF))])
    G["w_up"] = _mm("d_w_up", h2, du, "tn", grad_dtype, out_by_chip=by_chip, tn=FFN_TILE)
    dx1, G["g_ffn"] = _mm("d_h2", du, W["w_up"], "nt", tk=FFN_TILE, tm=512,
                          post=(through_norm, [x1, dx2], [W["g_ffn"]], stream_and_gain))

    b_gate = W["b_gate"]
    if early_grads is not None:
        b_gate = b_gate + early_grads(G, 0)[0:1, 0:1]
    G["w_out"] = _mm("d_w_out", merged, dx1, "tn", grad_dtype)

    def merge_bwd(dm, rows, consts):
        z, a_, b_ = rows
        gates = _sigmoid(z + consts[0])
        ga, gb = gates[:, :D_MODEL], gates[:, D_MODEL:]
        dz_ = jnp.concatenate([dm * a_ * ga * (1.0 - ga), dm * b_ * gb * (1.0 - gb)], axis=1)
        return [dm * ga, dm * gb, dz_, _colsum(dz_)]

    d_br_a, d_br_b, dz_gate, G["b_gate"] = _mm(
        "d_merged", dx1, W["w_out"], "nt", tm=512,
        post=(merge_bwd, [z_gate, br_a, br_b], [b_gate],
              [("row", D_MODEL, BF16), ("row", D_MODEL, BF16), ("row", 2 * D_MODEL, BF16),
               ("acc", (1, 2 * D_MODEL))]))
    G["w_branch_a"] = _mm("d_w_branch_a", y_a, d_br_a, "tn", grad_dtype, out_by_chip=by_chip)
    G["w_branch_b"] = _mm("d_w_branch_b", y_b, d_br_b, "tn", grad_dtype, out_by_chip=by_chip)
    G["w_gate"] = _mm("d_w_gate", h, dz_gate, "tn", grad_dtype, out_by_chip=by_chip)
    if early_grads is not None:
        post_consts = [post_consts[0] + early_grads(G, 1)[0:1, 0:1]] + post_consts[1:]
    dy_a = _mm("d_y_a", d_br_a, W["w_branch_a"], "nt")
    def att_comb_bwd(dy, rows, consts):
        os_, ls = rows[0:3], rows[3:6]
        wts = comb_weights(ls)
        dws = [_segsum(dy * o_, consts[0]) for o_ in os_]
        mix = wts[0] * dws[0] + wts[1] * dws[1] + wts[2] * dws[2]
        return [wts[g_] * dy for g_ in range(3)] + [wts[g_] * (dws[g_] - mix) for g_ in range(3)]

    comb = _mm("d_y_b", d_br_b, W["w_branch_b"], "nt",
               post=(att_comb_bwd, [att[0][0], att[1][0], att[2][0], att[0][1], att[1][1], att[2][1]], [bd256],
                     [("row", ATT_GROUP_WIDTH, F32)] * 6))
    dqkv = [_att_bwd(p_att, att[g][0], att[g][1], comb[g], comb[3 + g], g) for g in range(3)]
    dp_att = jnp.concatenate([dqkv[g][part] for part in range(3) for g in range(3)], axis=1).astype(BF16)

    def rw_post_bwd(i, n, rows, pv, nx, consts):
        y, r, k2, v, gate, dya = rows
        ln_g, ln_b, rk, bd = consts
        q = rw_post_core(rows, consts)
        dpre = dya * gate
        dgate = dya * q["pre"]
        dyn = dpre * ln_g
        inv = 1.0 / RW_HEAD_DIM
        dy_scan = q["rstd"] * (dyn - _segsum(dyn, bd) * inv - q["yn"] * (_segsum(dyn * q["yn"], bd) * inv))
        ds = _segsum(dpre * v, bd)
        return [dy_scan, dgate, ds * k2 * rk, ds * r * rk, dpre * q["s"],
                _colsum(dpre * q["yn"]), _colsum(dpre), _colsum(ds * r * k2)]

    dy_scan, dgate, dr_b, dk2_b, dv_b, G["rw_ln_g"], G["rw_ln_b"], d_rk = _rowwise(
        "d_rwkv_post", rw_post_bwd, T, tT, rows=[y_scan, r_s, k_s, v_s, gate_s, dy_a], consts=post_consts,
        outs=[("row", RW_WIDTH, F32)] * 5 + [("acc", (1, RW_WIDTH))] * 3)
    G["rw_r_k"] = d_rk.reshape(RW_HEADS, RW_HEAD_DIM)

    dr_s, dw_s, dk_s, da_s, db_s, dv_s = _rwkv_chunk_bwd(r_s, w_s, k_s, a_s, b_s, v_s, dy_scan, s0_s, tinv_s, w1_s,
                                                         a2_s, w2_s, sa_s)

    def rw_pre_bwd(i, n, rows, prevs, nx, consts):
        q = rw_pre_core(i, rows, prevs, consts)
        (mrkv, mwa, mxg, w0, a0, k_k, k_a, wup, aup, gup, bd) = consts
        dr, dlogdecay, dk2, dv, dav, dbv, dgate_ = rows[3:10]
        dr = dr + rows[10]
        dk2 = dk2 + rows[11]
        dv = dv + rows[12]
        a, k, kk = q["a"], q["k"], q["kk"]
        dk = dk2 * (1.0 + (a - 1.0) * k_a)
        da = dk2 * k * k_a + dbv * kk
        dkk = dbv * a - dav
        live = q["nrm"] > 1e-12
        dkkp = jnp.where(live, dkk - kk * _segsum(dkk * kk, bd), dkk) / q["nrm_c"]
        dk = dk + dkkp * k_k
        dlw = dlogdecay * q["log_decay"] * _sigmoid(-q["lw"])
        dla = da * a * (1.0 - a)
        nt = (((1,), (1,)), ((), ()))
        dtw = lax.dot_general(dlw.astype(BF16), wup.astype(BF16), nt, preferred_element_type=F32)
        dxa = lax.dot_general(dla.astype(BF16), aup.astype(BF16), nt, preferred_element_type=F32)
        dm_wa = dtw * (1.0 - q["tw"] * q["tw"]) + dxa
        dsg = lax.dot_general(dgate_.astype(BF16), gup.astype(BF16), nt, preferred_element_type=F32)
        dm_xg = dsg * q["sg"] * (1.0 - q["sg"])
        dm_rkv = jnp.concatenate([dr, dk, dv], axis=1)
        prkv, pwa, pxg = rows[:3]
        dmu = jnp.concatenate([_colsum(dm_rkv * (_shift_down(prkv, prevs[0], i, 1) - prkv)),
                               _colsum(dm_wa * (_shift_down(pwa, prevs[1], i, 1) - pwa)),
                               _colsum(dm_xg * (_shift_down(pxg, prevs[2], i, 1) - pxg))], axis=1)
        return [dm_rkv, dm_wa, dm_xg, dlw, dla, q["tw"], q["m_wa"], q["sg"], dmu,
                _colsum(dlw), _colsum(dla), _colsum(dkkp * k), _colsum(dk2 * k * (a - 1.0))]

    (dm_rkv, dm_wa, dm_xg, dlw, dla, tw_s, mwa_s, sg_s, G["rw_mu"], G["rw_w0"], G["rw_a0"], G["rw_k_k"],
     G["rw_k_a"]) = _rowwise(
        "d_rwkv_pre", rw_pre_bwd, T, tT,
        rows=[p_rkv, p_wa, p_xg, dr_s, dw_s, dk_s, dv_s, da_s, db_s, dgate, dr_b, dk2_b, dv_b],
        prevs=[p_rkv, p_wa, p_xg], consts=pre_consts,
        outs=[("row", RKV, F32), ("row", WA, F32), ("row", XG, F32), ("row", RW_WIDTH, BF16),
              ("row", RW_WIDTH, BF16), ("row", WA, BF16), ("row", WA, BF16), ("row", XG, BF16),
              ("acc", (1, RW_COLS))] + [("acc", (1, RW_WIDTH))] * 4)
    G["rw_w_up"] = _mm("d_rw_w_up", tw_s, dlw, "tn", grad_dtype)[:64]
    G["rw_a_up"] = _mm("d_rw_a_up", mwa_s, dla, "tn", grad_dtype)[64:]
    G["rw_g_up"] = _mm("d_rw_g_up", sg_s, dgate, "tn", grad_dtype)

    def shift_bwd(i, n, rows, pv, nexts, consts):
        return [rows[j] * (1.0 - consts[j]) + _shift_up(rows[j], nexts[j], i, n, 1) * consts[j] for j in range(3)]

    dp_rkv, dp_wa, dp_xg = _rowwise(
        "d_token_shift", shift_bwd, T, tT, rows=[dm_rkv, dm_wa, dm_xg], nexts=[dm_rkv, dm_wa, dm_xg],
        consts=[mu_rkv, mu_wa, mu_xg], outs=[("row", RKV, BF16), ("row", WA, BF16), ("row", XG, BF16)])

    G["w_in"] = jnp.concatenate([_mm("d_w_rkv", h, dp_rkv, "tn", grad_dtype), _mm("d_w_wa", h, dp_wa, "tn", grad_dtype),
                                 _mm("d_w_xg", h, dp_xg, "tn", grad_dtype), _mm("d_w_att", h, dp_att, "tn", grad_dtype, tn=768)], axis=1)
    if early_grads is not None:
        w_wa = w_wa + early_grads(G, 2)[0:1, 0:1].astype(w_wa.dtype)
    dh = _mm("d_h_gate", dz_gate, W["w_gate"], "nt")
    dh = _mm("d_h_rkv", dp_rkv, w_rkv, "nt", add=dh)
    dh = _mm("d_h_wa", dp_wa, w_wa, "nt", add=dh)
    dh = _mm("d_h_xg", dp_xg, w_xg, "nt", add=dh)
    dx, G["g_mix"] = _mm("d_h_att", dp_att, w_att, "nt", add=dh, tm=512,
                         post=(through_norm, [x, dx1], [W["g_mix"]], stream_and_gain))
    return loss_acc[:, :1], dx, G


HBM_SPEC = pl.BlockSpec(memory_space=pltpu.HBM)


def _place():
    x, y, c = lax.axis_index("x"), lax.axis_index("y"), lax.axis_index("c")
    return x, y, c, [(1 - x, y), (x, 1 - y), (1 - x, 1 - y)]


def _remote(src, dst, send_sems, recv_sems, k, to):
    return pltpu.make_async_remote_copy(src_ref=src, dst_ref=dst, send_sem=send_sems.at[k], recv_sem=recv_sems.at[k],
                                        device_id=to, device_id_type=MESH)


ROW_ALIGN = 16


def _splits(rows):
    return rows % (2 * ROW_ALIGN) == 0


def _half_rows(ref_rows, c, first):
    half = ref_rows // 2
    which = c if first else 1 - c
    return pl.ds(pl.multiple_of(which * half, ROW_ALIGN), half)


def _gather_chips(shards):
    n = len(shards)
    split = [_splits(s.shape[0]) for s in shards]

    def body(*refs):
        w_refs, out_refs = refs[:n], refs[n:2 * n]
        send_sems, recv_sems = refs[2 * n:]
        x, y, c, chips = _place()
        me = 2 * x + y
        sends, passed = [], []
        for i in range(n):
            for j, (px, py) in enumerate(chips):
                if split[i]:
                    mine = _half_rows(w_refs[i].shape[0], c, True)
                    cp = _remote(w_refs[i].at[mine], out_refs[i].at[me, mine], send_sems, recv_sems, 6 * i + j,
                                 (px, py, c))
                else:
                    cp = _remote(w_refs[i], out_refs[i].at[me], send_sems, recv_sems, 6 * i + j, (px, py, c))
                cp.start()
                sends.append(cp)
        for i in range(n):
            for j, (px, py) in enumerate(chips):
                if split[i]:
                    landed = out_refs[i].at[2 * px + py, _half_rows(w_refs[i].shape[0], c, True)]
                    _remote(landed, landed, send_sems, recv_sems, 6 * i + j, (px, py, c)).wait_recv()
                    cp = _remote(landed, landed, send_sems, recv_sems, 6 * i + 3 + j, (x, y, 1 - c))
                    cp.start()
                    passed.append(cp)
                else:
                    landed = out_refs[i].at[2 * px + py]
                    _remote(landed, landed, send_sems, recv_sems, 6 * i + j, (px, py, c)).wait_recv()
        for i in range(n):
            if split[i]:
                for j, (px, py) in enumerate(chips):
                    landed = out_refs[i].at[2 * px + py, _half_rows(w_refs[i].shape[0], c, False)]
                    _remote(landed, landed, send_sems, recv_sems, 6 * i + 3 + j, (x, y, 1 - c)).wait_recv()
        for cp in sends + passed:
            cp.wait_send()

    outs = pl.pallas_call(
        body, name="gather_weights", in_specs=[HBM_SPEC] * n, out_specs=[HBM_SPEC] * n,
        out_shape=[jax.ShapeDtypeStruct((N_CHIPS,) + s.shape, s.dtype) for s in shards],
        scratch_shapes=[pltpu.SemaphoreType.DMA((6 * n,)), pltpu.SemaphoreType.DMA((6 * n,))],
    )(*shards)
    me = 2 * lax.axis_index("x") + lax.axis_index("y")
    return [lax.dynamic_update_slice(o, s[None], (me, 0, 0)) for o, s in zip(outs, shards, strict=True)]


def _join_halves(reds):
    n = len(reds)

    def body(*refs):
        r_refs, out_refs = refs[:n], refs[n:2 * n]
        send_sems, recv_sems = refs[2 * n:]
        x, y, c, _ = _place()
        cps = []
        for i in range(n):
            mine = _half_rows(out_refs[i].shape[0], c, True)
            cp = _remote(r_refs[i], out_refs[i].at[mine], send_sems, recv_sems, i, (x, y, 1 - c))
            cp.start()
            cps.append(cp)
        for cp in cps:
            cp.wait()

    outs = pl.pallas_call(
        body, name="join_halves", in_specs=[HBM_SPEC] * n, out_specs=[HBM_SPEC] * n,
        out_shape=[jax.ShapeDtypeStruct((2 * r.shape[0], r.shape[1]), r.dtype) for r in reds],
        scratch_shapes=[pltpu.SemaphoreType.DMA((n,)), pltpu.SemaphoreType.DMA((n,))],
    )(*reds)
    c = lax.axis_index("c")
    return [lax.dynamic_update_slice(o, r, (c * r.shape[0], 0)) for o, r in zip(outs, reds, strict=True)]


SEM_SPEC = pl.BlockSpec(memory_space=pltpu.SEMAPHORE)
PEERS = N_DEV - 1
DATAFLOW = pltpu.SideEffectType.DATAFLOW_SIDE_EFFECTING


def _travel_copies(mode, src_refs, land_refs, send_sems, recv_sems):
    x, y, c, chips = _place()
    me = 2 * x + y
    pairs = []
    for i, (src, land) in enumerate(zip(src_refs, land_refs, strict=True)):
        if mode in ("scatter", "all"):
            for k in range(1, N_DEV):
                px, py, pc = x ^ (k >> 2), y ^ ((k >> 1) & 1), c ^ (k & 1)
                mine = src if mode == "all" else src.at[2 * px + py, _half_rows(src.shape[1], pc, True)]
                there, here = land.at[4 * x + 2 * y + c], land.at[4 * px + 2 * py + pc]
                send = functools.partial(_remote, mine, there, send_sems, recv_sems, PEERS * i + k - 1, (px, py, pc))
                arrival = functools.partial(_remote, mine, here, send_sems, recv_sems, PEERS * i + k - 1, (px, py, pc))
                pairs.append((send, arrival))
            continue
        for j, (px, py) in enumerate(chips):
            peer = 2 * px + py
            if _splits(src.shape[0]):
                rows = _half_rows(src.shape[0], c, True)
                mine, there, here = src.at[rows], land.at[me, rows], land.at[peer, rows]
            else:
                mine, there, here = src, land.at[me], land.at[peer]
            send = functools.partial(_remote, mine, there, send_sems, recv_sems, PEERS * i + j, (px, py, c))
            arrival = functools.partial(_remote, mine, here, send_sems, recv_sems, PEERS * i + j, (px, py, c))
            pairs.append((send, arrival))
    return pairs


def _share_halves(name, lands):
    idx = [i for i, a in enumerate(lands) if _splits(a.shape[1])]
    n = len(idx)

    def body(*refs):
        in_refs, out_refs = refs[:n], refs[n:2 * n]
        send_sems, recv_sems = refs[2 * n:]
        x, y, c, chips = _place()
        cps = []
        for i, (src, dst) in enumerate(zip(in_refs, out_refs, strict=True)):
            for j, (px, py) in enumerate(chips):
                mine = _half_rows(src.shape[1], c, True)
                cp = _remote(src.at[2 * px + py, mine], dst.at[2 * px + py, mine], send_sems, recv_sems, 3 * i + j,
                             (x, y, 1 - c))
                cp.start()
                cps.append(cp)
        for i, dst in enumerate(out_refs):
            for j, (px, py) in enumerate(chips):
                theirs = dst.at[2 * px + py, _half_rows(dst.shape[1], c, False)]
                _remote(theirs, theirs, send_sems, recv_sems, 3 * i + j, (x, y, 1 - c)).wait_recv()
        for cp in cps:
            cp.wait_send()

    outs = pl.pallas_call(
        body, name=name, in_specs=[HBM_SPEC] * n, out_specs=[HBM_SPEC] * n,
        out_shape=[jax.ShapeDtypeStruct(lands[i].shape, lands[i].dtype) for i in idx],
        input_output_aliases={i: i for i in range(n)},
        scratch_shapes=[pltpu.SemaphoreType.DMA((3 * n,)), pltpu.SemaphoreType.DMA((3 * n,))],
    )(*[lands[i] for i in idx])
    done = list(lands)
    for i, o in zip(idx, outs, strict=True):
        done[i] = o
    return done


def _travel_start(name, mode, srcs):
    n = len(srcs)
    land_shape = {"gather": lambda s: (N_CHIPS,) + s.shape, "all": lambda s: (N_DEV,) + s.shape,
                  "scatter": lambda s: (N_DEV, s.shape[1] // 2, s.shape[2])}[mode]
    lands = [lax.empty(land_shape(s), s.dtype) for s in srcs]

    def body(*refs):
        src_refs, land_refs = refs[:n], refs[n:2 * n]
        send_sems, recv_sems = refs[2 * n], refs[2 * n + 1]
        token = refs[-1]
        for send, _ in _travel_copies(mode, src_refs, land_refs, send_sems, recv_sems):
            send().start()
        token[...] = jnp.zeros_like(token)

    hbm = lambda a: pltpu.HBM(a.shape, a.dtype)
    outs = pl.pallas_call(
        body, name=name,
        out_shape=(pltpu.SemaphoreType.DMA((PEERS * n,)), pltpu.SemaphoreType.DMA((PEERS * n,)),
                   *[hbm(s) for s in srcs],
                   *[hbm(a) for a in lands], jax.ShapeDtypeStruct((SUBLANES, LANES), F32)),
        in_specs=[HBM_SPEC] * (2 * n),
        out_specs=(SEM_SPEC, SEM_SPEC, *[HBM_SPEC] * (2 * n), pl.BlockSpec(memory_space=pltpu.VMEM)),
        input_output_aliases={i: 2 + i for i in range(2 * n)},
        compiler_params=pltpu.CompilerParams(has_side_effects=DATAFLOW),
    )(*[pltpu.with_memory_space_constraint(a, pltpu.HBM) for a in list(srcs) + lands])
    return outs[0], outs[1], list(outs[2:2 + n]), list(outs[2 + n:2 + 2 * n]), outs[-1]


def _travel_wait(name, mode, send_sems, recv_sems, srcs, lands, after):
    n = len(srcs)

    def body(*refs):
        src_refs, land_refs = refs[:n], refs[n:2 * n]
        send_sems_, recv_sems_ = refs[2 * n], refs[2 * n + 1]
        for send, arrival in _travel_copies(mode, src_refs, land_refs, send_sems_, recv_sems_):
            send().wait_send()
            arrival().wait_recv()

    hbm = lambda a: pltpu.HBM(a.shape, a.dtype)
    outs = pl.pallas_call(
        body, name=name, out_shape=tuple(hbm(a) for a in list(srcs) + list(lands)),
        in_specs=[HBM_SPEC] * (2 * n) + [SEM_SPEC, SEM_SPEC, pl.BlockSpec(memory_space=pl.ANY)],
        out_specs=tuple([HBM_SPEC] * (2 * n)), input_output_aliases={i: i for i in range(2 * n)},
        compiler_params=pltpu.CompilerParams(has_side_effects=DATAFLOW),
    )(*srcs, *lands, send_sems, recv_sems, after)
    c = lax.axis_index("c")
    me = 2 * lax.axis_index("x") + lax.axis_index("y")
    if mode == "gather":
        slot, own = me, [s[None] for s in outs[:n]]
    elif mode == "all":
        slot, own = 2 * me + c, [s[None] for s in outs[:n]]
    else:
        slot = 2 * me + c
        own = [lax.dynamic_slice(s, (me, c * (s.shape[1] // 2), 0), (1, s.shape[1] // 2, s.shape[2])) for s in outs[:n]]
    return [lax.dynamic_update_slice(a, o, (slot,) + (0,) * (a.ndim - 1)) for a, o in zip(outs[n:], own, strict=True)]


SUM_TILE_BYTES = 4 * 1024 * 1024


def _sum_rows(half, cols):
    best = ROW_ALIGN
    for t in range(ROW_ALIGN, half + 1, ROW_ALIGN):
        if half % t == 0 and N_CHIPS * t * cols * 4 <= SUM_TILE_BYTES:
            best = t
    return best


def _sum_devices(name, parts):
    n, H, C = parts.shape
    tr = _sum_rows(H, C)

    def body(p_ref, o_ref):
        acc = p_ref[0].astype(F32)
        for k in range(1, n):
            acc = acc + p_ref[k].astype(F32)
        o_ref[...] = acc

    return pl.pallas_call(
        body, name=name, grid=(H // tr,),
        in_specs=[pl.BlockSpec((n, tr, C), lambda i: (0, i, 0))],
        out_specs=pl.BlockSpec((tr, C), lambda i: (i, 0)),
        out_shape=jax.ShapeDtypeStruct((H, C), F32),
        compiler_params=_params(("parallel",)),
    )(parts)


def _adamw_math(w, g, m, v):
    m = ADAM_B1 * m + (1.0 - ADAM_B1) * g
    v = ADAM_B2 * v + (1.0 - ADAM_B2) * (g * g)
    m_hat = m / (1.0 - ADAM_B1 ** ADAM_STEP)
    v_hat = v / (1.0 - ADAM_B2 ** ADAM_STEP)
    delta = -ADAM_LR * (m_hat / (jnp.sqrt(v_hat) + ADAM_EPS) + ADAM_WD * w)
    return delta, m, v


def _adamw(name, w, g, m, v):
    R, C = w.shape
    tr = R
    if R % SUBLANES == 0:
        for cand in range(SUBLANES, min(R, 256) + 1, SUBLANES):
            if R % cand == 0:
                tr = cand

    def body(w_ref, g_ref, m_ref, v_ref, d_ref, nm_ref, nv_ref):
        d, nm, nv = _adamw_math(w_ref[...], g_ref[...], m_ref[...], v_ref[...])
        d_ref[...] = d
        nm_ref[...] = nm
        nv_ref[...] = nv

    spec = pl.BlockSpec((tr, C), lambda i: (i, 0))
    shape = jax.ShapeDtypeStruct((R, C), F32)
    return pl.pallas_call(
        body, name=name, grid=(R // tr,), in_specs=[spec] * 4, out_specs=[spec] * 3, out_shape=[shape] * 3,
        compiler_params=_params(("parallel",)),
    )(w, g, m, v)


SMALL_ROW = 2048


def _small_layout(shapes):
    places, row = [], 0
    for R, C in shapes:
        pieces = []
        for r in range(R):
            for c0 in range(0, C, SMALL_ROW):
                pieces.append((r, c0, min(C, c0 + SMALL_ROW), row))
                row += 1
        places.append(pieces)
    return places, -(-row // SUBLANES) * SUBLANES


def _put_rows(block_ref, refs, places):
    block_ref[...] = jnp.zeros_like(block_ref)
    for ref, pieces in zip(refs, places, strict=True):
        for r, c0, c1, row in pieces:
            block_ref[row:row + 1, 0:c1 - c0] = ref[r:r + 1, c0:c1]


def _take_rows(block, refs, places):
    for ref, pieces in zip(refs, places, strict=True):
        for r, c0, c1, row in pieces:
            ref[r:r + 1, c0:c1] = block[row:row + 1, 0:c1 - c0]


def _pack_small(arrs):
    places, rows = _small_layout([a.shape for a in arrs])

    def body(*refs):
        _put_rows(refs[-1], refs[:-1], places)

    return pl.pallas_call(body, name="pack_small", out_shape=jax.ShapeDtypeStruct((rows, SMALL_ROW), F32),
                          compiler_params=_params())(*arrs)


def _adamw_small(parts, ws, ms, vs, extra_shapes):
    n_dev, rows, _ = parts.shape
    n = len(ws)
    places, rows_ = _small_layout([w.shape for w in ws] + list(extra_shapes))
    assert rows_ == rows, (rows_, rows)

    def body(*refs):
        p_ref = refs[0]
        w_refs, m_refs, v_refs = refs[1:1 + n], refs[1 + n:1 + 2 * n], refs[1 + 2 * n:1 + 3 * n]
        outs = refs[1 + 3 * n:-3]
        wb, mb, vb = refs[-3:]
        for block, srcs in ((wb, w_refs), (mb, m_refs), (vb, v_refs)):
            _put_rows(block, srcs, places[:n])
        g = p_ref[0]
        for k in range(1, n_dev):
            g = g + p_ref[k]
        d, nm, nv = _adamw_math(wb[...], g, mb[...], vb[...])
        _take_rows(g, outs[0:n], places[:n])
        _take_rows(d, outs[n:2 * n], places[:n])
        _take_rows(nm, outs[2 * n:3 * n], places[:n])
        _take_rows(nv, outs[3 * n:4 * n], places[:n])
        _take_rows(g, outs[4 * n:], places[n:])

    shapes = [jax.ShapeDtypeStruct(w.shape, F32) for w in ws]
    res = pl.pallas_call(
        body, name="adamw_small", out_shape=shapes * 4 + [jax.ShapeDtypeStruct(s, F32) for s in extra_shapes],
        scratch_shapes=[pltpu.VMEM((rows, SMALL_ROW), F32)] * 3, compiler_params=_params(),
    )(parts, *ws, *ms, *vs)
    return res[0:n], res[n:2 * n], res[2 * n:3 * n], res[3 * n:4 * n], res[4 * n:]


WEIGHTS = ['g_mix', 'w_in', 'rw_mu', 'rw_w0', 'rw_w_up', 'rw_a0', 'rw_a_up', 'rw_g_up', 'rw_k_k', 'rw_k_a',
           'rw_r_k', 'rw_ln_g', 'rw_ln_b', 'w_branch_a', 'w_branch_b', 'w_gate', 'b_gate', 'w_out', 'g_ffn', 'w_up',
           'conv_w', 'conv_b', 'w_down', 'g_ple', 'w_ple_gate', 'w_ple', 'g_final']
ARG_NAMES = (['x', 'p'] + WEIGHTS + ['loss_target'] + ['m_' + n for n in WEIGHTS] + ['v_' + n for n in WEIGHTS])
SHARDED = {'w_in': 1, 'rw_w_up': 1, 'rw_a_up': 1, 'rw_g_up': 1, 'w_branch_a': 1, 'w_branch_b': 1, 'w_gate': 1,
           'w_out': 0, 'w_up': 1, 'conv_w': 1, 'w_down': 0, 'w_ple_gate': 0, 'w_ple': 1}
SMALL = [n for n in WEIGHTS if n not in SHARDED]
WHOLE = ['conv_w']
FIRST_USED = ['w_in', 'rw_w_up', 'rw_a_up', 'rw_g_up', 'w_gate']
READ_BY_CHIP = ['w_gate', 'w_branch_a', 'w_branch_b', 'w_up', 'w_ple']
FIRST_DONE = [['w_up', 'w_down', 'w_ple_gate', 'w_ple'], ['w_out', 'w_branch_a', 'w_branch_b', 'w_gate'],
              ['w_in', 'rw_w_up', 'rw_a_up', 'rw_g_up']]
SPLIT = [n for n in SHARDED if n not in WHOLE]


def _full_from_shards(stack, axis):
    _, R, C = stack.shape
    if axis == 0:
        return stack.reshape(N_CHIPS * R, C)
    return stack.transpose(1, 0, 2).reshape(R, N_CHIPS * C)


def _shards_from_full(full, axis):
    R, C = full.shape
    if axis == 0:
        return full.reshape(N_CHIPS, R // N_CHIPS, C)
    return full.reshape(R, N_CHIPS, C // N_CHIPS).transpose(1, 0, 2)


def kernel(x, p, g_mix, w_in, rw_mu, rw_w0, rw_w_up, rw_a0, rw_a_up, rw_g_up, rw_k_k, rw_k_a, rw_r_k, rw_ln_g, rw_ln_b, w_branch_a, w_branch_b, w_gate, b_gate, w_out, g_ffn, w_up, conv_w, conv_b, w_down, g_ple, w_ple_gate, w_ple, g_final, loss_target, m_g_mix, m_w_in, m_rw_mu, m_rw_w0, m_rw_w_up, m_rw_a0, m_rw_a_up, m_rw_g_up, m_rw_k_k, m_rw_k_a, m_rw_r_k, m_rw_ln_g, m_rw_ln_b, m_w_branch_a, m_w_branch_b, m_w_gate, m_b_gate, m_w_out, m_g_ffn, m_w_up, m_conv_w, m_conv_b, m_w_down, m_g_ple, m_w_ple_gate, m_w_ple, m_g_final, v_g_mix, v_w_in, v_rw_mu, v_rw_w0, v_rw_w_up, v_rw_a0, v_rw_a_up, v_rw_g_up, v_rw_k_k, v_rw_k_a, v_rw_r_k, v_rw_ln_g, v_rw_ln_b, v_w_branch_a, v_w_branch_b, v_w_gate, v_b_gate, v_w_out, v_g_ffn, v_w_up, v_conv_w, v_conv_b, v_w_down, v_g_ple, v_w_ple_gate, v_w_ple, v_g_final):
    given = dict(zip(ARG_NAMES, (x, p, g_mix, w_in, rw_mu, rw_w0, rw_w_up, rw_a0, rw_a_up, rw_g_up, rw_k_k, rw_k_a, rw_r_k, rw_ln_g, rw_ln_b, w_branch_a, w_branch_b, w_gate, b_gate, w_out, g_ffn, w_up, conv_w, conv_b, w_down, g_ple, w_ple_gate, w_ple, g_final, loss_target, m_g_mix, m_w_in, m_rw_mu, m_rw_w0, m_rw_w_up, m_rw_a0, m_rw_a_up, m_rw_g_up, m_rw_k_k, m_rw_k_a, m_rw_r_k, m_rw_ln_g, m_rw_ln_b, m_w_branch_a, m_w_branch_b, m_w_gate, m_b_gate, m_w_out, m_g_ffn, m_w_up, m_conv_w, m_conv_b, m_w_down, m_g_ple, m_w_ple_gate, m_w_ple, m_g_final, v_g_mix, v_w_in, v_rw_mu, v_rw_w0, v_rw_w_up, v_rw_a0, v_rw_a_up, v_rw_g_up, v_rw_k_k, v_rw_k_a, v_rw_r_k, v_rw_ln_g, v_rw_ln_b, v_w_branch_a, v_w_branch_b, v_w_gate, v_b_gate, v_w_out, v_g_ffn, v_w_up, v_conv_w, v_conv_b, v_w_down, v_g_ple, v_w_ple_gate, v_w_ple, v_g_final), strict=True))

    def two_d(name, prefix=""):
        a = given[prefix + name]
        if name == "g_final":
            return a.reshape(1, D_MODEL)
        if name == "rw_r_k":
            return a.reshape(1, RW_WIDTH)
        return a[0] if a.ndim == 3 else a

    cast = lambda n: two_d(n) if n in WHOLE else two_d(n).astype(BF16)
    whole = lambda names, stacks: {n: g if n in READ_BY_CHIP else _full_from_shards(g, SHARDED[n])
                                   for n, g in zip(names, stacks, strict=True)}
    late_names = [n for n in SHARDED if n not in FIRST_USED]
    late_sends, late_recvs, late_srcs, late_lands, token = _travel_start(
        "gather_late_start", "gather", [cast(n) for n in late_names])
    W = whole(FIRST_USED, _gather_chips([cast(n) for n in FIRST_USED]))
    for n in SMALL:
        W[n] = two_d(n)
    W["rw_r_k"] = W["rw_r_k"].reshape(RW_HEADS, RW_HEAD_DIM)
    W["g_mix"] = W["g_mix"] + token[0:1, 0:1]

    def late_weights(after):
        lands = _travel_wait("gather_late_wait", "gather", late_sends, late_recvs, late_srcs, late_lands, after)
        return whole(late_names, _share_halves("share_late", lands))

    early_names = [[n for n in SPLIT if n in group] for group in FIRST_DONE]
    assert sorted(sum(early_names, [])) == sorted(SPLIT)
    travelling = []

    def early_grads(G, stage):
        by_chip = [G[n] if n in READ_BY_CHIP else _shards_from_full(G[n], SHARDED[n]) for n in early_names[stage]]
        sends, recvs, srcs, lands, started = _travel_start(f"scatter{stage}_start", "scatter", by_chip)
        travelling.append((sends, recvs, srcs, lands))
        return started

    loss_part, grad_x, G = _local_step(x[0], p[0, 0], W, loss_target[0], late_weights, early_grads, by_chip=True,
                                       grad_dtype=BF16)

    G["rw_r_k"] = G["rw_r_k"].reshape(1, RW_WIDTH)
    extras = [G[n] for n in WHOLE] + [loss_part]
    small_sends, small_recvs, small_srcs, small_lands, small_started = _travel_start(
        "gather_small_start", "all", [_pack_small([G[n] for n in SMALL] + extras)])

    landed = {}
    for stage, (sends, recvs, srcs, lands) in enumerate(travelling):
        landed.update(zip(early_names[stage], _travel_wait(f"scatter{stage}_wait", "scatter", sends, recvs, srcs,
                                                           lands, small_started), strict=True))
    reduced = [_sum_devices("sum_devices_" + n, landed[n]) for n in SPLIT]
    shard_grads = dict(zip(SPLIT, _join_halves(reduced), strict=True))

    grads, deltas, new_m, new_v = {}, {}, {}, {}

    def step(n):
        g = shard_grads[n]
        d, nm, nv = _adamw("adamw_" + n, two_d(n), g, two_d(n, "m_"), two_d(n, "v_"))
        grads[n], deltas[n], new_m[n], new_v[n] = g, d, nm, nv

    for n in SPLIT:
        step(n)

    (all_small,) = _travel_wait("gather_small_wait", "all", small_sends, small_recvs, small_srcs, small_lands,
                                deltas[SPLIT[-1]])
    gs, ds, nms, nvs, summed = _adamw_small(all_small, [two_d(n) for n in SMALL], [two_d(n, "m_") for n in SMALL],
                                            [two_d(n, "v_") for n in SMALL], [e.shape for e in extras])
    loss = summed[-1][0, 0]
    chip = 2 * lax.axis_index("x") + lax.axis_index("y")
    for n, full in zip(WHOLE, summed[:-1], strict=True):
        width = two_d(n).shape[1]
        shard_grads[n] = lax.dynamic_slice_in_dim(full, chip * width, width, axis=1)
        step(n)
    for i, n in enumerate(SMALL):
        grads[n], deltas[n], new_m[n], new_v[n] = gs[i], ds[i], nms[i], nvs[i]
    outs = [loss, grad_x[None]]
    for table in (grads, deltas, new_m, new_v):
        outs += [table[n].reshape(given[n].shape) for n in WEIGHTS]
    return tuple(outs)
```
